```python
import jax, jax.numpy as jnp
from jax import lax
import numpy as np

D_MODEL = 1024
BATCH = 8
SEQ = 8192
DEPTH = 1

CHUNK = 64
Q_BLOCK = 128
D_FF = 2816
HG_HEADS = 8
HG_HEAD_K = 128
HG_HEAD_V = 128
HG_WIDTH = HG_HEADS * HG_HEAD_K
HG_VWIDTH = HG_HEADS * HG_HEAD_V
MLA_HEADS = 8
MLA_NOPE = 128
MLA_ROPE = 64
MLA_V = 128
MLA_QK = MLA_NOPE + MLA_ROPE
Q_LORA = 384
KV_LORA = 256
ROPE_THETA = 10000.0
EPS = 1e-6
IN_SPLITS = (HG_WIDTH, HG_WIDTH, HG_VWIDTH, HG_VWIDTH, Q_LORA, KV_LORA, MLA_ROPE)
IN_COLS = 2 * HG_WIDTH + 2 * HG_VWIDTH + Q_LORA + KV_LORA + MLA_ROPE

kernel_name = 'hybrid_hgrn2_mla_macaron'


def _rms_norm(x, gain):
    xf = x.astype(jnp.float32)
    y = xf * lax.rsqrt(jnp.mean(xf * xf, axis=-1, keepdims=True) + EPS)
    return (y * gain.astype(jnp.float32)).astype(x.dtype)


def _swiglu(x, w_in, w_out):
    gate, up = jnp.split(x @ w_in, 2, axis=-1)
    return (jax.nn.silu(gate) * up) @ w_out


def _rotate(x, cos, sin):
    half = x.shape[-1] // 2
    x1, x2 = x[..., :half], x[..., half:]
    return jnp.concatenate([x1 * cos - x2 * sin, x2 * cos + x1 * sin], axis=-1)


def _to_chunks(t):
    b, s, h, d = t.shape
    return t.reshape(b, s // CHUNK, CHUNK, h, d).transpose(1, 0, 3, 2, 4)


def _from_chunks(t):
    n, b, h, c, d = t.shape
    return t.transpose(1, 0, 3, 2, 4).reshape(b, n * c, h, d)


def _hgrn2_chunk_step(state, inputs):
    q, k, v, log_f = inputs
    cum = jnp.cumsum(log_f, axis=2)
    o_inter = jnp.einsum('bhtk,bhkv->bhtv', q * jnp.exp(cum), state)
    causal = jnp.tril(jnp.ones((CHUNK, CHUNK), dtype=bool))[:, :, None]
    rel = cum[:, :, :, None, :] - cum[:, :, None, :, :]
    decay = jnp.exp(jnp.where(causal, rel, -jnp.inf))
    scores = jnp.einsum('bhtk,bhtsk,bhsk->bhts', q, decay, k)
    o_intra = jnp.einsum('bhts,bhsv->bhtv', scores, v)
    last = cum[:, :, -1, :]
    new_state = jnp.exp(last)[..., None] * state + jnp.einsum(
        'bhsk,bhsv->bhkv', k * jnp.exp(last[:, :, None, :] - cum), v)
    return new_state, o_intra + o_inter


def _hgrn2(q_raw, f_raw, i_raw, g_raw, lower_bound, out_gain):
    b, s, _ = q_raw.shape
    f32 = jnp.float32
    q = jax.nn.silu(q_raw.astype(f32)).reshape(b, s, HG_HEADS, HG_HEAD_K)
    z = f_raw.astype(f32).reshape(b, s, HG_HEADS, HG_HEAD_K)
    lb = lower_bound.astype(f32).reshape(HG_HEADS, HG_HEAD_K)
    log_f = jnp.logaddexp(jnp.log(lb), jnp.log1p(-lb) + jax.nn.log_sigmoid(z))
    k = -jnp.expm1(log_f)
    v = i_raw.astype(f32).reshape(b, s, HG_HEADS, HG_HEAD_V)
    state0 = jnp.zeros((b, HG_HEADS, HG_HEAD_K, HG_HEAD_V), f32)
    _, o = lax.scan(_hgrn2_chunk_step, state0,
                    (_to_chunks(q), _to_chunks(k), _to_chunks(v), _to_chunks(log_f)))
    o = _rms_norm(_from_chunks(o), out_gain)
    o = o * jax.nn.silu(g_raw.astype(f32)).reshape(b, s, HG_HEADS, HG_HEAD_V)
    return o.reshape(b, s, HG_VWIDTH).astype(q_raw.dtype)


def _mla(c_q, c_kv, k_pe, positions, q_lora_gain, w_q_up, kv_lora_gain, w_kv_up,
         q_head_gain, k_head_gain):
    b, s, _ = c_q.shape
    q = (_rms_norm(c_q, q_lora_gain) @ w_q_up).reshape(b, s, MLA_HEADS, MLA_QK)
    kv = (_rms_norm(c_kv, kv_lora_gain) @ w_kv_up).reshape(b, s, MLA_HEADS, MLA_NOPE + MLA_V)
    k_nope, v = kv[..., :MLA_NOPE], kv[..., MLA_NOPE:]
    k = jnp.concatenate(
        [k_nope, jnp.broadcast_to(k_pe[:, :, None, :], (b, s, MLA_HEADS, MLA_ROPE))], axis=-1)
    q = _rms_norm(q, q_head_gain)
    k = _rms_norm(k, k_head_gain)
    inv_freq = ROPE_THETA ** (-jnp.arange(0, MLA_ROPE, 2, dtype=jnp.float32) / MLA_ROPE)
    ang = positions.astype(jnp.float32)[:, :, None, None] * inv_freq
    cos = jnp.cos(ang).astype(q.dtype)
    sin = jnp.sin(ang).astype(q.dtype)
    q = jnp.concatenate([q[..., :MLA_NOPE], _rotate(q[..., MLA_NOPE:], cos, sin)], axis=-1)
    k = jnp.concatenate([k[..., :MLA_NOPE], _rotate(k[..., MLA_NOPE:], cos, sin)], axis=-1)
    n_blocks = s // Q_BLOCK
    q_blocks = q.reshape(b, n_blocks, Q_BLOCK, MLA_HEADS, MLA_QK).transpose(1, 0, 2, 3, 4)
    key_chunk = jnp.arange(s) // CHUNK
    scale = MLA_QK ** -0.5

    def attend(args):
        q_blk, blk = args
        q_chunk = (blk * Q_BLOCK + jnp.arange(Q_BLOCK)) // CHUNK
        allowed = key_chunk[None, :] <= q_chunk[:, None]
        logits = jnp.einsum('bqhd,bkhd->bhqk', q_blk, k).astype(jnp.float32) * scale
        logits = jnp.where(allowed, logits, -jnp.inf)
        probs = jax.nn.softmax(logits, axis=-1).astype(v.dtype)
        return jnp.einsum('bhqk,bkhd->bqhd', probs, v)

    o = lax.map(attend, (q_blocks, jnp.arange(n_blocks)))
    return o.transpose(1, 0, 2, 3, 4).reshape(b, s, MLA_HEADS * MLA_V)


def _fwd_setup_inputs(seed: int = 0) -> dict:
    key = jax.random.key(seed)
    ks = jax.random.split(key, 24)
    f32 = jnp.float32
    L = DEPTH

    def w(k, shape, fan_in):
        return jax.random.normal(k, shape, f32) * (fan_in ** -0.5)

    def gain(k, shape):
        return 1.0 + 0.05 * jax.random.normal(k, shape, f32)

    x = jax.random.normal(ks[0], (BATCH, SEQ, D_MODEL), f32)
    offsets = jax.random.randint(ks[1], (BATCH, 1), 0, 64, dtype=jnp.int32) * CHUNK
    positions = (offsets + jnp.arange(SEQ, dtype=jnp.int32)[None, :]).astype(jnp.int32)
    return {
        'x': x,
        'positions': positions,
        'ffn1_norm': gain(ks[2], (L, D_MODEL)),
        'ffn1_w_in': w(ks[3], (L, D_MODEL, 2 * D_FF), D_MODEL),
        'ffn1_w_out': w(ks[4], (L, D_FF, D_MODEL), D_FF),
        'mix_norm': gain(ks[5], (L, D_MODEL)),
        'w_in': w(ks[6], (L, D_MODEL, IN_COLS), D_MODEL),
        'hg_lb_table': 0.5 * jax.random.normal(ks[7], (L + 1, HG_WIDTH), f32),
        'hg_out_norm': gain(ks[8], (L, HG_HEAD_V)),
        'w_hg_branch': w(ks[9], (L, HG_VWIDTH, D_MODEL), HG_VWIDTH),
        'mla_q_lora_norm': gain(ks[10], (L, Q_LORA)),
        'w_q_up': w(ks[11], (L, Q_LORA, MLA_HEADS * MLA_QK), Q_LORA),
        'mla_kv_lora_norm': gain(ks[12], (L, KV_LORA)),
        'w_kv_up': w(ks[13], (L, KV_LORA, MLA_HEADS * (MLA_NOPE + MLA_V)), KV_LORA),
        'q_head_norm': gain(ks[14], (L, MLA_QK)),
        'k_head_norm': gain(ks[15], (L, MLA_QK)),
        'w_mla_branch': w(ks[16], (L, MLA_HEADS * MLA_V, D_MODEL), MLA_HEADS * MLA_V),
        'w_merge': w(ks[17], (L, D_MODEL, 2 * D_MODEL), D_MODEL),
        'b_merge': 0.02 * jax.random.normal(ks[18], (L, 2 * D_MODEL), f32),
        'w_out': w(ks[19], (L, D_MODEL, D_MODEL), D_MODEL),
        'ffn2_norm': gain(ks[20], (L, D_MODEL)),
        'ffn2_w_in': w(ks[21], (L, D_MODEL, 2 * D_FF), D_MODEL),
        'ffn2_w_out': w(ks[22], (L, D_FF, D_MODEL), D_FF),
        'final_norm': gain(ks[23], (L, D_MODEL)),
    }


def _fwd_reference(x, positions, ffn1_norm, ffn1_w_in, ffn1_w_out, mix_norm, w_in, hg_lb_table,
              hg_out_norm, w_hg_branch, mla_q_lora_norm, w_q_up, mla_kv_lora_norm, w_kv_up,
              q_head_norm, k_head_norm, w_mla_branch, w_merge, b_merge, w_out,
              ffn2_norm, ffn2_w_in, ffn2_w_out, final_norm):
    lower_bounds = jnp.cumsum(jax.nn.softmax(hg_lb_table.astype(jnp.float32), axis=0), axis=0)
    split_at = np.cumsum(IN_SPLITS)[:-1].tolist()
    h = x
    for l in range(DEPTH):
        h = h + 0.5 * _swiglu(_rms_norm(h, ffn1_norm[l]), ffn1_w_in[l], ffn1_w_out[l])
        u = _rms_norm(h, mix_norm[l])
        hg_q, hg_f, hg_i, hg_g, c_q, c_kv, k_pe = jnp.split(u @ w_in[l], split_at, axis=-1)
        y_hg = _hgrn2(hg_q, hg_f, hg_i, hg_g, lower_bounds[l], hg_out_norm[l]) @ w_hg_branch[l]
        y_mla = _mla(c_q, c_kv, k_pe, positions, mla_q_lora_norm[l], w_q_up[l],
                     mla_kv_lora_norm[l], w_kv_up[l], q_head_norm[l], k_head_norm[l]) @ w_mla_branch[l]
        g_hg, g_mla = jnp.split(jax.nn.sigmoid(u @ w_merge[l] + b_merge[l]), 2, axis=-1)
        h = h + (g_hg * y_hg + g_mla * y_mla) @ w_out[l]
        h = h + 0.5 * _swiglu(_rms_norm(h, ffn2_norm[l]), ffn2_w_in[l], ffn2_w_out[l])
        h = _rms_norm(h, final_norm[l])
    return h


import jax as _jax
import jax.numpy as _jnp

TWIN_FORMAT = 'train_step'
FWD_PARAMS = ['x', 'positions', 'ffn1_norm', 'ffn1_w_in', 'ffn1_w_out', 'mix_norm', 'w_in', 'hg_lb_table', 'hg_out_norm', 'w_hg_branch', 'mla_q_lora_norm', 'w_q_up', 'mla_kv_lora_norm', 'w_kv_up', 'q_head_norm', 'k_head_norm', 'w_mla_branch', 'w_merge', 'b_merge', 'w_out', 'ffn2_norm', 'ffn2_w_in', 'ffn2_w_out', 'final_norm']
TWIN_WEIGHTS = ['ffn1_norm', 'ffn1_w_in', 'ffn1_w_out', 'mix_norm', 'w_in', 'hg_lb_table', 'hg_out_norm', 'w_hg_branch', 'mla_q_lora_norm', 'w_q_up', 'mla_kv_lora_norm', 'w_kv_up', 'q_head_norm', 'k_head_norm', 'w_mla_branch', 'w_merge', 'b_merge', 'w_out', 'ffn2_norm', 'ffn2_w_in', 'ffn2_w_out', 'final_norm']
TWIN_DIFF_INPUT = 'x'
TWIN_INPUTS = ['x', 'positions', 'ffn1_norm', 'ffn1_w_in', 'ffn1_w_out', 'mix_norm', 'w_in', 'hg_lb_table', 'hg_out_norm', 'w_hg_branch', 'mla_q_lora_norm', 'w_q_up', 'mla_kv_lora_norm', 'w_kv_up', 'q_head_norm', 'k_head_norm', 'w_mla_branch', 'w_merge', 'b_merge', 'w_out', 'ffn2_norm', 'ffn2_w_in', 'ffn2_w_out', 'final_norm', 'loss_target', 'm_ffn1_norm', 'm_ffn1_w_in', 'm_ffn1_w_out', 'm_mix_norm', 'm_w_in', 'm_hg_lb_table', 'm_hg_out_norm', 'm_w_hg_branch', 'm_mla_q_lora_norm', 'm_w_q_up', 'm_mla_kv_lora_norm', 'm_w_kv_up', 'm_q_head_norm', 'm_k_head_norm', 'm_w_mla_branch', 'm_w_merge', 'm_b_merge', 'm_w_out', 'm_ffn2_norm', 'm_ffn2_w_in', 'm_ffn2_w_out', 'm_final_norm', 'v_ffn1_norm', 'v_ffn1_w_in', 'v_ffn1_w_out', 'v_mix_norm', 'v_w_in', 'v_hg_lb_table', 'v_hg_out_norm', 'v_w_hg_branch', 'v_mla_q_lora_norm', 'v_w_q_up', 'v_mla_kv_lora_norm', 'v_w_kv_up', 'v_q_head_norm', 'v_k_head_norm', 'v_w_mla_branch', 'v_w_merge', 'v_b_merge', 'v_w_out', 'v_ffn2_norm', 'v_ffn2_w_in', 'v_ffn2_w_out', 'v_final_norm']
TWIN_OUTPUTS = ['loss', 'grad_x', 'grad_ffn1_norm', 'grad_ffn1_w_in', 'grad_ffn1_w_out', 'grad_mix_norm', 'grad_w_in', 'grad_hg_lb_table', 'grad_hg_out_norm', 'grad_w_hg_branch', 'grad_mla_q_lora_norm', 'grad_w_q_up', 'grad_mla_kv_lora_norm', 'grad_w_kv_up', 'grad_q_head_norm', 'grad_k_head_norm', 'grad_w_mla_branch', 'grad_w_merge', 'grad_b_merge', 'grad_w_out', 'grad_ffn2_norm', 'grad_ffn2_w_in', 'grad_ffn2_w_out', 'grad_final_norm', 'delta_ffn1_norm', 'delta_ffn1_w_in', 'delta_ffn1_w_out', 'delta_mix_norm', 'delta_w_in', 'delta_hg_lb_table', 'delta_hg_out_norm', 'delta_w_hg_branch', 'delta_mla_q_lora_norm', 'delta_w_q_up', 'delta_mla_kv_lora_norm', 'delta_w_kv_up', 'delta_q_head_norm', 'delta_k_head_norm', 'delta_w_mla_branch', 'delta_w_merge', 'delta_b_merge', 'delta_w_out', 'delta_ffn2_norm', 'delta_ffn2_w_in', 'delta_ffn2_w_out', 'delta_final_norm', 'new_m_ffn1_norm', 'new_m_ffn1_w_in', 'new_m_ffn1_w_out', 'new_m_mix_norm', 'new_m_w_in', 'new_m_hg_lb_table', 'new_m_hg_out_norm', 'new_m_w_hg_branch', 'new_m_mla_q_lora_norm', 'new_m_w_q_up', 'new_m_mla_kv_lora_norm', 'new_m_w_kv_up', 'new_m_q_head_norm', 'new_m_k_head_norm', 'new_m_w_mla_branch', 'new_m_w_merge', 'new_m_b_merge', 'new_m_w_out', 'new_m_ffn2_norm', 'new_m_ffn2_w_in', 'new_m_ffn2_w_out', 'new_m_final_norm', 'new_v_ffn1_norm', 'new_v_ffn1_w_in', 'new_v_ffn1_w_out', 'new_v_mix_norm', 'new_v_w_in', 'new_v_hg_lb_table', 'new_v_hg_out_norm', 'new_v_w_hg_branch', 'new_v_mla_q_lora_norm', 'new_v_w_q_up', 'new_v_mla_kv_lora_norm', 'new_v_w_kv_up', 'new_v_q_head_norm', 'new_v_k_head_norm', 'new_v_w_mla_branch', 'new_v_w_merge', 'new_v_b_merge', 'new_v_w_out', 'new_v_ffn2_norm', 'new_v_ffn2_w_in', 'new_v_ffn2_w_out', 'new_v_final_norm']
TWIN_LEAF_KINDS = {'loss': 'loss', 'grad_x': 'grad_x', 'grad_ffn1_norm': 'grad_w', 'grad_ffn1_w_in': 'grad_w', 'grad_ffn1_w_out': 'grad_w', 'grad_mix_norm': 'grad_w', 'grad_w_in': 'grad_w', 'grad_hg_lb_table': 'grad_w', 'grad_hg_out_norm': 'grad_w', 'grad_w_hg_branch': 'grad_w', 'grad_mla_q_lora_norm': 'grad_w', 'grad_w_q_up': 'grad_w', 'grad_mla_kv_lora_norm': 'grad_w', 'grad_w_kv_up': 'grad_w', 'grad_q_head_norm': 'grad_w', 'grad_k_head_norm': 'grad_w', 'grad_w_mla_branch': 'grad_w', 'grad_w_merge': 'grad_w', 'grad_b_merge': 'grad_w', 'grad_w_out': 'grad_w', 'grad_ffn2_norm': 'grad_w', 'grad_ffn2_w_in': 'grad_w', 'grad_ffn2_w_out': 'grad_w', 'grad_final_norm': 'grad_w', 'delta_ffn1_norm': 'delta_w', 'delta_ffn1_w_in': 'delta_w', 'delta_ffn1_w_out': 'delta_w', 'delta_mix_norm': 'delta_w', 'delta_w_in': 'delta_w', 'delta_hg_lb_table': 'delta_w', 'delta_hg_out_norm': 'delta_w', 'delta_w_hg_branch': 'delta_w', 'delta_mla_q_lora_norm': 'delta_w', 'delta_w_q_up': 'delta_w', 'delta_mla_kv_lora_norm': 'delta_w', 'delta_w_kv_up': 'delta_w', 'delta_q_head_norm': 'delta_w', 'delta_k_head_norm': 'delta_w', 'delta_w_mla_branch': 'delta_w', 'delta_w_merge': 'delta_w', 'delta_b_merge': 'delta_w', 'delta_w_out': 'delta_w', 'delta_ffn2_norm': 'delta_w', 'delta_ffn2_w_in': 'delta_w', 'delta_ffn2_w_out': 'delta_w', 'delta_final_norm': 'delta_w', 'new_m_ffn1_norm': 'new_m', 'new_m_ffn1_w_in': 'new_m', 'new_m_ffn1_w_out': 'new_m', 'new_m_mix_norm': 'new_m', 'new_m_w_in': 'new_m', 'new_m_hg_lb_table': 'new_m', 'new_m_hg_out_norm': 'new_m', 'new_m_w_hg_branch': 'new_m', 'new_m_mla_q_lora_norm': 'new_m', 'new_m_w_q_up': 'new_m', 'new_m_mla_kv_lora_norm': 'new_m', 'new_m_w_kv_up': 'new_m', 'new_m_q_head_norm': 'new_m', 'new_m_k_head_norm': 'new_m', 'new_m_w_mla_branch': 'new_m', 'new_m_w_merge': 'new_m', 'new_m_b_merge': 'new_m', 'new_m_w_out': 'new_m', 'new_m_ffn2_norm': 'new_m', 'new_m_ffn2_w_in': 'new_m', 'new_m_ffn2_w_out': 'new_m', 'new_m_final_norm': 'new_m', 'new_v_ffn1_norm': 'new_v', 'new_v_ffn1_w_in': 'new_v', 'new_v_ffn1_w_out': 'new_v', 'new_v_mix_norm': 'new_v', 'new_v_w_in': 'new_v', 'new_v_hg_lb_table': 'new_v', 'new_v_hg_out_norm': 'new_v', 'new_v_w_hg_branch': 'new_v', 'new_v_mla_q_lora_norm': 'new_v', 'new_v_w_q_up': 'new_v', 'new_v_mla_kv_lora_norm': 'new_v', 'new_v_w_kv_up': 'new_v', 'new_v_q_head_norm': 'new_v', 'new_v_k_head_norm': 'new_v', 'new_v_w_mla_branch': 'new_v', 'new_v_w_merge': 'new_v', 'new_v_b_merge': 'new_v', 'new_v_w_out': 'new_v', 'new_v_ffn2_norm': 'new_v', 'new_v_ffn2_w_in': 'new_v', 'new_v_ffn2_w_out': 'new_v', 'new_v_final_norm': 'new_v'}


def _forward(args):
    return _fwd_reference(*[args[k] for k in FWD_PARAMS])


def _output_shape():
    out = _jax.eval_shape(lambda: _forward(_fwd_setup_inputs(0)))
    return out.shape, out.dtype

N_MICROBATCH = 1
ADAM_LR = 0.001
ADAM_B1 = 0.9
ADAM_B2 = 0.999
ADAM_EPS = 1e-08
ADAM_WD = 0.01
ADAM_STEP = 10
PER_EXAMPLE_BATCH_AXIS = {'x': 0, 'positions': 0, 'loss_target': 0}
SHARED_INPUTS = []
_WEIGHT_DTYPES = {'ffn1_norm': _jnp.float32, 'ffn1_w_in': _jnp.float32, 'ffn1_w_out': _jnp.float32, 'mix_norm': _jnp.float32, 'w_in': _jnp.float32, 'hg_lb_table': _jnp.float32, 'hg_out_norm': _jnp.float32, 'w_hg_branch': _jnp.float32, 'mla_q_lora_norm': _jnp.float32, 'w_q_up': _jnp.float32, 'mla_kv_lora_norm': _jnp.float32, 'w_kv_up': _jnp.float32, 'q_head_norm': _jnp.float32, 'k_head_norm': _jnp.float32, 'w_mla_branch': _jnp.float32, 'w_merge': _jnp.float32, 'b_merge': _jnp.float32, 'w_out': _jnp.float32, 'ffn2_norm': _jnp.float32, 'ffn2_w_in': _jnp.float32, 'ffn2_w_out': _jnp.float32, 'final_norm': _jnp.float32}
MOMENT_SCALE = {'ffn1_norm': 1.170331e-01, 'ffn1_w_in': 4.852386e-02, 'ffn1_w_out': 7.924507e-02, 'mix_norm': 1.235454e-01, 'w_in': 5.384164e-02, 'hg_lb_table': 6.785651e-03, 'hg_out_norm': 2.486564e-01, 'w_hg_branch': 7.848575e-02, 'mla_q_lora_norm': 3.159214e-02, 'w_q_up': 1.601653e-02, 'mla_kv_lora_norm': 5.714225e-02, 'w_kv_up': 1.865645e-02, 'q_head_norm': 4.653411e-02, 'k_head_norm': 4.695426e-02, 'w_mla_branch': 2.052524e-02, 'w_merge': 2.206453e-02, 'b_merge': 2.153616e-02, 'w_out': 7.969619e-02, 'ffn2_norm': 9.881469e-02, 'ffn2_w_in': 4.120699e-02, 'ffn2_w_out': 6.743422e-02, 'final_norm': 6.392565e+01}


def _to_microbatches(a, axis):
    t = _jnp.moveaxis(a, axis, 0)
    t = t.reshape((N_MICROBATCH, t.shape[0] // N_MICROBATCH) + t.shape[1:])
    return _jnp.moveaxis(t, 1, axis + 1)


def setup_inputs(seed: int = 0) -> dict:
    inp = _fwd_setup_inputs(seed)
    key = _jax.random.fold_in(_jax.random.key(seed), 7919)
    shape, _ = _output_shape()
    out = dict(inp)
    out["loss_target"] = _jax.random.normal(_jax.random.fold_in(key, 0), shape, _jnp.float32)
    for i, name in enumerate(TWIN_WEIGHTS):
        w = inp[name].astype(_jnp.float32)
        if MOMENT_SCALE is None:
            s = _jnp.sqrt(_jnp.mean(_jnp.square(w)) + 1e-30)
        else:
            s = MOMENT_SCALE[name]
        km, kv = _jax.random.split(_jax.random.fold_in(key, i + 1))
        out[name] = w
        out["m_" + name] = s * _jax.random.normal(km, w.shape, _jnp.float32)
        out["v_" + name] = (s * s) * _jax.random.uniform(kv, w.shape, _jnp.float32, 0.5, 1.5)
    if N_MICROBATCH > 1:
        for name, axis in PER_EXAMPLE_BATCH_AXIS.items():
            out[name] = _to_microbatches(out[name], axis)
    return {'x': out['x'], 'positions': out['positions'], 'ffn1_norm': out['ffn1_norm'], 'ffn1_w_in': out['ffn1_w_in'], 'ffn1_w_out': out['ffn1_w_out'], 'mix_norm': out['mix_norm'], 'w_in': out['w_in'], 'hg_lb_table': out['hg_lb_table'], 'hg_out_norm': out['hg_out_norm'], 'w_hg_branch': out['w_hg_branch'], 'mla_q_lora_norm': out['mla_q_lora_norm'], 'w_q_up': out['w_q_up'], 'mla_kv_lora_norm': out['mla_kv_lora_norm'], 'w_kv_up': out['w_kv_up'], 'q_head_norm': out['q_head_norm'], 'k_head_norm': out['k_head_norm'], 'w_mla_branch': out['w_mla_branch'], 'w_merge': out['w_merge'], 'b_merge': out['b_merge'], 'w_out': out['w_out'], 'ffn2_norm': out['ffn2_norm'], 'ffn2_w_in': out['ffn2_w_in'], 'ffn2_w_out': out['ffn2_w_out'], 'final_norm': out['final_norm'], 'loss_target': out['loss_target'], 'm_ffn1_norm': out['m_ffn1_norm'], 'm_ffn1_w_in': out['m_ffn1_w_in'], 'm_ffn1_w_out': out['m_ffn1_w_out'], 'm_mix_norm': out['m_mix_norm'], 'm_w_in': out['m_w_in'], 'm_hg_lb_table': out['m_hg_lb_table'], 'm_hg_out_norm': out['m_hg_out_norm'], 'm_w_hg_branch': out['m_w_hg_branch'], 'm_mla_q_lora_norm': out['m_mla_q_lora_norm'], 'm_w_q_up': out['m_w_q_up'], 'm_mla_kv_lora_norm': out['m_mla_kv_lora_norm'], 'm_w_kv_up': out['m_w_kv_up'], 'm_q_head_norm': out['m_q_head_norm'], 'm_k_head_norm': out['m_k_head_norm'], 'm_w_mla_branch': out['m_w_mla_branch'], 'm_w_merge': out['m_w_merge'], 'm_b_merge': out['m_b_merge'], 'm_w_out': out['m_w_out'], 'm_ffn2_norm': out['m_ffn2_norm'], 'm_ffn2_w_in': out['m_ffn2_w_in'], 'm_ffn2_w_out': out['m_ffn2_w_out'], 'm_final_norm': out['m_final_norm'], 'v_ffn1_norm': out['v_ffn1_norm'], 'v_ffn1_w_in': out['v_ffn1_w_in'], 'v_ffn1_w_out': out['v_ffn1_w_out'], 'v_mix_norm': out['v_mix_norm'], 'v_w_in': out['v_w_in'], 'v_hg_lb_table': out['v_hg_lb_table'], 'v_hg_out_norm': out['v_hg_out_norm'], 'v_w_hg_branch': out['v_w_hg_branch'], 'v_mla_q_lora_norm': out['v_mla_q_lora_norm'], 'v_w_q_up': out['v_w_q_up'], 'v_mla_kv_lora_norm': out['v_mla_kv_lora_norm'], 'v_w_kv_up': out['v_w_kv_up'], 'v_q_head_norm': out['v_q_head_norm'], 'v_k_head_norm': out['v_k_head_norm'], 'v_w_mla_branch': out['v_w_mla_branch'], 'v_w_merge': out['v_w_merge'], 'v_b_merge': out['v_b_merge'], 'v_w_out': out['v_w_out'], 'v_ffn2_norm': out['v_ffn2_norm'], 'v_ffn2_w_in': out['v_ffn2_w_in'], 'v_ffn2_w_out': out['v_ffn2_w_out'], 'v_final_norm': out['v_final_norm']}


def _loss(weights, diff, rest, loss_target):
    with _jax.named_scope("forward"):
        args = {**rest, TWIN_DIFF_INPUT: diff, **{k: w.astype(_WEIGHT_DTYPES[k]) for k, w in weights.items()}}
        y = _forward(args)
    with _jax.named_scope("loss_head"):
        err = _jnp.square(y.astype(_jnp.float32) - loss_target)
        return 0.5 * _jnp.sum(_jnp.mean(err, axis=-1)) if err.ndim else 0.5 * err


def _adamw(w, g, m, v):
    m = ADAM_B1 * m + (1.0 - ADAM_B1) * g
    v = ADAM_B2 * v + (1.0 - ADAM_B2) * _jnp.square(g)
    m_hat = m / (1.0 - ADAM_B1 ** ADAM_STEP)
    v_hat = v / (1.0 - ADAM_B2 ** ADAM_STEP)
    delta = -ADAM_LR * (m_hat / (_jnp.sqrt(v_hat) + ADAM_EPS) + ADAM_WD * w)
    return delta, m, v


def reference(x, positions, ffn1_norm, ffn1_w_in, ffn1_w_out, mix_norm, w_in, hg_lb_table, hg_out_norm, w_hg_branch, mla_q_lora_norm, w_q_up, mla_kv_lora_norm, w_kv_up, q_head_norm, k_head_norm, w_mla_branch, w_merge, b_merge, w_out, ffn2_norm, ffn2_w_in, ffn2_w_out, final_norm, loss_target, m_ffn1_norm, m_ffn1_w_in, m_ffn1_w_out, m_mix_norm, m_w_in, m_hg_lb_table, m_hg_out_norm, m_w_hg_branch, m_mla_q_lora_norm, m_w_q_up, m_mla_kv_lora_norm, m_w_kv_up, m_q_head_norm, m_k_head_norm, m_w_mla_branch, m_w_merge, m_b_merge, m_w_out, m_ffn2_norm, m_ffn2_w_in, m_ffn2_w_out, m_final_norm, v_ffn1_norm, v_ffn1_w_in, v_ffn1_w_out, v_mix_norm, v_w_in, v_hg_lb_table, v_hg_out_norm, v_w_hg_branch, v_mla_q_lora_norm, v_w_q_up, v_mla_kv_lora_norm, v_w_kv_up, v_q_head_norm, v_k_head_norm, v_w_mla_branch, v_w_merge, v_b_merge, v_w_out, v_ffn2_norm, v_ffn2_w_in, v_ffn2_w_out, v_final_norm):
    given = dict(x=x, positions=positions, ffn1_norm=ffn1_norm, ffn1_w_in=ffn1_w_in, ffn1_w_out=ffn1_w_out, mix_norm=mix_norm, w_in=w_in, hg_lb_table=hg_lb_table, hg_out_norm=hg_out_norm, w_hg_branch=w_hg_branch, mla_q_lora_norm=mla_q_lora_norm, w_q_up=w_q_up, mla_kv_lora_norm=mla_kv_lora_norm, w_kv_up=w_kv_up, q_head_norm=q_head_norm, k_head_norm=k_head_norm, w_mla_branch=w_mla_branch, w_merge=w_merge, b_merge=b_merge, w_out=w_out, ffn2_norm=ffn2_norm, ffn2_w_in=ffn2_w_in, ffn2_w_out=ffn2_w_out, final_norm=final_norm, loss_target=loss_target, m_ffn1_norm=m_ffn1_norm, m_ffn1_w_in=m_ffn1_w_in, m_ffn1_w_out=m_ffn1_w_out, m_mix_norm=m_mix_norm, m_w_in=m_w_in, m_hg_lb_table=m_hg_lb_table, m_hg_out_norm=m_hg_out_norm, m_w_hg_branch=m_w_hg_branch, m_mla_q_lora_norm=m_mla_q_lora_norm, m_w_q_up=m_w_q_up, m_mla_kv_lora_norm=m_mla_kv_lora_norm, m_w_kv_up=m_w_kv_up, m_q_head_norm=m_q_head_norm, m_k_head_norm=m_k_head_norm, m_w_mla_branch=m_w_mla_branch, m_w_merge=m_w_merge, m_b_merge=m_b_merge, m_w_out=m_w_out, m_ffn2_norm=m_ffn2_norm, m_ffn2_w_in=m_ffn2_w_in, m_ffn2_w_out=m_ffn2_w_out, m_final_norm=m_final_norm, v_ffn1_norm=v_ffn1_norm, v_ffn1_w_in=v_ffn1_w_in, v_ffn1_w_out=v_ffn1_w_out, v_mix_norm=v_mix_norm, v_w_in=v_w_in, v_hg_lb_table=v_hg_lb_table, v_hg_out_norm=v_hg_out_norm, v_w_hg_branch=v_w_hg_branch, v_mla_q_lora_norm=v_mla_q_lora_norm, v_w_q_up=v_w_q_up, v_mla_kv_lora_norm=v_mla_kv_lora_norm, v_w_kv_up=v_w_kv_up, v_q_head_norm=v_q_head_norm, v_k_head_norm=v_k_head_norm, v_w_mla_branch=v_w_mla_branch, v_w_merge=v_w_merge, v_b_merge=v_b_merge, v_w_out=v_w_out, v_ffn2_norm=v_ffn2_norm, v_ffn2_w_in=v_ffn2_w_in, v_ffn2_w_out=v_ffn2_w_out, v_final_norm=v_final_norm)
    weights = {n: given[n] for n in TWIN_WEIGHTS}
    shared = {n: given[n] for n in SHARED_INPUTS}
    per_example = {n: given[n] for n in ['x', 'positions']}
    grad_fn = _jax.value_and_grad(_loss, argnums=(0, 1))

    def one_microbatch(ex, loss_target):
        ex = dict(ex)
        diff = ex.pop(TWIN_DIFF_INPUT)
        return grad_fn(weights, diff, {**shared, **ex}, loss_target)

    if N_MICROBATCH == 1:
        loss, (grad_w, grad_x) = one_microbatch(per_example, given["loss_target"])
    else:
        def body(carry, xs):
            loss_sum, grad_sum = carry
            l_k, (gw_k, gx_k) = one_microbatch(xs[0], xs[1])
            with _jax.named_scope("update"):
                return (loss_sum + l_k, _jax.tree.map(_jnp.add, grad_sum, gw_k)), gx_k

        init = (_jnp.zeros((), _jnp.float32), _jax.tree.map(_jnp.zeros_like, weights))
        (loss, grad_w), grad_x = _jax.lax.scan(body, init, (per_example, given["loss_target"]))
    with _jax.named_scope("update"):
        delta_w, new_m, new_v = {}, {}, {}
        for n in TWIN_WEIGHTS:
            delta_w[n], new_m[n], new_v[n] = _adamw(weights[n], grad_w[n], given["m_" + n], given["v_" + n])
    return (loss, grad_x, *[grad_w[n] for n in TWIN_WEIGHTS], *[delta_w[n] for n in TWIN_WEIGHTS],
            *[new_m[n] for n in TWIN_WEIGHTS], *[new_v[n] for n in TWIN_WEIGHTS])
```

```python
import functools

import jax
import jax.numpy as jnp
from jax import lax
from jax.experimental import pallas as pl
from jax.experimental.pallas import tpu as pltpu

F32 = jnp.float32
BF16 = jnp.bfloat16

D = 1024
FF = 2816
NH = 8
HD = 128
ROPE = 64
QK = HD + ROPE
QL = 384
KVL = 256
HGW = NH * HD
CHUNK = 64
EPS = 1e-6
ROPE_THETA = 10000.0
SCALE = QK ** -0.5

LR, B1, B2, AEPS, WD, STEP = 0.001, 0.9, 0.999, 1e-08, 0.01, 10

HB = 128
SUB = 16
EXP_CLAMP = 80.0

LANE = 128
VMEM_LIMIT = 56 << 20

N_DEV = 8
MESH = pl.DeviceIdType.MESH

BIG = (
    ("ffn1_w_in", (D, 2 * FF), 1), ("ffn1_w_out", (FF, D), 0), ("w_in", (D, 4800), 1),
    ("w_hg_branch", (HGW, D), 0), ("w_q_up", (QL, NH * QK), 1), ("w_kv_up", (KVL, NH * 2 * HD), 1),
    ("w_mla_branch", (NH * HD, D), 0), ("w_merge", (D, 2 * D), 1), ("w_out", (D, D), 0),
    ("ffn2_w_in", (D, 2 * FF), 1), ("ffn2_w_out", (FF, D), 0),
)
SMALL = (
    ("ffn1_norm", (1, D)), ("mix_norm", (1, D)), ("hg_lb_table", (2, HGW)), ("hg_out_norm", (1, HD)),
    ("mla_q_lora_norm", (1, QL)), ("mla_kv_lora_norm", (1, KVL)), ("q_head_norm", (1, QK)),
    ("k_head_norm", (1, QK)), ("b_merge", (1, 2 * D)), ("ffn2_norm", (1, D)), ("final_norm", (1, D)),
)
WEIGHT_ORDER = ("ffn1_norm", "ffn1_w_in", "ffn1_w_out", "mix_norm", "w_in", "hg_lb_table", "hg_out_norm",
                "w_hg_branch", "mla_q_lora_norm", "w_q_up", "mla_kv_lora_norm", "w_kv_up", "q_head_norm",
                "k_head_norm", "w_mla_branch", "w_merge", "b_merge", "w_out", "ffn2_norm", "ffn2_w_in",
                "ffn2_w_out", "final_norm")
PACK_W = 1024


def _pick(n, cands):
    for c in cands:
        if n % c == 0:
            return c
    return n


def _params(sem):
    return pltpu.CompilerParams(dimension_semantics=sem, vmem_limit_bytes=VMEM_LIMIT)


def _sig(x):
    return 1.0 / (1.0 + jnp.exp(-x))


def _dot(a, b):
    return jnp.dot(a.astype(BF16), b.astype(BF16), preferred_element_type=F32)


def _dot_nt(a, b):
    return lax.dot_general(a.astype(BF16), b.astype(BF16), (((1,), (1,)), ((), ())),
                           preferred_element_type=F32)


def _dot_tn(a, b):
    return lax.dot_general(a.astype(BF16), b.astype(BF16), (((0,), (0,)), ((), ())),
                           preferred_element_type=F32)


def _split3(x):
    x1 = x.astype(BF16)
    r1 = x - x1.astype(F32)
    x2 = r1.astype(BF16)
    x3 = (r1 - x2.astype(F32)).astype(BF16)
    return x1, x2, x3


def _dot_sel(m, x):
    x1, x2, x3 = _split3(x)
    d = lambda p: jnp.dot(m, p, preferred_element_type=F32)
    return d(x1) + d(x2) + d(x3)


def _sel_dot(x, m):
    x1, x2, x3 = _split3(x)
    d = lambda p: jnp.dot(p, m, preferred_element_type=F32)
    return d(x1) + d(x2) + d(x3)


_TN = (1408, 1024, 768, 512, 384, 256, 128)


def _mm(a, b, mode, name, out_dtype=F32):
    if mode == "tn":
        t, m = a.shape
        n = b.shape[1]
        tt, tm, tn = _pick(t, (512, 256, 128)), _pick(m, _TN), _pick(n, _TN)

        def body(a_ref, b_ref, o_ref):
            @pl.when(pl.program_id(2) == 0)
            def _():
                o_ref[...] = jnp.zeros_like(o_ref)

            o_ref[...] += _dot_tn(a_ref[...], b_ref[...])

        return pl.pallas_call(
            body, name=name, grid=(m // tm, n // tn, t // tt),
            in_specs=[pl.BlockSpec((tt, tm), lambda i, j, k: (k, i)),
                      pl.BlockSpec((tt, tn), lambda i, j, k: (k, j))],
            out_specs=pl.BlockSpec((tm, tn), lambda i, j, k: (i, j)),
            out_shape=jax.ShapeDtypeStruct((m, n), F32),
            compiler_params=_params(("parallel", "parallel", "arbitrary")),
        )(a, b)

    m, k = a.shape
    tm = _pick(m, (512, 256, 128))
    if mode == "nn":
        n = b.shape[1]
        tn = _pick(n, _TN)
        b_spec = pl.BlockSpec((k, tn), lambda i, j: (0, j))
        dot = _dot
    else:
        n = b.shape[0]
        tn = _pick(n, _TN if k <= 4096 else (512, 256, 128))
        b_spec = pl.BlockSpec((tn, k), lambda i, j: (j, 0))
        dot = _dot_nt

    def body(a_ref, b_ref, o_ref):
        o_ref[...] = dot(a_ref[...], b_ref[...]).astype(o_ref.dtype)

    return pl.pallas_call(
        body, name=name, grid=(m // tm, n // tn),
        in_specs=[pl.BlockSpec((tm, k), lambda i, j: (i, 0)), b_spec],
        out_specs=pl.BlockSpec((tm, tn), lambda i, j: (i, j)),
        out_shape=jax.ShapeDtypeStruct((m, n), out_dtype),
        compiler_params=_params(("parallel", "parallel")),
    )(a, b)


def _rows(fn, name, t, tm, ins, vecs, outs, accs=()):
    n_in, n_out, n_acc = len(ins) + len(vecs), len(outs), len(accs)

    def body(*refs):
        res = fn(*refs[:n_in + n_out])
        if n_acc:
            acc_refs = refs[n_in + n_out:]

            @pl.when(pl.program_id(0) == 0)
            def _():
                for r in acc_refs:
                    r[...] = jnp.zeros_like(r)

            for r, val in zip(acc_refs, res):
                r[...] += val

    in_specs = [pl.BlockSpec((tm, bw), functools.partial(lambda i, cb: (i, cb), cb=cb)) for _, bw, cb in ins]
    in_specs += [pl.BlockSpec(v.shape, lambda i: (0, 0)) for v in vecs]
    out_specs = [pl.BlockSpec((tm, w), lambda i: (i, 0)) for w, _ in outs]
    out_specs += [pl.BlockSpec(s, lambda i: (0, 0)) for s in accs]
    out_shape = [jax.ShapeDtypeStruct((t, w), dt) for w, dt in outs]
    out_shape += [jax.ShapeDtypeStruct(s, F32) for s in accs]
    return pl.pallas_call(
        body, name=name, grid=(t // tm,), in_specs=in_specs, out_specs=out_specs, out_shape=out_shape,
        compiler_params=_params(("arbitrary",) if n_acc else ("parallel",)),
    )(*[a for a, _, _ in ins], *vecs)


def _rms(x, g):
    return x * lax.rsqrt(jnp.mean(x * x, axis=-1, keepdims=True) + EPS) * g


def _rms_bwd(x, g, dy):
    xh = x * lax.rsqrt(jnp.mean(x * x, axis=-1, keepdims=True) + EPS)
    r = lax.rsqrt(jnp.mean(x * x, axis=-1, keepdims=True) + EPS)
    dyg = dy * g
    dx = r * (dyg - xh * jnp.mean(dyg * xh, axis=-1, keepdims=True))
    return dx, jnp.sum(dy * xh, axis=0, keepdims=True)


def _hgrn_mats():
    row = lax.broadcasted_iota(jnp.int32, (HB, HB), 0)
    col = lax.broadcasted_iota(jnp.int32, (HB, HB), 1)
    return row, col


def _hgrn_gates(qr, z, t0, t1):
    lb = 1.0 / (1.0 + jnp.exp(t1 - t0))
    sz = _sig(z)
    sneg = 1.0 / (1.0 + jnp.exp(z))
    f = lb + (1.0 - lb) * sz
    return lb, sz, sneg, f, jnp.log(f), (1.0 - lb) * sneg, qr * _sig(qr)


def _hgrn_scores(q, k, cum, cmid):
    qd = q * jnp.exp(jnp.minimum(cmid, EXP_CLAMP))
    qd_b = qd.astype(BF16)
    kds, parts = [], []
    for i in range(HB // SUB):
        mid = cum[SUB * i + SUB // 2 - 1:SUB * i + SUB // 2, :]
        kd = k * jnp.exp(jnp.minimum(mid - cum, EXP_CLAMP))
        kds.append(kd)
        parts.append(_dot_nt(qd_b[SUB * i:SUB * (i + 1)], kd))
    return qd, kds, jnp.concatenate(parts, axis=0)


def _hgrn_fwd(p_hg, table, gain, t):
    nblk = t // HB

    def body(q_ref, f_ref, i_ref, g_ref, tab_ref, gain_ref, o_ref, y_ref, st_ref, state):
        @pl.when(pl.program_id(0) == 0)
        def _():
            state[...] = jnp.zeros_like(state)

        row, col = _hgrn_mats()
        causal = col <= row
        tri = causal.astype(BF16)
        trimid = tri - (col <= (row // SUB) * SUB + SUB // 2 - 1).astype(BF16)
        for h in range(NH):
            sl = slice(HD * h, HD * (h + 1))
            v, gr = i_ref[:, sl], g_ref[:, sl]
            _, _, _, _, lf, k, q = _hgrn_gates(q_ref[:, sl], f_ref[:, sl], tab_ref[0:1, sl], tab_ref[1:2, sl])
            cum = _dot_sel(tri, lf)
            cmid = _dot_sel(trimid, lf)
            _, _, s = _hgrn_scores(q, k, cum, cmid)
            p = jnp.where(causal, s, 0.0)
            st = state[h]
            st_ref[h, 0] = st
            o = _dot(p, v) + _dot_nt(q * jnp.exp(cum), st)
            last = cum[HB - 1:HB, :]
            state[h] = st * jnp.exp(last) + _dot_tn(v, k * jnp.exp(last - cum))
            o_ref[:, sl] = o
            y_ref[:, sl] = (_rms(o, gain_ref[...]) * gr * _sig(gr)).astype(BF16)

    blk = lambda cb: pl.BlockSpec((HB, HGW), functools.partial(lambda n, cb: (n, cb), cb=cb))
    return pl.pallas_call(
        body, name="hgrn_fwd", grid=(nblk,),
        in_specs=[blk(0), blk(1), blk(2), blk(3), pl.BlockSpec((2, HGW), lambda n: (0, 0)),
                  pl.BlockSpec((1, HD), lambda n: (0, 0))],
        out_specs=[pl.BlockSpec((HB, HGW), lambda n: (n, 0)), pl.BlockSpec((HB, HGW), lambda n: (n, 0)),
                   pl.BlockSpec((NH, 1, HD, HD), lambda n: (0, n, 0, 0))],
        out_shape=[jax.ShapeDtypeStruct((t, HGW), F32), jax.ShapeDtypeStruct((t, HGW), BF16),
                   jax.ShapeDtypeStruct((NH, nblk, HD, HD), F32)],
        scratch_shapes=[pltpu.VMEM((NH, HD, HD), F32)],
        compiler_params=_params(("arbitrary",)),
    )(p_hg, p_hg, p_hg, p_hg, table, gain)


def _hgrn_bwd(p_hg, table, gain, o_pre, states, dy, t):
    nblk = t // HB

    def body(q_ref, f_ref, i_ref, g_ref, tab_ref, gain_ref, o_ref, st_ref, dy_ref, dp_ref, dtab_ref, dgain_ref,
             dstate):
        @pl.when(pl.program_id(0) == 0)
        def _():
            dstate[...] = jnp.zeros_like(dstate)
            dtab_ref[...] = jnp.zeros_like(dtab_ref)
            dgain_ref[...] = jnp.zeros_like(dgain_ref)

        row, col = _hgrn_mats()
        causal = col <= row
        tri = causal.astype(BF16)
        midrow = (row // SUB) * SUB + SUB // 2 - 1
        trimid = tri - (col <= midrow).astype(BF16)
        tri_t = (row <= col).astype(BF16)
        midcol = (col // SUB) * SUB + SUB // 2 - 1
        trimid_t = tri_t - (row <= midcol).astype(BF16)
        dgain = jnp.zeros((1, HD), F32)
        for h in range(NH):
            sl = slice(HD * h, HD * (h + 1))
            qr, z, v, gr = q_ref[:, sl], f_ref[:, sl], i_ref[:, sl], g_ref[:, sl]
            lb, sz, sneg, f, lf, k, q = _hgrn_gates(qr, z, tab_ref[0:1, sl], tab_ref[1:2, sl])
            cum = _dot_sel(tri, lf)
            cmid = _dot_sel(trimid, lf)
            qd, kds, s = _hgrn_scores(q, k, cum, cmid)
            p = jnp.where(causal, s, 0.0)
            st = st_ref[h, 0]
            dst = dstate[h]
            o = o_ref[:, sl]
            sg = _sig(gr)
            dyh = dy_ref[:, sl]
            on = _rms(o, gain_ref[...])
            dgr = dyh * on * sg * (1.0 + gr * (1.0 - sg))
            do, dg_h = _rms_bwd(o, gain_ref[...], dyh * gr * sg)
            dgain = dgain + dg_h
            do_b = do.astype(BF16)
            ecum = jnp.exp(cum)
            qc = q * ecum
            last = cum[HB - 1:HB, :]
            edec = jnp.exp(last - cum)
            kdec = k * edec
            dp = jnp.where(causal, _dot_nt(do_b, v), 0.0)
            dv = _dot(p.T, do_b) + _dot_nt(kdec, dst)
            dqc = _dot(do_b, st)
            dkdec = _dot(v, dst)
            dstate[h] = dst * jnp.exp(last) + _dot(do.T, qc)
            dp_b = dp.astype(BF16)
            dqd = jnp.concatenate([_dot(dp_b[SUB * i:SUB * (i + 1)], kds[i]) for i in range(HB // SUB)], axis=0)
            gq = dqd * qd
            dq = dqd * jnp.exp(jnp.minimum(cmid, EXP_CLAMP)) + dqc * ecum
            gs = dkdec * kdec
            dk = dkdec * edec
            dcum = dqc * qc - gs
            dcum = dcum + jnp.where(row == HB - 1, jnp.sum(gs, axis=0, keepdims=True)
                                    + jnp.exp(last) * jnp.sum(st * dst, axis=0, keepdims=True), 0.0)
            qd_b = qd.astype(BF16)
            for i in range(HB // SUB):
                dkd = _dot_tn(dp_b[SUB * i:SUB * (i + 1)], qd_b[SUB * i:SUB * (i + 1)])
                mid = cum[SUB * i + SUB // 2 - 1:SUB * i + SUB // 2, :]
                dk = dk + dkd * jnp.exp(jnp.minimum(mid - cum, EXP_CLAMP))
                gk = dkd * kds[i]
                dcum = dcum - gk + jnp.where(row == SUB * i + SUB // 2 - 1, jnp.sum(gk, axis=0, keepdims=True), 0.0)
            dlf = _dot_sel(tri_t, dcum) + _dot_sel(trimid_t, gq)
            df = dlf / f - dk
            dz = df * (1.0 - lb) * sz * sneg
            dlb = jnp.sum(df * sneg, axis=0, keepdims=True) * lb * (1.0 - lb)
            dtab_ref[0:1, sl] += dlb
            dtab_ref[1:2, sl] -= dlb
            sq = _sig(qr)
            dp_ref[:, sl] = (dq * sq * (1.0 + qr * (1.0 - sq))).astype(BF16)
            dp_ref[:, HGW + HD * h:HGW + HD * (h + 1)] = dz.astype(BF16)
            dp_ref[:, 2 * HGW + HD * h:2 * HGW + HD * (h + 1)] = dv.astype(BF16)
            dp_ref[:, 3 * HGW + HD * h:3 * HGW + HD * (h + 1)] = dgr.astype(BF16)
        dgain_ref[...] += dgain

    rev = lambda cb: pl.BlockSpec((HB, HGW), functools.partial(lambda n, cb: (nblk - 1 - n, cb), cb=cb))
    return pl.pallas_call(
        body, name="hgrn_bwd", grid=(nblk,),
        in_specs=[rev(0), rev(1), rev(2), rev(3), pl.BlockSpec((2, HGW), lambda n: (0, 0)),
                  pl.BlockSpec((1, HD), lambda n: (0, 0)), rev(0),
                  pl.BlockSpec((NH, 1, HD, HD), lambda n: (0, nblk - 1 - n, 0, 0)), rev(0)],
        out_specs=[pl.BlockSpec((HB, 4 * HGW), lambda n: (nblk - 1 - n, 0)),
                   pl.BlockSpec((2, HGW), lambda n: (0, 0)), pl.BlockSpec((1, HD), lambda n: (0, 0))],
        out_shape=[jax.ShapeDtypeStruct((t, 4 * HGW), BF16), jax.ShapeDtypeStruct((2, HGW), F32),
                   jax.ShapeDtypeStruct((1, HD), F32)],
        scratch_shapes=[pltpu.VMEM((NH, HD, HD), F32)],
        compiler_params=_params(("arbitrary",)),
    )(p_hg, p_hg, p_hg, p_hg, table, gain, o_pre, states, dy)


def _rope_mat():
    r = lax.broadcasted_iota(jnp.int32, (ROPE, ROPE), 0)
    c = lax.broadcasted_iota(jnp.int32, (ROPE, ROPE), 1)
    half = ROPE // 2
    return ((r == c - half).astype(F32) - (r == c + half).astype(F32)).astype(BF16)


def _mla_prep_fwd(p_mla, cs, sn, wqa, wqr, wk, wv, gql, gkvl, gq, gk, t):
    tm = _pick(t, (256, 128))

    def body(p_ref, cs_ref, sn_ref, wqa_ref, wqr_ref, wk_ref, wv_ref, gql_ref, gkvl_ref, gq_ref, gk_ref,
             q_ref, k_ref, v_ref):
        rmat = _rope_mat()
        cqn = _rms(p_ref[:, 0:QL], gql_ref[...]).astype(BF16)
        ckvn = _rms(p_ref[:, QL:QL + KVL], gkvl_ref[...]).astype(BF16)
        kpe = p_ref[:, QL + KVL:QL + KVL + ROPE]
        c, s = cs_ref[...], sn_ref[...]
        rot = lambda x: x * c + _sel_dot(x, rmat) * s
        for h in range(NH):
            qa, qr = _dot(cqn, wqa_ref[h]), _dot(cqn, wqr_ref[h])
            rq = lax.rsqrt((jnp.sum(qa * qa, -1, keepdims=True) + jnp.sum(qr * qr, -1, keepdims=True)) / QK + EPS)
            q_ref[h, :, 0:HD] = (qa * rq * gq_ref[:, 0:HD] * SCALE).astype(BF16)
            q_ref[h, :, HD:QK] = (rot(qr * rq * gq_ref[:, HD:QK]) * SCALE).astype(BF16)
            kn = _dot(ckvn, wk_ref[h])
            rk = lax.rsqrt((jnp.sum(kn * kn, -1, keepdims=True) + jnp.sum(kpe * kpe, -1, keepdims=True)) / QK + EPS)
            k_ref[h, :, 0:HD] = (kn * rk * gk_ref[:, 0:HD]).astype(BF16)
            k_ref[h, :, HD:QK] = rot(kpe * rk * gk_ref[:, HD:QK]).astype(BF16)
            v_ref[h] = _dot(ckvn, wv_ref[h]).astype(BF16)

    whole = lambda a: pl.BlockSpec(a.shape, functools.partial(lambda i, nd: (0,) * nd, nd=a.ndim))
    return pl.pallas_call(
        body, name="mla_prep_fwd", grid=(t // tm,),
        in_specs=[pl.BlockSpec((tm, QL + KVL + ROPE), lambda i: (i, 0)), pl.BlockSpec((tm, ROPE), lambda i: (i, 0)),
                  pl.BlockSpec((tm, ROPE), lambda i: (i, 0))] + [whole(a) for a in (wqa, wqr, wk, wv, gql, gkvl, gq, gk)],
        out_specs=[pl.BlockSpec((NH, tm, QK), lambda i: (0, i, 0)), pl.BlockSpec((NH, tm, QK), lambda i: (0, i, 0)),
                   pl.BlockSpec((NH, tm, HD), lambda i: (0, i, 0))],
        out_shape=[jax.ShapeDtypeStruct((NH, t, QK), BF16), jax.ShapeDtypeStruct((NH, t, QK), BF16),
                   jax.ShapeDtypeStruct((NH, t, HD), BF16)],
        compiler_params=_params(("parallel",)),
    )(p_mla, cs, sn, wqa, wqr, wk, wv, gql, gkvl, gq, gk)


def _mla_prep_bwd(p_mla, cs, sn, wqa, wqr, wk, wv, gql, gkvl, gq, gk, dq, dk, dv, t):
    tm = _pick(t, (256, 128))

    def body(p_ref, cs_ref, sn_ref, wqa_ref, wqr_ref, wk_ref, wv_ref, gql_ref, gkvl_ref, gq_ref, gk_ref,
             dq_ref, dk_ref, dv_ref,
             dp_ref, dwqa_ref, dwqr_ref, dwk_ref, dwv_ref, dgql_ref, dgkvl_ref, dgq_ref, dgk_ref):
        accs = (dwqa_ref, dwqr_ref, dwk_ref, dwv_ref, dgql_ref, dgkvl_ref, dgq_ref, dgk_ref)

        @pl.when(pl.program_id(0) == 0)
        def _():
            for r in accs:
                r[...] = jnp.zeros_like(r)

        rmat = _rope_mat()
        rmat_t = -rmat
        cq, ckv = p_ref[:, 0:QL], p_ref[:, QL:QL + KVL]
        kpe = p_ref[:, QL + KVL:QL + KVL + ROPE]
        cqn_f, ckvn_f = _rms(cq, gql_ref[...]), _rms(ckv, gkvl_ref[...])
        cqn, ckvn = cqn_f.astype(BF16), ckvn_f.astype(BF16)
        cqn_t, ckvn_t = cqn_f.T.astype(BF16), ckvn_f.T.astype(BF16)
        c, s = cs_ref[...], sn_ref[...]
        unrot = lambda dy: dy * c + _sel_dot(dy * s, rmat_t)
        dcqn = jnp.zeros((tm, QL), F32)
        dckvn = jnp.zeros((tm, KVL), F32)
        dkpe = jnp.zeros((tm, ROPE), F32)
        dgq_a, dgq_r = jnp.zeros((1, HD), F32), jnp.zeros((1, ROPE), F32)
        dgk_a, dgk_r = jnp.zeros((1, HD), F32), jnp.zeros((1, ROPE), F32)
        for h in range(NH):
            qa, qr = _dot(cqn, wqa_ref[h]), _dot(cqn, wqr_ref[h])
            rq = lax.rsqrt((jnp.sum(qa * qa, -1, keepdims=True) + jnp.sum(qr * qr, -1, keepdims=True)) / QK + EPS)
            xa, xr = qa * rq, qr * rq
            dya = dq_ref[h, :, 0:HD] * SCALE
            dyr = unrot(dq_ref[h, :, HD:QK] * SCALE)
            dgq_a += jnp.sum(dya * xa, axis=0, keepdims=True)
            dgq_r += jnp.sum(dyr * xr, axis=0, keepdims=True)
            ga, gr_ = dya * gq_ref[:, 0:HD], dyr * gq_ref[:, HD:QK]
            mean = (jnp.sum(ga * xa, -1, keepdims=True) + jnp.sum(gr_ * xr, -1, keepdims=True)) / QK
            dqa = (rq * (ga - xa * mean)).astype(BF16)
            dqr = (rq * (gr_ - xr * mean)).astype(BF16)
            dwqa_ref[h] += _dot(cqn_t, dqa)
            dwqr_ref[h] += _dot(cqn_t, dqr)
            dcqn += _dot_nt(dqa, wqa_ref[h]) + _dot_nt(dqr, wqr_ref[h])
            kn = _dot(ckvn, wk_ref[h])
            rk = lax.rsqrt((jnp.sum(kn * kn, -1, keepdims=True) + jnp.sum(kpe * kpe, -1, keepdims=True)) / QK + EPS)
            ya, yr = kn * rk, kpe * rk
            dka = dk_ref[h, :, 0:HD]
            dkr = unrot(dk_ref[h, :, HD:QK])
            dgk_a += jnp.sum(dka * ya, axis=0, keepdims=True)
            dgk_r += jnp.sum(dkr * yr, axis=0, keepdims=True)
            ha, hr = dka * gk_ref[:, 0:HD], dkr * gk_ref[:, HD:QK]
            mean = (jnp.sum(ha * ya, -1, keepdims=True) + jnp.sum(hr * yr, -1, keepdims=True)) / QK
            dkn = (rk * (ha - ya * mean)).astype(BF16)
            dkpe += rk * (hr - yr * mean)
            dvh = dv_ref[h].astype(BF16)
            dwk_ref[h] += _dot(ckvn_t, dkn)
            dwv_ref[h] += _dot(ckvn_t, dvh)
            dckvn += _dot_nt(dkn, wk_ref[h]) + _dot_nt(dvh, wv_ref[h])
        dcq, dg1 = _rms_bwd(cq, gql_ref[...], dcqn)
        dckv, dg2 = _rms_bwd(ckv, gkvl_ref[...], dckvn)
        dp_ref[:, 0:QL] = dcq.astype(BF16)
        dp_ref[:, QL:QL + KVL] = dckv.astype(BF16)
        dp_ref[:, QL + KVL:QL + KVL + ROPE] = dkpe.astype(BF16)
        dgql_ref[...] += dg1
        dgkvl_ref[...] += dg2
        dgq_ref[:, 0:HD] += dgq_a
        dgq_ref[:, HD:QK] += dgq_r
        dgk_ref[:, 0:HD] += dgk_a
        dgk_ref[:, HD:QK] += dgk_r

    whole = lambda a: pl.BlockSpec(a.shape, functools.partial(lambda i, nd: (0,) * nd, nd=a.ndim))
    acc_shapes = [wqa.shape, wqr.shape, wk.shape, wv.shape, gql.shape, gkvl.shape, gq.shape, gk.shape]
    return pl.pallas_call(
        body, name="mla_prep_bwd", grid=(t // tm,),
        in_specs=[pl.BlockSpec((tm, QL + KVL + ROPE), lambda i: (i, 0)), pl.BlockSpec((tm, ROPE), lambda i: (i, 0)),
                  pl.BlockSpec((tm, ROPE), lambda i: (i, 0))]
        + [whole(a) for a in (wqa, wqr, wk, wv, gql, gkvl, gq, gk)]
        + [pl.BlockSpec((NH, tm, QK), lambda i: (0, i, 0)), pl.BlockSpec((NH, tm, QK), lambda i: (0, i, 0)),
           pl.BlockSpec((NH, tm, HD), lambda i: (0, i, 0))],
        out_specs=[pl.BlockSpec((tm, QL + KVL + ROPE), lambda i: (i, 0))]
        + [pl.BlockSpec(s, functools.partial(lambda i, nd: (0,) * nd, nd=len(s))) for s in acc_shapes],
        out_shape=[jax.ShapeDtypeStruct((t, QL + KVL + ROPE), BF16)]
        + [jax.ShapeDtypeStruct(s, F32) for s in acc_shapes],
        compiler_params=_params(("arbitrary",)),
    )(p_mla, cs, sn, wqa, wqr, wk, wv, gql, gkvl, gq, gk, dq, dk, dv)


def _diag_mask(tile, transposed):
    r = lax.broadcasted_iota(jnp.int32, (tile, tile), 0) // CHUNK
    c = lax.broadcasted_iota(jnp.int32, (tile, tile), 1) // CHUNK
    return (r <= c) if transposed else (c <= r)


def _flash_fwd(q, k, v, t):
    tq = _pick(t, (256, 128))

    def body(q_ref, k_ref, v_ref, o_ref, lse_ref):
        i = pl.program_id(1)
        qt = q_ref[0]

        def step(j, carry, masked):
            m, l, acc = carry
            kt = k_ref[0, pl.ds(pl.multiple_of(j * tq, tq), tq), :]
            vt = v_ref[0, pl.ds(pl.multiple_of(j * tq, tq), tq), :]
            s = _dot_nt(qt, kt)
            if masked:
                s = jnp.where(_diag_mask(tq, False), s, -jnp.inf)
            m_new = jnp.maximum(m, jnp.max(s, axis=-1, keepdims=True))
            p = jnp.exp(s - m_new)
            alpha = jnp.exp(m - m_new)
            return m_new, alpha * l + jnp.sum(p, axis=-1, keepdims=True), alpha * acc + _dot(p, vt)

        init = (jnp.full((tq, 1), -jnp.inf, F32), jnp.zeros((tq, 1), F32), jnp.zeros((tq, HD), F32))
        carry = lax.fori_loop(0, i, lambda j, cr: step(j, cr, False), init)
        m, l, acc = step(i, carry, True)
        o_ref[...] = (acc / l).astype(BF16)
        lse_ref[0] = m + jnp.log(l)

    return pl.pallas_call(
        body, name="flash_fwd", grid=(NH, t // tq),
        in_specs=[pl.BlockSpec((1, tq, QK), lambda h, i: (h, i, 0)), pl.BlockSpec((1, t, QK), lambda h, i: (h, 0, 0)),
                  pl.BlockSpec((1, t, HD), lambda h, i: (h, 0, 0))],
        out_specs=[pl.BlockSpec((tq, HD), lambda h, i: (i, h)), pl.BlockSpec((1, tq, 1), lambda h, i: (h, i, 0))],
        out_shape=[jax.ShapeDtypeStruct((t, NH * HD), BF16), jax.ShapeDtypeStruct((NH, t, 1), F32)],
        compiler_params=_params(("parallel", "parallel")),
    )(q, k, v)


def _flash_dq(q, k, v, o, do, lse, t):
    tq = _pick(t, (256, 128))

    def body(q_ref, k_ref, v_ref, o_ref, do_ref, lse_ref, dq_ref, delta_ref):
        i = pl.program_id(1)
        qt, dot_, lse_t = q_ref[0], do_ref[...], lse_ref[0]
        delta = jnp.sum(dot_.astype(F32) * o_ref[...].astype(F32), axis=-1, keepdims=True)

        def step(j, acc, masked):
            kt = k_ref[0, pl.ds(pl.multiple_of(j * tq, tq), tq), :]
            vt = v_ref[0, pl.ds(pl.multiple_of(j * tq, tq), tq), :]
            p = jnp.exp(_dot_nt(qt, kt) - lse_t)
            if masked:
                p = jnp.where(_diag_mask(tq, False), p, 0.0)
            ds = p * (_dot_nt(dot_, vt) - delta)
            return acc + _dot(ds, kt)

        acc = lax.fori_loop(0, i, lambda j, a: step(j, a, False), jnp.zeros((tq, QK), F32))
        dq_ref[0] = step(i, acc, True)
        delta_ref[0] = delta

    return pl.pallas_call(
        body, name="flash_dq", grid=(NH, t // tq),
        in_specs=[pl.BlockSpec((1, tq, QK), lambda h, i: (h, i, 0)), pl.BlockSpec((1, t, QK), lambda h, i: (h, 0, 0)),
                  pl.BlockSpec((1, t, HD), lambda h, i: (h, 0, 0)), pl.BlockSpec((tq, HD), lambda h, i: (i, h)),
                  pl.BlockSpec((tq, HD), lambda h, i: (i, h)), pl.BlockSpec((1, tq, 1), lambda h, i: (h, i, 0))],
        out_specs=[pl.BlockSpec((1, tq, QK), lambda h, i: (h, i, 0)), pl.BlockSpec((1, tq, 1), lambda h, i: (h, i, 0))],
        out_shape=[jax.ShapeDtypeStruct((NH, t, QK), F32), jax.ShapeDtypeStruct((NH, t, 1), F32)],
        compiler_params=_params(("parallel", "parallel")),
    )(q, k, v, o, do, lse)


def _flash_dkv(q, k, v, do, lse_row, delta_row, t):
    tk = _pick(t, (256, 128))
    nq = t // tk

    def body(q_ref, k_ref, v_ref, do_ref, lse_ref, delta_ref, dk_ref, dv_ref):
        j = pl.program_id(1)
        kt, vt = k_ref[0], v_ref[0]

        def step(i, carry, masked):
            dk, dv = carry
            rows = pl.ds(pl.multiple_of(i * tk, tk), tk)
            qt, dot_ = q_ref[0, rows, :], do_ref[rows, :]
            p = jnp.exp(_dot_nt(kt, qt) - lse_ref[0, :, rows])
            if masked:
                p = jnp.where(_diag_mask(tk, True), p, 0.0)
            ds = p * (_dot_nt(vt, dot_) - delta_ref[0, :, rows])
            return dk + _dot(ds, qt), dv + _dot(p, dot_)

        carry = step(j, (jnp.zeros((tk, QK), F32), jnp.zeros((tk, HD), F32)), True)
        dk, dv = lax.fori_loop(j + 1, nq, lambda i, cr: step(i, cr, False), carry)
        dk_ref[0] = dk
        dv_ref[0] = dv

    return pl.pallas_call(
        body, name="flash_dkv", grid=(NH, nq),
        in_specs=[pl.BlockSpec((1, t, QK), lambda h, j: (h, 0, 0)), pl.BlockSpec((1, tk, QK), lambda h, j: (h, j, 0)),
                  pl.BlockSpec((1, tk, HD), lambda h, j: (h, j, 0)), pl.BlockSpec((t, HD), lambda h, j: (0, h)),
                  pl.BlockSpec((1, 1, t), lambda h, j: (h, 0, 0)), pl.BlockSpec((1, 1, t), lambda h, j: (h, 0, 0))],
        out_specs=[pl.BlockSpec((1, tk, QK), lambda h, j: (h, j, 0)), pl.BlockSpec((1, tk, HD), lambda h, j: (h, j, 0))],
        out_shape=[jax.ShapeDtypeStruct((NH, t, QK), F32), jax.ShapeDtypeStruct((NH, t, HD), F32)],
        compiler_params=_params(("parallel", "parallel")),
    )(q, k, v, do, lse_row, delta_row)


def _swiglu_fwd_rows(gu, t):
    def fn(g_ref, u_ref, a_ref):
        g = g_ref[...]
        a_ref[...] = (g * _sig(g) * u_ref[...]).astype(BF16)

    return _rows(fn, "swiglu_fwd", t, _pick(t, (256, 128)), [(gu, FF, 0), (gu, FF, 1)], [], [(FF, BF16)])[0]


def _swiglu_bwd_rows(gu, da, t):
    def fn(g_ref, u_ref, da_ref, d_ref):
        g, u, da_ = g_ref[...], u_ref[...], da_ref[...]
        sg = _sig(g)
        d_ref[:, 0:FF] = (da_ * u * sg * (1.0 + g * (1.0 - sg))).astype(BF16)
        d_ref[:, FF:2 * FF] = (da_ * g * sg).astype(BF16)

    return _rows(fn, "swiglu_bwd", t, _pick(t, (256, 128)), [(gu, FF, 0), (gu, FF, 1), (da, FF, 0)], [],
                 [(2 * FF, BF16)])[0]


def _ffn_fwd(xn, w_in, w_out, tag, t):
    gu = _mm(xn, w_in, "nn", tag + "_in")
    a = _swiglu_fwd_rows(gu, t)
    return gu, a, _mm(a, w_out, "nn", tag + "_out")


def _ffn_bwd(dfo, xn, gu, a, w_in, w_out, tag, t):
    da = _mm(dfo, w_out, "nt", tag + "_da")
    dw_out = _mm(a, dfo, "tn", tag + "_dwout")
    dgu = _swiglu_bwd_rows(gu, da, t)
    return _mm(dgu, w_in, "nt", tag + "_dxn"), _mm(xn, dgu, "tn", tag + "_dwin"), dw_out


def _local_step(x, target, cs, sn, w, s):
    t = x.shape[0]
    tm = _pick(t, (256, 128))
    g = {}

    def norm_fn(x_ref, g_ref, o_ref):
        o_ref[...] = _rms(x_ref[...], g_ref[...]).astype(BF16)

    xn1 = _rows(norm_fn, "norm1", t, tm, [(x, D, 0)], [s["ffn1_norm"]], [(D, BF16)])[0]
    gu1, a1, f1 = _ffn_fwd(xn1, w["ffn1_w_in"], w["ffn1_w_out"], "ffn1", t)

    def res_norm_fn(scale):
        def fn(h_ref, f_ref, g_ref, h_out, n_out):
            h = h_ref[...] + scale * f_ref[...]
            h_out[...] = h
            n_out[...] = _rms(h, g_ref[...]).astype(BF16)
        return fn

    h1, u = _rows(res_norm_fn(0.5), "res_norm1", t, tm, [(x, D, 0), (f1, D, 0)], [s["mix_norm"]],
                  [(D, F32), (D, BF16)])
    w_in_hg, w_in_mla = w["w_in"][:, :4 * HGW], w["w_in"][:, 4 * HGW:]
    p_hg = _mm(u, w_in_hg, "nn", "proj_hg")
    p_mla = _mm(u, w_in_mla, "nn", "proj_mla")
    gpre = _mm(u, w["w_merge"], "nn", "proj_gate")
    o_pre, hgy, states = _hgrn_fwd(p_hg, s["hg_lb_table"], s["hg_out_norm"], t)
    wq = w["w_q_up"].reshape(QL, NH, QK).transpose(1, 0, 2)
    wkv = w["w_kv_up"].reshape(KVL, NH, 2 * HD).transpose(1, 0, 2)
    wqa, wqr, wk, wv = wq[:, :, :HD], wq[:, :, HD:], wkv[:, :, :HD], wkv[:, :, HD:]
    prep_args = (p_mla, cs, sn, wqa, wqr, wk, wv, s["mla_q_lora_norm"], s["mla_kv_lora_norm"], s["q_head_norm"],
                 s["k_head_norm"])
    q, k, v = _mla_prep_fwd(*prep_args, t)
    att, lse = _flash_fwd(q, k, v, t)
    y_hg = _mm(hgy, w["w_hg_branch"], "nn", "branch_hg")
    y_mla = _mm(att, w["w_mla_branch"], "nn", "branch_mla")

    def mix_fn(gh_ref, gm_ref, yh_ref, ym_ref, b_ref, o_ref):
        gh = _sig(gh_ref[...] + b_ref[:, 0:D])
        gm = _sig(gm_ref[...] + b_ref[:, D:2 * D])
        o_ref[...] = (gh * yh_ref[...] + gm * ym_ref[...]).astype(BF16)

    mixed = _rows(mix_fn, "mix", t, tm, [(gpre, D, 0), (gpre, D, 1), (y_hg, D, 0), (y_mla, D, 0)], [s["b_merge"]],
                  [(D, BF16)])[0]
    mo = _mm(mixed, w["w_out"], "nn", "mix_out")
    h2, xn2 = _rows(res_norm_fn(1.0), "res_norm2", t, tm, [(h1, D, 0), (mo, D, 0)], [s["ffn2_norm"]],
                    [(D, F32), (D, BF16)])
    gu2, a2, f2 = _ffn_fwd(xn2, w["ffn2_w_in"], w["ffn2_w_out"], "ffn2", t)

    def loss_fn(h_ref, f_ref, tg_ref, g_ref, dh_out, dhb_out):
        h = h_ref[...] + 0.5 * f_ref[...]
        e = _rms(h, g_ref[...]) - tg_ref[...]
        dh, dgain = _rms_bwd(h, g_ref[...], e / D)
        dh_out[...] = dh
        dhb_out[...] = (0.5 * dh).astype(BF16)
        return dgain, jnp.full((1, LANE), 0.5 / D * jnp.sum(e * e), F32)

    dh3, dfo2, g["final_norm"], loss = _rows(loss_fn, "loss", t, tm, [(h2, D, 0), (f2, D, 0), (target, D, 0)],
                                             [s["final_norm"]], [(D, F32), (D, BF16)], [(1, D), (1, LANE)])

    def norm_bwd_fn(scale):
        def fn(h_ref, dxn_ref, dh_ref, g_ref, dh_out, dhb_out):
            dx, dgain = _rms_bwd(h_ref[...], g_ref[...], dxn_ref[...])
            dh = dh_ref[...] + dx
            dh_out[...] = dh
            dhb_out[...] = (scale * dh).astype(BF16)
            return (dgain,)
        return fn

    dxn2, g["ffn2_w_in"], g["ffn2_w_out"] = _ffn_bwd(dfo2, xn2, gu2, a2, w["ffn2_w_in"], w["ffn2_w_out"], "ffn2", t)
    dh2, dh2b, g["ffn2_norm"] = _rows(norm_bwd_fn(1.0), "norm2_bwd", t, tm, [(h2, D, 0), (dxn2, D, 0), (dh3, D, 0)],
                                      [s["ffn2_norm"]], [(D, F32), (D, BF16)], [(1, D)])
    dmixed = _mm(dh2b, w["w_out"], "nt", "mix_out_dx")
    g["w_out"] = _mm(mixed, dh2b, "tn", "mix_out_dw")

    def mix_bwd_fn(gh_ref, gm_ref, yh_ref, ym_ref, dm_ref, b_ref, dyh_out, dym_out, dg_out):
        gh = _sig(gh_ref[...] + b_ref[:, 0:D])
        gm = _sig(gm_ref[...] + b_ref[:, D:2 * D])
        dm = dm_ref[...]
        dyh_out[...] = (dm * gh).astype(BF16)
        dym_out[...] = (dm * gm).astype(BF16)
        dgh = dm * yh_ref[...] * gh * (1.0 - gh)
        dgm = dm * ym_ref[...] * gm * (1.0 - gm)
        dg_out[:, 0:D] = dgh.astype(BF16)
        dg_out[:, D:2 * D] = dgm.astype(BF16)
        return (jnp.concatenate([jnp.sum(dgh, axis=0, keepdims=True), jnp.sum(dgm, axis=0, keepdims=True)], axis=1),)

    dyh, dym, dgpre, g["b_merge"] = _rows(
        mix_bwd_fn, "mix_bwd", t, tm, [(gpre, D, 0), (gpre, D, 1), (y_hg, D, 0), (y_mla, D, 0), (dmixed, D, 0)],
        [s["b_merge"]], [(D, BF16), (D, BF16), (2 * D, BF16)], [(1, 2 * D)])
    g["w_hg_branch"] = _mm(hgy, dyh, "tn", "branch_hg_dw")
    g["w_mla_branch"] = _mm(att, dym, "tn", "branch_mla_dw")
    g["w_merge"] = _mm(u, dgpre, "tn", "proj_gate_dw")
    dhgy = _mm(dyh, w["w_hg_branch"], "nt", "branch_hg_dx")
    datt = _mm(dym, w["w_mla_branch"], "nt", "branch_mla_dx", out_dtype=BF16)
    du_gate = _mm(dgpre, w["w_merge"], "nt", "proj_gate_dx")

    dq, delta = _flash_dq(q, k, v, att, datt, lse, t)
    dk, dv = _flash_dkv(q, k, v, datt, lse.reshape(NH, 1, t), delta.reshape(NH, 1, t), t)
    (dp_mla, dwqa, dwqr, dwk, dwv, g["mla_q_lora_norm"], g["mla_kv_lora_norm"], g["q_head_norm"],
     g["k_head_norm"]) = _mla_prep_bwd(*prep_args, dq, dk, dv, t)
    g["w_q_up"] = jnp.concatenate([dwqa, dwqr], axis=2).transpose(1, 0, 2).reshape(QL, NH * QK)
    g["w_kv_up"] = jnp.concatenate([dwk, dwv], axis=2).transpose(1, 0, 2).reshape(KVL, NH * 2 * HD)
    dp_hg, g["hg_lb_table"], g["hg_out_norm"] = _hgrn_bwd(p_hg, s["hg_lb_table"], s["hg_out_norm"], o_pre, states,
                                                          dhgy, t)
    g["w_in"] = jnp.concatenate([_mm(u, dp_hg, "tn", "proj_hg_dw"), _mm(u, dp_mla, "tn", "proj_mla_dw")], axis=1)
    du_hg = _mm(dp_hg, w_in_hg, "nt", "proj_hg_dx")
    du_mla = _mm(dp_mla, w_in_mla, "nt", "proj_mla_dx")

    def mixnorm_bwd_fn(h_ref, a_ref, b_ref, c_ref, dh_ref, g_ref, dh_out, dhb_out):
        dx, dgain = _rms_bwd(h_ref[...], g_ref[...], a_ref[...] + b_ref[...] + c_ref[...])
        dh = dh_ref[...] + dx
        dh_out[...] = dh
        dhb_out[...] = (0.5 * dh).astype(BF16)
        return (dgain,)

    dh1, dfo1, g["mix_norm"] = _rows(mixnorm_bwd_fn, "mixnorm_bwd", t, tm,
                                     [(h1, D, 0), (du_hg, D, 0), (du_mla, D, 0), (du_gate, D, 0), (dh2, D, 0)],
                                     [s["mix_norm"]], [(D, F32), (D, BF16)], [(1, D)])
    dxn1, g["ffn1_w_in"], g["ffn1_w_out"] = _ffn_bwd(dfo1, xn1, gu1, a1, w["ffn1_w_in"], w["ffn1_w_out"], "ffn1", t)
    grad_x, _, g["ffn1_norm"] = _rows(norm_bwd_fn(1.0), "norm1_bwd", t, tm, [(x, D, 0), (dxn1, D, 0), (dh1, D, 0)],
                                      [s["ffn1_norm"]], [(D, F32), (D, BF16)], [(1, D)])
    return loss, grad_x, g


def _coords():
    return lax.axis_index("x"), lax.axis_index("y"), lax.axis_index("c")


def _all_gather(block, name):
    def body(x_ref, out_ref, send_sems, recv_sems, local_sem):
        x, y, c = _coords()
        me, sibling = (x, y, c), (x, y, 1 - c)
        chips = [(1 - x, y), (x, 1 - y), (1 - x, 1 - y)]

        def slot(px, py, pc):
            return out_ref.at[4 * px + 2 * py + pc]

        def copy(kk, block_of, to, src=None):
            return pltpu.make_async_remote_copy(
                src_ref=slot(*block_of) if src is None else src, dst_ref=slot(*block_of),
                send_sem=send_sems.at[kk], recv_sem=recv_sems.at[kk], device_id=to, device_id_type=MESH)

        mine = pltpu.make_async_copy(x_ref, slot(*me), local_sem)
        mine.start()
        first = [copy(0, me, sibling, src=x_ref)]
        first += [copy(1 + j, me, (*chip, c), src=x_ref) for j, chip in enumerate(chips)]
        for cp in first:
            cp.start()
        passed = [copy(4 + j, (*chip, c), sibling) for j, chip in enumerate(chips)]
        for j, chip in enumerate(chips):
            copy(1 + j, (*chip, c), me).wait_recv()
            passed[j].start()
        copy(0, sibling, me).wait_recv()
        for j, chip in enumerate(chips):
            copy(4 + j, (*chip, 1 - c), me).wait_recv()
        for cp in first + passed:
            cp.wait_send()
        mine.wait()

    return pl.pallas_call(
        body, name=name, out_shape=jax.ShapeDtypeStruct((N_DEV,) + block.shape, block.dtype),
        in_specs=[pl.BlockSpec(memory_space=pl.ANY)], out_specs=pl.BlockSpec(memory_space=pl.ANY),
        scratch_shapes=[pltpu.SemaphoreType.DMA((7,)), pltpu.SemaphoreType.DMA((7,)), pltpu.SemaphoreType.DMA],
    )(block)


def _sibling_swap(buf, name):
    def body(x_ref, out_ref, send_sem, recv_sem):
        x, y, c = _coords()
        cp = pltpu.make_async_remote_copy(src_ref=x_ref.at[1 - c], dst_ref=out_ref, send_sem=send_sem,
                                          recv_sem=recv_sem, device_id=(x, y, 1 - c), device_id_type=MESH)
        cp.start()
        cp.wait()

    return pl.pallas_call(
        body, name=name, out_shape=jax.ShapeDtypeStruct(buf.shape[1:], buf.dtype),
        in_specs=[pl.BlockSpec(memory_space=pl.ANY)], out_specs=pl.BlockSpec(memory_space=pl.ANY),
        scratch_shapes=[pltpu.SemaphoreType.DMA, pltpu.SemaphoreType.DMA],
    )(buf)


def _chip_exchange(buf, name):
    def body(x_ref, out_ref, send_sems, recv_sems, local_sem):
        x, y, c = _coords()
        chips = [(1 - x, y), (x, 1 - y), (1 - x, 1 - y)]
        mine = pltpu.make_async_copy(x_ref.at[2 * x + y], out_ref.at[2 * x + y], local_sem)
        mine.start()
        copies = [pltpu.make_async_remote_copy(
            src_ref=x_ref.at[2 * px + py], dst_ref=out_ref.at[2 * x + y], send_sem=send_sems.at[j],
            recv_sem=recv_sems.at[j], device_id=(px, py, c), device_id_type=MESH) for j, (px, py) in enumerate(chips)]
        for cp in copies:
            cp.start()
        for j, (px, py) in enumerate(chips):
            pltpu.make_async_remote_copy(
                src_ref=x_ref.at[2 * x + y], dst_ref=out_ref.at[2 * px + py], send_sem=send_sems.at[j],
                recv_sem=recv_sems.at[j], device_id=(px, py, c), device_id_type=MESH).wait_recv()
        for cp in copies:
            cp.wait_send()
        mine.wait()

    return pl.pallas_call(
        body, name=name, out_shape=jax.ShapeDtypeStruct(buf.shape, buf.dtype),
        in_specs=[pl.BlockSpec(memory_space=pl.ANY)], out_specs=pl.BlockSpec(memory_space=pl.ANY),
        scratch_shapes=[pltpu.SemaphoreType.DMA((3,)), pltpu.SemaphoreType.DMA((3,)), pltpu.SemaphoreType.DMA],
    )(buf)


def _add_rows(parts, name):
    r = parts[0].shape[0]

    def fn(*refs):
        acc = refs[0][...]
        for ref in refs[1:-1]:
            acc = acc + ref[...]
        refs[-1][...] = acc

    return _rows(fn, name, r, _pick(r, (872, 256, 128, 64, 32, 16, 8)), [(p, PACK_W, 0) for p in parts], [],
                 [(PACK_W, F32)])[0]


def _adamw_math(w, g, m, v):
    m = B1 * m + (1.0 - B1) * g
    v = B2 * v + (1.0 - B2) * (g * g)
    m_hat = m / (1.0 - B1 ** STEP)
    v_hat = v / (1.0 - B2 ** STEP)
    return -LR * (m_hat / (jnp.sqrt(v_hat) + AEPS) + WD * w), m, v


def _adamw(w, g, m, v, name):
    r, c = w.shape

    def fn(w_ref, g_ref, m_ref, v_ref, d_out, m_out, v_out):
        d_out[...], m_out[...], v_out[...] = _adamw_math(w_ref[...], g_ref[...], m_ref[...], v_ref[...])

    return _rows(fn, name, r, _pick(r, (256, 128, 176, 64, 32, 16, 8)), [(a, c, 0) for a in (w, g, m, v)], [],
                 [(c, F32)] * 3)


def _small_update(gathered, w, m, v):
    r = w.shape[0]

    def body(ga_ref, w_ref, m_ref, v_ref, g_out, d_out, m_out, v_out):
        g = ga_ref[0]
        for dev in range(1, N_DEV):
            g = g + ga_ref[dev]
        g_out[...] = g
        d_out[...], m_out[...], v_out[...] = _adamw_math(w_ref[...], g, m_ref[...], v_ref[...])

    return pl.pallas_call(
        body, name="small_update", out_shape=[jax.ShapeDtypeStruct((r, LANE), F32)] * 4,
    )(gathered, w, m, v)


def _pack_small(vals):
    rows = []
    for name, (r, n) in SMALL:
        flat = vals[name].reshape(-1)
        pad = (-flat.shape[0]) % LANE
        rows.append(jnp.pad(flat, (0, pad)).reshape(-1, LANE))
    return jnp.concatenate(rows, axis=0)


def _unpack_small(packed):
    out, off = {}, 0
    for name, (r, n) in SMALL:
        nrow = (r * n + LANE - 1) // LANE
        out[name] = packed[off:off + nrow].reshape(-1)[:r * n].reshape(r, n)
        off += nrow
    return out


def _shard_shape(shape, dim):
    return (shape[0] // N_DEV, shape[1]) if dim == 0 else (shape[0], shape[1] // N_DEV)


def kernel(x, positions, ffn1_norm, ffn1_w_in, ffn1_w_out, mix_norm, w_in, hg_lb_table, hg_out_norm, w_hg_branch, mla_q_lora_norm, w_q_up, mla_kv_lora_norm, w_kv_up, q_head_norm, k_head_norm, w_mla_branch, w_merge, b_merge, w_out, ffn2_norm, ffn2_w_in, ffn2_w_out, final_norm, loss_target, m_ffn1_norm, m_ffn1_w_in, m_ffn1_w_out, m_mix_norm, m_w_in, m_hg_lb_table, m_hg_out_norm, m_w_hg_branch, m_mla_q_lora_norm, m_w_q_up, m_mla_kv_lora_norm, m_w_kv_up, m_q_head_norm, m_k_head_norm, m_w_mla_branch, m_w_merge, m_b_merge, m_w_out, m_ffn2_norm, m_ffn2_w_in, m_ffn2_w_out, m_final_norm, v_ffn1_norm, v_ffn1_w_in, v_ffn1_w_out, v_mix_norm, v_w_in, v_hg_lb_table, v_hg_out_norm, v_w_hg_branch, v_mla_q_lora_norm, v_w_q_up, v_mla_kv_lora_norm, v_w_kv_up, v_q_head_norm, v_k_head_norm, v_w_mla_branch, v_w_merge, v_b_merge, v_w_out, v_ffn2_norm, v_ffn2_w_in, v_ffn2_w_out, v_final_norm):
    args = dict(locals())
    t = x.shape[1]
    big_w = {n: args[n][0] for n, _, _ in BIG}
    small = {n: args[n].reshape(shape) for n, shape in SMALL}

    flat = jnp.concatenate([big_w[n].astype(BF16).reshape(-1) for n, _, _ in BIG])
    gathered = _all_gather(flat.reshape(-1, PACK_W), "weights_all_gather")
    full, off = {}, 0
    for n, shape, dim in BIG:
        ss = _shard_shape(shape, dim)
        rows = ss[0] * ss[1] // PACK_W
        seg = gathered[:, off:off + rows].reshape((N_DEV,) + ss)
        full[n] = seg.reshape(shape) if dim == 0 else seg.transpose(1, 0, 2).reshape(shape)
        off += rows

    inv_freq = ROPE_THETA ** (-jnp.arange(0, ROPE, 2, dtype=F32) / ROPE)
    ang = positions[0].astype(F32)[:, None] * inv_freq
    cs = jnp.concatenate([jnp.cos(ang), jnp.cos(ang)], axis=1)
    sn = jnp.concatenate([jnp.sin(ang), jnp.sin(ang)], axis=1)

    loss_row, grad_x, g = _local_step(x[0], loss_target[0], cs, sn, full, small)

    parts = []
    for n, shape, dim in BIG:
        ss = _shard_shape(shape, dim)
        gn = g[n]
        gn = gn.reshape((N_DEV,) + ss) if dim == 0 else gn.reshape(shape[0], N_DEV, ss[1]).transpose(1, 0, 2)
        parts.append(gn.reshape(N_DEV, -1))
    packed = jnp.concatenate(parts, axis=1)
    rows = packed.shape[1] // PACK_W
    packed = packed.reshape(4, 2, rows, PACK_W).transpose(1, 0, 2, 3)
    c = lax.axis_index("c")
    from_sibling = _sibling_swap(packed, "grads_sibling_swap")
    keep = lax.dynamic_index_in_dim(packed, c, axis=0, keepdims=False)
    chip_sum = _add_rows([keep.reshape(-1, PACK_W), from_sibling.reshape(-1, PACK_W)], "grads_chip_sum")
    exchanged = _chip_exchange(chip_sum.reshape(4, rows, PACK_W), "grads_chip_exchange")
    gshard = _add_rows([exchanged[i] for i in range(4)], "grads_sum")

    small_g = dict(g)
    small_packed = jnp.concatenate([_pack_small(small_g), loss_row], axis=0)
    pad = (-small_packed.shape[0]) % 8
    small_packed = jnp.pad(small_packed, ((0, pad), (0, 0)))
    small_all = _all_gather(small_packed, "small_all_gather")
    zero_tail = jnp.zeros((small_packed.shape[0] - _pack_small(small).shape[0], LANE), F32)
    pk = lambda d: jnp.concatenate([_pack_small(d), zero_tail], axis=0)
    sg, sd, sm, sv = _small_update(
        small_all, pk(small), pk({n: args["m_" + n].reshape(shape) for n, shape in SMALL}),
        pk({n: args["v_" + n].reshape(shape) for n, shape in SMALL}))
    n_small_rows = _pack_small(small).shape[0]
    loss = sg[n_small_rows, 0]
    outs = {k_: _unpack_small(a) for k_, a in (("grad", sg), ("delta", sd), ("new_m", sm), ("new_v", sv))}

    off = 0
    for n, shape, dim in BIG:
        ss = _shard_shape(shape, dim)
        nrow = ss[0] * ss[1] // PACK_W
        gn = gshard[off:off + nrow].reshape(ss)
        off += nrow
        d_, m_, v_ = _adamw(big_w[n], gn, args["m_" + n][0], args["v_" + n][0], "adamw_" + n)
        outs["grad"][n], outs["delta"][n], outs["new_m"][n], outs["new_v"][n] = gn, d_, m_, v_

    def shaped(kind, n):
        return outs[kind][n].reshape(args[n].shape)

    return (loss, grad_x[None], *[shaped("grad", n) for n in WEIGHT_ORDER], *[shaped("delta", n) for n in WEIGHT_ORDER],
            *[shaped("new_m", n) for n in WEIGHT_ORDER], *[shaped("new_v", n) for n in WEIGHT_ORDER])
```

```python
import functools

import jax
import jax.numpy as jnp
from jax import lax
from jax.experimental import pallas as pl
from jax.experimental.pallas import tpu as pltpu

F32 = jnp.float32
BF16 = jnp.bfloat16

D = 1024
FF = 2816
NH = 8
HD = 128
ROPE = 64
QK = HD + ROPE
QL = 384
KVL = 256
HGW = NH * HD
CHUNK = 64
EPS = 1e-6
ROPE_THETA = 10000.0
SCALE = QK ** -0.5

LR, B1, B2, AEPS, WD, STEP = 0.001, 0.9, 0.999, 1e-08, 0.01, 10

HB = 128
SUB = 16
EXP_CLAMP = 80.0
ATT_TILES = (512, 256, 128)

LANE = 128
VMEM_LIMIT = 56 << 20

N_DEV = 8
MESH = pl.DeviceIdType.MESH

BIG = (
    ("ffn1_w_in", (D, 2 * FF), 1), ("ffn1_w_out", (FF, D), 0), ("w_in", (D, 4800), 1),
    ("w_hg_branch", (HGW, D), 0), ("w_q_up", (QL, NH * QK), 1), ("w_kv_up", (KVL, NH * 2 * HD), 1),
    ("w_mla_branch", (NH * HD, D), 0), ("w_merge", (D, 2 * D), 1), ("w_out", (D, D), 0),
    ("ffn2_w_in", (D, 2 * FF), 1), ("ffn2_w_out", (FF, D), 0),
)
SMALL = (
    ("ffn1_norm", (1, D)), ("mix_norm", (1, D)), ("hg_lb_table", (2, HGW)), ("hg_out_norm", (1, HD)),
    ("mla_q_lora_norm", (1, QL)), ("mla_kv_lora_norm", (1, KVL)), ("q_head_norm", (1, QK)),
    ("k_head_norm", (1, QK)), ("b_merge", (1, 2 * D)), ("ffn2_norm", (1, D)), ("final_norm", (1, D)),
)
WEIGHT_ORDER = ("ffn1_norm", "ffn1_w_in", "ffn1_w_out", "mix_norm", "w_in", "hg_lb_table", "hg_out_norm",
                "w_hg_branch", "mla_q_lora_norm", "w_q_up", "mla_kv_lora_norm", "w_kv_up", "q_head_norm",
                "k_head_norm", "w_mla_branch", "w_merge", "b_merge", "w_out", "ffn2_norm", "ffn2_w_in",
                "ffn2_w_out", "final_norm")
PACK_W = 1024


def _pick(n, cands):
    for c in cands:
        if n % c == 0:
            return c
    return n


def _params(sem):
    return pltpu.CompilerParams(dimension_semantics=sem, vmem_limit_bytes=VMEM_LIMIT)


def _sig(x):
    return 1.0 / (1.0 + jnp.exp(-x))


def _dot(a, b):
    return jnp.dot(a.astype(BF16), b.astype(BF16), preferred_element_type=F32)


def _dot_nt(a, b):
    return lax.dot_general(a.astype(BF16), b.astype(BF16), (((1,), (1,)), ((), ())),
                           preferred_element_type=F32)


def _dot_tn(a, b):
    return lax.dot_general(a.astype(BF16), b.astype(BF16), (((0,), (0,)), ((), ())),
                           preferred_element_type=F32)


def _split3(x):
    x1 = x.astype(BF16)
    r1 = x - x1.astype(F32)
    x2 = r1.astype(BF16)
    x3 = (r1 - x2.astype(F32)).astype(BF16)
    return x1, x2, x3


def _dot_sel(m, x):
    x1, x2, x3 = _split3(x)
    d = lambda p: jnp.dot(m, p, preferred_element_type=F32)
    return d(x1) + d(x2) + d(x3)


def _sel_dot(x, m):
    x1, x2, x3 = _split3(x)
    d = lambda p: jnp.dot(p, m, preferred_element_type=F32)
    return d(x1) + d(x2) + d(x3)


_TN = (1408, 1024, 768, 512, 384, 256, 128)


def _mm(a, b, mode, name, out_dtype=F32):
    if mode == "tn":
        t, m = a.shape
        n = b.shape[1]
        tt, tm, tn = _pick(t, (512, 256, 128)), _pick(m, _TN), _pick(n, _TN)

        def body(a_ref, b_ref, o_ref):
            @pl.when(pl.program_id(2) == 0)
            def _():
                o_ref[...] = jnp.zeros_like(o_ref)

            o_ref[...] += _dot_tn(a_ref[...], b_ref[...])

        return pl.pallas_call(
            body, name=name, grid=(m // tm, n // tn, t // tt),
            in_specs=[pl.BlockSpec((tt, tm), lambda i, j, k: (k, i)),
                      pl.BlockSpec((tt, tn), lambda i, j, k: (k, j))],
            out_specs=pl.BlockSpec((tm, tn), lambda i, j, k: (i, j)),
            out_shape=jax.ShapeDtypeStruct((m, n), F32),
            compiler_params=_params(("parallel", "parallel", "arbitrary")),
        )(a, b)

    m, k = a.shape
    tm = _pick(m, (512, 256, 128))
    if mode == "nn":
        n = b.shape[1]
        tn = _pick(n, _TN)
        b_spec = pl.BlockSpec((k, tn), lambda i, j: (0, j))
        dot = _dot
    else:
        n = b.shape[0]
        tn = _pick(n, _TN if k <= 4096 else (512, 256, 128))
        b_spec = pl.BlockSpec((tn, k), lambda i, j: (j, 0))
        dot = _dot_nt

    def body(a_ref, b_ref, o_ref):
        o_ref[...] = dot(a_ref[...], b_ref[...]).astype(o_ref.dtype)

    return pl.pallas_call(
        body, name=name, grid=(m // tm, n // tn),
        in_specs=[pl.BlockSpec((tm, k), lambda i, j: (i, 0)), b_spec],
        out_specs=pl.BlockSpec((tm, tn), lambda i, j: (i, j)),
        out_shape=jax.ShapeDtypeStruct((m, n), out_dtype),
        compiler_params=_params(("parallel", "parallel")),
    )(a, b)


def _rows(fn, name, t, tm, ins, vecs, outs, accs=()):
    n_in, n_out, n_acc = len(ins) + len(vecs), len(outs), len(accs)

    def body(*refs):
        res = fn(*refs[:n_in + n_out])
        if n_acc:
            acc_refs = refs[n_in + n_out:]

            @pl.when(pl.program_id(0) == 0)
            def _():
                for r in acc_refs:
                    r[...] = jnp.zeros_like(r)

            for r, val in zip(acc_refs, res):
                r[...] += val

    in_specs = [pl.BlockSpec((tm, bw), functools.partial(lambda i, cb: (i, cb), cb=cb)) for _, bw, cb in ins]
    in_specs += [pl.BlockSpec(v.shape, lambda i: (0, 0)) for v in vecs]
    out_specs = [pl.BlockSpec((tm, w), lambda i: (i, 0)) for w, _ in outs]
    out_specs += [pl.BlockSpec(s, lambda i: (0, 0)) for s in accs]
    out_shape = [jax.ShapeDtypeStruct((t, w), dt) for w, dt in outs]
    out_shape += [jax.ShapeDtypeStruct(s, F32) for s in accs]
    return pl.pallas_call(
        body, name=name, grid=(t // tm,), in_specs=in_specs, out_specs=out_specs, out_shape=out_shape,
        compiler_params=_params(("arbitrary",) if n_acc else ("parallel",)),
    )(*[a for a, _, _ in ins], *vecs)


def _rms(x, g):
    return x * lax.rsqrt(jnp.mean(x * x, axis=-1, keepdims=True) + EPS) * g


def _rms_bwd(x, g, dy):
    xh = x * lax.rsqrt(jnp.mean(x * x, axis=-1, keepdims=True) + EPS)
    r = lax.rsqrt(jnp.mean(x * x, axis=-1, keepdims=True) + EPS)
    dyg = dy * g
    dx = r * (dyg - xh * jnp.mean(dyg * xh, axis=-1, keepdims=True))
    return dx, jnp.sum(dy * xh, axis=0, keepdims=True)


def _hgrn_mats():
    row = lax.broadcasted_iota(jnp.int32, (HB, HB), 0)
    col = lax.broadcasted_iota(jnp.int32, (HB, HB), 1)
    return row, col


def _hgrn_gates(qr, z, t0, t1):
    lb = 1.0 / (1.0 + jnp.exp(t1 - t0))
    sz = _sig(z)
    sneg = 1.0 / (1.0 + jnp.exp(z))
    f = lb + (1.0 - lb) * sz
    return lb, sz, sneg, f, jnp.log(f), (1.0 - lb) * sneg, qr * _sig(qr)


def _hgrn_scores(q, k, cum, cmid):
    qd = q * jnp.exp(jnp.minimum(cmid, EXP_CLAMP))
    qd_b = qd.astype(BF16)
    kds, parts = [], []
    for i in range(HB // SUB):
        mid = cum[SUB * i + SUB // 2 - 1:SUB * i + SUB // 2, :]
        kd = k * jnp.exp(jnp.minimum(mid - cum, EXP_CLAMP))
        kds.append(kd)
        parts.append(_dot_nt(qd_b[SUB * i:SUB * (i + 1)], kd))
    return qd, kds, jnp.concatenate(parts, axis=0)


def _hgrn_fwd(p_hg, table, gain, t):
    nblk = t // HB

    def body(q_ref, f_ref, i_ref, g_ref, tab_ref, gain_ref, o_ref, y_ref, st_ref, state):
        @pl.when(pl.program_id(0) == 0)
        def _():
            state[...] = jnp.zeros_like(state)

        row, col = _hgrn_mats()
        causal = col <= row
        tri = causal.astype(BF16)
        trimid = tri - (col <= (row // SUB) * SUB + SUB // 2 - 1).astype(BF16)
        for h in range(NH):
            sl = slice(HD * h, HD * (h + 1))
            v, gr = i_ref[:, sl], g_ref[:, sl]
            _, _, _, _, lf, k, q = _hgrn_gates(q_ref[:, sl], f_ref[:, sl], tab_ref[0:1, sl], tab_ref[1:2, sl])
            cum = _dot_sel(tri, lf)
            cmid = _dot_sel(trimid, lf)
            _, _, s = _hgrn_scores(q, k, cum, cmid)
            p = jnp.where(causal, s, 0.0)
            st = state[h]
            st_ref[h, 0] = st
            o = _dot(p, v) + _dot_nt(q * jnp.exp(cum), st)
            last = cum[HB - 1:HB, :]
            state[h] = st * jnp.exp(last) + _dot_tn(v, k * jnp.exp(last - cum))
            o_ref[:, sl] = o
            y_ref[:, sl] = (_rms(o, gain_ref[...]) * gr * _sig(gr)).astype(BF16)

    blk = lambda cb: pl.BlockSpec((HB, HGW), functools.partial(lambda n, cb: (n, cb), cb=cb))
    return pl.pallas_call(
        body, name="hgrn_fwd", grid=(nblk,),
        in_specs=[blk(0), blk(1), blk(2), blk(3), pl.BlockSpec((2, HGW), lambda n: (0, 0)),
                  pl.BlockSpec((1, HD), lambda n: (0, 0))],
        out_specs=[pl.BlockSpec((HB, HGW), lambda n: (n, 0)), pl.BlockSpec((HB, HGW), lambda n: (n, 0)),
                   pl.BlockSpec((NH, 1, HD, HD), lambda n: (0, n, 0, 0))],
        out_shape=[jax.ShapeDtypeStruct((t, HGW), F32), jax.ShapeDtypeStruct((t, HGW), BF16),
                   jax.ShapeDtypeStruct((NH, nblk, HD, HD), F32)],
        scratch_shapes=[pltpu.VMEM((NH, HD, HD), F32)],
        compiler_params=_params(("arbitrary",)),
    )(p_hg, p_hg, p_hg, p_hg, table, gain)


def _hgrn_bwd(p_hg, table, gain, o_pre, states, dy, t):
    nblk = t // HB

    def body(q_ref, f_ref, i_ref, g_ref, tab_ref, gain_ref, o_ref, st_ref, dy_ref, dp_ref, dtab_ref, dgain_ref,
             dstate):
        @pl.when(pl.program_id(0) == 0)
        def _():
            dstate[...] = jnp.zeros_like(dstate)
            dtab_ref[...] = jnp.zeros_like(dtab_ref)
            dgain_ref[...] = jnp.zeros_like(dgain_ref)

        row, col = _hgrn_mats()
        causal = col <= row
        tri = causal.astype(BF16)
        midrow = (row // SUB) * SUB + SUB // 2 - 1
        trimid = tri - (col <= midrow).astype(BF16)
        tri_t = (row <= col).astype(BF16)
        midcol = (col // SUB) * SUB + SUB // 2 - 1
        trimid_t = tri_t - (row <= midcol).astype(BF16)
        dgain = jnp.zeros((1, HD), F32)
        for h in range(NH):
            sl = slice(HD * h, HD * (h + 1))
            qr, z, v, gr = q_ref[:, sl], f_ref[:, sl], i_ref[:, sl], g_ref[:, sl]
            lb, sz, sneg, f, lf, k, q = _hgrn_gates(qr, z, tab_ref[0:1, sl], tab_ref[1:2, sl])
            cum = _dot_sel(tri, lf)
            cmid = _dot_sel(trimid, lf)
            qd, kds, s = _hgrn_scores(q, k, cum, cmid)
            p = jnp.where(causal, s, 0.0)
            st = st_ref[h, 0]
            dst = dstate[h]
            o = o_ref[:, sl]
            sg = _sig(gr)
            dyh = dy_ref[:, sl]
            on = _rms(o, gain_ref[...])
            dgr = dyh * on * sg * (1.0 + gr * (1.0 - sg))
            do, dg_h = _rms_bwd(o, gain_ref[...], dyh * gr * sg)
            dgain = dgain + dg_h
            do_b = do.astype(BF16)
            ecum = jnp.exp(cum)
            qc = q * ecum
            last = cum[HB - 1:HB, :]
            edec = jnp.exp(last - cum)
            kdec = k * edec
            dp = jnp.where(causal, _dot_nt(do_b, v), 0.0)
            dv = _dot(p.T, do_b) + _dot_nt(kdec, dst)
            dqc = _dot(do_b, st)
            dkdec = _dot(v, dst)
            dstate[h] = dst * jnp.exp(last) + _dot(do.T, qc)
            dp_b = dp.astype(BF16)
            dqd = jnp.concatenate([_dot(dp_b[SUB * i:SUB * (i + 1)], kds[i]) for i in range(HB // SUB)], axis=0)
            gq = dqd * qd
            dq = dqd * jnp.exp(jnp.minimum(cmid, EXP_CLAMP)) + dqc * ecum
            gs = dkdec * kdec
            dk = dkdec * edec
            dcum = dqc * qc - gs
            dcum = dcum + jnp.where(row == HB - 1, jnp.sum(gs, axis=0, keepdims=True)
                                    + jnp.exp(last) * jnp.sum(st * dst, axis=0, keepdims=True), 0.0)
            qd_b = qd.astype(BF16)
            for i in range(HB // SUB):
                dkd = _dot_tn(dp_b[SUB * i:SUB * (i + 1)], qd_b[SUB * i:SUB * (i + 1)])
                mid = cum[SUB * i + SUB // 2 - 1:SUB * i + SUB // 2, :]
                dk = dk + dkd * jnp.exp(jnp.minimum(mid - cum, EXP_CLAMP))
                gk = dkd * kds[i]
                dcum = dcum - gk + jnp.where(row == SUB * i + SUB // 2 - 1, jnp.sum(gk, axis=0, keepdims=True), 0.0)
            dlf = _dot_sel(tri_t, dcum) + _dot_sel(trimid_t, gq)
            df = dlf / f - dk
            dz = df * (1.0 - lb) * sz * sneg
            dlb = jnp.sum(df * sneg, axis=0, keepdims=True) * lb * (1.0 - lb)
            dtab_ref[0:1, sl] += dlb
            dtab_ref[1:2, sl] -= dlb
            sq = _sig(qr)
            dp_ref[:, sl] = (dq * sq * (1.0 + qr * (1.0 - sq))).astype(BF16)
            dp_ref[:, HGW + HD * h:HGW + HD * (h + 1)] = dz.astype(BF16)
            dp_ref[:, 2 * HGW + HD * h:2 * HGW + HD * (h + 1)] = dv.astype(BF16)
            dp_ref[:, 3 * HGW + HD * h:3 * HGW + HD * (h + 1)] = dgr.astype(BF16)
        dgain_ref[...] += dgain

    rev = lambda cb: pl.BlockSpec((HB, HGW), functools.partial(lambda n, cb: (nblk - 1 - n, cb), cb=cb))
    return pl.pallas_call(
        body, name="hgrn_bwd", grid=(nblk,),
        in_specs=[rev(0), rev(1), rev(2), rev(3), pl.BlockSpec((2, HGW), lambda n: (0, 0)),
                  pl.BlockSpec((1, HD), lambda n: (0, 0)), rev(0),
                  pl.BlockSpec((NH, 1, HD, HD), lambda n: (0, nblk - 1 - n, 0, 0)), rev(0)],
        out_specs=[pl.BlockSpec((HB, 4 * HGW), lambda n: (nblk - 1 - n, 0)),
                   pl.BlockSpec((2, HGW), lambda n: (0, 0)), pl.BlockSpec((1, HD), lambda n: (0, 0))],
        out_shape=[jax.ShapeDtypeStruct((t, 4 * HGW), BF16), jax.ShapeDtypeStruct((2, HGW), F32),
                   jax.ShapeDtypeStruct((1, HD), F32)],
        scratch_shapes=[pltpu.VMEM((NH, HD, HD), F32)],
        compiler_params=_params(("arbitrary",)),
    )(p_hg, p_hg, p_hg, p_hg, table, gain, o_pre, states, dy)


def _rope_mat():
    r = lax.broadcasted_iota(jnp.int32, (ROPE, ROPE), 0)
    c = lax.broadcasted_iota(jnp.int32, (ROPE, ROPE), 1)
    half = ROPE // 2
    return ((r == c - half).astype(F32) - (r == c + half).astype(F32)).astype(BF16)


def _mla_prep_fwd(p_mla, cs, sn, wqa, wqr, wk, wv, gql, gkvl, gq, gk, t):
    tm = _pick(t, (256, 128))

    def body(p_ref, cs_ref, sn_ref, wqa_ref, wqr_ref, wk_ref, wv_ref, gql_ref, gkvl_ref, gq_ref, gk_ref,
             q_ref, k_ref, v_ref):
        rmat = _rope_mat()
        cqn = _rms(p_ref[:, 0:QL], gql_ref[...]).astype(BF16)
        ckvn = _rms(p_ref[:, QL:QL + KVL], gkvl_ref[...]).astype(BF16)
        kpe = p_ref[:, QL + KVL:QL + KVL + ROPE]
        c, s = cs_ref[...], sn_ref[...]
        rot = lambda x: x * c + _sel_dot(x, rmat) * s
        for h in range(NH):
            qa, qr = _dot(cqn, wqa_ref[h]), _dot(cqn, wqr_ref[h])
            rq = lax.rsqrt((jnp.sum(qa * qa, -1, keepdims=True) + jnp.sum(qr * qr, -1, keepdims=True)) / QK + EPS)
            q_ref[h, :, 0:HD] = (qa * rq * gq_ref[:, 0:HD] * SCALE).astype(BF16)
            q_ref[h, :, HD:QK] = (rot(qr * rq * gq_ref[:, HD:QK]) * SCALE).astype(BF16)
            kn = _dot(ckvn, wk_ref[h])
            rk = lax.rsqrt((jnp.sum(kn * kn, -1, keepdims=True) + jnp.sum(kpe * kpe, -1, keepdims=True)) / QK + EPS)
            k_ref[h, :, 0:HD] = (kn * rk * gk_ref[:, 0:HD]).astype(BF16)
            k_ref[h, :, HD:QK] = rot(kpe * rk * gk_ref[:, HD:QK]).astype(BF16)
            v_ref[h] = _dot(ckvn, wv_ref[h]).astype(BF16)

    whole = lambda a: pl.BlockSpec(a.shape, functools.partial(lambda i, nd: (0,) * nd, nd=a.ndim))
    return pl.pallas_call(
        body, name="mla_prep_fwd", grid=(t // tm,),
        in_specs=[pl.BlockSpec((tm, QL + KVL + ROPE), lambda i: (i, 0)), pl.BlockSpec((tm, ROPE), lambda i: (i, 0)),
                  pl.BlockSpec((tm, ROPE), lambda i: (i, 0))] + [whole(a) for a in (wqa, wqr, wk, wv, gql, gkvl, gq, gk)],
        out_specs=[pl.BlockSpec((NH, tm, QK), lambda i: (0, i, 0)), pl.BlockSpec((NH, tm, QK), lambda i: (0, i, 0)),
                   pl.BlockSpec((NH, tm, HD), lambda i: (0, i, 0))],
        out_shape=[jax.ShapeDtypeStruct((NH, t, QK), BF16), jax.ShapeDtypeStruct((NH, t, QK), BF16),
                   jax.ShapeDtypeStruct((NH, t, HD), BF16)],
        compiler_params=_params(("parallel",)),
    )(p_mla, cs, sn, wqa, wqr, wk, wv, gql, gkvl, gq, gk)


def _mla_prep_bwd(p_mla, cs, sn, wqa, wqr, wk, wv, gql, gkvl, gq, gk, dq, dk, dv, t):
    tm = _pick(t, (256, 128))

    def body(p_ref, cs_ref, sn_ref, wqa_ref, wqr_ref, wk_ref, wv_ref, gql_ref, gkvl_ref, gq_ref, gk_ref,
             dq_ref, dk_ref, dv_ref,
             dp_ref, dwqa_ref, dwqr_ref, dwk_ref, dwv_ref, dgql_ref, dgkvl_ref, dgq_ref, dgk_ref):
        accs = (dwqa_ref, dwqr_ref, dwk_ref, dwv_ref, dgql_ref, dgkvl_ref, dgq_ref, dgk_ref)

        @pl.when(pl.program_id(0) == 0)
        def _():
            for r in accs:
                r[...] = jnp.zeros_like(r)

        rmat = _rope_mat()
        rmat_t = -rmat
        cq, ckv = p_ref[:, 0:QL], p_ref[:, QL:QL + KVL]
        kpe = p_ref[:, QL + KVL:QL + KVL + ROPE]
        cqn_f, ckvn_f = _rms(cq, gql_ref[...]), _rms(ckv, gkvl_ref[...])
        cqn, ckvn = cqn_f.astype(BF16), ckvn_f.astype(BF16)
        cqn_t, ckvn_t = cqn_f.T.astype(BF16), ckvn_f.T.astype(BF16)
        c, s = cs_ref[...], sn_ref[...]
        unrot = lambda dy: dy * c + _sel_dot(dy * s, rmat_t)
        dcqn = jnp.zeros((tm, QL), F32)
        dckvn = jnp.zeros((tm, KVL), F32)
        dkpe = jnp.zeros((tm, ROPE), F32)
        dgq_a, dgq_r = jnp.zeros((1, HD), F32), jnp.zeros((1, ROPE), F32)
        dgk_a, dgk_r = jnp.zeros((1, HD), F32), jnp.zeros((1, ROPE), F32)
        for h in range(NH):
            qa, qr = _dot(cqn, wqa_ref[h]), _dot(cqn, wqr_ref[h])
            rq = lax.rsqrt((jnp.sum(qa * qa, -1, keepdims=True) + jnp.sum(qr * qr, -1, keepdims=True)) / QK + EPS)
            xa, xr = qa * rq, qr * rq
            dya = dq_ref[h, :, 0:HD] * SCALE
            dyr = unrot(dq_ref[h, :, HD:QK] * SCALE)
            dgq_a += jnp.sum(dya * xa, axis=0, keepdims=True)
            dgq_r += jnp.sum(dyr * xr, axis=0, keepdims=True)
            ga, gr_ = dya * gq_ref[:, 0:HD], dyr * gq_ref[:, HD:QK]
            mean = (jnp.sum(ga * xa, -1, keepdims=True) + jnp.sum(gr_ * xr, -1, keepdims=True)) / QK
            dqa = (rq * (ga - xa * mean)).astype(BF16)
            dqr = (rq * (gr_ - xr * mean)).astype(BF16)
            dwqa_ref[h] += _dot(cqn_t, dqa)
            dwqr_ref[h] += _dot(cqn_t, dqr)
            dcqn += _dot_nt(dqa, wqa_ref[h]) + _dot_nt(dqr, wqr_ref[h])
            kn = _dot(ckvn, wk_ref[h])
            rk = lax.rsqrt((jnp.sum(kn * kn, -1, keepdims=True) + jnp.sum(kpe * kpe, -1, keepdims=True)) / QK + EPS)
            ya, yr = kn * rk, kpe * rk
            dka = dk_ref[h, :, 0:HD]
            dkr = unrot(dk_ref[h, :, HD:QK])
            dgk_a += jnp.sum(dka * ya, axis=0, keepdims=True)
            dgk_r += jnp.sum(dkr * yr, axis=0, keepdims=True)
            ha, hr = dka * gk_ref[:, 0:HD], dkr * gk_ref[:, HD:QK]
            mean = (jnp.sum(ha * ya, -1, keepdims=True) + jnp.sum(hr * yr, -1, keepdims=True)) / QK
            dkn = (rk * (ha - ya * mean)).astype(BF16)
            dkpe += rk * (hr - yr * mean)
            dvh = dv_ref[h].astype(BF16)
            dwk_ref[h] += _dot(ckvn_t, dkn)
            dwv_ref[h] += _dot(ckvn_t, dvh)
            dckvn += _dot_nt(dkn, wk_ref[h]) + _dot_nt(dvh, wv_ref[h])
        dcq, dg1 = _rms_bwd(cq, gql_ref[...], dcqn)
        dckv, dg2 = _rms_bwd(ckv, gkvl_ref[...], dckvn)
        dp_ref[:, 0:QL] = dcq.astype(BF16)
        dp_ref[:, QL:QL + KVL] = dckv.astype(BF16)
        dp_ref[:, QL + KVL:QL + KVL + ROPE] = dkpe.astype(BF16)
        dgql_ref[...] += dg1
        dgkvl_ref[...] += dg2
        dgq_ref[:, 0:HD] += dgq_a
        dgq_ref[:, HD:QK] += dgq_r
        dgk_ref[:, 0:HD] += dgk_a
        dgk_ref[:, HD:QK] += dgk_r

    whole = lambda a: pl.BlockSpec(a.shape, functools.partial(lambda i, nd: (0,) * nd, nd=a.ndim))
    acc_shapes = [wqa.shape, wqr.shape, wk.shape, wv.shape, gql.shape, gkvl.shape, gq.shape, gk.shape]
    return pl.pallas_call(
        body, name="mla_prep_bwd", grid=(t // tm,),
        in_specs=[pl.BlockSpec((tm, QL + KVL + ROPE), lambda i: (i, 0)), pl.BlockSpec((tm, ROPE), lambda i: (i, 0)),
                  pl.BlockSpec((tm, ROPE), lambda i: (i, 0))]
        + [whole(a) for a in (wqa, wqr, wk, wv, gql, gkvl, gq, gk)]
        + [pl.BlockSpec((NH, tm, QK), lambda i: (0, i, 0)), pl.BlockSpec((NH, tm, QK), lambda i: (0, i, 0)),
           pl.BlockSpec((NH, tm, HD), lambda i: (0, i, 0))],
        out_specs=[pl.BlockSpec((tm, QL + KVL + ROPE), lambda i: (i, 0))]
        + [pl.BlockSpec(s, functools.partial(lambda i, nd: (0,) * nd, nd=len(s))) for s in acc_shapes],
        out_shape=[jax.ShapeDtypeStruct((t, QL + KVL + ROPE), BF16)]
        + [jax.ShapeDtypeStruct(s, F32) for s in acc_shapes],
        compiler_params=_params(("arbitrary",)),
    )(p_mla, cs, sn, wqa, wqr, wk, wv, gql, gkvl, gq, gk, dq, dk, dv)


def _diag_mask(tile, transposed):
    r = lax.broadcasted_iota(jnp.int32, (tile, tile), 0) // CHUNK
    c = lax.broadcasted_iota(jnp.int32, (tile, tile), 1) // CHUNK
    return (r <= c) if transposed else (c <= r)


def _flash_fwd(q, k, v, t):
    tq = _pick(t, ATT_TILES)

    def body(q_ref, k_ref, v_ref, o_ref, lse_ref):
        i = pl.program_id(1)
        qt = q_ref[0]

        def step(j, carry, masked):
            m, l, acc = carry
            kt = k_ref[0, pl.ds(pl.multiple_of(j * tq, tq), tq), :]
            vt = v_ref[0, pl.ds(pl.multiple_of(j * tq, tq), tq), :]
            s = _dot_nt(qt, kt)
            if masked:
                s = jnp.where(_diag_mask(tq, False), s, -jnp.inf)
            m_new = jnp.maximum(m, jnp.max(s, axis=-1, keepdims=True))
            p = jnp.exp(s - m_new)
            alpha = jnp.exp(m - m_new)
            return m_new, alpha * l + jnp.sum(p, axis=-1, keepdims=True), alpha * acc + _dot(p, vt)

        init = (jnp.full((tq, 1), -jnp.inf, F32), jnp.zeros((tq, 1), F32), jnp.zeros((tq, HD), F32))
        carry = lax.fori_loop(0, i, lambda j, cr: step(j, cr, False), init)
        m, l, acc = step(i, carry, True)
        o_ref[...] = (acc / l).astype(BF16)
        lse_ref[0] = m + jnp.log(l)

    return pl.pallas_call(
        body, name="flash_fwd", grid=(NH, t // tq),
        in_specs=[pl.BlockSpec((1, tq, QK), lambda h, i: (h, i, 0)), pl.BlockSpec((1, t, QK), lambda h, i: (h, 0, 0)),
                  pl.BlockSpec((1, t, HD), lambda h, i: (h, 0, 0))],
        out_specs=[pl.BlockSpec((tq, HD), lambda h, i: (i, h)), pl.BlockSpec((1, tq, 1), lambda h, i: (h, i, 0))],
        out_shape=[jax.ShapeDtypeStruct((t, NH * HD), BF16), jax.ShapeDtypeStruct((NH, t, 1), F32)],
        compiler_params=_params(("parallel", "parallel")),
    )(q, k, v)


def _flash_dq(q, k, v, o, do, lse, t):
    tq = _pick(t, ATT_TILES)

    def body(q_ref, k_ref, v_ref, o_ref, do_ref, lse_ref, dq_ref, delta_ref):
        i = pl.program_id(1)
        qt, dot_, lse_t = q_ref[0], do_ref[...], lse_ref[0]
        delta = jnp.sum(dot_.astype(F32) * o_ref[...].astype(F32), axis=-1, keepdims=True)

        def step(j, acc, masked):
            kt = k_ref[0, pl.ds(pl.multiple_of(j * tq, tq), tq), :]
            vt = v_ref[0, pl.ds(pl.multiple_of(j * tq, tq), tq), :]
            p = jnp.exp(_dot_nt(qt, kt) - lse_t)
            if masked:
                p = jnp.where(_diag_mask(tq, False), p, 0.0)
            ds = p * (_dot_nt(dot_, vt) - delta)
            return acc + _dot(ds, kt)

        acc = lax.fori_loop(0, i, lambda j, a: step(j, a, False), jnp.zeros((tq, QK), F32))
        dq_ref[0] = step(i, acc, True)
        delta_ref[0] = delta

    return pl.pallas_call(
        body, name="flash_dq", grid=(NH, t // tq),
        in_specs=[pl.BlockSpec((1, tq, QK), lambda h, i: (h, i, 0)), pl.BlockSpec((1, t, QK), lambda h, i: (h, 0, 0)),
                  pl.BlockSpec((1, t, HD), lambda h, i: (h, 0, 0)), pl.BlockSpec((tq, HD), lambda h, i: (i, h)),
                  pl.BlockSpec((tq, HD), lambda h, i: (i, h)), pl.BlockSpec((1, tq, 1), lambda h, i: (h, i, 0))],
        out_specs=[pl.BlockSpec((1, tq, QK), lambda h, i: (h, i, 0)), pl.BlockSpec((1, tq, 1), lambda h, i: (h, i, 0))],
        out_shape=[jax.ShapeDtypeStruct((NH, t, QK), F32), jax.ShapeDtypeStruct((NH, t, 1), F32)],
        compiler_params=_params(("parallel", "parallel")),
    )(q, k, v, o, do, lse)


def _flash_dkv(q, k, v, do, lse_row, delta_row, t):
    tk = _pick(t, ATT_TILES)
    nq = t // tk

    def body(q_ref, k_ref, v_ref, do_ref, lse_ref, delta_ref, dk_ref, dv_ref):
        j = pl.program_id(1)
        kt, vt = k_ref[0], v_ref[0]

        def step(i, carry, masked):
            dk, dv = carry
            rows = pl.ds(pl.multiple_of(i * tk, tk), tk)
            qt, dot_ = q_ref[0, rows, :], do_ref[rows, :]
            p = jnp.exp(_dot_nt(kt, qt) - lse_ref[0, :, rows])
            if masked:
                p = jnp.where(_diag_mask(tk, True), p, 0.0)
            ds = p * (_dot_nt(vt, dot_) - delta_ref[0, :, rows])
            return dk + _dot(ds, qt), dv + _dot(p, dot_)

        carry = step(j, (jnp.zeros((tk, QK), F32), jnp.zeros((tk, HD), F32)), True)
        dk, dv = lax.fori_loop(j + 1, nq, lambda i, cr: step(i, cr, False), carry)
        dk_ref[0] = dk
        dv_ref[0] = dv

    return pl.pallas_call(
        body, name="flash_dkv", grid=(NH, nq),
        in_specs=[pl.BlockSpec((1, t, QK), lambda h, j: (h, 0, 0)), pl.BlockSpec((1, tk, QK), lambda h, j: (h, j, 0)),
                  pl.BlockSpec((1, tk, HD), lambda h, j: (h, j, 0)), pl.BlockSpec((t, HD), lambda h, j: (0, h)),
                  pl.BlockSpec((1, 1, t), lambda h, j: (h, 0, 0)), pl.BlockSpec((1, 1, t), lambda h, j: (h, 0, 0))],
        out_specs=[pl.BlockSpec((1, tk, QK), lambda h, j: (h, j, 0)), pl.BlockSpec((1, tk, HD), lambda h, j: (h, j, 0))],
        out_shape=[jax.ShapeDtypeStruct((NH, t, QK), F32), jax.ShapeDtypeStruct((NH, t, HD), F32)],
        compiler_params=_params(("parallel", "parallel")),
    )(q, k, v, do, lse_row, delta_row)


def _swiglu_fwd_rows(gu, t):
    def fn(g_ref, u_ref, a_ref):
        g = g_ref[...]
        a_ref[...] = (g * _sig(g) * u_ref[...]).astype(BF16)

    return _rows(fn, "swiglu_fwd", t, _pick(t, (256, 128)), [(gu, FF, 0), (gu, FF, 1)], [], [(FF, BF16)])[0]


def _swiglu_bwd_rows(gu, da, t):
    def fn(g_ref, u_ref, da_ref, d_ref):
        g, u, da_ = g_ref[...], u_ref[...], da_ref[...]
        sg = _sig(g)
        d_ref[:, 0:FF] = (da_ * u * sg * (1.0 + g * (1.0 - sg))).astype(BF16)
        d_ref[:, FF:2 * FF] = (da_ * g * sg).astype(BF16)

    return _rows(fn, "swiglu_bwd", t, _pick(t, (256, 128)), [(gu, FF, 0), (gu, FF, 1), (da, FF, 0)], [],
                 [(2 * FF, BF16)])[0]


def _ffn_fwd(xn, w_in, w_out, tag, t):
    gu = _mm(xn, w_in, "nn", tag + "_in")
    a = _swiglu_fwd_rows(gu, t)
    return gu, a, _mm(a, w_out, "nn", tag + "_out")


def _ffn_bwd(dfo, xn, gu, a, w_in, w_out, tag, t):
    da = _mm(dfo, w_out, "nt", tag + "_da")
    dw_out = _mm(a, dfo, "tn", tag + "_dwout")
    dgu = _swiglu_bwd_rows(gu, da, t)
    return _mm(dgu, w_in, "nt", tag + "_dxn"), _mm(xn, dgu, "tn", tag + "_dwin"), dw_out


def _local_step(x, target, cs, sn, w, s):
    t = x.shape[0]
    tm = _pick(t, (256, 128))
    g = {}

    def norm_fn(x_ref, g_ref, o_ref):
        o_ref[...] = _rms(x_ref[...], g_ref[...]).astype(BF16)

    xn1 = _rows(norm_fn, "norm1", t, tm, [(x, D, 0)], [s["ffn1_norm"]], [(D, BF16)])[0]
    gu1, a1, f1 = _ffn_fwd(xn1, w["ffn1_w_in"], w["ffn1_w_out"], "ffn1", t)

    def res_norm_fn(scale):
        def fn(h_ref, f_ref, g_ref, h_out, n_out):
            h = h_ref[...] + scale * f_ref[...]
            h_out[...] = h
            n_out[...] = _rms(h, g_ref[...]).astype(BF16)
        return fn

    h1, u = _rows(res_norm_fn(0.5), "res_norm1", t, tm, [(x, D, 0), (f1, D, 0)], [s["mix_norm"]],
                  [(D, F32), (D, BF16)])
    w_in_hg, w_in_mla = w["w_in"][:, :4 * HGW], w["w_in"][:, 4 * HGW:]
    p_hg = _mm(u, w_in_hg, "nn", "proj_hg")
    p_mla = _mm(u, w_in_mla, "nn", "proj_mla")
    gpre = _mm(u, w["w_merge"], "nn", "proj_gate")
    o_pre, hgy, states = _hgrn_fwd(p_hg, s["hg_lb_table"], s["hg_out_norm"], t)
    wq = w["w_q_up"].reshape(QL, NH, QK).transpose(1, 0, 2)
    wkv = w["w_kv_up"].reshape(KVL, NH, 2 * HD).transpose(1, 0, 2)
    wqa, wqr, wk, wv = wq[:, :, :HD], wq[:, :, HD:], wkv[:, :, :HD], wkv[:, :, HD:]
    prep_args = (p_mla, cs, sn, wqa, wqr, wk, wv, s["mla_q_lora_norm"], s["mla_kv_lora_norm"], s["q_head_norm"],
                 s["k_head_norm"])
    q, k, v = _mla_prep_fwd(*prep_args, t)
    att, lse = _flash_fwd(q, k, v, t)
    y_hg = _mm(hgy, w["w_hg_branch"], "nn", "branch_hg")
    y_mla = _mm(att, w["w_mla_branch"], "nn", "branch_mla")

    def mix_fn(gh_ref, gm_ref, yh_ref, ym_ref, b_ref, o_ref):
        gh = _sig(gh_ref[...] + b_ref[:, 0:D])
        gm = _sig(gm_ref[...] + b_ref[:, D:2 * D])
        o_ref[...] = (gh * yh_ref[...] + gm * ym_ref[...]).astype(BF16)

    mixed = _rows(mix_fn, "mix", t, tm, [(gpre, D, 0), (gpre, D, 1), (y_hg, D, 0), (y_mla, D, 0)], [s["b_merge"]],
                  [(D, BF16)])[0]
    mo = _mm(mixed, w["w_out"], "nn", "mix_out")
    h2, xn2 = _rows(res_norm_fn(1.0), "res_norm2", t, tm, [(h1, D, 0), (mo, D, 0)], [s["ffn2_norm"]],
                    [(D, F32), (D, BF16)])
    gu2, a2, f2 = _ffn_fwd(xn2, w["ffn2_w_in"], w["ffn2_w_out"], "ffn2", t)

    def loss_fn(h_ref, f_ref, tg_ref, g_ref, dh_out, dhb_out):
        h = h_ref[...] + 0.5 * f_ref[...]
        e = _rms(h, g_ref[...]) - tg_ref[...]
        dh, dgain = _rms_bwd(h, g_ref[...], e / D)
        dh_out[...] = dh
        dhb_out[...] = (0.5 * dh).astype(BF16)
        return dgain, jnp.full((1, LANE), 0.5 / D * jnp.sum(e * e), F32)

    dh3, dfo2, g["final_norm"], loss = _rows(loss_fn, "loss", t, tm, [(h2, D, 0), (f2, D, 0), (target, D, 0)],
                                             [s["final_norm"]], [(D, F32), (D, BF16)], [(1, D), (1, LANE)])

    def norm_bwd_fn(scale):
        def fn(h_ref, dxn_ref, dh_ref, g_ref, dh_out, dhb_out):
            dx, dgain = _rms_bwd(h_ref[...], g_ref[...], dxn_ref[...])
            dh = dh_ref[...] + dx
            dh_out[...] = dh
            dhb_out[...] = (scale * dh).astype(BF16)
            return (dgain,)
        return fn

    dxn2, g["ffn2_w_in"], g["ffn2_w_out"] = _ffn_bwd(dfo2, xn2, gu2, a2, w["ffn2_w_in"], w["ffn2_w_out"], "ffn2", t)
    dh2, dh2b, g["ffn2_norm"] = _rows(norm_bwd_fn(1.0), "norm2_bwd", t, tm, [(h2, D, 0), (dxn2, D, 0), (dh3, D, 0)],
                                      [s["ffn2_norm"]], [(D, F32), (D, BF16)], [(1, D)])
    dmixed = _mm(dh2b, w["w_out"], "nt", "mix_out_dx")
    g["w_out"] = _mm(mixed, dh2b, "tn", "mix_out_dw")

    def mix_bwd_fn(gh_ref, gm_ref, yh_ref, ym_ref, dm_ref, b_ref, dyh_out, dym_out, dg_out):
        gh = _sig(gh_ref[...] + b_ref[:, 0:D])
        gm = _sig(gm_ref[...] + b_ref[:, D:2 * D])
        dm = dm_ref[...]
        dyh_out[...] = (dm * gh).astype(BF16)
        dym_out[...] = (dm * gm).astype(BF16)
        dgh = dm * yh_ref[...] * gh * (1.0 - gh)
        dgm = dm * ym_ref[...] * gm * (1.0 - gm)
        dg_out[:, 0:D] = dgh.astype(BF16)
        dg_out[:, D:2 * D] = dgm.astype(BF16)
        return (jnp.concatenate([jnp.sum(dgh, axis=0, keepdims=True), jnp.sum(dgm, axis=0, keepdims=True)], axis=1),)

    dyh, dym, dgpre, g["b_merge"] = _rows(
        mix_bwd_fn, "mix_bwd", t, tm, [(gpre, D, 0), (gpre, D, 1), (y_hg, D, 0), (y_mla, D, 0), (dmixed, D, 0)],
        [s["b_merge"]], [(D, BF16), (D, BF16), (2 * D, BF16)], [(1, 2 * D)])
    g["w_hg_branch"] = _mm(hgy, dyh, "tn", "branch_hg_dw")
    g["w_mla_branch"] = _mm(att, dym, "tn", "branch_mla_dw")
    g["w_merge"] = _mm(u, dgpre, "tn", "proj_gate_dw")
    dhgy = _mm(dyh, w["w_hg_branch"], "nt", "branch_hg_dx")
    datt = _mm(dym, w["w_mla_branch"], "nt", "branch_mla_dx", out_dtype=BF16)
    du_gate = _mm(dgpre, w["w_merge"], "nt", "proj_gate_dx")

    dq, delta = _flash_dq(q, k, v, att, datt, lse, t)
    dk, dv = _flash_dkv(q, k, v, datt, lse.reshape(NH, 1, t), delta.reshape(NH, 1, t), t)
    (dp_mla, dwqa, dwqr, dwk, dwv, g["mla_q_lora_norm"], g["mla_kv_lora_norm"], g["q_head_norm"],
     g["k_head_norm"]) = _mla_prep_bwd(*prep_args, dq, dk, dv, t)
    g["w_q_up"] = jnp.concatenate([dwqa, dwqr], axis=2).transpose(1, 0, 2).reshape(QL, NH * QK)
    g["w_kv_up"] = jnp.concatenate([dwk, dwv], axis=2).transpose(1, 0, 2).reshape(KVL, NH * 2 * HD)
    dp_hg, g["hg_lb_table"], g["hg_out_norm"] = _hgrn_bwd(p_hg, s["hg_lb_table"], s["hg_out_norm"], o_pre, states,
                                                          dhgy, t)
    g["w_in"] = jnp.concatenate([_mm(u, dp_hg, "tn", "proj_hg_dw"), _mm(u, dp_mla, "tn", "proj_mla_dw")], axis=1)
    du_hg = _mm(dp_hg, w_in_hg, "nt", "proj_hg_dx")
    du_mla = _mm(dp_mla, w_in_mla, "nt", "proj_mla_dx")

    def mixnorm_bwd_fn(h_ref, a_ref, b_ref, c_ref, dh_ref, g_ref, dh_out, dhb_out):
        dx, dgain = _rms_bwd(h_ref[...], g_ref[...], a_ref[...] + b_ref[...] + c_ref[...])
        dh = dh_ref[...] + dx
        dh_out[...] = dh
        dhb_out[...] = (0.5 * dh).astype(BF16)
        return (dgain,)

    dh1, dfo1, g["mix_norm"] = _rows(mixnorm_bwd_fn, "mixnorm_bwd", t, tm,
                                     [(h1, D, 0), (du_hg, D, 0), (du_mla, D, 0), (du_gate, D, 0), (dh2, D, 0)],
                                     [s["mix_norm"]], [(D, F32), (D, BF16)], [(1, D)])
    dxn1, g["ffn1_w_in"], g["ffn1_w_out"] = _ffn_bwd(dfo1, xn1, gu1, a1, w["ffn1_w_in"], w["ffn1_w_out"], "ffn1", t)
    grad_x, _, g["ffn1_norm"] = _rows(norm_bwd_fn(1.0), "norm1_bwd", t, tm, [(x, D, 0), (dxn1, D, 0), (dh1, D, 0)],
                                      [s["ffn1_norm"]], [(D, F32), (D, BF16)], [(1, D)])
    return loss, grad_x, g


def _coords():
    return lax.axis_index("x"), lax.axis_index("y"), lax.axis_index("c")


def _all_gather(block, name):
    def body(x_ref, out_ref, send_sems, recv_sems, local_sem):
        x, y, c = _coords()
        me, sibling = (x, y, c), (x, y, 1 - c)
        chips = [(1 - x, y), (x, 1 - y), (1 - x, 1 - y)]

        def slot(px, py, pc):
            return out_ref.at[4 * px + 2 * py + pc]

        def copy(kk, block_of, to, src=None):
            return pltpu.make_async_remote_copy(
                src_ref=slot(*block_of) if src is None else src, dst_ref=slot(*block_of),
                send_sem=send_sems.at[kk], recv_sem=recv_sems.at[kk], device_id=to, device_id_type=MESH)

        mine = pltpu.make_async_copy(x_ref, slot(*me), local_sem)
        mine.start()
        first = [copy(0, me, sibling, src=x_ref)]
        first += [copy(1 + j, me, (*chip, c), src=x_ref) for j, chip in enumerate(chips)]
        for cp in first:
            cp.start()
        passed = [copy(4 + j, (*chip, c), sibling) for j, chip in enumerate(chips)]
        for j, chip in enumerate(chips):
            copy(1 + j, (*chip, c), me).wait_recv()
            passed[j].start()
        copy(0, sibling, me).wait_recv()
        for j, chip in enumerate(chips):
            copy(4 + j, (*chip, 1 - c), me).wait_recv()
        for cp in first + passed:
            cp.wait_send()
        mine.wait()

    return pl.pallas_call(
        body, name=name, out_shape=jax.ShapeDtypeStruct((N_DEV,) + block.shape, block.dtype),
        in_specs=[pl.BlockSpec(memory_space=pl.ANY)], out_specs=pl.BlockSpec(memory_space=pl.ANY),
        scratch_shapes=[pltpu.SemaphoreType.DMA((7,)), pltpu.SemaphoreType.DMA((7,)), pltpu.SemaphoreType.DMA],
    )(block)


def _sibling_swap(buf, name):
    def body(x_ref, out_ref, send_sem, recv_sem):
        x, y, c = _coords()
        cp = pltpu.make_async_remote_copy(src_ref=x_ref.at[1 - c], dst_ref=out_ref, send_sem=send_sem,
                                          recv_sem=recv_sem, device_id=(x, y, 1 - c), device_id_type=MESH)
        cp.start()
        cp.wait()

    return pl.pallas_call(
        body, name=name, out_shape=jax.ShapeDtypeStruct(buf.shape[1:], buf.dtype),
        in_specs=[pl.BlockSpec(memory_space=pl.ANY)], out_specs=pl.BlockSpec(memory_space=pl.ANY),
        scratch_shapes=[pltpu.SemaphoreType.DMA, pltpu.SemaphoreType.DMA],
    )(buf)


def _chip_exchange(buf, name):
    def body(x_ref, out_ref, send_sems, recv_sems, local_sem):
        x, y, c = _coords()
        chips = [(1 - x, y), (x, 1 - y), (1 - x, 1 - y)]
        mine = pltpu.make_async_copy(x_ref.at[2 * x + y], out_ref.at[2 * x + y], local_sem)
        mine.start()
        copies = [pltpu.make_async_remote_copy(
            src_ref=x_ref.at[2 * px + py], dst_ref=out_ref.at[2 * x + y], send_sem=send_sems.at[j],
            recv_sem=recv_sems.at[j], device_id=(px, py, c), device_id_type=MESH) for j, (px, py) in enumerate(chips)]
        for cp in copies:
            cp.start()
        for j, (px, py) in enumerate(chips):
            pltpu.make_async_remote_copy(
                src_ref=x_ref.at[2 * x + y], dst_ref=out_ref.at[2 * px + py], send_sem=send_sems.at[j],
                recv_sem=recv_sems.at[j], device_id=(px, py, c), device_id_type=MESH).wait_recv()
        for cp in copies:
            cp.wait_send()
        mine.wait()

    return pl.pallas_call(
        body, name=name, out_shape=jax.ShapeDtypeStruct(buf.shape, buf.dtype),
        in_specs=[pl.BlockSpec(memory_space=pl.ANY)], out_specs=pl.BlockSpec(memory_space=pl.ANY),
        scratch_shapes=[pltpu.SemaphoreType.DMA((3,)), pltpu.SemaphoreType.DMA((3,)), pltpu.SemaphoreType.DMA],
    )(buf)


def _add_rows(parts, name):
    r = parts[0].shape[0]

    def fn(*refs):
        acc = refs[0][...]
        for ref in refs[1:-1]:
            acc = acc + ref[...]
        refs[-1][...] = acc

    return _rows(fn, name, r, _pick(r, (872, 256, 128, 64, 32, 16, 8)), [(p, PACK_W, 0) for p in parts], [],
                 [(PACK_W, F32)])[0]


def _adamw_math(w, g, m, v):
    m = B1 * m + (1.0 - B1) * g
    v = B2 * v + (1.0 - B2) * (g * g)
    m_hat = m / (1.0 - B1 ** STEP)
    v_hat = v / (1.0 - B2 ** STEP)
    return -LR * (m_hat / (jnp.sqrt(v_hat) + AEPS) + WD * w), m, v


def _adamw(w, g, m, v, name):
    r, c = w.shape

    def fn(w_ref, g_ref, m_ref, v_ref, d_out, m_out, v_out):
        d_out[...], m_out[...], v_out[...] = _adamw_math(w_ref[...], g_ref[...], m_ref[...], v_ref[...])

    return _rows(fn, name, r, _pick(r, (256, 128, 176, 64, 32, 16, 8)), [(a, c, 0) for a in (w, g, m, v)], [],
                 [(c, F32)] * 3)


def _small_update(gathered, w, m, v):
    r = w.shape[0]

    def body(ga_ref, w_ref, m_ref, v_ref, g_out, d_out, m_out, v_out):
        g = ga_ref[0]
        for dev in range(1, N_DEV):
            g = g + ga_ref[dev]
        g_out[...] = g
        d_out[...], m_out[...], v_out[...] = _adamw_math(w_ref[...], g, m_ref[...], v_ref[...])

    return pl.pallas_call(
        body, name="small_update", out_shape=[jax.ShapeDtypeStruct((r, LANE), F32)] * 4,
    )(gathered, w, m, v)


def _pack_small(vals):
    rows = []
    for name, (r, n) in SMALL:
        flat = vals[name].reshape(-1)
        pad = (-flat.shape[0]) % LANE
        rows.append(jnp.pad(flat, (0, pad)).reshape(-1, LANE))
    return jnp.concatenate(rows, axis=0)


def _unpack_small(packed):
    out, off = {}, 0
    for name, (r, n) in SMALL:
        nrow = (r * n + LANE - 1) // LANE
        out[name] = packed[off:off + nrow].reshape(-1)[:r * n].reshape(r, n)
        off += nrow
    return out


def _shard_shape(shape, dim):
    return (shape[0] // N_DEV, shape[1]) if dim == 0 else (shape[0], shape[1] // N_DEV)


def kernel(x, positions, ffn1_norm, ffn1_w_in, ffn1_w_out, mix_norm, w_in, hg_lb_table, hg_out_norm, w_hg_branch, mla_q_lora_norm, w_q_up, mla_kv_lora_norm, w_kv_up, q_head_norm, k_head_norm, w_mla_branch, w_merge, b_merge, w_out, ffn2_norm, ffn2_w_in, ffn2_w_out, final_norm, loss_target, m_ffn1_norm, m_ffn1_w_in, m_ffn1_w_out, m_mix_norm, m_w_in, m_hg_lb_table, m_hg_out_norm, m_w_hg_branch, m_mla_q_lora_norm, m_w_q_up, m_mla_kv_lora_norm, m_w_kv_up, m_q_head_norm, m_k_head_norm, m_w_mla_branch, m_w_merge, m_b_merge, m_w_out, m_ffn2_norm, m_ffn2_w_in, m_ffn2_w_out, m_final_norm, v_ffn1_norm, v_ffn1_w_in, v_ffn1_w_out, v_mix_norm, v_w_in, v_hg_lb_table, v_hg_out_norm, v_w_hg_branch, v_mla_q_lora_norm, v_w_q_up, v_mla_kv_lora_norm, v_w_kv_up, v_q_head_norm, v_k_head_norm, v_w_mla_branch, v_w_merge, v_b_merge, v_w_out, v_ffn2_norm, v_ffn2_w_in, v_ffn2_w_out, v_final_norm):
    args = dict(locals())
    t = x.shape[1]
    big_w = {n: args[n][0] for n, _, _ in BIG}
    small = {n: args[n].reshape(shape) for n, shape in SMALL}

    flat = jnp.concatenate([big_w[n].astype(BF16).reshape(-1) for n, _, _ in BIG])
    gathered = _all_gather(flat.reshape(-1, PACK_W), "weights_all_gather")
    full, off = {}, 0
    for n, shape, dim in BIG:
        ss = _shard_shape(shape, dim)
        rows = ss[0] * ss[1] // PACK_W
        seg = gathered[:, off:off + rows].reshape((N_DEV,) + ss)
        full[n] = seg.reshape(shape) if dim == 0 else seg.transpose(1, 0, 2).reshape(shape)
        off += rows

    inv_freq = ROPE_THETA ** (-jnp.arange(0, ROPE, 2, dtype=F32) / ROPE)
    ang = positions[0].astype(F32)[:, None] * inv_freq
    cs = jnp.concatenate([jnp.cos(ang), jnp.cos(ang)], axis=1)
    sn = jnp.concatenate([jnp.sin(ang), jnp.sin(ang)], axis=1)

    loss_row, grad_x, g = _local_step(x[0], loss_target[0], cs, sn, full, small)

    parts = []
    for n, shape, dim in BIG:
        ss = _shard_shape(shape, dim)
        gn = g[n]
        gn = gn.reshape((N_DEV,) + ss) if dim == 0 else gn.reshape(shape[0], N_DEV, ss[1]).transpose(1, 0, 2)
        parts.append(gn.reshape(N_DEV, -1))
    packed = jnp.concatenate(parts, axis=1)
    rows = packed.shape[1] // PACK_W
    packed = packed.reshape(4, 2, rows, PACK_W).transpose(1, 0, 2, 3)
    c = lax.axis_index("c")
    from_sibling = _sibling_swap(packed, "grads_sibling_swap")
    keep = lax.dynamic_index_in_dim(packed, c, axis=0, keepdims=False)
    chip_sum = _add_rows([keep.reshape(-1, PACK_W), from_sibling.reshape(-1, PACK_W)], "grads_chip_sum")
    exchanged = _chip_exchange(chip_sum.reshape(4, rows, PACK_W), "grads_chip_exchange")
    gshard = _add_rows([exchanged[i] for i in range(4)], "grads_sum")

    small_g = dict(g)
    small_packed = jnp.concatenate([_pack_small(small_g), loss_row], axis=0)
    pad = (-small_packed.shape[0]) % 8
    small_packed = jnp.pad(small_packed, ((0, pad), (0, 0)))
    small_all = _all_gather(small_packed, "small_all_gather")
    zero_tail = jnp.zeros((small_packed.shape[0] - _pack_small(small).shape[0], LANE), F32)
    pk = lambda d: jnp.concatenate([_pack_small(d), zero_tail], axis=0)
    sg, sd, sm, sv = _small_update(
        small_all, pk(small), pk({n: args["m_" + n].reshape(shape) for n, shape in SMALL}),
        pk({n: args["v_" + n].reshape(shape) for n, shape in SMALL}))
    n_small_rows = _pack_small(small).shape[0]
    loss = sg[n_small_rows, 0]
    outs = {k_: _unpack_small(a) for k_, a in (("grad", sg), ("delta", sd), ("new_m", sm), ("new_v", sv))}

    off = 0
    for n, shape, dim in BIG:
        ss = _shard_shape(shape, dim)
        nrow = ss[0] * ss[1] // PACK_W
        gn = gshard[off:off + nrow].reshape(ss)
        off += nrow
        d_, m_, v_ = _adamw(big_w[n], gn, args["m_" + n][0], args["v_" + n][0], "adamw_" + n)
        outs["grad"][n], outs["delta"][n], outs["new_m"][n], outs["new_v"][n] = gn, d_, m_, v_

    def shaped(kind, n):
        return outs[kind][n].reshape(args[n].shape)

    return (loss, grad_x[None], *[shaped("grad", n) for n in WEIGHT_ORDER], *[shaped("delta", n) for n in WEIGHT_ORDER],
            *[shaped("new_m", n) for n in WEIGHT_ORDER], *[shaped("new_v", n) for n in WEIGHT_ORDER])
```

```python
import functools

import jax
import jax.numpy as jnp
from jax import lax
from jax.experimental import pallas as pl
from jax.experimental.pallas import tpu as pltpu

F32 = jnp.float32
BF16 = jnp.bfloat16

D = 1024
FF = 2816
NH = 8
HD = 128
ROPE = 64
QK = HD + ROPE
QL = 384
KVL = 256
HGW = NH * HD
CHUNK = 64
EPS = 1e-6
ROPE_THETA = 10000.0
SCALE = QK ** -0.5

LR, B1, B2, AEPS, WD, STEP = 0.001, 0.9, 0.999, 1e-08, 0.01, 10

HB = 128
SUB = 16
EXP_CLAMP = 80.0
ATT_TILES = (512, 256, 128)

LANE = 128
VMEM_LIMIT = 56 << 20

N_DEV = 8
MESH = pl.DeviceIdType.MESH

BIG = (
    ("ffn1_w_in", (D, 2 * FF), 1), ("ffn1_w_out", (FF, D), 0), ("w_in", (D, 4800), 1),
    ("w_hg_branch", (HGW, D), 0), ("w_q_up", (QL, NH * QK), 1), ("w_kv_up", (KVL, NH * 2 * HD), 1),
    ("w_mla_branch", (NH * HD, D), 0), ("w_merge", (D, 2 * D), 1), ("w_out", (D, D), 0),
    ("ffn2_w_in", (D, 2 * FF), 1), ("ffn2_w_out", (FF, D), 0),
)
SMALL = (
    ("ffn1_norm", (1, D)), ("mix_norm", (1, D)), ("hg_lb_table", (2, HGW)), ("hg_out_norm", (1, HD)),
    ("mla_q_lora_norm", (1, QL)), ("mla_kv_lora_norm", (1, KVL)), ("q_head_norm", (1, QK)),
    ("k_head_norm", (1, QK)), ("b_merge", (1, 2 * D)), ("ffn2_norm", (1, D)), ("final_norm", (1, D)),
)
WEIGHT_ORDER = ("ffn1_norm", "ffn1_w_in", "ffn1_w_out", "mix_norm", "w_in", "hg_lb_table", "hg_out_norm",
                "w_hg_branch", "mla_q_lora_norm", "w_q_up", "mla_kv_lora_norm", "w_kv_up", "q_head_norm",
                "k_head_norm", "w_mla_branch", "w_merge", "b_merge", "w_out", "ffn2_norm", "ffn2_w_in",
                "ffn2_w_out", "final_norm")


def _pick(n, cands):
    for c in cands:
        if n % c == 0:
            return c
    return n


def _params(sem):
    return pltpu.CompilerParams(dimension_semantics=sem, vmem_limit_bytes=VMEM_LIMIT)


def _sig(x):
    return 1.0 / (1.0 + jnp.exp(-x))


def _dot(a, b):
    return jnp.dot(a.astype(BF16), b.astype(BF16), preferred_element_type=F32)


def _dot_nt(a, b):
    return lax.dot_general(a.astype(BF16), b.astype(BF16), (((1,), (1,)), ((), ())),
                           preferred_element_type=F32)


def _dot_tn(a, b):
    return lax.dot_general(a.astype(BF16), b.astype(BF16), (((0,), (0,)), ((), ())),
                           preferred_element_type=F32)


def _split3(x):
    x1 = x.astype(BF16)
    r1 = x - x1.astype(F32)
    x2 = r1.astype(BF16)
    x3 = (r1 - x2.astype(F32)).astype(BF16)
    return x1, x2, x3


def _dot_sel(m, x):
    x1, x2, x3 = _split3(x)
    d = lambda p: jnp.dot(m, p, preferred_element_type=F32)
    return d(x1) + d(x2) + d(x3)


def _sel_dot(x, m):
    x1, x2, x3 = _split3(x)
    d = lambda p: jnp.dot(p, m, preferred_element_type=F32)
    return d(x1) + d(x2) + d(x3)


_TN = (1408, 1024, 768, 512, 384, 256, 128)


def _mm(a, b, mode, name, out_dtype=F32):
    if mode == "tn":
        t, m = a.shape
        n = b.shape[1]
        tt, tm, tn = _pick(t, (512, 256, 128)), _pick(m, _TN), _pick(n, _TN)

        def body(a_ref, b_ref, o_ref):
            @pl.when(pl.program_id(2) == 0)
            def _():
                o_ref[...] = jnp.zeros_like(o_ref)

            o_ref[...] += _dot_tn(a_ref[...], b_ref[...])

        return pl.pallas_call(
            body, name=name, grid=(m // tm, n // tn, t // tt),
            in_specs=[pl.BlockSpec((tt, tm), lambda i, j, k: (k, i)),
                      pl.BlockSpec((tt, tn), lambda i, j, k: (k, j))],
            out_specs=pl.BlockSpec((tm, tn), lambda i, j, k: (i, j)),
            out_shape=jax.ShapeDtypeStruct((m, n), F32),
            compiler_params=_params(("parallel", "parallel", "arbitrary")),
        )(a, b)

    m, k = a.shape
    tm = _pick(m, (512, 256, 128))
    if mode == "nn":
        n = b.shape[1]
        tn = _pick(n, _TN)
        b_spec = pl.BlockSpec((k, tn), lambda i, j: (0, j))
        dot = _dot
    else:
        n = b.shape[0]
        tn = _pick(n, _TN if k <= 4096 else (512, 256, 128))
        b_spec = pl.BlockSpec((tn, k), lambda i, j: (j, 0))
        dot = _dot_nt

    def body(a_ref, b_ref, o_ref):
        o_ref[...] = dot(a_ref[...], b_ref[...]).astype(o_ref.dtype)

    return pl.pallas_call(
        body, name=name, grid=(m // tm, n // tn),
        in_specs=[pl.BlockSpec((tm, k), lambda i, j: (i, 0)), b_spec],
        out_specs=pl.BlockSpec((tm, tn), lambda i, j: (i, j)),
        out_shape=jax.ShapeDtypeStruct((m, n), out_dtype),
        compiler_params=_params(("parallel", "parallel")),
    )(a, b)


_DOTS = {"nn": _dot, "nt": _dot_nt, "tn": _dot_tn}
_BS = pl.BlockSpec


def _mmcall(name, kind, a, b, a_spec, b_spec, o_spec, o_shape, grid, red_axis=None, out_dtype=F32):
    dot = _DOTS[kind]

    def body(a_ref, b_ref, o_ref):
        if red_axis is None:
            o_ref[...] = dot(a_ref[...], b_ref[...]).astype(o_ref.dtype)
        else:
            @pl.when(pl.program_id(red_axis) == 0)
            def _():
                o_ref[...] = jnp.zeros_like(o_ref)

            o_ref[...] += dot(a_ref[...], b_ref[...])

    sem = tuple("arbitrary" if ax == red_axis else "parallel" for ax in range(len(grid)))
    return pl.pallas_call(
        body, name=name, grid=grid, in_specs=[a_spec, b_spec], out_specs=o_spec,
        out_shape=jax.ShapeDtypeStruct(o_shape, out_dtype), compiler_params=_params(sem),
    )(a, b)


def _mm_stack_out(x, w, name, kind="nn"):
    t, k = x.shape
    s, n = (w.shape[0], w.shape[2]) if kind == "nn" else (w.shape[0], w.shape[1])
    tm = _pick(t, (512, 256, 128))
    w_block = (None, k, n) if kind == "nn" else (None, n, k)
    return _mmcall(name, kind, x, w, _BS((tm, k), lambda i, j: (i, 0)), _BS(w_block, lambda i, j: (j, 0, 0)),
                   _BS((None, tm, n), lambda i, j: (j, i, 0)), (s, t, n), (t // tm, s))


def _mm_stack_red(a, w, name, kind):
    s, t, n = a.shape
    nout = w.shape[2] if kind == "nn" else w.shape[1]
    tm = _pick(t, (512, 256, 128))
    return _mmcall(name, kind, a, w, _BS((None, tm, n), lambda i, j: (j, i, 0)),
                   _BS((None,) + w.shape[1:], lambda i, j: (j, 0, 0)), _BS((tm, nout), lambda i, j: (i, 0)),
                   (t, nout), (t // tm, s), red_axis=1)


def _mm_stack_tn(a, b, name):
    if a.ndim == 2:
        t, k = a.shape
        s, _, n = b.shape
        a_spec, b_spec, o_shape = (lambda tt: _BS((tt, k), lambda j, r: (r, 0))), \
            (lambda tt: _BS((None, tt, n), lambda j, r: (j, r, 0))), (s, k, n)
    else:
        s, t, k = a.shape
        n = b.shape[1]
        a_spec, b_spec, o_shape = (lambda tt: _BS((None, tt, k), lambda j, r: (j, r, 0))), \
            (lambda tt: _BS((tt, n), lambda j, r: (r, 0))), (s, k, n)
    tt = _pick(t, (512, 256, 128))
    return _mmcall(name, "tn", a, b, a_spec(tt), b_spec(tt), _BS((None,) + o_shape[1:], lambda j, r: (j, 0, 0)),
                   o_shape, (s, t // tt), red_axis=1)


def _rows(fn, name, t, tm, ins, vecs, outs, accs=()):
    n_in, n_out, n_acc = len(ins) + len(vecs), len(outs), len(accs)

    def body(*refs):
        res = fn(*refs[:n_in + n_out])
        if n_acc:
            acc_refs = refs[n_in + n_out:]

            @pl.when(pl.program_id(0) == 0)
            def _():
                for r in acc_refs:
                    r[...] = jnp.zeros_like(r)

            for r, val in zip(acc_refs, res):
                r[...] += val

    in_specs = [pl.BlockSpec((tm, bw), functools.partial(lambda i, cb: (i, cb), cb=cb)) for _, bw, cb in ins]
    in_specs += [pl.BlockSpec(v.shape, lambda i: (0, 0)) for v in vecs]
    out_specs = [pl.BlockSpec((tm, w), lambda i: (i, 0)) for w, _ in outs]
    out_specs += [pl.BlockSpec(s, lambda i: (0, 0)) for s in accs]
    out_shape = [jax.ShapeDtypeStruct((t, w), dt) for w, dt in outs]
    out_shape += [jax.ShapeDtypeStruct(s, F32) for s in accs]
    return pl.pallas_call(
        body, name=name, grid=(t // tm,), in_specs=in_specs, out_specs=out_specs, out_shape=out_shape,
        compiler_params=_params(("arbitrary",) if n_acc else ("parallel",)),
    )(*[a for a, _, _ in ins], *vecs)


def _rms(x, g):
    return x * lax.rsqrt(jnp.mean(x * x, axis=-1, keepdims=True) + EPS) * g


def _rms_bwd(x, g, dy):
    xh = x * lax.rsqrt(jnp.mean(x * x, axis=-1, keepdims=True) + EPS)
    r = lax.rsqrt(jnp.mean(x * x, axis=-1, keepdims=True) + EPS)
    dyg = dy * g
    dx = r * (dyg - xh * jnp.mean(dyg * xh, axis=-1, keepdims=True))
    return dx, jnp.sum(dy * xh, axis=0, keepdims=True)


def _hgrn_mats():
    row = lax.broadcasted_iota(jnp.int32, (HB, HB), 0)
    col = lax.broadcasted_iota(jnp.int32, (HB, HB), 1)
    return row, col


def _hgrn_gates(qr, z, t0, t1):
    lb = 1.0 / (1.0 + jnp.exp(t1 - t0))
    sz = _sig(z)
    sneg = 1.0 / (1.0 + jnp.exp(z))
    f = lb + (1.0 - lb) * sz
    return lb, sz, sneg, f, jnp.log(f), (1.0 - lb) * sneg, qr * _sig(qr)


def _hgrn_scores(q, k, cum, cmid):
    qd = q * jnp.exp(jnp.minimum(cmid, EXP_CLAMP))
    qd_b = qd.astype(BF16)
    kds, parts = [], []
    for i in range(HB // SUB):
        mid = cum[SUB * i + SUB // 2 - 1:SUB * i + SUB // 2, :]
        kd = k * jnp.exp(jnp.minimum(mid - cum, EXP_CLAMP))
        kds.append(kd)
        parts.append(_dot_nt(qd_b[SUB * i:SUB * (i + 1)], kd))
    return qd, kds, jnp.concatenate(parts, axis=0)


def _hgrn_fwd(p_hg, table, gain, t):
    nblk = t // HB

    def body(q_ref, f_ref, i_ref, g_ref, tab_ref, gain_ref, o_ref, y_ref, st_ref, state):
        @pl.when(pl.program_id(0) == 0)
        def _():
            state[...] = jnp.zeros_like(state)

        row, col = _hgrn_mats()
        causal = col <= row
        tri = causal.astype(BF16)
        trimid = tri - (col <= (row // SUB) * SUB + SUB // 2 - 1).astype(BF16)
        for h in range(NH):
            sl = slice(HD * h, HD * (h + 1))
            v, gr = i_ref[:, sl], g_ref[:, sl]
            _, _, _, _, lf, k, q = _hgrn_gates(q_ref[:, sl], f_ref[:, sl], tab_ref[0:1, sl], tab_ref[1:2, sl])
            cum = _dot_sel(tri, lf)
            cmid = _dot_sel(trimid, lf)
            _, _, s = _hgrn_scores(q, k, cum, cmid)
            p = jnp.where(causal, s, 0.0)
            st = state[h]
            st_ref[h, 0] = st
            o = _dot(p, v) + _dot_nt(q * jnp.exp(cum), st)
            last = cum[HB - 1:HB, :]
            state[h] = st * jnp.exp(last) + _dot_tn(v, k * jnp.exp(last - cum))
            o_ref[:, sl] = o
            y_ref[:, sl] = (_rms(o, gain_ref[...]) * gr * _sig(gr)).astype(BF16)

    blk = lambda cb: pl.BlockSpec((HB, HGW), functools.partial(lambda n, cb: (n, cb), cb=cb))
    return pl.pallas_call(
        body, name="hgrn_fwd", grid=(nblk,),
        in_specs=[blk(0), blk(1), blk(2), blk(3), pl.BlockSpec((2, HGW), lambda n: (0, 0)),
                  pl.BlockSpec((1, HD), lambda n: (0, 0))],
        out_specs=[pl.BlockSpec((HB, HGW), lambda n: (n, 0)), pl.BlockSpec((HB, HGW), lambda n: (n, 0)),
                   pl.BlockSpec((NH, 1, HD, HD), lambda n: (0, n, 0, 0))],
        out_shape=[jax.ShapeDtypeStruct((t, HGW), F32), jax.ShapeDtypeStruct((t, HGW), BF16),
                   jax.ShapeDtypeStruct((NH, nblk, HD, HD), F32)],
        scratch_shapes=[pltpu.VMEM((NH, HD, HD), F32)],
        compiler_params=_params(("arbitrary",)),
    )(p_hg, p_hg, p_hg, p_hg, table, gain)


def _hgrn_bwd(p_hg, table, gain, o_pre, states, dy, t):
    nblk = t // HB

    def body(q_ref, f_ref, i_ref, g_ref, tab_ref, gain_ref, o_ref, st_ref, dy_ref, dp_ref, dtab_ref, dgain_ref,
             dstate):
        @pl.when(pl.program_id(0) == 0)
        def _():
            dstate[...] = jnp.zeros_like(dstate)
            dtab_ref[...] = jnp.zeros_like(dtab_ref)
            dgain_ref[...] = jnp.zeros_like(dgain_ref)

        row, col = _hgrn_mats()
        causal = col <= row
        tri = causal.astype(BF16)
        midrow = (row // SUB) * SUB + SUB // 2 - 1
        trimid = tri - (col <= midrow).astype(BF16)
        tri_t = (row <= col).astype(BF16)
        midcol = (col // SUB) * SUB + SUB // 2 - 1
        trimid_t = tri_t - (row <= midcol).astype(BF16)
        dgain = jnp.zeros((1, HD), F32)
        for h in range(NH):
            sl = slice(HD * h, HD * (h + 1))
            qr, z, v, gr = q_ref[:, sl], f_ref[:, sl], i_ref[:, sl], g_ref[:, sl]
            lb, sz, sneg, f, lf, k, q = _hgrn_gates(qr, z, tab_ref[0:1, sl], tab_ref[1:2, sl])
            cum = _dot_sel(tri, lf)
            cmid = _dot_sel(trimid, lf)
            qd, kds, s = _hgrn_scores(q, k, cum, cmid)
            p = jnp.where(causal, s, 0.0)
            st = st_ref[h, 0]
            dst = dstate[h]
            o = o_ref[:, sl]
            sg = _sig(gr)
            dyh = dy_ref[:, sl]
            on = _rms(o, gain_ref[...])
            dgr = dyh * on * sg * (1.0 + gr * (1.0 - sg))
            do, dg_h = _rms_bwd(o, gain_ref[...], dyh * gr * sg)
            dgain = dgain + dg_h
            do_b = do.astype(BF16)
            ecum = jnp.exp(cum)
            qc = q * ecum
            last = cum[HB - 1:HB, :]
            edec = jnp.exp(last - cum)
            kdec = k * edec
            dp = jnp.where(causal, _dot_nt(do_b, v), 0.0)
            dv = _dot(p.T, do_b) + _dot_nt(kdec, dst)
            dqc = _dot(do_b, st)
            dkdec = _dot(v, dst)
            dstate[h] = dst * jnp.exp(last) + _dot(do.T, qc)
            dp_b = dp.astype(BF16)
            dqd = jnp.concatenate([_dot(dp_b[SUB * i:SUB * (i + 1)], kds[i]) for i in range(HB // SUB)], axis=0)
            gq = dqd * qd
            dq = dqd * jnp.exp(jnp.minimum(cmid, EXP_CLAMP)) + dqc * ecum
            gs = dkdec * kdec
            dk = dkdec * edec
            dcum = dqc * qc - gs
            dcum = dcum + jnp.where(row == HB - 1, jnp.sum(gs, axis=0, keepdims=True)
                                    + jnp.exp(last) * jnp.sum(st * dst, axis=0, keepdims=True), 0.0)
            qd_b = qd.astype(BF16)
            for i in range(HB // SUB):
                dkd = _dot_tn(dp_b[SUB * i:SUB * (i + 1)], qd_b[SUB * i:SUB * (i + 1)])
                mid = cum[SUB * i + SUB // 2 - 1:SUB * i + SUB // 2, :]
                dk = dk + dkd * jnp.exp(jnp.minimum(mid - cum, EXP_CLAMP))
                gk = dkd * kds[i]
                dcum = dcum - gk + jnp.where(row == SUB * i + SUB // 2 - 1, jnp.sum(gk, axis=0, keepdims=True), 0.0)
            dlf = _dot_sel(tri_t, dcum) + _dot_sel(trimid_t, gq)
            df = dlf / f - dk
            dz = df * (1.0 - lb) * sz * sneg
            dlb = jnp.sum(df * sneg, axis=0, keepdims=True) * lb * (1.0 - lb)
            dtab_ref[0:1, sl] += dlb
            dtab_ref[1:2, sl] -= dlb
            sq = _sig(qr)
            dp_ref[:, sl] = (dq * sq * (1.0 + qr * (1.0 - sq))).astype(BF16)
            dp_ref[:, HGW + HD * h:HGW + HD * (h + 1)] = dz.astype(BF16)
            dp_ref[:, 2 * HGW + HD * h:2 * HGW + HD * (h + 1)] = dv.astype(BF16)
            dp_ref[:, 3 * HGW + HD * h:3 * HGW + HD * (h + 1)] = dgr.astype(BF16)
        dgain_ref[...] += dgain

    rev = lambda cb: pl.BlockSpec((HB, HGW), functools.partial(lambda n, cb: (nblk - 1 - n, cb), cb=cb))
    return pl.pallas_call(
        body, name="hgrn_bwd", grid=(nblk,),
        in_specs=[rev(0), rev(1), rev(2), rev(3), pl.BlockSpec((2, HGW), lambda n: (0, 0)),
                  pl.BlockSpec((1, HD), lambda n: (0, 0)), rev(0),
                  pl.BlockSpec((NH, 1, HD, HD), lambda n: (0, nblk - 1 - n, 0, 0)), rev(0)],
        out_specs=[pl.BlockSpec((HB, 4 * HGW), lambda n: (nblk - 1 - n, 0)),
                   pl.BlockSpec((2, HGW), lambda n: (0, 0)), pl.BlockSpec((1, HD), lambda n: (0, 0))],
        out_shape=[jax.ShapeDtypeStruct((t, 4 * HGW), BF16), jax.ShapeDtypeStruct((2, HGW), F32),
                   jax.ShapeDtypeStruct((1, HD), F32)],
        scratch_shapes=[pltpu.VMEM((NH, HD, HD), F32)],
        compiler_params=_params(("arbitrary",)),
    )(p_hg, p_hg, p_hg, p_hg, table, gain, o_pre, states, dy)


def _rope_mat():
    r = lax.broadcasted_iota(jnp.int32, (ROPE, ROPE), 0)
    c = lax.broadcasted_iota(jnp.int32, (ROPE, ROPE), 1)
    half = ROPE // 2
    return ((r == c - half).astype(F32) - (r == c + half).astype(F32)).astype(BF16)


def _mla_prep_fwd(p_mla, cs, sn, wq, wkv, gql, gkvl, gq, gk, t):
    tm = _pick(t, (256, 128))

    def body(p_ref, cs_ref, sn_ref, wq_ref, wkv_ref, gql_ref, gkvl_ref, gq_ref, gk_ref,
             q_ref, k_ref, v_ref):
        rmat = _rope_mat()
        cqn = _rms(p_ref[:, 0:QL], gql_ref[...]).astype(BF16)
        ckvn = _rms(p_ref[:, QL:QL + KVL], gkvl_ref[...]).astype(BF16)
        kpe = p_ref[:, QL + KVL:QL + KVL + ROPE]
        c, s = cs_ref[...], sn_ref[...]
        rot = lambda x: x * c + _sel_dot(x, rmat) * s
        for h in range(NH):
            qa, qr = _dot(cqn, wq_ref[h, :, 0:HD]), _dot(cqn, wq_ref[h, :, HD:QK])
            rq = lax.rsqrt((jnp.sum(qa * qa, -1, keepdims=True) + jnp.sum(qr * qr, -1, keepdims=True)) / QK + EPS)
            q_ref[h, :, 0:HD] = (qa * rq * gq_ref[:, 0:HD] * SCALE).astype(BF16)
            q_ref[h, :, HD:QK] = (rot(qr * rq * gq_ref[:, HD:QK]) * SCALE).astype(BF16)
            kn = _dot(ckvn, wkv_ref[h, :, 0:HD])
            rk = lax.rsqrt((jnp.sum(kn * kn, -1, keepdims=True) + jnp.sum(kpe * kpe, -1, keepdims=True)) / QK + EPS)
            k_ref[h, :, 0:HD] = (kn * rk * gk_ref[:, 0:HD]).astype(BF16)
            k_ref[h, :, HD:QK] = rot(kpe * rk * gk_ref[:, HD:QK]).astype(BF16)
            v_ref[h] = _dot(ckvn, wkv_ref[h, :, HD:2 * HD]).astype(BF16)

    whole = lambda a: pl.BlockSpec(a.shape, functools.partial(lambda i, nd: (0,) * nd, nd=a.ndim))
    return pl.pallas_call(
        body, name="mla_prep_fwd", grid=(t // tm,),
        in_specs=[pl.BlockSpec((tm, QL + KVL + ROPE), lambda i: (i, 0)), pl.BlockSpec((tm, ROPE), lambda i: (i, 0)),
                  pl.BlockSpec((tm, ROPE), lambda i: (i, 0))] + [whole(a) for a in (wq, wkv, gql, gkvl, gq, gk)],
        out_specs=[pl.BlockSpec((NH, tm, QK), lambda i: (0, i, 0)), pl.BlockSpec((NH, tm, QK), lambda i: (0, i, 0)),
                   pl.BlockSpec((NH, tm, HD), lambda i: (0, i, 0))],
        out_shape=[jax.ShapeDtypeStruct((NH, t, QK), BF16), jax.ShapeDtypeStruct((NH, t, QK), BF16),
                   jax.ShapeDtypeStruct((NH, t, HD), BF16)],
        compiler_params=_params(("parallel",)),
    )(p_mla, cs, sn, wq, wkv, gql, gkvl, gq, gk)


def _mla_prep_bwd(p_mla, cs, sn, wq, wkv, gql, gkvl, gq, gk, dq, dk, dv, t):
    tm = _pick(t, (256, 128))

    def body(p_ref, cs_ref, sn_ref, wq_ref, wkv_ref, gql_ref, gkvl_ref, gq_ref, gk_ref,
             dq_ref, dk_ref, dv_ref,
             dp_ref, dwq_ref, dwkv_ref, dgql_ref, dgkvl_ref, dgq_ref, dgk_ref):
        accs = (dwq_ref, dwkv_ref, dgql_ref, dgkvl_ref, dgq_ref, dgk_ref)

        @pl.when(pl.program_id(0) == 0)
        def _():
            for r in accs:
                r[...] = jnp.zeros_like(r)

        rmat = _rope_mat()
        rmat_t = -rmat
        cq, ckv = p_ref[:, 0:QL], p_ref[:, QL:QL + KVL]
        kpe = p_ref[:, QL + KVL:QL + KVL + ROPE]
        cqn_f, ckvn_f = _rms(cq, gql_ref[...]), _rms(ckv, gkvl_ref[...])
        cqn, ckvn = cqn_f.astype(BF16), ckvn_f.astype(BF16)
        cqn_t, ckvn_t = cqn_f.T.astype(BF16), ckvn_f.T.astype(BF16)
        c, s = cs_ref[...], sn_ref[...]
        unrot = lambda dy: dy * c + _sel_dot(dy * s, rmat_t)
        dcqn = jnp.zeros((tm, QL), F32)
        dckvn = jnp.zeros((tm, KVL), F32)
        dkpe = jnp.zeros((tm, ROPE), F32)
        dgq_a, dgq_r = jnp.zeros((1, HD), F32), jnp.zeros((1, ROPE), F32)
        dgk_a, dgk_r = jnp.zeros((1, HD), F32), jnp.zeros((1, ROPE), F32)
        for h in range(NH):
            qa, qr = _dot(cqn, wq_ref[h, :, 0:HD]), _dot(cqn, wq_ref[h, :, HD:QK])
            rq = lax.rsqrt((jnp.sum(qa * qa, -1, keepdims=True) + jnp.sum(qr * qr, -1, keepdims=True)) / QK + EPS)
            xa, xr = qa * rq, qr * rq
            dya = dq_ref[h, :, 0:HD] * SCALE
            dyr = unrot(dq_ref[h, :, HD:QK] * SCALE)
            dgq_a += jnp.sum(dya * xa, axis=0, keepdims=True)
            dgq_r += jnp.sum(dyr * xr, axis=0, keepdims=True)
            ga, gr_ = dya * gq_ref[:, 0:HD], dyr * gq_ref[:, HD:QK]
            mean = (jnp.sum(ga * xa, -1, keepdims=True) + jnp.sum(gr_ * xr, -1, keepdims=True)) / QK
            dqa = (rq * (ga - xa * mean)).astype(BF16)
            dqr = (rq * (gr_ - xr * mean)).astype(BF16)
            dwq_ref[h, :, 0:HD] += _dot(cqn_t, dqa)
            dwq_ref[h, :, HD:QK] += _dot(cqn_t, dqr)
            dcqn += _dot_nt(dqa, wq_ref[h, :, 0:HD]) + _dot_nt(dqr, wq_ref[h, :, HD:QK])
            kn = _dot(ckvn, wkv_ref[h, :, 0:HD])
            rk = lax.rsqrt((jnp.sum(kn * kn, -1, keepdims=True) + jnp.sum(kpe * kpe, -1, keepdims=True)) / QK + EPS)
            ya, yr = kn * rk, kpe * rk
            dka = dk_ref[h, :, 0:HD]
            dkr = unrot(dk_ref[h, :, HD:QK])
            dgk_a += jnp.sum(dka * ya, axis=0, keepdims=True)
            dgk_r += jnp.sum(dkr * yr, axis=0, keepdims=True)
            ha, hr = dka * gk_ref[:, 0:HD], dkr * gk_ref[:, HD:QK]
            mean = (jnp.sum(ha * ya, -1, keepdims=True) + jnp.sum(hr * yr, -1, keepdims=True)) / QK
            dkn = (rk * (ha - ya * mean)).astype(BF16)
            dkpe += rk * (hr - yr * mean)
            dvh = dv_ref[h].astype(BF16)
            dwkv_ref[h, :, 0:HD] += _dot(ckvn_t, dkn)
            dwkv_ref[h, :, HD:2 * HD] += _dot(ckvn_t, dvh)
            dckvn += _dot_nt(dkn, wkv_ref[h, :, 0:HD]) + _dot_nt(dvh, wkv_ref[h, :, HD:2 * HD])
        dcq, dg1 = _rms_bwd(cq, gql_ref[...], dcqn)
        dckv, dg2 = _rms_bwd(ckv, gkvl_ref[...], dckvn)
        dp_ref[:, 0:QL] = dcq.astype(BF16)
        dp_ref[:, QL:QL + KVL] = dckv.astype(BF16)
        dp_ref[:, QL + KVL:QL + KVL + ROPE] = dkpe.astype(BF16)
        dgql_ref[...] += dg1
        dgkvl_ref[...] += dg2
        dgq_ref[:, 0:HD] += dgq_a
        dgq_ref[:, HD:QK] += dgq_r
        dgk_ref[:, 0:HD] += dgk_a
        dgk_ref[:, HD:QK] += dgk_r

    whole = lambda a: pl.BlockSpec(a.shape, functools.partial(lambda i, nd: (0,) * nd, nd=a.ndim))
    acc_shapes = [wq.shape, wkv.shape, gql.shape, gkvl.shape, gq.shape, gk.shape]
    return pl.pallas_call(
        body, name="mla_prep_bwd", grid=(t // tm,),
        in_specs=[pl.BlockSpec((tm, QL + KVL + ROPE), lambda i: (i, 0)), pl.BlockSpec((tm, ROPE), lambda i: (i, 0)),
                  pl.BlockSpec((tm, ROPE), lambda i: (i, 0))]
        + [whole(a) for a in (wq, wkv, gql, gkvl, gq, gk)]
        + [pl.BlockSpec((NH, tm, QK), lambda i: (0, i, 0)), pl.BlockSpec((NH, tm, QK), lambda i: (0, i, 0)),
           pl.BlockSpec((NH, tm, HD), lambda i: (0, i, 0))],
        out_specs=[pl.BlockSpec((tm, QL + KVL + ROPE), lambda i: (i, 0))]
        + [pl.BlockSpec(s, functools.partial(lambda i, nd: (0,) * nd, nd=len(s))) for s in acc_shapes],
        out_shape=[jax.ShapeDtypeStruct((t, QL + KVL + ROPE), BF16)]
        + [jax.ShapeDtypeStruct(s, F32) for s in acc_shapes],
        compiler_params=_params(("arbitrary",)),
    )(p_mla, cs, sn, wq, wkv, gql, gkvl, gq, gk, dq, dk, dv)


def _diag_mask(tile, transposed):
    r = lax.broadcasted_iota(jnp.int32, (tile, tile), 0) // CHUNK
    c = lax.broadcasted_iota(jnp.int32, (tile, tile), 1) // CHUNK
    return (r <= c) if transposed else (c <= r)


def _flash_fwd(q, k, v, t):
    tq = _pick(t, ATT_TILES)

    def body(q_ref, k_ref, v_ref, o_ref, lse_ref):
        i = pl.program_id(1)
        qt = q_ref[0]

        def step(j, carry, masked):
            m, l, acc = carry
            kt = k_ref[0, pl.ds(pl.multiple_of(j * tq, tq), tq), :]
            vt = v_ref[0, pl.ds(pl.multiple_of(j * tq, tq), tq), :]
            s = _dot_nt(qt, kt)
            if masked:
                s = jnp.where(_diag_mask(tq, False), s, -jnp.inf)
            m_new = jnp.maximum(m, jnp.max(s, axis=-1, keepdims=True))
            p = jnp.exp(s - m_new)
            alpha = jnp.exp(m - m_new)
            return m_new, alpha * l + jnp.sum(p, axis=-1, keepdims=True), alpha * acc + _dot(p, vt)

        init = (jnp.full((tq, 1), -jnp.inf, F32), jnp.zeros((tq, 1), F32), jnp.zeros((tq, HD), F32))
        carry = lax.fori_loop(0, i, lambda j, cr: step(j, cr, False), init)
        m, l, acc = step(i, carry, True)
        o_ref[...] = (acc / l).astype(BF16)
        lse_ref[0] = m + jnp.log(l)

    return pl.pallas_call(
        body, name="flash_fwd", grid=(NH, t // tq),
        in_specs=[pl.BlockSpec((1, tq, QK), lambda h, i: (h, i, 0)), pl.BlockSpec((1, t, QK), lambda h, i: (h, 0, 0)),
                  pl.BlockSpec((1, t, HD), lambda h, i: (h, 0, 0))],
        out_specs=[pl.BlockSpec((tq, HD), lambda h, i: (i, h)), pl.BlockSpec((1, tq, 1), lambda h, i: (h, i, 0))],
        out_shape=[jax.ShapeDtypeStruct((t, NH * HD), BF16), jax.ShapeDtypeStruct((NH, t, 1), F32)],
        compiler_params=_params(("parallel", "parallel")),
    )(q, k, v)


def _flash_dq(q, k, v, o, do, lse, t):
    tq = _pick(t, ATT_TILES)

    def body(q_ref, k_ref, v_ref, o_ref, do_ref, lse_ref, dq_ref, delta_ref):
        i = pl.program_id(1)
        qt, dot_, lse_t = q_ref[0], do_ref[...], lse_ref[0]
        delta = jnp.sum(dot_.astype(F32) * o_ref[...].astype(F32), axis=-1, keepdims=True)

        def step(j, acc, masked):
            kt = k_ref[0, pl.ds(pl.multiple_of(j * tq, tq), tq), :]
            vt = v_ref[0, pl.ds(pl.multiple_of(j * tq, tq), tq), :]
            p = jnp.exp(_dot_nt(qt, kt) - lse_t)
            if masked:
                p = jnp.where(_diag_mask(tq, False), p, 0.0)
            ds = p * (_dot_nt(dot_, vt) - delta)
            return acc + _dot(ds, kt)

        acc = lax.fori_loop(0, i, lambda j, a: step(j, a, False), jnp.zeros((tq, QK), F32))
        dq_ref[0] = step(i, acc, True)
        delta_ref[0] = delta

    return pl.pallas_call(
        body, name="flash_dq", grid=(NH, t // tq),
        in_specs=[pl.BlockSpec((1, tq, QK), lambda h, i: (h, i, 0)), pl.BlockSpec((1, t, QK), lambda h, i: (h, 0, 0)),
                  pl.BlockSpec((1, t, HD), lambda h, i: (h, 0, 0)), pl.BlockSpec((tq, HD), lambda h, i: (i, h)),
                  pl.BlockSpec((tq, HD), lambda h, i: (i, h)), pl.BlockSpec((1, tq, 1), lambda h, i: (h, i, 0))],
        out_specs=[pl.BlockSpec((1, tq, QK), lambda h, i: (h, i, 0)), pl.BlockSpec((1, tq, 1), lambda h, i: (h, i, 0))],
        out_shape=[jax.ShapeDtypeStruct((NH, t, QK), F32), jax.ShapeDtypeStruct((NH, t, 1), F32)],
        compiler_params=_params(("parallel", "parallel")),
    )(q, k, v, o, do, lse)


def _flash_dkv(q, k, v, do, lse_row, delta_row, t):
    tk = _pick(t, ATT_TILES)
    nq = t // tk

    def body(q_ref, k_ref, v_ref, do_ref, lse_ref, delta_ref, dk_ref, dv_ref):
        j = pl.program_id(1)
        kt, vt = k_ref[0], v_ref[0]

        def step(i, carry, masked):
            dk, dv = carry
            rows = pl.ds(pl.multiple_of(i * tk, tk), tk)
            qt, dot_ = q_ref[0, rows, :], do_ref[rows, :]
            p = jnp.exp(_dot_nt(kt, qt) - lse_ref[0, :, rows])
            if masked:
                p = jnp.where(_diag_mask(tk, True), p, 0.0)
            ds = p * (_dot_nt(vt, dot_) - delta_ref[0, :, rows])
            return dk + _dot(ds, qt), dv + _dot(p, dot_)

        carry = step(j, (jnp.zeros((tk, QK), F32), jnp.zeros((tk, HD), F32)), True)
        dk, dv = lax.fori_loop(j + 1, nq, lambda i, cr: step(i, cr, False), carry)
        dk_ref[0] = dk
        dv_ref[0] = dv

    return pl.pallas_call(
        body, name="flash_dkv", grid=(NH, nq),
        in_specs=[pl.BlockSpec((1, t, QK), lambda h, j: (h, 0, 0)), pl.BlockSpec((1, tk, QK), lambda h, j: (h, j, 0)),
                  pl.BlockSpec((1, tk, HD), lambda h, j: (h, j, 0)), pl.BlockSpec((t, HD), lambda h, j: (0, h)),
                  pl.BlockSpec((1, 1, t), lambda h, j: (h, 0, 0)), pl.BlockSpec((1, 1, t), lambda h, j: (h, 0, 0))],
        out_specs=[pl.BlockSpec((1, tk, QK), lambda h, j: (h, j, 0)), pl.BlockSpec((1, tk, HD), lambda h, j: (h, j, 0))],
        out_shape=[jax.ShapeDtypeStruct((NH, t, QK), F32), jax.ShapeDtypeStruct((NH, t, HD), F32)],
        compiler_params=_params(("parallel", "parallel")),
    )(q, k, v, do, lse_row, delta_row)


def _swiglu_fwd(gu, name):
    s, t, n = gu.shape
    tm = _pick(t, (256, 128))

    def body(gu_ref, a_ref):
        g = gu_ref[0]
        a_ref[...] = (g * _sig(g) * gu_ref[1]).astype(BF16)

    return pl.pallas_call(
        body, name=name, grid=(s // 2, t // tm),
        in_specs=[_BS((2, None, tm, n), lambda k, i: (0, k, i, 0))],
        out_specs=_BS((None, tm, n), lambda k, i: (k, i, 0)),
        out_shape=jax.ShapeDtypeStruct((s // 2, t, n), BF16), compiler_params=_params(("parallel", "parallel")),
    )(gu.reshape(2, s // 2, t, n))


def _swiglu_bwd(gu, da, name):
    s, t, n = gu.shape
    tm = _pick(t, (256, 128))

    def body(gu_ref, da_ref, d_ref):
        g, u, da_ = gu_ref[0], gu_ref[1], da_ref[...]
        sg = _sig(g)
        d_ref[0] = (da_ * u * sg * (1.0 + g * (1.0 - sg))).astype(BF16)
        d_ref[1] = (da_ * g * sg).astype(BF16)

    return pl.pallas_call(
        body, name=name, grid=(s // 2, t // tm),
        in_specs=[_BS((2, None, tm, n), lambda k, i: (0, k, i, 0)), _BS((None, tm, n), lambda k, i: (k, i, 0))],
        out_specs=_BS((2, None, tm, n), lambda k, i: (0, k, i, 0)),
        out_shape=jax.ShapeDtypeStruct((2, s // 2, t, n), BF16), compiler_params=_params(("parallel", "parallel")),
    )(gu.reshape(2, s // 2, t, n), da).reshape(s, t, n)


def _ffn_fwd(xn, w_in, w_out, tag):
    gu = _mm_stack_out(xn, w_in, tag + "_in")
    a = _swiglu_fwd(gu, tag + "_swiglu")
    return gu, a, _mm_stack_red(a, w_out, tag + "_out", "nn")


def _ffn_bwd(dfo, xn, gu, a, w_in, w_out, tag):
    da = _mm_stack_out(dfo, w_out, tag + "_da", "nt")
    dw_out = _mm_stack_tn(a, dfo, tag + "_dwout")
    dgu = _swiglu_bwd(gu, da, tag + "_swiglu_bwd")
    return _mm_stack_red(dgu, w_in, tag + "_dxn", "nt"), _mm_stack_tn(xn, dgu, tag + "_dwin"), dw_out


def _local_step(x, target, cs, sn, w, s):
    t = x.shape[0]
    tm = _pick(t, (256, 128))
    tt = _pick(t, (512, 256, 128))
    g = {}
    rows_of = lambda n: w[n].reshape(-1, w[n].shape[2])
    ffn_out = lambda n: w[n].reshape(4, FF // 4, D)
    w_hgb, w_mlab, w_o = rows_of("w_hg_branch"), rows_of("w_mla_branch"), rows_of("w_out")
    w_in_nat = w["w_in"].transpose(1, 0, 2).reshape(D, -1)
    w_mrg = w["w_merge"]
    mw = w_mrg.shape[2]

    def norm_fn(x_ref, g_ref, o_ref):
        o_ref[...] = _rms(x_ref[...], g_ref[...]).astype(BF16)

    xn1 = _rows(norm_fn, "norm1", t, tm, [(x, D, 0)], [s["ffn1_norm"]], [(D, BF16)])[0]
    gu1, a1, f1 = _ffn_fwd(xn1, w["ffn1_w_in"], ffn_out("ffn1_w_out"), "ffn1")

    def res_norm_fn(scale):
        def fn(h_ref, f_ref, g_ref, h_out, n_out):
            h = h_ref[...] + scale * f_ref[...]
            h_out[...] = h
            n_out[...] = _rms(h, g_ref[...]).astype(BF16)
        return fn

    h1, u = _rows(res_norm_fn(0.5), "res_norm1", t, tm, [(x, D, 0), (f1, D, 0)], [s["mix_norm"]],
                  [(D, F32), (D, BF16)])
    w_in_hg, w_in_mla = w_in_nat[:, :4 * HGW], w_in_nat[:, 4 * HGW:]
    p_hg = _mm(u, w_in_hg, "nn", "proj_hg")
    p_mla = _mm(u, w_in_mla, "nn", "proj_mla")
    gpre = _mmcall("proj_gate", "nn", u, w_mrg, _BS((tt, D), lambda i, j: (i, 0)),
                   _BS((None, D, mw), lambda i, j: (j, 0, 0)), _BS((tt, mw), lambda i, j: (i, j)),
                   (t, 2 * D), (t // tt, N_DEV))
    o_pre, hgy, states = _hgrn_fwd(p_hg, s["hg_lb_table"], s["hg_out_norm"], t)
    prep_args = (p_mla, cs, sn, w["w_q_up"], w["w_kv_up"], s["mla_q_lora_norm"], s["mla_kv_lora_norm"],
                 s["q_head_norm"], s["k_head_norm"])
    q, k, v = _mla_prep_fwd(*prep_args, t)
    att, lse = _flash_fwd(q, k, v, t)
    y_hg = _mm(hgy, w_hgb, "nn", "branch_hg")
    y_mla = _mm(att, w_mlab, "nn", "branch_mla")

    def mix_fn(gh_ref, gm_ref, yh_ref, ym_ref, b_ref, o_ref):
        gh = _sig(gh_ref[...] + b_ref[:, 0:D])
        gm = _sig(gm_ref[...] + b_ref[:, D:2 * D])
        o_ref[...] = (gh * yh_ref[...] + gm * ym_ref[...]).astype(BF16)

    mixed = _rows(mix_fn, "mix", t, tm, [(gpre, D, 0), (gpre, D, 1), (y_hg, D, 0), (y_mla, D, 0)], [s["b_merge"]],
                  [(D, BF16)])[0]
    mo = _mm(mixed, w_o, "nn", "mix_out")
    h2, xn2 = _rows(res_norm_fn(1.0), "res_norm2", t, tm, [(h1, D, 0), (mo, D, 0)], [s["ffn2_norm"]],
                    [(D, F32), (D, BF16)])
    gu2, a2, f2 = _ffn_fwd(xn2, w["ffn2_w_in"], ffn_out("ffn2_w_out"), "ffn2")

    def loss_fn(h_ref, f_ref, tg_ref, g_ref, dh_out, dhb_out):
        h = h_ref[...] + 0.5 * f_ref[...]
        e = _rms(h, g_ref[...]) - tg_ref[...]
        dh, dgain = _rms_bwd(h, g_ref[...], e / D)
        dh_out[...] = dh
        dhb_out[...] = (0.5 * dh).astype(BF16)
        return dgain, jnp.full((1, LANE), 0.5 / D * jnp.sum(e * e), F32)

    dh3, dfo2, g["final_norm"], loss = _rows(loss_fn, "loss", t, tm, [(h2, D, 0), (f2, D, 0), (target, D, 0)],
                                             [s["final_norm"]], [(D, F32), (D, BF16)], [(1, D), (1, LANE)])

    def norm_bwd_fn(scale):
        def fn(h_ref, dxn_ref, dh_ref, g_ref, dh_out, dhb_out):
            dx, dgain = _rms_bwd(h_ref[...], g_ref[...], dxn_ref[...])
            dh = dh_ref[...] + dx
            dh_out[...] = dh
            dhb_out[...] = (scale * dh).astype(BF16)
            return (dgain,)
        return fn

    as_rows = lambda a: a.reshape((N_DEV, -1) + a.shape[-1:])
    dxn2, g["ffn2_w_in"], dwo = _ffn_bwd(dfo2, xn2, gu2, a2, w["ffn2_w_in"], ffn_out("ffn2_w_out"), "ffn2")
    g["ffn2_w_out"] = as_rows(dwo)
    dh2, dh2b, g["ffn2_norm"] = _rows(norm_bwd_fn(1.0), "norm2_bwd", t, tm, [(h2, D, 0), (dxn2, D, 0), (dh3, D, 0)],
                                      [s["ffn2_norm"]], [(D, F32), (D, BF16)], [(1, D)])
    dmixed = _mm(dh2b, w_o, "nt", "mix_out_dx")
    g["w_out"] = as_rows(_mm(mixed, dh2b, "tn", "mix_out_dw"))

    def mix_bwd_fn(gh_ref, gm_ref, yh_ref, ym_ref, dm_ref, b_ref, dyh_out, dym_out, dg_out):
        gh = _sig(gh_ref[...] + b_ref[:, 0:D])
        gm = _sig(gm_ref[...] + b_ref[:, D:2 * D])
        dm = dm_ref[...]
        dyh_out[...] = (dm * gh).astype(BF16)
        dym_out[...] = (dm * gm).astype(BF16)
        dgh = dm * yh_ref[...] * gh * (1.0 - gh)
        dgm = dm * ym_ref[...] * gm * (1.0 - gm)
        dg_out[:, 0:D] = dgh.astype(BF16)
        dg_out[:, D:2 * D] = dgm.astype(BF16)
        return (jnp.concatenate([jnp.sum(dgh, axis=0, keepdims=True), jnp.sum(dgm, axis=0, keepdims=True)], axis=1),)

    dyh, dym, dgpre, g["b_merge"] = _rows(
        mix_bwd_fn, "mix_bwd", t, tm, [(gpre, D, 0), (gpre, D, 1), (y_hg, D, 0), (y_mla, D, 0), (dmixed, D, 0)],
        [s["b_merge"]], [(D, BF16), (D, BF16), (2 * D, BF16)], [(1, 2 * D)])
    g["w_hg_branch"] = as_rows(_mm(hgy, dyh, "tn", "branch_hg_dw"))
    g["w_mla_branch"] = as_rows(_mm(att, dym, "tn", "branch_mla_dw"))
    g["w_merge"] = _mmcall("proj_gate_dw", "tn", u, dgpre, _BS((tt, D), lambda j, r: (r, 0)),
                           _BS((tt, mw), lambda j, r: (r, j)), _BS((None, D, mw), lambda j, r: (j, 0, 0)),
                           (N_DEV, D, mw), (N_DEV, t // tt), red_axis=1)
    dhgy = _mm(dyh, w_hgb, "nt", "branch_hg_dx")
    datt = _mm(dym, w_mlab, "nt", "branch_mla_dx", out_dtype=BF16)
    du_gate = _mmcall("proj_gate_dx", "nt", dgpre, w_mrg, _BS((tt, mw), lambda i, j: (i, j)),
                      _BS((None, D, mw), lambda i, j: (j, 0, 0)), _BS((tt, D), lambda i, j: (i, 0)),
                      (t, D), (t // tt, N_DEV), red_axis=1)

    dq, delta = _flash_dq(q, k, v, att, datt, lse, t)
    dk, dv = _flash_dkv(q, k, v, datt, lse.reshape(NH, 1, t), delta.reshape(NH, 1, t), t)
    (dp_mla, g["w_q_up"], g["w_kv_up"], g["mla_q_lora_norm"], g["mla_kv_lora_norm"], g["q_head_norm"],
     g["k_head_norm"]) = _mla_prep_bwd(*prep_args, dq, dk, dv, t)
    dp_hg, g["hg_lb_table"], g["hg_out_norm"] = _hgrn_bwd(p_hg, s["hg_lb_table"], s["hg_out_norm"], o_pre, states,
                                                          dhgy, t)
    dw_in_nat = jnp.concatenate([_mm(u, dp_hg, "tn", "proj_hg_dw"), _mm(u, dp_mla, "tn", "proj_mla_dw")], axis=1)
    g["w_in"] = dw_in_nat.reshape(D, N_DEV, -1).transpose(1, 0, 2)
    du_hg = _mm(dp_hg, w_in_hg, "nt", "proj_hg_dx")
    du_mla = _mm(dp_mla, w_in_mla, "nt", "proj_mla_dx")

    def mixnorm_bwd_fn(h_ref, a_ref, b_ref, c_ref, dh_ref, g_ref, dh_out, dhb_out):
        dx, dgain = _rms_bwd(h_ref[...], g_ref[...], a_ref[...] + b_ref[...] + c_ref[...])
        dh = dh_ref[...] + dx
        dh_out[...] = dh
        dhb_out[...] = (0.5 * dh).astype(BF16)
        return (dgain,)

    dh1, dfo1, g["mix_norm"] = _rows(mixnorm_bwd_fn, "mixnorm_bwd", t, tm,
                                     [(h1, D, 0), (du_hg, D, 0), (du_mla, D, 0), (du_gate, D, 0), (dh2, D, 0)],
                                     [s["mix_norm"]], [(D, F32), (D, BF16)], [(1, D)])
    dxn1, g["ffn1_w_in"], dwo = _ffn_bwd(dfo1, xn1, gu1, a1, w["ffn1_w_in"], ffn_out("ffn1_w_out"), "ffn1")
    g["ffn1_w_out"] = as_rows(dwo)
    grad_x, _, g["ffn1_norm"] = _rows(norm_bwd_fn(1.0), "norm1_bwd", t, tm, [(x, D, 0), (dxn1, D, 0), (dh1, D, 0)],
                                      [s["ffn1_norm"]], [(D, F32), (D, BF16)], [(1, D)])
    return loss, grad_x, g


def _coords():
    return lax.axis_index("x"), lax.axis_index("y"), lax.axis_index("c")


def _hbm_call(body, name, ins, out_shapes, scratch):
    any_spec = pl.BlockSpec(memory_space=pl.ANY)
    return pl.pallas_call(
        body, name=name, out_shape=[jax.ShapeDtypeStruct(s, dt) for s, dt in out_shapes],
        in_specs=[any_spec] * len(ins), out_specs=[any_spec] * len(out_shapes), scratch_shapes=scratch,
    )(*ins)


def _all_gather(blocks, name):
    nb = len(blocks)

    def body(*refs):
        x_refs, out_refs = refs[:nb], refs[nb:2 * nb]
        send_sems, recv_sems, local_sems = refs[2 * nb:]
        x, y, c = _coords()
        me, sibling = (x, y, c), (x, y, 1 - c)
        chips = [(1 - x, y), (x, 1 - y), (1 - x, 1 - y)]

        def slot(b, px, py, pc):
            return out_refs[b].at[4 * px + 2 * py + pc]

        def copy(b, kk, block_of, to, src=None):
            return pltpu.make_async_remote_copy(
                src_ref=slot(b, *block_of) if src is None else src, dst_ref=slot(b, *block_of),
                send_sem=send_sems.at[b, kk], recv_sem=recv_sems.at[b, kk], device_id=to, device_id_type=MESH)

        mine = [pltpu.make_async_copy(x_refs[b], slot(b, *me), local_sems.at[b]) for b in range(nb)]
        for cp in mine:
            cp.start()
        first = [copy(b, 0, me, sibling, src=x_refs[b]) for b in range(nb)]
        first += [copy(b, 1 + j, me, (*chip, c), src=x_refs[b]) for j, chip in enumerate(chips) for b in range(nb)]
        for cp in first:
            cp.start()
        passed = []
        for j, chip in enumerate(chips):
            for b in range(nb):
                copy(b, 1 + j, (*chip, c), me).wait_recv()
                passed.append(copy(b, 4 + j, (*chip, c), sibling))
                passed[-1].start()
        for b in range(nb):
            copy(b, 0, sibling, me).wait_recv()
        for j, chip in enumerate(chips):
            for b in range(nb):
                copy(b, 4 + j, (*chip, 1 - c), me).wait_recv()
        for cp in first + passed:
            cp.wait_send()
        for cp in mine:
            cp.wait()

    return _hbm_call(body, name, blocks, [((N_DEV,) + b.shape, b.dtype) for b in blocks],
                     [pltpu.SemaphoreType.DMA((nb, 7)), pltpu.SemaphoreType.DMA((nb, 7)),
                      pltpu.SemaphoreType.DMA((nb,))])


def _sibling_swap(bufs, name):
    nb = len(bufs)

    def body(*refs):
        x_refs, out_refs = refs[:nb], refs[nb:2 * nb]
        send_sems, recv_sems = refs[2 * nb:]
        x, y, c = _coords()
        copies = [pltpu.make_async_remote_copy(
            src_ref=x_refs[b].at[2 * q + 1 - c], dst_ref=out_refs[b].at[q], send_sem=send_sems.at[b, q],
            recv_sem=recv_sems.at[b, q], device_id=(x, y, 1 - c), device_id_type=MESH)
            for b in range(nb) for q in range(4)]
        for cp in copies:
            cp.start()
        for cp in copies:
            cp.wait()

    return _hbm_call(body, name, bufs, [((4,) + b.shape[1:], b.dtype) for b in bufs],
                     [pltpu.SemaphoreType.DMA((nb, 4)), pltpu.SemaphoreType.DMA((nb, 4))])


def _chip_exchange(bufs, name):
    nb = len(bufs)

    def body(*refs):
        x_refs, out_refs = refs[:nb], refs[nb:2 * nb]
        send_sems, recv_sems, local_sems = refs[2 * nb:]
        x, y, c = _coords()
        chips = [(1 - x, y), (x, 1 - y), (1 - x, 1 - y)]
        mine = [pltpu.make_async_copy(x_refs[b].at[2 * x + y], out_refs[b].at[2 * x + y], local_sems.at[b])
                for b in range(nb)]
        for cp in mine:
            cp.start()
        copies = [pltpu.make_async_remote_copy(
            src_ref=x_refs[b].at[2 * px + py], dst_ref=out_refs[b].at[2 * x + y], send_sem=send_sems.at[b, j],
            recv_sem=recv_sems.at[b, j], device_id=(px, py, c), device_id_type=MESH)
            for j, (px, py) in enumerate(chips) for b in range(nb)]
        for cp in copies:
            cp.start()
        for j, (px, py) in enumerate(chips):
            for b in range(nb):
                pltpu.make_async_remote_copy(
                    src_ref=x_refs[b].at[2 * x + y], dst_ref=out_refs[b].at[2 * px + py], send_sem=send_sems.at[b, j],
                    recv_sem=recv_sems.at[b, j], device_id=(px, py, c), device_id_type=MESH).wait_recv()
        for cp in copies:
            cp.wait_send()
        for cp in mine:
            cp.wait()

    return _hbm_call(body, name, bufs, [(b.shape, b.dtype) for b in bufs],
                     [pltpu.SemaphoreType.DMA((nb, 3)), pltpu.SemaphoreType.DMA((nb, 3)),
                      pltpu.SemaphoreType.DMA((nb,))])


def _chip_sum(g, r1, c, name):
    _, r, cw = g.shape
    tr = _pick(r, (256, 176, 128))

    def body(c_ref, g_ref, r_ref, o_ref):
        o_ref[...] = g_ref[...] + r_ref[...]

    grid_spec = pltpu.PrefetchScalarGridSpec(
        num_scalar_prefetch=1, grid=(4, r // tr),
        in_specs=[_BS((None, None, tr, cw), lambda q, i, c_ref: (q, c_ref[0], i, 0)),
                  _BS((None, tr, cw), lambda q, i, c_ref: (q, i, 0))],
        out_specs=_BS((None, tr, cw), lambda q, i, c_ref: (q, i, 0)))
    return pl.pallas_call(
        body, name=name, grid_spec=grid_spec, out_shape=jax.ShapeDtypeStruct((4, r, cw), F32),
        compiler_params=_params(("parallel", "parallel")),
    )(c.reshape(1).astype(jnp.int32), g.reshape(4, 2, r, cw), r1)


def _adamw_math(w, g, m, v):
    m = B1 * m + (1.0 - B1) * g
    v = B2 * v + (1.0 - B2) * (g * g)
    m_hat = m / (1.0 - B1 ** STEP)
    v_hat = v / (1.0 - B2 ** STEP)
    return -LR * (m_hat / (jnp.sqrt(v_hat) + AEPS) + WD * w), m, v


def _sum_adamw(parts, w, m, v, name):
    r, c = w.shape
    tr = _pick(r, (256, 176, 128))

    def body(p0, p1, p2, p3, w_ref, m_ref, v_ref, g_out, d_out, m_out, v_out):
        g = ((p0[...] + p1[...]) + p2[...]) + p3[...]
        g_out[...] = g
        d_out[...], m_out[...], v_out[...] = _adamw_math(w_ref[...], g, m_ref[...], v_ref[...])

    part = lambda q: _BS((None, tr, c), functools.partial(lambda i, q: (q, i, 0), q=q))
    plain = _BS((tr, c), lambda i: (i, 0))
    return pl.pallas_call(
        body, name=name, grid=(r // tr,), in_specs=[part(q) for q in range(4)] + [plain] * 3,
        out_specs=[plain] * 4, out_shape=[jax.ShapeDtypeStruct((r, c), F32)] * 4,
        compiler_params=_params(("parallel",)),
    )(parts, parts, parts, parts, w, m, v)


def _small_update(gathered, w, m, v):
    r = w.shape[0]

    def body(ga_ref, w_ref, m_ref, v_ref, g_out, d_out, m_out, v_out):
        g = ga_ref[0]
        for dev in range(1, N_DEV):
            g = g + ga_ref[dev]
        g_out[...] = g
        d_out[...], m_out[...], v_out[...] = _adamw_math(w_ref[...], g, m_ref[...], v_ref[...])

    return pl.pallas_call(
        body, name="small_update", out_shape=[jax.ShapeDtypeStruct((r, LANE), F32)] * 4,
    )(gathered, w, m, v)


def _pack_small(vals):
    rows = []
    for name, (r, n) in SMALL:
        flat = vals[name].reshape(-1)
        pad = (-flat.shape[0]) % LANE
        rows.append(jnp.pad(flat, (0, pad)).reshape(-1, LANE))
    return jnp.concatenate(rows, axis=0)


def _unpack_small(packed):
    out, off = {}, 0
    for name, (r, n) in SMALL:
        nrow = (r * n + LANE - 1) // LANE
        out[name] = packed[off:off + nrow].reshape(-1)[:r * n].reshape(r, n)
        off += nrow
    return out


def kernel(x, positions, ffn1_norm, ffn1_w_in, ffn1_w_out, mix_norm, w_in, hg_lb_table, hg_out_norm, w_hg_branch, mla_q_lora_norm, w_q_up, mla_kv_lora_norm, w_kv_up, q_head_norm, k_head_norm, w_mla_branch, w_merge, b_merge, w_out, ffn2_norm, ffn2_w_in, ffn2_w_out, final_norm, loss_target, m_ffn1_norm, m_ffn1_w_in, m_ffn1_w_out, m_mix_norm, m_w_in, m_hg_lb_table, m_hg_out_norm, m_w_hg_branch, m_mla_q_lora_norm, m_w_q_up, m_mla_kv_lora_norm, m_w_kv_up, m_q_head_norm, m_k_head_norm, m_w_mla_branch, m_w_merge, m_b_merge, m_w_out, m_ffn2_norm, m_ffn2_w_in, m_ffn2_w_out, m_final_norm, v_ffn1_norm, v_ffn1_w_in, v_ffn1_w_out, v_mix_norm, v_w_in, v_hg_lb_table, v_hg_out_norm, v_w_hg_branch, v_mla_q_lora_norm, v_w_q_up, v_mla_kv_lora_norm, v_w_kv_up, v_q_head_norm, v_k_head_norm, v_w_mla_branch, v_w_merge, v_b_merge, v_w_out, v_ffn2_norm, v_ffn2_w_in, v_ffn2_w_out, v_final_norm):
    args = dict(locals())
    t = x.shape[1]
    big_w = {n: args[n][0] for n, _, _ in BIG}
    small = {n: args[n].reshape(shape) for n, shape in SMALL}

    names = [n for n, _, _ in BIG]
    full = dict(zip(names, _all_gather([big_w[n].astype(BF16) for n in names], "weights_all_gather")))

    inv_freq = ROPE_THETA ** (-jnp.arange(0, ROPE, 2, dtype=F32) / ROPE)
    ang = positions[0].astype(F32)[:, None] * inv_freq
    cs = jnp.concatenate([jnp.cos(ang), jnp.cos(ang)], axis=1)
    sn = jnp.concatenate([jnp.sin(ang), jnp.sin(ang)], axis=1)

    loss_row, grad_x, g = _local_step(x[0], loss_target[0], cs, sn, full, small)

    c = lax.axis_index("c")
    from_sibling = _sibling_swap([g[n] for n in names], "grads_sibling_swap")
    chip_sums = [_chip_sum(g[n], r1, c, "chip_sum_" + n) for n, r1 in zip(names, from_sibling)]
    exchanged = dict(zip(names, _chip_exchange(chip_sums, "grads_chip_exchange")))

    small_packed = jnp.concatenate([_pack_small(g), loss_row], axis=0)
    pad = (-small_packed.shape[0]) % 8
    small_packed = jnp.pad(small_packed, ((0, pad), (0, 0)))
    small_all = _all_gather([small_packed], "small_all_gather")[0]
    zero_tail = jnp.zeros((small_packed.shape[0] - _pack_small(small).shape[0], LANE), F32)
    pk = lambda d: jnp.concatenate([_pack_small(d), zero_tail], axis=0)
    sg, sd, sm, sv = _small_update(
        small_all, pk(small), pk({n: args["m_" + n].reshape(shape) for n, shape in SMALL}),
        pk({n: args["v_" + n].reshape(shape) for n, shape in SMALL}))
    n_small_rows = _pack_small(small).shape[0]
    loss = sg[n_small_rows, 0]
    outs = {k_: _unpack_small(a) for k_, a in (("grad", sg), ("delta", sd), ("new_m", sm), ("new_v", sv))}

    for n in names:
        outs["grad"][n], outs["delta"][n], outs["new_m"][n], outs["new_v"][n] = _sum_adamw(
            exchanged[n], big_w[n], args["m_" + n][0], args["v_" + n][0], "adamw_" + n)

    def shaped(kind, n):
        return outs[kind][n].reshape(args[n].shape)

    return (loss, grad_x[None], *[shaped("grad", n) for n in WEIGHT_ORDER], *[shaped("delta", n) for n in WEIGHT_ORDER],
            *[shaped("new_m", n) for n in WEIGHT_ORDER], *[shaped("new_v", n) for n in WEIGHT_ORDER])
```

```python
import functools

import jax
import jax.numpy as jnp
from jax import lax
from jax.experimental import pallas as pl
from jax.experimental.pallas import tpu as pltpu

F32 = jnp.float32
BF16 = jnp.bfloat16

D = 1024
FF = 2816
NH = 8
HD = 128
ROPE = 64
QK = HD + ROPE
QL = 384
KVL = 256
HGW = NH * HD
CHUNK = 64
EPS = 1e-6
ROPE_THETA = 10000.0
SCALE = QK ** -0.5

LR, B1, B2, AEPS, WD, STEP = 0.001, 0.9, 0.999, 1e-08, 0.01, 10

HB = 128
SUB = 16
EXP_CLAMP = 80.0
ATT_TILES = (512, 256, 128)
ROW_TILES = (1024, 512, 256, 128)

LANE = 128
VMEM_LIMIT = 56 << 20

N_DEV = 8
MESH = pl.DeviceIdType.MESH

BIG = (
    ("ffn1_w_in", (D, 2 * FF), 1), ("ffn1_w_out", (FF, D), 0), ("w_in", (D, 4800), 1),
    ("w_hg_branch", (HGW, D), 0), ("w_q_up", (QL, NH * QK), 1), ("w_kv_up", (KVL, NH * 2 * HD), 1),
    ("w_mla_branch", (NH * HD, D), 0), ("w_merge", (D, 2 * D), 1), ("w_out", (D, D), 0),
    ("ffn2_w_in", (D, 2 * FF), 1), ("ffn2_w_out", (FF, D), 0),
)
SMALL = (
    ("ffn1_norm", (1, D)), ("mix_norm", (1, D)), ("hg_lb_table", (2, HGW)), ("hg_out_norm", (1, HD)),
    ("mla_q_lora_norm", (1, QL)), ("mla_kv_lora_norm", (1, KVL)), ("q_head_norm", (1, QK)),
    ("k_head_norm", (1, QK)), ("b_merge", (1, 2 * D)), ("ffn2_norm", (1, D)), ("final_norm", (1, D)),
)
WEIGHT_ORDER = ("ffn1_norm", "ffn1_w_in", "ffn1_w_out", "mix_norm", "w_in", "hg_lb_table", "hg_out_norm",
                "w_hg_branch", "mla_q_lora_norm", "w_q_up", "mla_kv_lora_norm", "w_kv_up", "q_head_norm",
                "k_head_norm", "w_mla_branch", "w_merge", "b_merge", "w_out", "ffn2_norm", "ffn2_w_in",
                "ffn2_w_out", "final_norm")


def _pick(n, cands):
    for c in cands:
        if n % c == 0:
            return c
    return n


def _params(sem):
    return pltpu.CompilerParams(dimension_semantics=sem, vmem_limit_bytes=VMEM_LIMIT)


def _sig(x):
    return 1.0 / (1.0 + jnp.exp(-x))


def _dot(a, b):
    return jnp.dot(a.astype(BF16), b.astype(BF16), preferred_element_type=F32)


def _dot_nt(a, b):
    return lax.dot_general(a.astype(BF16), b.astype(BF16), (((1,), (1,)), ((), ())),
                           preferred_element_type=F32)


def _dot_tn(a, b):
    return lax.dot_general(a.astype(BF16), b.astype(BF16), (((0,), (0,)), ((), ())),
                           preferred_element_type=F32)


def _split3(x):
    x1 = x.astype(BF16)
    r1 = x - x1.astype(F32)
    x2 = r1.astype(BF16)
    x3 = (r1 - x2.astype(F32)).astype(BF16)
    return x1, x2, x3


def _dot_sel(m, x):
    x1, x2, x3 = _split3(x)
    d = lambda p: jnp.dot(m, p, preferred_element_type=F32)
    return d(x1) + d(x2) + d(x3)


def _sel_dot(x, m):
    x1, x2, x3 = _split3(x)
    d = lambda p: jnp.dot(p, m, preferred_element_type=F32)
    return d(x1) + d(x2) + d(x3)


_TN = (1408, 1024, 768, 512, 384, 256, 128)


def _mm(a, b, mode, name, out_dtype=F32):
    if mode == "tn":
        t, m = a.shape
        n = b.shape[1]
        tt, tm, tn = _pick(t, (512, 256, 128)), _pick(m, _TN), _pick(n, _TN)

        def body(a_ref, b_ref, o_ref):
            @pl.when(pl.program_id(2) == 0)
            def _():
                o_ref[...] = jnp.zeros_like(o_ref)

            o_ref[...] += _dot_tn(a_ref[...], b_ref[...])

        return pl.pallas_call(
            body, name=name, grid=(m // tm, n // tn, t // tt),
            in_specs=[pl.BlockSpec((tt, tm), lambda i, j, k: (k, i)),
                      pl.BlockSpec((tt, tn), lambda i, j, k: (k, j))],
            out_specs=pl.BlockSpec((tm, tn), lambda i, j, k: (i, j)),
            out_shape=jax.ShapeDtypeStruct((m, n), F32),
            compiler_params=_params(("parallel", "parallel", "arbitrary")),
        )(a, b)

    m, k = a.shape
    tm = _pick(m, ROW_TILES)
    if mode == "nn":
        n = b.shape[1]
        tn = _pick(n, _TN)
        b_spec = pl.BlockSpec((k, tn), lambda i, j: (0, j))
        dot = _dot
    else:
        n = b.shape[0]
        tn = _pick(n, _TN if k <= 4096 else (512, 256, 128))
        b_spec = pl.BlockSpec((tn, k), lambda i, j: (j, 0))
        dot = _dot_nt

    def body(a_ref, b_ref, o_ref):
        o_ref[...] = dot(a_ref[...], b_ref[...]).astype(o_ref.dtype)

    return pl.pallas_call(
        body, name=name, grid=(m // tm, n // tn),
        in_specs=[pl.BlockSpec((tm, k), lambda i, j: (i, 0)), b_spec],
        out_specs=pl.BlockSpec((tm, tn), lambda i, j: (i, j)),
        out_shape=jax.ShapeDtypeStruct((m, n), out_dtype),
        compiler_params=_params(("parallel", "parallel")),
    )(a, b)


_DOTS = {"nn": _dot, "nt": _dot_nt, "tn": _dot_tn}
_BS = pl.BlockSpec


def _mmcall(name, kind, a, b, a_spec, b_spec, o_spec, o_shape, grid, red_axis=None, out_dtype=F32):
    dot = _DOTS[kind]

    def body(a_ref, b_ref, o_ref):
        if red_axis is None:
            o_ref[...] = dot(a_ref[...], b_ref[...]).astype(o_ref.dtype)
        else:
            @pl.when(pl.program_id(red_axis) == 0)
            def _():
                o_ref[...] = jnp.zeros_like(o_ref)

            o_ref[...] += dot(a_ref[...], b_ref[...])

    sem = tuple("arbitrary" if ax == red_axis else "parallel" for ax in range(len(grid)))
    return pl.pallas_call(
        body, name=name, grid=grid, in_specs=[a_spec, b_spec], out_specs=o_spec,
        out_shape=jax.ShapeDtypeStruct(o_shape, out_dtype), compiler_params=_params(sem),
    )(a, b)


def _mm_stack_red(a, w, name, kind):
    s, t, n = a.shape
    nout = w.shape[2] if kind == "nn" else w.shape[1]
    tm = _pick(t, ROW_TILES)
    return _mmcall(name, kind, a, w, _BS((None, tm, n), lambda i, j: (j, i, 0)),
                   _BS((None,) + w.shape[1:], lambda i, j: (j, 0, 0)), _BS((tm, nout), lambda i, j: (i, 0)),
                   (t, nout), (t // tm, s), red_axis=1)


def _mm_stack_tn(a, b, name):
    if a.ndim == 2:
        t, k = a.shape
        s, _, n = b.shape
        a_spec, b_spec, o_shape = (lambda tt: _BS((tt, k), lambda j, r: (r, 0))), \
            (lambda tt: _BS((None, tt, n), lambda j, r: (j, r, 0))), (s, k, n)
    else:
        s, t, k = a.shape
        n = b.shape[1]
        a_spec, b_spec, o_shape = (lambda tt: _BS((None, tt, k), lambda j, r: (j, r, 0))), \
            (lambda tt: _BS((tt, n), lambda j, r: (r, 0))), (s, k, n)
    tt = _pick(t, (512, 256, 128))
    return _mmcall(name, "tn", a, b, a_spec(tt), b_spec(tt), _BS((None,) + o_shape[1:], lambda j, r: (j, 0, 0)),
                   o_shape, (s, t // tt), red_axis=1)


def _rows(fn, name, t, tm, ins, vecs, outs, accs=()):
    n_in, n_out, n_acc = len(ins) + len(vecs), len(outs), len(accs)

    def body(*refs):
        res = fn(*refs[:n_in + n_out])
        if n_acc:
            acc_refs = refs[n_in + n_out:]

            @pl.when(pl.program_id(0) == 0)
            def _():
                for r in acc_refs:
                    r[...] = jnp.zeros_like(r)

            for r, val in zip(acc_refs, res):
                r[...] += val

    in_specs = [pl.BlockSpec((tm, bw), functools.partial(lambda i, cb: (i, cb), cb=cb)) for _, bw, cb in ins]
    in_specs += [pl.BlockSpec(v.shape, lambda i: (0, 0)) for v in vecs]
    out_specs = [pl.BlockSpec((tm, w), lambda i: (i, 0)) for w, _ in outs]
    out_specs += [pl.BlockSpec(s, lambda i: (0, 0)) for s in accs]
    out_shape = [jax.ShapeDtypeStruct((t, w), dt) for w, dt in outs]
    out_shape += [jax.ShapeDtypeStruct(s, F32) for s in accs]
    return pl.pallas_call(
        body, name=name, grid=(t // tm,), in_specs=in_specs, out_specs=out_specs, out_shape=out_shape,
        compiler_params=_params(("arbitrary",) if n_acc else ("parallel",)),
    )(*[a for a, _, _ in ins], *vecs)


def _rms(x, g):
    return x * lax.rsqrt(jnp.mean(x * x, axis=-1, keepdims=True) + EPS) * g


def _rms_bwd(x, g, dy):
    xh = x * lax.rsqrt(jnp.mean(x * x, axis=-1, keepdims=True) + EPS)
    r = lax.rsqrt(jnp.mean(x * x, axis=-1, keepdims=True) + EPS)
    dyg = dy * g
    dx = r * (dyg - xh * jnp.mean(dyg * xh, axis=-1, keepdims=True))
    return dx, jnp.sum(dy * xh, axis=0, keepdims=True)


def _hgrn_mats():
    row = lax.broadcasted_iota(jnp.int32, (HB, HB), 0)
    col = lax.broadcasted_iota(jnp.int32, (HB, HB), 1)
    return row, col


def _hgrn_gates(qr, z, t0, t1):
    lb = 1.0 / (1.0 + jnp.exp(t1 - t0))
    sz = _sig(z)
    sneg = 1.0 / (1.0 + jnp.exp(z))
    f = lb + (1.0 - lb) * sz
    return lb, sz, sneg, f, jnp.log(f), (1.0 - lb) * sneg, qr * _sig(qr)


def _hgrn_scores(q, k, cum, cmid):
    qd = q * jnp.exp(jnp.minimum(cmid, EXP_CLAMP))
    qd_b = qd.astype(BF16)
    kds, parts = [], []
    for i in range(HB // SUB):
        mid = cum[SUB * i + SUB // 2 - 1:SUB * i + SUB // 2, :]
        kd = k * jnp.exp(jnp.minimum(mid - cum, EXP_CLAMP))
        kds.append(kd)
        parts.append(_dot_nt(qd_b[SUB * i:SUB * (i + 1)], kd))
    return qd, kds, jnp.concatenate(parts, axis=0)


def _hgrn_fwd(p_hg, table, gain, t):
    nblk = t // HB

    def body(q_ref, f_ref, i_ref, g_ref, tab_ref, gain_ref, o_ref, y_ref, st_ref, state):
        @pl.when(pl.program_id(0) == 0)
        def _():
            state[...] = jnp.zeros_like(state)

        row, col = _hgrn_mats()
        causal = col <= row
        tri = causal.astype(BF16)
        trimid = tri - (col <= (row // SUB) * SUB + SUB // 2 - 1).astype(BF16)
        for h in range(NH):
            sl = slice(HD * h, HD * (h + 1))
            v, gr = i_ref[:, sl], g_ref[:, sl]
            _, _, _, _, lf, k, q = _hgrn_gates(q_ref[:, sl], f_ref[:, sl], tab_ref[0:1, sl], tab_ref[1:2, sl])
            cum = _dot_sel(tri, lf)
            cmid = _dot_sel(trimid, lf)
            _, _, s = _hgrn_scores(q, k, cum, cmid)
            p = jnp.where(causal, s, 0.0)
            st = state[h]
            st_ref[h, 0] = st
            o = _dot(p, v) + _dot_nt(q * jnp.exp(cum), st)
            last = cum[HB - 1:HB, :]
            state[h] = st * jnp.exp(last) + _dot_tn(v, k * jnp.exp(last - cum))
            o_ref[:, sl] = o
            y_ref[:, sl] = (_rms(o, gain_ref[...]) * gr * _sig(gr)).astype(BF16)

    blk = lambda cb: pl.BlockSpec((HB, HGW), functools.partial(lambda n, cb: (n, cb), cb=cb))
    return pl.pallas_call(
        body, name="hgrn_fwd", grid=(nblk,),
        in_specs=[blk(0), blk(1), blk(2), blk(3), pl.BlockSpec((2, HGW), lambda n: (0, 0)),
                  pl.BlockSpec((1, HD), lambda n: (0, 0))],
        out_specs=[pl.BlockSpec((HB, HGW), lambda n: (n, 0)), pl.BlockSpec((HB, HGW), lambda n: (n, 0)),
                   pl.BlockSpec((NH, 1, HD, HD), lambda n: (0, n, 0, 0))],
        out_shape=[jax.ShapeDtypeStruct((t, HGW), F32), jax.ShapeDtypeStruct((t, HGW), BF16),
                   jax.ShapeDtypeStruct((NH, nblk, HD, HD), F32)],
        scratch_shapes=[pltpu.VMEM((NH, HD, HD), F32)],
        compiler_params=_params(("arbitrary",)),
    )(p_hg, p_hg, p_hg, p_hg, table, gain)


def _hgrn_bwd(p_hg, table, gain, o_pre, states, dy, t):
    nblk = t // HB

    def body(q_ref, f_ref, i_ref, g_ref, tab_ref, gain_ref, o_ref, st_ref, dy_ref, dp_ref, dtab_ref, dgain_ref,
             dstate):
        @pl.when(pl.program_id(0) == 0)
        def _():
            dstate[...] = jnp.zeros_like(dstate)
            dtab_ref[...] = jnp.zeros_like(dtab_ref)
            dgain_ref[...] = jnp.zeros_like(dgain_ref)

        row, col = _hgrn_mats()
        causal = col <= row
        tri = causal.astype(BF16)
        midrow = (row // SUB) * SUB + SUB // 2 - 1
        trimid = tri - (col <= midrow).astype(BF16)
        tri_t = (row <= col).astype(BF16)
        midcol = (col // SUB) * SUB + SUB // 2 - 1
        trimid_t = tri_t - (row <= midcol).astype(BF16)
        dgain = jnp.zeros((1, HD), F32)
        for h in range(NH):
            sl = slice(HD * h, HD * (h + 1))
            qr, z, v, gr = q_ref[:, sl], f_ref[:, sl], i_ref[:, sl], g_ref[:, sl]
            lb, sz, sneg, f, lf, k, q = _hgrn_gates(qr, z, tab_ref[0:1, sl], tab_ref[1:2, sl])
            cum = _dot_sel(tri, lf)
            cmid = _dot_sel(trimid, lf)
            qd, kds, s = _hgrn_scores(q, k, cum, cmid)
            p = jnp.where(causal, s, 0.0)
            st = st_ref[h, 0]
            dst = dstate[h]
            o = o_ref[:, sl]
            sg = _sig(gr)
            dyh = dy_ref[:, sl]
            on = _rms(o, gain_ref[...])
            dgr = dyh * on * sg * (1.0 + gr * (1.0 - sg))
            do, dg_h = _rms_bwd(o, gain_ref[...], dyh * gr * sg)
            dgain = dgain + dg_h
            do_b = do.astype(BF16)
            ecum = jnp.exp(cum)
            qc = q * ecum
            last = cum[HB - 1:HB, :]
            edec = jnp.exp(last - cum)
            kdec = k * edec
            dp = jnp.where(causal, _dot_nt(do_b, v), 0.0)
            dv = _dot(p.T, do_b) + _dot_nt(kdec, dst)
            dqc = _dot(do_b, st)
            dkdec = _dot(v, dst)
            dstate[h] = dst * jnp.exp(last) + _dot(do.T, qc)
            dp_b = dp.astype(BF16)
            dqd = jnp.concatenate([_dot(dp_b[SUB * i:SUB * (i + 1)], kds[i]) for i in range(HB // SUB)], axis=0)
            gq = dqd * qd
            dq = dqd * jnp.exp(jnp.minimum(cmid, EXP_CLAMP)) + dqc * ecum
            gs = dkdec * kdec
            dk = dkdec * edec
            dcum = dqc * qc - gs
            dcum = dcum + jnp.where(row == HB - 1, jnp.sum(gs, axis=0, keepdims=True)
                                    + jnp.exp(last) * jnp.sum(st * dst, axis=0, keepdims=True), 0.0)
            qd_b = qd.astype(BF16)
            for i in range(HB // SUB):
                dkd = _dot_tn(dp_b[SUB * i:SUB * (i + 1)], qd_b[SUB * i:SUB * (i + 1)])
                mid = cum[SUB * i + SUB // 2 - 1:SUB * i + SUB // 2, :]
                dk = dk + dkd * jnp.exp(jnp.minimum(mid - cum, EXP_CLAMP))
                gk = dkd * kds[i]
                dcum = dcum - gk + jnp.where(row == SUB * i + SUB // 2 - 1, jnp.sum(gk, axis=0, keepdims=True), 0.0)
            dlf = _dot_sel(tri_t, dcum) + _dot_sel(trimid_t, gq)
            df = dlf / f - dk
            dz = df * (1.0 - lb) * sz * sneg
            dlb = jnp.sum(df * sneg, axis=0, keepdims=True) * lb * (1.0 - lb)
            dtab_ref[0:1, sl] += dlb
            dtab_ref[1:2, sl] -= dlb
            sq = _sig(qr)
            dp_ref[:, sl] = (dq * sq * (1.0 + qr * (1.0 - sq))).astype(BF16)
            dp_ref[:, HGW + HD * h:HGW + HD * (h + 1)] = dz.astype(BF16)
            dp_ref[:, 2 * HGW + HD * h:2 * HGW + HD * (h + 1)] = dv.astype(BF16)
            dp_ref[:, 3 * HGW + HD * h:3 * HGW + HD * (h + 1)] = dgr.astype(BF16)
        dgain_ref[...] += dgain

    rev = lambda cb: pl.BlockSpec((HB, HGW), functools.partial(lambda n, cb: (nblk - 1 - n, cb), cb=cb))
    return pl.pallas_call(
        body, name="hgrn_bwd", grid=(nblk,),
        in_specs=[rev(0), rev(1), rev(2), rev(3), pl.BlockSpec((2, HGW), lambda n: (0, 0)),
                  pl.BlockSpec((1, HD), lambda n: (0, 0)), rev(0),
                  pl.BlockSpec((NH, 1, HD, HD), lambda n: (0, nblk - 1 - n, 0, 0)), rev(0)],
        out_specs=[pl.BlockSpec((HB, 4 * HGW), lambda n: (nblk - 1 - n, 0)),
                   pl.BlockSpec((2, HGW), lambda n: (0, 0)), pl.BlockSpec((1, HD), lambda n: (0, 0))],
        out_shape=[jax.ShapeDtypeStruct((t, 4 * HGW), BF16), jax.ShapeDtypeStruct((2, HGW), F32),
                   jax.ShapeDtypeStruct((1, HD), F32)],
        scratch_shapes=[pltpu.VMEM((NH, HD, HD), F32)],
        compiler_params=_params(("arbitrary",)),
    )(p_hg, p_hg, p_hg, p_hg, table, gain, o_pre, states, dy)


def _rope_mat():
    r = lax.broadcasted_iota(jnp.int32, (ROPE, ROPE), 0)
    c = lax.broadcasted_iota(jnp.int32, (ROPE, ROPE), 1)
    half = ROPE // 2
    return ((r == c - half).astype(F32) - (r == c + half).astype(F32)).astype(BF16)


def _mla_prep_fwd(p_mla, cs, sn, wq, wkv, gql, gkvl, gq, gk, t):
    tm = _pick(t, (256, 128))

    def body(p_ref, cs_ref, sn_ref, wq_ref, wkv_ref, gql_ref, gkvl_ref, gq_ref, gk_ref,
             q_ref, k_ref, v_ref):
        rmat = _rope_mat()
        cqn = _rms(p_ref[:, 0:QL], gql_ref[...]).astype(BF16)
        ckvn = _rms(p_ref[:, QL:QL + KVL], gkvl_ref[...]).astype(BF16)
        kpe = p_ref[:, QL + KVL:QL + KVL + ROPE]
        c, s = cs_ref[...], sn_ref[...]
        rot = lambda x: x * c + _sel_dot(x, rmat) * s
        for h in range(NH):
            qa, qr = _dot(cqn, wq_ref[h, :, 0:HD]), _dot(cqn, wq_ref[h, :, HD:QK])
            rq = lax.rsqrt((jnp.sum(qa * qa, -1, keepdims=True) + jnp.sum(qr * qr, -1, keepdims=True)) / QK + EPS)
            q_ref[h, :, 0:HD] = (qa * rq * gq_ref[:, 0:HD] * SCALE).astype(BF16)
            q_ref[h, :, HD:QK] = (rot(qr * rq * gq_ref[:, HD:QK]) * SCALE).astype(BF16)
            kn = _dot(ckvn, wkv_ref[h, :, 0:HD])
            rk = lax.rsqrt((jnp.sum(kn * kn, -1, keepdims=True) + jnp.sum(kpe * kpe, -1, keepdims=True)) / QK + EPS)
            k_ref[h, :, 0:HD] = (kn * rk * gk_ref[:, 0:HD]).astype(BF16)
            k_ref[h, :, HD:QK] = rot(kpe * rk * gk_ref[:, HD:QK]).astype(BF16)
            v_ref[h] = _dot(ckvn, wkv_ref[h, :, HD:2 * HD]).astype(BF16)

    whole = lambda a: pl.BlockSpec(a.shape, functools.partial(lambda i, nd: (0,) * nd, nd=a.ndim))
    return pl.pallas_call(
        body, name="mla_prep_fwd", grid=(t // tm,),
        in_specs=[pl.BlockSpec((tm, QL + KVL + ROPE), lambda i: (i, 0)), pl.BlockSpec((tm, ROPE), lambda i: (i, 0)),
                  pl.BlockSpec((tm, ROPE), lambda i: (i, 0))] + [whole(a) for a in (wq, wkv, gql, gkvl, gq, gk)],
        out_specs=[pl.BlockSpec((NH, tm, QK), lambda i: (0, i, 0)), pl.BlockSpec((NH, tm, QK), lambda i: (0, i, 0)),
                   pl.BlockSpec((NH, tm, HD), lambda i: (0, i, 0))],
        out_shape=[jax.ShapeDtypeStruct((NH, t, QK), BF16), jax.ShapeDtypeStruct((NH, t, QK), BF16),
                   jax.ShapeDtypeStruct((NH, t, HD), BF16)],
        compiler_params=_params(("parallel",)),
    )(p_mla, cs, sn, wq, wkv, gql, gkvl, gq, gk)


def _mla_prep_bwd(p_mla, cs, sn, wq, wkv, gql, gkvl, gq, gk, dq, dk, dv, t):
    tm = _pick(t, (256, 128))

    def body(p_ref, cs_ref, sn_ref, wq_ref, wkv_ref, gql_ref, gkvl_ref, gq_ref, gk_ref,
             dq_ref, dk_ref, dv_ref,
             dp_ref, dwq_ref, dwkv_ref, dgql_ref, dgkvl_ref, dgq_ref, dgk_ref):
        accs = (dwq_ref, dwkv_ref, dgql_ref, dgkvl_ref, dgq_ref, dgk_ref)

        @pl.when(pl.program_id(0) == 0)
        def _():
            for r in accs:
                r[...] = jnp.zeros_like(r)

        rmat = _rope_mat()
        rmat_t = -rmat
        cq, ckv = p_ref[:, 0:QL], p_ref[:, QL:QL + KVL]
        kpe = p_ref[:, QL + KVL:QL + KVL + ROPE]
        cqn_f, ckvn_f = _rms(cq, gql_ref[...]), _rms(ckv, gkvl_ref[...])
        cqn, ckvn = cqn_f.astype(BF16), ckvn_f.astype(BF16)
        cqn_t, ckvn_t = cqn_f.T.astype(BF16), ckvn_f.T.astype(BF16)
        c, s = cs_ref[...], sn_ref[...]
        unrot = lambda dy: dy * c + _sel_dot(dy * s, rmat_t)
        dcqn = jnp.zeros((tm, QL), F32)
        dckvn = jnp.zeros((tm, KVL), F32)
        dkpe = jnp.zeros((tm, ROPE), F32)
        dgq_a, dgq_r = jnp.zeros((1, HD), F32), jnp.zeros((1, ROPE), F32)
        dgk_a, dgk_r = jnp.zeros((1, HD), F32), jnp.zeros((1, ROPE), F32)
        for h in range(NH):
            qa, qr = _dot(cqn, wq_ref[h, :, 0:HD]), _dot(cqn, wq_ref[h, :, HD:QK])
            rq = lax.rsqrt((jnp.sum(qa * qa, -1, keepdims=True) + jnp.sum(qr * qr, -1, keepdims=True)) / QK + EPS)
            xa, xr = qa * rq, qr * rq
            dya = dq_ref[h, :, 0:HD] * SCALE
            dyr = unrot(dq_ref[h, :, HD:QK] * SCALE)
            dgq_a += jnp.sum(dya * xa, axis=0, keepdims=True)
            dgq_r += jnp.sum(dyr * xr, axis=0, keepdims=True)
            ga, gr_ = dya * gq_ref[:, 0:HD], dyr * gq_ref[:, HD:QK]
            mean = (jnp.sum(ga * xa, -1, keepdims=True) + jnp.sum(gr_ * xr, -1, keepdims=True)) / QK
            dqa = (rq * (ga - xa * mean)).astype(BF16)
            dqr = (rq * (gr_ - xr * mean)).astype(BF16)
            dwq_ref[h, :, 0:HD] += _dot(cqn_t, dqa)
            dwq_ref[h, :, HD:QK] += _dot(cqn_t, dqr)
            dcqn += _dot_nt(dqa, wq_ref[h, :, 0:HD]) + _dot_nt(dqr, wq_ref[h, :, HD:QK])
            kn = _dot(ckvn, wkv_ref[h, :, 0:HD])
            rk = lax.rsqrt((jnp.sum(kn * kn, -1, keepdims=True) + jnp.sum(kpe * kpe, -1, keepdims=True)) / QK + EPS)
            ya, yr = kn * rk, kpe * rk
            dka = dk_ref[h, :, 0:HD]
            dkr = unrot(dk_ref[h, :, HD:QK])
            dgk_a += jnp.sum(dka * ya, axis=0, keepdims=True)
            dgk_r += jnp.sum(dkr * yr, axis=0, keepdims=True)
            ha, hr = dka * gk_ref[:, 0:HD], dkr * gk_ref[:, HD:QK]
            mean = (jnp.sum(ha * ya, -1, keepdims=True) + jnp.sum(hr * yr, -1, keepdims=True)) / QK
            dkn = (rk * (ha - ya * mean)).astype(BF16)
            dkpe += rk * (hr - yr * mean)
            dvh = dv_ref[h].astype(BF16)
            dwkv_ref[h, :, 0:HD] += _dot(ckvn_t, dkn)
            dwkv_ref[h, :, HD:2 * HD] += _dot(ckvn_t, dvh)
            dckvn += _dot_nt(dkn, wkv_ref[h, :, 0:HD]) + _dot_nt(dvh, wkv_ref[h, :, HD:2 * HD])
        dcq, dg1 = _rms_bwd(cq, gql_ref[...], dcqn)
        dckv, dg2 = _rms_bwd(ckv, gkvl_ref[...], dckvn)
        dp_ref[:, 0:QL] = dcq.astype(BF16)
        dp_ref[:, QL:QL + KVL] = dckv.astype(BF16)
        dp_ref[:, QL + KVL:QL + KVL + ROPE] = dkpe.astype(BF16)
        dgql_ref[...] += dg1
        dgkvl_ref[...] += dg2
        dgq_ref[:, 0:HD] += dgq_a
        dgq_ref[:, HD:QK] += dgq_r
        dgk_ref[:, 0:HD] += dgk_a
        dgk_ref[:, HD:QK] += dgk_r

    whole = lambda a: pl.BlockSpec(a.shape, functools.partial(lambda i, nd: (0,) * nd, nd=a.ndim))
    acc_shapes = [wq.shape, wkv.shape, gql.shape, gkvl.shape, gq.shape, gk.shape]
    return pl.pallas_call(
        body, name="mla_prep_bwd", grid=(t // tm,),
        in_specs=[pl.BlockSpec((tm, QL + KVL + ROPE), lambda i: (i, 0)), pl.BlockSpec((tm, ROPE), lambda i: (i, 0)),
                  pl.BlockSpec((tm, ROPE), lambda i: (i, 0))]
        + [whole(a) for a in (wq, wkv, gql, gkvl, gq, gk)]
        + [pl.BlockSpec((NH, tm, QK), lambda i: (0, i, 0)), pl.BlockSpec((NH, tm, QK), lambda i: (0, i, 0)),
           pl.BlockSpec((NH, tm, HD), lambda i: (0, i, 0))],
        out_specs=[pl.BlockSpec((tm, QL + KVL + ROPE), lambda i: (i, 0))]
        + [pl.BlockSpec(s, functools.partial(lambda i, nd: (0,) * nd, nd=len(s))) for s in acc_shapes],
        out_shape=[jax.ShapeDtypeStruct((t, QL + KVL + ROPE), BF16)]
        + [jax.ShapeDtypeStruct(s, F32) for s in acc_shapes],
        compiler_params=_params(("arbitrary",)),
    )(p_mla, cs, sn, wq, wkv, gql, gkvl, gq, gk, dq, dk, dv)


def _diag_mask(tile, transposed):
    r = lax.broadcasted_iota(jnp.int32, (tile, tile), 0) // CHUNK
    c = lax.broadcasted_iota(jnp.int32, (tile, tile), 1) // CHUNK
    return (r <= c) if transposed else (c <= r)


def _flash_fwd(q, k, v, t):
    tq = _pick(t, ATT_TILES)

    def body(q_ref, k_ref, v_ref, o_ref, lse_ref):
        i = pl.program_id(1)
        qt = q_ref[0]

        def step(j, carry, masked):
            m, l, acc = carry
            kt = k_ref[0, pl.ds(pl.multiple_of(j * tq, tq), tq), :]
            vt = v_ref[0, pl.ds(pl.multiple_of(j * tq, tq), tq), :]
            s = _dot_nt(qt, kt)
            if masked:
                s = jnp.where(_diag_mask(tq, False), s, -jnp.inf)
            m_new = jnp.maximum(m, jnp.max(s, axis=-1, keepdims=True))
            p = jnp.exp(s - m_new)
            alpha = jnp.exp(m - m_new)
            return m_new, alpha * l + jnp.sum(p, axis=-1, keepdims=True), alpha * acc + _dot(p, vt)

        init = (jnp.full((tq, 1), -jnp.inf, F32), jnp.zeros((tq, 1), F32), jnp.zeros((tq, HD), F32))
        carry = lax.fori_loop(0, i, lambda j, cr: step(j, cr, False), init)
        m, l, acc = step(i, carry, True)
        o_ref[...] = (acc / l).astype(BF16)
        lse_ref[0] = m + jnp.log(l)

    return pl.pallas_call(
        body, name="flash_fwd", grid=(NH, t // tq),
        in_specs=[pl.BlockSpec((1, tq, QK), lambda h, i: (h, i, 0)), pl.BlockSpec((1, t, QK), lambda h, i: (h, 0, 0)),
                  pl.BlockSpec((1, t, HD), lambda h, i: (h, 0, 0))],
        out_specs=[pl.BlockSpec((tq, HD), lambda h, i: (i, h)), pl.BlockSpec((1, tq, 1), lambda h, i: (h, i, 0))],
        out_shape=[jax.ShapeDtypeStruct((t, NH * HD), BF16), jax.ShapeDtypeStruct((NH, t, 1), F32)],
        compiler_params=_params(("parallel", "parallel")),
    )(q, k, v)


def _flash_delta(o, do, t):
    tq = _pick(t, ATT_TILES)

    def body(o_ref, do_ref, delta_ref):
        prod = do_ref[...].astype(F32) * o_ref[...].astype(F32)
        x1, x2, x3 = _split3(prod)
        ones = jnp.ones((8, HD), BF16)
        d = lambda p: lax.dot_general(ones, p, (((1,), (1,)), ((), ())), preferred_element_type=F32)
        delta_ref[0] = (d(x1) + d(x2) + d(x3))[0:1, :]

    return pl.pallas_call(
        body, name="flash_delta", grid=(NH, t // tq),
        in_specs=[pl.BlockSpec((tq, HD), lambda h, i: (i, h)), pl.BlockSpec((tq, HD), lambda h, i: (i, h))],
        out_specs=pl.BlockSpec((1, 1, tq), lambda h, i: (h, 0, i)),
        out_shape=jax.ShapeDtypeStruct((NH, 1, t), F32), compiler_params=_params(("parallel", "parallel")),
    )(o, do)


def _flash_bwd(q, k, v, do, lse_row, delta_row, t):
    tk = _pick(t, ATT_TILES)
    nq = t // tk

    def body(q_ref, k_ref, v_ref, do_ref, lse_ref, delta_ref, dq_ref, dk_ref, dv_ref):
        j = pl.program_id(1)

        @pl.when(j == 0)
        def _():
            dq_ref[...] = jnp.zeros_like(dq_ref)

        kt, vt = k_ref[0], v_ref[0]

        def step(i, carry, masked):
            dk, dv = carry
            rows = pl.ds(pl.multiple_of(i * tk, tk), tk)
            qt, dot_ = q_ref[0, rows, :], do_ref[rows, :]
            p = jnp.exp(_dot_nt(kt, qt) - lse_ref[0, :, rows])
            if masked:
                p = jnp.where(_diag_mask(tk, True), p, 0.0)
            ds = (p * (_dot_nt(vt, dot_) - delta_ref[0, :, rows])).astype(BF16)
            dq_ref[0, rows, :] += _dot_tn(ds, kt)
            return dk + _dot(ds, qt), dv + _dot(p, dot_)

        carry = step(j, (jnp.zeros((tk, QK), F32), jnp.zeros((tk, HD), F32)), True)
        dk, dv = lax.fori_loop(j + 1, nq, lambda i, cr: step(i, cr, False), carry)
        dk_ref[0] = dk
        dv_ref[0] = dv

    return pl.pallas_call(
        body, name="flash_bwd", grid=(NH, nq),
        in_specs=[pl.BlockSpec((1, t, QK), lambda h, j: (h, 0, 0)), pl.BlockSpec((1, tk, QK), lambda h, j: (h, j, 0)),
                  pl.BlockSpec((1, tk, HD), lambda h, j: (h, j, 0)), pl.BlockSpec((t, HD), lambda h, j: (0, h)),
                  pl.BlockSpec((1, 1, t), lambda h, j: (h, 0, 0)), pl.BlockSpec((1, 1, t), lambda h, j: (h, 0, 0))],
        out_specs=[pl.BlockSpec((1, t, QK), lambda h, j: (h, 0, 0)), pl.BlockSpec((1, tk, QK), lambda h, j: (h, j, 0)),
                   pl.BlockSpec((1, tk, HD), lambda h, j: (h, j, 0))],
        out_shape=[jax.ShapeDtypeStruct((NH, t, QK), F32), jax.ShapeDtypeStruct((NH, t, QK), F32),
                   jax.ShapeDtypeStruct((NH, t, HD), F32)],
        compiler_params=_params(("parallel", "arbitrary")),
    )(q, k, v, do, lse_row, delta_row)


def _ffn_in(xn, w_in, name):
    t, k = xn.shape
    s, _, n = w_in.shape
    tm = _pick(t, ROW_TILES)

    def body(x_ref, w_ref, gu_ref, a_ref):
        x = x_ref[...]
        g, u = _dot(x, w_ref[0]), _dot(x, w_ref[1])
        gu_ref[0] = g.astype(BF16)
        gu_ref[1] = u.astype(BF16)
        a_ref[...] = (g * _sig(g) * u).astype(BF16)

    return pl.pallas_call(
        body, name=name, grid=(t // tm, s // 2),
        in_specs=[_BS((tm, k), lambda i, j: (i, 0)), _BS((2, None, k, n), lambda i, j: (0, j, 0, 0))],
        out_specs=[_BS((2, None, tm, n), lambda i, j: (0, j, i, 0)), _BS((None, tm, n), lambda i, j: (j, i, 0))],
        out_shape=[jax.ShapeDtypeStruct((2, s // 2, t, n), BF16), jax.ShapeDtypeStruct((s // 2, t, n), BF16)],
        compiler_params=_params(("parallel", "parallel")),
    )(xn, w_in.reshape(2, s // 2, k, n))


def _ffn_dgu(dfo, w_out, gu, name):
    t, k = dfo.shape
    s, n, _ = w_out.shape
    tm = _pick(t, ROW_TILES)

    def body(d_ref, w_ref, gu_ref, o_ref):
        da = _dot_nt(d_ref[...], w_ref[...])
        g, u = gu_ref[0].astype(F32), gu_ref[1].astype(F32)
        sg = _sig(g)
        o_ref[0] = (da * u * sg * (1.0 + g * (1.0 - sg))).astype(BF16)
        o_ref[1] = (da * g * sg).astype(BF16)

    pair = _BS((2, None, tm, n), lambda i, j: (0, j, i, 0))
    return pl.pallas_call(
        body, name=name, grid=(t // tm, s),
        in_specs=[_BS((tm, k), lambda i, j: (i, 0)), _BS((None, n, k), lambda i, j: (j, 0, 0)), pair],
        out_specs=pair, out_shape=jax.ShapeDtypeStruct((2, s, t, n), BF16),
        compiler_params=_params(("parallel", "parallel")),
    )(dfo, w_out, gu).reshape(2 * s, t, n)


def _ffn_fwd(xn, w_in, w_out, tag):
    gu, a = _ffn_in(xn, w_in, tag + "_in")
    return gu, a, _mm_stack_red(a, w_out, tag + "_out", "nn")


def _ffn_bwd(dfo, xn, gu, a, w_in, w_out, tag):
    dw_out = _mm_stack_tn(a, dfo, tag + "_dwout")
    dgu = _ffn_dgu(dfo, w_out, gu, tag + "_dgu")
    return _mm_stack_red(dgu, w_in, tag + "_dxn", "nt"), _mm_stack_tn(xn, dgu, tag + "_dwin"), dw_out


def _local_step(x, target, cs, sn, w, s):
    t = x.shape[0]
    tm = _pick(t, (256, 128))
    tt = _pick(t, (512, 256, 128))
    g = {}
    rows_of = lambda n: w[n].reshape(-1, w[n].shape[2])
    ffn_out = lambda n: w[n].reshape(4, FF // 4, D)
    w_hgb, w_mlab, w_o = rows_of("w_hg_branch"), rows_of("w_mla_branch"), rows_of("w_out")
    w_in_nat = w["w_in"].transpose(1, 0, 2).reshape(D, -1)
    w_mrg = w["w_merge"]
    mw = w_mrg.shape[2]

    def norm_fn(x_ref, g_ref, o_ref):
        o_ref[...] = _rms(x_ref[...], g_ref[...]).astype(BF16)

    xn1 = _rows(norm_fn, "norm1", t, tm, [(x, D, 0)], [s["ffn1_norm"]], [(D, BF16)])[0]
    gu1, a1, f1 = _ffn_fwd(xn1, w["ffn1_w_in"], ffn_out("ffn1_w_out"), "ffn1")

    def res_norm_fn(scale):
        def fn(h_ref, f_ref, g_ref, h_out, n_out):
            h = h_ref[...] + scale * f_ref[...]
            h_out[...] = h
            n_out[...] = _rms(h, g_ref[...]).astype(BF16)
        return fn

    h1, u = _rows(res_norm_fn(0.5), "res_norm1", t, tm, [(x, D, 0), (f1, D, 0)], [s["mix_norm"]],
                  [(D, F32), (D, BF16)])
    w_in_hg, w_in_mla = w_in_nat[:, :4 * HGW], w_in_nat[:, 4 * HGW:]
    p_hg = _mm(u, w_in_hg, "nn", "proj_hg")
    p_mla = _mm(u, w_in_mla, "nn", "proj_mla")
    gpre = _mmcall("proj_gate", "nn", u, w_mrg, _BS((tt, D), lambda i, j: (i, 0)),
                   _BS((None, D, mw), lambda i, j: (j, 0, 0)), _BS((tt, mw), lambda i, j: (i, j)),
                   (t, 2 * D), (t // tt, N_DEV))
    o_pre, hgy, states = _hgrn_fwd(p_hg, s["hg_lb_table"], s["hg_out_norm"], t)
    prep_args = (p_mla, cs, sn, w["w_q_up"], w["w_kv_up"], s["mla_q_lora_norm"], s["mla_kv_lora_norm"],
                 s["q_head_norm"], s["k_head_norm"])
    q, k, v = _mla_prep_fwd(*prep_args, t)
    att, lse = _flash_fwd(q, k, v, t)
    y_hg = _mm(hgy, w_hgb, "nn", "branch_hg")
    y_mla = _mm(att, w_mlab, "nn", "branch_mla")

    def mix_fn(gh_ref, gm_ref, yh_ref, ym_ref, b_ref, o_ref):
        gh = _sig(gh_ref[...] + b_ref[:, 0:D])
        gm = _sig(gm_ref[...] + b_ref[:, D:2 * D])
        o_ref[...] = (gh * yh_ref[...] + gm * ym_ref[...]).astype(BF16)

    mixed = _rows(mix_fn, "mix", t, tm, [(gpre, D, 0), (gpre, D, 1), (y_hg, D, 0), (y_mla, D, 0)], [s["b_merge"]],
                  [(D, BF16)])[0]
    mo = _mm(mixed, w_o, "nn", "mix_out")
    h2, xn2 = _rows(res_norm_fn(1.0), "res_norm2", t, tm, [(h1, D, 0), (mo, D, 0)], [s["ffn2_norm"]],
                    [(D, F32), (D, BF16)])
    gu2, a2, f2 = _ffn_fwd(xn2, w["ffn2_w_in"], ffn_out("ffn2_w_out"), "ffn2")

    def loss_fn(h_ref, f_ref, tg_ref, g_ref, dh_out, dhb_out):
        h = h_ref[...] + 0.5 * f_ref[...]
        e = _rms(h, g_ref[...]) - tg_ref[...]
        dh, dgain = _rms_bwd(h, g_ref[...], e / D)
        dh_out[...] = dh
        dhb_out[...] = (0.5 * dh).astype(BF16)
        return dgain, jnp.full((1, LANE), 0.5 / D * jnp.sum(e * e), F32)

    dh3, dfo2, g["final_norm"], loss = _rows(loss_fn, "loss", t, tm, [(h2, D, 0), (f2, D, 0), (target, D, 0)],
                                             [s["final_norm"]], [(D, F32), (D, BF16)], [(1, D), (1, LANE)])

    def norm_bwd_fn(scale):
        def fn(h_ref, dxn_ref, dh_ref, g_ref, dh_out, dhb_out):
            dx, dgain = _rms_bwd(h_ref[...], g_ref[...], dxn_ref[...])
            dh = dh_ref[...] + dx
            dh_out[...] = dh
            dhb_out[...] = (scale * dh).astype(BF16)
            return (dgain,)
        return fn

    as_rows = lambda a: a.reshape((N_DEV, -1) + a.shape[-1:])
    dxn2, g["ffn2_w_in"], dwo = _ffn_bwd(dfo2, xn2, gu2, a2, w["ffn2_w_in"], ffn_out("ffn2_w_out"), "ffn2")
    g["ffn2_w_out"] = as_rows(dwo)
    dh2, dh2b, g["ffn2_norm"] = _rows(norm_bwd_fn(1.0), "norm2_bwd", t, tm, [(h2, D, 0), (dxn2, D, 0), (dh3, D, 0)],
                                      [s["ffn2_norm"]], [(D, F32), (D, BF16)], [(1, D)])
    dmixed = _mm(dh2b, w_o, "nt", "mix_out_dx")
    g["w_out"] = as_rows(_mm(mixed, dh2b, "tn", "mix_out_dw"))

    def mix_bwd_fn(gh_ref, gm_ref, yh_ref, ym_ref, dm_ref, b_ref, dyh_out, dym_out, dg_out):
        gh = _sig(gh_ref[...] + b_ref[:, 0:D])
        gm = _sig(gm_ref[...] + b_ref[:, D:2 * D])
        dm = dm_ref[...]
        dyh_out[...] = (dm * gh).astype(BF16)
        dym_out[...] = (dm * gm).astype(BF16)
        dgh = dm * yh_ref[...] * gh * (1.0 - gh)
        dgm = dm * ym_ref[...] * gm * (1.0 - gm)
        dg_out[:, 0:D] = dgh.astype(BF16)
        dg_out[:, D:2 * D] = dgm.astype(BF16)
        return (jnp.concatenate([jnp.sum(dgh, axis=0, keepdims=True), jnp.sum(dgm, axis=0, keepdims=True)], axis=1),)

    dyh, dym, dgpre, g["b_merge"] = _rows(
        mix_bwd_fn, "mix_bwd", t, tm, [(gpre, D, 0), (gpre, D, 1), (y_hg, D, 0), (y_mla, D, 0), (dmixed, D, 0)],
        [s["b_merge"]], [(D, BF16), (D, BF16), (2 * D, BF16)], [(1, 2 * D)])
    g["w_hg_branch"] = as_rows(_mm(hgy, dyh, "tn", "branch_hg_dw"))
    g["w_mla_branch"] = as_rows(_mm(att, dym, "tn", "branch_mla_dw"))
    g["w_merge"] = _mmcall("proj_gate_dw", "tn", u, dgpre, _BS((tt, D), lambda j, r: (r, 0)),
                           _BS((tt, mw), lambda j, r: (r, j)), _BS((None, D, mw), lambda j, r: (j, 0, 0)),
                           (N_DEV, D, mw), (N_DEV, t // tt), red_axis=1)
    dhgy = _mm(dyh, w_hgb, "nt", "branch_hg_dx")
    datt = _mm(dym, w_mlab, "nt", "branch_mla_dx", out_dtype=BF16)
    du_gate = _mmcall("proj_gate_dx", "nt", dgpre, w_mrg, _BS((tt, mw), lambda i, j: (i, j)),
                      _BS((None, D, mw), lambda i, j: (j, 0, 0)), _BS((tt, D), lambda i, j: (i, 0)),
                      (t, D), (t // tt, N_DEV), red_axis=1)

    dq, dk, dv = _flash_bwd(q, k, v, datt, lse.reshape(NH, 1, t), _flash_delta(att, datt, t), t)
    (dp_mla, g["w_q_up"], g["w_kv_up"], g["mla_q_lora_norm"], g["mla_kv_lora_norm"], g["q_head_norm"],
     g["k_head_norm"]) = _mla_prep_bwd(*prep_args, dq, dk, dv, t)
    dp_hg, g["hg_lb_table"], g["hg_out_norm"] = _hgrn_bwd(p_hg, s["hg_lb_table"], s["hg_out_norm"], o_pre, states,
                                                          dhgy, t)
    dw_in_nat = jnp.concatenate([_mm(u, dp_hg, "tn", "proj_hg_dw"), _mm(u, dp_mla, "tn", "proj_mla_dw")], axis=1)
    g["w_in"] = dw_in_nat.reshape(D, N_DEV, -1).transpose(1, 0, 2)
    du_hg = _mm(dp_hg, w_in_hg, "nt", "proj_hg_dx")
    du_mla = _mm(dp_mla, w_in_mla, "nt", "proj_mla_dx")

    def mixnorm_bwd_fn(h_ref, a_ref, b_ref, c_ref, dh_ref, g_ref, dh_out, dhb_out):
        dx, dgain = _rms_bwd(h_ref[...], g_ref[...], a_ref[...] + b_ref[...] + c_ref[...])
        dh = dh_ref[...] + dx
        dh_out[...] = dh
        dhb_out[...] = (0.5 * dh).astype(BF16)
        return (dgain,)

    dh1, dfo1, g["mix_norm"] = _rows(mixnorm_bwd_fn, "mixnorm_bwd", t, tm,
                                     [(h1, D, 0), (du_hg, D, 0), (du_mla, D, 0), (du_gate, D, 0), (dh2, D, 0)],
                                     [s["mix_norm"]], [(D, F32), (D, BF16)], [(1, D)])
    dxn1, g["ffn1_w_in"], dwo = _ffn_bwd(dfo1, xn1, gu1, a1, w["ffn1_w_in"], ffn_out("ffn1_w_out"), "ffn1")
    g["ffn1_w_out"] = as_rows(dwo)
    grad_x, _, g["ffn1_norm"] = _rows(norm_bwd_fn(1.0), "norm1_bwd", t, tm, [(x, D, 0), (dxn1, D, 0), (dh1, D, 0)],
                                      [s["ffn1_norm"]], [(D, F32), (D, BF16)], [(1, D)])
    return loss, grad_x, g


def _coords():
    return lax.axis_index("x"), lax.axis_index("y"), lax.axis_index("c")


def _hbm_call(body, name, ins, out_shapes, scratch):
    any_spec = pl.BlockSpec(memory_space=pl.ANY)
    return pl.pallas_call(
        body, name=name, out_shape=[jax.ShapeDtypeStruct(s, dt) for s, dt in out_shapes],
        in_specs=[any_spec] * len(ins), out_specs=[any_spec] * len(out_shapes), scratch_shapes=scratch,
    )(*ins)


def _all_gather(blocks, name):
    nb = len(blocks)

    def body(*refs):
        x_refs, out_refs = refs[:nb], refs[nb:2 * nb]
        send_sems, recv_sems, local_sems = refs[2 * nb:]
        x, y, c = _coords()
        me, sibling = (x, y, c), (x, y, 1 - c)
        chips = [(1 - x, y), (x, 1 - y), (1 - x, 1 - y)]

        def slot(b, px, py, pc):
            return out_refs[b].at[4 * px + 2 * py + pc]

        def copy(b, kk, block_of, to, src=None):
            return pltpu.make_async_remote_copy(
                src_ref=slot(b, *block_of) if src is None else src, dst_ref=slot(b, *block_of),
                send_sem=send_sems.at[b, kk], recv_sem=recv_sems.at[b, kk], device_id=to, device_id_type=MESH)

        mine = [pltpu.make_async_copy(x_refs[b], slot(b, *me), local_sems.at[b]) for b in range(nb)]
        for cp in mine:
            cp.start()
        first = [copy(b, 0, me, sibling, src=x_refs[b]) for b in range(nb)]
        first += [copy(b, 1 + j, me, (*chip, c), src=x_refs[b]) for j, chip in enumerate(chips) for b in range(nb)]
        for cp in first:
            cp.start()
        passed = []
        for j, chip in enumerate(chips):
            for b in range(nb):
                copy(b, 1 + j, (*chip, c), me).wait_recv()
                passed.append(copy(b, 4 + j, (*chip, c), sibling))
                passed[-1].start()
        for b in range(nb):
            copy(b, 0, sibling, me).wait_recv()
        for j, chip in enumerate(chips):
            for b in range(nb):
                copy(b, 4 + j, (*chip, 1 - c), me).wait_recv()
        for cp in first + passed:
            cp.wait_send()
        for cp in mine:
            cp.wait()

    return _hbm_call(body, name, blocks, [((N_DEV,) + b.shape, b.dtype) for b in blocks],
                     [pltpu.SemaphoreType.DMA((nb, 7)), pltpu.SemaphoreType.DMA((nb, 7)),
                      pltpu.SemaphoreType.DMA((nb,))])


def _sibling_swap(bufs, name):
    nb = len(bufs)

    def body(*refs):
        x_refs, out_refs = refs[:nb], refs[nb:2 * nb]
        send_sems, recv_sems = refs[2 * nb:]
        x, y, c = _coords()
        copies = [pltpu.make_async_remote_copy(
            src_ref=x_refs[b].at[2 * q + 1 - c], dst_ref=out_refs[b].at[q], send_sem=send_sems.at[b, q],
            recv_sem=recv_sems.at[b, q], device_id=(x, y, 1 - c), device_id_type=MESH)
            for b in range(nb) for q in range(4)]
        for cp in copies:
            cp.start()
        for cp in copies:
            cp.wait()

    return _hbm_call(body, name, bufs, [((4,) + b.shape[1:], b.dtype) for b in bufs],
                     [pltpu.SemaphoreType.DMA((nb, 4)), pltpu.SemaphoreType.DMA((nb, 4))])


def _chip_exchange(bufs, name):
    nb = len(bufs)

    def body(*refs):
        x_refs, out_refs = refs[:nb], refs[nb:2 * nb]
        send_sems, recv_sems, local_sems = refs[2 * nb:]
        x, y, c = _coords()
        chips = [(1 - x, y), (x, 1 - y), (1 - x, 1 - y)]
        mine = [pltpu.make_async_copy(x_refs[b].at[2 * x + y], out_refs[b].at[2 * x + y], local_sems.at[b])
                for b in range(nb)]
        for cp in mine:
            cp.start()
        copies = [pltpu.make_async_remote_copy(
            src_ref=x_refs[b].at[2 * px + py], dst_ref=out_refs[b].at[2 * x + y], send_sem=send_sems.at[b, j],
            recv_sem=recv_sems.at[b, j], device_id=(px, py, c), device_id_type=MESH)
            for j, (px, py) in enumerate(chips) for b in range(nb)]
        for cp in copies:
            cp.start()
        for j, (px, py) in enumerate(chips):
            for b in range(nb):
                pltpu.make_async_remote_copy(
                    src_ref=x_refs[b].at[2 * x + y], dst_ref=out_refs[b].at[2 * px + py], send_sem=send_sems.at[b, j],
                    recv_sem=recv_sems.at[b, j], device_id=(px, py, c), device_id_type=MESH).wait_recv()
        for cp in copies:
            cp.wait_send()
        for cp in mine:
            cp.wait()

    return _hbm_call(body, name, bufs, [(b.shape, b.dtype) for b in bufs],
                     [pltpu.SemaphoreType.DMA((nb, 3)), pltpu.SemaphoreType.DMA((nb, 3)),
                      pltpu.SemaphoreType.DMA((nb,))])


def _chip_sum(g, r1, c, name):
    _, r, cw = g.shape
    tr = _pick(r, (256, 176, 128))

    def body(c_ref, g_ref, r_ref, o_ref):
        o_ref[...] = (g_ref[...] + r_ref[...]).astype(BF16)

    grid_spec = pltpu.PrefetchScalarGridSpec(
        num_scalar_prefetch=1, grid=(4, r // tr),
        in_specs=[_BS((None, None, tr, cw), lambda q, i, c_ref: (q, c_ref[0], i, 0)),
                  _BS((None, tr, cw), lambda q, i, c_ref: (q, i, 0))],
        out_specs=_BS((None, tr, cw), lambda q, i, c_ref: (q, i, 0)))
    return pl.pallas_call(
        body, name=name, grid_spec=grid_spec, out_shape=jax.ShapeDtypeStruct((4, r, cw), BF16),
        compiler_params=_params(("parallel", "parallel")),
    )(c.reshape(1).astype(jnp.int32), g.reshape(4, 2, r, cw), r1)


def _adamw_math(w, g, m, v):
    m = B1 * m + (1.0 - B1) * g
    v = B2 * v + (1.0 - B2) * (g * g)
    m_hat = m / (1.0 - B1 ** STEP)
    v_hat = v / (1.0 - B2 ** STEP)
    return -LR * (m_hat / (jnp.sqrt(v_hat) + AEPS) + WD * w), m, v


def _sum_adamw(parts, w, m, v, name):
    r, c = w.shape
    tr = _pick(r, (256, 176, 128))

    def body(p0, p1, p2, p3, w_ref, m_ref, v_ref, g_out, d_out, m_out, v_out):
        g = ((p0[...].astype(F32) + p1[...].astype(F32)) + p2[...].astype(F32)) + p3[...].astype(F32)
        g_out[...] = g
        d_out[...], m_out[...], v_out[...] = _adamw_math(w_ref[...], g, m_ref[...], v_ref[...])

    part = lambda q: _BS((None, tr, c), functools.partial(lambda i, q: (q, i, 0), q=q))
    plain = _BS((tr, c), lambda i: (i, 0))
    return pl.pallas_call(
        body, name=name, grid=(r // tr,), in_specs=[part(q) for q in range(4)] + [plain] * 3,
        out_specs=[plain] * 4, out_shape=[jax.ShapeDtypeStruct((r, c), F32)] * 4,
        compiler_params=_params(("parallel",)),
    )(parts, parts, parts, parts, w, m, v)


def _small_update(gathered, w, m, v):
    r = w.shape[0]

    def body(ga_ref, w_ref, m_ref, v_ref, g_out, d_out, m_out, v_out):
        g = ga_ref[0]
        for dev in range(1, N_DEV):
            g = g + ga_ref[dev]
        g_out[...] = g
        d_out[...], m_out[...], v_out[...] = _adamw_math(w_ref[...], g, m_ref[...], v_ref[...])

    return pl.pallas_call(
        body, name="small_update", out_shape=[jax.ShapeDtypeStruct((r, LANE), F32)] * 4,
    )(gathered, w, m, v)


def _pack_small(vals):
    rows = []
    for name, (r, n) in SMALL:
        flat = vals[name].reshape(-1)
        pad = (-flat.shape[0]) % LANE
        rows.append(jnp.pad(flat, (0, pad)).reshape(-1, LANE))
    return jnp.concatenate(rows, axis=0)


def _unpack_small(packed):
    out, off = {}, 0
    for name, (r, n) in SMALL:
        nrow = (r * n + LANE - 1) // LANE
        out[name] = packed[off:off + nrow].reshape(-1)[:r * n].reshape(r, n)
        off += nrow
    return out


def kernel(x, positions, ffn1_norm, ffn1_w_in, ffn1_w_out, mix_norm, w_in, hg_lb_table, hg_out_norm, w_hg_branch, mla_q_lora_norm, w_q_up, mla_kv_lora_norm, w_kv_up, q_head_norm, k_head_norm, w_mla_branch, w_merge, b_merge, w_out, ffn2_norm, ffn2_w_in, ffn2_w_out, final_norm, loss_target, m_ffn1_norm, m_ffn1_w_in, m_ffn1_w_out, m_mix_norm, m_w_in, m_hg_lb_table, m_hg_out_norm, m_w_hg_branch, m_mla_q_lora_norm, m_w_q_up, m_mla_kv_lora_norm, m_w_kv_up, m_q_head_norm, m_k_head_norm, m_w_mla_branch, m_w_merge, m_b_merge, m_w_out, m_ffn2_norm, m_ffn2_w_in, m_ffn2_w_out, m_final_norm, v_ffn1_norm, v_ffn1_w_in, v_ffn1_w_out, v_mix_norm, v_w_in, v_hg_lb_table, v_hg_out_norm, v_w_hg_branch, v_mla_q_lora_norm, v_w_q_up, v_mla_kv_lora_norm, v_w_kv_up, v_q_head_norm, v_k_head_norm, v_w_mla_branch, v_w_merge, v_b_merge, v_w_out, v_ffn2_norm, v_ffn2_w_in, v_ffn2_w_out, v_final_norm):
    args = dict(locals())
    t = x.shape[1]
    big_w = {n: args[n][0] for n, _, _ in BIG}
    small = {n: args[n].reshape(shape) for n, shape in SMALL}

    names = [n for n, _, _ in BIG]
    full = dict(zip(names, _all_gather([big_w[n].astype(BF16) for n in names], "weights_all_gather")))

    inv_freq = ROPE_THETA ** (-jnp.arange(0, ROPE, 2, dtype=F32) / ROPE)
    ang = positions[0].astype(F32)[:, None] * inv_freq
    cs = jnp.concatenate([jnp.cos(ang), jnp.cos(ang)], axis=1)
    sn = jnp.concatenate([jnp.sin(ang), jnp.sin(ang)], axis=1)

    loss_row, grad_x, g = _local_step(x[0], loss_target[0], cs, sn, full, small)

    c = lax.axis_index("c")
    from_sibling = _sibling_swap([g[n] for n in names], "grads_sibling_swap")
    chip_sums = [_chip_sum(g[n], r1, c, "chip_sum_" + n) for n, r1 in zip(names, from_sibling)]
    exchanged = dict(zip(names, _chip_exchange(chip_sums, "grads_chip_exchange")))

    small_packed = jnp.concatenate([_pack_small(g), loss_row], axis=0)
    pad = (-small_packed.shape[0]) % 8
    small_packed = jnp.pad(small_packed, ((0, pad), (0, 0)))
    small_all = _all_gather([small_packed], "small_all_gather")[0]
    zero_tail = jnp.zeros((small_packed.shape[0] - _pack_small(small).shape[0], LANE), F32)
    pk = lambda d: jnp.concatenate([_pack_small(d), zero_tail], axis=0)
    sg, sd, sm, sv = _small_update(
        small_all, pk(small), pk({n: args["m_" + n].reshape(shape) for n, shape in SMALL}),
        pk({n: args["v_" + n].reshape(shape) for n, shape in SMALL}))
    n_small_rows = _pack_small(small).shape[0]
    loss = sg[n_small_rows, 0]
    outs = {k_: _unpack_small(a) for k_, a in (("grad", sg), ("delta", sd), ("new_m", sm), ("new_v", sv))}

    for n in names:
        outs["grad"][n], outs["delta"][n], outs["new_m"][n], outs["new_v"][n] = _sum_adamw(
            exchanged[n], big_w[n], args["m_" + n][0], args["v_" + n][0], "adamw_" + n)

    def shaped(kind, n):
        return outs[kind][n].reshape(args[n].shape)

    return (loss, grad_x[None], *[shaped("grad", n) for n in WEIGHT_ORDER], *[shaped("delta", n) for n in WEIGHT_ORDER],
            *[shaped("new_m", n) for n in WEIGHT_ORDER], *[shaped("new_v", n) for n in WEIGHT_ORDER])
```

```python
import functools

import jax
import jax.numpy as jnp
from jax import lax
from jax.experimental import pallas as pl
from jax.experimental.pallas import tpu as pltpu

F32 = jnp.float32
BF16 = jnp.bfloat16

D = 1024
FF = 2816
NH = 8
HD = 128
ROPE = 64
QK = HD + ROPE
QL = 384
KVL = 256
HGW = NH * HD
CHUNK = 64
EPS = 1e-6
ROPE_THETA = 10000.0
SCALE = QK ** -0.5

LR, B1, B2, AEPS, WD, STEP = 0.001, 0.9, 0.999, 1e-08, 0.01, 10

HB = 128
SUB = 16
EXP_CLAMP = 80.0
ATT_TILES = (512, 256, 128)
ROW_TILES = (1024, 512, 256, 128)

LANE = 128
VMEM_LIMIT = 56 << 20

N_DEV = 8
MESH = pl.DeviceIdType.MESH

BIG = (
    ("ffn1_w_in", (D, 2 * FF), 1), ("ffn1_w_out", (FF, D), 0), ("w_in", (D, 4800), 1),
    ("w_hg_branch", (HGW, D), 0), ("w_q_up", (QL, NH * QK), 1), ("w_kv_up", (KVL, NH * 2 * HD), 1),
    ("w_mla_branch", (NH * HD, D), 0), ("w_merge", (D, 2 * D), 1), ("w_out", (D, D), 0),
    ("ffn2_w_in", (D, 2 * FF), 1), ("ffn2_w_out", (FF, D), 0),
)
SMALL = (
    ("ffn1_norm", (1, D)), ("mix_norm", (1, D)), ("hg_lb_table", (2, HGW)), ("hg_out_norm", (1, HD)),
    ("mla_q_lora_norm", (1, QL)), ("mla_kv_lora_norm", (1, KVL)), ("q_head_norm", (1, QK)),
    ("k_head_norm", (1, QK)), ("b_merge", (1, 2 * D)), ("ffn2_norm", (1, D)), ("final_norm", (1, D)),
)
WEIGHT_ORDER = ("ffn1_norm", "ffn1_w_in", "ffn1_w_out", "mix_norm", "w_in", "hg_lb_table", "hg_out_norm",
                "w_hg_branch", "mla_q_lora_norm", "w_q_up", "mla_kv_lora_norm", "w_kv_up", "q_head_norm",
                "k_head_norm", "w_mla_branch", "w_merge", "b_merge", "w_out", "ffn2_norm", "ffn2_w_in",
                "ffn2_w_out", "final_norm")


def _pick(n, cands):
    for c in cands:
        if n % c == 0:
            return c
    return n


def _params(sem):
    return pltpu.CompilerParams(dimension_semantics=sem, vmem_limit_bytes=VMEM_LIMIT)


def _sig(x):
    return 1.0 / (1.0 + jnp.exp(-x))


def _dot(a, b):
    return jnp.dot(a.astype(BF16), b.astype(BF16), preferred_element_type=F32)


def _dot_nt(a, b):
    return lax.dot_general(a.astype(BF16), b.astype(BF16), (((1,), (1,)), ((), ())),
                           preferred_element_type=F32)


def _dot_tn(a, b):
    return lax.dot_general(a.astype(BF16), b.astype(BF16), (((0,), (0,)), ((), ())),
                           preferred_element_type=F32)


def _split3(x):
    x1 = x.astype(BF16)
    r1 = x - x1.astype(F32)
    x2 = r1.astype(BF16)
    x3 = (r1 - x2.astype(F32)).astype(BF16)
    return x1, x2, x3


def _dot_sel(m, x):
    x1, x2, x3 = _split3(x)
    d = lambda p: jnp.dot(m, p, preferred_element_type=F32)
    return d(x1) + d(x2) + d(x3)


def _sel_dot(x, m):
    x1, x2, x3 = _split3(x)
    d = lambda p: jnp.dot(p, m, preferred_element_type=F32)
    return d(x1) + d(x2) + d(x3)


_TN = (1408, 1024, 768, 512, 384, 256, 128)


def _mm(a, b, mode, name, out_dtype=F32):
    if mode == "tn":
        t, m = a.shape
        n = b.shape[1]
        tt, tm, tn = _pick(t, (512, 256, 128)), _pick(m, _TN), _pick(n, _TN)

        def body(a_ref, b_ref, o_ref):
            @pl.when(pl.program_id(2) == 0)
            def _():
                o_ref[...] = jnp.zeros_like(o_ref)

            o_ref[...] += _dot_tn(a_ref[...], b_ref[...])

        return pl.pallas_call(
            body, name=name, grid=(m // tm, n // tn, t // tt),
            in_specs=[pl.BlockSpec((tt, tm), lambda i, j, k: (k, i)),
                      pl.BlockSpec((tt, tn), lambda i, j, k: (k, j))],
            out_specs=pl.BlockSpec((tm, tn), lambda i, j, k: (i, j)),
            out_shape=jax.ShapeDtypeStruct((m, n), F32),
            compiler_params=_params(("parallel", "parallel", "arbitrary")),
        )(a, b)

    m, k = a.shape
    tm = _pick(m, ROW_TILES)
    if mode == "nn":
        n = b.shape[1]
        tn = _pick(n, _TN)
        b_spec = pl.BlockSpec((k, tn), lambda i, j: (0, j))
        dot = _dot
    else:
        n = b.shape[0]
        tn = _pick(n, _TN if k <= 4096 else (512, 256, 128))
        b_spec = pl.BlockSpec((tn, k), lambda i, j: (j, 0))
        dot = _dot_nt

    def body(a_ref, b_ref, o_ref):
        o_ref[...] = dot(a_ref[...], b_ref[...]).astype(o_ref.dtype)

    return pl.pallas_call(
        body, name=name, grid=(m // tm, n // tn),
        in_specs=[pl.BlockSpec((tm, k), lambda i, j: (i, 0)), b_spec],
        out_specs=pl.BlockSpec((tm, tn), lambda i, j: (i, j)),
        out_shape=jax.ShapeDtypeStruct((m, n), out_dtype),
        compiler_params=_params(("parallel", "parallel")),
    )(a, b)


_DOTS = {"nn": _dot, "nt": _dot_nt, "tn": _dot_tn}
_BS = pl.BlockSpec


def _mmcall(name, kind, a, b, a_spec, b_spec, o_spec, o_shape, grid, red_axis=None, out_dtype=F32):
    dot = _DOTS[kind]

    def body(a_ref, b_ref, o_ref):
        if red_axis is None:
            o_ref[...] = dot(a_ref[...], b_ref[...]).astype(o_ref.dtype)
        else:
            @pl.when(pl.program_id(red_axis) == 0)
            def _():
                o_ref[...] = jnp.zeros_like(o_ref)

            o_ref[...] += dot(a_ref[...], b_ref[...])

    sem = tuple("arbitrary" if ax == red_axis else "parallel" for ax in range(len(grid)))
    return pl.pallas_call(
        body, name=name, grid=grid, in_specs=[a_spec, b_spec], out_specs=o_spec,
        out_shape=jax.ShapeDtypeStruct(o_shape, out_dtype), compiler_params=_params(sem),
    )(a, b)


def _mm_stack_red(a, w, name, kind):
    s, t, n = a.shape
    nout = w.shape[2] if kind == "nn" else w.shape[1]
    tm = _pick(t, ROW_TILES)
    return _mmcall(name, kind, a, w, _BS((None, tm, n), lambda i, j: (j, i, 0)),
                   _BS((None,) + w.shape[1:], lambda i, j: (j, 0, 0)), _BS((tm, nout), lambda i, j: (i, 0)),
                   (t, nout), (t // tm, s), red_axis=1)


def _mm_stack_tn(a, b, name):
    if a.ndim == 2:
        t, k = a.shape
        s, _, n = b.shape
        a_spec, b_spec, o_shape = (lambda tt: _BS((tt, k), lambda j, r: (r, 0))), \
            (lambda tt: _BS((None, tt, n), lambda j, r: (j, r, 0))), (s, k, n)
    else:
        s, t, k = a.shape
        n = b.shape[1]
        a_spec, b_spec, o_shape = (lambda tt: _BS((None, tt, k), lambda j, r: (j, r, 0))), \
            (lambda tt: _BS((tt, n), lambda j, r: (r, 0))), (s, k, n)
    tt = _pick(t, (512, 256, 128))
    return _mmcall(name, "tn", a, b, a_spec(tt), b_spec(tt), _BS((None,) + o_shape[1:], lambda j, r: (j, 0, 0)),
                   o_shape, (s, t // tt), red_axis=1)


def _cols_fwd(x, w, name):
    t, k = x.shape
    s, _, n = w.shape
    tm = _pick(t, ROW_TILES)

    def body(x_ref, w_ref, o_ref):
        x_ = x_ref[...]
        for j in range(s):
            o_ref[:, n * j:n * (j + 1)] = _dot(x_, w_ref[j])

    return pl.pallas_call(
        body, name=name, grid=(t // tm,),
        in_specs=[_BS((tm, k), lambda i: (i, 0)), _BS((s, k, n), lambda i: (0, 0, 0))],
        out_specs=_BS((tm, s * n), lambda i: (i, 0)), out_shape=jax.ShapeDtypeStruct((t, s * n), F32),
        compiler_params=_params(("parallel",)),
    )(x, w)


def _cols_dx(d, w, name):
    t = d.shape[0]
    s, k, n = w.shape
    tm = _pick(t, ROW_TILES)

    def body(d_ref, w_ref, o_ref):
        acc = _dot_nt(d_ref[:, 0:n], w_ref[0])
        for j in range(1, s):
            acc = acc + _dot_nt(d_ref[:, n * j:n * (j + 1)], w_ref[j])
        o_ref[...] = acc

    return pl.pallas_call(
        body, name=name, grid=(t // tm,),
        in_specs=[_BS((tm, s * n), lambda i: (i, 0)), _BS((s, k, n), lambda i: (0, 0, 0))],
        out_specs=_BS((tm, k), lambda i: (i, 0)), out_shape=jax.ShapeDtypeStruct((t, k), F32),
        compiler_params=_params(("parallel",)),
    )(d, w)


def _cols_dw(x, d, n, name):
    t, k = x.shape
    s = d.shape[1] // n
    tt = _pick(t, (512, 256, 128))

    def body(x_ref, d_ref, o_ref):
        @pl.when(pl.program_id(0) == 0)
        def _():
            o_ref[...] = jnp.zeros_like(o_ref)

        x_ = x_ref[...]
        for j in range(s):
            o_ref[j] += _dot_tn(x_, d_ref[:, n * j:n * (j + 1)])

    return pl.pallas_call(
        body, name=name, grid=(t // tt,),
        in_specs=[_BS((tt, k), lambda r: (r, 0)), _BS((tt, s * n), lambda r: (r, 0))],
        out_specs=_BS((s, k, n), lambda r: (0, 0, 0)), out_shape=jax.ShapeDtypeStruct((s, k, n), F32),
        compiler_params=_params(("arbitrary",)),
    )(x, d)


def _rows(fn, name, t, tm, ins, vecs, outs, accs=()):
    n_in, n_out, n_acc = len(ins) + len(vecs), len(outs), len(accs)

    def body(*refs):
        res = fn(*refs[:n_in + n_out])
        if n_acc:
            acc_refs = refs[n_in + n_out:]

            @pl.when(pl.program_id(0) == 0)
            def _():
                for r in acc_refs:
                    r[...] = jnp.zeros_like(r)

            for r, val in zip(acc_refs, res):
                r[...] += val

    in_specs = [pl.BlockSpec((tm, bw), functools.partial(lambda i, cb: (i, cb), cb=cb)) for _, bw, cb in ins]
    in_specs += [pl.BlockSpec(v.shape, lambda i: (0, 0)) for v in vecs]
    out_specs = [pl.BlockSpec((tm, w), lambda i: (i, 0)) for w, _ in outs]
    out_specs += [pl.BlockSpec(s, lambda i: (0, 0)) for s in accs]
    out_shape = [jax.ShapeDtypeStruct((t, w), dt) for w, dt in outs]
    out_shape += [jax.ShapeDtypeStruct(s, F32) for s in accs]
    return pl.pallas_call(
        body, name=name, grid=(t // tm,), in_specs=in_specs, out_specs=out_specs, out_shape=out_shape,
        compiler_params=_params(("arbitrary",) if n_acc else ("parallel",)),
    )(*[a for a, _, _ in ins], *vecs)


def _rms(x, g):
    return x * lax.rsqrt(jnp.mean(x * x, axis=-1, keepdims=True) + EPS) * g


def _rms_bwd(x, g, dy):
    xh = x * lax.rsqrt(jnp.mean(x * x, axis=-1, keepdims=True) + EPS)
    r = lax.rsqrt(jnp.mean(x * x, axis=-1, keepdims=True) + EPS)
    dyg = dy * g
    dx = r * (dyg - xh * jnp.mean(dyg * xh, axis=-1, keepdims=True))
    return dx, jnp.sum(dy * xh, axis=0, keepdims=True)


def _hgrn_mats():
    row = lax.broadcasted_iota(jnp.int32, (HB, HB), 0)
    col = lax.broadcasted_iota(jnp.int32, (HB, HB), 1)
    return row, col


def _hgrn_gates(qr, z, t0, t1):
    lb = 1.0 / (1.0 + jnp.exp(t1 - t0))
    sz = _sig(z)
    sneg = 1.0 / (1.0 + jnp.exp(z))
    f = lb + (1.0 - lb) * sz
    return lb, sz, sneg, f, jnp.log(f), (1.0 - lb) * sneg, qr * _sig(qr)


def _hgrn_scores(q, k, cum, cmid):
    qd = q * jnp.exp(jnp.minimum(cmid, EXP_CLAMP))
    qd_b = qd.astype(BF16)
    kds, parts = [], []
    for i in range(HB // SUB):
        mid = cum[SUB * i + SUB // 2 - 1:SUB * i + SUB // 2, :]
        kd = k * jnp.exp(jnp.minimum(mid - cum, EXP_CLAMP))
        kds.append(kd)
        parts.append(_dot_nt(qd_b[SUB * i:SUB * (i + 1)], kd))
    return qd, kds, jnp.concatenate(parts, axis=0)


def _hgrn_fwd(p_hg, table, gain, t):
    nblk = t // HB

    def body(q_ref, f_ref, i_ref, g_ref, tab_ref, gain_ref, o_ref, y_ref, st_ref, state):
        @pl.when(pl.program_id(0) == 0)
        def _():
            state[...] = jnp.zeros_like(state)

        row, col = _hgrn_mats()
        causal = col <= row
        tri = causal.astype(BF16)
        trimid = tri - (col <= (row // SUB) * SUB + SUB // 2 - 1).astype(BF16)
        for h in range(NH):
            sl = slice(HD * h, HD * (h + 1))
            v, gr = i_ref[:, sl], g_ref[:, sl]
            _, _, _, _, lf, k, q = _hgrn_gates(q_ref[:, sl], f_ref[:, sl], tab_ref[0:1, sl], tab_ref[1:2, sl])
            cum = _dot_sel(tri, lf)
            cmid = _dot_sel(trimid, lf)
            _, _, s = _hgrn_scores(q, k, cum, cmid)
            p = jnp.where(causal, s, 0.0)
            st = state[h]
            st_ref[h, 0] = st
            o = _dot(p, v) + _dot_nt(q * jnp.exp(cum), st)
            last = cum[HB - 1:HB, :]
            state[h] = st * jnp.exp(last) + _dot_tn(v, k * jnp.exp(last - cum))
            o_ref[:, sl] = o
            y_ref[:, sl] = (_rms(o, gain_ref[...]) * gr * _sig(gr)).astype(BF16)

    blk = lambda cb: pl.BlockSpec((HB, HGW), functools.partial(lambda n, cb: (n, cb), cb=cb))
    return pl.pallas_call(
        body, name="hgrn_fwd", grid=(nblk,),
        in_specs=[blk(0), blk(1), blk(2), blk(3), pl.BlockSpec((2, HGW), lambda n: (0, 0)),
                  pl.BlockSpec((1, HD), lambda n: (0, 0))],
        out_specs=[pl.BlockSpec((HB, HGW), lambda n: (n, 0)), pl.BlockSpec((HB, HGW), lambda n: (n, 0)),
                   pl.BlockSpec((NH, 1, HD, HD), lambda n: (0, n, 0, 0))],
        out_shape=[jax.ShapeDtypeStruct((t, HGW), F32), jax.ShapeDtypeStruct((t, HGW), BF16),
                   jax.ShapeDtypeStruct((NH, nblk, HD, HD), F32)],
        scratch_shapes=[pltpu.VMEM((NH, HD, HD), F32)],
        compiler_params=_params(("arbitrary",)),
    )(p_hg, p_hg, p_hg, p_hg, table, gain)


def _hgrn_bwd(p_hg, table, gain, o_pre, states, dy, t):
    nblk = t // HB

    def body(q_ref, f_ref, i_ref, g_ref, tab_ref, gain_ref, o_ref, st_ref, dy_ref, dp_ref, dtab_ref, dgain_ref,
             dstate):
        @pl.when(pl.program_id(0) == 0)
        def _():
            dstate[...] = jnp.zeros_like(dstate)
            dtab_ref[...] = jnp.zeros_like(dtab_ref)
            dgain_ref[...] = jnp.zeros_like(dgain_ref)

        row, col = _hgrn_mats()
        causal = col <= row
        tri = causal.astype(BF16)
        midrow = (row // SUB) * SUB + SUB // 2 - 1
        trimid = tri - (col <= midrow).astype(BF16)
        tri_t = (row <= col).astype(BF16)
        midcol = (col // SUB) * SUB + SUB // 2 - 1
        trimid_t = tri_t - (row <= midcol).astype(BF16)
        dgain = jnp.zeros((1, HD), F32)
        for h in range(NH):
            sl = slice(HD * h, HD * (h + 1))
            qr, z, v, gr = q_ref[:, sl], f_ref[:, sl], i_ref[:, sl], g_ref[:, sl]
            lb, sz, sneg, f, lf, k, q = _hgrn_gates(qr, z, tab_ref[0:1, sl], tab_ref[1:2, sl])
            cum = _dot_sel(tri, lf)
            cmid = _dot_sel(trimid, lf)
            qd, kds, s = _hgrn_scores(q, k, cum, cmid)
            p = jnp.where(causal, s, 0.0)
            st = st_ref[h, 0]
            dst = dstate[h]
            o = o_ref[:, sl]
            sg = _sig(gr)
            dyh = dy_ref[:, sl]
            on = _rms(o, gain_ref[...])
            dgr = dyh * on * sg * (1.0 + gr * (1.0 - sg))
            do, dg_h = _rms_bwd(o, gain_ref[...], dyh * gr * sg)
            dgain = dgain + dg_h
            do_b = do.astype(BF16)
            ecum = jnp.exp(cum)
            qc = q * ecum
            last = cum[HB - 1:HB, :]
            edec = jnp.exp(last - cum)
            kdec = k * edec
            dp = jnp.where(causal, _dot_nt(do_b, v), 0.0)
            dv = _dot(p.T, do_b) + _dot_nt(kdec, dst)
            dqc = _dot(do_b, st)
            dkdec = _dot(v, dst)
            dstate[h] = dst * jnp.exp(last) + _dot(do.T, qc)
            dp_b = dp.astype(BF16)
            dqd = jnp.concatenate([_dot(dp_b[SUB * i:SUB * (i + 1)], kds[i]) for i in range(HB // SUB)], axis=0)
            gq = dqd * qd
            dq = dqd * jnp.exp(jnp.minimum(cmid, EXP_CLAMP)) + dqc * ecum
            gs = dkdec * kdec
            dk = dkdec * edec
            dcum = dqc * qc - gs
            dcum = dcum + jnp.where(row == HB - 1, jnp.sum(gs, axis=0, keepdims=True)
                                    + jnp.exp(last) * jnp.sum(st * dst, axis=0, keepdims=True), 0.0)
            qd_b = qd.astype(BF16)
            for i in range(HB // SUB):
                dkd = _dot_tn(dp_b[SUB * i:SUB * (i + 1)], qd_b[SUB * i:SUB * (i + 1)])
                mid = cum[SUB * i + SUB // 2 - 1:SUB * i + SUB // 2, :]
                dk = dk + dkd * jnp.exp(jnp.minimum(mid - cum, EXP_CLAMP))
                gk = dkd * kds[i]
                dcum = dcum - gk + jnp.where(row == SUB * i + SUB // 2 - 1, jnp.sum(gk, axis=0, keepdims=True), 0.0)
            dlf = _dot_sel(tri_t, dcum) + _dot_sel(trimid_t, gq)
            df = dlf / f - dk
            dz = df * (1.0 - lb) * sz * sneg
            dlb = jnp.sum(df * sneg, axis=0, keepdims=True) * lb * (1.0 - lb)
            dtab_ref[0:1, sl] += dlb
            dtab_ref[1:2, sl] -= dlb
            sq = _sig(qr)
            dp_ref[:, sl] = (dq * sq * (1.0 + qr * (1.0 - sq))).astype(BF16)
            dp_ref[:, HGW + HD * h:HGW + HD * (h + 1)] = dz.astype(BF16)
            dp_ref[:, 2 * HGW + HD * h:2 * HGW + HD * (h + 1)] = dv.astype(BF16)
            dp_ref[:, 3 * HGW + HD * h:3 * HGW + HD * (h + 1)] = dgr.astype(BF16)
        dgain_ref[...] += dgain

    rev = lambda cb: pl.BlockSpec((HB, HGW), functools.partial(lambda n, cb: (nblk - 1 - n, cb), cb=cb))
    return pl.pallas_call(
        body, name="hgrn_bwd", grid=(nblk,),
        in_specs=[rev(0), rev(1), rev(2), rev(3), pl.BlockSpec((2, HGW), lambda n: (0, 0)),
                  pl.BlockSpec((1, HD), lambda n: (0, 0)), rev(0),
                  pl.BlockSpec((NH, 1, HD, HD), lambda n: (0, nblk - 1 - n, 0, 0)), rev(0)],
        out_specs=[pl.BlockSpec((HB, 4 * HGW), lambda n: (nblk - 1 - n, 0)),
                   pl.BlockSpec((2, HGW), lambda n: (0, 0)), pl.BlockSpec((1, HD), lambda n: (0, 0))],
        out_shape=[jax.ShapeDtypeStruct((t, 4 * HGW), BF16), jax.ShapeDtypeStruct((2, HGW), F32),
                   jax.ShapeDtypeStruct((1, HD), F32)],
        scratch_shapes=[pltpu.VMEM((NH, HD, HD), F32)],
        compiler_params=_params(("arbitrary",)),
    )(p_hg, p_hg, p_hg, p_hg, table, gain, o_pre, states, dy)


def _rope_mat():
    r = lax.broadcasted_iota(jnp.int32, (ROPE, ROPE), 0)
    c = lax.broadcasted_iota(jnp.int32, (ROPE, ROPE), 1)
    half = ROPE // 2
    return ((r == c - half).astype(F32) - (r == c + half).astype(F32)).astype(BF16)


def _mla_prep_fwd(p_mla, cs, sn, wq, wkv, gql, gkvl, gq, gk, t):
    tm = _pick(t, (512, 256, 128))

    def body(p_ref, cs_ref, sn_ref, wq_ref, wkv_ref, gql_ref, gkvl_ref, gq_ref, gk_ref,
             q_ref, k_ref, v_ref):
        rmat = _rope_mat()
        cqn = _rms(p_ref[:, 0:QL], gql_ref[...]).astype(BF16)
        ckvn = _rms(p_ref[:, QL:QL + KVL], gkvl_ref[...]).astype(BF16)
        kpe = p_ref[:, QL + KVL:QL + KVL + ROPE]
        c, s = cs_ref[...], sn_ref[...]
        rot = lambda x: x * c + _sel_dot(x, rmat) * s
        for h in range(NH):
            qa, qr = _dot(cqn, wq_ref[h, :, 0:HD]), _dot(cqn, wq_ref[h, :, HD:QK])
            rq = lax.rsqrt((jnp.sum(qa * qa, -1, keepdims=True) + jnp.sum(qr * qr, -1, keepdims=True)) / QK + EPS)
            q_ref[h, :, 0:HD] = (qa * rq * gq_ref[:, 0:HD] * SCALE).astype(BF16)
            q_ref[h, :, HD:QK] = (rot(qr * rq * gq_ref[:, HD:QK]) * SCALE).astype(BF16)
            kn = _dot(ckvn, wkv_ref[h, :, 0:HD])
            rk = lax.rsqrt((jnp.sum(kn * kn, -1, keepdims=True) + jnp.sum(kpe * kpe, -1, keepdims=True)) / QK + EPS)
            k_ref[h, :, 0:HD] = (kn * rk * gk_ref[:, 0:HD]).astype(BF16)
            k_ref[h, :, HD:QK] = rot(kpe * rk * gk_ref[:, HD:QK]).astype(BF16)
            v_ref[h] = _dot(ckvn, wkv_ref[h, :, HD:2 * HD]).astype(BF16)

    whole = lambda a: pl.BlockSpec(a.shape, functools.partial(lambda i, nd: (0,) * nd, nd=a.ndim))
    return pl.pallas_call(
        body, name="mla_prep_fwd", grid=(t // tm,),
        in_specs=[pl.BlockSpec((tm, QL + KVL + ROPE), lambda i: (i, 0)), pl.BlockSpec((tm, ROPE), lambda i: (i, 0)),
                  pl.BlockSpec((tm, ROPE), lambda i: (i, 0))] + [whole(a) for a in (wq, wkv, gql, gkvl, gq, gk)],
        out_specs=[pl.BlockSpec((NH, tm, QK), lambda i: (0, i, 0)), pl.BlockSpec((NH, tm, QK), lambda i: (0, i, 0)),
                   pl.BlockSpec((NH, tm, HD), lambda i: (0, i, 0))],
        out_shape=[jax.ShapeDtypeStruct((NH, t, QK), BF16), jax.ShapeDtypeStruct((NH, t, QK), BF16),
                   jax.ShapeDtypeStruct((NH, t, HD), BF16)],
        compiler_params=_params(("parallel",)),
    )(p_mla, cs, sn, wq, wkv, gql, gkvl, gq, gk)


def _mla_prep_bwd(p_mla, cs, sn, wq, wkv, gql, gkvl, gq, gk, dq, dk, dv, t):
    tm = _pick(t, (512, 256, 128))

    def body(p_ref, cs_ref, sn_ref, wq_ref, wkv_ref, gql_ref, gkvl_ref, gq_ref, gk_ref,
             dq_ref, dk_ref, dv_ref,
             dp_ref, dwq_ref, dwkv_ref, dgql_ref, dgkvl_ref, dgq_ref, dgk_ref):
        accs = (dwq_ref, dwkv_ref, dgql_ref, dgkvl_ref, dgq_ref, dgk_ref)

        @pl.when(pl.program_id(0) == 0)
        def _():
            for r in accs:
                r[...] = jnp.zeros_like(r)

        rmat = _rope_mat()
        rmat_t = -rmat
        cq, ckv = p_ref[:, 0:QL], p_ref[:, QL:QL + KVL]
        kpe = p_ref[:, QL + KVL:QL + KVL + ROPE]
        cqn_f, ckvn_f = _rms(cq, gql_ref[...]), _rms(ckv, gkvl_ref[...])
        cqn, ckvn = cqn_f.astype(BF16), ckvn_f.astype(BF16)
        cqn_t, ckvn_t = cqn_f.T.astype(BF16), ckvn_f.T.astype(BF16)
        c, s = cs_ref[...], sn_ref[...]
        unrot = lambda dy: dy * c + _sel_dot(dy * s, rmat_t)
        dcqn = jnp.zeros((tm, QL), F32)
        dckvn = jnp.zeros((tm, KVL), F32)
        dkpe = jnp.zeros((tm, ROPE), F32)
        dgq_a, dgq_r = jnp.zeros((1, HD), F32), jnp.zeros((1, ROPE), F32)
        dgk_a, dgk_r = jnp.zeros((1, HD), F32), jnp.zeros((1, ROPE), F32)
        for h in range(NH):
            qa, qr = _dot(cqn, wq_ref[h, :, 0:HD]), _dot(cqn, wq_ref[h, :, HD:QK])
            rq = lax.rsqrt((jnp.sum(qa * qa, -1, keepdims=True) + jnp.sum(qr * qr, -1, keepdims=True)) / QK + EPS)
            xa, xr = qa * rq, qr * rq
            dya = dq_ref[h, :, 0:HD] * SCALE
            dyr = unrot(dq_ref[h, :, HD:QK] * SCALE)
            dgq_a += jnp.sum(dya * xa, axis=0, keepdims=True)
            dgq_r += jnp.sum(dyr * xr, axis=0, keepdims=True)
            ga, gr_ = dya * gq_ref[:, 0:HD], dyr * gq_ref[:, HD:QK]
            mean = (jnp.sum(ga * xa, -1, keepdims=True) + jnp.sum(gr_ * xr, -1, keepdims=True)) / QK
            dqa = (rq * (ga - xa * mean)).astype(BF16)
            dqr = (rq * (gr_ - xr * mean)).astype(BF16)
            dwq_ref[h, :, 0:HD] += _dot(cqn_t, dqa)
            dwq_ref[h, :, HD:QK] += _dot(cqn_t, dqr)
            dcqn += _dot_nt(dqa, wq_ref[h, :, 0:HD]) + _dot_nt(dqr, wq_ref[h, :, HD:QK])
            kn = _dot(ckvn, wkv_ref[h, :, 0:HD])
            rk = lax.rsqrt((jnp.sum(kn * kn, -1, keepdims=True) + jnp.sum(kpe * kpe, -1, keepdims=True)) / QK + EPS)
            ya, yr = kn * rk, kpe * rk
            dka = dk_ref[h, :, 0:HD]
            dkr = unrot(dk_ref[h, :, HD:QK])
            dgk_a += jnp.sum(dka * ya, axis=0, keepdims=True)
            dgk_r += jnp.sum(dkr * yr, axis=0, keepdims=True)
            ha, hr = dka * gk_ref[:, 0:HD], dkr * gk_ref[:, HD:QK]
            mean = (jnp.sum(ha * ya, -1, keepdims=True) + jnp.sum(hr * yr, -1, keepdims=True)) / QK
            dkn = (rk * (ha - ya * mean)).astype(BF16)
            dkpe += rk * (hr - yr * mean)
            dvh = dv_ref[h].astype(BF16)
            dwkv_ref[h, :, 0:HD] += _dot(ckvn_t, dkn)
            dwkv_ref[h, :, HD:2 * HD] += _dot(ckvn_t, dvh)
            dckvn += _dot_nt(dkn, wkv_ref[h, :, 0:HD]) + _dot_nt(dvh, wkv_ref[h, :, HD:2 * HD])
        dcq, dg1 = _rms_bwd(cq, gql_ref[...], dcqn)
        dckv, dg2 = _rms_bwd(ckv, gkvl_ref[...], dckvn)
        dp_ref[:, 0:QL] = dcq.astype(BF16)
        dp_ref[:, QL:QL + KVL] = dckv.astype(BF16)
        dp_ref[:, QL + KVL:QL + KVL + ROPE] = dkpe.astype(BF16)
        dgql_ref[...] += dg1
        dgkvl_ref[...] += dg2
        dgq_ref[:, 0:HD] += dgq_a
        dgq_ref[:, HD:QK] += dgq_r
        dgk_ref[:, 0:HD] += dgk_a
        dgk_ref[:, HD:QK] += dgk_r

    whole = lambda a: pl.BlockSpec(a.shape, functools.partial(lambda i, nd: (0,) * nd, nd=a.ndim))
    acc_shapes = [wq.shape, wkv.shape, gql.shape, gkvl.shape, gq.shape, gk.shape]
    return pl.pallas_call(
        body, name="mla_prep_bwd", grid=(t // tm,),
        in_specs=[pl.BlockSpec((tm, QL + KVL + ROPE), lambda i: (i, 0)), pl.BlockSpec((tm, ROPE), lambda i: (i, 0)),
                  pl.BlockSpec((tm, ROPE), lambda i: (i, 0))]
        + [whole(a) for a in (wq, wkv, gql, gkvl, gq, gk)]
        + [pl.BlockSpec((NH, tm, QK), lambda i: (0, i, 0)), pl.BlockSpec((NH, tm, QK), lambda i: (0, i, 0)),
           pl.BlockSpec((NH, tm, HD), lambda i: (0, i, 0))],
        out_specs=[pl.BlockSpec((tm, QL + KVL + ROPE), lambda i: (i, 0))]
        + [pl.BlockSpec(s, functools.partial(lambda i, nd: (0,) * nd, nd=len(s))) for s in acc_shapes],
        out_shape=[jax.ShapeDtypeStruct((t, QL + KVL + ROPE), BF16)]
        + [jax.ShapeDtypeStruct(s, F32) for s in acc_shapes],
        compiler_params=_params(("arbitrary",)),
    )(p_mla, cs, sn, wq, wkv, gql, gkvl, gq, gk, dq, dk, dv)


def _diag_mask(tile, transposed):
    r = lax.broadcasted_iota(jnp.int32, (tile, tile), 0) // CHUNK
    c = lax.broadcasted_iota(jnp.int32, (tile, tile), 1) // CHUNK
    return (r <= c) if transposed else (c <= r)


def _flash_fwd(q, k, v, t):
    tq = _pick(t, ATT_TILES)

    def body(q_ref, k_ref, v_ref, o_ref, lse_ref):
        i = pl.program_id(1)
        qt = q_ref[0]

        def scores(j):
            return _dot_nt(qt, k_ref[0, pl.ds(pl.multiple_of(j * tq, tq), tq), :])

        def update(j, s, carry, masked):
            m, l, acc = carry
            vt = v_ref[0, pl.ds(pl.multiple_of(j * tq, tq), tq), :]
            if masked:
                s = jnp.where(_diag_mask(tq, False), s, -jnp.inf)
            m_new = jnp.maximum(m, jnp.max(s, axis=-1, keepdims=True))
            p = jnp.exp(s - m_new)
            alpha = jnp.exp(m - m_new)
            return m_new, alpha * l + jnp.sum(p, axis=-1, keepdims=True), alpha * acc + _dot(p, vt)

        def step(j, carry):
            s_next = scores(j + 1)
            return update(j, carry[3], carry[:3], False) + (s_next,)

        init = (jnp.full((tq, 1), -jnp.inf, F32), jnp.zeros((tq, 1), F32), jnp.zeros((tq, HD), F32), scores(0))
        carry = lax.fori_loop(0, i, step, init)
        m, l, acc = update(i, carry[3], carry[:3], True)
        o_ref[...] = (acc / l).astype(BF16)
        lse_ref[0] = m + jnp.log(l)

    return pl.pallas_call(
        body, name="flash_fwd", grid=(NH, t // tq),
        in_specs=[pl.BlockSpec((1, tq, QK), lambda h, i: (h, i, 0)), pl.BlockSpec((1, t, QK), lambda h, i: (h, 0, 0)),
                  pl.BlockSpec((1, t, HD), lambda h, i: (h, 0, 0))],
        out_specs=[pl.BlockSpec((tq, HD), lambda h, i: (i, h)), pl.BlockSpec((1, tq, 1), lambda h, i: (h, i, 0))],
        out_shape=[jax.ShapeDtypeStruct((t, NH * HD), BF16), jax.ShapeDtypeStruct((NH, t, 1), F32)],
        compiler_params=_params(("parallel", "parallel")),
    )(q, k, v)


def _flash_delta(o, do, t):
    tq = _pick(t, ATT_TILES)

    def body(o_ref, do_ref, delta_ref):
        prod = do_ref[...].astype(F32) * o_ref[...].astype(F32)
        x1, x2, x3 = _split3(prod)
        ones = jnp.ones((8, HD), BF16)
        d = lambda p: lax.dot_general(ones, p, (((1,), (1,)), ((), ())), preferred_element_type=F32)
        delta_ref[0] = (d(x1) + d(x2) + d(x3))[0:1, :]

    return pl.pallas_call(
        body, name="flash_delta", grid=(NH, t // tq),
        in_specs=[pl.BlockSpec((tq, HD), lambda h, i: (i, h)), pl.BlockSpec((tq, HD), lambda h, i: (i, h))],
        out_specs=pl.BlockSpec((1, 1, tq), lambda h, i: (h, 0, i)),
        out_shape=jax.ShapeDtypeStruct((NH, 1, t), F32), compiler_params=_params(("parallel", "parallel")),
    )(o, do)


def _flash_bwd(q, k, v, do, lse_row, delta_row, t):
    tk = _pick(t, ATT_TILES)
    nq = t // tk

    def body(q_ref, k_ref, v_ref, do_ref, lse_ref, delta_ref, dq_ref, dk_ref, dv_ref):
        j = pl.program_id(1)

        @pl.when(j == 0)
        def _():
            dq_ref[...] = jnp.zeros_like(dq_ref)

        kt, vt = k_ref[0], v_ref[0]
        dk_ref[...] = jnp.zeros_like(dk_ref)
        dv_ref[...] = jnp.zeros_like(dv_ref)

        def tile(i):
            return pl.ds(i * tk if isinstance(i, int) else pl.multiple_of(i * tk, tk), tk)

        def products(i):
            return _dot_nt(kt, q_ref[0, tile(i), :]), _dot_nt(vt, do_ref[tile(i), :])

        def consume(i, prod, masked):
            rows = tile(i)
            p = jnp.exp(prod[0] - lse_ref[0, :, rows])
            if masked:
                p = jnp.where(_diag_mask(tk, True), p, 0.0)
            ds = (p * (prod[1] - delta_ref[0, :, rows])).astype(BF16)
            dq_ref[0, rows, :] += _dot_tn(ds, kt)
            dk_ref[0] += _dot(ds, q_ref[0, rows, :])
            dv_ref[0] += _dot(p, do_ref[rows, :])

        def step(i, prod):
            nxt = products(i + 1)
            consume(i, prod, False)
            return nxt

        first = products(j)
        nxt = products(jnp.minimum(j + 1, nq - 1))
        consume(j, first, True)
        last = lax.fori_loop(j + 1, nq - 1, step, nxt)

        @pl.when(j < nq - 1)
        def _():
            consume(nq - 1, last, False)

    return pl.pallas_call(
        body, name="flash_bwd", grid=(NH, nq),
        in_specs=[pl.BlockSpec((1, t, QK), lambda h, j: (h, 0, 0)), pl.BlockSpec((1, tk, QK), lambda h, j: (h, j, 0)),
                  pl.BlockSpec((1, tk, HD), lambda h, j: (h, j, 0)), pl.BlockSpec((t, HD), lambda h, j: (0, h)),
                  pl.BlockSpec((1, 1, t), lambda h, j: (h, 0, 0)), pl.BlockSpec((1, 1, t), lambda h, j: (h, 0, 0))],
        out_specs=[pl.BlockSpec((1, t, QK), lambda h, j: (h, 0, 0)), pl.BlockSpec((1, tk, QK), lambda h, j: (h, j, 0)),
                   pl.BlockSpec((1, tk, HD), lambda h, j: (h, j, 0))],
        out_shape=[jax.ShapeDtypeStruct((NH, t, QK), F32), jax.ShapeDtypeStruct((NH, t, QK), F32),
                   jax.ShapeDtypeStruct((NH, t, HD), F32)],
        compiler_params=_params(("parallel", "arbitrary")),
    )(q, k, v, do, lse_row, delta_row)


def _ffn_in(xn, w_in, name):
    t, k = xn.shape
    s, _, n = w_in.shape
    tm = _pick(t, ROW_TILES)

    def body(x_ref, w_ref, gu_ref, a_ref):
        x = x_ref[...]
        g, u = _dot(x, w_ref[0]), _dot(x, w_ref[1])
        gu_ref[0] = g.astype(BF16)
        gu_ref[1] = u.astype(BF16)
        a_ref[...] = (g * _sig(g) * u).astype(BF16)

    return pl.pallas_call(
        body, name=name, grid=(t // tm, s // 2),
        in_specs=[_BS((tm, k), lambda i, j: (i, 0)), _BS((2, None, k, n), lambda i, j: (0, j, 0, 0))],
        out_specs=[_BS((2, None, tm, n), lambda i, j: (0, j, i, 0)), _BS((None, tm, n), lambda i, j: (j, i, 0))],
        out_shape=[jax.ShapeDtypeStruct((2, s // 2, t, n), BF16), jax.ShapeDtypeStruct((s // 2, t, n), BF16)],
        compiler_params=_params(("parallel", "parallel")),
    )(xn, w_in.reshape(2, s // 2, k, n))


def _ffn_dgu(dfo, w_out, gu, name):
    t, k = dfo.shape
    s, n, _ = w_out.shape
    tm = _pick(t, ROW_TILES)

    def body(d_ref, w_ref, gu_ref, o_ref):
        da = _dot_nt(d_ref[...], w_ref[...])
        g, u = gu_ref[0].astype(F32), gu_ref[1].astype(F32)
        sg = _sig(g)
        o_ref[0] = (da * u * sg * (1.0 + g * (1.0 - sg))).astype(BF16)
        o_ref[1] = (da * g * sg).astype(BF16)

    pair = _BS((2, None, tm, n), lambda i, j: (0, j, i, 0))
    return pl.pallas_call(
        body, name=name, grid=(t // tm, s),
        in_specs=[_BS((tm, k), lambda i, j: (i, 0)), _BS((None, n, k), lambda i, j: (j, 0, 0)), pair],
        out_specs=pair, out_shape=jax.ShapeDtypeStruct((2, s, t, n), BF16),
        compiler_params=_params(("parallel", "parallel")),
    )(dfo, w_out, gu).reshape(2 * s, t, n)


def _ffn_fwd(xn, w_in, w_out, tag):
    gu, a = _ffn_in(xn, w_in, tag + "_in")
    return gu, a, _mm_stack_red(a, w_out, tag + "_out", "nn")


def _ffn_bwd(dfo, xn, gu, a, w_in, w_out, tag):
    dw_out = _mm_stack_tn(a, dfo, tag + "_dwout")
    dgu = _ffn_dgu(dfo, w_out, gu, tag + "_dgu")
    return _mm_stack_red(dgu, w_in, tag + "_dxn", "nt"), _mm_stack_tn(xn, dgu, tag + "_dwin"), dw_out


def _local_step(x, target, cs, sn, w, s):
    t = x.shape[0]
    tm = _pick(t, (256, 128))
    g = {}
    rows_of = lambda n: w[n].reshape(-1, w[n].shape[2])
    ffn_out = lambda n: w[n].reshape(4, FF // 4, D)
    w_hgb, w_mlab, w_o = rows_of("w_hg_branch"), rows_of("w_mla_branch"), rows_of("w_out")
    w_in_nat = w["w_in"].transpose(1, 0, 2).reshape(D, -1)
    w_mrg = w["w_merge"]
    mw = w_mrg.shape[2]

    def norm_fn(x_ref, g_ref, o_ref):
        o_ref[...] = _rms(x_ref[...], g_ref[...]).astype(BF16)

    xn1 = _rows(norm_fn, "norm1", t, tm, [(x, D, 0)], [s["ffn1_norm"]], [(D, BF16)])[0]
    gu1, a1, f1 = _ffn_fwd(xn1, w["ffn1_w_in"], ffn_out("ffn1_w_out"), "ffn1")

    def res_norm_fn(scale):
        def fn(h_ref, f_ref, g_ref, h_out, n_out):
            h = h_ref[...] + scale * f_ref[...]
            h_out[...] = h
            n_out[...] = _rms(h, g_ref[...]).astype(BF16)
        return fn

    h1, u = _rows(res_norm_fn(0.5), "res_norm1", t, tm, [(x, D, 0), (f1, D, 0)], [s["mix_norm"]],
                  [(D, F32), (D, BF16)])
    w_in_hg, w_in_mla = w_in_nat[:, :4 * HGW], w_in_nat[:, 4 * HGW:]
    p_hg = _mm(u, w_in_hg, "nn", "proj_hg")
    p_mla = _mm(u, w_in_mla, "nn", "proj_mla")
    gpre = _cols_fwd(u, w_mrg, "proj_gate")
    o_pre, hgy, states = _hgrn_fwd(p_hg, s["hg_lb_table"], s["hg_out_norm"], t)
    prep_args = (p_mla, cs, sn, w["w_q_up"], w["w_kv_up"], s["mla_q_lora_norm"], s["mla_kv_lora_norm"],
                 s["q_head_norm"], s["k_head_norm"])
    q, k, v = _mla_prep_fwd(*prep_args, t)
    att, lse = _flash_fwd(q, k, v, t)
    y_hg = _mm(hgy, w_hgb, "nn", "branch_hg")
    y_mla = _mm(att, w_mlab, "nn", "branch_mla")

    def mix_fn(gh_ref, gm_ref, yh_ref, ym_ref, b_ref, o_ref):
        gh = _sig(gh_ref[...] + b_ref[:, 0:D])
        gm = _sig(gm_ref[...] + b_ref[:, D:2 * D])
        o_ref[...] = (gh * yh_ref[...] + gm * ym_ref[...]).astype(BF16)

    mixed = _rows(mix_fn, "mix", t, tm, [(gpre, D, 0), (gpre, D, 1), (y_hg, D, 0), (y_mla, D, 0)], [s["b_merge"]],
                  [(D, BF16)])[0]
    mo = _mm(mixed, w_o, "nn", "mix_out")
    h2, xn2 = _rows(res_norm_fn(1.0), "res_norm2", t, tm, [(h1, D, 0), (mo, D, 0)], [s["ffn2_norm"]],
                    [(D, F32), (D, BF16)])
    gu2, a2, f2 = _ffn_fwd(xn2, w["ffn2_w_in"], ffn_out("ffn2_w_out"), "ffn2")

    def loss_fn(h_ref, f_ref, tg_ref, g_ref, dh_out, dhb_out):
        h = h_ref[...] + 0.5 * f_ref[...]
        e = _rms(h, g_ref[...]) - tg_ref[...]
        dh, dgain = _rms_bwd(h, g_ref[...], e / D)
        dh_out[...] = dh
        dhb_out[...] = (0.5 * dh).astype(BF16)
        return dgain, jnp.full((1, LANE), 0.5 / D * jnp.sum(e * e), F32)

    dh3, dfo2, g["final_norm"], loss = _rows(loss_fn, "loss", t, tm, [(h2, D, 0), (f2, D, 0), (target, D, 0)],
                                             [s["final_norm"]], [(D, F32), (D, BF16)], [(1, D), (1, LANE)])

    def norm_bwd_fn(scale):
        def fn(h_ref, dxn_ref, dh_ref, g_ref, dh_out, dhb_out):
            dx, dgain = _rms_bwd(h_ref[...], g_ref[...], dxn_ref[...])
            dh = dh_ref[...] + dx
            dh_out[...] = dh
            dhb_out[...] = (scale * dh).astype(BF16)
            return (dgain,)
        return fn

    as_rows = lambda a: a.reshape((N_DEV, -1) + a.shape[-1:])
    dxn2, g["ffn2_w_in"], dwo = _ffn_bwd(dfo2, xn2, gu2, a2, w["ffn2_w_in"], ffn_out("ffn2_w_out"), "ffn2")
    g["ffn2_w_out"] = as_rows(dwo)
    dh2, dh2b, g["ffn2_norm"] = _rows(norm_bwd_fn(1.0), "norm2_bwd", t, tm, [(h2, D, 0), (dxn2, D, 0), (dh3, D, 0)],
                                      [s["ffn2_norm"]], [(D, F32), (D, BF16)], [(1, D)])
    dmixed = _mm(dh2b, w_o, "nt", "mix_out_dx")
    g["w_out"] = as_rows(_mm(mixed, dh2b, "tn", "mix_out_dw"))

    def mix_bwd_fn(gh_ref, gm_ref, yh_ref, ym_ref, dm_ref, b_ref, dyh_out, dym_out, dg_out):
        gh = _sig(gh_ref[...] + b_ref[:, 0:D])
        gm = _sig(gm_ref[...] + b_ref[:, D:2 * D])
        dm = dm_ref[...]
        dyh_out[...] = (dm * gh).astype(BF16)
        dym_out[...] = (dm * gm).astype(BF16)
        dgh = dm * yh_ref[...] * gh * (1.0 - gh)
        dgm = dm * ym_ref[...] * gm * (1.0 - gm)
        dg_out[:, 0:D] = dgh.astype(BF16)
        dg_out[:, D:2 * D] = dgm.astype(BF16)
        return (jnp.concatenate([jnp.sum(dgh, axis=0, keepdims=True), jnp.sum(dgm, axis=0, keepdims=True)], axis=1),)

    dyh, dym, dgpre, g["b_merge"] = _rows(
        mix_bwd_fn, "mix_bwd", t, tm, [(gpre, D, 0), (gpre, D, 1), (y_hg, D, 0), (y_mla, D, 0), (dmixed, D, 0)],
        [s["b_merge"]], [(D, BF16), (D, BF16), (2 * D, BF16)], [(1, 2 * D)])
    g["w_hg_branch"] = as_rows(_mm(hgy, dyh, "tn", "branch_hg_dw"))
    g["w_mla_branch"] = as_rows(_mm(att, dym, "tn", "branch_mla_dw"))
    g["w_merge"] = _cols_dw(u, dgpre, mw, "proj_gate_dw")
    dhgy = _mm(dyh, w_hgb, "nt", "branch_hg_dx")
    datt = _mm(dym, w_mlab, "nt", "branch_mla_dx", out_dtype=BF16)
    du_gate = _cols_dx(dgpre, w_mrg, "proj_gate_dx")

    dq, dk, dv = _flash_bwd(q, k, v, datt, lse.reshape(NH, 1, t), _flash_delta(att, datt, t), t)
    (dp_mla, g["w_q_up"], g["w_kv_up"], g["mla_q_lora_norm"], g["mla_kv_lora_norm"], g["q_head_norm"],
     g["k_head_norm"]) = _mla_prep_bwd(*prep_args, dq, dk, dv, t)
    dp_hg, g["hg_lb_table"], g["hg_out_norm"] = _hgrn_bwd(p_hg, s["hg_lb_table"], s["hg_out_norm"], o_pre, states,
                                                          dhgy, t)
    dw_in_nat = jnp.concatenate([_mm(u, dp_hg, "tn", "proj_hg_dw"), _mm(u, dp_mla, "tn", "proj_mla_dw")], axis=1)
    g["w_in"] = dw_in_nat.reshape(D, N_DEV, -1).transpose(1, 0, 2)
    du_hg = _mm(dp_hg, w_in_hg, "nt", "proj_hg_dx")
    du_mla = _mm(dp_mla, w_in_mla, "nt", "proj_mla_dx")

    def mixnorm_bwd_fn(h_ref, a_ref, b_ref, c_ref, dh_ref, g_ref, dh_out, dhb_out):
        dx, dgain = _rms_bwd(h_ref[...], g_ref[...], a_ref[...] + b_ref[...] + c_ref[...])
        dh = dh_ref[...] + dx
        dh_out[...] = dh
        dhb_out[...] = (0.5 * dh).astype(BF16)
        return (dgain,)

    dh1, dfo1, g["mix_norm"] = _rows(mixnorm_bwd_fn, "mixnorm_bwd", t, tm,
                                     [(h1, D, 0), (du_hg, D, 0), (du_mla, D, 0), (du_gate, D, 0), (dh2, D, 0)],
                                     [s["mix_norm"]], [(D, F32), (D, BF16)], [(1, D)])
    dxn1, g["ffn1_w_in"], dwo = _ffn_bwd(dfo1, xn1, gu1, a1, w["ffn1_w_in"], ffn_out("ffn1_w_out"), "ffn1")
    g["ffn1_w_out"] = as_rows(dwo)
    grad_x, _, g["ffn1_norm"] = _rows(norm_bwd_fn(1.0), "norm1_bwd", t, tm, [(x, D, 0), (dxn1, D, 0), (dh1, D, 0)],
                                      [s["ffn1_norm"]], [(D, F32), (D, BF16)], [(1, D)])
    return loss, grad_x, g


def _coords():
    return lax.axis_index("x"), lax.axis_index("y"), lax.axis_index("c")


def _hbm_call(body, name, ins, out_shapes, scratch):
    any_spec = pl.BlockSpec(memory_space=pl.ANY)
    return pl.pallas_call(
        body, name=name, out_shape=[jax.ShapeDtypeStruct(s, dt) for s, dt in out_shapes],
        in_specs=[any_spec] * len(ins), out_specs=[any_spec] * len(out_shapes), scratch_shapes=scratch,
    )(*ins)


def _all_gather(blocks, name):
    nb = len(blocks)

    def body(*refs):
        x_refs, out_refs = refs[:nb], refs[nb:2 * nb]
        send_sems, recv_sems, local_sems = refs[2 * nb:]
        x, y, c = _coords()
        me, sibling = (x, y, c), (x, y, 1 - c)
        chips = [(1 - x, y), (x, 1 - y), (1 - x, 1 - y)]

        def slot(b, px, py, pc):
            return out_refs[b].at[4 * px + 2 * py + pc]

        def copy(b, kk, block_of, to, src=None):
            return pltpu.make_async_remote_copy(
                src_ref=slot(b, *block_of) if src is None else src, dst_ref=slot(b, *block_of),
                send_sem=send_sems.at[b, kk], recv_sem=recv_sems.at[b, kk], device_id=to, device_id_type=MESH)

        mine = [pltpu.make_async_copy(x_refs[b], slot(b, *me), local_sems.at[b]) for b in range(nb)]
        for cp in mine:
            cp.start()
        first = [copy(b, 0, me, sibling, src=x_refs[b]) for b in range(nb)]
        first += [copy(b, 1 + j, me, (*chip, c), src=x_refs[b]) for j, chip in enumerate(chips) for b in range(nb)]
        for cp in first:
            cp.start()
        passed = []
        for j, chip in enumerate(chips):
            for b in range(nb):
                copy(b, 1 + j, (*chip, c), me).wait_recv()
                passed.append(copy(b, 4 + j, (*chip, c), sibling))
                passed[-1].start()
        for b in range(nb):
            copy(b, 0, sibling, me).wait_recv()
        for j, chip in enumerate(chips):
            for b in range(nb):
                copy(b, 4 + j, (*chip, 1 - c), me).wait_recv()
        for cp in first + passed:
            cp.wait_send()
        for cp in mine:
            cp.wait()

    return _hbm_call(body, name, blocks, [((N_DEV,) + b.shape, b.dtype) for b in blocks],
                     [pltpu.SemaphoreType.DMA((nb, 7)), pltpu.SemaphoreType.DMA((nb, 7)),
                      pltpu.SemaphoreType.DMA((nb,))])


def _sibling_swap(bufs, name):
    nb = len(bufs)

    def body(*refs):
        x_refs, out_refs = refs[:nb], refs[nb:2 * nb]
        send_sems, recv_sems = refs[2 * nb:]
        x, y, c = _coords()
        copies = [pltpu.make_async_remote_copy(
            src_ref=x_refs[b].at[2 * q + 1 - c], dst_ref=out_refs[b].at[q], send_sem=send_sems.at[b, q],
            recv_sem=recv_sems.at[b, q], device_id=(x, y, 1 - c), device_id_type=MESH)
            for b in range(nb) for q in range(4)]
        for cp in copies:
            cp.start()
        for cp in copies:
            cp.wait()

    return _hbm_call(body, name, bufs, [((4,) + b.shape[1:], b.dtype) for b in bufs],
                     [pltpu.SemaphoreType.DMA((nb, 4)), pltpu.SemaphoreType.DMA((nb, 4))])


def _chip_exchange(bufs, name):
    nb = len(bufs)

    def body(*refs):
        x_refs, out_refs = refs[:nb], refs[nb:2 * nb]
        send_sems, recv_sems, local_sems = refs[2 * nb:]
        x, y, c = _coords()
        chips = [(1 - x, y), (x, 1 - y), (1 - x, 1 - y)]
        mine = [pltpu.make_async_copy(x_refs[b].at[2 * x + y], out_refs[b].at[2 * x + y], local_sems.at[b])
                for b in range(nb)]
        for cp in mine:
            cp.start()
        copies = [pltpu.make_async_remote_copy(
            src_ref=x_refs[b].at[2 * px + py], dst_ref=out_refs[b].at[2 * x + y], send_sem=send_sems.at[b, j],
            recv_sem=recv_sems.at[b, j], device_id=(px, py, c), device_id_type=MESH)
            for j, (px, py) in enumerate(chips) for b in range(nb)]
        for cp in copies:
            cp.start()
        for j, (px, py) in enumerate(chips):
            for b in range(nb):
                pltpu.make_async_remote_copy(
                    src_ref=x_refs[b].at[2 * x + y], dst_ref=out_refs[b].at[2 * px + py], send_sem=send_sems.at[b, j],
                    recv_sem=recv_sems.at[b, j], device_id=(px, py, c), device_id_type=MESH).wait_recv()
        for cp in copies:
            cp.wait_send()
        for cp in mine:
            cp.wait()

    return _hbm_call(body, name, bufs, [(b.shape, b.dtype) for b in bufs],
                     [pltpu.SemaphoreType.DMA((nb, 3)), pltpu.SemaphoreType.DMA((nb, 3)),
                      pltpu.SemaphoreType.DMA((nb,))])


def _chip_sum(g, r1, c, name):
    _, r, cw = g.shape
    tr = _pick(r, (256, 176, 128))

    def body(c_ref, g_ref, r_ref, o_ref):
        o_ref[...] = (g_ref[...] + r_ref[...]).astype(BF16)

    grid_spec = pltpu.PrefetchScalarGridSpec(
        num_scalar_prefetch=1, grid=(4, r // tr),
        in_specs=[_BS((None, None, tr, cw), lambda q, i, c_ref: (q, c_ref[0], i, 0)),
                  _BS((None, tr, cw), lambda q, i, c_ref: (q, i, 0))],
        out_specs=_BS((None, tr, cw), lambda q, i, c_ref: (q, i, 0)))
    return pl.pallas_call(
        body, name=name, grid_spec=grid_spec, out_shape=jax.ShapeDtypeStruct((4, r, cw), BF16),
        compiler_params=_params(("parallel", "parallel")),
    )(c.reshape(1).astype(jnp.int32), g.reshape(4, 2, r, cw), r1)


def _adamw_math(w, g, m, v):
    m = B1 * m + (1.0 - B1) * g
    v = B2 * v + (1.0 - B2) * (g * g)
    m_hat = m / (1.0 - B1 ** STEP)
    v_hat = v / (1.0 - B2 ** STEP)
    return -LR * (m_hat / (jnp.sqrt(v_hat) + AEPS) + WD * w), m, v


def _sum_adamw(parts, w, m, v, name):
    r, c = w.shape
    tr = _pick(r, (256, 176, 128))

    def body(p0, p1, p2, p3, w_ref, m_ref, v_ref, g_out, d_out, m_out, v_out):
        g = ((p0[...].astype(F32) + p1[...].astype(F32)) + p2[...].astype(F32)) + p3[...].astype(F32)
        g_out[...] = g
        d_out[...], m_out[...], v_out[...] = _adamw_math(w_ref[...], g, m_ref[...], v_ref[...])

    part = lambda q: _BS((None, tr, c), functools.partial(lambda i, q: (q, i, 0), q=q))
    plain = _BS((tr, c), lambda i: (i, 0))
    return pl.pallas_call(
        body, name=name, grid=(r // tr,), in_specs=[part(q) for q in range(4)] + [plain] * 3,
        out_specs=[plain] * 4, out_shape=[jax.ShapeDtypeStruct((r, c), F32)] * 4,
        compiler_params=_params(("parallel",)),
    )(parts, parts, parts, parts, w, m, v)


def _small_update(gathered, w, m, v):
    r = w.shape[0]

    def body(ga_ref, w_ref, m_ref, v_ref, g_out, d_out, m_out, v_out):
        g = ga_ref[0]
        for dev in range(1, N_DEV):
            g = g + ga_ref[dev]
        g_out[...] = g
        d_out[...], m_out[...], v_out[...] = _adamw_math(w_ref[...], g, m_ref[...], v_ref[...])

    return pl.pallas_call(
        body, name="small_update", out_shape=[jax.ShapeDtypeStruct((r, LANE), F32)] * 4,
    )(gathered, w, m, v)


def _pack_small(vals):
    rows = []
    for name, (r, n) in SMALL:
        flat = vals[name].reshape(-1)
        pad = (-flat.shape[0]) % LANE
        rows.append(jnp.pad(flat, (0, pad)).reshape(-1, LANE))
    return jnp.concatenate(rows, axis=0)


def _unpack_small(packed):
    out, off = {}, 0
    for name, (r, n) in SMALL:
        nrow = (r * n + LANE - 1) // LANE
        out[name] = packed[off:off + nrow].reshape(-1)[:r * n].reshape(r, n)
        off += nrow
    return out


def kernel(x, positions, ffn1_norm, ffn1_w_in, ffn1_w_out, mix_norm, w_in, hg_lb_table, hg_out_norm, w_hg_branch, mla_q_lora_norm, w_q_up, mla_kv_lora_norm, w_kv_up, q_head_norm, k_head_norm, w_mla_branch, w_merge, b_merge, w_out, ffn2_norm, ffn2_w_in, ffn2_w_out, final_norm, loss_target, m_ffn1_norm, m_ffn1_w_in, m_ffn1_w_out, m_mix_norm, m_w_in, m_hg_lb_table, m_hg_out_norm, m_w_hg_branch, m_mla_q_lora_norm, m_w_q_up, m_mla_kv_lora_norm, m_w_kv_up, m_q_head_norm, m_k_head_norm, m_w_mla_branch, m_w_merge, m_b_merge, m_w_out, m_ffn2_norm, m_ffn2_w_in, m_ffn2_w_out, m_final_norm, v_ffn1_norm, v_ffn1_w_in, v_ffn1_w_out, v_mix_norm, v_w_in, v_hg_lb_table, v_hg_out_norm, v_w_hg_branch, v_mla_q_lora_norm, v_w_q_up, v_mla_kv_lora_norm, v_w_kv_up, v_q_head_norm, v_k_head_norm, v_w_mla_branch, v_w_merge, v_b_merge, v_w_out, v_ffn2_norm, v_ffn2_w_in, v_ffn2_w_out, v_final_norm):
    args = dict(locals())
    t = x.shape[1]
    big_w = {n: args[n][0] for n, _, _ in BIG}
    small = {n: args[n].reshape(shape) for n, shape in SMALL}

    names = [n for n, _, _ in BIG]
    full = dict(zip(names, _all_gather([big_w[n].astype(BF16) for n in names], "weights_all_gather")))

    inv_freq = ROPE_THETA ** (-jnp.arange(0, ROPE, 2, dtype=F32) / ROPE)
    ang = positions[0].astype(F32)[:, None] * inv_freq
    cs = jnp.concatenate([jnp.cos(ang), jnp.cos(ang)], axis=1)
    sn = jnp.concatenate([jnp.sin(ang), jnp.sin(ang)], axis=1)

    loss_row, grad_x, g = _local_step(x[0], loss_target[0], cs, sn, full, small)

    c = lax.axis_index("c")
    from_sibling = _sibling_swap([g[n] for n in names], "grads_sibling_swap")
    chip_sums = [_chip_sum(g[n], r1, c, "chip_sum_" + n) for n, r1 in zip(names, from_sibling)]
    exchanged = dict(zip(names, _chip_exchange(chip_sums, "grads_chip_exchange")))

    small_packed = jnp.concatenate([_pack_small(g), loss_row], axis=0)
    pad = (-small_packed.shape[0]) % 8
    small_packed = jnp.pad(small_packed, ((0, pad), (0, 0)))
    small_all = _all_gather([small_packed], "small_all_gather")[0]
    zero_tail = jnp.zeros((small_packed.shape[0] - _pack_small(small).shape[0], LANE), F32)
    pk = lambda d: jnp.concatenate([_pack_small(d), zero_tail], axis=0)
    sg, sd, sm, sv = _small_update(
        small_all, pk(small), pk({n: args["m_" + n].reshape(shape) for n, shape in SMALL}),
        pk({n: args["v_" + n].reshape(shape) for n, shape in SMALL}))
    n_small_rows = _pack_small(small).shape[0]
    loss = sg[n_small_rows, 0]
    outs = {k_: _unpack_small(a) for k_, a in (("grad", sg), ("delta", sd), ("new_m", sm), ("new_v", sv))}

    for n in names:
        outs["grad"][n], outs["delta"][n], outs["new_m"][n], outs["new_v"][n] = _sum_adamw(
            exchanged[n], big_w[n], args["m_" + n][0], args["v_" + n][0], "adamw_" + n)

    def shaped(kind, n):
        return outs[kind][n].reshape(args[n].shape)

    return (loss, grad_x[None], *[shaped("grad", n) for n in WEIGHT_ORDER], *[shaped("delta", n) for n in WEIGHT_ORDER],
            *[shaped("new_m", n) for n in WEIGHT_ORDER], *[shaped("new_v", n) for n in WEIGHT_ORDER])
```

```python
import functools

import jax
import jax.numpy as jnp
from jax import lax
from jax.experimental import pallas as pl
from jax.experimental.pallas import tpu as pltpu

F32 = jnp.float32
BF16 = jnp.bfloat16

D = 1024
FF = 2816
NH = 8
HD = 128
ROPE = 64
QK = HD + ROPE
QL = 384
KVL = 256
HGW = NH * HD
CHUNK = 64
EPS = 1e-6
ROPE_THETA = 10000.0
SCALE = QK ** -0.5

LR, B1, B2, AEPS, WD, STEP = 0.001, 0.9, 0.999, 1e-08, 0.01, 10

HB = 128
SUB = 16
EXP_CLAMP = 80.0
ATT_TILES = (512, 256, 128)
ATT_Q = (1024, 512, 256, 128)
ROW_TILES = (1024, 512, 256, 128)

LANE = 128
VMEM_LIMIT = 56 << 20

N_DEV = 8
MESH = pl.DeviceIdType.MESH

BIG = (
    ("ffn1_w_in", (D, 2 * FF), 1), ("ffn1_w_out", (FF, D), 0), ("w_in", (D, 4800), 1),
    ("w_hg_branch", (HGW, D), 0), ("w_q_up", (QL, NH * QK), 1), ("w_kv_up", (KVL, NH * 2 * HD), 1),
    ("w_mla_branch", (NH * HD, D), 0), ("w_merge", (D, 2 * D), 1), ("w_out", (D, D), 0),
    ("ffn2_w_in", (D, 2 * FF), 1), ("ffn2_w_out", (FF, D), 0),
)
SMALL = (
    ("ffn1_norm", (1, D)), ("mix_norm", (1, D)), ("hg_lb_table", (2, HGW)), ("hg_out_norm", (1, HD)),
    ("mla_q_lora_norm", (1, QL)), ("mla_kv_lora_norm", (1, KVL)), ("q_head_norm", (1, QK)),
    ("k_head_norm", (1, QK)), ("b_merge", (1, 2 * D)), ("ffn2_norm", (1, D)), ("final_norm", (1, D)),
)
WEIGHT_ORDER = ("ffn1_norm", "ffn1_w_in", "ffn1_w_out", "mix_norm", "w_in", "hg_lb_table", "hg_out_norm",
                "w_hg_branch", "mla_q_lora_norm", "w_q_up", "mla_kv_lora_norm", "w_kv_up", "q_head_norm",
                "k_head_norm", "w_mla_branch", "w_merge", "b_merge", "w_out", "ffn2_norm", "ffn2_w_in",
                "ffn2_w_out", "final_norm")


def _pick(n, cands):
    for c in cands:
        if n % c == 0:
            return c
    return n


def _params(sem):
    return pltpu.CompilerParams(dimension_semantics=sem, vmem_limit_bytes=VMEM_LIMIT)


def _sig(x):
    return 1.0 / (1.0 + jnp.exp(-x))


def _dot(a, b):
    return jnp.dot(a.astype(BF16), b.astype(BF16), preferred_element_type=F32)


def _dot_nt(a, b):
    return lax.dot_general(a.astype(BF16), b.astype(BF16), (((1,), (1,)), ((), ())),
                           preferred_element_type=F32)


def _dot_tn(a, b):
    return lax.dot_general(a.astype(BF16), b.astype(BF16), (((0,), (0,)), ((), ())),
                           preferred_element_type=F32)


def _split3(x):
    x1 = x.astype(BF16)
    r1 = x - x1.astype(F32)
    x2 = r1.astype(BF16)
    x3 = (r1 - x2.astype(F32)).astype(BF16)
    return x1, x2, x3


def _dot_sel(m, x):
    x1, x2, x3 = _split3(x)
    d = lambda p: jnp.dot(m, p, preferred_element_type=F32)
    return d(x1) + d(x2) + d(x3)


def _sel_dot(x, m):
    x1, x2, x3 = _split3(x)
    d = lambda p: jnp.dot(p, m, preferred_element_type=F32)
    return d(x1) + d(x2) + d(x3)


_TN = (1408, 1024, 768, 512, 384, 256, 128)


def _mm(a, b, mode, name, out_dtype=F32):
    if mode == "tn":
        t, m = a.shape
        n = b.shape[1]
        tt, tm, tn = _pick(t, (512, 256, 128)), _pick(m, _TN), _pick(n, _TN)

        def body(a_ref, b_ref, o_ref):
            @pl.when(pl.program_id(2) == 0)
            def _():
                o_ref[...] = jnp.zeros_like(o_ref)

            o_ref[...] += _dot_tn(a_ref[...], b_ref[...])

        return pl.pallas_call(
            body, name=name, grid=(m // tm, n // tn, t // tt),
            in_specs=[pl.BlockSpec((tt, tm), lambda i, j, k: (k, i)),
                      pl.BlockSpec((tt, tn), lambda i, j, k: (k, j))],
            out_specs=pl.BlockSpec((tm, tn), lambda i, j, k: (i, j)),
            out_shape=jax.ShapeDtypeStruct((m, n), F32),
            compiler_params=_params(("parallel", "parallel", "arbitrary")),
        )(a, b)

    m, k = a.shape
    tm = _pick(m, ROW_TILES)
    if mode == "nn":
        n = b.shape[1]
        tn = _pick(n, _TN)
        b_spec = pl.BlockSpec((k, tn), lambda i, j: (0, j))
        dot = _dot
    else:
        n = b.shape[0]
        tn = _pick(n, _TN if k <= 4096 else (512, 256, 128))
        b_spec = pl.BlockSpec((tn, k), lambda i, j: (j, 0))
        dot = _dot_nt

    def body(a_ref, b_ref, o_ref):
        o_ref[...] = dot(a_ref[...], b_ref[...]).astype(o_ref.dtype)

    return pl.pallas_call(
        body, name=name, grid=(m // tm, n // tn),
        in_specs=[pl.BlockSpec((tm, k), lambda i, j: (i, 0)), b_spec],
        out_specs=pl.BlockSpec((tm, tn), lambda i, j: (i, j)),
        out_shape=jax.ShapeDtypeStruct((m, n), out_dtype),
        compiler_params=_params(("parallel", "parallel")),
    )(a, b)


_DOTS = {"nn": _dot, "nt": _dot_nt, "tn": _dot_tn}
_BS = pl.BlockSpec


def _mmcall(name, kind, a, b, a_spec, b_spec, o_spec, o_shape, grid, red_axis=None, out_dtype=F32):
    dot = _DOTS[kind]

    def body(a_ref, b_ref, o_ref):
        if red_axis is None:
            o_ref[...] = dot(a_ref[...], b_ref[...]).astype(o_ref.dtype)
        else:
            @pl.when(pl.program_id(red_axis) == 0)
            def _():
                o_ref[...] = jnp.zeros_like(o_ref)

            o_ref[...] += dot(a_ref[...], b_ref[...])

    sem = tuple("arbitrary" if ax == red_axis else "parallel" for ax in range(len(grid)))
    return pl.pallas_call(
        body, name=name, grid=grid, in_specs=[a_spec, b_spec], out_specs=o_spec,
        out_shape=jax.ShapeDtypeStruct(o_shape, out_dtype), compiler_params=_params(sem),
    )(a, b)


def _mm_stack_red(a, w, name, kind):
    s, t, n = a.shape
    nout = w.shape[2] if kind == "nn" else w.shape[1]
    tm = _pick(t, ROW_TILES)
    return _mmcall(name, kind, a, w, _BS((None, tm, n), lambda i, j: (j, i, 0)),
                   _BS((None,) + w.shape[1:], lambda i, j: (j, 0, 0)), _BS((tm, nout), lambda i, j: (i, 0)),
                   (t, nout), (t // tm, s), red_axis=1)


def _mm_stack_tn(a, b, name):
    if a.ndim == 2:
        t, k = a.shape
        s, _, n = b.shape
        a_spec, b_spec, o_shape = (lambda tt: _BS((tt, k), lambda j, r: (r, 0))), \
            (lambda tt: _BS((None, tt, n), lambda j, r: (j, r, 0))), (s, k, n)
    else:
        s, t, k = a.shape
        n = b.shape[1]
        a_spec, b_spec, o_shape = (lambda tt: _BS((None, tt, k), lambda j, r: (j, r, 0))), \
            (lambda tt: _BS((tt, n), lambda j, r: (r, 0))), (s, k, n)
    tt = _pick(t, (512, 256, 128))
    return _mmcall(name, "tn", a, b, a_spec(tt), b_spec(tt), _BS((None,) + o_shape[1:], lambda j, r: (j, 0, 0)),
                   o_shape, (s, t // tt), red_axis=1)


def _cols_fwd(x, w, name):
    t, k = x.shape
    s, _, n = w.shape
    tm = _pick(t, ROW_TILES)

    def body(x_ref, w_ref, o_ref):
        x_ = x_ref[...]
        for j in range(s):
            o_ref[:, n * j:n * (j + 1)] = _dot(x_, w_ref[j])

    return pl.pallas_call(
        body, name=name, grid=(t // tm,),
        in_specs=[_BS((tm, k), lambda i: (i, 0)), _BS((s, k, n), lambda i: (0, 0, 0))],
        out_specs=_BS((tm, s * n), lambda i: (i, 0)), out_shape=jax.ShapeDtypeStruct((t, s * n), F32),
        compiler_params=_params(("parallel",)),
    )(x, w)


def _cols_dx(d, w, name):
    t = d.shape[0]
    s, k, n = w.shape
    tm = _pick(t, ROW_TILES)

    def body(d_ref, w_ref, o_ref):
        acc = _dot_nt(d_ref[:, 0:n], w_ref[0])
        for j in range(1, s):
            acc = acc + _dot_nt(d_ref[:, n * j:n * (j + 1)], w_ref[j])
        o_ref[...] = acc

    return pl.pallas_call(
        body, name=name, grid=(t // tm,),
        in_specs=[_BS((tm, s * n), lambda i: (i, 0)), _BS((s, k, n), lambda i: (0, 0, 0))],
        out_specs=_BS((tm, k), lambda i: (i, 0)), out_shape=jax.ShapeDtypeStruct((t, k), F32),
        compiler_params=_params(("parallel",)),
    )(d, w)


def _cols_dw(x, d, n, name):
    t, k = x.shape
    s = d.shape[1] // n
    tt = _pick(t, (512, 256, 128))

    def body(x_ref, d_ref, o_ref):
        @pl.when(pl.program_id(0) == 0)
        def _():
            o_ref[...] = jnp.zeros_like(o_ref)

        x_ = x_ref[...]
        for j in range(s):
            o_ref[j] += _dot_tn(x_, d_ref[:, n * j:n * (j + 1)])

    return pl.pallas_call(
        body, name=name, grid=(t // tt,),
        in_specs=[_BS((tt, k), lambda r: (r, 0)), _BS((tt, s * n), lambda r: (r, 0))],
        out_specs=_BS((s, k, n), lambda r: (0, 0, 0)), out_shape=jax.ShapeDtypeStruct((s, k, n), F32),
        compiler_params=_params(("arbitrary",)),
    )(x, d)


def _rows(fn, name, t, tm, ins, vecs, outs, accs=()):
    n_in, n_out, n_acc = len(ins) + len(vecs), len(outs), len(accs)

    def body(*refs):
        res = fn(*refs[:n_in + n_out])
        if n_acc:
            acc_refs = refs[n_in + n_out:]

            @pl.when(pl.program_id(0) == 0)
            def _():
                for r in acc_refs:
                    r[...] = jnp.zeros_like(r)

            for r, val in zip(acc_refs, res):
                r[...] += val

    in_specs = [pl.BlockSpec((tm, bw), functools.partial(lambda i, cb: (i, cb), cb=cb)) for _, bw, cb in ins]
    in_specs += [pl.BlockSpec(v.shape, lambda i: (0, 0)) for v in vecs]
    out_specs = [pl.BlockSpec((tm, w), lambda i: (i, 0)) for w, _ in outs]
    out_specs += [pl.BlockSpec(s, lambda i: (0, 0)) for s in accs]
    out_shape = [jax.ShapeDtypeStruct((t, w), dt) for w, dt in outs]
    out_shape += [jax.ShapeDtypeStruct(s, F32) for s in accs]
    return pl.pallas_call(
        body, name=name, grid=(t // tm,), in_specs=in_specs, out_specs=out_specs, out_shape=out_shape,
        compiler_params=_params(("arbitrary",) if n_acc else ("parallel",)),
    )(*[a for a, _, _ in ins], *vecs)


def _rms(x, g):
    return x * lax.rsqrt(jnp.mean(x * x, axis=-1, keepdims=True) + EPS) * g


def _rms_bwd(x, g, dy):
    xh = x * lax.rsqrt(jnp.mean(x * x, axis=-1, keepdims=True) + EPS)
    r = lax.rsqrt(jnp.mean(x * x, axis=-1, keepdims=True) + EPS)
    dyg = dy * g
    dx = r * (dyg - xh * jnp.mean(dyg * xh, axis=-1, keepdims=True))
    return dx, jnp.sum(dy * xh, axis=0, keepdims=True)


def _hgrn_mats():
    row = lax.broadcasted_iota(jnp.int32, (HB, HB), 0)
    col = lax.broadcasted_iota(jnp.int32, (HB, HB), 1)
    return row, col


def _hgrn_gates(qr, z, t0, t1):
    lb = 1.0 / (1.0 + jnp.exp(t1 - t0))
    sz = _sig(z)
    sneg = 1.0 / (1.0 + jnp.exp(z))
    f = lb + (1.0 - lb) * sz
    return lb, sz, sneg, f, jnp.log(f), (1.0 - lb) * sneg, qr * _sig(qr)


def _hgrn_scores(q, k, cum, cmid):
    qd = q * jnp.exp(jnp.minimum(cmid, EXP_CLAMP))
    qd_b = qd.astype(BF16)
    kds, parts = [], []
    for i in range(HB // SUB):
        mid = cum[SUB * i + SUB // 2 - 1:SUB * i + SUB // 2, :]
        kd = k * jnp.exp(jnp.minimum(mid - cum, EXP_CLAMP))
        kds.append(kd)
        parts.append(_dot_nt(qd_b[SUB * i:SUB * (i + 1)], kd))
    return qd, kds, jnp.concatenate(parts, axis=0)


def _hgrn_fwd(p_hg, table, gain, t):
    nblk = t // HB

    def body(q_ref, f_ref, i_ref, g_ref, tab_ref, gain_ref, o_ref, y_ref, st_ref, state):
        @pl.when(pl.program_id(0) == 0)
        def _():
            state[...] = jnp.zeros_like(state)

        row, col = _hgrn_mats()
        causal = col <= row
        tri = causal.astype(BF16)
        trimid = tri - (col <= (row // SUB) * SUB + SUB // 2 - 1).astype(BF16)
        for h in range(NH):
            sl = slice(HD * h, HD * (h + 1))
            v, gr = i_ref[:, sl], g_ref[:, sl]
            _, _, _, _, lf, k, q = _hgrn_gates(q_ref[:, sl], f_ref[:, sl], tab_ref[0:1, sl], tab_ref[1:2, sl])
            cum = _dot_sel(tri, lf)
            cmid = _dot_sel(trimid, lf)
            _, _, s = _hgrn_scores(q, k, cum, cmid)
            p = jnp.where(causal, s, 0.0)
            st = state[h]
            st_ref[h, 0] = st
            o = _dot(p, v) + _dot_nt(q * jnp.exp(cum), st)
            last = cum[HB - 1:HB, :]
            state[h] = st * jnp.exp(last) + _dot_tn(v, k * jnp.exp(last - cum))
            o_ref[:, sl] = o
            y_ref[:, sl] = (_rms(o, gain_ref[...]) * gr * _sig(gr)).astype(BF16)

    blk = lambda cb: pl.BlockSpec((HB, HGW), functools.partial(lambda n, cb: (n, cb), cb=cb))
    return pl.pallas_call(
        body, name="hgrn_fwd", grid=(nblk,),
        in_specs=[blk(0), blk(1), blk(2), blk(3), pl.BlockSpec((2, HGW), lambda n: (0, 0)),
                  pl.BlockSpec((1, HD), lambda n: (0, 0))],
        out_specs=[pl.BlockSpec((HB, HGW), lambda n: (n, 0)), pl.BlockSpec((HB, HGW), lambda n: (n, 0)),
                   pl.BlockSpec((NH, 1, HD, HD), lambda n: (0, n, 0, 0))],
        out_shape=[jax.ShapeDtypeStruct((t, HGW), F32), jax.ShapeDtypeStruct((t, HGW), BF16),
                   jax.ShapeDtypeStruct((NH, nblk, HD, HD), F32)],
        scratch_shapes=[pltpu.VMEM((NH, HD, HD), F32)],
        compiler_params=_params(("arbitrary",)),
    )(p_hg, p_hg, p_hg, p_hg, table, gain)


def _hgrn_bwd(p_hg, table, gain, o_pre, states, dy, t):
    nblk = t // HB

    def body(q_ref, f_ref, i_ref, g_ref, tab_ref, gain_ref, o_ref, st_ref, dy_ref, dp_ref, dtab_ref, dgain_ref,
             dstate):
        @pl.when(pl.program_id(0) == 0)
        def _():
            dstate[...] = jnp.zeros_like(dstate)
            dtab_ref[...] = jnp.zeros_like(dtab_ref)
            dgain_ref[...] = jnp.zeros_like(dgain_ref)

        row, col = _hgrn_mats()
        causal = col <= row
        tri = causal.astype(BF16)
        midrow = (row // SUB) * SUB + SUB // 2 - 1
        trimid = tri - (col <= midrow).astype(BF16)
        tri_t = (row <= col).astype(BF16)
        midcol = (col // SUB) * SUB + SUB // 2 - 1
        trimid_t = tri_t - (row <= midcol).astype(BF16)
        dgain = jnp.zeros((1, HD), F32)
        for h in range(NH):
            sl = slice(HD * h, HD * (h + 1))
            qr, z, v, gr = q_ref[:, sl], f_ref[:, sl], i_ref[:, sl], g_ref[:, sl]
            lb, sz, sneg, f, lf, k, q = _hgrn_gates(qr, z, tab_ref[0:1, sl], tab_ref[1:2, sl])
            cum = _dot_sel(tri, lf)
            cmid = _dot_sel(trimid, lf)
            qd, kds, s = _hgrn_scores(q, k, cum, cmid)
            p = jnp.where(causal, s, 0.0)
            st = st_ref[h, 0]
            dst = dstate[h]
            o = o_ref[:, sl]
            sg = _sig(gr)
            dyh = dy_ref[:, sl]
            on = _rms(o, gain_ref[...])
            dgr = dyh * on * sg * (1.0 + gr * (1.0 - sg))
            do, dg_h = _rms_bwd(o, gain_ref[...], dyh * gr * sg)
            dgain = dgain + dg_h
            do_b = do.astype(BF16)
            ecum = jnp.exp(cum)
            qc = q * ecum
            last = cum[HB - 1:HB, :]
            edec = jnp.exp(last - cum)
            kdec = k * edec
            dp = jnp.where(causal, _dot_nt(do_b, v), 0.0)
            dv = _dot(p.T, do_b) + _dot_nt(kdec, dst)
            dqc = _dot(do_b, st)
            dkdec = _dot(v, dst)
            dstate[h] = dst * jnp.exp(last) + _dot(do.T, qc)
            dp_b = dp.astype(BF16)
            dqd = jnp.concatenate([_dot(dp_b[SUB * i:SUB * (i + 1)], kds[i]) for i in range(HB // SUB)], axis=0)
            gq = dqd * qd
            dq = dqd * jnp.exp(jnp.minimum(cmid, EXP_CLAMP)) + dqc * ecum
            gs = dkdec * kdec
            dk = dkdec * edec
            dcum = dqc * qc - gs
            dcum = dcum + jnp.where(row == HB - 1, jnp.sum(gs, axis=0, keepdims=True)
                                    + jnp.exp(last) * jnp.sum(st * dst, axis=0, keepdims=True), 0.0)
            qd_b = qd.astype(BF16)
            for i in range(HB // SUB):
                dkd = _dot_tn(dp_b[SUB * i:SUB * (i + 1)], qd_b[SUB * i:SUB * (i + 1)])
                mid = cum[SUB * i + SUB // 2 - 1:SUB * i + SUB // 2, :]
                dk = dk + dkd * jnp.exp(jnp.minimum(mid - cum, EXP_CLAMP))
                gk = dkd * kds[i]
                dcum = dcum - gk + jnp.where(row == SUB * i + SUB // 2 - 1, jnp.sum(gk, axis=0, keepdims=True), 0.0)
            dlf = _dot_sel(tri_t, dcum) + _dot_sel(trimid_t, gq)
            df = dlf / f - dk
            dz = df * (1.0 - lb) * sz * sneg
            dlb = jnp.sum(df * sneg, axis=0, keepdims=True) * lb * (1.0 - lb)
            dtab_ref[0:1, sl] += dlb
            dtab_ref[1:2, sl] -= dlb
            sq = _sig(qr)
            dp_ref[:, sl] = (dq * sq * (1.0 + qr * (1.0 - sq))).astype(BF16)
            dp_ref[:, HGW + HD * h:HGW + HD * (h + 1)] = dz.astype(BF16)
            dp_ref[:, 2 * HGW + HD * h:2 * HGW + HD * (h + 1)] = dv.astype(BF16)
            dp_ref[:, 3 * HGW + HD * h:3 * HGW + HD * (h + 1)] = dgr.astype(BF16)
        dgain_ref[...] += dgain

    rev = lambda cb: pl.BlockSpec((HB, HGW), functools.partial(lambda n, cb: (nblk - 1 - n, cb), cb=cb))
    return pl.pallas_call(
        body, name="hgrn_bwd", grid=(nblk,),
        in_specs=[rev(0), rev(1), rev(2), rev(3), pl.BlockSpec((2, HGW), lambda n: (0, 0)),
                  pl.BlockSpec((1, HD), lambda n: (0, 0)), rev(0),
                  pl.BlockSpec((NH, 1, HD, HD), lambda n: (0, nblk - 1 - n, 0, 0)), rev(0)],
        out_specs=[pl.BlockSpec((HB, 4 * HGW), lambda n: (nblk - 1 - n, 0)),
                   pl.BlockSpec((2, HGW), lambda n: (0, 0)), pl.BlockSpec((1, HD), lambda n: (0, 0))],
        out_shape=[jax.ShapeDtypeStruct((t, 4 * HGW), BF16), jax.ShapeDtypeStruct((2, HGW), F32),
                   jax.ShapeDtypeStruct((1, HD), F32)],
        scratch_shapes=[pltpu.VMEM((NH, HD, HD), F32)],
        compiler_params=_params(("arbitrary",)),
    )(p_hg, p_hg, p_hg, p_hg, table, gain, o_pre, states, dy)


def _rope_mat():
    r = lax.broadcasted_iota(jnp.int32, (ROPE, ROPE), 0)
    c = lax.broadcasted_iota(jnp.int32, (ROPE, ROPE), 1)
    half = ROPE // 2
    return ((r == c - half).astype(F32) - (r == c + half).astype(F32)).astype(BF16)


def _mla_prep_fwd(p_mla, cs, sn, wq, wkv, gql, gkvl, gq, gk, t):
    tm = _pick(t, (512, 256, 128))

    def body(p_ref, cs_ref, sn_ref, wq_ref, wkv_ref, gql_ref, gkvl_ref, gq_ref, gk_ref,
             q_ref, k_ref, v_ref):
        rmat = _rope_mat()
        cqn = _rms(p_ref[:, 0:QL], gql_ref[...]).astype(BF16)
        ckvn = _rms(p_ref[:, QL:QL + KVL], gkvl_ref[...]).astype(BF16)
        kpe = p_ref[:, QL + KVL:QL + KVL + ROPE]
        c, s = cs_ref[...], sn_ref[...]
        rot = lambda x: x * c + _sel_dot(x, rmat) * s
        for h in range(NH):
            qa, qr = _dot(cqn, wq_ref[h, :, 0:HD]), _dot(cqn, wq_ref[h, :, HD:QK])
            rq = lax.rsqrt((jnp.sum(qa * qa, -1, keepdims=True) + jnp.sum(qr * qr, -1, keepdims=True)) / QK + EPS)
            q_ref[h, :, 0:HD] = (qa * rq * gq_ref[:, 0:HD] * SCALE).astype(BF16)
            q_ref[h, :, HD:QK] = (rot(qr * rq * gq_ref[:, HD:QK]) * SCALE).astype(BF16)
            kn = _dot(ckvn, wkv_ref[h, :, 0:HD])
            rk = lax.rsqrt((jnp.sum(kn * kn, -1, keepdims=True) + jnp.sum(kpe * kpe, -1, keepdims=True)) / QK + EPS)
            k_ref[h, :, 0:HD] = (kn * rk * gk_ref[:, 0:HD]).astype(BF16)
            k_ref[h, :, HD:QK] = rot(kpe * rk * gk_ref[:, HD:QK]).astype(BF16)
            v_ref[h] = _dot(ckvn, wkv_ref[h, :, HD:2 * HD]).astype(BF16)

    whole = lambda a: pl.BlockSpec(a.shape, functools.partial(lambda i, nd: (0,) * nd, nd=a.ndim))
    return pl.pallas_call(
        body, name="mla_prep_fwd", grid=(t // tm,),
        in_specs=[pl.BlockSpec((tm, QL + KVL + ROPE), lambda i: (i, 0)), pl.BlockSpec((tm, ROPE), lambda i: (i, 0)),
                  pl.BlockSpec((tm, ROPE), lambda i: (i, 0))] + [whole(a) for a in (wq, wkv, gql, gkvl, gq, gk)],
        out_specs=[pl.BlockSpec((NH, tm, QK), lambda i: (0, i, 0)), pl.BlockSpec((NH, tm, QK), lambda i: (0, i, 0)),
                   pl.BlockSpec((NH, tm, HD), lambda i: (0, i, 0))],
        out_shape=[jax.ShapeDtypeStruct((NH, t, QK), BF16), jax.ShapeDtypeStruct((NH, t, QK), BF16),
                   jax.ShapeDtypeStruct((NH, t, HD), BF16)],
        compiler_params=_params(("parallel",)),
    )(p_mla, cs, sn, wq, wkv, gql, gkvl, gq, gk)


def _mla_prep_bwd(p_mla, cs, sn, wq, wkv, gql, gkvl, gq, gk, dq, dk, dv, t):
    tm = _pick(t, (512, 256, 128))

    def body(p_ref, cs_ref, sn_ref, wq_ref, wkv_ref, gql_ref, gkvl_ref, gq_ref, gk_ref,
             dq_ref, dk_ref, dv_ref,
             dp_ref, dwq_ref, dwkv_ref, dgql_ref, dgkvl_ref, dgq_ref, dgk_ref):
        accs = (dwq_ref, dwkv_ref, dgql_ref, dgkvl_ref, dgq_ref, dgk_ref)

        @pl.when(pl.program_id(0) == 0)
        def _():
            for r in accs:
                r[...] = jnp.zeros_like(r)

        rmat = _rope_mat()
        rmat_t = -rmat
        cq, ckv = p_ref[:, 0:QL], p_ref[:, QL:QL + KVL]
        kpe = p_ref[:, QL + KVL:QL + KVL + ROPE]
        cqn_f, ckvn_f = _rms(cq, gql_ref[...]), _rms(ckv, gkvl_ref[...])
        cqn, ckvn = cqn_f.astype(BF16), ckvn_f.astype(BF16)
        cqn_t, ckvn_t = cqn_f.T.astype(BF16), ckvn_f.T.astype(BF16)
        c, s = cs_ref[...], sn_ref[...]
        unrot = lambda dy: dy * c + _sel_dot(dy * s, rmat_t)
        dcqn = jnp.zeros((tm, QL), F32)
        dckvn = jnp.zeros((tm, KVL), F32)
        dkpe = jnp.zeros((tm, ROPE), F32)
        dgq_a, dgq_r = jnp.zeros((1, HD), F32), jnp.zeros((1, ROPE), F32)
        dgk_a, dgk_r = jnp.zeros((1, HD), F32), jnp.zeros((1, ROPE), F32)
        for h in range(NH):
            qa, qr = _dot(cqn, wq_ref[h, :, 0:HD]), _dot(cqn, wq_ref[h, :, HD:QK])
            rq = lax.rsqrt((jnp.sum(qa * qa, -1, keepdims=True) + jnp.sum(qr * qr, -1, keepdims=True)) / QK + EPS)
            xa, xr = qa * rq, qr * rq
            dya = dq_ref[h, :, 0:HD] * SCALE
            dyr = unrot(dq_ref[h, :, HD:QK] * SCALE)
            dgq_a += jnp.sum(dya * xa, axis=0, keepdims=True)
            dgq_r += jnp.sum(dyr * xr, axis=0, keepdims=True)
            ga, gr_ = dya * gq_ref[:, 0:HD], dyr * gq_ref[:, HD:QK]
            mean = (jnp.sum(ga * xa, -1, keepdims=True) + jnp.sum(gr_ * xr, -1, keepdims=True)) / QK
            dqa = (rq * (ga - xa * mean)).astype(BF16)
            dqr = (rq * (gr_ - xr * mean)).astype(BF16)
            dwq_ref[h, :, 0:HD] += _dot(cqn_t, dqa)
            dwq_ref[h, :, HD:QK] += _dot(cqn_t, dqr)
            dcqn += _dot_nt(dqa, wq_ref[h, :, 0:HD]) + _dot_nt(dqr, wq_ref[h, :, HD:QK])
            kn = _dot(ckvn, wkv_ref[h, :, 0:HD])
            rk = lax.rsqrt((jnp.sum(kn * kn, -1, keepdims=True) + jnp.sum(kpe * kpe, -1, keepdims=True)) / QK + EPS)
            ya, yr = kn * rk, kpe * rk
            dka = dk_ref[h, :, 0:HD]
            dkr = unrot(dk_ref[h, :, HD:QK])
            dgk_a += jnp.sum(dka * ya, axis=0, keepdims=True)
            dgk_r += jnp.sum(dkr * yr, axis=0, keepdims=True)
            ha, hr = dka * gk_ref[:, 0:HD], dkr * gk_ref[:, HD:QK]
            mean = (jnp.sum(ha * ya, -1, keepdims=True) + jnp.sum(hr * yr, -1, keepdims=True)) / QK
            dkn = (rk * (ha - ya * mean)).astype(BF16)
            dkpe += rk * (hr - yr * mean)
            dvh = dv_ref[h].astype(BF16)
            dwkv_ref[h, :, 0:HD] += _dot(ckvn_t, dkn)
            dwkv_ref[h, :, HD:2 * HD] += _dot(ckvn_t, dvh)
            dckvn += _dot_nt(dkn, wkv_ref[h, :, 0:HD]) + _dot_nt(dvh, wkv_ref[h, :, HD:2 * HD])
        dcq, dg1 = _rms_bwd(cq, gql_ref[...], dcqn)
        dckv, dg2 = _rms_bwd(ckv, gkvl_ref[...], dckvn)
        dp_ref[:, 0:QL] = dcq.astype(BF16)
        dp_ref[:, QL:QL + KVL] = dckv.astype(BF16)
        dp_ref[:, QL + KVL:QL + KVL + ROPE] = dkpe.astype(BF16)
        dgql_ref[...] += dg1
        dgkvl_ref[...] += dg2
        dgq_ref[:, 0:HD] += dgq_a
        dgq_ref[:, HD:QK] += dgq_r
        dgk_ref[:, 0:HD] += dgk_a
        dgk_ref[:, HD:QK] += dgk_r

    whole = lambda a: pl.BlockSpec(a.shape, functools.partial(lambda i, nd: (0,) * nd, nd=a.ndim))
    acc_shapes = [wq.shape, wkv.shape, gql.shape, gkvl.shape, gq.shape, gk.shape]
    return pl.pallas_call(
        body, name="mla_prep_bwd", grid=(t // tm,),
        in_specs=[pl.BlockSpec((tm, QL + KVL + ROPE), lambda i: (i, 0)), pl.BlockSpec((tm, ROPE), lambda i: (i, 0)),
                  pl.BlockSpec((tm, ROPE), lambda i: (i, 0))]
        + [whole(a) for a in (wq, wkv, gql, gkvl, gq, gk)]
        + [pl.BlockSpec((NH, tm, QK), lambda i: (0, i, 0)), pl.BlockSpec((NH, tm, QK), lambda i: (0, i, 0)),
           pl.BlockSpec((NH, tm, HD), lambda i: (0, i, 0))],
        out_specs=[pl.BlockSpec((tm, QL + KVL + ROPE), lambda i: (i, 0))]
        + [pl.BlockSpec(s, functools.partial(lambda i, nd: (0,) * nd, nd=len(s))) for s in acc_shapes],
        out_shape=[jax.ShapeDtypeStruct((t, QL + KVL + ROPE), BF16)]
        + [jax.ShapeDtypeStruct(s, F32) for s in acc_shapes],
        compiler_params=_params(("arbitrary",)),
    )(p_mla, cs, sn, wq, wkv, gql, gkvl, gq, gk, dq, dk, dv)


def _chunk_mask(nq, nk, key0, keys_on_rows):
    shape = (nk, nq) if keys_on_rows else (nq, nk)
    qi = lax.broadcasted_iota(jnp.int32, shape, 1 if keys_on_rows else 0) // CHUNK
    ki = lax.broadcasted_iota(jnp.int32, shape, 0 if keys_on_rows else 1) // CHUNK + key0 // CHUNK
    return ki <= qi


def _flash_fwd(q, k, v, t):
    tq = _pick(t, ATT_Q)
    tk = tq // 2

    def body(q_ref, k_ref, v_ref, o_ref, lse_ref):
        i = pl.program_id(1)
        qt = q_ref[0]

        def step(j, carry, key0):
            m, l, acc = carry
            cols = pl.ds(pl.multiple_of(j * tk, tk), tk)
            s = _dot_nt(qt, k_ref[0, cols, :])
            if key0 is not None:
                s = jnp.where(_chunk_mask(tq, tk, key0, False), s, -jnp.inf)
            m_new = jnp.maximum(m, jnp.max(s, axis=-1, keepdims=True))
            p = jnp.exp(s - m_new)
            alpha = jnp.exp(m - m_new)
            return m_new, alpha * l + jnp.sum(p, axis=-1, keepdims=True), alpha * acc + _dot(p, v_ref[0, cols, :])

        init = (jnp.full((tq, 1), -jnp.inf, F32), jnp.zeros((tq, 1), F32), jnp.zeros((tq, HD), F32))
        carry = lax.fori_loop(0, 2 * i, lambda j, cr: step(j, cr, None), init)
        carry = step(2 * i, carry, 0)
        m, l, acc = step(2 * i + 1, carry, tk)
        o_ref[...] = (acc / l).astype(BF16)
        lse_ref[0] = m + jnp.log(l)

    return pl.pallas_call(
        body, name="flash_fwd", grid=(NH, t // tq),
        in_specs=[pl.BlockSpec((1, tq, QK), lambda h, i: (h, i, 0)), pl.BlockSpec((1, t, QK), lambda h, i: (h, 0, 0)),
                  pl.BlockSpec((1, t, HD), lambda h, i: (h, 0, 0))],
        out_specs=[pl.BlockSpec((tq, HD), lambda h, i: (i, h)), pl.BlockSpec((1, tq, 1), lambda h, i: (h, i, 0))],
        out_shape=[jax.ShapeDtypeStruct((t, NH * HD), BF16), jax.ShapeDtypeStruct((NH, t, 1), F32)],
        compiler_params=_params(("parallel", "parallel")),
    )(q, k, v)


def _flash_delta(o, do, t):
    tq = _pick(t, ATT_TILES)

    def body(o_ref, do_ref, delta_ref):
        prod = do_ref[...].astype(F32) * o_ref[...].astype(F32)
        x1, x2, x3 = _split3(prod)
        ones = jnp.ones((8, HD), BF16)
        d = lambda p: lax.dot_general(ones, p, (((1,), (1,)), ((), ())), preferred_element_type=F32)
        delta_ref[0] = (d(x1) + d(x2) + d(x3))[0:1, :]

    return pl.pallas_call(
        body, name="flash_delta", grid=(NH, t // tq),
        in_specs=[pl.BlockSpec((tq, HD), lambda h, i: (i, h)), pl.BlockSpec((tq, HD), lambda h, i: (i, h))],
        out_specs=pl.BlockSpec((1, 1, tq), lambda h, i: (h, 0, i)),
        out_shape=jax.ShapeDtypeStruct((NH, 1, t), F32), compiler_params=_params(("parallel", "parallel")),
    )(o, do)


def _flash_bwd(q, k, v, do, lse_row, delta_row, t):
    tq = _pick(t, ATT_Q)
    tk = tq // 2

    def body(q_ref, k_ref, v_ref, do_ref, lse_ref, delta_ref, dq_ref, dk_ref, dv_ref):
        j = pl.program_id(1)

        @pl.when(j == 0)
        def _():
            dq_ref[...] = jnp.zeros_like(dq_ref)

        kt, vt = k_ref[0], v_ref[0]

        def step(i, carry, key0):
            dk, dv = carry
            rows = pl.ds(pl.multiple_of(i * tq, tq), tq)
            qt, dot_ = q_ref[0, rows, :], do_ref[rows, :]
            p = jnp.exp(_dot_nt(kt, qt) - lse_ref[0, :, rows])
            if key0 is not None:
                p = jnp.where(_chunk_mask(tq, tk, key0, True), p, 0.0)
            ds = (p * (_dot_nt(vt, dot_) - delta_ref[0, :, rows])).astype(BF16)
            dq_ref[0, rows, :] += _dot_tn(ds, kt)
            return dk + _dot(ds, qt), dv + _dot(p, dot_)

        carry = step(j // 2, (jnp.zeros((tk, QK), F32), jnp.zeros((tk, HD), F32)), (j % 2) * tk)
        dk, dv = lax.fori_loop(j // 2 + 1, t // tq, lambda i, cr: step(i, cr, None), carry)
        dk_ref[0] = dk
        dv_ref[0] = dv

    return pl.pallas_call(
        body, name="flash_bwd", grid=(NH, t // tk),
        in_specs=[pl.BlockSpec((1, t, QK), lambda h, j: (h, 0, 0)), pl.BlockSpec((1, tk, QK), lambda h, j: (h, j, 0)),
                  pl.BlockSpec((1, tk, HD), lambda h, j: (h, j, 0)), pl.BlockSpec((t, HD), lambda h, j: (0, h)),
                  pl.BlockSpec((1, 1, t), lambda h, j: (h, 0, 0)), pl.BlockSpec((1, 1, t), lambda h, j: (h, 0, 0))],
        out_specs=[pl.BlockSpec((1, t, QK), lambda h, j: (h, 0, 0)), pl.BlockSpec((1, tk, QK), lambda h, j: (h, j, 0)),
                   pl.BlockSpec((1, tk, HD), lambda h, j: (h, j, 0))],
        out_shape=[jax.ShapeDtypeStruct((NH, t, QK), F32), jax.ShapeDtypeStruct((NH, t, QK), F32),
                   jax.ShapeDtypeStruct((NH, t, HD), F32)],
        compiler_params=_params(("parallel", "arbitrary")),
    )(q, k, v, do, lse_row, delta_row)


def _ffn_in(xn, w_in, name):
    t, k = xn.shape
    s, _, n = w_in.shape
    tm = _pick(t, ROW_TILES)

    def body(x_ref, w_ref, gu_ref, a_ref):
        x = x_ref[...]
        g, u = _dot(x, w_ref[0]), _dot(x, w_ref[1])
        gu_ref[0] = g.astype(BF16)
        gu_ref[1] = u.astype(BF16)
        a_ref[...] = (g * _sig(g) * u).astype(BF16)

    return pl.pallas_call(
        body, name=name, grid=(t // tm, s // 2),
        in_specs=[_BS((tm, k), lambda i, j: (i, 0)), _BS((2, None, k, n), lambda i, j: (0, j, 0, 0))],
        out_specs=[_BS((2, None, tm, n), lambda i, j: (0, j, i, 0)), _BS((None, tm, n), lambda i, j: (j, i, 0))],
        out_shape=[jax.ShapeDtypeStruct((2, s // 2, t, n), BF16), jax.ShapeDtypeStruct((s // 2, t, n), BF16)],
        compiler_params=_params(("parallel", "parallel")),
    )(xn, w_in.reshape(2, s // 2, k, n))


def _ffn_dgu(dfo, w_out, gu, name):
    t, k = dfo.shape
    s, n, _ = w_out.shape
    tm = _pick(t, ROW_TILES)

    def body(d_ref, w_ref, gu_ref, o_ref):
        da = _dot_nt(d_ref[...], w_ref[...])
        g, u = gu_ref[0].astype(F32), gu_ref[1].astype(F32)
        sg = _sig(g)
        o_ref[0] = (da * u * sg * (1.0 + g * (1.0 - sg))).astype(BF16)
        o_ref[1] = (da * g * sg).astype(BF16)

    pair = _BS((2, None, tm, n), lambda i, j: (0, j, i, 0))
    return pl.pallas_call(
        body, name=name, grid=(t // tm, s),
        in_specs=[_BS((tm, k), lambda i, j: (i, 0)), _BS((None, n, k), lambda i, j: (j, 0, 0)), pair],
        out_specs=pair, out_shape=jax.ShapeDtypeStruct((2, s, t, n), BF16),
        compiler_params=_params(("parallel", "parallel")),
    )(dfo, w_out, gu).reshape(2 * s, t, n)


def _ffn_fwd(xn, w_in, w_out, tag):
    gu, a = _ffn_in(xn, w_in, tag + "_in")
    return gu, a, _mm_stack_red(a, w_out, tag + "_out", "nn")


def _ffn_bwd(dfo, xn, gu, a, w_in, w_out, tag):
    dw_out = _mm_stack_tn(a, dfo, tag + "_dwout")
    dgu = _ffn_dgu(dfo, w_out, gu, tag + "_dgu")
    return _mm_stack_red(dgu, w_in, tag + "_dxn", "nt"), _mm_stack_tn(xn, dgu, tag + "_dwin"), dw_out


def _local_step(x, target, cs, sn, w, s):
    t = x.shape[0]
    tm = _pick(t, (256, 128))
    g = {}
    rows_of = lambda n: w[n].reshape(-1, w[n].shape[2])
    ffn_out = lambda n: w[n].reshape(4, FF // 4, D)
    w_hgb, w_mlab, w_o = rows_of("w_hg_branch"), rows_of("w_mla_branch"), rows_of("w_out")
    w_in_nat = w["w_in"].transpose(1, 0, 2).reshape(D, -1)
    w_mrg = w["w_merge"]
    mw = w_mrg.shape[2]

    def norm_fn(x_ref, g_ref, o_ref):
        o_ref[...] = _rms(x_ref[...], g_ref[...]).astype(BF16)

    xn1 = _rows(norm_fn, "norm1", t, tm, [(x, D, 0)], [s["ffn1_norm"]], [(D, BF16)])[0]
    gu1, a1, f1 = _ffn_fwd(xn1, w["ffn1_w_in"], ffn_out("ffn1_w_out"), "ffn1")

    def res_norm_fn(scale):
        def fn(h_ref, f_ref, g_ref, h_out, n_out):
            h = h_ref[...] + scale * f_ref[...]
            h_out[...] = h
            n_out[...] = _rms(h, g_ref[...]).astype(BF16)
        return fn

    h1, u = _rows(res_norm_fn(0.5), "res_norm1", t, tm, [(x, D, 0), (f1, D, 0)], [s["mix_norm"]],
                  [(D, F32), (D, BF16)])
    w_in_hg, w_in_mla = w_in_nat[:, :4 * HGW], w_in_nat[:, 4 * HGW:]
    p_hg = _mm(u, w_in_hg, "nn", "proj_hg")
    p_mla = _mm(u, w_in_mla, "nn", "proj_mla")
    gpre = _cols_fwd(u, w_mrg, "proj_gate")
    o_pre, hgy, states = _hgrn_fwd(p_hg, s["hg_lb_table"], s["hg_out_norm"], t)
    prep_args = (p_mla, cs, sn, w["w_q_up"], w["w_kv_up"], s["mla_q_lora_norm"], s["mla_kv_lora_norm"],
                 s["q_head_norm"], s["k_head_norm"])
    q, k, v = _mla_prep_fwd(*prep_args, t)
    att, lse = _flash_fwd(q, k, v, t)
    y_hg = _mm(hgy, w_hgb, "nn", "branch_hg")
    y_mla = _mm(att, w_mlab, "nn", "branch_mla")

    def mix_fn(gh_ref, gm_ref, yh_ref, ym_ref, b_ref, o_ref):
        gh = _sig(gh_ref[...] + b_ref[:, 0:D])
        gm = _sig(gm_ref[...] + b_ref[:, D:2 * D])
        o_ref[...] = (gh * yh_ref[...] + gm * ym_ref[...]).astype(BF16)

    mixed = _rows(mix_fn, "mix", t, tm, [(gpre, D, 0), (gpre, D, 1), (y_hg, D, 0), (y_mla, D, 0)], [s["b_merge"]],
                  [(D, BF16)])[0]
    mo = _mm(mixed, w_o, "nn", "mix_out")
    h2, xn2 = _rows(res_norm_fn(1.0), "res_norm2", t, tm, [(h1, D, 0), (mo, D, 0)], [s["ffn2_norm"]],
                    [(D, F32), (D, BF16)])
    gu2, a2, f2 = _ffn_fwd(xn2, w["ffn2_w_in"], ffn_out("ffn2_w_out"), "ffn2")

    def loss_fn(h_ref, f_ref, tg_ref, g_ref, dh_out, dhb_out):
        h = h_ref[...] + 0.5 * f_ref[...]
        e = _rms(h, g_ref[...]) - tg_ref[...]
        dh, dgain = _rms_bwd(h, g_ref[...], e / D)
        dh_out[...] = dh
        dhb_out[...] = (0.5 * dh).astype(BF16)
        return dgain, jnp.full((1, LANE), 0.5 / D * jnp.sum(e * e), F32)

    dh3, dfo2, g["final_norm"], loss = _rows(loss_fn, "loss", t, tm, [(h2, D, 0), (f2, D, 0), (target, D, 0)],
                                             [s["final_norm"]], [(D, F32), (D, BF16)], [(1, D), (1, LANE)])

    def norm_bwd_fn(scale):
        def fn(h_ref, dxn_ref, dh_ref, g_ref, dh_out, dhb_out):
            dx, dgain = _rms_bwd(h_ref[...], g_ref[...], dxn_ref[...])
            dh = dh_ref[...] + dx
            dh_out[...] = dh
            dhb_out[...] = (scale * dh).astype(BF16)
            return (dgain,)
        return fn

    as_rows = lambda a: a.reshape((N_DEV, -1) + a.shape[-1:])
    dxn2, g["ffn2_w_in"], dwo = _ffn_bwd(dfo2, xn2, gu2, a2, w["ffn2_w_in"], ffn_out("ffn2_w_out"), "ffn2")
    g["ffn2_w_out"] = as_rows(dwo)
    dh2, dh2b, g["ffn2_norm"] = _rows(norm_bwd_fn(1.0), "norm2_bwd", t, tm, [(h2, D, 0), (dxn2, D, 0), (dh3, D, 0)],
                                      [s["ffn2_norm"]], [(D, F32), (D, BF16)], [(1, D)])
    dmixed = _mm(dh2b, w_o, "nt", "mix_out_dx")
    g["w_out"] = as_rows(_mm(mixed, dh2b, "tn", "mix_out_dw"))

    def mix_bwd_fn(gh_ref, gm_ref, yh_ref, ym_ref, dm_ref, b_ref, dyh_out, dym_out, dg_out):
        gh = _sig(gh_ref[...] + b_ref[:, 0:D])
        gm = _sig(gm_ref[...] + b_ref[:, D:2 * D])
        dm = dm_ref[...]
        dyh_out[...] = (dm * gh).astype(BF16)
        dym_out[...] = (dm * gm).astype(BF16)
        dgh = dm * yh_ref[...] * gh * (1.0 - gh)
        dgm = dm * ym_ref[...] * gm * (1.0 - gm)
        dg_out[:, 0:D] = dgh.astype(BF16)
        dg_out[:, D:2 * D] = dgm.astype(BF16)
        return (jnp.concatenate([jnp.sum(dgh, axis=0, keepdims=True), jnp.sum(dgm, axis=0, keepdims=True)], axis=1),)

    dyh, dym, dgpre, g["b_merge"] = _rows(
        mix_bwd_fn, "mix_bwd", t, tm, [(gpre, D, 0), (gpre, D, 1), (y_hg, D, 0), (y_mla, D, 0), (dmixed, D, 0)],
        [s["b_merge"]], [(D, BF16), (D, BF16), (2 * D, BF16)], [(1, 2 * D)])
    g["w_hg_branch"] = as_rows(_mm(hgy, dyh, "tn", "branch_hg_dw"))
    g["w_mla_branch"] = as_rows(_mm(att, dym, "tn", "branch_mla_dw"))
    g["w_merge"] = _cols_dw(u, dgpre, mw, "proj_gate_dw")
    dhgy = _mm(dyh, w_hgb, "nt", "branch_hg_dx")
    datt = _mm(dym, w_mlab, "nt", "branch_mla_dx", out_dtype=BF16)
    du_gate = _cols_dx(dgpre, w_mrg, "proj_gate_dx")

    dq, dk, dv = _flash_bwd(q, k, v, datt, lse.reshape(NH, 1, t), _flash_delta(att, datt, t), t)
    (dp_mla, g["w_q_up"], g["w_kv_up"], g["mla_q_lora_norm"], g["mla_kv_lora_norm"], g["q_head_norm"],
     g["k_head_norm"]) = _mla_prep_bwd(*prep_args, dq, dk, dv, t)
    dp_hg, g["hg_lb_table"], g["hg_out_norm"] = _hgrn_bwd(p_hg, s["hg_lb_table"], s["hg_out_norm"], o_pre, states,
                                                          dhgy, t)
    dw_in_nat = jnp.concatenate([_mm(u, dp_hg, "tn", "proj_hg_dw"), _mm(u, dp_mla, "tn", "proj_mla_dw")], axis=1)
    g["w_in"] = dw_in_nat.reshape(D, N_DEV, -1).transpose(1, 0, 2)
    du_hg = _mm(dp_hg, w_in_hg, "nt", "proj_hg_dx")
    du_mla = _mm(dp_mla, w_in_mla, "nt", "proj_mla_dx")

    def mixnorm_bwd_fn(h_ref, a_ref, b_ref, c_ref, dh_ref, g_ref, dh_out, dhb_out):
        dx, dgain = _rms_bwd(h_ref[...], g_ref[...], a_ref[...] + b_ref[...] + c_ref[...])
        dh = dh_ref[...] + dx
        dh_out[...] = dh
        dhb_out[...] = (0.5 * dh).astype(BF16)
        return (dgain,)

    dh1, dfo1, g["mix_norm"] = _rows(mixnorm_bwd_fn, "mixnorm_bwd", t, tm,
                                     [(h1, D, 0), (du_hg, D, 0), (du_mla, D, 0), (du_gate, D, 0), (dh2, D, 0)],
                                     [s["mix_norm"]], [(D, F32), (D, BF16)], [(1, D)])
    dxn1, g["ffn1_w_in"], dwo = _ffn_bwd(dfo1, xn1, gu1, a1, w["ffn1_w_in"], ffn_out("ffn1_w_out"), "ffn1")
    g["ffn1_w_out"] = as_rows(dwo)
    grad_x, _, g["ffn1_norm"] = _rows(norm_bwd_fn(1.0), "norm1_bwd", t, tm, [(x, D, 0), (dxn1, D, 0), (dh1, D, 0)],
                                      [s["ffn1_norm"]], [(D, F32), (D, BF16)], [(1, D)])
    return loss, grad_x, g


def _coords():
    return lax.axis_index("x"), lax.axis_index("y"), lax.axis_index("c")


def _hbm_call(body, name, ins, out_shapes, scratch):
    any_spec = pl.BlockSpec(memory_space=pl.ANY)
    return pl.pallas_call(
        body, name=name, out_shape=[jax.ShapeDtypeStruct(s, dt) for s, dt in out_shapes],
        in_specs=[any_spec] * len(ins), out_specs=[any_spec] * len(out_shapes), scratch_shapes=scratch,
    )(*ins)


def _all_gather(blocks, name):
    nb = len(blocks)

    def body(*refs):
        x_refs, out_refs = refs[:nb], refs[nb:2 * nb]
        send_sems, recv_sems, local_sems = refs[2 * nb:]
        x, y, c = _coords()
        me, sibling = (x, y, c), (x, y, 1 - c)
        chips = [(1 - x, y), (x, 1 - y), (1 - x, 1 - y)]

        def slot(b, px, py, pc):
            return out_refs[b].at[4 * px + 2 * py + pc]

        def copy(b, kk, block_of, to, src=None):
            return pltpu.make_async_remote_copy(
                src_ref=slot(b, *block_of) if src is None else src, dst_ref=slot(b, *block_of),
                send_sem=send_sems.at[b, kk], recv_sem=recv_sems.at[b, kk], device_id=to, device_id_type=MESH)

        mine = [pltpu.make_async_copy(x_refs[b], slot(b, *me), local_sems.at[b]) for b in range(nb)]
        for cp in mine:
            cp.start()
        first = [copy(b, 0, me, sibling, src=x_refs[b]) for b in range(nb)]
        first += [copy(b, 1 + j, me, (*chip, c), src=x_refs[b]) for j, chip in enumerate(chips) for b in range(nb)]
        for cp in first:
            cp.start()
        passed = []
        for j, chip in enumerate(chips):
            for b in range(nb):
                copy(b, 1 + j, (*chip, c), me).wait_recv()
                passed.append(copy(b, 4 + j, (*chip, c), sibling))
                passed[-1].start()
        for b in range(nb):
            copy(b, 0, sibling, me).wait_recv()
        for j, chip in enumerate(chips):
            for b in range(nb):
                copy(b, 4 + j, (*chip, 1 - c), me).wait_recv()
        for cp in first + passed:
            cp.wait_send()
        for cp in mine:
            cp.wait()

    return _hbm_call(body, name, blocks, [((N_DEV,) + b.shape, b.dtype) for b in blocks],
                     [pltpu.SemaphoreType.DMA((nb, 7)), pltpu.SemaphoreType.DMA((nb, 7)),
                      pltpu.SemaphoreType.DMA((nb,))])


def _sibling_swap(bufs, name):
    nb = len(bufs)

    def body(*refs):
        x_refs, out_refs = refs[:nb], refs[nb:2 * nb]
        send_sems, recv_sems = refs[2 * nb:]
        x, y, c = _coords()
        copies = [pltpu.make_async_remote_copy(
            src_ref=x_refs[b].at[2 * q + 1 - c], dst_ref=out_refs[b].at[q], send_sem=send_sems.at[b, q],
            recv_sem=recv_sems.at[b, q], device_id=(x, y, 1 - c), device_id_type=MESH)
            for b in range(nb) for q in range(4)]
        for cp in copies:
            cp.start()
        for cp in copies:
            cp.wait()

    return _hbm_call(body, name, bufs, [((4,) + b.shape[1:], b.dtype) for b in bufs],
                     [pltpu.SemaphoreType.DMA((nb, 4)), pltpu.SemaphoreType.DMA((nb, 4))])


def _chip_exchange(bufs, name):
    nb = len(bufs)

    def body(*refs):
        x_refs, out_refs = refs[:nb], refs[nb:2 * nb]
        send_sems, recv_sems, local_sems = refs[2 * nb:]
        x, y, c = _coords()
        chips = [(1 - x, y), (x, 1 - y), (1 - x, 1 - y)]
        mine = [pltpu.make_async_copy(x_refs[b].at[2 * x + y], out_refs[b].at[2 * x + y], local_sems.at[b])
                for b in range(nb)]
        for cp in mine:
            cp.start()
        copies = [pltpu.make_async_remote_copy(
            src_ref=x_refs[b].at[2 * px + py], dst_ref=out_refs[b].at[2 * x + y], send_sem=send_sems.at[b, j],
            recv_sem=recv_sems.at[b, j], device_id=(px, py, c), device_id_type=MESH)
            for j, (px, py) in enumerate(chips) for b in range(nb)]
        for cp in copies:
            cp.start()
        for j, (px, py) in enumerate(chips):
            for b in range(nb):
                pltpu.make_async_remote_copy(
                    src_ref=x_refs[b].at[2 * x + y], dst_ref=out_refs[b].at[2 * px + py], send_sem=send_sems.at[b, j],
                    recv_sem=recv_sems.at[b, j], device_id=(px, py, c), device_id_type=MESH).wait_recv()
        for cp in copies:
            cp.wait_send()
        for cp in mine:
            cp.wait()

    return _hbm_call(body, name, bufs, [(b.shape, b.dtype) for b in bufs],
                     [pltpu.SemaphoreType.DMA((nb, 3)), pltpu.SemaphoreType.DMA((nb, 3)),
                      pltpu.SemaphoreType.DMA((nb,))])


def _chip_sum(g, r1, c, name):
    _, r, cw = g.shape
    tr = _pick(r, (256, 176, 128))

    def body(c_ref, g_ref, r_ref, o_ref):
        o_ref[...] = (g_ref[...] + r_ref[...]).astype(BF16)

    grid_spec = pltpu.PrefetchScalarGridSpec(
        num_scalar_prefetch=1, grid=(4, r // tr),
        in_specs=[_BS((None, None, tr, cw), lambda q, i, c_ref: (q, c_ref[0], i, 0)),
                  _BS((None, tr, cw), lambda q, i, c_ref: (q, i, 0))],
        out_specs=_BS((None, tr, cw), lambda q, i, c_ref: (q, i, 0)))
    return pl.pallas_call(
        body, name=name, grid_spec=grid_spec, out_shape=jax.ShapeDtypeStruct((4, r, cw), BF16),
        compiler_params=_params(("parallel", "parallel")),
    )(c.reshape(1).astype(jnp.int32), g.reshape(4, 2, r, cw), r1)


def _adamw_math(w, g, m, v):
    m = B1 * m + (1.0 - B1) * g
    v = B2 * v + (1.0 - B2) * (g * g)
    m_hat = m / (1.0 - B1 ** STEP)
    v_hat = v / (1.0 - B2 ** STEP)
    return -LR * (m_hat / (jnp.sqrt(v_hat) + AEPS) + WD * w), m, v


def _sum_adamw(parts, w, m, v, name):
    r, c = w.shape
    tr = _pick(r, (256, 176, 128))

    def body(p0, p1, p2, p3, w_ref, m_ref, v_ref, g_out, d_out, m_out, v_out):
        g = ((p0[...].astype(F32) + p1[...].astype(F32)) + p2[...].astype(F32)) + p3[...].astype(F32)
        g_out[...] = g
        d_out[...], m_out[...], v_out[...] = _adamw_math(w_ref[...], g, m_ref[...], v_ref[...])

    part = lambda q: _BS((None, tr, c), functools.partial(lambda i, q: (q, i, 0), q=q))
    plain = _BS((tr, c), lambda i: (i, 0))
    return pl.pallas_call(
        body, name=name, grid=(r // tr,), in_specs=[part(q) for q in range(4)] + [plain] * 3,
        out_specs=[plain] * 4, out_shape=[jax.ShapeDtypeStruct((r, c), F32)] * 4,
        compiler_params=_params(("parallel",)),
    )(parts, parts, parts, parts, w, m, v)


def _small_update(gathered, w, m, v):
    r = w.shape[0]

    def body(ga_ref, w_ref, m_ref, v_ref, g_out, d_out, m_out, v_out):
        g = ga_ref[0]
        for dev in range(1, N_DEV):
            g = g + ga_ref[dev]
        g_out[...] = g
        d_out[...], m_out[...], v_out[...] = _adamw_math(w_ref[...], g, m_ref[...], v_ref[...])

    return pl.pallas_call(
        body, name="small_update", out_shape=[jax.ShapeDtypeStruct((r, LANE), F32)] * 4,
    )(gathered, w, m, v)


def _pack_small(vals):
    rows = []
    for name, (r, n) in SMALL:
        flat = vals[name].reshape(-1)
        pad = (-flat.shape[0]) % LANE
        rows.append(jnp.pad(flat, (0, pad)).reshape(-1, LANE))
    return jnp.concatenate(rows, axis=0)


def _unpack_small(packed):
    out, off = {}, 0
    for name, (r, n) in SMALL:
        nrow = (r * n + LANE - 1) // LANE
        out[name] = packed[off:off + nrow].reshape(-1)[:r * n].reshape(r, n)
        off += nrow
    return out


def kernel(x, positions, ffn1_norm, ffn1_w_in, ffn1_w_out, mix_norm, w_in, hg_lb_table, hg_out_norm, w_hg_branch, mla_q_lora_norm, w_q_up, mla_kv_lora_norm, w_kv_up, q_head_norm, k_head_norm, w_mla_branch, w_merge, b_merge, w_out, ffn2_norm, ffn2_w_in, ffn2_w_out, final_norm, loss_target, m_ffn1_norm, m_ffn1_w_in, m_ffn1_w_out, m_mix_norm, m_w_in, m_hg_lb_table, m_hg_out_norm, m_w_hg_branch, m_mla_q_lora_norm, m_w_q_up, m_mla_kv_lora_norm, m_w_kv_up, m_q_head_norm, m_k_head_norm, m_w_mla_branch, m_w_merge, m_b_merge, m_w_out, m_ffn2_norm, m_ffn2_w_in, m_ffn2_w_out, m_final_norm, v_ffn1_norm, v_ffn1_w_in, v_ffn1_w_out, v_mix_norm, v_w_in, v_hg_lb_table, v_hg_out_norm, v_w_hg_branch, v_mla_q_lora_norm, v_w_q_up, v_mla_kv_lora_norm, v_w_kv_up, v_q_head_norm, v_k_head_norm, v_w_mla_branch, v_w_merge, v_b_merge, v_w_out, v_ffn2_norm, v_ffn2_w_in, v_ffn2_w_out, v_final_norm):
    args = dict(locals())
    t = x.shape[1]
    big_w = {n: args[n][0] for n, _, _ in BIG}
    small = {n: args[n].reshape(shape) for n, shape in SMALL}

    names = [n for n, _, _ in BIG]
    full = dict(zip(names, _all_gather([big_w[n].astype(BF16) for n in names], "weights_all_gather")))

    inv_freq = ROPE_THETA ** (-jnp.arange(0, ROPE, 2, dtype=F32) / ROPE)
    ang = positions[0].astype(F32)[:, None] * inv_freq
    cs = jnp.concatenate([jnp.cos(ang), jnp.cos(ang)], axis=1)
    sn = jnp.concatenate([jnp.sin(ang), jnp.sin(ang)], axis=1)

    loss_row, grad_x, g = _local_step(x[0], loss_target[0], cs, sn, full, small)

    c = lax.axis_index("c")
    from_sibling = _sibling_swap([g[n] for n in names], "grads_sibling_swap")
    chip_sums = [_chip_sum(g[n], r1, c, "chip_sum_" + n) for n, r1 in zip(names, from_sibling)]
    exchanged = dict(zip(names, _chip_exchange(chip_sums, "grads_chip_exchange")))

    small_packed = jnp.concatenate([_pack_small(g), loss_row], axis=0)
    pad = (-small_packed.shape[0]) % 8
    small_packed = jnp.pad(small_packed, ((0, pad), (0, 0)))
    small_all = _all_gather([small_packed], "small_all_gather")[0]
    zero_tail = jnp.zeros((small_packed.shape[0] - _pack_small(small).shape[0], LANE), F32)
    pk = lambda d: jnp.concatenate([_pack_small(d), zero_tail], axis=0)
    sg, sd, sm, sv = _small_update(
        small_all, pk(small), pk({n: args["m_" + n].reshape(shape) for n, shape in SMALL}),
        pk({n: args["v_" + n].reshape(shape) for n, shape in SMALL}))
    n_small_rows = _pack_small(small).shape[0]
    loss = sg[n_small_rows, 0]
    outs = {k_: _unpack_small(a) for k_, a in (("grad", sg), ("delta", sd), ("new_m", sm), ("new_v", sv))}

    for n in names:
        outs["grad"][n], outs["delta"][n], outs["new_m"][n], outs["new_v"][n] = _sum_adamw(
            exchanged[n], big_w[n], args["m_" + n][0], args["v_" + n][0], "adamw_" + n)

    def shaped(kind, n):
        return outs[kind][n].reshape(args[n].shape)

    return (loss, grad_x[None], *[shaped("grad", n) for n in WEIGHT_ORDER], *[shaped("delta", n) for n in WEIGHT_ORDER],
            *[shaped("new_m", n) for n in WEIGHT_ORDER], *[shaped("new_v", n) for n in WEIGHT_ORDER])
```

```python
import functools

import jax
import jax.numpy as jnp
from jax import lax
from jax.experimental import pallas as pl
from jax.experimental.pallas import tpu as pltpu

F32 = jnp.float32
BF16 = jnp.bfloat16

D = 1024
FF = 2816
NH = 8
HD = 128
ROPE = 64
QK = HD + ROPE
QL = 384
KVL = 256
HGW = NH * HD
CHUNK = 64
EPS = 1e-6
ROPE_THETA = 10000.0
SCALE = QK ** -0.5

LR, B1, B2, AEPS, WD, STEP = 0.001, 0.9, 0.999, 1e-08, 0.01, 10

HB = 128
SUB = 16
EXP_CLAMP = 80.0
ATT_TILES = (512, 256, 128)
ATT_Q = (1024, 512, 256, 128)
ATT_KEY_TILES = 1
ROW_TILES = (1024, 512, 256, 128)

LANE = 128
VMEM_LIMIT = 56 << 20

N_DEV = 8
MESH = pl.DeviceIdType.MESH

BIG = (
    ("ffn1_w_in", (D, 2 * FF), 1), ("ffn1_w_out", (FF, D), 0), ("w_in", (D, 4800), 1),
    ("w_hg_branch", (HGW, D), 0), ("w_q_up", (QL, NH * QK), 1), ("w_kv_up", (KVL, NH * 2 * HD), 1),
    ("w_mla_branch", (NH * HD, D), 0), ("w_merge", (D, 2 * D), 1), ("w_out", (D, D), 0),
    ("ffn2_w_in", (D, 2 * FF), 1), ("ffn2_w_out", (FF, D), 0),
)
SMALL = (
    ("ffn1_norm", (1, D)), ("mix_norm", (1, D)), ("hg_lb_table", (2, HGW)), ("hg_out_norm", (1, HD)),
    ("mla_q_lora_norm", (1, QL)), ("mla_kv_lora_norm", (1, KVL)), ("q_head_norm", (1, QK)),
    ("k_head_norm", (1, QK)), ("b_merge", (1, 2 * D)), ("ffn2_norm", (1, D)), ("final_norm", (1, D)),
)
WEIGHT_ORDER = ("ffn1_norm", "ffn1_w_in", "ffn1_w_out", "mix_norm", "w_in", "hg_lb_table", "hg_out_norm",
                "w_hg_branch", "mla_q_lora_norm", "w_q_up", "mla_kv_lora_norm", "w_kv_up", "q_head_norm",
                "k_head_norm", "w_mla_branch", "w_merge", "b_merge", "w_out", "ffn2_norm", "ffn2_w_in",
                "ffn2_w_out", "final_norm")


def _pick(n, cands):
    for c in cands:
        if n % c == 0:
            return c
    return n


def _params(sem):
    return pltpu.CompilerParams(dimension_semantics=sem, vmem_limit_bytes=VMEM_LIMIT)


def _sig(x):
    return 1.0 / (1.0 + jnp.exp(-x))


def _dot(a, b):
    return jnp.dot(a.astype(BF16), b.astype(BF16), preferred_element_type=F32)


def _dot_nt(a, b):
    return lax.dot_general(a.astype(BF16), b.astype(BF16), (((1,), (1,)), ((), ())),
                           preferred_element_type=F32)


def _dot_tn(a, b):
    return lax.dot_general(a.astype(BF16), b.astype(BF16), (((0,), (0,)), ((), ())),
                           preferred_element_type=F32)


def _split3(x):
    x1 = x.astype(BF16)
    r1 = x - x1.astype(F32)
    x2 = r1.astype(BF16)
    x3 = (r1 - x2.astype(F32)).astype(BF16)
    return x1, x2, x3


def _dot_sel(m, x):
    x1, x2, x3 = _split3(x)
    d = lambda p: jnp.dot(m, p, preferred_element_type=F32)
    return d(x1) + d(x2) + d(x3)


def _sel_dot(x, m):
    x1, x2, x3 = _split3(x)
    d = lambda p: jnp.dot(p, m, preferred_element_type=F32)
    return d(x1) + d(x2) + d(x3)


_TN = (1408, 1024, 768, 512, 384, 256, 128)


def _mm(a, b, mode, name, out_dtype=F32):
    if mode == "tn":
        t, m = a.shape
        n = b.shape[1]
        tt, tm, tn = _pick(t, (512, 256, 128)), _pick(m, _TN), _pick(n, _TN)

        def body(a_ref, b_ref, o_ref):
            @pl.when(pl.program_id(2) == 0)
            def _():
                o_ref[...] = jnp.zeros_like(o_ref)

            o_ref[...] += _dot_tn(a_ref[...], b_ref[...])

        return pl.pallas_call(
            body, name=name, grid=(m // tm, n // tn, t // tt),
            in_specs=[pl.BlockSpec((tt, tm), lambda i, j, k: (k, i)),
                      pl.BlockSpec((tt, tn), lambda i, j, k: (k, j))],
            out_specs=pl.BlockSpec((tm, tn), lambda i, j, k: (i, j)),
            out_shape=jax.ShapeDtypeStruct((m, n), F32),
            compiler_params=_params(("parallel", "parallel", "arbitrary")),
        )(a, b)

    m, k = a.shape
    tm = _pick(m, ROW_TILES)
    if mode == "nn":
        n = b.shape[1]
        tn = _pick(n, _TN)
        b_spec = pl.BlockSpec((k, tn), lambda i, j: (0, j))
        dot = _dot
    else:
        n = b.shape[0]
        tn = _pick(n, _TN if k <= 4096 else (512, 256, 128))
        b_spec = pl.BlockSpec((tn, k), lambda i, j: (j, 0))
        dot = _dot_nt

    def body(a_ref, b_ref, o_ref):
        o_ref[...] = dot(a_ref[...], b_ref[...]).astype(o_ref.dtype)

    return pl.pallas_call(
        body, name=name, grid=(m // tm, n // tn),
        in_specs=[pl.BlockSpec((tm, k), lambda i, j: (i, 0)), b_spec],
        out_specs=pl.BlockSpec((tm, tn), lambda i, j: (i, j)),
        out_shape=jax.ShapeDtypeStruct((m, n), out_dtype),
        compiler_params=_params(("parallel", "parallel")),
    )(a, b)


_DOTS = {"nn": _dot, "nt": _dot_nt, "tn": _dot_tn}
_BS = pl.BlockSpec


def _mmcall(name, kind, a, b, a_spec, b_spec, o_spec, o_shape, grid, red_axis=None, out_dtype=F32):
    dot = _DOTS[kind]

    def body(a_ref, b_ref, o_ref):
        if red_axis is None:
            o_ref[...] = dot(a_ref[...], b_ref[...]).astype(o_ref.dtype)
        else:
            @pl.when(pl.program_id(red_axis) == 0)
            def _():
                o_ref[...] = jnp.zeros_like(o_ref)

            o_ref[...] += dot(a_ref[...], b_ref[...])

    sem = tuple("arbitrary" if ax == red_axis else "parallel" for ax in range(len(grid)))
    return pl.pallas_call(
        body, name=name, grid=grid, in_specs=[a_spec, b_spec], out_specs=o_spec,
        out_shape=jax.ShapeDtypeStruct(o_shape, out_dtype), compiler_params=_params(sem),
    )(a, b)


def _mm_stack_red(a, w, name, kind):
    s, t, n = a.shape
    nout = w.shape[2] if kind == "nn" else w.shape[1]
    tm = _pick(t, ROW_TILES)
    return _mmcall(name, kind, a, w, _BS((None, tm, n), lambda i, j: (j, i, 0)),
                   _BS((None,) + w.shape[1:], lambda i, j: (j, 0, 0)), _BS((tm, nout), lambda i, j: (i, 0)),
                   (t, nout), (t // tm, s), red_axis=1)


def _mm_stack_tn(a, b, name):
    grp = 4
    if a.ndim == 2:
        t, k = a.shape
        s, _, n = b.shape
        tt = _pick(t, (512, 256, 128))
        a_spec = _BS((tt, k), lambda j, r: (r, 0))
        b_spec, b_in = _BS((None, grp, tt, n), lambda j, r: (j, 0, r, 0)), b.reshape(s // grp, grp, t, n)
        a_in = a
    else:
        s, t, k = a.shape
        n = b.shape[1]
        tt = _pick(t, (512, 256, 128))
        a_spec, a_in = _BS((None, grp, tt, k), lambda j, r: (j, 0, r, 0)), a.reshape(s // grp, grp, t, k)
        b_spec, b_in = _BS((tt, n), lambda j, r: (r, 0)), b

    def body(a_ref, b_ref, o_ref):
        @pl.when(pl.program_id(1) == 0)
        def _():
            o_ref[...] = jnp.zeros_like(o_ref)

        shared = a_ref[...] if a.ndim == 2 else b_ref[...]
        for e in range(grp):
            o_ref[e] += _dot_tn(shared, b_ref[e]) if a.ndim == 2 else _dot_tn(a_ref[e], shared)

    return pl.pallas_call(
        body, name=name, grid=(s // grp, t // tt), in_specs=[a_spec, b_spec],
        out_specs=_BS((None, grp, k, n), lambda j, r: (j, 0, 0, 0)),
        out_shape=jax.ShapeDtypeStruct((s // grp, grp, k, n), F32),
        compiler_params=_params(("parallel", "arbitrary")),
    )(a_in, b_in).reshape(s, k, n)


def _cols_fwd(x, w, name):
    t, k = x.shape
    s, _, n = w.shape
    tm = _pick(t, ROW_TILES)

    def body(x_ref, w_ref, o_ref):
        x_ = x_ref[...]
        for j in range(s):
            o_ref[:, n * j:n * (j + 1)] = _dot(x_, w_ref[j])

    return pl.pallas_call(
        body, name=name, grid=(t // tm,),
        in_specs=[_BS((tm, k), lambda i: (i, 0)), _BS((s, k, n), lambda i: (0, 0, 0))],
        out_specs=_BS((tm, s * n), lambda i: (i, 0)), out_shape=jax.ShapeDtypeStruct((t, s * n), F32),
        compiler_params=_params(("parallel",)),
    )(x, w)


def _cols_dx(d, w, name):
    t = d.shape[0]
    s, k, n = w.shape
    tm = _pick(t, ROW_TILES)

    def body(d_ref, w_ref, o_ref):
        acc = _dot_nt(d_ref[:, 0:n], w_ref[0])
        for j in range(1, s):
            acc = acc + _dot_nt(d_ref[:, n * j:n * (j + 1)], w_ref[j])
        o_ref[...] = acc

    return pl.pallas_call(
        body, name=name, grid=(t // tm,),
        in_specs=[_BS((tm, s * n), lambda i: (i, 0)), _BS((s, k, n), lambda i: (0, 0, 0))],
        out_specs=_BS((tm, k), lambda i: (i, 0)), out_shape=jax.ShapeDtypeStruct((t, k), F32),
        compiler_params=_params(("parallel",)),
    )(d, w)


def _cols_dw(x, d, n, name):
    t, k = x.shape
    s = d.shape[1] // n
    tt = _pick(t, (512, 256, 128))

    def body(x_ref, d_ref, o_ref):
        @pl.when(pl.program_id(0) == 0)
        def _():
            o_ref[...] = jnp.zeros_like(o_ref)

        x_ = x_ref[...]
        for j in range(s):
            o_ref[j] += _dot_tn(x_, d_ref[:, n * j:n * (j + 1)])

    return pl.pallas_call(
        body, name=name, grid=(t // tt,),
        in_specs=[_BS((tt, k), lambda r: (r, 0)), _BS((tt, s * n), lambda r: (r, 0))],
        out_specs=_BS((s, k, n), lambda r: (0, 0, 0)), out_shape=jax.ShapeDtypeStruct((s, k, n), F32),
        compiler_params=_params(("arbitrary",)),
    )(x, d)


def _rows(fn, name, t, tm, ins, vecs, outs, accs=()):
    n_in, n_out, n_acc = len(ins) + len(vecs), len(outs), len(accs)

    def body(*refs):
        res = fn(*refs[:n_in + n_out])
        if n_acc:
            acc_refs = refs[n_in + n_out:]

            @pl.when(pl.program_id(0) == 0)
            def _():
                for r in acc_refs:
                    r[...] = jnp.zeros_like(r)

            for r, val in zip(acc_refs, res):
                r[...] += val

    in_specs = [pl.BlockSpec((tm, bw), functools.partial(lambda i, cb: (i, cb), cb=cb)) for _, bw, cb in ins]
    in_specs += [pl.BlockSpec(v.shape, lambda i: (0, 0)) for v in vecs]
    out_specs = [pl.BlockSpec((tm, w), lambda i: (i, 0)) for w, _ in outs]
    out_specs += [pl.BlockSpec(s, lambda i: (0, 0)) for s in accs]
    out_shape = [jax.ShapeDtypeStruct((t, w), dt) for w, dt in outs]
    out_shape += [jax.ShapeDtypeStruct(s, F32) for s in accs]
    return pl.pallas_call(
        body, name=name, grid=(t // tm,), in_specs=in_specs, out_specs=out_specs, out_shape=out_shape,
        compiler_params=_params(("arbitrary",) if n_acc else ("parallel",)),
    )(*[a for a, _, _ in ins], *vecs)


def _rms(x, g):
    return x * lax.rsqrt(jnp.mean(x * x, axis=-1, keepdims=True) + EPS) * g


def _rms_bwd(x, g, dy):
    xh = x * lax.rsqrt(jnp.mean(x * x, axis=-1, keepdims=True) + EPS)
    r = lax.rsqrt(jnp.mean(x * x, axis=-1, keepdims=True) + EPS)
    dyg = dy * g
    dx = r * (dyg - xh * jnp.mean(dyg * xh, axis=-1, keepdims=True))
    return dx, jnp.sum(dy * xh, axis=0, keepdims=True)


def _hgrn_mats():
    row = lax.broadcasted_iota(jnp.int32, (HB, HB), 0)
    col = lax.broadcasted_iota(jnp.int32, (HB, HB), 1)
    return row, col


def _hgrn_gates(qr, z, t0, t1):
    lb = 1.0 / (1.0 + jnp.exp(t1 - t0))
    sz = _sig(z)
    sneg = 1.0 / (1.0 + jnp.exp(z))
    f = lb + (1.0 - lb) * sz
    return lb, sz, sneg, f, jnp.log(f), (1.0 - lb) * sneg, qr * _sig(qr)


def _hgrn_scores(q, k, cum, cmid):
    qd = q * jnp.exp(jnp.minimum(cmid, EXP_CLAMP))
    qd_b = qd.astype(BF16)
    kds, parts = [], []
    for i in range(HB // SUB):
        mid = cum[SUB * i + SUB // 2 - 1:SUB * i + SUB // 2, :]
        kd = k * jnp.exp(jnp.minimum(mid - cum, EXP_CLAMP))
        kds.append(kd)
        parts.append(_dot_nt(qd_b[SUB * i:SUB * (i + 1)], kd))
    return qd, kds, jnp.concatenate(parts, axis=0)


def _hgrn_fwd(p_hg, table, gain, t):
    nblk = t // HB

    def body(q_ref, f_ref, i_ref, g_ref, tab_ref, gain_ref, o_ref, y_ref, st_ref, state):
        @pl.when(pl.program_id(0) == 0)
        def _():
            state[...] = jnp.zeros_like(state)

        row, col = _hgrn_mats()
        causal = col <= row
        tri = causal.astype(BF16)
        trimid = tri - (col <= (row // SUB) * SUB + SUB // 2 - 1).astype(BF16)
        for h in range(NH):
            sl = slice(HD * h, HD * (h + 1))
            v, gr = i_ref[:, sl], g_ref[:, sl]
            _, _, _, _, lf, k, q = _hgrn_gates(q_ref[:, sl], f_ref[:, sl], tab_ref[0:1, sl], tab_ref[1:2, sl])
            cum = _dot_sel(tri, lf)
            cmid = _dot_sel(trimid, lf)
            _, _, s = _hgrn_scores(q, k, cum, cmid)
            p = jnp.where(causal, s, 0.0)
            st = state[h]
            st_ref[h, 0] = st
            o = _dot(p, v) + _dot_nt(q * jnp.exp(cum), st)
            last = cum[HB - 1:HB, :]
            state[h] = st * jnp.exp(last) + _dot_tn(v, k * jnp.exp(last - cum))
            o_ref[:, sl] = o
            y_ref[:, sl] = (_rms(o, gain_ref[...]) * gr * _sig(gr)).astype(BF16)

    blk = lambda cb: pl.BlockSpec((HB, HGW), functools.partial(lambda n, cb: (n, cb), cb=cb))
    return pl.pallas_call(
        body, name="hgrn_fwd", grid=(nblk,),
        in_specs=[blk(0), blk(1), blk(2), blk(3), pl.BlockSpec((2, HGW), lambda n: (0, 0)),
                  pl.BlockSpec((1, HD), lambda n: (0, 0))],
        out_specs=[pl.BlockSpec((HB, HGW), lambda n: (n, 0)), pl.BlockSpec((HB, HGW), lambda n: (n, 0)),
                   pl.BlockSpec((NH, 1, HD, HD), lambda n: (0, n, 0, 0))],
        out_shape=[jax.ShapeDtypeStruct((t, HGW), F32), jax.ShapeDtypeStruct((t, HGW), BF16),
                   jax.ShapeDtypeStruct((NH, nblk, HD, HD), F32)],
        scratch_shapes=[pltpu.VMEM((NH, HD, HD), F32)],
        compiler_params=_params(("arbitrary",)),
    )(p_hg, p_hg, p_hg, p_hg, table, gain)


def _hgrn_bwd(p_hg, table, gain, o_pre, states, dy, t):
    nblk = t // HB

    def body(q_ref, f_ref, i_ref, g_ref, tab_ref, gain_ref, o_ref, st_ref, dy_ref, dp_ref, dtab_ref, dgain_ref,
             dstate):
        @pl.when(pl.program_id(0) == 0)
        def _():
            dstate[...] = jnp.zeros_like(dstate)
            dtab_ref[...] = jnp.zeros_like(dtab_ref)
            dgain_ref[...] = jnp.zeros_like(dgain_ref)

        row, col = _hgrn_mats()
        causal = col <= row
        tri = causal.astype(BF16)
        midrow = (row // SUB) * SUB + SUB // 2 - 1
        trimid = tri - (col <= midrow).astype(BF16)
        tri_t = (row <= col).astype(BF16)
        midcol = (col // SUB) * SUB + SUB // 2 - 1
        trimid_t = tri_t - (row <= midcol).astype(BF16)
        dgain = jnp.zeros((1, HD), F32)
        for h in range(NH):
            sl = slice(HD * h, HD * (h + 1))
            qr, z, v, gr = q_ref[:, sl], f_ref[:, sl], i_ref[:, sl], g_ref[:, sl]
            lb, sz, sneg, f, lf, k, q = _hgrn_gates(qr, z, tab_ref[0:1, sl], tab_ref[1:2, sl])
            cum = _dot_sel(tri, lf)
            cmid = _dot_sel(trimid, lf)
            qd, kds, s = _hgrn_scores(q, k, cum, cmid)
            p = jnp.where(causal, s, 0.0)
            st = st_ref[h, 0]
            dst = dstate[h]
            o = o_ref[:, sl]
            sg = _sig(gr)
            dyh = dy_ref[:, sl]
            on = _rms(o, gain_ref[...])
            dgr = dyh * on * sg * (1.0 + gr * (1.0 - sg))
            do, dg_h = _rms_bwd(o, gain_ref[...], dyh * gr * sg)
            dgain = dgain + dg_h
            do_b = do.astype(BF16)
            ecum = jnp.exp(cum)
            qc = q * ecum
            last = cum[HB - 1:HB, :]
            edec = jnp.exp(last - cum)
            kdec = k * edec
            dp = jnp.where(causal, _dot_nt(do_b, v), 0.0)
            dv = _dot(p.T, do_b) + _dot_nt(kdec, dst)
            dqc = _dot(do_b, st)
            dkdec = _dot(v, dst)
            dstate[h] = dst * jnp.exp(last) + _dot(do.T, qc)
            dp_b = dp.astype(BF16)
            dqd = jnp.concatenate([_dot(dp_b[SUB * i:SUB * (i + 1)], kds[i]) for i in range(HB // SUB)], axis=0)
            gq = dqd * qd
            dq = dqd * jnp.exp(jnp.minimum(cmid, EXP_CLAMP)) + dqc * ecum
            gs = dkdec * kdec
            dk = dkdec * edec
            dcum = dqc * qc - gs
            dcum = dcum + jnp.where(row == HB - 1, jnp.sum(gs, axis=0, keepdims=True)
                                    + jnp.exp(last) * jnp.sum(st * dst, axis=0, keepdims=True), 0.0)
            qd_b = qd.astype(BF16)
            for i in range(HB // SUB):
                dkd = _dot_tn(dp_b[SUB * i:SUB * (i + 1)], qd_b[SUB * i:SUB * (i + 1)])
                mid = cum[SUB * i + SUB // 2 - 1:SUB * i + SUB // 2, :]
                dk = dk + dkd * jnp.exp(jnp.minimum(mid - cum, EXP_CLAMP))
                gk = dkd * kds[i]
                dcum = dcum - gk + jnp.where(row == SUB * i + SUB // 2 - 1, jnp.sum(gk, axis=0, keepdims=True), 0.0)
            dlf = _dot_sel(tri_t, dcum) + _dot_sel(trimid_t, gq)
            df = dlf / f - dk
            dz = df * (1.0 - lb) * sz * sneg
            dlb = jnp.sum(df * sneg, axis=0, keepdims=True) * lb * (1.0 - lb)
            dtab_ref[0:1, sl] += dlb
            dtab_ref[1:2, sl] -= dlb
            sq = _sig(qr)
            dp_ref[:, sl] = (dq * sq * (1.0 + qr * (1.0 - sq))).astype(BF16)
            dp_ref[:, HGW + HD * h:HGW + HD * (h + 1)] = dz.astype(BF16)
            dp_ref[:, 2 * HGW + HD * h:2 * HGW + HD * (h + 1)] = dv.astype(BF16)
            dp_ref[:, 3 * HGW + HD * h:3 * HGW + HD * (h + 1)] = dgr.astype(BF16)
        dgain_ref[...] += dgain

    rev = lambda cb: pl.BlockSpec((HB, HGW), functools.partial(lambda n, cb: (nblk - 1 - n, cb), cb=cb))
    return pl.pallas_call(
        body, name="hgrn_bwd", grid=(nblk,),
        in_specs=[rev(0), rev(1), rev(2), rev(3), pl.BlockSpec((2, HGW), lambda n: (0, 0)),
                  pl.BlockSpec((1, HD), lambda n: (0, 0)), rev(0),
                  pl.BlockSpec((NH, 1, HD, HD), lambda n: (0, nblk - 1 - n, 0, 0)), rev(0)],
        out_specs=[pl.BlockSpec((HB, 4 * HGW), lambda n: (nblk - 1 - n, 0)),
                   pl.BlockSpec((2, HGW), lambda n: (0, 0)), pl.BlockSpec((1, HD), lambda n: (0, 0))],
        out_shape=[jax.ShapeDtypeStruct((t, 4 * HGW), BF16), jax.ShapeDtypeStruct((2, HGW), F32),
                   jax.ShapeDtypeStruct((1, HD), F32)],
        scratch_shapes=[pltpu.VMEM((NH, HD, HD), F32)],
        compiler_params=_params(("arbitrary",)),
    )(p_hg, p_hg, p_hg, p_hg, table, gain, o_pre, states, dy)


def _rope_mat():
    r = lax.broadcasted_iota(jnp.int32, (ROPE, ROPE), 0)
    c = lax.broadcasted_iota(jnp.int32, (ROPE, ROPE), 1)
    half = ROPE // 2
    return ((r == c - half).astype(F32) - (r == c + half).astype(F32)).astype(BF16)


def _mla_prep_fwd(p_mla, cs, sn, wq, wkv, gql, gkvl, gq, gk, t):
    tm = _pick(t, (512, 256, 128))

    def body(p_ref, cs_ref, sn_ref, wq_ref, wkv_ref, gql_ref, gkvl_ref, gq_ref, gk_ref,
             q_ref, k_ref, v_ref):
        rmat = _rope_mat()
        cqn = _rms(p_ref[:, 0:QL], gql_ref[...]).astype(BF16)
        ckvn = _rms(p_ref[:, QL:QL + KVL], gkvl_ref[...]).astype(BF16)
        kpe = p_ref[:, QL + KVL:QL + KVL + ROPE]
        c, s = cs_ref[...], sn_ref[...]
        rot = lambda x: x * c + _sel_dot(x, rmat) * s
        for h in range(NH):
            qa, qr = _dot(cqn, wq_ref[h, :, 0:HD]), _dot(cqn, wq_ref[h, :, HD:QK])
            rq = lax.rsqrt((jnp.sum(qa * qa, -1, keepdims=True) + jnp.sum(qr * qr, -1, keepdims=True)) / QK + EPS)
            q_ref[h, :, 0:HD] = (qa * rq * gq_ref[:, 0:HD] * SCALE).astype(BF16)
            q_ref[h, :, HD:QK] = (rot(qr * rq * gq_ref[:, HD:QK]) * SCALE).astype(BF16)
            kn = _dot(ckvn, wkv_ref[h, :, 0:HD])
            rk = lax.rsqrt((jnp.sum(kn * kn, -1, keepdims=True) + jnp.sum(kpe * kpe, -1, keepdims=True)) / QK + EPS)
            k_ref[h, :, 0:HD] = (kn * rk * gk_ref[:, 0:HD]).astype(BF16)
            k_ref[h, :, HD:QK] = rot(kpe * rk * gk_ref[:, HD:QK]).astype(BF16)
            v_ref[h] = _dot(ckvn, wkv_ref[h, :, HD:2 * HD]).astype(BF16)

    whole = lambda a: pl.BlockSpec(a.shape, functools.partial(lambda i, nd: (0,) * nd, nd=a.ndim))
    return pl.pallas_call(
        body, name="mla_prep_fwd", grid=(t // tm,),
        in_specs=[pl.BlockSpec((tm, QL + KVL + ROPE), lambda i: (i, 0)), pl.BlockSpec((tm, ROPE), lambda i: (i, 0)),
                  pl.BlockSpec((tm, ROPE), lambda i: (i, 0))] + [whole(a) for a in (wq, wkv, gql, gkvl, gq, gk)],
        out_specs=[pl.BlockSpec((NH, tm, QK), lambda i: (0, i, 0)), pl.BlockSpec((NH, tm, QK), lambda i: (0, i, 0)),
                   pl.BlockSpec((NH, tm, HD), lambda i: (0, i, 0))],
        out_shape=[jax.ShapeDtypeStruct((NH, t, QK), BF16), jax.ShapeDtypeStruct((NH, t, QK), BF16),
                   jax.ShapeDtypeStruct((NH, t, HD), BF16)],
        compiler_params=_params(("parallel",)),
    )(p_mla, cs, sn, wq, wkv, gql, gkvl, gq, gk)


def _mla_prep_bwd(p_mla, cs, sn, wq, wkv, gql, gkvl, gq, gk, dq, dk, dv, t):
    tm = _pick(t, (512, 256, 128))

    def body(p_ref, cs_ref, sn_ref, wq_ref, wkv_ref, gql_ref, gkvl_ref, gq_ref, gk_ref,
             dq_ref, dk_ref, dv_ref,
             dp_ref, dwq_ref, dwkv_ref, dgql_ref, dgkvl_ref, dgq_ref, dgk_ref):
        accs = (dwq_ref, dwkv_ref, dgql_ref, dgkvl_ref, dgq_ref, dgk_ref)

        @pl.when(pl.program_id(0) == 0)
        def _():
            for r in accs:
                r[...] = jnp.zeros_like(r)

        rmat = _rope_mat()
        rmat_t = -rmat
        cq, ckv = p_ref[:, 0:QL], p_ref[:, QL:QL + KVL]
        kpe = p_ref[:, QL + KVL:QL + KVL + ROPE]
        cqn_f, ckvn_f = _rms(cq, gql_ref[...]), _rms(ckv, gkvl_ref[...])
        cqn, ckvn = cqn_f.astype(BF16), ckvn_f.astype(BF16)
        cqn_t, ckvn_t = cqn_f.T.astype(BF16), ckvn_f.T.astype(BF16)
        c, s = cs_ref[...], sn_ref[...]
        unrot = lambda dy: dy * c + _sel_dot(dy * s, rmat_t)
        dcqn = jnp.zeros((tm, QL), F32)
        dckvn = jnp.zeros((tm, KVL), F32)
        dkpe = jnp.zeros((tm, ROPE), F32)
        dgq_a, dgq_r = jnp.zeros((1, HD), F32), jnp.zeros((1, ROPE), F32)
        dgk_a, dgk_r = jnp.zeros((1, HD), F32), jnp.zeros((1, ROPE), F32)
        for h in range(NH):
            qa, qr = _dot(cqn, wq_ref[h, :, 0:HD]), _dot(cqn, wq_ref[h, :, HD:QK])
            rq = lax.rsqrt((jnp.sum(qa * qa, -1, keepdims=True) + jnp.sum(qr * qr, -1, keepdims=True)) / QK + EPS)
            xa, xr = qa * rq, qr * rq
            dya = dq_ref[h, :, 0:HD] * SCALE
            dyr = unrot(dq_ref[h, :, HD:QK] * SCALE)
            dgq_a += jnp.sum(dya * xa, axis=0, keepdims=True)
            dgq_r += jnp.sum(dyr * xr, axis=0, keepdims=True)
            ga, gr_ = dya * gq_ref[:, 0:HD], dyr * gq_ref[:, HD:QK]
            mean = (jnp.sum(ga * xa, -1, keepdims=True) + jnp.sum(gr_ * xr, -1, keepdims=True)) / QK
            dqa = (rq * (ga - xa * mean)).astype(BF16)
            dqr = (rq * (gr_ - xr * mean)).astype(BF16)
            dwq_ref[h, :, 0:HD] += _dot(cqn_t, dqa)
            dwq_ref[h, :, HD:QK] += _dot(cqn_t, dqr)
            dcqn += _dot_nt(dqa, wq_ref[h, :, 0:HD]) + _dot_nt(dqr, wq_ref[h, :, HD:QK])
            kn = _dot(ckvn, wkv_ref[h, :, 0:HD])
            rk = lax.rsqrt((jnp.sum(kn * kn, -1, keepdims=True) + jnp.sum(kpe * kpe, -1, keepdims=True)) / QK + EPS)
            ya, yr = kn * rk, kpe * rk
            dka = dk_ref[h, :, 0:HD]
            dkr = unrot(dk_ref[h, :, HD:QK])
            dgk_a += jnp.sum(dka * ya, axis=0, keepdims=True)
            dgk_r += jnp.sum(dkr * yr, axis=0, keepdims=True)
            ha, hr = dka * gk_ref[:, 0:HD], dkr * gk_ref[:, HD:QK]
            mean = (jnp.sum(ha * ya, -1, keepdims=True) + jnp.sum(hr * yr, -1, keepdims=True)) / QK
            dkn = (rk * (ha - ya * mean)).astype(BF16)
            dkpe += rk * (hr - yr * mean)
            dvh = dv_ref[h].astype(BF16)
            dwkv_ref[h, :, 0:HD] += _dot(ckvn_t, dkn)
            dwkv_ref[h, :, HD:2 * HD] += _dot(ckvn_t, dvh)
            dckvn += _dot_nt(dkn, wkv_ref[h, :, 0:HD]) + _dot_nt(dvh, wkv_ref[h, :, HD:2 * HD])
        dcq, dg1 = _rms_bwd(cq, gql_ref[...], dcqn)
        dckv, dg2 = _rms_bwd(ckv, gkvl_ref[...], dckvn)
        dp_ref[:, 0:QL] = dcq.astype(BF16)
        dp_ref[:, QL:QL + KVL] = dckv.astype(BF16)
        dp_ref[:, QL + KVL:QL + KVL + ROPE] = dkpe.astype(BF16)
        dgql_ref[...] += dg1
        dgkvl_ref[...] += dg2
        dgq_ref[:, 0:HD] += dgq_a
        dgq_ref[:, HD:QK] += dgq_r
        dgk_ref[:, 0:HD] += dgk_a
        dgk_ref[:, HD:QK] += dgk_r

    whole = lambda a: pl.BlockSpec(a.shape, functools.partial(lambda i, nd: (0,) * nd, nd=a.ndim))
    acc_shapes = [wq.shape, wkv.shape, gql.shape, gkvl.shape, gq.shape, gk.shape]
    return pl.pallas_call(
        body, name="mla_prep_bwd", grid=(t // tm,),
        in_specs=[pl.BlockSpec((tm, QL + KVL + ROPE), lambda i: (i, 0)), pl.BlockSpec((tm, ROPE), lambda i: (i, 0)),
                  pl.BlockSpec((tm, ROPE), lambda i: (i, 0))]
        + [whole(a) for a in (wq, wkv, gql, gkvl, gq, gk)]
        + [pl.BlockSpec((NH, tm, QK), lambda i: (0, i, 0)), pl.BlockSpec((NH, tm, QK), lambda i: (0, i, 0)),
           pl.BlockSpec((NH, tm, HD), lambda i: (0, i, 0))],
        out_specs=[pl.BlockSpec((tm, QL + KVL + ROPE), lambda i: (i, 0))]
        + [pl.BlockSpec(s, functools.partial(lambda i, nd: (0,) * nd, nd=len(s))) for s in acc_shapes],
        out_shape=[jax.ShapeDtypeStruct((t, QL + KVL + ROPE), BF16)]
        + [jax.ShapeDtypeStruct(s, F32) for s in acc_shapes],
        compiler_params=_params(("arbitrary",)),
    )(p_mla, cs, sn, wq, wkv, gql, gkvl, gq, gk, dq, dk, dv)


def _chunk_mask(nq, nk, key0, keys_on_rows):
    shape = (nk, nq) if keys_on_rows else (nq, nk)
    qi = lax.broadcasted_iota(jnp.int32, shape, 1 if keys_on_rows else 0) // CHUNK
    ki = lax.broadcasted_iota(jnp.int32, shape, 0 if keys_on_rows else 1) // CHUNK + key0 // CHUNK
    return ki <= qi


def _flash_fwd(q, k, v, t):
    tq = _pick(t, ATT_Q)
    tk = tq // ATT_KEY_TILES

    def body(q_ref, k_ref, v_ref, o_ref, lse_ref):
        i = pl.program_id(1)
        qt = q_ref[0]

        def step(j, carry, key0):
            m, l, acc = carry
            cols = pl.ds(pl.multiple_of(j * tk, tk), tk)
            s = _dot_nt(qt, k_ref[0, cols, :])
            if key0 is not None:
                s = jnp.where(_chunk_mask(tq, tk, key0, False), s, -jnp.inf)
            m_new = jnp.maximum(m, jnp.max(s, axis=-1, keepdims=True))
            p = jnp.exp(s - m_new)
            alpha = jnp.exp(m - m_new)
            return m_new, alpha * l + jnp.sum(p, axis=-1, keepdims=True), alpha * acc + _dot(p, v_ref[0, cols, :])

        init = (jnp.full((tq, 1), -jnp.inf, F32), jnp.zeros((tq, 1), F32), jnp.zeros((tq, HD), F32))
        carry = lax.fori_loop(0, ATT_KEY_TILES * i, lambda j, cr: step(j, cr, None), init)
        for h in range(ATT_KEY_TILES):
            carry = step(ATT_KEY_TILES * i + h, carry, h * tk)
        m, l, acc = carry
        o_ref[...] = (acc / l).astype(BF16)
        lse_ref[0] = m + jnp.log(l)

    return pl.pallas_call(
        body, name="flash_fwd", grid=(NH, t // tq),
        in_specs=[pl.BlockSpec((1, tq, QK), lambda h, i: (h, i, 0)), pl.BlockSpec((1, t, QK), lambda h, i: (h, 0, 0)),
                  pl.BlockSpec((1, t, HD), lambda h, i: (h, 0, 0))],
        out_specs=[pl.BlockSpec((tq, HD), lambda h, i: (i, h)), pl.BlockSpec((1, tq, 1), lambda h, i: (h, i, 0))],
        out_shape=[jax.ShapeDtypeStruct((t, NH * HD), BF16), jax.ShapeDtypeStruct((NH, t, 1), F32)],
        compiler_params=_params(("parallel", "parallel")),
    )(q, k, v)


def _flash_delta(o, do, t):
    tq = _pick(t, ATT_TILES)

    def body(o_ref, do_ref, delta_ref):
        prod = do_ref[...].astype(F32) * o_ref[...].astype(F32)
        x1, x2, x3 = _split3(prod)
        ones = jnp.ones((8, HD), BF16)
        d = lambda p: lax.dot_general(ones, p, (((1,), (1,)), ((), ())), preferred_element_type=F32)
        delta_ref[0] = (d(x1) + d(x2) + d(x3))[0:1, :]

    return pl.pallas_call(
        body, name="flash_delta", grid=(NH, t // tq),
        in_specs=[pl.BlockSpec((tq, HD), lambda h, i: (i, h)), pl.BlockSpec((tq, HD), lambda h, i: (i, h))],
        out_specs=pl.BlockSpec((1, 1, tq), lambda h, i: (h, 0, i)),
        out_shape=jax.ShapeDtypeStruct((NH, 1, t), F32), compiler_params=_params(("parallel", "parallel")),
    )(o, do)


def _flash_bwd(q, k, v, do, lse_row, delta_row, t):
    tq = _pick(t, ATT_Q)
    tk = tq // ATT_KEY_TILES

    def body(q_ref, k_ref, v_ref, do_ref, lse_ref, delta_ref, dq_ref, dk_ref, dv_ref):
        j = pl.program_id(1)

        @pl.when(j == 0)
        def _():
            dq_ref[...] = jnp.zeros_like(dq_ref)

        kt, vt = k_ref[0], v_ref[0]

        def step(i, carry, key0):
            dk, dv = carry
            rows = pl.ds(pl.multiple_of(i * tq, tq), tq)
            qt, dot_ = q_ref[0, rows, :], do_ref[rows, :]
            p = jnp.exp(_dot_nt(kt, qt) - lse_ref[0, :, rows])
            if key0 is not None:
                p = jnp.where(_chunk_mask(tq, tk, key0, True), p, 0.0)
            ds = (p * (_dot_nt(vt, dot_) - delta_ref[0, :, rows])).astype(BF16)
            dq_ref[0, rows, :] += _dot_tn(ds, kt)
            return dk + _dot(ds, qt), dv + _dot(p, dot_)

        own = j // ATT_KEY_TILES
        carry = step(own, (jnp.zeros((tk, QK), F32), jnp.zeros((tk, HD), F32)), (j % ATT_KEY_TILES) * tk)
        dk, dv = lax.fori_loop(own + 1, t // tq, lambda i, cr: step(i, cr, None), carry)
        dk_ref[0] = dk
        dv_ref[0] = dv

    return pl.pallas_call(
        body, name="flash_bwd", grid=(NH, t // tk),
        in_specs=[pl.BlockSpec((1, t, QK), lambda h, j: (h, 0, 0)), pl.BlockSpec((1, tk, QK), lambda h, j: (h, j, 0)),
                  pl.BlockSpec((1, tk, HD), lambda h, j: (h, j, 0)), pl.BlockSpec((t, HD), lambda h, j: (0, h)),
                  pl.BlockSpec((1, 1, t), lambda h, j: (h, 0, 0)), pl.BlockSpec((1, 1, t), lambda h, j: (h, 0, 0))],
        out_specs=[pl.BlockSpec((1, t, QK), lambda h, j: (h, 0, 0)), pl.BlockSpec((1, tk, QK), lambda h, j: (h, j, 0)),
                   pl.BlockSpec((1, tk, HD), lambda h, j: (h, j, 0))],
        out_shape=[jax.ShapeDtypeStruct((NH, t, QK), F32), jax.ShapeDtypeStruct((NH, t, QK), F32),
                   jax.ShapeDtypeStruct((NH, t, HD), F32)],
        compiler_params=_params(("parallel", "arbitrary")),
    )(q, k, v, do, lse_row, delta_row)


def _ffn_in(xn, w_in, name):
    t, k = xn.shape
    s, _, n = w_in.shape
    tm = _pick(t, ROW_TILES)

    def body(x_ref, w_ref, gu_ref, a_ref):
        x = x_ref[...]
        g, u = _dot(x, w_ref[0]), _dot(x, w_ref[1])
        gu_ref[0] = g.astype(BF16)
        gu_ref[1] = u.astype(BF16)
        a_ref[...] = (g * _sig(g) * u).astype(BF16)

    return pl.pallas_call(
        body, name=name, grid=(t // tm, s // 2),
        in_specs=[_BS((tm, k), lambda i, j: (i, 0)), _BS((2, None, k, n), lambda i, j: (0, j, 0, 0))],
        out_specs=[_BS((2, None, tm, n), lambda i, j: (0, j, i, 0)), _BS((None, tm, n), lambda i, j: (j, i, 0))],
        out_shape=[jax.ShapeDtypeStruct((2, s // 2, t, n), BF16), jax.ShapeDtypeStruct((s // 2, t, n), BF16)],
        compiler_params=_params(("parallel", "parallel")),
    )(xn, w_in.reshape(2, s // 2, k, n))


def _ffn_dgu(dfo, w_out, gu, name):
    t, k = dfo.shape
    s, n, _ = w_out.shape
    tm = _pick(t, ROW_TILES)

    def body(d_ref, w_ref, gu_ref, o_ref):
        da = _dot_nt(d_ref[...], w_ref[...])
        g, u = gu_ref[0].astype(F32), gu_ref[1].astype(F32)
        sg = _sig(g)
        o_ref[0] = (da * u * sg * (1.0 + g * (1.0 - sg))).astype(BF16)
        o_ref[1] = (da * g * sg).astype(BF16)

    pair = _BS((2, None, tm, n), lambda i, j: (0, j, i, 0))
    return pl.pallas_call(
        body, name=name, grid=(t // tm, s),
        in_specs=[_BS((tm, k), lambda i, j: (i, 0)), _BS((None, n, k), lambda i, j: (j, 0, 0)), pair],
        out_specs=pair, out_shape=jax.ShapeDtypeStruct((2, s, t, n), BF16),
        compiler_params=_params(("parallel", "parallel")),
    )(dfo, w_out, gu).reshape(2 * s, t, n)


def _ffn_fwd(xn, w_in, w_out, tag):
    gu, a = _ffn_in(xn, w_in, tag + "_in")
    return gu, a, _mm_stack_red(a, w_out, tag + "_out", "nn")


def _ffn_bwd(dfo, xn, gu, a, w_in, w_out, tag):
    dw_out = _mm_stack_tn(a, dfo, tag + "_dwout")
    dgu = _ffn_dgu(dfo, w_out, gu, tag + "_dgu")
    return _mm_stack_red(dgu, w_in, tag + "_dxn", "nt"), _mm_stack_tn(xn, dgu, tag + "_dwin"), dw_out


def _local_step(x, target, cs, sn, w, s):
    t = x.shape[0]
    tm = _pick(t, (256, 128))
    g = {}
    rows_of = lambda n: w[n].reshape(-1, w[n].shape[2])
    ffn_out = lambda n: w[n].reshape(4, FF // 4, D)
    w_hgb, w_mlab, w_o = rows_of("w_hg_branch"), rows_of("w_mla_branch"), rows_of("w_out")
    w_in_nat = w["w_in"].transpose(1, 0, 2).reshape(D, -1)
    w_mrg = w["w_merge"]
    mw = w_mrg.shape[2]

    def norm_fn(x_ref, g_ref, o_ref):
        o_ref[...] = _rms(x_ref[...], g_ref[...]).astype(BF16)

    xn1 = _rows(norm_fn, "norm1", t, tm, [(x, D, 0)], [s["ffn1_norm"]], [(D, BF16)])[0]
    gu1, a1, f1 = _ffn_fwd(xn1, w["ffn1_w_in"], ffn_out("ffn1_w_out"), "ffn1")

    def res_norm_fn(scale):
        def fn(h_ref, f_ref, g_ref, h_out, n_out):
            h = h_ref[...] + scale * f_ref[...]
            h_out[...] = h
            n_out[...] = _rms(h, g_ref[...]).astype(BF16)
        return fn

    h1, u = _rows(res_norm_fn(0.5), "res_norm1", t, tm, [(x, D, 0), (f1, D, 0)], [s["mix_norm"]],
                  [(D, F32), (D, BF16)])
    w_in_hg, w_in_mla = w_in_nat[:, :4 * HGW], w_in_nat[:, 4 * HGW:]
    p_hg = _mm(u, w_in_hg, "nn", "proj_hg")
    p_mla = _mm(u, w_in_mla, "nn", "proj_mla")
    gpre = _cols_fwd(u, w_mrg, "proj_gate")
    o_pre, hgy, states = _hgrn_fwd(p_hg, s["hg_lb_table"], s["hg_out_norm"], t)
    prep_args = (p_mla, cs, sn, w["w_q_up"], w["w_kv_up"], s["mla_q_lora_norm"], s["mla_kv_lora_norm"],
                 s["q_head_norm"], s["k_head_norm"])
    q, k, v = _mla_prep_fwd(*prep_args, t)
    att, lse = _flash_fwd(q, k, v, t)
    y_hg = _mm(hgy, w_hgb, "nn", "branch_hg")
    y_mla = _mm(att, w_mlab, "nn", "branch_mla")

    def mix_fn(gh_ref, gm_ref, yh_ref, ym_ref, b_ref, o_ref):
        gh = _sig(gh_ref[...] + b_ref[:, 0:D])
        gm = _sig(gm_ref[...] + b_ref[:, D:2 * D])
        o_ref[...] = (gh * yh_ref[...] + gm * ym_ref[...]).astype(BF16)

    mixed = _rows(mix_fn, "mix", t, tm, [(gpre, D, 0), (gpre, D, 1), (y_hg, D, 0), (y_mla, D, 0)], [s["b_merge"]],
                  [(D, BF16)])[0]
    mo = _mm(mixed, w_o, "nn", "mix_out")
    h2, xn2 = _rows(res_norm_fn(1.0), "res_norm2", t, tm, [(h1, D, 0), (mo, D, 0)], [s["ffn2_norm"]],
                    [(D, F32), (D, BF16)])
    gu2, a2, f2 = _ffn_fwd(xn2, w["ffn2_w_in"], ffn_out("ffn2_w_out"), "ffn2")

    def loss_fn(h_ref, f_ref, tg_ref, g_ref, dh_out, dhb_out):
        h = h_ref[...] + 0.5 * f_ref[...]
        e = _rms(h, g_ref[...]) - tg_ref[...]
        dh, dgain = _rms_bwd(h, g_ref[...], e / D)
        dh_out[...] = dh
        dhb_out[...] = (0.5 * dh).astype(BF16)
        return dgain, jnp.full((1, LANE), 0.5 / D * jnp.sum(e * e), F32)

    dh3, dfo2, g["final_norm"], loss = _rows(loss_fn, "loss", t, tm, [(h2, D, 0), (f2, D, 0), (target, D, 0)],
                                             [s["final_norm"]], [(D, F32), (D, BF16)], [(1, D), (1, LANE)])

    def norm_bwd_fn(scale):
        def fn(h_ref, dxn_ref, dh_ref, g_ref, dh_out, dhb_out):
            dx, dgain = _rms_bwd(h_ref[...], g_ref[...], dxn_ref[...])
            dh = dh_ref[...] + dx
            dh_out[...] = dh
            dhb_out[...] = (scale * dh).astype(BF16)
            return (dgain,)
        return fn

    as_rows = lambda a: a.reshape((N_DEV, -1) + a.shape[-1:])
    dxn2, g["ffn2_w_in"], dwo = _ffn_bwd(dfo2, xn2, gu2, a2, w["ffn2_w_in"], ffn_out("ffn2_w_out"), "ffn2")
    g["ffn2_w_out"] = as_rows(dwo)
    dh2, dh2b, g["ffn2_norm"] = _rows(norm_bwd_fn(1.0), "norm2_bwd", t, tm, [(h2, D, 0), (dxn2, D, 0), (dh3, D, 0)],
                                      [s["ffn2_norm"]], [(D, F32), (D, BF16)], [(1, D)])
    dmixed = _mm(dh2b, w_o, "nt", "mix_out_dx")
    g["w_out"] = as_rows(_mm(mixed, dh2b, "tn", "mix_out_dw"))

    def mix_bwd_fn(gh_ref, gm_ref, yh_ref, ym_ref, dm_ref, b_ref, dyh_out, dym_out, dg_out):
        gh = _sig(gh_ref[...] + b_ref[:, 0:D])
        gm = _sig(gm_ref[...] + b_ref[:, D:2 * D])
        dm = dm_ref[...]
        dyh_out[...] = (dm * gh).astype(BF16)
        dym_out[...] = (dm * gm).astype(BF16)
        dgh = dm * yh_ref[...] * gh * (1.0 - gh)
        dgm = dm * ym_ref[...] * gm * (1.0 - gm)
        dg_out[:, 0:D] = dgh.astype(BF16)
        dg_out[:, D:2 * D] = dgm.astype(BF16)
        return (jnp.concatenate([jnp.sum(dgh, axis=0, keepdims=True), jnp.sum(dgm, axis=0, keepdims=True)], axis=1),)

    dyh, dym, dgpre, g["b_merge"] = _rows(
        mix_bwd_fn, "mix_bwd", t, tm, [(gpre, D, 0), (gpre, D, 1), (y_hg, D, 0), (y_mla, D, 0), (dmixed, D, 0)],
        [s["b_merge"]], [(D, BF16), (D, BF16), (2 * D, BF16)], [(1, 2 * D)])
    g["w_hg_branch"] = as_rows(_mm(hgy, dyh, "tn", "branch_hg_dw"))
    g["w_mla_branch"] = as_rows(_mm(att, dym, "tn", "branch_mla_dw"))
    g["w_merge"] = _cols_dw(u, dgpre, mw, "proj_gate_dw")
    dhgy = _mm(dyh, w_hgb, "nt", "branch_hg_dx")
    datt = _mm(dym, w_mlab, "nt", "branch_mla_dx", out_dtype=BF16)
    du_gate = _cols_dx(dgpre, w_mrg, "proj_gate_dx")

    dq, dk, dv = _flash_bwd(q, k, v, datt, lse.reshape(NH, 1, t), _flash_delta(att, datt, t), t)
    (dp_mla, g["w_q_up"], g["w_kv_up"], g["mla_q_lora_norm"], g["mla_kv_lora_norm"], g["q_head_norm"],
     g["k_head_norm"]) = _mla_prep_bwd(*prep_args, dq, dk, dv, t)
    dp_hg, g["hg_lb_table"], g["hg_out_norm"] = _hgrn_bwd(p_hg, s["hg_lb_table"], s["hg_out_norm"], o_pre, states,
                                                          dhgy, t)
    dw_in_nat = jnp.concatenate([_mm(u, dp_hg, "tn", "proj_hg_dw"), _mm(u, dp_mla, "tn", "proj_mla_dw")], axis=1)
    g["w_in"] = dw_in_nat.reshape(D, N_DEV, -1).transpose(1, 0, 2)
    du_hg = _mm(dp_hg, w_in_hg, "nt", "proj_hg_dx")
    du_mla = _mm(dp_mla, w_in_mla, "nt", "proj_mla_dx")

    def mixnorm_bwd_fn(h_ref, a_ref, b_ref, c_ref, dh_ref, g_ref, dh_out, dhb_out):
        dx, dgain = _rms_bwd(h_ref[...], g_ref[...], a_ref[...] + b_ref[...] + c_ref[...])
        dh = dh_ref[...] + dx
        dh_out[...] = dh
        dhb_out[...] = (0.5 * dh).astype(BF16)
        return (dgain,)

    dh1, dfo1, g["mix_norm"] = _rows(mixnorm_bwd_fn, "mixnorm_bwd", t, tm,
                                     [(h1, D, 0), (du_hg, D, 0), (du_mla, D, 0), (du_gate, D, 0), (dh2, D, 0)],
                                     [s["mix_norm"]], [(D, F32), (D, BF16)], [(1, D)])
    dxn1, g["ffn1_w_in"], dwo = _ffn_bwd(dfo1, xn1, gu1, a1, w["ffn1_w_in"], ffn_out("ffn1_w_out"), "ffn1")
    g["ffn1_w_out"] = as_rows(dwo)
    grad_x, _, g["ffn1_norm"] = _rows(norm_bwd_fn(1.0), "norm1_bwd", t, tm, [(x, D, 0), (dxn1, D, 0), (dh1, D, 0)],
                                      [s["ffn1_norm"]], [(D, F32), (D, BF16)], [(1, D)])
    return loss, grad_x, g


def _coords():
    return lax.axis_index("x"), lax.axis_index("y"), lax.axis_index("c")


def _hbm_call(body, name, ins, out_shapes, scratch):
    any_spec = pl.BlockSpec(memory_space=pl.ANY)
    return pl.pallas_call(
        body, name=name, out_shape=[jax.ShapeDtypeStruct(s, dt) for s, dt in out_shapes],
        in_specs=[any_spec] * len(ins), out_specs=[any_spec] * len(out_shapes), scratch_shapes=scratch,
    )(*ins)


def _all_gather(blocks, name):
    nb = len(blocks)

    def body(*refs):
        x_refs, out_refs = refs[:nb], refs[nb:2 * nb]
        send_sems, recv_sems, local_sems = refs[2 * nb:]
        x, y, c = _coords()
        me, sibling = (x, y, c), (x, y, 1 - c)
        chips = [(1 - x, y), (x, 1 - y), (1 - x, 1 - y)]

        def slot(b, px, py, pc):
            return out_refs[b].at[4 * px + 2 * py + pc]

        def copy(b, kk, block_of, to, src=None):
            return pltpu.make_async_remote_copy(
                src_ref=slot(b, *block_of) if src is None else src, dst_ref=slot(b, *block_of),
                send_sem=send_sems.at[b, kk], recv_sem=recv_sems.at[b, kk], device_id=to, device_id_type=MESH)

        mine = [pltpu.make_async_copy(x_refs[b], slot(b, *me), local_sems.at[b]) for b in range(nb)]
        for cp in mine:
            cp.start()
        first = [copy(b, 0, me, sibling, src=x_refs[b]) for b in range(nb)]
        first += [copy(b, 1 + j, me, (*chip, c), src=x_refs[b]) for j, chip in enumerate(chips) for b in range(nb)]
        for cp in first:
            cp.start()
        passed = []
        for j, chip in enumerate(chips):
            for b in range(nb):
                copy(b, 1 + j, (*chip, c), me).wait_recv()
                passed.append(copy(b, 4 + j, (*chip, c), sibling))
                passed[-1].start()
        for b in range(nb):
            copy(b, 0, sibling, me).wait_recv()
        for j, chip in enumerate(chips):
            for b in range(nb):
                copy(b, 4 + j, (*chip, 1 - c), me).wait_recv()
        for cp in first + passed:
            cp.wait_send()
        for cp in mine:
            cp.wait()

    return _hbm_call(body, name, blocks, [((N_DEV,) + b.shape, b.dtype) for b in blocks],
                     [pltpu.SemaphoreType.DMA((nb, 7)), pltpu.SemaphoreType.DMA((nb, 7)),
                      pltpu.SemaphoreType.DMA((nb,))])


def _sibling_swap(bufs, name):
    nb = len(bufs)

    def body(*refs):
        x_refs, out_refs = refs[:nb], refs[nb:2 * nb]
        send_sems, recv_sems = refs[2 * nb:]
        x, y, c = _coords()
        copies = [pltpu.make_async_remote_copy(
            src_ref=x_refs[b].at[2 * q + 1 - c], dst_ref=out_refs[b].at[q], send_sem=send_sems.at[b, q],
            recv_sem=recv_sems.at[b, q], device_id=(x, y, 1 - c), device_id_type=MESH)
            for b in range(nb) for q in range(4)]
        for cp in copies:
            cp.start()
        for cp in copies:
            cp.wait()

    return _hbm_call(body, name, bufs, [((4,) + b.shape[1:], b.dtype) for b in bufs],
                     [pltpu.SemaphoreType.DMA((nb, 4)), pltpu.SemaphoreType.DMA((nb, 4))])


def _chip_exchange(bufs, name):
    nb = len(bufs)

    def body(*refs):
        x_refs, out_refs = refs[:nb], refs[nb:2 * nb]
        send_sems, recv_sems, local_sems = refs[2 * nb:]
        x, y, c = _coords()
        chips = [(1 - x, y), (x, 1 - y), (1 - x, 1 - y)]
        mine = [pltpu.make_async_copy(x_refs[b].at[2 * x + y], out_refs[b].at[2 * x + y], local_sems.at[b])
                for b in range(nb)]
        for cp in mine:
            cp.start()
        copies = [pltpu.make_async_remote_copy(
            src_ref=x_refs[b].at[2 * px + py], dst_ref=out_refs[b].at[2 * x + y], send_sem=send_sems.at[b, j],
            recv_sem=recv_sems.at[b, j], device_id=(px, py, c), device_id_type=MESH)
            for j, (px, py) in enumerate(chips) for b in range(nb)]
        for cp in copies:
            cp.start()
        for j, (px, py) in enumerate(chips):
            for b in range(nb):
                pltpu.make_async_remote_copy(
                    src_ref=x_refs[b].at[2 * x + y], dst_ref=out_refs[b].at[2 * px + py], send_sem=send_sems.at[b, j],
                    recv_sem=recv_sems.at[b, j], device_id=(px, py, c), device_id_type=MESH).wait_recv()
        for cp in copies:
            cp.wait_send()
        for cp in mine:
            cp.wait()

    return _hbm_call(body, name, bufs, [(b.shape, b.dtype) for b in bufs],
                     [pltpu.SemaphoreType.DMA((nb, 3)), pltpu.SemaphoreType.DMA((nb, 3)),
                      pltpu.SemaphoreType.DMA((nb,))])


def _chip_sum(g, r1, c, name):
    _, r, cw = g.shape
    tr = _pick(r, (256, 176, 128))

    def body(c_ref, g_ref, r_ref, o_ref):
        o_ref[...] = (g_ref[...] + r_ref[...]).astype(BF16)

    grid_spec = pltpu.PrefetchScalarGridSpec(
        num_scalar_prefetch=1, grid=(4, r // tr),
        in_specs=[_BS((None, None, tr, cw), lambda q, i, c_ref: (q, c_ref[0], i, 0)),
                  _BS((None, tr, cw), lambda q, i, c_ref: (q, i, 0))],
        out_specs=_BS((None, tr, cw), lambda q, i, c_ref: (q, i, 0)))
    return pl.pallas_call(
        body, name=name, grid_spec=grid_spec, out_shape=jax.ShapeDtypeStruct((4, r, cw), BF16),
        compiler_params=_params(("parallel", "parallel")),
    )(c.reshape(1).astype(jnp.int32), g.reshape(4, 2, r, cw), r1)


def _adamw_math(w, g, m, v):
    m = B1 * m + (1.0 - B1) * g
    v = B2 * v + (1.0 - B2) * (g * g)
    m_hat = m / (1.0 - B1 ** STEP)
    v_hat = v / (1.0 - B2 ** STEP)
    return -LR * (m_hat / (jnp.sqrt(v_hat) + AEPS) + WD * w), m, v


def _sum_adamw(parts, w, m, v, name):
    r, c = w.shape
    tr = _pick(r, (256, 176, 128))

    def body(p0, p1, p2, p3, w_ref, m_ref, v_ref, g_out, d_out, m_out, v_out):
        g = ((p0[...].astype(F32) + p1[...].astype(F32)) + p2[...].astype(F32)) + p3[...].astype(F32)
        g_out[...] = g
        d_out[...], m_out[...], v_out[...] = _adamw_math(w_ref[...], g, m_ref[...], v_ref[...])

    part = lambda q: _BS((None, tr, c), functools.partial(lambda i, q: (q, i, 0), q=q))
    plain = _BS((tr, c), lambda i: (i, 0))
    return pl.pallas_call(
        body, name=name, grid=(r // tr,), in_specs=[part(q) for q in range(4)] + [plain] * 3,
        out_specs=[plain] * 4, out_shape=[jax.ShapeDtypeStruct((r, c), F32)] * 4,
        compiler_params=_params(("parallel",)),
    )(parts, parts, parts, parts, w, m, v)


def _small_update(gathered, w, m, v):
    r = w.shape[0]

    def body(ga_ref, w_ref, m_ref, v_ref, g_out, d_out, m_out, v_out):
        g = ga_ref[0]
        for dev in range(1, N_DEV):
            g = g + ga_ref[dev]
        g_out[...] = g
        d_out[...], m_out[...], v_out[...] = _adamw_math(w_ref[...], g, m_ref[...], v_ref[...])

    return pl.pallas_call(
        body, name="small_update", out_shape=[jax.ShapeDtypeStruct((r, LANE), F32)] * 4,
    )(gathered, w, m, v)


def _pack_small(vals):
    rows = []
    for name, (r, n) in SMALL:
        flat = vals[name].reshape(-1)
        pad = (-flat.shape[0]) % LANE
        rows.append(jnp.pad(flat, (0, pad)).reshape(-1, LANE))
    return jnp.concatenate(rows, axis=0)


def _unpack_small(packed):
    out, off = {}, 0
    for name, (r, n) in SMALL:
        nrow = (r * n + LANE - 1) // LANE
        out[name] = packed[off:off + nrow].reshape(-1)[:r * n].reshape(r, n)
        off += nrow
    return out


def kernel(x, positions, ffn1_norm, ffn1_w_in, ffn1_w_out, mix_norm, w_in, hg_lb_table, hg_out_norm, w_hg_branch, mla_q_lora_norm, w_q_up, mla_kv_lora_norm, w_kv_up, q_head_norm, k_head_norm, w_mla_branch, w_merge, b_merge, w_out, ffn2_norm, ffn2_w_in, ffn2_w_out, final_norm, loss_target, m_ffn1_norm, m_ffn1_w_in, m_ffn1_w_out, m_mix_norm, m_w_in, m_hg_lb_table, m_hg_out_norm, m_w_hg_branch, m_mla_q_lora_norm, m_w_q_up, m_mla_kv_lora_norm, m_w_kv_up, m_q_head_norm, m_k_head_norm, m_w_mla_branch, m_w_merge, m_b_merge, m_w_out, m_ffn2_norm, m_ffn2_w_in, m_ffn2_w_out, m_final_norm, v_ffn1_norm, v_ffn1_w_in, v_ffn1_w_out, v_mix_norm, v_w_in, v_hg_lb_table, v_hg_out_norm, v_w_hg_branch, v_mla_q_lora_norm, v_w_q_up, v_mla_kv_lora_norm, v_w_kv_up, v_q_head_norm, v_k_head_norm, v_w_mla_branch, v_w_merge, v_b_merge, v_w_out, v_ffn2_norm, v_ffn2_w_in, v_ffn2_w_out, v_final_norm):
    args = dict(locals())
    t = x.shape[1]
    big_w = {n: args[n][0] for n, _, _ in BIG}
    small = {n: args[n].reshape(shape) for n, shape in SMALL}

    names = [n for n, _, _ in BIG]
    full = dict(zip(names, _all_gather([big_w[n].astype(BF16) for n in names], "weights_all_gather")))

    inv_freq = ROPE_THETA ** (-jnp.arange(0, ROPE, 2, dtype=F32) / ROPE)
    ang = positions[0].astype(F32)[:, None] * inv_freq
    cs = jnp.concatenate([jnp.cos(ang), jnp.cos(ang)], axis=1)
    sn = jnp.concatenate([jnp.sin(ang), jnp.sin(ang)], axis=1)

    loss_row, grad_x, g = _local_step(x[0], loss_target[0], cs, sn, full, small)

    c = lax.axis_index("c")
    from_sibling = _sibling_swap([g[n] for n in names], "grads_sibling_swap")
    chip_sums = [_chip_sum(g[n], r1, c, "chip_sum_" + n) for n, r1 in zip(names, from_sibling)]
    exchanged = dict(zip(names, _chip_exchange(chip_sums, "grads_chip_exchange")))

    small_packed = jnp.concatenate([_pack_small(g), loss_row], axis=0)
    pad = (-small_packed.shape[0]) % 8
    small_packed = jnp.pad(small_packed, ((0, pad), (0, 0)))
    small_all = _all_gather([small_packed], "small_all_gather")[0]
    zero_tail = jnp.zeros((small_packed.shape[0] - _pack_small(small).shape[0], LANE), F32)
    pk = lambda d: jnp.concatenate([_pack_small(d), zero_tail], axis=0)
    sg, sd, sm, sv = _small_update(
        small_all, pk(small), pk({n: args["m_" + n].reshape(shape) for n, shape in SMALL}),
        pk({n: args["v_" + n].reshape(shape) for n, shape in SMALL}))
    n_small_rows = _pack_small(small).shape[0]
    loss = sg[n_small_rows, 0]
    outs = {k_: _unpack_small(a) for k_, a in (("grad", sg), ("delta", sd), ("new_m", sm), ("new_v", sv))}

    for n in names:
        outs["grad"][n], outs["delta"][n], outs["new_m"][n], outs["new_v"][n] = _sum_adamw(
            exchanged[n], big_w[n], args["m_" + n][0], args["v_" + n][0], "adamw_" + n)

    def shaped(kind, n):
        return outs[kind][n].reshape(args[n].shape)

    return (loss, grad_x[None], *[shaped("grad", n) for n in WEIGHT_ORDER], *[shaped("delta", n) for n in WEIGHT_ORDER],
            *[shaped("new_m", n) for n in WEIGHT_ORDER], *[shaped("new_v", n) for n in WEIGHT_ORDER])
```

```python
import functools

import jax
import jax.numpy as jnp
from jax import lax
from jax.experimental import pallas as pl
from jax.experimental.pallas import tpu as pltpu

F32 = jnp.float32
BF16 = jnp.bfloat16

D = 1024
FF = 2816
NH = 8
HD = 128
ROPE = 64
QK = HD + ROPE
QL = 384
KVL = 256
HGW = NH * HD
CHUNK = 64
EPS = 1e-6
ROPE_THETA = 10000.0
SCALE = QK ** -0.5

LR, B1, B2, AEPS, WD, STEP = 0.001, 0.9, 0.999, 1e-08, 0.01, 10

HB = 128
SUB = 16
EXP_CLAMP = 80.0
ATT_TILES = (512, 256, 128)
ATT_Q = (1024, 512, 256, 128)
ATT_KEY_TILES = 1
ROW_TILES = (1024, 512, 256, 128)

LANE = 128
SUBLANE = 8
VMEM_LIMIT = 56 << 20

N_DEV = 8
MESH = pl.DeviceIdType.MESH

BIG = (
    ("ffn1_w_in", (D, 2 * FF), 1), ("ffn1_w_out", (FF, D), 0), ("w_in", (D, 4800), 1),
    ("w_hg_branch", (HGW, D), 0), ("w_q_up", (QL, NH * QK), 1), ("w_kv_up", (KVL, NH * 2 * HD), 1),
    ("w_mla_branch", (NH * HD, D), 0), ("w_merge", (D, 2 * D), 1), ("w_out", (D, D), 0),
    ("ffn2_w_in", (D, 2 * FF), 1), ("ffn2_w_out", (FF, D), 0),
)
SMALL = (
    ("ffn1_norm", (1, D)), ("mix_norm", (1, D)), ("hg_lb_table", (2, HGW)), ("hg_out_norm", (1, HD)),
    ("mla_q_lora_norm", (1, QL)), ("mla_kv_lora_norm", (1, KVL)), ("q_head_norm", (1, QK)),
    ("k_head_norm", (1, QK)), ("b_merge", (1, 2 * D)), ("ffn2_norm", (1, D)), ("final_norm", (1, D)),
)
WEIGHT_ORDER = ("ffn1_norm", "ffn1_w_in", "ffn1_w_out", "mix_norm", "w_in", "hg_lb_table", "hg_out_norm",
                "w_hg_branch", "mla_q_lora_norm", "w_q_up", "mla_kv_lora_norm", "w_kv_up", "q_head_norm",
                "k_head_norm", "w_mla_branch", "w_merge", "b_merge", "w_out", "ffn2_norm", "ffn2_w_in",
                "ffn2_w_out", "final_norm")


def _pick(n, cands):
    for c in cands:
        if n % c == 0:
            return c
    return n


def _params(sem):
    return pltpu.CompilerParams(dimension_semantics=sem, vmem_limit_bytes=VMEM_LIMIT)


def _sig(x):
    return 1.0 / (1.0 + jnp.exp(-x))


def _dot(a, b):
    return jnp.dot(a.astype(BF16), b.astype(BF16), preferred_element_type=F32)


def _dot_nt(a, b):
    return lax.dot_general(a.astype(BF16), b.astype(BF16), (((1,), (1,)), ((), ())),
                           preferred_element_type=F32)


def _dot_tn(a, b):
    return lax.dot_general(a.astype(BF16), b.astype(BF16), (((0,), (0,)), ((), ())),
                           preferred_element_type=F32)


def _split3(x):
    x1 = x.astype(BF16)
    r1 = x - x1.astype(F32)
    x2 = r1.astype(BF16)
    x3 = (r1 - x2.astype(F32)).astype(BF16)
    return x1, x2, x3


def _dot_sel(m, x):
    x1, x2, x3 = _split3(x)
    d = lambda p: jnp.dot(m, p, preferred_element_type=F32)
    return d(x1) + d(x2) + d(x3)


def _sel_dot(x, m):
    x1, x2, x3 = _split3(x)
    d = lambda p: jnp.dot(p, m, preferred_element_type=F32)
    return d(x1) + d(x2) + d(x3)


_TN = (1408, 1024, 768, 512, 384, 256, 128)


def _mm(a, b, mode, name, out_dtype=F32):
    if mode == "tn":
        t, m = a.shape
        n = b.shape[1]
        tt, tm, tn = _pick(t, (512, 256, 128)), _pick(m, _TN), _pick(n, _TN)

        def body(a_ref, b_ref, o_ref):
            @pl.when(pl.program_id(2) == 0)
            def _():
                o_ref[...] = jnp.zeros_like(o_ref)

            o_ref[...] += _dot_tn(a_ref[...], b_ref[...])

        return pl.pallas_call(
            body, name=name, grid=(m // tm, n // tn, t // tt),
            in_specs=[pl.BlockSpec((tt, tm), lambda i, j, k: (k, i)),
                      pl.BlockSpec((tt, tn), lambda i, j, k: (k, j))],
            out_specs=pl.BlockSpec((tm, tn), lambda i, j, k: (i, j)),
            out_shape=jax.ShapeDtypeStruct((m, n), F32),
            compiler_params=_params(("parallel", "parallel", "arbitrary")),
        )(a, b)

    m, k = a.shape
    tm = _pick(m, ROW_TILES)
    if mode == "nn":
        n = b.shape[1]
        tn = _pick(n, _TN)
        b_spec = pl.BlockSpec((k, tn), lambda i, j: (0, j))
        dot = _dot
    else:
        n = b.shape[0]
        tn = _pick(n, _TN if k <= 4096 else (512, 256, 128))
        b_spec = pl.BlockSpec((tn, k), lambda i, j: (j, 0))
        dot = _dot_nt

    def body(a_ref, b_ref, o_ref):
        o_ref[...] = dot(a_ref[...], b_ref[...]).astype(o_ref.dtype)

    return pl.pallas_call(
        body, name=name, grid=(m // tm, n // tn),
        in_specs=[pl.BlockSpec((tm, k), lambda i, j: (i, 0)), b_spec],
        out_specs=pl.BlockSpec((tm, tn), lambda i, j: (i, j)),
        out_shape=jax.ShapeDtypeStruct((m, n), out_dtype),
        compiler_params=_params(("parallel", "parallel")),
    )(a, b)


_DOTS = {"nn": _dot, "nt": _dot_nt, "tn": _dot_tn}
_BS = pl.BlockSpec


def _mmcall(name, kind, a, b, a_spec, b_spec, o_spec, o_shape, grid, red_axis=None, out_dtype=F32):
    dot = _DOTS[kind]

    def body(a_ref, b_ref, o_ref):
        if red_axis is None:
            o_ref[...] = dot(a_ref[...], b_ref[...]).astype(o_ref.dtype)
        else:
            @pl.when(pl.program_id(red_axis) == 0)
            def _():
                o_ref[...] = jnp.zeros_like(o_ref)

            o_ref[...] += dot(a_ref[...], b_ref[...])

    sem = tuple("arbitrary" if ax == red_axis else "parallel" for ax in range(len(grid)))
    return pl.pallas_call(
        body, name=name, grid=grid, in_specs=[a_spec, b_spec], out_specs=o_spec,
        out_shape=jax.ShapeDtypeStruct(o_shape, out_dtype), compiler_params=_params(sem),
    )(a, b)


def _mm_stack_red(a, w, name, kind):
    s, t, n = a.shape
    nout = w.shape[2] if kind == "nn" else w.shape[1]
    tm = _pick(t, ROW_TILES)
    return _mmcall(name, kind, a, w, _BS((None, tm, n), lambda i, j: (j, i, 0)),
                   _BS((None,) + w.shape[1:], lambda i, j: (j, 0, 0)), _BS((tm, nout), lambda i, j: (i, 0)),
                   (t, nout), (t // tm, s), red_axis=1)


def _mm_stack_tn(a, b, name):
    grp = 4
    if a.ndim == 2:
        t, k = a.shape
        s, _, n = b.shape
        tt = _pick(t, (512, 256, 128))
        a_spec = _BS((tt, k), lambda j, r: (r, 0))
        b_spec, b_in = _BS((None, grp, tt, n), lambda j, r: (j, 0, r, 0)), b.reshape(s // grp, grp, t, n)
        a_in = a
    else:
        s, t, k = a.shape
        n = b.shape[1]
        tt = _pick(t, (512, 256, 128))
        a_spec, a_in = _BS((None, grp, tt, k), lambda j, r: (j, 0, r, 0)), a.reshape(s // grp, grp, t, k)
        b_spec, b_in = _BS((tt, n), lambda j, r: (r, 0)), b

    def body(a_ref, b_ref, o_ref):
        @pl.when(pl.program_id(1) == 0)
        def _():
            o_ref[...] = jnp.zeros_like(o_ref)

        shared = a_ref[...] if a.ndim == 2 else b_ref[...]
        for e in range(grp):
            o_ref[e] += _dot_tn(shared, b_ref[e]) if a.ndim == 2 else _dot_tn(a_ref[e], shared)

    return pl.pallas_call(
        body, name=name, grid=(s // grp, t // tt), in_specs=[a_spec, b_spec],
        out_specs=_BS((None, grp, k, n), lambda j, r: (j, 0, 0, 0)),
        out_shape=jax.ShapeDtypeStruct((s // grp, grp, k, n), F32),
        compiler_params=_params(("parallel", "arbitrary")),
    )(a_in, b_in).reshape(s, k, n)


def _cols_fwd(x, w, name):
    t, k = x.shape
    s, _, n = w.shape
    tm = _pick(t, ROW_TILES)

    def body(x_ref, w_ref, o_ref):
        x_ = x_ref[...]
        for j in range(s):
            o_ref[:, n * j:n * (j + 1)] = _dot(x_, w_ref[j])

    return pl.pallas_call(
        body, name=name, grid=(t // tm,),
        in_specs=[_BS((tm, k), lambda i: (i, 0)), _BS((s, k, n), lambda i: (0, 0, 0))],
        out_specs=_BS((tm, s * n), lambda i: (i, 0)), out_shape=jax.ShapeDtypeStruct((t, s * n), F32),
        compiler_params=_params(("parallel",)),
    )(x, w)


def _cols_dx(d, w, name):
    t = d.shape[0]
    s, k, n = w.shape
    tm = _pick(t, ROW_TILES)

    def body(d_ref, w_ref, o_ref):
        acc = _dot_nt(d_ref[:, 0:n], w_ref[0])
        for j in range(1, s):
            acc = acc + _dot_nt(d_ref[:, n * j:n * (j + 1)], w_ref[j])
        o_ref[...] = acc

    return pl.pallas_call(
        body, name=name, grid=(t // tm,),
        in_specs=[_BS((tm, s * n), lambda i: (i, 0)), _BS((s, k, n), lambda i: (0, 0, 0))],
        out_specs=_BS((tm, k), lambda i: (i, 0)), out_shape=jax.ShapeDtypeStruct((t, k), F32),
        compiler_params=_params(("parallel",)),
    )(d, w)


def _cols_dw(x, d, n, name):
    t, k = x.shape
    s = d.shape[1] // n
    tt = _pick(t, (512, 256, 128))

    def body(x_ref, d_ref, o_ref):
        @pl.when(pl.program_id(0) == 0)
        def _():
            o_ref[...] = jnp.zeros_like(o_ref)

        x_ = x_ref[...]
        for j in range(s):
            o_ref[j] += _dot_tn(x_, d_ref[:, n * j:n * (j + 1)])

    return pl.pallas_call(
        body, name=name, grid=(t // tt,),
        in_specs=[_BS((tt, k), lambda r: (r, 0)), _BS((tt, s * n), lambda r: (r, 0))],
        out_specs=_BS((s, k, n), lambda r: (0, 0, 0)), out_shape=jax.ShapeDtypeStruct((s, k, n), F32),
        compiler_params=_params(("arbitrary",)),
    )(x, d)


def _rows(fn, name, t, tm, ins, vecs, outs, accs=()):
    n_in, n_out, n_acc = len(ins) + len(vecs), len(outs), len(accs)

    def body(*refs):
        res = fn(*refs[:n_in + n_out])
        if n_acc:
            acc_refs = refs[n_in + n_out:]

            @pl.when(pl.program_id(0) == 0)
            def _():
                for r in acc_refs:
                    r[...] = jnp.zeros_like(r)

            for r, val in zip(acc_refs, res):
                r[...] += val

    in_specs = [pl.BlockSpec((tm, bw), functools.partial(lambda i, cb: (i, cb), cb=cb)) for _, bw, cb in ins]
    in_specs += [pl.BlockSpec(v.shape, lambda i: (0, 0)) for v in vecs]
    out_specs = [pl.BlockSpec((tm, w), lambda i: (i, 0)) for w, _ in outs]
    out_specs += [pl.BlockSpec(s, lambda i: (0, 0)) for s in accs]
    out_shape = [jax.ShapeDtypeStruct((t, w), dt) for w, dt in outs]
    out_shape += [jax.ShapeDtypeStruct(s, F32) for s in accs]
    return pl.pallas_call(
        body, name=name, grid=(t // tm,), in_specs=in_specs, out_specs=out_specs, out_shape=out_shape,
        compiler_params=_params(("arbitrary",) if n_acc else ("parallel",)),
    )(*[a for a, _, _ in ins], *vecs)


def _rms(x, g):
    return x * lax.rsqrt(jnp.mean(x * x, axis=-1, keepdims=True) + EPS) * g


def _rms_bwd(x, g, dy):
    xh = x * lax.rsqrt(jnp.mean(x * x, axis=-1, keepdims=True) + EPS)
    r = lax.rsqrt(jnp.mean(x * x, axis=-1, keepdims=True) + EPS)
    dyg = dy * g
    dx = r * (dyg - xh * jnp.mean(dyg * xh, axis=-1, keepdims=True))
    return dx, jnp.sum(dy * xh, axis=0, keepdims=True)


def _hgrn_mats():
    row = lax.broadcasted_iota(jnp.int32, (HB, HB), 0)
    col = lax.broadcasted_iota(jnp.int32, (HB, HB), 1)
    return row, col


def _hgrn_gates(qr, z, t0, t1):
    lb = 1.0 / (1.0 + jnp.exp(t1 - t0))
    sz = _sig(z)
    sneg = 1.0 / (1.0 + jnp.exp(z))
    f = lb + (1.0 - lb) * sz
    return lb, sz, sneg, f, jnp.log(f), (1.0 - lb) * sneg, qr * _sig(qr)


def _hgrn_scores(q, k, cum, cmid):
    qd = q * jnp.exp(jnp.minimum(cmid, EXP_CLAMP))
    qd_b = qd.astype(BF16)
    kds, parts = [], []
    for i in range(HB // SUB):
        mid = cum[SUB * i + SUB // 2 - 1:SUB * i + SUB // 2, :]
        kd = k * jnp.exp(jnp.minimum(mid - cum, EXP_CLAMP))
        kds.append(kd)
        parts.append(_dot_nt(qd_b[SUB * i:SUB * (i + 1)], kd))
    return qd, kds, jnp.concatenate(parts, axis=0)


def _hgrn_fwd(p_hg, table, gain, t):
    nblk = t // HB

    def body(q_ref, f_ref, i_ref, g_ref, tab_ref, gain_ref, o_ref, y_ref, st_ref, state):
        @pl.when(pl.program_id(0) == 0)
        def _():
            state[...] = jnp.zeros_like(state)

        row, col = _hgrn_mats()
        causal = col <= row
        tri = causal.astype(BF16)
        trimid = tri - (col <= (row // SUB) * SUB + SUB // 2 - 1).astype(BF16)
        for h in range(NH):
            sl = slice(HD * h, HD * (h + 1))
            v, gr = i_ref[:, sl], g_ref[:, sl]
            _, _, _, _, lf, k, q = _hgrn_gates(q_ref[:, sl], f_ref[:, sl], tab_ref[0:1, sl], tab_ref[1:2, sl])
            cum = _dot_sel(tri, lf)
            cmid = _dot_sel(trimid, lf)
            _, _, s = _hgrn_scores(q, k, cum, cmid)
            p = jnp.where(causal, s, 0.0)
            st = state[h]
            st_ref[h, 0] = st
            o = _dot(p, v) + _dot_nt(q * jnp.exp(cum), st)
            last = cum[HB - 1:HB, :]
            state[h] = st * jnp.exp(last) + _dot_tn(v, k * jnp.exp(last - cum))
            o_ref[:, sl] = o
            y_ref[:, sl] = (_rms(o, gain_ref[...]) * gr * _sig(gr)).astype(BF16)

    blk = lambda cb: pl.BlockSpec((HB, HGW), functools.partial(lambda n, cb: (n, cb), cb=cb))
    return pl.pallas_call(
        body, name="hgrn_fwd", grid=(nblk,),
        in_specs=[blk(0), blk(1), blk(2), blk(3), pl.BlockSpec((2, HGW), lambda n: (0, 0)),
                  pl.BlockSpec((1, HD), lambda n: (0, 0))],
        out_specs=[pl.BlockSpec((HB, HGW), lambda n: (n, 0)), pl.BlockSpec((HB, HGW), lambda n: (n, 0)),
                   pl.BlockSpec((NH, 1, HD, HD), lambda n: (0, n, 0, 0))],
        out_shape=[jax.ShapeDtypeStruct((t, HGW), F32), jax.ShapeDtypeStruct((t, HGW), BF16),
                   jax.ShapeDtypeStruct((NH, nblk, HD, HD), F32)],
        scratch_shapes=[pltpu.VMEM((NH, HD, HD), F32)],
        compiler_params=_params(("arbitrary",)),
    )(p_hg, p_hg, p_hg, p_hg, table, gain)


def _hgrn_bwd(p_hg, table, gain, o_pre, states, dy, t):
    nblk = t // HB

    def body(q_ref, f_ref, i_ref, g_ref, tab_ref, gain_ref, o_ref, st_ref, dy_ref, dp_ref, dtab_ref, dgain_ref,
             dstate):
        @pl.when(pl.program_id(0) == 0)
        def _():
            dstate[...] = jnp.zeros_like(dstate)
            dtab_ref[...] = jnp.zeros_like(dtab_ref)
            dgain_ref[...] = jnp.zeros_like(dgain_ref)

        row, col = _hgrn_mats()
        causal = col <= row
        tri = causal.astype(BF16)
        midrow = (row // SUB) * SUB + SUB // 2 - 1
        trimid = tri - (col <= midrow).astype(BF16)
        tri_t = (row <= col).astype(BF16)
        midcol = (col // SUB) * SUB + SUB // 2 - 1
        trimid_t = tri_t - (row <= midcol).astype(BF16)
        dgain = jnp.zeros((1, HD), F32)
        for h in range(NH):
            sl = slice(HD * h, HD * (h + 1))
            qr, z, v, gr = q_ref[:, sl], f_ref[:, sl], i_ref[:, sl], g_ref[:, sl]
            lb, sz, sneg, f, lf, k, q = _hgrn_gates(qr, z, tab_ref[0:1, sl], tab_ref[1:2, sl])
            cum = _dot_sel(tri, lf)
            cmid = _dot_sel(trimid, lf)
            qd, kds, s = _hgrn_scores(q, k, cum, cmid)
            p = jnp.where(causal, s, 0.0)
            st = st_ref[h, 0]
            dst = dstate[h]
            o = o_ref[:, sl]
            sg = _sig(gr)
            dyh = dy_ref[:, sl]
            on = _rms(o, gain_ref[...])
            dgr = dyh * on * sg * (1.0 + gr * (1.0 - sg))
            do, dg_h = _rms_bwd(o, gain_ref[...], dyh * gr * sg)
            dgain = dgain + dg_h
            do_b = do.astype(BF16)
            ecum = jnp.exp(cum)
            qc = q * ecum
            last = cum[HB - 1:HB, :]
            edec = jnp.exp(last - cum)
            kdec = k * edec
            dp = jnp.where(causal, _dot_nt(do_b, v), 0.0)
            dv = _dot(p.T, do_b) + _dot_nt(kdec, dst)
            dqc = _dot(do_b, st)
            dkdec = _dot(v, dst)
            dstate[h] = dst * jnp.exp(last) + _dot(do.T, qc)
            dp_b = dp.astype(BF16)
            dqd = jnp.concatenate([_dot(dp_b[SUB * i:SUB * (i + 1)], kds[i]) for i in range(HB // SUB)], axis=0)
            gq = dqd * qd
            dq = dqd * jnp.exp(jnp.minimum(cmid, EXP_CLAMP)) + dqc * ecum
            gs = dkdec * kdec
            dk = dkdec * edec
            dcum = dqc * qc - gs
            dcum = dcum + jnp.where(row == HB - 1, jnp.sum(gs, axis=0, keepdims=True)
                                    + jnp.exp(last) * jnp.sum(st * dst, axis=0, keepdims=True), 0.0)
            qd_b = qd.astype(BF16)
            for i in range(HB // SUB):
                dkd = _dot_tn(dp_b[SUB * i:SUB * (i + 1)], qd_b[SUB * i:SUB * (i + 1)])
                mid = cum[SUB * i + SUB // 2 - 1:SUB * i + SUB // 2, :]
                dk = dk + dkd * jnp.exp(jnp.minimum(mid - cum, EXP_CLAMP))
                gk = dkd * kds[i]
                dcum = dcum - gk + jnp.where(row == SUB * i + SUB // 2 - 1, jnp.sum(gk, axis=0, keepdims=True), 0.0)
            dlf = _dot_sel(tri_t, dcum) + _dot_sel(trimid_t, gq)
            df = dlf / f - dk
            dz = df * (1.0 - lb) * sz * sneg
            dlb = jnp.sum(df * sneg, axis=0, keepdims=True) * lb * (1.0 - lb)
            dtab_ref[0:1, sl] += dlb
            dtab_ref[1:2, sl] -= dlb
            sq = _sig(qr)
            dp_ref[:, sl] = (dq * sq * (1.0 + qr * (1.0 - sq))).astype(BF16)
            dp_ref[:, HGW + HD * h:HGW + HD * (h + 1)] = dz.astype(BF16)
            dp_ref[:, 2 * HGW + HD * h:2 * HGW + HD * (h + 1)] = dv.astype(BF16)
            dp_ref[:, 3 * HGW + HD * h:3 * HGW + HD * (h + 1)] = dgr.astype(BF16)
        dgain_ref[...] += dgain

    rev = lambda cb: pl.BlockSpec((HB, HGW), functools.partial(lambda n, cb: (nblk - 1 - n, cb), cb=cb))
    return pl.pallas_call(
        body, name="hgrn_bwd", grid=(nblk,),
        in_specs=[rev(0), rev(1), rev(2), rev(3), pl.BlockSpec((2, HGW), lambda n: (0, 0)),
                  pl.BlockSpec((1, HD), lambda n: (0, 0)), rev(0),
                  pl.BlockSpec((NH, 1, HD, HD), lambda n: (0, nblk - 1 - n, 0, 0)), rev(0)],
        out_specs=[pl.BlockSpec((HB, 4 * HGW), lambda n: (nblk - 1 - n, 0)),
                   pl.BlockSpec((2, HGW), lambda n: (0, 0)), pl.BlockSpec((1, HD), lambda n: (0, 0))],
        out_shape=[jax.ShapeDtypeStruct((t, 4 * HGW), BF16), jax.ShapeDtypeStruct((2, HGW), F32),
                   jax.ShapeDtypeStruct((1, HD), F32)],
        scratch_shapes=[pltpu.VMEM((NH, HD, HD), F32)],
        compiler_params=_params(("arbitrary",)),
    )(p_hg, p_hg, p_hg, p_hg, table, gain, o_pre, states, dy)


def _rope_mat():
    r = lax.broadcasted_iota(jnp.int32, (ROPE, ROPE), 0)
    c = lax.broadcasted_iota(jnp.int32, (ROPE, ROPE), 1)
    half = ROPE // 2
    return ((r == c - half).astype(F32) - (r == c + half).astype(F32)).astype(BF16)


def _mla_prep_fwd(p_mla, cs, sn, wq, wkv, gql, gkvl, gq, gk, t):
    tm = _pick(t, (512, 256, 128))

    def body(p_ref, cs_ref, sn_ref, wq_ref, wkv_ref, gql_ref, gkvl_ref, gq_ref, gk_ref,
             q_ref, k_ref, v_ref):
        rmat = _rope_mat()
        cqn = _rms(p_ref[:, 0:QL], gql_ref[...]).astype(BF16)
        ckvn = _rms(p_ref[:, QL:QL + KVL], gkvl_ref[...]).astype(BF16)
        kpe = p_ref[:, QL + KVL:QL + KVL + ROPE]
        c, s = cs_ref[...], sn_ref[...]
        rot = lambda x: x * c + _sel_dot(x, rmat) * s
        for h in range(NH):
            qa, qr = _dot(cqn, wq_ref[h, :, 0:HD]), _dot(cqn, wq_ref[h, :, HD:QK])
            rq = lax.rsqrt((jnp.sum(qa * qa, -1, keepdims=True) + jnp.sum(qr * qr, -1, keepdims=True)) / QK + EPS)
            q_ref[h, :, 0:HD] = (qa * rq * gq_ref[:, 0:HD] * SCALE).astype(BF16)
            q_ref[h, :, HD:QK] = (rot(qr * rq * gq_ref[:, HD:QK]) * SCALE).astype(BF16)
            kn = _dot(ckvn, wkv_ref[h, :, 0:HD])
            rk = lax.rsqrt((jnp.sum(kn * kn, -1, keepdims=True) + jnp.sum(kpe * kpe, -1, keepdims=True)) / QK + EPS)
            k_ref[h, :, 0:HD] = (kn * rk * gk_ref[:, 0:HD]).astype(BF16)
            k_ref[h, :, HD:QK] = rot(kpe * rk * gk_ref[:, HD:QK]).astype(BF16)
            v_ref[h] = _dot(ckvn, wkv_ref[h, :, HD:2 * HD]).astype(BF16)

    whole = lambda a: pl.BlockSpec(a.shape, functools.partial(lambda i, nd: (0,) * nd, nd=a.ndim))
    return pl.pallas_call(
        body, name="mla_prep_fwd", grid=(t // tm,),
        in_specs=[pl.BlockSpec((tm, QL + KVL + ROPE), lambda i: (i, 0)), pl.BlockSpec((tm, ROPE), lambda i: (i, 0)),
                  pl.BlockSpec((tm, ROPE), lambda i: (i, 0))] + [whole(a) for a in (wq, wkv, gql, gkvl, gq, gk)],
        out_specs=[pl.BlockSpec((NH, tm, QK), lambda i: (0, i, 0)), pl.BlockSpec((NH, tm, QK), lambda i: (0, i, 0)),
                   pl.BlockSpec((NH, tm, HD), lambda i: (0, i, 0))],
        out_shape=[jax.ShapeDtypeStruct((NH, t, QK), BF16), jax.ShapeDtypeStruct((NH, t, QK), BF16),
                   jax.ShapeDtypeStruct((NH, t, HD), BF16)],
        compiler_params=_params(("parallel",)),
    )(p_mla, cs, sn, wq, wkv, gql, gkvl, gq, gk)


def _mla_prep_bwd(p_mla, cs, sn, wq, wkv, gql, gkvl, gq, gk, dq, dk, dv, t):
    tm = _pick(t, (512, 256, 128))

    def body(p_ref, cs_ref, sn_ref, wq_ref, wkv_ref, gql_ref, gkvl_ref, gq_ref, gk_ref,
             dq_ref, dk_ref, dv_ref,
             dp_ref, dwq_ref, dwkv_ref, dgql_ref, dgkvl_ref, dgq_ref, dgk_ref):
        accs = (dwq_ref, dwkv_ref, dgql_ref, dgkvl_ref, dgq_ref, dgk_ref)

        @pl.when(pl.program_id(0) == 0)
        def _():
            for r in accs:
                r[...] = jnp.zeros_like(r)

        rmat = _rope_mat()
        rmat_t = -rmat
        cq, ckv = p_ref[:, 0:QL], p_ref[:, QL:QL + KVL]
        kpe = p_ref[:, QL + KVL:QL + KVL + ROPE]
        cqn_f, ckvn_f = _rms(cq, gql_ref[...]), _rms(ckv, gkvl_ref[...])
        cqn, ckvn = cqn_f.astype(BF16), ckvn_f.astype(BF16)
        cqn_t, ckvn_t = cqn_f.T.astype(BF16), ckvn_f.T.astype(BF16)
        c, s = cs_ref[...], sn_ref[...]
        unrot = lambda dy: dy * c + _sel_dot(dy * s, rmat_t)
        dcqn = jnp.zeros((tm, QL), F32)
        dckvn = jnp.zeros((tm, KVL), F32)
        dkpe = jnp.zeros((tm, ROPE), F32)
        dgq_a, dgq_r = jnp.zeros((1, HD), F32), jnp.zeros((1, ROPE), F32)
        dgk_a, dgk_r = jnp.zeros((1, HD), F32), jnp.zeros((1, ROPE), F32)
        for h in range(NH):
            qa, qr = _dot(cqn, wq_ref[h, :, 0:HD]), _dot(cqn, wq_ref[h, :, HD:QK])
            rq = lax.rsqrt((jnp.sum(qa * qa, -1, keepdims=True) + jnp.sum(qr * qr, -1, keepdims=True)) / QK + EPS)
            xa, xr = qa * rq, qr * rq
            dya = dq_ref[h, :, 0:HD] * SCALE
            dyr = unrot(dq_ref[h, :, HD:QK] * SCALE)
            dgq_a += jnp.sum(dya * xa, axis=0, keepdims=True)
            dgq_r += jnp.sum(dyr * xr, axis=0, keepdims=True)
            ga, gr_ = dya * gq_ref[:, 0:HD], dyr * gq_ref[:, HD:QK]
            mean = (jnp.sum(ga * xa, -1, keepdims=True) + jnp.sum(gr_ * xr, -1, keepdims=True)) / QK
            dqa = (rq * (ga - xa * mean)).astype(BF16)
            dqr = (rq * (gr_ - xr * mean)).astype(BF16)
            dwq_ref[h, :, 0:HD] += _dot(cqn_t, dqa)
            dwq_ref[h, :, HD:QK] += _dot(cqn_t, dqr)
            dcqn += _dot_nt(dqa, wq_ref[h, :, 0:HD]) + _dot_nt(dqr, wq_ref[h, :, HD:QK])
            kn = _dot(ckvn, wkv_ref[h, :, 0:HD])
            rk = lax.rsqrt((jnp.sum(kn * kn, -1, keepdims=True) + jnp.sum(kpe * kpe, -1, keepdims=True)) / QK + EPS)
            ya, yr = kn * rk, kpe * rk
            dka = dk_ref[h, :, 0:HD]
            dkr = unrot(dk_ref[h, :, HD:QK])
            dgk_a += jnp.sum(dka * ya, axis=0, keepdims=True)
            dgk_r += jnp.sum(dkr * yr, axis=0, keepdims=True)
            ha, hr = dka * gk_ref[:, 0:HD], dkr * gk_ref[:, HD:QK]
            mean = (jnp.sum(ha * ya, -1, keepdims=True) + jnp.sum(hr * yr, -1, keepdims=True)) / QK
            dkn = (rk * (ha - ya * mean)).astype(BF16)
            dkpe += rk * (hr - yr * mean)
            dvh = dv_ref[h].astype(BF16)
            dwkv_ref[h, :, 0:HD] += _dot(ckvn_t, dkn)
            dwkv_ref[h, :, HD:2 * HD] += _dot(ckvn_t, dvh)
            dckvn += _dot_nt(dkn, wkv_ref[h, :, 0:HD]) + _dot_nt(dvh, wkv_ref[h, :, HD:2 * HD])
        dcq, dg1 = _rms_bwd(cq, gql_ref[...], dcqn)
        dckv, dg2 = _rms_bwd(ckv, gkvl_ref[...], dckvn)
        dp_ref[:, 0:QL] = dcq.astype(BF16)
        dp_ref[:, QL:QL + KVL] = dckv.astype(BF16)
        dp_ref[:, QL + KVL:QL + KVL + ROPE] = dkpe.astype(BF16)
        dgql_ref[...] += dg1
        dgkvl_ref[...] += dg2
        dgq_ref[:, 0:HD] += dgq_a
        dgq_ref[:, HD:QK] += dgq_r
        dgk_ref[:, 0:HD] += dgk_a
        dgk_ref[:, HD:QK] += dgk_r

    whole = lambda a: pl.BlockSpec(a.shape, functools.partial(lambda i, nd: (0,) * nd, nd=a.ndim))
    acc_shapes = [wq.shape, wkv.shape, gql.shape, gkvl.shape, gq.shape, gk.shape]
    return pl.pallas_call(
        body, name="mla_prep_bwd", grid=(t // tm,),
        in_specs=[pl.BlockSpec((tm, QL + KVL + ROPE), lambda i: (i, 0)), pl.BlockSpec((tm, ROPE), lambda i: (i, 0)),
                  pl.BlockSpec((tm, ROPE), lambda i: (i, 0))]
        + [whole(a) for a in (wq, wkv, gql, gkvl, gq, gk)]
        + [pl.BlockSpec((NH, tm, QK), lambda i: (0, i, 0)), pl.BlockSpec((NH, tm, QK), lambda i: (0, i, 0)),
           pl.BlockSpec((NH, tm, HD), lambda i: (0, i, 0))],
        out_specs=[pl.BlockSpec((tm, QL + KVL + ROPE), lambda i: (i, 0))]
        + [pl.BlockSpec(s, functools.partial(lambda i, nd: (0,) * nd, nd=len(s))) for s in acc_shapes],
        out_shape=[jax.ShapeDtypeStruct((t, QL + KVL + ROPE), BF16)]
        + [jax.ShapeDtypeStruct(s, F32) for s in acc_shapes],
        compiler_params=_params(("arbitrary",)),
    )(p_mla, cs, sn, wq, wkv, gql, gkvl, gq, gk, dq, dk, dv)


def _chunk_mask(nq, nk, key0, keys_on_rows):
    shape = (nk, nq) if keys_on_rows else (nq, nk)
    qi = lax.broadcasted_iota(jnp.int32, shape, 1 if keys_on_rows else 0) // CHUNK
    ki = lax.broadcasted_iota(jnp.int32, shape, 0 if keys_on_rows else 1) // CHUNK + key0 // CHUNK
    return ki <= qi


def _flash_fwd(q, k, v, t):
    tq = _pick(t, ATT_Q)
    tk = tq // ATT_KEY_TILES

    def body(q_ref, k_ref, v_ref, o_ref, lse_ref):
        i = pl.program_id(1)
        qt = q_ref[0]

        def step(j, carry, key0):
            m, l, acc = carry
            cols = pl.ds(pl.multiple_of(j * tk, tk), tk)
            s = _dot_nt(qt, k_ref[0, cols, :])
            if key0 is not None:
                s = jnp.where(_chunk_mask(tq, tk, key0, False), s, -jnp.inf)
            m_new = jnp.maximum(m, jnp.max(s, axis=-1, keepdims=True))
            p = jnp.exp(s - m_new)
            alpha = jnp.exp(m - m_new)
            return m_new, alpha * l + jnp.sum(p, axis=-1, keepdims=True), alpha * acc + _dot(p, v_ref[0, cols, :])

        init = (jnp.full((tq, 1), -jnp.inf, F32), jnp.zeros((tq, 1), F32), jnp.zeros((tq, HD), F32))
        carry = lax.fori_loop(0, ATT_KEY_TILES * i, lambda j, cr: step(j, cr, None), init)
        for h in range(ATT_KEY_TILES):
            carry = step(ATT_KEY_TILES * i + h, carry, h * tk)
        m, l, acc = carry
        o_ref[...] = (acc / l).astype(BF16)
        lse_ref[0] = m + jnp.log(l)

    return pl.pallas_call(
        body, name="flash_fwd", grid=(NH, t // tq),
        in_specs=[pl.BlockSpec((1, tq, QK), lambda h, i: (h, i, 0)), pl.BlockSpec((1, t, QK), lambda h, i: (h, 0, 0)),
                  pl.BlockSpec((1, t, HD), lambda h, i: (h, 0, 0))],
        out_specs=[pl.BlockSpec((tq, HD), lambda h, i: (i, h)), pl.BlockSpec((1, tq, 1), lambda h, i: (h, i, 0))],
        out_shape=[jax.ShapeDtypeStruct((t, NH * HD), BF16), jax.ShapeDtypeStruct((NH, t, 1), F32)],
        compiler_params=_params(("parallel", "parallel")),
    )(q, k, v)


def _flash_delta(o, do, t):
    tq = _pick(t, ATT_TILES)

    def body(o_ref, do_ref, delta_ref):
        prod = do_ref[...].astype(F32) * o_ref[...].astype(F32)
        x1, x2, x3 = _split3(prod)
        ones = jnp.ones((8, HD), BF16)
        d = lambda p: lax.dot_general(ones, p, (((1,), (1,)), ((), ())), preferred_element_type=F32)
        delta_ref[0] = (d(x1) + d(x2) + d(x3))[0:1, :]

    return pl.pallas_call(
        body, name="flash_delta", grid=(NH, t // tq),
        in_specs=[pl.BlockSpec((tq, HD), lambda h, i: (i, h)), pl.BlockSpec((tq, HD), lambda h, i: (i, h))],
        out_specs=pl.BlockSpec((1, 1, tq), lambda h, i: (h, 0, i)),
        out_shape=jax.ShapeDtypeStruct((NH, 1, t), F32), compiler_params=_params(("parallel", "parallel")),
    )(o, do)


def _flash_bwd(q, k, v, do, lse_row, delta_row, t):
    tq = _pick(t, ATT_Q)
    tk = tq // ATT_KEY_TILES

    def body(q_ref, k_ref, v_ref, do_ref, lse_ref, delta_ref, dq_ref, dk_ref, dv_ref):
        j = pl.program_id(1)

        @pl.when(j == 0)
        def _():
            dq_ref[...] = jnp.zeros_like(dq_ref)

        kt, vt = k_ref[0], v_ref[0]

        def step(i, carry, key0):
            dk, dv = carry
            rows = pl.ds(pl.multiple_of(i * tq, tq), tq)
            qt, dot_ = q_ref[0, rows, :], do_ref[rows, :]
            p = jnp.exp(_dot_nt(kt, qt) - lse_ref[0, :, rows])
            if key0 is not None:
                p = jnp.where(_chunk_mask(tq, tk, key0, True), p, 0.0)
            ds = (p * (_dot_nt(vt, dot_) - delta_ref[0, :, rows])).astype(BF16)
            dq_ref[0, rows, :] += _dot_tn(ds, kt)
            return dk + _dot(ds, qt), dv + _dot(p, dot_)

        own = j // ATT_KEY_TILES
        carry = step(own, (jnp.zeros((tk, QK), F32), jnp.zeros((tk, HD), F32)), (j % ATT_KEY_TILES) * tk)
        dk, dv = lax.fori_loop(own + 1, t // tq, lambda i, cr: step(i, cr, None), carry)
        dk_ref[0] = dk
        dv_ref[0] = dv

    return pl.pallas_call(
        body, name="flash_bwd", grid=(NH, t // tk),
        in_specs=[pl.BlockSpec((1, t, QK), lambda h, j: (h, 0, 0)), pl.BlockSpec((1, tk, QK), lambda h, j: (h, j, 0)),
                  pl.BlockSpec((1, tk, HD), lambda h, j: (h, j, 0)), pl.BlockSpec((t, HD), lambda h, j: (0, h)),
                  pl.BlockSpec((1, 1, t), lambda h, j: (h, 0, 0)), pl.BlockSpec((1, 1, t), lambda h, j: (h, 0, 0))],
        out_specs=[pl.BlockSpec((1, t, QK), lambda h, j: (h, 0, 0)), pl.BlockSpec((1, tk, QK), lambda h, j: (h, j, 0)),
                   pl.BlockSpec((1, tk, HD), lambda h, j: (h, j, 0))],
        out_shape=[jax.ShapeDtypeStruct((NH, t, QK), F32), jax.ShapeDtypeStruct((NH, t, QK), F32),
                   jax.ShapeDtypeStruct((NH, t, HD), F32)],
        compiler_params=_params(("parallel", "arbitrary")),
    )(q, k, v, do, lse_row, delta_row)


def _ffn_in(xn, w_in, name):
    t, k = xn.shape
    s, _, n = w_in.shape
    tm = _pick(t, ROW_TILES)

    def body(x_ref, w_ref, gu_ref, a_ref):
        x = x_ref[...]
        g, u = _dot(x, w_ref[0]), _dot(x, w_ref[1])
        gu_ref[0] = g.astype(BF16)
        gu_ref[1] = u.astype(BF16)
        a_ref[...] = (g * _sig(g) * u).astype(BF16)

    return pl.pallas_call(
        body, name=name, grid=(t // tm, s // 2),
        in_specs=[_BS((tm, k), lambda i, j: (i, 0)), _BS((2, None, k, n), lambda i, j: (0, j, 0, 0))],
        out_specs=[_BS((2, None, tm, n), lambda i, j: (0, j, i, 0)), _BS((None, tm, n), lambda i, j: (j, i, 0))],
        out_shape=[jax.ShapeDtypeStruct((2, s // 2, t, n), BF16), jax.ShapeDtypeStruct((s // 2, t, n), BF16)],
        compiler_params=_params(("parallel", "parallel")),
    )(xn, w_in.reshape(2, s // 2, k, n))


def _ffn_dgu(dfo, w_out, gu, name):
    t, k = dfo.shape
    s, n, _ = w_out.shape
    tm = _pick(t, ROW_TILES)

    def body(d_ref, w_ref, gu_ref, o_ref):
        da = _dot_nt(d_ref[...], w_ref[...])
        g, u = gu_ref[0].astype(F32), gu_ref[1].astype(F32)
        sg = _sig(g)
        o_ref[0] = (da * u * sg * (1.0 + g * (1.0 - sg))).astype(BF16)
        o_ref[1] = (da * g * sg).astype(BF16)

    pair = _BS((2, None, tm, n), lambda i, j: (0, j, i, 0))
    return pl.pallas_call(
        body, name=name, grid=(t // tm, s),
        in_specs=[_BS((tm, k), lambda i, j: (i, 0)), _BS((None, n, k), lambda i, j: (j, 0, 0)), pair],
        out_specs=pair, out_shape=jax.ShapeDtypeStruct((2, s, t, n), BF16),
        compiler_params=_params(("parallel", "parallel")),
    )(dfo, w_out, gu).reshape(2 * s, t, n)


def _ffn_fwd(xn, w_in, w_out, tag):
    gu, a = _ffn_in(xn, w_in, tag + "_in")
    return gu, a, _mm_stack_red(a, w_out, tag + "_out", "nn")


def _ffn_bwd(dfo, xn, gu, a, w_in, w_out, tag):
    dw_out = _mm_stack_tn(a, dfo, tag + "_dwout")
    dgu = _ffn_dgu(dfo, w_out, gu, tag + "_dgu")
    return _mm_stack_red(dgu, w_in, tag + "_dxn", "nt"), _mm_stack_tn(xn, dgu, tag + "_dwin"), dw_out


def _local_step(x, target, cs, sn, w, s, early_grads):
    t = x.shape[0]
    tm = _pick(t, (256, 128))
    g = {}
    rows_of = lambda n: w[n].reshape(-1, w[n].shape[2])
    ffn_out = lambda n: w[n].reshape(4, FF // 4, D)
    w_hgb, w_mlab, w_o = rows_of("w_hg_branch"), rows_of("w_mla_branch"), rows_of("w_out")
    w_in_nat = w["w_in"].transpose(1, 0, 2).reshape(D, -1)
    w_mrg = w["w_merge"]
    mw = w_mrg.shape[2]

    def norm_fn(x_ref, g_ref, o_ref):
        o_ref[...] = _rms(x_ref[...], g_ref[...]).astype(BF16)

    xn1 = _rows(norm_fn, "norm1", t, tm, [(x, D, 0)], [s["ffn1_norm"]], [(D, BF16)])[0]
    gu1, a1, f1 = _ffn_fwd(xn1, w["ffn1_w_in"], ffn_out("ffn1_w_out"), "ffn1")

    def res_norm_fn(scale):
        def fn(h_ref, f_ref, g_ref, h_out, n_out):
            h = h_ref[...] + scale * f_ref[...]
            h_out[...] = h
            n_out[...] = _rms(h, g_ref[...]).astype(BF16)
        return fn

    h1, u = _rows(res_norm_fn(0.5), "res_norm1", t, tm, [(x, D, 0), (f1, D, 0)], [s["mix_norm"]],
                  [(D, F32), (D, BF16)])
    w_in_hg, w_in_mla = w_in_nat[:, :4 * HGW], w_in_nat[:, 4 * HGW:]
    p_hg = _mm(u, w_in_hg, "nn", "proj_hg")
    p_mla = _mm(u, w_in_mla, "nn", "proj_mla")
    gpre = _cols_fwd(u, w_mrg, "proj_gate")
    o_pre, hgy, states = _hgrn_fwd(p_hg, s["hg_lb_table"], s["hg_out_norm"], t)
    prep_args = (p_mla, cs, sn, w["w_q_up"], w["w_kv_up"], s["mla_q_lora_norm"], s["mla_kv_lora_norm"],
                 s["q_head_norm"], s["k_head_norm"])
    q, k, v = _mla_prep_fwd(*prep_args, t)
    att, lse = _flash_fwd(q, k, v, t)
    y_hg = _mm(hgy, w_hgb, "nn", "branch_hg")
    y_mla = _mm(att, w_mlab, "nn", "branch_mla")

    def mix_fn(gh_ref, gm_ref, yh_ref, ym_ref, b_ref, o_ref):
        gh = _sig(gh_ref[...] + b_ref[:, 0:D])
        gm = _sig(gm_ref[...] + b_ref[:, D:2 * D])
        o_ref[...] = (gh * yh_ref[...] + gm * ym_ref[...]).astype(BF16)

    mixed = _rows(mix_fn, "mix", t, tm, [(gpre, D, 0), (gpre, D, 1), (y_hg, D, 0), (y_mla, D, 0)], [s["b_merge"]],
                  [(D, BF16)])[0]
    mo = _mm(mixed, w_o, "nn", "mix_out")
    h2, xn2 = _rows(res_norm_fn(1.0), "res_norm2", t, tm, [(h1, D, 0), (mo, D, 0)], [s["ffn2_norm"]],
                    [(D, F32), (D, BF16)])
    gu2, a2, f2 = _ffn_fwd(xn2, w["ffn2_w_in"], ffn_out("ffn2_w_out"), "ffn2")

    def loss_fn(h_ref, f_ref, tg_ref, g_ref, dh_out, dhb_out):
        h = h_ref[...] + 0.5 * f_ref[...]
        e = _rms(h, g_ref[...]) - tg_ref[...]
        dh, dgain = _rms_bwd(h, g_ref[...], e / D)
        dh_out[...] = dh
        dhb_out[...] = (0.5 * dh).astype(BF16)
        return dgain, jnp.full((1, LANE), 0.5 / D * jnp.sum(e * e), F32)

    dh3, dfo2, g["final_norm"], loss = _rows(loss_fn, "loss", t, tm, [(h2, D, 0), (f2, D, 0), (target, D, 0)],
                                             [s["final_norm"]], [(D, F32), (D, BF16)], [(1, D), (1, LANE)])

    def norm_bwd_fn(scale):
        def fn(h_ref, dxn_ref, dh_ref, g_ref, dh_out, dhb_out):
            dx, dgain = _rms_bwd(h_ref[...], g_ref[...], dxn_ref[...])
            dh = dh_ref[...] + dx
            dh_out[...] = dh
            dhb_out[...] = (scale * dh).astype(BF16)
            return (dgain,)
        return fn

    as_rows = lambda a: a.reshape((N_DEV, -1) + a.shape[-1:])
    dxn2, g["ffn2_w_in"], dwo = _ffn_bwd(dfo2, xn2, gu2, a2, w["ffn2_w_in"], ffn_out("ffn2_w_out"), "ffn2")
    g["ffn2_w_out"] = as_rows(dwo)
    dh2, dh2b, g["ffn2_norm"] = _rows(norm_bwd_fn(1.0), "norm2_bwd", t, tm, [(h2, D, 0), (dxn2, D, 0), (dh3, D, 0)],
                                      [s["ffn2_norm"]], [(D, F32), (D, BF16)], [(1, D)])
    dmixed = _mm(dh2b, w_o, "nt", "mix_out_dx")
    g["w_out"] = as_rows(_mm(mixed, dh2b, "tn", "mix_out_dw"))

    def mix_bwd_fn(gh_ref, gm_ref, yh_ref, ym_ref, dm_ref, b_ref, dyh_out, dym_out, dg_out):
        gh = _sig(gh_ref[...] + b_ref[:, 0:D])
        gm = _sig(gm_ref[...] + b_ref[:, D:2 * D])
        dm = dm_ref[...]
        dyh_out[...] = (dm * gh).astype(BF16)
        dym_out[...] = (dm * gm).astype(BF16)
        dgh = dm * yh_ref[...] * gh * (1.0 - gh)
        dgm = dm * ym_ref[...] * gm * (1.0 - gm)
        dg_out[:, 0:D] = dgh.astype(BF16)
        dg_out[:, D:2 * D] = dgm.astype(BF16)
        return (jnp.concatenate([jnp.sum(dgh, axis=0, keepdims=True), jnp.sum(dgm, axis=0, keepdims=True)], axis=1),)

    dyh, dym, dgpre, g["b_merge"] = _rows(
        mix_bwd_fn, "mix_bwd", t, tm, [(gpre, D, 0), (gpre, D, 1), (y_hg, D, 0), (y_mla, D, 0), (dmixed, D, 0)],
        [s["b_merge"]], [(D, BF16), (D, BF16), (2 * D, BF16)], [(1, 2 * D)])
    g["w_hg_branch"] = as_rows(_mm(hgy, dyh, "tn", "branch_hg_dw"))
    g["w_mla_branch"] = as_rows(_mm(att, dym, "tn", "branch_mla_dw"))
    g["w_merge"] = _cols_dw(u, dgpre, mw, "proj_gate_dw")
    dhgy = _mm(dyh, w_hgb, "nt", "branch_hg_dx")
    datt = _mm(dym, w_mlab, "nt", "branch_mla_dx", out_dtype=BF16)
    du_gate = _cols_dx(dgpre, w_mrg, "proj_gate_dx")

    dq, dk, dv = _flash_bwd(q, k, v, datt, lse.reshape(NH, 1, t), _flash_delta(att, datt, t), t)
    (dp_mla, g["w_q_up"], g["w_kv_up"], g["mla_q_lora_norm"], g["mla_kv_lora_norm"], g["q_head_norm"],
     g["k_head_norm"]) = _mla_prep_bwd(*prep_args, dq, dk, dv, t)
    dp_hg, g["hg_lb_table"], g["hg_out_norm"] = _hgrn_bwd(p_hg, s["hg_lb_table"], s["hg_out_norm"], o_pre, states,
                                                          dhgy, t)
    dw_in_nat = jnp.concatenate([_mm(u, dp_hg, "tn", "proj_hg_dw"), _mm(u, dp_mla, "tn", "proj_mla_dw")], axis=1)
    g["w_in"] = dw_in_nat.reshape(D, N_DEV, -1).transpose(1, 0, 2)
    du_hg = _mm(dp_hg, w_in_hg, "nt", "proj_hg_dx")
    du_mla = _mm(dp_mla, w_in_mla, "nt", "proj_mla_dx")

    def mixnorm_bwd_fn(h_ref, a_ref, b_ref, c_ref, dh_ref, g_ref, dh_out, dhb_out):
        dx, dgain = _rms_bwd(h_ref[...], g_ref[...], a_ref[...] + b_ref[...] + c_ref[...])
        dh = dh_ref[...] + dx
        dh_out[...] = dh
        dhb_out[...] = (0.5 * dh).astype(BF16)
        return (dgain,)

    mix_gain = s["mix_norm"] + early_grads(g)[0:1, 0:1]
    dh1, dfo1, g["mix_norm"] = _rows(mixnorm_bwd_fn, "mixnorm_bwd", t, tm,
                                     [(h1, D, 0), (du_hg, D, 0), (du_mla, D, 0), (du_gate, D, 0), (dh2, D, 0)],
                                     [mix_gain], [(D, F32), (D, BF16)], [(1, D)])
    dxn1, g["ffn1_w_in"], dwo = _ffn_bwd(dfo1, xn1, gu1, a1, w["ffn1_w_in"], ffn_out("ffn1_w_out"), "ffn1")
    g["ffn1_w_out"] = as_rows(dwo)
    grad_x, _, g["ffn1_norm"] = _rows(norm_bwd_fn(1.0), "norm1_bwd", t, tm, [(x, D, 0), (dxn1, D, 0), (dh1, D, 0)],
                                      [s["ffn1_norm"]], [(D, F32), (D, BF16)], [(1, D)])
    return loss, grad_x, g


def _coords():
    return lax.axis_index("x"), lax.axis_index("y"), lax.axis_index("c")


def _hbm_call(body, name, ins, out_shapes, scratch):
    any_spec = pl.BlockSpec(memory_space=pl.ANY)
    return pl.pallas_call(
        body, name=name, out_shape=[jax.ShapeDtypeStruct(s, dt) for s, dt in out_shapes],
        in_specs=[any_spec] * len(ins), out_specs=[any_spec] * len(out_shapes), scratch_shapes=scratch,
    )(*ins)


def _all_gather(blocks, name):
    nb = len(blocks)

    def body(*refs):
        x_refs, out_refs = refs[:nb], refs[nb:2 * nb]
        send_sems, recv_sems, local_sems = refs[2 * nb:]
        x, y, c = _coords()
        me, sibling = (x, y, c), (x, y, 1 - c)
        chips = [(1 - x, y), (x, 1 - y), (1 - x, 1 - y)]

        def slot(b, px, py, pc):
            return out_refs[b].at[4 * px + 2 * py + pc]

        def copy(b, kk, block_of, to, src=None):
            return pltpu.make_async_remote_copy(
                src_ref=slot(b, *block_of) if src is None else src, dst_ref=slot(b, *block_of),
                send_sem=send_sems.at[b, kk], recv_sem=recv_sems.at[b, kk], device_id=to, device_id_type=MESH)

        mine = [pltpu.make_async_copy(x_refs[b], slot(b, *me), local_sems.at[b]) for b in range(nb)]
        for cp in mine:
            cp.start()
        first = [copy(b, 0, me, sibling, src=x_refs[b]) for b in range(nb)]
        first += [copy(b, 1 + j, me, (*chip, c), src=x_refs[b]) for j, chip in enumerate(chips) for b in range(nb)]
        for cp in first:
            cp.start()
        passed = []
        for j, chip in enumerate(chips):
            for b in range(nb):
                copy(b, 1 + j, (*chip, c), me).wait_recv()
                passed.append(copy(b, 4 + j, (*chip, c), sibling))
                passed[-1].start()
        for b in range(nb):
            copy(b, 0, sibling, me).wait_recv()
        for j, chip in enumerate(chips):
            for b in range(nb):
                copy(b, 4 + j, (*chip, 1 - c), me).wait_recv()
        for cp in first + passed:
            cp.wait_send()
        for cp in mine:
            cp.wait()

    return _hbm_call(body, name, blocks, [((N_DEV,) + b.shape, b.dtype) for b in blocks],
                     [pltpu.SemaphoreType.DMA((nb, 7)), pltpu.SemaphoreType.DMA((nb, 7)),
                      pltpu.SemaphoreType.DMA((nb,))])


def _sibling_swap(bufs, name):
    nb = len(bufs)

    def body(*refs):
        x_refs, out_refs = refs[:nb], refs[nb:2 * nb]
        send_sems, recv_sems = refs[2 * nb:]
        x, y, c = _coords()
        copies = [pltpu.make_async_remote_copy(
            src_ref=x_refs[b].at[2 * q + 1 - c], dst_ref=out_refs[b].at[q], send_sem=send_sems.at[b, q],
            recv_sem=recv_sems.at[b, q], device_id=(x, y, 1 - c), device_id_type=MESH)
            for b in range(nb) for q in range(4)]
        for cp in copies:
            cp.start()
        for cp in copies:
            cp.wait()

    return _hbm_call(body, name, bufs, [((4,) + b.shape[1:], b.dtype) for b in bufs],
                     [pltpu.SemaphoreType.DMA((nb, 4)), pltpu.SemaphoreType.DMA((nb, 4))])


def _chip_exchange(bufs, name):
    nb = len(bufs)

    def body(*refs):
        x_refs, out_refs = refs[:nb], refs[nb:2 * nb]
        send_sems, recv_sems, local_sems = refs[2 * nb:]
        x, y, c = _coords()
        chips = [(1 - x, y), (x, 1 - y), (1 - x, 1 - y)]
        mine = [pltpu.make_async_copy(x_refs[b].at[2 * x + y], out_refs[b].at[2 * x + y], local_sems.at[b])
                for b in range(nb)]
        for cp in mine:
            cp.start()
        copies = [pltpu.make_async_remote_copy(
            src_ref=x_refs[b].at[2 * px + py], dst_ref=out_refs[b].at[2 * x + y], send_sem=send_sems.at[b, j],
            recv_sem=recv_sems.at[b, j], device_id=(px, py, c), device_id_type=MESH)
            for j, (px, py) in enumerate(chips) for b in range(nb)]
        for cp in copies:
            cp.start()
        for j, (px, py) in enumerate(chips):
            for b in range(nb):
                pltpu.make_async_remote_copy(
                    src_ref=x_refs[b].at[2 * x + y], dst_ref=out_refs[b].at[2 * px + py], send_sem=send_sems.at[b, j],
                    recv_sem=recv_sems.at[b, j], device_id=(px, py, c), device_id_type=MESH).wait_recv()
        for cp in copies:
            cp.wait_send()
        for cp in mine:
            cp.wait()

    return _hbm_call(body, name, bufs, [(b.shape, b.dtype) for b in bufs],
                     [pltpu.SemaphoreType.DMA((nb, 3)), pltpu.SemaphoreType.DMA((nb, 3)),
                      pltpu.SemaphoreType.DMA((nb,))])


def _chip_exchange_start(bufs, name):
    nb = len(bufs)
    hbm, sem = pl.BlockSpec(memory_space=pltpu.HBM), pl.BlockSpec(memory_space=pltpu.SEMAPHORE)

    def body(*refs):
        x_refs, land_refs = refs[:nb], refs[nb:2 * nb]
        send_sems, recv_sems, token = refs[2 * nb], refs[2 * nb + 1], refs[-1]
        x, y, c = _coords()
        for j, (px, py) in enumerate([(1 - x, y), (x, 1 - y), (1 - x, 1 - y)]):
            for b in range(nb):
                pltpu.make_async_remote_copy(
                    src_ref=x_refs[b].at[2 * px + py], dst_ref=land_refs[b].at[2 * x + y], send_sem=send_sems.at[3 * b + j],
                    recv_sem=recv_sems.at[3 * b + j], device_id=(px, py, c), device_id_type=MESH).start()
        token[...] = jnp.zeros_like(token)

    lands = [pltpu.with_memory_space_constraint(lax.empty(b.shape, b.dtype), pltpu.HBM) for b in bufs]
    outs = pl.pallas_call(
        body, name=name,
        out_shape=(pltpu.SemaphoreType.DMA((3 * nb,)), pltpu.SemaphoreType.DMA((3 * nb,)),
                   *[pltpu.HBM(b.shape, b.dtype) for b in bufs], *[pltpu.HBM(b.shape, b.dtype) for b in bufs],
                   jax.ShapeDtypeStruct((SUBLANE, LANE), F32)),
        in_specs=[hbm] * (2 * nb), out_specs=(sem, sem, *[hbm] * (2 * nb), pl.BlockSpec(memory_space=pltpu.VMEM)),
        input_output_aliases={i: 2 + i for i in range(2 * nb)},
        compiler_params=pltpu.CompilerParams(has_side_effects=pltpu.SideEffectType.DATAFLOW_SIDE_EFFECTING),
    )(*[pltpu.with_memory_space_constraint(b, pltpu.HBM) for b in bufs], *lands)
    return outs[0], outs[1], list(outs[2:2 + nb]), list(outs[2 + nb:2 + 2 * nb]), outs[-1]


def _chip_exchange_wait(send_sems, recv_sems, thru, lands, after, name):
    nb = len(thru)
    hbm, sem = pl.BlockSpec(memory_space=pltpu.HBM), pl.BlockSpec(memory_space=pltpu.SEMAPHORE)

    def body(*refs):
        x_refs, land_refs = refs[:nb], refs[nb:2 * nb]
        send_sems_, recv_sems_ = refs[2 * nb], refs[2 * nb + 1]
        x, y, c = _coords()
        for j, (px, py) in enumerate([(1 - x, y), (x, 1 - y), (1 - x, 1 - y)]):
            for b in range(nb):
                cp = pltpu.make_async_remote_copy(
                    src_ref=x_refs[b].at[2 * px + py], dst_ref=land_refs[b].at[2 * px + py],
                    send_sem=send_sems_.at[3 * b + j], recv_sem=recv_sems_.at[3 * b + j], device_id=(px, py, c),
                    device_id_type=MESH)
                cp.wait_send()
                cp.wait_recv()

    outs = pl.pallas_call(
        body, name=name,
        out_shape=(*[pltpu.HBM(b.shape, b.dtype) for b in thru], *[pltpu.HBM(b.shape, b.dtype) for b in lands]),
        in_specs=[hbm] * (2 * nb) + [sem, sem, pl.BlockSpec(memory_space=pl.ANY)], out_specs=[hbm] * (2 * nb),
        input_output_aliases={i: i for i in range(2 * nb)},
        compiler_params=pltpu.CompilerParams(has_side_effects=pltpu.SideEffectType.DATAFLOW_SIDE_EFFECTING),
    )(*thru, *lands, send_sems, recv_sems, after)
    return list(outs[:nb]), list(outs[nb:])


def _chip_sum(g, r1, c, name):
    _, r, cw = g.shape
    tr = _pick(r, (256, 176, 128))

    def body(c_ref, g_ref, r_ref, o_ref):
        o_ref[...] = (g_ref[...] + r_ref[...]).astype(BF16)

    grid_spec = pltpu.PrefetchScalarGridSpec(
        num_scalar_prefetch=1, grid=(4, r // tr),
        in_specs=[_BS((None, None, tr, cw), lambda q, i, c_ref: (q, c_ref[0], i, 0)),
                  _BS((None, tr, cw), lambda q, i, c_ref: (q, i, 0))],
        out_specs=_BS((None, tr, cw), lambda q, i, c_ref: (q, i, 0)))
    return pl.pallas_call(
        body, name=name, grid_spec=grid_spec, out_shape=jax.ShapeDtypeStruct((4, r, cw), BF16),
        compiler_params=_params(("parallel", "parallel")),
    )(c.reshape(1).astype(jnp.int32), g.reshape(4, 2, r, cw), r1)


def _adamw_math(w, g, m, v):
    m = B1 * m + (1.0 - B1) * g
    v = B2 * v + (1.0 - B2) * (g * g)
    m_hat = m / (1.0 - B1 ** STEP)
    v_hat = v / (1.0 - B2 ** STEP)
    return -LR * (m_hat / (jnp.sqrt(v_hat) + AEPS) + WD * w), m, v


def _sum_adamw(parts, w, m, v, name):
    r, c = w.shape
    tr = _pick(r, (256, 176, 128))

    def body(p0, p1, p2, p3, w_ref, m_ref, v_ref, g_out, d_out, m_out, v_out):
        g = ((p0[...].astype(F32) + p1[...].astype(F32)) + p2[...].astype(F32)) + p3[...].astype(F32)
        g_out[...] = g
        d_out[...], m_out[...], v_out[...] = _adamw_math(w_ref[...], g, m_ref[...], v_ref[...])

    part = lambda q: _BS((None, tr, c), functools.partial(lambda i, q: (q, i, 0), q=q))
    plain = _BS((tr, c), lambda i: (i, 0))
    return pl.pallas_call(
        body, name=name, grid=(r // tr,), in_specs=[part(q) for q in range(4)] + [plain] * 3,
        out_specs=[plain] * 4, out_shape=[jax.ShapeDtypeStruct((r, c), F32)] * 4,
        compiler_params=_params(("parallel",)),
    )(parts, parts, parts, parts, w, m, v)


def _small_update(gathered, w, m, v):
    r = w.shape[0]

    def body(ga_ref, w_ref, m_ref, v_ref, g_out, d_out, m_out, v_out):
        g = ga_ref[0]
        for dev in range(1, N_DEV):
            g = g + ga_ref[dev]
        g_out[...] = g
        d_out[...], m_out[...], v_out[...] = _adamw_math(w_ref[...], g, m_ref[...], v_ref[...])

    return pl.pallas_call(
        body, name="small_update", out_shape=[jax.ShapeDtypeStruct((r, LANE), F32)] * 4,
    )(gathered, w, m, v)


def _pack_small(vals):
    rows = []
    for name, (r, n) in SMALL:
        flat = vals[name].reshape(-1)
        pad = (-flat.shape[0]) % (SUBLANE * LANE)
        rows.append(jnp.pad(flat, (0, pad)).reshape(-1, LANE))
    return jnp.concatenate(rows, axis=0)


def _unpack_small(packed):
    out, off = {}, 0
    for name, (r, n) in SMALL:
        nrow = -(-(r * n) // (SUBLANE * LANE)) * SUBLANE
        out[name] = packed[off:off + nrow].reshape(-1)[:r * n].reshape(r, n)
        off += nrow
    return out


def kernel(x, positions, ffn1_norm, ffn1_w_in, ffn1_w_out, mix_norm, w_in, hg_lb_table, hg_out_norm, w_hg_branch, mla_q_lora_norm, w_q_up, mla_kv_lora_norm, w_kv_up, q_head_norm, k_head_norm, w_mla_branch, w_merge, b_merge, w_out, ffn2_norm, ffn2_w_in, ffn2_w_out, final_norm, loss_target, m_ffn1_norm, m_ffn1_w_in, m_ffn1_w_out, m_mix_norm, m_w_in, m_hg_lb_table, m_hg_out_norm, m_w_hg_branch, m_mla_q_lora_norm, m_w_q_up, m_mla_kv_lora_norm, m_w_kv_up, m_q_head_norm, m_k_head_norm, m_w_mla_branch, m_w_merge, m_b_merge, m_w_out, m_ffn2_norm, m_ffn2_w_in, m_ffn2_w_out, m_final_norm, v_ffn1_norm, v_ffn1_w_in, v_ffn1_w_out, v_mix_norm, v_w_in, v_hg_lb_table, v_hg_out_norm, v_w_hg_branch, v_mla_q_lora_norm, v_w_q_up, v_mla_kv_lora_norm, v_w_kv_up, v_q_head_norm, v_k_head_norm, v_w_mla_branch, v_w_merge, v_b_merge, v_w_out, v_ffn2_norm, v_ffn2_w_in, v_ffn2_w_out, v_final_norm):
    args = dict(locals())
    t = x.shape[1]
    big_w = {n: args[n][0] for n, _, _ in BIG}
    small = {n: args[n].reshape(shape) for n, shape in SMALL}

    names = [n for n, _, _ in BIG]
    full = dict(zip(names, _all_gather([big_w[n].astype(BF16) for n in names], "weights_all_gather")))

    inv_freq = ROPE_THETA ** (-jnp.arange(0, ROPE, 2, dtype=F32) / ROPE)
    ang = positions[0].astype(F32)[:, None] * inv_freq
    cs = jnp.concatenate([jnp.cos(ang), jnp.cos(ang)], axis=1)
    sn = jnp.concatenate([jnp.sin(ang), jnp.sin(ang)], axis=1)

    c = lax.axis_index("c")
    chip = 2 * lax.axis_index("x") + lax.axis_index("y")
    early = {}

    def chip_sums_of(g, ns, tag):
        from_sibling = _sibling_swap([g[n] for n in ns], "grads_sibling_swap_" + tag)
        return [_chip_sum(g[n], r1, c, "chip_sum_" + n) for n, r1 in zip(ns, from_sibling)]

    def early_grads(g):
        early["names"] = [n for n in names if n in g]
        early["sums"] = chip_sums_of(g, early["names"], "early")
        early["send"], early["recv"], early["thru"], early["lands"], token = _chip_exchange_start(
            early["sums"], "grads_exchange_start")
        return token

    loss_row, grad_x, g = _local_step(x[0], loss_target[0], cs, sn, full, small, early_grads)
    late = [n for n in names if n not in early["names"]]
    exchanged = dict(zip(late, _chip_exchange(chip_sums_of(g, late, "late"), "grads_chip_exchange_late")))
    sent, landed = _chip_exchange_wait(early["send"], early["recv"], early["thru"], early["lands"],
                                       exchanged[late[0]], "grads_exchange_wait")
    for n, land, own in zip(early["names"], landed, sent):
        exchanged[n] = lax.dynamic_update_index_in_dim(land, lax.dynamic_index_in_dim(own, chip, 0), chip, 0)

    small_packed = jnp.concatenate([_pack_small(g), jnp.pad(loss_row, ((0, SUBLANE - 1), (0, 0)))], axis=0)
    small_all = _all_gather([small_packed], "small_all_gather")[0]
    zero_tail = jnp.zeros((SUBLANE, LANE), F32)
    pk = lambda d: jnp.concatenate([_pack_small(d), zero_tail], axis=0)
    sg, sd, sm, sv = _small_update(
        small_all, pk(small), pk({n: args["m_" + n].reshape(shape) for n, shape in SMALL}),
        pk({n: args["v_" + n].reshape(shape) for n, shape in SMALL}))
    n_small_rows = _pack_small(small).shape[0]
    loss = sg[n_small_rows, 0]
    outs = {k_: _unpack_small(a) for k_, a in (("grad", sg), ("delta", sd), ("new_m", sm), ("new_v", sv))}

    for n in names:
        outs["grad"][n], outs["delta"][n], outs["new_m"][n], outs["new_v"][n] = _sum_adamw(
            exchanged[n], big_w[n], args["m_" + n][0], args["v_" + n][0], "adamw_" + n)

    def shaped(kind, n):
        return outs[kind][n].reshape(args[n].shape)

    return (loss, grad_x[None], *[shaped("grad", n) for n in WEIGHT_ORDER], *[shaped("delta", n) for n in WEIGHT_ORDER],
            *[shaped("new_m", n) for n in WEIGHT_ORDER], *[shaped("new_v", n) for n in WEIGHT_ORDER])
```

```python
import functools

import jax
import jax.numpy as jnp
from jax import lax
from jax.experimental import pallas as pl
from jax.experimental.pallas import tpu as pltpu

F32 = jnp.float32
BF16 = jnp.bfloat16

D = 1024
FF = 2816
NH = 8
HD = 128
ROPE = 64
QK = HD + ROPE
QL = 384
KVL = 256
HGW = NH * HD
CHUNK = 64
EPS = 1e-6
ROPE_THETA = 10000.0
SCALE = QK ** -0.5

LR, B1, B2, AEPS, WD, STEP = 0.001, 0.9, 0.999, 1e-08, 0.01, 10

HB = 128
SUB = 16
EXP_CLAMP = 80.0
ATT_TILES = (512, 256, 128)
ATT_Q = (1024, 512, 256, 128)
ATT_KEY_TILES = 1
ROW_TILES = (1024, 512, 256, 128)

LANE = 128
SUBLANE = 8
VMEM_LIMIT = 56 << 20

N_DEV = 8
MESH = pl.DeviceIdType.MESH

BIG = (
    ("ffn1_w_in", (D, 2 * FF), 1), ("ffn1_w_out", (FF, D), 0), ("w_in", (D, 4800), 1),
    ("w_hg_branch", (HGW, D), 0), ("w_q_up", (QL, NH * QK), 1), ("w_kv_up", (KVL, NH * 2 * HD), 1),
    ("w_mla_branch", (NH * HD, D), 0), ("w_merge", (D, 2 * D), 1), ("w_out", (D, D), 0),
    ("ffn2_w_in", (D, 2 * FF), 1), ("ffn2_w_out", (FF, D), 0),
)
SMALL = (
    ("ffn1_norm", (1, D)), ("mix_norm", (1, D)), ("hg_lb_table", (2, HGW)), ("hg_out_norm", (1, HD)),
    ("mla_q_lora_norm", (1, QL)), ("mla_kv_lora_norm", (1, KVL)), ("q_head_norm", (1, QK)),
    ("k_head_norm", (1, QK)), ("b_merge", (1, 2 * D)), ("ffn2_norm", (1, D)), ("final_norm", (1, D)),
)
WEIGHT_ORDER = ("ffn1_norm", "ffn1_w_in", "ffn1_w_out", "mix_norm", "w_in", "hg_lb_table", "hg_out_norm",
                "w_hg_branch", "mla_q_lora_norm", "w_q_up", "mla_kv_lora_norm", "w_kv_up", "q_head_norm",
                "k_head_norm", "w_mla_branch", "w_merge", "b_merge", "w_out", "ffn2_norm", "ffn2_w_in",
                "ffn2_w_out", "final_norm")


def _pick(n, cands):
    for c in cands:
        if n % c == 0:
            return c
    return n


def _params(sem):
    return pltpu.CompilerParams(dimension_semantics=sem, vmem_limit_bytes=VMEM_LIMIT)


def _sig(x):
    return 1.0 / (1.0 + jnp.exp(-x))


def _dot(a, b):
    return jnp.dot(a.astype(BF16), b.astype(BF16), preferred_element_type=F32)


def _dot_nt(a, b):
    return lax.dot_general(a.astype(BF16), b.astype(BF16), (((1,), (1,)), ((), ())),
                           preferred_element_type=F32)


def _dot_tn(a, b):
    return lax.dot_general(a.astype(BF16), b.astype(BF16), (((0,), (0,)), ((), ())),
                           preferred_element_type=F32)


def _split3(x):
    x1 = x.astype(BF16)
    r1 = x - x1.astype(F32)
    x2 = r1.astype(BF16)
    x3 = (r1 - x2.astype(F32)).astype(BF16)
    return x1, x2, x3


def _dot_sel(m, x):
    x1, x2, x3 = _split3(x)
    d = lambda p: jnp.dot(m, p, preferred_element_type=F32)
    return d(x1) + d(x2) + d(x3)


def _sel_dot(x, m):
    x1, x2, x3 = _split3(x)
    d = lambda p: jnp.dot(p, m, preferred_element_type=F32)
    return d(x1) + d(x2) + d(x3)


_TN = (1408, 1024, 768, 512, 384, 256, 128)


def _mm(a, b, mode, name, out_dtype=F32):
    if mode == "tn":
        t, m = a.shape
        n = b.shape[1]
        tt, tm, tn = _pick(t, (512, 256, 128)), _pick(m, _TN), _pick(n, _TN)

        def body(a_ref, b_ref, o_ref):
            @pl.when(pl.program_id(2) == 0)
            def _():
                o_ref[...] = jnp.zeros_like(o_ref)

            o_ref[...] += _dot_tn(a_ref[...], b_ref[...])

        return pl.pallas_call(
            body, name=name, grid=(m // tm, n // tn, t // tt),
            in_specs=[pl.BlockSpec((tt, tm), lambda i, j, k: (k, i)),
                      pl.BlockSpec((tt, tn), lambda i, j, k: (k, j))],
            out_specs=pl.BlockSpec((tm, tn), lambda i, j, k: (i, j)),
            out_shape=jax.ShapeDtypeStruct((m, n), F32),
            compiler_params=_params(("parallel", "parallel", "arbitrary")),
        )(a, b)

    m, k = a.shape
    tm = _pick(m, ROW_TILES)
    if mode == "nn":
        n = b.shape[1]
        tn = _pick(n, _TN)
        b_spec = pl.BlockSpec((k, tn), lambda i, j: (0, j))
        dot = _dot
    else:
        n = b.shape[0]
        tn = _pick(n, _TN if k <= 4096 else (512, 256, 128))
        b_spec = pl.BlockSpec((tn, k), lambda i, j: (j, 0))
        dot = _dot_nt

    def body(a_ref, b_ref, o_ref):
        o_ref[...] = dot(a_ref[...], b_ref[...]).astype(o_ref.dtype)

    return pl.pallas_call(
        body, name=name, grid=(m // tm, n // tn),
        in_specs=[pl.BlockSpec((tm, k), lambda i, j: (i, 0)), b_spec],
        out_specs=pl.BlockSpec((tm, tn), lambda i, j: (i, j)),
        out_shape=jax.ShapeDtypeStruct((m, n), out_dtype),
        compiler_params=_params(("parallel", "parallel")),
    )(a, b)


_DOTS = {"nn": _dot, "nt": _dot_nt, "tn": _dot_tn}
_BS = pl.BlockSpec


def _mmcall(name, kind, a, b, a_spec, b_spec, o_spec, o_shape, grid, red_axis=None, out_dtype=F32):
    dot = _DOTS[kind]

    def body(a_ref, b_ref, o_ref):
        if red_axis is None:
            o_ref[...] = dot(a_ref[...], b_ref[...]).astype(o_ref.dtype)
        else:
            @pl.when(pl.program_id(red_axis) == 0)
            def _():
                o_ref[...] = jnp.zeros_like(o_ref)

            o_ref[...] += dot(a_ref[...], b_ref[...])

    sem = tuple("arbitrary" if ax == red_axis else "parallel" for ax in range(len(grid)))
    return pl.pallas_call(
        body, name=name, grid=grid, in_specs=[a_spec, b_spec], out_specs=o_spec,
        out_shape=jax.ShapeDtypeStruct(o_shape, out_dtype), compiler_params=_params(sem),
    )(a, b)


def _mm_stack_red(a, w, name, kind):
    s, t, n = a.shape
    nout = w.shape[2] if kind == "nn" else w.shape[1]
    tm = _pick(t, ROW_TILES)
    return _mmcall(name, kind, a, w, _BS((None, tm, n), lambda i, j: (j, i, 0)),
                   _BS((None,) + w.shape[1:], lambda i, j: (j, 0, 0)), _BS((tm, nout), lambda i, j: (i, 0)),
                   (t, nout), (t // tm, s), red_axis=1)


def _mm_stack_tn(a, b, name):
    grp = 4
    if a.ndim == 2:
        t, k = a.shape
        s, _, n = b.shape
        tt = _pick(t, (512, 256, 128))
        a_spec = _BS((tt, k), lambda j, r: (r, 0))
        b_spec, b_in = _BS((None, grp, tt, n), lambda j, r: (j, 0, r, 0)), b.reshape(s // grp, grp, t, n)
        a_in = a
    else:
        s, t, k = a.shape
        n = b.shape[1]
        tt = _pick(t, (512, 256, 128))
        a_spec, a_in = _BS((None, grp, tt, k), lambda j, r: (j, 0, r, 0)), a.reshape(s // grp, grp, t, k)
        b_spec, b_in = _BS((tt, n), lambda j, r: (r, 0)), b

    def body(a_ref, b_ref, o_ref):
        @pl.when(pl.program_id(1) == 0)
        def _():
            o_ref[...] = jnp.zeros_like(o_ref)

        shared = a_ref[...] if a.ndim == 2 else b_ref[...]
        for e in range(grp):
            o_ref[e] += _dot_tn(shared, b_ref[e]) if a.ndim == 2 else _dot_tn(a_ref[e], shared)

    return pl.pallas_call(
        body, name=name, grid=(s // grp, t // tt), in_specs=[a_spec, b_spec],
        out_specs=_BS((None, grp, k, n), lambda j, r: (j, 0, 0, 0)),
        out_shape=jax.ShapeDtypeStruct((s // grp, grp, k, n), F32),
        compiler_params=_params(("parallel", "arbitrary")),
    )(a_in, b_in).reshape(s, k, n)


def _cols_fwd(x, w, name):
    t, k = x.shape
    s, _, n = w.shape
    tm = _pick(t, ROW_TILES)

    def body(x_ref, w_ref, o_ref):
        x_ = x_ref[...]
        for j in range(s):
            o_ref[:, n * j:n * (j + 1)] = _dot(x_, w_ref[j])

    return pl.pallas_call(
        body, name=name, grid=(t // tm,),
        in_specs=[_BS((tm, k), lambda i: (i, 0)), _BS((s, k, n), lambda i: (0, 0, 0))],
        out_specs=_BS((tm, s * n), lambda i: (i, 0)), out_shape=jax.ShapeDtypeStruct((t, s * n), F32),
        compiler_params=_params(("parallel",)),
    )(x, w)


def _cols_dx(d, w, name):
    t = d.shape[0]
    s, k, n = w.shape
    tm = _pick(t, ROW_TILES)

    def body(d_ref, w_ref, o_ref):
        acc = _dot_nt(d_ref[:, 0:n], w_ref[0])
        for j in range(1, s):
            acc = acc + _dot_nt(d_ref[:, n * j:n * (j + 1)], w_ref[j])
        o_ref[...] = acc

    return pl.pallas_call(
        body, name=name, grid=(t // tm,),
        in_specs=[_BS((tm, s * n), lambda i: (i, 0)), _BS((s, k, n), lambda i: (0, 0, 0))],
        out_specs=_BS((tm, k), lambda i: (i, 0)), out_shape=jax.ShapeDtypeStruct((t, k), F32),
        compiler_params=_params(("parallel",)),
    )(d, w)


def _cols_dw(x, d, n, name):
    t, k = x.shape
    s = d.shape[1] // n
    tt = _pick(t, (512, 256, 128))

    def body(x_ref, d_ref, o_ref):
        @pl.when(pl.program_id(0) == 0)
        def _():
            o_ref[...] = jnp.zeros_like(o_ref)

        x_ = x_ref[...]
        for j in range(s):
            o_ref[j] += _dot_tn(x_, d_ref[:, n * j:n * (j + 1)])

    return pl.pallas_call(
        body, name=name, grid=(t // tt,),
        in_specs=[_BS((tt, k), lambda r: (r, 0)), _BS((tt, s * n), lambda r: (r, 0))],
        out_specs=_BS((s, k, n), lambda r: (0, 0, 0)), out_shape=jax.ShapeDtypeStruct((s, k, n), F32),
        compiler_params=_params(("arbitrary",)),
    )(x, d)


def _rows(fn, name, t, tm, ins, vecs, outs, accs=()):
    n_in, n_out, n_acc = len(ins) + len(vecs), len(outs), len(accs)

    def body(*refs):
        res = fn(*refs[:n_in + n_out])
        if n_acc:
            acc_refs = refs[n_in + n_out:]

            @pl.when(pl.program_id(0) == 0)
            def _():
                for r in acc_refs:
                    r[...] = jnp.zeros_like(r)

            for r, val in zip(acc_refs, res):
                r[...] += val

    in_specs = [pl.BlockSpec((tm, bw), functools.partial(lambda i, cb: (i, cb), cb=cb)) for _, bw, cb in ins]
    in_specs += [pl.BlockSpec(v.shape, lambda i: (0, 0)) for v in vecs]
    out_specs = [pl.BlockSpec((tm, w), lambda i: (i, 0)) for w, _ in outs]
    out_specs += [pl.BlockSpec(s, lambda i: (0, 0)) for s in accs]
    out_shape = [jax.ShapeDtypeStruct((t, w), dt) for w, dt in outs]
    out_shape += [jax.ShapeDtypeStruct(s, F32) for s in accs]
    return pl.pallas_call(
        body, name=name, grid=(t // tm,), in_specs=in_specs, out_specs=out_specs, out_shape=out_shape,
        compiler_params=_params(("arbitrary",) if n_acc else ("parallel",)),
    )(*[a for a, _, _ in ins], *vecs)


def _rms(x, g):
    return x * lax.rsqrt(jnp.mean(x * x, axis=-1, keepdims=True) + EPS) * g


def _rms_bwd(x, g, dy):
    xh = x * lax.rsqrt(jnp.mean(x * x, axis=-1, keepdims=True) + EPS)
    r = lax.rsqrt(jnp.mean(x * x, axis=-1, keepdims=True) + EPS)
    dyg = dy * g
    dx = r * (dyg - xh * jnp.mean(dyg * xh, axis=-1, keepdims=True))
    return dx, jnp.sum(dy * xh, axis=0, keepdims=True)


def _hgrn_mats():
    row = lax.broadcasted_iota(jnp.int32, (HB, HB), 0)
    col = lax.broadcasted_iota(jnp.int32, (HB, HB), 1)
    return row, col


def _hgrn_gates(qr, z, t0, t1):
    lb = 1.0 / (1.0 + jnp.exp(t1 - t0))
    sz = _sig(z)
    sneg = 1.0 / (1.0 + jnp.exp(z))
    f = lb + (1.0 - lb) * sz
    return lb, sz, sneg, f, jnp.log(f), (1.0 - lb) * sneg, qr * _sig(qr)


def _hgrn_scores(q, k, cum, cmid):
    qd = q * jnp.exp(jnp.minimum(cmid, EXP_CLAMP))
    qd_b = qd.astype(BF16)
    kds, parts = [], []
    for i in range(HB // SUB):
        mid = cum[SUB * i + SUB // 2 - 1:SUB * i + SUB // 2, :]
        kd = k * jnp.exp(jnp.minimum(mid - cum, EXP_CLAMP))
        kds.append(kd)
        parts.append(_dot_nt(qd_b[SUB * i:SUB * (i + 1)], kd))
    return qd, kds, jnp.concatenate(parts, axis=0)


def _hgrn_fwd(p_hg, table, gain, t):
    nblk = t // HB

    def body(q_ref, f_ref, i_ref, g_ref, tab_ref, gain_ref, o_ref, y_ref, st_ref, state):
        @pl.when(pl.program_id(0) == 0)
        def _():
            state[...] = jnp.zeros_like(state)

        row, col = _hgrn_mats()
        causal = col <= row
        tri = causal.astype(BF16)
        trimid = tri - (col <= (row // SUB) * SUB + SUB // 2 - 1).astype(BF16)
        for h in range(NH):
            sl = slice(HD * h, HD * (h + 1))
            v, gr = i_ref[:, sl], g_ref[:, sl]
            _, _, _, _, lf, k, q = _hgrn_gates(q_ref[:, sl], f_ref[:, sl], tab_ref[0:1, sl], tab_ref[1:2, sl])
            cum = _dot_sel(tri, lf)
            cmid = _dot_sel(trimid, lf)
            _, _, s = _hgrn_scores(q, k, cum, cmid)
            p = jnp.where(causal, s, 0.0)
            st = state[h]
            st_ref[h, 0] = st
            o = _dot(p, v) + _dot_nt(q * jnp.exp(cum), st)
            last = cum[HB - 1:HB, :]
            state[h] = st * jnp.exp(last) + _dot_tn(v, k * jnp.exp(last - cum))
            o_ref[:, sl] = o
            y_ref[:, sl] = (_rms(o, gain_ref[...]) * gr * _sig(gr)).astype(BF16)

    blk = lambda cb: pl.BlockSpec((HB, HGW), functools.partial(lambda n, cb: (n, cb), cb=cb))
    return pl.pallas_call(
        body, name="hgrn_fwd", grid=(nblk,),
        in_specs=[blk(0), blk(1), blk(2), blk(3), pl.BlockSpec((2, HGW), lambda n: (0, 0)),
                  pl.BlockSpec((1, HD), lambda n: (0, 0))],
        out_specs=[pl.BlockSpec((HB, HGW), lambda n: (n, 0)), pl.BlockSpec((HB, HGW), lambda n: (n, 0)),
                   pl.BlockSpec((NH, 1, HD, HD), lambda n: (0, n, 0, 0))],
        out_shape=[jax.ShapeDtypeStruct((t, HGW), F32), jax.ShapeDtypeStruct((t, HGW), BF16),
                   jax.ShapeDtypeStruct((NH, nblk, HD, HD), F32)],
        scratch_shapes=[pltpu.VMEM((NH, HD, HD), F32)],
        compiler_params=_params(("arbitrary",)),
    )(p_hg, p_hg, p_hg, p_hg, table, gain)


def _hgrn_bwd(p_hg, table, gain, o_pre, states, dy, t):
    nblk = t // HB

    def body(q_ref, f_ref, i_ref, g_ref, tab_ref, gain_ref, o_ref, st_ref, dy_ref, dp_ref, dtab_ref, dgain_ref,
             dstate):
        @pl.when(pl.program_id(0) == 0)
        def _():
            dstate[...] = jnp.zeros_like(dstate)
            dtab_ref[...] = jnp.zeros_like(dtab_ref)
            dgain_ref[...] = jnp.zeros_like(dgain_ref)

        row, col = _hgrn_mats()
        causal = col <= row
        tri = causal.astype(BF16)
        midrow = (row // SUB) * SUB + SUB // 2 - 1
        trimid = tri - (col <= midrow).astype(BF16)
        tri_t = (row <= col).astype(BF16)
        midcol = (col // SUB) * SUB + SUB // 2 - 1
        trimid_t = tri_t - (row <= midcol).astype(BF16)
        dgain = jnp.zeros((1, HD), F32)
        for h in range(NH):
            sl = slice(HD * h, HD * (h + 1))
            qr, z, v, gr = q_ref[:, sl], f_ref[:, sl], i_ref[:, sl], g_ref[:, sl]
            lb, sz, sneg, f, lf, k, q = _hgrn_gates(qr, z, tab_ref[0:1, sl], tab_ref[1:2, sl])
            cum = _dot_sel(tri, lf)
            cmid = _dot_sel(trimid, lf)
            qd, kds, s = _hgrn_scores(q, k, cum, cmid)
            p = jnp.where(causal, s, 0.0)
            st = st_ref[h, 0]
            dst = dstate[h]
            o = o_ref[:, sl]
            sg = _sig(gr)
            dyh = dy_ref[:, sl]
            on = _rms(o, gain_ref[...])
            dgr = dyh * on * sg * (1.0 + gr * (1.0 - sg))
            do, dg_h = _rms_bwd(o, gain_ref[...], dyh * gr * sg)
            dgain = dgain + dg_h
            do_b = do.astype(BF16)
            ecum = jnp.exp(cum)
            qc = q * ecum
            last = cum[HB - 1:HB, :]
            edec = jnp.exp(last - cum)
            kdec = k * edec
            dp = jnp.where(causal, _dot_nt(do_b, v), 0.0)
            dv = _dot(p.T, do_b) + _dot_nt(kdec, dst)
            dqc = _dot(do_b, st)
            dkdec = _dot(v, dst)
            dstate[h] = dst * jnp.exp(last) + _dot(do.T, qc)
            dp_b = dp.astype(BF16)
            dqd = jnp.concatenate([_dot(dp_b[SUB * i:SUB * (i + 1)], kds[i]) for i in range(HB // SUB)], axis=0)
            gq = dqd * qd
            dq = dqd * jnp.exp(jnp.minimum(cmid, EXP_CLAMP)) + dqc * ecum
            gs = dkdec * kdec
            dk = dkdec * edec
            dcum = dqc * qc - gs
            dcum = dcum + jnp.where(row == HB - 1, jnp.sum(gs, axis=0, keepdims=True)
                                    + jnp.exp(last) * jnp.sum(st * dst, axis=0, keepdims=True), 0.0)
            qd_b = qd.astype(BF16)
            for i in range(HB // SUB):
                dkd = _dot_tn(dp_b[SUB * i:SUB * (i + 1)], qd_b[SUB * i:SUB * (i + 1)])
                mid = cum[SUB * i + SUB // 2 - 1:SUB * i + SUB // 2, :]
                dk = dk + dkd * jnp.exp(jnp.minimum(mid - cum, EXP_CLAMP))
                gk = dkd * kds[i]
                dcum = dcum - gk + jnp.where(row == SUB * i + SUB // 2 - 1, jnp.sum(gk, axis=0, keepdims=True), 0.0)
            dlf = _dot_sel(tri_t, dcum) + _dot_sel(trimid_t, gq)
            df = dlf / f - dk
            dz = df * (1.0 - lb) * sz * sneg
            dlb = jnp.sum(df * sneg, axis=0, keepdims=True) * lb * (1.0 - lb)
            dtab_ref[0:1, sl] += dlb
            dtab_ref[1:2, sl] -= dlb
            sq = _sig(qr)
            dp_ref[:, sl] = (dq * sq * (1.0 + qr * (1.0 - sq))).astype(BF16)
            dp_ref[:, HGW + HD * h:HGW + HD * (h + 1)] = dz.astype(BF16)
            dp_ref[:, 2 * HGW + HD * h:2 * HGW + HD * (h + 1)] = dv.astype(BF16)
            dp_ref[:, 3 * HGW + HD * h:3 * HGW + HD * (h + 1)] = dgr.astype(BF16)
        dgain_ref[...] += dgain

    rev = lambda cb: pl.BlockSpec((HB, HGW), functools.partial(lambda n, cb: (nblk - 1 - n, cb), cb=cb))
    return pl.pallas_call(
        body, name="hgrn_bwd", grid=(nblk,),
        in_specs=[rev(0), rev(1), rev(2), rev(3), pl.BlockSpec((2, HGW), lambda n: (0, 0)),
                  pl.BlockSpec((1, HD), lambda n: (0, 0)), rev(0),
                  pl.BlockSpec((NH, 1, HD, HD), lambda n: (0, nblk - 1 - n, 0, 0)), rev(0)],
        out_specs=[pl.BlockSpec((HB, 4 * HGW), lambda n: (nblk - 1 - n, 0)),
                   pl.BlockSpec((2, HGW), lambda n: (0, 0)), pl.BlockSpec((1, HD), lambda n: (0, 0))],
        out_shape=[jax.ShapeDtypeStruct((t, 4 * HGW), BF16), jax.ShapeDtypeStruct((2, HGW), F32),
                   jax.ShapeDtypeStruct((1, HD), F32)],
        scratch_shapes=[pltpu.VMEM((NH, HD, HD), F32)],
        compiler_params=_params(("arbitrary",)),
    )(p_hg, p_hg, p_hg, p_hg, table, gain, o_pre, states, dy)


def _rope_mat():
    r = lax.broadcasted_iota(jnp.int32, (ROPE, ROPE), 0)
    c = lax.broadcasted_iota(jnp.int32, (ROPE, ROPE), 1)
    half = ROPE // 2
    return ((r == c - half).astype(F32) - (r == c + half).astype(F32)).astype(BF16)


def _mla_prep_fwd(p_mla, cs, sn, wq, wkv, gql, gkvl, gq, gk, t):
    tm = _pick(t, (512, 256, 128))

    def body(p_ref, cs_ref, sn_ref, wq_ref, wkv_ref, gql_ref, gkvl_ref, gq_ref, gk_ref,
             q_ref, k_ref, v_ref):
        rmat = _rope_mat()
        cqn = _rms(p_ref[:, 0:QL], gql_ref[...]).astype(BF16)
        ckvn = _rms(p_ref[:, QL:QL + KVL], gkvl_ref[...]).astype(BF16)
        kpe = p_ref[:, QL + KVL:QL + KVL + ROPE]
        c, s = cs_ref[...], sn_ref[...]
        rot = lambda x: x * c + _sel_dot(x, rmat) * s
        for h in range(NH):
            qa, qr = _dot(cqn, wq_ref[h, :, 0:HD]), _dot(cqn, wq_ref[h, :, HD:QK])
            rq = lax.rsqrt((jnp.sum(qa * qa, -1, keepdims=True) + jnp.sum(qr * qr, -1, keepdims=True)) / QK + EPS)
            q_ref[h, :, 0:HD] = (qa * rq * gq_ref[:, 0:HD] * SCALE).astype(BF16)
            q_ref[h, :, HD:QK] = (rot(qr * rq * gq_ref[:, HD:QK]) * SCALE).astype(BF16)
            kn = _dot(ckvn, wkv_ref[h, :, 0:HD])
            rk = lax.rsqrt((jnp.sum(kn * kn, -1, keepdims=True) + jnp.sum(kpe * kpe, -1, keepdims=True)) / QK + EPS)
            k_ref[h, :, 0:HD] = (kn * rk * gk_ref[:, 0:HD]).astype(BF16)
            k_ref[h, :, HD:QK] = rot(kpe * rk * gk_ref[:, HD:QK]).astype(BF16)
            v_ref[h] = _dot(ckvn, wkv_ref[h, :, HD:2 * HD]).astype(BF16)

    whole = lambda a: pl.BlockSpec(a.shape, functools.partial(lambda i, nd: (0,) * nd, nd=a.ndim))
    return pl.pallas_call(
        body, name="mla_prep_fwd", grid=(t // tm,),
        in_specs=[pl.BlockSpec((tm, QL + KVL + ROPE), lambda i: (i, 0)), pl.BlockSpec((tm, ROPE), lambda i: (i, 0)),
                  pl.BlockSpec((tm, ROPE), lambda i: (i, 0))] + [whole(a) for a in (wq, wkv, gql, gkvl, gq, gk)],
        out_specs=[pl.BlockSpec((NH, tm, QK), lambda i: (0, i, 0)), pl.BlockSpec((NH, tm, QK), lambda i: (0, i, 0)),
                   pl.BlockSpec((NH, tm, HD), lambda i: (0, i, 0))],
        out_shape=[jax.ShapeDtypeStruct((NH, t, QK), BF16), jax.ShapeDtypeStruct((NH, t, QK), BF16),
                   jax.ShapeDtypeStruct((NH, t, HD), BF16)],
        compiler_params=_params(("parallel",)),
    )(p_mla, cs, sn, wq, wkv, gql, gkvl, gq, gk)


def _mla_prep_bwd(p_mla, cs, sn, wq, wkv, gql, gkvl, gq, gk, dq, dk, dv, t):
    tm = _pick(t, (512, 256, 128))

    def body(p_ref, cs_ref, sn_ref, wq_ref, wkv_ref, gql_ref, gkvl_ref, gq_ref, gk_ref,
             dq_ref, dk_ref, dv_ref,
             dp_ref, dwq_ref, dwkv_ref, dgql_ref, dgkvl_ref, dgq_ref, dgk_ref):
        accs = (dwq_ref, dwkv_ref, dgql_ref, dgkvl_ref, dgq_ref, dgk_ref)

        @pl.when(pl.program_id(0) == 0)
        def _():
            for r in accs:
                r[...] = jnp.zeros_like(r)

        rmat = _rope_mat()
        rmat_t = -rmat
        cq, ckv = p_ref[:, 0:QL], p_ref[:, QL:QL + KVL]
        kpe = p_ref[:, QL + KVL:QL + KVL + ROPE]
        cqn_f, ckvn_f = _rms(cq, gql_ref[...]), _rms(ckv, gkvl_ref[...])
        cqn, ckvn = cqn_f.astype(BF16), ckvn_f.astype(BF16)
        cqn_t, ckvn_t = cqn_f.T.astype(BF16), ckvn_f.T.astype(BF16)
        c, s = cs_ref[...], sn_ref[...]
        unrot = lambda dy: dy * c + _sel_dot(dy * s, rmat_t)
        dcqn = jnp.zeros((tm, QL), F32)
        dckvn = jnp.zeros((tm, KVL), F32)
        dkpe = jnp.zeros((tm, ROPE), F32)
        dgq_a, dgq_r = jnp.zeros((1, HD), F32), jnp.zeros((1, ROPE), F32)
        dgk_a, dgk_r = jnp.zeros((1, HD), F32), jnp.zeros((1, ROPE), F32)
        for h in range(NH):
            qa, qr = _dot(cqn, wq_ref[h, :, 0:HD]), _dot(cqn, wq_ref[h, :, HD:QK])
            rq = lax.rsqrt((jnp.sum(qa * qa, -1, keepdims=True) + jnp.sum(qr * qr, -1, keepdims=True)) / QK + EPS)
            xa, xr = qa * rq, qr * rq
            dya = dq_ref[h, :, 0:HD] * SCALE
            dyr = unrot(dq_ref[h, :, HD:QK] * SCALE)
            dgq_a += jnp.sum(dya * xa, axis=0, keepdims=True)
            dgq_r += jnp.sum(dyr * xr, axis=0, keepdims=True)
            ga, gr_ = dya * gq_ref[:, 0:HD], dyr * gq_ref[:, HD:QK]
            mean = (jnp.sum(ga * xa, -1, keepdims=True) + jnp.sum(gr_ * xr, -1, keepdims=True)) / QK
            dqa = (rq * (ga - xa * mean)).astype(BF16)
            dqr = (rq * (gr_ - xr * mean)).astype(BF16)
            dwq_ref[h, :, 0:HD] += _dot(cqn_t, dqa)
            dwq_ref[h, :, HD:QK] += _dot(cqn_t, dqr)
            dcqn += _dot_nt(dqa, wq_ref[h, :, 0:HD]) + _dot_nt(dqr, wq_ref[h, :, HD:QK])
            kn = _dot(ckvn, wkv_ref[h, :, 0:HD])
            rk = lax.rsqrt((jnp.sum(kn * kn, -1, keepdims=True) + jnp.sum(kpe * kpe, -1, keepdims=True)) / QK + EPS)
            ya, yr = kn * rk, kpe * rk
            dka = dk_ref[h, :, 0:HD]
            dkr = unrot(dk_ref[h, :, HD:QK])
            dgk_a += jnp.sum(dka * ya, axis=0, keepdims=True)
            dgk_r += jnp.sum(dkr * yr, axis=0, keepdims=True)
            ha, hr = dka * gk_ref[:, 0:HD], dkr * gk_ref[:, HD:QK]
            mean = (jnp.sum(ha * ya, -1, keepdims=True) + jnp.sum(hr * yr, -1, keepdims=True)) / QK
            dkn = (rk * (ha - ya * mean)).astype(BF16)
            dkpe += rk * (hr - yr * mean)
            dvh = dv_ref[h].astype(BF16)
            dwkv_ref[h, :, 0:HD] += _dot(ckvn_t, dkn)
            dwkv_ref[h, :, HD:2 * HD] += _dot(ckvn_t, dvh)
            dckvn += _dot_nt(dkn, wkv_ref[h, :, 0:HD]) + _dot_nt(dvh, wkv_ref[h, :, HD:2 * HD])
        dcq, dg1 = _rms_bwd(cq, gql_ref[...], dcqn)
        dckv, dg2 = _rms_bwd(ckv, gkvl_ref[...], dckvn)
        dp_ref[:, 0:QL] = dcq.astype(BF16)
        dp_ref[:, QL:QL + KVL] = dckv.astype(BF16)
        dp_ref[:, QL + KVL:QL + KVL + ROPE] = dkpe.astype(BF16)
        dgql_ref[...] += dg1
        dgkvl_ref[...] += dg2
        dgq_ref[:, 0:HD] += dgq_a
        dgq_ref[:, HD:QK] += dgq_r
        dgk_ref[:, 0:HD] += dgk_a
        dgk_ref[:, HD:QK] += dgk_r

    whole = lambda a: pl.BlockSpec(a.shape, functools.partial(lambda i, nd: (0,) * nd, nd=a.ndim))
    acc_shapes = [wq.shape, wkv.shape, gql.shape, gkvl.shape, gq.shape, gk.shape]
    return pl.pallas_call(
        body, name="mla_prep_bwd", grid=(t // tm,),
        in_specs=[pl.BlockSpec((tm, QL + KVL + ROPE), lambda i: (i, 0)), pl.BlockSpec((tm, ROPE), lambda i: (i, 0)),
                  pl.BlockSpec((tm, ROPE), lambda i: (i, 0))]
        + [whole(a) for a in (wq, wkv, gql, gkvl, gq, gk)]
        + [pl.BlockSpec((NH, tm, QK), lambda i: (0, i, 0)), pl.BlockSpec((NH, tm, QK), lambda i: (0, i, 0)),
           pl.BlockSpec((NH, tm, HD), lambda i: (0, i, 0))],
        out_specs=[pl.BlockSpec((tm, QL + KVL + ROPE), lambda i: (i, 0))]
        + [pl.BlockSpec(s, functools.partial(lambda i, nd: (0,) * nd, nd=len(s))) for s in acc_shapes],
        out_shape=[jax.ShapeDtypeStruct((t, QL + KVL + ROPE), BF16)]
        + [jax.ShapeDtypeStruct(s, F32) for s in acc_shapes],
        compiler_params=_params(("arbitrary",)),
    )(p_mla, cs, sn, wq, wkv, gql, gkvl, gq, gk, dq, dk, dv)


def _chunk_mask(nq, nk, key0, keys_on_rows):
    shape = (nk, nq) if keys_on_rows else (nq, nk)
    qi = lax.broadcasted_iota(jnp.int32, shape, 1 if keys_on_rows else 0) // CHUNK
    ki = lax.broadcasted_iota(jnp.int32, shape, 0 if keys_on_rows else 1) // CHUNK + key0 // CHUNK
    return ki <= qi


def _flash_fwd(q, k, v, t):
    tq = _pick(t, ATT_Q)
    tk = tq // ATT_KEY_TILES

    def body(q_ref, k_ref, v_ref, o_ref, lse_ref):
        i = pl.program_id(1)
        qt = q_ref[0]

        def step(j, carry, key0):
            m, l, acc = carry
            cols = pl.ds(pl.multiple_of(j * tk, tk), tk)
            s = _dot_nt(qt, k_ref[0, cols, :])
            if key0 is not None:
                s = jnp.where(_chunk_mask(tq, tk, key0, False), s, -jnp.inf)
            m_new = jnp.maximum(m, jnp.max(s, axis=-1, keepdims=True))
            p = jnp.exp(s - m_new)
            alpha = jnp.exp(m - m_new)
            return m_new, alpha * l + jnp.sum(p, axis=-1, keepdims=True), alpha * acc + _dot(p, v_ref[0, cols, :])

        init = (jnp.full((tq, 1), -jnp.inf, F32), jnp.zeros((tq, 1), F32), jnp.zeros((tq, HD), F32))
        carry = lax.fori_loop(0, ATT_KEY_TILES * i, lambda j, cr: step(j, cr, None), init)
        for h in range(ATT_KEY_TILES):
            carry = step(ATT_KEY_TILES * i + h, carry, h * tk)
        m, l, acc = carry
        o_ref[...] = (acc / l).astype(BF16)
        lse_ref[0] = m + jnp.log(l)

    return pl.pallas_call(
        body, name="flash_fwd", grid=(NH, t // tq),
        in_specs=[pl.BlockSpec((1, tq, QK), lambda h, i: (h, i, 0)), pl.BlockSpec((1, t, QK), lambda h, i: (h, 0, 0)),
                  pl.BlockSpec((1, t, HD), lambda h, i: (h, 0, 0))],
        out_specs=[pl.BlockSpec((tq, HD), lambda h, i: (i, h)), pl.BlockSpec((1, tq, 1), lambda h, i: (h, i, 0))],
        out_shape=[jax.ShapeDtypeStruct((t, NH * HD), BF16), jax.ShapeDtypeStruct((NH, t, 1), F32)],
        compiler_params=_params(("parallel", "parallel")),
    )(q, k, v)


def _flash_delta(o, do, t):
    tq = _pick(t, ATT_TILES)

    def body(o_ref, do_ref, delta_ref):
        prod = do_ref[...].astype(F32) * o_ref[...].astype(F32)
        x1, x2, x3 = _split3(prod)
        ones = jnp.ones((8, HD), BF16)
        d = lambda p: lax.dot_general(ones, p, (((1,), (1,)), ((), ())), preferred_element_type=F32)
        delta_ref[0] = (d(x1) + d(x2) + d(x3))[0:1, :]

    return pl.pallas_call(
        body, name="flash_delta", grid=(NH, t // tq),
        in_specs=[pl.BlockSpec((tq, HD), lambda h, i: (i, h)), pl.BlockSpec((tq, HD), lambda h, i: (i, h))],
        out_specs=pl.BlockSpec((1, 1, tq), lambda h, i: (h, 0, i)),
        out_shape=jax.ShapeDtypeStruct((NH, 1, t), F32), compiler_params=_params(("parallel", "parallel")),
    )(o, do)


def _flash_bwd(q, k, v, do, lse_row, delta_row, t):
    tq = _pick(t, ATT_Q)
    tk = tq // ATT_KEY_TILES

    def body(q_ref, k_ref, v_ref, do_ref, lse_ref, delta_ref, dq_ref, dk_ref, dv_ref):
        j = pl.program_id(1)

        @pl.when(j == 0)
        def _():
            dq_ref[...] = jnp.zeros_like(dq_ref)

        kt, vt = k_ref[0], v_ref[0]

        def step(i, carry, key0):
            dk, dv = carry
            rows = pl.ds(pl.multiple_of(i * tq, tq), tq)
            qt, dot_ = q_ref[0, rows, :], do_ref[rows, :]
            p = jnp.exp(_dot_nt(kt, qt) - lse_ref[0, :, rows])
            if key0 is not None:
                p = jnp.where(_chunk_mask(tq, tk, key0, True), p, 0.0)
            ds = (p * (_dot_nt(vt, dot_) - delta_ref[0, :, rows])).astype(BF16)
            dq_ref[0, rows, :] += _dot_tn(ds, kt)
            return dk + _dot(ds, qt), dv + _dot(p, dot_)

        own = j // ATT_KEY_TILES
        carry = step(own, (jnp.zeros((tk, QK), F32), jnp.zeros((tk, HD), F32)), (j % ATT_KEY_TILES) * tk)
        dk, dv = lax.fori_loop(own + 1, t // tq, lambda i, cr: step(i, cr, None), carry)
        dk_ref[0] = dk
        dv_ref[0] = dv

    return pl.pallas_call(
        body, name="flash_bwd", grid=(NH, t // tk),
        in_specs=[pl.BlockSpec((1, t, QK), lambda h, j: (h, 0, 0)), pl.BlockSpec((1, tk, QK), lambda h, j: (h, j, 0)),
                  pl.BlockSpec((1, tk, HD), lambda h, j: (h, j, 0)), pl.BlockSpec((t, HD), lambda h, j: (0, h)),
                  pl.BlockSpec((1, 1, t), lambda h, j: (h, 0, 0)), pl.BlockSpec((1, 1, t), lambda h, j: (h, 0, 0))],
        out_specs=[pl.BlockSpec((1, t, QK), lambda h, j: (h, 0, 0)), pl.BlockSpec((1, tk, QK), lambda h, j: (h, j, 0)),
                   pl.BlockSpec((1, tk, HD), lambda h, j: (h, j, 0))],
        out_shape=[jax.ShapeDtypeStruct((NH, t, QK), F32), jax.ShapeDtypeStruct((NH, t, QK), F32),
                   jax.ShapeDtypeStruct((NH, t, HD), F32)],
        compiler_params=_params(("parallel", "arbitrary")),
    )(q, k, v, do, lse_row, delta_row)


def _ffn_in(xn, w_in, name):
    t, k = xn.shape
    s, _, n = w_in.shape
    tm = _pick(t, ROW_TILES)

    def body(x_ref, w_ref, gu_ref, a_ref):
        x = x_ref[...]
        g, u = _dot(x, w_ref[0]), _dot(x, w_ref[1])
        gu_ref[0] = g.astype(BF16)
        gu_ref[1] = u.astype(BF16)
        a_ref[...] = (g * _sig(g) * u).astype(BF16)

    return pl.pallas_call(
        body, name=name, grid=(t // tm, s // 2),
        in_specs=[_BS((tm, k), lambda i, j: (i, 0)), _BS((2, None, k, n), lambda i, j: (0, j, 0, 0))],
        out_specs=[_BS((2, None, tm, n), lambda i, j: (0, j, i, 0)), _BS((None, tm, n), lambda i, j: (j, i, 0))],
        out_shape=[jax.ShapeDtypeStruct((2, s // 2, t, n), BF16), jax.ShapeDtypeStruct((s // 2, t, n), BF16)],
        compiler_params=_params(("parallel", "parallel")),
    )(xn, w_in.reshape(2, s // 2, k, n))


def _ffn_dgu(dfo, w_out, gu, name):
    t, k = dfo.shape
    s, n, _ = w_out.shape
    tm = _pick(t, ROW_TILES)

    def body(d_ref, w_ref, gu_ref, o_ref):
        da = _dot_nt(d_ref[...], w_ref[...])
        g, u = gu_ref[0].astype(F32), gu_ref[1].astype(F32)
        sg = _sig(g)
        o_ref[0] = (da * u * sg * (1.0 + g * (1.0 - sg))).astype(BF16)
        o_ref[1] = (da * g * sg).astype(BF16)

    pair = _BS((2, None, tm, n), lambda i, j: (0, j, i, 0))
    return pl.pallas_call(
        body, name=name, grid=(t // tm, s),
        in_specs=[_BS((tm, k), lambda i, j: (i, 0)), _BS((None, n, k), lambda i, j: (j, 0, 0)), pair],
        out_specs=pair, out_shape=jax.ShapeDtypeStruct((2, s, t, n), BF16),
        compiler_params=_params(("parallel", "parallel")),
    )(dfo, w_out, gu).reshape(2 * s, t, n)


def _ffn_fwd(xn, w_in, w_out, tag):
    gu, a = _ffn_in(xn, w_in, tag + "_in")
    return gu, a, _mm_stack_red(a, w_out, tag + "_out", "nn")


def _ffn_bwd(dfo, xn, gu, a, w_in, w_out, tag):
    dw_out = _mm_stack_tn(a, dfo, tag + "_dwout")
    dgu = _ffn_dgu(dfo, w_out, gu, tag + "_dgu")
    return _mm_stack_red(dgu, w_in, tag + "_dxn", "nt"), _mm_stack_tn(xn, dgu, tag + "_dwin"), dw_out


def _local_step(x, target, cs, sn, w, late_weights, s, early_grads):
    t = x.shape[0]
    tm = _pick(t, (256, 128))
    g = {}
    ffn_out = lambda n: w[n].reshape(4, FF // 4, D)

    def norm_fn(x_ref, g_ref, o_ref):
        o_ref[...] = _rms(x_ref[...], g_ref[...]).astype(BF16)

    xn1 = _rows(norm_fn, "norm1", t, tm, [(x, D, 0)], [s["ffn1_norm"]], [(D, BF16)])[0]
    gu1, a1, f1 = _ffn_fwd(xn1, w["ffn1_w_in"], ffn_out("ffn1_w_out"), "ffn1")

    def res_norm_fn(scale):
        def fn(h_ref, f_ref, g_ref, h_out, n_out):
            h = h_ref[...] + scale * f_ref[...]
            h_out[...] = h
            n_out[...] = _rms(h, g_ref[...]).astype(BF16)
        return fn

    h1, u = _rows(res_norm_fn(0.5), "res_norm1", t, tm, [(x, D, 0), (f1, D, 0)], [s["mix_norm"]],
                  [(D, F32), (D, BF16)])
    w = {**w, **late_weights(h1)}
    rows_of = lambda n: w[n].reshape(-1, w[n].shape[2])
    w_hgb, w_mlab, w_o = rows_of("w_hg_branch"), rows_of("w_mla_branch"), rows_of("w_out")
    w_in_nat = w["w_in"].transpose(1, 0, 2).reshape(D, -1)
    w_mrg = w["w_merge"]
    mw = w_mrg.shape[2]
    w_in_hg, w_in_mla = w_in_nat[:, :4 * HGW], w_in_nat[:, 4 * HGW:]
    p_hg = _mm(u, w_in_hg, "nn", "proj_hg")
    p_mla = _mm(u, w_in_mla, "nn", "proj_mla")
    gpre = _cols_fwd(u, w_mrg, "proj_gate")
    o_pre, hgy, states = _hgrn_fwd(p_hg, s["hg_lb_table"], s["hg_out_norm"], t)
    prep_args = (p_mla, cs, sn, w["w_q_up"], w["w_kv_up"], s["mla_q_lora_norm"], s["mla_kv_lora_norm"],
                 s["q_head_norm"], s["k_head_norm"])
    q, k, v = _mla_prep_fwd(*prep_args, t)
    att, lse = _flash_fwd(q, k, v, t)
    y_hg = _mm(hgy, w_hgb, "nn", "branch_hg")
    y_mla = _mm(att, w_mlab, "nn", "branch_mla")

    def mix_fn(gh_ref, gm_ref, yh_ref, ym_ref, b_ref, o_ref):
        gh = _sig(gh_ref[...] + b_ref[:, 0:D])
        gm = _sig(gm_ref[...] + b_ref[:, D:2 * D])
        o_ref[...] = (gh * yh_ref[...] + gm * ym_ref[...]).astype(BF16)

    mixed = _rows(mix_fn, "mix", t, tm, [(gpre, D, 0), (gpre, D, 1), (y_hg, D, 0), (y_mla, D, 0)], [s["b_merge"]],
                  [(D, BF16)])[0]
    mo = _mm(mixed, w_o, "nn", "mix_out")
    h2, xn2 = _rows(res_norm_fn(1.0), "res_norm2", t, tm, [(h1, D, 0), (mo, D, 0)], [s["ffn2_norm"]],
                    [(D, F32), (D, BF16)])
    gu2, a2, f2 = _ffn_fwd(xn2, w["ffn2_w_in"], ffn_out("ffn2_w_out"), "ffn2")

    def loss_fn(h_ref, f_ref, tg_ref, g_ref, dh_out, dhb_out):
        h = h_ref[...] + 0.5 * f_ref[...]
        e = _rms(h, g_ref[...]) - tg_ref[...]
        dh, dgain = _rms_bwd(h, g_ref[...], e / D)
        dh_out[...] = dh
        dhb_out[...] = (0.5 * dh).astype(BF16)
        return dgain, jnp.full((1, LANE), 0.5 / D * jnp.sum(e * e), F32)

    dh3, dfo2, g["final_norm"], loss = _rows(loss_fn, "loss", t, tm, [(h2, D, 0), (f2, D, 0), (target, D, 0)],
                                             [s["final_norm"]], [(D, F32), (D, BF16)], [(1, D), (1, LANE)])

    def norm_bwd_fn(scale):
        def fn(h_ref, dxn_ref, dh_ref, g_ref, dh_out, dhb_out):
            dx, dgain = _rms_bwd(h_ref[...], g_ref[...], dxn_ref[...])
            dh = dh_ref[...] + dx
            dh_out[...] = dh
            dhb_out[...] = (scale * dh).astype(BF16)
            return (dgain,)
        return fn

    as_rows = lambda a: a.reshape((N_DEV, -1) + a.shape[-1:])
    dxn2, g["ffn2_w_in"], dwo = _ffn_bwd(dfo2, xn2, gu2, a2, w["ffn2_w_in"], ffn_out("ffn2_w_out"), "ffn2")
    g["ffn2_w_out"] = as_rows(dwo)
    dh2, dh2b, g["ffn2_norm"] = _rows(norm_bwd_fn(1.0), "norm2_bwd", t, tm, [(h2, D, 0), (dxn2, D, 0), (dh3, D, 0)],
                                      [s["ffn2_norm"]], [(D, F32), (D, BF16)], [(1, D)])
    dmixed = _mm(dh2b, w_o, "nt", "mix_out_dx")
    g["w_out"] = as_rows(_mm(mixed, dh2b, "tn", "mix_out_dw"))

    def mix_bwd_fn(gh_ref, gm_ref, yh_ref, ym_ref, dm_ref, b_ref, dyh_out, dym_out, dg_out):
        gh = _sig(gh_ref[...] + b_ref[:, 0:D])
        gm = _sig(gm_ref[...] + b_ref[:, D:2 * D])
        dm = dm_ref[...]
        dyh_out[...] = (dm * gh).astype(BF16)
        dym_out[...] = (dm * gm).astype(BF16)
        dgh = dm * yh_ref[...] * gh * (1.0 - gh)
        dgm = dm * ym_ref[...] * gm * (1.0 - gm)
        dg_out[:, 0:D] = dgh.astype(BF16)
        dg_out[:, D:2 * D] = dgm.astype(BF16)
        return (jnp.concatenate([jnp.sum(dgh, axis=0, keepdims=True), jnp.sum(dgm, axis=0, keepdims=True)], axis=1),)

    dyh, dym, dgpre, g["b_merge"] = _rows(
        mix_bwd_fn, "mix_bwd", t, tm, [(gpre, D, 0), (gpre, D, 1), (y_hg, D, 0), (y_mla, D, 0), (dmixed, D, 0)],
        [s["b_merge"]], [(D, BF16), (D, BF16), (2 * D, BF16)], [(1, 2 * D)])
    g["w_hg_branch"] = as_rows(_mm(hgy, dyh, "tn", "branch_hg_dw"))
    g["w_mla_branch"] = as_rows(_mm(att, dym, "tn", "branch_mla_dw"))
    g["w_merge"] = _cols_dw(u, dgpre, mw, "proj_gate_dw")
    dhgy = _mm(dyh, w_hgb, "nt", "branch_hg_dx")
    datt = _mm(dym, w_mlab, "nt", "branch_mla_dx", out_dtype=BF16)
    du_gate = _cols_dx(dgpre, w_mrg, "proj_gate_dx")

    dq, dk, dv = _flash_bwd(q, k, v, datt, lse.reshape(NH, 1, t), _flash_delta(att, datt, t), t)
    (dp_mla, g["w_q_up"], g["w_kv_up"], g["mla_q_lora_norm"], g["mla_kv_lora_norm"], g["q_head_norm"],
     g["k_head_norm"]) = _mla_prep_bwd(*prep_args, dq, dk, dv, t)
    dp_hg, g["hg_lb_table"], g["hg_out_norm"] = _hgrn_bwd(p_hg, s["hg_lb_table"], s["hg_out_norm"], o_pre, states,
                                                          dhgy, t)
    dw_in_nat = jnp.concatenate([_mm(u, dp_hg, "tn", "proj_hg_dw"), _mm(u, dp_mla, "tn", "proj_mla_dw")], axis=1)
    g["w_in"] = dw_in_nat.reshape(D, N_DEV, -1).transpose(1, 0, 2)
    du_hg = _mm(dp_hg, w_in_hg, "nt", "proj_hg_dx")
    du_mla = _mm(dp_mla, w_in_mla, "nt", "proj_mla_dx")

    def mixnorm_bwd_fn(h_ref, a_ref, b_ref, c_ref, dh_ref, g_ref, dh_out, dhb_out):
        dx, dgain = _rms_bwd(h_ref[...], g_ref[...], a_ref[...] + b_ref[...] + c_ref[...])
        dh = dh_ref[...] + dx
        dh_out[...] = dh
        dhb_out[...] = (0.5 * dh).astype(BF16)
        return (dgain,)

    mix_gain = s["mix_norm"] + early_grads(g)[0:1, 0:1]
    dh1, dfo1, g["mix_norm"] = _rows(mixnorm_bwd_fn, "mixnorm_bwd", t, tm,
                                     [(h1, D, 0), (du_hg, D, 0), (du_mla, D, 0), (du_gate, D, 0), (dh2, D, 0)],
                                     [mix_gain], [(D, F32), (D, BF16)], [(1, D)])
    dxn1, g["ffn1_w_in"], dwo = _ffn_bwd(dfo1, xn1, gu1, a1, w["ffn1_w_in"], ffn_out("ffn1_w_out"), "ffn1")
    g["ffn1_w_out"] = as_rows(dwo)
    grad_x, _, g["ffn1_norm"] = _rows(norm_bwd_fn(1.0), "norm1_bwd", t, tm, [(x, D, 0), (dxn1, D, 0), (dh1, D, 0)],
                                      [s["ffn1_norm"]], [(D, F32), (D, BF16)], [(1, D)])
    return loss, grad_x, g


def _coords():
    return lax.axis_index("x"), lax.axis_index("y"), lax.axis_index("c")


def _hbm_call(body, name, ins, out_shapes, scratch):
    any_spec = pl.BlockSpec(memory_space=pl.ANY)
    return pl.pallas_call(
        body, name=name, out_shape=[jax.ShapeDtypeStruct(s, dt) for s, dt in out_shapes],
        in_specs=[any_spec] * len(ins), out_specs=[any_spec] * len(out_shapes), scratch_shapes=scratch,
    )(*ins)


def _all_gather(blocks, name):
    nb = len(blocks)

    def body(*refs):
        x_refs, out_refs = refs[:nb], refs[nb:2 * nb]
        send_sems, recv_sems, local_sems = refs[2 * nb:]
        x, y, c = _coords()
        me, sibling = (x, y, c), (x, y, 1 - c)
        chips = [(1 - x, y), (x, 1 - y), (1 - x, 1 - y)]

        def slot(b, px, py, pc):
            return out_refs[b].at[4 * px + 2 * py + pc]

        def copy(b, kk, block_of, to, src=None):
            return pltpu.make_async_remote_copy(
                src_ref=slot(b, *block_of) if src is None else src, dst_ref=slot(b, *block_of),
                send_sem=send_sems.at[b, kk], recv_sem=recv_sems.at[b, kk], device_id=to, device_id_type=MESH)

        mine = [pltpu.make_async_copy(x_refs[b], slot(b, *me), local_sems.at[b]) for b in range(nb)]
        for cp in mine:
            cp.start()
        first = [copy(b, 0, me, sibling, src=x_refs[b]) for b in range(nb)]
        first += [copy(b, 1 + j, me, (*chip, c), src=x_refs[b]) for j, chip in enumerate(chips) for b in range(nb)]
        for cp in first:
            cp.start()
        passed = []
        for j, chip in enumerate(chips):
            for b in range(nb):
                copy(b, 1 + j, (*chip, c), me).wait_recv()
                passed.append(copy(b, 4 + j, (*chip, c), sibling))
                passed[-1].start()
        for b in range(nb):
            copy(b, 0, sibling, me).wait_recv()
        for j, chip in enumerate(chips):
            for b in range(nb):
                copy(b, 4 + j, (*chip, 1 - c), me).wait_recv()
        for cp in first + passed:
            cp.wait_send()
        for cp in mine:
            cp.wait()

    return _hbm_call(body, name, blocks, [((N_DEV,) + b.shape, b.dtype) for b in blocks],
                     [pltpu.SemaphoreType.DMA((nb, 7)), pltpu.SemaphoreType.DMA((nb, 7)),
                      pltpu.SemaphoreType.DMA((nb,))])


def _gather_peers():
    x, y, c = _coords()
    return (x, y, c), [(x, y, 1 - c), (1 - x, y, c), (x, 1 - y, c), (1 - x, 1 - y, c)]


def _gather_start(blocks, after, name):
    nb = len(blocks)
    hbm, sem = pl.BlockSpec(memory_space=pltpu.HBM), pl.BlockSpec(memory_space=pltpu.SEMAPHORE)

    def body(*refs):
        x_refs, out_refs = refs[:nb], refs[nb:2 * nb]
        send_sems, recv_sems, token = refs[2 * nb + 1], refs[2 * nb + 2], refs[-1]
        (x, y, c), peers = _gather_peers()
        for kk, peer in enumerate(peers):
            for b in range(nb):
                pltpu.make_async_remote_copy(
                    src_ref=x_refs[b], dst_ref=out_refs[b].at[4 * x + 2 * y + c], send_sem=send_sems.at[4 * b + kk],
                    recv_sem=recv_sems.at[4 * b + kk], device_id=peer, device_id_type=MESH).start()
        token[...] = jnp.zeros_like(token)

    gathers = [pltpu.with_memory_space_constraint(lax.empty((N_DEV,) + b.shape, b.dtype), pltpu.HBM) for b in blocks]
    outs = pl.pallas_call(
        body, name=name,
        out_shape=(pltpu.SemaphoreType.DMA((4 * nb,)), pltpu.SemaphoreType.DMA((4 * nb,)),
                   *[pltpu.HBM(b.shape, b.dtype) for b in blocks], *[pltpu.HBM(b.shape, b.dtype) for b in gathers],
                   jax.ShapeDtypeStruct((SUBLANE, LANE), F32)),
        in_specs=[hbm] * (2 * nb) + [pl.BlockSpec(memory_space=pl.ANY)],
        out_specs=(sem, sem, *[hbm] * (2 * nb), pl.BlockSpec(memory_space=pltpu.VMEM)),
        input_output_aliases={i: 2 + i for i in range(2 * nb)},
        compiler_params=pltpu.CompilerParams(has_side_effects=pltpu.SideEffectType.DATAFLOW_SIDE_EFFECTING),
    )(*[pltpu.with_memory_space_constraint(b, pltpu.HBM) for b in blocks], *gathers, after)
    return outs[0], outs[1], list(outs[2:2 + nb]), list(outs[2 + nb:2 + 2 * nb]), outs[-1]


def _gather_wait(send_sems, recv_sems, thru, gathers, after, name):
    nb = len(thru)
    hbm, sem = pl.BlockSpec(memory_space=pltpu.HBM), pl.BlockSpec(memory_space=pltpu.SEMAPHORE)

    def body(*refs):
        x_refs, out_refs = refs[:nb], refs[nb:2 * nb]
        send_sems_, recv_sems_ = refs[2 * nb], refs[2 * nb + 1]
        _, peers = _gather_peers()
        for kk, (px, py, pc) in enumerate(peers):
            for b in range(nb):
                cp = pltpu.make_async_remote_copy(
                    src_ref=x_refs[b], dst_ref=out_refs[b].at[4 * px + 2 * py + pc], send_sem=send_sems_.at[4 * b + kk],
                    recv_sem=recv_sems_.at[4 * b + kk], device_id=(px, py, pc), device_id_type=MESH)
                cp.wait_send()
                cp.wait_recv()

    outs = pl.pallas_call(
        body, name=name,
        out_shape=(*[pltpu.HBM(b.shape, b.dtype) for b in thru], *[pltpu.HBM(b.shape, b.dtype) for b in gathers]),
        in_specs=[hbm] * (2 * nb) + [sem, sem, pl.BlockSpec(memory_space=pl.ANY)], out_specs=[hbm] * (2 * nb),
        input_output_aliases={i: i for i in range(2 * nb)},
        compiler_params=pltpu.CompilerParams(has_side_effects=pltpu.SideEffectType.DATAFLOW_SIDE_EFFECTING),
    )(*thru, *gathers, send_sems, recv_sems, after)
    return list(outs[:nb]), list(outs[nb:])


def _gather_finish(blocks, gathers, name):
    nb = len(blocks)

    def body(*refs):
        x_refs, in_refs, out_refs = refs[:nb], refs[nb:2 * nb], refs[2 * nb:3 * nb]
        send_sems, recv_sems, local_sems = refs[3 * nb:]
        (x, y, c), peers = _gather_peers()
        mine = [pltpu.make_async_copy(x_refs[b], out_refs[b].at[4 * x + 2 * y + c], local_sems.at[b]) for b in range(nb)]
        for cp in mine:
            cp.start()
        copies = []
        for j, (px, py, _) in enumerate(peers[1:]):
            for b in range(nb):
                copies.append(pltpu.make_async_remote_copy(
                    src_ref=in_refs[b].at[4 * px + 2 * py + c], dst_ref=out_refs[b].at[4 * px + 2 * py + c],
                    send_sem=send_sems.at[b, j], recv_sem=recv_sems.at[b, j], device_id=(x, y, 1 - c),
                    device_id_type=MESH))
                copies[-1].start()
        for j, (px, py, _) in enumerate(peers[1:]):
            for b in range(nb):
                pltpu.make_async_remote_copy(
                    src_ref=in_refs[b].at[4 * px + 2 * py + c], dst_ref=out_refs[b].at[4 * px + 2 * py + 1 - c],
                    send_sem=send_sems.at[b, j], recv_sem=recv_sems.at[b, j], device_id=(x, y, 1 - c),
                    device_id_type=MESH).wait_recv()
        for cp in copies:
            cp.wait_send()
        for cp in mine:
            cp.wait()

    any_spec = pl.BlockSpec(memory_space=pl.ANY)
    return pl.pallas_call(
        body, name=name, out_shape=[jax.ShapeDtypeStruct(b.shape, b.dtype) for b in gathers],
        in_specs=[any_spec] * (2 * nb), out_specs=[any_spec] * nb,
        input_output_aliases={nb + i: i for i in range(nb)},
        scratch_shapes=[pltpu.SemaphoreType.DMA((nb, 3)), pltpu.SemaphoreType.DMA((nb, 3)),
                        pltpu.SemaphoreType.DMA((nb,))],
    )(*blocks, *gathers)


def _sibling_swap(bufs, name):
    nb = len(bufs)

    def body(*refs):
        x_refs, out_refs = refs[:nb], refs[nb:2 * nb]
        send_sems, recv_sems = refs[2 * nb:]
        x, y, c = _coords()
        copies = [pltpu.make_async_remote_copy(
            src_ref=x_refs[b].at[2 * q + 1 - c], dst_ref=out_refs[b].at[q], send_sem=send_sems.at[b, q],
            recv_sem=recv_sems.at[b, q], device_id=(x, y, 1 - c), device_id_type=MESH)
            for b in range(nb) for q in range(4)]
        for cp in copies:
            cp.start()
        for cp in copies:
            cp.wait()

    return _hbm_call(body, name, bufs, [((4,) + b.shape[1:], b.dtype) for b in bufs],
                     [pltpu.SemaphoreType.DMA((nb, 4)), pltpu.SemaphoreType.DMA((nb, 4))])


def _chip_exchange(bufs, name):
    nb = len(bufs)

    def body(*refs):
        x_refs, out_refs = refs[:nb], refs[nb:2 * nb]
        send_sems, recv_sems, local_sems = refs[2 * nb:]
        x, y, c = _coords()
        chips = [(1 - x, y), (x, 1 - y), (1 - x, 1 - y)]
        mine = [pltpu.make_async_copy(x_refs[b].at[2 * x + y], out_refs[b].at[2 * x + y], local_sems.at[b])
                for b in range(nb)]
        for cp in mine:
            cp.start()
        copies = [pltpu.make_async_remote_copy(
            src_ref=x_refs[b].at[2 * px + py], dst_ref=out_refs[b].at[2 * x + y], send_sem=send_sems.at[b, j],
            recv_sem=recv_sems.at[b, j], device_id=(px, py, c), device_id_type=MESH)
            for j, (px, py) in enumerate(chips) for b in range(nb)]
        for cp in copies:
            cp.start()
        for j, (px, py) in enumerate(chips):
            for b in range(nb):
                pltpu.make_async_remote_copy(
                    src_ref=x_refs[b].at[2 * x + y], dst_ref=out_refs[b].at[2 * px + py], send_sem=send_sems.at[b, j],
                    recv_sem=recv_sems.at[b, j], device_id=(px, py, c), device_id_type=MESH).wait_recv()
        for cp in copies:
            cp.wait_send()
        for cp in mine:
            cp.wait()

    return _hbm_call(body, name, bufs, [(b.shape, b.dtype) for b in bufs],
                     [pltpu.SemaphoreType.DMA((nb, 3)), pltpu.SemaphoreType.DMA((nb, 3)),
                      pltpu.SemaphoreType.DMA((nb,))])


def _chip_exchange_start(bufs, name):
    nb = len(bufs)
    hbm, sem = pl.BlockSpec(memory_space=pltpu.HBM), pl.BlockSpec(memory_space=pltpu.SEMAPHORE)

    def body(*refs):
        x_refs, land_refs = refs[:nb], refs[nb:2 * nb]
        send_sems, recv_sems, token = refs[2 * nb], refs[2 * nb + 1], refs[-1]
        x, y, c = _coords()
        for j, (px, py) in enumerate([(1 - x, y), (x, 1 - y), (1 - x, 1 - y)]):
            for b in range(nb):
                pltpu.make_async_remote_copy(
                    src_ref=x_refs[b].at[2 * px + py], dst_ref=land_refs[b].at[2 * x + y], send_sem=send_sems.at[3 * b + j],
                    recv_sem=recv_sems.at[3 * b + j], device_id=(px, py, c), device_id_type=MESH).start()
        token[...] = jnp.zeros_like(token)

    lands = [pltpu.with_memory_space_constraint(lax.empty(b.shape, b.dtype), pltpu.HBM) for b in bufs]
    outs = pl.pallas_call(
        body, name=name,
        out_shape=(pltpu.SemaphoreType.DMA((3 * nb,)), pltpu.SemaphoreType.DMA((3 * nb,)),
                   *[pltpu.HBM(b.shape, b.dtype) for b in bufs], *[pltpu.HBM(b.shape, b.dtype) for b in bufs],
                   jax.ShapeDtypeStruct((SUBLANE, LANE), F32)),
        in_specs=[hbm] * (2 * nb), out_specs=(sem, sem, *[hbm] * (2 * nb), pl.BlockSpec(memory_space=pltpu.VMEM)),
        input_output_aliases={i: 2 + i for i in range(2 * nb)},
        compiler_params=pltpu.CompilerParams(has_side_effects=pltpu.SideEffectType.DATAFLOW_SIDE_EFFECTING),
    )(*[pltpu.with_memory_space_constraint(b, pltpu.HBM) for b in bufs], *lands)
    return outs[0], outs[1], list(outs[2:2 + nb]), list(outs[2 + nb:2 + 2 * nb]), outs[-1]


def _chip_exchange_wait(send_sems, recv_sems, thru, lands, after, name):
    nb = len(thru)
    hbm, sem = pl.BlockSpec(memory_space=pltpu.HBM), pl.BlockSpec(memory_space=pltpu.SEMAPHORE)

    def body(*refs):
        x_refs, land_refs = refs[:nb], refs[nb:2 * nb]
        send_sems_, recv_sems_ = refs[2 * nb], refs[2 * nb + 1]
        x, y, c = _coords()
        for j, (px, py) in enumerate([(1 - x, y), (x, 1 - y), (1 - x, 1 - y)]):
            for b in range(nb):
                cp = pltpu.make_async_remote_copy(
                    src_ref=x_refs[b].at[2 * px + py], dst_ref=land_refs[b].at[2 * px + py],
                    send_sem=send_sems_.at[3 * b + j], recv_sem=recv_sems_.at[3 * b + j], device_id=(px, py, c),
                    device_id_type=MESH)
                cp.wait_send()
                cp.wait_recv()

    outs = pl.pallas_call(
        body, name=name,
        out_shape=(*[pltpu.HBM(b.shape, b.dtype) for b in thru], *[pltpu.HBM(b.shape, b.dtype) for b in lands]),
        in_specs=[hbm] * (2 * nb) + [sem, sem, pl.BlockSpec(memory_space=pl.ANY)], out_specs=[hbm] * (2 * nb),
        input_output_aliases={i: i for i in range(2 * nb)},
        compiler_params=pltpu.CompilerParams(has_side_effects=pltpu.SideEffectType.DATAFLOW_SIDE_EFFECTING),
    )(*thru, *lands, send_sems, recv_sems, after)
    return list(outs[:nb]), list(outs[nb:])


def _chip_sum(g, r1, c, name):
    _, r, cw = g.shape
    tr = _pick(r, (256, 176, 128))

    def body(c_ref, g_ref, r_ref, o_ref):
        o_ref[...] = (g_ref[...] + r_ref[...]).astype(BF16)

    grid_spec = pltpu.PrefetchScalarGridSpec(
        num_scalar_prefetch=1, grid=(4, r // tr),
        in_specs=[_BS((None, None, tr, cw), lambda q, i, c_ref: (q, c_ref[0], i, 0)),
                  _BS((None, tr, cw), lambda q, i, c_ref: (q, i, 0))],
        out_specs=_BS((None, tr, cw), lambda q, i, c_ref: (q, i, 0)))
    return pl.pallas_call(
        body, name=name, grid_spec=grid_spec, out_shape=jax.ShapeDtypeStruct((4, r, cw), BF16),
        compiler_params=_params(("parallel", "parallel")),
    )(c.reshape(1).astype(jnp.int32), g.reshape(4, 2, r, cw), r1)


def _adamw_math(w, g, m, v):
    m = B1 * m + (1.0 - B1) * g
    v = B2 * v + (1.0 - B2) * (g * g)
    m_hat = m / (1.0 - B1 ** STEP)
    v_hat = v / (1.0 - B2 ** STEP)
    return -LR * (m_hat / (jnp.sqrt(v_hat) + AEPS) + WD * w), m, v


def _sum_adamw(parts, w, m, v, name):
    r, c = w.shape
    tr = _pick(r, (256, 176, 128))

    def body(p0, p1, p2, p3, w_ref, m_ref, v_ref, g_out, d_out, m_out, v_out):
        g = ((p0[...].astype(F32) + p1[...].astype(F32)) + p2[...].astype(F32)) + p3[...].astype(F32)
        g_out[...] = g
        d_out[...], m_out[...], v_out[...] = _adamw_math(w_ref[...], g, m_ref[...], v_ref[...])

    part = lambda q: _BS((None, tr, c), functools.partial(lambda i, q: (q, i, 0), q=q))
    plain = _BS((tr, c), lambda i: (i, 0))
    return pl.pallas_call(
        body, name=name, grid=(r // tr,), in_specs=[part(q) for q in range(4)] + [plain] * 3,
        out_specs=[plain] * 4, out_shape=[jax.ShapeDtypeStruct((r, c), F32)] * 4,
        compiler_params=_params(("parallel",)),
    )(parts, parts, parts, parts, w, m, v)


def _small_update(gathered, w, m, v):
    r = w.shape[0]

    def body(ga_ref, w_ref, m_ref, v_ref, g_out, d_out, m_out, v_out):
        g = ga_ref[0]
        for dev in range(1, N_DEV):
            g = g + ga_ref[dev]
        g_out[...] = g
        d_out[...], m_out[...], v_out[...] = _adamw_math(w_ref[...], g, m_ref[...], v_ref[...])

    return pl.pallas_call(
        body, name="small_update", out_shape=[jax.ShapeDtypeStruct((r, LANE), F32)] * 4,
    )(gathered, w, m, v)


def _pack_small(vals):
    rows = []
    for name, (r, n) in SMALL:
        flat = vals[name].reshape(-1)
        pad = (-flat.shape[0]) % (SUBLANE * LANE)
        rows.append(jnp.pad(flat, (0, pad)).reshape(-1, LANE))
    return jnp.concatenate(rows, axis=0)


def _unpack_small(packed):
    out, off = {}, 0
    for name, (r, n) in SMALL:
        nrow = -(-(r * n) // (SUBLANE * LANE)) * SUBLANE
        out[name] = packed[off:off + nrow].reshape(-1)[:r * n].reshape(r, n)
        off += nrow
    return out


def kernel(x, positions, ffn1_norm, ffn1_w_in, ffn1_w_out, mix_norm, w_in, hg_lb_table, hg_out_norm, w_hg_branch, mla_q_lora_norm, w_q_up, mla_kv_lora_norm, w_kv_up, q_head_norm, k_head_norm, w_mla_branch, w_merge, b_merge, w_out, ffn2_norm, ffn2_w_in, ffn2_w_out, final_norm, loss_target, m_ffn1_norm, m_ffn1_w_in, m_ffn1_w_out, m_mix_norm, m_w_in, m_hg_lb_table, m_hg_out_norm, m_w_hg_branch, m_mla_q_lora_norm, m_w_q_up, m_mla_kv_lora_norm, m_w_kv_up, m_q_head_norm, m_k_head_norm, m_w_mla_branch, m_w_merge, m_b_merge, m_w_out, m_ffn2_norm, m_ffn2_w_in, m_ffn2_w_out, m_final_norm, v_ffn1_norm, v_ffn1_w_in, v_ffn1_w_out, v_mix_norm, v_w_in, v_hg_lb_table, v_hg_out_norm, v_w_hg_branch, v_mla_q_lora_norm, v_w_q_up, v_mla_kv_lora_norm, v_w_kv_up, v_q_head_norm, v_k_head_norm, v_w_mla_branch, v_w_merge, v_b_merge, v_w_out, v_ffn2_norm, v_ffn2_w_in, v_ffn2_w_out, v_final_norm):
    args = dict(locals())
    t = x.shape[1]
    big_w = {n: args[n][0] for n, _, _ in BIG}
    small = {n: args[n].reshape(shape) for n, shape in SMALL}

    names = [n for n, _, _ in BIG]
    first, rest = names[:2], names[2:]
    full = dict(zip(first, _all_gather([big_w[n].astype(BF16) for n in first], "weights_all_gather_ffn1")))
    g_send, g_recv, g_thru, g_bufs, g_token = _gather_start([big_w[n].astype(BF16) for n in rest], full[first[0]],
                                                            "weights_gather_start")
    gains = dict(small, ffn1_norm=small["ffn1_norm"] + g_token[0:1, 0:1])

    def late_weights(after):
        blocks, bufs = _gather_wait(g_send, g_recv, g_thru, g_bufs, after, "weights_gather_wait")
        return dict(zip(rest, _gather_finish(blocks, bufs, "weights_gather_finish")))

    inv_freq = ROPE_THETA ** (-jnp.arange(0, ROPE, 2, dtype=F32) / ROPE)
    ang = positions[0].astype(F32)[:, None] * inv_freq
    cs = jnp.concatenate([jnp.cos(ang), jnp.cos(ang)], axis=1)
    sn = jnp.concatenate([jnp.sin(ang), jnp.sin(ang)], axis=1)

    c = lax.axis_index("c")
    chip = 2 * lax.axis_index("x") + lax.axis_index("y")
    early = {}

    def chip_sums_of(g, ns, tag):
        from_sibling = _sibling_swap([g[n] for n in ns], "grads_sibling_swap_" + tag)
        return [_chip_sum(g[n], r1, c, "chip_sum_" + n) for n, r1 in zip(ns, from_sibling)]

    def early_grads(g):
        early["names"] = [n for n in names if n in g]
        early["sums"] = chip_sums_of(g, early["names"], "early")
        early["send"], early["recv"], early["thru"], early["lands"], token = _chip_exchange_start(
            early["sums"], "grads_exchange_start")
        return token

    loss_row, grad_x, g = _local_step(x[0], loss_target[0], cs, sn, full, late_weights, gains, early_grads)
    late = [n for n in names if n not in early["names"]]
    exchanged = dict(zip(late, _chip_exchange(chip_sums_of(g, late, "late"), "grads_chip_exchange_late")))
    sent, landed = _chip_exchange_wait(early["send"], early["recv"], early["thru"], early["lands"],
                                       exchanged[late[0]], "grads_exchange_wait")
    for n, land, own in zip(early["names"], landed, sent):
        exchanged[n] = lax.dynamic_update_index_in_dim(land, lax.dynamic_index_in_dim(own, chip, 0), chip, 0)

    small_packed = jnp.concatenate([_pack_small(g), jnp.pad(loss_row, ((0, SUBLANE - 1), (0, 0)))], axis=0)
    small_all = _all_gather([small_packed], "small_all_gather")[0]
    zero_tail = jnp.zeros((SUBLANE, LANE), F32)
    pk = lambda d: jnp.concatenate([_pack_small(d), zero_tail], axis=0)
    sg, sd, sm, sv = _small_update(
        small_all, pk(small), pk({n: args["m_" + n].reshape(shape) for n, shape in SMALL}),
        pk({n: args["v_" + n].reshape(shape) for n, shape in SMALL}))
    n_small_rows = _pack_small(small).shape[0]
    loss = sg[n_small_rows, 0]
    outs = {k_: _unpack_small(a) for k_, a in (("grad", sg), ("delta", sd), ("new_m", sm), ("new_v", sv))}

    for n in names:
        outs["grad"][n], outs["delta"][n], outs["new_m"][n], outs["new_v"][n] = _sum_adamw(
            exchanged[n], big_w[n], args["m_" + n][0], args["v_" + n][0], "adamw_" + n)

    def shaped(kind, n):
        return outs[kind][n].reshape(args[n].shape)

    return (loss, grad_x[None], *[shaped("grad", n) for n in WEIGHT_ORDER], *[shaped("delta", n) for n in WEIGHT_ORDER],
            *[shaped("new_m", n) for n in WEIGHT_ORDER], *[shaped("new_v", n) for n in WEIGHT_ORDER])
```

```python
import functools

import jax
import jax.numpy as jnp
from jax import lax
from jax.experimental import pallas as pl
from jax.experimental.pallas import tpu as pltpu

F32 = jnp.float32
BF16 = jnp.bfloat16

D = 1024
FF = 2816
NH = 8
HD = 128
ROPE = 64
QK = HD + ROPE
QL = 384
KVL = 256
HGW = NH * HD
CHUNK = 64
EPS = 1e-6
ROPE_THETA = 10000.0
SCALE = QK ** -0.5

LR, B1, B2, AEPS, WD, STEP = 0.001, 0.9, 0.999, 1e-08, 0.01, 10

HB = 128
SUB = 16
EXP_CLAMP = 80.0
ATT_TILES = (512, 256, 128)
ATT_Q = (1024, 512, 256, 128)
ATT_KEY_TILES = 1
ROW_TILES = (1024, 512, 256, 128)

LANE = 128
SUBLANE = 8
VMEM_LIMIT = 56 << 20

N_DEV = 8
MESH = pl.DeviceIdType.MESH

BIG = (
    ("ffn1_w_in", (D, 2 * FF), 1), ("ffn1_w_out", (FF, D), 0), ("w_in", (D, 4800), 1),
    ("w_hg_branch", (HGW, D), 0), ("w_q_up", (QL, NH * QK), 1), ("w_kv_up", (KVL, NH * 2 * HD), 1),
    ("w_mla_branch", (NH * HD, D), 0), ("w_merge", (D, 2 * D), 1), ("w_out", (D, D), 0),
    ("ffn2_w_in", (D, 2 * FF), 1), ("ffn2_w_out", (FF, D), 0),
)
SMALL = (
    ("ffn1_norm", (1, D)), ("mix_norm", (1, D)), ("hg_lb_table", (2, HGW)), ("hg_out_norm", (1, HD)),
    ("mla_q_lora_norm", (1, QL)), ("mla_kv_lora_norm", (1, KVL)), ("q_head_norm", (1, QK)),
    ("k_head_norm", (1, QK)), ("b_merge", (1, 2 * D)), ("ffn2_norm", (1, D)), ("final_norm", (1, D)),
)
WEIGHT_ORDER = ("ffn1_norm", "ffn1_w_in", "ffn1_w_out", "mix_norm", "w_in", "hg_lb_table", "hg_out_norm",
                "w_hg_branch", "mla_q_lora_norm", "w_q_up", "mla_kv_lora_norm", "w_kv_up", "q_head_norm",
                "k_head_norm", "w_mla_branch", "w_merge", "b_merge", "w_out", "ffn2_norm", "ffn2_w_in",
                "ffn2_w_out", "final_norm")


def _pick(n, cands):
    for c in cands:
        if n % c == 0:
            return c
    return n


def _params(sem):
    return pltpu.CompilerParams(dimension_semantics=sem, vmem_limit_bytes=VMEM_LIMIT)


def _sig(x):
    return 1.0 / (1.0 + jnp.exp(-x))


def _dot(a, b):
    return jnp.dot(a.astype(BF16), b.astype(BF16), preferred_element_type=F32)


def _dot_nt(a, b):
    return lax.dot_general(a.astype(BF16), b.astype(BF16), (((1,), (1,)), ((), ())),
                           preferred_element_type=F32)


def _dot_tn(a, b):
    return lax.dot_general(a.astype(BF16), b.astype(BF16), (((0,), (0,)), ((), ())),
                           preferred_element_type=F32)


def _split3(x):
    x1 = x.astype(BF16)
    r1 = x - x1.astype(F32)
    x2 = r1.astype(BF16)
    x3 = (r1 - x2.astype(F32)).astype(BF16)
    return x1, x2, x3


def _dot_sel(m, x):
    x1, x2, x3 = _split3(x)
    d = lambda p: jnp.dot(m, p, preferred_element_type=F32)
    return d(x1) + d(x2) + d(x3)


def _sel_dot(x, m):
    x1, x2, x3 = _split3(x)
    d = lambda p: jnp.dot(p, m, preferred_element_type=F32)
    return d(x1) + d(x2) + d(x3)


_TN = (1408, 1024, 768, 512, 384, 256, 128)


def _mm(a, b, mode, name, out_dtype=F32):
    if mode == "tn":
        t, m = a.shape
        n = b.shape[1]
        tt, tm, tn = _pick(t, (512, 256, 128)), _pick(m, _TN), _pick(n, _TN)

        def body(a_ref, b_ref, o_ref):
            @pl.when(pl.program_id(2) == 0)
            def _():
                o_ref[...] = jnp.zeros_like(o_ref)

            o_ref[...] += _dot_tn(a_ref[...], b_ref[...])

        return pl.pallas_call(
            body, name=name, grid=(m // tm, n // tn, t // tt),
            in_specs=[pl.BlockSpec((tt, tm), lambda i, j, k: (k, i)),
                      pl.BlockSpec((tt, tn), lambda i, j, k: (k, j))],
            out_specs=pl.BlockSpec((tm, tn), lambda i, j, k: (i, j)),
            out_shape=jax.ShapeDtypeStruct((m, n), F32),
            compiler_params=_params(("parallel", "parallel", "arbitrary")),
        )(a, b)

    m, k = a.shape
    tm = _pick(m, ROW_TILES)
    if mode == "nn":
        n = b.shape[1]
        tn = _pick(n, _TN)
        b_spec = pl.BlockSpec((k, tn), lambda i, j: (0, j))
        dot = _dot
    else:
        n = b.shape[0]
        tn = _pick(n, _TN if k <= 4096 else (512, 256, 128))
        b_spec = pl.BlockSpec((tn, k), lambda i, j: (j, 0))
        dot = _dot_nt

    def body(a_ref, b_ref, o_ref):
        o_ref[...] = dot(a_ref[...], b_ref[...]).astype(o_ref.dtype)

    return pl.pallas_call(
        body, name=name, grid=(m // tm, n // tn),
        in_specs=[pl.BlockSpec((tm, k), lambda i, j: (i, 0)), b_spec],
        out_specs=pl.BlockSpec((tm, tn), lambda i, j: (i, j)),
        out_shape=jax.ShapeDtypeStruct((m, n), out_dtype),
        compiler_params=_params(("parallel", "parallel")),
    )(a, b)


_DOTS = {"nn": _dot, "nt": _dot_nt, "tn": _dot_tn}
_BS = pl.BlockSpec


def _mmcall(name, kind, a, b, a_spec, b_spec, o_spec, o_shape, grid, red_axis=None, out_dtype=F32):
    dot = _DOTS[kind]

    def body(a_ref, b_ref, o_ref):
        if red_axis is None:
            o_ref[...] = dot(a_ref[...], b_ref[...]).astype(o_ref.dtype)
        else:
            @pl.when(pl.program_id(red_axis) == 0)
            def _():
                o_ref[...] = jnp.zeros_like(o_ref)

            o_ref[...] += dot(a_ref[...], b_ref[...])

    sem = tuple("arbitrary" if ax == red_axis else "parallel" for ax in range(len(grid)))
    return pl.pallas_call(
        body, name=name, grid=grid, in_specs=[a_spec, b_spec], out_specs=o_spec,
        out_shape=jax.ShapeDtypeStruct(o_shape, out_dtype), compiler_params=_params(sem),
    )(a, b)


def _mm_stack_red(a, w, name, kind):
    s, t, n = a.shape
    nout = w.shape[2] if kind == "nn" else w.shape[1]
    tm = _pick(t, ROW_TILES)
    return _mmcall(name, kind, a, w, _BS((None, tm, n), lambda i, j: (j, i, 0)),
                   _BS((None,) + w.shape[1:], lambda i, j: (j, 0, 0)), _BS((tm, nout), lambda i, j: (i, 0)),
                   (t, nout), (t // tm, s), red_axis=1)


def _mm_stack_tn(a, b, name):
    grp = 4
    if a.ndim == 2:
        t, k = a.shape
        s, _, n = b.shape
        tt = _pick(t, (512, 256, 128))
        a_spec = _BS((tt, k), lambda j, r: (r, 0))
        b_spec, b_in = _BS((None, grp, tt, n), lambda j, r: (j, 0, r, 0)), b.reshape(s // grp, grp, t, n)
        a_in = a
    else:
        s, t, k = a.shape
        n = b.shape[1]
        tt = _pick(t, (512, 256, 128))
        a_spec, a_in = _BS((None, grp, tt, k), lambda j, r: (j, 0, r, 0)), a.reshape(s // grp, grp, t, k)
        b_spec, b_in = _BS((tt, n), lambda j, r: (r, 0)), b

    def body(a_ref, b_ref, o_ref):
        @pl.when(pl.program_id(1) == 0)
        def _():
            o_ref[...] = jnp.zeros_like(o_ref)

        shared = a_ref[...] if a.ndim == 2 else b_ref[...]
        for e in range(grp):
            o_ref[e] += _dot_tn(shared, b_ref[e]) if a.ndim == 2 else _dot_tn(a_ref[e], shared)

    return pl.pallas_call(
        body, name=name, grid=(s // grp, t // tt), in_specs=[a_spec, b_spec],
        out_specs=_BS((None, grp, k, n), lambda j, r: (j, 0, 0, 0)),
        out_shape=jax.ShapeDtypeStruct((s // grp, grp, k, n), F32),
        compiler_params=_params(("parallel", "arbitrary")),
    )(a_in, b_in).reshape(s, k, n)


def _cols_fwd(x, w, name):
    t, k = x.shape
    s, _, n = w.shape
    tm = _pick(t, ROW_TILES)

    def body(x_ref, w_ref, o_ref):
        x_ = x_ref[...]
        for j in range(s):
            o_ref[:, n * j:n * (j + 1)] = _dot(x_, w_ref[j])

    return pl.pallas_call(
        body, name=name, grid=(t // tm,),
        in_specs=[_BS((tm, k), lambda i: (i, 0)), _BS((s, k, n), lambda i: (0, 0, 0))],
        out_specs=_BS((tm, s * n), lambda i: (i, 0)), out_shape=jax.ShapeDtypeStruct((t, s * n), F32),
        compiler_params=_params(("parallel",)),
    )(x, w)


def _cols_dx(d, w, name):
    t = d.shape[0]
    s, k, n = w.shape
    tm = _pick(t, ROW_TILES)

    def body(d_ref, w_ref, o_ref):
        acc = _dot_nt(d_ref[:, 0:n], w_ref[0])
        for j in range(1, s):
            acc = acc + _dot_nt(d_ref[:, n * j:n * (j + 1)], w_ref[j])
        o_ref[...] = acc

    return pl.pallas_call(
        body, name=name, grid=(t // tm,),
        in_specs=[_BS((tm, s * n), lambda i: (i, 0)), _BS((s, k, n), lambda i: (0, 0, 0))],
        out_specs=_BS((tm, k), lambda i: (i, 0)), out_shape=jax.ShapeDtypeStruct((t, k), F32),
        compiler_params=_params(("parallel",)),
    )(d, w)


def _cols_dw(x, d, n, name):
    t, k = x.shape
    s = d.shape[1] // n
    tt = _pick(t, (512, 256, 128))

    def body(x_ref, d_ref, o_ref):
        @pl.when(pl.program_id(0) == 0)
        def _():
            o_ref[...] = jnp.zeros_like(o_ref)

        x_ = x_ref[...]
        for j in range(s):
            o_ref[j] += _dot_tn(x_, d_ref[:, n * j:n * (j + 1)])

    return pl.pallas_call(
        body, name=name, grid=(t // tt,),
        in_specs=[_BS((tt, k), lambda r: (r, 0)), _BS((tt, s * n), lambda r: (r, 0))],
        out_specs=_BS((s, k, n), lambda r: (0, 0, 0)), out_shape=jax.ShapeDtypeStruct((s, k, n), F32),
        compiler_params=_params(("arbitrary",)),
    )(x, d)


def _rows(fn, name, t, tm, ins, vecs, outs, accs=()):
    n_in, n_out, n_acc = len(ins) + len(vecs), len(outs), len(accs)

    def body(*refs):
        res = fn(*refs[:n_in + n_out])
        if n_acc:
            acc_refs = refs[n_in + n_out:]

            @pl.when(pl.program_id(0) == 0)
            def _():
                for r in acc_refs:
                    r[...] = jnp.zeros_like(r)

            for r, val in zip(acc_refs, res):
                r[...] += val

    in_specs = [pl.BlockSpec((tm, bw), functools.partial(lambda i, cb: (i, cb), cb=cb)) for _, bw, cb in ins]
    in_specs += [pl.BlockSpec(v.shape, lambda i: (0, 0)) for v in vecs]
    out_specs = [pl.BlockSpec((tm, w), lambda i: (i, 0)) for w, _ in outs]
    out_specs += [pl.BlockSpec(s, lambda i: (0, 0)) for s in accs]
    out_shape = [jax.ShapeDtypeStruct((t, w), dt) for w, dt in outs]
    out_shape += [jax.ShapeDtypeStruct(s, F32) for s in accs]
    return pl.pallas_call(
        body, name=name, grid=(t // tm,), in_specs=in_specs, out_specs=out_specs, out_shape=out_shape,
        compiler_params=_params(("arbitrary",) if n_acc else ("parallel",)),
    )(*[a for a, _, _ in ins], *vecs)


def _rms(x, g):
    return x * lax.rsqrt(jnp.mean(x * x, axis=-1, keepdims=True) + EPS) * g


def _rms_bwd(x, g, dy):
    xh = x * lax.rsqrt(jnp.mean(x * x, axis=-1, keepdims=True) + EPS)
    r = lax.rsqrt(jnp.mean(x * x, axis=-1, keepdims=True) + EPS)
    dyg = dy * g
    dx = r * (dyg - xh * jnp.mean(dyg * xh, axis=-1, keepdims=True))
    return dx, jnp.sum(dy * xh, axis=0, keepdims=True)


def _hgrn_mats():
    row = lax.broadcasted_iota(jnp.int32, (HB, HB), 0)
    col = lax.broadcasted_iota(jnp.int32, (HB, HB), 1)
    return row, col


def _hgrn_gates(qr, z, t0, t1):
    lb = 1.0 / (1.0 + jnp.exp(t1 - t0))
    sz = _sig(z)
    sneg = 1.0 / (1.0 + jnp.exp(z))
    f = lb + (1.0 - lb) * sz
    return lb, sz, sneg, f, jnp.log(f), (1.0 - lb) * sneg, qr * _sig(qr)


def _hgrn_scores(q, k, cum):
    mids = [cum[SUB * i + SUB // 2 - 1:SUB * i + SUB // 2, :] for i in range(HB // SUB)]
    cmid = jnp.concatenate([cum[SUB * i:SUB * (i + 1)] - mids[i] for i in range(HB // SUB)], axis=0)
    qd = q * jnp.exp(jnp.minimum(cmid, EXP_CLAMP))
    qd_b = qd.astype(BF16)
    kds, parts = [], []
    for i in range(HB // SUB):
        kd = k * jnp.exp(jnp.minimum(mids[i] - cum, EXP_CLAMP))
        kds.append(kd)
        parts.append(_dot_nt(qd_b[SUB * i:SUB * (i + 1)], kd))
    return qd, kds, jnp.concatenate(parts, axis=0), cmid


def _hgrn_fwd(p_hg, table, gain, t):
    nblk = t // HB

    def body(q_ref, f_ref, i_ref, g_ref, tab_ref, gain_ref, o_ref, y_ref, st_ref, state):
        @pl.when(pl.program_id(0) == 0)
        def _():
            state[...] = jnp.zeros_like(state)

        row, col = _hgrn_mats()
        causal = col <= row
        tri = causal.astype(BF16)
        for h in range(NH):
            sl = slice(HD * h, HD * (h + 1))
            v, gr = i_ref[:, sl], g_ref[:, sl]
            _, _, _, _, lf, k, q = _hgrn_gates(q_ref[:, sl], f_ref[:, sl], tab_ref[0:1, sl], tab_ref[1:2, sl])
            cum = _dot_sel(tri, lf)
            _, _, s, _ = _hgrn_scores(q, k, cum)
            p = jnp.where(causal, s, 0.0)
            st = state[h]
            st_ref[h, 0] = st
            o = _dot(p, v) + _dot_nt(q * jnp.exp(cum), st)
            last = cum[HB - 1:HB, :]
            state[h] = st * jnp.exp(last) + _dot_tn(v, k * jnp.exp(last - cum))
            o_ref[:, sl] = o
            y_ref[:, sl] = (_rms(o, gain_ref[...]) * gr * _sig(gr)).astype(BF16)

    blk = lambda cb: pl.BlockSpec((HB, HGW), functools.partial(lambda n, cb: (n, cb), cb=cb))
    return pl.pallas_call(
        body, name="hgrn_fwd", grid=(nblk,),
        in_specs=[blk(0), blk(1), blk(2), blk(3), pl.BlockSpec((2, HGW), lambda n: (0, 0)),
                  pl.BlockSpec((1, HD), lambda n: (0, 0))],
        out_specs=[pl.BlockSpec((HB, HGW), lambda n: (n, 0)), pl.BlockSpec((HB, HGW), lambda n: (n, 0)),
                   pl.BlockSpec((NH, 1, HD, HD), lambda n: (0, n, 0, 0))],
        out_shape=[jax.ShapeDtypeStruct((t, HGW), F32), jax.ShapeDtypeStruct((t, HGW), BF16),
                   jax.ShapeDtypeStruct((NH, nblk, HD, HD), F32)],
        scratch_shapes=[pltpu.VMEM((NH, HD, HD), F32)],
        compiler_params=_params(("arbitrary",)),
    )(p_hg, p_hg, p_hg, p_hg, table, gain)


def _hgrn_bwd(p_hg, table, gain, o_pre, states, dy, t):
    nblk = t // HB

    def body(q_ref, f_ref, i_ref, g_ref, tab_ref, gain_ref, o_ref, st_ref, dy_ref, dp_ref, dtab_ref, dgain_ref,
             dstate):
        @pl.when(pl.program_id(0) == 0)
        def _():
            dstate[...] = jnp.zeros_like(dstate)
            dtab_ref[...] = jnp.zeros_like(dtab_ref)
            dgain_ref[...] = jnp.zeros_like(dgain_ref)

        row, col = _hgrn_mats()
        causal = col <= row
        tri = causal.astype(BF16)
        tri_t = (row <= col).astype(BF16)
        dgain = jnp.zeros((1, HD), F32)
        for h in range(NH):
            sl = slice(HD * h, HD * (h + 1))
            qr, z, v, gr = q_ref[:, sl], f_ref[:, sl], i_ref[:, sl], g_ref[:, sl]
            lb, sz, sneg, f, lf, k, q = _hgrn_gates(qr, z, tab_ref[0:1, sl], tab_ref[1:2, sl])
            cum = _dot_sel(tri, lf)
            qd, kds, s, cmid = _hgrn_scores(q, k, cum)
            p = jnp.where(causal, s, 0.0)
            st = st_ref[h, 0]
            dst = dstate[h]
            o = o_ref[:, sl]
            sg = _sig(gr)
            dyh = dy_ref[:, sl]
            on = _rms(o, gain_ref[...])
            dgr = dyh * on * sg * (1.0 + gr * (1.0 - sg))
            do, dg_h = _rms_bwd(o, gain_ref[...], dyh * gr * sg)
            dgain = dgain + dg_h
            do_b = do.astype(BF16)
            ecum = jnp.exp(cum)
            qc = q * ecum
            last = cum[HB - 1:HB, :]
            edec = jnp.exp(last - cum)
            kdec = k * edec
            dp = jnp.where(causal, _dot_nt(do_b, v), 0.0)
            dv = _dot(p.T, do_b) + _dot_nt(kdec, dst)
            dqc = _dot(do_b, st)
            dkdec = _dot(v, dst)
            dstate[h] = dst * jnp.exp(last) + _dot(do.T, qc)
            dp_b = dp.astype(BF16)
            dqd = jnp.concatenate([_dot(dp_b[SUB * i:SUB * (i + 1)], kds[i]) for i in range(HB // SUB)], axis=0)
            gq = dqd * qd
            dq = dqd * jnp.exp(jnp.minimum(cmid, EXP_CLAMP)) + dqc * ecum
            gs = dkdec * kdec
            dk = dkdec * edec
            dcum = dqc * qc - gs + gq
            dcum = dcum + jnp.where(row == HB - 1, jnp.sum(gs, axis=0, keepdims=True)
                                    + jnp.exp(last) * jnp.sum(st * dst, axis=0, keepdims=True), 0.0)
            qd_b = qd.astype(BF16)
            for i in range(HB // SUB):
                dkd = _dot_tn(dp_b[SUB * i:SUB * (i + 1)], qd_b[SUB * i:SUB * (i + 1)])
                mid = cum[SUB * i + SUB // 2 - 1:SUB * i + SUB // 2, :]
                dk = dk + dkd * jnp.exp(jnp.minimum(mid - cum, EXP_CLAMP))
                gk = dkd * kds[i]
                to_mid = jnp.sum(gk, axis=0, keepdims=True) - jnp.sum(gq[SUB * i:SUB * (i + 1)], axis=0, keepdims=True)
                dcum = dcum - gk + jnp.where(row == SUB * i + SUB // 2 - 1, to_mid, 0.0)
            dlf = _dot_sel(tri_t, dcum)
            df = dlf / f - dk
            dz = df * (1.0 - lb) * sz * sneg
            dlb = jnp.sum(df * sneg, axis=0, keepdims=True) * lb * (1.0 - lb)
            dtab_ref[0:1, sl] += dlb
            dtab_ref[1:2, sl] -= dlb
            sq = _sig(qr)
            dp_ref[:, sl] = (dq * sq * (1.0 + qr * (1.0 - sq))).astype(BF16)
            dp_ref[:, HGW + HD * h:HGW + HD * (h + 1)] = dz.astype(BF16)
            dp_ref[:, 2 * HGW + HD * h:2 * HGW + HD * (h + 1)] = dv.astype(BF16)
            dp_ref[:, 3 * HGW + HD * h:3 * HGW + HD * (h + 1)] = dgr.astype(BF16)
        dgain_ref[...] += dgain

    rev = lambda cb: pl.BlockSpec((HB, HGW), functools.partial(lambda n, cb: (nblk - 1 - n, cb), cb=cb))
    return pl.pallas_call(
        body, name="hgrn_bwd", grid=(nblk,),
        in_specs=[rev(0), rev(1), rev(2), rev(3), pl.BlockSpec((2, HGW), lambda n: (0, 0)),
                  pl.BlockSpec((1, HD), lambda n: (0, 0)), rev(0),
                  pl.BlockSpec((NH, 1, HD, HD), lambda n: (0, nblk - 1 - n, 0, 0)), rev(0)],
        out_specs=[pl.BlockSpec((HB, 4 * HGW), lambda n: (nblk - 1 - n, 0)),
                   pl.BlockSpec((2, HGW), lambda n: (0, 0)), pl.BlockSpec((1, HD), lambda n: (0, 0))],
        out_shape=[jax.ShapeDtypeStruct((t, 4 * HGW), BF16), jax.ShapeDtypeStruct((2, HGW), F32),
                   jax.ShapeDtypeStruct((1, HD), F32)],
        scratch_shapes=[pltpu.VMEM((NH, HD, HD), F32)],
        compiler_params=_params(("arbitrary",)),
    )(p_hg, p_hg, p_hg, p_hg, table, gain, o_pre, states, dy)


def _rope_mat():
    r = lax.broadcasted_iota(jnp.int32, (ROPE, ROPE), 0)
    c = lax.broadcasted_iota(jnp.int32, (ROPE, ROPE), 1)
    half = ROPE // 2
    return ((r == c - half).astype(F32) - (r == c + half).astype(F32)).astype(BF16)


def _mla_prep_fwd(p_mla, cs, sn, wq, wkv, gql, gkvl, gq, gk, t):
    tm = _pick(t, (512, 256, 128))

    def body(p_ref, cs_ref, sn_ref, wq_ref, wkv_ref, gql_ref, gkvl_ref, gq_ref, gk_ref,
             q_ref, k_ref, v_ref):
        rmat = _rope_mat()
        cqn = _rms(p_ref[:, 0:QL], gql_ref[...]).astype(BF16)
        ckvn = _rms(p_ref[:, QL:QL + KVL], gkvl_ref[...]).astype(BF16)
        kpe = p_ref[:, QL + KVL:QL + KVL + ROPE]
        c, s = cs_ref[...], sn_ref[...]
        rot = lambda x: x * c + _sel_dot(x, rmat) * s
        for h in range(NH):
            qa, qr = _dot(cqn, wq_ref[h, :, 0:HD]), _dot(cqn, wq_ref[h, :, HD:QK])
            rq = lax.rsqrt((jnp.sum(qa * qa, -1, keepdims=True) + jnp.sum(qr * qr, -1, keepdims=True)) / QK + EPS)
            q_ref[h, :, 0:HD] = (qa * rq * gq_ref[:, 0:HD] * SCALE).astype(BF16)
            q_ref[h, :, HD:QK] = (rot(qr * rq * gq_ref[:, HD:QK]) * SCALE).astype(BF16)
            kn = _dot(ckvn, wkv_ref[h, :, 0:HD])
            rk = lax.rsqrt((jnp.sum(kn * kn, -1, keepdims=True) + jnp.sum(kpe * kpe, -1, keepdims=True)) / QK + EPS)
            k_ref[h, :, 0:HD] = (kn * rk * gk_ref[:, 0:HD]).astype(BF16)
            k_ref[h, :, HD:QK] = rot(kpe * rk * gk_ref[:, HD:QK]).astype(BF16)
            v_ref[h] = _dot(ckvn, wkv_ref[h, :, HD:2 * HD]).astype(BF16)

    whole = lambda a: pl.BlockSpec(a.shape, functools.partial(lambda i, nd: (0,) * nd, nd=a.ndim))
    return pl.pallas_call(
        body, name="mla_prep_fwd", grid=(t // tm,),
        in_specs=[pl.BlockSpec((tm, QL + KVL + ROPE), lambda i: (i, 0)), pl.BlockSpec((tm, ROPE), lambda i: (i, 0)),
                  pl.BlockSpec((tm, ROPE), lambda i: (i, 0))] + [whole(a) for a in (wq, wkv, gql, gkvl, gq, gk)],
        out_specs=[pl.BlockSpec((NH, tm, QK), lambda i: (0, i, 0)), pl.BlockSpec((NH, tm, QK), lambda i: (0, i, 0)),
                   pl.BlockSpec((NH, tm, HD), lambda i: (0, i, 0))],
        out_shape=[jax.ShapeDtypeStruct((NH, t, QK), BF16), jax.ShapeDtypeStruct((NH, t, QK), BF16),
                   jax.ShapeDtypeStruct((NH, t, HD), BF16)],
        compiler_params=_params(("parallel",)),
    )(p_mla, cs, sn, wq, wkv, gql, gkvl, gq, gk)


def _mla_prep_bwd(p_mla, cs, sn, wq, wkv, gql, gkvl, gq, gk, dq, dk, dv, t):
    tm = _pick(t, (512, 256, 128))

    def body(p_ref, cs_ref, sn_ref, wq_ref, wkv_ref, gql_ref, gkvl_ref, gq_ref, gk_ref,
             dq_ref, dk_ref, dv_ref,
             dp_ref, dwq_ref, dwkv_ref, dgql_ref, dgkvl_ref, dgq_ref, dgk_ref):
        accs = (dwq_ref, dwkv_ref, dgql_ref, dgkvl_ref, dgq_ref, dgk_ref)

        @pl.when(pl.program_id(0) == 0)
        def _():
            for r in accs:
                r[...] = jnp.zeros_like(r)

        rmat = _rope_mat()
        rmat_t = -rmat
        cq, ckv = p_ref[:, 0:QL], p_ref[:, QL:QL + KVL]
        kpe = p_ref[:, QL + KVL:QL + KVL + ROPE]
        cqn_f, ckvn_f = _rms(cq, gql_ref[...]), _rms(ckv, gkvl_ref[...])
        cqn, ckvn = cqn_f.astype(BF16), ckvn_f.astype(BF16)
        cqn_t, ckvn_t = cqn_f.T.astype(BF16), ckvn_f.T.astype(BF16)
        c, s = cs_ref[...], sn_ref[...]
        unrot = lambda dy: dy * c + _sel_dot(dy * s, rmat_t)
        dcqn = jnp.zeros((tm, QL), F32)
        dckvn = jnp.zeros((tm, KVL), F32)
        dkpe = jnp.zeros((tm, ROPE), F32)
        dgq_a, dgq_r = jnp.zeros((1, HD), F32), jnp.zeros((1, ROPE), F32)
        dgk_a, dgk_r = jnp.zeros((1, HD), F32), jnp.zeros((1, ROPE), F32)
        for h in range(NH):
            qa, qr = _dot(cqn, wq_ref[h, :, 0:HD]), _dot(cqn, wq_ref[h, :, HD:QK])
            rq = lax.rsqrt((jnp.sum(qa * qa, -1, keepdims=True) + jnp.sum(qr * qr, -1, keepdims=True)) / QK + EPS)
            xa, xr = qa * rq, qr * rq
            dya = dq_ref[h, :, 0:HD] * SCALE
            dyr = unrot(dq_ref[h, :, HD:QK] * SCALE)
            dgq_a += jnp.sum(dya * xa, axis=0, keepdims=True)
            dgq_r += jnp.sum(dyr * xr, axis=0, keepdims=True)
            ga, gr_ = dya * gq_ref[:, 0:HD], dyr * gq_ref[:, HD:QK]
            mean = (jnp.sum(ga * xa, -1, keepdims=True) + jnp.sum(gr_ * xr, -1, keepdims=True)) / QK
            dqa = (rq * (ga - xa * mean)).astype(BF16)
            dqr = (rq * (gr_ - xr * mean)).astype(BF16)
            dwq_ref[h, :, 0:HD] += _dot(cqn_t, dqa)
            dwq_ref[h, :, HD:QK] += _dot(cqn_t, dqr)
            dcqn += _dot_nt(dqa, wq_ref[h, :, 0:HD]) + _dot_nt(dqr, wq_ref[h, :, HD:QK])
            kn = _dot(ckvn, wkv_ref[h, :, 0:HD])
            rk = lax.rsqrt((jnp.sum(kn * kn, -1, keepdims=True) + jnp.sum(kpe * kpe, -1, keepdims=True)) / QK + EPS)
            ya, yr = kn * rk, kpe * rk
            dka = dk_ref[h, :, 0:HD]
            dkr = unrot(dk_ref[h, :, HD:QK])
            dgk_a += jnp.sum(dka * ya, axis=0, keepdims=True)
            dgk_r += jnp.sum(dkr * yr, axis=0, keepdims=True)
            ha, hr = dka * gk_ref[:, 0:HD], dkr * gk_ref[:, HD:QK]
            mean = (jnp.sum(ha * ya, -1, keepdims=True) + jnp.sum(hr * yr, -1, keepdims=True)) / QK
            dkn = (rk * (ha - ya * mean)).astype(BF16)
            dkpe += rk * (hr - yr * mean)
            dvh = dv_ref[h].astype(BF16)
            dwkv_ref[h, :, 0:HD] += _dot(ckvn_t, dkn)
            dwkv_ref[h, :, HD:2 * HD] += _dot(ckvn_t, dvh)
            dckvn += _dot_nt(dkn, wkv_ref[h, :, 0:HD]) + _dot_nt(dvh, wkv_ref[h, :, HD:2 * HD])
        dcq, dg1 = _rms_bwd(cq, gql_ref[...], dcqn)
        dckv, dg2 = _rms_bwd(ckv, gkvl_ref[...], dckvn)
        dp_ref[:, 0:QL] = dcq.astype(BF16)
        dp_ref[:, QL:QL + KVL] = dckv.astype(BF16)
        dp_ref[:, QL + KVL:QL + KVL + ROPE] = dkpe.astype(BF16)
        dgql_ref[...] += dg1
        dgkvl_ref[...] += dg2
        dgq_ref[:, 0:HD] += dgq_a
        dgq_ref[:, HD:QK] += dgq_r
        dgk_ref[:, 0:HD] += dgk_a
        dgk_ref[:, HD:QK] += dgk_r

    whole = lambda a: pl.BlockSpec(a.shape, functools.partial(lambda i, nd: (0,) * nd, nd=a.ndim))
    acc_shapes = [wq.shape, wkv.shape, gql.shape, gkvl.shape, gq.shape, gk.shape]
    return pl.pallas_call(
        body, name="mla_prep_bwd", grid=(t // tm,),
        in_specs=[pl.BlockSpec((tm, QL + KVL + ROPE), lambda i: (i, 0)), pl.BlockSpec((tm, ROPE), lambda i: (i, 0)),
                  pl.BlockSpec((tm, ROPE), lambda i: (i, 0))]
        + [whole(a) for a in (wq, wkv, gql, gkvl, gq, gk)]
        + [pl.BlockSpec((NH, tm, QK), lambda i: (0, i, 0)), pl.BlockSpec((NH, tm, QK), lambda i: (0, i, 0)),
           pl.BlockSpec((NH, tm, HD), lambda i: (0, i, 0))],
        out_specs=[pl.BlockSpec((tm, QL + KVL + ROPE), lambda i: (i, 0))]
        + [pl.BlockSpec(s, functools.partial(lambda i, nd: (0,) * nd, nd=len(s))) for s in acc_shapes],
        out_shape=[jax.ShapeDtypeStruct((t, QL + KVL + ROPE), BF16)]
        + [jax.ShapeDtypeStruct(s, F32) for s in acc_shapes],
        compiler_params=_params(("arbitrary",)),
    )(p_mla, cs, sn, wq, wkv, gql, gkvl, gq, gk, dq, dk, dv)


def _chunk_mask(nq, nk, key0, keys_on_rows):
    shape = (nk, nq) if keys_on_rows else (nq, nk)
    qi = lax.broadcasted_iota(jnp.int32, shape, 1 if keys_on_rows else 0) // CHUNK
    ki = lax.broadcasted_iota(jnp.int32, shape, 0 if keys_on_rows else 1) // CHUNK + key0 // CHUNK
    return ki <= qi


def _flash_fwd(q, k, v, t):
    tq = _pick(t, ATT_Q)
    tk = tq // ATT_KEY_TILES

    def body(q_ref, k_ref, v_ref, o_ref, lse_ref):
        i = pl.program_id(1)
        qt = q_ref[0]

        def step(j, carry, key0):
            m, l, acc = carry
            cols = pl.ds(pl.multiple_of(j * tk, tk), tk)
            s = _dot_nt(qt, k_ref[0, cols, :])
            if key0 is not None:
                s = jnp.where(_chunk_mask(tq, tk, key0, False), s, -jnp.inf)
            m_new = jnp.maximum(m, jnp.max(s, axis=-1, keepdims=True))
            p = jnp.exp(s - m_new)
            alpha = jnp.exp(m - m_new)
            return m_new, alpha * l + jnp.sum(p, axis=-1, keepdims=True), alpha * acc + _dot(p, v_ref[0, cols, :])

        init = (jnp.full((tq, 1), -jnp.inf, F32), jnp.zeros((tq, 1), F32), jnp.zeros((tq, HD), F32))
        carry = lax.fori_loop(0, ATT_KEY_TILES * i, lambda j, cr: step(j, cr, None), init)
        for h in range(ATT_KEY_TILES):
            carry = step(ATT_KEY_TILES * i + h, carry, h * tk)
        m, l, acc = carry
        o_ref[...] = (acc / l).astype(BF16)
        lse_ref[0] = m + jnp.log(l)

    return pl.pallas_call(
        body, name="flash_fwd", grid=(NH, t // tq),
        in_specs=[pl.BlockSpec((1, tq, QK), lambda h, i: (h, i, 0)), pl.BlockSpec((1, t, QK), lambda h, i: (h, 0, 0)),
                  pl.BlockSpec((1, t, HD), lambda h, i: (h, 0, 0))],
        out_specs=[pl.BlockSpec((tq, HD), lambda h, i: (i, h)), pl.BlockSpec((1, tq, 1), lambda h, i: (h, i, 0))],
        out_shape=[jax.ShapeDtypeStruct((t, NH * HD), BF16), jax.ShapeDtypeStruct((NH, t, 1), F32)],
        compiler_params=_params(("parallel", "parallel")),
    )(q, k, v)


def _flash_delta(o, do, t):
    tq = _pick(t, ATT_TILES)

    def body(o_ref, do_ref, delta_ref):
        prod = do_ref[...].astype(F32) * o_ref[...].astype(F32)
        x1, x2, x3 = _split3(prod)
        ones = jnp.ones((8, HD), BF16)
        d = lambda p: lax.dot_general(ones, p, (((1,), (1,)), ((), ())), preferred_element_type=F32)
        delta_ref[0] = (d(x1) + d(x2) + d(x3))[0:1, :]

    return pl.pallas_call(
        body, name="flash_delta", grid=(NH, t // tq),
        in_specs=[pl.BlockSpec((tq, HD), lambda h, i: (i, h)), pl.BlockSpec((tq, HD), lambda h, i: (i, h))],
        out_specs=pl.BlockSpec((1, 1, tq), lambda h, i: (h, 0, i)),
        out_shape=jax.ShapeDtypeStruct((NH, 1, t), F32), compiler_params=_params(("parallel", "parallel")),
    )(o, do)


def _flash_bwd(q, k, v, do, lse_row, delta_row, t):
    tq = _pick(t, ATT_Q)
    tk = tq // ATT_KEY_TILES

    def body(q_ref, k_ref, v_ref, do_ref, lse_ref, delta_ref, dq_ref, dk_ref, dv_ref):
        j = pl.program_id(1)

        @pl.when(j == 0)
        def _():
            dq_ref[...] = jnp.zeros_like(dq_ref)

        kt, vt = k_ref[0], v_ref[0]

        def step(i, carry, key0):
            dk, dv = carry
            rows = pl.ds(pl.multiple_of(i * tq, tq), tq)
            qt, dot_ = q_ref[0, rows, :], do_ref[rows, :]
            p = jnp.exp(_dot_nt(kt, qt) - lse_ref[0, :, rows])
            if key0 is not None:
                p = jnp.where(_chunk_mask(tq, tk, key0, True), p, 0.0)
            ds = (p * (_dot_nt(vt, dot_) - delta_ref[0, :, rows])).astype(BF16)
            dq_ref[0, rows, :] += _dot_tn(ds, kt)
            return dk + _dot(ds, qt), dv + _dot(p, dot_)

        own = j // ATT_KEY_TILES
        carry = step(own, (jnp.zeros((tk, QK), F32), jnp.zeros((tk, HD), F32)), (j % ATT_KEY_TILES) * tk)
        dk, dv = lax.fori_loop(own + 1, t // tq, lambda i, cr: step(i, cr, None), carry)
        dk_ref[0] = dk
        dv_ref[0] = dv

    return pl.pallas_call(
        body, name="flash_bwd", grid=(NH, t // tk),
        in_specs=[pl.BlockSpec((1, t, QK), lambda h, j: (h, 0, 0)), pl.BlockSpec((1, tk, QK), lambda h, j: (h, j, 0)),
                  pl.BlockSpec((1, tk, HD), lambda h, j: (h, j, 0)), pl.BlockSpec((t, HD), lambda h, j: (0, h)),
                  pl.BlockSpec((1, 1, t), lambda h, j: (h, 0, 0)), pl.BlockSpec((1, 1, t), lambda h, j: (h, 0, 0))],
        out_specs=[pl.BlockSpec((1, t, QK), lambda h, j: (h, 0, 0)), pl.BlockSpec((1, tk, QK), lambda h, j: (h, j, 0)),
                   pl.BlockSpec((1, tk, HD), lambda h, j: (h, j, 0))],
        out_shape=[jax.ShapeDtypeStruct((NH, t, QK), F32), jax.ShapeDtypeStruct((NH, t, QK), F32),
                   jax.ShapeDtypeStruct((NH, t, HD), F32)],
        compiler_params=_params(("parallel", "arbitrary")),
    )(q, k, v, do, lse_row, delta_row)


def _ffn_in(xn, w_in, name):
    t, k = xn.shape
    s, _, n = w_in.shape
    tm = _pick(t, ROW_TILES)

    def body(x_ref, w_ref, gu_ref, a_ref):
        x = x_ref[...]
        g, u = _dot(x, w_ref[0]), _dot(x, w_ref[1])
        gu_ref[0] = g.astype(BF16)
        gu_ref[1] = u.astype(BF16)
        a_ref[...] = (g * _sig(g) * u).astype(BF16)

    return pl.pallas_call(
        body, name=name, grid=(t // tm, s // 2),
        in_specs=[_BS((tm, k), lambda i, j: (i, 0)), _BS((2, None, k, n), lambda i, j: (0, j, 0, 0))],
        out_specs=[_BS((2, None, tm, n), lambda i, j: (0, j, i, 0)), _BS((None, tm, n), lambda i, j: (j, i, 0))],
        out_shape=[jax.ShapeDtypeStruct((2, s // 2, t, n), BF16), jax.ShapeDtypeStruct((s // 2, t, n), BF16)],
        compiler_params=_params(("parallel", "parallel")),
    )(xn, w_in.reshape(2, s // 2, k, n))


def _ffn_dgu(dfo, w_out, gu, name):
    t, k = dfo.shape
    s, n, _ = w_out.shape
    tm = _pick(t, ROW_TILES)

    def body(d_ref, w_ref, gu_ref, o_ref):
        da = _dot_nt(d_ref[...], w_ref[...])
        g, u = gu_ref[0].astype(F32), gu_ref[1].astype(F32)
        sg = _sig(g)
        o_ref[0] = (da * u * sg * (1.0 + g * (1.0 - sg))).astype(BF16)
        o_ref[1] = (da * g * sg).astype(BF16)

    pair = _BS((2, None, tm, n), lambda i, j: (0, j, i, 0))
    return pl.pallas_call(
        body, name=name, grid=(t // tm, s),
        in_specs=[_BS((tm, k), lambda i, j: (i, 0)), _BS((None, n, k), lambda i, j: (j, 0, 0)), pair],
        out_specs=pair, out_shape=jax.ShapeDtypeStruct((2, s, t, n), BF16),
        compiler_params=_params(("parallel", "parallel")),
    )(dfo, w_out, gu).reshape(2 * s, t, n)


def _ffn_fwd(xn, w_in, w_out, tag):
    gu, a = _ffn_in(xn, w_in, tag + "_in")
    return gu, a, _mm_stack_red(a, w_out, tag + "_out", "nn")


def _ffn_bwd(dfo, xn, gu, a, w_in, w_out, tag):
    dw_out = _mm_stack_tn(a, dfo, tag + "_dwout")
    dgu = _ffn_dgu(dfo, w_out, gu, tag + "_dgu")
    return _mm_stack_red(dgu, w_in, tag + "_dxn", "nt"), _mm_stack_tn(xn, dgu, tag + "_dwin"), dw_out


def _local_step(x, target, cs, sn, w, late_weights, s, early_grads):
    t = x.shape[0]
    tm = _pick(t, (256, 128))
    g = {}
    ffn_out = lambda n: w[n].reshape(4, FF // 4, D)

    def norm_fn(x_ref, g_ref, o_ref):
        o_ref[...] = _rms(x_ref[...], g_ref[...]).astype(BF16)

    xn1 = _rows(norm_fn, "norm1", t, tm, [(x, D, 0)], [s["ffn1_norm"]], [(D, BF16)])[0]
    gu1, a1, f1 = _ffn_fwd(xn1, w["ffn1_w_in"], ffn_out("ffn1_w_out"), "ffn1")

    def res_norm_fn(scale):
        def fn(h_ref, f_ref, g_ref, h_out, n_out):
            h = h_ref[...] + scale * f_ref[...]
            h_out[...] = h
            n_out[...] = _rms(h, g_ref[...]).astype(BF16)
        return fn

    h1, u = _rows(res_norm_fn(0.5), "res_norm1", t, tm, [(x, D, 0), (f1, D, 0)], [s["mix_norm"]],
                  [(D, F32), (D, BF16)])
    w = {**w, **late_weights(h1)}
    rows_of = lambda n: w[n].reshape(-1, w[n].shape[2])
    w_hgb, w_mlab, w_o = rows_of("w_hg_branch"), rows_of("w_mla_branch"), rows_of("w_out")
    w_in_nat = w["w_in"].transpose(1, 0, 2).reshape(D, -1)
    w_mrg = w["w_merge"]
    mw = w_mrg.shape[2]
    w_in_hg, w_in_mla = w_in_nat[:, :4 * HGW], w_in_nat[:, 4 * HGW:]
    p_hg = _mm(u, w_in_hg, "nn", "proj_hg")
    p_mla = _mm(u, w_in_mla, "nn", "proj_mla")
    gpre = _cols_fwd(u, w_mrg, "proj_gate")
    o_pre, hgy, states = _hgrn_fwd(p_hg, s["hg_lb_table"], s["hg_out_norm"], t)
    prep_args = (p_mla, cs, sn, w["w_q_up"], w["w_kv_up"], s["mla_q_lora_norm"], s["mla_kv_lora_norm"],
                 s["q_head_norm"], s["k_head_norm"])
    q, k, v = _mla_prep_fwd(*prep_args, t)
    att, lse = _flash_fwd(q, k, v, t)
    y_hg = _mm(hgy, w_hgb, "nn", "branch_hg")
    y_mla = _mm(att, w_mlab, "nn", "branch_mla")

    def mix_fn(gh_ref, gm_ref, yh_ref, ym_ref, b_ref, o_ref):
        gh = _sig(gh_ref[...] + b_ref[:, 0:D])
        gm = _sig(gm_ref[...] + b_ref[:, D:2 * D])
        o_ref[...] = (gh * yh_ref[...] + gm * ym_ref[...]).astype(BF16)

    mixed = _rows(mix_fn, "mix", t, tm, [(gpre, D, 0), (gpre, D, 1), (y_hg, D, 0), (y_mla, D, 0)], [s["b_merge"]],
                  [(D, BF16)])[0]
    mo = _mm(mixed, w_o, "nn", "mix_out")
    h2, xn2 = _rows(res_norm_fn(1.0), "res_norm2", t, tm, [(h1, D, 0), (mo, D, 0)], [s["ffn2_norm"]],
                    [(D, F32), (D, BF16)])
    gu2, a2, f2 = _ffn_fwd(xn2, w["ffn2_w_in"], ffn_out("ffn2_w_out"), "ffn2")

    def loss_fn(h_ref, f_ref, tg_ref, g_ref, dh_out, dhb_out):
        h = h_ref[...] + 0.5 * f_ref[...]
        e = _rms(h, g_ref[...]) - tg_ref[...]
        dh, dgain = _rms_bwd(h, g_ref[...], e / D)
        dh_out[...] = dh
        dhb_out[...] = (0.5 * dh).astype(BF16)
        return dgain, jnp.full((1, LANE), 0.5 / D * jnp.sum(e * e), F32)

    dh3, dfo2, g["final_norm"], loss = _rows(loss_fn, "loss", t, tm, [(h2, D, 0), (f2, D, 0), (target, D, 0)],
                                             [s["final_norm"]], [(D, F32), (D, BF16)], [(1, D), (1, LANE)])

    def norm_bwd_fn(scale):
        def fn(h_ref, dxn_ref, dh_ref, g_ref, dh_out, dhb_out):
            dx, dgain = _rms_bwd(h_ref[...], g_ref[...], dxn_ref[...])
            dh = dh_ref[...] + dx
            dh_out[...] = dh
            dhb_out[...] = (scale * dh).astype(BF16)
            return (dgain,)
        return fn

    as_rows = lambda a: a.reshape((N_DEV, -1) + a.shape[-1:])
    dxn2, g["ffn2_w_in"], dwo = _ffn_bwd(dfo2, xn2, gu2, a2, w["ffn2_w_in"], ffn_out("ffn2_w_out"), "ffn2")
    g["ffn2_w_out"] = as_rows(dwo)
    dh2, dh2b, g["ffn2_norm"] = _rows(norm_bwd_fn(1.0), "norm2_bwd", t, tm, [(h2, D, 0), (dxn2, D, 0), (dh3, D, 0)],
                                      [s["ffn2_norm"]], [(D, F32), (D, BF16)], [(1, D)])
    dmixed = _mm(dh2b, w_o, "nt", "mix_out_dx")
    g["w_out"] = as_rows(_mm(mixed, dh2b, "tn", "mix_out_dw"))

    def mix_bwd_fn(gh_ref, gm_ref, yh_ref, ym_ref, dm_ref, b_ref, dyh_out, dym_out, dg_out):
        gh = _sig(gh_ref[...] + b_ref[:, 0:D])
        gm = _sig(gm_ref[...] + b_ref[:, D:2 * D])
        dm = dm_ref[...]
        dyh_out[...] = (dm * gh).astype(BF16)
        dym_out[...] = (dm * gm).astype(BF16)
        dgh = dm * yh_ref[...] * gh * (1.0 - gh)
        dgm = dm * ym_ref[...] * gm * (1.0 - gm)
        dg_out[:, 0:D] = dgh.astype(BF16)
        dg_out[:, D:2 * D] = dgm.astype(BF16)
        return (jnp.concatenate([jnp.sum(dgh, axis=0, keepdims=True), jnp.sum(dgm, axis=0, keepdims=True)], axis=1),)

    dyh, dym, dgpre, g["b_merge"] = _rows(
        mix_bwd_fn, "mix_bwd", t, tm, [(gpre, D, 0), (gpre, D, 1), (y_hg, D, 0), (y_mla, D, 0), (dmixed, D, 0)],
        [s["b_merge"]], [(D, BF16), (D, BF16), (2 * D, BF16)], [(1, 2 * D)])
    g["w_hg_branch"] = as_rows(_mm(hgy, dyh, "tn", "branch_hg_dw"))
    g["w_mla_branch"] = as_rows(_mm(att, dym, "tn", "branch_mla_dw"))
    g["w_merge"] = _cols_dw(u, dgpre, mw, "proj_gate_dw")
    dhgy = _mm(dyh, w_hgb, "nt", "branch_hg_dx")
    datt = _mm(dym, w_mlab, "nt", "branch_mla_dx", out_dtype=BF16)
    du_gate = _cols_dx(dgpre, w_mrg, "proj_gate_dx")

    dq, dk, dv = _flash_bwd(q, k, v, datt, lse.reshape(NH, 1, t), _flash_delta(att, datt, t), t)
    (dp_mla, g["w_q_up"], g["w_kv_up"], g["mla_q_lora_norm"], g["mla_kv_lora_norm"], g["q_head_norm"],
     g["k_head_norm"]) = _mla_prep_bwd(*prep_args, dq, dk, dv, t)
    dp_hg, g["hg_lb_table"], g["hg_out_norm"] = _hgrn_bwd(p_hg, s["hg_lb_table"], s["hg_out_norm"], o_pre, states,
                                                          dhgy, t)
    dw_in_nat = jnp.concatenate([_mm(u, dp_hg, "tn", "proj_hg_dw"), _mm(u, dp_mla, "tn", "proj_mla_dw")], axis=1)
    g["w_in"] = dw_in_nat.reshape(D, N_DEV, -1).transpose(1, 0, 2)
    du_hg = _mm(dp_hg, w_in_hg, "nt", "proj_hg_dx")
    du_mla = _mm(dp_mla, w_in_mla, "nt", "proj_mla_dx")

    def mixnorm_bwd_fn(h_ref, a_ref, b_ref, c_ref, dh_ref, g_ref, dh_out, dhb_out):
        dx, dgain = _rms_bwd(h_ref[...], g_ref[...], a_ref[...] + b_ref[...] + c_ref[...])
        dh = dh_ref[...] + dx
        dh_out[...] = dh
        dhb_out[...] = (0.5 * dh).astype(BF16)
        return (dgain,)

    mix_gain = s["mix_norm"] + early_grads(g)[0:1, 0:1]
    dh1, dfo1, g["mix_norm"] = _rows(mixnorm_bwd_fn, "mixnorm_bwd", t, tm,
                                     [(h1, D, 0), (du_hg, D, 0), (du_mla, D, 0), (du_gate, D, 0), (dh2, D, 0)],
                                     [mix_gain], [(D, F32), (D, BF16)], [(1, D)])
    dxn1, g["ffn1_w_in"], dwo = _ffn_bwd(dfo1, xn1, gu1, a1, w["ffn1_w_in"], ffn_out("ffn1_w_out"), "ffn1")
    g["ffn1_w_out"] = as_rows(dwo)
    grad_x, _, g["ffn1_norm"] = _rows(norm_bwd_fn(1.0), "norm1_bwd", t, tm, [(x, D, 0), (dxn1, D, 0), (dh1, D, 0)],
                                      [s["ffn1_norm"]], [(D, F32), (D, BF16)], [(1, D)])
    return loss, grad_x, g


def _coords():
    return lax.axis_index("x"), lax.axis_index("y"), lax.axis_index("c")


def _hbm_call(body, name, ins, out_shapes, scratch):
    any_spec = pl.BlockSpec(memory_space=pl.ANY)
    return pl.pallas_call(
        body, name=name, out_shape=[jax.ShapeDtypeStruct(s, dt) for s, dt in out_shapes],
        in_specs=[any_spec] * len(ins), out_specs=[any_spec] * len(out_shapes), scratch_shapes=scratch,
    )(*ins)


def _all_gather(blocks, name):
    nb = len(blocks)

    def body(*refs):
        x_refs, out_refs = refs[:nb], refs[nb:2 * nb]
        send_sems, recv_sems, local_sems = refs[2 * nb:]
        x, y, c = _coords()
        me, sibling = (x, y, c), (x, y, 1 - c)
        chips = [(1 - x, y), (x, 1 - y), (1 - x, 1 - y)]

        def slot(b, px, py, pc):
            return out_refs[b].at[4 * px + 2 * py + pc]

        def copy(b, kk, block_of, to, src=None):
            return pltpu.make_async_remote_copy(
                src_ref=slot(b, *block_of) if src is None else src, dst_ref=slot(b, *block_of),
                send_sem=send_sems.at[b, kk], recv_sem=recv_sems.at[b, kk], device_id=to, device_id_type=MESH)

        mine = [pltpu.make_async_copy(x_refs[b], slot(b, *me), local_sems.at[b]) for b in range(nb)]
        for cp in mine:
            cp.start()
        first = [copy(b, 0, me, sibling, src=x_refs[b]) for b in range(nb)]
        first += [copy(b, 1 + j, me, (*chip, c), src=x_refs[b]) for j, chip in enumerate(chips) for b in range(nb)]
        for cp in first:
            cp.start()
        passed = []
        for j, chip in enumerate(chips):
            for b in range(nb):
                copy(b, 1 + j, (*chip, c), me).wait_recv()
                passed.append(copy(b, 4 + j, (*chip, c), sibling))
                passed[-1].start()
        for b in range(nb):
            copy(b, 0, sibling, me).wait_recv()
        for j, chip in enumerate(chips):
            for b in range(nb):
                copy(b, 4 + j, (*chip, 1 - c), me).wait_recv()
        for cp in first + passed:
            cp.wait_send()
        for cp in mine:
            cp.wait()

    return _hbm_call(body, name, blocks, [((N_DEV,) + b.shape, b.dtype) for b in blocks],
                     [pltpu.SemaphoreType.DMA((nb, 7)), pltpu.SemaphoreType.DMA((nb, 7)),
                      pltpu.SemaphoreType.DMA((nb,))])


def _gather_peers():
    x, y, c = _coords()
    return (x, y, c), [(x, y, 1 - c), (1 - x, y, c), (x, 1 - y, c), (1 - x, 1 - y, c)]


def _gather_start(blocks, after, name):
    nb = len(blocks)
    hbm, sem = pl.BlockSpec(memory_space=pltpu.HBM), pl.BlockSpec(memory_space=pltpu.SEMAPHORE)

    def body(*refs):
        x_refs, out_refs = refs[:nb], refs[nb:2 * nb]
        send_sems, recv_sems, token = refs[2 * nb + 1], refs[2 * nb + 2], refs[-1]
        (x, y, c), peers = _gather_peers()
        for kk, peer in enumerate(peers):
            for b in range(nb):
                pltpu.make_async_remote_copy(
                    src_ref=x_refs[b], dst_ref=out_refs[b].at[4 * x + 2 * y + c], send_sem=send_sems.at[4 * b + kk],
                    recv_sem=recv_sems.at[4 * b + kk], device_id=peer, device_id_type=MESH).start()
        token[...] = jnp.zeros_like(token)

    gathers = [pltpu.with_memory_space_constraint(lax.empty((N_DEV,) + b.shape, b.dtype), pltpu.HBM) for b in blocks]
    outs = pl.pallas_call(
        body, name=name,
        out_shape=(pltpu.SemaphoreType.DMA((4 * nb,)), pltpu.SemaphoreType.DMA((4 * nb,)),
                   *[pltpu.HBM(b.shape, b.dtype) for b in blocks], *[pltpu.HBM(b.shape, b.dtype) for b in gathers],
                   jax.ShapeDtypeStruct((SUBLANE, LANE), F32)),
        in_specs=[hbm] * (2 * nb) + [pl.BlockSpec(memory_space=pl.ANY)],
        out_specs=(sem, sem, *[hbm] * (2 * nb), pl.BlockSpec(memory_space=pltpu.VMEM)),
        input_output_aliases={i: 2 + i for i in range(2 * nb)},
        compiler_params=pltpu.CompilerParams(has_side_effects=pltpu.SideEffectType.DATAFLOW_SIDE_EFFECTING),
    )(*[pltpu.with_memory_space_constraint(b, pltpu.HBM) for b in blocks], *gathers, after)
    return outs[0], outs[1], list(outs[2:2 + nb]), list(outs[2 + nb:2 + 2 * nb]), outs[-1]


def _gather_wait(send_sems, recv_sems, thru, gathers, after, name):
    nb = len(thru)
    hbm, sem = pl.BlockSpec(memory_space=pltpu.HBM), pl.BlockSpec(memory_space=pltpu.SEMAPHORE)

    def body(*refs):
        x_refs, out_refs = refs[:nb], refs[nb:2 * nb]
        send_sems_, recv_sems_ = refs[2 * nb], refs[2 * nb + 1]
        _, peers = _gather_peers()
        for kk, (px, py, pc) in enumerate(peers):
            for b in range(nb):
                cp = pltpu.make_async_remote_copy(
                    src_ref=x_refs[b], dst_ref=out_refs[b].at[4 * px + 2 * py + pc], send_sem=send_sems_.at[4 * b + kk],
                    recv_sem=recv_sems_.at[4 * b + kk], device_id=(px, py, pc), device_id_type=MESH)
                cp.wait_send()
                cp.wait_recv()

    outs = pl.pallas_call(
        body, name=name,
        out_shape=(*[pltpu.HBM(b.shape, b.dtype) for b in thru], *[pltpu.HBM(b.shape, b.dtype) for b in gathers]),
        in_specs=[hbm] * (2 * nb) + [sem, sem, pl.BlockSpec(memory_space=pl.ANY)], out_specs=[hbm] * (2 * nb),
        input_output_aliases={i: i for i in range(2 * nb)},
        compiler_params=pltpu.CompilerParams(has_side_effects=pltpu.SideEffectType.DATAFLOW_SIDE_EFFECTING),
    )(*thru, *gathers, send_sems, recv_sems, after)
    return list(outs[:nb]), list(outs[nb:])


def _gather_finish(blocks, gathers, name):
    nb = len(blocks)

    def body(*refs):
        x_refs, in_refs, out_refs = refs[:nb], refs[nb:2 * nb], refs[2 * nb:3 * nb]
        send_sems, recv_sems, local_sems = refs[3 * nb:]
        (x, y, c), peers = _gather_peers()
        mine = [pltpu.make_async_copy(x_refs[b], out_refs[b].at[4 * x + 2 * y + c], local_sems.at[b, 0])
                for b in range(nb)]
        for kk, (px, py, pc) in enumerate(peers):
            slot = 4 * px + 2 * py + pc
            mine += [pltpu.make_async_copy(in_refs[b].at[slot], out_refs[b].at[slot], local_sems.at[b, 1 + kk])
                     for b in range(nb)]
        for cp in mine:
            cp.start()
        copies = []
        for j, (px, py, _) in enumerate(peers[1:]):
            for b in range(nb):
                copies.append(pltpu.make_async_remote_copy(
                    src_ref=in_refs[b].at[4 * px + 2 * py + c], dst_ref=out_refs[b].at[4 * px + 2 * py + c],
                    send_sem=send_sems.at[b, j], recv_sem=recv_sems.at[b, j], device_id=(x, y, 1 - c),
                    device_id_type=MESH))
                copies[-1].start()
        for j, (px, py, _) in enumerate(peers[1:]):
            for b in range(nb):
                pltpu.make_async_remote_copy(
                    src_ref=in_refs[b].at[4 * px + 2 * py + c], dst_ref=out_refs[b].at[4 * px + 2 * py + 1 - c],
                    send_sem=send_sems.at[b, j], recv_sem=recv_sems.at[b, j], device_id=(x, y, 1 - c),
                    device_id_type=MESH).wait_recv()
        for cp in copies:
            cp.wait_send()
        for cp in mine:
            cp.wait()

    any_spec = pl.BlockSpec(memory_space=pl.ANY)
    return pl.pallas_call(
        body, name=name, out_shape=[jax.ShapeDtypeStruct(b.shape, b.dtype) for b in gathers],
        in_specs=[any_spec] * (2 * nb), out_specs=[any_spec] * nb,
        scratch_shapes=[pltpu.SemaphoreType.DMA((nb, 3)), pltpu.SemaphoreType.DMA((nb, 3)),
                        pltpu.SemaphoreType.DMA((nb, 5))],
    )(*blocks, *gathers)


def _sibling_swap(bufs, name):
    nb = len(bufs)

    def body(*refs):
        x_refs, out_refs = refs[:nb], refs[nb:2 * nb]
        send_sems, recv_sems = refs[2 * nb:]
        x, y, c = _coords()
        copies = [pltpu.make_async_remote_copy(
            src_ref=x_refs[b].at[2 * q + 1 - c], dst_ref=out_refs[b].at[q], send_sem=send_sems.at[b, q],
            recv_sem=recv_sems.at[b, q], device_id=(x, y, 1 - c), device_id_type=MESH)
            for b in range(nb) for q in range(4)]
        for cp in copies:
            cp.start()
        for cp in copies:
            cp.wait()

    return _hbm_call(body, name, bufs, [((4,) + b.shape[1:], b.dtype) for b in bufs],
                     [pltpu.SemaphoreType.DMA((nb, 4)), pltpu.SemaphoreType.DMA((nb, 4))])


def _chip_exchange(bufs, name):
    nb = len(bufs)

    def body(*refs):
        x_refs, out_refs = refs[:nb], refs[nb:2 * nb]
        send_sems, recv_sems, local_sems = refs[2 * nb:]
        x, y, c = _coords()
        chips = [(1 - x, y), (x, 1 - y), (1 - x, 1 - y)]
        mine = [pltpu.make_async_copy(x_refs[b].at[2 * x + y], out_refs[b].at[2 * x + y], local_sems.at[b])
                for b in range(nb)]
        for cp in mine:
            cp.start()
        copies = [pltpu.make_async_remote_copy(
            src_ref=x_refs[b].at[2 * px + py], dst_ref=out_refs[b].at[2 * x + y], send_sem=send_sems.at[b, j],
            recv_sem=recv_sems.at[b, j], device_id=(px, py, c), device_id_type=MESH)
            for j, (px, py) in enumerate(chips) for b in range(nb)]
        for cp in copies:
            cp.start()
        for j, (px, py) in enumerate(chips):
            for b in range(nb):
                pltpu.make_async_remote_copy(
                    src_ref=x_refs[b].at[2 * x + y], dst_ref=out_refs[b].at[2 * px + py], send_sem=send_sems.at[b, j],
                    recv_sem=recv_sems.at[b, j], device_id=(px, py, c), device_id_type=MESH).wait_recv()
        for cp in copies:
            cp.wait_send()
        for cp in mine:
            cp.wait()

    return _hbm_call(body, name, bufs, [(b.shape, b.dtype) for b in bufs],
                     [pltpu.SemaphoreType.DMA((nb, 3)), pltpu.SemaphoreType.DMA((nb, 3)),
                      pltpu.SemaphoreType.DMA((nb,))])


def _chip_exchange_start(bufs, name):
    nb = len(bufs)
    hbm, sem = pl.BlockSpec(memory_space=pltpu.HBM), pl.BlockSpec(memory_space=pltpu.SEMAPHORE)

    def body(*refs):
        x_refs, land_refs = refs[:nb], refs[nb:2 * nb]
        send_sems, recv_sems, token = refs[2 * nb], refs[2 * nb + 1], refs[-1]
        x, y, c = _coords()
        for j, (px, py) in enumerate([(1 - x, y), (x, 1 - y), (1 - x, 1 - y)]):
            for b in range(nb):
                pltpu.make_async_remote_copy(
                    src_ref=x_refs[b].at[2 * px + py], dst_ref=land_refs[b].at[2 * x + y], send_sem=send_sems.at[3 * b + j],
                    recv_sem=recv_sems.at[3 * b + j], device_id=(px, py, c), device_id_type=MESH).start()
        token[...] = jnp.zeros_like(token)

    lands = [pltpu.with_memory_space_constraint(lax.empty(b.shape, b.dtype), pltpu.HBM) for b in bufs]
    outs = pl.pallas_call(
        body, name=name,
        out_shape=(pltpu.SemaphoreType.DMA((3 * nb,)), pltpu.SemaphoreType.DMA((3 * nb,)),
                   *[pltpu.HBM(b.shape, b.dtype) for b in bufs], *[pltpu.HBM(b.shape, b.dtype) for b in bufs],
                   jax.ShapeDtypeStruct((SUBLANE, LANE), F32)),
        in_specs=[hbm] * (2 * nb), out_specs=(sem, sem, *[hbm] * (2 * nb), pl.BlockSpec(memory_space=pltpu.VMEM)),
        input_output_aliases={i: 2 + i for i in range(2 * nb)},
        compiler_params=pltpu.CompilerParams(has_side_effects=pltpu.SideEffectType.DATAFLOW_SIDE_EFFECTING),
    )(*[pltpu.with_memory_space_constraint(b, pltpu.HBM) for b in bufs], *lands)
    return outs[0], outs[1], list(outs[2:2 + nb]), list(outs[2 + nb:2 + 2 * nb]), outs[-1]


def _chip_exchange_wait(send_sems, recv_sems, thru, lands, after, name):
    nb = len(thru)
    hbm, sem = pl.BlockSpec(memory_space=pltpu.HBM), pl.BlockSpec(memory_space=pltpu.SEMAPHORE)

    def body(*refs):
        x_refs, land_refs = refs[:nb], refs[nb:2 * nb]
        send_sems_, recv_sems_ = refs[2 * nb], refs[2 * nb + 1]
        x, y, c = _coords()
        for j, (px, py) in enumerate([(1 - x, y), (x, 1 - y), (1 - x, 1 - y)]):
            for b in range(nb):
                cp = pltpu.make_async_remote_copy(
                    src_ref=x_refs[b].at[2 * px + py], dst_ref=land_refs[b].at[2 * px + py],
                    send_sem=send_sems_.at[3 * b + j], recv_sem=recv_sems_.at[3 * b + j], device_id=(px, py, c),
                    device_id_type=MESH)
                cp.wait_send()
                cp.wait_recv()

    outs = pl.pallas_call(
        body, name=name,
        out_shape=(*[pltpu.HBM(b.shape, b.dtype) for b in thru], *[pltpu.HBM(b.shape, b.dtype) for b in lands]),
        in_specs=[hbm] * (2 * nb) + [sem, sem, pl.BlockSpec(memory_space=pl.ANY)], out_specs=[hbm] * (2 * nb),
        input_output_aliases={i: i for i in range(2 * nb)},
        compiler_params=pltpu.CompilerParams(has_side_effects=pltpu.SideEffectType.DATAFLOW_SIDE_EFFECTING),
    )(*thru, *lands, send_sems, recv_sems, after)
    return list(outs[:nb]), list(outs[nb:])


def _chip_sum(g, r1, c, name):
    _, r, cw = g.shape
    tr = _pick(r, (256, 176, 128))

    def body(c_ref, g_ref, r_ref, o_ref):
        o_ref[...] = (g_ref[...] + r_ref[...]).astype(BF16)

    grid_spec = pltpu.PrefetchScalarGridSpec(
        num_scalar_prefetch=1, grid=(4, r // tr),
        in_specs=[_BS((None, None, tr, cw), lambda q, i, c_ref: (q, c_ref[0], i, 0)),
                  _BS((None, tr, cw), lambda q, i, c_ref: (q, i, 0))],
        out_specs=_BS((None, tr, cw), lambda q, i, c_ref: (q, i, 0)))
    return pl.pallas_call(
        body, name=name, grid_spec=grid_spec, out_shape=jax.ShapeDtypeStruct((4, r, cw), BF16),
        compiler_params=_params(("parallel", "parallel")),
    )(c.reshape(1).astype(jnp.int32), g.reshape(4, 2, r, cw), r1)


def _adamw_math(w, g, m, v):
    m = B1 * m + (1.0 - B1) * g
    v = B2 * v + (1.0 - B2) * (g * g)
    m_hat = m / (1.0 - B1 ** STEP)
    v_hat = v / (1.0 - B2 ** STEP)
    return -LR * (m_hat / (jnp.sqrt(v_hat) + AEPS) + WD * w), m, v


def _sum_adamw(parts, w, m, v, name):
    r, c = w.shape
    tr = _pick(r, (256, 176, 128))

    def body(p0, p1, p2, p3, w_ref, m_ref, v_ref, g_out, d_out, m_out, v_out):
        g = ((p0[...].astype(F32) + p1[...].astype(F32)) + p2[...].astype(F32)) + p3[...].astype(F32)
        g_out[...] = g
        d_out[...], m_out[...], v_out[...] = _adamw_math(w_ref[...], g, m_ref[...], v_ref[...])

    part = lambda q: _BS((None, tr, c), functools.partial(lambda i, q: (q, i, 0), q=q))
    plain = _BS((tr, c), lambda i: (i, 0))
    return pl.pallas_call(
        body, name=name, grid=(r // tr,), in_specs=[part(q) for q in range(4)] + [plain] * 3,
        out_specs=[plain] * 4, out_shape=[jax.ShapeDtypeStruct((r, c), F32)] * 4,
        compiler_params=_params(("parallel",)),
    )(parts, parts, parts, parts, w, m, v)


def _small_update(gathered, w, m, v):
    r = w.shape[0]

    def body(ga_ref, w_ref, m_ref, v_ref, g_out, d_out, m_out, v_out):
        g = ga_ref[0]
        for dev in range(1, N_DEV):
            g = g + ga_ref[dev]
        g_out[...] = g
        d_out[...], m_out[...], v_out[...] = _adamw_math(w_ref[...], g, m_ref[...], v_ref[...])

    return pl.pallas_call(
        body, name="small_update", out_shape=[jax.ShapeDtypeStruct((r, LANE), F32)] * 4,
    )(gathered, w, m, v)


def _pack_small(vals):
    rows = []
    for name, (r, n) in SMALL:
        flat = vals[name].reshape(-1)
        pad = (-flat.shape[0]) % (SUBLANE * LANE)
        rows.append(jnp.pad(flat, (0, pad)).reshape(-1, LANE))
    return jnp.concatenate(rows, axis=0)


def _unpack_small(packed):
    out, off = {}, 0
    for name, (r, n) in SMALL:
        nrow = -(-(r * n) // (SUBLANE * LANE)) * SUBLANE
        out[name] = packed[off:off + nrow].reshape(-1)[:r * n].reshape(r, n)
        off += nrow
    return out


def kernel(x, positions, ffn1_norm, ffn1_w_in, ffn1_w_out, mix_norm, w_in, hg_lb_table, hg_out_norm, w_hg_branch, mla_q_lora_norm, w_q_up, mla_kv_lora_norm, w_kv_up, q_head_norm, k_head_norm, w_mla_branch, w_merge, b_merge, w_out, ffn2_norm, ffn2_w_in, ffn2_w_out, final_norm, loss_target, m_ffn1_norm, m_ffn1_w_in, m_ffn1_w_out, m_mix_norm, m_w_in, m_hg_lb_table, m_hg_out_norm, m_w_hg_branch, m_mla_q_lora_norm, m_w_q_up, m_mla_kv_lora_norm, m_w_kv_up, m_q_head_norm, m_k_head_norm, m_w_mla_branch, m_w_merge, m_b_merge, m_w_out, m_ffn2_norm, m_ffn2_w_in, m_ffn2_w_out, m_final_norm, v_ffn1_norm, v_ffn1_w_in, v_ffn1_w_out, v_mix_norm, v_w_in, v_hg_lb_table, v_hg_out_norm, v_w_hg_branch, v_mla_q_lora_norm, v_w_q_up, v_mla_kv_lora_norm, v_w_kv_up, v_q_head_norm, v_k_head_norm, v_w_mla_branch, v_w_merge, v_b_merge, v_w_out, v_ffn2_norm, v_ffn2_w_in, v_ffn2_w_out, v_final_norm):
    args = dict(locals())
    t = x.shape[1]
    big_w = {n: args[n][0] for n, _, _ in BIG}
    small = {n: args[n].reshape(shape) for n, shape in SMALL}

    names = [n for n, _, _ in BIG]
    first, rest = names[:2], names[2:]
    full = dict(zip(first, _all_gather([big_w[n].astype(BF16) for n in first], "weights_all_gather_ffn1")))
    g_send, g_recv, g_thru, g_bufs, g_token = _gather_start([big_w[n].astype(BF16) for n in rest], full[first[0]],
                                                            "weights_gather_start")
    gains = dict(small, ffn1_norm=small["ffn1_norm"] + g_token[0:1, 0:1])

    def late_weights(after):
        blocks, bufs = _gather_wait(g_send, g_recv, g_thru, g_bufs, after, "weights_gather_wait")
        return dict(zip(rest, _gather_finish(blocks, bufs, "weights_gather_finish")))

    inv_freq = ROPE_THETA ** (-jnp.arange(0, ROPE, 2, dtype=F32) / ROPE)
    ang = positions[0].astype(F32)[:, None] * inv_freq
    cs = jnp.concatenate([jnp.cos(ang), jnp.cos(ang)], axis=1)
    sn = jnp.concatenate([jnp.sin(ang), jnp.sin(ang)], axis=1)

    c = lax.axis_index("c")
    chip = 2 * lax.axis_index("x") + lax.axis_index("y")
    early = {}

    def chip_sums_of(g, ns, tag):
        from_sibling = _sibling_swap([g[n] for n in ns], "grads_sibling_swap_" + tag)
        return [_chip_sum(g[n], r1, c, "chip_sum_" + n) for n, r1 in zip(ns, from_sibling)]

    def early_grads(g):
        early["names"] = [n for n in names if n in g]
        early["sums"] = chip_sums_of(g, early["names"], "early")
        early["send"], early["recv"], early["thru"], early["lands"], token = _chip_exchange_start(
            early["sums"], "grads_exchange_start")
        return token

    loss_row, grad_x, g = _local_step(x[0], loss_target[0], cs, sn, full, late_weights, gains, early_grads)
    late = [n for n in names if n not in early["names"]]
    exchanged = dict(zip(late, _chip_exchange(chip_sums_of(g, late, "late"), "grads_chip_exchange_late")))
    sent, landed = _chip_exchange_wait(early["send"], early["recv"], early["thru"], early["lands"],
                                       exchanged[late[0]], "grads_exchange_wait")
    for n, land, own in zip(early["names"], landed, sent):
        exchanged[n] = lax.dynamic_update_index_in_dim(land, lax.dynamic_index_in_dim(own, chip, 0), chip, 0)

    small_packed = jnp.concatenate([_pack_small(g), jnp.pad(loss_row, ((0, SUBLANE - 1), (0, 0)))], axis=0)
    small_all = _all_gather([small_packed], "small_all_gather")[0]
    zero_tail = jnp.zeros((SUBLANE, LANE), F32)
    pk = lambda d: jnp.concatenate([_pack_small(d), zero_tail], axis=0)
    sg, sd, sm, sv = _small_update(
        small_all, pk(small), pk({n: args["m_" + n].reshape(shape) for n, shape in SMALL}),
        pk({n: args["v_" + n].reshape(shape) for n, shape in SMALL}))
    n_small_rows = _pack_small(small).shape[0]
    loss = sg[n_small_rows, 0]
    outs = {k_: _unpack_small(a) for k_, a in (("grad", sg), ("delta", sd), ("new_m", sm), ("new_v", sv))}

    for n in names:
        outs["grad"][n], outs["delta"][n], outs["new_m"][n], outs["new_v"][n] = _sum_adamw(
            exchanged[n], big_w[n], args["m_" + n][0], args["v_" + n][0], "adamw_" + n)

    def shaped(kind, n):
        return outs[kind][n].reshape(args[n].shape)

    return (loss, grad_x[None], *[shaped("grad", n) for n in WEIGHT_ORDER], *[shaped("delta", n) for n in WEIGHT_ORDER],
            *[shaped("new_m", n) for n in WEIGHT_ORDER], *[shaped("new_v", n) for n in WEIGHT_ORDER])
```

```python
import functools

import jax
import jax.numpy as jnp
from jax import lax
from jax.experimental import pallas as pl
from jax.experimental.pallas import tpu as pltpu

F32 = jnp.float32
BF16 = jnp.bfloat16

D = 1024
FF = 2816
NH = 8
HD = 128
ROPE = 64
QK = HD + ROPE
QL = 384
KVL = 256
HGW = NH * HD
CHUNK = 64
EPS = 1e-6
ROPE_THETA = 10000.0
SCALE = QK ** -0.5

LR, B1, B2, AEPS, WD, STEP = 0.001, 0.9, 0.999, 1e-08, 0.01, 10

HB = 128
SUB = 16
EXP_CLAMP = 80.0
ATT_TILES = (512, 256, 128)
ATT_Q = (1024, 512, 256, 128)
ATT_KEY_TILES = 1
ROW_TILES = (1024, 512, 256, 128)

LANE = 128
SUBLANE = 8
VMEM_LIMIT = 56 << 20

N_DEV = 8
MESH = pl.DeviceIdType.MESH

BIG = (
    ("ffn1_w_in", (D, 2 * FF), 1), ("ffn1_w_out", (FF, D), 0), ("w_in", (D, 4800), 1),
    ("w_hg_branch", (HGW, D), 0), ("w_q_up", (QL, NH * QK), 1), ("w_kv_up", (KVL, NH * 2 * HD), 1),
    ("w_mla_branch", (NH * HD, D), 0), ("w_merge", (D, 2 * D), 1), ("w_out", (D, D), 0),
    ("ffn2_w_in", (D, 2 * FF), 1), ("ffn2_w_out", (FF, D), 0),
)
SMALL = (
    ("ffn1_norm", (1, D)), ("mix_norm", (1, D)), ("hg_lb_table", (2, HGW)), ("hg_out_norm", (1, HD)),
    ("mla_q_lora_norm", (1, QL)), ("mla_kv_lora_norm", (1, KVL)), ("q_head_norm", (1, QK)),
    ("k_head_norm", (1, QK)), ("b_merge", (1, 2 * D)), ("ffn2_norm", (1, D)), ("final_norm", (1, D)),
)
WEIGHT_ORDER = ("ffn1_norm", "ffn1_w_in", "ffn1_w_out", "mix_norm", "w_in", "hg_lb_table", "hg_out_norm",
                "w_hg_branch", "mla_q_lora_norm", "w_q_up", "mla_kv_lora_norm", "w_kv_up", "q_head_norm",
                "k_head_norm", "w_mla_branch", "w_merge", "b_merge", "w_out", "ffn2_norm", "ffn2_w_in",
                "ffn2_w_out", "final_norm")


def _pick(n, cands):
    for c in cands:
        if n % c == 0:
            return c
    return n


def _params(sem):
    return pltpu.CompilerParams(dimension_semantics=sem, vmem_limit_bytes=VMEM_LIMIT)


def _sig(x):
    return 1.0 / (1.0 + jnp.exp(-x))


def _dot(a, b):
    return jnp.dot(a.astype(BF16), b.astype(BF16), preferred_element_type=F32)


def _dot_nt(a, b):
    return lax.dot_general(a.astype(BF16), b.astype(BF16), (((1,), (1,)), ((), ())),
                           preferred_element_type=F32)


def _dot_tn(a, b):
    return lax.dot_general(a.astype(BF16), b.astype(BF16), (((0,), (0,)), ((), ())),
                           preferred_element_type=F32)


def _split3(x):
    x1 = x.astype(BF16)
    r1 = x - x1.astype(F32)
    x2 = r1.astype(BF16)
    x3 = (r1 - x2.astype(F32)).astype(BF16)
    return x1, x2, x3


def _dot_sel(m, x):
    x1, x2, x3 = _split3(x)
    d = lambda p: jnp.dot(m, p, preferred_element_type=F32)
    return d(x1) + d(x2) + d(x3)


def _sel_dot(x, m):
    x1, x2, x3 = _split3(x)
    d = lambda p: jnp.dot(p, m, preferred_element_type=F32)
    return d(x1) + d(x2) + d(x3)


_TN = (1408, 1024, 768, 512, 384, 256, 128)


def _mm(a, b, mode, name, out_dtype=F32):
    if mode == "tn":
        t, m = a.shape
        n = b.shape[1]
        tt, tm, tn = _pick(t, (512, 256, 128)), _pick(m, _TN), _pick(n, _TN)

        def body(a_ref, b_ref, o_ref):
            @pl.when(pl.program_id(2) == 0)
            def _():
                o_ref[...] = jnp.zeros_like(o_ref)

            o_ref[...] += _dot_tn(a_ref[...], b_ref[...])

        return pl.pallas_call(
            body, name=name, grid=(m // tm, n // tn, t // tt),
            in_specs=[pl.BlockSpec((tt, tm), lambda i, j, k: (k, i)),
                      pl.BlockSpec((tt, tn), lambda i, j, k: (k, j))],
            out_specs=pl.BlockSpec((tm, tn), lambda i, j, k: (i, j)),
            out_shape=jax.ShapeDtypeStruct((m, n), F32),
            compiler_params=_params(("parallel", "parallel", "arbitrary")),
        )(a, b)

    m, k = a.shape
    tm = _pick(m, ROW_TILES)
    if mode == "nn":
        n = b.shape[1]
        tn = _pick(n, _TN)
        b_spec = pl.BlockSpec((k, tn), lambda i, j: (0, j))
        dot = _dot
    else:
        n = b.shape[0]
        tn = _pick(n, _TN if k <= 4096 else (512, 256, 128))
        b_spec = pl.BlockSpec((tn, k), lambda i, j: (j, 0))
        dot = _dot_nt

    def body(a_ref, b_ref, o_ref):
        o_ref[...] = dot(a_ref[...], b_ref[...]).astype(o_ref.dtype)

    return pl.pallas_call(
        body, name=name, grid=(m // tm, n // tn),
        in_specs=[pl.BlockSpec((tm, k), lambda i, j: (i, 0)), b_spec],
        out_specs=pl.BlockSpec((tm, tn), lambda i, j: (i, j)),
        out_shape=jax.ShapeDtypeStruct((m, n), out_dtype),
        compiler_params=_params(("parallel", "parallel")),
    )(a, b)


_DOTS = {"nn": _dot, "nt": _dot_nt, "tn": _dot_tn}
_BS = pl.BlockSpec


def _mmcall(name, kind, a, b, a_spec, b_spec, o_spec, o_shape, grid, red_axis=None, out_dtype=F32):
    dot = _DOTS[kind]

    def body(a_ref, b_ref, o_ref):
        if red_axis is None:
            o_ref[...] = dot(a_ref[...], b_ref[...]).astype(o_ref.dtype)
        else:
            @pl.when(pl.program_id(red_axis) == 0)
            def _():
                o_ref[...] = jnp.zeros_like(o_ref)

            o_ref[...] += dot(a_ref[...], b_ref[...])

    sem = tuple("arbitrary" if ax == red_axis else "parallel" for ax in range(len(grid)))
    return pl.pallas_call(
        body, name=name, grid=grid, in_specs=[a_spec, b_spec], out_specs=o_spec,
        out_shape=jax.ShapeDtypeStruct(o_shape, out_dtype), compiler_params=_params(sem),
    )(a, b)


def _mm_stack_red(a, w, name, kind):
    s, t, n = a.shape
    nout = w.shape[2] if kind == "nn" else w.shape[1]
    tm = _pick(t, ROW_TILES)
    return _mmcall(name, kind, a, w, _BS((None, tm, n), lambda i, j: (j, i, 0)),
                   _BS((None,) + w.shape[1:], lambda i, j: (j, 0, 0)), _BS((tm, nout), lambda i, j: (i, 0)),
                   (t, nout), (t // tm, s), red_axis=1)


def _mm_stack_tn(a, b, name):
    grp = 4
    if a.ndim == 2:
        t, k = a.shape
        s, _, n = b.shape
        tt = _pick(t, (512, 256, 128))
        a_spec = _BS((tt, k), lambda j, r: (r, 0))
        b_spec, b_in = _BS((None, grp, tt, n), lambda j, r: (j, 0, r, 0)), b.reshape(s // grp, grp, t, n)
        a_in = a
    else:
        s, t, k = a.shape
        n = b.shape[1]
        tt = _pick(t, (512, 256, 128))
        a_spec, a_in = _BS((None, grp, tt, k), lambda j, r: (j, 0, r, 0)), a.reshape(s // grp, grp, t, k)
        b_spec, b_in = _BS((tt, n), lambda j, r: (r, 0)), b

    def body(a_ref, b_ref, o_ref):
        @pl.when(pl.program_id(1) == 0)
        def _():
            o_ref[...] = jnp.zeros_like(o_ref)

        shared = a_ref[...] if a.ndim == 2 else b_ref[...]
        for e in range(grp):
            o_ref[e] += _dot_tn(shared, b_ref[e]) if a.ndim == 2 else _dot_tn(a_ref[e], shared)

    return pl.pallas_call(
        body, name=name, grid=(s // grp, t // tt), in_specs=[a_spec, b_spec],
        out_specs=_BS((None, grp, k, n), lambda j, r: (j, 0, 0, 0)),
        out_shape=jax.ShapeDtypeStruct((s // grp, grp, k, n), F32),
        compiler_params=_params(("parallel", "arbitrary")),
    )(a_in, b_in).reshape(s, k, n)


def _cols_fwd(x, w, name):
    t, k = x.shape
    s, _, n = w.shape
    tm = _pick(t, ROW_TILES)

    def body(x_ref, w_ref, o_ref):
        x_ = x_ref[...]
        for j in range(s):
            o_ref[:, n * j:n * (j + 1)] = _dot(x_, w_ref[j])

    return pl.pallas_call(
        body, name=name, grid=(t // tm,),
        in_specs=[_BS((tm, k), lambda i: (i, 0)), _BS((s, k, n), lambda i: (0, 0, 0))],
        out_specs=_BS((tm, s * n), lambda i: (i, 0)), out_shape=jax.ShapeDtypeStruct((t, s * n), F32),
        compiler_params=_params(("parallel",)),
    )(x, w)


def _cols_dx(d, w, name):
    t = d.shape[0]
    s, k, n = w.shape
    tm = _pick(t, ROW_TILES)

    def body(d_ref, w_ref, o_ref):
        acc = _dot_nt(d_ref[:, 0:n], w_ref[0])
        for j in range(1, s):
            acc = acc + _dot_nt(d_ref[:, n * j:n * (j + 1)], w_ref[j])
        o_ref[...] = acc

    return pl.pallas_call(
        body, name=name, grid=(t // tm,),
        in_specs=[_BS((tm, s * n), lambda i: (i, 0)), _BS((s, k, n), lambda i: (0, 0, 0))],
        out_specs=_BS((tm, k), lambda i: (i, 0)), out_shape=jax.ShapeDtypeStruct((t, k), F32),
        compiler_params=_params(("parallel",)),
    )(d, w)


def _cols_dw(x, d, n, name):
    t, k = x.shape
    s = d.shape[1] // n
    tt = _pick(t, (512, 256, 128))

    def body(x_ref, d_ref, o_ref):
        @pl.when(pl.program_id(0) == 0)
        def _():
            o_ref[...] = jnp.zeros_like(o_ref)

        x_ = x_ref[...]
        for j in range(s):
            o_ref[j] += _dot_tn(x_, d_ref[:, n * j:n * (j + 1)])

    return pl.pallas_call(
        body, name=name, grid=(t // tt,),
        in_specs=[_BS((tt, k), lambda r: (r, 0)), _BS((tt, s * n), lambda r: (r, 0))],
        out_specs=_BS((s, k, n), lambda r: (0, 0, 0)), out_shape=jax.ShapeDtypeStruct((s, k, n), F32),
        compiler_params=_params(("arbitrary",)),
    )(x, d)


def _rows(fn, name, t, tm, ins, vecs, outs, accs=()):
    n_in, n_out, n_acc = len(ins) + len(vecs), len(outs), len(accs)

    def body(*refs):
        res = fn(*refs[:n_in + n_out])
        if n_acc:
            acc_refs = refs[n_in + n_out:]

            @pl.when(pl.program_id(0) == 0)
            def _():
                for r in acc_refs:
                    r[...] = jnp.zeros_like(r)

            for r, val in zip(acc_refs, res):
                r[...] += val

    in_specs = [pl.BlockSpec((tm, bw), functools.partial(lambda i, cb: (i, cb), cb=cb)) for _, bw, cb in ins]
    in_specs += [pl.BlockSpec(v.shape, lambda i: (0, 0)) for v in vecs]
    out_specs = [pl.BlockSpec((tm, w), lambda i: (i, 0)) for w, _ in outs]
    out_specs += [pl.BlockSpec(s, lambda i: (0, 0)) for s in accs]
    out_shape = [jax.ShapeDtypeStruct((t, w), dt) for w, dt in outs]
    out_shape += [jax.ShapeDtypeStruct(s, F32) for s in accs]
    return pl.pallas_call(
        body, name=name, grid=(t // tm,), in_specs=in_specs, out_specs=out_specs, out_shape=out_shape,
        compiler_params=_params(("arbitrary",) if n_acc else ("parallel",)),
    )(*[a for a, _, _ in ins], *vecs)


def _rms(x, g):
    return x * lax.rsqrt(jnp.mean(x * x, axis=-1, keepdims=True) + EPS) * g


def _rms_bwd(x, g, dy):
    xh = x * lax.rsqrt(jnp.mean(x * x, axis=-1, keepdims=True) + EPS)
    r = lax.rsqrt(jnp.mean(x * x, axis=-1, keepdims=True) + EPS)
    dyg = dy * g
    dx = r * (dyg - xh * jnp.mean(dyg * xh, axis=-1, keepdims=True))
    return dx, jnp.sum(dy * xh, axis=0, keepdims=True)


def _hgrn_mats():
    row = lax.broadcasted_iota(jnp.int32, (HB, HB), 0)
    col = lax.broadcasted_iota(jnp.int32, (HB, HB), 1)
    return row, col


def _hgrn_gates(qr, z, t0, t1):
    lb = 1.0 / (1.0 + jnp.exp(t1 - t0))
    sz = _sig(z)
    sneg = 1.0 / (1.0 + jnp.exp(z))
    f = lb + (1.0 - lb) * sz
    return lb, sz, sneg, f, jnp.log(f), (1.0 - lb) * sneg, qr * _sig(qr)


def _hgrn_scores(q, k, cum):
    mids = [cum[SUB * i + SUB // 2 - 1:SUB * i + SUB // 2, :] for i in range(HB // SUB)]
    cmid = jnp.concatenate([cum[SUB * i:SUB * (i + 1)] - mids[i] for i in range(HB // SUB)], axis=0)
    qd = q * jnp.exp(jnp.minimum(cmid, EXP_CLAMP))
    qd_b = qd.astype(BF16)
    kds, parts = [], []
    for i in range(HB // SUB):
        kd = k * jnp.exp(jnp.minimum(mids[i] - cum, EXP_CLAMP))
        kds.append(kd)
        parts.append(_dot_nt(qd_b[SUB * i:SUB * (i + 1)], kd))
    return qd, kds, jnp.concatenate(parts, axis=0), cmid


def _hgrn_fwd(p_hg, table, gain, t):
    nblk = t // HB

    def body(q_ref, f_ref, i_ref, g_ref, tab_ref, gain_ref, o_ref, y_ref, st_ref, state):
        @pl.when(pl.program_id(0) == 0)
        def _():
            state[...] = jnp.zeros_like(state)

        row, col = _hgrn_mats()
        causal = col <= row
        tri = causal.astype(BF16)
        for h in range(NH):
            sl = slice(HD * h, HD * (h + 1))
            v, gr = i_ref[:, sl], g_ref[:, sl]
            _, _, _, _, lf, k, q = _hgrn_gates(q_ref[:, sl], f_ref[:, sl], tab_ref[0:1, sl], tab_ref[1:2, sl])
            cum = _dot_sel(tri, lf)
            _, _, s, _ = _hgrn_scores(q, k, cum)
            p = jnp.where(causal, s, 0.0)
            st = state[h]
            st_ref[h, 0] = st
            o = _dot(p, v) + _dot_nt(q * jnp.exp(cum), st)
            last = cum[HB - 1:HB, :]
            state[h] = st * jnp.exp(last) + _dot_tn(v, k * jnp.exp(last - cum))
            o_ref[:, sl] = o
            y_ref[:, sl] = (_rms(o, gain_ref[...]) * gr * _sig(gr)).astype(BF16)

    blk = lambda cb: pl.BlockSpec((HB, HGW), functools.partial(lambda n, cb: (n, cb), cb=cb))
    return pl.pallas_call(
        body, name="hgrn_fwd", grid=(nblk,),
        in_specs=[blk(0), blk(1), blk(2), blk(3), pl.BlockSpec((2, HGW), lambda n: (0, 0)),
                  pl.BlockSpec((1, HD), lambda n: (0, 0))],
        out_specs=[pl.BlockSpec((HB, HGW), lambda n: (n, 0)), pl.BlockSpec((HB, HGW), lambda n: (n, 0)),
                   pl.BlockSpec((NH, 1, HD, HD), lambda n: (0, n, 0, 0))],
        out_shape=[jax.ShapeDtypeStruct((t, HGW), F32), jax.ShapeDtypeStruct((t, HGW), BF16),
                   jax.ShapeDtypeStruct((NH, nblk, HD, HD), F32)],
        scratch_shapes=[pltpu.VMEM((NH, HD, HD), F32)],
        compiler_params=_params(("arbitrary",)),
    )(p_hg, p_hg, p_hg, p_hg, table, gain)


def _hgrn_bwd(p_hg, table, gain, o_pre, states, dy, t):
    nblk = t // HB

    def body(q_ref, f_ref, i_ref, g_ref, tab_ref, gain_ref, o_ref, st_ref, dy_ref, dp_ref, dtab_ref, dgain_ref,
             dstate):
        @pl.when(pl.program_id(0) == 0)
        def _():
            dstate[...] = jnp.zeros_like(dstate)
            dtab_ref[...] = jnp.zeros_like(dtab_ref)
            dgain_ref[...] = jnp.zeros_like(dgain_ref)

        row, col = _hgrn_mats()
        causal = col <= row
        tri = causal.astype(BF16)
        tri_t = (row <= col).astype(BF16)
        dgain = jnp.zeros((1, HD), F32)
        for h in range(NH):
            sl = slice(HD * h, HD * (h + 1))
            qr, z, v, gr = q_ref[:, sl], f_ref[:, sl], i_ref[:, sl], g_ref[:, sl]
            lb, sz, sneg, f, lf, k, q = _hgrn_gates(qr, z, tab_ref[0:1, sl], tab_ref[1:2, sl])
            cum = _dot_sel(tri, lf)
            qd, kds, s, cmid = _hgrn_scores(q, k, cum)
            p = jnp.where(causal, s, 0.0)
            st = st_ref[h, 0]
            dst = dstate[h]
            o = o_ref[:, sl]
            sg = _sig(gr)
            dyh = dy_ref[:, sl]
            on = _rms(o, gain_ref[...])
            dgr = dyh * on * sg * (1.0 + gr * (1.0 - sg))
            do, dg_h = _rms_bwd(o, gain_ref[...], dyh * gr * sg)
            dgain = dgain + dg_h
            do_b = do.astype(BF16)
            ecum = jnp.exp(cum)
            qc = q * ecum
            last = cum[HB - 1:HB, :]
            edec = jnp.exp(last - cum)
            kdec = k * edec
            dp = jnp.where(causal, _dot_nt(do_b, v), 0.0)
            dv = _dot(p.T, do_b) + _dot_nt(kdec, dst)
            dqc = _dot(do_b, st)
            dkdec = _dot(v, dst)
            dstate[h] = dst * jnp.exp(last) + _dot(do.T, qc)
            dp_b = dp.astype(BF16)
            dqd = jnp.concatenate([_dot(dp_b[SUB * i:SUB * (i + 1)], kds[i]) for i in range(HB // SUB)], axis=0)
            gq = dqd * qd
            dq = dqd * jnp.exp(jnp.minimum(cmid, EXP_CLAMP)) + dqc * ecum
            gs = dkdec * kdec
            dk = dkdec * edec
            dcum = dqc * qc - gs + gq
            dcum = dcum + jnp.where(row == HB - 1, jnp.sum(gs, axis=0, keepdims=True)
                                    + jnp.exp(last) * jnp.sum(st * dst, axis=0, keepdims=True), 0.0)
            qd_b = qd.astype(BF16)
            for i in range(HB // SUB):
                dkd = _dot_tn(dp_b[SUB * i:SUB * (i + 1)], qd_b[SUB * i:SUB * (i + 1)])
                mid = cum[SUB * i + SUB // 2 - 1:SUB * i + SUB // 2, :]
                dk = dk + dkd * jnp.exp(jnp.minimum(mid - cum, EXP_CLAMP))
                gk = dkd * kds[i]
                to_mid = jnp.sum(gk, axis=0, keepdims=True) - jnp.sum(gq[SUB * i:SUB * (i + 1)], axis=0, keepdims=True)
                dcum = dcum - gk + jnp.where(row == SUB * i + SUB // 2 - 1, to_mid, 0.0)
            dlf = _dot_sel(tri_t, dcum)
            df = dlf / f - dk
            dz = df * (1.0 - lb) * sz * sneg
            dlb = jnp.sum(df * sneg, axis=0, keepdims=True) * lb * (1.0 - lb)
            dtab_ref[0:1, sl] += dlb
            dtab_ref[1:2, sl] -= dlb
            sq = _sig(qr)
            dp_ref[:, sl] = (dq * sq * (1.0 + qr * (1.0 - sq))).astype(BF16)
            dp_ref[:, HGW + HD * h:HGW + HD * (h + 1)] = dz.astype(BF16)
            dp_ref[:, 2 * HGW + HD * h:2 * HGW + HD * (h + 1)] = dv.astype(BF16)
            dp_ref[:, 3 * HGW + HD * h:3 * HGW + HD * (h + 1)] = dgr.astype(BF16)
        dgain_ref[...] += dgain

    rev = lambda cb: pl.BlockSpec((HB, HGW), functools.partial(lambda n, cb: (nblk - 1 - n, cb), cb=cb))
    return pl.pallas_call(
        body, name="hgrn_bwd", grid=(nblk,),
        in_specs=[rev(0), rev(1), rev(2), rev(3), pl.BlockSpec((2, HGW), lambda n: (0, 0)),
                  pl.BlockSpec((1, HD), lambda n: (0, 0)), rev(0),
                  pl.BlockSpec((NH, 1, HD, HD), lambda n: (0, nblk - 1 - n, 0, 0)), rev(0)],
        out_specs=[pl.BlockSpec((HB, 4 * HGW), lambda n: (nblk - 1 - n, 0)),
                   pl.BlockSpec((2, HGW), lambda n: (0, 0)), pl.BlockSpec((1, HD), lambda n: (0, 0))],
        out_shape=[jax.ShapeDtypeStruct((t, 4 * HGW), BF16), jax.ShapeDtypeStruct((2, HGW), F32),
                   jax.ShapeDtypeStruct((1, HD), F32)],
        scratch_shapes=[pltpu.VMEM((NH, HD, HD), F32)],
        compiler_params=_params(("arbitrary",)),
    )(p_hg, p_hg, p_hg, p_hg, table, gain, o_pre, states, dy)


def _rope_mat():
    r = lax.broadcasted_iota(jnp.int32, (ROPE, ROPE), 0)
    c = lax.broadcasted_iota(jnp.int32, (ROPE, ROPE), 1)
    half = ROPE // 2
    return ((r == c - half).astype(F32) - (r == c + half).astype(F32)).astype(BF16)


def _mla_prep_fwd(p_mla, cs, sn, wq, wkv, gql, gkvl, gq, gk, t):
    tm = _pick(t, (512, 256, 128))

    def body(p_ref, cs_ref, sn_ref, wq_ref, wkv_ref, gql_ref, gkvl_ref, gq_ref, gk_ref,
             q_ref, k_ref, v_ref):
        rmat = _rope_mat()
        cqn = _rms(p_ref[:, 0:QL], gql_ref[...]).astype(BF16)
        ckvn = _rms(p_ref[:, QL:QL + KVL], gkvl_ref[...]).astype(BF16)
        kpe = p_ref[:, QL + KVL:QL + KVL + ROPE]
        c, s = cs_ref[...], sn_ref[...]
        rot = lambda x: x * c + _sel_dot(x, rmat) * s
        for h in range(NH):
            qa, qr = _dot(cqn, wq_ref[h, :, 0:HD]), _dot(cqn, wq_ref[h, :, HD:QK])
            rq = lax.rsqrt((jnp.sum(qa * qa, -1, keepdims=True) + jnp.sum(qr * qr, -1, keepdims=True)) / QK + EPS)
            q_ref[h, :, 0:HD] = (qa * rq * gq_ref[:, 0:HD] * SCALE).astype(BF16)
            q_ref[h, :, HD:QK] = (rot(qr * rq * gq_ref[:, HD:QK]) * SCALE).astype(BF16)
            kn = _dot(ckvn, wkv_ref[h, :, 0:HD])
            rk = lax.rsqrt((jnp.sum(kn * kn, -1, keepdims=True) + jnp.sum(kpe * kpe, -1, keepdims=True)) / QK + EPS)
            k_ref[h, :, 0:HD] = (kn * rk * gk_ref[:, 0:HD]).astype(BF16)
            k_ref[h, :, HD:QK] = rot(kpe * rk * gk_ref[:, HD:QK]).astype(BF16)
            v_ref[h] = _dot(ckvn, wkv_ref[h, :, HD:2 * HD]).astype(BF16)

    whole = lambda a: pl.BlockSpec(a.shape, functools.partial(lambda i, nd: (0,) * nd, nd=a.ndim))
    return pl.pallas_call(
        body, name="mla_prep_fwd", grid=(t // tm,),
        in_specs=[pl.BlockSpec((tm, QL + KVL + ROPE), lambda i: (i, 0)), pl.BlockSpec((tm, ROPE), lambda i: (i, 0)),
                  pl.BlockSpec((tm, ROPE), lambda i: (i, 0))] + [whole(a) for a in (wq, wkv, gql, gkvl, gq, gk)],
        out_specs=[pl.BlockSpec((NH, tm, QK), lambda i: (0, i, 0)), pl.BlockSpec((NH, tm, QK), lambda i: (0, i, 0)),
                   pl.BlockSpec((NH, tm, HD), lambda i: (0, i, 0))],
        out_shape=[jax.ShapeDtypeStruct((NH, t, QK), BF16), jax.ShapeDtypeStruct((NH, t, QK), BF16),
                   jax.ShapeDtypeStruct((NH, t, HD), BF16)],
        compiler_params=_params(("parallel",)),
    )(p_mla, cs, sn, wq, wkv, gql, gkvl, gq, gk)


def _mla_prep_bwd(p_mla, cs, sn, wq, wkv, gql, gkvl, gq, gk, dq, dk, dv, t):
    tm = _pick(t, (512, 256, 128))

    def body(p_ref, cs_ref, sn_ref, wq_ref, wkv_ref, gql_ref, gkvl_ref, gq_ref, gk_ref,
             dq_ref, dk_ref, dv_ref,
             dp_ref, dwq_ref, dwkv_ref, dgql_ref, dgkvl_ref, dgq_ref, dgk_ref):
        accs = (dwq_ref, dwkv_ref, dgql_ref, dgkvl_ref, dgq_ref, dgk_ref)

        @pl.when(pl.program_id(0) == 0)
        def _():
            for r in accs:
                r[...] = jnp.zeros_like(r)

        rmat = _rope_mat()
        rmat_t = -rmat
        cq, ckv = p_ref[:, 0:QL], p_ref[:, QL:QL + KVL]
        kpe = p_ref[:, QL + KVL:QL + KVL + ROPE]
        cqn_f, ckvn_f = _rms(cq, gql_ref[...]), _rms(ckv, gkvl_ref[...])
        cqn, ckvn = cqn_f.astype(BF16), ckvn_f.astype(BF16)
        cqn_t, ckvn_t = cqn_f.T.astype(BF16), ckvn_f.T.astype(BF16)
        c, s = cs_ref[...], sn_ref[...]
        unrot = lambda dy: dy * c + _sel_dot(dy * s, rmat_t)
        dcqn = jnp.zeros((tm, QL), F32)
        dckvn = jnp.zeros((tm, KVL), F32)
        dkpe = jnp.zeros((tm, ROPE), F32)
        dgq_a, dgq_r = jnp.zeros((1, HD), F32), jnp.zeros((1, ROPE), F32)
        dgk_a, dgk_r = jnp.zeros((1, HD), F32), jnp.zeros((1, ROPE), F32)
        for h in range(NH):
            qa, qr = _dot(cqn, wq_ref[h, :, 0:HD]), _dot(cqn, wq_ref[h, :, HD:QK])
            rq = lax.rsqrt((jnp.sum(qa * qa, -1, keepdims=True) + jnp.sum(qr * qr, -1, keepdims=True)) / QK + EPS)
            xa, xr = qa * rq, qr * rq
            dya = dq_ref[h, :, 0:HD] * SCALE
            dyr = unrot(dq_ref[h, :, HD:QK] * SCALE)
            dgq_a += jnp.sum(dya * xa, axis=0, keepdims=True)
            dgq_r += jnp.sum(dyr * xr, axis=0, keepdims=True)
            ga, gr_ = dya * gq_ref[:, 0:HD], dyr * gq_ref[:, HD:QK]
            mean = (jnp.sum(ga * xa, -1, keepdims=True) + jnp.sum(gr_ * xr, -1, keepdims=True)) / QK
            dqa = (rq * (ga - xa * mean)).astype(BF16)
            dqr = (rq * (gr_ - xr * mean)).astype(BF16)
            dwq_ref[h, :, 0:HD] += _dot(cqn_t, dqa)
            dwq_ref[h, :, HD:QK] += _dot(cqn_t, dqr)
            dcqn += _dot_nt(dqa, wq_ref[h, :, 0:HD]) + _dot_nt(dqr, wq_ref[h, :, HD:QK])
            kn = _dot(ckvn, wkv_ref[h, :, 0:HD])
            rk = lax.rsqrt((jnp.sum(kn * kn, -1, keepdims=True) + jnp.sum(kpe * kpe, -1, keepdims=True)) / QK + EPS)
            ya, yr = kn * rk, kpe * rk
            dka = dk_ref[h, :, 0:HD]
            dkr = unrot(dk_ref[h, :, HD:QK])
            dgk_a += jnp.sum(dka * ya, axis=0, keepdims=True)
            dgk_r += jnp.sum(dkr * yr, axis=0, keepdims=True)
            ha, hr = dka * gk_ref[:, 0:HD], dkr * gk_ref[:, HD:QK]
            mean = (jnp.sum(ha * ya, -1, keepdims=True) + jnp.sum(hr * yr, -1, keepdims=True)) / QK
            dkn = (rk * (ha - ya * mean)).astype(BF16)
            dkpe += rk * (hr - yr * mean)
            dvh = dv_ref[h].astype(BF16)
            dwkv_ref[h, :, 0:HD] += _dot(ckvn_t, dkn)
            dwkv_ref[h, :, HD:2 * HD] += _dot(ckvn_t, dvh)
            dckvn += _dot_nt(dkn, wkv_ref[h, :, 0:HD]) + _dot_nt(dvh, wkv_ref[h, :, HD:2 * HD])
        dcq, dg1 = _rms_bwd(cq, gql_ref[...], dcqn)
        dckv, dg2 = _rms_bwd(ckv, gkvl_ref[...], dckvn)
        dp_ref[:, 0:QL] = dcq.astype(BF16)
        dp_ref[:, QL:QL + KVL] = dckv.astype(BF16)
        dp_ref[:, QL + KVL:QL + KVL + ROPE] = dkpe.astype(BF16)
        dgql_ref[...] += dg1
        dgkvl_ref[...] += dg2
        dgq_ref[:, 0:HD] += dgq_a
        dgq_ref[:, HD:QK] += dgq_r
        dgk_ref[:, 0:HD] += dgk_a
        dgk_ref[:, HD:QK] += dgk_r

    whole = lambda a: pl.BlockSpec(a.shape, functools.partial(lambda i, nd: (0,) * nd, nd=a.ndim))
    acc_shapes = [wq.shape, wkv.shape, gql.shape, gkvl.shape, gq.shape, gk.shape]
    return pl.pallas_call(
        body, name="mla_prep_bwd", grid=(t // tm,),
        in_specs=[pl.BlockSpec((tm, QL + KVL + ROPE), lambda i: (i, 0)), pl.BlockSpec((tm, ROPE), lambda i: (i, 0)),
                  pl.BlockSpec((tm, ROPE), lambda i: (i, 0))]
        + [whole(a) for a in (wq, wkv, gql, gkvl, gq, gk)]
        + [pl.BlockSpec((NH, tm, QK), lambda i: (0, i, 0)), pl.BlockSpec((NH, tm, QK), lambda i: (0, i, 0)),
           pl.BlockSpec((NH, tm, HD), lambda i: (0, i, 0))],
        out_specs=[pl.BlockSpec((tm, QL + KVL + ROPE), lambda i: (i, 0))]
        + [pl.BlockSpec(s, functools.partial(lambda i, nd: (0,) * nd, nd=len(s))) for s in acc_shapes],
        out_shape=[jax.ShapeDtypeStruct((t, QL + KVL + ROPE), BF16)]
        + [jax.ShapeDtypeStruct(s, F32) for s in acc_shapes],
        compiler_params=_params(("arbitrary",)),
    )(p_mla, cs, sn, wq, wkv, gql, gkvl, gq, gk, dq, dk, dv)


def _chunk_mask(nq, nk, key0, keys_on_rows):
    shape = (nk, nq) if keys_on_rows else (nq, nk)
    qi = lax.broadcasted_iota(jnp.int32, shape, 1 if keys_on_rows else 0) // CHUNK
    ki = lax.broadcasted_iota(jnp.int32, shape, 0 if keys_on_rows else 1) // CHUNK + key0 // CHUNK
    return ki <= qi


def _flash_fwd(q, k, v, t):
    tq = _pick(t, ATT_Q)
    tk = tq // ATT_KEY_TILES

    def body(q_ref, k_ref, v_ref, o_ref, lse_ref):
        i = pl.program_id(1)
        qt = q_ref[0]

        def step(j, carry, key0):
            m, l, acc = carry
            cols = pl.ds(pl.multiple_of(j * tk, tk), tk)
            s = _dot_nt(qt, k_ref[0, cols, :])
            if key0 is not None:
                s = jnp.where(_chunk_mask(tq, tk, key0, False), s, -jnp.inf)
            m_new = jnp.maximum(m, jnp.max(s, axis=-1, keepdims=True))
            p = jnp.exp(s - m_new)
            alpha = jnp.exp(m - m_new)
            return m_new, alpha * l + jnp.sum(p, axis=-1, keepdims=True), alpha * acc + _dot(p, v_ref[0, cols, :])

        init = (jnp.full((tq, 1), -jnp.inf, F32), jnp.zeros((tq, 1), F32), jnp.zeros((tq, HD), F32))
        carry = lax.fori_loop(0, ATT_KEY_TILES * i, lambda j, cr: step(j, cr, None), init)
        for h in range(ATT_KEY_TILES):
            carry = step(ATT_KEY_TILES * i + h, carry, h * tk)
        m, l, acc = carry
        o_ref[...] = (acc / l).astype(BF16)
        lse_ref[0] = m + jnp.log(l)

    return pl.pallas_call(
        body, name="flash_fwd", grid=(NH, t // tq),
        in_specs=[pl.BlockSpec((1, tq, QK), lambda h, i: (h, i, 0)), pl.BlockSpec((1, t, QK), lambda h, i: (h, 0, 0)),
                  pl.BlockSpec((1, t, HD), lambda h, i: (h, 0, 0))],
        out_specs=[pl.BlockSpec((tq, HD), lambda h, i: (i, h)), pl.BlockSpec((1, tq, 1), lambda h, i: (h, i, 0))],
        out_shape=[jax.ShapeDtypeStruct((t, NH * HD), BF16), jax.ShapeDtypeStruct((NH, t, 1), F32)],
        compiler_params=_params(("parallel", "parallel")),
    )(q, k, v)


def _flash_delta(o, do, t):
    tq = _pick(t, ATT_TILES)

    def body(o_ref, do_ref, delta_ref):
        prod = do_ref[...].astype(F32) * o_ref[...].astype(F32)
        x1, x2, x3 = _split3(prod)
        ones = jnp.ones((8, HD), BF16)
        d = lambda p: lax.dot_general(ones, p, (((1,), (1,)), ((), ())), preferred_element_type=F32)
        delta_ref[0] = (d(x1) + d(x2) + d(x3))[0:1, :]

    return pl.pallas_call(
        body, name="flash_delta", grid=(NH, t // tq),
        in_specs=[pl.BlockSpec((tq, HD), lambda h, i: (i, h)), pl.BlockSpec((tq, HD), lambda h, i: (i, h))],
        out_specs=pl.BlockSpec((1, 1, tq), lambda h, i: (h, 0, i)),
        out_shape=jax.ShapeDtypeStruct((NH, 1, t), F32), compiler_params=_params(("parallel", "parallel")),
    )(o, do)


def _flash_bwd(q, k, v, do, lse_row, delta_row, t):
    tq = _pick(t, ATT_Q)
    tk = tq // ATT_KEY_TILES

    def body(q_ref, k_ref, v_ref, do_ref, lse_ref, delta_ref, dq_ref, dk_ref, dv_ref):
        j = pl.program_id(1)

        @pl.when(j == 0)
        def _():
            dq_ref[...] = jnp.zeros_like(dq_ref)

        kt, vt = k_ref[0], v_ref[0]

        def step(i, carry, key0):
            dk, dv = carry
            rows = pl.ds(pl.multiple_of(i * tq, tq), tq)
            qt, dot_ = q_ref[0, rows, :], do_ref[rows, :]
            p = jnp.exp(_dot_nt(kt, qt) - lse_ref[0, :, rows])
            if key0 is not None:
                p = jnp.where(_chunk_mask(tq, tk, key0, True), p, 0.0)
            ds = (p * (_dot_nt(vt, dot_) - delta_ref[0, :, rows])).astype(BF16)
            dq_ref[0, rows, :] += _dot_tn(ds, kt)
            return dk + _dot(ds, qt), dv + _dot(p, dot_)

        own = j // ATT_KEY_TILES
        carry = step(own, (jnp.zeros((tk, QK), F32), jnp.zeros((tk, HD), F32)), (j % ATT_KEY_TILES) * tk)
        dk, dv = lax.fori_loop(own + 1, t // tq, lambda i, cr: step(i, cr, None), carry)
        dk_ref[0] = dk
        dv_ref[0] = dv

    return pl.pallas_call(
        body, name="flash_bwd", grid=(NH, t // tk),
        in_specs=[pl.BlockSpec((1, t, QK), lambda h, j: (h, 0, 0)), pl.BlockSpec((1, tk, QK), lambda h, j: (h, j, 0)),
                  pl.BlockSpec((1, tk, HD), lambda h, j: (h, j, 0)), pl.BlockSpec((t, HD), lambda h, j: (0, h)),
                  pl.BlockSpec((1, 1, t), lambda h, j: (h, 0, 0)), pl.BlockSpec((1, 1, t), lambda h, j: (h, 0, 0))],
        out_specs=[pl.BlockSpec((1, t, QK), lambda h, j: (h, 0, 0)), pl.BlockSpec((1, tk, QK), lambda h, j: (h, j, 0)),
                   pl.BlockSpec((1, tk, HD), lambda h, j: (h, j, 0))],
        out_shape=[jax.ShapeDtypeStruct((NH, t, QK), F32), jax.ShapeDtypeStruct((NH, t, QK), F32),
                   jax.ShapeDtypeStruct((NH, t, HD), F32)],
        compiler_params=_params(("parallel", "arbitrary")),
    )(q, k, v, do, lse_row, delta_row)


def _ffn_in(xn, w_in, name):
    t, k = xn.shape
    s, _, n = w_in.shape
    tm = _pick(t, ROW_TILES)

    def body(x_ref, w_ref, gu_ref, a_ref):
        x = x_ref[...]
        g, u = _dot(x, w_ref[0]), _dot(x, w_ref[1])
        gu_ref[0] = g.astype(BF16)
        gu_ref[1] = u.astype(BF16)
        a_ref[...] = (g * _sig(g) * u).astype(BF16)

    return pl.pallas_call(
        body, name=name, grid=(t // tm, s // 2),
        in_specs=[_BS((tm, k), lambda i, j: (i, 0)), _BS((2, None, k, n), lambda i, j: (0, j, 0, 0))],
        out_specs=[_BS((2, None, tm, n), lambda i, j: (0, j, i, 0)), _BS((None, tm, n), lambda i, j: (j, i, 0))],
        out_shape=[jax.ShapeDtypeStruct((2, s // 2, t, n), BF16), jax.ShapeDtypeStruct((s // 2, t, n), BF16)],
        compiler_params=_params(("parallel", "parallel")),
    )(xn, w_in.reshape(2, s // 2, k, n))


def _ffn_dgu(dfo, w_out, gu, name):
    t, k = dfo.shape
    s, n, _ = w_out.shape
    tm = _pick(t, ROW_TILES)

    def body(d_ref, w_ref, gu_ref, o_ref):
        da = _dot_nt(d_ref[...], w_ref[...])
        g, u = gu_ref[0].astype(F32), gu_ref[1].astype(F32)
        sg = _sig(g)
        o_ref[0] = (da * u * sg * (1.0 + g * (1.0 - sg))).astype(BF16)
        o_ref[1] = (da * g * sg).astype(BF16)

    pair = _BS((2, None, tm, n), lambda i, j: (0, j, i, 0))
    return pl.pallas_call(
        body, name=name, grid=(t // tm, s),
        in_specs=[_BS((tm, k), lambda i, j: (i, 0)), _BS((None, n, k), lambda i, j: (j, 0, 0)), pair],
        out_specs=pair, out_shape=jax.ShapeDtypeStruct((2, s, t, n), BF16),
        compiler_params=_params(("parallel", "parallel")),
    )(dfo, w_out, gu).reshape(2 * s, t, n)


def _ffn_fwd(xn, w_in, w_out, tag):
    gu, a = _ffn_in(xn, w_in, tag + "_in")
    return gu, a, _mm_stack_red(a, w_out, tag + "_out", "nn")


def _ffn_bwd(dfo, xn, gu, a, w_in, w_out, tag):
    dw_out = _mm_stack_tn(a, dfo, tag + "_dwout")
    dgu = _ffn_dgu(dfo, w_out, gu, tag + "_dgu")
    return _mm_stack_red(dgu, w_in, tag + "_dxn", "nt"), _mm_stack_tn(xn, dgu, tag + "_dwin"), dw_out


def _local_step(x, target, cs, sn, w, late_weights, s, early_grads):
    t = x.shape[0]
    tm = _pick(t, (256, 128))
    g = {}
    ffn_out = lambda n: w[n].reshape(4, FF // 4, D)

    def norm_fn(x_ref, g_ref, o_ref):
        o_ref[...] = _rms(x_ref[...], g_ref[...]).astype(BF16)

    xn1 = _rows(norm_fn, "norm1", t, tm, [(x, D, 0)], [s["ffn1_norm"]], [(D, BF16)])[0]
    gu1, a1, f1 = _ffn_fwd(xn1, w["ffn1_w_in"], ffn_out("ffn1_w_out"), "ffn1")

    def res_norm_fn(scale):
        def fn(h_ref, f_ref, g_ref, h_out, n_out):
            h = h_ref[...] + scale * f_ref[...]
            h_out[...] = h
            n_out[...] = _rms(h, g_ref[...]).astype(BF16)
        return fn

    h1, u = _rows(res_norm_fn(0.5), "res_norm1", t, tm, [(x, D, 0), (f1, D, 0)], [s["mix_norm"]],
                  [(D, F32), (D, BF16)])
    w = {**w, **late_weights(h1)}
    rows_of = lambda n: w[n].reshape(-1, w[n].shape[2])
    w_hgb, w_mlab, w_o = rows_of("w_hg_branch"), rows_of("w_mla_branch"), rows_of("w_out")
    w_in_nat = w["w_in"].transpose(1, 0, 2).reshape(D, -1)
    w_mrg = w["w_merge"]
    mw = w_mrg.shape[2]
    w_in_hg, w_in_mla = w_in_nat[:, :4 * HGW], w_in_nat[:, 4 * HGW:]
    p_hg = _mm(u, w_in_hg, "nn", "proj_hg")
    p_mla = _mm(u, w_in_mla, "nn", "proj_mla")
    gpre = _cols_fwd(u, w_mrg, "proj_gate")
    o_pre, hgy, states = _hgrn_fwd(p_hg, s["hg_lb_table"], s["hg_out_norm"], t)
    prep_args = (p_mla, cs, sn, w["w_q_up"], w["w_kv_up"], s["mla_q_lora_norm"], s["mla_kv_lora_norm"],
                 s["q_head_norm"], s["k_head_norm"])
    q, k, v = _mla_prep_fwd(*prep_args, t)
    att, lse = _flash_fwd(q, k, v, t)
    y_hg = _mm(hgy, w_hgb, "nn", "branch_hg")
    y_mla = _mm(att, w_mlab, "nn", "branch_mla")

    def mix_fn(gh_ref, gm_ref, yh_ref, ym_ref, b_ref, o_ref):
        gh = _sig(gh_ref[...] + b_ref[:, 0:D])
        gm = _sig(gm_ref[...] + b_ref[:, D:2 * D])
        o_ref[...] = (gh * yh_ref[...] + gm * ym_ref[...]).astype(BF16)

    mixed = _rows(mix_fn, "mix", t, tm, [(gpre, D, 0), (gpre, D, 1), (y_hg, D, 0), (y_mla, D, 0)], [s["b_merge"]],
                  [(D, BF16)])[0]
    mo = _mm(mixed, w_o, "nn", "mix_out")
    h2, xn2 = _rows(res_norm_fn(1.0), "res_norm2", t, tm, [(h1, D, 0), (mo, D, 0)], [s["ffn2_norm"]],
                    [(D, F32), (D, BF16)])
    gu2, a2, f2 = _ffn_fwd(xn2, w["ffn2_w_in"], ffn_out("ffn2_w_out"), "ffn2")

    def loss_fn(h_ref, f_ref, tg_ref, g_ref, dh_out, dhb_out):
        h = h_ref[...] + 0.5 * f_ref[...]
        e = _rms(h, g_ref[...]) - tg_ref[...]
        dh, dgain = _rms_bwd(h, g_ref[...], e / D)
        dh_out[...] = dh
        dhb_out[...] = (0.5 * dh).astype(BF16)
        return dgain, jnp.full((1, LANE), 0.5 / D * jnp.sum(e * e), F32)

    dh3, dfo2, g["final_norm"], loss = _rows(loss_fn, "loss", t, tm, [(h2, D, 0), (f2, D, 0), (target, D, 0)],
                                             [s["final_norm"]], [(D, F32), (D, BF16)], [(1, D), (1, LANE)])

    def norm_bwd_fn(scale):
        def fn(h_ref, dxn_ref, dh_ref, g_ref, dh_out, dhb_out):
            dx, dgain = _rms_bwd(h_ref[...], g_ref[...], dxn_ref[...])
            dh = dh_ref[...] + dx
            dh_out[...] = dh
            dhb_out[...] = (scale * dh).astype(BF16)
            return (dgain,)
        return fn

    as_rows = lambda a: a.reshape((N_DEV, -1) + a.shape[-1:])
    dxn2, g["ffn2_w_in"], dwo = _ffn_bwd(dfo2, xn2, gu2, a2, w["ffn2_w_in"], ffn_out("ffn2_w_out"), "ffn2")
    g["ffn2_w_out"] = as_rows(dwo)
    dh2, dh2b, g["ffn2_norm"] = _rows(norm_bwd_fn(1.0), "norm2_bwd", t, tm, [(h2, D, 0), (dxn2, D, 0), (dh3, D, 0)],
                                      [s["ffn2_norm"]], [(D, F32), (D, BF16)], [(1, D)])
    dmixed = _mm(dh2b, w_o, "nt", "mix_out_dx")
    g["w_out"] = as_rows(_mm(mixed, dh2b, "tn", "mix_out_dw"))

    def mix_bwd_fn(gh_ref, gm_ref, yh_ref, ym_ref, dm_ref, b_ref, dyh_out, dym_out, dg_out):
        gh = _sig(gh_ref[...] + b_ref[:, 0:D])
        gm = _sig(gm_ref[...] + b_ref[:, D:2 * D])
        dm = dm_ref[...]
        dyh_out[...] = (dm * gh).astype(BF16)
        dym_out[...] = (dm * gm).astype(BF16)
        dgh = dm * yh_ref[...] * gh * (1.0 - gh)
        dgm = dm * ym_ref[...] * gm * (1.0 - gm)
        dg_out[:, 0:D] = dgh.astype(BF16)
        dg_out[:, D:2 * D] = dgm.astype(BF16)
        return (jnp.concatenate([jnp.sum(dgh, axis=0, keepdims=True), jnp.sum(dgm, axis=0, keepdims=True)], axis=1),)

    dyh, dym, dgpre, g["b_merge"] = _rows(
        mix_bwd_fn, "mix_bwd", t, tm, [(gpre, D, 0), (gpre, D, 1), (y_hg, D, 0), (y_mla, D, 0), (dmixed, D, 0)],
        [s["b_merge"]], [(D, BF16), (D, BF16), (2 * D, BF16)], [(1, 2 * D)])
    g["w_hg_branch"] = as_rows(_mm(hgy, dyh, "tn", "branch_hg_dw"))
    g["w_mla_branch"] = as_rows(_mm(att, dym, "tn", "branch_mla_dw"))
    g["w_merge"] = _cols_dw(u, dgpre, mw, "proj_gate_dw")
    dhgy = _mm(dyh, w_hgb, "nt", "branch_hg_dx")
    datt = _mm(dym, w_mlab, "nt", "branch_mla_dx", out_dtype=BF16)
    du_gate = _cols_dx(dgpre, w_mrg, "proj_gate_dx")

    dq, dk, dv = _flash_bwd(q, k, v, datt, lse.reshape(NH, 1, t), _flash_delta(att, datt, t), t)
    (dp_mla, g["w_q_up"], g["w_kv_up"], g["mla_q_lora_norm"], g["mla_kv_lora_norm"], g["q_head_norm"],
     g["k_head_norm"]) = _mla_prep_bwd(*prep_args, dq, dk, dv, t)
    dp_hg, g["hg_lb_table"], g["hg_out_norm"] = _hgrn_bwd(p_hg, s["hg_lb_table"], s["hg_out_norm"], o_pre, states,
                                                          dhgy, t)
    dw_in_nat = jnp.concatenate([_mm(u, dp_hg, "tn", "proj_hg_dw"), _mm(u, dp_mla, "tn", "proj_mla_dw")], axis=1)
    g["w_in"] = dw_in_nat.reshape(D, N_DEV, -1).transpose(1, 0, 2)
    du_hg = _mm(dp_hg, w_in_hg, "nt", "proj_hg_dx")
    du_mla = _mm(dp_mla, w_in_mla, "nt", "proj_mla_dx")

    def mixnorm_bwd_fn(h_ref, a_ref, b_ref, c_ref, dh_ref, g_ref, dh_out, dhb_out):
        dx, dgain = _rms_bwd(h_ref[...], g_ref[...], a_ref[...] + b_ref[...] + c_ref[...])
        dh = dh_ref[...] + dx
        dh_out[...] = dh
        dhb_out[...] = (0.5 * dh).astype(BF16)
        return (dgain,)

    mix_gain = s["mix_norm"] + early_grads(g)[0:1, 0:1]
    dh1, dfo1, g["mix_norm"] = _rows(mixnorm_bwd_fn, "mixnorm_bwd", t, tm,
                                     [(h1, D, 0), (du_hg, D, 0), (du_mla, D, 0), (du_gate, D, 0), (dh2, D, 0)],
                                     [mix_gain], [(D, F32), (D, BF16)], [(1, D)])
    dxn1, g["ffn1_w_in"], dwo = _ffn_bwd(dfo1, xn1, gu1, a1, w["ffn1_w_in"], ffn_out("ffn1_w_out"), "ffn1")
    g["ffn1_w_out"] = as_rows(dwo)
    grad_x, _, g["ffn1_norm"] = _rows(norm_bwd_fn(1.0), "norm1_bwd", t, tm, [(x, D, 0), (dxn1, D, 0), (dh1, D, 0)],
                                      [s["ffn1_norm"]], [(D, F32), (D, BF16)], [(1, D)])
    return loss, grad_x, g


def _coords():
    return lax.axis_index("x"), lax.axis_index("y"), lax.axis_index("c")


def _hbm_call(body, name, ins, out_shapes, scratch):
    any_spec = pl.BlockSpec(memory_space=pl.ANY)
    return pl.pallas_call(
        body, name=name, out_shape=[jax.ShapeDtypeStruct(s, dt) for s, dt in out_shapes],
        in_specs=[any_spec] * len(ins), out_specs=[any_spec] * len(out_shapes), scratch_shapes=scratch,
    )(*ins)


def _my_slot():
    return 4 * lax.axis_index("x") + 2 * lax.axis_index("y") + lax.axis_index("c")


def _put_own(buf, own, index):
    return lax.dynamic_update_index_in_dim(buf, own, index, 0)


def _all_gather(blocks, name):
    nb = len(blocks)

    def body(*refs):
        x_refs, out_refs = refs[:nb], refs[nb:2 * nb]
        send_sems, recv_sems = refs[2 * nb:]
        x, y, c = _coords()
        me, sibling = (x, y, c), (x, y, 1 - c)
        chips = [(1 - x, y), (x, 1 - y), (1 - x, 1 - y)]

        def slot(b, px, py, pc):
            return out_refs[b].at[4 * px + 2 * py + pc]

        def copy(b, kk, block_of, to, src=None):
            return pltpu.make_async_remote_copy(
                src_ref=slot(b, *block_of) if src is None else src, dst_ref=slot(b, *block_of),
                send_sem=send_sems.at[b, kk], recv_sem=recv_sems.at[b, kk], device_id=to, device_id_type=MESH)

        first = [copy(b, 0, me, sibling, src=x_refs[b]) for b in range(nb)]
        first += [copy(b, 1 + j, me, (*chip, c), src=x_refs[b]) for j, chip in enumerate(chips) for b in range(nb)]
        for cp in first:
            cp.start()
        passed = []
        for j, chip in enumerate(chips):
            for b in range(nb):
                copy(b, 1 + j, (*chip, c), me).wait_recv()
                passed.append(copy(b, 4 + j, (*chip, c), sibling))
                passed[-1].start()
        for b in range(nb):
            copy(b, 0, sibling, me).wait_recv()
        for j, chip in enumerate(chips):
            for b in range(nb):
                copy(b, 4 + j, (*chip, 1 - c), me).wait_recv()
        for cp in first + passed:
            cp.wait_send()

    outs = _hbm_call(body, name, blocks, [((N_DEV,) + b.shape, b.dtype) for b in blocks],
                     [pltpu.SemaphoreType.DMA((nb, 7)), pltpu.SemaphoreType.DMA((nb, 7))])
    return [_put_own(o, b[None], _my_slot()) for o, b in zip(outs, blocks)]


def _gather_peers():
    x, y, c = _coords()
    return (x, y, c), [(x, y, 1 - c), (1 - x, y, c), (x, 1 - y, c), (1 - x, 1 - y, c)]


def _gather_start(blocks, after, name):
    nb = len(blocks)
    hbm, sem = pl.BlockSpec(memory_space=pltpu.HBM), pl.BlockSpec(memory_space=pltpu.SEMAPHORE)

    def body(*refs):
        x_refs, out_refs = refs[:nb], refs[nb:2 * nb]
        send_sems, recv_sems, token = refs[2 * nb + 1], refs[2 * nb + 2], refs[-1]
        (x, y, c), peers = _gather_peers()
        for kk, peer in enumerate(peers):
            for b in range(nb):
                pltpu.make_async_remote_copy(
                    src_ref=x_refs[b], dst_ref=out_refs[b].at[4 * x + 2 * y + c], send_sem=send_sems.at[4 * b + kk],
                    recv_sem=recv_sems.at[4 * b + kk], device_id=peer, device_id_type=MESH).start()
        token[...] = jnp.zeros_like(token)

    gathers = [pltpu.with_memory_space_constraint(lax.empty((N_DEV,) + b.shape, b.dtype), pltpu.HBM) for b in blocks]
    outs = pl.pallas_call(
        body, name=name,
        out_shape=(pltpu.SemaphoreType.DMA((4 * nb,)), pltpu.SemaphoreType.DMA((4 * nb,)),
                   *[pltpu.HBM(b.shape, b.dtype) for b in blocks], *[pltpu.HBM(b.shape, b.dtype) for b in gathers],
                   jax.ShapeDtypeStruct((SUBLANE, LANE), F32)),
        in_specs=[hbm] * (2 * nb) + [pl.BlockSpec(memory_space=pl.ANY)],
        out_specs=(sem, sem, *[hbm] * (2 * nb), pl.BlockSpec(memory_space=pltpu.VMEM)),
        input_output_aliases={i: 2 + i for i in range(2 * nb)},
        compiler_params=pltpu.CompilerParams(has_side_effects=pltpu.SideEffectType.DATAFLOW_SIDE_EFFECTING),
    )(*[pltpu.with_memory_space_constraint(b, pltpu.HBM) for b in blocks], *gathers, after)
    return outs[0], outs[1], list(outs[2:2 + nb]), list(outs[2 + nb:2 + 2 * nb]), outs[-1]


def _gather_wait(send_sems, recv_sems, thru, gathers, after, name):
    nb = len(thru)
    hbm, sem = pl.BlockSpec(memory_space=pltpu.HBM), pl.BlockSpec(memory_space=pltpu.SEMAPHORE)

    def body(*refs):
        x_refs, out_refs = refs[:nb], refs[nb:2 * nb]
        send_sems_, recv_sems_ = refs[2 * nb], refs[2 * nb + 1]
        _, peers = _gather_peers()
        for kk, (px, py, pc) in enumerate(peers):
            for b in range(nb):
                cp = pltpu.make_async_remote_copy(
                    src_ref=x_refs[b], dst_ref=out_refs[b].at[4 * px + 2 * py + pc], send_sem=send_sems_.at[4 * b + kk],
                    recv_sem=recv_sems_.at[4 * b + kk], device_id=(px, py, pc), device_id_type=MESH)
                cp.wait_send()
                cp.wait_recv()

    outs = pl.pallas_call(
        body, name=name,
        out_shape=(*[pltpu.HBM(b.shape, b.dtype) for b in thru], *[pltpu.HBM(b.shape, b.dtype) for b in gathers]),
        in_specs=[hbm] * (2 * nb) + [sem, sem, pl.BlockSpec(memory_space=pl.ANY)], out_specs=[hbm] * (2 * nb),
        input_output_aliases={i: i for i in range(2 * nb)},
        compiler_params=pltpu.CompilerParams(has_side_effects=pltpu.SideEffectType.DATAFLOW_SIDE_EFFECTING),
    )(*thru, *gathers, send_sems, recv_sems, after)
    return list(outs[:nb]), list(outs[nb:])


def _gather_finish(blocks, gathers, name):
    nb = len(blocks)

    def body(*refs):
        x_refs, in_refs, out_refs = refs[:nb], refs[nb:2 * nb], refs[2 * nb:3 * nb]
        send_sems, recv_sems = refs[3 * nb:]
        (x, y, c), peers = _gather_peers()
        copies = []
        for j, (px, py, _) in enumerate(peers[1:]):
            for b in range(nb):
                copies.append(pltpu.make_async_remote_copy(
                    src_ref=in_refs[b].at[4 * px + 2 * py + c], dst_ref=out_refs[b].at[4 * px + 2 * py + c],
                    send_sem=send_sems.at[b, j], recv_sem=recv_sems.at[b, j], device_id=(x, y, 1 - c),
                    device_id_type=MESH))
                copies[-1].start()
        for j, (px, py, _) in enumerate(peers[1:]):
            for b in range(nb):
                pltpu.make_async_remote_copy(
                    src_ref=in_refs[b].at[4 * px + 2 * py + c], dst_ref=out_refs[b].at[4 * px + 2 * py + 1 - c],
                    send_sem=send_sems.at[b, j], recv_sem=recv_sems.at[b, j], device_id=(x, y, 1 - c),
                    device_id_type=MESH).wait_recv()
        for cp in copies:
            cp.wait_send()

    any_spec = pl.BlockSpec(memory_space=pl.ANY)
    outs = pl.pallas_call(
        body, name=name, out_shape=[jax.ShapeDtypeStruct(b.shape, b.dtype) for b in gathers],
        in_specs=[any_spec] * (2 * nb), out_specs=[any_spec] * nb,
        input_output_aliases={nb + i: i for i in range(nb)},
        scratch_shapes=[pltpu.SemaphoreType.DMA((nb, 3)), pltpu.SemaphoreType.DMA((nb, 3))],
    )(*blocks, *gathers)
    return [_put_own(o, b[None], _my_slot()) for o, b in zip(outs, blocks)]


def _sibling_swap(bufs, name):
    nb = len(bufs)

    def body(*refs):
        x_refs, out_refs = refs[:nb], refs[nb:2 * nb]
        send_sems, recv_sems = refs[2 * nb:]
        x, y, c = _coords()
        copies = [pltpu.make_async_remote_copy(
            src_ref=x_refs[b].at[2 * q + 1 - c], dst_ref=out_refs[b].at[q], send_sem=send_sems.at[b, q],
            recv_sem=recv_sems.at[b, q], device_id=(x, y, 1 - c), device_id_type=MESH)
            for b in range(nb) for q in range(4)]
        for cp in copies:
            cp.start()
        for cp in copies:
            cp.wait()

    return _hbm_call(body, name, bufs, [((4,) + b.shape[1:], b.dtype) for b in bufs],
                     [pltpu.SemaphoreType.DMA((nb, 4)), pltpu.SemaphoreType.DMA((nb, 4))])


def _chip_exchange(bufs, name):
    nb = len(bufs)

    def body(*refs):
        x_refs, out_refs = refs[:nb], refs[nb:2 * nb]
        send_sems, recv_sems = refs[2 * nb:]
        x, y, c = _coords()
        chips = [(1 - x, y), (x, 1 - y), (1 - x, 1 - y)]
        copies = [pltpu.make_async_remote_copy(
            src_ref=x_refs[b].at[2 * px + py], dst_ref=out_refs[b].at[2 * x + y], send_sem=send_sems.at[b, j],
            recv_sem=recv_sems.at[b, j], device_id=(px, py, c), device_id_type=MESH)
            for j, (px, py) in enumerate(chips) for b in range(nb)]
        for cp in copies:
            cp.start()
        for j, (px, py) in enumerate(chips):
            for b in range(nb):
                pltpu.make_async_remote_copy(
                    src_ref=x_refs[b].at[2 * x + y], dst_ref=out_refs[b].at[2 * px + py], send_sem=send_sems.at[b, j],
                    recv_sem=recv_sems.at[b, j], device_id=(px, py, c), device_id_type=MESH).wait_recv()
        for cp in copies:
            cp.wait_send()

    outs = _hbm_call(body, name, bufs, [(b.shape, b.dtype) for b in bufs],
                     [pltpu.SemaphoreType.DMA((nb, 3)), pltpu.SemaphoreType.DMA((nb, 3))])
    chip = 2 * lax.axis_index("x") + lax.axis_index("y")
    return [_put_own(o, lax.dynamic_index_in_dim(b, chip, 0), chip) for o, b in zip(outs, bufs)]


def _chip_exchange_start(bufs, name):
    nb = len(bufs)
    hbm, sem = pl.BlockSpec(memory_space=pltpu.HBM), pl.BlockSpec(memory_space=pltpu.SEMAPHORE)

    def body(*refs):
        x_refs, land_refs = refs[:nb], refs[nb:2 * nb]
        send_sems, recv_sems, token = refs[2 * nb], refs[2 * nb + 1], refs[-1]
        x, y, c = _coords()
        for j, (px, py) in enumerate([(1 - x, y), (x, 1 - y), (1 - x, 1 - y)]):
            for b in range(nb):
                pltpu.make_async_remote_copy(
                    src_ref=x_refs[b].at[2 * px + py], dst_ref=land_refs[b].at[2 * x + y], send_sem=send_sems.at[3 * b + j],
                    recv_sem=recv_sems.at[3 * b + j], device_id=(px, py, c), device_id_type=MESH).start()
        token[...] = jnp.zeros_like(token)

    lands = [pltpu.with_memory_space_constraint(lax.empty(b.shape, b.dtype), pltpu.HBM) for b in bufs]
    outs = pl.pallas_call(
        body, name=name,
        out_shape=(pltpu.SemaphoreType.DMA((3 * nb,)), pltpu.SemaphoreType.DMA((3 * nb,)),
                   *[pltpu.HBM(b.shape, b.dtype) for b in bufs], *[pltpu.HBM(b.shape, b.dtype) for b in bufs],
                   jax.ShapeDtypeStruct((SUBLANE, LANE), F32)),
        in_specs=[hbm] * (2 * nb), out_specs=(sem, sem, *[hbm] * (2 * nb), pl.BlockSpec(memory_space=pltpu.VMEM)),
        input_output_aliases={i: 2 + i for i in range(2 * nb)},
        compiler_params=pltpu.CompilerParams(has_side_effects=pltpu.SideEffectType.DATAFLOW_SIDE_EFFECTING),
    )(*[pltpu.with_memory_space_constraint(b, pltpu.HBM) for b in bufs], *lands)
    return outs[0], outs[1], list(outs[2:2 + nb]), list(outs[2 + nb:2 + 2 * nb]), outs[-1]


def _chip_exchange_wait(send_sems, recv_sems, thru, lands, after, name):
    nb = len(thru)
    hbm, sem = pl.BlockSpec(memory_space=pltpu.HBM), pl.BlockSpec(memory_space=pltpu.SEMAPHORE)

    def body(*refs):
        x_refs, land_refs = refs[:nb], refs[nb:2 * nb]
        send_sems_, recv_sems_ = refs[2 * nb], refs[2 * nb + 1]
        x, y, c = _coords()
        for j, (px, py) in enumerate([(1 - x, y), (x, 1 - y), (1 - x, 1 - y)]):
            for b in range(nb):
                cp = pltpu.make_async_remote_copy(
                    src_ref=x_refs[b].at[2 * px + py], dst_ref=land_refs[b].at[2 * px + py],
                    send_sem=send_sems_.at[3 * b + j], recv_sem=recv_sems_.at[3 * b + j], device_id=(px, py, c),
                    device_id_type=MESH)
                cp.wait_send()
                cp.wait_recv()

    outs = pl.pallas_call(
        body, name=name,
        out_shape=(*[pltpu.HBM(b.shape, b.dtype) for b in thru], *[pltpu.HBM(b.shape, b.dtype) for b in lands]),
        in_specs=[hbm] * (2 * nb) + [sem, sem, pl.BlockSpec(memory_space=pl.ANY)], out_specs=[hbm] * (2 * nb),
        input_output_aliases={i: i for i in range(2 * nb)},
        compiler_params=pltpu.CompilerParams(has_side_effects=pltpu.SideEffectType.DATAFLOW_SIDE_EFFECTING),
    )(*thru, *lands, send_sems, recv_sems, after)
    return list(outs[:nb]), list(outs[nb:])


def _chip_sum(g, r1, c, name):
    _, r, cw = g.shape
    tr = _pick(r, (256, 176, 128))

    def body(c_ref, g_ref, r_ref, o_ref):
        o_ref[...] = (g_ref[...] + r_ref[...]).astype(BF16)

    grid_spec = pltpu.PrefetchScalarGridSpec(
        num_scalar_prefetch=1, grid=(4, r // tr),
        in_specs=[_BS((None, None, tr, cw), lambda q, i, c_ref: (q, c_ref[0], i, 0)),
                  _BS((None, tr, cw), lambda q, i, c_ref: (q, i, 0))],
        out_specs=_BS((None, tr, cw), lambda q, i, c_ref: (q, i, 0)))
    return pl.pallas_call(
        body, name=name, grid_spec=grid_spec, out_shape=jax.ShapeDtypeStruct((4, r, cw), BF16),
        compiler_params=_params(("parallel", "parallel")),
    )(c.reshape(1).astype(jnp.int32), g.reshape(4, 2, r, cw), r1)


def _adamw_math(w, g, m, v):
    m = B1 * m + (1.0 - B1) * g
    v = B2 * v + (1.0 - B2) * (g * g)
    m_hat = m / (1.0 - B1 ** STEP)
    v_hat = v / (1.0 - B2 ** STEP)
    return -LR * (m_hat / (jnp.sqrt(v_hat) + AEPS) + WD * w), m, v


def _sum_adamw(parts, w, m, v, name):
    r, c = w.shape
    tr = _pick(r, (256, 176, 128))

    def body(p0, p1, p2, p3, w_ref, m_ref, v_ref, g_out, d_out, m_out, v_out):
        g = ((p0[...].astype(F32) + p1[...].astype(F32)) + p2[...].astype(F32)) + p3[...].astype(F32)
        g_out[...] = g
        d_out[...], m_out[...], v_out[...] = _adamw_math(w_ref[...], g, m_ref[...], v_ref[...])

    part = lambda q: _BS((None, tr, c), functools.partial(lambda i, q: (q, i, 0), q=q))
    plain = _BS((tr, c), lambda i: (i, 0))
    return pl.pallas_call(
        body, name=name, grid=(r // tr,), in_specs=[part(q) for q in range(4)] + [plain] * 3,
        out_specs=[plain] * 4, out_shape=[jax.ShapeDtypeStruct((r, c), F32)] * 4,
        compiler_params=_params(("parallel",)),
    )(parts, parts, parts, parts, w, m, v)


def _small_update(gathered, w, m, v):
    r = w.shape[0]

    def body(ga_ref, w_ref, m_ref, v_ref, g_out, d_out, m_out, v_out):
        g = ga_ref[0]
        for dev in range(1, N_DEV):
            g = g + ga_ref[dev]
        g_out[...] = g
        d_out[...], m_out[...], v_out[...] = _adamw_math(w_ref[...], g, m_ref[...], v_ref[...])

    return pl.pallas_call(
        body, name="small_update", out_shape=[jax.ShapeDtypeStruct((r, LANE), F32)] * 4,
    )(gathered, w, m, v)


def _pack_small(vals):
    rows = []
    for name, (r, n) in SMALL:
        flat = vals[name].reshape(-1)
        pad = (-flat.shape[0]) % (SUBLANE * LANE)
        rows.append(jnp.pad(flat, (0, pad)).reshape(-1, LANE))
    return jnp.concatenate(rows, axis=0)


def _unpack_small(packed):
    out, off = {}, 0
    for name, (r, n) in SMALL:
        nrow = -(-(r * n) // (SUBLANE * LANE)) * SUBLANE
        out[name] = packed[off:off + nrow].reshape(-1)[:r * n].reshape(r, n)
        off += nrow
    return out


def kernel(x, positions, ffn1_norm, ffn1_w_in, ffn1_w_out, mix_norm, w_in, hg_lb_table, hg_out_norm, w_hg_branch, mla_q_lora_norm, w_q_up, mla_kv_lora_norm, w_kv_up, q_head_norm, k_head_norm, w_mla_branch, w_merge, b_merge, w_out, ffn2_norm, ffn2_w_in, ffn2_w_out, final_norm, loss_target, m_ffn1_norm, m_ffn1_w_in, m_ffn1_w_out, m_mix_norm, m_w_in, m_hg_lb_table, m_hg_out_norm, m_w_hg_branch, m_mla_q_lora_norm, m_w_q_up, m_mla_kv_lora_norm, m_w_kv_up, m_q_head_norm, m_k_head_norm, m_w_mla_branch, m_w_merge, m_b_merge, m_w_out, m_ffn2_norm, m_ffn2_w_in, m_ffn2_w_out, m_final_norm, v_ffn1_norm, v_ffn1_w_in, v_ffn1_w_out, v_mix_norm, v_w_in, v_hg_lb_table, v_hg_out_norm, v_w_hg_branch, v_mla_q_lora_norm, v_w_q_up, v_mla_kv_lora_norm, v_w_kv_up, v_q_head_norm, v_k_head_norm, v_w_mla_branch, v_w_merge, v_b_merge, v_w_out, v_ffn2_norm, v_ffn2_w_in, v_ffn2_w_out, v_final_norm):
    args = dict(locals())
    t = x.shape[1]
    big_w = {n: args[n][0] for n, _, _ in BIG}
    small = {n: args[n].reshape(shape) for n, shape in SMALL}

    names = [n for n, _, _ in BIG]
    first, rest = names[:2], names[2:]
    full = dict(zip(first, _all_gather([big_w[n].astype(BF16) for n in first], "weights_all_gather_ffn1")))
    g_send, g_recv, g_thru, g_bufs, g_token = _gather_start([big_w[n].astype(BF16) for n in rest], full[first[0]],
                                                            "weights_gather_start")
    gains = dict(small, ffn1_norm=small["ffn1_norm"] + g_token[0:1, 0:1])

    def late_weights(after):
        blocks, bufs = _gather_wait(g_send, g_recv, g_thru, g_bufs, after, "weights_gather_wait")
        return dict(zip(rest, _gather_finish(blocks, bufs, "weights_gather_finish")))

    inv_freq = ROPE_THETA ** (-jnp.arange(0, ROPE, 2, dtype=F32) / ROPE)
    ang = positions[0].astype(F32)[:, None] * inv_freq
    cs = jnp.concatenate([jnp.cos(ang), jnp.cos(ang)], axis=1)
    sn = jnp.concatenate([jnp.sin(ang), jnp.sin(ang)], axis=1)

    c = lax.axis_index("c")
    chip = 2 * lax.axis_index("x") + lax.axis_index("y")
    early = {}

    def chip_sums_of(g, ns, tag):
        from_sibling = _sibling_swap([g[n] for n in ns], "grads_sibling_swap_" + tag)
        return [_chip_sum(g[n], r1, c, "chip_sum_" + n) for n, r1 in zip(ns, from_sibling)]

    def early_grads(g):
        early["names"] = [n for n in names if n in g]
        early["sums"] = chip_sums_of(g, early["names"], "early")
        early["send"], early["recv"], early["thru"], early["lands"], token = _chip_exchange_start(
            early["sums"], "grads_exchange_start")
        return token

    loss_row, grad_x, g = _local_step(x[0], loss_target[0], cs, sn, full, late_weights, gains, early_grads)
    late = [n for n in names if n not in early["names"]]
    exchanged = dict(zip(late, _chip_exchange(chip_sums_of(g, late, "late"), "grads_chip_exchange_late")))
    sent, landed = _chip_exchange_wait(early["send"], early["recv"], early["thru"], early["lands"],
                                       exchanged[late[0]], "grads_exchange_wait")
    for n, land, own in zip(early["names"], landed, sent):
        exchanged[n] = lax.dynamic_update_index_in_dim(land, lax.dynamic_index_in_dim(own, chip, 0), chip, 0)

    small_packed = jnp.concatenate([_pack_small(g), jnp.pad(loss_row, ((0, SUBLANE - 1), (0, 0)))], axis=0)
    small_all = _all_gather([small_packed], "small_all_gather")[0]
    zero_tail = jnp.zeros((SUBLANE, LANE), F32)
    pk = lambda d: jnp.concatenate([_pack_small(d), zero_tail], axis=0)
    sg, sd, sm, sv = _small_update(
        small_all, pk(small), pk({n: args["m_" + n].reshape(shape) for n, shape in SMALL}),
        pk({n: args["v_" + n].reshape(shape) for n, shape in SMALL}))
    n_small_rows = _pack_small(small).shape[0]
    loss = sg[n_small_rows, 0]
    outs = {k_: _unpack_small(a) for k_, a in (("grad", sg), ("delta", sd), ("new_m", sm), ("new_v", sv))}

    for n in names:
        outs["grad"][n], outs["delta"][n], outs["new_m"][n], outs["new_v"][n] = _sum_adamw(
            exchanged[n], big_w[n], args["m_" + n][0], args["v_" + n][0], "adamw_" + n)

    def shaped(kind, n):
        return outs[kind][n].reshape(args[n].shape)

    return (loss, grad_x[None], *[shaped("grad", n) for n in WEIGHT_ORDER], *[shaped("delta", n) for n in WEIGHT_ORDER],
            *[shaped("new_m", n) for n in WEIGHT_ORDER], *[shaped("new_v", n) for n in WEIGHT_ORDER])
```

```python
import functools

import jax
import jax.numpy as jnp
from jax import lax
from jax.experimental import pallas as pl
from jax.experimental.pallas import tpu as pltpu

F32 = jnp.float32
BF16 = jnp.bfloat16

D = 1024
FF = 2816
NH = 8
HD = 128
ROPE = 64
QK = HD + ROPE
QL = 384
KVL = 256
HGW = NH * HD
CHUNK = 64
EPS = 1e-6
ROPE_THETA = 10000.0
SCALE = QK ** -0.5

LR, B1, B2, AEPS, WD, STEP = 0.001, 0.9, 0.999, 1e-08, 0.01, 10

HB = 128
SUB = 16
EXP_CLAMP = 80.0
ATT_TILES = (512, 256, 128)
ATT_Q = (1024, 512, 256, 128)
ATT_KEY_TILES = 1
ROW_TILES = (1024, 512, 256, 128)

LANE = 128
SUBLANE = 8
VMEM_LIMIT = 56 << 20

N_DEV = 8
MESH = pl.DeviceIdType.MESH

BIG = (
    ("ffn1_w_in", (D, 2 * FF), 1), ("ffn1_w_out", (FF, D), 0), ("w_in", (D, 4800), 1),
    ("w_hg_branch", (HGW, D), 0), ("w_q_up", (QL, NH * QK), 1), ("w_kv_up", (KVL, NH * 2 * HD), 1),
    ("w_mla_branch", (NH * HD, D), 0), ("w_merge", (D, 2 * D), 1), ("w_out", (D, D), 0),
    ("ffn2_w_in", (D, 2 * FF), 1), ("ffn2_w_out", (FF, D), 0),
)
SMALL = (
    ("ffn1_norm", (1, D)), ("mix_norm", (1, D)), ("hg_lb_table", (2, HGW)), ("hg_out_norm", (1, HD)),
    ("mla_q_lora_norm", (1, QL)), ("mla_kv_lora_norm", (1, KVL)), ("q_head_norm", (1, QK)),
    ("k_head_norm", (1, QK)), ("b_merge", (1, 2 * D)), ("ffn2_norm", (1, D)), ("final_norm", (1, D)),
)
WEIGHT_ORDER = ("ffn1_norm", "ffn1_w_in", "ffn1_w_out", "mix_norm", "w_in", "hg_lb_table", "hg_out_norm",
                "w_hg_branch", "mla_q_lora_norm", "w_q_up", "mla_kv_lora_norm", "w_kv_up", "q_head_norm",
                "k_head_norm", "w_mla_branch", "w_merge", "b_merge", "w_out", "ffn2_norm", "ffn2_w_in",
                "ffn2_w_out", "final_norm")


def _pick(n, cands):
    for c in cands:
        if n % c == 0:
            return c
    return n


def _params(sem):
    return pltpu.CompilerParams(dimension_semantics=sem, vmem_limit_bytes=VMEM_LIMIT)


def _sig(x):
    return 1.0 / (1.0 + jnp.exp(-x))


def _dot(a, b):
    return jnp.dot(a.astype(BF16), b.astype(BF16), preferred_element_type=F32)


def _dot_nt(a, b):
    return lax.dot_general(a.astype(BF16), b.astype(BF16), (((1,), (1,)), ((), ())),
                           preferred_element_type=F32)


def _dot_tn(a, b):
    return lax.dot_general(a.astype(BF16), b.astype(BF16), (((0,), (0,)), ((), ())),
                           preferred_element_type=F32)


def _split3(x):
    x1 = x.astype(BF16)
    r1 = x - x1.astype(F32)
    x2 = r1.astype(BF16)
    x3 = (r1 - x2.astype(F32)).astype(BF16)
    return x1, x2, x3


def _dot_sel(m, x):
    x1, x2, x3 = _split3(x)
    d = lambda p: jnp.dot(m, p, preferred_element_type=F32)
    return d(x1) + d(x2) + d(x3)


def _sel_dot(x, m):
    x1, x2, x3 = _split3(x)
    d = lambda p: jnp.dot(p, m, preferred_element_type=F32)
    return d(x1) + d(x2) + d(x3)


_TN = (1408, 1024, 768, 512, 384, 256, 128)


def _mm(a, b, mode, name, out_dtype=F32):
    if mode == "tn":
        t, m = a.shape
        n = b.shape[1]
        tt, tm, tn = _pick(t, (512, 256, 128)), _pick(m, _TN), _pick(n, _TN)

        def body(a_ref, b_ref, o_ref):
            @pl.when(pl.program_id(2) == 0)
            def _():
                o_ref[...] = jnp.zeros_like(o_ref)

            o_ref[...] += _dot_tn(a_ref[...], b_ref[...])

        return pl.pallas_call(
            body, name=name, grid=(m // tm, n // tn, t // tt),
            in_specs=[pl.BlockSpec((tt, tm), lambda i, j, k: (k, i)),
                      pl.BlockSpec((tt, tn), lambda i, j, k: (k, j))],
            out_specs=pl.BlockSpec((tm, tn), lambda i, j, k: (i, j)),
            out_shape=jax.ShapeDtypeStruct((m, n), F32),
            compiler_params=_params(("parallel", "parallel", "arbitrary")),
        )(a, b)

    m, k = a.shape
    tm = _pick(m, ROW_TILES)
    if mode == "nn":
        n = b.shape[1]
        tn = _pick(n, _TN)
        b_spec = pl.BlockSpec((k, tn), lambda i, j: (0, j))
        dot = _dot
    else:
        n = b.shape[0]
        tn = _pick(n, _TN if k <= 4096 else (512, 256, 128))
        b_spec = pl.BlockSpec((tn, k), lambda i, j: (j, 0))
        dot = _dot_nt

    def body(a_ref, b_ref, o_ref):
        o_ref[...] = dot(a_ref[...], b_ref[...]).astype(o_ref.dtype)

    return pl.pallas_call(
        body, name=name, grid=(m // tm, n // tn),
        in_specs=[pl.BlockSpec((tm, k), lambda i, j: (i, 0)), b_spec],
        out_specs=pl.BlockSpec((tm, tn), lambda i, j: (i, j)),
        out_shape=jax.ShapeDtypeStruct((m, n), out_dtype),
        compiler_params=_params(("parallel", "parallel")),
    )(a, b)


_DOTS = {"nn": _dot, "nt": _dot_nt, "tn": _dot_tn}
_BS = pl.BlockSpec


def _mmcall(name, kind, a, b, a_spec, b_spec, o_spec, o_shape, grid, red_axis=None, out_dtype=F32):
    dot = _DOTS[kind]

    def body(a_ref, b_ref, o_ref):
        if red_axis is None:
            o_ref[...] = dot(a_ref[...], b_ref[...]).astype(o_ref.dtype)
        else:
            @pl.when(pl.program_id(red_axis) == 0)
            def _():
                o_ref[...] = jnp.zeros_like(o_ref)

            o_ref[...] += dot(a_ref[...], b_ref[...])

    sem = tuple("arbitrary" if ax == red_axis else "parallel" for ax in range(len(grid)))
    return pl.pallas_call(
        body, name=name, grid=grid, in_specs=[a_spec, b_spec], out_specs=o_spec,
        out_shape=jax.ShapeDtypeStruct(o_shape, out_dtype), compiler_params=_params(sem),
    )(a, b)


class _AsRef:
    def __init__(self, value):
        self.value = value

    def __getitem__(self, idx):
        return self.value


def _mm_stack_red(a, w, name, kind, epilogue=None):
    s, t, n = a.shape
    nout = w.shape[2] if kind == "nn" else w.shape[1]
    if epilogue is None:
        tm = _pick(t, ROW_TILES)
        return _mmcall(name, kind, a, w, _BS((None, tm, n), lambda i, j: (j, i, 0)),
                       _BS((None,) + w.shape[1:], lambda i, j: (j, 0, 0)), _BS((tm, nout), lambda i, j: (i, 0)),
                       (t, nout), (t // tm, s), red_axis=1)
    fn, first, rows, vecs, outs, accs = epilogue
    rows = [first] + list(rows)
    tm = _pick(t, (512, 256, 128))
    n_row, n_vec, n_out = len(rows), len(vecs), len(outs)
    dot = _DOTS[kind]

    def body(*refs):
        a_ref, w_ref, prod = refs[0], refs[1], refs[-1]
        row_refs = refs[2:2 + n_row]
        vec_refs = refs[2 + n_row:2 + n_row + n_vec]
        out_refs = refs[2 + n_row + n_vec:2 + n_row + n_vec + n_out]
        acc_refs = refs[2 + n_row + n_vec + n_out:-1]
        i, j = pl.program_id(0), pl.program_id(1)

        @pl.when(j == 0)
        def _():
            prod[...] = jnp.zeros_like(prod)

        prod[...] += dot(a_ref[...], w_ref[...])

        @pl.when(j == s - 1)
        def _():
            res = fn(row_refs[0], _AsRef(prod[...]), *row_refs[1:], *vec_refs, *out_refs)
            if acc_refs:
                @pl.when(i == 0)
                def _():
                    for r in acc_refs:
                        r[...] = jnp.zeros_like(r)

                for r, val in zip(acc_refs, res):
                    r[...] += val

    in_specs = [_BS((None, tm, n), lambda i, j: (j, i, 0)), _BS((None,) + w.shape[1:], lambda i, j: (j, 0, 0))]
    in_specs += [_BS((tm, r.shape[1]), lambda i, j: (i, 0)) for r in rows]
    in_specs += [_BS(v.shape, lambda i, j: (0, 0)) for v in vecs]
    out_specs = [_BS((tm, wd), lambda i, j: (i, 0)) for wd, _ in outs] + [_BS(sh, lambda i, j: (0, 0)) for sh in accs]
    out_shape = [jax.ShapeDtypeStruct((t, wd), dt) for wd, dt in outs] + [jax.ShapeDtypeStruct(sh, F32) for sh in accs]
    return pl.pallas_call(
        body, name=name, grid=(t // tm, s), in_specs=in_specs, out_specs=out_specs, out_shape=out_shape,
        scratch_shapes=[pltpu.VMEM((tm, nout), F32)],
        compiler_params=_params(("arbitrary" if accs else "parallel", "arbitrary")),
    )(a, w, *rows, *vecs)


def _mm_stack_tn(a, b, name):
    grp = 4
    if a.ndim == 2:
        t, k = a.shape
        s, _, n = b.shape
        tt = _pick(t, (512, 256, 128))
        a_spec = _BS((tt, k), lambda j, r: (r, 0))
        b_spec, b_in = _BS((None, grp, tt, n), lambda j, r: (j, 0, r, 0)), b.reshape(s // grp, grp, t, n)
        a_in = a
    else:
        s, t, k = a.shape
        n = b.shape[1]
        tt = _pick(t, (512, 256, 128))
        a_spec, a_in = _BS((None, grp, tt, k), lambda j, r: (j, 0, r, 0)), a.reshape(s // grp, grp, t, k)
        b_spec, b_in = _BS((tt, n), lambda j, r: (r, 0)), b

    def body(a_ref, b_ref, o_ref):
        @pl.when(pl.program_id(1) == 0)
        def _():
            o_ref[...] = jnp.zeros_like(o_ref)

        shared = a_ref[...] if a.ndim == 2 else b_ref[...]
        for e in range(grp):
            o_ref[e] += _dot_tn(shared, b_ref[e]) if a.ndim == 2 else _dot_tn(a_ref[e], shared)

    return pl.pallas_call(
        body, name=name, grid=(s // grp, t // tt), in_specs=[a_spec, b_spec],
        out_specs=_BS((None, grp, k, n), lambda j, r: (j, 0, 0, 0)),
        out_shape=jax.ShapeDtypeStruct((s // grp, grp, k, n), F32),
        compiler_params=_params(("parallel", "arbitrary")),
    )(a_in, b_in).reshape(s, k, n)


def _cols_fwd(x, w, name):
    t, k = x.shape
    s, _, n = w.shape
    tm = _pick(t, ROW_TILES)

    def body(x_ref, w_ref, o_ref):
        x_ = x_ref[...]
        for j in range(s):
            o_ref[:, n * j:n * (j + 1)] = _dot(x_, w_ref[j])

    return pl.pallas_call(
        body, name=name, grid=(t // tm,),
        in_specs=[_BS((tm, k), lambda i: (i, 0)), _BS((s, k, n), lambda i: (0, 0, 0))],
        out_specs=_BS((tm, s * n), lambda i: (i, 0)), out_shape=jax.ShapeDtypeStruct((t, s * n), F32),
        compiler_params=_params(("parallel",)),
    )(x, w)


def _cols_dx(d, w, name):
    t = d.shape[0]
    s, k, n = w.shape
    tm = _pick(t, ROW_TILES)

    def body(d_ref, w_ref, o_ref):
        acc = _dot_nt(d_ref[:, 0:n], w_ref[0])
        for j in range(1, s):
            acc = acc + _dot_nt(d_ref[:, n * j:n * (j + 1)], w_ref[j])
        o_ref[...] = acc

    return pl.pallas_call(
        body, name=name, grid=(t // tm,),
        in_specs=[_BS((tm, s * n), lambda i: (i, 0)), _BS((s, k, n), lambda i: (0, 0, 0))],
        out_specs=_BS((tm, k), lambda i: (i, 0)), out_shape=jax.ShapeDtypeStruct((t, k), F32),
        compiler_params=_params(("parallel",)),
    )(d, w)


def _cols_dw(x, d, n, name):
    t, k = x.shape
    s = d.shape[1] // n
    tt = _pick(t, (512, 256, 128))

    def body(x_ref, d_ref, o_ref):
        @pl.when(pl.program_id(0) == 0)
        def _():
            o_ref[...] = jnp.zeros_like(o_ref)

        x_ = x_ref[...]
        for j in range(s):
            o_ref[j] += _dot_tn(x_, d_ref[:, n * j:n * (j + 1)])

    return pl.pallas_call(
        body, name=name, grid=(t // tt,),
        in_specs=[_BS((tt, k), lambda r: (r, 0)), _BS((tt, s * n), lambda r: (r, 0))],
        out_specs=_BS((s, k, n), lambda r: (0, 0, 0)), out_shape=jax.ShapeDtypeStruct((s, k, n), F32),
        compiler_params=_params(("arbitrary",)),
    )(x, d)


def _rows(fn, name, t, tm, ins, vecs, outs, accs=()):
    n_in, n_out, n_acc = len(ins) + len(vecs), len(outs), len(accs)

    def body(*refs):
        res = fn(*refs[:n_in + n_out])
        if n_acc:
            acc_refs = refs[n_in + n_out:]

            @pl.when(pl.program_id(0) == 0)
            def _():
                for r in acc_refs:
                    r[...] = jnp.zeros_like(r)

            for r, val in zip(acc_refs, res):
                r[...] += val

    in_specs = [pl.BlockSpec((tm, bw), functools.partial(lambda i, cb: (i, cb), cb=cb)) for _, bw, cb in ins]
    in_specs += [pl.BlockSpec(v.shape, lambda i: (0, 0)) for v in vecs]
    out_specs = [pl.BlockSpec((tm, w), lambda i: (i, 0)) for w, _ in outs]
    out_specs += [pl.BlockSpec(s, lambda i: (0, 0)) for s in accs]
    out_shape = [jax.ShapeDtypeStruct((t, w), dt) for w, dt in outs]
    out_shape += [jax.ShapeDtypeStruct(s, F32) for s in accs]
    return pl.pallas_call(
        body, name=name, grid=(t // tm,), in_specs=in_specs, out_specs=out_specs, out_shape=out_shape,
        compiler_params=_params(("arbitrary",) if n_acc else ("parallel",)),
    )(*[a for a, _, _ in ins], *vecs)


def _rms(x, g):
    return x * lax.rsqrt(jnp.mean(x * x, axis=-1, keepdims=True) + EPS) * g


def _rms_bwd(x, g, dy):
    xh = x * lax.rsqrt(jnp.mean(x * x, axis=-1, keepdims=True) + EPS)
    r = lax.rsqrt(jnp.mean(x * x, axis=-1, keepdims=True) + EPS)
    dyg = dy * g
    dx = r * (dyg - xh * jnp.mean(dyg * xh, axis=-1, keepdims=True))
    return dx, jnp.sum(dy * xh, axis=0, keepdims=True)


def _hgrn_mats():
    row = lax.broadcasted_iota(jnp.int32, (HB, HB), 0)
    col = lax.broadcasted_iota(jnp.int32, (HB, HB), 1)
    return row, col


def _hgrn_gates(qr, z, t0, t1):
    lb = 1.0 / (1.0 + jnp.exp(t1 - t0))
    sz = _sig(z)
    sneg = 1.0 / (1.0 + jnp.exp(z))
    f = lb + (1.0 - lb) * sz
    return lb, sz, sneg, f, jnp.log(f), (1.0 - lb) * sneg, qr * _sig(qr)


def _hgrn_scores(q, k, cum):
    mids = [cum[SUB * i + SUB // 2 - 1:SUB * i + SUB // 2, :] for i in range(HB // SUB)]
    cmid = jnp.concatenate([cum[SUB * i:SUB * (i + 1)] - mids[i] for i in range(HB // SUB)], axis=0)
    qd = q * jnp.exp(jnp.minimum(cmid, EXP_CLAMP))
    qd_b = qd.astype(BF16)
    kds, parts = [], []
    for i in range(HB // SUB):
        kd = k * jnp.exp(jnp.minimum(mids[i] - cum, EXP_CLAMP))
        kds.append(kd)
        parts.append(_dot_nt(qd_b[SUB * i:SUB * (i + 1)], kd))
    return qd, kds, jnp.concatenate(parts, axis=0), cmid


def _hgrn_fwd(p_hg, table, gain, t):
    nblk = t // HB

    def body(q_ref, f_ref, i_ref, g_ref, tab_ref, gain_ref, o_ref, y_ref, st_ref, state):
        @pl.when(pl.program_id(0) == 0)
        def _():
            state[...] = jnp.zeros_like(state)

        row, col = _hgrn_mats()
        causal = col <= row
        tri = causal.astype(BF16)
        for h in range(NH):
            sl = slice(HD * h, HD * (h + 1))
            v, gr = i_ref[:, sl], g_ref[:, sl]
            _, _, _, _, lf, k, q = _hgrn_gates(q_ref[:, sl], f_ref[:, sl], tab_ref[0:1, sl], tab_ref[1:2, sl])
            cum = _dot_sel(tri, lf)
            _, _, s, _ = _hgrn_scores(q, k, cum)
            p = jnp.where(causal, s, 0.0)
            st = state[h]
            st_ref[h, 0] = st
            o = _dot(p, v) + _dot_nt(q * jnp.exp(cum), st)
            last = cum[HB - 1:HB, :]
            state[h] = st * jnp.exp(last) + _dot_tn(v, k * jnp.exp(last - cum))
            o_ref[:, sl] = o
            y_ref[:, sl] = (_rms(o, gain_ref[...]) * gr * _sig(gr)).astype(BF16)

    blk = lambda cb: pl.BlockSpec((HB, HGW), functools.partial(lambda n, cb: (n, cb), cb=cb))
    return pl.pallas_call(
        body, name="hgrn_fwd", grid=(nblk,),
        in_specs=[blk(0), blk(1), blk(2), blk(3), pl.BlockSpec((2, HGW), lambda n: (0, 0)),
                  pl.BlockSpec((1, HD), lambda n: (0, 0))],
        out_specs=[pl.BlockSpec((HB, HGW), lambda n: (n, 0)), pl.BlockSpec((HB, HGW), lambda n: (n, 0)),
                   pl.BlockSpec((NH, 1, HD, HD), lambda n: (0, n, 0, 0))],
        out_shape=[jax.ShapeDtypeStruct((t, HGW), F32), jax.ShapeDtypeStruct((t, HGW), BF16),
                   jax.ShapeDtypeStruct((NH, nblk, HD, HD), F32)],
        scratch_shapes=[pltpu.VMEM((NH, HD, HD), F32)],
        compiler_params=_params(("arbitrary",)),
    )(p_hg, p_hg, p_hg, p_hg, table, gain)


def _hgrn_bwd(p_hg, table, gain, o_pre, states, dy, t):
    nblk = t // HB

    def body(q_ref, f_ref, i_ref, g_ref, tab_ref, gain_ref, o_ref, st_ref, dy_ref, dp_ref, dtab_ref, dgain_ref,
             dstate):
        @pl.when(pl.program_id(0) == 0)
        def _():
            dstate[...] = jnp.zeros_like(dstate)
            dtab_ref[...] = jnp.zeros_like(dtab_ref)
            dgain_ref[...] = jnp.zeros_like(dgain_ref)

        row, col = _hgrn_mats()
        causal = col <= row
        tri = causal.astype(BF16)
        tri_t = (row <= col).astype(BF16)
        dgain = jnp.zeros((1, HD), F32)
        for h in range(NH):
            sl = slice(HD * h, HD * (h + 1))
            qr, z, v, gr = q_ref[:, sl], f_ref[:, sl], i_ref[:, sl], g_ref[:, sl]
            lb, sz, sneg, f, lf, k, q = _hgrn_gates(qr, z, tab_ref[0:1, sl], tab_ref[1:2, sl])
            cum = _dot_sel(tri, lf)
            qd, kds, s, cmid = _hgrn_scores(q, k, cum)
            p = jnp.where(causal, s, 0.0)
            st = st_ref[h, 0]
            dst = dstate[h]
            o = o_ref[:, sl]
            sg = _sig(gr)
            dyh = dy_ref[:, sl]
            on = _rms(o, gain_ref[...])
            dgr = dyh * on * sg * (1.0 + gr * (1.0 - sg))
            do, dg_h = _rms_bwd(o, gain_ref[...], dyh * gr * sg)
            dgain = dgain + dg_h
            do_b = do.astype(BF16)
            ecum = jnp.exp(cum)
            qc = q * ecum
            last = cum[HB - 1:HB, :]
            edec = jnp.exp(last - cum)
            kdec = k * edec
            dp = jnp.where(causal, _dot_nt(do_b, v), 0.0)
            dv = _dot(p.T, do_b) + _dot_nt(kdec, dst)
            dqc = _dot(do_b, st)
            dkdec = _dot(v, dst)
            dstate[h] = dst * jnp.exp(last) + _dot(do.T, qc)
            dp_b = dp.astype(BF16)
            dqd = jnp.concatenate([_dot(dp_b[SUB * i:SUB * (i + 1)], kds[i]) for i in range(HB // SUB)], axis=0)
            gq = dqd * qd
            dq = dqd * jnp.exp(jnp.minimum(cmid, EXP_CLAMP)) + dqc * ecum
            gs = dkdec * kdec
            dk = dkdec * edec
            dcum = dqc * qc - gs + gq
            dcum = dcum + jnp.where(row == HB - 1, jnp.sum(gs, axis=0, keepdims=True)
                                    + jnp.exp(last) * jnp.sum(st * dst, axis=0, keepdims=True), 0.0)
            qd_b = qd.astype(BF16)
            for i in range(HB // SUB):
                dkd = _dot_tn(dp_b[SUB * i:SUB * (i + 1)], qd_b[SUB * i:SUB * (i + 1)])
                mid = cum[SUB * i + SUB // 2 - 1:SUB * i + SUB // 2, :]
                dk = dk + dkd * jnp.exp(jnp.minimum(mid - cum, EXP_CLAMP))
                gk = dkd * kds[i]
                to_mid = jnp.sum(gk, axis=0, keepdims=True) - jnp.sum(gq[SUB * i:SUB * (i + 1)], axis=0, keepdims=True)
                dcum = dcum - gk + jnp.where(row == SUB * i + SUB // 2 - 1, to_mid, 0.0)
            dlf = _dot_sel(tri_t, dcum)
            df = dlf / f - dk
            dz = df * (1.0 - lb) * sz * sneg
            dlb = jnp.sum(df * sneg, axis=0, keepdims=True) * lb * (1.0 - lb)
            dtab_ref[0:1, sl] += dlb
            dtab_ref[1:2, sl] -= dlb
            sq = _sig(qr)
            dp_ref[:, sl] = (dq * sq * (1.0 + qr * (1.0 - sq))).astype(BF16)
            dp_ref[:, HGW + HD * h:HGW + HD * (h + 1)] = dz.astype(BF16)
            dp_ref[:, 2 * HGW + HD * h:2 * HGW + HD * (h + 1)] = dv.astype(BF16)
            dp_ref[:, 3 * HGW + HD * h:3 * HGW + HD * (h + 1)] = dgr.astype(BF16)
        dgain_ref[...] += dgain

    rev = lambda cb: pl.BlockSpec((HB, HGW), functools.partial(lambda n, cb: (nblk - 1 - n, cb), cb=cb))
    return pl.pallas_call(
        body, name="hgrn_bwd", grid=(nblk,),
        in_specs=[rev(0), rev(1), rev(2), rev(3), pl.BlockSpec((2, HGW), lambda n: (0, 0)),
                  pl.BlockSpec((1, HD), lambda n: (0, 0)), rev(0),
                  pl.BlockSpec((NH, 1, HD, HD), lambda n: (0, nblk - 1 - n, 0, 0)), rev(0)],
        out_specs=[pl.BlockSpec((HB, 4 * HGW), lambda n: (nblk - 1 - n, 0)),
                   pl.BlockSpec((2, HGW), lambda n: (0, 0)), pl.BlockSpec((1, HD), lambda n: (0, 0))],
        out_shape=[jax.ShapeDtypeStruct((t, 4 * HGW), BF16), jax.ShapeDtypeStruct((2, HGW), F32),
                   jax.ShapeDtypeStruct((1, HD), F32)],
        scratch_shapes=[pltpu.VMEM((NH, HD, HD), F32)],
        compiler_params=_params(("arbitrary",)),
    )(p_hg, p_hg, p_hg, p_hg, table, gain, o_pre, states, dy)


def _rope_mat():
    r = lax.broadcasted_iota(jnp.int32, (ROPE, ROPE), 0)
    c = lax.broadcasted_iota(jnp.int32, (ROPE, ROPE), 1)
    half = ROPE // 2
    return ((r == c - half).astype(F32) - (r == c + half).astype(F32)).astype(BF16)


def _mla_prep_fwd(p_mla, cs, sn, wq, wkv, gql, gkvl, gq, gk, t):
    tm = _pick(t, (512, 256, 128))

    def body(p_ref, cs_ref, sn_ref, wq_ref, wkv_ref, gql_ref, gkvl_ref, gq_ref, gk_ref,
             q_ref, k_ref, v_ref):
        rmat = _rope_mat()
        cqn = _rms(p_ref[:, 0:QL], gql_ref[...]).astype(BF16)
        ckvn = _rms(p_ref[:, QL:QL + KVL], gkvl_ref[...]).astype(BF16)
        kpe = p_ref[:, QL + KVL:QL + KVL + ROPE]
        c, s = cs_ref[...], sn_ref[...]
        rot = lambda x: x * c + _sel_dot(x, rmat) * s
        for h in range(NH):
            qa, qr = _dot(cqn, wq_ref[h, :, 0:HD]), _dot(cqn, wq_ref[h, :, HD:QK])
            rq = lax.rsqrt((jnp.sum(qa * qa, -1, keepdims=True) + jnp.sum(qr * qr, -1, keepdims=True)) / QK + EPS)
            q_ref[h, :, 0:HD] = (qa * rq * gq_ref[:, 0:HD] * SCALE).astype(BF16)
            q_ref[h, :, HD:QK] = (rot(qr * rq * gq_ref[:, HD:QK]) * SCALE).astype(BF16)
            kn = _dot(ckvn, wkv_ref[h, :, 0:HD])
            rk = lax.rsqrt((jnp.sum(kn * kn, -1, keepdims=True) + jnp.sum(kpe * kpe, -1, keepdims=True)) / QK + EPS)
            k_ref[h, :, 0:HD] = (kn * rk * gk_ref[:, 0:HD]).astype(BF16)
            k_ref[h, :, HD:QK] = rot(kpe * rk * gk_ref[:, HD:QK]).astype(BF16)
            v_ref[h] = _dot(ckvn, wkv_ref[h, :, HD:2 * HD]).astype(BF16)

    whole = lambda a: pl.BlockSpec(a.shape, functools.partial(lambda i, nd: (0,) * nd, nd=a.ndim))
    return pl.pallas_call(
        body, name="mla_prep_fwd", grid=(t // tm,),
        in_specs=[pl.BlockSpec((tm, QL + KVL + ROPE), lambda i: (i, 0)), pl.BlockSpec((tm, ROPE), lambda i: (i, 0)),
                  pl.BlockSpec((tm, ROPE), lambda i: (i, 0))] + [whole(a) for a in (wq, wkv, gql, gkvl, gq, gk)],
        out_specs=[pl.BlockSpec((NH, tm, QK), lambda i: (0, i, 0)), pl.BlockSpec((NH, tm, QK), lambda i: (0, i, 0)),
                   pl.BlockSpec((NH, tm, HD), lambda i: (0, i, 0))],
        out_shape=[jax.ShapeDtypeStruct((NH, t, QK), BF16), jax.ShapeDtypeStruct((NH, t, QK), BF16),
                   jax.ShapeDtypeStruct((NH, t, HD), BF16)],
        compiler_params=_params(("parallel",)),
    )(p_mla, cs, sn, wq, wkv, gql, gkvl, gq, gk)


def _mla_prep_bwd(p_mla, cs, sn, wq, wkv, gql, gkvl, gq, gk, dq, dk, dv, t):
    tm = _pick(t, (512, 256, 128))

    def body(p_ref, cs_ref, sn_ref, wq_ref, wkv_ref, gql_ref, gkvl_ref, gq_ref, gk_ref,
             dq_ref, dk_ref, dv_ref,
             dp_ref, dwq_ref, dwkv_ref, dgql_ref, dgkvl_ref, dgq_ref, dgk_ref):
        accs = (dwq_ref, dwkv_ref, dgql_ref, dgkvl_ref, dgq_ref, dgk_ref)

        @pl.when(pl.program_id(0) == 0)
        def _():
            for r in accs:
                r[...] = jnp.zeros_like(r)

        rmat = _rope_mat()
        rmat_t = -rmat
        cq, ckv = p_ref[:, 0:QL], p_ref[:, QL:QL + KVL]
        kpe = p_ref[:, QL + KVL:QL + KVL + ROPE]
        cqn_f, ckvn_f = _rms(cq, gql_ref[...]), _rms(ckv, gkvl_ref[...])
        cqn, ckvn = cqn_f.astype(BF16), ckvn_f.astype(BF16)
        cqn_t, ckvn_t = cqn_f.T.astype(BF16), ckvn_f.T.astype(BF16)
        c, s = cs_ref[...], sn_ref[...]
        unrot = lambda dy: dy * c + _sel_dot(dy * s, rmat_t)
        dcqn = jnp.zeros((tm, QL), F32)
        dckvn = jnp.zeros((tm, KVL), F32)
        dkpe = jnp.zeros((tm, ROPE), F32)
        dgq_a, dgq_r = jnp.zeros((1, HD), F32), jnp.zeros((1, ROPE), F32)
        dgk_a, dgk_r = jnp.zeros((1, HD), F32), jnp.zeros((1, ROPE), F32)
        for h in range(NH):
            qa, qr = _dot(cqn, wq_ref[h, :, 0:HD]), _dot(cqn, wq_ref[h, :, HD:QK])
            rq = lax.rsqrt((jnp.sum(qa * qa, -1, keepdims=True) + jnp.sum(qr * qr, -1, keepdims=True)) / QK + EPS)
            xa, xr = qa * rq, qr * rq
            dya = dq_ref[h, :, 0:HD] * SCALE
            dyr = unrot(dq_ref[h, :, HD:QK] * SCALE)
            dgq_a += jnp.sum(dya * xa, axis=0, keepdims=True)
            dgq_r += jnp.sum(dyr * xr, axis=0, keepdims=True)
            ga, gr_ = dya * gq_ref[:, 0:HD], dyr * gq_ref[:, HD:QK]
            mean = (jnp.sum(ga * xa, -1, keepdims=True) + jnp.sum(gr_ * xr, -1, keepdims=True)) / QK
            dqa = (rq * (ga - xa * mean)).astype(BF16)
            dqr = (rq * (gr_ - xr * mean)).astype(BF16)
            dwq_ref[h, :, 0:HD] += _dot(cqn_t, dqa)
            dwq_ref[h, :, HD:QK] += _dot(cqn_t, dqr)
            dcqn += _dot_nt(dqa, wq_ref[h, :, 0:HD]) + _dot_nt(dqr, wq_ref[h, :, HD:QK])
            kn = _dot(ckvn, wkv_ref[h, :, 0:HD])
            rk = lax.rsqrt((jnp.sum(kn * kn, -1, keepdims=True) + jnp.sum(kpe * kpe, -1, keepdims=True)) / QK + EPS)
            ya, yr = kn * rk, kpe * rk
            dka = dk_ref[h, :, 0:HD]
            dkr = unrot(dk_ref[h, :, HD:QK])
            dgk_a += jnp.sum(dka * ya, axis=0, keepdims=True)
            dgk_r += jnp.sum(dkr * yr, axis=0, keepdims=True)
            ha, hr = dka * gk_ref[:, 0:HD], dkr * gk_ref[:, HD:QK]
            mean = (jnp.sum(ha * ya, -1, keepdims=True) + jnp.sum(hr * yr, -1, keepdims=True)) / QK
            dkn = (rk * (ha - ya * mean)).astype(BF16)
            dkpe += rk * (hr - yr * mean)
            dvh = dv_ref[h].astype(BF16)
            dwkv_ref[h, :, 0:HD] += _dot(ckvn_t, dkn)
            dwkv_ref[h, :, HD:2 * HD] += _dot(ckvn_t, dvh)
            dckvn += _dot_nt(dkn, wkv_ref[h, :, 0:HD]) + _dot_nt(dvh, wkv_ref[h, :, HD:2 * HD])
        dcq, dg1 = _rms_bwd(cq, gql_ref[...], dcqn)
        dckv, dg2 = _rms_bwd(ckv, gkvl_ref[...], dckvn)
        dp_ref[:, 0:QL] = dcq.astype(BF16)
        dp_ref[:, QL:QL + KVL] = dckv.astype(BF16)
        dp_ref[:, QL + KVL:QL + KVL + ROPE] = dkpe.astype(BF16)
        dgql_ref[...] += dg1
        dgkvl_ref[...] += dg2
        dgq_ref[:, 0:HD] += dgq_a
        dgq_ref[:, HD:QK] += dgq_r
        dgk_ref[:, 0:HD] += dgk_a
        dgk_ref[:, HD:QK] += dgk_r

    whole = lambda a: pl.BlockSpec(a.shape, functools.partial(lambda i, nd: (0,) * nd, nd=a.ndim))
    acc_shapes = [wq.shape, wkv.shape, gql.shape, gkvl.shape, gq.shape, gk.shape]
    return pl.pallas_call(
        body, name="mla_prep_bwd", grid=(t // tm,),
        in_specs=[pl.BlockSpec((tm, QL + KVL + ROPE), lambda i: (i, 0)), pl.BlockSpec((tm, ROPE), lambda i: (i, 0)),
                  pl.BlockSpec((tm, ROPE), lambda i: (i, 0))]
        + [whole(a) for a in (wq, wkv, gql, gkvl, gq, gk)]
        + [pl.BlockSpec((NH, tm, QK), lambda i: (0, i, 0)), pl.BlockSpec((NH, tm, QK), lambda i: (0, i, 0)),
           pl.BlockSpec((NH, tm, HD), lambda i: (0, i, 0))],
        out_specs=[pl.BlockSpec((tm, QL + KVL + ROPE), lambda i: (i, 0))]
        + [pl.BlockSpec(s, functools.partial(lambda i, nd: (0,) * nd, nd=len(s))) for s in acc_shapes],
        out_shape=[jax.ShapeDtypeStruct((t, QL + KVL + ROPE), BF16)]
        + [jax.ShapeDtypeStruct(s, F32) for s in acc_shapes],
        compiler_params=_params(("arbitrary",)),
    )(p_mla, cs, sn, wq, wkv, gql, gkvl, gq, gk, dq, dk, dv)


def _chunk_mask(nq, nk, key0, keys_on_rows):
    shape = (nk, nq) if keys_on_rows else (nq, nk)
    qi = lax.broadcasted_iota(jnp.int32, shape, 1 if keys_on_rows else 0) // CHUNK
    ki = lax.broadcasted_iota(jnp.int32, shape, 0 if keys_on_rows else 1) // CHUNK + key0 // CHUNK
    return ki <= qi


def _flash_fwd(q, k, v, t):
    tq = _pick(t, ATT_Q)
    tk = tq // ATT_KEY_TILES

    def body(q_ref, k_ref, v_ref, o_ref, lse_ref):
        i = pl.program_id(1)
        qt = q_ref[0]

        def step(j, carry, key0):
            m, l, acc = carry
            cols = pl.ds(pl.multiple_of(j * tk, tk), tk)
            s = _dot_nt(qt, k_ref[0, cols, :])
            if key0 is not None:
                s = jnp.where(_chunk_mask(tq, tk, key0, False), s, -jnp.inf)
            m_new = jnp.maximum(m, jnp.max(s, axis=-1, keepdims=True))
            p = jnp.exp(s - m_new)
            alpha = jnp.exp(m - m_new)
            return m_new, alpha * l + jnp.sum(p, axis=-1, keepdims=True), alpha * acc + _dot(p, v_ref[0, cols, :])

        init = (jnp.full((tq, 1), -jnp.inf, F32), jnp.zeros((tq, 1), F32), jnp.zeros((tq, HD), F32))
        carry = lax.fori_loop(0, ATT_KEY_TILES * i, lambda j, cr: step(j, cr, None), init)
        for h in range(ATT_KEY_TILES):
            carry = step(ATT_KEY_TILES * i + h, carry, h * tk)
        m, l, acc = carry
        o_ref[...] = (acc / l).astype(BF16)
        lse_ref[0] = m + jnp.log(l)

    return pl.pallas_call(
        body, name="flash_fwd", grid=(NH, t // tq),
        in_specs=[pl.BlockSpec((1, tq, QK), lambda h, i: (h, i, 0)), pl.BlockSpec((1, t, QK), lambda h, i: (h, 0, 0)),
                  pl.BlockSpec((1, t, HD), lambda h, i: (h, 0, 0))],
        out_specs=[pl.BlockSpec((tq, HD), lambda h, i: (i, h)), pl.BlockSpec((1, tq, 1), lambda h, i: (h, i, 0))],
        out_shape=[jax.ShapeDtypeStruct((t, NH * HD), BF16), jax.ShapeDtypeStruct((NH, t, 1), F32)],
        compiler_params=_params(("parallel", "parallel")),
    )(q, k, v)


def _attn_out_bwd(dy, w, o, t):
    tm = _pick(t, ATT_TILES)

    def body(dy_ref, w_ref, o_ref, do_ref, delta_ref):
        do = _dot_nt(dy_ref[...], w_ref[...]).astype(BF16)
        do_ref[...] = do
        ones = jnp.ones((SUBLANE, HD), BF16)
        rowsum = lambda p: lax.dot_general(ones, p, (((1,), (1,)), ((), ())), preferred_element_type=F32)
        for h in range(NH):
            sl = slice(HD * h, HD * (h + 1))
            x1, x2, x3 = _split3(do[:, sl].astype(F32) * o_ref[:, sl].astype(F32))
            delta_ref[h] = (rowsum(x1) + rowsum(x2) + rowsum(x3))[0:1, :]

    return pl.pallas_call(
        body, name="attn_out_bwd", grid=(t // tm,),
        in_specs=[_BS((tm, dy.shape[1]), lambda i: (i, 0)), _BS(w.shape, lambda i: (0, 0)),
                  _BS((tm, NH * HD), lambda i: (i, 0))],
        out_specs=[_BS((tm, NH * HD), lambda i: (i, 0)), _BS((NH, 1, tm), lambda i: (0, 0, i))],
        out_shape=[jax.ShapeDtypeStruct((t, NH * HD), BF16), jax.ShapeDtypeStruct((NH, 1, t), F32)],
        compiler_params=_params(("parallel",)),
    )(dy, w, o)


def _flash_bwd(q, k, v, do, lse_row, delta_row, t):
    tq = _pick(t, ATT_Q)
    tk = tq // ATT_KEY_TILES

    def body(q_ref, k_ref, v_ref, do_ref, lse_ref, delta_ref, dq_ref, dk_ref, dv_ref):
        j = pl.program_id(1)

        @pl.when(j == 0)
        def _():
            dq_ref[...] = jnp.zeros_like(dq_ref)

        kt, vt = k_ref[0], v_ref[0]

        def step(i, carry, key0):
            dk, dv = carry
            rows = pl.ds(pl.multiple_of(i * tq, tq), tq)
            qt, dot_ = q_ref[0, rows, :], do_ref[rows, :]
            p = jnp.exp(_dot_nt(kt, qt) - lse_ref[0, :, rows])
            if key0 is not None:
                p = jnp.where(_chunk_mask(tq, tk, key0, True), p, 0.0)
            ds = (p * (_dot_nt(vt, dot_) - delta_ref[0, :, rows])).astype(BF16)
            dq_ref[0, rows, :] += _dot_tn(ds, kt)
            return dk + _dot(ds, qt), dv + _dot(p, dot_)

        own = j // ATT_KEY_TILES
        carry = step(own, (jnp.zeros((tk, QK), F32), jnp.zeros((tk, HD), F32)), (j % ATT_KEY_TILES) * tk)
        dk, dv = lax.fori_loop(own + 1, t // tq, lambda i, cr: step(i, cr, None), carry)
        dk_ref[0] = dk
        dv_ref[0] = dv

    return pl.pallas_call(
        body, name="flash_bwd", grid=(NH, t // tk),
        in_specs=[pl.BlockSpec((1, t, QK), lambda h, j: (h, 0, 0)), pl.BlockSpec((1, tk, QK), lambda h, j: (h, j, 0)),
                  pl.BlockSpec((1, tk, HD), lambda h, j: (h, j, 0)), pl.BlockSpec((t, HD), lambda h, j: (0, h)),
                  pl.BlockSpec((1, 1, t), lambda h, j: (h, 0, 0)), pl.BlockSpec((1, 1, t), lambda h, j: (h, 0, 0))],
        out_specs=[pl.BlockSpec((1, t, QK), lambda h, j: (h, 0, 0)), pl.BlockSpec((1, tk, QK), lambda h, j: (h, j, 0)),
                   pl.BlockSpec((1, tk, HD), lambda h, j: (h, j, 0))],
        out_shape=[jax.ShapeDtypeStruct((NH, t, QK), F32), jax.ShapeDtypeStruct((NH, t, QK), F32),
                   jax.ShapeDtypeStruct((NH, t, HD), F32)],
        compiler_params=_params(("parallel", "arbitrary")),
    )(q, k, v, do, lse_row, delta_row)


def _ffn_in(xn, w_in, name):
    t, k = xn.shape
    s, _, n = w_in.shape
    tm = _pick(t, ROW_TILES)

    def body(x_ref, w_ref, gu_ref, a_ref):
        x = x_ref[...]
        g, u = _dot(x, w_ref[0]), _dot(x, w_ref[1])
        gu_ref[0] = g.astype(BF16)
        gu_ref[1] = u.astype(BF16)
        a_ref[...] = (g * _sig(g) * u).astype(BF16)

    return pl.pallas_call(
        body, name=name, grid=(t // tm, s // 2),
        in_specs=[_BS((tm, k), lambda i, j: (i, 0)), _BS((2, None, k, n), lambda i, j: (0, j, 0, 0))],
        out_specs=[_BS((2, None, tm, n), lambda i, j: (0, j, i, 0)), _BS((None, tm, n), lambda i, j: (j, i, 0))],
        out_shape=[jax.ShapeDtypeStruct((2, s // 2, t, n), BF16), jax.ShapeDtypeStruct((s // 2, t, n), BF16)],
        compiler_params=_params(("parallel", "parallel")),
    )(xn, w_in.reshape(2, s // 2, k, n))


def _ffn_dgu(dfo, w_out, gu, name):
    t, k = dfo.shape
    s, n, _ = w_out.shape
    tm = _pick(t, ROW_TILES)

    def body(d_ref, w_ref, gu_ref, o_ref):
        da = _dot_nt(d_ref[...], w_ref[...])
        g, u = gu_ref[0].astype(F32), gu_ref[1].astype(F32)
        sg = _sig(g)
        o_ref[0] = (da * u * sg * (1.0 + g * (1.0 - sg))).astype(BF16)
        o_ref[1] = (da * g * sg).astype(BF16)

    pair = _BS((2, None, tm, n), lambda i, j: (0, j, i, 0))
    return pl.pallas_call(
        body, name=name, grid=(t // tm, s),
        in_specs=[_BS((tm, k), lambda i, j: (i, 0)), _BS((None, n, k), lambda i, j: (j, 0, 0)), pair],
        out_specs=pair, out_shape=jax.ShapeDtypeStruct((2, s, t, n), BF16),
        compiler_params=_params(("parallel", "parallel")),
    )(dfo, w_out, gu).reshape(2 * s, t, n)


def _ffn_fwd(xn, w_in, w_out, tag, epilogue):
    gu, a = _ffn_in(xn, w_in, tag + "_in")
    return gu, a, _mm_stack_red(a, w_out, tag + "_out", "nn", epilogue)


def _ffn_bwd(dfo, xn, gu, a, w_in, w_out, tag, epilogue):
    dw_out = _mm_stack_tn(a, dfo, tag + "_dwout")
    dgu = _ffn_dgu(dfo, w_out, gu, tag + "_dgu")
    return _mm_stack_red(dgu, w_in, tag + "_dxn", "nt", epilogue), _mm_stack_tn(xn, dgu, tag + "_dwin"), dw_out


def _local_step(x, target, cs, sn, w, late_weights, s, early_grads):
    t = x.shape[0]
    tm = _pick(t, (256, 128))
    g = {}
    ffn_out = lambda n: w[n].reshape(4, FF // 4, D)

    def norm_fn(x_ref, g_ref, o_ref):
        o_ref[...] = _rms(x_ref[...], g_ref[...]).astype(BF16)

    xn1 = _rows(norm_fn, "norm1", t, tm, [(x, D, 0)], [s["ffn1_norm"]], [(D, BF16)])[0]
    def res_norm_fn(scale):
        def fn(h_ref, f_ref, g_ref, h_out, n_out):
            h = h_ref[...] + scale * f_ref[...]
            h_out[...] = h
            n_out[...] = _rms(h, g_ref[...]).astype(BF16)
        return fn

    gu1, a1, (h1, u) = _ffn_fwd(xn1, w["ffn1_w_in"], ffn_out("ffn1_w_out"), "ffn1",
                                (res_norm_fn(0.5), x, [], [s["mix_norm"]], [(D, F32), (D, BF16)], []))
    w = {**w, **late_weights(h1)}
    rows_of = lambda n: w[n].reshape(-1, w[n].shape[2])
    w_hgb, w_mlab, w_o = rows_of("w_hg_branch"), rows_of("w_mla_branch"), rows_of("w_out")
    w_in_nat = w["w_in"].transpose(1, 0, 2).reshape(D, -1)
    w_mrg = w["w_merge"]
    mw = w_mrg.shape[2]
    w_in_hg, w_in_mla = w_in_nat[:, :4 * HGW], w_in_nat[:, 4 * HGW:]
    p_hg = _mm(u, w_in_hg, "nn", "proj_hg")
    p_mla = _mm(u, w_in_mla, "nn", "proj_mla")
    gpre = _cols_fwd(u, w_mrg, "proj_gate")
    o_pre, hgy, states = _hgrn_fwd(p_hg, s["hg_lb_table"], s["hg_out_norm"], t)
    prep_args = (p_mla, cs, sn, w["w_q_up"], w["w_kv_up"], s["mla_q_lora_norm"], s["mla_kv_lora_norm"],
                 s["q_head_norm"], s["k_head_norm"])
    q, k, v = _mla_prep_fwd(*prep_args, t)
    att, lse = _flash_fwd(q, k, v, t)
    y_hg = _mm(hgy, w_hgb, "nn", "branch_hg")
    y_mla = _mm(att, w_mlab, "nn", "branch_mla")

    def mix_fn(gh_ref, gm_ref, yh_ref, ym_ref, b_ref, o_ref):
        gh = _sig(gh_ref[...] + b_ref[:, 0:D])
        gm = _sig(gm_ref[...] + b_ref[:, D:2 * D])
        o_ref[...] = (gh * yh_ref[...] + gm * ym_ref[...]).astype(BF16)

    mixed = _rows(mix_fn, "mix", t, tm, [(gpre, D, 0), (gpre, D, 1), (y_hg, D, 0), (y_mla, D, 0)], [s["b_merge"]],
                  [(D, BF16)])[0]
    h2, xn2 = _mm_stack_red(mixed[None], w_o[None], "mix_out", "nn",
                            (res_norm_fn(1.0), h1, [], [s["ffn2_norm"]], [(D, F32), (D, BF16)], []))

    def loss_fn(h_ref, f_ref, tg_ref, g_ref, dh_out, dhb_out):
        h = h_ref[...] + 0.5 * f_ref[...]
        e = _rms(h, g_ref[...]) - tg_ref[...]
        dh, dgain = _rms_bwd(h, g_ref[...], e / D)
        dh_out[...] = dh
        dhb_out[...] = (0.5 * dh).astype(BF16)
        return dgain, jnp.full((1, LANE), 0.5 / D * jnp.sum(e * e), F32)

    gu2, a2, (dh3, dfo2, g["final_norm"], loss) = _ffn_fwd(
        xn2, w["ffn2_w_in"], ffn_out("ffn2_w_out"), "ffn2",
        (loss_fn, h2, [target], [s["final_norm"]], [(D, F32), (D, BF16)], [(1, D), (1, LANE)]))

    def norm_bwd_fn(scale):
        def fn(h_ref, dxn_ref, dh_ref, g_ref, dh_out, dhb_out):
            dx, dgain = _rms_bwd(h_ref[...], g_ref[...], dxn_ref[...])
            dh = dh_ref[...] + dx
            dh_out[...] = dh
            dhb_out[...] = (scale * dh).astype(BF16)
            return (dgain,)
        return fn

    as_rows = lambda a: a.reshape((N_DEV, -1) + a.shape[-1:])
    (dh2, dh2b, g["ffn2_norm"]), g["ffn2_w_in"], dwo = _ffn_bwd(
        dfo2, xn2, gu2, a2, w["ffn2_w_in"], ffn_out("ffn2_w_out"), "ffn2",
        (norm_bwd_fn(1.0), h2, [dh3], [s["ffn2_norm"]], [(D, F32), (D, BF16)], [(1, D)]))
    g["ffn2_w_out"] = as_rows(dwo)
    dmixed = _mm(dh2b, w_o, "nt", "mix_out_dx")
    g["w_out"] = as_rows(_mm(mixed, dh2b, "tn", "mix_out_dw"))

    def mix_bwd_fn(gh_ref, gm_ref, yh_ref, ym_ref, dm_ref, b_ref, dyh_out, dym_out, dg_out):
        gh = _sig(gh_ref[...] + b_ref[:, 0:D])
        gm = _sig(gm_ref[...] + b_ref[:, D:2 * D])
        dm = dm_ref[...]
        dyh_out[...] = (dm * gh).astype(BF16)
        dym_out[...] = (dm * gm).astype(BF16)
        dgh = dm * yh_ref[...] * gh * (1.0 - gh)
        dgm = dm * ym_ref[...] * gm * (1.0 - gm)
        dg_out[:, 0:D] = dgh.astype(BF16)
        dg_out[:, D:2 * D] = dgm.astype(BF16)
        return (jnp.concatenate([jnp.sum(dgh, axis=0, keepdims=True), jnp.sum(dgm, axis=0, keepdims=True)], axis=1),)

    dyh, dym, dgpre, g["b_merge"] = _rows(
        mix_bwd_fn, "mix_bwd", t, tm, [(gpre, D, 0), (gpre, D, 1), (y_hg, D, 0), (y_mla, D, 0), (dmixed, D, 0)],
        [s["b_merge"]], [(D, BF16), (D, BF16), (2 * D, BF16)], [(1, 2 * D)])
    g["w_hg_branch"] = as_rows(_mm(hgy, dyh, "tn", "branch_hg_dw"))
    g["w_mla_branch"] = as_rows(_mm(att, dym, "tn", "branch_mla_dw"))
    g["w_merge"] = _cols_dw(u, dgpre, mw, "proj_gate_dw")
    dhgy = _mm(dyh, w_hgb, "nt", "branch_hg_dx")
    datt, delta = _attn_out_bwd(dym, w_mlab, att, t)
    du_gate = _cols_dx(dgpre, w_mrg, "proj_gate_dx")

    dq, dk, dv = _flash_bwd(q, k, v, datt, lse.reshape(NH, 1, t), delta, t)
    (dp_mla, g["w_q_up"], g["w_kv_up"], g["mla_q_lora_norm"], g["mla_kv_lora_norm"], g["q_head_norm"],
     g["k_head_norm"]) = _mla_prep_bwd(*prep_args, dq, dk, dv, t)
    dp_hg, g["hg_lb_table"], g["hg_out_norm"] = _hgrn_bwd(p_hg, s["hg_lb_table"], s["hg_out_norm"], o_pre, states,
                                                          dhgy, t)
    dw_in_nat = jnp.concatenate([_mm(u, dp_hg, "tn", "proj_hg_dw"), _mm(u, dp_mla, "tn", "proj_mla_dw")], axis=1)
    g["w_in"] = dw_in_nat.reshape(D, N_DEV, -1).transpose(1, 0, 2)
    du_hg = _mm(dp_hg, w_in_hg, "nt", "proj_hg_dx")
    du_mla = _mm(dp_mla, w_in_mla, "nt", "proj_mla_dx")

    def mixnorm_bwd_fn(h_ref, a_ref, b_ref, c_ref, dh_ref, g_ref, dh_out, dhb_out):
        dx, dgain = _rms_bwd(h_ref[...], g_ref[...], a_ref[...] + b_ref[...] + c_ref[...])
        dh = dh_ref[...] + dx
        dh_out[...] = dh
        dhb_out[...] = (0.5 * dh).astype(BF16)
        return (dgain,)

    mix_gain = s["mix_norm"] + early_grads(g)[0:1, 0:1]
    dh1, dfo1, g["mix_norm"] = _rows(mixnorm_bwd_fn, "mixnorm_bwd", t, tm,
                                     [(h1, D, 0), (du_hg, D, 0), (du_mla, D, 0), (du_gate, D, 0), (dh2, D, 0)],
                                     [mix_gain], [(D, F32), (D, BF16)], [(1, D)])
    (grad_x, _, g["ffn1_norm"]), g["ffn1_w_in"], dwo = _ffn_bwd(
        dfo1, xn1, gu1, a1, w["ffn1_w_in"], ffn_out("ffn1_w_out"), "ffn1",
        (norm_bwd_fn(1.0), x, [dh1], [s["ffn1_norm"]], [(D, F32), (D, BF16)], [(1, D)]))
    g["ffn1_w_out"] = as_rows(dwo)
    return loss, grad_x, g


def _coords():
    return lax.axis_index("x"), lax.axis_index("y"), lax.axis_index("c")


def _hbm_call(body, name, ins, out_shapes, scratch):
    any_spec = pl.BlockSpec(memory_space=pl.ANY)
    return pl.pallas_call(
        body, name=name, out_shape=[jax.ShapeDtypeStruct(s, dt) for s, dt in out_shapes],
        in_specs=[any_spec] * len(ins), out_specs=[any_spec] * len(out_shapes), scratch_shapes=scratch,
    )(*ins)


def _my_slot():
    return 4 * lax.axis_index("x") + 2 * lax.axis_index("y") + lax.axis_index("c")


def _put_own(buf, own, index):
    return lax.dynamic_update_index_in_dim(buf, own, index, 0)


def _all_gather(blocks, name):
    nb = len(blocks)

    def body(*refs):
        x_refs, out_refs = refs[:nb], refs[nb:2 * nb]
        send_sems, recv_sems = refs[2 * nb:]
        x, y, c = _coords()
        me, sibling = (x, y, c), (x, y, 1 - c)
        chips = [(1 - x, y), (x, 1 - y), (1 - x, 1 - y)]

        def slot(b, px, py, pc):
            return out_refs[b].at[4 * px + 2 * py + pc]

        def copy(b, kk, block_of, to, src=None):
            return pltpu.make_async_remote_copy(
                src_ref=slot(b, *block_of) if src is None else src, dst_ref=slot(b, *block_of),
                send_sem=send_sems.at[b, kk], recv_sem=recv_sems.at[b, kk], device_id=to, device_id_type=MESH)

        first = [copy(b, 0, me, sibling, src=x_refs[b]) for b in range(nb)]
        first += [copy(b, 1 + j, me, (*chip, c), src=x_refs[b]) for j, chip in enumerate(chips) for b in range(nb)]
        for cp in first:
            cp.start()
        passed = []
        for j, chip in enumerate(chips):
            for b in range(nb):
                copy(b, 1 + j, (*chip, c), me).wait_recv()
                passed.append(copy(b, 4 + j, (*chip, c), sibling))
                passed[-1].start()
        for b in range(nb):
            copy(b, 0, sibling, me).wait_recv()
        for j, chip in enumerate(chips):
            for b in range(nb):
                copy(b, 4 + j, (*chip, 1 - c), me).wait_recv()
        for cp in first + passed:
            cp.wait_send()

    outs = _hbm_call(body, name, blocks, [((N_DEV,) + b.shape, b.dtype) for b in blocks],
                     [pltpu.SemaphoreType.DMA((nb, 7)), pltpu.SemaphoreType.DMA((nb, 7))])
    return [_put_own(o, b[None], _my_slot()) for o, b in zip(outs, blocks)]


def _gather_peers():
    x, y, c = _coords()
    return (x, y, c), [(x, y, 1 - c), (1 - x, y, c), (x, 1 - y, c), (1 - x, 1 - y, c)]


def _gather_start(blocks, after, name):
    nb = len(blocks)
    hbm, sem = pl.BlockSpec(memory_space=pltpu.HBM), pl.BlockSpec(memory_space=pltpu.SEMAPHORE)

    def body(*refs):
        x_refs, out_refs = refs[:nb], refs[nb:2 * nb]
        send_sems, recv_sems, token = refs[2 * nb + 1], refs[2 * nb + 2], refs[-1]
        (x, y, c), peers = _gather_peers()
        for kk, peer in enumerate(peers):
            for b in range(nb):
                pltpu.make_async_remote_copy(
                    src_ref=x_refs[b], dst_ref=out_refs[b].at[4 * x + 2 * y + c], send_sem=send_sems.at[4 * b + kk],
                    recv_sem=recv_sems.at[4 * b + kk], device_id=peer, device_id_type=MESH).start()
        token[...] = jnp.zeros_like(token)

    gathers = [pltpu.with_memory_space_constraint(lax.empty((N_DEV,) + b.shape, b.dtype), pltpu.HBM) for b in blocks]
    outs = pl.pallas_call(
        body, name=name,
        out_shape=(pltpu.SemaphoreType.DMA((4 * nb,)), pltpu.SemaphoreType.DMA((4 * nb,)),
                   *[pltpu.HBM(b.shape, b.dtype) for b in blocks], *[pltpu.HBM(b.shape, b.dtype) for b in gathers],
                   jax.ShapeDtypeStruct((SUBLANE, LANE), F32)),
        in_specs=[hbm] * (2 * nb) + [pl.BlockSpec(memory_space=pl.ANY)],
        out_specs=(sem, sem, *[hbm] * (2 * nb), pl.BlockSpec(memory_space=pltpu.VMEM)),
        input_output_aliases={i: 2 + i for i in range(2 * nb)},
        compiler_params=pltpu.CompilerParams(has_side_effects=pltpu.SideEffectType.DATAFLOW_SIDE_EFFECTING),
    )(*[pltpu.with_memory_space_constraint(b, pltpu.HBM) for b in blocks], *gathers, after)
    return outs[0], outs[1], list(outs[2:2 + nb]), list(outs[2 + nb:2 + 2 * nb]), outs[-1]


def _gather_wait(send_sems, recv_sems, thru, gathers, after, name):
    nb = len(thru)
    hbm, sem = pl.BlockSpec(memory_space=pltpu.HBM), pl.BlockSpec(memory_space=pltpu.SEMAPHORE)

    def body(*refs):
        x_refs, out_refs = refs[:nb], refs[nb:2 * nb]
        send_sems_, recv_sems_ = refs[2 * nb], refs[2 * nb + 1]
        _, peers = _gather_peers()
        for kk, (px, py, pc) in enumerate(peers):
            for b in range(nb):
                cp = pltpu.make_async_remote_copy(
                    src_ref=x_refs[b], dst_ref=out_refs[b].at[4 * px + 2 * py + pc], send_sem=send_sems_.at[4 * b + kk],
                    recv_sem=recv_sems_.at[4 * b + kk], device_id=(px, py, pc), device_id_type=MESH)
                cp.wait_send()
                cp.wait_recv()

    outs = pl.pallas_call(
        body, name=name,
        out_shape=(*[pltpu.HBM(b.shape, b.dtype) for b in thru], *[pltpu.HBM(b.shape, b.dtype) for b in gathers]),
        in_specs=[hbm] * (2 * nb) + [sem, sem, pl.BlockSpec(memory_space=pl.ANY)], out_specs=[hbm] * (2 * nb),
        input_output_aliases={i: i for i in range(2 * nb)},
        compiler_params=pltpu.CompilerParams(has_side_effects=pltpu.SideEffectType.DATAFLOW_SIDE_EFFECTING),
    )(*thru, *gathers, send_sems, recv_sems, after)
    return list(outs[:nb]), list(outs[nb:])


def _gather_finish(blocks, gathers, name):
    nb = len(blocks)

    def body(*refs):
        x_refs, in_refs, out_refs = refs[:nb], refs[nb:2 * nb], refs[2 * nb:3 * nb]
        send_sems, recv_sems = refs[3 * nb:]
        (x, y, c), peers = _gather_peers()
        copies = []
        for j, (px, py, _) in enumerate(peers[1:]):
            for b in range(nb):
                copies.append(pltpu.make_async_remote_copy(
                    src_ref=in_refs[b].at[4 * px + 2 * py + c], dst_ref=out_refs[b].at[4 * px + 2 * py + c],
                    send_sem=send_sems.at[b, j], recv_sem=recv_sems.at[b, j], device_id=(x, y, 1 - c),
                    device_id_type=MESH))
                copies[-1].start()
        for j, (px, py, _) in enumerate(peers[1:]):
            for b in range(nb):
                pltpu.make_async_remote_copy(
                    src_ref=in_refs[b].at[4 * px + 2 * py + c], dst_ref=out_refs[b].at[4 * px + 2 * py + 1 - c],
                    send_sem=send_sems.at[b, j], recv_sem=recv_sems.at[b, j], device_id=(x, y, 1 - c),
                    device_id_type=MESH).wait_recv()
        for cp in copies:
            cp.wait_send()

    any_spec = pl.BlockSpec(memory_space=pl.ANY)
    outs = pl.pallas_call(
        body, name=name, out_shape=[jax.ShapeDtypeStruct(b.shape, b.dtype) for b in gathers],
        in_specs=[any_spec] * (2 * nb), out_specs=[any_spec] * nb,
        input_output_aliases={nb + i: i for i in range(nb)},
        scratch_shapes=[pltpu.SemaphoreType.DMA((nb, 3)), pltpu.SemaphoreType.DMA((nb, 3))],
    )(*blocks, *gathers)
    return [_put_own(o, b[None], _my_slot()) for o, b in zip(outs, blocks)]


def _sibling_swap(bufs, name):
    nb = len(bufs)

    def body(*refs):
        x_refs, out_refs = refs[:nb], refs[nb:2 * nb]
        send_sems, recv_sems = refs[2 * nb:]
        x, y, c = _coords()
        copies = [pltpu.make_async_remote_copy(
            src_ref=x_refs[b].at[2 * q + 1 - c], dst_ref=out_refs[b].at[q], send_sem=send_sems.at[b, q],
            recv_sem=recv_sems.at[b, q], device_id=(x, y, 1 - c), device_id_type=MESH)
            for b in range(nb) for q in range(4)]
        for cp in copies:
            cp.start()
        for cp in copies:
            cp.wait()

    return _hbm_call(body, name, bufs, [((4,) + b.shape[1:], b.dtype) for b in bufs],
                     [pltpu.SemaphoreType.DMA((nb, 4)), pltpu.SemaphoreType.DMA((nb, 4))])


def _chip_exchange(bufs, name):
    nb = len(bufs)

    def body(*refs):
        x_refs, out_refs = refs[:nb], refs[nb:2 * nb]
        send_sems, recv_sems = refs[2 * nb:]
        x, y, c = _coords()
        chips = [(1 - x, y), (x, 1 - y), (1 - x, 1 - y)]
        copies = [pltpu.make_async_remote_copy(
            src_ref=x_refs[b].at[2 * px + py], dst_ref=out_refs[b].at[2 * x + y], send_sem=send_sems.at[b, j],
            recv_sem=recv_sems.at[b, j], device_id=(px, py, c), device_id_type=MESH)
            for j, (px, py) in enumerate(chips) for b in range(nb)]
        for cp in copies:
            cp.start()
        for j, (px, py) in enumerate(chips):
            for b in range(nb):
                pltpu.make_async_remote_copy(
                    src_ref=x_refs[b].at[2 * x + y], dst_ref=out_refs[b].at[2 * px + py], send_sem=send_sems.at[b, j],
                    recv_sem=recv_sems.at[b, j], device_id=(px, py, c), device_id_type=MESH).wait_recv()
        for cp in copies:
            cp.wait_send()

    outs = _hbm_call(body, name, bufs, [(b.shape, b.dtype) for b in bufs],
                     [pltpu.SemaphoreType.DMA((nb, 3)), pltpu.SemaphoreType.DMA((nb, 3))])
    chip = 2 * lax.axis_index("x") + lax.axis_index("y")
    return [_put_own(o, lax.dynamic_index_in_dim(b, chip, 0), chip) for o, b in zip(outs, bufs)]


def _chip_exchange_start(bufs, name):
    nb = len(bufs)
    hbm, sem = pl.BlockSpec(memory_space=pltpu.HBM), pl.BlockSpec(memory_space=pltpu.SEMAPHORE)

    def body(*refs):
        x_refs, land_refs = refs[:nb], refs[nb:2 * nb]
        send_sems, recv_sems, token = refs[2 * nb], refs[2 * nb + 1], refs[-1]
        x, y, c = _coords()
        for j, (px, py) in enumerate([(1 - x, y), (x, 1 - y), (1 - x, 1 - y)]):
            for b in range(nb):
                pltpu.make_async_remote_copy(
                    src_ref=x_refs[b].at[2 * px + py], dst_ref=land_refs[b].at[2 * x + y], send_sem=send_sems.at[3 * b + j],
                    recv_sem=recv_sems.at[3 * b + j], device_id=(px, py, c), device_id_type=MESH).start()
        token[...] = jnp.zeros_like(token)

    lands = [pltpu.with_memory_space_constraint(lax.empty(b.shape, b.dtype), pltpu.HBM) for b in bufs]
    outs = pl.pallas_call(
        body, name=name,
        out_shape=(pltpu.SemaphoreType.DMA((3 * nb,)), pltpu.SemaphoreType.DMA((3 * nb,)),
                   *[pltpu.HBM(b.shape, b.dtype) for b in bufs], *[pltpu.HBM(b.shape, b.dtype) for b in bufs],
                   jax.ShapeDtypeStruct((SUBLANE, LANE), F32)),
        in_specs=[hbm] * (2 * nb), out_specs=(sem, sem, *[hbm] * (2 * nb), pl.BlockSpec(memory_space=pltpu.VMEM)),
        input_output_aliases={i: 2 + i for i in range(2 * nb)},
        compiler_params=pltpu.CompilerParams(has_side_effects=pltpu.SideEffectType.DATAFLOW_SIDE_EFFECTING),
    )(*[pltpu.with_memory_space_constraint(b, pltpu.HBM) for b in bufs], *lands)
    return outs[0], outs[1], list(outs[2:2 + nb]), list(outs[2 + nb:2 + 2 * nb]), outs[-1]


def _chip_exchange_wait(send_sems, recv_sems, thru, lands, after, name):
    nb = len(thru)
    hbm, sem = pl.BlockSpec(memory_space=pltpu.HBM), pl.BlockSpec(memory_space=pltpu.SEMAPHORE)

    def body(*refs):
        x_refs, land_refs = refs[:nb], refs[nb:2 * nb]
        send_sems_, recv_sems_ = refs[2 * nb], refs[2 * nb + 1]
        x, y, c = _coords()
        for j, (px, py) in enumerate([(1 - x, y), (x, 1 - y), (1 - x, 1 - y)]):
            for b in range(nb):
                cp = pltpu.make_async_remote_copy(
                    src_ref=x_refs[b].at[2 * px + py], dst_ref=land_refs[b].at[2 * px + py],
                    send_sem=send_sems_.at[3 * b + j], recv_sem=recv_sems_.at[3 * b + j], device_id=(px, py, c),
                    device_id_type=MESH)
                cp.wait_send()
                cp.wait_recv()

    outs = pl.pallas_call(
        body, name=name,
        out_shape=(*[pltpu.HBM(b.shape, b.dtype) for b in thru], *[pltpu.HBM(b.shape, b.dtype) for b in lands]),
        in_specs=[hbm] * (2 * nb) + [sem, sem, pl.BlockSpec(memory_space=pl.ANY)], out_specs=[hbm] * (2 * nb),
        input_output_aliases={i: i for i in range(2 * nb)},
        compiler_params=pltpu.CompilerParams(has_side_effects=pltpu.SideEffectType.DATAFLOW_SIDE_EFFECTING),
    )(*thru, *lands, send_sems, recv_sems, after)
    return list(outs[:nb]), list(outs[nb:])


def _chip_sum(g, r1, c, name):
    _, r, cw = g.shape
    tr = _pick(r, (256, 176, 128))

    def body(c_ref, g_ref, r_ref, o_ref):
        o_ref[...] = (g_ref[...] + r_ref[...]).astype(BF16)

    grid_spec = pltpu.PrefetchScalarGridSpec(
        num_scalar_prefetch=1, grid=(4, r // tr),
        in_specs=[_BS((None, None, tr, cw), lambda q, i, c_ref: (q, c_ref[0], i, 0)),
                  _BS((None, tr, cw), lambda q, i, c_ref: (q, i, 0))],
        out_specs=_BS((None, tr, cw), lambda q, i, c_ref: (q, i, 0)))
    return pl.pallas_call(
        body, name=name, grid_spec=grid_spec, out_shape=jax.ShapeDtypeStruct((4, r, cw), BF16),
        compiler_params=_params(("parallel", "parallel")),
    )(c.reshape(1).astype(jnp.int32), g.reshape(4, 2, r, cw), r1)


def _adamw_math(w, g, m, v):
    m = B1 * m + (1.0 - B1) * g
    v = B2 * v + (1.0 - B2) * (g * g)
    m_hat = m / (1.0 - B1 ** STEP)
    v_hat = v / (1.0 - B2 ** STEP)
    return -LR * (m_hat / (jnp.sqrt(v_hat) + AEPS) + WD * w), m, v


def _sum_adamw(parts, w, m, v, name):
    r, c = w.shape
    tr = _pick(r, (256, 176, 128))

    def body(p0, p1, p2, p3, w_ref, m_ref, v_ref, g_out, d_out, m_out, v_out):
        g = ((p0[...].astype(F32) + p1[...].astype(F32)) + p2[...].astype(F32)) + p3[...].astype(F32)
        g_out[...] = g
        d_out[...], m_out[...], v_out[...] = _adamw_math(w_ref[...], g, m_ref[...], v_ref[...])

    part = lambda q: _BS((None, tr, c), functools.partial(lambda i, q: (q, i, 0), q=q))
    plain = _BS((tr, c), lambda i: (i, 0))
    return pl.pallas_call(
        body, name=name, grid=(r // tr,), in_specs=[part(q) for q in range(4)] + [plain] * 3,
        out_specs=[plain] * 4, out_shape=[jax.ShapeDtypeStruct((r, c), F32)] * 4,
        compiler_params=_params(("parallel",)),
    )(parts, parts, parts, parts, w, m, v)


def _small_update(gathered, w, m, v):
    r = w.shape[0]

    def body(ga_ref, w_ref, m_ref, v_ref, g_out, d_out, m_out, v_out):
        g = ga_ref[0]
        for dev in range(1, N_DEV):
            g = g + ga_ref[dev]
        g_out[...] = g
        d_out[...], m_out[...], v_out[...] = _adamw_math(w_ref[...], g, m_ref[...], v_ref[...])

    return pl.pallas_call(
        body, name="small_update", out_shape=[jax.ShapeDtypeStruct((r, LANE), F32)] * 4,
    )(gathered, w, m, v)


def _pack_small(vals):
    rows = []
    for name, (r, n) in SMALL:
        flat = vals[name].reshape(-1)
        pad = (-flat.shape[0]) % (SUBLANE * LANE)
        rows.append(jnp.pad(flat, (0, pad)).reshape(-1, LANE))
    return jnp.concatenate(rows, axis=0)


def _unpack_small(packed):
    out, off = {}, 0
    for name, (r, n) in SMALL:
        nrow = -(-(r * n) // (SUBLANE * LANE)) * SUBLANE
        out[name] = packed[off:off + nrow].reshape(-1)[:r * n].reshape(r, n)
        off += nrow
    return out


def kernel(x, positions, ffn1_norm, ffn1_w_in, ffn1_w_out, mix_norm, w_in, hg_lb_table, hg_out_norm, w_hg_branch, mla_q_lora_norm, w_q_up, mla_kv_lora_norm, w_kv_up, q_head_norm, k_head_norm, w_mla_branch, w_merge, b_merge, w_out, ffn2_norm, ffn2_w_in, ffn2_w_out, final_norm, loss_target, m_ffn1_norm, m_ffn1_w_in, m_ffn1_w_out, m_mix_norm, m_w_in, m_hg_lb_table, m_hg_out_norm, m_w_hg_branch, m_mla_q_lora_norm, m_w_q_up, m_mla_kv_lora_norm, m_w_kv_up, m_q_head_norm, m_k_head_norm, m_w_mla_branch, m_w_merge, m_b_merge, m_w_out, m_ffn2_norm, m_ffn2_w_in, m_ffn2_w_out, m_final_norm, v_ffn1_norm, v_ffn1_w_in, v_ffn1_w_out, v_mix_norm, v_w_in, v_hg_lb_table, v_hg_out_norm, v_w_hg_branch, v_mla_q_lora_norm, v_w_q_up, v_mla_kv_lora_norm, v_w_kv_up, v_q_head_norm, v_k_head_norm, v_w_mla_branch, v_w_merge, v_b_merge, v_w_out, v_ffn2_norm, v_ffn2_w_in, v_ffn2_w_out, v_final_norm):
    args = dict(locals())
    t = x.shape[1]
    big_w = {n: args[n][0] for n, _, _ in BIG}
    small = {n: args[n].reshape(shape) for n, shape in SMALL}

    names = [n for n, _, _ in BIG]
    first, rest = names[:2], names[2:]
    full = dict(zip(first, _all_gather([big_w[n].astype(BF16) for n in first], "weights_all_gather_ffn1")))
    g_send, g_recv, g_thru, g_bufs, g_token = _gather_start([big_w[n].astype(BF16) for n in rest], full[first[0]],
                                                            "weights_gather_start")
    gains = dict(small, ffn1_norm=small["ffn1_norm"] + g_token[0:1, 0:1])

    def late_weights(after):
        blocks, bufs = _gather_wait(g_send, g_recv, g_thru, g_bufs, after, "weights_gather_wait")
        return dict(zip(rest, _gather_finish(blocks, bufs, "weights_gather_finish")))

    inv_freq = ROPE_THETA ** (-jnp.arange(0, ROPE, 2, dtype=F32) / ROPE)
    ang = positions[0].astype(F32)[:, None] * inv_freq
    cs = jnp.concatenate([jnp.cos(ang), jnp.cos(ang)], axis=1)
    sn = jnp.concatenate([jnp.sin(ang), jnp.sin(ang)], axis=1)

    c = lax.axis_index("c")
    chip = 2 * lax.axis_index("x") + lax.axis_index("y")
    early = {}

    def chip_sums_of(g, ns, tag):
        from_sibling = _sibling_swap([g[n] for n in ns], "grads_sibling_swap_" + tag)
        return [_chip_sum(g[n], r1, c, "chip_sum_" + n) for n, r1 in zip(ns, from_sibling)]

    def early_grads(g):
        early["names"] = [n for n in names if n in g]
        early["sums"] = chip_sums_of(g, early["names"], "early")
        early["send"], early["recv"], early["thru"], early["lands"], token = _chip_exchange_start(
            early["sums"], "grads_exchange_start")
        return token

    loss_row, grad_x, g = _local_step(x[0], loss_target[0], cs, sn, full, late_weights, gains, early_grads)
    late = [n for n in names if n not in early["names"]]
    exchanged = dict(zip(late, _chip_exchange(chip_sums_of(g, late, "late"), "grads_chip_exchange_late")))
    sent, landed = _chip_exchange_wait(early["send"], early["recv"], early["thru"], early["lands"],
                                       exchanged[late[0]], "grads_exchange_wait")
    for n, land, own in zip(early["names"], landed, sent):
        exchanged[n] = lax.dynamic_update_index_in_dim(land, lax.dynamic_index_in_dim(own, chip, 0), chip, 0)

    small_packed = jnp.concatenate([_pack_small(g), jnp.pad(loss_row, ((0, SUBLANE - 1), (0, 0)))], axis=0)
    small_all = _all_gather([small_packed], "small_all_gather")[0]
    zero_tail = jnp.zeros((SUBLANE, LANE), F32)
    pk = lambda d: jnp.concatenate([_pack_small(d), zero_tail], axis=0)
    sg, sd, sm, sv = _small_update(
        small_all, pk(small), pk({n: args["m_" + n].reshape(shape) for n, shape in SMALL}),
        pk({n: args["v_" + n].reshape(shape) for n, shape in SMALL}))
    n_small_rows = _pack_small(small).shape[0]
    loss = sg[n_small_rows, 0]
    outs = {k_: _unpack_small(a) for k_, a in (("grad", sg), ("delta", sd), ("new_m", sm), ("new_v", sv))}

    for n in names:
        outs["grad"][n], outs["delta"][n], outs["new_m"][n], outs["new_v"][n] = _sum_adamw(
            exchanged[n], big_w[n], args["m_" + n][0], args["v_" + n][0], "adamw_" + n)

    def shaped(kind, n):
        return outs[kind][n].reshape(args[n].shape)

    return (loss, grad_x[None], *[shaped("grad", n) for n in WEIGHT_ORDER], *[shaped("delta", n) for n in WEIGHT_ORDER],
            *[shaped("new_m", n) for n in WEIGHT_ORDER], *[shaped("new_v", n) for n in WEIGHT_ORDER])
```

```python
import functools

import jax
import jax.numpy as jnp
from jax import lax
from jax.experimental import pallas as pl
from jax.experimental.pallas import tpu as pltpu

F32 = jnp.float32
BF16 = jnp.bfloat16

D = 1024
FF = 2816
NH = 8
HD = 128
ROPE = 64
QK = HD + ROPE
QL = 384
KVL = 256
HGW = NH * HD
CHUNK = 64
EPS = 1e-6
ROPE_THETA = 10000.0
SCALE = QK ** -0.5

LR, B1, B2, AEPS, WD, STEP = 0.001, 0.9, 0.999, 1e-08, 0.01, 10

HB = 128
SUB = 16
EXP_CLAMP = 80.0
ATT_TILES = (512, 256, 128)
ATT_Q = (1024, 512, 256, 128)
ATT_KEY_TILES = 1
ROW_TILES = (1024, 512, 256, 128)

LANE = 128
SUBLANE = 8
VMEM_LIMIT = 56 << 20

N_DEV = 8
MESH = pl.DeviceIdType.MESH

BIG = (
    ("ffn1_w_in", (D, 2 * FF), 1), ("ffn1_w_out", (FF, D), 0), ("w_in", (D, 4800), 1),
    ("w_hg_branch", (HGW, D), 0), ("w_q_up", (QL, NH * QK), 1), ("w_kv_up", (KVL, NH * 2 * HD), 1),
    ("w_mla_branch", (NH * HD, D), 0), ("w_merge", (D, 2 * D), 1), ("w_out", (D, D), 0),
    ("ffn2_w_in", (D, 2 * FF), 1), ("ffn2_w_out", (FF, D), 0),
)
SMALL = (
    ("ffn1_norm", (1, D)), ("mix_norm", (1, D)), ("hg_lb_table", (2, HGW)), ("hg_out_norm", (1, HD)),
    ("mla_q_lora_norm", (1, QL)), ("mla_kv_lora_norm", (1, KVL)), ("q_head_norm", (1, QK)),
    ("k_head_norm", (1, QK)), ("b_merge", (1, 2 * D)), ("ffn2_norm", (1, D)), ("final_norm", (1, D)),
)
WEIGHT_ORDER = ("ffn1_norm", "ffn1_w_in", "ffn1_w_out", "mix_norm", "w_in", "hg_lb_table", "hg_out_norm",
                "w_hg_branch", "mla_q_lora_norm", "w_q_up", "mla_kv_lora_norm", "w_kv_up", "q_head_norm",
                "k_head_norm", "w_mla_branch", "w_merge", "b_merge", "w_out", "ffn2_norm", "ffn2_w_in",
                "ffn2_w_out", "final_norm")


def _pick(n, cands):
    for c in cands:
        if n % c == 0:
            return c
    return n


def _params(sem):
    return pltpu.CompilerParams(dimension_semantics=sem, vmem_limit_bytes=VMEM_LIMIT)


def _sig(x):
    return 1.0 / (1.0 + jnp.exp(-x))


def _dot(a, b):
    return jnp.dot(a.astype(BF16), b.astype(BF16), preferred_element_type=F32)


def _dot_nt(a, b):
    return lax.dot_general(a.astype(BF16), b.astype(BF16), (((1,), (1,)), ((), ())),
                           preferred_element_type=F32)


def _dot_tn(a, b):
    return lax.dot_general(a.astype(BF16), b.astype(BF16), (((0,), (0,)), ((), ())),
                           preferred_element_type=F32)


def _split3(x):
    x1 = x.astype(BF16)
    r1 = x - x1.astype(F32)
    x2 = r1.astype(BF16)
    x3 = (r1 - x2.astype(F32)).astype(BF16)
    return x1, x2, x3


def _dot_sel(m, x):
    x1, x2, x3 = _split3(x)
    d = lambda p: jnp.dot(m, p, preferred_element_type=F32)
    return d(x1) + d(x2) + d(x3)


def _sel_dot(x, m):
    x1, x2, x3 = _split3(x)
    d = lambda p: jnp.dot(p, m, preferred_element_type=F32)
    return d(x1) + d(x2) + d(x3)


_TN = (1408, 1024, 768, 512, 384, 256, 128)


def _mm(a, b, mode, name, out_dtype=F32):
    if mode == "tn":
        t, m = a.shape
        n = b.shape[1]
        tt, tm, tn = _pick(t, (512, 256, 128)), _pick(m, _TN), _pick(n, _TN)

        def body(a_ref, b_ref, o_ref):
            @pl.when(pl.program_id(2) == 0)
            def _():
                o_ref[...] = jnp.zeros_like(o_ref)

            o_ref[...] += _dot_tn(a_ref[...], b_ref[...])

        return pl.pallas_call(
            body, name=name, grid=(m // tm, n // tn, t // tt),
            in_specs=[pl.BlockSpec((tt, tm), lambda i, j, k: (k, i)),
                      pl.BlockSpec((tt, tn), lambda i, j, k: (k, j))],
            out_specs=pl.BlockSpec((tm, tn), lambda i, j, k: (i, j)),
            out_shape=jax.ShapeDtypeStruct((m, n), F32),
            compiler_params=_params(("parallel", "parallel", "arbitrary")),
        )(a, b)

    m, k = a.shape
    tm = _pick(m, ROW_TILES)
    if mode == "nn":
        n = b.shape[1]
        tn = _pick(n, _TN)
        b_spec = pl.BlockSpec((k, tn), lambda i, j: (0, j))
        dot = _dot
    else:
        n = b.shape[0]
        tn = _pick(n, _TN if k <= 4096 else (512, 256, 128))
        b_spec = pl.BlockSpec((tn, k), lambda i, j: (j, 0))
        dot = _dot_nt

    def body(a_ref, b_ref, o_ref):
        o_ref[...] = dot(a_ref[...], b_ref[...]).astype(o_ref.dtype)

    return pl.pallas_call(
        body, name=name, grid=(m // tm, n // tn),
        in_specs=[pl.BlockSpec((tm, k), lambda i, j: (i, 0)), b_spec],
        out_specs=pl.BlockSpec((tm, tn), lambda i, j: (i, j)),
        out_shape=jax.ShapeDtypeStruct((m, n), out_dtype),
        compiler_params=_params(("parallel", "parallel")),
    )(a, b)


_DOTS = {"nn": _dot, "nt": _dot_nt, "tn": _dot_tn}
_BS = pl.BlockSpec


def _mmcall(name, kind, a, b, a_spec, b_spec, o_spec, o_shape, grid, red_axis=None, out_dtype=F32):
    dot = _DOTS[kind]

    def body(a_ref, b_ref, o_ref):
        if red_axis is None:
            o_ref[...] = dot(a_ref[...], b_ref[...]).astype(o_ref.dtype)
        else:
            @pl.when(pl.program_id(red_axis) == 0)
            def _():
                o_ref[...] = jnp.zeros_like(o_ref)

            o_ref[...] += dot(a_ref[...], b_ref[...])

    sem = tuple("arbitrary" if ax == red_axis else "parallel" for ax in range(len(grid)))
    return pl.pallas_call(
        body, name=name, grid=grid, in_specs=[a_spec, b_spec], out_specs=o_spec,
        out_shape=jax.ShapeDtypeStruct(o_shape, out_dtype), compiler_params=_params(sem),
    )(a, b)


class _AsRef:
    def __init__(self, value):
        self.value = value

    def __getitem__(self, idx):
        return self.value


def _mm_stack_red(a, w, name, kind, epilogue=None):
    s, t, n = a.shape
    nout = w.shape[2] if kind == "nn" else w.shape[1]
    if epilogue is None:
        tm = _pick(t, ROW_TILES)
        return _mmcall(name, kind, a, w, _BS((None, tm, n), lambda i, j: (j, i, 0)),
                       _BS((None,) + w.shape[1:], lambda i, j: (j, 0, 0)), _BS((tm, nout), lambda i, j: (i, 0)),
                       (t, nout), (t // tm, s), red_axis=1)
    fn, first, rows, vecs, outs, accs = epilogue
    rows = [first] + list(rows)
    tm = _pick(t, ROW_TILES)
    n_row, n_vec, n_out = len(rows), len(vecs), len(outs)
    dot = _DOTS[kind]

    def body(*refs):
        a_ref, w_ref, prod = refs[0], refs[1], refs[-1]
        row_refs = refs[2:2 + n_row]
        vec_refs = refs[2 + n_row:2 + n_row + n_vec]
        out_refs = refs[2 + n_row + n_vec:2 + n_row + n_vec + n_out]
        acc_refs = refs[2 + n_row + n_vec + n_out:-1]
        i, j = pl.program_id(0), pl.program_id(1)

        @pl.when(j == 0)
        def _():
            prod[...] = jnp.zeros_like(prod)

        prod[...] += dot(a_ref[...], w_ref[...])

        @pl.when(j == s - 1)
        def _():
            res = fn(row_refs[0], _AsRef(prod[...]), *row_refs[1:], *vec_refs, *out_refs)
            if acc_refs:
                @pl.when(i == 0)
                def _():
                    for r in acc_refs:
                        r[...] = jnp.zeros_like(r)

                for r, val in zip(acc_refs, res):
                    r[...] += val

    in_specs = [_BS((None, tm, n), lambda i, j: (j, i, 0)), _BS((None,) + w.shape[1:], lambda i, j: (j, 0, 0))]
    in_specs += [_BS((tm, r.shape[1]), lambda i, j: (i, 0)) for r in rows]
    in_specs += [_BS(v.shape, lambda i, j: (0, 0)) for v in vecs]
    out_specs = [_BS((tm, wd), lambda i, j: (i, 0)) for wd, _ in outs] + [_BS(sh, lambda i, j: (0, 0)) for sh in accs]
    out_shape = [jax.ShapeDtypeStruct((t, wd), dt) for wd, dt in outs] + [jax.ShapeDtypeStruct(sh, F32) for sh in accs]
    return pl.pallas_call(
        body, name=name, grid=(t // tm, s), in_specs=in_specs, out_specs=out_specs, out_shape=out_shape,
        scratch_shapes=[pltpu.VMEM((tm, nout), F32)],
        compiler_params=_params(("arbitrary" if accs else "parallel", "arbitrary")),
    )(a, w, *rows, *vecs)


def _mm_stack_tn(a, b, name):
    grp = 4
    if a.ndim == 2:
        t, k = a.shape
        s, _, n = b.shape
        tt = _pick(t, (512, 256, 128))
        a_spec = _BS((tt, k), lambda j, r: (r, 0))
        b_spec, b_in = _BS((None, grp, tt, n), lambda j, r: (j, 0, r, 0)), b.reshape(s // grp, grp, t, n)
        a_in = a
    else:
        s, t, k = a.shape
        n = b.shape[1]
        tt = _pick(t, (512, 256, 128))
        a_spec, a_in = _BS((None, grp, tt, k), lambda j, r: (j, 0, r, 0)), a.reshape(s // grp, grp, t, k)
        b_spec, b_in = _BS((tt, n), lambda j, r: (r, 0)), b

    def body(a_ref, b_ref, o_ref):
        @pl.when(pl.program_id(1) == 0)
        def _():
            o_ref[...] = jnp.zeros_like(o_ref)

        shared = a_ref[...] if a.ndim == 2 else b_ref[...]
        for e in range(grp):
            o_ref[e] += _dot_tn(shared, b_ref[e]) if a.ndim == 2 else _dot_tn(a_ref[e], shared)

    return pl.pallas_call(
        body, name=name, grid=(s // grp, t // tt), in_specs=[a_spec, b_spec],
        out_specs=_BS((None, grp, k, n), lambda j, r: (j, 0, 0, 0)),
        out_shape=jax.ShapeDtypeStruct((s // grp, grp, k, n), F32),
        compiler_params=_params(("parallel", "arbitrary")),
    )(a_in, b_in).reshape(s, k, n)


def _cols_fwd(x, w, name):
    t, k = x.shape
    s, _, n = w.shape
    tm = _pick(t, ROW_TILES)

    def body(x_ref, w_ref, o_ref):
        x_ = x_ref[...]
        for j in range(s):
            o_ref[:, n * j:n * (j + 1)] = _dot(x_, w_ref[j])

    return pl.pallas_call(
        body, name=name, grid=(t // tm,),
        in_specs=[_BS((tm, k), lambda i: (i, 0)), _BS((s, k, n), lambda i: (0, 0, 0))],
        out_specs=_BS((tm, s * n), lambda i: (i, 0)), out_shape=jax.ShapeDtypeStruct((t, s * n), F32),
        compiler_params=_params(("parallel",)),
    )(x, w)


def _cols_dx(d, w, name):
    t = d.shape[0]
    s, k, n = w.shape
    tm = _pick(t, ROW_TILES)

    def body(d_ref, w_ref, o_ref):
        acc = _dot_nt(d_ref[:, 0:n], w_ref[0])
        for j in range(1, s):
            acc = acc + _dot_nt(d_ref[:, n * j:n * (j + 1)], w_ref[j])
        o_ref[...] = acc

    return pl.pallas_call(
        body, name=name, grid=(t // tm,),
        in_specs=[_BS((tm, s * n), lambda i: (i, 0)), _BS((s, k, n), lambda i: (0, 0, 0))],
        out_specs=_BS((tm, k), lambda i: (i, 0)), out_shape=jax.ShapeDtypeStruct((t, k), F32),
        compiler_params=_params(("parallel",)),
    )(d, w)


def _cols_dw(x, d, n, name):
    t, k = x.shape
    s = d.shape[1] // n
    tt = _pick(t, (512, 256, 128))

    def body(x_ref, d_ref, o_ref):
        @pl.when(pl.program_id(0) == 0)
        def _():
            o_ref[...] = jnp.zeros_like(o_ref)

        x_ = x_ref[...]
        for j in range(s):
            o_ref[j] += _dot_tn(x_, d_ref[:, n * j:n * (j + 1)])

    return pl.pallas_call(
        body, name=name, grid=(t // tt,),
        in_specs=[_BS((tt, k), lambda r: (r, 0)), _BS((tt, s * n), lambda r: (r, 0))],
        out_specs=_BS((s, k, n), lambda r: (0, 0, 0)), out_shape=jax.ShapeDtypeStruct((s, k, n), F32),
        compiler_params=_params(("arbitrary",)),
    )(x, d)


def _rows(fn, name, t, tm, ins, vecs, outs, accs=()):
    n_in, n_out, n_acc = len(ins) + len(vecs), len(outs), len(accs)

    def body(*refs):
        res = fn(*refs[:n_in + n_out])
        if n_acc:
            acc_refs = refs[n_in + n_out:]

            @pl.when(pl.program_id(0) == 0)
            def _():
                for r in acc_refs:
                    r[...] = jnp.zeros_like(r)

            for r, val in zip(acc_refs, res):
                r[...] += val

    in_specs = [pl.BlockSpec((tm, bw), functools.partial(lambda i, cb: (i, cb), cb=cb)) for _, bw, cb in ins]
    in_specs += [pl.BlockSpec(v.shape, lambda i: (0, 0)) for v in vecs]
    out_specs = [pl.BlockSpec((tm, w), lambda i: (i, 0)) for w, _ in outs]
    out_specs += [pl.BlockSpec(s, lambda i: (0, 0)) for s in accs]
    out_shape = [jax.ShapeDtypeStruct((t, w), dt) for w, dt in outs]
    out_shape += [jax.ShapeDtypeStruct(s, F32) for s in accs]
    return pl.pallas_call(
        body, name=name, grid=(t // tm,), in_specs=in_specs, out_specs=out_specs, out_shape=out_shape,
        compiler_params=_params(("arbitrary",) if n_acc else ("parallel",)),
    )(*[a for a, _, _ in ins], *vecs)


def _rms(x, g):
    return x * lax.rsqrt(jnp.mean(x * x, axis=-1, keepdims=True) + EPS) * g


def _rms_bwd(x, g, dy):
    xh = x * lax.rsqrt(jnp.mean(x * x, axis=-1, keepdims=True) + EPS)
    r = lax.rsqrt(jnp.mean(x * x, axis=-1, keepdims=True) + EPS)
    dyg = dy * g
    dx = r * (dyg - xh * jnp.mean(dyg * xh, axis=-1, keepdims=True))
    return dx, jnp.sum(dy * xh, axis=0, keepdims=True)


def _hgrn_mats():
    row = lax.broadcasted_iota(jnp.int32, (HB, HB), 0)
    col = lax.broadcasted_iota(jnp.int32, (HB, HB), 1)
    return row, col


def _hgrn_gates(qr, z, t0, t1):
    lb = 1.0 / (1.0 + jnp.exp(t1 - t0))
    sz = _sig(z)
    sneg = 1.0 / (1.0 + jnp.exp(z))
    f = lb + (1.0 - lb) * sz
    return lb, sz, sneg, f, jnp.log(f), (1.0 - lb) * sneg, qr * _sig(qr)


def _hgrn_scores(q, k, cum):
    mids = [cum[SUB * i + SUB // 2 - 1:SUB * i + SUB // 2, :] for i in range(HB // SUB)]
    cmid = jnp.concatenate([cum[SUB * i:SUB * (i + 1)] - mids[i] for i in range(HB // SUB)], axis=0)
    qd = q * jnp.exp(jnp.minimum(cmid, EXP_CLAMP))
    qd_b = qd.astype(BF16)
    kds, parts = [], []
    for i in range(HB // SUB):
        kd = k * jnp.exp(jnp.minimum(mids[i] - cum, EXP_CLAMP))
        kds.append(kd)
        parts.append(_dot_nt(qd_b[SUB * i:SUB * (i + 1)], kd))
    return qd, kds, jnp.concatenate(parts, axis=0), cmid


def _hgrn_fwd(p_hg, table, gain, t):
    nblk = t // HB

    def body(q_ref, f_ref, i_ref, g_ref, tab_ref, gain_ref, o_ref, y_ref, st_ref, state):
        @pl.when(pl.program_id(0) == 0)
        def _():
            state[...] = jnp.zeros_like(state)

        row, col = _hgrn_mats()
        causal = col <= row
        tri = causal.astype(BF16)
        for h in range(NH):
            sl = slice(HD * h, HD * (h + 1))
            v, gr = i_ref[:, sl], g_ref[:, sl]
            _, _, _, _, lf, k, q = _hgrn_gates(q_ref[:, sl], f_ref[:, sl], tab_ref[0:1, sl], tab_ref[1:2, sl])
            cum = _dot_sel(tri, lf)
            _, _, s, _ = _hgrn_scores(q, k, cum)
            p = jnp.where(causal, s, 0.0)
            st = state[h]
            st_ref[h, 0] = st
            o = _dot(p, v) + _dot_nt(q * jnp.exp(cum), st)
            last = cum[HB - 1:HB, :]
            state[h] = st * jnp.exp(last) + _dot_tn(v, k * jnp.exp(last - cum))
            o_ref[:, sl] = o
            y_ref[:, sl] = (_rms(o, gain_ref[...]) * gr * _sig(gr)).astype(BF16)

    blk = lambda cb: pl.BlockSpec((HB, HGW), functools.partial(lambda n, cb: (n, cb), cb=cb))
    return pl.pallas_call(
        body, name="hgrn_fwd", grid=(nblk,),
        in_specs=[blk(0), blk(1), blk(2), blk(3), pl.BlockSpec((2, HGW), lambda n: (0, 0)),
                  pl.BlockSpec((1, HD), lambda n: (0, 0))],
        out_specs=[pl.BlockSpec((HB, HGW), lambda n: (n, 0)), pl.BlockSpec((HB, HGW), lambda n: (n, 0)),
                   pl.BlockSpec((NH, 1, HD, HD), lambda n: (0, n, 0, 0))],
        out_shape=[jax.ShapeDtypeStruct((t, HGW), F32), jax.ShapeDtypeStruct((t, HGW), BF16),
                   jax.ShapeDtypeStruct((NH, nblk, HD, HD), F32)],
        scratch_shapes=[pltpu.VMEM((NH, HD, HD), F32)],
        compiler_params=_params(("arbitrary",)),
    )(p_hg, p_hg, p_hg, p_hg, table, gain)


def _hgrn_bwd(p_hg, table, gain, o_pre, states, dy, t):
    nblk = t // HB

    def body(q_ref, f_ref, i_ref, g_ref, tab_ref, gain_ref, o_ref, st_ref, dy_ref, dp_ref, dtab_ref, dgain_ref,
             dstate):
        @pl.when(pl.program_id(0) == 0)
        def _():
            dstate[...] = jnp.zeros_like(dstate)
            dtab_ref[...] = jnp.zeros_like(dtab_ref)
            dgain_ref[...] = jnp.zeros_like(dgain_ref)

        row, col = _hgrn_mats()
        causal = col <= row
        tri = causal.astype(BF16)
        tri_t = (row <= col).astype(BF16)
        dgain = jnp.zeros((1, HD), F32)
        for h in range(NH):
            sl = slice(HD * h, HD * (h + 1))
            qr, z, v, gr = q_ref[:, sl], f_ref[:, sl], i_ref[:, sl], g_ref[:, sl]
            lb, sz, sneg, f, lf, k, q = _hgrn_gates(qr, z, tab_ref[0:1, sl], tab_ref[1:2, sl])
            cum = _dot_sel(tri, lf)
            qd, kds, s, cmid = _hgrn_scores(q, k, cum)
            p = jnp.where(causal, s, 0.0)
            st = st_ref[h, 0]
            dst = dstate[h]
            o = o_ref[:, sl]
            sg = _sig(gr)
            dyh = dy_ref[:, sl]
            on = _rms(o, gain_ref[...])
            dgr = dyh * on * sg * (1.0 + gr * (1.0 - sg))
            do, dg_h = _rms_bwd(o, gain_ref[...], dyh * gr * sg)
            dgain = dgain + dg_h
            do_b = do.astype(BF16)
            ecum = jnp.exp(cum)
            qc = q * ecum
            last = cum[HB - 1:HB, :]
            edec = jnp.exp(last - cum)
            kdec = k * edec
            dp = jnp.where(causal, _dot_nt(do_b, v), 0.0)
            dv = _dot(p.T, do_b) + _dot_nt(kdec, dst)
            dqc = _dot(do_b, st)
            dkdec = _dot(v, dst)
            dstate[h] = dst * jnp.exp(last) + _dot(do.T, qc)
            dp_b = dp.astype(BF16)
            dqd = jnp.concatenate([_dot(dp_b[SUB * i:SUB * (i + 1)], kds[i]) for i in range(HB // SUB)], axis=0)
            gq = dqd * qd
            dq = dqd * jnp.exp(jnp.minimum(cmid, EXP_CLAMP)) + dqc * ecum
            gs = dkdec * kdec
            dk = dkdec * edec
            dcum = dqc * qc - gs + gq
            dcum = dcum + jnp.where(row == HB - 1, jnp.sum(gs, axis=0, keepdims=True)
                                    + jnp.exp(last) * jnp.sum(st * dst, axis=0, keepdims=True), 0.0)
            qd_b = qd.astype(BF16)
            for i in range(HB // SUB):
                dkd = _dot_tn(dp_b[SUB * i:SUB * (i + 1)], qd_b[SUB * i:SUB * (i + 1)])
                mid = cum[SUB * i + SUB // 2 - 1:SUB * i + SUB // 2, :]
                dk = dk + dkd * jnp.exp(jnp.minimum(mid - cum, EXP_CLAMP))
                gk = dkd * kds[i]
                to_mid = jnp.sum(gk, axis=0, keepdims=True) - jnp.sum(gq[SUB * i:SUB * (i + 1)], axis=0, keepdims=True)
                dcum = dcum - gk + jnp.where(row == SUB * i + SUB // 2 - 1, to_mid, 0.0)
            dlf = _dot_sel(tri_t, dcum)
            df = dlf / f - dk
            dz = df * (1.0 - lb) * sz * sneg
            dlb = jnp.sum(df * sneg, axis=0, keepdims=True) * lb * (1.0 - lb)
            dtab_ref[0:1, sl] += dlb
            dtab_ref[1:2, sl] -= dlb
            sq = _sig(qr)
            dp_ref[:, sl] = (dq * sq * (1.0 + qr * (1.0 - sq))).astype(BF16)
            dp_ref[:, HGW + HD * h:HGW + HD * (h + 1)] = dz.astype(BF16)
            dp_ref[:, 2 * HGW + HD * h:2 * HGW + HD * (h + 1)] = dv.astype(BF16)
            dp_ref[:, 3 * HGW + HD * h:3 * HGW + HD * (h + 1)] = dgr.astype(BF16)
        dgain_ref[...] += dgain

    rev = lambda cb: pl.BlockSpec((HB, HGW), functools.partial(lambda n, cb: (nblk - 1 - n, cb), cb=cb))
    return pl.pallas_call(
        body, name="hgrn_bwd", grid=(nblk,),
        in_specs=[rev(0), rev(1), rev(2), rev(3), pl.BlockSpec((2, HGW), lambda n: (0, 0)),
                  pl.BlockSpec((1, HD), lambda n: (0, 0)), rev(0),
                  pl.BlockSpec((NH, 1, HD, HD), lambda n: (0, nblk - 1 - n, 0, 0)), rev(0)],
        out_specs=[pl.BlockSpec((HB, 4 * HGW), lambda n: (nblk - 1 - n, 0)),
                   pl.BlockSpec((2, HGW), lambda n: (0, 0)), pl.BlockSpec((1, HD), lambda n: (0, 0))],
        out_shape=[jax.ShapeDtypeStruct((t, 4 * HGW), BF16), jax.ShapeDtypeStruct((2, HGW), F32),
                   jax.ShapeDtypeStruct((1, HD), F32)],
        scratch_shapes=[pltpu.VMEM((NH, HD, HD), F32)],
        compiler_params=_params(("arbitrary",)),
    )(p_hg, p_hg, p_hg, p_hg, table, gain, o_pre, states, dy)


def _rope_mat():
    r = lax.broadcasted_iota(jnp.int32, (ROPE, ROPE), 0)
    c = lax.broadcasted_iota(jnp.int32, (ROPE, ROPE), 1)
    half = ROPE // 2
    return ((r == c - half).astype(F32) - (r == c + half).astype(F32)).astype(BF16)


def _mla_prep_fwd(p_mla, cs, sn, wq, wkv, gql, gkvl, gq, gk, t):
    tm = _pick(t, (512, 256, 128))

    def body(p_ref, cs_ref, sn_ref, wq_ref, wkv_ref, gql_ref, gkvl_ref, gq_ref, gk_ref,
             q_ref, k_ref, v_ref):
        rmat = _rope_mat()
        cqn = _rms(p_ref[:, 0:QL], gql_ref[...]).astype(BF16)
        ckvn = _rms(p_ref[:, QL:QL + KVL], gkvl_ref[...]).astype(BF16)
        kpe = p_ref[:, QL + KVL:QL + KVL + ROPE]
        c, s = cs_ref[...], sn_ref[...]
        rot = lambda x: x * c + _sel_dot(x, rmat) * s
        for h in range(NH):
            qa, qr = _dot(cqn, wq_ref[h, :, 0:HD]), _dot(cqn, wq_ref[h, :, HD:QK])
            rq = lax.rsqrt((jnp.sum(qa * qa, -1, keepdims=True) + jnp.sum(qr * qr, -1, keepdims=True)) / QK + EPS)
            q_ref[h, :, 0:HD] = (qa * rq * gq_ref[:, 0:HD] * SCALE).astype(BF16)
            q_ref[h, :, HD:QK] = (rot(qr * rq * gq_ref[:, HD:QK]) * SCALE).astype(BF16)
            kn = _dot(ckvn, wkv_ref[h, :, 0:HD])
            rk = lax.rsqrt((jnp.sum(kn * kn, -1, keepdims=True) + jnp.sum(kpe * kpe, -1, keepdims=True)) / QK + EPS)
            k_ref[h, :, 0:HD] = (kn * rk * gk_ref[:, 0:HD]).astype(BF16)
            k_ref[h, :, HD:QK] = rot(kpe * rk * gk_ref[:, HD:QK]).astype(BF16)
            v_ref[h] = _dot(ckvn, wkv_ref[h, :, HD:2 * HD]).astype(BF16)

    whole = lambda a: pl.BlockSpec(a.shape, functools.partial(lambda i, nd: (0,) * nd, nd=a.ndim))
    return pl.pallas_call(
        body, name="mla_prep_fwd", grid=(t // tm,),
        in_specs=[pl.BlockSpec((tm, QL + KVL + ROPE), lambda i: (i, 0)), pl.BlockSpec((tm, ROPE), lambda i: (i, 0)),
                  pl.BlockSpec((tm, ROPE), lambda i: (i, 0))] + [whole(a) for a in (wq, wkv, gql, gkvl, gq, gk)],
        out_specs=[pl.BlockSpec((NH, tm, QK), lambda i: (0, i, 0)), pl.BlockSpec((NH, tm, QK), lambda i: (0, i, 0)),
                   pl.BlockSpec((NH, tm, HD), lambda i: (0, i, 0))],
        out_shape=[jax.ShapeDtypeStruct((NH, t, QK), BF16), jax.ShapeDtypeStruct((NH, t, QK), BF16),
                   jax.ShapeDtypeStruct((NH, t, HD), BF16)],
        compiler_params=_params(("parallel",)),
    )(p_mla, cs, sn, wq, wkv, gql, gkvl, gq, gk)


def _mla_prep_bwd(p_mla, cs, sn, wq, wkv, gql, gkvl, gq, gk, dq, dk, dv, t):
    tm = _pick(t, (512, 256, 128))

    def body(p_ref, cs_ref, sn_ref, wq_ref, wkv_ref, gql_ref, gkvl_ref, gq_ref, gk_ref,
             dq_ref, dk_ref, dv_ref,
             dp_ref, dwq_ref, dwkv_ref, dgql_ref, dgkvl_ref, dgq_ref, dgk_ref):
        accs = (dwq_ref, dwkv_ref, dgql_ref, dgkvl_ref, dgq_ref, dgk_ref)

        @pl.when(pl.program_id(0) == 0)
        def _():
            for r in accs:
                r[...] = jnp.zeros_like(r)

        rmat = _rope_mat()
        rmat_t = -rmat
        cq, ckv = p_ref[:, 0:QL], p_ref[:, QL:QL + KVL]
        kpe = p_ref[:, QL + KVL:QL + KVL + ROPE]
        cqn_f, ckvn_f = _rms(cq, gql_ref[...]), _rms(ckv, gkvl_ref[...])
        cqn, ckvn = cqn_f.astype(BF16), ckvn_f.astype(BF16)
        cqn_t, ckvn_t = cqn_f.T.astype(BF16), ckvn_f.T.astype(BF16)
        c, s = cs_ref[...], sn_ref[...]
        unrot = lambda dy: dy * c + _sel_dot(dy * s, rmat_t)
        dcqn = jnp.zeros((tm, QL), F32)
        dckvn = jnp.zeros((tm, KVL), F32)
        dkpe = jnp.zeros((tm, ROPE), F32)
        dgq_a, dgq_r = jnp.zeros((1, HD), F32), jnp.zeros((1, ROPE), F32)
        dgk_a, dgk_r = jnp.zeros((1, HD), F32), jnp.zeros((1, ROPE), F32)
        for h in range(NH):
            qa, qr = _dot(cqn, wq_ref[h, :, 0:HD]), _dot(cqn, wq_ref[h, :, HD:QK])
            rq = lax.rsqrt((jnp.sum(qa * qa, -1, keepdims=True) + jnp.sum(qr * qr, -1, keepdims=True)) / QK + EPS)
            xa, xr = qa * rq, qr * rq
            dya = dq_ref[h, :, 0:HD] * SCALE
            dyr = unrot(dq_ref[h, :, HD:QK] * SCALE)
            dgq_a += jnp.sum(dya * xa, axis=0, keepdims=True)
            dgq_r += jnp.sum(dyr * xr, axis=0, keepdims=True)
            ga, gr_ = dya * gq_ref[:, 0:HD], dyr * gq_ref[:, HD:QK]
            mean = (jnp.sum(ga * xa, -1, keepdims=True) + jnp.sum(gr_ * xr, -1, keepdims=True)) / QK
            dqa = (rq * (ga - xa * mean)).astype(BF16)
            dqr = (rq * (gr_ - xr * mean)).astype(BF16)
            dwq_ref[h, :, 0:HD] += _dot(cqn_t, dqa)
            dwq_ref[h, :, HD:QK] += _dot(cqn_t, dqr)
            dcqn += _dot_nt(dqa, wq_ref[h, :, 0:HD]) + _dot_nt(dqr, wq_ref[h, :, HD:QK])
            kn = _dot(ckvn, wkv_ref[h, :, 0:HD])
            rk = lax.rsqrt((jnp.sum(kn * kn, -1, keepdims=True) + jnp.sum(kpe * kpe, -1, keepdims=True)) / QK + EPS)
            ya, yr = kn * rk, kpe * rk
            dka = dk_ref[h, :, 0:HD]
            dkr = unrot(dk_ref[h, :, HD:QK])
            dgk_a += jnp.sum(dka * ya, axis=0, keepdims=True)
            dgk_r += jnp.sum(dkr * yr, axis=0, keepdims=True)
            ha, hr = dka * gk_ref[:, 0:HD], dkr * gk_ref[:, HD:QK]
            mean = (jnp.sum(ha * ya, -1, keepdims=True) + jnp.sum(hr * yr, -1, keepdims=True)) / QK
            dkn = (rk * (ha - ya * mean)).astype(BF16)
            dkpe += rk * (hr - yr * mean)
            dvh = dv_ref[h].astype(BF16)
            dwkv_ref[h, :, 0:HD] += _dot(ckvn_t, dkn)
            dwkv_ref[h, :, HD:2 * HD] += _dot(ckvn_t, dvh)
            dckvn += _dot_nt(dkn, wkv_ref[h, :, 0:HD]) + _dot_nt(dvh, wkv_ref[h, :, HD:2 * HD])
        dcq, dg1 = _rms_bwd(cq, gql_ref[...], dcqn)
        dckv, dg2 = _rms_bwd(ckv, gkvl_ref[...], dckvn)
        dp_ref[:, 0:QL] = dcq.astype(BF16)
        dp_ref[:, QL:QL + KVL] = dckv.astype(BF16)
        dp_ref[:, QL + KVL:QL + KVL + ROPE] = dkpe.astype(BF16)
        dgql_ref[...] += dg1
        dgkvl_ref[...] += dg2
        dgq_ref[:, 0:HD] += dgq_a
        dgq_ref[:, HD:QK] += dgq_r
        dgk_ref[:, 0:HD] += dgk_a
        dgk_ref[:, HD:QK] += dgk_r

    whole = lambda a: pl.BlockSpec(a.shape, functools.partial(lambda i, nd: (0,) * nd, nd=a.ndim))
    acc_shapes = [wq.shape, wkv.shape, gql.shape, gkvl.shape, gq.shape, gk.shape]
    return pl.pallas_call(
        body, name="mla_prep_bwd", grid=(t // tm,),
        in_specs=[pl.BlockSpec((tm, QL + KVL + ROPE), lambda i: (i, 0)), pl.BlockSpec((tm, ROPE), lambda i: (i, 0)),
                  pl.BlockSpec((tm, ROPE), lambda i: (i, 0))]
        + [whole(a) for a in (wq, wkv, gql, gkvl, gq, gk)]
        + [pl.BlockSpec((NH, tm, QK), lambda i: (0, i, 0)), pl.BlockSpec((NH, tm, QK), lambda i: (0, i, 0)),
           pl.BlockSpec((NH, tm, HD), lambda i: (0, i, 0))],
        out_specs=[pl.BlockSpec((tm, QL + KVL + ROPE), lambda i: (i, 0))]
        + [pl.BlockSpec(s, functools.partial(lambda i, nd: (0,) * nd, nd=len(s))) for s in acc_shapes],
        out_shape=[jax.ShapeDtypeStruct((t, QL + KVL + ROPE), BF16)]
        + [jax.ShapeDtypeStruct(s, F32) for s in acc_shapes],
        compiler_params=_params(("arbitrary",)),
    )(p_mla, cs, sn, wq, wkv, gql, gkvl, gq, gk, dq, dk, dv)


def _chunk_mask(nq, nk, key0, keys_on_rows):
    shape = (nk, nq) if keys_on_rows else (nq, nk)
    qi = lax.broadcasted_iota(jnp.int32, shape, 1 if keys_on_rows else 0) // CHUNK
    ki = lax.broadcasted_iota(jnp.int32, shape, 0 if keys_on_rows else 1) // CHUNK + key0 // CHUNK
    return ki <= qi


def _flash_fwd(q, k, v, t):
    tq = _pick(t, ATT_Q)
    tk = tq // ATT_KEY_TILES

    def body(q_ref, k_ref, v_ref, o_ref, lse_ref):
        i = pl.program_id(1)
        qt = q_ref[0]

        def step(j, carry, key0):
            m, l, acc = carry
            cols = pl.ds(pl.multiple_of(j * tk, tk), tk)
            s = _dot_nt(qt, k_ref[0, cols, :])
            if key0 is not None:
                s = jnp.where(_chunk_mask(tq, tk, key0, False), s, -jnp.inf)
            m_new = jnp.maximum(m, jnp.max(s, axis=-1, keepdims=True))
            p = jnp.exp(s - m_new)
            alpha = jnp.exp(m - m_new)
            return m_new, alpha * l + jnp.sum(p, axis=-1, keepdims=True), alpha * acc + _dot(p, v_ref[0, cols, :])

        init = (jnp.full((tq, 1), -jnp.inf, F32), jnp.zeros((tq, 1), F32), jnp.zeros((tq, HD), F32))
        carry = lax.fori_loop(0, ATT_KEY_TILES * i, lambda j, cr: step(j, cr, None), init)
        for h in range(ATT_KEY_TILES):
            carry = step(ATT_KEY_TILES * i + h, carry, h * tk)
        m, l, acc = carry
        o_ref[...] = (acc / l).astype(BF16)
        lse_ref[0] = m + jnp.log(l)

    return pl.pallas_call(
        body, name="flash_fwd", grid=(NH, t // tq),
        in_specs=[pl.BlockSpec((1, tq, QK), lambda h, i: (h, i, 0)), pl.BlockSpec((1, t, QK), lambda h, i: (h, 0, 0)),
                  pl.BlockSpec((1, t, HD), lambda h, i: (h, 0, 0))],
        out_specs=[pl.BlockSpec((tq, HD), lambda h, i: (i, h)), pl.BlockSpec((1, tq, 1), lambda h, i: (h, i, 0))],
        out_shape=[jax.ShapeDtypeStruct((t, NH * HD), BF16), jax.ShapeDtypeStruct((NH, t, 1), F32)],
        compiler_params=_params(("parallel", "parallel")),
    )(q, k, v)


def _attn_out_bwd(dy, w, o, t):
    tm = _pick(t, ATT_TILES)

    def body(dy_ref, w_ref, o_ref, do_ref, delta_ref):
        do = _dot_nt(dy_ref[...], w_ref[...]).astype(BF16)
        do_ref[...] = do
        ones = jnp.ones((SUBLANE, HD), BF16)
        rowsum = lambda p: lax.dot_general(ones, p, (((1,), (1,)), ((), ())), preferred_element_type=F32)
        for h in range(NH):
            sl = slice(HD * h, HD * (h + 1))
            x1, x2, x3 = _split3(do[:, sl].astype(F32) * o_ref[:, sl].astype(F32))
            delta_ref[h] = (rowsum(x1) + rowsum(x2) + rowsum(x3))[0:1, :]

    return pl.pallas_call(
        body, name="attn_out_bwd", grid=(t // tm,),
        in_specs=[_BS((tm, dy.shape[1]), lambda i: (i, 0)), _BS(w.shape, lambda i: (0, 0)),
                  _BS((tm, NH * HD), lambda i: (i, 0))],
        out_specs=[_BS((tm, NH * HD), lambda i: (i, 0)), _BS((NH, 1, tm), lambda i: (0, 0, i))],
        out_shape=[jax.ShapeDtypeStruct((t, NH * HD), BF16), jax.ShapeDtypeStruct((NH, 1, t), F32)],
        compiler_params=_params(("parallel",)),
    )(dy, w, o)


def _flash_bwd(q, k, v, do, lse_row, delta_row, t):
    tq = _pick(t, ATT_Q)
    tk = tq // ATT_KEY_TILES

    def body(q_ref, k_ref, v_ref, do_ref, lse_ref, delta_ref, dq_ref, dk_ref, dv_ref):
        j = pl.program_id(1)

        @pl.when(j == 0)
        def _():
            dq_ref[...] = jnp.zeros_like(dq_ref)

        kt, vt = k_ref[0], v_ref[0]

        def step(i, carry, key0):
            dk, dv = carry
            rows = pl.ds(pl.multiple_of(i * tq, tq), tq)
            qt, dot_ = q_ref[0, rows, :], do_ref[rows, :]
            p = jnp.exp(_dot_nt(kt, qt) - lse_ref[0, :, rows])
            if key0 is not None:
                p = jnp.where(_chunk_mask(tq, tk, key0, True), p, 0.0)
            ds = (p * (_dot_nt(vt, dot_) - delta_ref[0, :, rows])).astype(BF16)
            dq_ref[0, rows, :] += _dot_tn(ds, kt)
            return dk + _dot(ds, qt), dv + _dot(p, dot_)

        own = j // ATT_KEY_TILES
        carry = step(own, (jnp.zeros((tk, QK), F32), jnp.zeros((tk, HD), F32)), (j % ATT_KEY_TILES) * tk)
        dk, dv = lax.fori_loop(own + 1, t // tq, lambda i, cr: step(i, cr, None), carry)
        dk_ref[0] = dk
        dv_ref[0] = dv

    return pl.pallas_call(
        body, name="flash_bwd", grid=(NH, t // tk),
        in_specs=[pl.BlockSpec((1, t, QK), lambda h, j: (h, 0, 0)), pl.BlockSpec((1, tk, QK), lambda h, j: (h, j, 0)),
                  pl.BlockSpec((1, tk, HD), lambda h, j: (h, j, 0)), pl.BlockSpec((t, HD), lambda h, j: (0, h)),
                  pl.BlockSpec((1, 1, t), lambda h, j: (h, 0, 0)), pl.BlockSpec((1, 1, t), lambda h, j: (h, 0, 0))],
        out_specs=[pl.BlockSpec((1, t, QK), lambda h, j: (h, 0, 0)), pl.BlockSpec((1, tk, QK), lambda h, j: (h, j, 0)),
                   pl.BlockSpec((1, tk, HD), lambda h, j: (h, j, 0))],
        out_shape=[jax.ShapeDtypeStruct((NH, t, QK), F32), jax.ShapeDtypeStruct((NH, t, QK), F32),
                   jax.ShapeDtypeStruct((NH, t, HD), F32)],
        compiler_params=_params(("parallel", "arbitrary")),
    )(q, k, v, do, lse_row, delta_row)


def _ffn_in(xn, w_in, name):
    t, k = xn.shape
    s, _, n = w_in.shape
    tm = _pick(t, ROW_TILES)

    def body(x_ref, w_ref, gu_ref, a_ref):
        x = x_ref[...]
        g, u = _dot(x, w_ref[0]), _dot(x, w_ref[1])
        gu_ref[0] = g.astype(BF16)
        gu_ref[1] = u.astype(BF16)
        a_ref[...] = (g * _sig(g) * u).astype(BF16)

    return pl.pallas_call(
        body, name=name, grid=(t // tm, s // 2),
        in_specs=[_BS((tm, k), lambda i, j: (i, 0)), _BS((2, None, k, n), lambda i, j: (0, j, 0, 0))],
        out_specs=[_BS((2, None, tm, n), lambda i, j: (0, j, i, 0)), _BS((None, tm, n), lambda i, j: (j, i, 0))],
        out_shape=[jax.ShapeDtypeStruct((2, s // 2, t, n), BF16), jax.ShapeDtypeStruct((s // 2, t, n), BF16)],
        compiler_params=_params(("parallel", "parallel")),
    )(xn, w_in.reshape(2, s // 2, k, n))


def _ffn_dgu(dfo, w_out, gu, name):
    t, k = dfo.shape
    s, n, _ = w_out.shape
    tm = _pick(t, ROW_TILES)

    def body(d_ref, w_ref, gu_ref, o_ref):
        da = _dot_nt(d_ref[...], w_ref[...])
        g, u = gu_ref[0].astype(F32), gu_ref[1].astype(F32)
        sg = _sig(g)
        o_ref[0] = (da * u * sg * (1.0 + g * (1.0 - sg))).astype(BF16)
        o_ref[1] = (da * g * sg).astype(BF16)

    pair = _BS((2, None, tm, n), lambda i, j: (0, j, i, 0))
    return pl.pallas_call(
        body, name=name, grid=(t // tm, s),
        in_specs=[_BS((tm, k), lambda i, j: (i, 0)), _BS((None, n, k), lambda i, j: (j, 0, 0)), pair],
        out_specs=pair, out_shape=jax.ShapeDtypeStruct((2, s, t, n), BF16),
        compiler_params=_params(("parallel", "parallel")),
    )(dfo, w_out, gu).reshape(2 * s, t, n)


def _ffn_fwd(xn, w_in, w_out, tag, epilogue):
    gu, a = _ffn_in(xn, w_in, tag + "_in")
    return gu, a, _mm_stack_red(a, w_out, tag + "_out", "nn", epilogue)


def _ffn_bwd(dfo, xn, gu, a, w_in, w_out, tag, epilogue):
    dw_out = _mm_stack_tn(a, dfo, tag + "_dwout")
    dgu = _ffn_dgu(dfo, w_out, gu, tag + "_dgu")
    return _mm_stack_red(dgu, w_in, tag + "_dxn", "nt", epilogue), _mm_stack_tn(xn, dgu, tag + "_dwin"), dw_out


def _local_step(x, target, cs, sn, w, late_weights, s, early_grads):
    t = x.shape[0]
    tm = _pick(t, (256, 128))
    g = {}
    ffn_out = lambda n: w[n].reshape(4, FF // 4, D)

    def norm_fn(x_ref, g_ref, o_ref):
        o_ref[...] = _rms(x_ref[...], g_ref[...]).astype(BF16)

    xn1 = _rows(norm_fn, "norm1", t, tm, [(x, D, 0)], [s["ffn1_norm"]], [(D, BF16)])[0]
    def res_norm_fn(scale):
        def fn(h_ref, f_ref, g_ref, h_out, n_out):
            h = h_ref[...] + scale * f_ref[...]
            h_out[...] = h
            n_out[...] = _rms(h, g_ref[...]).astype(BF16)
        return fn

    gu1, a1, (h1, u) = _ffn_fwd(xn1, w["ffn1_w_in"], ffn_out("ffn1_w_out"), "ffn1",
                                (res_norm_fn(0.5), x, [], [s["mix_norm"]], [(D, F32), (D, BF16)], []))
    w = {**w, **late_weights(h1)}
    rows_of = lambda n: w[n].reshape(-1, w[n].shape[2])
    w_hgb, w_mlab, w_o = rows_of("w_hg_branch"), rows_of("w_mla_branch"), rows_of("w_out")
    w_in_nat = w["w_in"].transpose(1, 0, 2).reshape(D, -1)
    w_mrg = w["w_merge"]
    mw = w_mrg.shape[2]
    w_in_hg, w_in_mla = w_in_nat[:, :4 * HGW], w_in_nat[:, 4 * HGW:]
    p_hg = _mm(u, w_in_hg, "nn", "proj_hg")
    p_mla = _mm(u, w_in_mla, "nn", "proj_mla")
    gpre = _cols_fwd(u, w_mrg, "proj_gate")
    o_pre, hgy, states = _hgrn_fwd(p_hg, s["hg_lb_table"], s["hg_out_norm"], t)
    prep_args = (p_mla, cs, sn, w["w_q_up"], w["w_kv_up"], s["mla_q_lora_norm"], s["mla_kv_lora_norm"],
                 s["q_head_norm"], s["k_head_norm"])
    q, k, v = _mla_prep_fwd(*prep_args, t)
    att, lse = _flash_fwd(q, k, v, t)
    y_hg = _mm(hgy, w_hgb, "nn", "branch_hg")
    y_mla = _mm(att, w_mlab, "nn", "branch_mla")

    def mix_fn(gh_ref, gm_ref, yh_ref, ym_ref, b_ref, o_ref):
        gh = _sig(gh_ref[...] + b_ref[:, 0:D])
        gm = _sig(gm_ref[...] + b_ref[:, D:2 * D])
        o_ref[...] = (gh * yh_ref[...] + gm * ym_ref[...]).astype(BF16)

    mixed = _rows(mix_fn, "mix", t, tm, [(gpre, D, 0), (gpre, D, 1), (y_hg, D, 0), (y_mla, D, 0)], [s["b_merge"]],
                  [(D, BF16)])[0]
    h2, xn2 = _mm_stack_red(mixed[None], w_o[None], "mix_out", "nn",
                            (res_norm_fn(1.0), h1, [], [s["ffn2_norm"]], [(D, F32), (D, BF16)], []))

    def loss_fn(h_ref, f_ref, tg_ref, g_ref, dh_out, dhb_out):
        h = h_ref[...] + 0.5 * f_ref[...]
        e = _rms(h, g_ref[...]) - tg_ref[...]
        dh, dgain = _rms_bwd(h, g_ref[...], e / D)
        dh_out[...] = dh
        dhb_out[...] = (0.5 * dh).astype(BF16)
        return dgain, jnp.full((1, LANE), 0.5 / D * jnp.sum(e * e), F32)

    gu2, a2, (dh3, dfo2, g["final_norm"], loss) = _ffn_fwd(
        xn2, w["ffn2_w_in"], ffn_out("ffn2_w_out"), "ffn2",
        (loss_fn, h2, [target], [s["final_norm"]], [(D, F32), (D, BF16)], [(1, D), (1, LANE)]))

    def norm_bwd_fn(scale):
        def fn(h_ref, dxn_ref, dh_ref, g_ref, dh_out, dhb_out):
            dx, dgain = _rms_bwd(h_ref[...], g_ref[...], dxn_ref[...])
            dh = dh_ref[...] + dx
            dh_out[...] = dh
            dhb_out[...] = (scale * dh).astype(BF16)
            return (dgain,)
        return fn

    as_rows = lambda a: a.reshape((N_DEV, -1) + a.shape[-1:])
    (dh2, dh2b, g["ffn2_norm"]), g["ffn2_w_in"], dwo = _ffn_bwd(
        dfo2, xn2, gu2, a2, w["ffn2_w_in"], ffn_out("ffn2_w_out"), "ffn2",
        (norm_bwd_fn(1.0), h2, [dh3], [s["ffn2_norm"]], [(D, F32), (D, BF16)], [(1, D)]))
    g["ffn2_w_out"] = as_rows(dwo)
    dmixed = _mm(dh2b, w_o, "nt", "mix_out_dx")
    g["w_out"] = as_rows(_mm(mixed, dh2b, "tn", "mix_out_dw"))

    def mix_bwd_fn(gh_ref, gm_ref, yh_ref, ym_ref, dm_ref, b_ref, dyh_out, dym_out, dg_out):
        gh = _sig(gh_ref[...] + b_ref[:, 0:D])
        gm = _sig(gm_ref[...] + b_ref[:, D:2 * D])
        dm = dm_ref[...]
        dyh_out[...] = (dm * gh).astype(BF16)
        dym_out[...] = (dm * gm).astype(BF16)
        dgh = dm * yh_ref[...] * gh * (1.0 - gh)
        dgm = dm * ym_ref[...] * gm * (1.0 - gm)
        dg_out[:, 0:D] = dgh.astype(BF16)
        dg_out[:, D:2 * D] = dgm.astype(BF16)
        return (jnp.concatenate([jnp.sum(dgh, axis=0, keepdims=True), jnp.sum(dgm, axis=0, keepdims=True)], axis=1),)

    dyh, dym, dgpre, g["b_merge"] = _rows(
        mix_bwd_fn, "mix_bwd", t, tm, [(gpre, D, 0), (gpre, D, 1), (y_hg, D, 0), (y_mla, D, 0), (dmixed, D, 0)],
        [s["b_merge"]], [(D, BF16), (D, BF16), (2 * D, BF16)], [(1, 2 * D)])
    g["w_hg_branch"] = as_rows(_mm(hgy, dyh, "tn", "branch_hg_dw"))
    g["w_mla_branch"] = as_rows(_mm(att, dym, "tn", "branch_mla_dw"))
    g["w_merge"] = _cols_dw(u, dgpre, mw, "proj_gate_dw")
    dhgy = _mm(dyh, w_hgb, "nt", "branch_hg_dx")
    datt, delta = _attn_out_bwd(dym, w_mlab, att, t)
    du_gate = _cols_dx(dgpre, w_mrg, "proj_gate_dx")

    dq, dk, dv = _flash_bwd(q, k, v, datt, lse.reshape(NH, 1, t), delta, t)
    (dp_mla, g["w_q_up"], g["w_kv_up"], g["mla_q_lora_norm"], g["mla_kv_lora_norm"], g["q_head_norm"],
     g["k_head_norm"]) = _mla_prep_bwd(*prep_args, dq, dk, dv, t)
    dp_hg, g["hg_lb_table"], g["hg_out_norm"] = _hgrn_bwd(p_hg, s["hg_lb_table"], s["hg_out_norm"], o_pre, states,
                                                          dhgy, t)
    dw_in_nat = jnp.concatenate([_mm(u, dp_hg, "tn", "proj_hg_dw"), _mm(u, dp_mla, "tn", "proj_mla_dw")], axis=1)
    g["w_in"] = dw_in_nat.reshape(D, N_DEV, -1).transpose(1, 0, 2)
    du_hg = _mm(dp_hg, w_in_hg, "nt", "proj_hg_dx")
    du_mla = _mm(dp_mla, w_in_mla, "nt", "proj_mla_dx")

    def mixnorm_bwd_fn(h_ref, a_ref, b_ref, c_ref, dh_ref, g_ref, dh_out, dhb_out):
        dx, dgain = _rms_bwd(h_ref[...], g_ref[...], a_ref[...] + b_ref[...] + c_ref[...])
        dh = dh_ref[...] + dx
        dh_out[...] = dh
        dhb_out[...] = (0.5 * dh).astype(BF16)
        return (dgain,)

    mix_gain = s["mix_norm"] + early_grads(g)[0:1, 0:1]
    dh1, dfo1, g["mix_norm"] = _rows(mixnorm_bwd_fn, "mixnorm_bwd", t, tm,
                                     [(h1, D, 0), (du_hg, D, 0), (du_mla, D, 0), (du_gate, D, 0), (dh2, D, 0)],
                                     [mix_gain], [(D, F32), (D, BF16)], [(1, D)])
    (grad_x, _, g["ffn1_norm"]), g["ffn1_w_in"], dwo = _ffn_bwd(
        dfo1, xn1, gu1, a1, w["ffn1_w_in"], ffn_out("ffn1_w_out"), "ffn1",
        (norm_bwd_fn(1.0), x, [dh1], [s["ffn1_norm"]], [(D, F32), (D, BF16)], [(1, D)]))
    g["ffn1_w_out"] = as_rows(dwo)
    return loss, grad_x, g


def _coords():
    return lax.axis_index("x"), lax.axis_index("y"), lax.axis_index("c")


def _hbm_call(body, name, ins, out_shapes, scratch):
    any_spec = pl.BlockSpec(memory_space=pl.ANY)
    return pl.pallas_call(
        body, name=name, out_shape=[jax.ShapeDtypeStruct(s, dt) for s, dt in out_shapes],
        in_specs=[any_spec] * len(ins), out_specs=[any_spec] * len(out_shapes), scratch_shapes=scratch,
    )(*ins)


def _my_slot():
    return 4 * lax.axis_index("x") + 2 * lax.axis_index("y") + lax.axis_index("c")


def _put_own(buf, own, index):
    return lax.dynamic_update_index_in_dim(buf, own, index, 0)


def _all_gather(blocks, name):
    nb = len(blocks)

    def body(*refs):
        x_refs, out_refs = refs[:nb], refs[nb:2 * nb]
        send_sems, recv_sems = refs[2 * nb:]
        x, y, c = _coords()
        me, sibling = (x, y, c), (x, y, 1 - c)
        chips = [(1 - x, y), (x, 1 - y), (1 - x, 1 - y)]

        def slot(b, px, py, pc):
            return out_refs[b].at[4 * px + 2 * py + pc]

        def copy(b, kk, block_of, to, src=None):
            return pltpu.make_async_remote_copy(
                src_ref=slot(b, *block_of) if src is None else src, dst_ref=slot(b, *block_of),
                send_sem=send_sems.at[b, kk], recv_sem=recv_sems.at[b, kk], device_id=to, device_id_type=MESH)

        first = [copy(b, 0, me, sibling, src=x_refs[b]) for b in range(nb)]
        first += [copy(b, 1 + j, me, (*chip, c), src=x_refs[b]) for j, chip in enumerate(chips) for b in range(nb)]
        for cp in first:
            cp.start()
        passed = []
        for j, chip in enumerate(chips):
            for b in range(nb):
                copy(b, 1 + j, (*chip, c), me).wait_recv()
                passed.append(copy(b, 4 + j, (*chip, c), sibling))
                passed[-1].start()
        for b in range(nb):
            copy(b, 0, sibling, me).wait_recv()
        for j, chip in enumerate(chips):
            for b in range(nb):
                copy(b, 4 + j, (*chip, 1 - c), me).wait_recv()
        for cp in first + passed:
            cp.wait_send()

    outs = _hbm_call(body, name, blocks, [((N_DEV,) + b.shape, b.dtype) for b in blocks],
                     [pltpu.SemaphoreType.DMA((nb, 7)), pltpu.SemaphoreType.DMA((nb, 7))])
    return [_put_own(o, b[None], _my_slot()) for o, b in zip(outs, blocks)]


def _gather_peers():
    x, y, c = _coords()
    return (x, y, c), [(x, y, 1 - c), (1 - x, y, c), (x, 1 - y, c), (1 - x, 1 - y, c)]


def _gather_start(blocks, after, name):
    nb = len(blocks)
    hbm, sem = pl.BlockSpec(memory_space=pltpu.HBM), pl.BlockSpec(memory_space=pltpu.SEMAPHORE)

    def body(*refs):
        x_refs, out_refs = refs[:nb], refs[nb:2 * nb]
        send_sems, recv_sems, token = refs[2 * nb + 1], refs[2 * nb + 2], refs[-1]
        (x, y, c), peers = _gather_peers()
        for kk, peer in enumerate(peers):
            for b in range(nb):
                pltpu.make_async_remote_copy(
                    src_ref=x_refs[b], dst_ref=out_refs[b].at[4 * x + 2 * y + c], send_sem=send_sems.at[4 * b + kk],
                    recv_sem=recv_sems.at[4 * b + kk], device_id=peer, device_id_type=MESH).start()
        token[...] = jnp.zeros_like(token)

    gathers = [pltpu.with_memory_space_constraint(lax.empty((N_DEV,) + b.shape, b.dtype), pltpu.HBM) for b in blocks]
    outs = pl.pallas_call(
        body, name=name,
        out_shape=(pltpu.SemaphoreType.DMA((4 * nb,)), pltpu.SemaphoreType.DMA((4 * nb,)),
                   *[pltpu.HBM(b.shape, b.dtype) for b in blocks], *[pltpu.HBM(b.shape, b.dtype) for b in gathers],
                   jax.ShapeDtypeStruct((SUBLANE, LANE), F32)),
        in_specs=[hbm] * (2 * nb) + [pl.BlockSpec(memory_space=pl.ANY)],
        out_specs=(sem, sem, *[hbm] * (2 * nb), pl.BlockSpec(memory_space=pltpu.VMEM)),
        input_output_aliases={i: 2 + i for i in range(2 * nb)},
        compiler_params=pltpu.CompilerParams(has_side_effects=pltpu.SideEffectType.DATAFLOW_SIDE_EFFECTING),
    )(*[pltpu.with_memory_space_constraint(b, pltpu.HBM) for b in blocks], *gathers, after)
    return outs[0], outs[1], list(outs[2:2 + nb]), list(outs[2 + nb:2 + 2 * nb]), outs[-1]


def _gather_wait(send_sems, recv_sems, thru, gathers, after, name):
    nb = len(thru)
    hbm, sem = pl.BlockSpec(memory_space=pltpu.HBM), pl.BlockSpec(memory_space=pltpu.SEMAPHORE)

    def body(*refs):
        x_refs, out_refs = refs[:nb], refs[nb:2 * nb]
        send_sems_, recv_sems_ = refs[2 * nb], refs[2 * nb + 1]
        _, peers = _gather_peers()
        for kk, (px, py, pc) in enumerate(peers):
            for b in range(nb):
                cp = pltpu.make_async_remote_copy(
                    src_ref=x_refs[b], dst_ref=out_refs[b].at[4 * px + 2 * py + pc], send_sem=send_sems_.at[4 * b + kk],
                    recv_sem=recv_sems_.at[4 * b + kk], device_id=(px, py, pc), device_id_type=MESH)
                cp.wait_send()
                cp.wait_recv()

    outs = pl.pallas_call(
        body, name=name,
        out_shape=(*[pltpu.HBM(b.shape, b.dtype) for b in thru], *[pltpu.HBM(b.shape, b.dtype) for b in gathers]),
        in_specs=[hbm] * (2 * nb) + [sem, sem, pl.BlockSpec(memory_space=pl.ANY)], out_specs=[hbm] * (2 * nb),
        input_output_aliases={i: i for i in range(2 * nb)},
        compiler_params=pltpu.CompilerParams(has_side_effects=pltpu.SideEffectType.DATAFLOW_SIDE_EFFECTING),
    )(*thru, *gathers, send_sems, recv_sems, after)
    return list(outs[:nb]), list(outs[nb:])


def _gather_finish(blocks, gathers, name):
    nb = len(blocks)

    def body(*refs):
        x_refs, in_refs, out_refs = refs[:nb], refs[nb:2 * nb], refs[2 * nb:3 * nb]
        send_sems, recv_sems = refs[3 * nb:]
        (x, y, c), peers = _gather_peers()
        copies = []
        for j, (px, py, _) in enumerate(peers[1:]):
            for b in range(nb):
                copies.append(pltpu.make_async_remote_copy(
                    src_ref=in_refs[b].at[4 * px + 2 * py + c], dst_ref=out_refs[b].at[4 * px + 2 * py + c],
                    send_sem=send_sems.at[b, j], recv_sem=recv_sems.at[b, j], device_id=(x, y, 1 - c),
                    device_id_type=MESH))
                copies[-1].start()
        for j, (px, py, _) in enumerate(peers[1:]):
            for b in range(nb):
                pltpu.make_async_remote_copy(
                    src_ref=in_refs[b].at[4 * px + 2 * py + c], dst_ref=out_refs[b].at[4 * px + 2 * py + 1 - c],
                    send_sem=send_sems.at[b, j], recv_sem=recv_sems.at[b, j], device_id=(x, y, 1 - c),
                    device_id_type=MESH).wait_recv()
        for cp in copies:
            cp.wait_send()

    any_spec = pl.BlockSpec(memory_space=pl.ANY)
    outs = pl.pallas_call(
        body, name=name, out_shape=[jax.ShapeDtypeStruct(b.shape, b.dtype) for b in gathers],
        in_specs=[any_spec] * (2 * nb), out_specs=[any_spec] * nb,
        input_output_aliases={nb + i: i for i in range(nb)},
        scratch_shapes=[pltpu.SemaphoreType.DMA((nb, 3)), pltpu.SemaphoreType.DMA((nb, 3))],
    )(*blocks, *gathers)
    return [_put_own(o, b[None], _my_slot()) for o, b in zip(outs, blocks)]


def _sibling_swap(bufs, name):
    nb = len(bufs)

    def body(*refs):
        x_refs, out_refs = refs[:nb], refs[nb:2 * nb]
        send_sems, recv_sems = refs[2 * nb:]
        x, y, c = _coords()
        copies = [pltpu.make_async_remote_copy(
            src_ref=x_refs[b].at[2 * q + 1 - c], dst_ref=out_refs[b].at[q], send_sem=send_sems.at[b, q],
            recv_sem=recv_sems.at[b, q], device_id=(x, y, 1 - c), device_id_type=MESH)
            for b in range(nb) for q in range(4)]
        for cp in copies:
            cp.start()
        for cp in copies:
            cp.wait()

    return _hbm_call(body, name, bufs, [((4,) + b.shape[1:], b.dtype) for b in bufs],
                     [pltpu.SemaphoreType.DMA((nb, 4)), pltpu.SemaphoreType.DMA((nb, 4))])


def _chip_exchange_start(bufs, name):
    nb = len(bufs)
    hbm, sem = pl.BlockSpec(memory_space=pltpu.HBM), pl.BlockSpec(memory_space=pltpu.SEMAPHORE)

    def body(*refs):
        x_refs, land_refs = refs[:nb], refs[nb:2 * nb]
        send_sems, recv_sems, token = refs[2 * nb], refs[2 * nb + 1], refs[-1]
        x, y, c = _coords()
        for j, (px, py) in enumerate([(1 - x, y), (x, 1 - y), (1 - x, 1 - y)]):
            for b in range(nb):
                pltpu.make_async_remote_copy(
                    src_ref=x_refs[b].at[2 * px + py], dst_ref=land_refs[b].at[2 * x + y], send_sem=send_sems.at[3 * b + j],
                    recv_sem=recv_sems.at[3 * b + j], device_id=(px, py, c), device_id_type=MESH).start()
        token[...] = jnp.zeros_like(token)

    lands = [pltpu.with_memory_space_constraint(lax.empty(b.shape, b.dtype), pltpu.HBM) for b in bufs]
    outs = pl.pallas_call(
        body, name=name,
        out_shape=(pltpu.SemaphoreType.DMA((3 * nb,)), pltpu.SemaphoreType.DMA((3 * nb,)),
                   *[pltpu.HBM(b.shape, b.dtype) for b in bufs], *[pltpu.HBM(b.shape, b.dtype) for b in bufs],
                   jax.ShapeDtypeStruct((SUBLANE, LANE), F32)),
        in_specs=[hbm] * (2 * nb), out_specs=(sem, sem, *[hbm] * (2 * nb), pl.BlockSpec(memory_space=pltpu.VMEM)),
        input_output_aliases={i: 2 + i for i in range(2 * nb)},
        compiler_params=pltpu.CompilerParams(has_side_effects=pltpu.SideEffectType.DATAFLOW_SIDE_EFFECTING),
    )(*[pltpu.with_memory_space_constraint(b, pltpu.HBM) for b in bufs], *lands)
    return outs[0], outs[1], list(outs[2:2 + nb]), list(outs[2 + nb:2 + 2 * nb]), outs[-1]


def _chip_exchange_wait(send_sems, recv_sems, thru, lands, after, name):
    nb = len(thru)
    hbm, sem = pl.BlockSpec(memory_space=pltpu.HBM), pl.BlockSpec(memory_space=pltpu.SEMAPHORE)

    def body(*refs):
        x_refs, land_refs = refs[:nb], refs[nb:2 * nb]
        send_sems_, recv_sems_ = refs[2 * nb], refs[2 * nb + 1]
        x, y, c = _coords()
        for j, (px, py) in enumerate([(1 - x, y), (x, 1 - y), (1 - x, 1 - y)]):
            for b in range(nb):
                cp = pltpu.make_async_remote_copy(
                    src_ref=x_refs[b].at[2 * px + py], dst_ref=land_refs[b].at[2 * px + py],
                    send_sem=send_sems_.at[3 * b + j], recv_sem=recv_sems_.at[3 * b + j], device_id=(px, py, c),
                    device_id_type=MESH)
                cp.wait_send()
                cp.wait_recv()

    outs = pl.pallas_call(
        body, name=name,
        out_shape=(*[pltpu.HBM(b.shape, b.dtype) for b in thru], *[pltpu.HBM(b.shape, b.dtype) for b in lands]),
        in_specs=[hbm] * (2 * nb) + [sem, sem, pl.BlockSpec(memory_space=pl.ANY)], out_specs=[hbm] * (2 * nb),
        input_output_aliases={i: i for i in range(2 * nb)},
        compiler_params=pltpu.CompilerParams(has_side_effects=pltpu.SideEffectType.DATAFLOW_SIDE_EFFECTING),
    )(*thru, *lands, send_sems, recv_sems, after)
    return list(outs[:nb]), list(outs[nb:])


def _chip_sum(g, r1, c, name):
    _, r, cw = g.shape
    tr = _pick(r, (256, 176, 128))

    def body(c_ref, g_ref, r_ref, o_ref):
        o_ref[...] = (g_ref[...] + r_ref[...]).astype(BF16)

    grid_spec = pltpu.PrefetchScalarGridSpec(
        num_scalar_prefetch=1, grid=(4, r // tr),
        in_specs=[_BS((None, None, tr, cw), lambda q, i, c_ref: (q, c_ref[0], i, 0)),
                  _BS((None, tr, cw), lambda q, i, c_ref: (q, i, 0))],
        out_specs=_BS((None, tr, cw), lambda q, i, c_ref: (q, i, 0)))
    return pl.pallas_call(
        body, name=name, grid_spec=grid_spec, out_shape=jax.ShapeDtypeStruct((4, r, cw), BF16),
        compiler_params=_params(("parallel", "parallel")),
    )(c.reshape(1).astype(jnp.int32), g.reshape(4, 2, r, cw), r1)


def _adamw_math(w, g, m, v):
    m = B1 * m + (1.0 - B1) * g
    v = B2 * v + (1.0 - B2) * (g * g)
    m_hat = m / (1.0 - B1 ** STEP)
    v_hat = v / (1.0 - B2 ** STEP)
    return -LR * (m_hat / (jnp.sqrt(v_hat) + AEPS) + WD * w), m, v


def _sum_adamw(parts, w, m, v, name):
    r, c = w.shape
    tr = _pick(r, (256, 176, 128))

    def body(p0, p1, p2, p3, w_ref, m_ref, v_ref, g_out, d_out, m_out, v_out):
        g = ((p0[...].astype(F32) + p1[...].astype(F32)) + p2[...].astype(F32)) + p3[...].astype(F32)
        g_out[...] = g
        d_out[...], m_out[...], v_out[...] = _adamw_math(w_ref[...], g, m_ref[...], v_ref[...])

    part = lambda q: _BS((None, tr, c), functools.partial(lambda i, q: (q, i, 0), q=q))
    plain = _BS((tr, c), lambda i: (i, 0))
    return pl.pallas_call(
        body, name=name, grid=(r // tr,), in_specs=[part(q) for q in range(4)] + [plain] * 3,
        out_specs=[plain] * 4, out_shape=[jax.ShapeDtypeStruct((r, c), F32)] * 4,
        compiler_params=_params(("parallel",)),
    )(parts, parts, parts, parts, w, m, v)


def _small_update(gathered, w, m, v):
    r = w.shape[0]

    def body(ga_ref, w_ref, m_ref, v_ref, g_out, d_out, m_out, v_out):
        g = ga_ref[0]
        for dev in range(1, N_DEV):
            g = g + ga_ref[dev]
        g_out[...] = g
        d_out[...], m_out[...], v_out[...] = _adamw_math(w_ref[...], g, m_ref[...], v_ref[...])

    return pl.pallas_call(
        body, name="small_update", out_shape=[jax.ShapeDtypeStruct((r, LANE), F32)] * 4,
    )(gathered, w, m, v)


def _pack_small(vals):
    rows = []
    for name, (r, n) in SMALL:
        flat = vals[name].reshape(-1)
        pad = (-flat.shape[0]) % (SUBLANE * LANE)
        rows.append(jnp.pad(flat, (0, pad)).reshape(-1, LANE))
    return jnp.concatenate(rows, axis=0)


def _unpack_small(packed):
    out, off = {}, 0
    for name, (r, n) in SMALL:
        nrow = -(-(r * n) // (SUBLANE * LANE)) * SUBLANE
        out[name] = packed[off:off + nrow].reshape(-1)[:r * n].reshape(r, n)
        off += nrow
    return out


def kernel(x, positions, ffn1_norm, ffn1_w_in, ffn1_w_out, mix_norm, w_in, hg_lb_table, hg_out_norm, w_hg_branch, mla_q_lora_norm, w_q_up, mla_kv_lora_norm, w_kv_up, q_head_norm, k_head_norm, w_mla_branch, w_merge, b_merge, w_out, ffn2_norm, ffn2_w_in, ffn2_w_out, final_norm, loss_target, m_ffn1_norm, m_ffn1_w_in, m_ffn1_w_out, m_mix_norm, m_w_in, m_hg_lb_table, m_hg_out_norm, m_w_hg_branch, m_mla_q_lora_norm, m_w_q_up, m_mla_kv_lora_norm, m_w_kv_up, m_q_head_norm, m_k_head_norm, m_w_mla_branch, m_w_merge, m_b_merge, m_w_out, m_ffn2_norm, m_ffn2_w_in, m_ffn2_w_out, m_final_norm, v_ffn1_norm, v_ffn1_w_in, v_ffn1_w_out, v_mix_norm, v_w_in, v_hg_lb_table, v_hg_out_norm, v_w_hg_branch, v_mla_q_lora_norm, v_w_q_up, v_mla_kv_lora_norm, v_w_kv_up, v_q_head_norm, v_k_head_norm, v_w_mla_branch, v_w_merge, v_b_merge, v_w_out, v_ffn2_norm, v_ffn2_w_in, v_ffn2_w_out, v_final_norm):
    args = dict(locals())
    t = x.shape[1]
    big_w = {n: args[n][0] for n, _, _ in BIG}
    small = {n: args[n].reshape(shape) for n, shape in SMALL}

    names = [n for n, _, _ in BIG]
    first, rest = names[:2], names[2:]
    full = dict(zip(first, _all_gather([big_w[n].astype(BF16) for n in first], "weights_all_gather_ffn1")))
    g_send, g_recv, g_thru, g_bufs, g_token = _gather_start([big_w[n].astype(BF16) for n in rest], full[first[0]],
                                                            "weights_gather_start")
    gains = dict(small, ffn1_norm=small["ffn1_norm"] + g_token[0:1, 0:1])

    def late_weights(after):
        blocks, bufs = _gather_wait(g_send, g_recv, g_thru, g_bufs, after, "weights_gather_wait")
        return dict(zip(rest, _gather_finish(blocks, bufs, "weights_gather_finish")))

    inv_freq = ROPE_THETA ** (-jnp.arange(0, ROPE, 2, dtype=F32) / ROPE)
    ang = positions[0].astype(F32)[:, None] * inv_freq
    cs = jnp.concatenate([jnp.cos(ang), jnp.cos(ang)], axis=1)
    sn = jnp.concatenate([jnp.sin(ang), jnp.sin(ang)], axis=1)

    c = lax.axis_index("c")
    chip = 2 * lax.axis_index("x") + lax.axis_index("y")
    early = {}

    def chip_sums_of(g, ns, tag):
        from_sibling = _sibling_swap([g[n] for n in ns], "grads_sibling_swap_" + tag)
        return [_chip_sum(g[n], r1, c, "chip_sum_" + n) for n, r1 in zip(ns, from_sibling)]

    def early_grads(g):
        early["names"] = [n for n in names if n in g]
        early["sums"] = chip_sums_of(g, early["names"], "early")
        early["send"], early["recv"], early["thru"], early["lands"], token = _chip_exchange_start(
            early["sums"], "grads_exchange_start")
        return token

    loss_row, grad_x, g = _local_step(x[0], loss_target[0], cs, sn, full, late_weights, gains, early_grads)
    late = [n for n in names if n not in early["names"]]
    l_send, l_recv, l_thru, l_lands, l_token = _chip_exchange_start(chip_sums_of(g, late, "late"),
                                                                    "grads_exchange_late_start")
    sent, landed = _chip_exchange_wait(early["send"], early["recv"], early["thru"], early["lands"], l_token,
                                       "grads_exchange_wait")
    exchanged = {n: _put_own(land, lax.dynamic_index_in_dim(own, chip, 0), chip)
                 for n, land, own in zip(early["names"], landed, sent)}

    small_packed = jnp.concatenate([_pack_small(g), jnp.pad(loss_row, ((0, SUBLANE - 1), (0, 0)))], axis=0)
    small_all = _all_gather([small_packed], "small_all_gather")[0]
    zero_tail = jnp.zeros((SUBLANE, LANE), F32)
    pk = lambda d: jnp.concatenate([_pack_small(d), zero_tail], axis=0)
    sg, sd, sm, sv = _small_update(
        small_all, pk(small), pk({n: args["m_" + n].reshape(shape) for n, shape in SMALL}),
        pk({n: args["v_" + n].reshape(shape) for n, shape in SMALL}))
    n_small_rows = _pack_small(small).shape[0]
    loss = sg[n_small_rows, 0]
    outs = {k_: _unpack_small(a) for k_, a in (("grad", sg), ("delta", sd), ("new_m", sm), ("new_v", sv))}

    def adamw(n):
        outs["grad"][n], outs["delta"][n], outs["new_m"][n], outs["new_v"][n] = _sum_adamw(
            exchanged[n], big_w[n], args["m_" + n][0], args["v_" + n][0], "adamw_" + n)

    for n in early["names"]:
        adamw(n)
    sent, landed = _chip_exchange_wait(l_send, l_recv, l_thru, l_lands, outs["delta"][early["names"][-1]],
                                       "grads_exchange_late_wait")
    for n, land, own in zip(late, landed, sent):
        exchanged[n] = _put_own(land, lax.dynamic_index_in_dim(own, chip, 0), chip)
        adamw(n)

    def shaped(kind, n):
        return outs[kind][n].reshape(args[n].shape)

    return (loss, grad_x[None], *[shaped("grad", n) for n in WEIGHT_ORDER], *[shaped("delta", n) for n in WEIGHT_ORDER],
            *[shaped("new_m", n) for n in WEIGHT_ORDER], *[shaped("new_v", n) for n in WEIGHT_ORDER])
```

```python
import functools

import jax
import jax.numpy as jnp
from jax import lax
from jax.experimental import pallas as pl
from jax.experimental.pallas import tpu as pltpu

F32 = jnp.float32
BF16 = jnp.bfloat16

D = 1024
FF = 2816
NH = 8
HD = 128
ROPE = 64
QK = HD + ROPE
QL = 384
KVL = 256
HGW = NH * HD
CHUNK = 64
EPS = 1e-6
ROPE_THETA = 10000.0
SCALE = QK ** -0.5
LOG2E = 1.4426950408889634

LR, B1, B2, AEPS, WD, STEP = 0.001, 0.9, 0.999, 1e-08, 0.01, 10

HB = 128
SUB = 16
EXP_CLAMP = 80.0
ATT_TILES = (512, 256, 128)
ATT_Q = (1024, 512, 256, 128)
ATT_KEY_TILES = 1
ROW_TILES = (1024, 512, 256, 128)

LANE = 128
SUBLANE = 8
VMEM_LIMIT = 56 << 20

N_DEV = 8
MESH = pl.DeviceIdType.MESH

BIG = (
    ("ffn1_w_in", (D, 2 * FF), 1), ("ffn1_w_out", (FF, D), 0), ("w_in", (D, 4800), 1),
    ("w_hg_branch", (HGW, D), 0), ("w_q_up", (QL, NH * QK), 1), ("w_kv_up", (KVL, NH * 2 * HD), 1),
    ("w_mla_branch", (NH * HD, D), 0), ("w_merge", (D, 2 * D), 1), ("w_out", (D, D), 0),
    ("ffn2_w_in", (D, 2 * FF), 1), ("ffn2_w_out", (FF, D), 0),
)
SMALL = (
    ("ffn1_norm", (1, D)), ("mix_norm", (1, D)), ("hg_lb_table", (2, HGW)), ("hg_out_norm", (1, HD)),
    ("mla_q_lora_norm", (1, QL)), ("mla_kv_lora_norm", (1, KVL)), ("q_head_norm", (1, QK)),
    ("k_head_norm", (1, QK)), ("b_merge", (1, 2 * D)), ("ffn2_norm", (1, D)), ("final_norm", (1, D)),
)
WEIGHT_ORDER = ("ffn1_norm", "ffn1_w_in", "ffn1_w_out", "mix_norm", "w_in", "hg_lb_table", "hg_out_norm",
                "w_hg_branch", "mla_q_lora_norm", "w_q_up", "mla_kv_lora_norm", "w_kv_up", "q_head_norm",
                "k_head_norm", "w_mla_branch", "w_merge", "b_merge", "w_out", "ffn2_norm", "ffn2_w_in",
                "ffn2_w_out", "final_norm")


def _pick(n, cands):
    for c in cands:
        if n % c == 0:
            return c
    return n


def _params(sem):
    return pltpu.CompilerParams(dimension_semantics=sem, vmem_limit_bytes=VMEM_LIMIT)


def _sig(x):
    return 0.5 * jnp.tanh(0.5 * x) + 0.5


def _dot(a, b):
    return jnp.dot(a.astype(BF16), b.astype(BF16), preferred_element_type=F32)


def _dot_nt(a, b):
    return lax.dot_general(a.astype(BF16), b.astype(BF16), (((1,), (1,)), ((), ())),
                           preferred_element_type=F32)


def _dot_tn(a, b):
    return lax.dot_general(a.astype(BF16), b.astype(BF16), (((0,), (0,)), ((), ())),
                           preferred_element_type=F32)


def _split3(x):
    x1 = x.astype(BF16)
    r1 = x - x1.astype(F32)
    x2 = r1.astype(BF16)
    x3 = (r1 - x2.astype(F32)).astype(BF16)
    return x1, x2, x3


def _dot_sel(m, x):
    x1, x2, x3 = _split3(x)
    d = lambda p: jnp.dot(m, p, preferred_element_type=F32)
    return d(x1) + d(x2) + d(x3)


def _sel_dot(x, m):
    x1, x2, x3 = _split3(x)
    d = lambda p: jnp.dot(p, m, preferred_element_type=F32)
    return d(x1) + d(x2) + d(x3)


_TN = (1408, 1024, 768, 512, 384, 256, 128)


def _mm(a, b, mode, name, out_dtype=F32):
    if mode == "tn":
        t, m = a.shape
        n = b.shape[1]
        tt, tm, tn = _pick(t, (512, 256, 128)), _pick(m, _TN), _pick(n, _TN)

        def body(a_ref, b_ref, o_ref):
            @pl.when(pl.program_id(2) == 0)
            def _():
                o_ref[...] = jnp.zeros_like(o_ref)

            o_ref[...] += _dot_tn(a_ref[...], b_ref[...])

        return pl.pallas_call(
            body, name=name, grid=(m // tm, n // tn, t // tt),
            in_specs=[pl.BlockSpec((tt, tm), lambda i, j, k: (k, i)),
                      pl.BlockSpec((tt, tn), lambda i, j, k: (k, j))],
            out_specs=pl.BlockSpec((tm, tn), lambda i, j, k: (i, j)),
            out_shape=jax.ShapeDtypeStruct((m, n), F32),
            compiler_params=_params(("parallel", "parallel", "arbitrary")),
        )(a, b)

    m, k = a.shape
    tm = _pick(m, ROW_TILES)
    if mode == "nn":
        n = b.shape[1]
        tn = _pick(n, _TN)
        b_spec = pl.BlockSpec((k, tn), lambda i, j: (0, j))
        dot = _dot
    else:
        n = b.shape[0]
        tn = _pick(n, _TN if k <= 4096 else (512, 256, 128))
        b_spec = pl.BlockSpec((tn, k), lambda i, j: (j, 0))
        dot = _dot_nt

    def body(a_ref, b_ref, o_ref):
        o_ref[...] = dot(a_ref[...], b_ref[...]).astype(o_ref.dtype)

    return pl.pallas_call(
        body, name=name, grid=(m // tm, n // tn),
        in_specs=[pl.BlockSpec((tm, k), lambda i, j: (i, 0)), b_spec],
        out_specs=pl.BlockSpec((tm, tn), lambda i, j: (i, j)),
        out_shape=jax.ShapeDtypeStruct((m, n), out_dtype),
        compiler_params=_params(("parallel", "parallel")),
    )(a, b)


_DOTS = {"nn": _dot, "nt": _dot_nt, "tn": _dot_tn}
_BS = pl.BlockSpec


def _mmcall(name, kind, a, b, a_spec, b_spec, o_spec, o_shape, grid, red_axis=None, out_dtype=F32):
    dot = _DOTS[kind]

    def body(a_ref, b_ref, o_ref):
        if red_axis is None:
            o_ref[...] = dot(a_ref[...], b_ref[...]).astype(o_ref.dtype)
        else:
            @pl.when(pl.program_id(red_axis) == 0)
            def _():
                o_ref[...] = jnp.zeros_like(o_ref)

            o_ref[...] += dot(a_ref[...], b_ref[...])

    sem = tuple("arbitrary" if ax == red_axis else "parallel" for ax in range(len(grid)))
    return pl.pallas_call(
        body, name=name, grid=grid, in_specs=[a_spec, b_spec], out_specs=o_spec,
        out_shape=jax.ShapeDtypeStruct(o_shape, out_dtype), compiler_params=_params(sem),
    )(a, b)


class _AsRef:
    def __init__(self, value):
        self.value = value

    def __getitem__(self, idx):
        return self.value


def _mm_stack_red(a, w, name, kind, epilogue=None):
    s, t, n = a.shape
    nout = w.shape[2] if kind == "nn" else w.shape[1]
    if epilogue is None:
        tm = _pick(t, ROW_TILES)
        return _mmcall(name, kind, a, w, _BS((None, tm, n), lambda i, j: (j, i, 0)),
                       _BS((None,) + w.shape[1:], lambda i, j: (j, 0, 0)), _BS((tm, nout), lambda i, j: (i, 0)),
                       (t, nout), (t // tm, s), red_axis=1)
    fn, first, rows, vecs, outs, accs = epilogue
    rows = [first] + list(rows)
    tm = _pick(t, ROW_TILES)
    n_row, n_vec, n_out = len(rows), len(vecs), len(outs)
    dot = _DOTS[kind]

    def body(*refs):
        a_ref, w_ref, prod = refs[0], refs[1], refs[-1]
        row_refs = refs[2:2 + n_row]
        vec_refs = refs[2 + n_row:2 + n_row + n_vec]
        out_refs = refs[2 + n_row + n_vec:2 + n_row + n_vec + n_out]
        acc_refs = refs[2 + n_row + n_vec + n_out:-1]
        i, j = pl.program_id(0), pl.program_id(1)

        @pl.when(j == 0)
        def _():
            prod[...] = jnp.zeros_like(prod)

        prod[...] += dot(a_ref[...], w_ref[...])

        @pl.when(j == s - 1)
        def _():
            res = fn(row_refs[0], _AsRef(prod[...]), *row_refs[1:], *vec_refs, *out_refs)
            if acc_refs:
                @pl.when(i == 0)
                def _():
                    for r in acc_refs:
                        r[...] = jnp.zeros_like(r)

                for r, val in zip(acc_refs, res):
                    r[...] += val

    in_specs = [_BS((None, tm, n), lambda i, j: (j, i, 0)), _BS((None,) + w.shape[1:], lambda i, j: (j, 0, 0))]
    in_specs += [_BS((tm, r.shape[1]), lambda i, j: (i, 0)) for r in rows]
    in_specs += [_BS(v.shape, lambda i, j: (0, 0)) for v in vecs]
    out_specs = [_BS((tm, wd), lambda i, j: (i, 0)) for wd, _ in outs] + [_BS(sh, lambda i, j: (0, 0)) for sh in accs]
    out_shape = [jax.ShapeDtypeStruct((t, wd), dt) for wd, dt in outs] + [jax.ShapeDtypeStruct(sh, F32) for sh in accs]
    return pl.pallas_call(
        body, name=name, grid=(t // tm, s), in_specs=in_specs, out_specs=out_specs, out_shape=out_shape,
        scratch_shapes=[pltpu.VMEM((tm, nout), F32)],
        compiler_params=_params(("arbitrary" if accs else "parallel", "arbitrary")),
    )(a, w, *rows, *vecs)


def _mm_stack_tn(a, b, name):
    grp = 4
    if a.ndim == 2:
        t, k = a.shape
        s, _, n = b.shape
        tt = _pick(t, (512, 256, 128))
        a_spec = _BS((tt, k), lambda j, r: (r, 0))
        b_spec, b_in = _BS((None, grp, tt, n), lambda j, r: (j, 0, r, 0)), b.reshape(s // grp, grp, t, n)
        a_in = a
    else:
        s, t, k = a.shape
        n = b.shape[1]
        tt = _pick(t, (512, 256, 128))
        a_spec, a_in = _BS((None, grp, tt, k), lambda j, r: (j, 0, r, 0)), a.reshape(s // grp, grp, t, k)
        b_spec, b_in = _BS((tt, n), lambda j, r: (r, 0)), b

    def body(a_ref, b_ref, o_ref):
        @pl.when(pl.program_id(1) == 0)
        def _():
            o_ref[...] = jnp.zeros_like(o_ref)

        shared = a_ref[...] if a.ndim == 2 else b_ref[...]
        for e in range(grp):
            o_ref[e] += _dot_tn(shared, b_ref[e]) if a.ndim == 2 else _dot_tn(a_ref[e], shared)

    return pl.pallas_call(
        body, name=name, grid=(s // grp, t // tt), in_specs=[a_spec, b_spec],
        out_specs=_BS((None, grp, k, n), lambda j, r: (j, 0, 0, 0)),
        out_shape=jax.ShapeDtypeStruct((s // grp, grp, k, n), F32),
        compiler_params=_params(("parallel", "arbitrary")),
    )(a_in, b_in).reshape(s, k, n)


def _cols_fwd(x, w, name):
    t, k = x.shape
    s, _, n = w.shape
    tm = _pick(t, ROW_TILES)

    def body(x_ref, w_ref, o_ref):
        x_ = x_ref[...]
        for j in range(s):
            o_ref[:, n * j:n * (j + 1)] = _dot(x_, w_ref[j])

    return pl.pallas_call(
        body, name=name, grid=(t // tm,),
        in_specs=[_BS((tm, k), lambda i: (i, 0)), _BS((s, k, n), lambda i: (0, 0, 0))],
        out_specs=_BS((tm, s * n), lambda i: (i, 0)), out_shape=jax.ShapeDtypeStruct((t, s * n), F32),
        compiler_params=_params(("parallel",)),
    )(x, w)


def _cols_dx(d, w, name):
    t = d.shape[0]
    s, k, n = w.shape
    tm = _pick(t, ROW_TILES)

    def body(d_ref, w_ref, o_ref):
        acc = _dot_nt(d_ref[:, 0:n], w_ref[0])
        for j in range(1, s):
            acc = acc + _dot_nt(d_ref[:, n * j:n * (j + 1)], w_ref[j])
        o_ref[...] = acc

    return pl.pallas_call(
        body, name=name, grid=(t // tm,),
        in_specs=[_BS((tm, s * n), lambda i: (i, 0)), _BS((s, k, n), lambda i: (0, 0, 0))],
        out_specs=_BS((tm, k), lambda i: (i, 0)), out_shape=jax.ShapeDtypeStruct((t, k), F32),
        compiler_params=_params(("parallel",)),
    )(d, w)


def _cols_dw(x, d, n, name):
    t, k = x.shape
    s = d.shape[1] // n
    tt = _pick(t, (512, 256, 128))

    def body(x_ref, d_ref, o_ref):
        @pl.when(pl.program_id(0) == 0)
        def _():
            o_ref[...] = jnp.zeros_like(o_ref)

        x_ = x_ref[...]
        for j in range(s):
            o_ref[j] += _dot_tn(x_, d_ref[:, n * j:n * (j + 1)])

    return pl.pallas_call(
        body, name=name, grid=(t // tt,),
        in_specs=[_BS((tt, k), lambda r: (r, 0)), _BS((tt, s * n), lambda r: (r, 0))],
        out_specs=_BS((s, k, n), lambda r: (0, 0, 0)), out_shape=jax.ShapeDtypeStruct((s, k, n), F32),
        compiler_params=_params(("arbitrary",)),
    )(x, d)


def _rows(fn, name, t, tm, ins, vecs, outs, accs=()):
    n_in, n_out, n_acc = len(ins) + len(vecs), len(outs), len(accs)

    def body(*refs):
        res = fn(*refs[:n_in + n_out])
        if n_acc:
            acc_refs = refs[n_in + n_out:]

            @pl.when(pl.program_id(0) == 0)
            def _():
                for r in acc_refs:
                    r[...] = jnp.zeros_like(r)

            for r, val in zip(acc_refs, res):
                r[...] += val

    in_specs = [pl.BlockSpec((tm, bw), functools.partial(lambda i, cb: (i, cb), cb=cb)) for _, bw, cb in ins]
    in_specs += [pl.BlockSpec(v.shape, lambda i: (0, 0)) for v in vecs]
    out_specs = [pl.BlockSpec((tm, w), lambda i: (i, 0)) for w, _ in outs]
    out_specs += [pl.BlockSpec(s, lambda i: (0, 0)) for s in accs]
    out_shape = [jax.ShapeDtypeStruct((t, w), dt) for w, dt in outs]
    out_shape += [jax.ShapeDtypeStruct(s, F32) for s in accs]
    return pl.pallas_call(
        body, name=name, grid=(t // tm,), in_specs=in_specs, out_specs=out_specs, out_shape=out_shape,
        compiler_params=_params(("arbitrary",) if n_acc else ("parallel",)),
    )(*[a for a, _, _ in ins], *vecs)


def _rms(x, g):
    return x * lax.rsqrt(jnp.mean(x * x, axis=-1, keepdims=True) + EPS) * g


def _rms_bwd(x, g, dy):
    xh = x * lax.rsqrt(jnp.mean(x * x, axis=-1, keepdims=True) + EPS)
    r = lax.rsqrt(jnp.mean(x * x, axis=-1, keepdims=True) + EPS)
    dyg = dy * g
    dx = r * (dyg - xh * jnp.mean(dyg * xh, axis=-1, keepdims=True))
    return dx, jnp.sum(dy * xh, axis=0, keepdims=True)


def _hgrn_mats():
    row = lax.broadcasted_iota(jnp.int32, (HB, HB), 0)
    col = lax.broadcasted_iota(jnp.int32, (HB, HB), 1)
    return row, col


def _hgrn_gates(qr, z, t0, t1):
    lb = 1.0 / (1.0 + jnp.exp(t1 - t0))
    th = jnp.tanh(0.5 * z)
    sz, sneg = 0.5 + 0.5 * th, 0.5 - 0.5 * th
    f = lb + (1.0 - lb) * sz
    return lb, sz, sneg, f, jnp.log(f), (1.0 - lb) * sneg, qr * _sig(qr)


def _hgrn_scores(q, k, cum):
    mids = [cum[SUB * i + SUB // 2 - 1:SUB * i + SUB // 2, :] for i in range(HB // SUB)]
    cmid = jnp.concatenate([cum[SUB * i:SUB * (i + 1)] - mids[i] for i in range(HB // SUB)], axis=0)
    qd = q * jnp.exp(jnp.minimum(cmid, EXP_CLAMP))
    qd_b = qd.astype(BF16)
    kds, parts = [], []
    for i in range(HB // SUB):
        kd = k * jnp.exp(jnp.minimum(mids[i] - cum, EXP_CLAMP))
        kds.append(kd)
        parts.append(_dot_nt(qd_b[SUB * i:SUB * (i + 1)], kd))
    return qd, kds, jnp.concatenate(parts, axis=0), cmid


def _hgrn_fwd(p_hg, table, gain, t):
    nblk = t // HB

    def body(q_ref, f_ref, i_ref, g_ref, tab_ref, gain_ref, o_ref, y_ref, st_ref, state):
        @pl.when(pl.program_id(0) == 0)
        def _():
            state[...] = jnp.zeros_like(state)

        row, col = _hgrn_mats()
        causal = col <= row
        tri = causal.astype(BF16)
        for h in range(NH):
            sl = slice(HD * h, HD * (h + 1))
            v, gr = i_ref[:, sl], g_ref[:, sl]
            _, _, _, _, lf, k, q = _hgrn_gates(q_ref[:, sl], f_ref[:, sl], tab_ref[0:1, sl], tab_ref[1:2, sl])
            cum = _dot_sel(tri, lf)
            _, _, s, _ = _hgrn_scores(q, k, cum)
            p = jnp.where(causal, s, 0.0)
            st = state[h]
            st_ref[h, 0] = st
            o = _dot(p, v) + _dot_nt(q * jnp.exp(cum), st)
            last = cum[HB - 1:HB, :]
            state[h] = st * jnp.exp(last) + _dot_tn(v, k * jnp.exp(last - cum))
            o_ref[:, sl] = o
            y_ref[:, sl] = (_rms(o, gain_ref[...]) * gr * _sig(gr)).astype(BF16)

    blk = lambda cb: pl.BlockSpec((HB, HGW), functools.partial(lambda n, cb: (n, cb), cb=cb))
    return pl.pallas_call(
        body, name="hgrn_fwd", grid=(nblk,),
        in_specs=[blk(0), blk(1), blk(2), blk(3), pl.BlockSpec((2, HGW), lambda n: (0, 0)),
                  pl.BlockSpec((1, HD), lambda n: (0, 0))],
        out_specs=[pl.BlockSpec((HB, HGW), lambda n: (n, 0)), pl.BlockSpec((HB, HGW), lambda n: (n, 0)),
                   pl.BlockSpec((NH, 1, HD, HD), lambda n: (0, n, 0, 0))],
        out_shape=[jax.ShapeDtypeStruct((t, HGW), F32), jax.ShapeDtypeStruct((t, HGW), BF16),
                   jax.ShapeDtypeStruct((NH, nblk, HD, HD), F32)],
        scratch_shapes=[pltpu.VMEM((NH, HD, HD), F32)],
        compiler_params=_params(("arbitrary",)),
    )(p_hg, p_hg, p_hg, p_hg, table, gain)


def _hgrn_bwd(p_hg, table, gain, o_pre, states, dy, t):
    nblk = t // HB

    def body(q_ref, f_ref, i_ref, g_ref, tab_ref, gain_ref, o_ref, st_ref, dy_ref, dp_ref, dtab_ref, dgain_ref,
             dstate):
        @pl.when(pl.program_id(0) == 0)
        def _():
            dstate[...] = jnp.zeros_like(dstate)
            dtab_ref[...] = jnp.zeros_like(dtab_ref)
            dgain_ref[...] = jnp.zeros_like(dgain_ref)

        row, col = _hgrn_mats()
        causal = col <= row
        tri = causal.astype(BF16)
        tri_t = (row <= col).astype(BF16)
        dgain = jnp.zeros((1, HD), F32)
        for h in range(NH):
            sl = slice(HD * h, HD * (h + 1))
            qr, z, v, gr = q_ref[:, sl], f_ref[:, sl], i_ref[:, sl], g_ref[:, sl]
            lb, sz, sneg, f, lf, k, q = _hgrn_gates(qr, z, tab_ref[0:1, sl], tab_ref[1:2, sl])
            cum = _dot_sel(tri, lf)
            qd, kds, s, cmid = _hgrn_scores(q, k, cum)
            p = jnp.where(causal, s, 0.0)
            st = st_ref[h, 0]
            dst = dstate[h]
            o = o_ref[:, sl]
            sg = _sig(gr)
            dyh = dy_ref[:, sl]
            on = _rms(o, gain_ref[...])
            dgr = dyh * on * sg * (1.0 + gr * (1.0 - sg))
            do, dg_h = _rms_bwd(o, gain_ref[...], dyh * gr * sg)
            dgain = dgain + dg_h
            do_b = do.astype(BF16)
            ecum = jnp.exp(cum)
            qc = q * ecum
            last = cum[HB - 1:HB, :]
            edec = jnp.exp(last - cum)
            kdec = k * edec
            dp = jnp.where(causal, _dot_nt(do_b, v), 0.0)
            dv = _dot(p.T, do_b) + _dot_nt(kdec, dst)
            dqc = _dot(do_b, st)
            dkdec = _dot(v, dst)
            dstate[h] = dst * jnp.exp(last) + _dot(do.T, qc)
            dp_b = dp.astype(BF16)
            dqd = jnp.concatenate([_dot(dp_b[SUB * i:SUB * (i + 1)], kds[i]) for i in range(HB // SUB)], axis=0)
            gq = dqd * qd
            dq = dqd * jnp.exp(jnp.minimum(cmid, EXP_CLAMP)) + dqc * ecum
            gs = dkdec * kdec
            dk = dkdec * edec
            dcum = dqc * qc - gs + gq
            dcum = dcum + jnp.where(row == HB - 1, jnp.sum(gs, axis=0, keepdims=True)
                                    + jnp.exp(last) * jnp.sum(st * dst, axis=0, keepdims=True), 0.0)
            qd_b = qd.astype(BF16)
            for i in range(HB // SUB):
                dkd = _dot_tn(dp_b[SUB * i:SUB * (i + 1)], qd_b[SUB * i:SUB * (i + 1)])
                mid = cum[SUB * i + SUB // 2 - 1:SUB * i + SUB // 2, :]
                dk = dk + dkd * jnp.exp(jnp.minimum(mid - cum, EXP_CLAMP))
                gk = dkd * kds[i]
                to_mid = jnp.sum(gk, axis=0, keepdims=True) - jnp.sum(gq[SUB * i:SUB * (i + 1)], axis=0, keepdims=True)
                dcum = dcum - gk + jnp.where(row == SUB * i + SUB // 2 - 1, to_mid, 0.0)
            dlf = _dot_sel(tri_t, dcum)
            df = dlf / f - dk
            dz = df * (1.0 - lb) * sz * sneg
            dlb = jnp.sum(df * sneg, axis=0, keepdims=True) * lb * (1.0 - lb)
            dtab_ref[0:1, sl] += dlb
            dtab_ref[1:2, sl] -= dlb
            sq = _sig(qr)
            dp_ref[:, sl] = (dq * sq * (1.0 + qr * (1.0 - sq))).astype(BF16)
            dp_ref[:, HGW + HD * h:HGW + HD * (h + 1)] = dz.astype(BF16)
            dp_ref[:, 2 * HGW + HD * h:2 * HGW + HD * (h + 1)] = dv.astype(BF16)
            dp_ref[:, 3 * HGW + HD * h:3 * HGW + HD * (h + 1)] = dgr.astype(BF16)
        dgain_ref[...] += dgain

    rev = lambda cb: pl.BlockSpec((HB, HGW), functools.partial(lambda n, cb: (nblk - 1 - n, cb), cb=cb))
    return pl.pallas_call(
        body, name="hgrn_bwd", grid=(nblk,),
        in_specs=[rev(0), rev(1), rev(2), rev(3), pl.BlockSpec((2, HGW), lambda n: (0, 0)),
                  pl.BlockSpec((1, HD), lambda n: (0, 0)), rev(0),
                  pl.BlockSpec((NH, 1, HD, HD), lambda n: (0, nblk - 1 - n, 0, 0)), rev(0)],
        out_specs=[pl.BlockSpec((HB, 4 * HGW), lambda n: (nblk - 1 - n, 0)),
                   pl.BlockSpec((2, HGW), lambda n: (0, 0)), pl.BlockSpec((1, HD), lambda n: (0, 0))],
        out_shape=[jax.ShapeDtypeStruct((t, 4 * HGW), BF16), jax.ShapeDtypeStruct((2, HGW), F32),
                   jax.ShapeDtypeStruct((1, HD), F32)],
        scratch_shapes=[pltpu.VMEM((NH, HD, HD), F32)],
        compiler_params=_params(("arbitrary",)),
    )(p_hg, p_hg, p_hg, p_hg, table, gain, o_pre, states, dy)


def _rope_mat():
    r = lax.broadcasted_iota(jnp.int32, (ROPE, ROPE), 0)
    c = lax.broadcasted_iota(jnp.int32, (ROPE, ROPE), 1)
    half = ROPE // 2
    return ((r == c - half).astype(F32) - (r == c + half).astype(F32)).astype(BF16)


def _mla_prep_fwd(p_mla, cs, sn, wq, wkv, gql, gkvl, gq, gk, t):
    tm = _pick(t, (512, 256, 128))

    def body(p_ref, cs_ref, sn_ref, wq_ref, wkv_ref, gql_ref, gkvl_ref, gq_ref, gk_ref,
             q_ref, k_ref, v_ref):
        rmat = _rope_mat()
        cqn = _rms(p_ref[:, 0:QL], gql_ref[...]).astype(BF16)
        ckvn = _rms(p_ref[:, QL:QL + KVL], gkvl_ref[...]).astype(BF16)
        kpe = p_ref[:, QL + KVL:QL + KVL + ROPE]
        c, s = cs_ref[...], sn_ref[...]
        rot = lambda x: x * c + _sel_dot(x, rmat) * s
        for h in range(NH):
            qa, qr = _dot(cqn, wq_ref[h, :, 0:HD]), _dot(cqn, wq_ref[h, :, HD:QK])
            rq = lax.rsqrt((jnp.sum(qa * qa, -1, keepdims=True) + jnp.sum(qr * qr, -1, keepdims=True)) / QK + EPS)
            q_ref[h, :, 0:HD] = (qa * rq * gq_ref[:, 0:HD] * (SCALE * LOG2E)).astype(BF16)
            q_ref[h, :, HD:QK] = (rot(qr * rq * gq_ref[:, HD:QK]) * (SCALE * LOG2E)).astype(BF16)
            kn = _dot(ckvn, wkv_ref[h, :, 0:HD])
            rk = lax.rsqrt((jnp.sum(kn * kn, -1, keepdims=True) + jnp.sum(kpe * kpe, -1, keepdims=True)) / QK + EPS)
            k_ref[h, :, 0:HD] = (kn * rk * gk_ref[:, 0:HD]).astype(BF16)
            k_ref[h, :, HD:QK] = rot(kpe * rk * gk_ref[:, HD:QK]).astype(BF16)
            v_ref[h] = _dot(ckvn, wkv_ref[h, :, HD:2 * HD]).astype(BF16)

    whole = lambda a: pl.BlockSpec(a.shape, functools.partial(lambda i, nd: (0,) * nd, nd=a.ndim))
    return pl.pallas_call(
        body, name="mla_prep_fwd", grid=(t // tm,),
        in_specs=[pl.BlockSpec((tm, QL + KVL + ROPE), lambda i: (i, 0)), pl.BlockSpec((tm, ROPE), lambda i: (i, 0)),
                  pl.BlockSpec((tm, ROPE), lambda i: (i, 0))] + [whole(a) for a in (wq, wkv, gql, gkvl, gq, gk)],
        out_specs=[pl.BlockSpec((NH, tm, QK), lambda i: (0, i, 0)), pl.BlockSpec((NH, tm, QK), lambda i: (0, i, 0)),
                   pl.BlockSpec((NH, tm, HD), lambda i: (0, i, 0))],
        out_shape=[jax.ShapeDtypeStruct((NH, t, QK), BF16), jax.ShapeDtypeStruct((NH, t, QK), BF16),
                   jax.ShapeDtypeStruct((NH, t, HD), BF16)],
        compiler_params=_params(("parallel",)),
    )(p_mla, cs, sn, wq, wkv, gql, gkvl, gq, gk)


def _mla_prep_bwd(p_mla, cs, sn, wq, wkv, gql, gkvl, gq, gk, dq, dk, dv, t):
    tm = _pick(t, (512, 256, 128))

    def body(p_ref, cs_ref, sn_ref, wq_ref, wkv_ref, gql_ref, gkvl_ref, gq_ref, gk_ref,
             dq_ref, dk_ref, dv_ref,
             dp_ref, dwq_ref, dwkv_ref, dgql_ref, dgkvl_ref, dgq_ref, dgk_ref):
        accs = (dwq_ref, dwkv_ref, dgql_ref, dgkvl_ref, dgq_ref, dgk_ref)

        @pl.when(pl.program_id(0) == 0)
        def _():
            for r in accs:
                r[...] = jnp.zeros_like(r)

        rmat = _rope_mat()
        rmat_t = -rmat
        cq, ckv = p_ref[:, 0:QL], p_ref[:, QL:QL + KVL]
        kpe = p_ref[:, QL + KVL:QL + KVL + ROPE]
        cqn_f, ckvn_f = _rms(cq, gql_ref[...]), _rms(ckv, gkvl_ref[...])
        cqn, ckvn = cqn_f.astype(BF16), ckvn_f.astype(BF16)
        cqn_t, ckvn_t = cqn_f.T.astype(BF16), ckvn_f.T.astype(BF16)
        c, s = cs_ref[...], sn_ref[...]
        unrot = lambda dy: dy * c + _sel_dot(dy * s, rmat_t)
        dcqn = jnp.zeros((tm, QL), F32)
        dckvn = jnp.zeros((tm, KVL), F32)
        dkpe = jnp.zeros((tm, ROPE), F32)
        dgq_a, dgq_r = jnp.zeros((1, HD), F32), jnp.zeros((1, ROPE), F32)
        dgk_a, dgk_r = jnp.zeros((1, HD), F32), jnp.zeros((1, ROPE), F32)
        for h in range(NH):
            qa, qr = _dot(cqn, wq_ref[h, :, 0:HD]), _dot(cqn, wq_ref[h, :, HD:QK])
            rq = lax.rsqrt((jnp.sum(qa * qa, -1, keepdims=True) + jnp.sum(qr * qr, -1, keepdims=True)) / QK + EPS)
            xa, xr = qa * rq, qr * rq
            dya = dq_ref[h, :, 0:HD] * SCALE
            dyr = unrot(dq_ref[h, :, HD:QK] * SCALE)
            dgq_a += jnp.sum(dya * xa, axis=0, keepdims=True)
            dgq_r += jnp.sum(dyr * xr, axis=0, keepdims=True)
            ga, gr_ = dya * gq_ref[:, 0:HD], dyr * gq_ref[:, HD:QK]
            mean = (jnp.sum(ga * xa, -1, keepdims=True) + jnp.sum(gr_ * xr, -1, keepdims=True)) / QK
            dqa = (rq * (ga - xa * mean)).astype(BF16)
            dqr = (rq * (gr_ - xr * mean)).astype(BF16)
            dwq_ref[h, :, 0:HD] += _dot(cqn_t, dqa)
            dwq_ref[h, :, HD:QK] += _dot(cqn_t, dqr)
            dcqn += _dot_nt(dqa, wq_ref[h, :, 0:HD]) + _dot_nt(dqr, wq_ref[h, :, HD:QK])
            kn = _dot(ckvn, wkv_ref[h, :, 0:HD])
            rk = lax.rsqrt((jnp.sum(kn * kn, -1, keepdims=True) + jnp.sum(kpe * kpe, -1, keepdims=True)) / QK + EPS)
            ya, yr = kn * rk, kpe * rk
            dka = dk_ref[h, :, 0:HD]
            dkr = unrot(dk_ref[h, :, HD:QK])
            dgk_a += jnp.sum(dka * ya, axis=0, keepdims=True)
            dgk_r += jnp.sum(dkr * yr, axis=0, keepdims=True)
            ha, hr = dka * gk_ref[:, 0:HD], dkr * gk_ref[:, HD:QK]
            mean = (jnp.sum(ha * ya, -1, keepdims=True) + jnp.sum(hr * yr, -1, keepdims=True)) / QK
            dkn = (rk * (ha - ya * mean)).astype(BF16)
            dkpe += rk * (hr - yr * mean)
            dvh = dv_ref[h].astype(BF16)
            dwkv_ref[h, :, 0:HD] += _dot(ckvn_t, dkn)
            dwkv_ref[h, :, HD:2 * HD] += _dot(ckvn_t, dvh)
            dckvn += _dot_nt(dkn, wkv_ref[h, :, 0:HD]) + _dot_nt(dvh, wkv_ref[h, :, HD:2 * HD])
        dcq, dg1 = _rms_bwd(cq, gql_ref[...], dcqn)
        dckv, dg2 = _rms_bwd(ckv, gkvl_ref[...], dckvn)
        dp_ref[:, 0:QL] = dcq.astype(BF16)
        dp_ref[:, QL:QL + KVL] = dckv.astype(BF16)
        dp_ref[:, QL + KVL:QL + KVL + ROPE] = dkpe.astype(BF16)
        dgql_ref[...] += dg1
        dgkvl_ref[...] += dg2
        dgq_ref[:, 0:HD] += dgq_a
        dgq_ref[:, HD:QK] += dgq_r
        dgk_ref[:, 0:HD] += dgk_a
        dgk_ref[:, HD:QK] += dgk_r

    whole = lambda a: pl.BlockSpec(a.shape, functools.partial(lambda i, nd: (0,) * nd, nd=a.ndim))
    acc_shapes = [wq.shape, wkv.shape, gql.shape, gkvl.shape, gq.shape, gk.shape]
    return pl.pallas_call(
        body, name="mla_prep_bwd", grid=(t // tm,),
        in_specs=[pl.BlockSpec((tm, QL + KVL + ROPE), lambda i: (i, 0)), pl.BlockSpec((tm, ROPE), lambda i: (i, 0)),
                  pl.BlockSpec((tm, ROPE), lambda i: (i, 0))]
        + [whole(a) for a in (wq, wkv, gql, gkvl, gq, gk)]
        + [pl.BlockSpec((NH, tm, QK), lambda i: (0, i, 0)), pl.BlockSpec((NH, tm, QK), lambda i: (0, i, 0)),
           pl.BlockSpec((NH, tm, HD), lambda i: (0, i, 0))],
        out_specs=[pl.BlockSpec((tm, QL + KVL + ROPE), lambda i: (i, 0))]
        + [pl.BlockSpec(s, functools.partial(lambda i, nd: (0,) * nd, nd=len(s))) for s in acc_shapes],
        out_shape=[jax.ShapeDtypeStruct((t, QL + KVL + ROPE), BF16)]
        + [jax.ShapeDtypeStruct(s, F32) for s in acc_shapes],
        compiler_params=_params(("arbitrary",)),
    )(p_mla, cs, sn, wq, wkv, gql, gkvl, gq, gk, dq, dk, dv)


def _chunk_mask(nq, nk, key0, keys_on_rows):
    shape = (nk, nq) if keys_on_rows else (nq, nk)
    qi = lax.broadcasted_iota(jnp.int32, shape, 1 if keys_on_rows else 0) // CHUNK
    ki = lax.broadcasted_iota(jnp.int32, shape, 0 if keys_on_rows else 1) // CHUNK + key0 // CHUNK
    return ki <= qi


def _flash_fwd(q, k, v, t):
    tq = _pick(t, ATT_Q)
    tk = tq // ATT_KEY_TILES

    def body(q_ref, k_ref, v_ref, o_ref, lse_ref):
        i = pl.program_id(1)
        qt = q_ref[0]

        def step(j, carry, key0):
            m, l, acc = carry
            cols = pl.ds(pl.multiple_of(j * tk, tk), tk)
            s = _dot_nt(qt, k_ref[0, cols, :])
            if key0 is not None:
                s = jnp.where(_chunk_mask(tq, tk, key0, False), s, -jnp.inf)
            m_new = jnp.maximum(m, jnp.max(s, axis=-1, keepdims=True))
            p = jnp.exp2(s - m_new)
            alpha = jnp.exp2(m - m_new)
            return m_new, alpha * l + jnp.sum(p, axis=-1, keepdims=True), alpha * acc + _dot(p, v_ref[0, cols, :])

        init = (jnp.full((tq, 1), -jnp.inf, F32), jnp.zeros((tq, 1), F32), jnp.zeros((tq, HD), F32))
        carry = lax.fori_loop(0, ATT_KEY_TILES * i, lambda j, cr: step(j, cr, None), init)
        for h in range(ATT_KEY_TILES):
            carry = step(ATT_KEY_TILES * i + h, carry, h * tk)
        m, l, acc = carry
        o_ref[...] = (acc / l).astype(BF16)
        lse_ref[0] = m + jnp.log2(l)

    return pl.pallas_call(
        body, name="flash_fwd", grid=(NH, t // tq),
        in_specs=[pl.BlockSpec((1, tq, QK), lambda h, i: (h, i, 0)), pl.BlockSpec((1, t, QK), lambda h, i: (h, 0, 0)),
                  pl.BlockSpec((1, t, HD), lambda h, i: (h, 0, 0))],
        out_specs=[pl.BlockSpec((tq, HD), lambda h, i: (i, h)), pl.BlockSpec((1, tq, 1), lambda h, i: (h, i, 0))],
        out_shape=[jax.ShapeDtypeStruct((t, NH * HD), BF16), jax.ShapeDtypeStruct((NH, t, 1), F32)],
        compiler_params=_params(("parallel", "parallel")),
    )(q, k, v)


def _attn_out_bwd(dy, w, o, t):
    tm = _pick(t, ATT_TILES)

    def body(dy_ref, w_ref, o_ref, do_ref, delta_ref):
        do = _dot_nt(dy_ref[...], w_ref[...]).astype(BF16)
        do_ref[...] = do
        ones = jnp.ones((SUBLANE, HD), BF16)
        rowsum = lambda p: lax.dot_general(ones, p, (((1,), (1,)), ((), ())), preferred_element_type=F32)
        for h in range(NH):
            sl = slice(HD * h, HD * (h + 1))
            x1, x2, x3 = _split3(do[:, sl].astype(F32) * o_ref[:, sl].astype(F32))
            delta_ref[h] = (rowsum(x1) + rowsum(x2) + rowsum(x3))[0:1, :]

    return pl.pallas_call(
        body, name="attn_out_bwd", grid=(t // tm,),
        in_specs=[_BS((tm, dy.shape[1]), lambda i: (i, 0)), _BS(w.shape, lambda i: (0, 0)),
                  _BS((tm, NH * HD), lambda i: (i, 0))],
        out_specs=[_BS((tm, NH * HD), lambda i: (i, 0)), _BS((NH, 1, tm), lambda i: (0, 0, i))],
        out_shape=[jax.ShapeDtypeStruct((t, NH * HD), BF16), jax.ShapeDtypeStruct((NH, 1, t), F32)],
        compiler_params=_params(("parallel",)),
    )(dy, w, o)


def _flash_bwd(q, k, v, do, lse_row, delta_row, t):
    tq = _pick(t, ATT_Q)
    tk = tq // ATT_KEY_TILES

    def body(q_ref, k_ref, v_ref, do_ref, lse_ref, delta_ref, dq_ref, dk_ref, dv_ref):
        j = pl.program_id(1)

        @pl.when(j == 0)
        def _():
            dq_ref[...] = jnp.zeros_like(dq_ref)

        kt, vt = k_ref[0], v_ref[0]

        def step(i, carry, key0):
            dk, dv = carry
            rows = pl.ds(pl.multiple_of(i * tq, tq), tq)
            qt, dot_ = q_ref[0, rows, :], do_ref[rows, :]
            p = jnp.exp2(_dot_nt(kt, qt) - lse_ref[0, :, rows])
            if key0 is not None:
                p = jnp.where(_chunk_mask(tq, tk, key0, True), p, 0.0)
            ds = (p * (_dot_nt(vt, dot_) - delta_ref[0, :, rows])).astype(BF16)
            dq_ref[0, rows, :] += _dot_tn(ds, kt)
            return dk + _dot(ds, qt), dv + _dot(p, dot_)

        own = j // ATT_KEY_TILES
        carry = step(own, (jnp.zeros((tk, QK), F32), jnp.zeros((tk, HD), F32)), (j % ATT_KEY_TILES) * tk)
        dk, dv = lax.fori_loop(own + 1, t // tq, lambda i, cr: step(i, cr, None), carry)
        dk_ref[0] = dk * (1.0 / LOG2E)
        dv_ref[0] = dv

    return pl.pallas_call(
        body, name="flash_bwd", grid=(NH, t // tk),
        in_specs=[pl.BlockSpec((1, t, QK), lambda h, j: (h, 0, 0)), pl.BlockSpec((1, tk, QK), lambda h, j: (h, j, 0)),
                  pl.BlockSpec((1, tk, HD), lambda h, j: (h, j, 0)), pl.BlockSpec((t, HD), lambda h, j: (0, h)),
                  pl.BlockSpec((1, 1, t), lambda h, j: (h, 0, 0)), pl.BlockSpec((1, 1, t), lambda h, j: (h, 0, 0))],
        out_specs=[pl.BlockSpec((1, t, QK), lambda h, j: (h, 0, 0)), pl.BlockSpec((1, tk, QK), lambda h, j: (h, j, 0)),
                   pl.BlockSpec((1, tk, HD), lambda h, j: (h, j, 0))],
        out_shape=[jax.ShapeDtypeStruct((NH, t, QK), F32), jax.ShapeDtypeStruct((NH, t, QK), F32),
                   jax.ShapeDtypeStruct((NH, t, HD), F32)],
        compiler_params=_params(("parallel", "arbitrary")),
    )(q, k, v, do, lse_row, delta_row)


def _ffn_in(xn, w_in, name):
    t, k = xn.shape
    s, _, n = w_in.shape
    tm = _pick(t, ROW_TILES)

    def body(x_ref, w_ref, gu_ref, a_ref):
        x = x_ref[...]
        g, u = _dot(x, w_ref[0]), _dot(x, w_ref[1])
        gu_ref[0] = g.astype(BF16)
        gu_ref[1] = u.astype(BF16)
        a_ref[...] = (g * _sig(g) * u).astype(BF16)

    return pl.pallas_call(
        body, name=name, grid=(t // tm, s // 2),
        in_specs=[_BS((tm, k), lambda i, j: (i, 0)), _BS((2, None, k, n), lambda i, j: (0, j, 0, 0))],
        out_specs=[_BS((2, None, tm, n), lambda i, j: (0, j, i, 0)), _BS((None, tm, n), lambda i, j: (j, i, 0))],
        out_shape=[jax.ShapeDtypeStruct((2, s // 2, t, n), BF16), jax.ShapeDtypeStruct((s // 2, t, n), BF16)],
        compiler_params=_params(("parallel", "parallel")),
    )(xn, w_in.reshape(2, s // 2, k, n))


def _ffn_dgu(dfo, w_out, gu, name):
    t, k = dfo.shape
    s, n, _ = w_out.shape
    tm = _pick(t, ROW_TILES)

    def body(d_ref, w_ref, gu_ref, o_ref):
        da = _dot_nt(d_ref[...], w_ref[...])
        g, u = gu_ref[0].astype(F32), gu_ref[1].astype(F32)
        sg = _sig(g)
        o_ref[0] = (da * u * sg * (1.0 + g * (1.0 - sg))).astype(BF16)
        o_ref[1] = (da * g * sg).astype(BF16)

    pair = _BS((2, None, tm, n), lambda i, j: (0, j, i, 0))
    return pl.pallas_call(
        body, name=name, grid=(t // tm, s),
        in_specs=[_BS((tm, k), lambda i, j: (i, 0)), _BS((None, n, k), lambda i, j: (j, 0, 0)), pair],
        out_specs=pair, out_shape=jax.ShapeDtypeStruct((2, s, t, n), BF16),
        compiler_params=_params(("parallel", "parallel")),
    )(dfo, w_out, gu).reshape(2 * s, t, n)


def _ffn_fwd(xn, w_in, w_out, tag, epilogue):
    gu, a = _ffn_in(xn, w_in, tag + "_in")
    return gu, a, _mm_stack_red(a, w_out, tag + "_out", "nn", epilogue)


def _ffn_bwd(dfo, xn, gu, a, w_in, w_out, tag, epilogue):
    dw_out = _mm_stack_tn(a, dfo, tag + "_dwout")
    dgu = _ffn_dgu(dfo, w_out, gu, tag + "_dgu")
    return _mm_stack_red(dgu, w_in, tag + "_dxn", "nt", epilogue), _mm_stack_tn(xn, dgu, tag + "_dwin"), dw_out


def _local_step(x, target, cs, sn, w, late_weights, s, early_grads):
    t = x.shape[0]
    tm = _pick(t, (256, 128))
    g = {}
    ffn_out = lambda n: w[n].reshape(4, FF // 4, D)

    def norm_fn(x_ref, g_ref, o_ref):
        o_ref[...] = _rms(x_ref[...], g_ref[...]).astype(BF16)

    xn1 = _rows(norm_fn, "norm1", t, tm, [(x, D, 0)], [s["ffn1_norm"]], [(D, BF16)])[0]
    def res_norm_fn(scale):
        def fn(h_ref, f_ref, g_ref, h_out, n_out):
            h = h_ref[...] + scale * f_ref[...]
            h_out[...] = h
            n_out[...] = _rms(h, g_ref[...]).astype(BF16)
        return fn

    gu1, a1, (h1, u) = _ffn_fwd(xn1, w["ffn1_w_in"], ffn_out("ffn1_w_out"), "ffn1",
                                (res_norm_fn(0.5), x, [], [s["mix_norm"]], [(D, F32), (D, BF16)], []))
    w = {**w, **late_weights(h1)}
    rows_of = lambda n: w[n].reshape(-1, w[n].shape[2])
    w_hgb, w_mlab, w_o = rows_of("w_hg_branch"), rows_of("w_mla_branch"), rows_of("w_out")
    w_in_nat = w["w_in"].transpose(1, 0, 2).reshape(D, -1)
    w_mrg = w["w_merge"]
    mw = w_mrg.shape[2]
    w_in_hg, w_in_mla = w_in_nat[:, :4 * HGW], w_in_nat[:, 4 * HGW:]
    p_hg = _mm(u, w_in_hg, "nn", "proj_hg")
    p_mla = _mm(u, w_in_mla, "nn", "proj_mla")
    gpre = _cols_fwd(u, w_mrg, "proj_gate")
    o_pre, hgy, states = _hgrn_fwd(p_hg, s["hg_lb_table"], s["hg_out_norm"], t)
    prep_args = (p_mla, cs, sn, w["w_q_up"], w["w_kv_up"], s["mla_q_lora_norm"], s["mla_kv_lora_norm"],
                 s["q_head_norm"], s["k_head_norm"])
    q, k, v = _mla_prep_fwd(*prep_args, t)
    att, lse = _flash_fwd(q, k, v, t)
    y_hg = _mm(hgy, w_hgb, "nn", "branch_hg")
    y_mla = _mm(att, w_mlab, "nn", "branch_mla")

    def mix_fn(gh_ref, gm_ref, yh_ref, ym_ref, b_ref, o_ref):
        gh = _sig(gh_ref[...] + b_ref[:, 0:D])
        gm = _sig(gm_ref[...] + b_ref[:, D:2 * D])
        o_ref[...] = (gh * yh_ref[...] + gm * ym_ref[...]).astype(BF16)

    mixed = _rows(mix_fn, "mix", t, tm, [(gpre, D, 0), (gpre, D, 1), (y_hg, D, 0), (y_mla, D, 0)], [s["b_merge"]],
                  [(D, BF16)])[0]
    h2, xn2 = _mm_stack_red(mixed[None], w_o[None], "mix_out", "nn",
                            (res_norm_fn(1.0), h1, [], [s["ffn2_norm"]], [(D, F32), (D, BF16)], []))

    def loss_fn(h_ref, f_ref, tg_ref, g_ref, dh_out, dhb_out):
        h = h_ref[...] + 0.5 * f_ref[...]
        e = _rms(h, g_ref[...]) - tg_ref[...]
        dh, dgain = _rms_bwd(h, g_ref[...], e / D)
        dh_out[...] = dh
        dhb_out[...] = (0.5 * dh).astype(BF16)
        return dgain, jnp.full((1, LANE), 0.5 / D * jnp.sum(e * e), F32)

    gu2, a2, (dh3, dfo2, g["final_norm"], loss) = _ffn_fwd(
        xn2, w["ffn2_w_in"], ffn_out("ffn2_w_out"), "ffn2",
        (loss_fn, h2, [target], [s["final_norm"]], [(D, F32), (D, BF16)], [(1, D), (1, LANE)]))

    def norm_bwd_fn(scale):
        def fn(h_ref, dxn_ref, dh_ref, g_ref, dh_out, dhb_out):
            dx, dgain = _rms_bwd(h_ref[...], g_ref[...], dxn_ref[...])
            dh = dh_ref[...] + dx
            dh_out[...] = dh
            dhb_out[...] = (scale * dh).astype(BF16)
            return (dgain,)
        return fn

    as_rows = lambda a: a.reshape((N_DEV, -1) + a.shape[-1:])
    (dh2, dh2b, g["ffn2_norm"]), g["ffn2_w_in"], dwo = _ffn_bwd(
        dfo2, xn2, gu2, a2, w["ffn2_w_in"], ffn_out("ffn2_w_out"), "ffn2",
        (norm_bwd_fn(1.0), h2, [dh3], [s["ffn2_norm"]], [(D, F32), (D, BF16)], [(1, D)]))
    g["ffn2_w_out"] = as_rows(dwo)
    dmixed = _mm(dh2b, w_o, "nt", "mix_out_dx")
    g["w_out"] = as_rows(_mm(mixed, dh2b, "tn", "mix_out_dw"))

    def mix_bwd_fn(gh_ref, gm_ref, yh_ref, ym_ref, dm_ref, b_ref, dyh_out, dym_out, dg_out):
        gh = _sig(gh_ref[...] + b_ref[:, 0:D])
        gm = _sig(gm_ref[...] + b_ref[:, D:2 * D])
        dm = dm_ref[...]
        dyh_out[...] = (dm * gh).astype(BF16)
        dym_out[...] = (dm * gm).astype(BF16)
        dgh = dm * yh_ref[...] * gh * (1.0 - gh)
        dgm = dm * ym_ref[...] * gm * (1.0 - gm)
        dg_out[:, 0:D] = dgh.astype(BF16)
        dg_out[:, D:2 * D] = dgm.astype(BF16)
        return (jnp.concatenate([jnp.sum(dgh, axis=0, keepdims=True), jnp.sum(dgm, axis=0, keepdims=True)], axis=1),)

    dyh, dym, dgpre, g["b_merge"] = _rows(
        mix_bwd_fn, "mix_bwd", t, tm, [(gpre, D, 0), (gpre, D, 1), (y_hg, D, 0), (y_mla, D, 0), (dmixed, D, 0)],
        [s["b_merge"]], [(D, BF16), (D, BF16), (2 * D, BF16)], [(1, 2 * D)])
    g["w_hg_branch"] = as_rows(_mm(hgy, dyh, "tn", "branch_hg_dw"))
    g["w_mla_branch"] = as_rows(_mm(att, dym, "tn", "branch_mla_dw"))
    g["w_merge"] = _cols_dw(u, dgpre, mw, "proj_gate_dw")
    dhgy = _mm(dyh, w_hgb, "nt", "branch_hg_dx")
    datt, delta = _attn_out_bwd(dym, w_mlab, att, t)
    du_gate = _cols_dx(dgpre, w_mrg, "proj_gate_dx")

    dq, dk, dv = _flash_bwd(q, k, v, datt, lse.reshape(NH, 1, t), delta, t)
    (dp_mla, g["w_q_up"], g["w_kv_up"], g["mla_q_lora_norm"], g["mla_kv_lora_norm"], g["q_head_norm"],
     g["k_head_norm"]) = _mla_prep_bwd(*prep_args, dq, dk, dv, t)
    dp_hg, g["hg_lb_table"], g["hg_out_norm"] = _hgrn_bwd(p_hg, s["hg_lb_table"], s["hg_out_norm"], o_pre, states,
                                                          dhgy, t)
    dw_in_nat = jnp.concatenate([_mm(u, dp_hg, "tn", "proj_hg_dw"), _mm(u, dp_mla, "tn", "proj_mla_dw")], axis=1)
    g["w_in"] = dw_in_nat.reshape(D, N_DEV, -1).transpose(1, 0, 2)
    du_hg = _mm(dp_hg, w_in_hg, "nt", "proj_hg_dx")
    du_mla = _mm(dp_mla, w_in_mla, "nt", "proj_mla_dx")

    def mixnorm_bwd_fn(h_ref, a_ref, b_ref, c_ref, dh_ref, g_ref, dh_out, dhb_out):
        dx, dgain = _rms_bwd(h_ref[...], g_ref[...], a_ref[...] + b_ref[...] + c_ref[...])
        dh = dh_ref[...] + dx
        dh_out[...] = dh
        dhb_out[...] = (0.5 * dh).astype(BF16)
        return (dgain,)

    mix_gain = s["mix_norm"] + early_grads(g)[0:1, 0:1]
    dh1, dfo1, g["mix_norm"] = _rows(mixnorm_bwd_fn, "mixnorm_bwd", t, tm,
                                     [(h1, D, 0), (du_hg, D, 0), (du_mla, D, 0), (du_gate, D, 0), (dh2, D, 0)],
                                     [mix_gain], [(D, F32), (D, BF16)], [(1, D)])
    (grad_x, _, g["ffn1_norm"]), g["ffn1_w_in"], dwo = _ffn_bwd(
        dfo1, xn1, gu1, a1, w["ffn1_w_in"], ffn_out("ffn1_w_out"), "ffn1",
        (norm_bwd_fn(1.0), x, [dh1], [s["ffn1_norm"]], [(D, F32), (D, BF16)], [(1, D)]))
    g["ffn1_w_out"] = as_rows(dwo)
    return loss, grad_x, g


def _coords():
    return lax.axis_index("x"), lax.axis_index("y"), lax.axis_index("c")


def _hbm_call(body, name, ins, out_shapes, scratch):
    any_spec = pl.BlockSpec(memory_space=pl.ANY)
    return pl.pallas_call(
        body, name=name, out_shape=[jax.ShapeDtypeStruct(s, dt) for s, dt in out_shapes],
        in_specs=[any_spec] * len(ins), out_specs=[any_spec] * len(out_shapes), scratch_shapes=scratch,
    )(*ins)


def _my_slot():
    return 4 * lax.axis_index("x") + 2 * lax.axis_index("y") + lax.axis_index("c")


def _put_own(buf, own, index):
    return lax.dynamic_update_index_in_dim(buf, own, index, 0)


def _all_gather(blocks, name):
    nb = len(blocks)

    def body(*refs):
        x_refs, out_refs = refs[:nb], refs[nb:2 * nb]
        send_sems, recv_sems = refs[2 * nb:]
        x, y, c = _coords()
        me, sibling = (x, y, c), (x, y, 1 - c)
        chips = [(1 - x, y), (x, 1 - y), (1 - x, 1 - y)]

        def slot(b, px, py, pc):
            return out_refs[b].at[4 * px + 2 * py + pc]

        def copy(b, kk, block_of, to, src=None):
            return pltpu.make_async_remote_copy(
                src_ref=slot(b, *block_of) if src is None else src, dst_ref=slot(b, *block_of),
                send_sem=send_sems.at[b, kk], recv_sem=recv_sems.at[b, kk], device_id=to, device_id_type=MESH)

        first = [copy(b, 0, me, sibling, src=x_refs[b]) for b in range(nb)]
        first += [copy(b, 1 + j, me, (*chip, c), src=x_refs[b]) for j, chip in enumerate(chips) for b in range(nb)]
        for cp in first:
            cp.start()
        passed = []
        for j, chip in enumerate(chips):
            for b in range(nb):
                copy(b, 1 + j, (*chip, c), me).wait_recv()
                passed.append(copy(b, 4 + j, (*chip, c), sibling))
                passed[-1].start()
        for b in range(nb):
            copy(b, 0, sibling, me).wait_recv()
        for j, chip in enumerate(chips):
            for b in range(nb):
                copy(b, 4 + j, (*chip, 1 - c), me).wait_recv()
        for cp in first + passed:
            cp.wait_send()

    outs = _hbm_call(body, name, blocks, [((N_DEV,) + b.shape, b.dtype) for b in blocks],
                     [pltpu.SemaphoreType.DMA((nb, 7)), pltpu.SemaphoreType.DMA((nb, 7))])
    return [_put_own(o, b[None], _my_slot()) for o, b in zip(outs, blocks)]


def _gather_peers():
    x, y, c = _coords()
    return (x, y, c), [(x, y, 1 - c), (1 - x, y, c), (x, 1 - y, c), (1 - x, 1 - y, c)]


def _gather_start(blocks, after, name):
    nb = len(blocks)
    hbm, sem = pl.BlockSpec(memory_space=pltpu.HBM), pl.BlockSpec(memory_space=pltpu.SEMAPHORE)

    def body(*refs):
        x_refs, out_refs = refs[:nb], refs[nb:2 * nb]
        send_sems, recv_sems, token = refs[2 * nb + 1], refs[2 * nb + 2], refs[-1]
        (x, y, c), peers = _gather_peers()
        for kk, peer in enumerate(peers):
            for b in range(nb):
                pltpu.make_async_remote_copy(
                    src_ref=x_refs[b], dst_ref=out_refs[b].at[4 * x + 2 * y + c], send_sem=send_sems.at[4 * b + kk],
                    recv_sem=recv_sems.at[4 * b + kk], device_id=peer, device_id_type=MESH).start()
        token[...] = jnp.zeros_like(token)

    gathers = [pltpu.with_memory_space_constraint(lax.empty((N_DEV,) + b.shape, b.dtype), pltpu.HBM) for b in blocks]
    outs = pl.pallas_call(
        body, name=name,
        out_shape=(pltpu.SemaphoreType.DMA((4 * nb,)), pltpu.SemaphoreType.DMA((4 * nb,)),
                   *[pltpu.HBM(b.shape, b.dtype) for b in blocks], *[pltpu.HBM(b.shape, b.dtype) for b in gathers],
                   jax.ShapeDtypeStruct((SUBLANE, LANE), F32)),
        in_specs=[hbm] * (2 * nb) + [pl.BlockSpec(memory_space=pl.ANY)],
        out_specs=(sem, sem, *[hbm] * (2 * nb), pl.BlockSpec(memory_space=pltpu.VMEM)),
        input_output_aliases={i: 2 + i for i in range(2 * nb)},
        compiler_params=pltpu.CompilerParams(has_side_effects=pltpu.SideEffectType.DATAFLOW_SIDE_EFFECTING),
    )(*[pltpu.with_memory_space_constraint(b, pltpu.HBM) for b in blocks], *gathers, after)
    return outs[0], outs[1], list(outs[2:2 + nb]), list(outs[2 + nb:2 + 2 * nb]), outs[-1]


def _gather_wait(send_sems, recv_sems, thru, gathers, after, name):
    nb = len(thru)
    hbm, sem = pl.BlockSpec(memory_space=pltpu.HBM), pl.BlockSpec(memory_space=pltpu.SEMAPHORE)

    def body(*refs):
        x_refs, out_refs = refs[:nb], refs[nb:2 * nb]
        send_sems_, recv_sems_ = refs[2 * nb], refs[2 * nb + 1]
        _, peers = _gather_peers()
        for kk, (px, py, pc) in enumerate(peers):
            for b in range(nb):
                cp = pltpu.make_async_remote_copy(
                    src_ref=x_refs[b], dst_ref=out_refs[b].at[4 * px + 2 * py + pc], send_sem=send_sems_.at[4 * b + kk],
                    recv_sem=recv_sems_.at[4 * b + kk], device_id=(px, py, pc), device_id_type=MESH)
                cp.wait_send()
                cp.wait_recv()

    outs = pl.pallas_call(
        body, name=name,
        out_shape=(*[pltpu.HBM(b.shape, b.dtype) for b in thru], *[pltpu.HBM(b.shape, b.dtype) for b in gathers]),
        in_specs=[hbm] * (2 * nb) + [sem, sem, pl.BlockSpec(memory_space=pl.ANY)], out_specs=[hbm] * (2 * nb),
        input_output_aliases={i: i for i in range(2 * nb)},
        compiler_params=pltpu.CompilerParams(has_side_effects=pltpu.SideEffectType.DATAFLOW_SIDE_EFFECTING),
    )(*thru, *gathers, send_sems, recv_sems, after)
    return list(outs[:nb]), list(outs[nb:])


def _gather_finish(blocks, gathers, name):
    nb = len(blocks)

    def body(*refs):
        x_refs, in_refs, out_refs = refs[:nb], refs[nb:2 * nb], refs[2 * nb:3 * nb]
        send_sems, recv_sems = refs[3 * nb:]
        (x, y, c), peers = _gather_peers()
        copies = []
        for j, (px, py, _) in enumerate(peers[1:]):
            for b in range(nb):
                copies.append(pltpu.make_async_remote_copy(
                    src_ref=in_refs[b].at[4 * px + 2 * py + c], dst_ref=out_refs[b].at[4 * px + 2 * py + c],
                    send_sem=send_sems.at[b, j], recv_sem=recv_sems.at[b, j], device_id=(x, y, 1 - c),
                    device_id_type=MESH))
                copies[-1].start()
        for j, (px, py, _) in enumerate(peers[1:]):
            for b in range(nb):
                pltpu.make_async_remote_copy(
                    src_ref=in_refs[b].at[4 * px + 2 * py + c], dst_ref=out_refs[b].at[4 * px + 2 * py + 1 - c],
                    send_sem=send_sems.at[b, j], recv_sem=recv_sems.at[b, j], device_id=(x, y, 1 - c),
                    device_id_type=MESH).wait_recv()
        for cp in copies:
            cp.wait_send()

    any_spec = pl.BlockSpec(memory_space=pl.ANY)
    outs = pl.pallas_call(
        body, name=name, out_shape=[jax.ShapeDtypeStruct(b.shape, b.dtype) for b in gathers],
        in_specs=[any_spec] * (2 * nb), out_specs=[any_spec] * nb,
        input_output_aliases={nb + i: i for i in range(nb)},
        scratch_shapes=[pltpu.SemaphoreType.DMA((nb, 3)), pltpu.SemaphoreType.DMA((nb, 3))],
    )(*blocks, *gathers)
    return [_put_own(o, b[None], _my_slot()) for o, b in zip(outs, blocks)]


def _sibling_swap(bufs, name):
    nb = len(bufs)

    def body(*refs):
        x_refs, out_refs = refs[:nb], refs[nb:2 * nb]
        send_sems, recv_sems = refs[2 * nb:]
        x, y, c = _coords()
        copies = [pltpu.make_async_remote_copy(
            src_ref=x_refs[b].at[2 * q + 1 - c], dst_ref=out_refs[b].at[q], send_sem=send_sems.at[b, q],
            recv_sem=recv_sems.at[b, q], device_id=(x, y, 1 - c), device_id_type=MESH)
            for b in range(nb) for q in range(4)]
        for cp in copies:
            cp.start()
        for cp in copies:
            cp.wait()

    return _hbm_call(body, name, bufs, [((4,) + b.shape[1:], b.dtype) for b in bufs],
                     [pltpu.SemaphoreType.DMA((nb, 4)), pltpu.SemaphoreType.DMA((nb, 4))])


def _chip_exchange_start(bufs, name):
    nb = len(bufs)
    hbm, sem = pl.BlockSpec(memory_space=pltpu.HBM), pl.BlockSpec(memory_space=pltpu.SEMAPHORE)

    def body(*refs):
        x_refs, land_refs = refs[:nb], refs[nb:2 * nb]
        send_sems, recv_sems, token = refs[2 * nb], refs[2 * nb + 1], refs[-1]
        x, y, c = _coords()
        for j, (px, py) in enumerate([(1 - x, y), (x, 1 - y), (1 - x, 1 - y)]):
            for b in range(nb):
                pltpu.make_async_remote_copy(
                    src_ref=x_refs[b].at[2 * px + py], dst_ref=land_refs[b].at[2 * x + y], send_sem=send_sems.at[3 * b + j],
                    recv_sem=recv_sems.at[3 * b + j], device_id=(px, py, c), device_id_type=MESH).start()
        token[...] = jnp.zeros_like(token)

    lands = [pltpu.with_memory_space_constraint(lax.empty(b.shape, b.dtype), pltpu.HBM) for b in bufs]
    outs = pl.pallas_call(
        body, name=name,
        out_shape=(pltpu.SemaphoreType.DMA((3 * nb,)), pltpu.SemaphoreType.DMA((3 * nb,)),
                   *[pltpu.HBM(b.shape, b.dtype) for b in bufs], *[pltpu.HBM(b.shape, b.dtype) for b in bufs],
                   jax.ShapeDtypeStruct((SUBLANE, LANE), F32)),
        in_specs=[hbm] * (2 * nb), out_specs=(sem, sem, *[hbm] * (2 * nb), pl.BlockSpec(memory_space=pltpu.VMEM)),
        input_output_aliases={i: 2 + i for i in range(2 * nb)},
        compiler_params=pltpu.CompilerParams(has_side_effects=pltpu.SideEffectType.DATAFLOW_SIDE_EFFECTING),
    )(*[pltpu.with_memory_space_constraint(b, pltpu.HBM) for b in bufs], *lands)
    return outs[0], outs[1], list(outs[2:2 + nb]), list(outs[2 + nb:2 + 2 * nb]), outs[-1]


def _chip_exchange_wait(send_sems, recv_sems, thru, lands, after, name):
    nb = len(thru)
    hbm, sem = pl.BlockSpec(memory_space=pltpu.HBM), pl.BlockSpec(memory_space=pltpu.SEMAPHORE)

    def body(*refs):
        x_refs, land_refs = refs[:nb], refs[nb:2 * nb]
        send_sems_, recv_sems_ = refs[2 * nb], refs[2 * nb + 1]
        x, y, c = _coords()
        for j, (px, py) in enumerate([(1 - x, y), (x, 1 - y), (1 - x, 1 - y)]):
            for b in range(nb):
                cp = pltpu.make_async_remote_copy(
                    src_ref=x_refs[b].at[2 * px + py], dst_ref=land_refs[b].at[2 * px + py],
                    send_sem=send_sems_.at[3 * b + j], recv_sem=recv_sems_.at[3 * b + j], device_id=(px, py, c),
                    device_id_type=MESH)
                cp.wait_send()
                cp.wait_recv()

    outs = pl.pallas_call(
        body, name=name,
        out_shape=(*[pltpu.HBM(b.shape, b.dtype) for b in thru], *[pltpu.HBM(b.shape, b.dtype) for b in lands]),
        in_specs=[hbm] * (2 * nb) + [sem, sem, pl.BlockSpec(memory_space=pl.ANY)], out_specs=[hbm] * (2 * nb),
        input_output_aliases={i: i for i in range(2 * nb)},
        compiler_params=pltpu.CompilerParams(has_side_effects=pltpu.SideEffectType.DATAFLOW_SIDE_EFFECTING),
    )(*thru, *lands, send_sems, recv_sems, after)
    return list(outs[:nb]), list(outs[nb:])


def _chip_sum(g, r1, c, name):
    _, r, cw = g.shape
    tr = _pick(r, (256, 176, 128))

    def body(c_ref, g_ref, r_ref, o_ref):
        o_ref[...] = (g_ref[...] + r_ref[...]).astype(BF16)

    grid_spec = pltpu.PrefetchScalarGridSpec(
        num_scalar_prefetch=1, grid=(4, r // tr),
        in_specs=[_BS((None, None, tr, cw), lambda q, i, c_ref: (q, c_ref[0], i, 0)),
                  _BS((None, tr, cw), lambda q, i, c_ref: (q, i, 0))],
        out_specs=_BS((None, tr, cw), lambda q, i, c_ref: (q, i, 0)))
    return pl.pallas_call(
        body, name=name, grid_spec=grid_spec, out_shape=jax.ShapeDtypeStruct((4, r, cw), BF16),
        compiler_params=_params(("parallel", "parallel")),
    )(c.reshape(1).astype(jnp.int32), g.reshape(4, 2, r, cw), r1)


def _adamw_math(w, g, m, v):
    m = B1 * m + (1.0 - B1) * g
    v = B2 * v + (1.0 - B2) * (g * g)
    m_hat = m / (1.0 - B1 ** STEP)
    v_hat = v / (1.0 - B2 ** STEP)
    return -LR * (m_hat / (jnp.sqrt(v_hat) + AEPS) + WD * w), m, v


def _sum_adamw(parts, w, m, v, name):
    r, c = w.shape
    tr = _pick(r, (256, 176, 128))

    def body(p0, p1, p2, p3, w_ref, m_ref, v_ref, g_out, d_out, m_out, v_out):
        g = ((p0[...].astype(F32) + p1[...].astype(F32)) + p2[...].astype(F32)) + p3[...].astype(F32)
        g_out[...] = g
        d_out[...], m_out[...], v_out[...] = _adamw_math(w_ref[...], g, m_ref[...], v_ref[...])

    part = lambda q: _BS((None, tr, c), functools.partial(lambda i, q: (q, i, 0), q=q))
    plain = _BS((tr, c), lambda i: (i, 0))
    return pl.pallas_call(
        body, name=name, grid=(r // tr,), in_specs=[part(q) for q in range(4)] + [plain] * 3,
        out_specs=[plain] * 4, out_shape=[jax.ShapeDtypeStruct((r, c), F32)] * 4,
        compiler_params=_params(("parallel",)),
    )(parts, parts, parts, parts, w, m, v)


def _small_update(gathered, w, m, v):
    r = w.shape[0]

    def body(ga_ref, w_ref, m_ref, v_ref, g_out, d_out, m_out, v_out):
        g = ga_ref[0]
        for dev in range(1, N_DEV):
            g = g + ga_ref[dev]
        g_out[...] = g
        d_out[...], m_out[...], v_out[...] = _adamw_math(w_ref[...], g, m_ref[...], v_ref[...])

    return pl.pallas_call(
        body, name="small_update", out_shape=[jax.ShapeDtypeStruct((r, LANE), F32)] * 4,
    )(gathered, w, m, v)


def _pack_small(vals):
    rows = []
    for name, (r, n) in SMALL:
        flat = vals[name].reshape(-1)
        pad = (-flat.shape[0]) % (SUBLANE * LANE)
        rows.append(jnp.pad(flat, (0, pad)).reshape(-1, LANE))
    return jnp.concatenate(rows, axis=0)


def _unpack_small(packed):
    out, off = {}, 0
    for name, (r, n) in SMALL:
        nrow = -(-(r * n) // (SUBLANE * LANE)) * SUBLANE
        out[name] = packed[off:off + nrow].reshape(-1)[:r * n].reshape(r, n)
        off += nrow
    return out


def kernel(x, positions, ffn1_norm, ffn1_w_in, ffn1_w_out, mix_norm, w_in, hg_lb_table, hg_out_norm, w_hg_branch, mla_q_lora_norm, w_q_up, mla_kv_lora_norm, w_kv_up, q_head_norm, k_head_norm, w_mla_branch, w_merge, b_merge, w_out, ffn2_norm, ffn2_w_in, ffn2_w_out, final_norm, loss_target, m_ffn1_norm, m_ffn1_w_in, m_ffn1_w_out, m_mix_norm, m_w_in, m_hg_lb_table, m_hg_out_norm, m_w_hg_branch, m_mla_q_lora_norm, m_w_q_up, m_mla_kv_lora_norm, m_w_kv_up, m_q_head_norm, m_k_head_norm, m_w_mla_branch, m_w_merge, m_b_merge, m_w_out, m_ffn2_norm, m_ffn2_w_in, m_ffn2_w_out, m_final_norm, v_ffn1_norm, v_ffn1_w_in, v_ffn1_w_out, v_mix_norm, v_w_in, v_hg_lb_table, v_hg_out_norm, v_w_hg_branch, v_mla_q_lora_norm, v_w_q_up, v_mla_kv_lora_norm, v_w_kv_up, v_q_head_norm, v_k_head_norm, v_w_mla_branch, v_w_merge, v_b_merge, v_w_out, v_ffn2_norm, v_ffn2_w_in, v_ffn2_w_out, v_final_norm):
    args = dict(locals())
    t = x.shape[1]
    big_w = {n: args[n][0] for n, _, _ in BIG}
    small = {n: args[n].reshape(shape) for n, shape in SMALL}

    names = [n for n, _, _ in BIG]
    first, rest = names[:2], names[2:]
    full = dict(zip(first, _all_gather([big_w[n].astype(BF16) for n in first], "weights_all_gather_ffn1")))
    g_send, g_recv, g_thru, g_bufs, g_token = _gather_start([big_w[n].astype(BF16) for n in rest], full[first[0]],
                                                            "weights_gather_start")
    gains = dict(small, ffn1_norm=small["ffn1_norm"] + g_token[0:1, 0:1])

    def late_weights(after):
        blocks, bufs = _gather_wait(g_send, g_recv, g_thru, g_bufs, after, "weights_gather_wait")
        return dict(zip(rest, _gather_finish(blocks, bufs, "weights_gather_finish")))

    inv_freq = ROPE_THETA ** (-jnp.arange(0, ROPE, 2, dtype=F32) / ROPE)
    ang = positions[0].astype(F32)[:, None] * inv_freq
    cs = jnp.concatenate([jnp.cos(ang), jnp.cos(ang)], axis=1)
    sn = jnp.concatenate([jnp.sin(ang), jnp.sin(ang)], axis=1)

    c = lax.axis_index("c")
    chip = 2 * lax.axis_index("x") + lax.axis_index("y")
    early = {}

    def chip_sums_of(g, ns, tag):
        from_sibling = _sibling_swap([g[n] for n in ns], "grads_sibling_swap_" + tag)
        return [_chip_sum(g[n], r1, c, "chip_sum_" + n) for n, r1 in zip(ns, from_sibling)]

    def early_grads(g):
        early["names"] = [n for n in names if n in g]
        early["sums"] = chip_sums_of(g, early["names"], "early")
        early["send"], early["recv"], early["thru"], early["lands"], token = _chip_exchange_start(
            early["sums"], "grads_exchange_start")
        return token

    loss_row, grad_x, g = _local_step(x[0], loss_target[0], cs, sn, full, late_weights, gains, early_grads)
    late = [n for n in names if n not in early["names"]]
    l_send, l_recv, l_thru, l_lands, l_token = _chip_exchange_start(chip_sums_of(g, late, "late"),
                                                                    "grads_exchange_late_start")
    sent, landed = _chip_exchange_wait(early["send"], early["recv"], early["thru"], early["lands"], l_token,
                                       "grads_exchange_wait")
    exchanged = {n: _put_own(land, lax.dynamic_index_in_dim(own, chip, 0), chip)
                 for n, land, own in zip(early["names"], landed, sent)}

    small_packed = jnp.concatenate([_pack_small(g), jnp.pad(loss_row, ((0, SUBLANE - 1), (0, 0)))], axis=0)
    small_all = _all_gather([small_packed], "small_all_gather")[0]
    zero_tail = jnp.zeros((SUBLANE, LANE), F32)
    pk = lambda d: jnp.concatenate([_pack_small(d), zero_tail], axis=0)
    sg, sd, sm, sv = _small_update(
        small_all, pk(small), pk({n: args["m_" + n].reshape(shape) for n, shape in SMALL}),
        pk({n: args["v_" + n].reshape(shape) for n, shape in SMALL}))
    n_small_rows = _pack_small(small).shape[0]
    loss = sg[n_small_rows, 0]
    outs = {k_: _unpack_small(a) for k_, a in (("grad", sg), ("delta", sd), ("new_m", sm), ("new_v", sv))}

    def adamw(n):
        outs["grad"][n], outs["delta"][n], outs["new_m"][n], outs["new_v"][n] = _sum_adamw(
            exchanged[n], big_w[n], args["m_" + n][0], args["v_" + n][0], "adamw_" + n)

    for n in early["names"]:
        adamw(n)
    sent, landed = _chip_exchange_wait(l_send, l_recv, l_thru, l_lands, outs["delta"][early["names"][-1]],
                                       "grads_exchange_late_wait")
    for n, land, own in zip(late, landed, sent):
        exchanged[n] = _put_own(land, lax.dynamic_index_in_dim(own, chip, 0), chip)
        adamw(n)

    def shaped(kind, n):
        return outs[kind][n].reshape(args[n].shape)

    return (loss, grad_x[None], *[shaped("grad", n) for n in WEIGHT_ORDER], *[shaped("delta", n) for n in WEIGHT_ORDER],
            *[shaped("new_m", n) for n in WEIGHT_ORDER], *[shaped("new_v", n) for n in WEIGHT_ORDER])
```

```python
import functools

import jax
import jax.numpy as jnp
from jax import lax
from jax.experimental import pallas as pl
from jax.experimental.pallas import tpu as pltpu

F32 = jnp.float32
BF16 = jnp.bfloat16

D = 1024
FF = 2816
NH = 8
HD = 128
ROPE = 64
QK = HD + ROPE
QL = 384
KVL = 256
HGW = NH * HD
CHUNK = 64
EPS = 1e-6
ROPE_THETA = 10000.0
SCALE = QK ** -0.5
LOG2E = 1.4426950408889634

LR, B1, B2, AEPS, WD, STEP = 0.001, 0.9, 0.999, 1e-08, 0.01, 10

HB = 128
SUB = 16
EXP_CLAMP = 80.0
ATT_TILES = (512, 256, 128)
ATT_Q = (1024, 512, 256, 128)
ATT_KEY_TILES = 1
ROW_TILES = (1024, 512, 256, 128)

LANE = 128
SUBLANE = 8
VMEM_LIMIT = 56 << 20

N_DEV = 8
MESH = pl.DeviceIdType.MESH

BIG = (
    ("ffn1_w_in", (D, 2 * FF), 1), ("ffn1_w_out", (FF, D), 0), ("w_in", (D, 4800), 1),
    ("w_hg_branch", (HGW, D), 0), ("w_q_up", (QL, NH * QK), 1), ("w_kv_up", (KVL, NH * 2 * HD), 1),
    ("w_mla_branch", (NH * HD, D), 0), ("w_merge", (D, 2 * D), 1), ("w_out", (D, D), 0),
    ("ffn2_w_in", (D, 2 * FF), 1), ("ffn2_w_out", (FF, D), 0),
)
SMALL = (
    ("ffn1_norm", (1, D)), ("mix_norm", (1, D)), ("hg_lb_table", (2, HGW)), ("hg_out_norm", (1, HD)),
    ("mla_q_lora_norm", (1, QL)), ("mla_kv_lora_norm", (1, KVL)), ("q_head_norm", (1, QK)),
    ("k_head_norm", (1, QK)), ("b_merge", (1, 2 * D)), ("ffn2_norm", (1, D)), ("final_norm", (1, D)),
)
WEIGHT_ORDER = ("ffn1_norm", "ffn1_w_in", "ffn1_w_out", "mix_norm", "w_in", "hg_lb_table", "hg_out_norm",
                "w_hg_branch", "mla_q_lora_norm", "w_q_up", "mla_kv_lora_norm", "w_kv_up", "q_head_norm",
                "k_head_norm", "w_mla_branch", "w_merge", "b_merge", "w_out", "ffn2_norm", "ffn2_w_in",
                "ffn2_w_out", "final_norm")


def _pick(n, cands):
    for c in cands:
        if n % c == 0:
            return c
    return n


def _params(sem):
    return pltpu.CompilerParams(dimension_semantics=sem, vmem_limit_bytes=VMEM_LIMIT)


def _sig(x):
    return 0.5 * jnp.tanh(0.5 * x) + 0.5


def _dot(a, b):
    return jnp.dot(a.astype(BF16), b.astype(BF16), preferred_element_type=F32)


def _dot_nt(a, b):
    return lax.dot_general(a.astype(BF16), b.astype(BF16), (((1,), (1,)), ((), ())),
                           preferred_element_type=F32)


def _dot_tn(a, b):
    return lax.dot_general(a.astype(BF16), b.astype(BF16), (((0,), (0,)), ((), ())),
                           preferred_element_type=F32)


def _split3(x):
    x1 = x.astype(BF16)
    r1 = x - x1.astype(F32)
    x2 = r1.astype(BF16)
    x3 = (r1 - x2.astype(F32)).astype(BF16)
    return x1, x2, x3


def _dot_sel(m, x):
    x1, x2, x3 = _split3(x)
    d = lambda p: jnp.dot(m, p, preferred_element_type=F32)
    return d(x1) + d(x2) + d(x3)


def _sel_dot(x, m):
    x1, x2, x3 = _split3(x)
    d = lambda p: jnp.dot(p, m, preferred_element_type=F32)
    return d(x1) + d(x2) + d(x3)


_TN = (1408, 1024, 768, 512, 384, 256, 128)


def _mm(a, b, mode, name, out_dtype=F32):
    if mode == "tn":
        t, m = a.shape
        n = b.shape[1]
        tt, tm, tn = _pick(t, (512, 256, 128)), _pick(m, _TN), _pick(n, _TN)

        def body(a_ref, b_ref, o_ref):
            @pl.when(pl.program_id(2) == 0)
            def _():
                o_ref[...] = jnp.zeros_like(o_ref)

            o_ref[...] += _dot_tn(a_ref[...], b_ref[...])

        return pl.pallas_call(
            body, name=name, grid=(m // tm, n // tn, t // tt),
            in_specs=[pl.BlockSpec((tt, tm), lambda i, j, k: (k, i)),
                      pl.BlockSpec((tt, tn), lambda i, j, k: (k, j))],
            out_specs=pl.BlockSpec((tm, tn), lambda i, j, k: (i, j)),
            out_shape=jax.ShapeDtypeStruct((m, n), F32),
            compiler_params=_params(("parallel", "parallel", "arbitrary")),
        )(a, b)

    m, k = a.shape
    tm = _pick(m, ROW_TILES)
    if mode == "nn":
        n = b.shape[1]
        tn = _pick(n, _TN)
        b_spec = pl.BlockSpec((k, tn), lambda i, j: (0, j))
        dot = _dot
    else:
        n = b.shape[0]
        tn = _pick(n, _TN if k <= 4096 else (512, 256, 128))
        b_spec = pl.BlockSpec((tn, k), lambda i, j: (j, 0))
        dot = _dot_nt

    def body(a_ref, b_ref, o_ref):
        o_ref[...] = dot(a_ref[...], b_ref[...]).astype(o_ref.dtype)

    return pl.pallas_call(
        body, name=name, grid=(m // tm, n // tn),
        in_specs=[pl.BlockSpec((tm, k), lambda i, j: (i, 0)), b_spec],
        out_specs=pl.BlockSpec((tm, tn), lambda i, j: (i, j)),
        out_shape=jax.ShapeDtypeStruct((m, n), out_dtype),
        compiler_params=_params(("parallel", "parallel")),
    )(a, b)


_DOTS = {"nn": _dot, "nt": _dot_nt, "tn": _dot_tn}
_BS = pl.BlockSpec


def _mmcall(name, kind, a, b, a_spec, b_spec, o_spec, o_shape, grid, red_axis=None, out_dtype=F32):
    dot = _DOTS[kind]

    def body(a_ref, b_ref, o_ref):
        if red_axis is None:
            o_ref[...] = dot(a_ref[...], b_ref[...]).astype(o_ref.dtype)
        else:
            @pl.when(pl.program_id(red_axis) == 0)
            def _():
                o_ref[...] = jnp.zeros_like(o_ref)

            o_ref[...] += dot(a_ref[...], b_ref[...])

    sem = tuple("arbitrary" if ax == red_axis else "parallel" for ax in range(len(grid)))
    return pl.pallas_call(
        body, name=name, grid=grid, in_specs=[a_spec, b_spec], out_specs=o_spec,
        out_shape=jax.ShapeDtypeStruct(o_shape, out_dtype), compiler_params=_params(sem),
    )(a, b)


class _AsRef:
    def __init__(self, value):
        self.value = value

    def __getitem__(self, idx):
        return self.value


def _mm_stack_red(a, w, name, kind, epilogue=None):
    s, t, n = a.shape
    nout = w.shape[2] if kind == "nn" else w.shape[1]
    if epilogue is None:
        tm = _pick(t, ROW_TILES)
        return _mmcall(name, kind, a, w, _BS((None, tm, n), lambda i, j: (j, i, 0)),
                       _BS((None,) + w.shape[1:], lambda i, j: (j, 0, 0)), _BS((tm, nout), lambda i, j: (i, 0)),
                       (t, nout), (t // tm, s), red_axis=1)
    fn, first, rows, vecs, outs, accs = epilogue
    rows = [first] + list(rows)
    tm = _pick(t, ROW_TILES)
    n_row, n_vec, n_out = len(rows), len(vecs), len(outs)
    dot = _DOTS[kind]

    def body(*refs):
        a_ref, w_ref, prod = refs[0], refs[1], refs[-1]
        row_refs = refs[2:2 + n_row]
        vec_refs = refs[2 + n_row:2 + n_row + n_vec]
        out_refs = refs[2 + n_row + n_vec:2 + n_row + n_vec + n_out]
        acc_refs = refs[2 + n_row + n_vec + n_out:-1]
        i, j = pl.program_id(0), pl.program_id(1)

        @pl.when(j == 0)
        def _():
            prod[...] = jnp.zeros_like(prod)

        prod[...] += dot(a_ref[...], w_ref[...])

        @pl.when(j == s - 1)
        def _():
            res = fn(row_refs[0], _AsRef(prod[...]), *row_refs[1:], *vec_refs, *out_refs)
            if acc_refs:
                @pl.when(i == 0)
                def _():
                    for r in acc_refs:
                        r[...] = jnp.zeros_like(r)

                for r, val in zip(acc_refs, res):
                    r[...] += val

    in_specs = [_BS((None, tm, n), lambda i, j: (j, i, 0)), _BS((None,) + w.shape[1:], lambda i, j: (j, 0, 0))]
    in_specs += [_BS((tm, r.shape[1]), lambda i, j: (i, 0)) for r in rows]
    in_specs += [_BS(v.shape, lambda i, j: (0, 0)) for v in vecs]
    out_specs = [_BS((tm, wd), lambda i, j: (i, 0)) for wd, _ in outs] + [_BS(sh, lambda i, j: (0, 0)) for sh in accs]
    out_shape = [jax.ShapeDtypeStruct((t, wd), dt) for wd, dt in outs] + [jax.ShapeDtypeStruct(sh, F32) for sh in accs]
    return pl.pallas_call(
        body, name=name, grid=(t // tm, s), in_specs=in_specs, out_specs=out_specs, out_shape=out_shape,
        scratch_shapes=[pltpu.VMEM((tm, nout), F32)],
        compiler_params=_params(("arbitrary" if accs else "parallel", "arbitrary")),
    )(a, w, *rows, *vecs)


def _mm_stack_tn(a, b, name):
    grp = 4
    if a.ndim == 2:
        t, k = a.shape
        s, _, n = b.shape
        tt = _pick(t, (512, 256, 128))
        a_spec = _BS((tt, k), lambda j, r: (r, 0))
        b_spec, b_in = _BS((None, grp, tt, n), lambda j, r: (j, 0, r, 0)), b.reshape(s // grp, grp, t, n)
        a_in = a
    else:
        s, t, k = a.shape
        n = b.shape[1]
        tt = _pick(t, (512, 256, 128))
        a_spec, a_in = _BS((None, grp, tt, k), lambda j, r: (j, 0, r, 0)), a.reshape(s // grp, grp, t, k)
        b_spec, b_in = _BS((tt, n), lambda j, r: (r, 0)), b

    def body(a_ref, b_ref, o_ref):
        @pl.when(pl.program_id(1) == 0)
        def _():
            o_ref[...] = jnp.zeros_like(o_ref)

        shared = a_ref[...] if a.ndim == 2 else b_ref[...]
        for e in range(grp):
            o_ref[e] += _dot_tn(shared, b_ref[e]) if a.ndim == 2 else _dot_tn(a_ref[e], shared)

    return pl.pallas_call(
        body, name=name, grid=(s // grp, t // tt), in_specs=[a_spec, b_spec],
        out_specs=_BS((None, grp, k, n), lambda j, r: (j, 0, 0, 0)),
        out_shape=jax.ShapeDtypeStruct((s // grp, grp, k, n), F32),
        compiler_params=_params(("parallel", "arbitrary")),
    )(a_in, b_in).reshape(s, k, n)


def _cols_fwd(x, w, name):
    t, k = x.shape
    s, _, n = w.shape
    tm = _pick(t, ROW_TILES)

    def body(x_ref, w_ref, o_ref):
        x_ = x_ref[...]
        for j in range(s):
            o_ref[:, n * j:n * (j + 1)] = _dot(x_, w_ref[j])

    return pl.pallas_call(
        body, name=name, grid=(t // tm,),
        in_specs=[_BS((tm, k), lambda i: (i, 0)), _BS((s, k, n), lambda i: (0, 0, 0))],
        out_specs=_BS((tm, s * n), lambda i: (i, 0)), out_shape=jax.ShapeDtypeStruct((t, s * n), F32),
        compiler_params=_params(("parallel",)),
    )(x, w)


def _cols_dx(d, w, name):
    t = d.shape[0]
    s, k, n = w.shape
    tm = _pick(t, ROW_TILES)

    def body(d_ref, w_ref, o_ref):
        acc = _dot_nt(d_ref[:, 0:n], w_ref[0])
        for j in range(1, s):
            acc = acc + _dot_nt(d_ref[:, n * j:n * (j + 1)], w_ref[j])
        o_ref[...] = acc

    return pl.pallas_call(
        body, name=name, grid=(t // tm,),
        in_specs=[_BS((tm, s * n), lambda i: (i, 0)), _BS((s, k, n), lambda i: (0, 0, 0))],
        out_specs=_BS((tm, k), lambda i: (i, 0)), out_shape=jax.ShapeDtypeStruct((t, k), F32),
        compiler_params=_params(("parallel",)),
    )(d, w)


def _cols_dw(x, d, n, name):
    t, k = x.shape
    s = d.shape[1] // n
    tt = _pick(t, (512, 256, 128))

    def body(x_ref, d_ref, o_ref):
        @pl.when(pl.program_id(0) == 0)
        def _():
            o_ref[...] = jnp.zeros_like(o_ref)

        x_ = x_ref[...]
        for j in range(s):
            o_ref[j] += _dot_tn(x_, d_ref[:, n * j:n * (j + 1)])

    return pl.pallas_call(
        body, name=name, grid=(t // tt,),
        in_specs=[_BS((tt, k), lambda r: (r, 0)), _BS((tt, s * n), lambda r: (r, 0))],
        out_specs=_BS((s, k, n), lambda r: (0, 0, 0)), out_shape=jax.ShapeDtypeStruct((s, k, n), F32),
        compiler_params=_params(("arbitrary",)),
    )(x, d)


def _rows(fn, name, t, tm, ins, vecs, outs, accs=()):
    n_in, n_out, n_acc = len(ins) + len(vecs), len(outs), len(accs)

    def body(*refs):
        res = fn(*refs[:n_in + n_out])
        if n_acc:
            acc_refs = refs[n_in + n_out:]

            @pl.when(pl.program_id(0) == 0)
            def _():
                for r in acc_refs:
                    r[...] = jnp.zeros_like(r)

            for r, val in zip(acc_refs, res):
                r[...] += val

    in_specs = [pl.BlockSpec((tm, bw), functools.partial(lambda i, cb: (i, cb), cb=cb)) for _, bw, cb in ins]
    in_specs += [pl.BlockSpec(v.shape, lambda i: (0, 0)) for v in vecs]
    out_specs = [pl.BlockSpec((tm, w), lambda i: (i, 0)) for w, _ in outs]
    out_specs += [pl.BlockSpec(s, lambda i: (0, 0)) for s in accs]
    out_shape = [jax.ShapeDtypeStruct((t, w), dt) for w, dt in outs]
    out_shape += [jax.ShapeDtypeStruct(s, F32) for s in accs]
    return pl.pallas_call(
        body, name=name, grid=(t // tm,), in_specs=in_specs, out_specs=out_specs, out_shape=out_shape,
        compiler_params=_params(("arbitrary",) if n_acc else ("parallel",)),
    )(*[a for a, _, _ in ins], *vecs)


def _rms(x, g):
    return x * lax.rsqrt(jnp.mean(x * x, axis=-1, keepdims=True) + EPS) * g


def _rms_bwd(x, g, dy):
    xh = x * lax.rsqrt(jnp.mean(x * x, axis=-1, keepdims=True) + EPS)
    r = lax.rsqrt(jnp.mean(x * x, axis=-1, keepdims=True) + EPS)
    dyg = dy * g
    dx = r * (dyg - xh * jnp.mean(dyg * xh, axis=-1, keepdims=True))
    return dx, jnp.sum(dy * xh, axis=0, keepdims=True)


def _hgrn_mats():
    row = lax.broadcasted_iota(jnp.int32, (HB, HB), 0)
    col = lax.broadcasted_iota(jnp.int32, (HB, HB), 1)
    return row, col


def _hgrn_gates(qr, z, t0, t1):
    lb = 1.0 / (1.0 + jnp.exp(t1 - t0))
    th = jnp.tanh(0.5 * z)
    sz, sneg = 0.5 + 0.5 * th, 0.5 - 0.5 * th
    f = lb + (1.0 - lb) * sz
    return lb, sz, sneg, f, jnp.log(f), (1.0 - lb) * sneg, qr * _sig(qr)


def _hgrn_scores(q, k, cum):
    heads, nsub = range(len(q)), HB // SUB
    mids = [[cum[h][SUB * i + SUB // 2 - 1:SUB * i + SUB // 2, :] for i in range(nsub)] for h in heads]
    cmid = [jnp.concatenate([cum[h][SUB * i:SUB * (i + 1)] - mids[h][i] for i in range(nsub)], axis=0) for h in heads]
    qd = [q[h] * jnp.exp(jnp.minimum(cmid[h], EXP_CLAMP)) for h in heads]
    qd_b = [x.astype(BF16) for x in qd]
    kds = [[None] * nsub for _ in heads]
    parts = [[None] * nsub for _ in heads]
    for i in range(nsub):
        for h in heads:
            kds[h][i] = k[h] * jnp.exp(jnp.minimum(mids[h][i] - cum[h], EXP_CLAMP))
        for h in heads:
            parts[h][i] = _dot_nt(qd_b[h][SUB * i:SUB * (i + 1)], kds[h][i])
    return qd, kds, [jnp.concatenate(parts[h], axis=0) for h in heads], cmid


def _hgrn_fwd(p_hg, table, gain, t):
    nblk = t // HB

    def body(q_ref, f_ref, i_ref, g_ref, tab_ref, gain_ref, o_ref, y_ref, st_ref, state):
        @pl.when(pl.program_id(0) == 0)
        def _():
            state[...] = jnp.zeros_like(state)

        row, col = _hgrn_mats()
        causal = col <= row
        tri = causal.astype(BF16)
        heads = range(NH)
        sl = [slice(HD * h, HD * (h + 1)) for h in heads]
        gates = [_hgrn_gates(q_ref[:, sl[h]], f_ref[:, sl[h]], tab_ref[0:1, sl[h]], tab_ref[1:2, sl[h]]) for h in heads]
        lf, k, q = [g[4] for g in gates], [g[5] for g in gates], [g[6] for g in gates]
        v = [i_ref[:, sl[h]] for h in heads]
        cum = [_dot_sel(tri, lf[h]) for h in heads]
        _, _, s, _ = _hgrn_scores(q, k, cum)
        p = [jnp.where(causal, s[h], 0.0) for h in heads]
        st = [state[h] for h in heads]
        o = [_dot(p[h], v[h]) + _dot_nt(q[h] * jnp.exp(cum[h]), st[h]) for h in heads]
        last = [cum[h][HB - 1:HB, :] for h in heads]
        new_st = [st[h] * jnp.exp(last[h]) + _dot_tn(v[h], k[h] * jnp.exp(last[h] - cum[h])) for h in heads]
        for h in heads:
            gr = g_ref[:, sl[h]]
            st_ref[h, 0] = st[h]
            state[h] = new_st[h]
            o_ref[:, sl[h]] = o[h]
            y_ref[:, sl[h]] = (_rms(o[h], gain_ref[...]) * gr * _sig(gr)).astype(BF16)

    blk = lambda cb: pl.BlockSpec((HB, HGW), functools.partial(lambda n, cb: (n, cb), cb=cb))
    return pl.pallas_call(
        body, name="hgrn_fwd", grid=(nblk,),
        in_specs=[blk(0), blk(1), blk(2), blk(3), pl.BlockSpec((2, HGW), lambda n: (0, 0)),
                  pl.BlockSpec((1, HD), lambda n: (0, 0))],
        out_specs=[pl.BlockSpec((HB, HGW), lambda n: (n, 0)), pl.BlockSpec((HB, HGW), lambda n: (n, 0)),
                   pl.BlockSpec((NH, 1, HD, HD), lambda n: (0, n, 0, 0))],
        out_shape=[jax.ShapeDtypeStruct((t, HGW), F32), jax.ShapeDtypeStruct((t, HGW), BF16),
                   jax.ShapeDtypeStruct((NH, nblk, HD, HD), F32)],
        scratch_shapes=[pltpu.VMEM((NH, HD, HD), F32)],
        compiler_params=_params(("arbitrary",)),
    )(p_hg, p_hg, p_hg, p_hg, table, gain)


def _hgrn_bwd(p_hg, table, gain, o_pre, states, dy, t):
    nblk = t // HB

    def body(q_ref, f_ref, i_ref, g_ref, tab_ref, gain_ref, o_ref, st_ref, dy_ref, dp_ref, dtab_ref, dgain_ref,
             dstate):
        @pl.when(pl.program_id(0) == 0)
        def _():
            dstate[...] = jnp.zeros_like(dstate)
            dtab_ref[...] = jnp.zeros_like(dtab_ref)
            dgain_ref[...] = jnp.zeros_like(dgain_ref)

        row, col = _hgrn_mats()
        causal = col <= row
        tri = causal.astype(BF16)
        tri_t = (row <= col).astype(BF16)
        heads, nsub = range(NH), HB // SUB
        sl = [slice(HD * h, HD * (h + 1)) for h in heads]
        rows_of = lambda i: slice(SUB * i, SUB * (i + 1))
        gain_ = gain_ref[...]
        qr, z = [q_ref[:, sl[h]] for h in heads], [f_ref[:, sl[h]] for h in heads]
        v, gr = [i_ref[:, sl[h]] for h in heads], [g_ref[:, sl[h]] for h in heads]
        gates = [_hgrn_gates(qr[h], z[h], tab_ref[0:1, sl[h]], tab_ref[1:2, sl[h]]) for h in heads]
        lb, sz, sneg, f, lf, k, q = ([g[j] for g in gates] for j in range(7))
        cum = [_dot_sel(tri, lf[h]) for h in heads]
        qd, kds, s, cmid = _hgrn_scores(q, k, cum)
        p = [jnp.where(causal, s[h], 0.0) for h in heads]
        st, dst = [st_ref[h, 0] for h in heads], [dstate[h] for h in heads]
        o, dyh, sg = [o_ref[:, sl[h]] for h in heads], [dy_ref[:, sl[h]] for h in heads], [_sig(x) for x in gr]
        dgr = [dyh[h] * _rms(o[h], gain_) * sg[h] * (1.0 + gr[h] * (1.0 - sg[h])) for h in heads]
        norm_bwd = [_rms_bwd(o[h], gain_, dyh[h] * gr[h] * sg[h]) for h in heads]
        do = [x[0] for x in norm_bwd]
        do_b = [x.astype(BF16) for x in do]
        ecum, last = [jnp.exp(x) for x in cum], [x[HB - 1:HB, :] for x in cum]
        qc = [q[h] * ecum[h] for h in heads]
        edec = [jnp.exp(last[h] - cum[h]) for h in heads]
        kdec = [k[h] * edec[h] for h in heads]
        dp = [jnp.where(causal, _dot_nt(do_b[h], v[h]), 0.0) for h in heads]
        dv = [_dot(p[h].T, do_b[h]) + _dot_nt(kdec[h], dst[h]) for h in heads]
        dqc = [_dot(do_b[h], st[h]) for h in heads]
        dkdec = [_dot(v[h], dst[h]) for h in heads]
        new_dst = [dst[h] * jnp.exp(last[h]) + _dot(do[h].T, qc[h]) for h in heads]
        dp_b = [x.astype(BF16) for x in dp]
        dqd = [jnp.concatenate(parts, axis=0) for parts in
               zip(*[[_dot(dp_b[h][rows_of(i)], kds[h][i]) for h in heads] for i in range(nsub)])]
        gq = [dqd[h] * qd[h] for h in heads]
        dq = [dqd[h] * jnp.exp(jnp.minimum(cmid[h], EXP_CLAMP)) + dqc[h] * ecum[h] for h in heads]
        gs = [dkdec[h] * kdec[h] for h in heads]
        dk = [dkdec[h] * edec[h] for h in heads]
        dcum = [dqc[h] * qc[h] - gs[h] + gq[h]
                + jnp.where(row == HB - 1, jnp.sum(gs[h], axis=0, keepdims=True)
                            + jnp.exp(last[h]) * jnp.sum(st[h] * dst[h], axis=0, keepdims=True), 0.0) for h in heads]
        qd_b = [x.astype(BF16) for x in qd]
        for i in range(nsub):
            dkd = [_dot_tn(dp_b[h][rows_of(i)], qd_b[h][rows_of(i)]) for h in heads]
            for h in heads:
                mid = cum[h][SUB * i + SUB // 2 - 1:SUB * i + SUB // 2, :]
                dk[h] = dk[h] + dkd[h] * jnp.exp(jnp.minimum(mid - cum[h], EXP_CLAMP))
                gk = dkd[h] * kds[h][i]
                to_mid = jnp.sum(gk, axis=0, keepdims=True) - jnp.sum(gq[h][rows_of(i)], axis=0, keepdims=True)
                dcum[h] = dcum[h] - gk + jnp.where(row == SUB * i + SUB // 2 - 1, to_mid, 0.0)
        dlf = [_dot_sel(tri_t, dcum[h]) for h in heads]
        dgain = jnp.zeros((1, HD), F32)
        for h in heads:
            df = dlf[h] / f[h] - dk[h]
            dlb = jnp.sum(df * sneg[h], axis=0, keepdims=True) * lb[h] * (1.0 - lb[h])
            dtab_ref[0:1, sl[h]] += dlb
            dtab_ref[1:2, sl[h]] -= dlb
            sq = _sig(qr[h])
            dstate[h] = new_dst[h]
            dp_ref[:, sl[h]] = (dq[h] * sq * (1.0 + qr[h] * (1.0 - sq))).astype(BF16)
            dp_ref[:, HGW + HD * h:HGW + HD * (h + 1)] = (df * (1.0 - lb[h]) * sz[h] * sneg[h]).astype(BF16)
            dp_ref[:, 2 * HGW + HD * h:2 * HGW + HD * (h + 1)] = dv[h].astype(BF16)
            dp_ref[:, 3 * HGW + HD * h:3 * HGW + HD * (h + 1)] = dgr[h].astype(BF16)
            dgain = dgain + norm_bwd[h][1]
        dgain_ref[...] += dgain

    rev = lambda cb: pl.BlockSpec((HB, HGW), functools.partial(lambda n, cb: (nblk - 1 - n, cb), cb=cb))
    return pl.pallas_call(
        body, name="hgrn_bwd", grid=(nblk,),
        in_specs=[rev(0), rev(1), rev(2), rev(3), pl.BlockSpec((2, HGW), lambda n: (0, 0)),
                  pl.BlockSpec((1, HD), lambda n: (0, 0)), rev(0),
                  pl.BlockSpec((NH, 1, HD, HD), lambda n: (0, nblk - 1 - n, 0, 0)), rev(0)],
        out_specs=[pl.BlockSpec((HB, 4 * HGW), lambda n: (nblk - 1 - n, 0)),
                   pl.BlockSpec((2, HGW), lambda n: (0, 0)), pl.BlockSpec((1, HD), lambda n: (0, 0))],
        out_shape=[jax.ShapeDtypeStruct((t, 4 * HGW), BF16), jax.ShapeDtypeStruct((2, HGW), F32),
                   jax.ShapeDtypeStruct((1, HD), F32)],
        scratch_shapes=[pltpu.VMEM((NH, HD, HD), F32)],
        compiler_params=_params(("arbitrary",)),
    )(p_hg, p_hg, p_hg, p_hg, table, gain, o_pre, states, dy)


def _rope_mat():
    r = lax.broadcasted_iota(jnp.int32, (ROPE, ROPE), 0)
    c = lax.broadcasted_iota(jnp.int32, (ROPE, ROPE), 1)
    half = ROPE // 2
    return ((r == c - half).astype(F32) - (r == c + half).astype(F32)).astype(BF16)


def _mla_prep_fwd(p_mla, cs, sn, wq, wkv, gql, gkvl, gq, gk, t):
    tm = _pick(t, (512, 256, 128))

    def body(p_ref, cs_ref, sn_ref, wq_ref, wkv_ref, gql_ref, gkvl_ref, gq_ref, gk_ref,
             q_ref, k_ref, v_ref):
        rmat = _rope_mat()
        cqn = _rms(p_ref[:, 0:QL], gql_ref[...]).astype(BF16)
        ckvn = _rms(p_ref[:, QL:QL + KVL], gkvl_ref[...]).astype(BF16)
        kpe = p_ref[:, QL + KVL:QL + KVL + ROPE]
        c, s = cs_ref[...], sn_ref[...]
        rot = lambda x: x * c + _sel_dot(x, rmat) * s
        heads = range(NH)
        ssq = lambda x: jnp.sum(x * x, -1, keepdims=True)
        qa = [_dot(cqn, wq_ref[h, :, 0:HD]) for h in heads]
        qr = [_dot(cqn, wq_ref[h, :, HD:QK]) for h in heads]
        kn = [_dot(ckvn, wkv_ref[h, :, 0:HD]) for h in heads]
        vv = [_dot(ckvn, wkv_ref[h, :, HD:2 * HD]) for h in heads]
        kpe_ss = ssq(kpe)
        rq = [lax.rsqrt((ssq(qa[h]) + ssq(qr[h])) / QK + EPS) for h in heads]
        rk = [lax.rsqrt((ssq(kn[h]) + kpe_ss) / QK + EPS) for h in heads]
        q_rope = [rot(qr[h] * rq[h] * gq_ref[:, HD:QK]) for h in heads]
        k_rope = [rot(kpe * rk[h] * gk_ref[:, HD:QK]) for h in heads]
        for h in heads:
            q_ref[h, :, 0:HD] = (qa[h] * rq[h] * gq_ref[:, 0:HD] * (SCALE * LOG2E)).astype(BF16)
            q_ref[h, :, HD:QK] = (q_rope[h] * (SCALE * LOG2E)).astype(BF16)
            k_ref[h, :, 0:HD] = (kn[h] * rk[h] * gk_ref[:, 0:HD]).astype(BF16)
            k_ref[h, :, HD:QK] = k_rope[h].astype(BF16)
            v_ref[h] = vv[h].astype(BF16)

    whole = lambda a: pl.BlockSpec(a.shape, functools.partial(lambda i, nd: (0,) * nd, nd=a.ndim))
    return pl.pallas_call(
        body, name="mla_prep_fwd", grid=(t // tm,),
        in_specs=[pl.BlockSpec((tm, QL + KVL + ROPE), lambda i: (i, 0)), pl.BlockSpec((tm, ROPE), lambda i: (i, 0)),
                  pl.BlockSpec((tm, ROPE), lambda i: (i, 0))] + [whole(a) for a in (wq, wkv, gql, gkvl, gq, gk)],
        out_specs=[pl.BlockSpec((NH, tm, QK), lambda i: (0, i, 0)), pl.BlockSpec((NH, tm, QK), lambda i: (0, i, 0)),
                   pl.BlockSpec((NH, tm, HD), lambda i: (0, i, 0))],
        out_shape=[jax.ShapeDtypeStruct((NH, t, QK), BF16), jax.ShapeDtypeStruct((NH, t, QK), BF16),
                   jax.ShapeDtypeStruct((NH, t, HD), BF16)],
        compiler_params=_params(("parallel",)),
    )(p_mla, cs, sn, wq, wkv, gql, gkvl, gq, gk)


def _mla_prep_bwd(p_mla, cs, sn, wq, wkv, gql, gkvl, gq, gk, dq, dk, dv, t):
    tm = _pick(t, (512, 256, 128))

    def body(p_ref, cs_ref, sn_ref, wq_ref, wkv_ref, gql_ref, gkvl_ref, gq_ref, gk_ref,
             dq_ref, dk_ref, dv_ref,
             dp_ref, dwq_ref, dwkv_ref, dgql_ref, dgkvl_ref, dgq_ref, dgk_ref):
        accs = (dwq_ref, dwkv_ref, dgql_ref, dgkvl_ref, dgq_ref, dgk_ref)

        @pl.when(pl.program_id(0) == 0)
        def _():
            for r in accs:
                r[...] = jnp.zeros_like(r)

        rmat = _rope_mat()
        rmat_t = -rmat
        cq, ckv = p_ref[:, 0:QL], p_ref[:, QL:QL + KVL]
        kpe = p_ref[:, QL + KVL:QL + KVL + ROPE]
        cqn_f, ckvn_f = _rms(cq, gql_ref[...]), _rms(ckv, gkvl_ref[...])
        cqn, ckvn = cqn_f.astype(BF16), ckvn_f.astype(BF16)
        cqn_t, ckvn_t = cqn_f.T.astype(BF16), ckvn_f.T.astype(BF16)
        c, s = cs_ref[...], sn_ref[...]
        unrot = lambda dy: dy * c + _sel_dot(dy * s, rmat_t)
        heads = range(NH)
        rsum = lambda x: jnp.sum(x, -1, keepdims=True)
        csum = lambda x: jnp.sum(x, axis=0, keepdims=True)
        qa = [_dot(cqn, wq_ref[h, :, 0:HD]) for h in heads]
        qr = [_dot(cqn, wq_ref[h, :, HD:QK]) for h in heads]
        kn = [_dot(ckvn, wkv_ref[h, :, 0:HD]) for h in heads]
        dyr = [unrot(dq_ref[h, :, HD:QK] * SCALE) for h in heads]
        dkr = [unrot(dk_ref[h, :, HD:QK]) for h in heads]
        rq = [lax.rsqrt((rsum(qa[h] * qa[h]) + rsum(qr[h] * qr[h])) / QK + EPS) for h in heads]
        xa, xr = [qa[h] * rq[h] for h in heads], [qr[h] * rq[h] for h in heads]
        dya = [dq_ref[h, :, 0:HD] * SCALE for h in heads]
        ga, gr_ = [dya[h] * gq_ref[:, 0:HD] for h in heads], [dyr[h] * gq_ref[:, HD:QK] for h in heads]
        mean = [(rsum(ga[h] * xa[h]) + rsum(gr_[h] * xr[h])) / QK for h in heads]
        dqa = [(rq[h] * (ga[h] - xa[h] * mean[h])).astype(BF16) for h in heads]
        dqr = [(rq[h] * (gr_[h] - xr[h] * mean[h])).astype(BF16) for h in heads]
        kpe_ss = rsum(kpe * kpe)
        rk = [lax.rsqrt((rsum(kn[h] * kn[h]) + kpe_ss) / QK + EPS) for h in heads]
        ya, yr = [kn[h] * rk[h] for h in heads], [kpe * rk[h] for h in heads]
        dka = [dk_ref[h, :, 0:HD] for h in heads]
        ha, hr = [dka[h] * gk_ref[:, 0:HD] for h in heads], [dkr[h] * gk_ref[:, HD:QK] for h in heads]
        mean_k = [(rsum(ha[h] * ya[h]) + rsum(hr[h] * yr[h])) / QK for h in heads]
        dkn = [(rk[h] * (ha[h] - ya[h] * mean_k[h])).astype(BF16) for h in heads]
        dvh = [dv_ref[h].astype(BF16) for h in heads]
        dw = [(_dot(cqn_t, dqa[h]), _dot(cqn_t, dqr[h]), _dot(ckvn_t, dkn[h]), _dot(ckvn_t, dvh[h])) for h in heads]
        back_q = [_dot_nt(dqa[h], wq_ref[h, :, 0:HD]) + _dot_nt(dqr[h], wq_ref[h, :, HD:QK]) for h in heads]
        back_kv = [_dot_nt(dkn[h], wkv_ref[h, :, 0:HD]) + _dot_nt(dvh[h], wkv_ref[h, :, HD:2 * HD]) for h in heads]
        dcqn, dckvn = sum(back_q[1:], back_q[0]), sum(back_kv[1:], back_kv[0])
        dkpe = sum([rk[h] * (hr[h] - yr[h] * mean_k[h]) for h in heads][1:], rk[0] * (hr[0] - yr[0] * mean_k[0]))
        dgq_a = sum([csum(dya[h] * xa[h]) for h in heads][1:], csum(dya[0] * xa[0]))
        dgq_r = sum([csum(dyr[h] * xr[h]) for h in heads][1:], csum(dyr[0] * xr[0]))
        dgk_a = sum([csum(dka[h] * ya[h]) for h in heads][1:], csum(dka[0] * ya[0]))
        dgk_r = sum([csum(dkr[h] * yr[h]) for h in heads][1:], csum(dkr[0] * yr[0]))
        for h in heads:
            dwq_ref[h, :, 0:HD] += dw[h][0]
            dwq_ref[h, :, HD:QK] += dw[h][1]
            dwkv_ref[h, :, 0:HD] += dw[h][2]
            dwkv_ref[h, :, HD:2 * HD] += dw[h][3]
        dcq, dg1 = _rms_bwd(cq, gql_ref[...], dcqn)
        dckv, dg2 = _rms_bwd(ckv, gkvl_ref[...], dckvn)
        dp_ref[:, 0:QL] = dcq.astype(BF16)
        dp_ref[:, QL:QL + KVL] = dckv.astype(BF16)
        dp_ref[:, QL + KVL:QL + KVL + ROPE] = dkpe.astype(BF16)
        dgql_ref[...] += dg1
        dgkvl_ref[...] += dg2
        dgq_ref[:, 0:HD] += dgq_a
        dgq_ref[:, HD:QK] += dgq_r
        dgk_ref[:, 0:HD] += dgk_a
        dgk_ref[:, HD:QK] += dgk_r

    whole = lambda a: pl.BlockSpec(a.shape, functools.partial(lambda i, nd: (0,) * nd, nd=a.ndim))
    acc_shapes = [wq.shape, wkv.shape, gql.shape, gkvl.shape, gq.shape, gk.shape]
    return pl.pallas_call(
        body, name="mla_prep_bwd", grid=(t // tm,),
        in_specs=[pl.BlockSpec((tm, QL + KVL + ROPE), lambda i: (i, 0)), pl.BlockSpec((tm, ROPE), lambda i: (i, 0)),
                  pl.BlockSpec((tm, ROPE), lambda i: (i, 0))]
        + [whole(a) for a in (wq, wkv, gql, gkvl, gq, gk)]
        + [pl.BlockSpec((NH, tm, QK), lambda i: (0, i, 0)), pl.BlockSpec((NH, tm, QK), lambda i: (0, i, 0)),
           pl.BlockSpec((NH, tm, HD), lambda i: (0, i, 0))],
        out_specs=[pl.BlockSpec((tm, QL + KVL + ROPE), lambda i: (i, 0))]
        + [pl.BlockSpec(s, functools.partial(lambda i, nd: (0,) * nd, nd=len(s))) for s in acc_shapes],
        out_shape=[jax.ShapeDtypeStruct((t, QL + KVL + ROPE), BF16)]
        + [jax.ShapeDtypeStruct(s, F32) for s in acc_shapes],
        compiler_params=_params(("arbitrary",)),
    )(p_mla, cs, sn, wq, wkv, gql, gkvl, gq, gk, dq, dk, dv)


def _chunk_mask(nq, nk, key0, keys_on_rows):
    shape = (nk, nq) if keys_on_rows else (nq, nk)
    qi = lax.broadcasted_iota(jnp.int32, shape, 1 if keys_on_rows else 0) // CHUNK
    ki = lax.broadcasted_iota(jnp.int32, shape, 0 if keys_on_rows else 1) // CHUNK + key0 // CHUNK
    return ki <= qi


def _flash_fwd(q, k, v, t):
    tq = _pick(t, ATT_Q)
    tk = tq // ATT_KEY_TILES

    def body(q_ref, k_ref, v_ref, o_ref, lse_ref):
        i = pl.program_id(1)
        qt = q_ref[0]

        def step(j, carry, key0):
            m, l, acc = carry
            cols = pl.ds(pl.multiple_of(j * tk, tk), tk)
            s = _dot_nt(qt, k_ref[0, cols, :])
            if key0 is not None:
                s = jnp.where(_chunk_mask(tq, tk, key0, False), s, -jnp.inf)
            m_new = jnp.maximum(m, jnp.max(s, axis=-1, keepdims=True))
            p = jnp.exp2(s - m_new)
            alpha = jnp.exp2(m - m_new)
            return m_new, alpha * l + jnp.sum(p, axis=-1, keepdims=True), alpha * acc + _dot(p, v_ref[0, cols, :])

        init = (jnp.full((tq, 1), -jnp.inf, F32), jnp.zeros((tq, 1), F32), jnp.zeros((tq, HD), F32))
        carry = lax.fori_loop(0, ATT_KEY_TILES * i, lambda j, cr: step(j, cr, None), init)
        for h in range(ATT_KEY_TILES):
            carry = step(ATT_KEY_TILES * i + h, carry, h * tk)
        m, l, acc = carry
        o_ref[...] = (acc / l).astype(BF16)
        lse_ref[0] = m + jnp.log2(l)

    return pl.pallas_call(
        body, name="flash_fwd", grid=(NH, t // tq),
        in_specs=[pl.BlockSpec((1, tq, QK), lambda h, i: (h, i, 0)), pl.BlockSpec((1, t, QK), lambda h, i: (h, 0, 0)),
                  pl.BlockSpec((1, t, HD), lambda h, i: (h, 0, 0))],
        out_specs=[pl.BlockSpec((tq, HD), lambda h, i: (i, h)), pl.BlockSpec((1, tq, 1), lambda h, i: (h, i, 0))],
        out_shape=[jax.ShapeDtypeStruct((t, NH * HD), BF16), jax.ShapeDtypeStruct((NH, t, 1), F32)],
        compiler_params=_params(("parallel", "parallel")),
    )(q, k, v)


def _attn_out_bwd(dy, w, o, t):
    tm = _pick(t, ATT_TILES)

    def body(dy_ref, w_ref, o_ref, do_ref, delta_ref):
        do = _dot_nt(dy_ref[...], w_ref[...]).astype(BF16)
        do_ref[...] = do
        ones = jnp.ones((SUBLANE, HD), BF16)
        rowsum = lambda p: lax.dot_general(ones, p, (((1,), (1,)), ((), ())), preferred_element_type=F32)
        for h in range(NH):
            sl = slice(HD * h, HD * (h + 1))
            x1, x2, x3 = _split3(do[:, sl].astype(F32) * o_ref[:, sl].astype(F32))
            delta_ref[h] = (rowsum(x1) + rowsum(x2) + rowsum(x3))[0:1, :]

    return pl.pallas_call(
        body, name="attn_out_bwd", grid=(t // tm,),
        in_specs=[_BS((tm, dy.shape[1]), lambda i: (i, 0)), _BS(w.shape, lambda i: (0, 0)),
                  _BS((tm, NH * HD), lambda i: (i, 0))],
        out_specs=[_BS((tm, NH * HD), lambda i: (i, 0)), _BS((NH, 1, tm), lambda i: (0, 0, i))],
        out_shape=[jax.ShapeDtypeStruct((t, NH * HD), BF16), jax.ShapeDtypeStruct((NH, 1, t), F32)],
        compiler_params=_params(("parallel",)),
    )(dy, w, o)


def _flash_bwd(q, k, v, do, lse_row, delta_row, t):
    tq = _pick(t, ATT_Q)
    tk = tq // ATT_KEY_TILES

    def body(q_ref, k_ref, v_ref, do_ref, lse_ref, delta_ref, dq_ref, dk_ref, dv_ref):
        j = pl.program_id(1)

        @pl.when(j == 0)
        def _():
            dq_ref[...] = jnp.zeros_like(dq_ref)

        kt, vt = k_ref[0], v_ref[0]

        def step(i, carry, key0):
            dk, dv = carry
            rows = pl.ds(pl.multiple_of(i * tq, tq), tq)
            qt, dot_ = q_ref[0, rows, :], do_ref[rows, :]
            p = jnp.exp2(_dot_nt(kt, qt) - lse_ref[0, :, rows])
            if key0 is not None:
                p = jnp.where(_chunk_mask(tq, tk, key0, True), p, 0.0)
            ds = (p * (_dot_nt(vt, dot_) - delta_ref[0, :, rows])).astype(BF16)
            dq_ref[0, rows, :] += _dot_tn(ds, kt)
            return dk + _dot(ds, qt), dv + _dot(p, dot_)

        own = j // ATT_KEY_TILES
        carry = step(own, (jnp.zeros((tk, QK), F32), jnp.zeros((tk, HD), F32)), (j % ATT_KEY_TILES) * tk)
        dk, dv = lax.fori_loop(own + 1, t // tq, lambda i, cr: step(i, cr, None), carry)
        dk_ref[0] = dk * (1.0 / LOG2E)
        dv_ref[0] = dv

    return pl.pallas_call(
        body, name="flash_bwd", grid=(NH, t // tk),
        in_specs=[pl.BlockSpec((1, t, QK), lambda h, j: (h, 0, 0)), pl.BlockSpec((1, tk, QK), lambda h, j: (h, j, 0)),
                  pl.BlockSpec((1, tk, HD), lambda h, j: (h, j, 0)), pl.BlockSpec((t, HD), lambda h, j: (0, h)),
                  pl.BlockSpec((1, 1, t), lambda h, j: (h, 0, 0)), pl.BlockSpec((1, 1, t), lambda h, j: (h, 0, 0))],
        out_specs=[pl.BlockSpec((1, t, QK), lambda h, j: (h, 0, 0)), pl.BlockSpec((1, tk, QK), lambda h, j: (h, j, 0)),
                   pl.BlockSpec((1, tk, HD), lambda h, j: (h, j, 0))],
        out_shape=[jax.ShapeDtypeStruct((NH, t, QK), F32), jax.ShapeDtypeStruct((NH, t, QK), F32),
                   jax.ShapeDtypeStruct((NH, t, HD), F32)],
        compiler_params=_params(("parallel", "arbitrary")),
    )(q, k, v, do, lse_row, delta_row)


def _ffn_in(xn, w_in, name):
    t, k = xn.shape
    s, _, n = w_in.shape
    tm = _pick(t, ROW_TILES)

    def body(x_ref, w_ref, gu_ref, a_ref):
        x = x_ref[...]
        g, u = _dot(x, w_ref[0]), _dot(x, w_ref[1])
        gu_ref[0] = g.astype(BF16)
        gu_ref[1] = u.astype(BF16)
        a_ref[...] = (g * _sig(g) * u).astype(BF16)

    return pl.pallas_call(
        body, name=name, grid=(t // tm, s // 2),
        in_specs=[_BS((tm, k), lambda i, j: (i, 0)), _BS((2, None, k, n), lambda i, j: (0, j, 0, 0))],
        out_specs=[_BS((2, None, tm, n), lambda i, j: (0, j, i, 0)), _BS((None, tm, n), lambda i, j: (j, i, 0))],
        out_shape=[jax.ShapeDtypeStruct((2, s // 2, t, n), BF16), jax.ShapeDtypeStruct((s // 2, t, n), BF16)],
        compiler_params=_params(("parallel", "parallel")),
    )(xn, w_in.reshape(2, s // 2, k, n))


def _ffn_dgu(dfo, w_out, gu, name):
    t, k = dfo.shape
    s, n, _ = w_out.shape
    tm = _pick(t, ROW_TILES)

    def body(d_ref, w_ref, gu_ref, o_ref):
        da = _dot_nt(d_ref[...], w_ref[...])
        g, u = gu_ref[0].astype(F32), gu_ref[1].astype(F32)
        sg = _sig(g)
        o_ref[0] = (da * u * sg * (1.0 + g * (1.0 - sg))).astype(BF16)
        o_ref[1] = (da * g * sg).astype(BF16)

    pair = _BS((2, None, tm, n), lambda i, j: (0, j, i, 0))
    return pl.pallas_call(
        body, name=name, grid=(t // tm, s),
        in_specs=[_BS((tm, k), lambda i, j: (i, 0)), _BS((None, n, k), lambda i, j: (j, 0, 0)), pair],
        out_specs=pair, out_shape=jax.ShapeDtypeStruct((2, s, t, n), BF16),
        compiler_params=_params(("parallel", "parallel")),
    )(dfo, w_out, gu).reshape(2 * s, t, n)


def _ffn_fwd(xn, w_in, w_out, tag, epilogue):
    gu, a = _ffn_in(xn, w_in, tag + "_in")
    return gu, a, _mm_stack_red(a, w_out, tag + "_out", "nn", epilogue)


def _ffn_bwd(dfo, xn, gu, a, w_in, w_out, tag, epilogue):
    dw_out = _mm_stack_tn(a, dfo, tag + "_dwout")
    dgu = _ffn_dgu(dfo, w_out, gu, tag + "_dgu")
    return _mm_stack_red(dgu, w_in, tag + "_dxn", "nt", epilogue), _mm_stack_tn(xn, dgu, tag + "_dwin"), dw_out


def _local_step(x, target, cs, sn, w, late_weights, s, early_grads):
    t = x.shape[0]
    tm = _pick(t, (256, 128))
    g = {}
    ffn_out = lambda n: w[n].reshape(4, FF // 4, D)

    def norm_fn(x_ref, g_ref, o_ref):
        o_ref[...] = _rms(x_ref[...], g_ref[...]).astype(BF16)

    xn1 = _rows(norm_fn, "norm1", t, tm, [(x, D, 0)], [s["ffn1_norm"]], [(D, BF16)])[0]
    def res_norm_fn(scale):
        def fn(h_ref, f_ref, g_ref, h_out, n_out):
            h = h_ref[...] + scale * f_ref[...]
            h_out[...] = h
            n_out[...] = _rms(h, g_ref[...]).astype(BF16)
        return fn

    gu1, a1, (h1, u) = _ffn_fwd(xn1, w["ffn1_w_in"], ffn_out("ffn1_w_out"), "ffn1",
                                (res_norm_fn(0.5), x, [], [s["mix_norm"]], [(D, F32), (D, BF16)], []))
    w = {**w, **late_weights(h1)}
    rows_of = lambda n: w[n].reshape(-1, w[n].shape[2])
    w_hgb, w_mlab, w_o = rows_of("w_hg_branch"), rows_of("w_mla_branch"), rows_of("w_out")
    w_in_nat = w["w_in"].transpose(1, 0, 2).reshape(D, -1)
    w_mrg = w["w_merge"]
    mw = w_mrg.shape[2]
    w_in_hg, w_in_mla = w_in_nat[:, :4 * HGW], w_in_nat[:, 4 * HGW:]
    p_hg = _mm(u, w_in_hg, "nn", "proj_hg")
    p_mla = _mm(u, w_in_mla, "nn", "proj_mla")
    gpre = _cols_fwd(u, w_mrg, "proj_gate")
    o_pre, hgy, states = _hgrn_fwd(p_hg, s["hg_lb_table"], s["hg_out_norm"], t)
    prep_args = (p_mla, cs, sn, w["w_q_up"], w["w_kv_up"], s["mla_q_lora_norm"], s["mla_kv_lora_norm"],
                 s["q_head_norm"], s["k_head_norm"])
    q, k, v = _mla_prep_fwd(*prep_args, t)
    att, lse = _flash_fwd(q, k, v, t)
    y_hg = _mm(hgy, w_hgb, "nn", "branch_hg")
    y_mla = _mm(att, w_mlab, "nn", "branch_mla")

    def mix_fn(gh_ref, gm_ref, yh_ref, ym_ref, b_ref, o_ref):
        gh = _sig(gh_ref[...] + b_ref[:, 0:D])
        gm = _sig(gm_ref[...] + b_ref[:, D:2 * D])
        o_ref[...] = (gh * yh_ref[...] + gm * ym_ref[...]).astype(BF16)

    mixed = _rows(mix_fn, "mix", t, tm, [(gpre, D, 0), (gpre, D, 1), (y_hg, D, 0), (y_mla, D, 0)], [s["b_merge"]],
                  [(D, BF16)])[0]
    h2, xn2 = _mm_stack_red(mixed[None], w_o[None], "mix_out", "nn",
                            (res_norm_fn(1.0), h1, [], [s["ffn2_norm"]], [(D, F32), (D, BF16)], []))

    def loss_fn(h_ref, f_ref, tg_ref, g_ref, dh_out, dhb_out):
        h = h_ref[...] + 0.5 * f_ref[...]
        e = _rms(h, g_ref[...]) - tg_ref[...]
        dh, dgain = _rms_bwd(h, g_ref[...], e / D)
        dh_out[...] = dh
        dhb_out[...] = (0.5 * dh).astype(BF16)
        return dgain, jnp.full((1, LANE), 0.5 / D * jnp.sum(e * e), F32)

    gu2, a2, (dh3, dfo2, g["final_norm"], loss) = _ffn_fwd(
        xn2, w["ffn2_w_in"], ffn_out("ffn2_w_out"), "ffn2",
        (loss_fn, h2, [target], [s["final_norm"]], [(D, F32), (D, BF16)], [(1, D), (1, LANE)]))

    def norm_bwd_fn(scale):
        def fn(h_ref, dxn_ref, dh_ref, g_ref, dh_out, dhb_out):
            dx, dgain = _rms_bwd(h_ref[...], g_ref[...], dxn_ref[...])
            dh = dh_ref[...] + dx
            dh_out[...] = dh
            dhb_out[...] = (scale * dh).astype(BF16)
            return (dgain,)
        return fn

    as_rows = lambda a: a.reshape((N_DEV, -1) + a.shape[-1:])
    (dh2, dh2b, g["ffn2_norm"]), g["ffn2_w_in"], dwo = _ffn_bwd(
        dfo2, xn2, gu2, a2, w["ffn2_w_in"], ffn_out("ffn2_w_out"), "ffn2",
        (norm_bwd_fn(1.0), h2, [dh3], [s["ffn2_norm"]], [(D, F32), (D, BF16)], [(1, D)]))
    g["ffn2_w_out"] = as_rows(dwo)
    dmixed = _mm(dh2b, w_o, "nt", "mix_out_dx")
    g["w_out"] = as_rows(_mm(mixed, dh2b, "tn", "mix_out_dw"))

    def mix_bwd_fn(gh_ref, gm_ref, yh_ref, ym_ref, dm_ref, b_ref, dyh_out, dym_out, dg_out):
        gh = _sig(gh_ref[...] + b_ref[:, 0:D])
        gm = _sig(gm_ref[...] + b_ref[:, D:2 * D])
        dm = dm_ref[...]
        dyh_out[...] = (dm * gh).astype(BF16)
        dym_out[...] = (dm * gm).astype(BF16)
        dgh = dm * yh_ref[...] * gh * (1.0 - gh)
        dgm = dm * ym_ref[...] * gm * (1.0 - gm)
        dg_out[:, 0:D] = dgh.astype(BF16)
        dg_out[:, D:2 * D] = dgm.astype(BF16)
        return (jnp.concatenate([jnp.sum(dgh, axis=0, keepdims=True), jnp.sum(dgm, axis=0, keepdims=True)], axis=1),)

    dyh, dym, dgpre, g["b_merge"] = _rows(
        mix_bwd_fn, "mix_bwd", t, tm, [(gpre, D, 0), (gpre, D, 1), (y_hg, D, 0), (y_mla, D, 0), (dmixed, D, 0)],
        [s["b_merge"]], [(D, BF16), (D, BF16), (2 * D, BF16)], [(1, 2 * D)])
    g["w_hg_branch"] = as_rows(_mm(hgy, dyh, "tn", "branch_hg_dw"))
    g["w_mla_branch"] = as_rows(_mm(att, dym, "tn", "branch_mla_dw"))
    g["w_merge"] = _cols_dw(u, dgpre, mw, "proj_gate_dw")
    dhgy = _mm(dyh, w_hgb, "nt", "branch_hg_dx")
    datt, delta = _attn_out_bwd(dym, w_mlab, att, t)
    du_gate = _cols_dx(dgpre, w_mrg, "proj_gate_dx")

    dq, dk, dv = _flash_bwd(q, k, v, datt, lse.reshape(NH, 1, t), delta, t)
    (dp_mla, g["w_q_up"], g["w_kv_up"], g["mla_q_lora_norm"], g["mla_kv_lora_norm"], g["q_head_norm"],
     g["k_head_norm"]) = _mla_prep_bwd(*prep_args, dq, dk, dv, t)
    dp_hg, g["hg_lb_table"], g["hg_out_norm"] = _hgrn_bwd(p_hg, s["hg_lb_table"], s["hg_out_norm"], o_pre, states,
                                                          dhgy, t)
    dw_in_nat = jnp.concatenate([_mm(u, dp_hg, "tn", "proj_hg_dw"), _mm(u, dp_mla, "tn", "proj_mla_dw")], axis=1)
    g["w_in"] = dw_in_nat.reshape(D, N_DEV, -1).transpose(1, 0, 2)
    du_hg = _mm(dp_hg, w_in_hg, "nt", "proj_hg_dx")
    du_mla = _mm(dp_mla, w_in_mla, "nt", "proj_mla_dx")

    def mixnorm_bwd_fn(h_ref, a_ref, b_ref, c_ref, dh_ref, g_ref, dh_out, dhb_out):
        dx, dgain = _rms_bwd(h_ref[...], g_ref[...], a_ref[...] + b_ref[...] + c_ref[...])
        dh = dh_ref[...] + dx
        dh_out[...] = dh
        dhb_out[...] = (0.5 * dh).astype(BF16)
        return (dgain,)

    mix_gain = s["mix_norm"] + early_grads(g)[0:1, 0:1]
    dh1, dfo1, g["mix_norm"] = _rows(mixnorm_bwd_fn, "mixnorm_bwd", t, tm,
                                     [(h1, D, 0), (du_hg, D, 0), (du_mla, D, 0), (du_gate, D, 0), (dh2, D, 0)],
                                     [mix_gain], [(D, F32), (D, BF16)], [(1, D)])
    (grad_x, _, g["ffn1_norm"]), g["ffn1_w_in"], dwo = _ffn_bwd(
        dfo1, xn1, gu1, a1, w["ffn1_w_in"], ffn_out("ffn1_w_out"), "ffn1",
        (norm_bwd_fn(1.0), x, [dh1], [s["ffn1_norm"]], [(D, F32), (D, BF16)], [(1, D)]))
    g["ffn1_w_out"] = as_rows(dwo)
    return loss, grad_x, g


def _coords():
    return lax.axis_index("x"), lax.axis_index("y"), lax.axis_index("c")


def _hbm_call(body, name, ins, out_shapes, scratch):
    any_spec = pl.BlockSpec(memory_space=pl.ANY)
    return pl.pallas_call(
        body, name=name, out_shape=[jax.ShapeDtypeStruct(s, dt) for s, dt in out_shapes],
        in_specs=[any_spec] * len(ins), out_specs=[any_spec] * len(out_shapes), scratch_shapes=scratch,
    )(*ins)


def _my_slot():
    return 4 * lax.axis_index("x") + 2 * lax.axis_index("y") + lax.axis_index("c")


def _put_own(buf, own, index):
    return lax.dynamic_update_index_in_dim(buf, own, index, 0)


def _all_gather(blocks, name):
    nb = len(blocks)

    def body(*refs):
        x_refs, out_refs = refs[:nb], refs[nb:2 * nb]
        send_sems, recv_sems = refs[2 * nb:]
        x, y, c = _coords()
        me, sibling = (x, y, c), (x, y, 1 - c)
        chips = [(1 - x, y), (x, 1 - y), (1 - x, 1 - y)]

        def slot(b, px, py, pc):
            return out_refs[b].at[4 * px + 2 * py + pc]

        def copy(b, kk, block_of, to, src=None):
            return pltpu.make_async_remote_copy(
                src_ref=slot(b, *block_of) if src is None else src, dst_ref=slot(b, *block_of),
                send_sem=send_sems.at[b, kk], recv_sem=recv_sems.at[b, kk], device_id=to, device_id_type=MESH)

        first = [copy(b, 0, me, sibling, src=x_refs[b]) for b in range(nb)]
        first += [copy(b, 1 + j, me, (*chip, c), src=x_refs[b]) for j, chip in enumerate(chips) for b in range(nb)]
        for cp in first:
            cp.start()
        passed = []
        for j, chip in enumerate(chips):
            for b in range(nb):
                copy(b, 1 + j, (*chip, c), me).wait_recv()
                passed.append(copy(b, 4 + j, (*chip, c), sibling))
                passed[-1].start()
        for b in range(nb):
            copy(b, 0, sibling, me).wait_recv()
        for j, chip in enumerate(chips):
            for b in range(nb):
                copy(b, 4 + j, (*chip, 1 - c), me).wait_recv()
        for cp in first + passed:
            cp.wait_send()

    outs = _hbm_call(body, name, blocks, [((N_DEV,) + b.shape, b.dtype) for b in blocks],
                     [pltpu.SemaphoreType.DMA((nb, 7)), pltpu.SemaphoreType.DMA((nb, 7))])
    return [_put_own(o, b[None], _my_slot()) for o, b in zip(outs, blocks)]


def _gather_peers():
    x, y, c = _coords()
    return (x, y, c), [(x, y, 1 - c), (1 - x, y, c), (x, 1 - y, c), (1 - x, 1 - y, c)]


def _gather_start(blocks, after, name):
    nb = len(blocks)
    hbm, sem = pl.BlockSpec(memory_space=pltpu.HBM), pl.BlockSpec(memory_space=pltpu.SEMAPHORE)

    def body(*refs):
        x_refs, out_refs = refs[:nb], refs[nb:2 * nb]
        send_sems, recv_sems, token = refs[2 * nb + 1], refs[2 * nb + 2], refs[-1]
        (x, y, c), peers = _gather_peers()
        for kk, peer in enumerate(peers):
            for b in range(nb):
                pltpu.make_async_remote_copy(
                    src_ref=x_refs[b], dst_ref=out_refs[b].at[4 * x + 2 * y + c], send_sem=send_sems.at[4 * b + kk],
                    recv_sem=recv_sems.at[4 * b + kk], device_id=peer, device_id_type=MESH).start()
        token[...] = jnp.zeros_like(token)

    gathers = [pltpu.with_memory_space_constraint(lax.empty((N_DEV,) + b.shape, b.dtype), pltpu.HBM) for b in blocks]
    outs = pl.pallas_call(
        body, name=name,
        out_shape=(pltpu.SemaphoreType.DMA((4 * nb,)), pltpu.SemaphoreType.DMA((4 * nb,)),
                   *[pltpu.HBM(b.shape, b.dtype) for b in blocks], *[pltpu.HBM(b.shape, b.dtype) for b in gathers],
                   jax.ShapeDtypeStruct((SUBLANE, LANE), F32)),
        in_specs=[hbm] * (2 * nb) + [pl.BlockSpec(memory_space=pl.ANY)],
        out_specs=(sem, sem, *[hbm] * (2 * nb), pl.BlockSpec(memory_space=pltpu.VMEM)),
        input_output_aliases={i: 2 + i for i in range(2 * nb)},
        compiler_params=pltpu.CompilerParams(has_side_effects=pltpu.SideEffectType.DATAFLOW_SIDE_EFFECTING),
    )(*[pltpu.with_memory_space_constraint(b, pltpu.HBM) for b in blocks], *gathers, after)
    return outs[0], outs[1], list(outs[2:2 + nb]), list(outs[2 + nb:2 + 2 * nb]), outs[-1]


def _gather_wait(send_sems, recv_sems, thru, gathers, after, name):
    nb = len(thru)
    hbm, sem = pl.BlockSpec(memory_space=pltpu.HBM), pl.BlockSpec(memory_space=pltpu.SEMAPHORE)

    def body(*refs):
        x_refs, out_refs = refs[:nb], refs[nb:2 * nb]
        send_sems_, recv_sems_ = refs[2 * nb], refs[2 * nb + 1]
        _, peers = _gather_peers()
        for kk, (px, py, pc) in enumerate(peers):
            for b in range(nb):
                cp = pltpu.make_async_remote_copy(
                    src_ref=x_refs[b], dst_ref=out_refs[b].at[4 * px + 2 * py + pc], send_sem=send_sems_.at[4 * b + kk],
                    recv_sem=recv_sems_.at[4 * b + kk], device_id=(px, py, pc), device_id_type=MESH)
                cp.wait_send()
                cp.wait_recv()

    outs = pl.pallas_call(
        body, name=name,
        out_shape=(*[pltpu.HBM(b.shape, b.dtype) for b in thru], *[pltpu.HBM(b.shape, b.dtype) for b in gathers]),
        in_specs=[hbm] * (2 * nb) + [sem, sem, pl.BlockSpec(memory_space=pl.ANY)], out_specs=[hbm] * (2 * nb),
        input_output_aliases={i: i for i in range(2 * nb)},
        compiler_params=pltpu.CompilerParams(has_side_effects=pltpu.SideEffectType.DATAFLOW_SIDE_EFFECTING),
    )(*thru, *gathers, send_sems, recv_sems, after)
    return list(outs[:nb]), list(outs[nb:])


def _gather_finish(blocks, gathers, name):
    nb = len(blocks)

    def body(*refs):
        x_refs, in_refs, out_refs = refs[:nb], refs[nb:2 * nb], refs[2 * nb:3 * nb]
        send_sems, recv_sems = refs[3 * nb:]
        (x, y, c), peers = _gather_peers()
        copies = []
        for j, (px, py, _) in enumerate(peers[1:]):
            for b in range(nb):
                copies.append(pltpu.make_async_remote_copy(
                    src_ref=in_refs[b].at[4 * px + 2 * py + c], dst_ref=out_refs[b].at[4 * px + 2 * py + c],
                    send_sem=send_sems.at[b, j], recv_sem=recv_sems.at[b, j], device_id=(x, y, 1 - c),
                    device_id_type=MESH))
                copies[-1].start()
        for j, (px, py, _) in enumerate(peers[1:]):
            for b in range(nb):
                pltpu.make_async_remote_copy(
                    src_ref=in_refs[b].at[4 * px + 2 * py + c], dst_ref=out_refs[b].at[4 * px + 2 * py + 1 - c],
                    send_sem=send_sems.at[b, j], recv_sem=recv_sems.at[b, j], device_id=(x, y, 1 - c),
                    device_id_type=MESH).wait_recv()
        for cp in copies:
            cp.wait_send()

    any_spec = pl.BlockSpec(memory_space=pl.ANY)
    outs = pl.pallas_call(
        body, name=name, out_shape=[jax.ShapeDtypeStruct(b.shape, b.dtype) for b in gathers],
        in_specs=[any_spec] * (2 * nb), out_specs=[any_spec] * nb,
        input_output_aliases={nb + i: i for i in range(nb)},
        scratch_shapes=[pltpu.SemaphoreType.DMA((nb, 3)), pltpu.SemaphoreType.DMA((nb, 3))],
    )(*blocks, *gathers)
    return [_put_own(o, b[None], _my_slot()) for o, b in zip(outs, blocks)]


def _sibling_swap(bufs, name):
    nb = len(bufs)

    def body(*refs):
        x_refs, out_refs = refs[:nb], refs[nb:2 * nb]
        send_sems, recv_sems = refs[2 * nb:]
        x, y, c = _coords()
        copies = [pltpu.make_async_remote_copy(
            src_ref=x_refs[b].at[2 * q + 1 - c], dst_ref=out_refs[b].at[q], send_sem=send_sems.at[b, q],
            recv_sem=recv_sems.at[b, q], device_id=(x, y, 1 - c), device_id_type=MESH)
            for b in range(nb) for q in range(4)]
        for cp in copies:
            cp.start()
        for cp in copies:
            cp.wait()

    return _hbm_call(body, name, bufs, [((4,) + b.shape[1:], b.dtype) for b in bufs],
                     [pltpu.SemaphoreType.DMA((nb, 4)), pltpu.SemaphoreType.DMA((nb, 4))])


def _chip_exchange_start(bufs, name):
    nb = len(bufs)
    hbm, sem = pl.BlockSpec(memory_space=pltpu.HBM), pl.BlockSpec(memory_space=pltpu.SEMAPHORE)

    def body(*refs):
        x_refs, land_refs = refs[:nb], refs[nb:2 * nb]
        send_sems, recv_sems, token = refs[2 * nb], refs[2 * nb + 1], refs[-1]
        x, y, c = _coords()
        for j, (px, py) in enumerate([(1 - x, y), (x, 1 - y), (1 - x, 1 - y)]):
            for b in range(nb):
                pltpu.make_async_remote_copy(
                    src_ref=x_refs[b].at[2 * px + py], dst_ref=land_refs[b].at[2 * x + y], send_sem=send_sems.at[3 * b + j],
                    recv_sem=recv_sems.at[3 * b + j], device_id=(px, py, c), device_id_type=MESH).start()
        token[...] = jnp.zeros_like(token)

    lands = [pltpu.with_memory_space_constraint(lax.empty(b.shape, b.dtype), pltpu.HBM) for b in bufs]
    outs = pl.pallas_call(
        body, name=name,
        out_shape=(pltpu.SemaphoreType.DMA((3 * nb,)), pltpu.SemaphoreType.DMA((3 * nb,)),
                   *[pltpu.HBM(b.shape, b.dtype) for b in bufs], *[pltpu.HBM(b.shape, b.dtype) for b in bufs],
                   jax.ShapeDtypeStruct((SUBLANE, LANE), F32)),
        in_specs=[hbm] * (2 * nb), out_specs=(sem, sem, *[hbm] * (2 * nb), pl.BlockSpec(memory_space=pltpu.VMEM)),
        input_output_aliases={i: 2 + i for i in range(2 * nb)},
        compiler_params=pltpu.CompilerParams(has_side_effects=pltpu.SideEffectType.DATAFLOW_SIDE_EFFECTING),
    )(*[pltpu.with_memory_space_constraint(b, pltpu.HBM) for b in bufs], *lands)
    return outs[0], outs[1], list(outs[2:2 + nb]), list(outs[2 + nb:2 + 2 * nb]), outs[-1]


def _chip_exchange_wait(send_sems, recv_sems, thru, lands, after, name):
    nb = len(thru)
    hbm, sem = pl.BlockSpec(memory_space=pltpu.HBM), pl.BlockSpec(memory_space=pltpu.SEMAPHORE)

    def body(*refs):
        x_refs, land_refs = refs[:nb], refs[nb:2 * nb]
        send_sems_, recv_sems_ = refs[2 * nb], refs[2 * nb + 1]
        x, y, c = _coords()
        for j, (px, py) in enumerate([(1 - x, y), (x, 1 - y), (1 - x, 1 - y)]):
            for b in range(nb):
                cp = pltpu.make_async_remote_copy(
                    src_ref=x_refs[b].at[2 * px + py], dst_ref=land_refs[b].at[2 * px + py],
                    send_sem=send_sems_.at[3 * b + j], recv_sem=recv_sems_.at[3 * b + j], device_id=(px, py, c),
                    device_id_type=MESH)
                cp.wait_send()
                cp.wait_recv()

    outs = pl.pallas_call(
        body, name=name,
        out_shape=(*[pltpu.HBM(b.shape, b.dtype) for b in thru], *[pltpu.HBM(b.shape, b.dtype) for b in lands]),
        in_specs=[hbm] * (2 * nb) + [sem, sem, pl.BlockSpec(memory_space=pl.ANY)], out_specs=[hbm] * (2 * nb),
        input_output_aliases={i: i for i in range(2 * nb)},
        compiler_params=pltpu.CompilerParams(has_side_effects=pltpu.SideEffectType.DATAFLOW_SIDE_EFFECTING),
    )(*thru, *lands, send_sems, recv_sems, after)
    return list(outs[:nb]), list(outs[nb:])


def _chip_sum(g, r1, c, name):
    _, r, cw = g.shape
    tr = _pick(r, (256, 176, 128))

    def body(c_ref, g_ref, r_ref, o_ref):
        o_ref[...] = (g_ref[...] + r_ref[...]).astype(BF16)

    grid_spec = pltpu.PrefetchScalarGridSpec(
        num_scalar_prefetch=1, grid=(4, r // tr),
        in_specs=[_BS((None, None, tr, cw), lambda q, i, c_ref: (q, c_ref[0], i, 0)),
                  _BS((None, tr, cw), lambda q, i, c_ref: (q, i, 0))],
        out_specs=_BS((None, tr, cw), lambda q, i, c_ref: (q, i, 0)))
    return pl.pallas_call(
        body, name=name, grid_spec=grid_spec, out_shape=jax.ShapeDtypeStruct((4, r, cw), BF16),
        compiler_params=_params(("parallel", "parallel")),
    )(c.reshape(1).astype(jnp.int32), g.reshape(4, 2, r, cw), r1)


def _adamw_math(w, g, m, v):
    m = B1 * m + (1.0 - B1) * g
    v = B2 * v + (1.0 - B2) * (g * g)
    m_hat = m / (1.0 - B1 ** STEP)
    v_hat = v / (1.0 - B2 ** STEP)
    return -LR * (m_hat / (jnp.sqrt(v_hat) + AEPS) + WD * w), m, v


def _sum_adamw(parts, w, m, v, name):
    r, c = w.shape
    tr = _pick(r, (256, 176, 128))

    def body(p0, p1, p2, p3, w_ref, m_ref, v_ref, g_out, d_out, m_out, v_out):
        g = ((p0[...].astype(F32) + p1[...].astype(F32)) + p2[...].astype(F32)) + p3[...].astype(F32)
        g_out[...] = g
        d_out[...], m_out[...], v_out[...] = _adamw_math(w_ref[...], g, m_ref[...], v_ref[...])

    part = lambda q: _BS((None, tr, c), functools.partial(lambda i, q: (q, i, 0), q=q))
    plain = _BS((tr, c), lambda i: (i, 0))
    return pl.pallas_call(
        body, name=name, grid=(r // tr,), in_specs=[part(q) for q in range(4)] + [plain] * 3,
        out_specs=[plain] * 4, out_shape=[jax.ShapeDtypeStruct((r, c), F32)] * 4,
        compiler_params=_params(("parallel",)),
    )(parts, parts, parts, parts, w, m, v)


def _small_update(gathered, w, m, v):
    r = w.shape[0]

    def body(ga_ref, w_ref, m_ref, v_ref, g_out, d_out, m_out, v_out):
        g = ga_ref[0]
        for dev in range(1, N_DEV):
            g = g + ga_ref[dev]
        g_out[...] = g
        d_out[...], m_out[...], v_out[...] = _adamw_math(w_ref[...], g, m_ref[...], v_ref[...])

    return pl.pallas_call(
        body, name="small_update", out_shape=[jax.ShapeDtypeStruct((r, LANE), F32)] * 4,
    )(gathered, w, m, v)


def _pack_small(vals):
    rows = []
    for name, (r, n) in SMALL:
        flat = vals[name].reshape(-1)
        pad = (-flat.shape[0]) % (SUBLANE * LANE)
        rows.append(jnp.pad(flat, (0, pad)).reshape(-1, LANE))
    return jnp.concatenate(rows, axis=0)


def _unpack_small(packed):
    out, off = {}, 0
    for name, (r, n) in SMALL:
        nrow = -(-(r * n) // (SUBLANE * LANE)) * SUBLANE
        out[name] = packed[off:off + nrow].reshape(-1)[:r * n].reshape(r, n)
        off += nrow
    return out


def kernel(x, positions, ffn1_norm, ffn1_w_in, ffn1_w_out, mix_norm, w_in, hg_lb_table, hg_out_norm, w_hg_branch, mla_q_lora_norm, w_q_up, mla_kv_lora_norm, w_kv_up, q_head_norm, k_head_norm, w_mla_branch, w_merge, b_merge, w_out, ffn2_norm, ffn2_w_in, ffn2_w_out, final_norm, loss_target, m_ffn1_norm, m_ffn1_w_in, m_ffn1_w_out, m_mix_norm, m_w_in, m_hg_lb_table, m_hg_out_norm, m_w_hg_branch, m_mla_q_lora_norm, m_w_q_up, m_mla_kv_lora_norm, m_w_kv_up, m_q_head_norm, m_k_head_norm, m_w_mla_branch, m_w_merge, m_b_merge, m_w_out, m_ffn2_norm, m_ffn2_w_in, m_ffn2_w_out, m_final_norm, v_ffn1_norm, v_ffn1_w_in, v_ffn1_w_out, v_mix_norm, v_w_in, v_hg_lb_table, v_hg_out_norm, v_w_hg_branch, v_mla_q_lora_norm, v_w_q_up, v_mla_kv_lora_norm, v_w_kv_up, v_q_head_norm, v_k_head_norm, v_w_mla_branch, v_w_merge, v_b_merge, v_w_out, v_ffn2_norm, v_ffn2_w_in, v_ffn2_w_out, v_final_norm):
    args = dict(locals())
    t = x.shape[1]
    big_w = {n: args[n][0] for n, _, _ in BIG}
    small = {n: args[n].reshape(shape) for n, shape in SMALL}

    names = [n for n, _, _ in BIG]
    first, rest = names[:2], names[2:]
    full = dict(zip(first, _all_gather([big_w[n].astype(BF16) for n in first], "weights_all_gather_ffn1")))
    g_send, g_recv, g_thru, g_bufs, g_token = _gather_start([big_w[n].astype(BF16) for n in rest], full[first[0]],
                                                            "weights_gather_start")
    gains = dict(small, ffn1_norm=small["ffn1_norm"] + g_token[0:1, 0:1])

    def late_weights(after):
        blocks, bufs = _gather_wait(g_send, g_recv, g_thru, g_bufs, after, "weights_gather_wait")
        return dict(zip(rest, _gather_finish(blocks, bufs, "weights_gather_finish")))

    inv_freq = ROPE_THETA ** (-jnp.arange(0, ROPE, 2, dtype=F32) / ROPE)
    ang = positions[0].astype(F32)[:, None] * inv_freq
    cs = jnp.concatenate([jnp.cos(ang), jnp.cos(ang)], axis=1)
    sn = jnp.concatenate([jnp.sin(ang), jnp.sin(ang)], axis=1)

    c = lax.axis_index("c")
    chip = 2 * lax.axis_index("x") + lax.axis_index("y")
    early = {}

    def chip_sums_of(g, ns, tag):
        from_sibling = _sibling_swap([g[n] for n in ns], "grads_sibling_swap_" + tag)
        return [_chip_sum(g[n], r1, c, "chip_sum_" + n) for n, r1 in zip(ns, from_sibling)]

    def early_grads(g):
        early["names"] = [n for n in names if n in g]
        early["sums"] = chip_sums_of(g, early["names"], "early")
        early["send"], early["recv"], early["thru"], early["lands"], token = _chip_exchange_start(
            early["sums"], "grads_exchange_start")
        return token

    loss_row, grad_x, g = _local_step(x[0], loss_target[0], cs, sn, full, late_weights, gains, early_grads)
    late = [n for n in names if n not in early["names"]]
    l_send, l_recv, l_thru, l_lands, l_token = _chip_exchange_start(chip_sums_of(g, late, "late"),
                                                                    "grads_exchange_late_start")
    sent, landed = _chip_exchange_wait(early["send"], early["recv"], early["thru"], early["lands"], l_token,
                                       "grads_exchange_wait")
    exchanged = {n: _put_own(land, lax.dynamic_index_in_dim(own, chip, 0), chip)
                 for n, land, own in zip(early["names"], landed, sent)}

    small_packed = jnp.concatenate([_pack_small(g), jnp.pad(loss_row, ((0, SUBLANE - 1), (0, 0)))], axis=0)
    small_all = _all_gather([small_packed], "small_all_gather")[0]
    zero_tail = jnp.zeros((SUBLANE, LANE), F32)
    pk = lambda d: jnp.concatenate([_pack_small(d), zero_tail], axis=0)
    sg, sd, sm, sv = _small_update(
        small_all, pk(small), pk({n: args["m_" + n].reshape(shape) for n, shape in SMALL}),
        pk({n: args["v_" + n].reshape(shape) for n, shape in SMALL}))
    n_small_rows = _pack_small(small).shape[0]
    loss = sg[n_small_rows, 0]
    outs = {k_: _unpack_small(a) for k_, a in (("grad", sg), ("delta", sd), ("new_m", sm), ("new_v", sv))}

    def adamw(n):
        outs["grad"][n], outs["delta"][n], outs["new_m"][n], outs["new_v"][n] = _sum_adamw(
            exchanged[n], big_w[n], args["m_" + n][0], args["v_" + n][0], "adamw_" + n)

    for n in early["names"]:
        adamw(n)
    sent, landed = _chip_exchange_wait(l_send, l_recv, l_thru, l_lands, outs["delta"][early["names"][-1]],
                                       "grads_exchange_late_wait")
    for n, land, own in zip(late, landed, sent):
        exchanged[n] = _put_own(land, lax.dynamic_index_in_dim(own, chip, 0), chip)
        adamw(n)

    def shaped(kind, n):
        return outs[kind][n].reshape(args[n].shape)

    return (loss, grad_x[None], *[shaped("grad", n) for n in WEIGHT_ORDER], *[shaped("delta", n) for n in WEIGHT_ORDER],
            *[shaped("new_m", n) for n in WEIGHT_ORDER], *[shaped("new_v", n) for n in WEIGHT_ORDER])
```

```python
import functools

import jax
import jax.numpy as jnp
from jax import lax
from jax.experimental import pallas as pl
from jax.experimental.pallas import tpu as pltpu

F32 = jnp.float32
BF16 = jnp.bfloat16

D = 1024
FF = 2816
NH = 8
HD = 128
ROPE = 64
QK = HD + ROPE
QL = 384
KVL = 256
HGW = NH * HD
CHUNK = 64
EPS = 1e-6
ROPE_THETA = 10000.0
SCALE = QK ** -0.5
LOG2E = 1.4426950408889634

LR, B1, B2, AEPS, WD, STEP = 0.001, 0.9, 0.999, 1e-08, 0.01, 10

HB = 128
SUB = 16
EXP_CLAMP = 80.0
ATT_TILES = (512, 256, 128)
ATT_Q = (1024, 512, 256, 128)
ATT_KEY_TILES = 1
ROW_TILES = (1024, 512, 256, 128)

LANE = 128
SUBLANE = 8
VMEM_LIMIT = 56 << 20

N_DEV = 8
MESH = pl.DeviceIdType.MESH

BIG = (
    ("ffn1_w_in", (D, 2 * FF), 1), ("ffn1_w_out", (FF, D), 0), ("w_in", (D, 4800), 1),
    ("w_hg_branch", (HGW, D), 0), ("w_q_up", (QL, NH * QK), 1), ("w_kv_up", (KVL, NH * 2 * HD), 1),
    ("w_mla_branch", (NH * HD, D), 0), ("w_merge", (D, 2 * D), 1), ("w_out", (D, D), 0),
    ("ffn2_w_in", (D, 2 * FF), 1), ("ffn2_w_out", (FF, D), 0),
)
SMALL = (
    ("ffn1_norm", (1, D)), ("mix_norm", (1, D)), ("hg_lb_table", (2, HGW)), ("hg_out_norm", (1, HD)),
    ("mla_q_lora_norm", (1, QL)), ("mla_kv_lora_norm", (1, KVL)), ("q_head_norm", (1, QK)),
    ("k_head_norm", (1, QK)), ("b_merge", (1, 2 * D)), ("ffn2_norm", (1, D)), ("final_norm", (1, D)),
)
GRAD_T = ("ffn1_w_in", "ffn2_w_in", "w_in", "w_q_up")
WEIGHT_ORDER = ("ffn1_norm", "ffn1_w_in", "ffn1_w_out", "mix_norm", "w_in", "hg_lb_table", "hg_out_norm",
                "w_hg_branch", "mla_q_lora_norm", "w_q_up", "mla_kv_lora_norm", "w_kv_up", "q_head_norm",
                "k_head_norm", "w_mla_branch", "w_merge", "b_merge", "w_out", "ffn2_norm", "ffn2_w_in",
                "ffn2_w_out", "final_norm")


def _pick(n, cands):
    for c in cands:
        if n % c == 0:
            return c
    return n


def _params(sem):
    return pltpu.CompilerParams(dimension_semantics=sem, vmem_limit_bytes=VMEM_LIMIT)


def _sig(x):
    return 0.5 * jnp.tanh(0.5 * x) + 0.5


def _dot(a, b):
    return jnp.dot(a.astype(BF16), b.astype(BF16), preferred_element_type=F32)


def _dot_nt(a, b):
    return lax.dot_general(a.astype(BF16), b.astype(BF16), (((1,), (1,)), ((), ())),
                           preferred_element_type=F32)


def _dot_tn(a, b):
    return lax.dot_general(a.astype(BF16), b.astype(BF16), (((0,), (0,)), ((), ())),
                           preferred_element_type=F32)


def _split3(x):
    x1 = x.astype(BF16)
    r1 = x - x1.astype(F32)
    x2 = r1.astype(BF16)
    x3 = (r1 - x2.astype(F32)).astype(BF16)
    return x1, x2, x3


def _dot_sel(m, x):
    x1, x2, x3 = _split3(x)
    d = lambda p: jnp.dot(m, p, preferred_element_type=F32)
    return d(x1) + d(x2) + d(x3)


def _sel_dot(x, m):
    x1, x2, x3 = _split3(x)
    d = lambda p: jnp.dot(p, m, preferred_element_type=F32)
    return d(x1) + d(x2) + d(x3)


_TN = (1408, 1024, 768, 512, 384, 256, 128)


def _mm(a, b, mode, name, out_dtype=F32):
    if mode == "tn":
        t, m = a.shape
        n = b.shape[1]
        tt, tm, tn = _pick(t, (512, 256, 128)), _pick(m, _TN), _pick(n, _TN)

        def body(a_ref, b_ref, o_ref):
            @pl.when(pl.program_id(2) == 0)
            def _():
                o_ref[...] = jnp.zeros_like(o_ref)

            o_ref[...] += _dot_tn(a_ref[...], b_ref[...])

        return pl.pallas_call(
            body, name=name, grid=(m // tm, n // tn, t // tt),
            in_specs=[pl.BlockSpec((tt, tm), lambda i, j, k: (k, i)),
                      pl.BlockSpec((tt, tn), lambda i, j, k: (k, j))],
            out_specs=pl.BlockSpec((tm, tn), lambda i, j, k: (i, j)),
            out_shape=jax.ShapeDtypeStruct((m, n), F32),
            compiler_params=_params(("parallel", "parallel", "arbitrary")),
        )(a, b)

    m, k = a.shape
    tm = _pick(m, ROW_TILES)
    if mode == "nn":
        n = b.shape[1]
        tn = _pick(n, _TN)
        b_spec = pl.BlockSpec((k, tn), lambda i, j: (0, j))
        dot = _dot
    else:
        n = b.shape[0]
        tn = _pick(n, _TN if k <= 4096 else (512, 256, 128))
        b_spec = pl.BlockSpec((tn, k), lambda i, j: (j, 0))
        dot = _dot_nt

    def body(a_ref, b_ref, o_ref):
        o_ref[...] = dot(a_ref[...], b_ref[...]).astype(o_ref.dtype)

    return pl.pallas_call(
        body, name=name, grid=(m // tm, n // tn),
        in_specs=[pl.BlockSpec((tm, k), lambda i, j: (i, 0)), b_spec],
        out_specs=pl.BlockSpec((tm, tn), lambda i, j: (i, j)),
        out_shape=jax.ShapeDtypeStruct((m, n), out_dtype),
        compiler_params=_params(("parallel", "parallel")),
    )(a, b)


_DOTS = {"nn": _dot, "nt": _dot_nt, "tn": _dot_tn}
_BS = pl.BlockSpec


def _mmcall(name, kind, a, b, a_spec, b_spec, o_spec, o_shape, grid, red_axis=None, out_dtype=F32):
    dot = _DOTS[kind]

    def body(a_ref, b_ref, o_ref):
        if red_axis is None:
            o_ref[...] = dot(a_ref[...], b_ref[...]).astype(o_ref.dtype)
        else:
            @pl.when(pl.program_id(red_axis) == 0)
            def _():
                o_ref[...] = jnp.zeros_like(o_ref)

            o_ref[...] += dot(a_ref[...], b_ref[...])

    sem = tuple("arbitrary" if ax == red_axis else "parallel" for ax in range(len(grid)))
    return pl.pallas_call(
        body, name=name, grid=grid, in_specs=[a_spec, b_spec], out_specs=o_spec,
        out_shape=jax.ShapeDtypeStruct(o_shape, out_dtype), compiler_params=_params(sem),
    )(a, b)


class _AsRef:
    def __init__(self, value):
        self.value = value

    def __getitem__(self, idx):
        return self.value


def _mm_stack_red(a, w, name, kind, epilogue=None):
    s, t, n = a.shape
    nout = w.shape[2] if kind == "nn" else w.shape[1]
    if epilogue is None:
        tm = _pick(t, ROW_TILES)
        return _mmcall(name, kind, a, w, _BS((None, tm, n), lambda i, j: (j, i, 0)),
                       _BS((None,) + w.shape[1:], lambda i, j: (j, 0, 0)), _BS((tm, nout), lambda i, j: (i, 0)),
                       (t, nout), (t // tm, s), red_axis=1)
    fn, first, rows, vecs, outs, accs = epilogue
    rows = [first] + list(rows)
    tm = _pick(t, ROW_TILES)
    n_row, n_vec, n_out = len(rows), len(vecs), len(outs)
    dot = _DOTS[kind]

    def body(*refs):
        a_ref, w_ref, prod = refs[0], refs[1], refs[-1]
        row_refs = refs[2:2 + n_row]
        vec_refs = refs[2 + n_row:2 + n_row + n_vec]
        out_refs = refs[2 + n_row + n_vec:2 + n_row + n_vec + n_out]
        acc_refs = refs[2 + n_row + n_vec + n_out:-1]
        i, j = pl.program_id(0), pl.program_id(1)

        @pl.when(j == 0)
        def _():
            prod[...] = jnp.zeros_like(prod)

        prod[...] += dot(a_ref[...], w_ref[...])

        @pl.when(j == s - 1)
        def _():
            res = fn(row_refs[0], _AsRef(prod[...]), *row_refs[1:], *vec_refs, *out_refs)
            if acc_refs:
                @pl.when(i == 0)
                def _():
                    for r in acc_refs:
                        r[...] = jnp.zeros_like(r)

                for r, val in zip(acc_refs, res):
                    r[...] += val

    in_specs = [_BS((None, tm, n), lambda i, j: (j, i, 0)), _BS((None,) + w.shape[1:], lambda i, j: (j, 0, 0))]
    in_specs += [_BS((tm, r.shape[1]), lambda i, j: (i, 0)) for r in rows]
    in_specs += [_BS(v.shape, lambda i, j: (0, 0)) for v in vecs]
    out_specs = [_BS((tm, wd), lambda i, j: (i, 0)) for wd, _ in outs] + [_BS(sh, lambda i, j: (0, 0)) for sh in accs]
    out_shape = [jax.ShapeDtypeStruct((t, wd), dt) for wd, dt in outs] + [jax.ShapeDtypeStruct(sh, F32) for sh in accs]
    return pl.pallas_call(
        body, name=name, grid=(t // tm, s), in_specs=in_specs, out_specs=out_specs, out_shape=out_shape,
        scratch_shapes=[pltpu.VMEM((tm, nout), F32)],
        compiler_params=_params(("arbitrary" if accs else "parallel", "arbitrary")),
    )(a, w, *rows, *vecs)


def _mm_stack_tn(a, b, name):
    grp = 4
    if a.ndim == 2:
        t, k = a.shape
        s, _, n = b.shape
        tt = _pick(t, (512, 256, 128))
        a_spec = _BS((tt, k), lambda j, r: (r, 0))
        b_spec, b_in = _BS((None, grp, tt, n), lambda j, r: (j, 0, r, 0)), b.reshape(s // grp, grp, t, n)
        a_in = a
    else:
        s, t, k = a.shape
        n = b.shape[1]
        tt = _pick(t, (512, 256, 128))
        a_spec, a_in = _BS((None, grp, tt, k), lambda j, r: (j, 0, r, 0)), a.reshape(s // grp, grp, t, k)
        b_spec, b_in = _BS((tt, n), lambda j, r: (r, 0)), b

    def body(a_ref, b_ref, o_ref):
        @pl.when(pl.program_id(1) == 0)
        def _():
            o_ref[...] = jnp.zeros_like(o_ref)

        shared = a_ref[...] if a.ndim == 2 else b_ref[...]
        for e in range(grp):
            o_ref[e] += _dot_tn(shared, b_ref[e]) if a.ndim == 2 else _dot_tn(a_ref[e], shared)

    return pl.pallas_call(
        body, name=name, grid=(s // grp, t // tt), in_specs=[a_spec, b_spec],
        out_specs=_BS((None, grp, k, n), lambda j, r: (j, 0, 0, 0)),
        out_shape=jax.ShapeDtypeStruct((s // grp, grp, k, n), F32),
        compiler_params=_params(("parallel", "arbitrary")),
    )(a_in, b_in).reshape(s, k, n)


def _cols_fwd(x, w, name):
    t, k = x.shape
    s, _, n = w.shape
    tm = _pick(t, ROW_TILES)

    def body(x_ref, w_ref, o_ref):
        x_ = x_ref[...]
        for j in range(s):
            o_ref[:, n * j:n * (j + 1)] = _dot(x_, w_ref[j])

    return pl.pallas_call(
        body, name=name, grid=(t // tm,),
        in_specs=[_BS((tm, k), lambda i: (i, 0)), _BS((s, k, n), lambda i: (0, 0, 0))],
        out_specs=_BS((tm, s * n), lambda i: (i, 0)), out_shape=jax.ShapeDtypeStruct((t, s * n), F32),
        compiler_params=_params(("parallel",)),
    )(x, w)


def _cols_dx(d, w, name):
    t = d.shape[0]
    s, k, n = w.shape
    tm = _pick(t, ROW_TILES)

    def body(d_ref, w_ref, o_ref):
        acc = _dot_nt(d_ref[:, 0:n], w_ref[0])
        for j in range(1, s):
            acc = acc + _dot_nt(d_ref[:, n * j:n * (j + 1)], w_ref[j])
        o_ref[...] = acc

    return pl.pallas_call(
        body, name=name, grid=(t // tm,),
        in_specs=[_BS((tm, s * n), lambda i: (i, 0)), _BS((s, k, n), lambda i: (0, 0, 0))],
        out_specs=_BS((tm, k), lambda i: (i, 0)), out_shape=jax.ShapeDtypeStruct((t, k), F32),
        compiler_params=_params(("parallel",)),
    )(d, w)


def _cols_dw(x, d, n, name):
    t, k = x.shape
    s = d.shape[1] // n
    tt = _pick(t, (512, 256, 128))

    def body(x_ref, d_ref, o_ref):
        @pl.when(pl.program_id(0) == 0)
        def _():
            o_ref[...] = jnp.zeros_like(o_ref)

        x_ = x_ref[...]
        for j in range(s):
            o_ref[j] += _dot_tn(x_, d_ref[:, n * j:n * (j + 1)])

    return pl.pallas_call(
        body, name=name, grid=(t // tt,),
        in_specs=[_BS((tt, k), lambda r: (r, 0)), _BS((tt, s * n), lambda r: (r, 0))],
        out_specs=_BS((s, k, n), lambda r: (0, 0, 0)), out_shape=jax.ShapeDtypeStruct((s, k, n), F32),
        compiler_params=_params(("arbitrary",)),
    )(x, d)


def _rows(fn, name, t, tm, ins, vecs, outs, accs=()):
    n_in, n_out, n_acc = len(ins) + len(vecs), len(outs), len(accs)

    def body(*refs):
        res = fn(*refs[:n_in + n_out])
        if n_acc:
            acc_refs = refs[n_in + n_out:]

            @pl.when(pl.program_id(0) == 0)
            def _():
                for r in acc_refs:
                    r[...] = jnp.zeros_like(r)

            for r, val in zip(acc_refs, res):
                r[...] += val

    in_specs = [pl.BlockSpec((tm, bw), functools.partial(lambda i, cb: (i, cb), cb=cb)) for _, bw, cb in ins]
    in_specs += [pl.BlockSpec(v.shape, lambda i: (0, 0)) for v in vecs]
    out_specs = [pl.BlockSpec((tm, w), lambda i: (i, 0)) for w, _ in outs]
    out_specs += [pl.BlockSpec(s, lambda i: (0, 0)) for s in accs]
    out_shape = [jax.ShapeDtypeStruct((t, w), dt) for w, dt in outs]
    out_shape += [jax.ShapeDtypeStruct(s, F32) for s in accs]
    return pl.pallas_call(
        body, name=name, grid=(t // tm,), in_specs=in_specs, out_specs=out_specs, out_shape=out_shape,
        compiler_params=_params(("arbitrary",) if n_acc else ("parallel",)),
    )(*[a for a, _, _ in ins], *vecs)


def _rms(x, g):
    return x * lax.rsqrt(jnp.mean(x * x, axis=-1, keepdims=True) + EPS) * g


def _rms_bwd(x, g, dy):
    xh = x * lax.rsqrt(jnp.mean(x * x, axis=-1, keepdims=True) + EPS)
    r = lax.rsqrt(jnp.mean(x * x, axis=-1, keepdims=True) + EPS)
    dyg = dy * g
    dx = r * (dyg - xh * jnp.mean(dyg * xh, axis=-1, keepdims=True))
    return dx, jnp.sum(dy * xh, axis=0, keepdims=True)


def _hgrn_mats():
    row = lax.broadcasted_iota(jnp.int32, (HB, HB), 0)
    col = lax.broadcasted_iota(jnp.int32, (HB, HB), 1)
    return row, col


def _hgrn_gates(qr, z, t0, t1):
    lb = 1.0 / (1.0 + jnp.exp(t1 - t0))
    th = jnp.tanh(0.5 * z)
    sz, sneg = 0.5 + 0.5 * th, 0.5 - 0.5 * th
    f = lb + (1.0 - lb) * sz
    return lb, sz, sneg, f, jnp.log(f), (1.0 - lb) * sneg, qr * _sig(qr)


def _hgrn_scores(q, k, cum):
    heads, nsub = range(len(q)), HB // SUB
    mids = [[cum[h][SUB * i + SUB // 2 - 1:SUB * i + SUB // 2, :] for i in range(nsub)] for h in heads]
    cmid = [jnp.concatenate([cum[h][SUB * i:SUB * (i + 1)] - mids[h][i] for i in range(nsub)], axis=0) for h in heads]
    qd = [q[h] * jnp.exp(jnp.minimum(cmid[h], EXP_CLAMP)) for h in heads]
    qd_b = [x.astype(BF16) for x in qd]
    kds = [[None] * nsub for _ in heads]
    parts = [[None] * nsub for _ in heads]
    for i in range(nsub):
        for h in heads:
            kds[h][i] = k[h] * jnp.exp(jnp.minimum(mids[h][i] - cum[h], EXP_CLAMP))
        for h in heads:
            parts[h][i] = _dot_nt(qd_b[h][SUB * i:SUB * (i + 1)], kds[h][i])
    return qd, kds, [jnp.concatenate(parts[h], axis=0) for h in heads], cmid


def _hgrn_fwd(p_hg, table, gain, t):
    nblk = t // HB

    def body(q_ref, f_ref, i_ref, g_ref, tab_ref, gain_ref, o_ref, y_ref, st_ref, state):
        @pl.when(pl.program_id(0) == 0)
        def _():
            state[...] = jnp.zeros_like(state)

        row, col = _hgrn_mats()
        causal = col <= row
        tri = causal.astype(BF16)
        heads = range(NH)
        sl = [slice(HD * h, HD * (h + 1)) for h in heads]
        gates = [_hgrn_gates(q_ref[:, sl[h]], f_ref[:, sl[h]], tab_ref[0:1, sl[h]], tab_ref[1:2, sl[h]]) for h in heads]
        lf, k, q = [g[4] for g in gates], [g[5] for g in gates], [g[6] for g in gates]
        v = [i_ref[:, sl[h]] for h in heads]
        cum = [_dot_sel(tri, lf[h]) for h in heads]
        _, _, s, _ = _hgrn_scores(q, k, cum)
        p = [jnp.where(causal, s[h], 0.0) for h in heads]
        st = [state[h] for h in heads]
        o = [_dot(p[h], v[h]) + _dot_nt(q[h] * jnp.exp(cum[h]), st[h]) for h in heads]
        last = [cum[h][HB - 1:HB, :] for h in heads]
        new_st = [st[h] * jnp.exp(last[h]) + _dot_tn(v[h], k[h] * jnp.exp(last[h] - cum[h])) for h in heads]
        for h in heads:
            gr = g_ref[:, sl[h]]
            st_ref[h, 0] = st[h]
            state[h] = new_st[h]
            o_ref[:, sl[h]] = o[h]
            y_ref[:, sl[h]] = (_rms(o[h], gain_ref[...]) * gr * _sig(gr)).astype(BF16)

    blk = lambda cb: pl.BlockSpec((HB, HGW), functools.partial(lambda n, cb: (n, cb), cb=cb))
    return pl.pallas_call(
        body, name="hgrn_fwd", grid=(nblk,),
        in_specs=[blk(0), blk(1), blk(2), blk(3), pl.BlockSpec((2, HGW), lambda n: (0, 0)),
                  pl.BlockSpec((1, HD), lambda n: (0, 0))],
        out_specs=[pl.BlockSpec((HB, HGW), lambda n: (n, 0)), pl.BlockSpec((HB, HGW), lambda n: (n, 0)),
                   pl.BlockSpec((NH, 1, HD, HD), lambda n: (0, n, 0, 0))],
        out_shape=[jax.ShapeDtypeStruct((t, HGW), F32), jax.ShapeDtypeStruct((t, HGW), BF16),
                   jax.ShapeDtypeStruct((NH, nblk, HD, HD), F32)],
        scratch_shapes=[pltpu.VMEM((NH, HD, HD), F32)],
        compiler_params=_params(("arbitrary",)),
    )(p_hg, p_hg, p_hg, p_hg, table, gain)


def _hgrn_bwd(p_hg, table, gain, o_pre, states, dy, t):
    nblk = t // HB

    def body(q_ref, f_ref, i_ref, g_ref, tab_ref, gain_ref, o_ref, st_ref, dy_ref, dp_ref, dtab_ref, dgain_ref,
             dstate):
        @pl.when(pl.program_id(0) == 0)
        def _():
            dstate[...] = jnp.zeros_like(dstate)
            dtab_ref[...] = jnp.zeros_like(dtab_ref)
            dgain_ref[...] = jnp.zeros_like(dgain_ref)

        row, col = _hgrn_mats()
        causal = col <= row
        tri = causal.astype(BF16)
        tri_t = (row <= col).astype(BF16)
        heads, nsub = range(NH), HB // SUB
        sl = [slice(HD * h, HD * (h + 1)) for h in heads]
        rows_of = lambda i: slice(SUB * i, SUB * (i + 1))
        gain_ = gain_ref[...]
        qr, z = [q_ref[:, sl[h]] for h in heads], [f_ref[:, sl[h]] for h in heads]
        v, gr = [i_ref[:, sl[h]] for h in heads], [g_ref[:, sl[h]] for h in heads]
        gates = [_hgrn_gates(qr[h], z[h], tab_ref[0:1, sl[h]], tab_ref[1:2, sl[h]]) for h in heads]
        lb, sz, sneg, f, lf, k, q = ([g[j] for g in gates] for j in range(7))
        cum = [_dot_sel(tri, lf[h]) for h in heads]
        qd, kds, s, cmid = _hgrn_scores(q, k, cum)
        p = [jnp.where(causal, s[h], 0.0) for h in heads]
        st, dst = [st_ref[h, 0] for h in heads], [dstate[h] for h in heads]
        o, dyh, sg = [o_ref[:, sl[h]] for h in heads], [dy_ref[:, sl[h]] for h in heads], [_sig(x) for x in gr]
        dgr = [dyh[h] * _rms(o[h], gain_) * sg[h] * (1.0 + gr[h] * (1.0 - sg[h])) for h in heads]
        norm_bwd = [_rms_bwd(o[h], gain_, dyh[h] * gr[h] * sg[h]) for h in heads]
        do = [x[0] for x in norm_bwd]
        do_b = [x.astype(BF16) for x in do]
        ecum, last = [jnp.exp(x) for x in cum], [x[HB - 1:HB, :] for x in cum]
        qc = [q[h] * ecum[h] for h in heads]
        edec = [jnp.exp(last[h] - cum[h]) for h in heads]
        kdec = [k[h] * edec[h] for h in heads]
        dp = [jnp.where(causal, _dot_nt(do_b[h], v[h]), 0.0) for h in heads]
        dv = [_dot(p[h].T, do_b[h]) + _dot_nt(kdec[h], dst[h]) for h in heads]
        dqc = [_dot(do_b[h], st[h]) for h in heads]
        dkdec = [_dot(v[h], dst[h]) for h in heads]
        new_dst = [dst[h] * jnp.exp(last[h]) + _dot(do[h].T, qc[h]) for h in heads]
        dp_b = [x.astype(BF16) for x in dp]
        dqd = [jnp.concatenate(parts, axis=0) for parts in
               zip(*[[_dot(dp_b[h][rows_of(i)], kds[h][i]) for h in heads] for i in range(nsub)])]
        gq = [dqd[h] * qd[h] for h in heads]
        dq = [dqd[h] * jnp.exp(jnp.minimum(cmid[h], EXP_CLAMP)) + dqc[h] * ecum[h] for h in heads]
        gs = [dkdec[h] * kdec[h] for h in heads]
        dk = [dkdec[h] * edec[h] for h in heads]
        dcum = [dqc[h] * qc[h] - gs[h] + gq[h]
                + jnp.where(row == HB - 1, jnp.sum(gs[h], axis=0, keepdims=True)
                            + jnp.exp(last[h]) * jnp.sum(st[h] * dst[h], axis=0, keepdims=True), 0.0) for h in heads]
        qd_b = [x.astype(BF16) for x in qd]
        for i in range(nsub):
            dkd = [_dot_tn(dp_b[h][rows_of(i)], qd_b[h][rows_of(i)]) for h in heads]
            for h in heads:
                mid = cum[h][SUB * i + SUB // 2 - 1:SUB * i + SUB // 2, :]
                dk[h] = dk[h] + dkd[h] * jnp.exp(jnp.minimum(mid - cum[h], EXP_CLAMP))
                gk = dkd[h] * kds[h][i]
                to_mid = jnp.sum(gk, axis=0, keepdims=True) - jnp.sum(gq[h][rows_of(i)], axis=0, keepdims=True)
                dcum[h] = dcum[h] - gk + jnp.where(row == SUB * i + SUB // 2 - 1, to_mid, 0.0)
        dlf = [_dot_sel(tri_t, dcum[h]) for h in heads]
        dgain = jnp.zeros((1, HD), F32)
        for h in heads:
            df = dlf[h] / f[h] - dk[h]
            dlb = jnp.sum(df * sneg[h], axis=0, keepdims=True) * lb[h] * (1.0 - lb[h])
            dtab_ref[0:1, sl[h]] += dlb
            dtab_ref[1:2, sl[h]] -= dlb
            sq = _sig(qr[h])
            dstate[h] = new_dst[h]
            dp_ref[:, sl[h]] = (dq[h] * sq * (1.0 + qr[h] * (1.0 - sq))).astype(BF16)
            dp_ref[:, HGW + HD * h:HGW + HD * (h + 1)] = (df * (1.0 - lb[h]) * sz[h] * sneg[h]).astype(BF16)
            dp_ref[:, 2 * HGW + HD * h:2 * HGW + HD * (h + 1)] = dv[h].astype(BF16)
            dp_ref[:, 3 * HGW + HD * h:3 * HGW + HD * (h + 1)] = dgr[h].astype(BF16)
            dgain = dgain + norm_bwd[h][1]
        dgain_ref[...] += dgain

    rev = lambda cb: pl.BlockSpec((HB, HGW), functools.partial(lambda n, cb: (nblk - 1 - n, cb), cb=cb))
    return pl.pallas_call(
        body, name="hgrn_bwd", grid=(nblk,),
        in_specs=[rev(0), rev(1), rev(2), rev(3), pl.BlockSpec((2, HGW), lambda n: (0, 0)),
                  pl.BlockSpec((1, HD), lambda n: (0, 0)), rev(0),
                  pl.BlockSpec((NH, 1, HD, HD), lambda n: (0, nblk - 1 - n, 0, 0)), rev(0)],
        out_specs=[pl.BlockSpec((HB, 4 * HGW), lambda n: (nblk - 1 - n, 0)),
                   pl.BlockSpec((2, HGW), lambda n: (0, 0)), pl.BlockSpec((1, HD), lambda n: (0, 0))],
        out_shape=[jax.ShapeDtypeStruct((t, 4 * HGW), BF16), jax.ShapeDtypeStruct((2, HGW), F32),
                   jax.ShapeDtypeStruct((1, HD), F32)],
        scratch_shapes=[pltpu.VMEM((NH, HD, HD), F32)],
        compiler_params=_params(("arbitrary",)),
    )(p_hg, p_hg, p_hg, p_hg, table, gain, o_pre, states, dy)


def _rope_mat():
    r = lax.broadcasted_iota(jnp.int32, (ROPE, ROPE), 0)
    c = lax.broadcasted_iota(jnp.int32, (ROPE, ROPE), 1)
    half = ROPE // 2
    return ((r == c - half).astype(F32) - (r == c + half).astype(F32)).astype(BF16)


def _mla_prep_fwd(p_mla, cs, sn, wq, wkv, gql, gkvl, gq, gk, t):
    tm = _pick(t, (512, 256, 128))

    def body(p_ref, cs_ref, sn_ref, wq_ref, wkv_ref, gql_ref, gkvl_ref, gq_ref, gk_ref,
             q_ref, k_ref, v_ref):
        rmat = _rope_mat()
        cqn = _rms(p_ref[:, 0:QL], gql_ref[...]).astype(BF16)
        ckvn = _rms(p_ref[:, QL:QL + KVL], gkvl_ref[...]).astype(BF16)
        kpe = p_ref[:, QL + KVL:QL + KVL + ROPE]
        c, s = cs_ref[...], sn_ref[...]
        rot = lambda x: x * c + _sel_dot(x, rmat) * s
        heads = range(NH)
        ssq = lambda x: jnp.sum(x * x, -1, keepdims=True)
        qa = [_dot(cqn, wq_ref[h, :, 0:HD]) for h in heads]
        qr = [_dot(cqn, wq_ref[h, :, HD:QK]) for h in heads]
        kn = [_dot(ckvn, wkv_ref[h, :, 0:HD]) for h in heads]
        vv = [_dot(ckvn, wkv_ref[h, :, HD:2 * HD]) for h in heads]
        kpe_ss = ssq(kpe)
        rq = [lax.rsqrt((ssq(qa[h]) + ssq(qr[h])) / QK + EPS) for h in heads]
        rk = [lax.rsqrt((ssq(kn[h]) + kpe_ss) / QK + EPS) for h in heads]
        q_rope = [rot(qr[h] * rq[h] * gq_ref[:, HD:QK]) for h in heads]
        k_rope = [rot(kpe * rk[h] * gk_ref[:, HD:QK]) for h in heads]
        for h in heads:
            q_ref[h, :, 0:HD] = (qa[h] * rq[h] * gq_ref[:, 0:HD] * (SCALE * LOG2E)).astype(BF16)
            q_ref[h, :, HD:QK] = (q_rope[h] * (SCALE * LOG2E)).astype(BF16)
            k_ref[h, :, 0:HD] = (kn[h] * rk[h] * gk_ref[:, 0:HD]).astype(BF16)
            k_ref[h, :, HD:QK] = k_rope[h].astype(BF16)
            v_ref[h] = vv[h].astype(BF16)

    whole = lambda a: pl.BlockSpec(a.shape, functools.partial(lambda i, nd: (0,) * nd, nd=a.ndim))
    return pl.pallas_call(
        body, name="mla_prep_fwd", grid=(t // tm,),
        in_specs=[pl.BlockSpec((tm, QL + KVL + ROPE), lambda i: (i, 0)), pl.BlockSpec((tm, ROPE), lambda i: (i, 0)),
                  pl.BlockSpec((tm, ROPE), lambda i: (i, 0))] + [whole(a) for a in (wq, wkv, gql, gkvl, gq, gk)],
        out_specs=[pl.BlockSpec((NH, tm, QK), lambda i: (0, i, 0)), pl.BlockSpec((NH, tm, QK), lambda i: (0, i, 0)),
                   pl.BlockSpec((NH, tm, HD), lambda i: (0, i, 0))],
        out_shape=[jax.ShapeDtypeStruct((NH, t, QK), BF16), jax.ShapeDtypeStruct((NH, t, QK), BF16),
                   jax.ShapeDtypeStruct((NH, t, HD), BF16)],
        compiler_params=_params(("parallel",)),
    )(p_mla, cs, sn, wq, wkv, gql, gkvl, gq, gk)


def _mla_prep_bwd(p_mla, cs, sn, wq, wkv, gql, gkvl, gq, gk, dq, dk, dv, t):
    tm = _pick(t, (512, 256, 128))

    def body(p_ref, cs_ref, sn_ref, wq_ref, wkv_ref, gql_ref, gkvl_ref, gq_ref, gk_ref,
             dq_ref, dk_ref, dv_ref,
             dp_ref, dwq_ref, dwkv_ref, dgql_ref, dgkvl_ref, dgq_ref, dgk_ref):
        accs = (dwq_ref, dwkv_ref, dgql_ref, dgkvl_ref, dgq_ref, dgk_ref)

        @pl.when(pl.program_id(0) == 0)
        def _():
            for r in accs:
                r[...] = jnp.zeros_like(r)

        rmat = _rope_mat()
        rmat_t = -rmat
        cq, ckv = p_ref[:, 0:QL], p_ref[:, QL:QL + KVL]
        kpe = p_ref[:, QL + KVL:QL + KVL + ROPE]
        cqn_f, ckvn_f = _rms(cq, gql_ref[...]), _rms(ckv, gkvl_ref[...])
        cqn, ckvn = cqn_f.astype(BF16), ckvn_f.astype(BF16)
        ckvn_t = ckvn_f.T.astype(BF16)
        c, s = cs_ref[...], sn_ref[...]
        unrot = lambda dy: dy * c + _sel_dot(dy * s, rmat_t)
        heads = range(NH)
        rsum = lambda x: jnp.sum(x, -1, keepdims=True)
        csum = lambda x: jnp.sum(x, axis=0, keepdims=True)
        qa = [_dot(cqn, wq_ref[h, :, 0:HD]) for h in heads]
        qr = [_dot(cqn, wq_ref[h, :, HD:QK]) for h in heads]
        kn = [_dot(ckvn, wkv_ref[h, :, 0:HD]) for h in heads]
        dyr = [unrot(dq_ref[h, :, HD:QK] * SCALE) for h in heads]
        dkr = [unrot(dk_ref[h, :, HD:QK]) for h in heads]
        rq = [lax.rsqrt((rsum(qa[h] * qa[h]) + rsum(qr[h] * qr[h])) / QK + EPS) for h in heads]
        xa, xr = [qa[h] * rq[h] for h in heads], [qr[h] * rq[h] for h in heads]
        dya = [dq_ref[h, :, 0:HD] * SCALE for h in heads]
        ga, gr_ = [dya[h] * gq_ref[:, 0:HD] for h in heads], [dyr[h] * gq_ref[:, HD:QK] for h in heads]
        mean = [(rsum(ga[h] * xa[h]) + rsum(gr_[h] * xr[h])) / QK for h in heads]
        dqa = [(rq[h] * (ga[h] - xa[h] * mean[h])).astype(BF16) for h in heads]
        dqr = [(rq[h] * (gr_[h] - xr[h] * mean[h])).astype(BF16) for h in heads]
        kpe_ss = rsum(kpe * kpe)
        rk = [lax.rsqrt((rsum(kn[h] * kn[h]) + kpe_ss) / QK + EPS) for h in heads]
        ya, yr = [kn[h] * rk[h] for h in heads], [kpe * rk[h] for h in heads]
        dka = [dk_ref[h, :, 0:HD] for h in heads]
        ha, hr = [dka[h] * gk_ref[:, 0:HD] for h in heads], [dkr[h] * gk_ref[:, HD:QK] for h in heads]
        mean_k = [(rsum(ha[h] * ya[h]) + rsum(hr[h] * yr[h])) / QK for h in heads]
        dkn = [(rk[h] * (ha[h] - ya[h] * mean_k[h])).astype(BF16) for h in heads]
        dvh = [dv_ref[h].astype(BF16) for h in heads]
        dw = [(_dot_tn(dqa[h], cqn), _dot_tn(dqr[h], cqn), _dot(ckvn_t, dkn[h]), _dot(ckvn_t, dvh[h])) for h in heads]
        back_q = [_dot_nt(dqa[h], wq_ref[h, :, 0:HD]) + _dot_nt(dqr[h], wq_ref[h, :, HD:QK]) for h in heads]
        back_kv = [_dot_nt(dkn[h], wkv_ref[h, :, 0:HD]) + _dot_nt(dvh[h], wkv_ref[h, :, HD:2 * HD]) for h in heads]
        dcqn, dckvn = sum(back_q[1:], back_q[0]), sum(back_kv[1:], back_kv[0])
        dkpe = sum([rk[h] * (hr[h] - yr[h] * mean_k[h]) for h in heads][1:], rk[0] * (hr[0] - yr[0] * mean_k[0]))
        dgq_a = sum([csum(dya[h] * xa[h]) for h in heads][1:], csum(dya[0] * xa[0]))
        dgq_r = sum([csum(dyr[h] * xr[h]) for h in heads][1:], csum(dyr[0] * xr[0]))
        dgk_a = sum([csum(dka[h] * ya[h]) for h in heads][1:], csum(dka[0] * ya[0]))
        dgk_r = sum([csum(dkr[h] * yr[h]) for h in heads][1:], csum(dkr[0] * yr[0]))
        for h in heads:
            dwq_ref[h, 0:HD, :] += dw[h][0]
            dwq_ref[h, HD:QK, :] += dw[h][1]
            dwkv_ref[h, :, 0:HD] += dw[h][2]
            dwkv_ref[h, :, HD:2 * HD] += dw[h][3]
        dcq, dg1 = _rms_bwd(cq, gql_ref[...], dcqn)
        dckv, dg2 = _rms_bwd(ckv, gkvl_ref[...], dckvn)
        dp_ref[:, 0:QL] = dcq.astype(BF16)
        dp_ref[:, QL:QL + KVL] = dckv.astype(BF16)
        dp_ref[:, QL + KVL:QL + KVL + ROPE] = dkpe.astype(BF16)
        dgql_ref[...] += dg1
        dgkvl_ref[...] += dg2
        dgq_ref[:, 0:HD] += dgq_a
        dgq_ref[:, HD:QK] += dgq_r
        dgk_ref[:, 0:HD] += dgk_a
        dgk_ref[:, HD:QK] += dgk_r

    whole = lambda a: pl.BlockSpec(a.shape, functools.partial(lambda i, nd: (0,) * nd, nd=a.ndim))
    acc_shapes = [(NH, QK, QL), wkv.shape, gql.shape, gkvl.shape, gq.shape, gk.shape]
    return pl.pallas_call(
        body, name="mla_prep_bwd", grid=(t // tm,),
        in_specs=[pl.BlockSpec((tm, QL + KVL + ROPE), lambda i: (i, 0)), pl.BlockSpec((tm, ROPE), lambda i: (i, 0)),
                  pl.BlockSpec((tm, ROPE), lambda i: (i, 0))]
        + [whole(a) for a in (wq, wkv, gql, gkvl, gq, gk)]
        + [pl.BlockSpec((NH, tm, QK), lambda i: (0, i, 0)), pl.BlockSpec((NH, tm, QK), lambda i: (0, i, 0)),
           pl.BlockSpec((NH, tm, HD), lambda i: (0, i, 0))],
        out_specs=[pl.BlockSpec((tm, QL + KVL + ROPE), lambda i: (i, 0))]
        + [pl.BlockSpec(s, functools.partial(lambda i, nd: (0,) * nd, nd=len(s))) for s in acc_shapes],
        out_shape=[jax.ShapeDtypeStruct((t, QL + KVL + ROPE), BF16)]
        + [jax.ShapeDtypeStruct(s, F32) for s in acc_shapes],
        compiler_params=_params(("arbitrary",)),
    )(p_mla, cs, sn, wq, wkv, gql, gkvl, gq, gk, dq, dk, dv)


def _chunk_mask(nq, nk, key0, keys_on_rows):
    shape = (nk, nq) if keys_on_rows else (nq, nk)
    qi = lax.broadcasted_iota(jnp.int32, shape, 1 if keys_on_rows else 0) // CHUNK
    ki = lax.broadcasted_iota(jnp.int32, shape, 0 if keys_on_rows else 1) // CHUNK + key0 // CHUNK
    return ki <= qi


def _flash_fwd(q, k, v, t):
    tq = _pick(t, ATT_Q)
    tk = tq // ATT_KEY_TILES

    def body(q_ref, k_ref, v_ref, o_ref, lse_ref):
        i = pl.program_id(1)
        qt = q_ref[0]

        def step(j, carry, key0):
            m, l, acc = carry
            cols = pl.ds(pl.multiple_of(j * tk, tk), tk)
            s = _dot_nt(qt, k_ref[0, cols, :])
            if key0 is not None:
                s = jnp.where(_chunk_mask(tq, tk, key0, False), s, -jnp.inf)
            m_new = jnp.maximum(m, jnp.max(s, axis=-1, keepdims=True))
            p = jnp.exp2(s - m_new)
            alpha = jnp.exp2(m - m_new)
            return m_new, alpha * l + jnp.sum(p, axis=-1, keepdims=True), alpha * acc + _dot(p, v_ref[0, cols, :])

        init = (jnp.full((tq, 1), -jnp.inf, F32), jnp.zeros((tq, 1), F32), jnp.zeros((tq, HD), F32))
        carry = lax.fori_loop(0, ATT_KEY_TILES * i, lambda j, cr: step(j, cr, None), init)
        for h in range(ATT_KEY_TILES):
            carry = step(ATT_KEY_TILES * i + h, carry, h * tk)
        m, l, acc = carry
        o_ref[...] = (acc / l).astype(BF16)
        lse_ref[0] = m + jnp.log2(l)

    return pl.pallas_call(
        body, name="flash_fwd", grid=(NH, t // tq),
        in_specs=[pl.BlockSpec((1, tq, QK), lambda h, i: (h, i, 0)), pl.BlockSpec((1, t, QK), lambda h, i: (h, 0, 0)),
                  pl.BlockSpec((1, t, HD), lambda h, i: (h, 0, 0))],
        out_specs=[pl.BlockSpec((tq, HD), lambda h, i: (i, h)), pl.BlockSpec((1, tq, 1), lambda h, i: (h, i, 0))],
        out_shape=[jax.ShapeDtypeStruct((t, NH * HD), BF16), jax.ShapeDtypeStruct((NH, t, 1), F32)],
        compiler_params=_params(("parallel", "parallel")),
    )(q, k, v)


def _attn_out_bwd(dy, w, o, t):
    tm = _pick(t, ATT_TILES)

    def body(dy_ref, w_ref, o_ref, do_ref, delta_ref):
        do = _dot_nt(dy_ref[...], w_ref[...]).astype(BF16)
        do_ref[...] = do
        ones = jnp.ones((SUBLANE, HD), BF16)
        rowsum = lambda p: lax.dot_general(ones, p, (((1,), (1,)), ((), ())), preferred_element_type=F32)
        for h in range(NH):
            sl = slice(HD * h, HD * (h + 1))
            x1, x2, x3 = _split3(do[:, sl].astype(F32) * o_ref[:, sl].astype(F32))
            delta_ref[h] = (rowsum(x1) + rowsum(x2) + rowsum(x3))[0:1, :]

    return pl.pallas_call(
        body, name="attn_out_bwd", grid=(t // tm,),
        in_specs=[_BS((tm, dy.shape[1]), lambda i: (i, 0)), _BS(w.shape, lambda i: (0, 0)),
                  _BS((tm, NH * HD), lambda i: (i, 0))],
        out_specs=[_BS((tm, NH * HD), lambda i: (i, 0)), _BS((NH, 1, tm), lambda i: (0, 0, i))],
        out_shape=[jax.ShapeDtypeStruct((t, NH * HD), BF16), jax.ShapeDtypeStruct((NH, 1, t), F32)],
        compiler_params=_params(("parallel",)),
    )(dy, w, o)


def _flash_bwd(q, k, v, do, lse_row, delta_row, t):
    tq = _pick(t, ATT_Q)
    tk = tq // ATT_KEY_TILES

    def body(q_ref, k_ref, v_ref, do_ref, lse_ref, delta_ref, dq_ref, dk_ref, dv_ref):
        j = pl.program_id(1)

        @pl.when(j == 0)
        def _():
            dq_ref[...] = jnp.zeros_like(dq_ref)

        kt, vt = k_ref[0], v_ref[0]

        def step(i, carry, key0):
            dk, dv = carry
            rows = pl.ds(pl.multiple_of(i * tq, tq), tq)
            qt, dot_ = q_ref[0, rows, :], do_ref[rows, :]
            p = jnp.exp2(_dot_nt(kt, qt) - lse_ref[0, :, rows])
            if key0 is not None:
                p = jnp.where(_chunk_mask(tq, tk, key0, True), p, 0.0)
            ds = (p * (_dot_nt(vt, dot_) - delta_ref[0, :, rows])).astype(BF16)
            dq_ref[0, rows, :] += _dot_tn(ds, kt)
            return dk + _dot(ds, qt), dv + _dot(p, dot_)

        own = j // ATT_KEY_TILES
        carry = step(own, (jnp.zeros((tk, QK), F32), jnp.zeros((tk, HD), F32)), (j % ATT_KEY_TILES) * tk)
        dk, dv = lax.fori_loop(own + 1, t // tq, lambda i, cr: step(i, cr, None), carry)
        dk_ref[0] = dk * (1.0 / LOG2E)
        dv_ref[0] = dv

    return pl.pallas_call(
        body, name="flash_bwd", grid=(NH, t // tk),
        in_specs=[pl.BlockSpec((1, t, QK), lambda h, j: (h, 0, 0)), pl.BlockSpec((1, tk, QK), lambda h, j: (h, j, 0)),
                  pl.BlockSpec((1, tk, HD), lambda h, j: (h, j, 0)), pl.BlockSpec((t, HD), lambda h, j: (0, h)),
                  pl.BlockSpec((1, 1, t), lambda h, j: (h, 0, 0)), pl.BlockSpec((1, 1, t), lambda h, j: (h, 0, 0))],
        out_specs=[pl.BlockSpec((1, t, QK), lambda h, j: (h, 0, 0)), pl.BlockSpec((1, tk, QK), lambda h, j: (h, j, 0)),
                   pl.BlockSpec((1, tk, HD), lambda h, j: (h, j, 0))],
        out_shape=[jax.ShapeDtypeStruct((NH, t, QK), F32), jax.ShapeDtypeStruct((NH, t, QK), F32),
                   jax.ShapeDtypeStruct((NH, t, HD), F32)],
        compiler_params=_params(("parallel", "arbitrary")),
    )(q, k, v, do, lse_row, delta_row)


def _ffn_in(xn, w_in, name):
    t, k = xn.shape
    s, _, n = w_in.shape
    tm = _pick(t, ROW_TILES)

    def body(x_ref, w_ref, gu_ref, a_ref):
        x = x_ref[...]
        g, u = _dot(x, w_ref[0]), _dot(x, w_ref[1])
        gu_ref[0] = g.astype(BF16)
        gu_ref[1] = u.astype(BF16)
        a_ref[...] = (g * _sig(g) * u).astype(BF16)

    return pl.pallas_call(
        body, name=name, grid=(t // tm, s // 2),
        in_specs=[_BS((tm, k), lambda i, j: (i, 0)), _BS((2, None, k, n), lambda i, j: (0, j, 0, 0))],
        out_specs=[_BS((2, None, tm, n), lambda i, j: (0, j, i, 0)), _BS((None, tm, n), lambda i, j: (j, i, 0))],
        out_shape=[jax.ShapeDtypeStruct((2, s // 2, t, n), BF16), jax.ShapeDtypeStruct((s // 2, t, n), BF16)],
        compiler_params=_params(("parallel", "parallel")),
    )(xn, w_in.reshape(2, s // 2, k, n))


def _ffn_dgu(dfo, w_out, gu, name):
    t, k = dfo.shape
    s, n, _ = w_out.shape
    tm = _pick(t, ROW_TILES)

    def body(d_ref, w_ref, gu_ref, o_ref):
        da = _dot_nt(d_ref[...], w_ref[...])
        g, u = gu_ref[0].astype(F32), gu_ref[1].astype(F32)
        sg = _sig(g)
        o_ref[0] = (da * u * sg * (1.0 + g * (1.0 - sg))).astype(BF16)
        o_ref[1] = (da * g * sg).astype(BF16)

    pair = _BS((2, None, tm, n), lambda i, j: (0, j, i, 0))
    return pl.pallas_call(
        body, name=name, grid=(t // tm, s),
        in_specs=[_BS((tm, k), lambda i, j: (i, 0)), _BS((None, n, k), lambda i, j: (j, 0, 0)), pair],
        out_specs=pair, out_shape=jax.ShapeDtypeStruct((2, s, t, n), BF16),
        compiler_params=_params(("parallel", "parallel")),
    )(dfo, w_out, gu).reshape(2 * s, t, n)


def _ffn_fwd(xn, w_in, w_out, tag, epilogue):
    gu, a = _ffn_in(xn, w_in, tag + "_in")
    return gu, a, _mm_stack_red(a, w_out, tag + "_out", "nn", epilogue)


def _ffn_bwd(dfo, xn, gu, a, w_in, w_out, tag, epilogue):
    dw_out = _mm_stack_tn(a, dfo, tag + "_dwout")
    dgu = _ffn_dgu(dfo, w_out, gu, tag + "_dgu")
    return _mm_stack_red(dgu, w_in, tag + "_dxn", "nt", epilogue), _mm_stack_tn(dgu, xn, tag + "_dwin"), dw_out


def _local_step(x, target, cs, sn, w, late_weights, s, early_grads):
    t = x.shape[0]
    tm = _pick(t, (256, 128))
    g = {}
    ffn_out = lambda n: w[n].reshape(4, FF // 4, D)

    def norm_fn(x_ref, g_ref, o_ref):
        o_ref[...] = _rms(x_ref[...], g_ref[...]).astype(BF16)

    xn1 = _rows(norm_fn, "norm1", t, tm, [(x, D, 0)], [s["ffn1_norm"]], [(D, BF16)])[0]
    def res_norm_fn(scale):
        def fn(h_ref, f_ref, g_ref, h_out, n_out):
            h = h_ref[...] + scale * f_ref[...]
            h_out[...] = h
            n_out[...] = _rms(h, g_ref[...]).astype(BF16)
        return fn

    gu1, a1, (h1, u) = _ffn_fwd(xn1, w["ffn1_w_in"], ffn_out("ffn1_w_out"), "ffn1",
                                (res_norm_fn(0.5), x, [], [s["mix_norm"]], [(D, F32), (D, BF16)], []))
    w = {**w, **late_weights(h1)}
    rows_of = lambda n: w[n].reshape(-1, w[n].shape[2])
    w_hgb, w_mlab, w_o = rows_of("w_hg_branch"), rows_of("w_mla_branch"), rows_of("w_out")
    w_in_nat = w["w_in"].transpose(1, 0, 2).reshape(D, -1)
    w_mrg = w["w_merge"]
    mw = w_mrg.shape[2]
    w_in_hg, w_in_mla = w_in_nat[:, :4 * HGW], w_in_nat[:, 4 * HGW:]
    p_hg = _mm(u, w_in_hg, "nn", "proj_hg")
    p_mla = _mm(u, w_in_mla, "nn", "proj_mla")
    gpre = _cols_fwd(u, w_mrg, "proj_gate")
    o_pre, hgy, states = _hgrn_fwd(p_hg, s["hg_lb_table"], s["hg_out_norm"], t)
    prep_args = (p_mla, cs, sn, w["w_q_up"], w["w_kv_up"], s["mla_q_lora_norm"], s["mla_kv_lora_norm"],
                 s["q_head_norm"], s["k_head_norm"])
    q, k, v = _mla_prep_fwd(*prep_args, t)
    att, lse = _flash_fwd(q, k, v, t)
    y_hg = _mm(hgy, w_hgb, "nn", "branch_hg")
    y_mla = _mm(att, w_mlab, "nn", "branch_mla")

    def mix_fn(gh_ref, gm_ref, yh_ref, ym_ref, b_ref, o_ref):
        gh = _sig(gh_ref[...] + b_ref[:, 0:D])
        gm = _sig(gm_ref[...] + b_ref[:, D:2 * D])
        o_ref[...] = (gh * yh_ref[...] + gm * ym_ref[...]).astype(BF16)

    mixed = _rows(mix_fn, "mix", t, tm, [(gpre, D, 0), (gpre, D, 1), (y_hg, D, 0), (y_mla, D, 0)], [s["b_merge"]],
                  [(D, BF16)])[0]
    h2, xn2 = _mm_stack_red(mixed[None], w_o[None], "mix_out", "nn",
                            (res_norm_fn(1.0), h1, [], [s["ffn2_norm"]], [(D, F32), (D, BF16)], []))

    def loss_fn(h_ref, f_ref, tg_ref, g_ref, dh_out, dhb_out):
        h = h_ref[...] + 0.5 * f_ref[...]
        e = _rms(h, g_ref[...]) - tg_ref[...]
        dh, dgain = _rms_bwd(h, g_ref[...], e / D)
        dh_out[...] = dh
        dhb_out[...] = (0.5 * dh).astype(BF16)
        return dgain, jnp.full((1, LANE), 0.5 / D * jnp.sum(e * e), F32)

    gu2, a2, (dh3, dfo2, g["final_norm"], loss) = _ffn_fwd(
        xn2, w["ffn2_w_in"], ffn_out("ffn2_w_out"), "ffn2",
        (loss_fn, h2, [target], [s["final_norm"]], [(D, F32), (D, BF16)], [(1, D), (1, LANE)]))

    def norm_bwd_fn(scale):
        def fn(h_ref, dxn_ref, dh_ref, g_ref, dh_out, dhb_out):
            dx, dgain = _rms_bwd(h_ref[...], g_ref[...], dxn_ref[...])
            dh = dh_ref[...] + dx
            dh_out[...] = dh
            dhb_out[...] = (scale * dh).astype(BF16)
            return (dgain,)
        return fn

    as_rows = lambda a: a.reshape((N_DEV, -1) + a.shape[-1:])
    (dh2, dh2b, g["ffn2_norm"]), g["ffn2_w_in"], dwo = _ffn_bwd(
        dfo2, xn2, gu2, a2, w["ffn2_w_in"], ffn_out("ffn2_w_out"), "ffn2",
        (norm_bwd_fn(1.0), h2, [dh3], [s["ffn2_norm"]], [(D, F32), (D, BF16)], [(1, D)]))
    g["ffn2_w_out"] = as_rows(dwo)
    dmixed = _mm(dh2b, w_o, "nt", "mix_out_dx")
    g["w_out"] = as_rows(_mm(mixed, dh2b, "tn", "mix_out_dw"))

    def mix_bwd_fn(gh_ref, gm_ref, yh_ref, ym_ref, dm_ref, b_ref, dyh_out, dym_out, dg_out):
        gh = _sig(gh_ref[...] + b_ref[:, 0:D])
        gm = _sig(gm_ref[...] + b_ref[:, D:2 * D])
        dm = dm_ref[...]
        dyh_out[...] = (dm * gh).astype(BF16)
        dym_out[...] = (dm * gm).astype(BF16)
        dgh = dm * yh_ref[...] * gh * (1.0 - gh)
        dgm = dm * ym_ref[...] * gm * (1.0 - gm)
        dg_out[:, 0:D] = dgh.astype(BF16)
        dg_out[:, D:2 * D] = dgm.astype(BF16)
        return (jnp.concatenate([jnp.sum(dgh, axis=0, keepdims=True), jnp.sum(dgm, axis=0, keepdims=True)], axis=1),)

    dyh, dym, dgpre, g["b_merge"] = _rows(
        mix_bwd_fn, "mix_bwd", t, tm, [(gpre, D, 0), (gpre, D, 1), (y_hg, D, 0), (y_mla, D, 0), (dmixed, D, 0)],
        [s["b_merge"]], [(D, BF16), (D, BF16), (2 * D, BF16)], [(1, 2 * D)])
    g["w_hg_branch"] = as_rows(_mm(hgy, dyh, "tn", "branch_hg_dw"))
    g["w_mla_branch"] = as_rows(_mm(att, dym, "tn", "branch_mla_dw"))
    g["w_merge"] = _cols_dw(u, dgpre, mw, "proj_gate_dw")
    dhgy = _mm(dyh, w_hgb, "nt", "branch_hg_dx")
    datt, delta = _attn_out_bwd(dym, w_mlab, att, t)
    du_gate = _cols_dx(dgpre, w_mrg, "proj_gate_dx")

    dq, dk, dv = _flash_bwd(q, k, v, datt, lse.reshape(NH, 1, t), delta, t)
    (dp_mla, g["w_q_up"], g["w_kv_up"], g["mla_q_lora_norm"], g["mla_kv_lora_norm"], g["q_head_norm"],
     g["k_head_norm"]) = _mla_prep_bwd(*prep_args, dq, dk, dv, t)
    dp_hg, g["hg_lb_table"], g["hg_out_norm"] = _hgrn_bwd(p_hg, s["hg_lb_table"], s["hg_out_norm"], o_pre, states,
                                                          dhgy, t)
    dw_in_nat = jnp.concatenate([_mm(u, dp_hg, "tn", "proj_hg_dw"), _mm(u, dp_mla, "tn", "proj_mla_dw")], axis=1)
    g["w_in"] = dw_in_nat.T.reshape(N_DEV, -1, D)
    du_hg = _mm(dp_hg, w_in_hg, "nt", "proj_hg_dx")
    du_mla = _mm(dp_mla, w_in_mla, "nt", "proj_mla_dx")

    def mixnorm_bwd_fn(h_ref, a_ref, b_ref, c_ref, dh_ref, g_ref, dh_out, dhb_out):
        dx, dgain = _rms_bwd(h_ref[...], g_ref[...], a_ref[...] + b_ref[...] + c_ref[...])
        dh = dh_ref[...] + dx
        dh_out[...] = dh
        dhb_out[...] = (0.5 * dh).astype(BF16)
        return (dgain,)

    mix_gain = s["mix_norm"] + early_grads(g)[0:1, 0:1]
    dh1, dfo1, g["mix_norm"] = _rows(mixnorm_bwd_fn, "mixnorm_bwd", t, tm,
                                     [(h1, D, 0), (du_hg, D, 0), (du_mla, D, 0), (du_gate, D, 0), (dh2, D, 0)],
                                     [mix_gain], [(D, F32), (D, BF16)], [(1, D)])
    (grad_x, _, g["ffn1_norm"]), g["ffn1_w_in"], dwo = _ffn_bwd(
        dfo1, xn1, gu1, a1, w["ffn1_w_in"], ffn_out("ffn1_w_out"), "ffn1",
        (norm_bwd_fn(1.0), x, [dh1], [s["ffn1_norm"]], [(D, F32), (D, BF16)], [(1, D)]))
    g["ffn1_w_out"] = as_rows(dwo)
    return loss, grad_x, g


def _coords():
    return lax.axis_index("x"), lax.axis_index("y"), lax.axis_index("c")


def _hbm_call(body, name, ins, out_shapes, scratch):
    any_spec = pl.BlockSpec(memory_space=pl.ANY)
    return pl.pallas_call(
        body, name=name, out_shape=[jax.ShapeDtypeStruct(s, dt) for s, dt in out_shapes],
        in_specs=[any_spec] * len(ins), out_specs=[any_spec] * len(out_shapes), scratch_shapes=scratch,
    )(*ins)


def _my_slot():
    return 4 * lax.axis_index("x") + 2 * lax.axis_index("y") + lax.axis_index("c")


def _put_own(buf, own, index):
    return lax.dynamic_update_index_in_dim(buf, own, index, 0)


def _all_gather(blocks, name):
    nb = len(blocks)

    def body(*refs):
        x_refs, out_refs = refs[:nb], refs[nb:2 * nb]
        send_sems, recv_sems = refs[2 * nb:]
        x, y, c = _coords()
        me, sibling = (x, y, c), (x, y, 1 - c)
        chips = [(1 - x, y), (x, 1 - y), (1 - x, 1 - y)]

        def slot(b, px, py, pc):
            return out_refs[b].at[4 * px + 2 * py + pc]

        def copy(b, kk, block_of, to, src=None):
            return pltpu.make_async_remote_copy(
                src_ref=slot(b, *block_of) if src is None else src, dst_ref=slot(b, *block_of),
                send_sem=send_sems.at[b, kk], recv_sem=recv_sems.at[b, kk], device_id=to, device_id_type=MESH)

        first = [copy(b, 0, me, sibling, src=x_refs[b]) for b in range(nb)]
        first += [copy(b, 1 + j, me, (*chip, c), src=x_refs[b]) for j, chip in enumerate(chips) for b in range(nb)]
        for cp in first:
            cp.start()
        passed = []
        for j, chip in enumerate(chips):
            for b in range(nb):
                copy(b, 1 + j, (*chip, c), me).wait_recv()
                passed.append(copy(b, 4 + j, (*chip, c), sibling))
                passed[-1].start()
        for b in range(nb):
            copy(b, 0, sibling, me).wait_recv()
        for j, chip in enumerate(chips):
            for b in range(nb):
                copy(b, 4 + j, (*chip, 1 - c), me).wait_recv()
        for cp in first + passed:
            cp.wait_send()

    outs = _hbm_call(body, name, blocks, [((N_DEV,) + b.shape, b.dtype) for b in blocks],
                     [pltpu.SemaphoreType.DMA((nb, 7)), pltpu.SemaphoreType.DMA((nb, 7))])
    return [_put_own(o, b[None], _my_slot()) for o, b in zip(outs, blocks)]


def _gather_peers():
    x, y, c = _coords()
    return (x, y, c), [(x, y, 1 - c), (1 - x, y, c), (x, 1 - y, c), (1 - x, 1 - y, c)]


def _gather_start(blocks, after, name):
    nb = len(blocks)
    hbm, sem = pl.BlockSpec(memory_space=pltpu.HBM), pl.BlockSpec(memory_space=pltpu.SEMAPHORE)

    def body(*refs):
        x_refs, out_refs = refs[:nb], refs[nb:2 * nb]
        send_sems, recv_sems, token = refs[2 * nb + 1], refs[2 * nb + 2], refs[-1]
        (x, y, c), peers = _gather_peers()
        for kk, peer in enumerate(peers):
            for b in range(nb):
                pltpu.make_async_remote_copy(
                    src_ref=x_refs[b], dst_ref=out_refs[b].at[4 * x + 2 * y + c], send_sem=send_sems.at[4 * b + kk],
                    recv_sem=recv_sems.at[4 * b + kk], device_id=peer, device_id_type=MESH).start()
        token[...] = jnp.zeros_like(token)

    gathers = [pltpu.with_memory_space_constraint(lax.empty((N_DEV,) + b.shape, b.dtype), pltpu.HBM) for b in blocks]
    outs = pl.pallas_call(
        body, name=name,
        out_shape=(pltpu.SemaphoreType.DMA((4 * nb,)), pltpu.SemaphoreType.DMA((4 * nb,)),
                   *[pltpu.HBM(b.shape, b.dtype) for b in blocks], *[pltpu.HBM(b.shape, b.dtype) for b in gathers],
                   jax.ShapeDtypeStruct((SUBLANE, LANE), F32)),
        in_specs=[hbm] * (2 * nb) + [pl.BlockSpec(memory_space=pl.ANY)],
        out_specs=(sem, sem, *[hbm] * (2 * nb), pl.BlockSpec(memory_space=pltpu.VMEM)),
        input_output_aliases={i: 2 + i for i in range(2 * nb)},
        compiler_params=pltpu.CompilerParams(has_side_effects=pltpu.SideEffectType.DATAFLOW_SIDE_EFFECTING),
    )(*[pltpu.with_memory_space_constraint(b, pltpu.HBM) for b in blocks], *gathers, after)
    return outs[0], outs[1], list(outs[2:2 + nb]), list(outs[2 + nb:2 + 2 * nb]), outs[-1]


def _gather_wait(send_sems, recv_sems, thru, gathers, after, name):
    nb = len(thru)
    hbm, sem = pl.BlockSpec(memory_space=pltpu.HBM), pl.BlockSpec(memory_space=pltpu.SEMAPHORE)

    def body(*refs):
        x_refs, out_refs = refs[:nb], refs[nb:2 * nb]
        send_sems_, recv_sems_ = refs[2 * nb], refs[2 * nb + 1]
        _, peers = _gather_peers()
        for kk, (px, py, pc) in enumerate(peers):
            for b in range(nb):
                cp = pltpu.make_async_remote_copy(
                    src_ref=x_refs[b], dst_ref=out_refs[b].at[4 * px + 2 * py + pc], send_sem=send_sems_.at[4 * b + kk],
                    recv_sem=recv_sems_.at[4 * b + kk], device_id=(px, py, pc), device_id_type=MESH)
                cp.wait_send()
                cp.wait_recv()

    outs = pl.pallas_call(
        body, name=name,
        out_shape=(*[pltpu.HBM(b.shape, b.dtype) for b in thru], *[pltpu.HBM(b.shape, b.dtype) for b in gathers]),
        in_specs=[hbm] * (2 * nb) + [sem, sem, pl.BlockSpec(memory_space=pl.ANY)], out_specs=[hbm] * (2 * nb),
        input_output_aliases={i: i for i in range(2 * nb)},
        compiler_params=pltpu.CompilerParams(has_side_effects=pltpu.SideEffectType.DATAFLOW_SIDE_EFFECTING),
    )(*thru, *gathers, send_sems, recv_sems, after)
    return list(outs[:nb]), list(outs[nb:])


def _gather_finish(blocks, gathers, name):
    nb = len(blocks)

    def body(*refs):
        x_refs, in_refs, out_refs = refs[:nb], refs[nb:2 * nb], refs[2 * nb:3 * nb]
        send_sems, recv_sems = refs[3 * nb:]
        (x, y, c), peers = _gather_peers()
        copies = []
        for j, (px, py, _) in enumerate(peers[1:]):
            for b in range(nb):
                copies.append(pltpu.make_async_remote_copy(
                    src_ref=in_refs[b].at[4 * px + 2 * py + c], dst_ref=out_refs[b].at[4 * px + 2 * py + c],
                    send_sem=send_sems.at[b, j], recv_sem=recv_sems.at[b, j], device_id=(x, y, 1 - c),
                    device_id_type=MESH))
                copies[-1].start()
        for j, (px, py, _) in enumerate(peers[1:]):
            for b in range(nb):
                pltpu.make_async_remote_copy(
                    src_ref=in_refs[b].at[4 * px + 2 * py + c], dst_ref=out_refs[b].at[4 * px + 2 * py + 1 - c],
                    send_sem=send_sems.at[b, j], recv_sem=recv_sems.at[b, j], device_id=(x, y, 1 - c),
                    device_id_type=MESH).wait_recv()
        for cp in copies:
            cp.wait_send()

    any_spec = pl.BlockSpec(memory_space=pl.ANY)
    outs = pl.pallas_call(
        body, name=name, out_shape=[jax.ShapeDtypeStruct(b.shape, b.dtype) for b in gathers],
        in_specs=[any_spec] * (2 * nb), out_specs=[any_spec] * nb,
        input_output_aliases={nb + i: i for i in range(nb)},
        scratch_shapes=[pltpu.SemaphoreType.DMA((nb, 3)), pltpu.SemaphoreType.DMA((nb, 3))],
    )(*blocks, *gathers)
    return [_put_own(o, b[None], _my_slot()) for o, b in zip(outs, blocks)]


def _sibling_swap(bufs, name):
    nb = len(bufs)

    def body(*refs):
        x_refs, out_refs = refs[:nb], refs[nb:2 * nb]
        send_sems, recv_sems = refs[2 * nb:]
        x, y, c = _coords()
        copies = [pltpu.make_async_remote_copy(
            src_ref=x_refs[b].at[2 * q + 1 - c], dst_ref=out_refs[b].at[q], send_sem=send_sems.at[b, q],
            recv_sem=recv_sems.at[b, q], device_id=(x, y, 1 - c), device_id_type=MESH)
            for b in range(nb) for q in range(4)]
        for cp in copies:
            cp.start()
        for cp in copies:
            cp.wait()

    return _hbm_call(body, name, bufs, [((4,) + b.shape[1:], b.dtype) for b in bufs],
                     [pltpu.SemaphoreType.DMA((nb, 4)), pltpu.SemaphoreType.DMA((nb, 4))])


def _chip_exchange_start(bufs, name):
    nb = len(bufs)
    hbm, sem = pl.BlockSpec(memory_space=pltpu.HBM), pl.BlockSpec(memory_space=pltpu.SEMAPHORE)

    def body(*refs):
        x_refs, land_refs = refs[:nb], refs[nb:2 * nb]
        send_sems, recv_sems, token = refs[2 * nb], refs[2 * nb + 1], refs[-1]
        x, y, c = _coords()
        for j, (px, py) in enumerate([(1 - x, y), (x, 1 - y), (1 - x, 1 - y)]):
            for b in range(nb):
                pltpu.make_async_remote_copy(
                    src_ref=x_refs[b].at[2 * px + py], dst_ref=land_refs[b].at[2 * x + y], send_sem=send_sems.at[3 * b + j],
                    recv_sem=recv_sems.at[3 * b + j], device_id=(px, py, c), device_id_type=MESH).start()
        token[...] = jnp.zeros_like(token)

    lands = [pltpu.with_memory_space_constraint(lax.empty(b.shape, b.dtype), pltpu.HBM) for b in bufs]
    outs = pl.pallas_call(
        body, name=name,
        out_shape=(pltpu.SemaphoreType.DMA((3 * nb,)), pltpu.SemaphoreType.DMA((3 * nb,)),
                   *[pltpu.HBM(b.shape, b.dtype) for b in bufs], *[pltpu.HBM(b.shape, b.dtype) for b in bufs],
                   jax.ShapeDtypeStruct((SUBLANE, LANE), F32)),
        in_specs=[hbm] * (2 * nb), out_specs=(sem, sem, *[hbm] * (2 * nb), pl.BlockSpec(memory_space=pltpu.VMEM)),
        input_output_aliases={i: 2 + i for i in range(2 * nb)},
        compiler_params=pltpu.CompilerParams(has_side_effects=pltpu.SideEffectType.DATAFLOW_SIDE_EFFECTING),
    )(*[pltpu.with_memory_space_constraint(b, pltpu.HBM) for b in bufs], *lands)
    return outs[0], outs[1], list(outs[2:2 + nb]), list(outs[2 + nb:2 + 2 * nb]), outs[-1]


def _chip_exchange_wait(send_sems, recv_sems, thru, lands, after, name):
    nb = len(thru)
    hbm, sem = pl.BlockSpec(memory_space=pltpu.HBM), pl.BlockSpec(memory_space=pltpu.SEMAPHORE)

    def body(*refs):
        x_refs, land_refs = refs[:nb], refs[nb:2 * nb]
        send_sems_, recv_sems_ = refs[2 * nb], refs[2 * nb + 1]
        x, y, c = _coords()
        for j, (px, py) in enumerate([(1 - x, y), (x, 1 - y), (1 - x, 1 - y)]):
            for b in range(nb):
                cp = pltpu.make_async_remote_copy(
                    src_ref=x_refs[b].at[2 * px + py], dst_ref=land_refs[b].at[2 * px + py],
                    send_sem=send_sems_.at[3 * b + j], recv_sem=recv_sems_.at[3 * b + j], device_id=(px, py, c),
                    device_id_type=MESH)
                cp.wait_send()
                cp.wait_recv()

    outs = pl.pallas_call(
        body, name=name,
        out_shape=(*[pltpu.HBM(b.shape, b.dtype) for b in thru], *[pltpu.HBM(b.shape, b.dtype) for b in lands]),
        in_specs=[hbm] * (2 * nb) + [sem, sem, pl.BlockSpec(memory_space=pl.ANY)], out_specs=[hbm] * (2 * nb),
        input_output_aliases={i: i for i in range(2 * nb)},
        compiler_params=pltpu.CompilerParams(has_side_effects=pltpu.SideEffectType.DATAFLOW_SIDE_EFFECTING),
    )(*thru, *lands, send_sems, recv_sems, after)
    return list(outs[:nb]), list(outs[nb:])


def _chip_sum(g, r1, c, name):
    _, r, cw = g.shape
    tr = _pick(r, (256, 176, 128))

    def body(c_ref, g_ref, r_ref, o_ref):
        o_ref[...] = (g_ref[...] + r_ref[...]).astype(BF16)

    grid_spec = pltpu.PrefetchScalarGridSpec(
        num_scalar_prefetch=1, grid=(4, r // tr),
        in_specs=[_BS((None, None, tr, cw), lambda q, i, c_ref: (q, c_ref[0], i, 0)),
                  _BS((None, tr, cw), lambda q, i, c_ref: (q, i, 0))],
        out_specs=_BS((None, tr, cw), lambda q, i, c_ref: (q, i, 0)))
    return pl.pallas_call(
        body, name=name, grid_spec=grid_spec, out_shape=jax.ShapeDtypeStruct((4, r, cw), BF16),
        compiler_params=_params(("parallel", "parallel")),
    )(c.reshape(1).astype(jnp.int32), g.reshape(4, 2, r, cw), r1)


def _adamw_math(w, g, m, v):
    m = B1 * m + (1.0 - B1) * g
    v = B2 * v + (1.0 - B2) * (g * g)
    m_hat = m / (1.0 - B1 ** STEP)
    v_hat = v / (1.0 - B2 ** STEP)
    return -LR * (m_hat / (jnp.sqrt(v_hat) + AEPS) + WD * w), m, v


def _sum_adamw(parts, w, m, v, name):
    r, c = w.shape
    tr = _pick(r, (256, 176, 128))
    tc = _pick(c, (256, 128)) if tr == r and r > 256 else c

    def body(p0, p1, p2, p3, w_ref, m_ref, v_ref, g_out, d_out, m_out, v_out):
        g = ((p0[...].astype(F32) + p1[...].astype(F32)) + p2[...].astype(F32)) + p3[...].astype(F32)
        g_out[...] = g
        d_out[...], m_out[...], v_out[...] = _adamw_math(w_ref[...], g, m_ref[...], v_ref[...])

    part = lambda q: _BS((None, tr, tc), functools.partial(lambda i, j, q: (q, i, j), q=q))
    plain = _BS((tr, tc), lambda i, j: (i, j))
    return pl.pallas_call(
        body, name=name, grid=(r // tr, c // tc), in_specs=[part(q) for q in range(4)] + [plain] * 3,
        out_specs=[plain] * 4, out_shape=[jax.ShapeDtypeStruct((r, c), F32)] * 4,
        compiler_params=_params(("parallel", "parallel")),
    )(parts, parts, parts, parts, w, m, v)


def _small_update(gathered, w, m, v):
    r = w.shape[0]

    def body(ga_ref, w_ref, m_ref, v_ref, g_out, d_out, m_out, v_out):
        g = ga_ref[0]
        for dev in range(1, N_DEV):
            g = g + ga_ref[dev]
        g_out[...] = g
        d_out[...], m_out[...], v_out[...] = _adamw_math(w_ref[...], g, m_ref[...], v_ref[...])

    return pl.pallas_call(
        body, name="small_update", out_shape=[jax.ShapeDtypeStruct((r, LANE), F32)] * 4,
    )(gathered, w, m, v)


def _pack_small(vals):
    rows = []
    for name, (r, n) in SMALL:
        flat = vals[name].reshape(-1)
        pad = (-flat.shape[0]) % (SUBLANE * LANE)
        rows.append(jnp.pad(flat, (0, pad)).reshape(-1, LANE))
    return jnp.concatenate(rows, axis=0)


def _unpack_small(packed):
    out, off = {}, 0
    for name, (r, n) in SMALL:
        nrow = -(-(r * n) // (SUBLANE * LANE)) * SUBLANE
        out[name] = packed[off:off + nrow].reshape(-1)[:r * n].reshape(r, n)
        off += nrow
    return out


def kernel(x, positions, ffn1_norm, ffn1_w_in, ffn1_w_out, mix_norm, w_in, hg_lb_table, hg_out_norm, w_hg_branch, mla_q_lora_norm, w_q_up, mla_kv_lora_norm, w_kv_up, q_head_norm, k_head_norm, w_mla_branch, w_merge, b_merge, w_out, ffn2_norm, ffn2_w_in, ffn2_w_out, final_norm, loss_target, m_ffn1_norm, m_ffn1_w_in, m_ffn1_w_out, m_mix_norm, m_w_in, m_hg_lb_table, m_hg_out_norm, m_w_hg_branch, m_mla_q_lora_norm, m_w_q_up, m_mla_kv_lora_norm, m_w_kv_up, m_q_head_norm, m_k_head_norm, m_w_mla_branch, m_w_merge, m_b_merge, m_w_out, m_ffn2_norm, m_ffn2_w_in, m_ffn2_w_out, m_final_norm, v_ffn1_norm, v_ffn1_w_in, v_ffn1_w_out, v_mix_norm, v_w_in, v_hg_lb_table, v_hg_out_norm, v_w_hg_branch, v_mla_q_lora_norm, v_w_q_up, v_mla_kv_lora_norm, v_w_kv_up, v_q_head_norm, v_k_head_norm, v_w_mla_branch, v_w_merge, v_b_merge, v_w_out, v_ffn2_norm, v_ffn2_w_in, v_ffn2_w_out, v_final_norm):
    args = dict(locals())
    t = x.shape[1]
    big_w = {n: args[n][0] for n, _, _ in BIG}
    small = {n: args[n].reshape(shape) for n, shape in SMALL}

    names = [n for n, _, _ in BIG]
    first, rest = names[:2], names[2:]
    full = dict(zip(first, _all_gather([big_w[n].astype(BF16) for n in first], "weights_all_gather_ffn1")))
    g_send, g_recv, g_thru, g_bufs, g_token = _gather_start([big_w[n].astype(BF16) for n in rest], full[first[0]],
                                                            "weights_gather_start")
    gains = dict(small, ffn1_norm=small["ffn1_norm"] + g_token[0:1, 0:1])

    def late_weights(after):
        blocks, bufs = _gather_wait(g_send, g_recv, g_thru, g_bufs, after, "weights_gather_wait")
        return dict(zip(rest, _gather_finish(blocks, bufs, "weights_gather_finish")))

    inv_freq = ROPE_THETA ** (-jnp.arange(0, ROPE, 2, dtype=F32) / ROPE)
    ang = positions[0].astype(F32)[:, None] * inv_freq
    cs = jnp.concatenate([jnp.cos(ang), jnp.cos(ang)], axis=1)
    sn = jnp.concatenate([jnp.sin(ang), jnp.sin(ang)], axis=1)

    c = lax.axis_index("c")
    chip = 2 * lax.axis_index("x") + lax.axis_index("y")
    early = {}

    def chip_sums_of(g, ns, tag):
        from_sibling = _sibling_swap([g[n] for n in ns], "grads_sibling_swap_" + tag)
        return [_chip_sum(g[n], r1, c, "chip_sum_" + n) for n, r1 in zip(ns, from_sibling)]

    def early_grads(g):
        early["names"] = [n for n in names if n in g]
        early["sums"] = chip_sums_of(g, early["names"], "early")
        early["send"], early["recv"], early["thru"], early["lands"], token = _chip_exchange_start(
            early["sums"], "grads_exchange_start")
        return token

    loss_row, grad_x, g = _local_step(x[0], loss_target[0], cs, sn, full, late_weights, gains, early_grads)
    late = [n for n in names if n not in early["names"]]
    l_send, l_recv, l_thru, l_lands, l_token = _chip_exchange_start(chip_sums_of(g, late, "late"),
                                                                    "grads_exchange_late_start")
    sent, landed = _chip_exchange_wait(early["send"], early["recv"], early["thru"], early["lands"], l_token,
                                       "grads_exchange_wait")
    exchanged = {n: _put_own(land, lax.dynamic_index_in_dim(own, chip, 0), chip)
                 for n, land, own in zip(early["names"], landed, sent)}

    small_packed = jnp.concatenate([_pack_small(g), jnp.pad(loss_row, ((0, SUBLANE - 1), (0, 0)))], axis=0)
    small_all = _all_gather([small_packed], "small_all_gather")[0]
    zero_tail = jnp.zeros((SUBLANE, LANE), F32)
    pk = lambda d: jnp.concatenate([_pack_small(d), zero_tail], axis=0)
    sg, sd, sm, sv = _small_update(
        small_all, pk(small), pk({n: args["m_" + n].reshape(shape) for n, shape in SMALL}),
        pk({n: args["v_" + n].reshape(shape) for n, shape in SMALL}))
    n_small_rows = _pack_small(small).shape[0]
    loss = sg[n_small_rows, 0]
    outs = {k_: _unpack_small(a) for k_, a in (("grad", sg), ("delta", sd), ("new_m", sm), ("new_v", sv))}

    def adamw(n):
        tr = (lambda a: a.T) if n in GRAD_T else (lambda a: a)
        res = _sum_adamw(exchanged[n], tr(big_w[n]), tr(args["m_" + n][0]), tr(args["v_" + n][0]), "adamw_" + n)
        outs["grad"][n], outs["delta"][n], outs["new_m"][n], outs["new_v"][n] = [tr(a) for a in res]

    for n in early["names"]:
        adamw(n)
    sent, landed = _chip_exchange_wait(l_send, l_recv, l_thru, l_lands, outs["delta"][early["names"][-1]],
                                       "grads_exchange_late_wait")
    for n, land, own in zip(late, landed, sent):
        exchanged[n] = _put_own(land, lax.dynamic_index_in_dim(own, chip, 0), chip)
        adamw(n)

    def shaped(kind, n):
        return outs[kind][n].reshape(args[n].shape)

    return (loss, grad_x[None], *[shaped("grad", n) for n in WEIGHT_ORDER], *[shaped("delta", n) for n in WEIGHT_ORDER],
            *[shaped("new_m", n) for n in WEIGHT_ORDER], *[shaped("new_v", n) for n in WEIGHT_ORDER])
```

```python
import functools

import jax
import jax.numpy as jnp
from jax import lax
from jax.experimental import pallas as pl
from jax.experimental.pallas import tpu as pltpu

F32 = jnp.float32
BF16 = jnp.bfloat16

D = 1024
FF = 2816
NH = 8
HD = 128
ROPE = 64
QK = HD + ROPE
QL = 384
KVL = 256
HGW = NH * HD
CHUNK = 64
EPS = 1e-6
ROPE_THETA = 10000.0
SCALE = QK ** -0.5
LOG2E = 1.4426950408889634

LR, B1, B2, AEPS, WD, STEP = 0.001, 0.9, 0.999, 1e-08, 0.01, 10

HB = 128
SUB = 16
EXP_CLAMP = 80.0
ATT_TILES = (512, 256, 128)
ATT_Q = (1024, 512, 256, 128)
ATT_KEY_TILES = 1
ROW_TILES = (1024, 512, 256, 128)

LANE = 128
SUBLANE = 8
VMEM_LIMIT = 56 << 20

N_DEV = 8
MESH = pl.DeviceIdType.MESH

BIG = (
    ("ffn1_w_in", (D, 2 * FF), 1), ("ffn1_w_out", (FF, D), 0), ("w_in", (D, 4800), 1),
    ("w_hg_branch", (HGW, D), 0), ("w_q_up", (QL, NH * QK), 1), ("w_kv_up", (KVL, NH * 2 * HD), 1),
    ("w_mla_branch", (NH * HD, D), 0), ("w_merge", (D, 2 * D), 1), ("w_out", (D, D), 0),
    ("ffn2_w_in", (D, 2 * FF), 1), ("ffn2_w_out", (FF, D), 0),
)
SMALL = (
    ("ffn1_norm", (1, D)), ("mix_norm", (1, D)), ("hg_lb_table", (2, HGW)), ("hg_out_norm", (1, HD)),
    ("mla_q_lora_norm", (1, QL)), ("mla_kv_lora_norm", (1, KVL)), ("q_head_norm", (1, QK)),
    ("k_head_norm", (1, QK)), ("b_merge", (1, 2 * D)), ("ffn2_norm", (1, D)), ("final_norm", (1, D)),
)
GRAD_T = ("ffn1_w_in", "ffn2_w_in", "w_in", "w_q_up")
WEIGHT_ORDER = ("ffn1_norm", "ffn1_w_in", "ffn1_w_out", "mix_norm", "w_in", "hg_lb_table", "hg_out_norm",
                "w_hg_branch", "mla_q_lora_norm", "w_q_up", "mla_kv_lora_norm", "w_kv_up", "q_head_norm",
                "k_head_norm", "w_mla_branch", "w_merge", "b_merge", "w_out", "ffn2_norm", "ffn2_w_in",
                "ffn2_w_out", "final_norm")


def _pick(n, cands):
    for c in cands:
        if n % c == 0:
            return c
    return n


def _params(sem):
    return pltpu.CompilerParams(dimension_semantics=sem, vmem_limit_bytes=VMEM_LIMIT)


def _sig(x):
    return 0.5 * jnp.tanh(0.5 * x) + 0.5


def _dot(a, b):
    return jnp.dot(a.astype(BF16), b.astype(BF16), preferred_element_type=F32)


def _dot_nt(a, b):
    return lax.dot_general(a.astype(BF16), b.astype(BF16), (((1,), (1,)), ((), ())),
                           preferred_element_type=F32)


def _dot_tn(a, b):
    return lax.dot_general(a.astype(BF16), b.astype(BF16), (((0,), (0,)), ((), ())),
                           preferred_element_type=F32)


def _split3(x):
    x1 = x.astype(BF16)
    r1 = x - x1.astype(F32)
    x2 = r1.astype(BF16)
    x3 = (r1 - x2.astype(F32)).astype(BF16)
    return x1, x2, x3


def _dot_sel(m, x):
    x1, x2, x3 = _split3(x)
    d = lambda p: jnp.dot(m, p, preferred_element_type=F32)
    return d(x1) + d(x2) + d(x3)


def _sel_dot(x, m):
    x1, x2, x3 = _split3(x)
    d = lambda p: jnp.dot(p, m, preferred_element_type=F32)
    return d(x1) + d(x2) + d(x3)


_TN = (1408, 1024, 768, 512, 384, 256, 128)


def _mm(a, b, mode, name, out_dtype=F32):
    if mode == "tn":
        t, m = a.shape
        n = b.shape[1]
        tt, tm, tn = _pick(t, (512, 256, 128)), _pick(m, _TN), _pick(n, _TN)

        def body(a_ref, b_ref, o_ref):
            @pl.when(pl.program_id(2) == 0)
            def _():
                o_ref[...] = jnp.zeros_like(o_ref)

            o_ref[...] += _dot_tn(a_ref[...], b_ref[...])

        return pl.pallas_call(
            body, name=name, grid=(m // tm, n // tn, t // tt),
            in_specs=[pl.BlockSpec((tt, tm), lambda i, j, k: (k, i)),
                      pl.BlockSpec((tt, tn), lambda i, j, k: (k, j))],
            out_specs=pl.BlockSpec((tm, tn), lambda i, j, k: (i, j)),
            out_shape=jax.ShapeDtypeStruct((m, n), F32),
            compiler_params=_params(("parallel", "parallel", "arbitrary")),
        )(a, b)

    m, k = a.shape
    tm = _pick(m, ROW_TILES)
    if mode == "nn":
        n = b.shape[1]
        tn = _pick(n, _TN)
        b_spec = pl.BlockSpec((k, tn), lambda i, j: (0, j))
        dot = _dot
    else:
        n = b.shape[0]
        tn = _pick(n, _TN if k <= 4096 else (512, 256, 128))
        b_spec = pl.BlockSpec((tn, k), lambda i, j: (j, 0))
        dot = _dot_nt

    def body(a_ref, b_ref, o_ref):
        o_ref[...] = dot(a_ref[...], b_ref[...]).astype(o_ref.dtype)

    return pl.pallas_call(
        body, name=name, grid=(m // tm, n // tn),
        in_specs=[pl.BlockSpec((tm, k), lambda i, j: (i, 0)), b_spec],
        out_specs=pl.BlockSpec((tm, tn), lambda i, j: (i, j)),
        out_shape=jax.ShapeDtypeStruct((m, n), out_dtype),
        compiler_params=_params(("parallel", "parallel")),
    )(a, b)


_DOTS = {"nn": _dot, "nt": _dot_nt, "tn": _dot_tn}
_BS = pl.BlockSpec


def _mmcall(name, kind, a, b, a_spec, b_spec, o_spec, o_shape, grid, red_axis=None, out_dtype=F32):
    dot = _DOTS[kind]

    def body(a_ref, b_ref, o_ref):
        if red_axis is None:
            o_ref[...] = dot(a_ref[...], b_ref[...]).astype(o_ref.dtype)
        else:
            @pl.when(pl.program_id(red_axis) == 0)
            def _():
                o_ref[...] = jnp.zeros_like(o_ref)

            o_ref[...] += dot(a_ref[...], b_ref[...])

    sem = tuple("arbitrary" if ax == red_axis else "parallel" for ax in range(len(grid)))
    return pl.pallas_call(
        body, name=name, grid=grid, in_specs=[a_spec, b_spec], out_specs=o_spec,
        out_shape=jax.ShapeDtypeStruct(o_shape, out_dtype), compiler_params=_params(sem),
    )(a, b)


class _AsRef:
    def __init__(self, value):
        self.value = value

    def __getitem__(self, idx):
        return self.value


def _mm_stack_red(a, w, name, kind, epilogue=None):
    s, t, n = a.shape
    nout = w.shape[2] if kind == "nn" else w.shape[1]
    if epilogue is None:
        tm = _pick(t, ROW_TILES)
        return _mmcall(name, kind, a, w, _BS((None, tm, n), lambda i, j: (j, i, 0)),
                       _BS((None,) + w.shape[1:], lambda i, j: (j, 0, 0)), _BS((tm, nout), lambda i, j: (i, 0)),
                       (t, nout), (t // tm, s), red_axis=1)
    fn, first, rows, vecs, outs, accs = epilogue
    rows = [first] + list(rows)
    tm = _pick(t, ROW_TILES)
    n_row, n_vec, n_out = len(rows), len(vecs), len(outs)
    dot = _DOTS[kind]

    def body(*refs):
        a_ref, w_ref, prod = refs[0], refs[1], refs[-1]
        row_refs = refs[2:2 + n_row]
        vec_refs = refs[2 + n_row:2 + n_row + n_vec]
        out_refs = refs[2 + n_row + n_vec:2 + n_row + n_vec + n_out]
        acc_refs = refs[2 + n_row + n_vec + n_out:-1]
        i, j = pl.program_id(0), pl.program_id(1)

        @pl.when(j == 0)
        def _():
            prod[...] = jnp.zeros_like(prod)

        prod[...] += dot(a_ref[...], w_ref[...])

        @pl.when(j == s - 1)
        def _():
            res = fn(row_refs[0], _AsRef(prod[...]), *row_refs[1:], *vec_refs, *out_refs)
            if acc_refs:
                @pl.when(i == 0)
                def _():
                    for r in acc_refs:
                        r[...] = jnp.zeros_like(r)

                for r, val in zip(acc_refs, res):
                    r[...] += val

    in_specs = [_BS((None, tm, n), lambda i, j: (j, i, 0)), _BS((None,) + w.shape[1:], lambda i, j: (j, 0, 0))]
    in_specs += [_BS((tm, r.shape[1]), lambda i, j: (i, 0)) for r in rows]
    in_specs += [_BS(v.shape, lambda i, j: (0, 0)) for v in vecs]
    out_specs = [_BS((tm, wd), lambda i, j: (i, 0)) for wd, _ in outs] + [_BS(sh, lambda i, j: (0, 0)) for sh in accs]
    out_shape = [jax.ShapeDtypeStruct((t, wd), dt) for wd, dt in outs] + [jax.ShapeDtypeStruct(sh, F32) for sh in accs]
    return pl.pallas_call(
        body, name=name, grid=(t // tm, s), in_specs=in_specs, out_specs=out_specs, out_shape=out_shape,
        scratch_shapes=[pltpu.VMEM((tm, nout), F32)],
        compiler_params=_params(("arbitrary" if accs else "parallel", "arbitrary")),
    )(a, w, *rows, *vecs)


def _mm_stack_tn(a, b, name):
    grp = 4
    if a.ndim == 2:
        t, k = a.shape
        s, _, n = b.shape
        tt = _pick(t, (512, 256, 128))
        a_spec = _BS((tt, k), lambda j, r: (r, 0))
        b_spec, b_in = _BS((None, grp, tt, n), lambda j, r: (j, 0, r, 0)), b.reshape(s // grp, grp, t, n)
        a_in = a
    else:
        s, t, k = a.shape
        n = b.shape[1]
        tt = _pick(t, (512, 256, 128))
        a_spec, a_in = _BS((None, grp, tt, k), lambda j, r: (j, 0, r, 0)), a.reshape(s // grp, grp, t, k)
        b_spec, b_in = _BS((tt, n), lambda j, r: (r, 0)), b

    def body(a_ref, b_ref, o_ref):
        @pl.when(pl.program_id(1) == 0)
        def _():
            o_ref[...] = jnp.zeros_like(o_ref)

        shared = a_ref[...] if a.ndim == 2 else b_ref[...]
        for e in range(grp):
            o_ref[e] += _dot_tn(shared, b_ref[e]) if a.ndim == 2 else _dot_tn(a_ref[e], shared)

    return pl.pallas_call(
        body, name=name, grid=(s // grp, t // tt), in_specs=[a_spec, b_spec],
        out_specs=_BS((None, grp, k, n), lambda j, r: (j, 0, 0, 0)),
        out_shape=jax.ShapeDtypeStruct((s // grp, grp, k, n), F32),
        compiler_params=_params(("parallel", "arbitrary")),
    )(a_in, b_in).reshape(s, k, n)


def _cols_fwd(x, w, name):
    t, k = x.shape
    s, _, n = w.shape
    tm = _pick(t, ROW_TILES)

    def body(x_ref, w_ref, o_ref):
        x_ = x_ref[...]
        for j in range(s):
            o_ref[:, n * j:n * (j + 1)] = _dot(x_, w_ref[j])

    return pl.pallas_call(
        body, name=name, grid=(t // tm,),
        in_specs=[_BS((tm, k), lambda i: (i, 0)), _BS((s, k, n), lambda i: (0, 0, 0))],
        out_specs=_BS((tm, s * n), lambda i: (i, 0)), out_shape=jax.ShapeDtypeStruct((t, s * n), F32),
        compiler_params=_params(("parallel",)),
    )(x, w)


def _cols_dx(d, w, name):
    t = d.shape[0]
    s, k, n = w.shape
    tm = _pick(t, ROW_TILES)

    def body(d_ref, w_ref, o_ref):
        acc = _dot_nt(d_ref[:, 0:n], w_ref[0])
        for j in range(1, s):
            acc = acc + _dot_nt(d_ref[:, n * j:n * (j + 1)], w_ref[j])
        o_ref[...] = acc

    return pl.pallas_call(
        body, name=name, grid=(t // tm,),
        in_specs=[_BS((tm, s * n), lambda i: (i, 0)), _BS((s, k, n), lambda i: (0, 0, 0))],
        out_specs=_BS((tm, k), lambda i: (i, 0)), out_shape=jax.ShapeDtypeStruct((t, k), F32),
        compiler_params=_params(("parallel",)),
    )(d, w)


def _cols_dw(x, d, n, name):
    t, k = x.shape
    s = d.shape[1] // n
    tt = _pick(t, (512, 256, 128))

    def body(x_ref, d_ref, o_ref):
        @pl.when(pl.program_id(0) == 0)
        def _():
            o_ref[...] = jnp.zeros_like(o_ref)

        x_ = x_ref[...]
        for j in range(s):
            o_ref[j] += _dot_tn(x_, d_ref[:, n * j:n * (j + 1)])

    return pl.pallas_call(
        body, name=name, grid=(t // tt,),
        in_specs=[_BS((tt, k), lambda r: (r, 0)), _BS((tt, s * n), lambda r: (r, 0))],
        out_specs=_BS((s, k, n), lambda r: (0, 0, 0)), out_shape=jax.ShapeDtypeStruct((s, k, n), F32),
        compiler_params=_params(("arbitrary",)),
    )(x, d)


def _rows(fn, name, t, tm, ins, vecs, outs, accs=()):
    n_in, n_out, n_acc = len(ins) + len(vecs), len(outs), len(accs)

    def body(*refs):
        res = fn(*refs[:n_in + n_out])
        if n_acc:
            acc_refs = refs[n_in + n_out:]

            @pl.when(pl.program_id(0) == 0)
            def _():
                for r in acc_refs:
                    r[...] = jnp.zeros_like(r)

            for r, val in zip(acc_refs, res):
                r[...] += val

    in_specs = [pl.BlockSpec((tm, bw), functools.partial(lambda i, cb: (i, cb), cb=cb)) for _, bw, cb in ins]
    in_specs += [pl.BlockSpec(v.shape, lambda i: (0, 0)) for v in vecs]
    out_specs = [pl.BlockSpec((tm, w), lambda i: (i, 0)) for w, _ in outs]
    out_specs += [pl.BlockSpec(s, lambda i: (0, 0)) for s in accs]
    out_shape = [jax.ShapeDtypeStruct((t, w), dt) for w, dt in outs]
    out_shape += [jax.ShapeDtypeStruct(s, F32) for s in accs]
    return pl.pallas_call(
        body, name=name, grid=(t // tm,), in_specs=in_specs, out_specs=out_specs, out_shape=out_shape,
        compiler_params=_params(("arbitrary",) if n_acc else ("parallel",)),
    )(*[a for a, _, _ in ins], *vecs)


def _rms(x, g):
    return x * lax.rsqrt(jnp.mean(x * x, axis=-1, keepdims=True) + EPS) * g


def _rms_bwd(x, g, dy):
    xh = x * lax.rsqrt(jnp.mean(x * x, axis=-1, keepdims=True) + EPS)
    r = lax.rsqrt(jnp.mean(x * x, axis=-1, keepdims=True) + EPS)
    dyg = dy * g
    dx = r * (dyg - xh * jnp.mean(dyg * xh, axis=-1, keepdims=True))
    return dx, jnp.sum(dy * xh, axis=0, keepdims=True)


def _hgrn_mats():
    row = lax.broadcasted_iota(jnp.int32, (HB, HB), 0)
    col = lax.broadcasted_iota(jnp.int32, (HB, HB), 1)
    return row, col


def _hgrn_gates(qr, z, t0, t1):
    lb = 1.0 / (1.0 + jnp.exp(t1 - t0))
    th = jnp.tanh(0.5 * z)
    sz, sneg = 0.5 + 0.5 * th, 0.5 - 0.5 * th
    f = lb + (1.0 - lb) * sz
    return lb, sz, sneg, f, jnp.log(f), (1.0 - lb) * sneg, qr * _sig(qr)


def _hgrn_scores(q, k, cum):
    heads, nsub = range(len(q)), HB // SUB
    mids = [[cum[h][SUB * i + SUB // 2 - 1:SUB * i + SUB // 2, :] for i in range(nsub)] for h in heads]
    cmid = [jnp.concatenate([cum[h][SUB * i:SUB * (i + 1)] - mids[h][i] for i in range(nsub)], axis=0) for h in heads]
    qd = [q[h] * jnp.exp(jnp.minimum(cmid[h], EXP_CLAMP)) for h in heads]
    qd_b = [x.astype(BF16) for x in qd]
    kds = [[None] * nsub for _ in heads]
    parts = [[None] * nsub for _ in heads]
    for i in range(nsub):
        for h in heads:
            kds[h][i] = k[h] * jnp.exp(jnp.minimum(mids[h][i] - cum[h], EXP_CLAMP))
        for h in heads:
            parts[h][i] = _dot_nt(qd_b[h][SUB * i:SUB * (i + 1)], kds[h][i])
    return qd, kds, [jnp.concatenate(parts[h], axis=0) for h in heads], cmid


def _hgrn_fwd(p_hg, table, gain, t):
    nblk = t // HB

    def body(q_ref, f_ref, i_ref, g_ref, tab_ref, gain_ref, o_ref, y_ref, st_ref, state):
        @pl.when(pl.program_id(0) == 0)
        def _():
            state[...] = jnp.zeros_like(state)

        row, col = _hgrn_mats()
        causal = col <= row
        tri = causal.astype(BF16)
        heads = range(NH)
        sl = [slice(HD * h, HD * (h + 1)) for h in heads]
        gates = [_hgrn_gates(q_ref[:, sl[h]], f_ref[:, sl[h]], tab_ref[0:1, sl[h]], tab_ref[1:2, sl[h]]) for h in heads]
        lf, k, q = [g[4] for g in gates], [g[5] for g in gates], [g[6] for g in gates]
        v = [i_ref[:, sl[h]] for h in heads]
        cum = [_dot_sel(tri, lf[h]) for h in heads]
        _, _, s, _ = _hgrn_scores(q, k, cum)
        p = [jnp.where(causal, s[h], 0.0) for h in heads]
        st = [state[h] for h in heads]
        o = [_dot(p[h], v[h]) + _dot_nt(q[h] * jnp.exp(cum[h]), st[h]) for h in heads]
        last = [cum[h][HB - 1:HB, :] for h in heads]
        new_st = [st[h] * jnp.exp(last[h]) + _dot_tn(v[h], k[h] * jnp.exp(last[h] - cum[h])) for h in heads]
        for h in heads:
            gr = g_ref[:, sl[h]]
            st_ref[h, 0] = st[h]
            state[h] = new_st[h]
            o_ref[:, sl[h]] = o[h]
            y_ref[:, sl[h]] = (_rms(o[h], gain_ref[...]) * gr * _sig(gr)).astype(BF16)

    blk = lambda cb: pl.BlockSpec((HB, HGW), functools.partial(lambda n, cb: (n, cb), cb=cb))
    return pl.pallas_call(
        body, name="hgrn_fwd", grid=(nblk,),
        in_specs=[blk(0), blk(1), blk(2), blk(3), pl.BlockSpec((2, HGW), lambda n: (0, 0)),
                  pl.BlockSpec((1, HD), lambda n: (0, 0))],
        out_specs=[pl.BlockSpec((HB, HGW), lambda n: (n, 0)), pl.BlockSpec((HB, HGW), lambda n: (n, 0)),
                   pl.BlockSpec((NH, 1, HD, HD), lambda n: (0, n, 0, 0))],
        out_shape=[jax.ShapeDtypeStruct((t, HGW), F32), jax.ShapeDtypeStruct((t, HGW), BF16),
                   jax.ShapeDtypeStruct((NH, nblk, HD, HD), F32)],
        scratch_shapes=[pltpu.VMEM((NH, HD, HD), F32)],
        compiler_params=_params(("arbitrary",)),
    )(p_hg, p_hg, p_hg, p_hg, table, gain)


def _hgrn_bwd(p_hg, table, gain, o_pre, states, dy, t):
    nblk = t // HB

    def body(q_ref, f_ref, i_ref, g_ref, tab_ref, gain_ref, o_ref, st_ref, dy_ref, dp_ref, dtab_ref, dgain_ref,
             dstate):
        @pl.when(pl.program_id(0) == 0)
        def _():
            dstate[...] = jnp.zeros_like(dstate)
            dtab_ref[...] = jnp.zeros_like(dtab_ref)
            dgain_ref[...] = jnp.zeros_like(dgain_ref)

        row, col = _hgrn_mats()
        causal = col <= row
        tri = causal.astype(BF16)
        tri_t = (row <= col).astype(BF16)
        heads, nsub = range(NH), HB // SUB
        sl = [slice(HD * h, HD * (h + 1)) for h in heads]
        rows_of = lambda i: slice(SUB * i, SUB * (i + 1))
        gain_ = gain_ref[...]
        qr, z = [q_ref[:, sl[h]] for h in heads], [f_ref[:, sl[h]] for h in heads]
        v, gr = [i_ref[:, sl[h]] for h in heads], [g_ref[:, sl[h]] for h in heads]
        gates = [_hgrn_gates(qr[h], z[h], tab_ref[0:1, sl[h]], tab_ref[1:2, sl[h]]) for h in heads]
        lb, sz, sneg, f, lf, k, q = ([g[j] for g in gates] for j in range(7))
        cum = [_dot_sel(tri, lf[h]) for h in heads]
        qd, kds, s, cmid = _hgrn_scores(q, k, cum)
        p = [jnp.where(causal, s[h], 0.0) for h in heads]
        st, dst = [st_ref[h, 0] for h in heads], [dstate[h] for h in heads]
        o, dyh, sg = [o_ref[:, sl[h]] for h in heads], [dy_ref[:, sl[h]] for h in heads], [_sig(x) for x in gr]
        dgr = [dyh[h] * _rms(o[h], gain_) * sg[h] * (1.0 + gr[h] * (1.0 - sg[h])) for h in heads]
        norm_bwd = [_rms_bwd(o[h], gain_, dyh[h] * gr[h] * sg[h]) for h in heads]
        do = [x[0] for x in norm_bwd]
        do_b = [x.astype(BF16) for x in do]
        ecum, last = [jnp.exp(x) for x in cum], [x[HB - 1:HB, :] for x in cum]
        qc = [q[h] * ecum[h] for h in heads]
        edec = [jnp.exp(last[h] - cum[h]) for h in heads]
        kdec = [k[h] * edec[h] for h in heads]
        dp = [jnp.where(causal, _dot_nt(do_b[h], v[h]), 0.0) for h in heads]
        dv = [_dot(p[h].T, do_b[h]) + _dot_nt(kdec[h], dst[h]) for h in heads]
        dqc = [_dot(do_b[h], st[h]) for h in heads]
        dkdec = [_dot(v[h], dst[h]) for h in heads]
        new_dst = [dst[h] * jnp.exp(last[h]) + _dot(do[h].T, qc[h]) for h in heads]
        dp_b = [x.astype(BF16) for x in dp]
        dqd = [jnp.concatenate(parts, axis=0) for parts in
               zip(*[[_dot(dp_b[h][rows_of(i)], kds[h][i]) for h in heads] for i in range(nsub)])]
        gq = [dqd[h] * qd[h] for h in heads]
        dq = [dqd[h] * jnp.exp(jnp.minimum(cmid[h], EXP_CLAMP)) + dqc[h] * ecum[h] for h in heads]
        gs = [dkdec[h] * kdec[h] for h in heads]
        dk = [dkdec[h] * edec[h] for h in heads]
        dcum = [dqc[h] * qc[h] - gs[h] + gq[h]
                + jnp.where(row == HB - 1, jnp.sum(gs[h], axis=0, keepdims=True)
                            + jnp.exp(last[h]) * jnp.sum(st[h] * dst[h], axis=0, keepdims=True), 0.0) for h in heads]
        qd_b = [x.astype(BF16) for x in qd]
        for i in range(nsub):
            dkd = [_dot_tn(dp_b[h][rows_of(i)], qd_b[h][rows_of(i)]) for h in heads]
            for h in heads:
                mid = cum[h][SUB * i + SUB // 2 - 1:SUB * i + SUB // 2, :]
                dk[h] = dk[h] + dkd[h] * jnp.exp(jnp.minimum(mid - cum[h], EXP_CLAMP))
                gk = dkd[h] * kds[h][i]
                to_mid = jnp.sum(gk, axis=0, keepdims=True) - jnp.sum(gq[h][rows_of(i)], axis=0, keepdims=True)
                dcum[h] = dcum[h] - gk + jnp.where(row == SUB * i + SUB // 2 - 1, to_mid, 0.0)
        dlf = [_dot_sel(tri_t, dcum[h]) for h in heads]
        dgain = jnp.zeros((1, HD), F32)
        for h in heads:
            df = dlf[h] / f[h] - dk[h]
            dlb = jnp.sum(df * sneg[h], axis=0, keepdims=True) * lb[h] * (1.0 - lb[h])
            dtab_ref[0:1, sl[h]] += dlb
            dtab_ref[1:2, sl[h]] -= dlb
            sq = _sig(qr[h])
            dstate[h] = new_dst[h]
            dp_ref[:, sl[h]] = (dq[h] * sq * (1.0 + qr[h] * (1.0 - sq))).astype(BF16)
            dp_ref[:, HGW + HD * h:HGW + HD * (h + 1)] = (df * (1.0 - lb[h]) * sz[h] * sneg[h]).astype(BF16)
            dp_ref[:, 2 * HGW + HD * h:2 * HGW + HD * (h + 1)] = dv[h].astype(BF16)
            dp_ref[:, 3 * HGW + HD * h:3 * HGW + HD * (h + 1)] = dgr[h].astype(BF16)
            dgain = dgain + norm_bwd[h][1]
        dgain_ref[...] += dgain

    rev = lambda cb: pl.BlockSpec((HB, HGW), functools.partial(lambda n, cb: (nblk - 1 - n, cb), cb=cb))
    return pl.pallas_call(
        body, name="hgrn_bwd", grid=(nblk,),
        in_specs=[rev(0), rev(1), rev(2), rev(3), pl.BlockSpec((2, HGW), lambda n: (0, 0)),
                  pl.BlockSpec((1, HD), lambda n: (0, 0)), rev(0),
                  pl.BlockSpec((NH, 1, HD, HD), lambda n: (0, nblk - 1 - n, 0, 0)), rev(0)],
        out_specs=[pl.BlockSpec((HB, 4 * HGW), lambda n: (nblk - 1 - n, 0)),
                   pl.BlockSpec((2, HGW), lambda n: (0, 0)), pl.BlockSpec((1, HD), lambda n: (0, 0))],
        out_shape=[jax.ShapeDtypeStruct((t, 4 * HGW), BF16), jax.ShapeDtypeStruct((2, HGW), F32),
                   jax.ShapeDtypeStruct((1, HD), F32)],
        scratch_shapes=[pltpu.VMEM((NH, HD, HD), F32)],
        compiler_params=_params(("arbitrary",)),
    )(p_hg, p_hg, p_hg, p_hg, table, gain, o_pre, states, dy)


def _rope_mat():
    r = lax.broadcasted_iota(jnp.int32, (ROPE, ROPE), 0)
    c = lax.broadcasted_iota(jnp.int32, (ROPE, ROPE), 1)
    half = ROPE // 2
    return ((r == c - half).astype(F32) - (r == c + half).astype(F32)).astype(BF16)


def _mla_prep_fwd(p_mla, cs, sn, wq, wkv, gql, gkvl, gq, gk, t):
    tm = _pick(t, (512, 256, 128))

    def body(p_ref, cs_ref, sn_ref, wq_ref, wkv_ref, gql_ref, gkvl_ref, gq_ref, gk_ref,
             q_ref, k_ref, v_ref):
        rmat = _rope_mat()
        cqn = _rms(p_ref[:, 0:QL], gql_ref[...]).astype(BF16)
        ckvn = _rms(p_ref[:, QL:QL + KVL], gkvl_ref[...]).astype(BF16)
        kpe = p_ref[:, QL + KVL:QL + KVL + ROPE]
        c, s = cs_ref[...], sn_ref[...]
        rot = lambda x: x * c + _sel_dot(x, rmat) * s
        heads = range(NH)
        ssq = lambda x: jnp.sum(x * x, -1, keepdims=True)
        qa = [_dot(cqn, wq_ref[h, :, 0:HD]) for h in heads]
        qr = [_dot(cqn, wq_ref[h, :, HD:QK]) for h in heads]
        kn = [_dot(ckvn, wkv_ref[h, :, 0:HD]) for h in heads]
        vv = [_dot(ckvn, wkv_ref[h, :, HD:2 * HD]) for h in heads]
        kpe_ss = ssq(kpe)
        rq = [lax.rsqrt((ssq(qa[h]) + ssq(qr[h])) / QK + EPS) for h in heads]
        rk = [lax.rsqrt((ssq(kn[h]) + kpe_ss) / QK + EPS) for h in heads]
        q_rope = [rot(qr[h] * rq[h] * gq_ref[:, HD:QK]) for h in heads]
        k_rope = [rot(kpe * rk[h] * gk_ref[:, HD:QK]) for h in heads]
        for h in heads:
            q_ref[h, :, 0:HD] = (qa[h] * rq[h] * gq_ref[:, 0:HD] * (SCALE * LOG2E)).astype(BF16)
            q_ref[h, :, HD:QK] = (q_rope[h] * (SCALE * LOG2E)).astype(BF16)
            k_ref[h, :, 0:HD] = (kn[h] * rk[h] * gk_ref[:, 0:HD]).astype(BF16)
            k_ref[h, :, HD:QK] = k_rope[h].astype(BF16)
            v_ref[h] = vv[h].astype(BF16)

    whole = lambda a: pl.BlockSpec(a.shape, functools.partial(lambda i, nd: (0,) * nd, nd=a.ndim))
    return pl.pallas_call(
        body, name="mla_prep_fwd", grid=(t // tm,),
        in_specs=[pl.BlockSpec((tm, QL + KVL + ROPE), lambda i: (i, 0)), pl.BlockSpec((tm, ROPE), lambda i: (i, 0)),
                  pl.BlockSpec((tm, ROPE), lambda i: (i, 0))] + [whole(a) for a in (wq, wkv, gql, gkvl, gq, gk)],
        out_specs=[pl.BlockSpec((NH, tm, QK), lambda i: (0, i, 0)), pl.BlockSpec((NH, tm, QK), lambda i: (0, i, 0)),
                   pl.BlockSpec((NH, tm, HD), lambda i: (0, i, 0))],
        out_shape=[jax.ShapeDtypeStruct((NH, t, QK), BF16), jax.ShapeDtypeStruct((NH, t, QK), BF16),
                   jax.ShapeDtypeStruct((NH, t, HD), BF16)],
        compiler_params=_params(("parallel",)),
    )(p_mla, cs, sn, wq, wkv, gql, gkvl, gq, gk)


def _mla_prep_bwd(p_mla, cs, sn, wq, wkv, gql, gkvl, gq, gk, dq, dk, dv, t):
    tm = _pick(t, (512, 256, 128))

    def body(p_ref, cs_ref, sn_ref, wq_ref, wkv_ref, gql_ref, gkvl_ref, gq_ref, gk_ref,
             dq_ref, dk_ref, dv_ref,
             dp_ref, dwq_ref, dwkv_ref, dgql_ref, dgkvl_ref, dgq_ref, dgk_ref):
        accs = (dwq_ref, dwkv_ref, dgql_ref, dgkvl_ref, dgq_ref, dgk_ref)

        @pl.when(pl.program_id(0) == 0)
        def _():
            for r in accs:
                r[...] = jnp.zeros_like(r)

        rmat = _rope_mat()
        rmat_t = -rmat
        cq, ckv = p_ref[:, 0:QL], p_ref[:, QL:QL + KVL]
        kpe = p_ref[:, QL + KVL:QL + KVL + ROPE]
        cqn_f, ckvn_f = _rms(cq, gql_ref[...]), _rms(ckv, gkvl_ref[...])
        cqn, ckvn = cqn_f.astype(BF16), ckvn_f.astype(BF16)
        ckvn_t = ckvn_f.T.astype(BF16)
        c, s = cs_ref[...], sn_ref[...]
        unrot = lambda dy: dy * c + _sel_dot(dy * s, rmat_t)
        heads = range(NH)
        rsum = lambda x: jnp.sum(x, -1, keepdims=True)
        csum = lambda x: jnp.sum(x, axis=0, keepdims=True)
        qa = [_dot(cqn, wq_ref[h, :, 0:HD]) for h in heads]
        qr = [_dot(cqn, wq_ref[h, :, HD:QK]) for h in heads]
        kn = [_dot(ckvn, wkv_ref[h, :, 0:HD]) for h in heads]
        dyr = [unrot(dq_ref[h, :, HD:QK] * SCALE) for h in heads]
        dkr = [unrot(dk_ref[h, :, HD:QK]) for h in heads]
        rq = [lax.rsqrt((rsum(qa[h] * qa[h]) + rsum(qr[h] * qr[h])) / QK + EPS) for h in heads]
        xa, xr = [qa[h] * rq[h] for h in heads], [qr[h] * rq[h] for h in heads]
        dya = [dq_ref[h, :, 0:HD] * SCALE for h in heads]
        ga, gr_ = [dya[h] * gq_ref[:, 0:HD] for h in heads], [dyr[h] * gq_ref[:, HD:QK] for h in heads]
        mean = [(rsum(ga[h] * xa[h]) + rsum(gr_[h] * xr[h])) / QK for h in heads]
        dqa = [(rq[h] * (ga[h] - xa[h] * mean[h])).astype(BF16) for h in heads]
        dqr = [(rq[h] * (gr_[h] - xr[h] * mean[h])).astype(BF16) for h in heads]
        kpe_ss = rsum(kpe * kpe)
        rk = [lax.rsqrt((rsum(kn[h] * kn[h]) + kpe_ss) / QK + EPS) for h in heads]
        ya, yr = [kn[h] * rk[h] for h in heads], [kpe * rk[h] for h in heads]
        dka = [dk_ref[h, :, 0:HD] for h in heads]
        ha, hr = [dka[h] * gk_ref[:, 0:HD] for h in heads], [dkr[h] * gk_ref[:, HD:QK] for h in heads]
        mean_k = [(rsum(ha[h] * ya[h]) + rsum(hr[h] * yr[h])) / QK for h in heads]
        dkn = [(rk[h] * (ha[h] - ya[h] * mean_k[h])).astype(BF16) for h in heads]
        dvh = [dv_ref[h].astype(BF16) for h in heads]
        dw = [(_dot_tn(dqa[h], cqn), _dot_tn(dqr[h], cqn), _dot(ckvn_t, dkn[h]), _dot(ckvn_t, dvh[h])) for h in heads]
        back_q = [_dot_nt(dqa[h], wq_ref[h, :, 0:HD]) + _dot_nt(dqr[h], wq_ref[h, :, HD:QK]) for h in heads]
        back_kv = [_dot_nt(dkn[h], wkv_ref[h, :, 0:HD]) + _dot_nt(dvh[h], wkv_ref[h, :, HD:2 * HD]) for h in heads]
        dcqn, dckvn = sum(back_q[1:], back_q[0]), sum(back_kv[1:], back_kv[0])
        dkpe = sum([rk[h] * (hr[h] - yr[h] * mean_k[h]) for h in heads][1:], rk[0] * (hr[0] - yr[0] * mean_k[0]))
        dgq_a = sum([csum(dya[h] * xa[h]) for h in heads][1:], csum(dya[0] * xa[0]))
        dgq_r = sum([csum(dyr[h] * xr[h]) for h in heads][1:], csum(dyr[0] * xr[0]))
        dgk_a = sum([csum(dka[h] * ya[h]) for h in heads][1:], csum(dka[0] * ya[0]))
        dgk_r = sum([csum(dkr[h] * yr[h]) for h in heads][1:], csum(dkr[0] * yr[0]))
        for h in heads:
            dwq_ref[h, 0:HD, :] += dw[h][0]
            dwq_ref[h, HD:QK, :] += dw[h][1]
            dwkv_ref[h, :, 0:HD] += dw[h][2]
            dwkv_ref[h, :, HD:2 * HD] += dw[h][3]
        dcq, dg1 = _rms_bwd(cq, gql_ref[...], dcqn)
        dckv, dg2 = _rms_bwd(ckv, gkvl_ref[...], dckvn)
        dp_ref[:, 0:QL] = dcq.astype(BF16)
        dp_ref[:, QL:QL + KVL] = dckv.astype(BF16)
        dp_ref[:, QL + KVL:QL + KVL + ROPE] = dkpe.astype(BF16)
        dgql_ref[...] += dg1
        dgkvl_ref[...] += dg2
        dgq_ref[:, 0:HD] += dgq_a
        dgq_ref[:, HD:QK] += dgq_r
        dgk_ref[:, 0:HD] += dgk_a
        dgk_ref[:, HD:QK] += dgk_r

    whole = lambda a: pl.BlockSpec(a.shape, functools.partial(lambda i, nd: (0,) * nd, nd=a.ndim))
    acc_shapes = [(NH, QK, QL), wkv.shape, gql.shape, gkvl.shape, gq.shape, gk.shape]
    return pl.pallas_call(
        body, name="mla_prep_bwd", grid=(t // tm,),
        in_specs=[pl.BlockSpec((tm, QL + KVL + ROPE), lambda i: (i, 0)), pl.BlockSpec((tm, ROPE), lambda i: (i, 0)),
                  pl.BlockSpec((tm, ROPE), lambda i: (i, 0))]
        + [whole(a) for a in (wq, wkv, gql, gkvl, gq, gk)]
        + [pl.BlockSpec((NH, tm, QK), lambda i: (0, i, 0)), pl.BlockSpec((NH, tm, QK), lambda i: (0, i, 0)),
           pl.BlockSpec((NH, tm, HD), lambda i: (0, i, 0))],
        out_specs=[pl.BlockSpec((tm, QL + KVL + ROPE), lambda i: (i, 0))]
        + [pl.BlockSpec(s, functools.partial(lambda i, nd: (0,) * nd, nd=len(s))) for s in acc_shapes],
        out_shape=[jax.ShapeDtypeStruct((t, QL + KVL + ROPE), BF16)]
        + [jax.ShapeDtypeStruct(s, F32) for s in acc_shapes],
        compiler_params=_params(("arbitrary",)),
    )(p_mla, cs, sn, wq, wkv, gql, gkvl, gq, gk, dq, dk, dv)


def _chunk_mask(nq, nk, key0, keys_on_rows):
    shape = (nk, nq) if keys_on_rows else (nq, nk)
    qi = lax.broadcasted_iota(jnp.int32, shape, 1 if keys_on_rows else 0) // CHUNK
    ki = lax.broadcasted_iota(jnp.int32, shape, 0 if keys_on_rows else 1) // CHUNK + key0 // CHUNK
    return ki <= qi


def _flash_fwd(q, k, v, t):
    tq = _pick(t, ATT_Q)
    tk = tq // ATT_KEY_TILES

    def body(q_ref, k_ref, v_ref, o_ref, lse_ref):
        i = pl.program_id(1)
        qt = q_ref[0]

        def step(j, carry, key0):
            m, l, acc = carry
            cols = pl.ds(pl.multiple_of(j * tk, tk), tk)
            s = _dot_nt(qt, k_ref[0, cols, :])
            if key0 is not None:
                s = jnp.where(_chunk_mask(tq, tk, key0, False), s, -jnp.inf)
            m_new = jnp.maximum(m, jnp.max(s, axis=-1, keepdims=True))
            p = jnp.exp2(s - m_new)
            alpha = jnp.exp2(m - m_new)
            return m_new, alpha * l + jnp.sum(p, axis=-1, keepdims=True), alpha * acc + _dot(p, v_ref[0, cols, :])

        init = (jnp.full((tq, 1), -jnp.inf, F32), jnp.zeros((tq, 1), F32), jnp.zeros((tq, HD), F32))
        carry = lax.fori_loop(0, ATT_KEY_TILES * i, lambda j, cr: step(j, cr, None), init)
        for h in range(ATT_KEY_TILES):
            carry = step(ATT_KEY_TILES * i + h, carry, h * tk)
        m, l, acc = carry
        o_ref[...] = (acc / l).astype(BF16)
        lse_ref[0] = m + jnp.log2(l)

    return pl.pallas_call(
        body, name="flash_fwd", grid=(NH, t // tq),
        in_specs=[pl.BlockSpec((1, tq, QK), lambda h, i: (h, i, 0)), pl.BlockSpec((1, t, QK), lambda h, i: (h, 0, 0)),
                  pl.BlockSpec((1, t, HD), lambda h, i: (h, 0, 0))],
        out_specs=[pl.BlockSpec((tq, HD), lambda h, i: (i, h)), pl.BlockSpec((1, tq, 1), lambda h, i: (h, i, 0))],
        out_shape=[jax.ShapeDtypeStruct((t, NH * HD), BF16), jax.ShapeDtypeStruct((NH, t, 1), F32)],
        compiler_params=_params(("parallel", "parallel")),
    )(q, k, v)


def _attn_out_bwd(dy, w, o, t):
    tm = _pick(t, ATT_TILES)

    def body(dy_ref, w_ref, o_ref, do_ref, delta_ref):
        do = _dot_nt(dy_ref[...], w_ref[...]).astype(BF16)
        do_ref[...] = do
        ones = jnp.ones((SUBLANE, HD), BF16)
        rowsum = lambda p: lax.dot_general(ones, p, (((1,), (1,)), ((), ())), preferred_element_type=F32)
        for h in range(NH):
            sl = slice(HD * h, HD * (h + 1))
            x1, x2, x3 = _split3(do[:, sl].astype(F32) * o_ref[:, sl].astype(F32))
            delta_ref[h] = (rowsum(x1) + rowsum(x2) + rowsum(x3))[0:1, :]

    return pl.pallas_call(
        body, name="attn_out_bwd", grid=(t // tm,),
        in_specs=[_BS((tm, dy.shape[1]), lambda i: (i, 0)), _BS(w.shape, lambda i: (0, 0)),
                  _BS((tm, NH * HD), lambda i: (i, 0))],
        out_specs=[_BS((tm, NH * HD), lambda i: (i, 0)), _BS((NH, 1, tm), lambda i: (0, 0, i))],
        out_shape=[jax.ShapeDtypeStruct((t, NH * HD), BF16), jax.ShapeDtypeStruct((NH, 1, t), F32)],
        compiler_params=_params(("parallel",)),
    )(dy, w, o)


def _flash_bwd(q, k, v, do, lse_row, delta_row, t):
    tq = _pick(t, ATT_Q)
    tk = tq // ATT_KEY_TILES

    def body(q_ref, k_ref, v_ref, do_ref, lse_ref, delta_ref, dq_ref, dk_ref, dv_ref):
        j = pl.program_id(1)

        @pl.when(j == 0)
        def _():
            dq_ref[...] = jnp.zeros_like(dq_ref)

        kt, vt = k_ref[0], v_ref[0]

        def step(i, carry, key0):
            dk, dv = carry
            rows = pl.ds(pl.multiple_of(i * tq, tq), tq)
            qt, dot_ = q_ref[0, rows, :], do_ref[rows, :]
            p = jnp.exp2(_dot_nt(kt, qt) - lse_ref[0, :, rows])
            if key0 is not None:
                p = jnp.where(_chunk_mask(tq, tk, key0, True), p, 0.0)
            ds = (p * (_dot_nt(vt, dot_) - delta_ref[0, :, rows])).astype(BF16)
            dq_ref[0, rows, :] += _dot_tn(ds, kt)
            return dk + _dot(ds, qt), dv + _dot(p, dot_)

        own = j // ATT_KEY_TILES
        carry = step(own, (jnp.zeros((tk, QK), F32), jnp.zeros((tk, HD), F32)), (j % ATT_KEY_TILES) * tk)
        dk, dv = lax.fori_loop(own + 1, t // tq, lambda i, cr: step(i, cr, None), carry)
        dk_ref[0] = dk * (1.0 / LOG2E)
        dv_ref[0] = dv

    return pl.pallas_call(
        body, name="flash_bwd", grid=(NH, t // tk),
        in_specs=[pl.BlockSpec((1, t, QK), lambda h, j: (h, 0, 0)), pl.BlockSpec((1, tk, QK), lambda h, j: (h, j, 0)),
                  pl.BlockSpec((1, tk, HD), lambda h, j: (h, j, 0)), pl.BlockSpec((t, HD), lambda h, j: (0, h)),
                  pl.BlockSpec((1, 1, t), lambda h, j: (h, 0, 0)), pl.BlockSpec((1, 1, t), lambda h, j: (h, 0, 0))],
        out_specs=[pl.BlockSpec((1, t, QK), lambda h, j: (h, 0, 0)), pl.BlockSpec((1, tk, QK), lambda h, j: (h, j, 0)),
                   pl.BlockSpec((1, tk, HD), lambda h, j: (h, j, 0))],
        out_shape=[jax.ShapeDtypeStruct((NH, t, QK), F32), jax.ShapeDtypeStruct((NH, t, QK), F32),
                   jax.ShapeDtypeStruct((NH, t, HD), F32)],
        compiler_params=_params(("parallel", "arbitrary")),
    )(q, k, v, do, lse_row, delta_row)


def _ffn_in(xn, w_in, name):
    t, k = xn.shape
    s, _, n = w_in.shape
    tm = _pick(t, ROW_TILES)

    def body(x_ref, w_ref, gu_ref, a_ref):
        x = x_ref[...]
        g, u = _dot(x, w_ref[0]), _dot(x, w_ref[1])
        gu_ref[0] = g.astype(BF16)
        gu_ref[1] = u.astype(BF16)
        a_ref[...] = (g * _sig(g) * u).astype(BF16)

    return pl.pallas_call(
        body, name=name, grid=(t // tm, s // 2),
        in_specs=[_BS((tm, k), lambda i, j: (i, 0)), _BS((2, None, k, n), lambda i, j: (0, j, 0, 0))],
        out_specs=[_BS((2, None, tm, n), lambda i, j: (0, j, i, 0)), _BS((None, tm, n), lambda i, j: (j, i, 0))],
        out_shape=[jax.ShapeDtypeStruct((2, s // 2, t, n), BF16), jax.ShapeDtypeStruct((s // 2, t, n), BF16)],
        compiler_params=_params(("parallel", "parallel")),
    )(xn, w_in.reshape(2, s // 2, k, n))


def _ffn_dgu(dfo, w_out, gu, name):
    t, k = dfo.shape
    s, n, _ = w_out.shape
    tm = _pick(t, ROW_TILES)

    def body(d_ref, w_ref, gu_ref, o_ref):
        da = _dot_nt(d_ref[...], w_ref[...])
        g, u = gu_ref[0].astype(F32), gu_ref[1].astype(F32)
        sg = _sig(g)
        o_ref[0] = (da * u * sg * (1.0 + g * (1.0 - sg))).astype(BF16)
        o_ref[1] = (da * g * sg).astype(BF16)

    pair = _BS((2, None, tm, n), lambda i, j: (0, j, i, 0))
    return pl.pallas_call(
        body, name=name, grid=(t // tm, s),
        in_specs=[_BS((tm, k), lambda i, j: (i, 0)), _BS((None, n, k), lambda i, j: (j, 0, 0)), pair],
        out_specs=pair, out_shape=jax.ShapeDtypeStruct((2, s, t, n), BF16),
        compiler_params=_params(("parallel", "parallel")),
    )(dfo, w_out, gu).reshape(2 * s, t, n)


def _ffn_fwd(xn, w_in, w_out, tag, epilogue):
    gu, a = _ffn_in(xn, w_in, tag + "_in")
    return gu, a, _mm_stack_red(a, w_out, tag + "_out", "nn", epilogue)


def _ffn_bwd(dfo, xn, gu, a, w_in, w_out, tag, epilogue):
    dw_out = _mm_stack_tn(a, dfo, tag + "_dwout")
    dgu = _ffn_dgu(dfo, w_out, gu, tag + "_dgu")
    dw_in = _mm_stack_tn(dgu, xn, tag + "_dwin")
    if callable(epilogue):
        epilogue = epilogue(dw_in, dw_out)
    return _mm_stack_red(dgu, w_in, tag + "_dxn", "nt", epilogue), dw_in, dw_out


def _local_step(x, target, cs, sn, w, late_weights, s, early_grads, last_grads):
    t = x.shape[0]
    tm = _pick(t, (256, 128))
    g = {}
    ffn_out = lambda n: w[n].reshape(4, FF // 4, D)

    def norm_fn(x_ref, g_ref, o_ref):
        o_ref[...] = _rms(x_ref[...], g_ref[...]).astype(BF16)

    xn1 = _rows(norm_fn, "norm1", t, tm, [(x, D, 0)], [s["ffn1_norm"]], [(D, BF16)])[0]
    def res_norm_fn(scale):
        def fn(h_ref, f_ref, g_ref, h_out, n_out):
            h = h_ref[...] + scale * f_ref[...]
            h_out[...] = h
            n_out[...] = _rms(h, g_ref[...]).astype(BF16)
        return fn

    gu1, a1, (h1, u) = _ffn_fwd(xn1, w["ffn1_w_in"], ffn_out("ffn1_w_out"), "ffn1",
                                (res_norm_fn(0.5), x, [], [s["mix_norm"]], [(D, F32), (D, BF16)], []))
    w = {**w, **late_weights(h1)}
    rows_of = lambda n: w[n].reshape(-1, w[n].shape[2])
    w_hgb, w_mlab, w_o = rows_of("w_hg_branch"), rows_of("w_mla_branch"), rows_of("w_out")
    w_in_nat = w["w_in"].transpose(1, 0, 2).reshape(D, -1)
    w_mrg = w["w_merge"]
    mw = w_mrg.shape[2]
    w_in_hg, w_in_mla = w_in_nat[:, :4 * HGW], w_in_nat[:, 4 * HGW:]
    p_hg = _mm(u, w_in_hg, "nn", "proj_hg")
    p_mla = _mm(u, w_in_mla, "nn", "proj_mla")
    gpre = _cols_fwd(u, w_mrg, "proj_gate")
    o_pre, hgy, states = _hgrn_fwd(p_hg, s["hg_lb_table"], s["hg_out_norm"], t)
    prep_args = (p_mla, cs, sn, w["w_q_up"], w["w_kv_up"], s["mla_q_lora_norm"], s["mla_kv_lora_norm"],
                 s["q_head_norm"], s["k_head_norm"])
    q, k, v = _mla_prep_fwd(*prep_args, t)
    att, lse = _flash_fwd(q, k, v, t)
    y_hg = _mm(hgy, w_hgb, "nn", "branch_hg")
    y_mla = _mm(att, w_mlab, "nn", "branch_mla")

    def mix_fn(gh_ref, gm_ref, yh_ref, ym_ref, b_ref, o_ref):
        gh = _sig(gh_ref[...] + b_ref[:, 0:D])
        gm = _sig(gm_ref[...] + b_ref[:, D:2 * D])
        o_ref[...] = (gh * yh_ref[...] + gm * ym_ref[...]).astype(BF16)

    mixed = _rows(mix_fn, "mix", t, tm, [(gpre, D, 0), (gpre, D, 1), (y_hg, D, 0), (y_mla, D, 0)], [s["b_merge"]],
                  [(D, BF16)])[0]
    h2, xn2 = _mm_stack_red(mixed[None], w_o[None], "mix_out", "nn",
                            (res_norm_fn(1.0), h1, [], [s["ffn2_norm"]], [(D, F32), (D, BF16)], []))

    def loss_fn(h_ref, f_ref, tg_ref, g_ref, dh_out, dhb_out):
        h = h_ref[...] + 0.5 * f_ref[...]
        e = _rms(h, g_ref[...]) - tg_ref[...]
        dh, dgain = _rms_bwd(h, g_ref[...], e / D)
        dh_out[...] = dh
        dhb_out[...] = (0.5 * dh).astype(BF16)
        return dgain, jnp.full((1, LANE), 0.5 / D * jnp.sum(e * e), F32)

    gu2, a2, (dh3, dfo2, g["final_norm"], loss) = _ffn_fwd(
        xn2, w["ffn2_w_in"], ffn_out("ffn2_w_out"), "ffn2",
        (loss_fn, h2, [target], [s["final_norm"]], [(D, F32), (D, BF16)], [(1, D), (1, LANE)]))

    def norm_bwd_fn(scale):
        def fn(h_ref, dxn_ref, dh_ref, g_ref, dh_out, dhb_out):
            dx, dgain = _rms_bwd(h_ref[...], g_ref[...], dxn_ref[...])
            dh = dh_ref[...] + dx
            dh_out[...] = dh
            dhb_out[...] = (scale * dh).astype(BF16)
            return (dgain,)
        return fn

    as_rows = lambda a: a.reshape((N_DEV, -1) + a.shape[-1:])
    (dh2, dh2b, g["ffn2_norm"]), g["ffn2_w_in"], dwo = _ffn_bwd(
        dfo2, xn2, gu2, a2, w["ffn2_w_in"], ffn_out("ffn2_w_out"), "ffn2",
        (norm_bwd_fn(1.0), h2, [dh3], [s["ffn2_norm"]], [(D, F32), (D, BF16)], [(1, D)]))
    g["ffn2_w_out"] = as_rows(dwo)
    dmixed = _mm(dh2b, w_o, "nt", "mix_out_dx")
    g["w_out"] = as_rows(_mm(mixed, dh2b, "tn", "mix_out_dw"))

    def mix_bwd_fn(gh_ref, gm_ref, yh_ref, ym_ref, dm_ref, b_ref, dyh_out, dym_out, dg_out):
        gh = _sig(gh_ref[...] + b_ref[:, 0:D])
        gm = _sig(gm_ref[...] + b_ref[:, D:2 * D])
        dm = dm_ref[...]
        dyh_out[...] = (dm * gh).astype(BF16)
        dym_out[...] = (dm * gm).astype(BF16)
        dgh = dm * yh_ref[...] * gh * (1.0 - gh)
        dgm = dm * ym_ref[...] * gm * (1.0 - gm)
        dg_out[:, 0:D] = dgh.astype(BF16)
        dg_out[:, D:2 * D] = dgm.astype(BF16)
        return (jnp.concatenate([jnp.sum(dgh, axis=0, keepdims=True), jnp.sum(dgm, axis=0, keepdims=True)], axis=1),)

    dyh, dym, dgpre, g["b_merge"] = _rows(
        mix_bwd_fn, "mix_bwd", t, tm, [(gpre, D, 0), (gpre, D, 1), (y_hg, D, 0), (y_mla, D, 0), (dmixed, D, 0)],
        [s["b_merge"]], [(D, BF16), (D, BF16), (2 * D, BF16)], [(1, 2 * D)])
    g["w_hg_branch"] = as_rows(_mm(hgy, dyh, "tn", "branch_hg_dw"))
    g["w_mla_branch"] = as_rows(_mm(att, dym, "tn", "branch_mla_dw"))
    g["w_merge"] = _cols_dw(u, dgpre, mw, "proj_gate_dw")
    dhgy = _mm(dyh, w_hgb, "nt", "branch_hg_dx")
    datt, delta = _attn_out_bwd(dym, w_mlab, att, t)
    du_gate = _cols_dx(dgpre, w_mrg, "proj_gate_dx")

    dq, dk, dv = _flash_bwd(q, k, v, datt, lse.reshape(NH, 1, t), delta, t)
    (dp_mla, g["w_q_up"], g["w_kv_up"], g["mla_q_lora_norm"], g["mla_kv_lora_norm"], g["q_head_norm"],
     g["k_head_norm"]) = _mla_prep_bwd(*prep_args, dq, dk, dv, t)
    dp_hg, g["hg_lb_table"], g["hg_out_norm"] = _hgrn_bwd(p_hg, s["hg_lb_table"], s["hg_out_norm"], o_pre, states,
                                                          dhgy, t)
    dw_in_nat = jnp.concatenate([_mm(u, dp_hg, "tn", "proj_hg_dw"), _mm(u, dp_mla, "tn", "proj_mla_dw")], axis=1)
    g["w_in"] = dw_in_nat.T.reshape(N_DEV, -1, D)
    du_hg = _mm(dp_hg, w_in_hg, "nt", "proj_hg_dx")
    du_mla = _mm(dp_mla, w_in_mla, "nt", "proj_mla_dx")

    def mixnorm_bwd_fn(h_ref, a_ref, b_ref, c_ref, dh_ref, g_ref, dh_out, dhb_out):
        dx, dgain = _rms_bwd(h_ref[...], g_ref[...], a_ref[...] + b_ref[...] + c_ref[...])
        dh = dh_ref[...] + dx
        dh_out[...] = dh
        dhb_out[...] = (0.5 * dh).astype(BF16)
        return (dgain,)

    mix_gain = s["mix_norm"] + early_grads(g)[0:1, 0:1]
    dh1, dfo1, g["mix_norm"] = _rows(mixnorm_bwd_fn, "mixnorm_bwd", t, tm,
                                     [(h1, D, 0), (du_hg, D, 0), (du_mla, D, 0), (du_gate, D, 0), (dh2, D, 0)],
                                     [mix_gain], [(D, F32), (D, BF16)], [(1, D)])
    def last_stage(dw_in, dw_out):
        g["ffn1_w_in"], g["ffn1_w_out"] = dw_in, as_rows(dw_out)
        gain = s["ffn1_norm"] + last_grads(g)[0:1, 0:1]
        return norm_bwd_fn(1.0), x, [dh1], [gain], [(D, F32), (D, BF16)], [(1, D)]

    (grad_x, _, g["ffn1_norm"]), _, _ = _ffn_bwd(dfo1, xn1, gu1, a1, w["ffn1_w_in"], ffn_out("ffn1_w_out"), "ffn1",
                                                 last_stage)
    return loss, grad_x, g


def _coords():
    return lax.axis_index("x"), lax.axis_index("y"), lax.axis_index("c")


def _hbm_call(body, name, ins, out_shapes, scratch):
    any_spec = pl.BlockSpec(memory_space=pl.ANY)
    return pl.pallas_call(
        body, name=name, out_shape=[jax.ShapeDtypeStruct(s, dt) for s, dt in out_shapes],
        in_specs=[any_spec] * len(ins), out_specs=[any_spec] * len(out_shapes), scratch_shapes=scratch,
    )(*ins)


def _my_slot():
    return 4 * lax.axis_index("x") + 2 * lax.axis_index("y") + lax.axis_index("c")


def _put_own(buf, own, index):
    return lax.dynamic_update_index_in_dim(buf, own, index, 0)


def _all_gather(blocks, name):
    nb = len(blocks)

    def body(*refs):
        x_refs, out_refs = refs[:nb], refs[nb:2 * nb]
        send_sems, recv_sems = refs[2 * nb:]
        x, y, c = _coords()
        me, sibling = (x, y, c), (x, y, 1 - c)
        chips = [(1 - x, y), (x, 1 - y), (1 - x, 1 - y)]

        def slot(b, px, py, pc):
            return out_refs[b].at[4 * px + 2 * py + pc]

        def copy(b, kk, block_of, to, src=None):
            return pltpu.make_async_remote_copy(
                src_ref=slot(b, *block_of) if src is None else src, dst_ref=slot(b, *block_of),
                send_sem=send_sems.at[b, kk], recv_sem=recv_sems.at[b, kk], device_id=to, device_id_type=MESH)

        first = [copy(b, 0, me, sibling, src=x_refs[b]) for b in range(nb)]
        first += [copy(b, 1 + j, me, (*chip, c), src=x_refs[b]) for j, chip in enumerate(chips) for b in range(nb)]
        for cp in first:
            cp.start()
        passed = []
        for j, chip in enumerate(chips):
            for b in range(nb):
                copy(b, 1 + j, (*chip, c), me).wait_recv()
                passed.append(copy(b, 4 + j, (*chip, c), sibling))
                passed[-1].start()
        for b in range(nb):
            copy(b, 0, sibling, me).wait_recv()
        for j, chip in enumerate(chips):
            for b in range(nb):
                copy(b, 4 + j, (*chip, 1 - c), me).wait_recv()
        for cp in first + passed:
            cp.wait_send()

    outs = _hbm_call(body, name, blocks, [((N_DEV,) + b.shape, b.dtype) for b in blocks],
                     [pltpu.SemaphoreType.DMA((nb, 7)), pltpu.SemaphoreType.DMA((nb, 7))])
    return [_put_own(o, b[None], _my_slot()) for o, b in zip(outs, blocks)]


def _gather_peers():
    x, y, c = _coords()
    return (x, y, c), [(x, y, 1 - c), (1 - x, y, c), (x, 1 - y, c), (1 - x, 1 - y, c)]


def _gather_start(blocks, after, name):
    nb = len(blocks)
    hbm, sem = pl.BlockSpec(memory_space=pltpu.HBM), pl.BlockSpec(memory_space=pltpu.SEMAPHORE)

    def body(*refs):
        x_refs, out_refs = refs[:nb], refs[nb:2 * nb]
        send_sems, recv_sems, token = refs[2 * nb + 1], refs[2 * nb + 2], refs[-1]
        (x, y, c), peers = _gather_peers()
        for kk, peer in enumerate(peers):
            for b in range(nb):
                pltpu.make_async_remote_copy(
                    src_ref=x_refs[b], dst_ref=out_refs[b].at[4 * x + 2 * y + c], send_sem=send_sems.at[4 * b + kk],
                    recv_sem=recv_sems.at[4 * b + kk], device_id=peer, device_id_type=MESH).start()
        token[...] = jnp.zeros_like(token)

    gathers = [pltpu.with_memory_space_constraint(lax.empty((N_DEV,) + b.shape, b.dtype), pltpu.HBM) for b in blocks]
    outs = pl.pallas_call(
        body, name=name,
        out_shape=(pltpu.SemaphoreType.DMA((4 * nb,)), pltpu.SemaphoreType.DMA((4 * nb,)),
                   *[pltpu.HBM(b.shape, b.dtype) for b in blocks], *[pltpu.HBM(b.shape, b.dtype) for b in gathers],
                   jax.ShapeDtypeStruct((SUBLANE, LANE), F32)),
        in_specs=[hbm] * (2 * nb) + [pl.BlockSpec(memory_space=pl.ANY)],
        out_specs=(sem, sem, *[hbm] * (2 * nb), pl.BlockSpec(memory_space=pltpu.VMEM)),
        input_output_aliases={i: 2 + i for i in range(2 * nb)},
        compiler_params=pltpu.CompilerParams(has_side_effects=pltpu.SideEffectType.DATAFLOW_SIDE_EFFECTING),
    )(*[pltpu.with_memory_space_constraint(b, pltpu.HBM) for b in blocks], *gathers, after)
    return outs[0], outs[1], list(outs[2:2 + nb]), list(outs[2 + nb:2 + 2 * nb]), outs[-1]


def _gather_wait(send_sems, recv_sems, thru, gathers, after, name):
    nb = len(thru)
    hbm, sem = pl.BlockSpec(memory_space=pltpu.HBM), pl.BlockSpec(memory_space=pltpu.SEMAPHORE)

    def body(*refs):
        x_refs, out_refs = refs[:nb], refs[nb:2 * nb]
        send_sems_, recv_sems_ = refs[2 * nb], refs[2 * nb + 1]
        _, peers = _gather_peers()
        for kk, (px, py, pc) in enumerate(peers):
            for b in range(nb):
                cp = pltpu.make_async_remote_copy(
                    src_ref=x_refs[b], dst_ref=out_refs[b].at[4 * px + 2 * py + pc], send_sem=send_sems_.at[4 * b + kk],
                    recv_sem=recv_sems_.at[4 * b + kk], device_id=(px, py, pc), device_id_type=MESH)
                cp.wait_send()
                cp.wait_recv()

    outs = pl.pallas_call(
        body, name=name,
        out_shape=(*[pltpu.HBM(b.shape, b.dtype) for b in thru], *[pltpu.HBM(b.shape, b.dtype) for b in gathers]),
        in_specs=[hbm] * (2 * nb) + [sem, sem, pl.BlockSpec(memory_space=pl.ANY)], out_specs=[hbm] * (2 * nb),
        input_output_aliases={i: i for i in range(2 * nb)},
        compiler_params=pltpu.CompilerParams(has_side_effects=pltpu.SideEffectType.DATAFLOW_SIDE_EFFECTING),
    )(*thru, *gathers, send_sems, recv_sems, after)
    return list(outs[:nb]), list(outs[nb:])


def _gather_finish(blocks, gathers, name):
    nb = len(blocks)

    def body(*refs):
        x_refs, in_refs, out_refs = refs[:nb], refs[nb:2 * nb], refs[2 * nb:3 * nb]
        send_sems, recv_sems = refs[3 * nb:]
        (x, y, c), peers = _gather_peers()
        copies = []
        for j, (px, py, _) in enumerate(peers[1:]):
            for b in range(nb):
                copies.append(pltpu.make_async_remote_copy(
                    src_ref=in_refs[b].at[4 * px + 2 * py + c], dst_ref=out_refs[b].at[4 * px + 2 * py + c],
                    send_sem=send_sems.at[b, j], recv_sem=recv_sems.at[b, j], device_id=(x, y, 1 - c),
                    device_id_type=MESH))
                copies[-1].start()
        for j, (px, py, _) in enumerate(peers[1:]):
            for b in range(nb):
                pltpu.make_async_remote_copy(
                    src_ref=in_refs[b].at[4 * px + 2 * py + c], dst_ref=out_refs[b].at[4 * px + 2 * py + 1 - c],
                    send_sem=send_sems.at[b, j], recv_sem=recv_sems.at[b, j], device_id=(x, y, 1 - c),
                    device_id_type=MESH).wait_recv()
        for cp in copies:
            cp.wait_send()

    any_spec = pl.BlockSpec(memory_space=pl.ANY)
    outs = pl.pallas_call(
        body, name=name, out_shape=[jax.ShapeDtypeStruct(b.shape, b.dtype) for b in gathers],
        in_specs=[any_spec] * (2 * nb), out_specs=[any_spec] * nb,
        input_output_aliases={nb + i: i for i in range(nb)},
        scratch_shapes=[pltpu.SemaphoreType.DMA((nb, 3)), pltpu.SemaphoreType.DMA((nb, 3))],
    )(*blocks, *gathers)
    return [_put_own(o, b[None], _my_slot()) for o, b in zip(outs, blocks)]


def _sibling_swap(bufs, name):
    nb = len(bufs)

    def body(*refs):
        x_refs, out_refs = refs[:nb], refs[nb:2 * nb]
        send_sems, recv_sems = refs[2 * nb:]
        x, y, c = _coords()
        copies = [pltpu.make_async_remote_copy(
            src_ref=x_refs[b].at[2 * q + 1 - c], dst_ref=out_refs[b].at[q], send_sem=send_sems.at[b, q],
            recv_sem=recv_sems.at[b, q], device_id=(x, y, 1 - c), device_id_type=MESH)
            for b in range(nb) for q in range(4)]
        for cp in copies:
            cp.start()
        for cp in copies:
            cp.wait()

    return _hbm_call(body, name, bufs, [((4,) + b.shape[1:], b.dtype) for b in bufs],
                     [pltpu.SemaphoreType.DMA((nb, 4)), pltpu.SemaphoreType.DMA((nb, 4))])


def _chip_exchange_start(bufs, name):
    nb = len(bufs)
    hbm, sem = pl.BlockSpec(memory_space=pltpu.HBM), pl.BlockSpec(memory_space=pltpu.SEMAPHORE)

    def body(*refs):
        x_refs, land_refs = refs[:nb], refs[nb:2 * nb]
        send_sems, recv_sems, token = refs[2 * nb], refs[2 * nb + 1], refs[-1]
        x, y, c = _coords()
        for j, (px, py) in enumerate([(1 - x, y), (x, 1 - y), (1 - x, 1 - y)]):
            for b in range(nb):
                pltpu.make_async_remote_copy(
                    src_ref=x_refs[b].at[2 * px + py], dst_ref=land_refs[b].at[2 * x + y], send_sem=send_sems.at[3 * b + j],
                    recv_sem=recv_sems.at[3 * b + j], device_id=(px, py, c), device_id_type=MESH).start()
        token[...] = jnp.zeros_like(token)

    lands = [pltpu.with_memory_space_constraint(lax.empty(b.shape, b.dtype), pltpu.HBM) for b in bufs]
    outs = pl.pallas_call(
        body, name=name,
        out_shape=(pltpu.SemaphoreType.DMA((3 * nb,)), pltpu.SemaphoreType.DMA((3 * nb,)),
                   *[pltpu.HBM(b.shape, b.dtype) for b in bufs], *[pltpu.HBM(b.shape, b.dtype) for b in bufs],
                   jax.ShapeDtypeStruct((SUBLANE, LANE), F32)),
        in_specs=[hbm] * (2 * nb), out_specs=(sem, sem, *[hbm] * (2 * nb), pl.BlockSpec(memory_space=pltpu.VMEM)),
        input_output_aliases={i: 2 + i for i in range(2 * nb)},
        compiler_params=pltpu.CompilerParams(has_side_effects=pltpu.SideEffectType.DATAFLOW_SIDE_EFFECTING),
    )(*[pltpu.with_memory_space_constraint(b, pltpu.HBM) for b in bufs], *lands)
    return outs[0], outs[1], list(outs[2:2 + nb]), list(outs[2 + nb:2 + 2 * nb]), outs[-1]


def _chip_exchange_wait(send_sems, recv_sems, thru, lands, after, name):
    nb = len(thru)
    hbm, sem = pl.BlockSpec(memory_space=pltpu.HBM), pl.BlockSpec(memory_space=pltpu.SEMAPHORE)

    def body(*refs):
        x_refs, land_refs = refs[:nb], refs[nb:2 * nb]
        send_sems_, recv_sems_ = refs[2 * nb], refs[2 * nb + 1]
        x, y, c = _coords()
        for j, (px, py) in enumerate([(1 - x, y), (x, 1 - y), (1 - x, 1 - y)]):
            for b in range(nb):
                cp = pltpu.make_async_remote_copy(
                    src_ref=x_refs[b].at[2 * px + py], dst_ref=land_refs[b].at[2 * px + py],
                    send_sem=send_sems_.at[3 * b + j], recv_sem=recv_sems_.at[3 * b + j], device_id=(px, py, c),
                    device_id_type=MESH)
                cp.wait_send()
                cp.wait_recv()

    outs = pl.pallas_call(
        body, name=name,
        out_shape=(*[pltpu.HBM(b.shape, b.dtype) for b in thru], *[pltpu.HBM(b.shape, b.dtype) for b in lands]),
        in_specs=[hbm] * (2 * nb) + [sem, sem, pl.BlockSpec(memory_space=pl.ANY)], out_specs=[hbm] * (2 * nb),
        input_output_aliases={i: i for i in range(2 * nb)},
        compiler_params=pltpu.CompilerParams(has_side_effects=pltpu.SideEffectType.DATAFLOW_SIDE_EFFECTING),
    )(*thru, *lands, send_sems, recv_sems, after)
    return list(outs[:nb]), list(outs[nb:])


def _chip_sum(g, r1, c, name):
    _, r, cw = g.shape
    tr = _pick(r, (256, 176, 128))

    def body(c_ref, g_ref, r_ref, o_ref):
        o_ref[...] = (g_ref[...] + r_ref[...]).astype(BF16)

    grid_spec = pltpu.PrefetchScalarGridSpec(
        num_scalar_prefetch=1, grid=(4, r // tr),
        in_specs=[_BS((None, None, tr, cw), lambda q, i, c_ref: (q, c_ref[0], i, 0)),
                  _BS((None, tr, cw), lambda q, i, c_ref: (q, i, 0))],
        out_specs=_BS((None, tr, cw), lambda q, i, c_ref: (q, i, 0)))
    return pl.pallas_call(
        body, name=name, grid_spec=grid_spec, out_shape=jax.ShapeDtypeStruct((4, r, cw), BF16),
        compiler_params=_params(("parallel", "parallel")),
    )(c.reshape(1).astype(jnp.int32), g.reshape(4, 2, r, cw), r1)


def _adamw_math(w, g, m, v):
    m = B1 * m + (1.0 - B1) * g
    v = B2 * v + (1.0 - B2) * (g * g)
    m_hat = m / (1.0 - B1 ** STEP)
    v_hat = v / (1.0 - B2 ** STEP)
    return -LR * (m_hat / (jnp.sqrt(v_hat) + AEPS) + WD * w), m, v


def _sum_adamw(parts, w, m, v, name):
    r, c = w.shape
    tr = _pick(r, (256, 176, 128))
    tc = _pick(c, (256, 128)) if tr == r and r > 256 else c

    def body(p0, p1, p2, p3, w_ref, m_ref, v_ref, g_out, d_out, m_out, v_out):
        g = ((p0[...].astype(F32) + p1[...].astype(F32)) + p2[...].astype(F32)) + p3[...].astype(F32)
        g_out[...] = g
        d_out[...], m_out[...], v_out[...] = _adamw_math(w_ref[...], g, m_ref[...], v_ref[...])

    part = lambda q: _BS((None, tr, tc), functools.partial(lambda i, j, q: (q, i, j), q=q))
    plain = _BS((tr, tc), lambda i, j: (i, j))
    return pl.pallas_call(
        body, name=name, grid=(r // tr, c // tc), in_specs=[part(q) for q in range(4)] + [plain] * 3,
        out_specs=[plain] * 4, out_shape=[jax.ShapeDtypeStruct((r, c), F32)] * 4,
        compiler_params=_params(("parallel", "parallel")),
    )(parts, parts, parts, parts, w, m, v)


def _small_update(gathered, w, m, v):
    r = w.shape[0]

    def body(ga_ref, w_ref, m_ref, v_ref, g_out, d_out, m_out, v_out):
        g = ga_ref[0]
        for dev in range(1, N_DEV):
            g = g + ga_ref[dev]
        g_out[...] = g
        d_out[...], m_out[...], v_out[...] = _adamw_math(w_ref[...], g, m_ref[...], v_ref[...])

    return pl.pallas_call(
        body, name="small_update", out_shape=[jax.ShapeDtypeStruct((r, LANE), F32)] * 4,
    )(gathered, w, m, v)


def _pack_small(vals):
    rows = []
    for name, (r, n) in SMALL:
        flat = vals[name].reshape(-1)
        pad = (-flat.shape[0]) % (SUBLANE * LANE)
        rows.append(jnp.pad(flat, (0, pad)).reshape(-1, LANE))
    return jnp.concatenate(rows, axis=0)


def _unpack_small(packed):
    out, off = {}, 0
    for name, (r, n) in SMALL:
        nrow = -(-(r * n) // (SUBLANE * LANE)) * SUBLANE
        out[name] = packed[off:off + nrow].reshape(-1)[:r * n].reshape(r, n)
        off += nrow
    return out


def kernel(x, positions, ffn1_norm, ffn1_w_in, ffn1_w_out, mix_norm, w_in, hg_lb_table, hg_out_norm, w_hg_branch, mla_q_lora_norm, w_q_up, mla_kv_lora_norm, w_kv_up, q_head_norm, k_head_norm, w_mla_branch, w_merge, b_merge, w_out, ffn2_norm, ffn2_w_in, ffn2_w_out, final_norm, loss_target, m_ffn1_norm, m_ffn1_w_in, m_ffn1_w_out, m_mix_norm, m_w_in, m_hg_lb_table, m_hg_out_norm, m_w_hg_branch, m_mla_q_lora_norm, m_w_q_up, m_mla_kv_lora_norm, m_w_kv_up, m_q_head_norm, m_k_head_norm, m_w_mla_branch, m_w_merge, m_b_merge, m_w_out, m_ffn2_norm, m_ffn2_w_in, m_ffn2_w_out, m_final_norm, v_ffn1_norm, v_ffn1_w_in, v_ffn1_w_out, v_mix_norm, v_w_in, v_hg_lb_table, v_hg_out_norm, v_w_hg_branch, v_mla_q_lora_norm, v_w_q_up, v_mla_kv_lora_norm, v_w_kv_up, v_q_head_norm, v_k_head_norm, v_w_mla_branch, v_w_merge, v_b_merge, v_w_out, v_ffn2_norm, v_ffn2_w_in, v_ffn2_w_out, v_final_norm):
    args = dict(locals())
    t = x.shape[1]
    big_w = {n: args[n][0] for n, _, _ in BIG}
    small = {n: args[n].reshape(shape) for n, shape in SMALL}

    names = [n for n, _, _ in BIG]
    first, rest = names[:2], names[2:]
    full = dict(zip(first, _all_gather([big_w[n].astype(BF16) for n in first], "weights_all_gather_ffn1")))
    g_send, g_recv, g_thru, g_bufs, g_token = _gather_start([big_w[n].astype(BF16) for n in rest], full[first[0]],
                                                            "weights_gather_start")
    gains = dict(small, ffn1_norm=small["ffn1_norm"] + g_token[0:1, 0:1])

    def late_weights(after):
        blocks, bufs = _gather_wait(g_send, g_recv, g_thru, g_bufs, after, "weights_gather_wait")
        return dict(zip(rest, _gather_finish(blocks, bufs, "weights_gather_finish")))

    inv_freq = ROPE_THETA ** (-jnp.arange(0, ROPE, 2, dtype=F32) / ROPE)
    ang = positions[0].astype(F32)[:, None] * inv_freq
    cs = jnp.concatenate([jnp.cos(ang), jnp.cos(ang)], axis=1)
    sn = jnp.concatenate([jnp.sin(ang), jnp.sin(ang)], axis=1)

    c = lax.axis_index("c")
    chip = 2 * lax.axis_index("x") + lax.axis_index("y")
    early = {}

    def chip_sums_of(g, ns, tag):
        from_sibling = _sibling_swap([g[n] for n in ns], "grads_sibling_swap_" + tag)
        return [_chip_sum(g[n], r1, c, "chip_sum_" + n) for n, r1 in zip(ns, from_sibling)]

    def early_grads(g):
        early["names"] = [n for n in names if n in g]
        early["sums"] = chip_sums_of(g, early["names"], "early")
        early["send"], early["recv"], early["thru"], early["lands"], token = _chip_exchange_start(
            early["sums"], "grads_exchange_start")
        return token

    def last_grads(g):
        early["late"] = [n for n in names if n not in early["names"]]
        early["l_send"], early["l_recv"], early["l_thru"], early["l_lands"], token = _chip_exchange_start(
            chip_sums_of(g, early["late"], "late"), "grads_exchange_late_start")
        return token

    loss_row, grad_x, g = _local_step(x[0], loss_target[0], cs, sn, full, late_weights, gains, early_grads,
                                      last_grads)
    late, l_send, l_recv = early["late"], early["l_send"], early["l_recv"]
    l_thru, l_lands = early["l_thru"], early["l_lands"]
    sent, landed = _chip_exchange_wait(early["send"], early["recv"], early["thru"], early["lands"], grad_x,
                                       "grads_exchange_wait")
    exchanged = {n: _put_own(land, lax.dynamic_index_in_dim(own, chip, 0), chip)
                 for n, land, own in zip(early["names"], landed, sent)}

    small_packed = jnp.concatenate([_pack_small(g), jnp.pad(loss_row, ((0, SUBLANE - 1), (0, 0)))], axis=0)
    small_all = _all_gather([small_packed], "small_all_gather")[0]
    zero_tail = jnp.zeros((SUBLANE, LANE), F32)
    pk = lambda d: jnp.concatenate([_pack_small(d), zero_tail], axis=0)
    sg, sd, sm, sv = _small_update(
        small_all, pk(small), pk({n: args["m_" + n].reshape(shape) for n, shape in SMALL}),
        pk({n: args["v_" + n].reshape(shape) for n, shape in SMALL}))
    n_small_rows = _pack_small(small).shape[0]
    loss = sg[n_small_rows, 0]
    outs = {k_: _unpack_small(a) for k_, a in (("grad", sg), ("delta", sd), ("new_m", sm), ("new_v", sv))}

    def adamw(n):
        tr = (lambda a: a.T) if n in GRAD_T else (lambda a: a)
        res = _sum_adamw(exchanged[n], tr(big_w[n]), tr(args["m_" + n][0]), tr(args["v_" + n][0]), "adamw_" + n)
        outs["grad"][n], outs["delta"][n], outs["new_m"][n], outs["new_v"][n] = [tr(a) for a in res]

    for n in early["names"]:
        adamw(n)
    sent, landed = _chip_exchange_wait(l_send, l_recv, l_thru, l_lands, outs["delta"][early["names"][-1]],
                                       "grads_exchange_late_wait")
    for n, land, own in zip(late, landed, sent):
        exchanged[n] = _put_own(land, lax.dynamic_index_in_dim(own, chip, 0), chip)
        adamw(n)

    def shaped(kind, n):
        return outs[kind][n].reshape(args[n].shape)

    return (loss, grad_x[None], *[shaped("grad", n) for n in WEIGHT_ORDER], *[shaped("delta", n) for n in WEIGHT_ORDER],
            *[shaped("new_m", n) for n in WEIGHT_ORDER], *[shaped("new_v", n) for n in WEIGHT_ORDER])
```

```python
import functools

import jax
import jax.numpy as jnp
from jax import lax
from jax.experimental import pallas as pl
from jax.experimental.pallas import tpu as pltpu

F32 = jnp.float32
BF16 = jnp.bfloat16

D = 1024
FF = 2816
NH = 8
HD = 128
ROPE = 64
QK = HD + ROPE
QL = 384
KVL = 256
HGW = NH * HD
CHUNK = 64
EPS = 1e-6
ROPE_THETA = 10000.0
SCALE = QK ** -0.5
LOG2E = 1.4426950408889634

LR, B1, B2, AEPS, WD, STEP = 0.001, 0.9, 0.999, 1e-08, 0.01, 10

HB = 128
SUB = 16
EXP_CLAMP = 80.0
ATT_TILES = (512, 256, 128)
ATT_Q = (1024, 512, 256, 128)
ATT_KEY_TILES = 1
GRAD_DT = BF16
ROW_TILES = (1024, 512, 256, 128)

LANE = 128
SUBLANE = 8
VMEM_LIMIT = 56 << 20

N_DEV = 8
MESH = pl.DeviceIdType.MESH

BIG = (
    ("ffn1_w_in", (D, 2 * FF), 1), ("ffn1_w_out", (FF, D), 0), ("w_in", (D, 4800), 1),
    ("w_hg_branch", (HGW, D), 0), ("w_q_up", (QL, NH * QK), 1), ("w_kv_up", (KVL, NH * 2 * HD), 1),
    ("w_mla_branch", (NH * HD, D), 0), ("w_merge", (D, 2 * D), 1), ("w_out", (D, D), 0),
    ("ffn2_w_in", (D, 2 * FF), 1), ("ffn2_w_out", (FF, D), 0),
)
SMALL = (
    ("ffn1_norm", (1, D)), ("mix_norm", (1, D)), ("hg_lb_table", (2, HGW)), ("hg_out_norm", (1, HD)),
    ("mla_q_lora_norm", (1, QL)), ("mla_kv_lora_norm", (1, KVL)), ("q_head_norm", (1, QK)),
    ("k_head_norm", (1, QK)), ("b_merge", (1, 2 * D)), ("ffn2_norm", (1, D)), ("final_norm", (1, D)),
)
GRAD_T = ("ffn1_w_in", "ffn2_w_in", "w_in", "w_q_up")
WEIGHT_ORDER = ("ffn1_norm", "ffn1_w_in", "ffn1_w_out", "mix_norm", "w_in", "hg_lb_table", "hg_out_norm",
                "w_hg_branch", "mla_q_lora_norm", "w_q_up", "mla_kv_lora_norm", "w_kv_up", "q_head_norm",
                "k_head_norm", "w_mla_branch", "w_merge", "b_merge", "w_out", "ffn2_norm", "ffn2_w_in",
                "ffn2_w_out", "final_norm")


def _pick(n, cands):
    for c in cands:
        if n % c == 0:
            return c
    return n


def _params(sem):
    return pltpu.CompilerParams(dimension_semantics=sem, vmem_limit_bytes=VMEM_LIMIT)


def _sig(x):
    return 0.5 * jnp.tanh(0.5 * x) + 0.5


def _dot(a, b):
    return jnp.dot(a.astype(BF16), b.astype(BF16), preferred_element_type=F32)


def _dot_nt(a, b):
    return lax.dot_general(a.astype(BF16), b.astype(BF16), (((1,), (1,)), ((), ())),
                           preferred_element_type=F32)


def _dot_tn(a, b):
    return lax.dot_general(a.astype(BF16), b.astype(BF16), (((0,), (0,)), ((), ())),
                           preferred_element_type=F32)


def _split3(x):
    x1 = x.astype(BF16)
    r1 = x - x1.astype(F32)
    x2 = r1.astype(BF16)
    x3 = (r1 - x2.astype(F32)).astype(BF16)
    return x1, x2, x3


def _dot_sel(m, x):
    x1, x2, x3 = _split3(x)
    d = lambda p: jnp.dot(m, p, preferred_element_type=F32)
    return d(x1) + d(x2) + d(x3)


def _sel_dot(x, m):
    x1, x2, x3 = _split3(x)
    d = lambda p: jnp.dot(p, m, preferred_element_type=F32)
    return d(x1) + d(x2) + d(x3)


_TN = (1408, 1024, 768, 512, 384, 256, 128)


def _accumulate(acc, o_ref, axis, steps, term):
    step = pl.program_id(axis)

    @pl.when(step == 0)
    def _():
        acc[...] = jnp.zeros_like(acc)

    val = term()
    if isinstance(val, (list, tuple)):
        for e, v in enumerate(val):
            acc[e] += v
    else:
        acc[...] += val

    @pl.when(step == steps - 1)
    def _():
        o_ref[...] = acc[...].astype(o_ref.dtype)


def _mm(a, b, mode, name, out_dtype=F32):
    if mode == "tn":
        t, m = a.shape
        n = b.shape[1]
        tt, tm, tn = _pick(t, (512, 256, 128)), _pick(m, _TN), _pick(n, _TN)

        def body(a_ref, b_ref, o_ref, acc):
            _accumulate(acc, o_ref, 2, t // tt, lambda: _dot_tn(a_ref[...], b_ref[...]))

        return pl.pallas_call(
            body, name=name, grid=(m // tm, n // tn, t // tt),
            in_specs=[pl.BlockSpec((tt, tm), lambda i, j, k: (k, i)),
                      pl.BlockSpec((tt, tn), lambda i, j, k: (k, j))],
            out_specs=pl.BlockSpec((tm, tn), lambda i, j, k: (i, j)),
            out_shape=jax.ShapeDtypeStruct((m, n), GRAD_DT), scratch_shapes=[pltpu.VMEM((tm, tn), F32)],
            compiler_params=_params(("parallel", "parallel", "arbitrary")),
        )(a, b)

    m, k = a.shape
    tm = _pick(m, ROW_TILES)
    if mode == "nn":
        n = b.shape[1]
        tn = _pick(n, _TN)
        b_spec = pl.BlockSpec((k, tn), lambda i, j: (0, j))
        dot = _dot
    else:
        n = b.shape[0]
        tn = _pick(n, _TN if k <= 4096 else (512, 256, 128))
        b_spec = pl.BlockSpec((tn, k), lambda i, j: (j, 0))
        dot = _dot_nt

    def body(a_ref, b_ref, o_ref):
        o_ref[...] = dot(a_ref[...], b_ref[...]).astype(o_ref.dtype)

    return pl.pallas_call(
        body, name=name, grid=(m // tm, n // tn),
        in_specs=[pl.BlockSpec((tm, k), lambda i, j: (i, 0)), b_spec],
        out_specs=pl.BlockSpec((tm, tn), lambda i, j: (i, j)),
        out_shape=jax.ShapeDtypeStruct((m, n), out_dtype),
        compiler_params=_params(("parallel", "parallel")),
    )(a, b)


_DOTS = {"nn": _dot, "nt": _dot_nt, "tn": _dot_tn}
_BS = pl.BlockSpec


def _mmcall(name, kind, a, b, a_spec, b_spec, o_spec, o_shape, grid, red_axis=None, out_dtype=F32):
    dot = _DOTS[kind]

    def body(a_ref, b_ref, o_ref):
        if red_axis is None:
            o_ref[...] = dot(a_ref[...], b_ref[...]).astype(o_ref.dtype)
        else:
            @pl.when(pl.program_id(red_axis) == 0)
            def _():
                o_ref[...] = jnp.zeros_like(o_ref)

            o_ref[...] += dot(a_ref[...], b_ref[...])

    sem = tuple("arbitrary" if ax == red_axis else "parallel" for ax in range(len(grid)))
    return pl.pallas_call(
        body, name=name, grid=grid, in_specs=[a_spec, b_spec], out_specs=o_spec,
        out_shape=jax.ShapeDtypeStruct(o_shape, out_dtype), compiler_params=_params(sem),
    )(a, b)


class _AsRef:
    def __init__(self, value):
        self.value = value

    def __getitem__(self, idx):
        return self.value


def _mm_stack_red(a, w, name, kind, epilogue=None):
    s, t, n = a.shape
    nout = w.shape[2] if kind == "nn" else w.shape[1]
    if epilogue is None:
        tm = _pick(t, ROW_TILES)
        return _mmcall(name, kind, a, w, _BS((None, tm, n), lambda i, j: (j, i, 0)),
                       _BS((None,) + w.shape[1:], lambda i, j: (j, 0, 0)), _BS((tm, nout), lambda i, j: (i, 0)),
                       (t, nout), (t // tm, s), red_axis=1)
    fn, first, rows, vecs, outs, accs = epilogue
    rows = [first] + list(rows)
    tm = _pick(t, ROW_TILES)
    n_row, n_vec, n_out = len(rows), len(vecs), len(outs)
    dot = _DOTS[kind]

    def body(*refs):
        a_ref, w_ref, prod = refs[0], refs[1], refs[-1]
        row_refs = refs[2:2 + n_row]
        vec_refs = refs[2 + n_row:2 + n_row + n_vec]
        out_refs = refs[2 + n_row + n_vec:2 + n_row + n_vec + n_out]
        acc_refs = refs[2 + n_row + n_vec + n_out:-1]
        i, j = pl.program_id(0), pl.program_id(1)

        @pl.when(j == 0)
        def _():
            prod[...] = jnp.zeros_like(prod)

        prod[...] += dot(a_ref[...], w_ref[...])

        @pl.when(j == s - 1)
        def _():
            res = fn(row_refs[0], _AsRef(prod[...]), *row_refs[1:], *vec_refs, *out_refs)
            if acc_refs:
                @pl.when(i == 0)
                def _():
                    for r in acc_refs:
                        r[...] = jnp.zeros_like(r)

                for r, val in zip(acc_refs, res):
                    r[...] += val

    in_specs = [_BS((None, tm, n), lambda i, j: (j, i, 0)), _BS((None,) + w.shape[1:], lambda i, j: (j, 0, 0))]
    in_specs += [_BS((tm, r.shape[1]), lambda i, j: (i, 0)) for r in rows]
    in_specs += [_BS(v.shape, lambda i, j: (0, 0)) for v in vecs]
    out_specs = [_BS((tm, wd), lambda i, j: (i, 0)) for wd, _ in outs] + [_BS(sh, lambda i, j: (0, 0)) for sh in accs]
    out_shape = [jax.ShapeDtypeStruct((t, wd), dt) for wd, dt in outs] + [jax.ShapeDtypeStruct(sh, F32) for sh in accs]
    return pl.pallas_call(
        body, name=name, grid=(t // tm, s), in_specs=in_specs, out_specs=out_specs, out_shape=out_shape,
        scratch_shapes=[pltpu.VMEM((tm, nout), F32)],
        compiler_params=_params(("arbitrary" if accs else "parallel", "arbitrary")),
    )(a, w, *rows, *vecs)


def _mm_stack_tn(a, b, name):
    grp = 4
    if a.ndim == 2:
        t, k = a.shape
        s, _, n = b.shape
        tt = _pick(t, (512, 256, 128))
        a_spec = _BS((tt, k), lambda j, r: (r, 0))
        b_spec, b_in = _BS((None, grp, tt, n), lambda j, r: (j, 0, r, 0)), b.reshape(s // grp, grp, t, n)
        a_in = a
    else:
        s, t, k = a.shape
        n = b.shape[1]
        tt = _pick(t, (512, 256, 128))
        a_spec, a_in = _BS((None, grp, tt, k), lambda j, r: (j, 0, r, 0)), a.reshape(s // grp, grp, t, k)
        b_spec, b_in = _BS((tt, n), lambda j, r: (r, 0)), b

    def body(a_ref, b_ref, o_ref, acc):
        def terms():
            shared = a_ref[...] if a.ndim == 2 else b_ref[...]
            return [_dot_tn(shared, b_ref[e]) if a.ndim == 2 else _dot_tn(a_ref[e], shared) for e in range(grp)]

        _accumulate(acc, o_ref, 1, t // tt, terms)

    return pl.pallas_call(
        body, name=name, grid=(s // grp, t // tt), in_specs=[a_spec, b_spec],
        out_specs=_BS((None, grp, k, n), lambda j, r: (j, 0, 0, 0)),
        out_shape=jax.ShapeDtypeStruct((s // grp, grp, k, n), GRAD_DT), scratch_shapes=[pltpu.VMEM((grp, k, n), F32)],
        compiler_params=_params(("parallel", "arbitrary")),
    )(a_in, b_in).reshape(s, k, n)


def _cols_fwd(x, w, name):
    t, k = x.shape
    s, _, n = w.shape
    tm = _pick(t, ROW_TILES)

    def body(x_ref, w_ref, o_ref):
        x_ = x_ref[...]
        for j in range(s):
            o_ref[:, n * j:n * (j + 1)] = _dot(x_, w_ref[j])

    return pl.pallas_call(
        body, name=name, grid=(t // tm,),
        in_specs=[_BS((tm, k), lambda i: (i, 0)), _BS((s, k, n), lambda i: (0, 0, 0))],
        out_specs=_BS((tm, s * n), lambda i: (i, 0)), out_shape=jax.ShapeDtypeStruct((t, s * n), F32),
        compiler_params=_params(("parallel",)),
    )(x, w)


def _cols_dx(d, w, name):
    t = d.shape[0]
    s, k, n = w.shape
    tm = _pick(t, ROW_TILES)

    def body(d_ref, w_ref, o_ref):
        acc = _dot_nt(d_ref[:, 0:n], w_ref[0])
        for j in range(1, s):
            acc = acc + _dot_nt(d_ref[:, n * j:n * (j + 1)], w_ref[j])
        o_ref[...] = acc

    return pl.pallas_call(
        body, name=name, grid=(t // tm,),
        in_specs=[_BS((tm, s * n), lambda i: (i, 0)), _BS((s, k, n), lambda i: (0, 0, 0))],
        out_specs=_BS((tm, k), lambda i: (i, 0)), out_shape=jax.ShapeDtypeStruct((t, k), F32),
        compiler_params=_params(("parallel",)),
    )(d, w)


def _cols_dw(x, d, n, name):
    t, k = x.shape
    s = d.shape[1] // n
    tt = _pick(t, (512, 256, 128))

    def body(x_ref, d_ref, o_ref, acc):
        def terms():
            x_ = x_ref[...]
            return [_dot_tn(x_, d_ref[:, n * j:n * (j + 1)]) for j in range(s)]

        _accumulate(acc, o_ref, 0, t // tt, terms)

    return pl.pallas_call(
        body, name=name, grid=(t // tt,),
        in_specs=[_BS((tt, k), lambda r: (r, 0)), _BS((tt, s * n), lambda r: (r, 0))],
        out_specs=_BS((s, k, n), lambda r: (0, 0, 0)), out_shape=jax.ShapeDtypeStruct((s, k, n), GRAD_DT),
        scratch_shapes=[pltpu.VMEM((s, k, n), F32)], compiler_params=_params(("arbitrary",)),
    )(x, d)


def _rows(fn, name, t, tm, ins, vecs, outs, accs=()):
    n_in, n_out, n_acc = len(ins) + len(vecs), len(outs), len(accs)

    def body(*refs):
        res = fn(*refs[:n_in + n_out])
        if n_acc:
            acc_refs = refs[n_in + n_out:]

            @pl.when(pl.program_id(0) == 0)
            def _():
                for r in acc_refs:
                    r[...] = jnp.zeros_like(r)

            for r, val in zip(acc_refs, res):
                r[...] += val

    in_specs = [pl.BlockSpec((tm, bw), functools.partial(lambda i, cb: (i, cb), cb=cb)) for _, bw, cb in ins]
    in_specs += [pl.BlockSpec(v.shape, lambda i: (0, 0)) for v in vecs]
    out_specs = [pl.BlockSpec((tm, w), lambda i: (i, 0)) for w, _ in outs]
    out_specs += [pl.BlockSpec(s, lambda i: (0, 0)) for s in accs]
    out_shape = [jax.ShapeDtypeStruct((t, w), dt) for w, dt in outs]
    out_shape += [jax.ShapeDtypeStruct(s, F32) for s in accs]
    return pl.pallas_call(
        body, name=name, grid=(t // tm,), in_specs=in_specs, out_specs=out_specs, out_shape=out_shape,
        compiler_params=_params(("arbitrary",) if n_acc else ("parallel",)),
    )(*[a for a, _, _ in ins], *vecs)


def _rms(x, g):
    return x * lax.rsqrt(jnp.mean(x * x, axis=-1, keepdims=True) + EPS) * g


def _rms_bwd(x, g, dy):
    xh = x * lax.rsqrt(jnp.mean(x * x, axis=-1, keepdims=True) + EPS)
    r = lax.rsqrt(jnp.mean(x * x, axis=-1, keepdims=True) + EPS)
    dyg = dy * g
    dx = r * (dyg - xh * jnp.mean(dyg * xh, axis=-1, keepdims=True))
    return dx, jnp.sum(dy * xh, axis=0, keepdims=True)


def _hgrn_mats():
    row = lax.broadcasted_iota(jnp.int32, (HB, HB), 0)
    col = lax.broadcasted_iota(jnp.int32, (HB, HB), 1)
    return row, col


def _hgrn_gates(qr, z, t0, t1):
    lb = 1.0 / (1.0 + jnp.exp(t1 - t0))
    th = jnp.tanh(0.5 * z)
    sz, sneg = 0.5 + 0.5 * th, 0.5 - 0.5 * th
    f = lb + (1.0 - lb) * sz
    return lb, sz, sneg, f, jnp.log(f), (1.0 - lb) * sneg, qr * _sig(qr)


def _hgrn_scores(q, k, cum):
    heads, nsub = range(len(q)), HB // SUB
    mids = [[cum[h][SUB * i + SUB // 2 - 1:SUB * i + SUB // 2, :] for i in range(nsub)] for h in heads]
    cmid = [jnp.concatenate([cum[h][SUB * i:SUB * (i + 1)] - mids[h][i] for i in range(nsub)], axis=0) for h in heads]
    qd = [q[h] * jnp.exp(jnp.minimum(cmid[h], EXP_CLAMP)) for h in heads]
    qd_b = [x.astype(BF16) for x in qd]
    kds = [[None] * nsub for _ in heads]
    parts = [[None] * nsub for _ in heads]
    for i in range(nsub):
        for h in heads:
            kds[h][i] = k[h] * jnp.exp(jnp.minimum(mids[h][i] - cum[h], EXP_CLAMP))
        for h in heads:
            parts[h][i] = _dot_nt(qd_b[h][SUB * i:SUB * (i + 1)], kds[h][i])
    return qd, kds, [jnp.concatenate(parts[h], axis=0) for h in heads], cmid


def _hgrn_fwd(p_hg, table, gain, t):
    nblk = t // HB

    def body(q_ref, f_ref, i_ref, g_ref, tab_ref, gain_ref, o_ref, y_ref, st_ref, state):
        @pl.when(pl.program_id(0) == 0)
        def _():
            state[...] = jnp.zeros_like(state)

        row, col = _hgrn_mats()
        causal = col <= row
        tri = causal.astype(BF16)
        heads = range(NH)
        sl = [slice(HD * h, HD * (h + 1)) for h in heads]
        gates = [_hgrn_gates(q_ref[:, sl[h]], f_ref[:, sl[h]], tab_ref[0:1, sl[h]], tab_ref[1:2, sl[h]]) for h in heads]
        lf, k, q = [g[4] for g in gates], [g[5] for g in gates], [g[6] for g in gates]
        v = [i_ref[:, sl[h]] for h in heads]
        cum = [_dot_sel(tri, lf[h]) for h in heads]
        _, _, s, _ = _hgrn_scores(q, k, cum)
        p = [jnp.where(causal, s[h], 0.0) for h in heads]
        st = [state[h] for h in heads]
        o = [_dot(p[h], v[h]) + _dot_nt(q[h] * jnp.exp(cum[h]), st[h]) for h in heads]
        last = [cum[h][HB - 1:HB, :] for h in heads]
        new_st = [st[h] * jnp.exp(last[h]) + _dot_tn(v[h], k[h] * jnp.exp(last[h] - cum[h])) for h in heads]
        for h in heads:
            gr = g_ref[:, sl[h]]
            st_ref[h, 0] = st[h]
            state[h] = new_st[h]
            o_ref[:, sl[h]] = o[h]
            y_ref[:, sl[h]] = (_rms(o[h], gain_ref[...]) * gr * _sig(gr)).astype(BF16)

    blk = lambda cb: pl.BlockSpec((HB, HGW), functools.partial(lambda n, cb: (n, cb), cb=cb))
    return pl.pallas_call(
        body, name="hgrn_fwd", grid=(nblk,),
        in_specs=[blk(0), blk(1), blk(2), blk(3), pl.BlockSpec((2, HGW), lambda n: (0, 0)),
                  pl.BlockSpec((1, HD), lambda n: (0, 0))],
        out_specs=[pl.BlockSpec((HB, HGW), lambda n: (n, 0)), pl.BlockSpec((HB, HGW), lambda n: (n, 0)),
                   pl.BlockSpec((NH, 1, HD, HD), lambda n: (0, n, 0, 0))],
        out_shape=[jax.ShapeDtypeStruct((t, HGW), F32), jax.ShapeDtypeStruct((t, HGW), BF16),
                   jax.ShapeDtypeStruct((NH, nblk, HD, HD), F32)],
        scratch_shapes=[pltpu.VMEM((NH, HD, HD), F32)],
        compiler_params=_params(("arbitrary",)),
    )(p_hg, p_hg, p_hg, p_hg, table, gain)


def _hgrn_bwd(p_hg, table, gain, o_pre, states, dy, t):
    nblk = t // HB

    def body(q_ref, f_ref, i_ref, g_ref, tab_ref, gain_ref, o_ref, st_ref, dy_ref, dp_ref, dtab_ref, dgain_ref,
             dstate):
        @pl.when(pl.program_id(0) == 0)
        def _():
            dstate[...] = jnp.zeros_like(dstate)
            dtab_ref[...] = jnp.zeros_like(dtab_ref)
            dgain_ref[...] = jnp.zeros_like(dgain_ref)

        row, col = _hgrn_mats()
        causal = col <= row
        tri = causal.astype(BF16)
        tri_t = (row <= col).astype(BF16)
        heads, nsub = range(NH), HB // SUB
        sl = [slice(HD * h, HD * (h + 1)) for h in heads]
        rows_of = lambda i: slice(SUB * i, SUB * (i + 1))
        gain_ = gain_ref[...]
        qr, z = [q_ref[:, sl[h]] for h in heads], [f_ref[:, sl[h]] for h in heads]
        v, gr = [i_ref[:, sl[h]] for h in heads], [g_ref[:, sl[h]] for h in heads]
        gates = [_hgrn_gates(qr[h], z[h], tab_ref[0:1, sl[h]], tab_ref[1:2, sl[h]]) for h in heads]
        lb, sz, sneg, f, lf, k, q = ([g[j] for g in gates] for j in range(7))
        cum = [_dot_sel(tri, lf[h]) for h in heads]
        qd, kds, s, cmid = _hgrn_scores(q, k, cum)
        p = [jnp.where(causal, s[h], 0.0) for h in heads]
        st, dst = [st_ref[h, 0] for h in heads], [dstate[h] for h in heads]
        o, dyh, sg = [o_ref[:, sl[h]] for h in heads], [dy_ref[:, sl[h]] for h in heads], [_sig(x) for x in gr]
        dgr = [dyh[h] * _rms(o[h], gain_) * sg[h] * (1.0 + gr[h] * (1.0 - sg[h])) for h in heads]
        norm_bwd = [_rms_bwd(o[h], gain_, dyh[h] * gr[h] * sg[h]) for h in heads]
        do = [x[0] for x in norm_bwd]
        do_b = [x.astype(BF16) for x in do]
        ecum, last = [jnp.exp(x) for x in cum], [x[HB - 1:HB, :] for x in cum]
        qc = [q[h] * ecum[h] for h in heads]
        edec = [jnp.exp(last[h] - cum[h]) for h in heads]
        kdec = [k[h] * edec[h] for h in heads]
        dp = [jnp.where(causal, _dot_nt(do_b[h], v[h]), 0.0) for h in heads]
        dv = [_dot(p[h].T, do_b[h]) + _dot_nt(kdec[h], dst[h]) for h in heads]
        dqc = [_dot(do_b[h], st[h]) for h in heads]
        dkdec = [_dot(v[h], dst[h]) for h in heads]
        new_dst = [dst[h] * jnp.exp(last[h]) + _dot(do[h].T, qc[h]) for h in heads]
        dp_b = [x.astype(BF16) for x in dp]
        dqd = [jnp.concatenate(parts, axis=0) for parts in
               zip(*[[_dot(dp_b[h][rows_of(i)], kds[h][i]) for h in heads] for i in range(nsub)])]
        gq = [dqd[h] * qd[h] for h in heads]
        dq = [dqd[h] * jnp.exp(jnp.minimum(cmid[h], EXP_CLAMP)) + dqc[h] * ecum[h] for h in heads]
        gs = [dkdec[h] * kdec[h] for h in heads]
        dk = [dkdec[h] * edec[h] for h in heads]
        dcum = [dqc[h] * qc[h] - gs[h] + gq[h]
                + jnp.where(row == HB - 1, jnp.sum(gs[h], axis=0, keepdims=True)
                            + jnp.exp(last[h]) * jnp.sum(st[h] * dst[h], axis=0, keepdims=True), 0.0) for h in heads]
        qd_b = [x.astype(BF16) for x in qd]
        for i in range(nsub):
            dkd = [_dot_tn(dp_b[h][rows_of(i)], qd_b[h][rows_of(i)]) for h in heads]
            for h in heads:
                mid = cum[h][SUB * i + SUB // 2 - 1:SUB * i + SUB // 2, :]
                dk[h] = dk[h] + dkd[h] * jnp.exp(jnp.minimum(mid - cum[h], EXP_CLAMP))
                gk = dkd[h] * kds[h][i]
                to_mid = jnp.sum(gk, axis=0, keepdims=True) - jnp.sum(gq[h][rows_of(i)], axis=0, keepdims=True)
                dcum[h] = dcum[h] - gk + jnp.where(row == SUB * i + SUB // 2 - 1, to_mid, 0.0)
        dlf = [_dot_sel(tri_t, dcum[h]) for h in heads]
        dgain = jnp.zeros((1, HD), F32)
        for h in heads:
            df = dlf[h] / f[h] - dk[h]
            dlb = jnp.sum(df * sneg[h], axis=0, keepdims=True) * lb[h] * (1.0 - lb[h])
            dtab_ref[0:1, sl[h]] += dlb
            dtab_ref[1:2, sl[h]] -= dlb
            sq = _sig(qr[h])
            dstate[h] = new_dst[h]
            dp_ref[:, sl[h]] = (dq[h] * sq * (1.0 + qr[h] * (1.0 - sq))).astype(BF16)
            dp_ref[:, HGW + HD * h:HGW + HD * (h + 1)] = (df * (1.0 - lb[h]) * sz[h] * sneg[h]).astype(BF16)
            dp_ref[:, 2 * HGW + HD * h:2 * HGW + HD * (h + 1)] = dv[h].astype(BF16)
            dp_ref[:, 3 * HGW + HD * h:3 * HGW + HD * (h + 1)] = dgr[h].astype(BF16)
            dgain = dgain + norm_bwd[h][1]
        dgain_ref[...] += dgain

    rev = lambda cb: pl.BlockSpec((HB, HGW), functools.partial(lambda n, cb: (nblk - 1 - n, cb), cb=cb))
    return pl.pallas_call(
        body, name="hgrn_bwd", grid=(nblk,),
        in_specs=[rev(0), rev(1), rev(2), rev(3), pl.BlockSpec((2, HGW), lambda n: (0, 0)),
                  pl.BlockSpec((1, HD), lambda n: (0, 0)), rev(0),
                  pl.BlockSpec((NH, 1, HD, HD), lambda n: (0, nblk - 1 - n, 0, 0)), rev(0)],
        out_specs=[pl.BlockSpec((HB, 4 * HGW), lambda n: (nblk - 1 - n, 0)),
                   pl.BlockSpec((2, HGW), lambda n: (0, 0)), pl.BlockSpec((1, HD), lambda n: (0, 0))],
        out_shape=[jax.ShapeDtypeStruct((t, 4 * HGW), BF16), jax.ShapeDtypeStruct((2, HGW), F32),
                   jax.ShapeDtypeStruct((1, HD), F32)],
        scratch_shapes=[pltpu.VMEM((NH, HD, HD), F32)],
        compiler_params=_params(("arbitrary",)),
    )(p_hg, p_hg, p_hg, p_hg, table, gain, o_pre, states, dy)


def _rope_mat():
    r = lax.broadcasted_iota(jnp.int32, (ROPE, ROPE), 0)
    c = lax.broadcasted_iota(jnp.int32, (ROPE, ROPE), 1)
    half = ROPE // 2
    return ((r == c - half).astype(F32) - (r == c + half).astype(F32)).astype(BF16)


def _mla_prep_fwd(p_mla, cs, sn, wq, wkv, gql, gkvl, gq, gk, t):
    tm = _pick(t, (512, 256, 128))

    def body(p_ref, cs_ref, sn_ref, wq_ref, wkv_ref, gql_ref, gkvl_ref, gq_ref, gk_ref,
             q_ref, k_ref, v_ref):
        rmat = _rope_mat()
        cqn = _rms(p_ref[:, 0:QL], gql_ref[...]).astype(BF16)
        ckvn = _rms(p_ref[:, QL:QL + KVL], gkvl_ref[...]).astype(BF16)
        kpe = p_ref[:, QL + KVL:QL + KVL + ROPE]
        c, s = cs_ref[...], sn_ref[...]
        rot = lambda x: x * c + _sel_dot(x, rmat) * s
        heads = range(NH)
        ssq = lambda x: jnp.sum(x * x, -1, keepdims=True)
        qa = [_dot(cqn, wq_ref[h, :, 0:HD]) for h in heads]
        qr = [_dot(cqn, wq_ref[h, :, HD:QK]) for h in heads]
        kn = [_dot(ckvn, wkv_ref[h, :, 0:HD]) for h in heads]
        vv = [_dot(ckvn, wkv_ref[h, :, HD:2 * HD]) for h in heads]
        kpe_ss = ssq(kpe)
        rq = [lax.rsqrt((ssq(qa[h]) + ssq(qr[h])) / QK + EPS) for h in heads]
        rk = [lax.rsqrt((ssq(kn[h]) + kpe_ss) / QK + EPS) for h in heads]
        q_rope = [rot(qr[h] * rq[h] * gq_ref[:, HD:QK]) for h in heads]
        k_rope = [rot(kpe * rk[h] * gk_ref[:, HD:QK]) for h in heads]
        for h in heads:
            q_ref[h, :, 0:HD] = (qa[h] * rq[h] * gq_ref[:, 0:HD] * (SCALE * LOG2E)).astype(BF16)
            q_ref[h, :, HD:QK] = (q_rope[h] * (SCALE * LOG2E)).astype(BF16)
            k_ref[h, :, 0:HD] = (kn[h] * rk[h] * gk_ref[:, 0:HD]).astype(BF16)
            k_ref[h, :, HD:QK] = k_rope[h].astype(BF16)
            v_ref[h] = vv[h].astype(BF16)

    whole = lambda a: pl.BlockSpec(a.shape, functools.partial(lambda i, nd: (0,) * nd, nd=a.ndim))
    return pl.pallas_call(
        body, name="mla_prep_fwd", grid=(t // tm,),
        in_specs=[pl.BlockSpec((tm, QL + KVL + ROPE), lambda i: (i, 0)), pl.BlockSpec((tm, ROPE), lambda i: (i, 0)),
                  pl.BlockSpec((tm, ROPE), lambda i: (i, 0))] + [whole(a) for a in (wq, wkv, gql, gkvl, gq, gk)],
        out_specs=[pl.BlockSpec((NH, tm, QK), lambda i: (0, i, 0)), pl.BlockSpec((NH, tm, QK), lambda i: (0, i, 0)),
                   pl.BlockSpec((NH, tm, HD), lambda i: (0, i, 0))],
        out_shape=[jax.ShapeDtypeStruct((NH, t, QK), BF16), jax.ShapeDtypeStruct((NH, t, QK), BF16),
                   jax.ShapeDtypeStruct((NH, t, HD), BF16)],
        compiler_params=_params(("parallel",)),
    )(p_mla, cs, sn, wq, wkv, gql, gkvl, gq, gk)


def _mla_prep_bwd(p_mla, cs, sn, wq, wkv, gql, gkvl, gq, gk, dq, dk, dv, t):
    tm = _pick(t, (512, 256, 128))

    def body(p_ref, cs_ref, sn_ref, wq_ref, wkv_ref, gql_ref, gkvl_ref, gq_ref, gk_ref,
             dq_ref, dk_ref, dv_ref,
             dp_ref, dwq_ref, dwkv_ref, dgql_ref, dgkvl_ref, dgq_ref, dgk_ref):
        accs = (dwq_ref, dwkv_ref, dgql_ref, dgkvl_ref, dgq_ref, dgk_ref)

        @pl.when(pl.program_id(0) == 0)
        def _():
            for r in accs:
                r[...] = jnp.zeros_like(r)

        rmat = _rope_mat()
        rmat_t = -rmat
        cq, ckv = p_ref[:, 0:QL], p_ref[:, QL:QL + KVL]
        kpe = p_ref[:, QL + KVL:QL + KVL + ROPE]
        cqn_f, ckvn_f = _rms(cq, gql_ref[...]), _rms(ckv, gkvl_ref[...])
        cqn, ckvn = cqn_f.astype(BF16), ckvn_f.astype(BF16)
        ckvn_t = ckvn_f.T.astype(BF16)
        c, s = cs_ref[...], sn_ref[...]
        unrot = lambda dy: dy * c + _sel_dot(dy * s, rmat_t)
        heads = range(NH)
        rsum = lambda x: jnp.sum(x, -1, keepdims=True)
        csum = lambda x: jnp.sum(x, axis=0, keepdims=True)
        qa = [_dot(cqn, wq_ref[h, :, 0:HD]) for h in heads]
        qr = [_dot(cqn, wq_ref[h, :, HD:QK]) for h in heads]
        kn = [_dot(ckvn, wkv_ref[h, :, 0:HD]) for h in heads]
        dyr = [unrot(dq_ref[h, :, HD:QK] * SCALE) for h in heads]
        dkr = [unrot(dk_ref[h, :, HD:QK]) for h in heads]
        rq = [lax.rsqrt((rsum(qa[h] * qa[h]) + rsum(qr[h] * qr[h])) / QK + EPS) for h in heads]
        xa, xr = [qa[h] * rq[h] for h in heads], [qr[h] * rq[h] for h in heads]
        dya = [dq_ref[h, :, 0:HD] * SCALE for h in heads]
        ga, gr_ = [dya[h] * gq_ref[:, 0:HD] for h in heads], [dyr[h] * gq_ref[:, HD:QK] for h in heads]
        mean = [(rsum(ga[h] * xa[h]) + rsum(gr_[h] * xr[h])) / QK for h in heads]
        dqa = [(rq[h] * (ga[h] - xa[h] * mean[h])).astype(BF16) for h in heads]
        dqr = [(rq[h] * (gr_[h] - xr[h] * mean[h])).astype(BF16) for h in heads]
        kpe_ss = rsum(kpe * kpe)
        rk = [lax.rsqrt((rsum(kn[h] * kn[h]) + kpe_ss) / QK + EPS) for h in heads]
        ya, yr = [kn[h] * rk[h] for h in heads], [kpe * rk[h] for h in heads]
        dka = [dk_ref[h, :, 0:HD] for h in heads]
        ha, hr = [dka[h] * gk_ref[:, 0:HD] for h in heads], [dkr[h] * gk_ref[:, HD:QK] for h in heads]
        mean_k = [(rsum(ha[h] * ya[h]) + rsum(hr[h] * yr[h])) / QK for h in heads]
        dkn = [(rk[h] * (ha[h] - ya[h] * mean_k[h])).astype(BF16) for h in heads]
        dvh = [dv_ref[h].astype(BF16) for h in heads]
        dw = [(_dot_tn(dqa[h], cqn), _dot_tn(dqr[h], cqn), _dot(ckvn_t, dkn[h]), _dot(ckvn_t, dvh[h])) for h in heads]
        back_q = [_dot_nt(dqa[h], wq_ref[h, :, 0:HD]) + _dot_nt(dqr[h], wq_ref[h, :, HD:QK]) for h in heads]
        back_kv = [_dot_nt(dkn[h], wkv_ref[h, :, 0:HD]) + _dot_nt(dvh[h], wkv_ref[h, :, HD:2 * HD]) for h in heads]
        dcqn, dckvn = sum(back_q[1:], back_q[0]), sum(back_kv[1:], back_kv[0])
        dkpe = sum([rk[h] * (hr[h] - yr[h] * mean_k[h]) for h in heads][1:], rk[0] * (hr[0] - yr[0] * mean_k[0]))
        dgq_a = sum([csum(dya[h] * xa[h]) for h in heads][1:], csum(dya[0] * xa[0]))
        dgq_r = sum([csum(dyr[h] * xr[h]) for h in heads][1:], csum(dyr[0] * xr[0]))
        dgk_a = sum([csum(dka[h] * ya[h]) for h in heads][1:], csum(dka[0] * ya[0]))
        dgk_r = sum([csum(dkr[h] * yr[h]) for h in heads][1:], csum(dkr[0] * yr[0]))
        for h in heads:
            dwq_ref[h, 0:HD, :] += dw[h][0]
            dwq_ref[h, HD:QK, :] += dw[h][1]
            dwkv_ref[h, :, 0:HD] += dw[h][2]
            dwkv_ref[h, :, HD:2 * HD] += dw[h][3]
        dcq, dg1 = _rms_bwd(cq, gql_ref[...], dcqn)
        dckv, dg2 = _rms_bwd(ckv, gkvl_ref[...], dckvn)
        dp_ref[:, 0:QL] = dcq.astype(BF16)
        dp_ref[:, QL:QL + KVL] = dckv.astype(BF16)
        dp_ref[:, QL + KVL:QL + KVL + ROPE] = dkpe.astype(BF16)
        dgql_ref[...] += dg1
        dgkvl_ref[...] += dg2
        dgq_ref[:, 0:HD] += dgq_a
        dgq_ref[:, HD:QK] += dgq_r
        dgk_ref[:, 0:HD] += dgk_a
        dgk_ref[:, HD:QK] += dgk_r

    whole = lambda a: pl.BlockSpec(a.shape, functools.partial(lambda i, nd: (0,) * nd, nd=a.ndim))
    acc_shapes = [(NH, QK, QL), wkv.shape, gql.shape, gkvl.shape, gq.shape, gk.shape]
    return pl.pallas_call(
        body, name="mla_prep_bwd", grid=(t // tm,),
        in_specs=[pl.BlockSpec((tm, QL + KVL + ROPE), lambda i: (i, 0)), pl.BlockSpec((tm, ROPE), lambda i: (i, 0)),
                  pl.BlockSpec((tm, ROPE), lambda i: (i, 0))]
        + [whole(a) for a in (wq, wkv, gql, gkvl, gq, gk)]
        + [pl.BlockSpec((NH, tm, QK), lambda i: (0, i, 0)), pl.BlockSpec((NH, tm, QK), lambda i: (0, i, 0)),
           pl.BlockSpec((NH, tm, HD), lambda i: (0, i, 0))],
        out_specs=[pl.BlockSpec((tm, QL + KVL + ROPE), lambda i: (i, 0))]
        + [pl.BlockSpec(s, functools.partial(lambda i, nd: (0,) * nd, nd=len(s))) for s in acc_shapes],
        out_shape=[jax.ShapeDtypeStruct((t, QL + KVL + ROPE), BF16)]
        + [jax.ShapeDtypeStruct(s, F32) for s in acc_shapes],
        compiler_params=_params(("arbitrary",)),
    )(p_mla, cs, sn, wq, wkv, gql, gkvl, gq, gk, dq, dk, dv)


def _chunk_mask(nq, nk, key0, keys_on_rows):
    shape = (nk, nq) if keys_on_rows else (nq, nk)
    qi = lax.broadcasted_iota(jnp.int32, shape, 1 if keys_on_rows else 0) // CHUNK
    ki = lax.broadcasted_iota(jnp.int32, shape, 0 if keys_on_rows else 1) // CHUNK + key0 // CHUNK
    return ki <= qi


def _flash_fwd(q, k, v, t):
    tq = _pick(t, ATT_Q)
    tk = tq // ATT_KEY_TILES

    def body(q_ref, k_ref, v_ref, o_ref, lse_ref):
        i = pl.program_id(1)
        qt = q_ref[0]

        def step(j, carry, key0):
            m, l, acc = carry
            cols = pl.ds(pl.multiple_of(j * tk, tk), tk)
            s = _dot_nt(qt, k_ref[0, cols, :])
            if key0 is not None:
                s = jnp.where(_chunk_mask(tq, tk, key0, False), s, -jnp.inf)
            m_new = jnp.maximum(m, jnp.max(s, axis=-1, keepdims=True))
            p = jnp.exp2(s - m_new)
            alpha = jnp.exp2(m - m_new)
            return m_new, alpha * l + jnp.sum(p, axis=-1, keepdims=True), alpha * acc + _dot(p, v_ref[0, cols, :])

        init = (jnp.full((tq, 1), -jnp.inf, F32), jnp.zeros((tq, 1), F32), jnp.zeros((tq, HD), F32))
        carry = lax.fori_loop(0, ATT_KEY_TILES * i, lambda j, cr: step(j, cr, None), init)
        for h in range(ATT_KEY_TILES):
            carry = step(ATT_KEY_TILES * i + h, carry, h * tk)
        m, l, acc = carry
        o_ref[...] = (acc / l).astype(BF16)
        lse_ref[0] = m + jnp.log2(l)

    return pl.pallas_call(
        body, name="flash_fwd", grid=(NH, t // tq),
        in_specs=[pl.BlockSpec((1, tq, QK), lambda h, i: (h, i, 0)), pl.BlockSpec((1, t, QK), lambda h, i: (h, 0, 0)),
                  pl.BlockSpec((1, t, HD), lambda h, i: (h, 0, 0))],
        out_specs=[pl.BlockSpec((tq, HD), lambda h, i: (i, h)), pl.BlockSpec((1, tq, 1), lambda h, i: (h, i, 0))],
        out_shape=[jax.ShapeDtypeStruct((t, NH * HD), BF16), jax.ShapeDtypeStruct((NH, t, 1), F32)],
        compiler_params=_params(("parallel", "parallel")),
    )(q, k, v)


def _attn_out_bwd(dy, w, o, t):
    tm = _pick(t, ATT_TILES)

    def body(dy_ref, w_ref, o_ref, do_ref, delta_ref):
        do = _dot_nt(dy_ref[...], w_ref[...]).astype(BF16)
        do_ref[...] = do
        ones = jnp.ones((SUBLANE, HD), BF16)
        rowsum = lambda p: lax.dot_general(ones, p, (((1,), (1,)), ((), ())), preferred_element_type=F32)
        for h in range(NH):
            sl = slice(HD * h, HD * (h + 1))
            x1, x2, x3 = _split3(do[:, sl].astype(F32) * o_ref[:, sl].astype(F32))
            delta_ref[h] = (rowsum(x1) + rowsum(x2) + rowsum(x3))[0:1, :]

    return pl.pallas_call(
        body, name="attn_out_bwd", grid=(t // tm,),
        in_specs=[_BS((tm, dy.shape[1]), lambda i: (i, 0)), _BS(w.shape, lambda i: (0, 0)),
                  _BS((tm, NH * HD), lambda i: (i, 0))],
        out_specs=[_BS((tm, NH * HD), lambda i: (i, 0)), _BS((NH, 1, tm), lambda i: (0, 0, i))],
        out_shape=[jax.ShapeDtypeStruct((t, NH * HD), BF16), jax.ShapeDtypeStruct((NH, 1, t), F32)],
        compiler_params=_params(("parallel",)),
    )(dy, w, o)


def _flash_bwd(q, k, v, do, lse_row, delta_row, t):
    tq = _pick(t, ATT_Q)
    tk = tq // ATT_KEY_TILES

    def body(q_ref, k_ref, v_ref, do_ref, lse_ref, delta_ref, dq_ref, dk_ref, dv_ref):
        j = pl.program_id(1)

        @pl.when(j == 0)
        def _():
            dq_ref[...] = jnp.zeros_like(dq_ref)

        kt, vt = k_ref[0], v_ref[0]

        def step(i, carry, key0):
            dk, dv = carry
            rows = pl.ds(pl.multiple_of(i * tq, tq), tq)
            qt, dot_ = q_ref[0, rows, :], do_ref[rows, :]
            p = jnp.exp2(_dot_nt(kt, qt) - lse_ref[0, :, rows])
            if key0 is not None:
                p = jnp.where(_chunk_mask(tq, tk, key0, True), p, 0.0)
            ds = (p * (_dot_nt(vt, dot_) - delta_ref[0, :, rows])).astype(BF16)
            dq_ref[0, rows, :] += _dot_tn(ds, kt)
            return dk + _dot(ds, qt), dv + _dot(p, dot_)

        own = j // ATT_KEY_TILES
        carry = step(own, (jnp.zeros((tk, QK), F32), jnp.zeros((tk, HD), F32)), (j % ATT_KEY_TILES) * tk)
        dk, dv = lax.fori_loop(own + 1, t // tq, lambda i, cr: step(i, cr, None), carry)
        dk_ref[0] = dk * (1.0 / LOG2E)
        dv_ref[0] = dv

    return pl.pallas_call(
        body, name="flash_bwd", grid=(NH, t // tk),
        in_specs=[pl.BlockSpec((1, t, QK), lambda h, j: (h, 0, 0)), pl.BlockSpec((1, tk, QK), lambda h, j: (h, j, 0)),
                  pl.BlockSpec((1, tk, HD), lambda h, j: (h, j, 0)), pl.BlockSpec((t, HD), lambda h, j: (0, h)),
                  pl.BlockSpec((1, 1, t), lambda h, j: (h, 0, 0)), pl.BlockSpec((1, 1, t), lambda h, j: (h, 0, 0))],
        out_specs=[pl.BlockSpec((1, t, QK), lambda h, j: (h, 0, 0)), pl.BlockSpec((1, tk, QK), lambda h, j: (h, j, 0)),
                   pl.BlockSpec((1, tk, HD), lambda h, j: (h, j, 0))],
        out_shape=[jax.ShapeDtypeStruct((NH, t, QK), F32), jax.ShapeDtypeStruct((NH, t, QK), F32),
                   jax.ShapeDtypeStruct((NH, t, HD), F32)],
        compiler_params=_params(("parallel", "arbitrary")),
    )(q, k, v, do, lse_row, delta_row)


def _ffn_in(xn, w_in, name):
    t, k = xn.shape
    s, _, n = w_in.shape
    tm = _pick(t, ROW_TILES)

    def body(x_ref, w_ref, gu_ref, a_ref):
        x = x_ref[...]
        g, u = _dot(x, w_ref[0]), _dot(x, w_ref[1])
        gu_ref[0] = g.astype(BF16)
        gu_ref[1] = u.astype(BF16)
        a_ref[...] = (g * _sig(g) * u).astype(BF16)

    return pl.pallas_call(
        body, name=name, grid=(t // tm, s // 2),
        in_specs=[_BS((tm, k), lambda i, j: (i, 0)), _BS((2, None, k, n), lambda i, j: (0, j, 0, 0))],
        out_specs=[_BS((2, None, tm, n), lambda i, j: (0, j, i, 0)), _BS((None, tm, n), lambda i, j: (j, i, 0))],
        out_shape=[jax.ShapeDtypeStruct((2, s // 2, t, n), BF16), jax.ShapeDtypeStruct((s // 2, t, n), BF16)],
        compiler_params=_params(("parallel", "parallel")),
    )(xn, w_in.reshape(2, s // 2, k, n))


def _ffn_dgu(dfo, w_out, gu, name):
    t, k = dfo.shape
    s, n, _ = w_out.shape
    tm = _pick(t, ROW_TILES)

    def body(d_ref, w_ref, gu_ref, o_ref):
        da = _dot_nt(d_ref[...], w_ref[...])
        g, u = gu_ref[0].astype(F32), gu_ref[1].astype(F32)
        sg = _sig(g)
        o_ref[0] = (da * u * sg * (1.0 + g * (1.0 - sg))).astype(BF16)
        o_ref[1] = (da * g * sg).astype(BF16)

    pair = _BS((2, None, tm, n), lambda i, j: (0, j, i, 0))
    return pl.pallas_call(
        body, name=name, grid=(t // tm, s),
        in_specs=[_BS((tm, k), lambda i, j: (i, 0)), _BS((None, n, k), lambda i, j: (j, 0, 0)), pair],
        out_specs=pair, out_shape=jax.ShapeDtypeStruct((2, s, t, n), BF16),
        compiler_params=_params(("parallel", "parallel")),
    )(dfo, w_out, gu).reshape(2 * s, t, n)


def _ffn_fwd(xn, w_in, w_out, tag, epilogue):
    gu, a = _ffn_in(xn, w_in, tag + "_in")
    return gu, a, _mm_stack_red(a, w_out, tag + "_out", "nn", epilogue)


def _ffn_bwd(dfo, xn, gu, a, w_in, w_out, tag, epilogue):
    dw_out = _mm_stack_tn(a, dfo, tag + "_dwout")
    dgu = _ffn_dgu(dfo, w_out, gu, tag + "_dgu")
    dw_in = _mm_stack_tn(dgu, xn, tag + "_dwin")
    if callable(epilogue):
        epilogue = epilogue(dw_in, dw_out)
    return _mm_stack_red(dgu, w_in, tag + "_dxn", "nt", epilogue), dw_in, dw_out


def _local_step(x, target, cs, sn, w, late_weights, s, early_grads, last_grads):
    t = x.shape[0]
    tm = _pick(t, (256, 128))
    g = {}
    ffn_out = lambda n: w[n].reshape(4, FF // 4, D)

    def norm_fn(x_ref, g_ref, o_ref):
        o_ref[...] = _rms(x_ref[...], g_ref[...]).astype(BF16)

    xn1 = _rows(norm_fn, "norm1", t, tm, [(x, D, 0)], [s["ffn1_norm"]], [(D, BF16)])[0]
    def res_norm_fn(scale):
        def fn(h_ref, f_ref, g_ref, h_out, n_out):
            h = h_ref[...] + scale * f_ref[...]
            h_out[...] = h
            n_out[...] = _rms(h, g_ref[...]).astype(BF16)
        return fn

    gu1, a1, (h1, u) = _ffn_fwd(xn1, w["ffn1_w_in"], ffn_out("ffn1_w_out"), "ffn1",
                                (res_norm_fn(0.5), x, [], [s["mix_norm"]], [(D, F32), (D, BF16)], []))
    w = {**w, **late_weights(h1)}
    rows_of = lambda n: w[n].reshape(-1, w[n].shape[2])
    w_hgb, w_mlab, w_o = rows_of("w_hg_branch"), rows_of("w_mla_branch"), rows_of("w_out")
    w_in_nat = w["w_in"].transpose(1, 0, 2).reshape(D, -1)
    w_mrg = w["w_merge"]
    mw = w_mrg.shape[2]
    w_in_hg, w_in_mla = w_in_nat[:, :4 * HGW], w_in_nat[:, 4 * HGW:]
    p_hg = _mm(u, w_in_hg, "nn", "proj_hg")
    p_mla = _mm(u, w_in_mla, "nn", "proj_mla")
    gpre = _cols_fwd(u, w_mrg, "proj_gate")
    o_pre, hgy, states = _hgrn_fwd(p_hg, s["hg_lb_table"], s["hg_out_norm"], t)
    prep_args = (p_mla, cs, sn, w["w_q_up"], w["w_kv_up"], s["mla_q_lora_norm"], s["mla_kv_lora_norm"],
                 s["q_head_norm"], s["k_head_norm"])
    q, k, v = _mla_prep_fwd(*prep_args, t)
    att, lse = _flash_fwd(q, k, v, t)
    y_hg = _mm(hgy, w_hgb, "nn", "branch_hg")
    y_mla = _mm(att, w_mlab, "nn", "branch_mla")

    def mix_fn(gh_ref, gm_ref, yh_ref, ym_ref, b_ref, o_ref):
        gh = _sig(gh_ref[...] + b_ref[:, 0:D])
        gm = _sig(gm_ref[...] + b_ref[:, D:2 * D])
        o_ref[...] = (gh * yh_ref[...] + gm * ym_ref[...]).astype(BF16)

    mixed = _rows(mix_fn, "mix", t, tm, [(gpre, D, 0), (gpre, D, 1), (y_hg, D, 0), (y_mla, D, 0)], [s["b_merge"]],
                  [(D, BF16)])[0]
    h2, xn2 = _mm_stack_red(mixed[None], w_o[None], "mix_out", "nn",
                            (res_norm_fn(1.0), h1, [], [s["ffn2_norm"]], [(D, F32), (D, BF16)], []))

    def loss_fn(h_ref, f_ref, tg_ref, g_ref, dh_out, dhb_out):
        h = h_ref[...] + 0.5 * f_ref[...]
        e = _rms(h, g_ref[...]) - tg_ref[...]
        dh, dgain = _rms_bwd(h, g_ref[...], e / D)
        dh_out[...] = dh
        dhb_out[...] = (0.5 * dh).astype(BF16)
        return dgain, jnp.full((1, LANE), 0.5 / D * jnp.sum(e * e), F32)

    gu2, a2, (dh3, dfo2, g["final_norm"], loss) = _ffn_fwd(
        xn2, w["ffn2_w_in"], ffn_out("ffn2_w_out"), "ffn2",
        (loss_fn, h2, [target], [s["final_norm"]], [(D, F32), (D, BF16)], [(1, D), (1, LANE)]))

    def norm_bwd_fn(scale):
        def fn(h_ref, dxn_ref, dh_ref, g_ref, dh_out, dhb_out):
            dx, dgain = _rms_bwd(h_ref[...], g_ref[...], dxn_ref[...])
            dh = dh_ref[...] + dx
            dh_out[...] = dh
            dhb_out[...] = (scale * dh).astype(BF16)
            return (dgain,)
        return fn

    as_rows = lambda a: a.reshape((N_DEV, -1) + a.shape[-1:])
    (dh2, dh2b, g["ffn2_norm"]), g["ffn2_w_in"], dwo = _ffn_bwd(
        dfo2, xn2, gu2, a2, w["ffn2_w_in"], ffn_out("ffn2_w_out"), "ffn2",
        (norm_bwd_fn(1.0), h2, [dh3], [s["ffn2_norm"]], [(D, F32), (D, BF16)], [(1, D)]))
    g["ffn2_w_out"] = as_rows(dwo)
    dmixed = _mm(dh2b, w_o, "nt", "mix_out_dx")
    g["w_out"] = as_rows(_mm(mixed, dh2b, "tn", "mix_out_dw"))

    def mix_bwd_fn(gh_ref, gm_ref, yh_ref, ym_ref, dm_ref, b_ref, dyh_out, dym_out, dg_out):
        gh = _sig(gh_ref[...] + b_ref[:, 0:D])
        gm = _sig(gm_ref[...] + b_ref[:, D:2 * D])
        dm = dm_ref[...]
        dyh_out[...] = (dm * gh).astype(BF16)
        dym_out[...] = (dm * gm).astype(BF16)
        dgh = dm * yh_ref[...] * gh * (1.0 - gh)
        dgm = dm * ym_ref[...] * gm * (1.0 - gm)
        dg_out[:, 0:D] = dgh.astype(BF16)
        dg_out[:, D:2 * D] = dgm.astype(BF16)
        return (jnp.concatenate([jnp.sum(dgh, axis=0, keepdims=True), jnp.sum(dgm, axis=0, keepdims=True)], axis=1),)

    dyh, dym, dgpre, g["b_merge"] = _rows(
        mix_bwd_fn, "mix_bwd", t, tm, [(gpre, D, 0), (gpre, D, 1), (y_hg, D, 0), (y_mla, D, 0), (dmixed, D, 0)],
        [s["b_merge"]], [(D, BF16), (D, BF16), (2 * D, BF16)], [(1, 2 * D)])
    g["w_hg_branch"] = as_rows(_mm(hgy, dyh, "tn", "branch_hg_dw"))
    g["w_mla_branch"] = as_rows(_mm(att, dym, "tn", "branch_mla_dw"))
    g["w_merge"] = _cols_dw(u, dgpre, mw, "proj_gate_dw")
    dhgy = _mm(dyh, w_hgb, "nt", "branch_hg_dx")
    datt, delta = _attn_out_bwd(dym, w_mlab, att, t)
    du_gate = _cols_dx(dgpre, w_mrg, "proj_gate_dx")

    dq, dk, dv = _flash_bwd(q, k, v, datt, lse.reshape(NH, 1, t), delta, t)
    (dp_mla, dwq, dwkv, g["mla_q_lora_norm"], g["mla_kv_lora_norm"], g["q_head_norm"],
     g["k_head_norm"]) = _mla_prep_bwd(*prep_args, dq, dk, dv, t)
    g["w_q_up"], g["w_kv_up"] = dwq.astype(GRAD_DT), dwkv.astype(GRAD_DT)
    dp_hg, g["hg_lb_table"], g["hg_out_norm"] = _hgrn_bwd(p_hg, s["hg_lb_table"], s["hg_out_norm"], o_pre, states,
                                                          dhgy, t)
    dw_in_nat = jnp.concatenate([_mm(u, dp_hg, "tn", "proj_hg_dw"), _mm(u, dp_mla, "tn", "proj_mla_dw")], axis=1)
    g["w_in"] = dw_in_nat.T.reshape(N_DEV, -1, D)
    du_hg = _mm(dp_hg, w_in_hg, "nt", "proj_hg_dx")
    du_mla = _mm(dp_mla, w_in_mla, "nt", "proj_mla_dx")

    def mixnorm_bwd_fn(h_ref, a_ref, b_ref, c_ref, dh_ref, g_ref, dh_out, dhb_out):
        dx, dgain = _rms_bwd(h_ref[...], g_ref[...], a_ref[...] + b_ref[...] + c_ref[...])
        dh = dh_ref[...] + dx
        dh_out[...] = dh
        dhb_out[...] = (0.5 * dh).astype(BF16)
        return (dgain,)

    mix_gain = s["mix_norm"] + early_grads(g)[0:1, 0:1]
    dh1, dfo1, g["mix_norm"] = _rows(mixnorm_bwd_fn, "mixnorm_bwd", t, tm,
                                     [(h1, D, 0), (du_hg, D, 0), (du_mla, D, 0), (du_gate, D, 0), (dh2, D, 0)],
                                     [mix_gain], [(D, F32), (D, BF16)], [(1, D)])
    def last_stage(dw_in, dw_out):
        g["ffn1_w_in"], g["ffn1_w_out"] = dw_in, as_rows(dw_out)
        gain = s["ffn1_norm"] + last_grads(g)[0:1, 0:1]
        return norm_bwd_fn(1.0), x, [dh1], [gain], [(D, F32), (D, BF16)], [(1, D)]

    (grad_x, _, g["ffn1_norm"]), _, _ = _ffn_bwd(dfo1, xn1, gu1, a1, w["ffn1_w_in"], ffn_out("ffn1_w_out"), "ffn1",
                                                 last_stage)
    return loss, grad_x, g


def _coords():
    return lax.axis_index("x"), lax.axis_index("y"), lax.axis_index("c")


def _hbm_call(body, name, ins, out_shapes, scratch):
    any_spec = pl.BlockSpec(memory_space=pl.ANY)
    return pl.pallas_call(
        body, name=name, out_shape=[jax.ShapeDtypeStruct(s, dt) for s, dt in out_shapes],
        in_specs=[any_spec] * len(ins), out_specs=[any_spec] * len(out_shapes), scratch_shapes=scratch,
    )(*ins)


def _my_slot():
    return 4 * lax.axis_index("x") + 2 * lax.axis_index("y") + lax.axis_index("c")


def _put_own(buf, own, index):
    return lax.dynamic_update_index_in_dim(buf, own, index, 0)


def _all_gather(blocks, name):
    nb = len(blocks)

    def body(*refs):
        x_refs, out_refs = refs[:nb], refs[nb:2 * nb]
        send_sems, recv_sems = refs[2 * nb:]
        x, y, c = _coords()
        me, sibling = (x, y, c), (x, y, 1 - c)
        chips = [(1 - x, y), (x, 1 - y), (1 - x, 1 - y)]

        def slot(b, px, py, pc):
            return out_refs[b].at[4 * px + 2 * py + pc]

        def copy(b, kk, block_of, to, src=None):
            return pltpu.make_async_remote_copy(
                src_ref=slot(b, *block_of) if src is None else src, dst_ref=slot(b, *block_of),
                send_sem=send_sems.at[b, kk], recv_sem=recv_sems.at[b, kk], device_id=to, device_id_type=MESH)

        first = [copy(b, 0, me, sibling, src=x_refs[b]) for b in range(nb)]
        first += [copy(b, 1 + j, me, (*chip, c), src=x_refs[b]) for j, chip in enumerate(chips) for b in range(nb)]
        for cp in first:
            cp.start()
        passed = []
        for j, chip in enumerate(chips):
            for b in range(nb):
                copy(b, 1 + j, (*chip, c), me).wait_recv()
                passed.append(copy(b, 4 + j, (*chip, c), sibling))
                passed[-1].start()
        for b in range(nb):
            copy(b, 0, sibling, me).wait_recv()
        for j, chip in enumerate(chips):
            for b in range(nb):
                copy(b, 4 + j, (*chip, 1 - c), me).wait_recv()
        for cp in first + passed:
            cp.wait_send()

    outs = _hbm_call(body, name, blocks, [((N_DEV,) + b.shape, b.dtype) for b in blocks],
                     [pltpu.SemaphoreType.DMA((nb, 7)), pltpu.SemaphoreType.DMA((nb, 7))])
    return [_put_own(o, b[None], _my_slot()) for o, b in zip(outs, blocks)]


def _gather_peers():
    x, y, c = _coords()
    return (x, y, c), [(x, y, 1 - c), (1 - x, y, c), (x, 1 - y, c), (1 - x, 1 - y, c)]


def _gather_start(blocks, after, name):
    nb = len(blocks)
    hbm, sem = pl.BlockSpec(memory_space=pltpu.HBM), pl.BlockSpec(memory_space=pltpu.SEMAPHORE)

    def body(*refs):
        x_refs, out_refs = refs[:nb], refs[nb:2 * nb]
        send_sems, recv_sems, token = refs[2 * nb + 1], refs[2 * nb + 2], refs[-1]
        (x, y, c), peers = _gather_peers()
        for kk, peer in enumerate(peers):
            for b in range(nb):
                pltpu.make_async_remote_copy(
                    src_ref=x_refs[b], dst_ref=out_refs[b].at[4 * x + 2 * y + c], send_sem=send_sems.at[4 * b + kk],
                    recv_sem=recv_sems.at[4 * b + kk], device_id=peer, device_id_type=MESH).start()
        token[...] = jnp.zeros_like(token)

    gathers = [pltpu.with_memory_space_constraint(lax.empty((N_DEV,) + b.shape, b.dtype), pltpu.HBM) for b in blocks]
    outs = pl.pallas_call(
        body, name=name,
        out_shape=(pltpu.SemaphoreType.DMA((4 * nb,)), pltpu.SemaphoreType.DMA((4 * nb,)),
                   *[pltpu.HBM(b.shape, b.dtype) for b in blocks], *[pltpu.HBM(b.shape, b.dtype) for b in gathers],
                   jax.ShapeDtypeStruct((SUBLANE, LANE), F32)),
        in_specs=[hbm] * (2 * nb) + [pl.BlockSpec(memory_space=pl.ANY)],
        out_specs=(sem, sem, *[hbm] * (2 * nb), pl.BlockSpec(memory_space=pltpu.VMEM)),
        input_output_aliases={i: 2 + i for i in range(2 * nb)},
        compiler_params=pltpu.CompilerParams(has_side_effects=pltpu.SideEffectType.DATAFLOW_SIDE_EFFECTING),
    )(*[pltpu.with_memory_space_constraint(b, pltpu.HBM) for b in blocks], *gathers, after)
    return outs[0], outs[1], list(outs[2:2 + nb]), list(outs[2 + nb:2 + 2 * nb]), outs[-1]


def _gather_wait(send_sems, recv_sems, thru, gathers, after, name):
    nb = len(thru)
    hbm, sem = pl.BlockSpec(memory_space=pltpu.HBM), pl.BlockSpec(memory_space=pltpu.SEMAPHORE)

    def body(*refs):
        x_refs, out_refs = refs[:nb], refs[nb:2 * nb]
        send_sems_, recv_sems_ = refs[2 * nb], refs[2 * nb + 1]
        _, peers = _gather_peers()
        for kk, (px, py, pc) in enumerate(peers):
            for b in range(nb):
                cp = pltpu.make_async_remote_copy(
                    src_ref=x_refs[b], dst_ref=out_refs[b].at[4 * px + 2 * py + pc], send_sem=send_sems_.at[4 * b + kk],
                    recv_sem=recv_sems_.at[4 * b + kk], device_id=(px, py, pc), device_id_type=MESH)
                cp.wait_send()
                cp.wait_recv()

    outs = pl.pallas_call(
        body, name=name,
        out_shape=(*[pltpu.HBM(b.shape, b.dtype) for b in thru], *[pltpu.HBM(b.shape, b.dtype) for b in gathers]),
        in_specs=[hbm] * (2 * nb) + [sem, sem, pl.BlockSpec(memory_space=pl.ANY)], out_specs=[hbm] * (2 * nb),
        input_output_aliases={i: i for i in range(2 * nb)},
        compiler_params=pltpu.CompilerParams(has_side_effects=pltpu.SideEffectType.DATAFLOW_SIDE_EFFECTING),
    )(*thru, *gathers, send_sems, recv_sems, after)
    return list(outs[:nb]), list(outs[nb:])


def _gather_finish(blocks, gathers, name):
    nb = len(blocks)

    def body(*refs):
        x_refs, in_refs, out_refs = refs[:nb], refs[nb:2 * nb], refs[2 * nb:3 * nb]
        send_sems, recv_sems = refs[3 * nb:]
        (x, y, c), peers = _gather_peers()
        copies = []
        for j, (px, py, _) in enumerate(peers[1:]):
            for b in range(nb):
                copies.append(pltpu.make_async_remote_copy(
                    src_ref=in_refs[b].at[4 * px + 2 * py + c], dst_ref=out_refs[b].at[4 * px + 2 * py + c],
                    send_sem=send_sems.at[b, j], recv_sem=recv_sems.at[b, j], device_id=(x, y, 1 - c),
                    device_id_type=MESH))
                copies[-1].start()
        for j, (px, py, _) in enumerate(peers[1:]):
            for b in range(nb):
                pltpu.make_async_remote_copy(
                    src_ref=in_refs[b].at[4 * px + 2 * py + c], dst_ref=out_refs[b].at[4 * px + 2 * py + 1 - c],
                    send_sem=send_sems.at[b, j], recv_sem=recv_sems.at[b, j], device_id=(x, y, 1 - c),
                    device_id_type=MESH).wait_recv()
        for cp in copies:
            cp.wait_send()

    any_spec = pl.BlockSpec(memory_space=pl.ANY)
    outs = pl.pallas_call(
        body, name=name, out_shape=[jax.ShapeDtypeStruct(b.shape, b.dtype) for b in gathers],
        in_specs=[any_spec] * (2 * nb), out_specs=[any_spec] * nb,
        input_output_aliases={nb + i: i for i in range(nb)},
        scratch_shapes=[pltpu.SemaphoreType.DMA((nb, 3)), pltpu.SemaphoreType.DMA((nb, 3))],
    )(*blocks, *gathers)
    return [_put_own(o, b[None], _my_slot()) for o, b in zip(outs, blocks)]


def _sibling_swap(bufs, name):
    nb = len(bufs)

    def body(*refs):
        x_refs, out_refs = refs[:nb], refs[nb:2 * nb]
        send_sems, recv_sems = refs[2 * nb:]
        x, y, c = _coords()
        copies = [pltpu.make_async_remote_copy(
            src_ref=x_refs[b].at[2 * q + 1 - c], dst_ref=out_refs[b].at[q], send_sem=send_sems.at[b, q],
            recv_sem=recv_sems.at[b, q], device_id=(x, y, 1 - c), device_id_type=MESH)
            for b in range(nb) for q in range(4)]
        for cp in copies:
            cp.start()
        for cp in copies:
            cp.wait()

    return _hbm_call(body, name, bufs, [((4,) + b.shape[1:], b.dtype) for b in bufs],
                     [pltpu.SemaphoreType.DMA((nb, 4)), pltpu.SemaphoreType.DMA((nb, 4))])


def _chip_exchange_start(bufs, name):
    nb = len(bufs)
    hbm, sem = pl.BlockSpec(memory_space=pltpu.HBM), pl.BlockSpec(memory_space=pltpu.SEMAPHORE)

    def body(*refs):
        x_refs, land_refs = refs[:nb], refs[nb:2 * nb]
        send_sems, recv_sems, token = refs[2 * nb], refs[2 * nb + 1], refs[-1]
        x, y, c = _coords()
        for j, (px, py) in enumerate([(1 - x, y), (x, 1 - y), (1 - x, 1 - y)]):
            for b in range(nb):
                pltpu.make_async_remote_copy(
                    src_ref=x_refs[b].at[2 * px + py], dst_ref=land_refs[b].at[2 * x + y], send_sem=send_sems.at[3 * b + j],
                    recv_sem=recv_sems.at[3 * b + j], device_id=(px, py, c), device_id_type=MESH).start()
        token[...] = jnp.zeros_like(token)

    lands = [pltpu.with_memory_space_constraint(lax.empty(b.shape, b.dtype), pltpu.HBM) for b in bufs]
    outs = pl.pallas_call(
        body, name=name,
        out_shape=(pltpu.SemaphoreType.DMA((3 * nb,)), pltpu.SemaphoreType.DMA((3 * nb,)),
                   *[pltpu.HBM(b.shape, b.dtype) for b in bufs], *[pltpu.HBM(b.shape, b.dtype) for b in bufs],
                   jax.ShapeDtypeStruct((SUBLANE, LANE), F32)),
        in_specs=[hbm] * (2 * nb), out_specs=(sem, sem, *[hbm] * (2 * nb), pl.BlockSpec(memory_space=pltpu.VMEM)),
        input_output_aliases={i: 2 + i for i in range(2 * nb)},
        compiler_params=pltpu.CompilerParams(has_side_effects=pltpu.SideEffectType.DATAFLOW_SIDE_EFFECTING),
    )(*[pltpu.with_memory_space_constraint(b, pltpu.HBM) for b in bufs], *lands)
    return outs[0], outs[1], list(outs[2:2 + nb]), list(outs[2 + nb:2 + 2 * nb]), outs[-1]


def _chip_exchange_wait(send_sems, recv_sems, thru, lands, after, name):
    nb = len(thru)
    hbm, sem = pl.BlockSpec(memory_space=pltpu.HBM), pl.BlockSpec(memory_space=pltpu.SEMAPHORE)

    def body(*refs):
        x_refs, land_refs = refs[:nb], refs[nb:2 * nb]
        send_sems_, recv_sems_ = refs[2 * nb], refs[2 * nb + 1]
        x, y, c = _coords()
        for j, (px, py) in enumerate([(1 - x, y), (x, 1 - y), (1 - x, 1 - y)]):
            for b in range(nb):
                cp = pltpu.make_async_remote_copy(
                    src_ref=x_refs[b].at[2 * px + py], dst_ref=land_refs[b].at[2 * px + py],
                    send_sem=send_sems_.at[3 * b + j], recv_sem=recv_sems_.at[3 * b + j], device_id=(px, py, c),
                    device_id_type=MESH)
                cp.wait_send()
                cp.wait_recv()

    outs = pl.pallas_call(
        body, name=name,
        out_shape=(*[pltpu.HBM(b.shape, b.dtype) for b in thru], *[pltpu.HBM(b.shape, b.dtype) for b in lands]),
        in_specs=[hbm] * (2 * nb) + [sem, sem, pl.BlockSpec(memory_space=pl.ANY)], out_specs=[hbm] * (2 * nb),
        input_output_aliases={i: i for i in range(2 * nb)},
        compiler_params=pltpu.CompilerParams(has_side_effects=pltpu.SideEffectType.DATAFLOW_SIDE_EFFECTING),
    )(*thru, *lands, send_sems, recv_sems, after)
    return list(outs[:nb]), list(outs[nb:])


def _chip_sum(g, r1, c, name):
    _, r, cw = g.shape
    tr = _pick(r, (256, 176, 128))

    def body(c_ref, g_ref, r_ref, o_ref):
        o_ref[...] = (g_ref[...].astype(F32) + r_ref[...].astype(F32)).astype(GRAD_DT)

    grid_spec = pltpu.PrefetchScalarGridSpec(
        num_scalar_prefetch=1, grid=(4, r // tr),
        in_specs=[_BS((None, None, tr, cw), lambda q, i, c_ref: (q, c_ref[0], i, 0)),
                  _BS((None, tr, cw), lambda q, i, c_ref: (q, i, 0))],
        out_specs=_BS((None, tr, cw), lambda q, i, c_ref: (q, i, 0)))
    return pl.pallas_call(
        body, name=name, grid_spec=grid_spec, out_shape=jax.ShapeDtypeStruct((4, r, cw), GRAD_DT),
        compiler_params=_params(("parallel", "parallel")),
    )(c.reshape(1).astype(jnp.int32), g.reshape(4, 2, r, cw), r1)


def _adamw_math(w, g, m, v):
    m = B1 * m + (1.0 - B1) * g
    v = B2 * v + (1.0 - B2) * (g * g)
    m_hat = m / (1.0 - B1 ** STEP)
    v_hat = v / (1.0 - B2 ** STEP)
    return -LR * (m_hat / (jnp.sqrt(v_hat) + AEPS) + WD * w), m, v


def _sum_adamw(parts, w, m, v, name):
    r, c = w.shape
    tr = _pick(r, (256, 176, 128))
    tc = _pick(c, (256, 128)) if tr == r and r > 256 else c

    def body(p0, p1, p2, p3, w_ref, m_ref, v_ref, g_out, d_out, m_out, v_out):
        g = ((p0[...].astype(F32) + p1[...].astype(F32)) + p2[...].astype(F32)) + p3[...].astype(F32)
        g_out[...] = g
        d_out[...], m_out[...], v_out[...] = _adamw_math(w_ref[...], g, m_ref[...], v_ref[...])

    part = lambda q: _BS((None, tr, tc), functools.partial(lambda i, j, q: (q, i, j), q=q))
    plain = _BS((tr, tc), lambda i, j: (i, j))
    return pl.pallas_call(
        body, name=name, grid=(r // tr, c // tc), in_specs=[part(q) for q in range(4)] + [plain] * 3,
        out_specs=[plain] * 4, out_shape=[jax.ShapeDtypeStruct((r, c), F32)] * 4,
        compiler_params=_params(("parallel", "parallel")),
    )(parts, parts, parts, parts, w, m, v)


def _small_update(gathered, w, m, v):
    r = w.shape[0]

    def body(ga_ref, w_ref, m_ref, v_ref, g_out, d_out, m_out, v_out):
        g = ga_ref[0]
        for dev in range(1, N_DEV):
            g = g + ga_ref[dev]
        g_out[...] = g
        d_out[...], m_out[...], v_out[...] = _adamw_math(w_ref[...], g, m_ref[...], v_ref[...])

    return pl.pallas_call(
        body, name="small_update", out_shape=[jax.ShapeDtypeStruct((r, LANE), F32)] * 4,
    )(gathered, w, m, v)


def _pack_small(vals):
    rows = []
    for name, (r, n) in SMALL:
        flat = vals[name].reshape(-1)
        pad = (-flat.shape[0]) % (SUBLANE * LANE)
        rows.append(jnp.pad(flat, (0, pad)).reshape(-1, LANE))
    return jnp.concatenate(rows, axis=0)


def _unpack_small(packed):
    out, off = {}, 0
    for name, (r, n) in SMALL:
        nrow = -(-(r * n) // (SUBLANE * LANE)) * SUBLANE
        out[name] = packed[off:off + nrow].reshape(-1)[:r * n].reshape(r, n)
        off += nrow
    return out


def kernel(x, positions, ffn1_norm, ffn1_w_in, ffn1_w_out, mix_norm, w_in, hg_lb_table, hg_out_norm, w_hg_branch, mla_q_lora_norm, w_q_up, mla_kv_lora_norm, w_kv_up, q_head_norm, k_head_norm, w_mla_branch, w_merge, b_merge, w_out, ffn2_norm, ffn2_w_in, ffn2_w_out, final_norm, loss_target, m_ffn1_norm, m_ffn1_w_in, m_ffn1_w_out, m_mix_norm, m_w_in, m_hg_lb_table, m_hg_out_norm, m_w_hg_branch, m_mla_q_lora_norm, m_w_q_up, m_mla_kv_lora_norm, m_w_kv_up, m_q_head_norm, m_k_head_norm, m_w_mla_branch, m_w_merge, m_b_merge, m_w_out, m_ffn2_norm, m_ffn2_w_in, m_ffn2_w_out, m_final_norm, v_ffn1_norm, v_ffn1_w_in, v_ffn1_w_out, v_mix_norm, v_w_in, v_hg_lb_table, v_hg_out_norm, v_w_hg_branch, v_mla_q_lora_norm, v_w_q_up, v_mla_kv_lora_norm, v_w_kv_up, v_q_head_norm, v_k_head_norm, v_w_mla_branch, v_w_merge, v_b_merge, v_w_out, v_ffn2_norm, v_ffn2_w_in, v_ffn2_w_out, v_final_norm):
    args = dict(locals())
    t = x.shape[1]
    big_w = {n: args[n][0] for n, _, _ in BIG}
    small = {n: args[n].reshape(shape) for n, shape in SMALL}

    names = [n for n, _, _ in BIG]
    first, rest = names[:2], names[2:]
    full = dict(zip(first, _all_gather([big_w[n].astype(BF16) for n in first], "weights_all_gather_ffn1")))
    g_send, g_recv, g_thru, g_bufs, g_token = _gather_start([big_w[n].astype(BF16) for n in rest], full[first[0]],
                                                            "weights_gather_start")
    gains = dict(small, ffn1_norm=small["ffn1_norm"] + g_token[0:1, 0:1])

    def late_weights(after):
        blocks, bufs = _gather_wait(g_send, g_recv, g_thru, g_bufs, after, "weights_gather_wait")
        return dict(zip(rest, _gather_finish(blocks, bufs, "weights_gather_finish")))

    inv_freq = ROPE_THETA ** (-jnp.arange(0, ROPE, 2, dtype=F32) / ROPE)
    ang = positions[0].astype(F32)[:, None] * inv_freq
    cs = jnp.concatenate([jnp.cos(ang), jnp.cos(ang)], axis=1)
    sn = jnp.concatenate([jnp.sin(ang), jnp.sin(ang)], axis=1)

    c = lax.axis_index("c")
    chip = 2 * lax.axis_index("x") + lax.axis_index("y")
    early = {}

    def chip_sums_of(g, ns, tag):
        from_sibling = _sibling_swap([g[n] for n in ns], "grads_sibling_swap_" + tag)
        return [_chip_sum(g[n], r1, c, "chip_sum_" + n) for n, r1 in zip(ns, from_sibling)]

    def early_grads(g):
        early["names"] = [n for n in names if n in g]
        early["sums"] = chip_sums_of(g, early["names"], "early")
        early["send"], early["recv"], early["thru"], early["lands"], token = _chip_exchange_start(
            early["sums"], "grads_exchange_start")
        return token

    def last_grads(g):
        early["late"] = [n for n in names if n not in early["names"]]
        early["l_send"], early["l_recv"], early["l_thru"], early["l_lands"], token = _chip_exchange_start(
            chip_sums_of(g, early["late"], "late"), "grads_exchange_late_start")
        return token

    loss_row, grad_x, g = _local_step(x[0], loss_target[0], cs, sn, full, late_weights, gains, early_grads,
                                      last_grads)
    late, l_send, l_recv = early["late"], early["l_send"], early["l_recv"]
    l_thru, l_lands = early["l_thru"], early["l_lands"]
    sent, landed = _chip_exchange_wait(early["send"], early["recv"], early["thru"], early["lands"], grad_x,
                                       "grads_exchange_wait")
    exchanged = {n: _put_own(land, lax.dynamic_index_in_dim(own, chip, 0), chip)
                 for n, land, own in zip(early["names"], landed, sent)}

    small_packed = jnp.concatenate([_pack_small(g), jnp.pad(loss_row, ((0, SUBLANE - 1), (0, 0)))], axis=0)
    small_all = _all_gather([small_packed], "small_all_gather")[0]
    zero_tail = jnp.zeros((SUBLANE, LANE), F32)
    pk = lambda d: jnp.concatenate([_pack_small(d), zero_tail], axis=0)
    sg, sd, sm, sv = _small_update(
        small_all, pk(small), pk({n: args["m_" + n].reshape(shape) for n, shape in SMALL}),
        pk({n: args["v_" + n].reshape(shape) for n, shape in SMALL}))
    n_small_rows = _pack_small(small).shape[0]
    loss = sg[n_small_rows, 0]
    outs = {k_: _unpack_small(a) for k_, a in (("grad", sg), ("delta", sd), ("new_m", sm), ("new_v", sv))}

    def adamw(n):
        tr = (lambda a: a.T) if n in GRAD_T else (lambda a: a)
        res = _sum_adamw(exchanged[n], tr(big_w[n]), tr(args["m_" + n][0]), tr(args["v_" + n][0]), "adamw_" + n)
        outs["grad"][n], outs["delta"][n], outs["new_m"][n], outs["new_v"][n] = [tr(a) for a in res]

    for n in early["names"]:
        adamw(n)
    sent, landed = _chip_exchange_wait(l_send, l_recv, l_thru, l_lands, outs["delta"][early["names"][-1]],
                                       "grads_exchange_late_wait")
    for n, land, own in zip(late, landed, sent):
        exchanged[n] = _put_own(land, lax.dynamic_index_in_dim(own, chip, 0), chip)
        adamw(n)

    def shaped(kind, n):
        return outs[kind][n].reshape(args[n].shape)

    return (loss, grad_x[None], *[shaped("grad", n) for n in WEIGHT_ORDER], *[shaped("delta", n) for n in WEIGHT_ORDER],
            *[shaped("new_m", n) for n in WEIGHT_ORDER], *[shaped("new_v", n) for n in WEIGHT_ORDER])
```

```python
import functools

import jax
import jax.numpy as jnp
from jax import lax
from jax.experimental import pallas as pl
from jax.experimental.pallas import tpu as pltpu

F32 = jnp.float32
BF16 = jnp.bfloat16

D = 1024
FF = 2816
NH = 8
HD = 128
ROPE = 64
QK = HD + ROPE
QL = 384
KVL = 256
HGW = NH * HD
CHUNK = 64
EPS = 1e-6
ROPE_THETA = 10000.0
SCALE = QK ** -0.5
LOG2E = 1.4426950408889634

LR, B1, B2, AEPS, WD, STEP = 0.001, 0.9, 0.999, 1e-08, 0.01, 10

HB = 128
SUB = 16
EXP_CLAMP = 80.0
ATT_TILES = (512, 256, 128)
ATT_Q = (1024, 512, 256, 128)
ATT_KEY_TILES = 1
GRAD_DT = BF16
ROW_TILES = (1024, 512, 256, 128)

LANE = 128
SUBLANE = 8
VMEM_LIMIT = 56 << 20

N_DEV = 8
MESH = pl.DeviceIdType.MESH

BIG = (
    ("ffn1_w_in", (D, 2 * FF), 1), ("ffn1_w_out", (FF, D), 0), ("w_in", (D, 4800), 1),
    ("w_hg_branch", (HGW, D), 0), ("w_q_up", (QL, NH * QK), 1), ("w_kv_up", (KVL, NH * 2 * HD), 1),
    ("w_mla_branch", (NH * HD, D), 0), ("w_merge", (D, 2 * D), 1), ("w_out", (D, D), 0),
    ("ffn2_w_in", (D, 2 * FF), 1), ("ffn2_w_out", (FF, D), 0),
)
SMALL = (
    ("ffn1_norm", (1, D)), ("mix_norm", (1, D)), ("hg_lb_table", (2, HGW)), ("hg_out_norm", (1, HD)),
    ("mla_q_lora_norm", (1, QL)), ("mla_kv_lora_norm", (1, KVL)), ("q_head_norm", (1, QK)),
    ("k_head_norm", (1, QK)), ("b_merge", (1, 2 * D)), ("ffn2_norm", (1, D)), ("final_norm", (1, D)),
)
GRAD_T = ("ffn1_w_in", "ffn2_w_in", "w_in", "w_q_up")
WEIGHT_ORDER = ("ffn1_norm", "ffn1_w_in", "ffn1_w_out", "mix_norm", "w_in", "hg_lb_table", "hg_out_norm",
                "w_hg_branch", "mla_q_lora_norm", "w_q_up", "mla_kv_lora_norm", "w_kv_up", "q_head_norm",
                "k_head_norm", "w_mla_branch", "w_merge", "b_merge", "w_out", "ffn2_norm", "ffn2_w_in",
                "ffn2_w_out", "final_norm")


def _pick(n, cands):
    for c in cands:
        if n % c == 0:
            return c
    return n


def _params(sem):
    return pltpu.CompilerParams(dimension_semantics=sem, vmem_limit_bytes=VMEM_LIMIT)


def _sig(x):
    return 0.5 * jnp.tanh(0.5 * x) + 0.5


def _dot(a, b):
    return jnp.dot(a.astype(BF16), b.astype(BF16), preferred_element_type=F32)


def _dot_nt(a, b):
    return lax.dot_general(a.astype(BF16), b.astype(BF16), (((1,), (1,)), ((), ())),
                           preferred_element_type=F32)


def _dot_tn(a, b):
    return lax.dot_general(a.astype(BF16), b.astype(BF16), (((0,), (0,)), ((), ())),
                           preferred_element_type=F32)


def _split3(x):
    x1 = x.astype(BF16)
    r1 = x - x1.astype(F32)
    x2 = r1.astype(BF16)
    x3 = (r1 - x2.astype(F32)).astype(BF16)
    return x1, x2, x3


def _dot_sel(m, x):
    x1, x2, x3 = _split3(x)
    d = lambda p: jnp.dot(m, p, preferred_element_type=F32)
    return d(x1) + d(x2) + d(x3)


def _sel_dot(x, m):
    x1, x2, x3 = _split3(x)
    d = lambda p: jnp.dot(p, m, preferred_element_type=F32)
    return d(x1) + d(x2) + d(x3)


_TN = (1408, 1024, 768, 512, 384, 256, 128)


def _accumulate(acc, o_ref, axis, steps, term):
    step = pl.program_id(axis)

    @pl.when(step == 0)
    def _():
        acc[...] = jnp.zeros_like(acc)

    val = term()
    if isinstance(val, (list, tuple)):
        for e, v in enumerate(val):
            acc[e] += v
    else:
        acc[...] += val

    @pl.when(step == steps - 1)
    def _():
        o_ref[...] = acc[...].astype(o_ref.dtype)


def _mm(a, b, mode, name, out_dtype=F32):
    if mode == "tn":
        t, m = a.shape
        n = b.shape[1]
        tt, tm, tn = _pick(t, (512, 256, 128)), _pick(m, _TN), _pick(n, _TN)

        def body(a_ref, b_ref, o_ref, acc):
            _accumulate(acc, o_ref, 2, t // tt, lambda: _dot_tn(a_ref[...], b_ref[...]))

        return pl.pallas_call(
            body, name=name, grid=(m // tm, n // tn, t // tt),
            in_specs=[pl.BlockSpec((tt, tm), lambda i, j, k: (k, i)),
                      pl.BlockSpec((tt, tn), lambda i, j, k: (k, j))],
            out_specs=pl.BlockSpec((tm, tn), lambda i, j, k: (i, j)),
            out_shape=jax.ShapeDtypeStruct((m, n), GRAD_DT), scratch_shapes=[pltpu.VMEM((tm, tn), F32)],
            compiler_params=_params(("parallel", "parallel", "arbitrary")),
        )(a, b)

    m, k = a.shape
    tm = _pick(m, ROW_TILES)
    if mode == "nn":
        n = b.shape[1]
        tn = _pick(n, _TN)
        b_spec = pl.BlockSpec((k, tn), lambda i, j: (0, j))
        dot = _dot
    else:
        n = b.shape[0]
        tn = _pick(n, _TN if k <= 4096 else (512, 256, 128))
        b_spec = pl.BlockSpec((tn, k), lambda i, j: (j, 0))
        dot = _dot_nt

    def body(a_ref, b_ref, o_ref):
        o_ref[...] = dot(a_ref[...], b_ref[...]).astype(o_ref.dtype)

    return pl.pallas_call(
        body, name=name, grid=(m // tm, n // tn),
        in_specs=[pl.BlockSpec((tm, k), lambda i, j: (i, 0)), b_spec],
        out_specs=pl.BlockSpec((tm, tn), lambda i, j: (i, j)),
        out_shape=jax.ShapeDtypeStruct((m, n), out_dtype),
        compiler_params=_params(("parallel", "parallel")),
    )(a, b)


_DOTS = {"nn": _dot, "nt": _dot_nt, "tn": _dot_tn}
_BS = pl.BlockSpec


def _mmcall(name, kind, a, b, a_spec, b_spec, o_spec, o_shape, grid, red_axis=None, out_dtype=F32):
    dot = _DOTS[kind]

    def body(a_ref, b_ref, o_ref):
        if red_axis is None:
            o_ref[...] = dot(a_ref[...], b_ref[...]).astype(o_ref.dtype)
        else:
            @pl.when(pl.program_id(red_axis) == 0)
            def _():
                o_ref[...] = jnp.zeros_like(o_ref)

            o_ref[...] += dot(a_ref[...], b_ref[...])

    sem = tuple("arbitrary" if ax == red_axis else "parallel" for ax in range(len(grid)))
    return pl.pallas_call(
        body, name=name, grid=grid, in_specs=[a_spec, b_spec], out_specs=o_spec,
        out_shape=jax.ShapeDtypeStruct(o_shape, out_dtype), compiler_params=_params(sem),
    )(a, b)


class _AsRef:
    def __init__(self, value):
        self.value = value

    def __getitem__(self, idx):
        return self.value


def _mm_stack_red(a, w, name, kind, epilogue=None):
    s, t, n = a.shape
    nout = w.shape[2] if kind == "nn" else w.shape[1]
    if epilogue is None:
        tm = _pick(t, ROW_TILES)
        return _mmcall(name, kind, a, w, _BS((None, tm, n), lambda i, j: (j, i, 0)),
                       _BS((None,) + w.shape[1:], lambda i, j: (j, 0, 0)), _BS((tm, nout), lambda i, j: (i, 0)),
                       (t, nout), (t // tm, s), red_axis=1)
    fn, first, rows, vecs, outs, accs = epilogue[:6]
    rows = [first] + list(rows)
    tm = _pick(t, epilogue[6] if len(epilogue) > 6 else ROW_TILES)
    n_row, n_vec, n_out = len(rows), len(vecs), len(outs)
    dot = _DOTS[kind]

    def body(*refs):
        a_ref, w_ref, prod = refs[0], refs[1], refs[-1]
        row_refs = refs[2:2 + n_row]
        vec_refs = refs[2 + n_row:2 + n_row + n_vec]
        out_refs = refs[2 + n_row + n_vec:2 + n_row + n_vec + n_out]
        acc_refs = refs[2 + n_row + n_vec + n_out:-1]
        i, j = pl.program_id(0), pl.program_id(1)

        @pl.when(j == 0)
        def _():
            prod[...] = jnp.zeros_like(prod)

        prod[...] += dot(a_ref[...], w_ref[...])

        @pl.when(j == s - 1)
        def _():
            res = fn(row_refs[0], _AsRef(prod[...]), *row_refs[1:], *vec_refs, *out_refs)
            if acc_refs:
                @pl.when(i == 0)
                def _():
                    for r in acc_refs:
                        r[...] = jnp.zeros_like(r)

                for r, val in zip(acc_refs, res):
                    r[...] += val

    in_specs = [_BS((None, tm, n), lambda i, j: (j, i, 0)), _BS((None,) + w.shape[1:], lambda i, j: (j, 0, 0))]
    in_specs += [_BS((tm, r.shape[1]), lambda i, j: (i, 0)) for r in rows]
    in_specs += [_BS(v.shape, lambda i, j: (0, 0)) for v in vecs]
    out_specs = [_BS((tm, wd), lambda i, j: (i, 0)) for wd, _ in outs] + [_BS(sh, lambda i, j: (0, 0)) for sh in accs]
    out_shape = [jax.ShapeDtypeStruct((t, wd), dt) for wd, dt in outs] + [jax.ShapeDtypeStruct(sh, F32) for sh in accs]
    return pl.pallas_call(
        body, name=name, grid=(t // tm, s), in_specs=in_specs, out_specs=out_specs, out_shape=out_shape,
        scratch_shapes=[pltpu.VMEM((tm, nout), F32)],
        compiler_params=_params(("arbitrary" if accs else "parallel", "arbitrary")),
    )(a, w, *rows, *vecs)


def _mm_stack_tn(a, b, name):
    grp = 4
    if a.ndim == 2:
        t, k = a.shape
        s, _, n = b.shape
        tt = _pick(t, (512, 256, 128))
        a_spec = _BS((tt, k), lambda j, r: (r, 0))
        b_spec, b_in = _BS((None, grp, tt, n), lambda j, r: (j, 0, r, 0)), b.reshape(s // grp, grp, t, n)
        a_in = a
    else:
        s, t, k = a.shape
        n = b.shape[1]
        tt = _pick(t, (512, 256, 128))
        a_spec, a_in = _BS((None, grp, tt, k), lambda j, r: (j, 0, r, 0)), a.reshape(s // grp, grp, t, k)
        b_spec, b_in = _BS((tt, n), lambda j, r: (r, 0)), b

    def body(a_ref, b_ref, o_ref, acc):
        def terms():
            shared = a_ref[...] if a.ndim == 2 else b_ref[...]
            return [_dot_tn(shared, b_ref[e]) if a.ndim == 2 else _dot_tn(a_ref[e], shared) for e in range(grp)]

        _accumulate(acc, o_ref, 1, t // tt, terms)

    return pl.pallas_call(
        body, name=name, grid=(s // grp, t // tt), in_specs=[a_spec, b_spec],
        out_specs=_BS((None, grp, k, n), lambda j, r: (j, 0, 0, 0)),
        out_shape=jax.ShapeDtypeStruct((s // grp, grp, k, n), GRAD_DT), scratch_shapes=[pltpu.VMEM((grp, k, n), F32)],
        compiler_params=_params(("parallel", "arbitrary")),
    )(a_in, b_in).reshape(s, k, n)


def _cols_fwd(x, w, name):
    t, k = x.shape
    s, _, n = w.shape
    tm = _pick(t, ROW_TILES)

    def body(x_ref, w_ref, o_ref):
        x_ = x_ref[...]
        for j in range(s):
            o_ref[:, n * j:n * (j + 1)] = _dot(x_, w_ref[j])

    return pl.pallas_call(
        body, name=name, grid=(t // tm,),
        in_specs=[_BS((tm, k), lambda i: (i, 0)), _BS((s, k, n), lambda i: (0, 0, 0))],
        out_specs=_BS((tm, s * n), lambda i: (i, 0)), out_shape=jax.ShapeDtypeStruct((t, s * n), F32),
        compiler_params=_params(("parallel",)),
    )(x, w)


def _cols_dx(d, w, name):
    t = d.shape[0]
    s, k, n = w.shape
    tm = _pick(t, ROW_TILES)

    def body(d_ref, w_ref, o_ref):
        acc = _dot_nt(d_ref[:, 0:n], w_ref[0])
        for j in range(1, s):
            acc = acc + _dot_nt(d_ref[:, n * j:n * (j + 1)], w_ref[j])
        o_ref[...] = acc

    return pl.pallas_call(
        body, name=name, grid=(t // tm,),
        in_specs=[_BS((tm, s * n), lambda i: (i, 0)), _BS((s, k, n), lambda i: (0, 0, 0))],
        out_specs=_BS((tm, k), lambda i: (i, 0)), out_shape=jax.ShapeDtypeStruct((t, k), F32),
        compiler_params=_params(("parallel",)),
    )(d, w)


def _cols_dw(x, d, n, name):
    t, k = x.shape
    s = d.shape[1] // n
    tt = _pick(t, (512, 256, 128))

    def body(x_ref, d_ref, o_ref, acc):
        def terms():
            x_ = x_ref[...]
            return [_dot_tn(x_, d_ref[:, n * j:n * (j + 1)]) for j in range(s)]

        _accumulate(acc, o_ref, 0, t // tt, terms)

    return pl.pallas_call(
        body, name=name, grid=(t // tt,),
        in_specs=[_BS((tt, k), lambda r: (r, 0)), _BS((tt, s * n), lambda r: (r, 0))],
        out_specs=_BS((s, k, n), lambda r: (0, 0, 0)), out_shape=jax.ShapeDtypeStruct((s, k, n), GRAD_DT),
        scratch_shapes=[pltpu.VMEM((s, k, n), F32)], compiler_params=_params(("arbitrary",)),
    )(x, d)


def _rows(fn, name, t, tm, ins, vecs, outs, accs=()):
    n_in, n_out, n_acc = len(ins) + len(vecs), len(outs), len(accs)

    def body(*refs):
        res = fn(*refs[:n_in + n_out])
        if n_acc:
            acc_refs = refs[n_in + n_out:]

            @pl.when(pl.program_id(0) == 0)
            def _():
                for r in acc_refs:
                    r[...] = jnp.zeros_like(r)

            for r, val in zip(acc_refs, res):
                r[...] += val

    in_specs = [pl.BlockSpec((tm, bw), functools.partial(lambda i, cb: (i, cb), cb=cb)) for _, bw, cb in ins]
    in_specs += [pl.BlockSpec(v.shape, lambda i: (0, 0)) for v in vecs]
    out_specs = [pl.BlockSpec((tm, w), lambda i: (i, 0)) for w, _ in outs]
    out_specs += [pl.BlockSpec(s, lambda i: (0, 0)) for s in accs]
    out_shape = [jax.ShapeDtypeStruct((t, w), dt) for w, dt in outs]
    out_shape += [jax.ShapeDtypeStruct(s, F32) for s in accs]
    return pl.pallas_call(
        body, name=name, grid=(t // tm,), in_specs=in_specs, out_specs=out_specs, out_shape=out_shape,
        compiler_params=_params(("arbitrary",) if n_acc else ("parallel",)),
    )(*[a for a, _, _ in ins], *vecs)


def _rms(x, g):
    return x * lax.rsqrt(jnp.mean(x * x, axis=-1, keepdims=True) + EPS) * g


def _rms_bwd(x, g, dy):
    xh = x * lax.rsqrt(jnp.mean(x * x, axis=-1, keepdims=True) + EPS)
    r = lax.rsqrt(jnp.mean(x * x, axis=-1, keepdims=True) + EPS)
    dyg = dy * g
    dx = r * (dyg - xh * jnp.mean(dyg * xh, axis=-1, keepdims=True))
    return dx, jnp.sum(dy * xh, axis=0, keepdims=True)


def _hgrn_mats():
    row = lax.broadcasted_iota(jnp.int32, (HB, HB), 0)
    col = lax.broadcasted_iota(jnp.int32, (HB, HB), 1)
    return row, col


def _hgrn_gates(qr, z, t0, t1):
    lb = 1.0 / (1.0 + jnp.exp(t1 - t0))
    th = jnp.tanh(0.5 * z)
    sz, sneg = 0.5 + 0.5 * th, 0.5 - 0.5 * th
    f = lb + (1.0 - lb) * sz
    return lb, sz, sneg, f, jnp.log(f), (1.0 - lb) * sneg, qr * _sig(qr)


def _hgrn_scores(q, k, cum):
    heads, nsub = range(len(q)), HB // SUB
    mids = [[cum[h][SUB * i + SUB // 2 - 1:SUB * i + SUB // 2, :] for i in range(nsub)] for h in heads]
    cmid = [jnp.concatenate([cum[h][SUB * i:SUB * (i + 1)] - mids[h][i] for i in range(nsub)], axis=0) for h in heads]
    qd = [q[h] * jnp.exp(jnp.minimum(cmid[h], EXP_CLAMP)) for h in heads]
    qd_b = [x.astype(BF16) for x in qd]
    kds = [[None] * nsub for _ in heads]
    parts = [[None] * nsub for _ in heads]
    for i in range(nsub):
        for h in heads:
            kds[h][i] = k[h] * jnp.exp(jnp.minimum(mids[h][i] - cum[h], EXP_CLAMP))
        for h in heads:
            parts[h][i] = _dot_nt(qd_b[h][SUB * i:SUB * (i + 1)], kds[h][i])
    return qd, kds, [jnp.concatenate(parts[h], axis=0) for h in heads], cmid


def _hgrn_fwd(p_hg, table, gain, t):
    nblk = t // HB

    def body(q_ref, f_ref, i_ref, g_ref, tab_ref, gain_ref, o_ref, y_ref, st_ref, state):
        @pl.when(pl.program_id(0) == 0)
        def _():
            state[...] = jnp.zeros_like(state)

        row, col = _hgrn_mats()
        causal = col <= row
        tri = causal.astype(BF16)
        heads = range(NH)
        sl = [slice(HD * h, HD * (h + 1)) for h in heads]
        gates = [_hgrn_gates(q_ref[:, sl[h]], f_ref[:, sl[h]], tab_ref[0:1, sl[h]], tab_ref[1:2, sl[h]]) for h in heads]
        lf, k, q = [g[4] for g in gates], [g[5] for g in gates], [g[6] for g in gates]
        v = [i_ref[:, sl[h]] for h in heads]
        cum = [_dot_sel(tri, lf[h]) for h in heads]
        _, _, s, _ = _hgrn_scores(q, k, cum)
        p = [jnp.where(causal, s[h], 0.0) for h in heads]
        st = [state[h] for h in heads]
        o = [_dot(p[h], v[h]) + _dot_nt(q[h] * jnp.exp(cum[h]), st[h]) for h in heads]
        last = [cum[h][HB - 1:HB, :] for h in heads]
        new_st = [st[h] * jnp.exp(last[h]) + _dot_tn(v[h], k[h] * jnp.exp(last[h] - cum[h])) for h in heads]
        for h in heads:
            gr = g_ref[:, sl[h]]
            st_ref[h, 0] = st[h]
            state[h] = new_st[h]
            o_ref[:, sl[h]] = o[h]
            y_ref[:, sl[h]] = (_rms(o[h], gain_ref[...]) * gr * _sig(gr)).astype(BF16)

    blk = lambda cb: pl.BlockSpec((HB, HGW), functools.partial(lambda n, cb: (n, cb), cb=cb))
    return pl.pallas_call(
        body, name="hgrn_fwd", grid=(nblk,),
        in_specs=[blk(0), blk(1), blk(2), blk(3), pl.BlockSpec((2, HGW), lambda n: (0, 0)),
                  pl.BlockSpec((1, HD), lambda n: (0, 0))],
        out_specs=[pl.BlockSpec((HB, HGW), lambda n: (n, 0)), pl.BlockSpec((HB, HGW), lambda n: (n, 0)),
                   pl.BlockSpec((NH, 1, HD, HD), lambda n: (0, n, 0, 0))],
        out_shape=[jax.ShapeDtypeStruct((t, HGW), F32), jax.ShapeDtypeStruct((t, HGW), BF16),
                   jax.ShapeDtypeStruct((NH, nblk, HD, HD), F32)],
        scratch_shapes=[pltpu.VMEM((NH, HD, HD), F32)],
        compiler_params=_params(("arbitrary",)),
    )(p_hg, p_hg, p_hg, p_hg, table, gain)


def _hgrn_bwd(p_hg, table, gain, o_pre, states, dy, t):
    nblk = t // HB

    def body(q_ref, f_ref, i_ref, g_ref, tab_ref, gain_ref, o_ref, st_ref, dy_ref, dp_ref, dtab_ref, dgain_ref,
             dstate):
        @pl.when(pl.program_id(0) == 0)
        def _():
            dstate[...] = jnp.zeros_like(dstate)
            dtab_ref[...] = jnp.zeros_like(dtab_ref)
            dgain_ref[...] = jnp.zeros_like(dgain_ref)

        row, col = _hgrn_mats()
        causal = col <= row
        tri = causal.astype(BF16)
        tri_t = (row <= col).astype(BF16)
        heads, nsub = range(NH), HB // SUB
        sl = [slice(HD * h, HD * (h + 1)) for h in heads]
        rows_of = lambda i: slice(SUB * i, SUB * (i + 1))
        gain_ = gain_ref[...]
        qr, z = [q_ref[:, sl[h]] for h in heads], [f_ref[:, sl[h]] for h in heads]
        v, gr = [i_ref[:, sl[h]] for h in heads], [g_ref[:, sl[h]] for h in heads]
        gates = [_hgrn_gates(qr[h], z[h], tab_ref[0:1, sl[h]], tab_ref[1:2, sl[h]]) for h in heads]
        lb, sz, sneg, f, lf, k, q = ([g[j] for g in gates] for j in range(7))
        cum = [_dot_sel(tri, lf[h]) for h in heads]
        qd, kds, s, cmid = _hgrn_scores(q, k, cum)
        p = [jnp.where(causal, s[h], 0.0) for h in heads]
        st, dst = [st_ref[h, 0] for h in heads], [dstate[h] for h in heads]
        o, dyh, sg = [o_ref[:, sl[h]] for h in heads], [dy_ref[:, sl[h]] for h in heads], [_sig(x) for x in gr]
        dgr = [dyh[h] * _rms(o[h], gain_) * sg[h] * (1.0 + gr[h] * (1.0 - sg[h])) for h in heads]
        norm_bwd = [_rms_bwd(o[h], gain_, dyh[h] * gr[h] * sg[h]) for h in heads]
        do = [x[0] for x in norm_bwd]
        do_b = [x.astype(BF16) for x in do]
        ecum, last = [jnp.exp(x) for x in cum], [x[HB - 1:HB, :] for x in cum]
        qc = [q[h] * ecum[h] for h in heads]
        edec = [jnp.exp(last[h] - cum[h]) for h in heads]
        kdec = [k[h] * edec[h] for h in heads]
        dp = [jnp.where(causal, _dot_nt(do_b[h], v[h]), 0.0) for h in heads]
        dv = [_dot(p[h].T, do_b[h]) + _dot_nt(kdec[h], dst[h]) for h in heads]
        dqc = [_dot(do_b[h], st[h]) for h in heads]
        dkdec = [_dot(v[h], dst[h]) for h in heads]
        new_dst = [dst[h] * jnp.exp(last[h]) + _dot(do[h].T, qc[h]) for h in heads]
        dp_b = [x.astype(BF16) for x in dp]
        dqd = [jnp.concatenate(parts, axis=0) for parts in
               zip(*[[_dot(dp_b[h][rows_of(i)], kds[h][i]) for h in heads] for i in range(nsub)])]
        gq = [dqd[h] * qd[h] for h in heads]
        dq = [dqd[h] * jnp.exp(jnp.minimum(cmid[h], EXP_CLAMP)) + dqc[h] * ecum[h] for h in heads]
        gs = [dkdec[h] * kdec[h] for h in heads]
        dk = [dkdec[h] * edec[h] for h in heads]
        dcum = [dqc[h] * qc[h] - gs[h] + gq[h]
                + jnp.where(row == HB - 1, jnp.sum(gs[h], axis=0, keepdims=True)
                            + jnp.exp(last[h]) * jnp.sum(st[h] * dst[h], axis=0, keepdims=True), 0.0) for h in heads]
        qd_b = [x.astype(BF16) for x in qd]
        for i in range(nsub):
            dkd = [_dot_tn(dp_b[h][rows_of(i)], qd_b[h][rows_of(i)]) for h in heads]
            for h in heads:
                mid = cum[h][SUB * i + SUB // 2 - 1:SUB * i + SUB // 2, :]
                dk[h] = dk[h] + dkd[h] * jnp.exp(jnp.minimum(mid - cum[h], EXP_CLAMP))
                gk = dkd[h] * kds[h][i]
                to_mid = jnp.sum(gk, axis=0, keepdims=True) - jnp.sum(gq[h][rows_of(i)], axis=0, keepdims=True)
                dcum[h] = dcum[h] - gk + jnp.where(row == SUB * i + SUB // 2 - 1, to_mid, 0.0)
        dlf = [_dot_sel(tri_t, dcum[h]) for h in heads]
        dgain = jnp.zeros((1, HD), F32)
        for h in heads:
            df = dlf[h] / f[h] - dk[h]
            dlb = jnp.sum(df * sneg[h], axis=0, keepdims=True) * lb[h] * (1.0 - lb[h])
            dtab_ref[0:1, sl[h]] += dlb
            dtab_ref[1:2, sl[h]] -= dlb
            sq = _sig(qr[h])
            dstate[h] = new_dst[h]
            dp_ref[:, sl[h]] = (dq[h] * sq * (1.0 + qr[h] * (1.0 - sq))).astype(BF16)
            dp_ref[:, HGW + HD * h:HGW + HD * (h + 1)] = (df * (1.0 - lb[h]) * sz[h] * sneg[h]).astype(BF16)
            dp_ref[:, 2 * HGW + HD * h:2 * HGW + HD * (h + 1)] = dv[h].astype(BF16)
            dp_ref[:, 3 * HGW + HD * h:3 * HGW + HD * (h + 1)] = dgr[h].astype(BF16)
            dgain = dgain + norm_bwd[h][1]
        dgain_ref[...] += dgain

    rev = lambda cb: pl.BlockSpec((HB, HGW), functools.partial(lambda n, cb: (nblk - 1 - n, cb), cb=cb))
    return pl.pallas_call(
        body, name="hgrn_bwd", grid=(nblk,),
        in_specs=[rev(0), rev(1), rev(2), rev(3), pl.BlockSpec((2, HGW), lambda n: (0, 0)),
                  pl.BlockSpec((1, HD), lambda n: (0, 0)), rev(0),
                  pl.BlockSpec((NH, 1, HD, HD), lambda n: (0, nblk - 1 - n, 0, 0)), rev(0)],
        out_specs=[pl.BlockSpec((HB, 4 * HGW), lambda n: (nblk - 1 - n, 0)),
                   pl.BlockSpec((2, HGW), lambda n: (0, 0)), pl.BlockSpec((1, HD), lambda n: (0, 0))],
        out_shape=[jax.ShapeDtypeStruct((t, 4 * HGW), BF16), jax.ShapeDtypeStruct((2, HGW), F32),
                   jax.ShapeDtypeStruct((1, HD), F32)],
        scratch_shapes=[pltpu.VMEM((NH, HD, HD), F32)],
        compiler_params=_params(("arbitrary",)),
    )(p_hg, p_hg, p_hg, p_hg, table, gain, o_pre, states, dy)


def _rope_mat():
    r = lax.broadcasted_iota(jnp.int32, (ROPE, ROPE), 0)
    c = lax.broadcasted_iota(jnp.int32, (ROPE, ROPE), 1)
    half = ROPE // 2
    return ((r == c - half).astype(F32) - (r == c + half).astype(F32)).astype(BF16)


def _mla_prep_fwd(p_mla, cs, sn, wq, wkv, gql, gkvl, gq, gk, t):
    tm = _pick(t, (512, 256, 128))

    def body(p_ref, cs_ref, sn_ref, wq_ref, wkv_ref, gql_ref, gkvl_ref, gq_ref, gk_ref,
             q_ref, k_ref, v_ref):
        rmat = _rope_mat()
        cqn = _rms(p_ref[:, 0:QL], gql_ref[...]).astype(BF16)
        ckvn = _rms(p_ref[:, QL:QL + KVL], gkvl_ref[...]).astype(BF16)
        kpe = p_ref[:, QL + KVL:QL + KVL + ROPE]
        c, s = cs_ref[...], sn_ref[...]
        rot = lambda x: x * c + _sel_dot(x, rmat) * s
        heads = range(NH)
        ssq = lambda x: jnp.sum(x * x, -1, keepdims=True)
        qa = [_dot(cqn, wq_ref[h, :, 0:HD]) for h in heads]
        qr = [_dot(cqn, wq_ref[h, :, HD:QK]) for h in heads]
        kn = [_dot(ckvn, wkv_ref[h, :, 0:HD]) for h in heads]
        vv = [_dot(ckvn, wkv_ref[h, :, HD:2 * HD]) for h in heads]
        kpe_ss = ssq(kpe)
        rq = [lax.rsqrt((ssq(qa[h]) + ssq(qr[h])) / QK + EPS) for h in heads]
        rk = [lax.rsqrt((ssq(kn[h]) + kpe_ss) / QK + EPS) for h in heads]
        q_rope = [rot(qr[h] * rq[h] * gq_ref[:, HD:QK]) for h in heads]
        k_rope = [rot(kpe * rk[h] * gk_ref[:, HD:QK]) for h in heads]
        for h in heads:
            q_ref[h, :, 0:HD] = (qa[h] * rq[h] * gq_ref[:, 0:HD] * (SCALE * LOG2E)).astype(BF16)
            q_ref[h, :, HD:QK] = (q_rope[h] * (SCALE * LOG2E)).astype(BF16)
            k_ref[h, :, 0:HD] = (kn[h] * rk[h] * gk_ref[:, 0:HD]).astype(BF16)
            k_ref[h, :, HD:QK] = k_rope[h].astype(BF16)
            v_ref[h] = vv[h].astype(BF16)

    whole = lambda a: pl.BlockSpec(a.shape, functools.partial(lambda i, nd: (0,) * nd, nd=a.ndim))
    return pl.pallas_call(
        body, name="mla_prep_fwd", grid=(t // tm,),
        in_specs=[pl.BlockSpec((tm, QL + KVL + ROPE), lambda i: (i, 0)), pl.BlockSpec((tm, ROPE), lambda i: (i, 0)),
                  pl.BlockSpec((tm, ROPE), lambda i: (i, 0))] + [whole(a) for a in (wq, wkv, gql, gkvl, gq, gk)],
        out_specs=[pl.BlockSpec((NH, tm, QK), lambda i: (0, i, 0)), pl.BlockSpec((NH, tm, QK), lambda i: (0, i, 0)),
                   pl.BlockSpec((NH, tm, HD), lambda i: (0, i, 0))],
        out_shape=[jax.ShapeDtypeStruct((NH, t, QK), BF16), jax.ShapeDtypeStruct((NH, t, QK), BF16),
                   jax.ShapeDtypeStruct((NH, t, HD), BF16)],
        compiler_params=_params(("parallel",)),
    )(p_mla, cs, sn, wq, wkv, gql, gkvl, gq, gk)


def _mla_prep_bwd(p_mla, cs, sn, wq, wkv, gql, gkvl, gq, gk, dq, dk, dv, t):
    tm = _pick(t, (512, 256, 128))

    def body(p_ref, cs_ref, sn_ref, wq_ref, wkv_ref, gql_ref, gkvl_ref, gq_ref, gk_ref,
             dq_ref, dk_ref, dv_ref,
             dp_ref, dwq_ref, dwkv_ref, dgql_ref, dgkvl_ref, dgq_ref, dgk_ref):
        accs = (dwq_ref, dwkv_ref, dgql_ref, dgkvl_ref, dgq_ref, dgk_ref)

        @pl.when(pl.program_id(0) == 0)
        def _():
            for r in accs:
                r[...] = jnp.zeros_like(r)

        rmat = _rope_mat()
        rmat_t = -rmat
        cq, ckv = p_ref[:, 0:QL], p_ref[:, QL:QL + KVL]
        kpe = p_ref[:, QL + KVL:QL + KVL + ROPE]
        cqn_f, ckvn_f = _rms(cq, gql_ref[...]), _rms(ckv, gkvl_ref[...])
        cqn, ckvn = cqn_f.astype(BF16), ckvn_f.astype(BF16)
        ckvn_t = ckvn_f.T.astype(BF16)
        c, s = cs_ref[...], sn_ref[...]
        unrot = lambda dy: dy * c + _sel_dot(dy * s, rmat_t)
        heads = range(NH)
        rsum = lambda x: jnp.sum(x, -1, keepdims=True)
        csum = lambda x: jnp.sum(x, axis=0, keepdims=True)
        qa = [_dot(cqn, wq_ref[h, :, 0:HD]) for h in heads]
        qr = [_dot(cqn, wq_ref[h, :, HD:QK]) for h in heads]
        kn = [_dot(ckvn, wkv_ref[h, :, 0:HD]) for h in heads]
        dyr = [unrot(dq_ref[h, :, HD:QK] * SCALE) for h in heads]
        dkr = [unrot(dk_ref[h, :, HD:QK]) for h in heads]
        rq = [lax.rsqrt((rsum(qa[h] * qa[h]) + rsum(qr[h] * qr[h])) / QK + EPS) for h in heads]
        xa, xr = [qa[h] * rq[h] for h in heads], [qr[h] * rq[h] for h in heads]
        dya = [dq_ref[h, :, 0:HD] * SCALE for h in heads]
        ga, gr_ = [dya[h] * gq_ref[:, 0:HD] for h in heads], [dyr[h] * gq_ref[:, HD:QK] for h in heads]
        mean = [(rsum(ga[h] * xa[h]) + rsum(gr_[h] * xr[h])) / QK for h in heads]
        dqa = [(rq[h] * (ga[h] - xa[h] * mean[h])).astype(BF16) for h in heads]
        dqr = [(rq[h] * (gr_[h] - xr[h] * mean[h])).astype(BF16) for h in heads]
        kpe_ss = rsum(kpe * kpe)
        rk = [lax.rsqrt((rsum(kn[h] * kn[h]) + kpe_ss) / QK + EPS) for h in heads]
        ya, yr = [kn[h] * rk[h] for h in heads], [kpe * rk[h] for h in heads]
        dka = [dk_ref[h, :, 0:HD] for h in heads]
        ha, hr = [dka[h] * gk_ref[:, 0:HD] for h in heads], [dkr[h] * gk_ref[:, HD:QK] for h in heads]
        mean_k = [(rsum(ha[h] * ya[h]) + rsum(hr[h] * yr[h])) / QK for h in heads]
        dkn = [(rk[h] * (ha[h] - ya[h] * mean_k[h])).astype(BF16) for h in heads]
        dvh = [dv_ref[h].astype(BF16) for h in heads]
        dw = [(_dot_tn(dqa[h], cqn), _dot_tn(dqr[h], cqn), _dot(ckvn_t, dkn[h]), _dot(ckvn_t, dvh[h])) for h in heads]
        back_q = [_dot_nt(dqa[h], wq_ref[h, :, 0:HD]) + _dot_nt(dqr[h], wq_ref[h, :, HD:QK]) for h in heads]
        back_kv = [_dot_nt(dkn[h], wkv_ref[h, :, 0:HD]) + _dot_nt(dvh[h], wkv_ref[h, :, HD:2 * HD]) for h in heads]
        dcqn, dckvn = sum(back_q[1:], back_q[0]), sum(back_kv[1:], back_kv[0])
        dkpe = sum([rk[h] * (hr[h] - yr[h] * mean_k[h]) for h in heads][1:], rk[0] * (hr[0] - yr[0] * mean_k[0]))
        dgq_a = sum([csum(dya[h] * xa[h]) for h in heads][1:], csum(dya[0] * xa[0]))
        dgq_r = sum([csum(dyr[h] * xr[h]) for h in heads][1:], csum(dyr[0] * xr[0]))
        dgk_a = sum([csum(dka[h] * ya[h]) for h in heads][1:], csum(dka[0] * ya[0]))
        dgk_r = sum([csum(dkr[h] * yr[h]) for h in heads][1:], csum(dkr[0] * yr[0]))
        for h in heads:
            dwq_ref[h, 0:HD, :] += dw[h][0]
            dwq_ref[h, HD:QK, :] += dw[h][1]
            dwkv_ref[h, :, 0:HD] += dw[h][2]
            dwkv_ref[h, :, HD:2 * HD] += dw[h][3]
        dcq, dg1 = _rms_bwd(cq, gql_ref[...], dcqn)
        dckv, dg2 = _rms_bwd(ckv, gkvl_ref[...], dckvn)
        dp_ref[:, 0:QL] = dcq.astype(BF16)
        dp_ref[:, QL:QL + KVL] = dckv.astype(BF16)
        dp_ref[:, QL + KVL:QL + KVL + ROPE] = dkpe.astype(BF16)
        dgql_ref[...] += dg1
        dgkvl_ref[...] += dg2
        dgq_ref[:, 0:HD] += dgq_a
        dgq_ref[:, HD:QK] += dgq_r
        dgk_ref[:, 0:HD] += dgk_a
        dgk_ref[:, HD:QK] += dgk_r

    whole = lambda a: pl.BlockSpec(a.shape, functools.partial(lambda i, nd: (0,) * nd, nd=a.ndim))
    acc_shapes = [(NH, QK, QL), wkv.shape, gql.shape, gkvl.shape, gq.shape, gk.shape]
    return pl.pallas_call(
        body, name="mla_prep_bwd", grid=(t // tm,),
        in_specs=[pl.BlockSpec((tm, QL + KVL + ROPE), lambda i: (i, 0)), pl.BlockSpec((tm, ROPE), lambda i: (i, 0)),
                  pl.BlockSpec((tm, ROPE), lambda i: (i, 0))]
        + [whole(a) for a in (wq, wkv, gql, gkvl, gq, gk)]
        + [pl.BlockSpec((NH, tm, QK), lambda i: (0, i, 0)), pl.BlockSpec((NH, tm, QK), lambda i: (0, i, 0)),
           pl.BlockSpec((NH, tm, HD), lambda i: (0, i, 0))],
        out_specs=[pl.BlockSpec((tm, QL + KVL + ROPE), lambda i: (i, 0))]
        + [pl.BlockSpec(s, functools.partial(lambda i, nd: (0,) * nd, nd=len(s))) for s in acc_shapes],
        out_shape=[jax.ShapeDtypeStruct((t, QL + KVL + ROPE), BF16)]
        + [jax.ShapeDtypeStruct(s, F32) for s in acc_shapes],
        compiler_params=_params(("arbitrary",)),
    )(p_mla, cs, sn, wq, wkv, gql, gkvl, gq, gk, dq, dk, dv)


def _chunk_mask(nq, nk, key0, keys_on_rows):
    shape = (nk, nq) if keys_on_rows else (nq, nk)
    qi = lax.broadcasted_iota(jnp.int32, shape, 1 if keys_on_rows else 0) // CHUNK
    ki = lax.broadcasted_iota(jnp.int32, shape, 0 if keys_on_rows else 1) // CHUNK + key0 // CHUNK
    return ki <= qi


def _flash_fwd(q, k, v, t):
    tq = _pick(t, ATT_Q)
    tk = tq // ATT_KEY_TILES

    def body(q_ref, k_ref, v_ref, o_ref, lse_ref):
        i = pl.program_id(1)
        qt = q_ref[0]

        def step(j, carry, key0):
            m, l, acc = carry
            cols = pl.ds(pl.multiple_of(j * tk, tk), tk)
            s = _dot_nt(qt, k_ref[0, cols, :])
            if key0 is not None:
                s = jnp.where(_chunk_mask(tq, tk, key0, False), s, -jnp.inf)
            m_new = jnp.maximum(m, jnp.max(s, axis=-1, keepdims=True))
            p = jnp.exp2(s - m_new)
            alpha = jnp.exp2(m - m_new)
            return m_new, alpha * l + jnp.sum(p, axis=-1, keepdims=True), alpha * acc + _dot(p, v_ref[0, cols, :])

        init = (jnp.full((tq, 1), -jnp.inf, F32), jnp.zeros((tq, 1), F32), jnp.zeros((tq, HD), F32))
        carry = lax.fori_loop(0, ATT_KEY_TILES * i, lambda j, cr: step(j, cr, None), init)
        for h in range(ATT_KEY_TILES):
            carry = step(ATT_KEY_TILES * i + h, carry, h * tk)
        m, l, acc = carry
        o_ref[...] = (acc / l).astype(BF16)
        x1, x2, x3 = _split3(jnp.broadcast_to((m + jnp.log2(l)) * (1.0 / HD), (tq, HD)))
        ones = jnp.ones((SUBLANE, HD), BF16)
        rowsum = lambda p: lax.dot_general(ones, p, (((1,), (1,)), ((), ())), preferred_element_type=F32)
        lse_ref[0] = (rowsum(x1) + rowsum(x2) + rowsum(x3))[0:1, :]

    return pl.pallas_call(
        body, name="flash_fwd", grid=(NH, t // tq),
        in_specs=[pl.BlockSpec((1, tq, QK), lambda h, i: (h, i, 0)), pl.BlockSpec((1, t, QK), lambda h, i: (h, 0, 0)),
                  pl.BlockSpec((1, t, HD), lambda h, i: (h, 0, 0))],
        out_specs=[pl.BlockSpec((tq, HD), lambda h, i: (i, h)), pl.BlockSpec((1, 1, tq), lambda h, i: (h, 0, i))],
        out_shape=[jax.ShapeDtypeStruct((t, NH * HD), BF16), jax.ShapeDtypeStruct((NH, 1, t), F32)],
        compiler_params=_params(("parallel", "parallel")),
    )(q, k, v)


def _attn_out_bwd(dy, w, o, t):
    tm = _pick(t, ATT_TILES)

    def body(dy_ref, w_ref, o_ref, do_ref, delta_ref):
        do = _dot_nt(dy_ref[...], w_ref[...]).astype(BF16)
        do_ref[...] = do
        ones = jnp.ones((SUBLANE, HD), BF16)
        rowsum = lambda p: lax.dot_general(ones, p, (((1,), (1,)), ((), ())), preferred_element_type=F32)
        sl = [slice(HD * h, HD * (h + 1)) for h in range(NH)]
        parts = [_split3(do[:, sl[h]].astype(F32) * o_ref[:, sl[h]].astype(F32)) for h in range(NH)]
        sums = [[rowsum(x) for x in parts[h]] for h in range(NH)]
        for h in range(NH):
            delta_ref[h] = (sums[h][0] + sums[h][1] + sums[h][2])[0:1, :]

    return pl.pallas_call(
        body, name="attn_out_bwd", grid=(t // tm,),
        in_specs=[_BS((tm, dy.shape[1]), lambda i: (i, 0)), _BS(w.shape, lambda i: (0, 0)),
                  _BS((tm, NH * HD), lambda i: (i, 0))],
        out_specs=[_BS((tm, NH * HD), lambda i: (i, 0)), _BS((NH, 1, tm), lambda i: (0, 0, i))],
        out_shape=[jax.ShapeDtypeStruct((t, NH * HD), BF16), jax.ShapeDtypeStruct((NH, 1, t), F32)],
        compiler_params=_params(("parallel",)),
    )(dy, w, o)


def _flash_bwd(q, k, v, do, lse_row, delta_row, t):
    tq = _pick(t, ATT_Q)
    tk = tq // ATT_KEY_TILES

    def body(q_ref, k_ref, v_ref, do_ref, lse_ref, delta_ref, dq_ref, dk_ref, dv_ref):
        j = pl.program_id(1)

        @pl.when(j == 0)
        def _():
            dq_ref[...] = jnp.zeros_like(dq_ref)

        kt, vt = k_ref[0], v_ref[0]

        def step(i, carry, key0):
            dk, dv = carry
            rows = pl.ds(pl.multiple_of(i * tq, tq), tq)
            qt, dot_ = q_ref[0, rows, :], do_ref[rows, :]
            p = jnp.exp2(_dot_nt(kt, qt) - lse_ref[0, :, rows])
            if key0 is not None:
                p = jnp.where(_chunk_mask(tq, tk, key0, True), p, 0.0)
            ds = (p * (_dot_nt(vt, dot_) - delta_ref[0, :, rows])).astype(BF16)
            dq_ref[0, rows, :] += _dot_tn(ds, kt)
            return dk + _dot(ds, qt), dv + _dot(p, dot_)

        own = j // ATT_KEY_TILES
        carry = step(own, (jnp.zeros((tk, QK), F32), jnp.zeros((tk, HD), F32)), (j % ATT_KEY_TILES) * tk)
        dk, dv = lax.fori_loop(own + 1, t // tq, lambda i, cr: step(i, cr, None), carry)
        dk_ref[0] = dk * (1.0 / LOG2E)
        dv_ref[0] = dv

    return pl.pallas_call(
        body, name="flash_bwd", grid=(NH, t // tk),
        in_specs=[pl.BlockSpec((1, t, QK), lambda h, j: (h, 0, 0)), pl.BlockSpec((1, tk, QK), lambda h, j: (h, j, 0)),
                  pl.BlockSpec((1, tk, HD), lambda h, j: (h, j, 0)), pl.BlockSpec((t, HD), lambda h, j: (0, h)),
                  pl.BlockSpec((1, 1, t), lambda h, j: (h, 0, 0)), pl.BlockSpec((1, 1, t), lambda h, j: (h, 0, 0))],
        out_specs=[pl.BlockSpec((1, t, QK), lambda h, j: (h, 0, 0)), pl.BlockSpec((1, tk, QK), lambda h, j: (h, j, 0)),
                   pl.BlockSpec((1, tk, HD), lambda h, j: (h, j, 0))],
        out_shape=[jax.ShapeDtypeStruct((NH, t, QK), F32), jax.ShapeDtypeStruct((NH, t, QK), F32),
                   jax.ShapeDtypeStruct((NH, t, HD), F32)],
        compiler_params=_params(("parallel", "arbitrary")),
    )(q, k, v, do, lse_row, delta_row)


def _ffn_in(xn, w_in, name):
    t, k = xn.shape
    s, _, n = w_in.shape
    tm = _pick(t, ROW_TILES)

    def body(x_ref, w_ref, gu_ref, a_ref):
        x = x_ref[...]
        g, u = _dot(x, w_ref[0]), _dot(x, w_ref[1])
        gu_ref[0] = g.astype(BF16)
        gu_ref[1] = u.astype(BF16)
        a_ref[...] = (g * _sig(g) * u).astype(BF16)

    return pl.pallas_call(
        body, name=name, grid=(t // tm, s // 2),
        in_specs=[_BS((tm, k), lambda i, j: (i, 0)), _BS((2, None, k, n), lambda i, j: (0, j, 0, 0))],
        out_specs=[_BS((2, None, tm, n), lambda i, j: (0, j, i, 0)), _BS((None, tm, n), lambda i, j: (j, i, 0))],
        out_shape=[jax.ShapeDtypeStruct((2, s // 2, t, n), BF16), jax.ShapeDtypeStruct((s // 2, t, n), BF16)],
        compiler_params=_params(("parallel", "parallel")),
    )(xn, w_in.reshape(2, s // 2, k, n))


def _ffn_dgu(dfo, w_out, gu, name):
    t, k = dfo.shape
    s, n, _ = w_out.shape
    tm = _pick(t, ROW_TILES)

    def body(d_ref, w_ref, gu_ref, o_ref):
        da = _dot_nt(d_ref[...], w_ref[...])
        g, u = gu_ref[0].astype(F32), gu_ref[1].astype(F32)
        sg = _sig(g)
        o_ref[0] = (da * u * sg * (1.0 + g * (1.0 - sg))).astype(BF16)
        o_ref[1] = (da * g * sg).astype(BF16)

    pair = _BS((2, None, tm, n), lambda i, j: (0, j, i, 0))
    return pl.pallas_call(
        body, name=name, grid=(t // tm, s),
        in_specs=[_BS((tm, k), lambda i, j: (i, 0)), _BS((None, n, k), lambda i, j: (j, 0, 0)), pair],
        out_specs=pair, out_shape=jax.ShapeDtypeStruct((2, s, t, n), BF16),
        compiler_params=_params(("parallel", "parallel")),
    )(dfo, w_out, gu).reshape(2 * s, t, n)


def _ffn_fwd(xn, w_in, w_out, tag, epilogue):
    gu, a = _ffn_in(xn, w_in, tag + "_in")
    return gu, a, _mm_stack_red(a, w_out, tag + "_out", "nn", epilogue)


def _ffn_bwd(dfo, xn, gu, a, w_in, w_out, tag, epilogue):
    dw_out = _mm_stack_tn(a, dfo, tag + "_dwout")
    dgu = _ffn_dgu(dfo, w_out, gu, tag + "_dgu")
    dw_in = _mm_stack_tn(dgu, xn, tag + "_dwin")
    if callable(epilogue):
        epilogue = epilogue(dw_in, dw_out)
    return _mm_stack_red(dgu, w_in, tag + "_dxn", "nt", epilogue), dw_in, dw_out


def _local_step(x, target, cs, sn, w, late_weights, s, early_grads, last_grads):
    t = x.shape[0]
    tm = _pick(t, (256, 128))
    g = {}
    ffn_out = lambda n: w[n].reshape(4, FF // 4, D)

    def norm_fn(x_ref, g_ref, o_ref):
        o_ref[...] = _rms(x_ref[...], g_ref[...]).astype(BF16)

    xn1 = _rows(norm_fn, "norm1", t, tm, [(x, D, 0)], [s["ffn1_norm"]], [(D, BF16)])[0]
    def res_norm_fn(scale):
        def fn(h_ref, f_ref, g_ref, h_out, n_out):
            h = h_ref[...] + scale * f_ref[...]
            h_out[...] = h
            n_out[...] = _rms(h, g_ref[...]).astype(BF16)
        return fn

    gu1, a1, (h1, u) = _ffn_fwd(xn1, w["ffn1_w_in"], ffn_out("ffn1_w_out"), "ffn1",
                                (res_norm_fn(0.5), x, [], [s["mix_norm"]], [(D, F32), (D, BF16)], []))
    w = {**w, **late_weights(h1)}
    rows_of = lambda n: w[n].reshape(-1, w[n].shape[2])
    w_hgb, w_mlab, w_o = rows_of("w_hg_branch"), rows_of("w_mla_branch"), rows_of("w_out")
    w_in_nat = w["w_in"].transpose(1, 0, 2).reshape(D, -1)
    w_mrg = w["w_merge"]
    mw = w_mrg.shape[2]
    w_in_hg, w_in_mla = w_in_nat[:, :4 * HGW], w_in_nat[:, 4 * HGW:]
    p_hg = _mm(u, w_in_hg, "nn", "proj_hg")
    p_mla = _mm(u, w_in_mla, "nn", "proj_mla")
    gpre = _cols_fwd(u, w_mrg, "proj_gate")
    o_pre, hgy, states = _hgrn_fwd(p_hg, s["hg_lb_table"], s["hg_out_norm"], t)
    prep_args = (p_mla, cs, sn, w["w_q_up"], w["w_kv_up"], s["mla_q_lora_norm"], s["mla_kv_lora_norm"],
                 s["q_head_norm"], s["k_head_norm"])
    q, k, v = _mla_prep_fwd(*prep_args, t)
    att, lse = _flash_fwd(q, k, v, t)
    y_hg = _mm(hgy, w_hgb, "nn", "branch_hg")
    y_mla = _mm(att, w_mlab, "nn", "branch_mla")

    def mix_fn(gh_ref, gm_ref, yh_ref, ym_ref, b_ref, o_ref):
        gh = _sig(gh_ref[...] + b_ref[:, 0:D])
        gm = _sig(gm_ref[...] + b_ref[:, D:2 * D])
        o_ref[...] = (gh * yh_ref[...] + gm * ym_ref[...]).astype(BF16)

    mixed = _rows(mix_fn, "mix", t, tm, [(gpre, D, 0), (gpre, D, 1), (y_hg, D, 0), (y_mla, D, 0)], [s["b_merge"]],
                  [(D, BF16)])[0]
    h2, xn2 = _mm_stack_red(mixed[None], w_o[None], "mix_out", "nn",
                            (res_norm_fn(1.0), h1, [], [s["ffn2_norm"]], [(D, F32), (D, BF16)], []))

    def loss_fn(h_ref, f_ref, tg_ref, g_ref, dh_out, dhb_out):
        h = h_ref[...] + 0.5 * f_ref[...]
        e = _rms(h, g_ref[...]) - tg_ref[...]
        dh, dgain = _rms_bwd(h, g_ref[...], e / D)
        dh_out[...] = dh
        dhb_out[...] = (0.5 * dh).astype(BF16)
        return dgain, jnp.full((1, LANE), 0.5 / D * jnp.sum(e * e), F32)

    gu2, a2, (dh3, dfo2, g["final_norm"], loss) = _ffn_fwd(
        xn2, w["ffn2_w_in"], ffn_out("ffn2_w_out"), "ffn2",
        (loss_fn, h2, [target], [s["final_norm"]], [(D, F32), (D, BF16)], [(1, D), (1, LANE)]))

    def norm_bwd_fn(scale):
        def fn(h_ref, dxn_ref, dh_ref, g_ref, dh_out, dhb_out):
            dx, dgain = _rms_bwd(h_ref[...], g_ref[...], dxn_ref[...])
            dh = dh_ref[...] + dx
            dh_out[...] = dh
            dhb_out[...] = (scale * dh).astype(BF16)
            return (dgain,)
        return fn

    as_rows = lambda a: a.reshape((N_DEV, -1) + a.shape[-1:])
    (dh2, dh2b, g["ffn2_norm"]), g["ffn2_w_in"], dwo = _ffn_bwd(
        dfo2, xn2, gu2, a2, w["ffn2_w_in"], ffn_out("ffn2_w_out"), "ffn2",
        (norm_bwd_fn(1.0), h2, [dh3], [s["ffn2_norm"]], [(D, F32), (D, BF16)], [(1, D)]))
    g["ffn2_w_out"] = as_rows(dwo)
    g["w_out"] = as_rows(_mm(mixed, dh2b, "tn", "mix_out_dw"))

    def mix_bwd_fn(gp_ref, dm_ref, yh_ref, ym_ref, b_ref, dyh_out, dym_out, dg_out):
        gh = _sig(gp_ref[:, 0:D] + b_ref[:, 0:D])
        gm = _sig(gp_ref[:, D:2 * D] + b_ref[:, D:2 * D])
        dm = dm_ref[...]
        dyh_out[...] = (dm * gh).astype(BF16)
        dym_out[...] = (dm * gm).astype(BF16)
        dgh = dm * yh_ref[...] * gh * (1.0 - gh)
        dgm = dm * ym_ref[...] * gm * (1.0 - gm)
        dg_out[:, 0:D] = dgh.astype(BF16)
        dg_out[:, D:2 * D] = dgm.astype(BF16)
        return (jnp.concatenate([jnp.sum(dgh, axis=0, keepdims=True), jnp.sum(dgm, axis=0, keepdims=True)], axis=1),)

    dyh, dym, dgpre, g["b_merge"] = _mm_stack_red(
        dh2b[None], w_o[None], "mix_out_dx", "nt",
        (mix_bwd_fn, gpre, [y_hg, y_mla], [s["b_merge"]], [(D, BF16), (D, BF16), (2 * D, BF16)], [(1, 2 * D)],
         (512, 256, 128)))
    g["w_hg_branch"] = as_rows(_mm(hgy, dyh, "tn", "branch_hg_dw"))
    g["w_mla_branch"] = as_rows(_mm(att, dym, "tn", "branch_mla_dw"))
    g["w_merge"] = _cols_dw(u, dgpre, mw, "proj_gate_dw")
    dhgy = _mm(dyh, w_hgb, "nt", "branch_hg_dx")
    datt, delta = _attn_out_bwd(dym, w_mlab, att, t)
    du_gate = _cols_dx(dgpre, w_mrg, "proj_gate_dx")

    dq, dk, dv = _flash_bwd(q, k, v, datt, lse, delta, t)
    (dp_mla, dwq, dwkv, g["mla_q_lora_norm"], g["mla_kv_lora_norm"], g["q_head_norm"],
     g["k_head_norm"]) = _mla_prep_bwd(*prep_args, dq, dk, dv, t)
    g["w_q_up"], g["w_kv_up"] = dwq.astype(GRAD_DT), dwkv.astype(GRAD_DT)
    dp_hg, g["hg_lb_table"], g["hg_out_norm"] = _hgrn_bwd(p_hg, s["hg_lb_table"], s["hg_out_norm"], o_pre, states,
                                                          dhgy, t)
    dw_in_nat = jnp.concatenate([_mm(u, dp_hg, "tn", "proj_hg_dw"), _mm(u, dp_mla, "tn", "proj_mla_dw")], axis=1)
    g["w_in"] = dw_in_nat.T.reshape(N_DEV, -1, D)
    du_hg = _mm(dp_hg, w_in_hg, "nt", "proj_hg_dx")
    du_mla = _mm(dp_mla, w_in_mla, "nt", "proj_mla_dx")

    def mixnorm_bwd_fn(h_ref, a_ref, b_ref, c_ref, dh_ref, g_ref, dh_out, dhb_out):
        dx, dgain = _rms_bwd(h_ref[...], g_ref[...], a_ref[...] + b_ref[...] + c_ref[...])
        dh = dh_ref[...] + dx
        dh_out[...] = dh
        dhb_out[...] = (0.5 * dh).astype(BF16)
        return (dgain,)

    mix_gain = s["mix_norm"] + early_grads(g)[0:1, 0:1]
    dh1, dfo1, g["mix_norm"] = _rows(mixnorm_bwd_fn, "mixnorm_bwd", t, tm,
                                     [(h1, D, 0), (du_hg, D, 0), (du_mla, D, 0), (du_gate, D, 0), (dh2, D, 0)],
                                     [mix_gain], [(D, F32), (D, BF16)], [(1, D)])
    def last_stage(dw_in, dw_out):
        g["ffn1_w_in"], g["ffn1_w_out"] = dw_in, as_rows(dw_out)
        gain = s["ffn1_norm"] + last_grads(g)[0:1, 0:1]
        return norm_bwd_fn(1.0), x, [dh1], [gain], [(D, F32), (D, BF16)], [(1, D)]

    (grad_x, _, g["ffn1_norm"]), _, _ = _ffn_bwd(dfo1, xn1, gu1, a1, w["ffn1_w_in"], ffn_out("ffn1_w_out"), "ffn1",
                                                 last_stage)
    return loss, grad_x, g


def _coords():
    return lax.axis_index("x"), lax.axis_index("y"), lax.axis_index("c")


def _hbm_call(body, name, ins, out_shapes, scratch):
    any_spec = pl.BlockSpec(memory_space=pl.ANY)
    return pl.pallas_call(
        body, name=name, out_shape=[jax.ShapeDtypeStruct(s, dt) for s, dt in out_shapes],
        in_specs=[any_spec] * len(ins), out_specs=[any_spec] * len(out_shapes), scratch_shapes=scratch,
    )(*ins)


def _my_slot():
    return 4 * lax.axis_index("x") + 2 * lax.axis_index("y") + lax.axis_index("c")


def _put_own(buf, own, index):
    return lax.dynamic_update_index_in_dim(buf, own, index, 0)


def _all_gather(blocks, name):
    nb = len(blocks)

    def body(*refs):
        x_refs, out_refs = refs[:nb], refs[nb:2 * nb]
        send_sems, recv_sems = refs[2 * nb:]
        x, y, c = _coords()
        me, sibling = (x, y, c), (x, y, 1 - c)
        chips = [(1 - x, y), (x, 1 - y), (1 - x, 1 - y)]

        def slot(b, px, py, pc):
            return out_refs[b].at[4 * px + 2 * py + pc]

        def copy(b, kk, block_of, to, src=None):
            return pltpu.make_async_remote_copy(
                src_ref=slot(b, *block_of) if src is None else src, dst_ref=slot(b, *block_of),
                send_sem=send_sems.at[b, kk], recv_sem=recv_sems.at[b, kk], device_id=to, device_id_type=MESH)

        first = [copy(b, 0, me, sibling, src=x_refs[b]) for b in range(nb)]
        first += [copy(b, 1 + j, me, (*chip, c), src=x_refs[b]) for j, chip in enumerate(chips) for b in range(nb)]
        for cp in first:
            cp.start()
        passed = []
        for j, chip in enumerate(chips):
            for b in range(nb):
                copy(b, 1 + j, (*chip, c), me).wait_recv()
                passed.append(copy(b, 4 + j, (*chip, c), sibling))
                passed[-1].start()
        for b in range(nb):
            copy(b, 0, sibling, me).wait_recv()
        for j, chip in enumerate(chips):
            for b in range(nb):
                copy(b, 4 + j, (*chip, 1 - c), me).wait_recv()
        for cp in first + passed:
            cp.wait_send()

    outs = _hbm_call(body, name, blocks, [((N_DEV,) + b.shape, b.dtype) for b in blocks],
                     [pltpu.SemaphoreType.DMA((nb, 7)), pltpu.SemaphoreType.DMA((nb, 7))])
    return [_put_own(o, b[None], _my_slot()) for o, b in zip(outs, blocks)]


def _gather_peers():
    x, y, c = _coords()
    return (x, y, c), [(x, y, 1 - c), (1 - x, y, c), (x, 1 - y, c), (1 - x, 1 - y, c)]


def _gather_start(blocks, after, name):
    nb = len(blocks)
    hbm, sem = pl.BlockSpec(memory_space=pltpu.HBM), pl.BlockSpec(memory_space=pltpu.SEMAPHORE)

    def body(*refs):
        x_refs, out_refs = refs[:nb], refs[nb:2 * nb]
        send_sems, recv_sems, token = refs[2 * nb + 1], refs[2 * nb + 2], refs[-1]
        (x, y, c), peers = _gather_peers()
        for kk, peer in enumerate(peers):
            for b in range(nb):
                pltpu.make_async_remote_copy(
                    src_ref=x_refs[b], dst_ref=out_refs[b].at[4 * x + 2 * y + c], send_sem=send_sems.at[4 * b + kk],
                    recv_sem=recv_sems.at[4 * b + kk], device_id=peer, device_id_type=MESH).start()
        token[...] = jnp.zeros_like(token)

    gathers = [pltpu.with_memory_space_constraint(lax.empty((N_DEV,) + b.shape, b.dtype), pltpu.HBM) for b in blocks]
    outs = pl.pallas_call(
        body, name=name,
        out_shape=(pltpu.SemaphoreType.DMA((4 * nb,)), pltpu.SemaphoreType.DMA((4 * nb,)),
                   *[pltpu.HBM(b.shape, b.dtype) for b in blocks], *[pltpu.HBM(b.shape, b.dtype) for b in gathers],
                   jax.ShapeDtypeStruct((SUBLANE, LANE), F32)),
        in_specs=[hbm] * (2 * nb) + [pl.BlockSpec(memory_space=pl.ANY)],
        out_specs=(sem, sem, *[hbm] * (2 * nb), pl.BlockSpec(memory_space=pltpu.VMEM)),
        input_output_aliases={i: 2 + i for i in range(2 * nb)},
        compiler_params=pltpu.CompilerParams(has_side_effects=pltpu.SideEffectType.DATAFLOW_SIDE_EFFECTING),
    )(*[pltpu.with_memory_space_constraint(b, pltpu.HBM) for b in blocks], *gathers, after)
    return outs[0], outs[1], list(outs[2:2 + nb]), list(outs[2 + nb:2 + 2 * nb]), outs[-1]


def _gather_wait(send_sems, recv_sems, thru, gathers, after, name):
    nb = len(thru)
    hbm, sem = pl.BlockSpec(memory_space=pltpu.HBM), pl.BlockSpec(memory_space=pltpu.SEMAPHORE)

    def body(*refs):
        x_refs, out_refs = refs[:nb], refs[nb:2 * nb]
        send_sems_, recv_sems_ = refs[2 * nb], refs[2 * nb + 1]
        _, peers = _gather_peers()
        for kk, (px, py, pc) in enumerate(peers):
            for b in range(nb):
                cp = pltpu.make_async_remote_copy(
                    src_ref=x_refs[b], dst_ref=out_refs[b].at[4 * px + 2 * py + pc], send_sem=send_sems_.at[4 * b + kk],
                    recv_sem=recv_sems_.at[4 * b + kk], device_id=(px, py, pc), device_id_type=MESH)
                cp.wait_send()
                cp.wait_recv()

    outs = pl.pallas_call(
        body, name=name,
        out_shape=(*[pltpu.HBM(b.shape, b.dtype) for b in thru], *[pltpu.HBM(b.shape, b.dtype) for b in gathers]),
        in_specs=[hbm] * (2 * nb) + [sem, sem, pl.BlockSpec(memory_space=pl.ANY)], out_specs=[hbm] * (2 * nb),
        input_output_aliases={i: i for i in range(2 * nb)},
        compiler_params=pltpu.CompilerParams(has_side_effects=pltpu.SideEffectType.DATAFLOW_SIDE_EFFECTING),
    )(*thru, *gathers, send_sems, recv_sems, after)
    return list(outs[:nb]), list(outs[nb:])


def _gather_finish(blocks, gathers, name):
    nb = len(blocks)

    def body(*refs):
        x_refs, in_refs, out_refs = refs[:nb], refs[nb:2 * nb], refs[2 * nb:3 * nb]
        send_sems, recv_sems = refs[3 * nb:]
        (x, y, c), peers = _gather_peers()
        copies = []
        for j, (px, py, _) in enumerate(peers[1:]):
            for b in range(nb):
                copies.append(pltpu.make_async_remote_copy(
                    src_ref=in_refs[b].at[4 * px + 2 * py + c], dst_ref=out_refs[b].at[4 * px + 2 * py + c],
                    send_sem=send_sems.at[b, j], recv_sem=recv_sems.at[b, j], device_id=(x, y, 1 - c),
                    device_id_type=MESH))
                copies[-1].start()
        for j, (px, py, _) in enumerate(peers[1:]):
            for b in range(nb):
                pltpu.make_async_remote_copy(
                    src_ref=in_refs[b].at[4 * px + 2 * py + c], dst_ref=out_refs[b].at[4 * px + 2 * py + 1 - c],
                    send_sem=send_sems.at[b, j], recv_sem=recv_sems.at[b, j], device_id=(x, y, 1 - c),
                    device_id_type=MESH).wait_recv()
        for cp in copies:
            cp.wait_send()

    any_spec = pl.BlockSpec(memory_space=pl.ANY)
    outs = pl.pallas_call(
        body, name=name, out_shape=[jax.ShapeDtypeStruct(b.shape, b.dtype) for b in gathers],
        in_specs=[any_spec] * (2 * nb), out_specs=[any_spec] * nb,
        input_output_aliases={nb + i: i for i in range(nb)},
        scratch_shapes=[pltpu.SemaphoreType.DMA((nb, 3)), pltpu.SemaphoreType.DMA((nb, 3))],
    )(*blocks, *gathers)
    return [_put_own(o, b[None], _my_slot()) for o, b in zip(outs, blocks)]


def _sibling_swap(bufs, name):
    nb = len(bufs)

    def body(*refs):
        x_refs, out_refs = refs[:nb], refs[nb:2 * nb]
        send_sems, recv_sems = refs[2 * nb:]
        x, y, c = _coords()
        copies = [pltpu.make_async_remote_copy(
            src_ref=x_refs[b].at[2 * q + 1 - c], dst_ref=out_refs[b].at[q], send_sem=send_sems.at[b, q],
            recv_sem=recv_sems.at[b, q], device_id=(x, y, 1 - c), device_id_type=MESH)
            for b in range(nb) for q in range(4)]
        for cp in copies:
            cp.start()
        for cp in copies:
            cp.wait()

    return _hbm_call(body, name, bufs, [((4,) + b.shape[1:], b.dtype) for b in bufs],
                     [pltpu.SemaphoreType.DMA((nb, 4)), pltpu.SemaphoreType.DMA((nb, 4))])


def _chip_exchange_start(bufs, name):
    nb = len(bufs)
    hbm, sem = pl.BlockSpec(memory_space=pltpu.HBM), pl.BlockSpec(memory_space=pltpu.SEMAPHORE)

    def body(*refs):
        x_refs, land_refs = refs[:nb], refs[nb:2 * nb]
        send_sems, recv_sems, token = refs[2 * nb], refs[2 * nb + 1], refs[-1]
        x, y, c = _coords()
        for j, (px, py) in enumerate([(1 - x, y), (x, 1 - y), (1 - x, 1 - y)]):
            for b in range(nb):
                pltpu.make_async_remote_copy(
                    src_ref=x_refs[b].at[2 * px + py], dst_ref=land_refs[b].at[2 * x + y], send_sem=send_sems.at[3 * b + j],
                    recv_sem=recv_sems.at[3 * b + j], device_id=(px, py, c), device_id_type=MESH).start()
        token[...] = jnp.zeros_like(token)

    lands = [pltpu.with_memory_space_constraint(lax.empty(b.shape, b.dtype), pltpu.HBM) for b in bufs]
    outs = pl.pallas_call(
        body, name=name,
        out_shape=(pltpu.SemaphoreType.DMA((3 * nb,)), pltpu.SemaphoreType.DMA((3 * nb,)),
                   *[pltpu.HBM(b.shape, b.dtype) for b in bufs], *[pltpu.HBM(b.shape, b.dtype) for b in bufs],
                   jax.ShapeDtypeStruct((SUBLANE, LANE), F32)),
        in_specs=[hbm] * (2 * nb), out_specs=(sem, sem, *[hbm] * (2 * nb), pl.BlockSpec(memory_space=pltpu.VMEM)),
        input_output_aliases={i: 2 + i for i in range(2 * nb)},
        compiler_params=pltpu.CompilerParams(has_side_effects=pltpu.SideEffectType.DATAFLOW_SIDE_EFFECTING),
    )(*[pltpu.with_memory_space_constraint(b, pltpu.HBM) for b in bufs], *lands)
    return outs[0], outs[1], list(outs[2:2 + nb]), list(outs[2 + nb:2 + 2 * nb]), outs[-1]


def _chip_exchange_wait(send_sems, recv_sems, thru, lands, after, name):
    nb = len(thru)
    hbm, sem = pl.BlockSpec(memory_space=pltpu.HBM), pl.BlockSpec(memory_space=pltpu.SEMAPHORE)

    def body(*refs):
        x_refs, land_refs = refs[:nb], refs[nb:2 * nb]
        send_sems_, recv_sems_ = refs[2 * nb], refs[2 * nb + 1]
        x, y, c = _coords()
        for j, (px, py) in enumerate([(1 - x, y), (x, 1 - y), (1 - x, 1 - y)]):
            for b in range(nb):
                cp = pltpu.make_async_remote_copy(
                    src_ref=x_refs[b].at[2 * px + py], dst_ref=land_refs[b].at[2 * px + py],
                    send_sem=send_sems_.at[3 * b + j], recv_sem=recv_sems_.at[3 * b + j], device_id=(px, py, c),
                    device_id_type=MESH)
                cp.wait_send()
                cp.wait_recv()

    outs = pl.pallas_call(
        body, name=name,
        out_shape=(*[pltpu.HBM(b.shape, b.dtype) for b in thru], *[pltpu.HBM(b.shape, b.dtype) for b in lands]),
        in_specs=[hbm] * (2 * nb) + [sem, sem, pl.BlockSpec(memory_space=pl.ANY)], out_specs=[hbm] * (2 * nb),
        input_output_aliases={i: i for i in range(2 * nb)},
        compiler_params=pltpu.CompilerParams(has_side_effects=pltpu.SideEffectType.DATAFLOW_SIDE_EFFECTING),
    )(*thru, *lands, send_sems, recv_sems, after)
    return list(outs[:nb]), list(outs[nb:])


def _chip_sum(g, r1, c, name):
    _, r, cw = g.shape
    tr = _pick(r, (256, 176, 128))

    def body(c_ref, g_ref, r_ref, o_ref):
        o_ref[...] = (g_ref[...].astype(F32) + r_ref[...].astype(F32)).astype(GRAD_DT)

    grid_spec = pltpu.PrefetchScalarGridSpec(
        num_scalar_prefetch=1, grid=(4, r // tr),
        in_specs=[_BS((None, None, tr, cw), lambda q, i, c_ref: (q, c_ref[0], i, 0)),
                  _BS((None, tr, cw), lambda q, i, c_ref: (q, i, 0))],
        out_specs=_BS((None, tr, cw), lambda q, i, c_ref: (q, i, 0)))
    return pl.pallas_call(
        body, name=name, grid_spec=grid_spec, out_shape=jax.ShapeDtypeStruct((4, r, cw), GRAD_DT),
        compiler_params=_params(("parallel", "parallel")),
    )(c.reshape(1).astype(jnp.int32), g.reshape(4, 2, r, cw), r1)


def _adamw_math(w, g, m, v):
    m = B1 * m + (1.0 - B1) * g
    v = B2 * v + (1.0 - B2) * (g * g)
    m_hat = m / (1.0 - B1 ** STEP)
    v_hat = v / (1.0 - B2 ** STEP)
    return -LR * (m_hat / (jnp.sqrt(v_hat) + AEPS) + WD * w), m, v


def _sum_adamw(parts, w, m, v, name):
    r, c = w.shape
    tr = _pick(r, (256, 176, 128))
    tc = _pick(c, (256, 128)) if tr == r and r > 256 else c

    def body(p0, p1, p2, p3, w_ref, m_ref, v_ref, g_out, d_out, m_out, v_out):
        g = ((p0[...].astype(F32) + p1[...].astype(F32)) + p2[...].astype(F32)) + p3[...].astype(F32)
        g_out[...] = g
        d_out[...], m_out[...], v_out[...] = _adamw_math(w_ref[...], g, m_ref[...], v_ref[...])

    part = lambda q: _BS((None, tr, tc), functools.partial(lambda i, j, q: (q, i, j), q=q))
    plain = _BS((tr, tc), lambda i, j: (i, j))
    return pl.pallas_call(
        body, name=name, grid=(r // tr, c // tc), in_specs=[part(q) for q in range(4)] + [plain] * 3,
        out_specs=[plain] * 4, out_shape=[jax.ShapeDtypeStruct((r, c), F32)] * 4,
        compiler_params=_params(("parallel", "parallel")),
    )(parts, parts, parts, parts, w, m, v)


def _small_update(gathered, w, m, v):
    r = w.shape[0]

    def body(ga_ref, w_ref, m_ref, v_ref, g_out, d_out, m_out, v_out):
        g = ga_ref[0]
        for dev in range(1, N_DEV):
            g = g + ga_ref[dev]
        g_out[...] = g
        d_out[...], m_out[...], v_out[...] = _adamw_math(w_ref[...], g, m_ref[...], v_ref[...])

    return pl.pallas_call(
        body, name="small_update", out_shape=[jax.ShapeDtypeStruct((r, LANE), F32)] * 4,
    )(gathered, w, m, v)


def _pack_small(vals):
    rows = []
    for name, (r, n) in SMALL:
        flat = vals[name].reshape(-1)
        pad = (-flat.shape[0]) % (SUBLANE * LANE)
        rows.append(jnp.pad(flat, (0, pad)).reshape(-1, LANE))
    return jnp.concatenate(rows, axis=0)


def _unpack_small(packed):
    out, off = {}, 0
    for name, (r, n) in SMALL:
        nrow = -(-(r * n) // (SUBLANE * LANE)) * SUBLANE
        out[name] = packed[off:off + nrow].reshape(-1)[:r * n].reshape(r, n)
        off += nrow
    return out


def kernel(x, positions, ffn1_norm, ffn1_w_in, ffn1_w_out, mix_norm, w_in, hg_lb_table, hg_out_norm, w_hg_branch, mla_q_lora_norm, w_q_up, mla_kv_lora_norm, w_kv_up, q_head_norm, k_head_norm, w_mla_branch, w_merge, b_merge, w_out, ffn2_norm, ffn2_w_in, ffn2_w_out, final_norm, loss_target, m_ffn1_norm, m_ffn1_w_in, m_ffn1_w_out, m_mix_norm, m_w_in, m_hg_lb_table, m_hg_out_norm, m_w_hg_branch, m_mla_q_lora_norm, m_w_q_up, m_mla_kv_lora_norm, m_w_kv_up, m_q_head_norm, m_k_head_norm, m_w_mla_branch, m_w_merge, m_b_merge, m_w_out, m_ffn2_norm, m_ffn2_w_in, m_ffn2_w_out, m_final_norm, v_ffn1_norm, v_ffn1_w_in, v_ffn1_w_out, v_mix_norm, v_w_in, v_hg_lb_table, v_hg_out_norm, v_w_hg_branch, v_mla_q_lora_norm, v_w_q_up, v_mla_kv_lora_norm, v_w_kv_up, v_q_head_norm, v_k_head_norm, v_w_mla_branch, v_w_merge, v_b_merge, v_w_out, v_ffn2_norm, v_ffn2_w_in, v_ffn2_w_out, v_final_norm):
    args = dict(locals())
    t = x.shape[1]
    big_w = {n: args[n][0] for n, _, _ in BIG}
    small = {n: args[n].reshape(shape) for n, shape in SMALL}

    names = [n for n, _, _ in BIG]
    first, rest = names[:2], names[2:]
    full = dict(zip(first, _all_gather([big_w[n].astype(BF16) for n in first], "weights_all_gather_ffn1")))
    g_send, g_recv, g_thru, g_bufs, g_token = _gather_start([big_w[n].astype(BF16) for n in rest], full[first[0]],
                                                            "weights_gather_start")
    gains = dict(small, ffn1_norm=small["ffn1_norm"] + g_token[0:1, 0:1])

    def late_weights(after):
        blocks, bufs = _gather_wait(g_send, g_recv, g_thru, g_bufs, after, "weights_gather_wait")
        return dict(zip(rest, _gather_finish(blocks, bufs, "weights_gather_finish")))

    inv_freq = ROPE_THETA ** (-jnp.arange(0, ROPE, 2, dtype=F32) / ROPE)
    ang = positions[0].astype(F32)[:, None] * inv_freq
    cs = jnp.concatenate([jnp.cos(ang), jnp.cos(ang)], axis=1)
    sn = jnp.concatenate([jnp.sin(ang), jnp.sin(ang)], axis=1)

    c = lax.axis_index("c")
    chip = 2 * lax.axis_index("x") + lax.axis_index("y")
    early = {}

    def chip_sums_of(g, ns, tag):
        from_sibling = _sibling_swap([g[n] for n in ns], "grads_sibling_swap_" + tag)
        return [_chip_sum(g[n], r1, c, "chip_sum_" + n) for n, r1 in zip(ns, from_sibling)]

    def early_grads(g):
        early["names"] = [n for n in names if n in g]
        early["sums"] = chip_sums_of(g, early["names"], "early")
        early["send"], early["recv"], early["thru"], early["lands"], token = _chip_exchange_start(
            early["sums"], "grads_exchange_start")
        return token

    def last_grads(g):
        early["late"] = [n for n in names if n not in early["names"]]
        early["l_send"], early["l_recv"], early["l_thru"], early["l_lands"], token = _chip_exchange_start(
            chip_sums_of(g, early["late"], "late"), "grads_exchange_late_start")
        return token

    loss_row, grad_x, g = _local_step(x[0], loss_target[0], cs, sn, full, late_weights, gains, early_grads,
                                      last_grads)
    late, l_send, l_recv = early["late"], early["l_send"], early["l_recv"]
    l_thru, l_lands = early["l_thru"], early["l_lands"]
    sent, landed = _chip_exchange_wait(early["send"], early["recv"], early["thru"], early["lands"], grad_x,
                                       "grads_exchange_wait")
    exchanged = {n: _put_own(land, lax.dynamic_index_in_dim(own, chip, 0), chip)
                 for n, land, own in zip(early["names"], landed, sent)}

    small_packed = jnp.concatenate([_pack_small(g), jnp.pad(loss_row, ((0, SUBLANE - 1), (0, 0)))], axis=0)
    small_all = _all_gather([small_packed], "small_all_gather")[0]
    zero_tail = jnp.zeros((SUBLANE, LANE), F32)
    pk = lambda d: jnp.concatenate([_pack_small(d), zero_tail], axis=0)
    sg, sd, sm, sv = _small_update(
        small_all, pk(small), pk({n: args["m_" + n].reshape(shape) for n, shape in SMALL}),
        pk({n: args["v_" + n].reshape(shape) for n, shape in SMALL}))
    n_small_rows = _pack_small(small).shape[0]
    loss = sg[n_small_rows, 0]
    outs = {k_: _unpack_small(a) for k_, a in (("grad", sg), ("delta", sd), ("new_m", sm), ("new_v", sv))}

    def adamw(n):
        tr = (lambda a: a.T) if n in GRAD_T else (lambda a: a)
        res = _sum_adamw(exchanged[n], tr(big_w[n]), tr(args["m_" + n][0]), tr(args["v_" + n][0]), "adamw_" + n)
        outs["grad"][n], outs["delta"][n], outs["new_m"][n], outs["new_v"][n] = [tr(a) for a in res]

    for n in early["names"]:
        adamw(n)
    sent, landed = _chip_exchange_wait(l_send, l_recv, l_thru, l_lands, outs["delta"][early["names"][-1]],
                                       "grads_exchange_late_wait")
    for n, land, own in zip(late, landed, sent):
        exchanged[n] = _put_own(land, lax.dynamic_index_in_dim(own, chip, 0), chip)
        adamw(n)

    def shaped(kind, n):
        return outs[kind][n].reshape(args[n].shape)

    return (loss, grad_x[None], *[shaped("grad", n) for n in WEIGHT_ORDER], *[shaped("delta", n) for n in WEIGHT_ORDER],
            *[shaped("new_m", n) for n in WEIGHT_ORDER], *[shaped("new_v", n) for n in WEIGHT_ORDER])
```

```python
import functools

import jax
import jax.numpy as jnp
from jax import lax
from jax.experimental import pallas as pl
from jax.experimental.pallas import tpu as pltpu

F32 = jnp.float32
BF16 = jnp.bfloat16

D = 1024
FF = 2816
NH = 8
HD = 128
ROPE = 64
QK = HD + ROPE
QL = 384
KVL = 256
HGW = NH * HD
CHUNK = 64
EPS = 1e-6
ROPE_THETA = 10000.0
SCALE = QK ** -0.5
LOG2E = 1.4426950408889634

LR, B1, B2, AEPS, WD, STEP = 0.001, 0.9, 0.999, 1e-08, 0.01, 10

HB = 128
SUB = 16
EXP_CLAMP = 80.0
ATT_TILES = (512, 256, 128)
ATT_Q = (1024, 512, 256, 128)
ATT_KEY_TILES = 1
GRAD_DT = BF16
ROW_TILES = (1024, 512, 256, 128)

LANE = 128
SUBLANE = 8
VMEM_LIMIT = 56 << 20

N_DEV = 8
MESH = pl.DeviceIdType.MESH

BIG = (
    ("ffn1_w_in", (D, 2 * FF), 1), ("ffn1_w_out", (FF, D), 0), ("w_in", (D, 4800), 1),
    ("w_hg_branch", (HGW, D), 0), ("w_q_up", (QL, NH * QK), 1), ("w_kv_up", (KVL, NH * 2 * HD), 1),
    ("w_mla_branch", (NH * HD, D), 0), ("w_merge", (D, 2 * D), 1), ("w_out", (D, D), 0),
    ("ffn2_w_in", (D, 2 * FF), 1), ("ffn2_w_out", (FF, D), 0),
)
SMALL = (
    ("ffn1_norm", (1, D)), ("mix_norm", (1, D)), ("hg_lb_table", (2, HGW)), ("hg_out_norm", (1, HD)),
    ("mla_q_lora_norm", (1, QL)), ("mla_kv_lora_norm", (1, KVL)), ("q_head_norm", (1, QK)),
    ("k_head_norm", (1, QK)), ("b_merge", (1, 2 * D)), ("ffn2_norm", (1, D)), ("final_norm", (1, D)),
)
GRAD_T = ("ffn1_w_in", "ffn2_w_in", "w_in", "w_q_up")
WEIGHT_ORDER = ("ffn1_norm", "ffn1_w_in", "ffn1_w_out", "mix_norm", "w_in", "hg_lb_table", "hg_out_norm",
                "w_hg_branch", "mla_q_lora_norm", "w_q_up", "mla_kv_lora_norm", "w_kv_up", "q_head_norm",
                "k_head_norm", "w_mla_branch", "w_merge", "b_merge", "w_out", "ffn2_norm", "ffn2_w_in",
                "ffn2_w_out", "final_norm")


def _pick(n, cands):
    for c in cands:
        if n % c == 0:
            return c
    return n


def _params(sem):
    return pltpu.CompilerParams(dimension_semantics=sem, vmem_limit_bytes=VMEM_LIMIT)


def _sig(x):
    return 0.5 * jnp.tanh(0.5 * x) + 0.5


def _dot(a, b):
    return jnp.dot(a.astype(BF16), b.astype(BF16), preferred_element_type=F32)


def _dot_nt(a, b):
    return lax.dot_general(a.astype(BF16), b.astype(BF16), (((1,), (1,)), ((), ())),
                           preferred_element_type=F32)


def _dot_tn(a, b):
    return lax.dot_general(a.astype(BF16), b.astype(BF16), (((0,), (0,)), ((), ())),
                           preferred_element_type=F32)


def _split3(x):
    x1 = x.astype(BF16)
    r1 = x - x1.astype(F32)
    x2 = r1.astype(BF16)
    x3 = (r1 - x2.astype(F32)).astype(BF16)
    return x1, x2, x3


def _dot_sel(m, x):
    x1, x2, x3 = _split3(x)
    d = lambda p: jnp.dot(m, p, preferred_element_type=F32)
    return d(x1) + d(x2) + d(x3)


def _sel_dot(x, m):
    x1, x2, x3 = _split3(x)
    d = lambda p: jnp.dot(p, m, preferred_element_type=F32)
    return d(x1) + d(x2) + d(x3)


_TN = (1408, 1024, 768, 512, 384, 256, 128)


def _accumulate(acc, o_ref, axis, steps, term):
    step = pl.program_id(axis)

    @pl.when(step == 0)
    def _():
        acc[...] = jnp.zeros_like(acc)

    val = term()
    if isinstance(val, (list, tuple)):
        for e, v in enumerate(val):
            acc[e] += v
    else:
        acc[...] += val

    @pl.when(step == steps - 1)
    def _():
        o_ref[...] = acc[...].astype(o_ref.dtype)


def _mm(a, b, mode, name, out_dtype=F32):
    if mode == "tn":
        t, m = a.shape
        n = b.shape[1]
        tt, tm, tn = _pick(t, (512, 256, 128)), _pick(m, _TN), _pick(n, _TN)

        def body(a_ref, b_ref, o_ref, acc):
            _accumulate(acc, o_ref, 2, t // tt, lambda: _dot_tn(a_ref[...], b_ref[...]))

        return pl.pallas_call(
            body, name=name, grid=(m // tm, n // tn, t // tt),
            in_specs=[pl.BlockSpec((tt, tm), lambda i, j, k: (k, i)),
                      pl.BlockSpec((tt, tn), lambda i, j, k: (k, j))],
            out_specs=pl.BlockSpec((tm, tn), lambda i, j, k: (i, j)),
            out_shape=jax.ShapeDtypeStruct((m, n), GRAD_DT), scratch_shapes=[pltpu.VMEM((tm, tn), F32)],
            compiler_params=_params(("parallel", "parallel", "arbitrary")),
        )(a, b)

    m, k = a.shape
    tm = _pick(m, ROW_TILES)
    if mode == "nn":
        n = b.shape[1]
        tn = _pick(n, _TN)
        b_spec = pl.BlockSpec((k, tn), lambda i, j: (0, j))
        dot = _dot
    else:
        n = b.shape[0]
        tn = _pick(n, _TN if k <= 4096 else (512, 256, 128))
        b_spec = pl.BlockSpec((tn, k), lambda i, j: (j, 0))
        dot = _dot_nt

    def body(a_ref, b_ref, o_ref):
        o_ref[...] = dot(a_ref[...], b_ref[...]).astype(o_ref.dtype)

    return pl.pallas_call(
        body, name=name, grid=(m // tm, n // tn),
        in_specs=[pl.BlockSpec((tm, k), lambda i, j: (i, 0)), b_spec],
        out_specs=pl.BlockSpec((tm, tn), lambda i, j: (i, j)),
        out_shape=jax.ShapeDtypeStruct((m, n), out_dtype),
        compiler_params=_params(("parallel", "parallel")),
    )(a, b)


_DOTS = {"nn": _dot, "nt": _dot_nt, "tn": _dot_tn}
_BS = pl.BlockSpec


def _mmcall(name, kind, a, b, a_spec, b_spec, o_spec, o_shape, grid, red_axis=None, out_dtype=F32):
    dot = _DOTS[kind]

    def body(a_ref, b_ref, o_ref):
        if red_axis is None:
            o_ref[...] = dot(a_ref[...], b_ref[...]).astype(o_ref.dtype)
        else:
            @pl.when(pl.program_id(red_axis) == 0)
            def _():
                o_ref[...] = jnp.zeros_like(o_ref)

            o_ref[...] += dot(a_ref[...], b_ref[...])

    sem = tuple("arbitrary" if ax == red_axis else "parallel" for ax in range(len(grid)))
    return pl.pallas_call(
        body, name=name, grid=grid, in_specs=[a_spec, b_spec], out_specs=o_spec,
        out_shape=jax.ShapeDtypeStruct(o_shape, out_dtype), compiler_params=_params(sem),
    )(a, b)


class _AsRef:
    def __init__(self, value):
        self.value = value

    def __getitem__(self, idx):
        return self.value


def _mm_stack_red(a, w, name, kind, epilogue=None):
    s, t, n = a.shape
    nout = w.shape[2] if kind == "nn" else w.shape[1]
    if epilogue is None:
        tm = _pick(t, ROW_TILES)
        return _mmcall(name, kind, a, w, _BS((None, tm, n), lambda i, j: (j, i, 0)),
                       _BS((None,) + w.shape[1:], lambda i, j: (j, 0, 0)), _BS((tm, nout), lambda i, j: (i, 0)),
                       (t, nout), (t // tm, s), red_axis=1)
    fn, first, rows, vecs, outs, accs = epilogue[:6]
    rows = [first] + list(rows)
    tm = _pick(t, epilogue[6] if len(epilogue) > 6 else ROW_TILES)
    n_row, n_vec, n_out = len(rows), len(vecs), len(outs)
    dot = _DOTS[kind]

    def body(*refs):
        a_ref, w_ref, prod = refs[0], refs[1], refs[-1]
        row_refs = refs[2:2 + n_row]
        vec_refs = refs[2 + n_row:2 + n_row + n_vec]
        out_refs = refs[2 + n_row + n_vec:2 + n_row + n_vec + n_out]
        acc_refs = refs[2 + n_row + n_vec + n_out:-1]
        i, j = pl.program_id(0), pl.program_id(1)

        @pl.when(j == 0)
        def _():
            prod[...] = jnp.zeros_like(prod)

        prod[...] += dot(a_ref[...], w_ref[...])

        @pl.when(j == s - 1)
        def _():
            res = fn(row_refs[0], _AsRef(prod[...]), *row_refs[1:], *vec_refs, *out_refs)
            if acc_refs:
                @pl.when(i == 0)
                def _():
                    for r in acc_refs:
                        r[...] = jnp.zeros_like(r)

                for r, val in zip(acc_refs, res):
                    r[...] += val

    in_specs = [_BS((None, tm, n), lambda i, j: (j, i, 0)), _BS((None,) + w.shape[1:], lambda i, j: (j, 0, 0))]
    in_specs += [_BS((tm, r.shape[1]), lambda i, j: (i, 0)) for r in rows]
    in_specs += [_BS(v.shape, lambda i, j: (0, 0)) for v in vecs]
    out_specs = [_BS((tm, wd), lambda i, j: (i, 0)) for wd, _ in outs] + [_BS(sh, lambda i, j: (0, 0)) for sh in accs]
    out_shape = [jax.ShapeDtypeStruct((t, wd), dt) for wd, dt in outs] + [jax.ShapeDtypeStruct(sh, F32) for sh in accs]
    return pl.pallas_call(
        body, name=name, grid=(t // tm, s), in_specs=in_specs, out_specs=out_specs, out_shape=out_shape,
        scratch_shapes=[pltpu.VMEM((tm, nout), F32)],
        compiler_params=_params(("arbitrary" if accs else "parallel", "arbitrary")),
    )(a, w, *rows, *vecs)


def _mm_stack_tn(a, b, name):
    grp = 4
    if a.ndim == 2:
        t, k = a.shape
        s, _, n = b.shape
        tt = _pick(t, (512, 256, 128))
        a_spec = _BS((tt, k), lambda j, r: (r, 0))
        b_spec, b_in = _BS((None, grp, tt, n), lambda j, r: (j, 0, r, 0)), b.reshape(s // grp, grp, t, n)
        a_in = a
    else:
        s, t, k = a.shape
        n = b.shape[1]
        tt = _pick(t, (512, 256, 128))
        a_spec, a_in = _BS((None, grp, tt, k), lambda j, r: (j, 0, r, 0)), a.reshape(s // grp, grp, t, k)
        b_spec, b_in = _BS((tt, n), lambda j, r: (r, 0)), b

    def body(a_ref, b_ref, o_ref, acc):
        def terms():
            shared = a_ref[...] if a.ndim == 2 else b_ref[...]
            return [_dot_tn(shared, b_ref[e]) if a.ndim == 2 else _dot_tn(a_ref[e], shared) for e in range(grp)]

        _accumulate(acc, o_ref, 1, t // tt, terms)

    return pl.pallas_call(
        body, name=name, grid=(s // grp, t // tt), in_specs=[a_spec, b_spec],
        out_specs=_BS((None, grp, k, n), lambda j, r: (j, 0, 0, 0)),
        out_shape=jax.ShapeDtypeStruct((s // grp, grp, k, n), GRAD_DT), scratch_shapes=[pltpu.VMEM((grp, k, n), F32)],
        compiler_params=_params(("parallel", "arbitrary")),
    )(a_in, b_in).reshape(s, k, n)


def _cols_fwd(x, w, name):
    t, k = x.shape
    s, _, n = w.shape
    tm = _pick(t, ROW_TILES)

    def body(x_ref, w_ref, o_ref):
        x_ = x_ref[...]
        for j in range(s):
            o_ref[:, n * j:n * (j + 1)] = _dot(x_, w_ref[j])

    return pl.pallas_call(
        body, name=name, grid=(t // tm,),
        in_specs=[_BS((tm, k), lambda i: (i, 0)), _BS((s, k, n), lambda i: (0, 0, 0))],
        out_specs=_BS((tm, s * n), lambda i: (i, 0)), out_shape=jax.ShapeDtypeStruct((t, s * n), F32),
        compiler_params=_params(("parallel",)),
    )(x, w)


def _cols_dx(d, w, name):
    t = d.shape[0]
    s, k, n = w.shape
    tm = _pick(t, ROW_TILES)

    def body(d_ref, w_ref, o_ref):
        acc = _dot_nt(d_ref[:, 0:n], w_ref[0])
        for j in range(1, s):
            acc = acc + _dot_nt(d_ref[:, n * j:n * (j + 1)], w_ref[j])
        o_ref[...] = acc

    return pl.pallas_call(
        body, name=name, grid=(t // tm,),
        in_specs=[_BS((tm, s * n), lambda i: (i, 0)), _BS((s, k, n), lambda i: (0, 0, 0))],
        out_specs=_BS((tm, k), lambda i: (i, 0)), out_shape=jax.ShapeDtypeStruct((t, k), F32),
        compiler_params=_params(("parallel",)),
    )(d, w)


def _cols_dw(x, d, n, name):
    t, k = x.shape
    s = d.shape[1] // n
    tt = _pick(t, (512, 256, 128))

    def body(x_ref, d_ref, o_ref, acc):
        def terms():
            x_ = x_ref[...]
            return [_dot_tn(x_, d_ref[:, n * j:n * (j + 1)]) for j in range(s)]

        _accumulate(acc, o_ref, 0, t // tt, terms)

    return pl.pallas_call(
        body, name=name, grid=(t // tt,),
        in_specs=[_BS((tt, k), lambda r: (r, 0)), _BS((tt, s * n), lambda r: (r, 0))],
        out_specs=_BS((s, k, n), lambda r: (0, 0, 0)), out_shape=jax.ShapeDtypeStruct((s, k, n), GRAD_DT),
        scratch_shapes=[pltpu.VMEM((s, k, n), F32)], compiler_params=_params(("arbitrary",)),
    )(x, d)


def _rows(fn, name, t, tm, ins, vecs, outs, accs=()):
    n_in, n_out, n_acc = len(ins) + len(vecs), len(outs), len(accs)

    def body(*refs):
        res = fn(*refs[:n_in + n_out])
        if n_acc:
            acc_refs = refs[n_in + n_out:]

            @pl.when(pl.program_id(0) == 0)
            def _():
                for r in acc_refs:
                    r[...] = jnp.zeros_like(r)

            for r, val in zip(acc_refs, res):
                r[...] += val

    in_specs = [pl.BlockSpec((tm, bw), functools.partial(lambda i, cb: (i, cb), cb=cb)) for _, bw, cb in ins]
    in_specs += [pl.BlockSpec(v.shape, lambda i: (0, 0)) for v in vecs]
    out_specs = [pl.BlockSpec((tm, w), lambda i: (i, 0)) for w, _ in outs]
    out_specs += [pl.BlockSpec(s, lambda i: (0, 0)) for s in accs]
    out_shape = [jax.ShapeDtypeStruct((t, w), dt) for w, dt in outs]
    out_shape += [jax.ShapeDtypeStruct(s, F32) for s in accs]
    return pl.pallas_call(
        body, name=name, grid=(t // tm,), in_specs=in_specs, out_specs=out_specs, out_shape=out_shape,
        compiler_params=_params(("arbitrary",) if n_acc else ("parallel",)),
    )(*[a for a, _, _ in ins], *vecs)


def _rms(x, g):
    return x * lax.rsqrt(jnp.mean(x * x, axis=-1, keepdims=True) + EPS) * g


def _rms_bwd(x, g, dy):
    xh = x * lax.rsqrt(jnp.mean(x * x, axis=-1, keepdims=True) + EPS)
    r = lax.rsqrt(jnp.mean(x * x, axis=-1, keepdims=True) + EPS)
    dyg = dy * g
    dx = r * (dyg - xh * jnp.mean(dyg * xh, axis=-1, keepdims=True))
    return dx, jnp.sum(dy * xh, axis=0, keepdims=True)


def _hgrn_mats():
    row = lax.broadcasted_iota(jnp.int32, (HB, HB), 0)
    col = lax.broadcasted_iota(jnp.int32, (HB, HB), 1)
    return row, col


def _hgrn_gates(qr, z, t0, t1):
    lb = 1.0 / (1.0 + jnp.exp(t1 - t0))
    th = jnp.tanh(0.5 * z)
    sz, sneg = 0.5 + 0.5 * th, 0.5 - 0.5 * th
    f = lb + (1.0 - lb) * sz
    return lb, sz, sneg, f, jnp.log(f), (1.0 - lb) * sneg, qr * _sig(qr)


def _hgrn_scores(q, k, cum):
    heads, nsub = range(len(q)), HB // SUB
    mids = [[cum[h][SUB * i + SUB // 2 - 1:SUB * i + SUB // 2, :] for i in range(nsub)] for h in heads]
    cmid = [jnp.concatenate([cum[h][SUB * i:SUB * (i + 1)] - mids[h][i] for i in range(nsub)], axis=0) for h in heads]
    qd = [q[h] * jnp.exp(jnp.minimum(cmid[h], EXP_CLAMP)) for h in heads]
    qd_b = [x.astype(BF16) for x in qd]
    kds = [[None] * nsub for _ in heads]
    parts = [[None] * nsub for _ in heads]
    for i in range(nsub):
        for h in heads:
            kds[h][i] = k[h] * jnp.exp(jnp.minimum(mids[h][i] - cum[h], EXP_CLAMP))
        for h in heads:
            parts[h][i] = _dot_nt(qd_b[h][SUB * i:SUB * (i + 1)], kds[h][i])
    return qd, kds, [jnp.concatenate(parts[h], axis=0) for h in heads], cmid


def _hgrn_fwd(p_hg, table, gain, t):
    nblk = t // HB

    def body(q_ref, f_ref, i_ref, g_ref, tab_ref, gain_ref, o_ref, y_ref, st_ref, state):
        @pl.when(pl.program_id(0) == 0)
        def _():
            state[...] = jnp.zeros_like(state)

        row, col = _hgrn_mats()
        causal = col <= row
        tri = causal.astype(BF16)
        heads = range(NH)
        sl = [slice(HD * h, HD * (h + 1)) for h in heads]
        gates = [_hgrn_gates(q_ref[:, sl[h]], f_ref[:, sl[h]], tab_ref[0:1, sl[h]], tab_ref[1:2, sl[h]]) for h in heads]
        lf, k, q = [g[4] for g in gates], [g[5] for g in gates], [g[6] for g in gates]
        v = [i_ref[:, sl[h]] for h in heads]
        cum = [_dot_sel(tri, lf[h]) for h in heads]
        _, _, s, _ = _hgrn_scores(q, k, cum)
        p = [jnp.where(causal, s[h], 0.0) for h in heads]
        st = [state[h] for h in heads]
        o = [_dot(p[h], v[h]) + _dot_nt(q[h] * jnp.exp(cum[h]), st[h]) for h in heads]
        last = [cum[h][HB - 1:HB, :] for h in heads]
        new_st = [st[h] * jnp.exp(last[h]) + _dot_tn(v[h], k[h] * jnp.exp(last[h] - cum[h])) for h in heads]
        for h in heads:
            gr = g_ref[:, sl[h]]
            st_ref[h, 0] = st[h]
            state[h] = new_st[h]
            o_ref[:, sl[h]] = o[h]
            y_ref[:, sl[h]] = (_rms(o[h], gain_ref[...]) * gr * _sig(gr)).astype(BF16)

    blk = lambda cb: pl.BlockSpec((HB, HGW), functools.partial(lambda n, cb: (n, cb), cb=cb))
    return pl.pallas_call(
        body, name="hgrn_fwd", grid=(nblk,),
        in_specs=[blk(0), blk(1), blk(2), blk(3), pl.BlockSpec((2, HGW), lambda n: (0, 0)),
                  pl.BlockSpec((1, HD), lambda n: (0, 0))],
        out_specs=[pl.BlockSpec((HB, HGW), lambda n: (n, 0)), pl.BlockSpec((HB, HGW), lambda n: (n, 0)),
                   pl.BlockSpec((NH, 1, HD, HD), lambda n: (0, n, 0, 0))],
        out_shape=[jax.ShapeDtypeStruct((t, HGW), F32), jax.ShapeDtypeStruct((t, HGW), BF16),
                   jax.ShapeDtypeStruct((NH, nblk, HD, HD), F32)],
        scratch_shapes=[pltpu.VMEM((NH, HD, HD), F32)],
        compiler_params=_params(("arbitrary",)),
    )(p_hg, p_hg, p_hg, p_hg, table, gain)


def _hgrn_bwd(p_hg, table, gain, o_pre, states, dy, t):
    nblk = t // HB

    def body(q_ref, f_ref, i_ref, g_ref, tab_ref, gain_ref, o_ref, st_ref, dy_ref, dp_ref, dtab_ref, dgain_ref,
             dstate):
        @pl.when(pl.program_id(0) == 0)
        def _():
            dstate[...] = jnp.zeros_like(dstate)
            dtab_ref[...] = jnp.zeros_like(dtab_ref)
            dgain_ref[...] = jnp.zeros_like(dgain_ref)

        row, col = _hgrn_mats()
        causal = col <= row
        tri = causal.astype(BF16)
        tri_t = (row <= col).astype(BF16)
        heads, nsub = range(NH), HB // SUB
        sl = [slice(HD * h, HD * (h + 1)) for h in heads]
        rows_of = lambda i: slice(SUB * i, SUB * (i + 1))
        gain_ = gain_ref[...]
        qr, z = [q_ref[:, sl[h]] for h in heads], [f_ref[:, sl[h]] for h in heads]
        v, gr = [i_ref[:, sl[h]] for h in heads], [g_ref[:, sl[h]] for h in heads]
        gates = [_hgrn_gates(qr[h], z[h], tab_ref[0:1, sl[h]], tab_ref[1:2, sl[h]]) for h in heads]
        lb, sz, sneg, f, lf, k, q = ([g[j] for g in gates] for j in range(7))
        cum = [_dot_sel(tri, lf[h]) for h in heads]
        qd, kds, s, cmid = _hgrn_scores(q, k, cum)
        p = [jnp.where(causal, s[h], 0.0) for h in heads]
        st, dst = [st_ref[h, 0] for h in heads], [dstate[h] for h in heads]
        o, dyh, sg = [o_ref[:, sl[h]] for h in heads], [dy_ref[:, sl[h]] for h in heads], [_sig(x) for x in gr]
        dgr = [dyh[h] * _rms(o[h], gain_) * sg[h] * (1.0 + gr[h] * (1.0 - sg[h])) for h in heads]
        norm_bwd = [_rms_bwd(o[h], gain_, dyh[h] * gr[h] * sg[h]) for h in heads]
        do = [x[0] for x in norm_bwd]
        do_b = [x.astype(BF16) for x in do]
        ecum, last = [jnp.exp(x) for x in cum], [x[HB - 1:HB, :] for x in cum]
        qc = [q[h] * ecum[h] for h in heads]
        edec = [jnp.exp(last[h] - cum[h]) for h in heads]
        kdec = [k[h] * edec[h] for h in heads]
        dp = [jnp.where(causal, _dot_nt(do_b[h], v[h]), 0.0) for h in heads]
        dv = [_dot(p[h].T, do_b[h]) + _dot_nt(kdec[h], dst[h]) for h in heads]
        dqc = [_dot(do_b[h], st[h]) for h in heads]
        dkdec = [_dot(v[h], dst[h]) for h in heads]
        new_dst = [dst[h] * jnp.exp(last[h]) + _dot(do[h].T, qc[h]) for h in heads]
        dp_b = [x.astype(BF16) for x in dp]
        dqd = [jnp.concatenate(parts, axis=0) for parts in
               zip(*[[_dot(dp_b[h][rows_of(i)], kds[h][i]) for h in heads] for i in range(nsub)])]
        gq = [dqd[h] * qd[h] for h in heads]
        dq = [dqd[h] * jnp.exp(jnp.minimum(cmid[h], EXP_CLAMP)) + dqc[h] * ecum[h] for h in heads]
        gs = [dkdec[h] * kdec[h] for h in heads]
        dk = [dkdec[h] * edec[h] for h in heads]
        dcum = [dqc[h] * qc[h] - gs[h] + gq[h]
                + jnp.where(row == HB - 1, jnp.sum(gs[h], axis=0, keepdims=True)
                            + jnp.exp(last[h]) * jnp.sum(st[h] * dst[h], axis=0, keepdims=True), 0.0) for h in heads]
        qd_b = [x.astype(BF16) for x in qd]
        for i in range(nsub):
            dkd = [_dot_tn(dp_b[h][rows_of(i)], qd_b[h][rows_of(i)]) for h in heads]
            for h in heads:
                mid = cum[h][SUB * i + SUB // 2 - 1:SUB * i + SUB // 2, :]
                dk[h] = dk[h] + dkd[h] * jnp.exp(jnp.minimum(mid - cum[h], EXP_CLAMP))
                gk = dkd[h] * kds[h][i]
                to_mid = jnp.sum(gk, axis=0, keepdims=True) - jnp.sum(gq[h][rows_of(i)], axis=0, keepdims=True)
                dcum[h] = dcum[h] - gk + jnp.where(row == SUB * i + SUB // 2 - 1, to_mid, 0.0)
        dlf = [_dot_sel(tri_t, dcum[h]) for h in heads]
        dgain = jnp.zeros((1, HD), F32)
        for h in heads:
            df = dlf[h] / f[h] - dk[h]
            dlb = jnp.sum(df * sneg[h], axis=0, keepdims=True) * lb[h] * (1.0 - lb[h])
            dtab_ref[0:1, sl[h]] += dlb
            dtab_ref[1:2, sl[h]] -= dlb
            sq = _sig(qr[h])
            dstate[h] = new_dst[h]
            dp_ref[:, sl[h]] = (dq[h] * sq * (1.0 + qr[h] * (1.0 - sq))).astype(BF16)
            dp_ref[:, HGW + HD * h:HGW + HD * (h + 1)] = (df * (1.0 - lb[h]) * sz[h] * sneg[h]).astype(BF16)
            dp_ref[:, 2 * HGW + HD * h:2 * HGW + HD * (h + 1)] = dv[h].astype(BF16)
            dp_ref[:, 3 * HGW + HD * h:3 * HGW + HD * (h + 1)] = dgr[h].astype(BF16)
            dgain = dgain + norm_bwd[h][1]
        dgain_ref[...] += dgain

    rev = lambda cb: pl.BlockSpec((HB, HGW), functools.partial(lambda n, cb: (nblk - 1 - n, cb), cb=cb))
    return pl.pallas_call(
        body, name="hgrn_bwd", grid=(nblk,),
        in_specs=[rev(0), rev(1), rev(2), rev(3), pl.BlockSpec((2, HGW), lambda n: (0, 0)),
                  pl.BlockSpec((1, HD), lambda n: (0, 0)), rev(0),
                  pl.BlockSpec((NH, 1, HD, HD), lambda n: (0, nblk - 1 - n, 0, 0)), rev(0)],
        out_specs=[pl.BlockSpec((HB, 4 * HGW), lambda n: (nblk - 1 - n, 0)),
                   pl.BlockSpec((2, HGW), lambda n: (0, 0)), pl.BlockSpec((1, HD), lambda n: (0, 0))],
        out_shape=[jax.ShapeDtypeStruct((t, 4 * HGW), BF16), jax.ShapeDtypeStruct((2, HGW), F32),
                   jax.ShapeDtypeStruct((1, HD), F32)],
        scratch_shapes=[pltpu.VMEM((NH, HD, HD), F32)],
        compiler_params=_params(("arbitrary",)),
    )(p_hg, p_hg, p_hg, p_hg, table, gain, o_pre, states, dy)


def _rope_mat():
    r = lax.broadcasted_iota(jnp.int32, (ROPE, ROPE), 0)
    c = lax.broadcasted_iota(jnp.int32, (ROPE, ROPE), 1)
    half = ROPE // 2
    return ((r == c - half).astype(F32) - (r == c + half).astype(F32)).astype(BF16)


def _mla_prep_fwd(p_mla, cs, sn, wq, wkv, gql, gkvl, gq, gk, t):
    tm = _pick(t, (512, 256, 128))

    def body(p_ref, cs_ref, sn_ref, wq_ref, wkv_ref, gql_ref, gkvl_ref, gq_ref, gk_ref,
             q_ref, k_ref, v_ref):
        rmat = _rope_mat()
        cqn = _rms(p_ref[:, 0:QL], gql_ref[...]).astype(BF16)
        ckvn = _rms(p_ref[:, QL:QL + KVL], gkvl_ref[...]).astype(BF16)
        kpe = p_ref[:, QL + KVL:QL + KVL + ROPE]
        c, s = cs_ref[...], sn_ref[...]
        rot = lambda x: x * c + _sel_dot(x, rmat) * s
        heads = range(NH)
        ssq = lambda x: jnp.sum(x * x, -1, keepdims=True)
        qa = [_dot(cqn, wq_ref[h, :, 0:HD]) for h in heads]
        qr = [_dot(cqn, wq_ref[h, :, HD:QK]) for h in heads]
        kn = [_dot(ckvn, wkv_ref[h, :, 0:HD]) for h in heads]
        vv = [_dot(ckvn, wkv_ref[h, :, HD:2 * HD]) for h in heads]
        kpe_ss = ssq(kpe)
        rq = [lax.rsqrt((ssq(qa[h]) + ssq(qr[h])) / QK + EPS) for h in heads]
        rk = [lax.rsqrt((ssq(kn[h]) + kpe_ss) / QK + EPS) for h in heads]
        q_rope = [rot(qr[h] * rq[h] * gq_ref[:, HD:QK]) for h in heads]
        k_rope = [rot(kpe * rk[h] * gk_ref[:, HD:QK]) for h in heads]
        for h in heads:
            q_ref[h, :, 0:HD] = (qa[h] * rq[h] * gq_ref[:, 0:HD] * (SCALE * LOG2E)).astype(BF16)
            q_ref[h, :, HD:QK] = (q_rope[h] * (SCALE * LOG2E)).astype(BF16)
            k_ref[h, :, 0:HD] = (kn[h] * rk[h] * gk_ref[:, 0:HD]).astype(BF16)
            k_ref[h, :, HD:QK] = k_rope[h].astype(BF16)
            v_ref[h] = vv[h].astype(BF16)

    whole = lambda a: pl.BlockSpec(a.shape, functools.partial(lambda i, nd: (0,) * nd, nd=a.ndim))
    return pl.pallas_call(
        body, name="mla_prep_fwd", grid=(t // tm,),
        in_specs=[pl.BlockSpec((tm, QL + KVL + ROPE), lambda i: (i, 0)), pl.BlockSpec((tm, ROPE), lambda i: (i, 0)),
                  pl.BlockSpec((tm, ROPE), lambda i: (i, 0))] + [whole(a) for a in (wq, wkv, gql, gkvl, gq, gk)],
        out_specs=[pl.BlockSpec((NH, tm, QK), lambda i: (0, i, 0)), pl.BlockSpec((NH, tm, QK), lambda i: (0, i, 0)),
                   pl.BlockSpec((NH, tm, HD), lambda i: (0, i, 0))],
        out_shape=[jax.ShapeDtypeStruct((NH, t, QK), BF16), jax.ShapeDtypeStruct((NH, t, QK), BF16),
                   jax.ShapeDtypeStruct((NH, t, HD), BF16)],
        compiler_params=_params(("parallel",)),
    )(p_mla, cs, sn, wq, wkv, gql, gkvl, gq, gk)


def _mla_prep_bwd(p_mla, cs, sn, wq, wkv, gql, gkvl, gq, gk, dq, dk, dv, t):
    tm = _pick(t, (512, 256, 128))

    def body(p_ref, cs_ref, sn_ref, wq_ref, wkv_ref, gql_ref, gkvl_ref, gq_ref, gk_ref,
             dq_ref, dk_ref, dv_ref,
             dp_ref, dwq_ref, dwkv_ref, dgql_ref, dgkvl_ref, dgq_ref, dgk_ref):
        accs = (dwq_ref, dwkv_ref, dgql_ref, dgkvl_ref, dgq_ref, dgk_ref)

        @pl.when(pl.program_id(0) == 0)
        def _():
            for r in accs:
                r[...] = jnp.zeros_like(r)

        rmat = _rope_mat()
        rmat_t = -rmat
        cq, ckv = p_ref[:, 0:QL], p_ref[:, QL:QL + KVL]
        kpe = p_ref[:, QL + KVL:QL + KVL + ROPE]
        cqn_f, ckvn_f = _rms(cq, gql_ref[...]), _rms(ckv, gkvl_ref[...])
        cqn, ckvn = cqn_f.astype(BF16), ckvn_f.astype(BF16)
        ckvn_t = ckvn_f.T.astype(BF16)
        c, s = cs_ref[...], sn_ref[...]
        unrot = lambda dy: dy * c + _sel_dot(dy * s, rmat_t)
        heads = range(NH)
        rsum = lambda x: jnp.sum(x, -1, keepdims=True)
        csum = lambda x: jnp.sum(x, axis=0, keepdims=True)
        qa = [_dot(cqn, wq_ref[h, :, 0:HD]) for h in heads]
        qr = [_dot(cqn, wq_ref[h, :, HD:QK]) for h in heads]
        kn = [_dot(ckvn, wkv_ref[h, :, 0:HD]) for h in heads]
        dyr = [unrot(dq_ref[h, :, HD:QK] * SCALE) for h in heads]
        dkr = [unrot(dk_ref[h, :, HD:QK]) for h in heads]
        rq = [lax.rsqrt((rsum(qa[h] * qa[h]) + rsum(qr[h] * qr[h])) / QK + EPS) for h in heads]
        xa, xr = [qa[h] * rq[h] for h in heads], [qr[h] * rq[h] for h in heads]
        dya = [dq_ref[h, :, 0:HD] * SCALE for h in heads]
        ga, gr_ = [dya[h] * gq_ref[:, 0:HD] for h in heads], [dyr[h] * gq_ref[:, HD:QK] for h in heads]
        mean = [(rsum(ga[h] * xa[h]) + rsum(gr_[h] * xr[h])) / QK for h in heads]
        dqa = [(rq[h] * (ga[h] - xa[h] * mean[h])).astype(BF16) for h in heads]
        dqr = [(rq[h] * (gr_[h] - xr[h] * mean[h])).astype(BF16) for h in heads]
        kpe_ss = rsum(kpe * kpe)
        rk = [lax.rsqrt((rsum(kn[h] * kn[h]) + kpe_ss) / QK + EPS) for h in heads]
        ya, yr = [kn[h] * rk[h] for h in heads], [kpe * rk[h] for h in heads]
        dka = [dk_ref[h, :, 0:HD] for h in heads]
        ha, hr = [dka[h] * gk_ref[:, 0:HD] for h in heads], [dkr[h] * gk_ref[:, HD:QK] for h in heads]
        mean_k = [(rsum(ha[h] * ya[h]) + rsum(hr[h] * yr[h])) / QK for h in heads]
        dkn = [(rk[h] * (ha[h] - ya[h] * mean_k[h])).astype(BF16) for h in heads]
        dvh = [dv_ref[h].astype(BF16) for h in heads]
        dw = [(_dot_tn(dqa[h], cqn), _dot_tn(dqr[h], cqn), _dot(ckvn_t, dkn[h]), _dot(ckvn_t, dvh[h])) for h in heads]
        back_q = [_dot_nt(dqa[h], wq_ref[h, :, 0:HD]) + _dot_nt(dqr[h], wq_ref[h, :, HD:QK]) for h in heads]
        back_kv = [_dot_nt(dkn[h], wkv_ref[h, :, 0:HD]) + _dot_nt(dvh[h], wkv_ref[h, :, HD:2 * HD]) for h in heads]
        dcqn, dckvn = sum(back_q[1:], back_q[0]), sum(back_kv[1:], back_kv[0])
        dkpe = sum([rk[h] * (hr[h] - yr[h] * mean_k[h]) for h in heads][1:], rk[0] * (hr[0] - yr[0] * mean_k[0]))
        dgq_a = sum([csum(dya[h] * xa[h]) for h in heads][1:], csum(dya[0] * xa[0]))
        dgq_r = sum([csum(dyr[h] * xr[h]) for h in heads][1:], csum(dyr[0] * xr[0]))
        dgk_a = sum([csum(dka[h] * ya[h]) for h in heads][1:], csum(dka[0] * ya[0]))
        dgk_r = sum([csum(dkr[h] * yr[h]) for h in heads][1:], csum(dkr[0] * yr[0]))
        for h in heads:
            dwq_ref[h, 0:HD, :] += dw[h][0]
            dwq_ref[h, HD:QK, :] += dw[h][1]
            dwkv_ref[h, :, 0:HD] += dw[h][2]
            dwkv_ref[h, :, HD:2 * HD] += dw[h][3]
        dcq, dg1 = _rms_bwd(cq, gql_ref[...], dcqn)
        dckv, dg2 = _rms_bwd(ckv, gkvl_ref[...], dckvn)
        dp_ref[:, 0:QL] = dcq.astype(BF16)
        dp_ref[:, QL:QL + KVL] = dckv.astype(BF16)
        dp_ref[:, QL + KVL:QL + KVL + ROPE] = dkpe.astype(BF16)
        dgql_ref[...] += dg1
        dgkvl_ref[...] += dg2
        dgq_ref[:, 0:HD] += dgq_a
        dgq_ref[:, HD:QK] += dgq_r
        dgk_ref[:, 0:HD] += dgk_a
        dgk_ref[:, HD:QK] += dgk_r

    whole = lambda a: pl.BlockSpec(a.shape, functools.partial(lambda i, nd: (0,) * nd, nd=a.ndim))
    acc_shapes = [(NH, QK, QL), wkv.shape, gql.shape, gkvl.shape, gq.shape, gk.shape]
    return pl.pallas_call(
        body, name="mla_prep_bwd", grid=(t // tm,),
        in_specs=[pl.BlockSpec((tm, QL + KVL + ROPE), lambda i: (i, 0)), pl.BlockSpec((tm, ROPE), lambda i: (i, 0)),
                  pl.BlockSpec((tm, ROPE), lambda i: (i, 0))]
        + [whole(a) for a in (wq, wkv, gql, gkvl, gq, gk)]
        + [pl.BlockSpec((NH, tm, QK), lambda i: (0, i, 0)), pl.BlockSpec((NH, tm, QK), lambda i: (0, i, 0)),
           pl.BlockSpec((NH, tm, HD), lambda i: (0, i, 0))],
        out_specs=[pl.BlockSpec((tm, QL + KVL + ROPE), lambda i: (i, 0))]
        + [pl.BlockSpec(s, functools.partial(lambda i, nd: (0,) * nd, nd=len(s))) for s in acc_shapes],
        out_shape=[jax.ShapeDtypeStruct((t, QL + KVL + ROPE), BF16)]
        + [jax.ShapeDtypeStruct(s, F32) for s in acc_shapes],
        compiler_params=_params(("arbitrary",)),
    )(p_mla, cs, sn, wq, wkv, gql, gkvl, gq, gk, dq, dk, dv)


def _chunk_mask(nq, nk, key0, keys_on_rows):
    shape = (nk, nq) if keys_on_rows else (nq, nk)
    qi = lax.broadcasted_iota(jnp.int32, shape, 1 if keys_on_rows else 0) // CHUNK
    ki = lax.broadcasted_iota(jnp.int32, shape, 0 if keys_on_rows else 1) // CHUNK + key0 // CHUNK
    return ki <= qi


def _flash_fwd(q, k, v, t):
    tq = _pick(t, ATT_Q)
    tk = tq // ATT_KEY_TILES

    def body(q_ref, k_ref, v_ref, o_ref, lse_ref):
        i = pl.program_id(1)
        qt = q_ref[0]

        def step(j, carry, key0):
            m, l, acc = carry
            cols = pl.ds(pl.multiple_of(j * tk, tk), tk)
            s = _dot_nt(qt, k_ref[0, cols, :])
            if key0 is not None:
                s = jnp.where(_chunk_mask(tq, tk, key0, False), s, -jnp.inf)
            m_new = jnp.maximum(m, jnp.max(s, axis=-1, keepdims=True))
            p = jnp.exp2(s - m_new)
            alpha = jnp.exp2(m - m_new)
            return m_new, alpha * l + jnp.sum(p, axis=-1, keepdims=True), alpha * acc + _dot(p, v_ref[0, cols, :])

        init = (jnp.full((tq, 1), -jnp.inf, F32), jnp.zeros((tq, 1), F32), jnp.zeros((tq, HD), F32))
        carry = lax.fori_loop(0, ATT_KEY_TILES * i, lambda j, cr: step(j, cr, None), init)
        for h in range(ATT_KEY_TILES):
            carry = step(ATT_KEY_TILES * i + h, carry, h * tk)
        m, l, acc = carry
        o_ref[...] = (acc / l).astype(BF16)
        x1, x2, x3 = _split3(jnp.broadcast_to((m + jnp.log2(l)) * (1.0 / HD), (tq, HD)))
        ones = jnp.ones((SUBLANE, HD), BF16)
        rowsum = lambda p: lax.dot_general(ones, p, (((1,), (1,)), ((), ())), preferred_element_type=F32)
        lse_ref[0] = (rowsum(x1) + rowsum(x2) + rowsum(x3))[0:1, :]

    return pl.pallas_call(
        body, name="flash_fwd", grid=(NH, t // tq),
        in_specs=[pl.BlockSpec((1, tq, QK), lambda h, i: (h, i, 0)), pl.BlockSpec((1, t, QK), lambda h, i: (h, 0, 0)),
                  pl.BlockSpec((1, t, HD), lambda h, i: (h, 0, 0))],
        out_specs=[pl.BlockSpec((tq, HD), lambda h, i: (i, h)), pl.BlockSpec((1, 1, tq), lambda h, i: (h, 0, i))],
        out_shape=[jax.ShapeDtypeStruct((t, NH * HD), BF16), jax.ShapeDtypeStruct((NH, 1, t), F32)],
        compiler_params=_params(("parallel", "parallel")),
    )(q, k, v)


def _attn_out_bwd(dy, w, o, t):
    tm = _pick(t, ATT_TILES)

    def body(dy_ref, w_ref, o_ref, do_ref, delta_ref):
        do = _dot_nt(dy_ref[...], w_ref[...]).astype(BF16)
        do_ref[...] = do
        ones = jnp.ones((SUBLANE, HD), BF16)
        rowsum = lambda p: lax.dot_general(ones, p, (((1,), (1,)), ((), ())), preferred_element_type=F32)
        sl = [slice(HD * h, HD * (h + 1)) for h in range(NH)]
        parts = [_split3(do[:, sl[h]].astype(F32) * o_ref[:, sl[h]].astype(F32)) for h in range(NH)]
        sums = [[rowsum(x) for x in parts[h]] for h in range(NH)]
        for h in range(NH):
            delta_ref[h] = (sums[h][0] + sums[h][1] + sums[h][2])[0:1, :]

    return pl.pallas_call(
        body, name="attn_out_bwd", grid=(t // tm,),
        in_specs=[_BS((tm, dy.shape[1]), lambda i: (i, 0)), _BS(w.shape, lambda i: (0, 0)),
                  _BS((tm, NH * HD), lambda i: (i, 0))],
        out_specs=[_BS((tm, NH * HD), lambda i: (i, 0)), _BS((NH, 1, tm), lambda i: (0, 0, i))],
        out_shape=[jax.ShapeDtypeStruct((t, NH * HD), BF16), jax.ShapeDtypeStruct((NH, 1, t), F32)],
        compiler_params=_params(("parallel",)),
    )(dy, w, o)


def _flash_bwd(q, k, v, do, lse_row, delta_row, t):
    tq = _pick(t, ATT_Q)
    tk = tq // ATT_KEY_TILES

    def body(q_ref, k_ref, v_ref, do_ref, lse_ref, delta_ref, dq_ref, dk_ref, dv_ref):
        j = pl.program_id(1)

        @pl.when(j == 0)
        def _():
            dq_ref[...] = jnp.zeros_like(dq_ref)

        kt, vt = k_ref[0], v_ref[0]

        def step(i, carry, key0):
            dk, dv = carry
            rows = pl.ds(pl.multiple_of(i * tq, tq), tq)
            qt, dot_ = q_ref[0, rows, :], do_ref[rows, :]
            p = jnp.exp2(_dot_nt(kt, qt) - lse_ref[0, :, rows])
            if key0 is not None:
                p = jnp.where(_chunk_mask(tq, tk, key0, True), p, 0.0)
            ds = (p * (_dot_nt(vt, dot_) - delta_ref[0, :, rows])).astype(BF16)
            dq_ref[0, rows, :] += _dot_tn(ds, kt)
            return dk + _dot(ds, qt), dv + _dot(p, dot_)

        own = j // ATT_KEY_TILES
        carry = step(own, (jnp.zeros((tk, QK), F32), jnp.zeros((tk, HD), F32)), (j % ATT_KEY_TILES) * tk)
        dk, dv = lax.fori_loop(own + 1, t // tq, lambda i, cr: step(i, cr, None), carry)
        dk_ref[0] = dk * (1.0 / LOG2E)
        dv_ref[0] = dv

    return pl.pallas_call(
        body, name="flash_bwd", grid=(NH, t // tk),
        in_specs=[pl.BlockSpec((1, t, QK), lambda h, j: (h, 0, 0)), pl.BlockSpec((1, tk, QK), lambda h, j: (h, j, 0)),
                  pl.BlockSpec((1, tk, HD), lambda h, j: (h, j, 0)), pl.BlockSpec((t, HD), lambda h, j: (0, h)),
                  pl.BlockSpec((1, 1, t), lambda h, j: (h, 0, 0)), pl.BlockSpec((1, 1, t), lambda h, j: (h, 0, 0))],
        out_specs=[pl.BlockSpec((1, t, QK), lambda h, j: (h, 0, 0)), pl.BlockSpec((1, tk, QK), lambda h, j: (h, j, 0)),
                   pl.BlockSpec((1, tk, HD), lambda h, j: (h, j, 0))],
        out_shape=[jax.ShapeDtypeStruct((NH, t, QK), F32), jax.ShapeDtypeStruct((NH, t, QK), F32),
                   jax.ShapeDtypeStruct((NH, t, HD), F32)],
        compiler_params=_params(("parallel", "arbitrary")),
    )(q, k, v, do, lse_row, delta_row)


def _ffn_in(xn, w_in, name):
    t, k = xn.shape
    s, _, n = w_in.shape
    tm = _pick(t, ROW_TILES)

    def body(x_ref, w_ref, gu_ref, a_ref):
        x = x_ref[...]
        g, u = _dot(x, w_ref[0]), _dot(x, w_ref[1])
        gu_ref[0] = g.astype(BF16)
        gu_ref[1] = u.astype(BF16)
        a_ref[...] = (g * _sig(g) * u).astype(BF16)

    return pl.pallas_call(
        body, name=name, grid=(t // tm, s // 2),
        in_specs=[_BS((tm, k), lambda i, j: (i, 0)), _BS((2, None, k, n), lambda i, j: (0, j, 0, 0))],
        out_specs=[_BS((2, None, tm, n), lambda i, j: (0, j, i, 0)), _BS((None, tm, n), lambda i, j: (j, i, 0))],
        out_shape=[jax.ShapeDtypeStruct((2, s // 2, t, n), BF16), jax.ShapeDtypeStruct((s // 2, t, n), BF16)],
        compiler_params=_params(("parallel", "parallel")),
    )(xn, w_in.reshape(2, s // 2, k, n))


def _ffn_dgu(dfo, w_out, gu, name):
    t, k = dfo.shape
    s, n, _ = w_out.shape
    tm = _pick(t, ROW_TILES)

    def body(d_ref, w_ref, gu_ref, o_ref):
        da = _dot_nt(d_ref[...], w_ref[...])
        g, u = gu_ref[0].astype(F32), gu_ref[1].astype(F32)
        sg = _sig(g)
        o_ref[0] = (da * u * sg * (1.0 + g * (1.0 - sg))).astype(BF16)
        o_ref[1] = (da * g * sg).astype(BF16)

    pair = _BS((2, None, tm, n), lambda i, j: (0, j, i, 0))
    return pl.pallas_call(
        body, name=name, grid=(t // tm, s),
        in_specs=[_BS((tm, k), lambda i, j: (i, 0)), _BS((None, n, k), lambda i, j: (j, 0, 0)), pair],
        out_specs=pair, out_shape=jax.ShapeDtypeStruct((2, s, t, n), BF16),
        compiler_params=_params(("parallel", "parallel")),
    )(dfo, w_out, gu).reshape(2 * s, t, n)


def _ffn_fwd(xn, w_in, w_out, tag, epilogue):
    gu, a = _ffn_in(xn, w_in, tag + "_in")
    return gu, a, _mm_stack_red(a, w_out, tag + "_out", "nn", epilogue)


def _ffn_bwd(dfo, xn, gu, a, w_in, w_out, tag, epilogue):
    dw_out = _mm_stack_tn(a, dfo, tag + "_dwout")
    dgu = _ffn_dgu(dfo, w_out, gu, tag + "_dgu")
    dw_in = _mm_stack_tn(dgu, xn, tag + "_dwin")
    if callable(epilogue):
        epilogue = epilogue(dw_in, dw_out)
    return _mm_stack_red(dgu, w_in, tag + "_dxn", "nt", epilogue), dw_in, dw_out


def _local_step(x, target, cs, sn, w, late_weights, s, early_grads, last_grads):
    t = x.shape[0]
    tm = _pick(t, (256, 128))
    g = {}
    ffn_out = lambda n: w[n].reshape(4, FF // 4, D)

    def norm_fn(x_ref, g_ref, o_ref):
        o_ref[...] = _rms(x_ref[...], g_ref[...]).astype(BF16)

    xn1 = _rows(norm_fn, "norm1", t, tm, [(x, D, 0)], [s["ffn1_norm"]], [(D, BF16)])[0]
    def res_norm_fn(scale):
        def fn(h_ref, f_ref, g_ref, h_out, n_out):
            h = h_ref[...] + scale * f_ref[...]
            h_out[...] = h
            n_out[...] = _rms(h, g_ref[...]).astype(BF16)
        return fn

    gu1, a1, (h1, u) = _ffn_fwd(xn1, w["ffn1_w_in"], ffn_out("ffn1_w_out"), "ffn1",
                                (res_norm_fn(0.5), x, [], [s["mix_norm"]], [(D, F32), (D, BF16)], []))
    w = {**w, **late_weights(h1)}
    rows_of = lambda n: w[n].reshape(-1, w[n].shape[2])
    w_hgb, w_mlab, w_o = rows_of("w_hg_branch"), rows_of("w_mla_branch"), rows_of("w_out")
    w_in_nat = w["w_in"].transpose(1, 0, 2).reshape(D, -1)
    w_mrg = w["w_merge"]
    mw = w_mrg.shape[2]
    w_in_hg, w_in_mla = w_in_nat[:, :4 * HGW], w_in_nat[:, 4 * HGW:]
    p_hg = _mm(u, w_in_hg, "nn", "proj_hg")
    p_mla = _mm(u, w_in_mla, "nn", "proj_mla")
    gpre = _cols_fwd(u, w_mrg, "proj_gate")
    o_pre, hgy, states = _hgrn_fwd(p_hg, s["hg_lb_table"], s["hg_out_norm"], t)
    prep_args = (p_mla, cs, sn, w["w_q_up"], w["w_kv_up"], s["mla_q_lora_norm"], s["mla_kv_lora_norm"],
                 s["q_head_norm"], s["k_head_norm"])
    q, k, v = _mla_prep_fwd(*prep_args, t)
    att, lse = _flash_fwd(q, k, v, t)
    y_hg = _mm(hgy, w_hgb, "nn", "branch_hg")
    y_mla = _mm(att, w_mlab, "nn", "branch_mla")

    def mix_fn(gh_ref, gm_ref, yh_ref, ym_ref, b_ref, o_ref):
        gh = _sig(gh_ref[...] + b_ref[:, 0:D])
        gm = _sig(gm_ref[...] + b_ref[:, D:2 * D])
        o_ref[...] = (gh * yh_ref[...] + gm * ym_ref[...]).astype(BF16)

    mixed = _rows(mix_fn, "mix", t, tm, [(gpre, D, 0), (gpre, D, 1), (y_hg, D, 0), (y_mla, D, 0)], [s["b_merge"]],
                  [(D, BF16)])[0]
    h2, xn2 = _mm_stack_red(mixed[None], w_o[None], "mix_out", "nn",
                            (res_norm_fn(1.0), h1, [], [s["ffn2_norm"]], [(D, F32), (D, BF16)], []))

    def loss_fn(h_ref, f_ref, tg_ref, g_ref, dh_out, dhb_out):
        h = h_ref[...] + 0.5 * f_ref[...]
        e = _rms(h, g_ref[...]) - tg_ref[...]
        dh, dgain = _rms_bwd(h, g_ref[...], e / D)
        dh_out[...] = dh
        dhb_out[...] = (0.5 * dh).astype(BF16)
        return dgain, jnp.full((1, LANE), 0.5 / D * jnp.sum(e * e), F32)

    gu2, a2, (dh3, dfo2, g["final_norm"], loss) = _ffn_fwd(
        xn2, w["ffn2_w_in"], ffn_out("ffn2_w_out"), "ffn2",
        (loss_fn, h2, [target], [s["final_norm"]], [(D, F32), (D, BF16)], [(1, D), (1, LANE)]))

    def norm_bwd_fn(scale):
        def fn(h_ref, dxn_ref, dh_ref, g_ref, dh_out, dhb_out):
            dx, dgain = _rms_bwd(h_ref[...], g_ref[...], dxn_ref[...])
            dh = dh_ref[...] + dx
            dh_out[...] = dh
            dhb_out[...] = (scale * dh).astype(BF16)
            return (dgain,)
        return fn

    as_rows = lambda a: a.reshape((N_DEV, -1) + a.shape[-1:])
    (dh2, dh2b, g["ffn2_norm"]), g["ffn2_w_in"], dwo = _ffn_bwd(
        dfo2, xn2, gu2, a2, w["ffn2_w_in"], ffn_out("ffn2_w_out"), "ffn2",
        (norm_bwd_fn(1.0), h2, [dh3], [s["ffn2_norm"]], [(D, F32), (D, BF16)], [(1, D)]))
    g["ffn2_w_out"] = as_rows(dwo)
    g["w_out"] = as_rows(_mm(mixed, dh2b, "tn", "mix_out_dw"))

    def mix_bwd_fn(gp_ref, dm_ref, yh_ref, ym_ref, b_ref, dyh_out, dym_out, dg_out):
        gh = _sig(gp_ref[:, 0:D] + b_ref[:, 0:D])
        gm = _sig(gp_ref[:, D:2 * D] + b_ref[:, D:2 * D])
        dm = dm_ref[...]
        dyh_out[...] = (dm * gh).astype(BF16)
        dym_out[...] = (dm * gm).astype(BF16)
        dgh = dm * yh_ref[...] * gh * (1.0 - gh)
        dgm = dm * ym_ref[...] * gm * (1.0 - gm)
        dg_out[:, 0:D] = dgh.astype(BF16)
        dg_out[:, D:2 * D] = dgm.astype(BF16)
        return (jnp.concatenate([jnp.sum(dgh, axis=0, keepdims=True), jnp.sum(dgm, axis=0, keepdims=True)], axis=1),)

    dyh, dym, dgpre, g["b_merge"] = _mm_stack_red(
        dh2b[None], w_o[None], "mix_out_dx", "nt",
        (mix_bwd_fn, gpre, [y_hg, y_mla], [s["b_merge"]], [(D, BF16), (D, BF16), (2 * D, BF16)], [(1, 2 * D)],
         (512, 256, 128)))
    g["w_hg_branch"] = as_rows(_mm(hgy, dyh, "tn", "branch_hg_dw"))
    g["w_mla_branch"] = as_rows(_mm(att, dym, "tn", "branch_mla_dw"))
    g["w_merge"] = _cols_dw(u, dgpre, mw, "proj_gate_dw")
    dhgy = _mm(dyh, w_hgb, "nt", "branch_hg_dx")
    datt, delta = _attn_out_bwd(dym, w_mlab, att, t)
    du_gate = _cols_dx(dgpre, w_mrg, "proj_gate_dx")

    dq, dk, dv = _flash_bwd(q, k, v, datt, lse, delta, t)
    (dp_mla, dwq, dwkv, g["mla_q_lora_norm"], g["mla_kv_lora_norm"], g["q_head_norm"],
     g["k_head_norm"]) = _mla_prep_bwd(*prep_args, dq, dk, dv, t)
    g["w_q_up"], g["w_kv_up"] = dwq.astype(GRAD_DT), dwkv.astype(GRAD_DT)
    dp_hg, g["hg_lb_table"], g["hg_out_norm"] = _hgrn_bwd(p_hg, s["hg_lb_table"], s["hg_out_norm"], o_pre, states,
                                                          dhgy, t)
    dw_in_nat = jnp.concatenate([_mm(u, dp_hg, "tn", "proj_hg_dw"), _mm(u, dp_mla, "tn", "proj_mla_dw")], axis=1)
    g["w_in"] = dw_in_nat.T.reshape(N_DEV, -1, D)
    du_mla = _mm(dp_mla, w_in_mla, "nt", "proj_mla_dx")

    def mixnorm_bwd_fn(h_ref, a_ref, b_ref, c_ref, dh_ref, g_ref, dh_out, dhb_out):
        dx, dgain = _rms_bwd(h_ref[...], g_ref[...], a_ref[...] + b_ref[...] + c_ref[...])
        dh = dh_ref[...] + dx
        dh_out[...] = dh
        dhb_out[...] = (0.5 * dh).astype(BF16)
        return (dgain,)

    mix_gain = s["mix_norm"] + early_grads(g)[0:1, 0:1]
    dh1, dfo1, g["mix_norm"] = _mm_stack_red(
        dp_hg[None], w_in_hg[None], "proj_hg_dx", "nt",
        (mixnorm_bwd_fn, h1, [du_mla, du_gate, dh2], [mix_gain], [(D, F32), (D, BF16)], [(1, D)], (512, 256, 128)))
    def last_stage(dw_in, dw_out):
        g["ffn1_w_in"], g["ffn1_w_out"] = dw_in, as_rows(dw_out)
        gain = s["ffn1_norm"] + last_grads(g)[0:1, 0:1]
        return norm_bwd_fn(1.0), x, [dh1], [gain], [(D, F32), (D, BF16)], [(1, D)]

    (grad_x, _, g["ffn1_norm"]), _, _ = _ffn_bwd(dfo1, xn1, gu1, a1, w["ffn1_w_in"], ffn_out("ffn1_w_out"), "ffn1",
                                                 last_stage)
    return loss, grad_x, g


def _coords():
    return lax.axis_index("x"), lax.axis_index("y"), lax.axis_index("c")


def _hbm_call(body, name, ins, out_shapes, scratch):
    any_spec = pl.BlockSpec(memory_space=pl.ANY)
    return pl.pallas_call(
        body, name=name, out_shape=[jax.ShapeDtypeStruct(s, dt) for s, dt in out_shapes],
        in_specs=[any_spec] * len(ins), out_specs=[any_spec] * len(out_shapes), scratch_shapes=scratch,
    )(*ins)


def _my_slot():
    return 4 * lax.axis_index("x") + 2 * lax.axis_index("y") + lax.axis_index("c")


def _put_own(buf, own, index):
    return lax.dynamic_update_index_in_dim(buf, own, index, 0)


def _all_gather(blocks, name):
    nb = len(blocks)

    def body(*refs):
        x_refs, out_refs = refs[:nb], refs[nb:2 * nb]
        send_sems, recv_sems = refs[2 * nb:]
        x, y, c = _coords()
        me, sibling = (x, y, c), (x, y, 1 - c)
        chips = [(1 - x, y), (x, 1 - y), (1 - x, 1 - y)]

        def slot(b, px, py, pc):
            return out_refs[b].at[4 * px + 2 * py + pc]

        def copy(b, kk, block_of, to, src=None):
            return pltpu.make_async_remote_copy(
                src_ref=slot(b, *block_of) if src is None else src, dst_ref=slot(b, *block_of),
                send_sem=send_sems.at[b, kk], recv_sem=recv_sems.at[b, kk], device_id=to, device_id_type=MESH)

        first = [copy(b, 0, me, sibling, src=x_refs[b]) for b in range(nb)]
        first += [copy(b, 1 + j, me, (*chip, c), src=x_refs[b]) for j, chip in enumerate(chips) for b in range(nb)]
        for cp in first:
            cp.start()
        passed = []
        for j, chip in enumerate(chips):
            for b in range(nb):
                copy(b, 1 + j, (*chip, c), me).wait_recv()
                passed.append(copy(b, 4 + j, (*chip, c), sibling))
                passed[-1].start()
        for b in range(nb):
            copy(b, 0, sibling, me).wait_recv()
        for j, chip in enumerate(chips):
            for b in range(nb):
                copy(b, 4 + j, (*chip, 1 - c), me).wait_recv()
        for cp in first + passed:
            cp.wait_send()

    outs = _hbm_call(body, name, blocks, [((N_DEV,) + b.shape, b.dtype) for b in blocks],
                     [pltpu.SemaphoreType.DMA((nb, 7)), pltpu.SemaphoreType.DMA((nb, 7))])
    return [_put_own(o, b[None], _my_slot()) for o, b in zip(outs, blocks)]


def _gather_peers():
    x, y, c = _coords()
    return (x, y, c), [(x, y, 1 - c), (1 - x, y, c), (x, 1 - y, c), (1 - x, 1 - y, c)]


def _gather_start(blocks, after, name):
    nb = len(blocks)
    hbm, sem = pl.BlockSpec(memory_space=pltpu.HBM), pl.BlockSpec(memory_space=pltpu.SEMAPHORE)

    def body(*refs):
        x_refs, out_refs = refs[:nb], refs[nb:2 * nb]
        send_sems, recv_sems, token = refs[2 * nb + 1], refs[2 * nb + 2], refs[-1]
        (x, y, c), peers = _gather_peers()
        for kk, peer in enumerate(peers):
            for b in range(nb):
                pltpu.make_async_remote_copy(
                    src_ref=x_refs[b], dst_ref=out_refs[b].at[4 * x + 2 * y + c], send_sem=send_sems.at[4 * b + kk],
                    recv_sem=recv_sems.at[4 * b + kk], device_id=peer, device_id_type=MESH).start()
        token[...] = jnp.zeros_like(token)

    gathers = [pltpu.with_memory_space_constraint(lax.empty((N_DEV,) + b.shape, b.dtype), pltpu.HBM) for b in blocks]
    outs = pl.pallas_call(
        body, name=name,
        out_shape=(pltpu.SemaphoreType.DMA((4 * nb,)), pltpu.SemaphoreType.DMA((4 * nb,)),
                   *[pltpu.HBM(b.shape, b.dtype) for b in blocks], *[pltpu.HBM(b.shape, b.dtype) for b in gathers],
                   jax.ShapeDtypeStruct((SUBLANE, LANE), F32)),
        in_specs=[hbm] * (2 * nb) + [pl.BlockSpec(memory_space=pl.ANY)],
        out_specs=(sem, sem, *[hbm] * (2 * nb), pl.BlockSpec(memory_space=pltpu.VMEM)),
        input_output_aliases={i: 2 + i for i in range(2 * nb)},
        compiler_params=pltpu.CompilerParams(has_side_effects=pltpu.SideEffectType.DATAFLOW_SIDE_EFFECTING),
    )(*[pltpu.with_memory_space_constraint(b, pltpu.HBM) for b in blocks], *gathers, after)
    return outs[0], outs[1], list(outs[2:2 + nb]), list(outs[2 + nb:2 + 2 * nb]), outs[-1]


def _gather_wait(send_sems, recv_sems, thru, gathers, after, name):
    nb = len(thru)
    hbm, sem = pl.BlockSpec(memory_space=pltpu.HBM), pl.BlockSpec(memory_space=pltpu.SEMAPHORE)

    def body(*refs):
        x_refs, out_refs = refs[:nb], refs[nb:2 * nb]
        send_sems_, recv_sems_ = refs[2 * nb], refs[2 * nb + 1]
        _, peers = _gather_peers()
        for kk, (px, py, pc) in enumerate(peers):
            for b in range(nb):
                cp = pltpu.make_async_remote_copy(
                    src_ref=x_refs[b], dst_ref=out_refs[b].at[4 * px + 2 * py + pc], send_sem=send_sems_.at[4 * b + kk],
                    recv_sem=recv_sems_.at[4 * b + kk], device_id=(px, py, pc), device_id_type=MESH)
                cp.wait_send()
                cp.wait_recv()

    outs = pl.pallas_call(
        body, name=name,
        out_shape=(*[pltpu.HBM(b.shape, b.dtype) for b in thru], *[pltpu.HBM(b.shape, b.dtype) for b in gathers]),
        in_specs=[hbm] * (2 * nb) + [sem, sem, pl.BlockSpec(memory_space=pl.ANY)], out_specs=[hbm] * (2 * nb),
        input_output_aliases={i: i for i in range(2 * nb)},
        compiler_params=pltpu.CompilerParams(has_side_effects=pltpu.SideEffectType.DATAFLOW_SIDE_EFFECTING),
    )(*thru, *gathers, send_sems, recv_sems, after)
    return list(outs[:nb]), list(outs[nb:])


def _gather_finish(blocks, gathers, name):
    nb = len(blocks)

    def body(*refs):
        x_refs, in_refs, out_refs = refs[:nb], refs[nb:2 * nb], refs[2 * nb:3 * nb]
        send_sems, recv_sems = refs[3 * nb:]
        (x, y, c), peers = _gather_peers()
        copies = []
        for j, (px, py, _) in enumerate(peers[1:]):
            for b in range(nb):
                copies.append(pltpu.make_async_remote_copy(
                    src_ref=in_refs[b].at[4 * px + 2 * py + c], dst_ref=out_refs[b].at[4 * px + 2 * py + c],
                    send_sem=send_sems.at[b, j], recv_sem=recv_sems.at[b, j], device_id=(x, y, 1 - c),
                    device_id_type=MESH))
                copies[-1].start()
        for j, (px, py, _) in enumerate(peers[1:]):
            for b in range(nb):
                pltpu.make_async_remote_copy(
                    src_ref=in_refs[b].at[4 * px + 2 * py + c], dst_ref=out_refs[b].at[4 * px + 2 * py + 1 - c],
                    send_sem=send_sems.at[b, j], recv_sem=recv_sems.at[b, j], device_id=(x, y, 1 - c),
                    device_id_type=MESH).wait_recv()
        for cp in copies:
            cp.wait_send()

    any_spec = pl.BlockSpec(memory_space=pl.ANY)
    outs = pl.pallas_call(
        body, name=name, out_shape=[jax.ShapeDtypeStruct(b.shape, b.dtype) for b in gathers],
        in_specs=[any_spec] * (2 * nb), out_specs=[any_spec] * nb,
        input_output_aliases={nb + i: i for i in range(nb)},
        scratch_shapes=[pltpu.SemaphoreType.DMA((nb, 3)), pltpu.SemaphoreType.DMA((nb, 3))],
    )(*blocks, *gathers)
    return [_put_own(o, b[None], _my_slot()) for o, b in zip(outs, blocks)]


def _sibling_swap(bufs, name):
    nb = len(bufs)

    def body(*refs):
        x_refs, out_refs = refs[:nb], refs[nb:2 * nb]
        send_sems, recv_sems = refs[2 * nb:]
        x, y, c = _coords()
        copies = [pltpu.make_async_remote_copy(
            src_ref=x_refs[b].at[2 * q + 1 - c], dst_ref=out_refs[b].at[q], send_sem=send_sems.at[b, q],
            recv_sem=recv_sems.at[b, q], device_id=(x, y, 1 - c), device_id_type=MESH)
            for b in range(nb) for q in range(4)]
        for cp in copies:
            cp.start()
        for cp in copies:
            cp.wait()

    return _hbm_call(body, name, bufs, [((4,) + b.shape[1:], b.dtype) for b in bufs],
                     [pltpu.SemaphoreType.DMA((nb, 4)), pltpu.SemaphoreType.DMA((nb, 4))])


def _chip_exchange_start(bufs, name):
    nb = len(bufs)
    hbm, sem = pl.BlockSpec(memory_space=pltpu.HBM), pl.BlockSpec(memory_space=pltpu.SEMAPHORE)

    def body(*refs):
        x_refs, land_refs = refs[:nb], refs[nb:2 * nb]
        send_sems, recv_sems, token = refs[2 * nb], refs[2 * nb + 1], refs[-1]
        x, y, c = _coords()
        for j, (px, py) in enumerate([(1 - x, y), (x, 1 - y), (1 - x, 1 - y)]):
            for b in range(nb):
                pltpu.make_async_remote_copy(
                    src_ref=x_refs[b].at[2 * px + py], dst_ref=land_refs[b].at[2 * x + y], send_sem=send_sems.at[3 * b + j],
                    recv_sem=recv_sems.at[3 * b + j], device_id=(px, py, c), device_id_type=MESH).start()
        token[...] = jnp.zeros_like(token)

    lands = [pltpu.with_memory_space_constraint(lax.empty(b.shape, b.dtype), pltpu.HBM) for b in bufs]
    outs = pl.pallas_call(
        body, name=name,
        out_shape=(pltpu.SemaphoreType.DMA((3 * nb,)), pltpu.SemaphoreType.DMA((3 * nb,)),
                   *[pltpu.HBM(b.shape, b.dtype) for b in bufs], *[pltpu.HBM(b.shape, b.dtype) for b in bufs],
                   jax.ShapeDtypeStruct((SUBLANE, LANE), F32)),
        in_specs=[hbm] * (2 * nb), out_specs=(sem, sem, *[hbm] * (2 * nb), pl.BlockSpec(memory_space=pltpu.VMEM)),
        input_output_aliases={i: 2 + i for i in range(2 * nb)},
        compiler_params=pltpu.CompilerParams(has_side_effects=pltpu.SideEffectType.DATAFLOW_SIDE_EFFECTING),
    )(*[pltpu.with_memory_space_constraint(b, pltpu.HBM) for b in bufs], *lands)
    return outs[0], outs[1], list(outs[2:2 + nb]), list(outs[2 + nb:2 + 2 * nb]), outs[-1]


def _chip_exchange_wait(send_sems, recv_sems, thru, lands, after, name):
    nb = len(thru)
    hbm, sem = pl.BlockSpec(memory_space=pltpu.HBM), pl.BlockSpec(memory_space=pltpu.SEMAPHORE)

    def body(*refs):
        x_refs, land_refs = refs[:nb], refs[nb:2 * nb]
        send_sems_, recv_sems_ = refs[2 * nb], refs[2 * nb + 1]
        x, y, c = _coords()
        for j, (px, py) in enumerate([(1 - x, y), (x, 1 - y), (1 - x, 1 - y)]):
            for b in range(nb):
                cp = pltpu.make_async_remote_copy(
                    src_ref=x_refs[b].at[2 * px + py], dst_ref=land_refs[b].at[2 * px + py],
                    send_sem=send_sems_.at[3 * b + j], recv_sem=recv_sems_.at[3 * b + j], device_id=(px, py, c),
                    device_id_type=MESH)
                cp.wait_send()
                cp.wait_recv()

    outs = pl.pallas_call(
        body, name=name,
        out_shape=(*[pltpu.HBM(b.shape, b.dtype) for b in thru], *[pltpu.HBM(b.shape, b.dtype) for b in lands]),
        in_specs=[hbm] * (2 * nb) + [sem, sem, pl.BlockSpec(memory_space=pl.ANY)], out_specs=[hbm] * (2 * nb),
        input_output_aliases={i: i for i in range(2 * nb)},
        compiler_params=pltpu.CompilerParams(has_side_effects=pltpu.SideEffectType.DATAFLOW_SIDE_EFFECTING),
    )(*thru, *lands, send_sems, recv_sems, after)
    return list(outs[:nb]), list(outs[nb:])


def _chip_sum(g, r1, c, name):
    _, r, cw = g.shape
    tr = _pick(r, (256, 176, 128))

    def body(c_ref, g_ref, r_ref, o_ref):
        o_ref[...] = (g_ref[...].astype(F32) + r_ref[...].astype(F32)).astype(GRAD_DT)

    grid_spec = pltpu.PrefetchScalarGridSpec(
        num_scalar_prefetch=1, grid=(4, r // tr),
        in_specs=[_BS((None, None, tr, cw), lambda q, i, c_ref: (q, c_ref[0], i, 0)),
                  _BS((None, tr, cw), lambda q, i, c_ref: (q, i, 0))],
        out_specs=_BS((None, tr, cw), lambda q, i, c_ref: (q, i, 0)))
    return pl.pallas_call(
        body, name=name, grid_spec=grid_spec, out_shape=jax.ShapeDtypeStruct((4, r, cw), GRAD_DT),
        compiler_params=_params(("parallel", "parallel")),
    )(c.reshape(1).astype(jnp.int32), g.reshape(4, 2, r, cw), r1)


def _adamw_math(w, g, m, v):
    m = B1 * m + (1.0 - B1) * g
    v = B2 * v + (1.0 - B2) * (g * g)
    m_hat = m / (1.0 - B1 ** STEP)
    v_hat = v / (1.0 - B2 ** STEP)
    return -LR * (m_hat / (jnp.sqrt(v_hat) + AEPS) + WD * w), m, v


def _sum_adamw(parts, w, m, v, name):
    r, c = w.shape
    tr = _pick(r, (256, 176, 128))
    tc = _pick(c, (256, 128)) if tr == r and r > 256 else c

    def body(p0, p1, p2, p3, w_ref, m_ref, v_ref, g_out, d_out, m_out, v_out):
        g = ((p0[...].astype(F32) + p1[...].astype(F32)) + p2[...].astype(F32)) + p3[...].astype(F32)
        g_out[...] = g
        d_out[...], m_out[...], v_out[...] = _adamw_math(w_ref[...], g, m_ref[...], v_ref[...])

    part = lambda q: _BS((None, tr, tc), functools.partial(lambda i, j, q: (q, i, j), q=q))
    plain = _BS((tr, tc), lambda i, j: (i, j))
    return pl.pallas_call(
        body, name=name, grid=(r // tr, c // tc), in_specs=[part(q) for q in range(4)] + [plain] * 3,
        out_specs=[plain] * 4, out_shape=[jax.ShapeDtypeStruct((r, c), F32)] * 4,
        compiler_params=_params(("parallel", "parallel")),
    )(parts, parts, parts, parts, w, m, v)


def _small_update(gathered, w, m, v):
    r = w.shape[0]

    def body(ga_ref, w_ref, m_ref, v_ref, g_out, d_out, m_out, v_out):
        g = ga_ref[0]
        for dev in range(1, N_DEV):
            g = g + ga_ref[dev]
        g_out[...] = g
        d_out[...], m_out[...], v_out[...] = _adamw_math(w_ref[...], g, m_ref[...], v_ref[...])

    return pl.pallas_call(
        body, name="small_update", out_shape=[jax.ShapeDtypeStruct((r, LANE), F32)] * 4,
    )(gathered, w, m, v)


def _pack_small(vals):
    rows = []
    for name, (r, n) in SMALL:
        flat = vals[name].reshape(-1)
        pad = (-flat.shape[0]) % (SUBLANE * LANE)
        rows.append(jnp.pad(flat, (0, pad)).reshape(-1, LANE))
    return jnp.concatenate(rows, axis=0)


def _unpack_small(packed):
    out, off = {}, 0
    for name, (r, n) in SMALL:
        nrow = -(-(r * n) // (SUBLANE * LANE)) * SUBLANE
        out[name] = packed[off:off + nrow].reshape(-1)[:r * n].reshape(r, n)
        off += nrow
    return out


def kernel(x, positions, ffn1_norm, ffn1_w_in, ffn1_w_out, mix_norm, w_in, hg_lb_table, hg_out_norm, w_hg_branch, mla_q_lora_norm, w_q_up, mla_kv_lora_norm, w_kv_up, q_head_norm, k_head_norm, w_mla_branch, w_merge, b_merge, w_out, ffn2_norm, ffn2_w_in, ffn2_w_out, final_norm, loss_target, m_ffn1_norm, m_ffn1_w_in, m_ffn1_w_out, m_mix_norm, m_w_in, m_hg_lb_table, m_hg_out_norm, m_w_hg_branch, m_mla_q_lora_norm, m_w_q_up, m_mla_kv_lora_norm, m_w_kv_up, m_q_head_norm, m_k_head_norm, m_w_mla_branch, m_w_merge, m_b_merge, m_w_out, m_ffn2_norm, m_ffn2_w_in, m_ffn2_w_out, m_final_norm, v_ffn1_norm, v_ffn1_w_in, v_ffn1_w_out, v_mix_norm, v_w_in, v_hg_lb_table, v_hg_out_norm, v_w_hg_branch, v_mla_q_lora_norm, v_w_q_up, v_mla_kv_lora_norm, v_w_kv_up, v_q_head_norm, v_k_head_norm, v_w_mla_branch, v_w_merge, v_b_merge, v_w_out, v_ffn2_norm, v_ffn2_w_in, v_ffn2_w_out, v_final_norm):
    args = dict(locals())
    t = x.shape[1]
    big_w = {n: args[n][0] for n, _, _ in BIG}
    small = {n: args[n].reshape(shape) for n, shape in SMALL}

    names = [n for n, _, _ in BIG]
    first, rest = names[:2], names[2:]
    full = dict(zip(first, _all_gather([big_w[n].astype(BF16) for n in first], "weights_all_gather_ffn1")))
    g_send, g_recv, g_thru, g_bufs, g_token = _gather_start([big_w[n].astype(BF16) for n in rest], full[first[0]],
                                                            "weights_gather_start")
    gains = dict(small, ffn1_norm=small["ffn1_norm"] + g_token[0:1, 0:1])

    def late_weights(after):
        blocks, bufs = _gather_wait(g_send, g_recv, g_thru, g_bufs, after, "weights_gather_wait")
        return dict(zip(rest, _gather_finish(blocks, bufs, "weights_gather_finish")))

    inv_freq = ROPE_THETA ** (-jnp.arange(0, ROPE, 2, dtype=F32) / ROPE)
    ang = positions[0].astype(F32)[:, None] * inv_freq
    cs = jnp.concatenate([jnp.cos(ang), jnp.cos(ang)], axis=1)
    sn = jnp.concatenate([jnp.sin(ang), jnp.sin(ang)], axis=1)

    c = lax.axis_index("c")
    chip = 2 * lax.axis_index("x") + lax.axis_index("y")
    early = {}

    def chip_sums_of(g, ns, tag):
        from_sibling = _sibling_swap([g[n] for n in ns], "grads_sibling_swap_" + tag)
        return [_chip_sum(g[n], r1, c, "chip_sum_" + n) for n, r1 in zip(ns, from_sibling)]

    def early_grads(g):
        early["names"] = [n for n in names if n in g]
        early["sums"] = chip_sums_of(g, early["names"], "early")
        early["send"], early["recv"], early["thru"], early["lands"], token = _chip_exchange_start(
            early["sums"], "grads_exchange_start")
        return token

    def last_grads(g):
        early["late"] = [n for n in names if n not in early["names"]]
        early["l_send"], early["l_recv"], early["l_thru"], early["l_lands"], token = _chip_exchange_start(
            chip_sums_of(g, early["late"], "late"), "grads_exchange_late_start")
        return token

    loss_row, grad_x, g = _local_step(x[0], loss_target[0], cs, sn, full, late_weights, gains, early_grads,
                                      last_grads)
    late, l_send, l_recv = early["late"], early["l_send"], early["l_recv"]
    l_thru, l_lands = early["l_thru"], early["l_lands"]
    sent, landed = _chip_exchange_wait(early["send"], early["recv"], early["thru"], early["lands"], grad_x,
                                       "grads_exchange_wait")
    exchanged = {n: _put_own(land, lax.dynamic_index_in_dim(own, chip, 0), chip)
                 for n, land, own in zip(early["names"], landed, sent)}

    small_packed = jnp.concatenate([_pack_small(g), jnp.pad(loss_row, ((0, SUBLANE - 1), (0, 0)))], axis=0)
    small_all = _all_gather([small_packed], "small_all_gather")[0]
    zero_tail = jnp.zeros((SUBLANE, LANE), F32)
    pk = lambda d: jnp.concatenate([_pack_small(d), zero_tail], axis=0)
    sg, sd, sm, sv = _small_update(
        small_all, pk(small), pk({n: args["m_" + n].reshape(shape) for n, shape in SMALL}),
        pk({n: args["v_" + n].reshape(shape) for n, shape in SMALL}))
    n_small_rows = _pack_small(small).shape[0]
    loss = sg[n_small_rows, 0]
    outs = {k_: _unpack_small(a) for k_, a in (("grad", sg), ("delta", sd), ("new_m", sm), ("new_v", sv))}

    def adamw(n):
        tr = (lambda a: a.T) if n in GRAD_T else (lambda a: a)
        res = _sum_adamw(exchanged[n], tr(big_w[n]), tr(args["m_" + n][0]), tr(args["v_" + n][0]), "adamw_" + n)
        outs["grad"][n], outs["delta"][n], outs["new_m"][n], outs["new_v"][n] = [tr(a) for a in res]

    for n in early["names"]:
        adamw(n)
    sent, landed = _chip_exchange_wait(l_send, l_recv, l_thru, l_lands, outs["delta"][early["names"][-1]],
                                       "grads_exchange_late_wait")
    for n, land, own in zip(late, landed, sent):
        exchanged[n] = _put_own(land, lax.dynamic_index_in_dim(own, chip, 0), chip)
        adamw(n)

    def shaped(kind, n):
        return outs[kind][n].reshape(args[n].shape)

    return (loss, grad_x[None], *[shaped("grad", n) for n in WEIGHT_ORDER], *[shaped("delta", n) for n in WEIGHT_ORDER],
            *[shaped("new_m", n) for n in WEIGHT_ORDER], *[shaped("new_v", n) for n in WEIGHT_ORDER])
```

```python
import functools

import jax
import jax.numpy as jnp
from jax import lax
from jax.experimental import pallas as pl
from jax.experimental.pallas import tpu as pltpu

F32 = jnp.float32
BF16 = jnp.bfloat16

D = 1024
FF = 2816
NH = 8
HD = 128
ROPE = 64
QK = HD + ROPE
QL = 384
KVL = 256
HGW = NH * HD
CHUNK = 64
EPS = 1e-6
ROPE_THETA = 10000.0
SCALE = QK ** -0.5
LOG2E = 1.4426950408889634

LR, B1, B2, AEPS, WD, STEP = 0.001, 0.9, 0.999, 1e-08, 0.01, 10

HB = 128
SUB = 16
EXP_CLAMP = 80.0
ATT_TILES = (512, 256, 128)
ATT_Q = (1024, 512, 256, 128)
ATT_KEY_TILES = 1
GRAD_DT = BF16
ROW_TILES = (1024, 512, 256, 128)

LANE = 128
SUBLANE = 8
VMEM_LIMIT = 56 << 20

N_DEV = 8
MESH = pl.DeviceIdType.MESH

BIG = (
    ("ffn1_w_in", (D, 2 * FF), 1), ("ffn1_w_out", (FF, D), 0), ("w_in", (D, 4800), 1),
    ("w_hg_branch", (HGW, D), 0), ("w_q_up", (QL, NH * QK), 1), ("w_kv_up", (KVL, NH * 2 * HD), 1),
    ("w_mla_branch", (NH * HD, D), 0), ("w_merge", (D, 2 * D), 1), ("w_out", (D, D), 0),
    ("ffn2_w_in", (D, 2 * FF), 1), ("ffn2_w_out", (FF, D), 0),
)
SMALL = (
    ("ffn1_norm", (1, D)), ("mix_norm", (1, D)), ("hg_lb_table", (2, HGW)), ("hg_out_norm", (1, HD)),
    ("mla_q_lora_norm", (1, QL)), ("mla_kv_lora_norm", (1, KVL)), ("q_head_norm", (1, QK)),
    ("k_head_norm", (1, QK)), ("b_merge", (1, 2 * D)), ("ffn2_norm", (1, D)), ("final_norm", (1, D)),
)
GRAD_T = ("ffn1_w_in", "ffn2_w_in", "w_in", "w_q_up")
WEIGHT_ORDER = ("ffn1_norm", "ffn1_w_in", "ffn1_w_out", "mix_norm", "w_in", "hg_lb_table", "hg_out_norm",
                "w_hg_branch", "mla_q_lora_norm", "w_q_up", "mla_kv_lora_norm", "w_kv_up", "q_head_norm",
                "k_head_norm", "w_mla_branch", "w_merge", "b_merge", "w_out", "ffn2_norm", "ffn2_w_in",
                "ffn2_w_out", "final_norm")


def _pick(n, cands):
    for c in cands:
        if n % c == 0:
            return c
    return n


def _params(sem):
    return pltpu.CompilerParams(dimension_semantics=sem, vmem_limit_bytes=VMEM_LIMIT)


def _sig(x):
    return 0.5 * jnp.tanh(0.5 * x) + 0.5


def _dot(a, b):
    return jnp.dot(a.astype(BF16), b.astype(BF16), preferred_element_type=F32)


def _dot_nt(a, b):
    return lax.dot_general(a.astype(BF16), b.astype(BF16), (((1,), (1,)), ((), ())),
                           preferred_element_type=F32)


def _dot_tn(a, b):
    return lax.dot_general(a.astype(BF16), b.astype(BF16), (((0,), (0,)), ((), ())),
                           preferred_element_type=F32)


def _split3(x):
    x1 = x.astype(BF16)
    r1 = x - x1.astype(F32)
    x2 = r1.astype(BF16)
    x3 = (r1 - x2.astype(F32)).astype(BF16)
    return x1, x2, x3


def _dot_sel(m, x):
    x1, x2, x3 = _split3(x)
    d = lambda p: jnp.dot(m, p, preferred_element_type=F32)
    return d(x1) + d(x2) + d(x3)


def _sel_dot(x, m):
    x1, x2, x3 = _split3(x)
    d = lambda p: jnp.dot(p, m, preferred_element_type=F32)
    return d(x1) + d(x2) + d(x3)


_TN = (1408, 1024, 768, 512, 384, 256, 128)


def _accumulate(acc, o_ref, axis, steps, term):
    step = pl.program_id(axis)

    @pl.when(step == 0)
    def _():
        acc[...] = jnp.zeros_like(acc)

    val = term()
    if isinstance(val, (list, tuple)):
        for e, v in enumerate(val):
            acc[e] += v
    else:
        acc[...] += val

    @pl.when(step == steps - 1)
    def _():
        o_ref[...] = acc[...].astype(o_ref.dtype)


def _mm(a, b, mode, name, out_dtype=F32):
    if mode == "tn":
        t, m = a.shape
        n = b.shape[1]
        tt, tm, tn = _pick(t, ROW_TILES), _pick(m, _TN), _pick(n, _TN)

        def body(a_ref, b_ref, o_ref, acc):
            _accumulate(acc, o_ref, 2, t // tt, lambda: _dot_tn(a_ref[...], b_ref[...]))

        return pl.pallas_call(
            body, name=name, grid=(m // tm, n // tn, t // tt),
            in_specs=[pl.BlockSpec((tt, tm), lambda i, j, k: (k, i)),
                      pl.BlockSpec((tt, tn), lambda i, j, k: (k, j))],
            out_specs=pl.BlockSpec((tm, tn), lambda i, j, k: (i, j)),
            out_shape=jax.ShapeDtypeStruct((m, n), GRAD_DT), scratch_shapes=[pltpu.VMEM((tm, tn), F32)],
            compiler_params=_params(("parallel", "parallel", "arbitrary")),
        )(a, b)

    m, k = a.shape
    tm = _pick(m, ROW_TILES)
    if mode == "nn":
        n = b.shape[1]
        tn = _pick(n, _TN)
        b_spec = pl.BlockSpec((k, tn), lambda i, j: (0, j))
        dot = _dot
    else:
        n = b.shape[0]
        tn = _pick(n, _TN if k <= 4096 else (512, 256, 128))
        b_spec = pl.BlockSpec((tn, k), lambda i, j: (j, 0))
        dot = _dot_nt

    def body(a_ref, b_ref, o_ref):
        o_ref[...] = dot(a_ref[...], b_ref[...]).astype(o_ref.dtype)

    return pl.pallas_call(
        body, name=name, grid=(m // tm, n // tn),
        in_specs=[pl.BlockSpec((tm, k), lambda i, j: (i, 0)), b_spec],
        out_specs=pl.BlockSpec((tm, tn), lambda i, j: (i, j)),
        out_shape=jax.ShapeDtypeStruct((m, n), out_dtype),
        compiler_params=_params(("parallel", "parallel")),
    )(a, b)


_DOTS = {"nn": _dot, "nt": _dot_nt, "tn": _dot_tn}
_BS = pl.BlockSpec


def _mmcall(name, kind, a, b, a_spec, b_spec, o_spec, o_shape, grid, red_axis=None, out_dtype=F32):
    dot = _DOTS[kind]

    def body(a_ref, b_ref, o_ref):
        if red_axis is None:
            o_ref[...] = dot(a_ref[...], b_ref[...]).astype(o_ref.dtype)
        else:
            @pl.when(pl.program_id(red_axis) == 0)
            def _():
                o_ref[...] = jnp.zeros_like(o_ref)

            o_ref[...] += dot(a_ref[...], b_ref[...])

    sem = tuple("arbitrary" if ax == red_axis else "parallel" for ax in range(len(grid)))
    return pl.pallas_call(
        body, name=name, grid=grid, in_specs=[a_spec, b_spec], out_specs=o_spec,
        out_shape=jax.ShapeDtypeStruct(o_shape, out_dtype), compiler_params=_params(sem),
    )(a, b)


class _AsRef:
    def __init__(self, value):
        self.value = value

    def __getitem__(self, idx):
        return self.value


def _mm_stack_red(a, w, name, kind, epilogue=None):
    s, t, n = a.shape
    nout = w.shape[2] if kind == "nn" else w.shape[1]
    if epilogue is None:
        tm = _pick(t, ROW_TILES)
        return _mmcall(name, kind, a, w, _BS((None, tm, n), lambda i, j: (j, i, 0)),
                       _BS((None,) + w.shape[1:], lambda i, j: (j, 0, 0)), _BS((tm, nout), lambda i, j: (i, 0)),
                       (t, nout), (t // tm, s), red_axis=1)
    fn, first, rows, vecs, outs, accs = epilogue[:6]
    rows = [first] + list(rows)
    tm = _pick(t, epilogue[6] if len(epilogue) > 6 else ROW_TILES)
    n_row, n_vec, n_out = len(rows), len(vecs), len(outs)
    dot = _DOTS[kind]

    def body(*refs):
        a_ref, w_ref, prod = refs[0], refs[1], refs[-1]
        row_refs = refs[2:2 + n_row]
        vec_refs = refs[2 + n_row:2 + n_row + n_vec]
        out_refs = refs[2 + n_row + n_vec:2 + n_row + n_vec + n_out]
        acc_refs = refs[2 + n_row + n_vec + n_out:-1]
        i, j = pl.program_id(0), pl.program_id(1)

        @pl.when(j == 0)
        def _():
            prod[...] = jnp.zeros_like(prod)

        prod[...] += dot(a_ref[...], w_ref[...])

        @pl.when(j == s - 1)
        def _():
            res = fn(row_refs[0], _AsRef(prod[...]), *row_refs[1:], *vec_refs, *out_refs)
            if acc_refs:
                @pl.when(i == 0)
                def _():
                    for r in acc_refs:
                        r[...] = jnp.zeros_like(r)

                for r, val in zip(acc_refs, res):
                    r[...] += val

    in_specs = [_BS((None, tm, n), lambda i, j: (j, i, 0)), _BS((None,) + w.shape[1:], lambda i, j: (j, 0, 0))]
    in_specs += [_BS((tm, r.shape[1]), lambda i, j: (i, 0)) for r in rows]
    in_specs += [_BS(v.shape, lambda i, j: (0, 0)) for v in vecs]
    out_specs = [_BS((tm, wd), lambda i, j: (i, 0)) for wd, _ in outs] + [_BS(sh, lambda i, j: (0, 0)) for sh in accs]
    out_shape = [jax.ShapeDtypeStruct((t, wd), dt) for wd, dt in outs] + [jax.ShapeDtypeStruct(sh, F32) for sh in accs]
    return pl.pallas_call(
        body, name=name, grid=(t // tm, s), in_specs=in_specs, out_specs=out_specs, out_shape=out_shape,
        scratch_shapes=[pltpu.VMEM((tm, nout), F32)],
        compiler_params=_params(("arbitrary" if accs else "parallel", "arbitrary")),
    )(a, w, *rows, *vecs)


def _mm_stack_tn(a, b, name):
    grp = 4
    if a.ndim == 2:
        t, k = a.shape
        s, _, n = b.shape
        tt = _pick(t, ROW_TILES)
        a_spec = _BS((tt, k), lambda j, r: (r, 0))
        b_spec, b_in = _BS((None, grp, tt, n), lambda j, r: (j, 0, r, 0)), b.reshape(s // grp, grp, t, n)
        a_in = a
    else:
        s, t, k = a.shape
        n = b.shape[1]
        tt = _pick(t, ROW_TILES)
        a_spec, a_in = _BS((None, grp, tt, k), lambda j, r: (j, 0, r, 0)), a.reshape(s // grp, grp, t, k)
        b_spec, b_in = _BS((tt, n), lambda j, r: (r, 0)), b

    def body(a_ref, b_ref, o_ref, acc):
        def terms():
            shared = a_ref[...] if a.ndim == 2 else b_ref[...]
            return [_dot_tn(shared, b_ref[e]) if a.ndim == 2 else _dot_tn(a_ref[e], shared) for e in range(grp)]

        _accumulate(acc, o_ref, 1, t // tt, terms)

    return pl.pallas_call(
        body, name=name, grid=(s // grp, t // tt), in_specs=[a_spec, b_spec],
        out_specs=_BS((None, grp, k, n), lambda j, r: (j, 0, 0, 0)),
        out_shape=jax.ShapeDtypeStruct((s // grp, grp, k, n), GRAD_DT), scratch_shapes=[pltpu.VMEM((grp, k, n), F32)],
        compiler_params=_params(("parallel", "arbitrary")),
    )(a_in, b_in).reshape(s, k, n)


def _cols_fwd(x, w, name):
    t, k = x.shape
    s, _, n = w.shape
    tm = _pick(t, ROW_TILES)

    def body(x_ref, w_ref, o_ref):
        x_ = x_ref[...]
        for j in range(s):
            o_ref[:, n * j:n * (j + 1)] = _dot(x_, w_ref[j])

    return pl.pallas_call(
        body, name=name, grid=(t // tm,),
        in_specs=[_BS((tm, k), lambda i: (i, 0)), _BS((s, k, n), lambda i: (0, 0, 0))],
        out_specs=_BS((tm, s * n), lambda i: (i, 0)), out_shape=jax.ShapeDtypeStruct((t, s * n), F32),
        compiler_params=_params(("parallel",)),
    )(x, w)


def _cols_dx(d, w, name):
    t = d.shape[0]
    s, k, n = w.shape
    tm = _pick(t, ROW_TILES)

    def body(d_ref, w_ref, o_ref):
        acc = _dot_nt(d_ref[:, 0:n], w_ref[0])
        for j in range(1, s):
            acc = acc + _dot_nt(d_ref[:, n * j:n * (j + 1)], w_ref[j])
        o_ref[...] = acc

    return pl.pallas_call(
        body, name=name, grid=(t // tm,),
        in_specs=[_BS((tm, s * n), lambda i: (i, 0)), _BS((s, k, n), lambda i: (0, 0, 0))],
        out_specs=_BS((tm, k), lambda i: (i, 0)), out_shape=jax.ShapeDtypeStruct((t, k), F32),
        compiler_params=_params(("parallel",)),
    )(d, w)


def _cols_dw(x, d, n, name):
    t, k = x.shape
    s = d.shape[1] // n
    tt = _pick(t, ROW_TILES)

    def body(x_ref, d_ref, o_ref, acc):
        def terms():
            x_ = x_ref[...]
            return [_dot_tn(x_, d_ref[:, n * j:n * (j + 1)]) for j in range(s)]

        _accumulate(acc, o_ref, 0, t // tt, terms)

    return pl.pallas_call(
        body, name=name, grid=(t // tt,),
        in_specs=[_BS((tt, k), lambda r: (r, 0)), _BS((tt, s * n), lambda r: (r, 0))],
        out_specs=_BS((s, k, n), lambda r: (0, 0, 0)), out_shape=jax.ShapeDtypeStruct((s, k, n), GRAD_DT),
        scratch_shapes=[pltpu.VMEM((s, k, n), F32)], compiler_params=_params(("arbitrary",)),
    )(x, d)


def _rows(fn, name, t, tm, ins, vecs, outs, accs=()):
    n_in, n_out, n_acc = len(ins) + len(vecs), len(outs), len(accs)

    def body(*refs):
        res = fn(*refs[:n_in + n_out])
        if n_acc:
            acc_refs = refs[n_in + n_out:]

            @pl.when(pl.program_id(0) == 0)
            def _():
                for r in acc_refs:
                    r[...] = jnp.zeros_like(r)

            for r, val in zip(acc_refs, res):
                r[...] += val

    in_specs = [pl.BlockSpec((tm, bw), functools.partial(lambda i, cb: (i, cb), cb=cb)) for _, bw, cb in ins]
    in_specs += [pl.BlockSpec(v.shape, lambda i: (0, 0)) for v in vecs]
    out_specs = [pl.BlockSpec((tm, w), lambda i: (i, 0)) for w, _ in outs]
    out_specs += [pl.BlockSpec(s, lambda i: (0, 0)) for s in accs]
    out_shape = [jax.ShapeDtypeStruct((t, w), dt) for w, dt in outs]
    out_shape += [jax.ShapeDtypeStruct(s, F32) for s in accs]
    return pl.pallas_call(
        body, name=name, grid=(t // tm,), in_specs=in_specs, out_specs=out_specs, out_shape=out_shape,
        compiler_params=_params(("arbitrary",) if n_acc else ("parallel",)),
    )(*[a for a, _, _ in ins], *vecs)


def _rms(x, g):
    return x * lax.rsqrt(jnp.mean(x * x, axis=-1, keepdims=True) + EPS) * g


def _rms_bwd(x, g, dy):
    xh = x * lax.rsqrt(jnp.mean(x * x, axis=-1, keepdims=True) + EPS)
    r = lax.rsqrt(jnp.mean(x * x, axis=-1, keepdims=True) + EPS)
    dyg = dy * g
    dx = r * (dyg - xh * jnp.mean(dyg * xh, axis=-1, keepdims=True))
    return dx, jnp.sum(dy * xh, axis=0, keepdims=True)


def _hgrn_mats():
    row = lax.broadcasted_iota(jnp.int32, (HB, HB), 0)
    col = lax.broadcasted_iota(jnp.int32, (HB, HB), 1)
    return row, col


def _hgrn_gates(qr, z, t0, t1):
    lb = 1.0 / (1.0 + jnp.exp(t1 - t0))
    th = jnp.tanh(0.5 * z)
    sz, sneg = 0.5 + 0.5 * th, 0.5 - 0.5 * th
    f = lb + (1.0 - lb) * sz
    return lb, sz, sneg, f, jnp.log(f), (1.0 - lb) * sneg, qr * _sig(qr)


def _hgrn_scores(q, k, cum):
    heads, nsub = range(len(q)), HB // SUB
    mids = [[cum[h][SUB * i + SUB // 2 - 1:SUB * i + SUB // 2, :] for i in range(nsub)] for h in heads]
    cmid = [jnp.concatenate([cum[h][SUB * i:SUB * (i + 1)] - mids[h][i] for i in range(nsub)], axis=0) for h in heads]
    qd = [q[h] * jnp.exp(jnp.minimum(cmid[h], EXP_CLAMP)) for h in heads]
    qd_b = [x.astype(BF16) for x in qd]
    kds = [[None] * nsub for _ in heads]
    parts = [[None] * nsub for _ in heads]
    for i in range(nsub):
        for h in heads:
            kds[h][i] = k[h] * jnp.exp(jnp.minimum(mids[h][i] - cum[h], EXP_CLAMP))
        for h in heads:
            parts[h][i] = _dot_nt(qd_b[h][SUB * i:SUB * (i + 1)], kds[h][i])
    return qd, kds, [jnp.concatenate(parts[h], axis=0) for h in heads], cmid


def _hgrn_fwd(p_hg, table, gain, t):
    nblk = t // HB

    def body(q_ref, f_ref, i_ref, g_ref, tab_ref, gain_ref, o_ref, y_ref, st_ref, state):
        @pl.when(pl.program_id(0) == 0)
        def _():
            state[...] = jnp.zeros_like(state)

        row, col = _hgrn_mats()
        causal = col <= row
        tri = causal.astype(BF16)
        heads = range(NH)
        sl = [slice(HD * h, HD * (h + 1)) for h in heads]
        gates = [_hgrn_gates(q_ref[:, sl[h]], f_ref[:, sl[h]], tab_ref[0:1, sl[h]], tab_ref[1:2, sl[h]]) for h in heads]
        lf, k, q = [g[4] for g in gates], [g[5] for g in gates], [g[6] for g in gates]
        v = [i_ref[:, sl[h]] for h in heads]
        cum = [_dot_sel(tri, lf[h]) for h in heads]
        _, _, s, _ = _hgrn_scores(q, k, cum)
        p = [jnp.where(causal, s[h], 0.0) for h in heads]
        st = [state[h] for h in heads]
        o = [_dot(p[h], v[h]) + _dot_nt(q[h] * jnp.exp(cum[h]), st[h]) for h in heads]
        last = [cum[h][HB - 1:HB, :] for h in heads]
        new_st = [st[h] * jnp.exp(last[h]) + _dot_tn(v[h], k[h] * jnp.exp(last[h] - cum[h])) for h in heads]
        for h in heads:
            gr = g_ref[:, sl[h]]
            st_ref[h, 0] = st[h]
            state[h] = new_st[h]
            o_ref[:, sl[h]] = o[h]
            y_ref[:, sl[h]] = (_rms(o[h], gain_ref[...]) * gr * _sig(gr)).astype(BF16)

    blk = lambda cb: pl.BlockSpec((HB, HGW), functools.partial(lambda n, cb: (n, cb), cb=cb))
    return pl.pallas_call(
        body, name="hgrn_fwd", grid=(nblk,),
        in_specs=[blk(0), blk(1), blk(2), blk(3), pl.BlockSpec((2, HGW), lambda n: (0, 0)),
                  pl.BlockSpec((1, HD), lambda n: (0, 0))],
        out_specs=[pl.BlockSpec((HB, HGW), lambda n: (n, 0)), pl.BlockSpec((HB, HGW), lambda n: (n, 0)),
                   pl.BlockSpec((NH, 1, HD, HD), lambda n: (0, n, 0, 0))],
        out_shape=[jax.ShapeDtypeStruct((t, HGW), F32), jax.ShapeDtypeStruct((t, HGW), BF16),
                   jax.ShapeDtypeStruct((NH, nblk, HD, HD), F32)],
        scratch_shapes=[pltpu.VMEM((NH, HD, HD), F32)],
        compiler_params=_params(("arbitrary",)),
    )(p_hg, p_hg, p_hg, p_hg, table, gain)


def _hgrn_bwd(p_hg, table, gain, o_pre, states, dy, t):
    nblk = t // HB

    def body(q_ref, f_ref, i_ref, g_ref, tab_ref, gain_ref, o_ref, st_ref, dy_ref, dp_ref, dtab_ref, dgain_ref,
             dstate):
        @pl.when(pl.program_id(0) == 0)
        def _():
            dstate[...] = jnp.zeros_like(dstate)
            dtab_ref[...] = jnp.zeros_like(dtab_ref)
            dgain_ref[...] = jnp.zeros_like(dgain_ref)

        row, col = _hgrn_mats()
        causal = col <= row
        tri = causal.astype(BF16)
        tri_t = (row <= col).astype(BF16)
        heads, nsub = range(NH), HB // SUB
        sl = [slice(HD * h, HD * (h + 1)) for h in heads]
        rows_of = lambda i: slice(SUB * i, SUB * (i + 1))
        gain_ = gain_ref[...]
        qr, z = [q_ref[:, sl[h]] for h in heads], [f_ref[:, sl[h]] for h in heads]
        v, gr = [i_ref[:, sl[h]] for h in heads], [g_ref[:, sl[h]] for h in heads]
        gates = [_hgrn_gates(qr[h], z[h], tab_ref[0:1, sl[h]], tab_ref[1:2, sl[h]]) for h in heads]
        lb, sz, sneg, f, lf, k, q = ([g[j] for g in gates] for j in range(7))
        cum = [_dot_sel(tri, lf[h]) for h in heads]
        qd, kds, s, cmid = _hgrn_scores(q, k, cum)
        p = [jnp.where(causal, s[h], 0.0) for h in heads]
        st, dst = [st_ref[h, 0] for h in heads], [dstate[h] for h in heads]
        o, dyh, sg = [o_ref[:, sl[h]] for h in heads], [dy_ref[:, sl[h]] for h in heads], [_sig(x) for x in gr]
        dgr = [dyh[h] * _rms(o[h], gain_) * sg[h] * (1.0 + gr[h] * (1.0 - sg[h])) for h in heads]
        norm_bwd = [_rms_bwd(o[h], gain_, dyh[h] * gr[h] * sg[h]) for h in heads]
        do = [x[0] for x in norm_bwd]
        do_b = [x.astype(BF16) for x in do]
        ecum, last = [jnp.exp(x) for x in cum], [x[HB - 1:HB, :] for x in cum]
        qc = [q[h] * ecum[h] for h in heads]
        edec = [jnp.exp(last[h] - cum[h]) for h in heads]
        kdec = [k[h] * edec[h] for h in heads]
        dp = [jnp.where(causal, _dot_nt(do_b[h], v[h]), 0.0) for h in heads]
        dv = [_dot(p[h].T, do_b[h]) + _dot_nt(kdec[h], dst[h]) for h in heads]
        dqc = [_dot(do_b[h], st[h]) for h in heads]
        dkdec = [_dot(v[h], dst[h]) for h in heads]
        new_dst = [dst[h] * jnp.exp(last[h]) + _dot(do[h].T, qc[h]) for h in heads]
        dp_b = [x.astype(BF16) for x in dp]
        dqd = [jnp.concatenate(parts, axis=0) for parts in
               zip(*[[_dot(dp_b[h][rows_of(i)], kds[h][i]) for h in heads] for i in range(nsub)])]
        gq = [dqd[h] * qd[h] for h in heads]
        dq = [dqd[h] * jnp.exp(jnp.minimum(cmid[h], EXP_CLAMP)) + dqc[h] * ecum[h] for h in heads]
        gs = [dkdec[h] * kdec[h] for h in heads]
        dk = [dkdec[h] * edec[h] for h in heads]
        dcum = [dqc[h] * qc[h] - gs[h] + gq[h]
                + jnp.where(row == HB - 1, jnp.sum(gs[h], axis=0, keepdims=True)
                            + jnp.exp(last[h]) * jnp.sum(st[h] * dst[h], axis=0, keepdims=True), 0.0) for h in heads]
        qd_b = [x.astype(BF16) for x in qd]
        for i in range(nsub):
            dkd = [_dot_tn(dp_b[h][rows_of(i)], qd_b[h][rows_of(i)]) for h in heads]
            for h in heads:
                mid = cum[h][SUB * i + SUB // 2 - 1:SUB * i + SUB // 2, :]
                dk[h] = dk[h] + dkd[h] * jnp.exp(jnp.minimum(mid - cum[h], EXP_CLAMP))
                gk = dkd[h] * kds[h][i]
                to_mid = jnp.sum(gk, axis=0, keepdims=True) - jnp.sum(gq[h][rows_of(i)], axis=0, keepdims=True)
                dcum[h] = dcum[h] - gk + jnp.where(row == SUB * i + SUB // 2 - 1, to_mid, 0.0)
        dlf = [_dot_sel(tri_t, dcum[h]) for h in heads]
        dgain = jnp.zeros((1, HD), F32)
        for h in heads:
            df = dlf[h] / f[h] - dk[h]
            dlb = jnp.sum(df * sneg[h], axis=0, keepdims=True) * lb[h] * (1.0 - lb[h])
            dtab_ref[0:1, sl[h]] += dlb
            dtab_ref[1:2, sl[h]] -= dlb
            sq = _sig(qr[h])
            dstate[h] = new_dst[h]
            dp_ref[:, sl[h]] = (dq[h] * sq * (1.0 + qr[h] * (1.0 - sq))).astype(BF16)
            dp_ref[:, HGW + HD * h:HGW + HD * (h + 1)] = (df * (1.0 - lb[h]) * sz[h] * sneg[h]).astype(BF16)
            dp_ref[:, 2 * HGW + HD * h:2 * HGW + HD * (h + 1)] = dv[h].astype(BF16)
            dp_ref[:, 3 * HGW + HD * h:3 * HGW + HD * (h + 1)] = dgr[h].astype(BF16)
            dgain = dgain + norm_bwd[h][1]
        dgain_ref[...] += dgain

    rev = lambda cb: pl.BlockSpec((HB, HGW), functools.partial(lambda n, cb: (nblk - 1 - n, cb), cb=cb))
    return pl.pallas_call(
        body, name="hgrn_bwd", grid=(nblk,),
        in_specs=[rev(0), rev(1), rev(2), rev(3), pl.BlockSpec((2, HGW), lambda n: (0, 0)),
                  pl.BlockSpec((1, HD), lambda n: (0, 0)), rev(0),
                  pl.BlockSpec((NH, 1, HD, HD), lambda n: (0, nblk - 1 - n, 0, 0)), rev(0)],
        out_specs=[pl.BlockSpec((HB, 4 * HGW), lambda n: (nblk - 1 - n, 0)),
                   pl.BlockSpec((2, HGW), lambda n: (0, 0)), pl.BlockSpec((1, HD), lambda n: (0, 0))],
        out_shape=[jax.ShapeDtypeStruct((t, 4 * HGW), BF16), jax.ShapeDtypeStruct((2, HGW), F32),
                   jax.ShapeDtypeStruct((1, HD), F32)],
        scratch_shapes=[pltpu.VMEM((NH, HD, HD), F32)],
        compiler_params=_params(("arbitrary",)),
    )(p_hg, p_hg, p_hg, p_hg, table, gain, o_pre, states, dy)


def _rope_mat():
    r = lax.broadcasted_iota(jnp.int32, (ROPE, ROPE), 0)
    c = lax.broadcasted_iota(jnp.int32, (ROPE, ROPE), 1)
    half = ROPE // 2
    return ((r == c - half).astype(F32) - (r == c + half).astype(F32)).astype(BF16)


def _mla_prep_fwd(p_mla, cs, sn, wq, wkv, gql, gkvl, gq, gk, t):
    tm = _pick(t, (512, 256, 128))

    def body(p_ref, cs_ref, sn_ref, wq_ref, wkv_ref, gql_ref, gkvl_ref, gq_ref, gk_ref,
             q_ref, k_ref, v_ref):
        rmat = _rope_mat()
        cqn = _rms(p_ref[:, 0:QL], gql_ref[...]).astype(BF16)
        ckvn = _rms(p_ref[:, QL:QL + KVL], gkvl_ref[...]).astype(BF16)
        kpe = p_ref[:, QL + KVL:QL + KVL + ROPE]
        c, s = cs_ref[...], sn_ref[...]
        rot = lambda x: x * c + _sel_dot(x, rmat) * s
        heads = range(NH)
        ssq = lambda x: jnp.sum(x * x, -1, keepdims=True)
        qa = [_dot(cqn, wq_ref[h, :, 0:HD]) for h in heads]
        qr = [_dot(cqn, wq_ref[h, :, HD:QK]) for h in heads]
        kn = [_dot(ckvn, wkv_ref[h, :, 0:HD]) for h in heads]
        vv = [_dot(ckvn, wkv_ref[h, :, HD:2 * HD]) for h in heads]
        kpe_ss = ssq(kpe)
        rq = [lax.rsqrt((ssq(qa[h]) + ssq(qr[h])) / QK + EPS) for h in heads]
        rk = [lax.rsqrt((ssq(kn[h]) + kpe_ss) / QK + EPS) for h in heads]
        q_rope = [rot(qr[h] * rq[h] * gq_ref[:, HD:QK]) for h in heads]
        k_rope = [rot(kpe * rk[h] * gk_ref[:, HD:QK]) for h in heads]
        for h in heads:
            q_ref[h, :, 0:HD] = (qa[h] * rq[h] * gq_ref[:, 0:HD] * (SCALE * LOG2E)).astype(BF16)
            q_ref[h, :, HD:QK] = (q_rope[h] * (SCALE * LOG2E)).astype(BF16)
            k_ref[h, :, 0:HD] = (kn[h] * rk[h] * gk_ref[:, 0:HD]).astype(BF16)
            k_ref[h, :, HD:QK] = k_rope[h].astype(BF16)
            v_ref[h] = vv[h].astype(BF16)

    whole = lambda a: pl.BlockSpec(a.shape, functools.partial(lambda i, nd: (0,) * nd, nd=a.ndim))
    return pl.pallas_call(
        body, name="mla_prep_fwd", grid=(t // tm,),
        in_specs=[pl.BlockSpec((tm, QL + KVL + ROPE), lambda i: (i, 0)), pl.BlockSpec((tm, ROPE), lambda i: (i, 0)),
                  pl.BlockSpec((tm, ROPE), lambda i: (i, 0))] + [whole(a) for a in (wq, wkv, gql, gkvl, gq, gk)],
        out_specs=[pl.BlockSpec((NH, tm, QK), lambda i: (0, i, 0)), pl.BlockSpec((NH, tm, QK), lambda i: (0, i, 0)),
                   pl.BlockSpec((NH, tm, HD), lambda i: (0, i, 0))],
        out_shape=[jax.ShapeDtypeStruct((NH, t, QK), BF16), jax.ShapeDtypeStruct((NH, t, QK), BF16),
                   jax.ShapeDtypeStruct((NH, t, HD), BF16)],
        compiler_params=_params(("parallel",)),
    )(p_mla, cs, sn, wq, wkv, gql, gkvl, gq, gk)


def _mla_prep_bwd(p_mla, cs, sn, wq, wkv, gql, gkvl, gq, gk, dq, dk, dv, t):
    tm = _pick(t, (512, 256, 128))

    def body(p_ref, cs_ref, sn_ref, wq_ref, wkv_ref, gql_ref, gkvl_ref, gq_ref, gk_ref,
             dq_ref, dk_ref, dv_ref,
             dp_ref, dwq_ref, dwkv_ref, dgql_ref, dgkvl_ref, dgq_ref, dgk_ref):
        accs = (dwq_ref, dwkv_ref, dgql_ref, dgkvl_ref, dgq_ref, dgk_ref)

        @pl.when(pl.program_id(0) == 0)
        def _():
            for r in accs:
                r[...] = jnp.zeros_like(r)

        rmat = _rope_mat()
        rmat_t = -rmat
        cq, ckv = p_ref[:, 0:QL], p_ref[:, QL:QL + KVL]
        kpe = p_ref[:, QL + KVL:QL + KVL + ROPE]
        cqn_f, ckvn_f = _rms(cq, gql_ref[...]), _rms(ckv, gkvl_ref[...])
        cqn, ckvn = cqn_f.astype(BF16), ckvn_f.astype(BF16)
        ckvn_t = ckvn_f.T.astype(BF16)
        c, s = cs_ref[...], sn_ref[...]
        unrot = lambda dy: dy * c + _sel_dot(dy * s, rmat_t)
        heads = range(NH)
        rsum = lambda x: jnp.sum(x, -1, keepdims=True)
        csum = lambda x: jnp.sum(x, axis=0, keepdims=True)
        qa = [_dot(cqn, wq_ref[h, :, 0:HD]) for h in heads]
        qr = [_dot(cqn, wq_ref[h, :, HD:QK]) for h in heads]
        kn = [_dot(ckvn, wkv_ref[h, :, 0:HD]) for h in heads]
        dyr = [unrot(dq_ref[h, :, HD:QK] * SCALE) for h in heads]
        dkr = [unrot(dk_ref[h, :, HD:QK]) for h in heads]
        rq = [lax.rsqrt((rsum(qa[h] * qa[h]) + rsum(qr[h] * qr[h])) / QK + EPS) for h in heads]
        xa, xr = [qa[h] * rq[h] for h in heads], [qr[h] * rq[h] for h in heads]
        dya = [dq_ref[h, :, 0:HD] * SCALE for h in heads]
        ga, gr_ = [dya[h] * gq_ref[:, 0:HD] for h in heads], [dyr[h] * gq_ref[:, HD:QK] for h in heads]
        mean = [(rsum(ga[h] * xa[h]) + rsum(gr_[h] * xr[h])) / QK for h in heads]
        dqa = [(rq[h] * (ga[h] - xa[h] * mean[h])).astype(BF16) for h in heads]
        dqr = [(rq[h] * (gr_[h] - xr[h] * mean[h])).astype(BF16) for h in heads]
        kpe_ss = rsum(kpe * kpe)
        rk = [lax.rsqrt((rsum(kn[h] * kn[h]) + kpe_ss) / QK + EPS) for h in heads]
        ya, yr = [kn[h] * rk[h] for h in heads], [kpe * rk[h] for h in heads]
        dka = [dk_ref[h, :, 0:HD] for h in heads]
        ha, hr = [dka[h] * gk_ref[:, 0:HD] for h in heads], [dkr[h] * gk_ref[:, HD:QK] for h in heads]
        mean_k = [(rsum(ha[h] * ya[h]) + rsum(hr[h] * yr[h])) / QK for h in heads]
        dkn = [(rk[h] * (ha[h] - ya[h] * mean_k[h])).astype(BF16) for h in heads]
        dvh = [dv_ref[h].astype(BF16) for h in heads]
        dw = [(_dot_tn(dqa[h], cqn), _dot_tn(dqr[h], cqn), _dot(ckvn_t, dkn[h]), _dot(ckvn_t, dvh[h])) for h in heads]
        back_q = [_dot_nt(dqa[h], wq_ref[h, :, 0:HD]) + _dot_nt(dqr[h], wq_ref[h, :, HD:QK]) for h in heads]
        back_kv = [_dot_nt(dkn[h], wkv_ref[h, :, 0:HD]) + _dot_nt(dvh[h], wkv_ref[h, :, HD:2 * HD]) for h in heads]
        dcqn, dckvn = sum(back_q[1:], back_q[0]), sum(back_kv[1:], back_kv[0])
        dkpe = sum([rk[h] * (hr[h] - yr[h] * mean_k[h]) for h in heads][1:], rk[0] * (hr[0] - yr[0] * mean_k[0]))
        dgq_a = sum([csum(dya[h] * xa[h]) for h in heads][1:], csum(dya[0] * xa[0]))
        dgq_r = sum([csum(dyr[h] * xr[h]) for h in heads][1:], csum(dyr[0] * xr[0]))
        dgk_a = sum([csum(dka[h] * ya[h]) for h in heads][1:], csum(dka[0] * ya[0]))
        dgk_r = sum([csum(dkr[h] * yr[h]) for h in heads][1:], csum(dkr[0] * yr[0]))
        for h in heads:
            dwq_ref[h, 0:HD, :] += dw[h][0]
            dwq_ref[h, HD:QK, :] += dw[h][1]
            dwkv_ref[h, :, 0:HD] += dw[h][2]
            dwkv_ref[h, :, HD:2 * HD] += dw[h][3]
        dcq, dg1 = _rms_bwd(cq, gql_ref[...], dcqn)
        dckv, dg2 = _rms_bwd(ckv, gkvl_ref[...], dckvn)
        dp_ref[:, 0:QL] = dcq.astype(BF16)
        dp_ref[:, QL:QL + KVL] = dckv.astype(BF16)
        dp_ref[:, QL + KVL:QL + KVL + ROPE] = dkpe.astype(BF16)
        dgql_ref[...] += dg1
        dgkvl_ref[...] += dg2
        dgq_ref[:, 0:HD] += dgq_a
        dgq_ref[:, HD:QK] += dgq_r
        dgk_ref[:, 0:HD] += dgk_a
        dgk_ref[:, HD:QK] += dgk_r

    whole = lambda a: pl.BlockSpec(a.shape, functools.partial(lambda i, nd: (0,) * nd, nd=a.ndim))
    acc_shapes = [(NH, QK, QL), wkv.shape, gql.shape, gkvl.shape, gq.shape, gk.shape]
    return pl.pallas_call(
        body, name="mla_prep_bwd", grid=(t // tm,),
        in_specs=[pl.BlockSpec((tm, QL + KVL + ROPE), lambda i: (i, 0)), pl.BlockSpec((tm, ROPE), lambda i: (i, 0)),
                  pl.BlockSpec((tm, ROPE), lambda i: (i, 0))]
        + [whole(a) for a in (wq, wkv, gql, gkvl, gq, gk)]
        + [pl.BlockSpec((NH, tm, QK), lambda i: (0, i, 0)), pl.BlockSpec((NH, tm, QK), lambda i: (0, i, 0)),
           pl.BlockSpec((NH, tm, HD), lambda i: (0, i, 0))],
        out_specs=[pl.BlockSpec((tm, QL + KVL + ROPE), lambda i: (i, 0))]
        + [pl.BlockSpec(s, functools.partial(lambda i, nd: (0,) * nd, nd=len(s))) for s in acc_shapes],
        out_shape=[jax.ShapeDtypeStruct((t, QL + KVL + ROPE), BF16)]
        + [jax.ShapeDtypeStruct(s, F32) for s in acc_shapes],
        compiler_params=_params(("arbitrary",)),
    )(p_mla, cs, sn, wq, wkv, gql, gkvl, gq, gk, dq, dk, dv)


def _chunk_mask(nq, nk, key0, keys_on_rows):
    shape = (nk, nq) if keys_on_rows else (nq, nk)
    qi = lax.broadcasted_iota(jnp.int32, shape, 1 if keys_on_rows else 0) // CHUNK
    ki = lax.broadcasted_iota(jnp.int32, shape, 0 if keys_on_rows else 1) // CHUNK + key0 // CHUNK
    return ki <= qi


def _flash_fwd(q, k, v, t):
    tq = _pick(t, ATT_Q)
    tk = tq // ATT_KEY_TILES

    def body(q_ref, k_ref, v_ref, o_ref, lse_ref):
        i = pl.program_id(1)
        qt = q_ref[0]

        def step(j, carry, key0):
            m, l, acc = carry
            cols = pl.ds(pl.multiple_of(j * tk, tk), tk)
            s = _dot_nt(qt, k_ref[0, cols, :])
            if key0 is not None:
                s = jnp.where(_chunk_mask(tq, tk, key0, False), s, -jnp.inf)
            m_new = jnp.maximum(m, jnp.max(s, axis=-1, keepdims=True))
            p = jnp.exp2(s - m_new)
            alpha = jnp.exp2(m - m_new)
            return m_new, alpha * l + jnp.sum(p, axis=-1, keepdims=True), alpha * acc + _dot(p, v_ref[0, cols, :])

        init = (jnp.full((tq, 1), -jnp.inf, F32), jnp.zeros((tq, 1), F32), jnp.zeros((tq, HD), F32))
        carry = lax.fori_loop(0, ATT_KEY_TILES * i, lambda j, cr: step(j, cr, None), init)
        for h in range(ATT_KEY_TILES):
            carry = step(ATT_KEY_TILES * i + h, carry, h * tk)
        m, l, acc = carry
        o_ref[...] = (acc / l).astype(BF16)
        x1, x2, x3 = _split3(jnp.broadcast_to((m + jnp.log2(l)) * (1.0 / HD), (tq, HD)))
        ones = jnp.ones((SUBLANE, HD), BF16)
        rowsum = lambda p: lax.dot_general(ones, p, (((1,), (1,)), ((), ())), preferred_element_type=F32)
        lse_ref[0] = (rowsum(x1) + rowsum(x2) + rowsum(x3))[0:1, :]

    return pl.pallas_call(
        body, name="flash_fwd", grid=(NH, t // tq),
        in_specs=[pl.BlockSpec((1, tq, QK), lambda h, i: (h, i, 0)), pl.BlockSpec((1, t, QK), lambda h, i: (h, 0, 0)),
                  pl.BlockSpec((1, t, HD), lambda h, i: (h, 0, 0))],
        out_specs=[pl.BlockSpec((tq, HD), lambda h, i: (i, h)), pl.BlockSpec((1, 1, tq), lambda h, i: (h, 0, i))],
        out_shape=[jax.ShapeDtypeStruct((t, NH * HD), BF16), jax.ShapeDtypeStruct((NH, 1, t), F32)],
        compiler_params=_params(("parallel", "parallel")),
    )(q, k, v)


def _attn_out_bwd(dy, w, o, t):
    tm = _pick(t, ATT_TILES)

    def body(dy_ref, w_ref, o_ref, do_ref, delta_ref):
        do = _dot_nt(dy_ref[...], w_ref[...]).astype(BF16)
        do_ref[...] = do
        ones = jnp.ones((SUBLANE, HD), BF16)
        rowsum = lambda p: lax.dot_general(ones, p, (((1,), (1,)), ((), ())), preferred_element_type=F32)
        sl = [slice(HD * h, HD * (h + 1)) for h in range(NH)]
        parts = [_split3(do[:, sl[h]].astype(F32) * o_ref[:, sl[h]].astype(F32)) for h in range(NH)]
        sums = [[rowsum(x) for x in parts[h]] for h in range(NH)]
        for h in range(NH):
            delta_ref[h] = (sums[h][0] + sums[h][1] + sums[h][2])[0:1, :]

    return pl.pallas_call(
        body, name="attn_out_bwd", grid=(t // tm,),
        in_specs=[_BS((tm, dy.shape[1]), lambda i: (i, 0)), _BS(w.shape, lambda i: (0, 0)),
                  _BS((tm, NH * HD), lambda i: (i, 0))],
        out_specs=[_BS((tm, NH * HD), lambda i: (i, 0)), _BS((NH, 1, tm), lambda i: (0, 0, i))],
        out_shape=[jax.ShapeDtypeStruct((t, NH * HD), BF16), jax.ShapeDtypeStruct((NH, 1, t), F32)],
        compiler_params=_params(("parallel",)),
    )(dy, w, o)


def _flash_bwd(q, k, v, do, lse_row, delta_row, t):
    tq = _pick(t, ATT_Q)
    tk = tq // ATT_KEY_TILES

    def body(q_ref, k_ref, v_ref, do_ref, lse_ref, delta_ref, dq_ref, dk_ref, dv_ref):
        j = pl.program_id(1)

        @pl.when(j == 0)
        def _():
            dq_ref[...] = jnp.zeros_like(dq_ref)

        kt, vt = k_ref[0], v_ref[0]

        def step(i, carry, key0):
            dk, dv = carry
            rows = pl.ds(pl.multiple_of(i * tq, tq), tq)
            qt, dot_ = q_ref[0, rows, :], do_ref[rows, :]
            p = jnp.exp2(_dot_nt(kt, qt) - lse_ref[0, :, rows])
            if key0 is not None:
                p = jnp.where(_chunk_mask(tq, tk, key0, True), p, 0.0)
            ds = (p * (_dot_nt(vt, dot_) - delta_ref[0, :, rows])).astype(BF16)
            dq_ref[0, rows, :] += _dot_tn(ds, kt)
            return dk + _dot(ds, qt), dv + _dot(p, dot_)

        own = j // ATT_KEY_TILES
        carry = step(own, (jnp.zeros((tk, QK), F32), jnp.zeros((tk, HD), F32)), (j % ATT_KEY_TILES) * tk)
        dk, dv = lax.fori_loop(own + 1, t // tq, lambda i, cr: step(i, cr, None), carry)
        dk_ref[0] = dk * (1.0 / LOG2E)
        dv_ref[0] = dv

    return pl.pallas_call(
        body, name="flash_bwd", grid=(NH, t // tk),
        in_specs=[pl.BlockSpec((1, t, QK), lambda h, j: (h, 0, 0)), pl.BlockSpec((1, tk, QK), lambda h, j: (h, j, 0)),
                  pl.BlockSpec((1, tk, HD), lambda h, j: (h, j, 0)), pl.BlockSpec((t, HD), lambda h, j: (0, h)),
                  pl.BlockSpec((1, 1, t), lambda h, j: (h, 0, 0)), pl.BlockSpec((1, 1, t), lambda h, j: (h, 0, 0))],
        out_specs=[pl.BlockSpec((1, t, QK), lambda h, j: (h, 0, 0)), pl.BlockSpec((1, tk, QK), lambda h, j: (h, j, 0)),
                   pl.BlockSpec((1, tk, HD), lambda h, j: (h, j, 0))],
        out_shape=[jax.ShapeDtypeStruct((NH, t, QK), F32), jax.ShapeDtypeStruct((NH, t, QK), F32),
                   jax.ShapeDtypeStruct((NH, t, HD), F32)],
        compiler_params=_params(("parallel", "arbitrary")),
    )(q, k, v, do, lse_row, delta_row)


def _ffn_in(xn, w_in, name):
    t, k = xn.shape
    s, _, n = w_in.shape
    tm = _pick(t, ROW_TILES)

    def body(x_ref, w_ref, gu_ref, a_ref):
        x = x_ref[...]
        g, u = _dot(x, w_ref[0]), _dot(x, w_ref[1])
        gu_ref[0] = g.astype(BF16)
        gu_ref[1] = u.astype(BF16)
        a_ref[...] = (g * _sig(g) * u).astype(BF16)

    return pl.pallas_call(
        body, name=name, grid=(t // tm, s // 2),
        in_specs=[_BS((tm, k), lambda i, j: (i, 0)), _BS((2, None, k, n), lambda i, j: (0, j, 0, 0))],
        out_specs=[_BS((2, None, tm, n), lambda i, j: (0, j, i, 0)), _BS((None, tm, n), lambda i, j: (j, i, 0))],
        out_shape=[jax.ShapeDtypeStruct((2, s // 2, t, n), BF16), jax.ShapeDtypeStruct((s // 2, t, n), BF16)],
        compiler_params=_params(("parallel", "parallel")),
    )(xn, w_in.reshape(2, s // 2, k, n))


def _ffn_dgu(dfo, w_out, gu, name):
    t, k = dfo.shape
    s, n, _ = w_out.shape
    tm = _pick(t, ROW_TILES)

    def body(d_ref, w_ref, gu_ref, o_ref):
        da = _dot_nt(d_ref[...], w_ref[...])
        g, u = gu_ref[0].astype(F32), gu_ref[1].astype(F32)
        sg = _sig(g)
        o_ref[0] = (da * u * sg * (1.0 + g * (1.0 - sg))).astype(BF16)
        o_ref[1] = (da * g * sg).astype(BF16)

    pair = _BS((2, None, tm, n), lambda i, j: (0, j, i, 0))
    return pl.pallas_call(
        body, name=name, grid=(t // tm, s),
        in_specs=[_BS((tm, k), lambda i, j: (i, 0)), _BS((None, n, k), lambda i, j: (j, 0, 0)), pair],
        out_specs=pair, out_shape=jax.ShapeDtypeStruct((2, s, t, n), BF16),
        compiler_params=_params(("parallel", "parallel")),
    )(dfo, w_out, gu).reshape(2 * s, t, n)


def _ffn_fwd(xn, w_in, w_out, tag, epilogue):
    gu, a = _ffn_in(xn, w_in, tag + "_in")
    return gu, a, _mm_stack_red(a, w_out, tag + "_out", "nn", epilogue)


def _ffn_bwd(dfo, xn, gu, a, w_in, w_out, tag, epilogue):
    dw_out = _mm_stack_tn(a, dfo, tag + "_dwout")
    dgu = _ffn_dgu(dfo, w_out, gu, tag + "_dgu")
    dw_in = _mm_stack_tn(dgu, xn, tag + "_dwin")
    if callable(epilogue):
        epilogue = epilogue(dw_in, dw_out)
    return _mm_stack_red(dgu, w_in, tag + "_dxn", "nt", epilogue), dw_in, dw_out


def _local_step(x, target, cs, sn, w, late_weights, s, early_grads, last_grads):
    t = x.shape[0]
    tm = _pick(t, (256, 128))
    g = {}
    ffn_out = lambda n: w[n].reshape(4, FF // 4, D)

    def norm_fn(x_ref, g_ref, o_ref):
        o_ref[...] = _rms(x_ref[...], g_ref[...]).astype(BF16)

    xn1 = _rows(norm_fn, "norm1", t, tm, [(x, D, 0)], [s["ffn1_norm"]], [(D, BF16)])[0]
    def res_norm_fn(scale):
        def fn(h_ref, f_ref, g_ref, h_out, n_out):
            h = h_ref[...] + scale * f_ref[...]
            h_out[...] = h
            n_out[...] = _rms(h, g_ref[...]).astype(BF16)
        return fn

    gu1, a1, (h1, u) = _ffn_fwd(xn1, w["ffn1_w_in"], ffn_out("ffn1_w_out"), "ffn1",
                                (res_norm_fn(0.5), x, [], [s["mix_norm"]], [(D, F32), (D, BF16)], []))
    w = {**w, **late_weights(h1)}
    rows_of = lambda n: w[n].reshape(-1, w[n].shape[2])
    w_hgb, w_mlab, w_o = rows_of("w_hg_branch"), rows_of("w_mla_branch"), rows_of("w_out")
    w_in_nat = w["w_in"].transpose(1, 0, 2).reshape(D, -1)
    w_mrg = w["w_merge"]
    mw = w_mrg.shape[2]
    w_in_hg, w_in_mla = w_in_nat[:, :4 * HGW], w_in_nat[:, 4 * HGW:]
    p_hg = _mm(u, w_in_hg, "nn", "proj_hg")
    p_mla = _mm(u, w_in_mla, "nn", "proj_mla")
    gpre = _cols_fwd(u, w_mrg, "proj_gate")
    o_pre, hgy, states = _hgrn_fwd(p_hg, s["hg_lb_table"], s["hg_out_norm"], t)
    prep_args = (p_mla, cs, sn, w["w_q_up"], w["w_kv_up"], s["mla_q_lora_norm"], s["mla_kv_lora_norm"],
                 s["q_head_norm"], s["k_head_norm"])
    q, k, v = _mla_prep_fwd(*prep_args, t)
    att, lse = _flash_fwd(q, k, v, t)
    y_hg = _mm(hgy, w_hgb, "nn", "branch_hg")
    y_mla = _mm(att, w_mlab, "nn", "branch_mla")

    def mix_fn(gh_ref, gm_ref, yh_ref, ym_ref, b_ref, o_ref):
        gh = _sig(gh_ref[...] + b_ref[:, 0:D])
        gm = _sig(gm_ref[...] + b_ref[:, D:2 * D])
        o_ref[...] = (gh * yh_ref[...] + gm * ym_ref[...]).astype(BF16)

    mixed = _rows(mix_fn, "mix", t, tm, [(gpre, D, 0), (gpre, D, 1), (y_hg, D, 0), (y_mla, D, 0)], [s["b_merge"]],
                  [(D, BF16)])[0]
    h2, xn2 = _mm_stack_red(mixed[None], w_o[None], "mix_out", "nn",
                            (res_norm_fn(1.0), h1, [], [s["ffn2_norm"]], [(D, F32), (D, BF16)], []))

    def loss_fn(h_ref, f_ref, tg_ref, g_ref, dh_out, dhb_out):
        h = h_ref[...] + 0.5 * f_ref[...]
        e = _rms(h, g_ref[...]) - tg_ref[...]
        dh, dgain = _rms_bwd(h, g_ref[...], e / D)
        dh_out[...] = dh
        dhb_out[...] = (0.5 * dh).astype(BF16)
        return dgain, jnp.full((1, LANE), 0.5 / D * jnp.sum(e * e), F32)

    gu2, a2, (dh3, dfo2, g["final_norm"], loss) = _ffn_fwd(
        xn2, w["ffn2_w_in"], ffn_out("ffn2_w_out"), "ffn2",
        (loss_fn, h2, [target], [s["final_norm"]], [(D, F32), (D, BF16)], [(1, D), (1, LANE)]))

    def norm_bwd_fn(scale):
        def fn(h_ref, dxn_ref, dh_ref, g_ref, dh_out, dhb_out):
            dx, dgain = _rms_bwd(h_ref[...], g_ref[...], dxn_ref[...])
            dh = dh_ref[...] + dx
            dh_out[...] = dh
            dhb_out[...] = (scale * dh).astype(BF16)
            return (dgain,)
        return fn

    as_rows = lambda a: a.reshape((N_DEV, -1) + a.shape[-1:])
    (dh2, dh2b, g["ffn2_norm"]), g["ffn2_w_in"], dwo = _ffn_bwd(
        dfo2, xn2, gu2, a2, w["ffn2_w_in"], ffn_out("ffn2_w_out"), "ffn2",
        (norm_bwd_fn(1.0), h2, [dh3], [s["ffn2_norm"]], [(D, F32), (D, BF16)], [(1, D)]))
    g["ffn2_w_out"] = as_rows(dwo)
    g["w_out"] = as_rows(_mm(mixed, dh2b, "tn", "mix_out_dw"))

    def mix_bwd_fn(gp_ref, dm_ref, yh_ref, ym_ref, b_ref, dyh_out, dym_out, dg_out):
        gh = _sig(gp_ref[:, 0:D] + b_ref[:, 0:D])
        gm = _sig(gp_ref[:, D:2 * D] + b_ref[:, D:2 * D])
        dm = dm_ref[...]
        dyh_out[...] = (dm * gh).astype(BF16)
        dym_out[...] = (dm * gm).astype(BF16)
        dgh = dm * yh_ref[...] * gh * (1.0 - gh)
        dgm = dm * ym_ref[...] * gm * (1.0 - gm)
        dg_out[:, 0:D] = dgh.astype(BF16)
        dg_out[:, D:2 * D] = dgm.astype(BF16)
        return (jnp.concatenate([jnp.sum(dgh, axis=0, keepdims=True), jnp.sum(dgm, axis=0, keepdims=True)], axis=1),)

    dyh, dym, dgpre, g["b_merge"] = _mm_stack_red(
        dh2b[None], w_o[None], "mix_out_dx", "nt",
        (mix_bwd_fn, gpre, [y_hg, y_mla], [s["b_merge"]], [(D, BF16), (D, BF16), (2 * D, BF16)], [(1, 2 * D)],
         (512, 256, 128)))
    g["w_hg_branch"] = as_rows(_mm(hgy, dyh, "tn", "branch_hg_dw"))
    g["w_mla_branch"] = as_rows(_mm(att, dym, "tn", "branch_mla_dw"))
    g["w_merge"] = _cols_dw(u, dgpre, mw, "proj_gate_dw")
    dhgy = _mm(dyh, w_hgb, "nt", "branch_hg_dx")
    datt, delta = _attn_out_bwd(dym, w_mlab, att, t)
    du_gate = _cols_dx(dgpre, w_mrg, "proj_gate_dx")

    dq, dk, dv = _flash_bwd(q, k, v, datt, lse, delta, t)
    (dp_mla, dwq, dwkv, g["mla_q_lora_norm"], g["mla_kv_lora_norm"], g["q_head_norm"],
     g["k_head_norm"]) = _mla_prep_bwd(*prep_args, dq, dk, dv, t)
    g["w_q_up"], g["w_kv_up"] = dwq.astype(GRAD_DT), dwkv.astype(GRAD_DT)
    dp_hg, g["hg_lb_table"], g["hg_out_norm"] = _hgrn_bwd(p_hg, s["hg_lb_table"], s["hg_out_norm"], o_pre, states,
                                                          dhgy, t)
    dw_in_nat = jnp.concatenate([_mm(u, dp_hg, "tn", "proj_hg_dw"), _mm(u, dp_mla, "tn", "proj_mla_dw")], axis=1)
    g["w_in"] = dw_in_nat.T.reshape(N_DEV, -1, D)
    du_mla = _mm(dp_mla, w_in_mla, "nt", "proj_mla_dx")

    def mixnorm_bwd_fn(h_ref, a_ref, b_ref, c_ref, dh_ref, g_ref, dh_out, dhb_out):
        dx, dgain = _rms_bwd(h_ref[...], g_ref[...], a_ref[...] + b_ref[...] + c_ref[...])
        dh = dh_ref[...] + dx
        dh_out[...] = dh
        dhb_out[...] = (0.5 * dh).astype(BF16)
        return (dgain,)

    mix_gain = s["mix_norm"] + early_grads(g)[0:1, 0:1]
    dh1, dfo1, g["mix_norm"] = _mm_stack_red(
        dp_hg[None], w_in_hg[None], "proj_hg_dx", "nt",
        (mixnorm_bwd_fn, h1, [du_mla, du_gate, dh2], [mix_gain], [(D, F32), (D, BF16)], [(1, D)], (512, 256, 128)))
    def last_stage(dw_in, dw_out):
        g["ffn1_w_in"], g["ffn1_w_out"] = dw_in, as_rows(dw_out)
        gain = s["ffn1_norm"] + last_grads(g)[0:1, 0:1]
        return norm_bwd_fn(1.0), x, [dh1], [gain], [(D, F32), (D, BF16)], [(1, D)]

    (grad_x, _, g["ffn1_norm"]), _, _ = _ffn_bwd(dfo1, xn1, gu1, a1, w["ffn1_w_in"], ffn_out("ffn1_w_out"), "ffn1",
                                                 last_stage)
    return loss, grad_x, g


def _coords():
    return lax.axis_index("x"), lax.axis_index("y"), lax.axis_index("c")


def _hbm_call(body, name, ins, out_shapes, scratch):
    any_spec = pl.BlockSpec(memory_space=pl.ANY)
    return pl.pallas_call(
        body, name=name, out_shape=[jax.ShapeDtypeStruct(s, dt) for s, dt in out_shapes],
        in_specs=[any_spec] * len(ins), out_specs=[any_spec] * len(out_shapes), scratch_shapes=scratch,
    )(*ins)


def _my_slot():
    return 4 * lax.axis_index("x") + 2 * lax.axis_index("y") + lax.axis_index("c")


def _put_own(buf, own, index):
    return lax.dynamic_update_index_in_dim(buf, own, index, 0)


def _all_gather(blocks, name):
    nb = len(blocks)

    def body(*refs):
        x_refs, out_refs = refs[:nb], refs[nb:2 * nb]
        send_sems, recv_sems = refs[2 * nb:]
        x, y, c = _coords()
        me, sibling = (x, y, c), (x, y, 1 - c)
        chips = [(1 - x, y), (x, 1 - y), (1 - x, 1 - y)]

        def slot(b, px, py, pc):
            return out_refs[b].at[4 * px + 2 * py + pc]

        def copy(b, kk, block_of, to, src=None):
            return pltpu.make_async_remote_copy(
                src_ref=slot(b, *block_of) if src is None else src, dst_ref=slot(b, *block_of),
                send_sem=send_sems.at[b, kk], recv_sem=recv_sems.at[b, kk], device_id=to, device_id_type=MESH)

        first = [copy(b, 0, me, sibling, src=x_refs[b]) for b in range(nb)]
        first += [copy(b, 1 + j, me, (*chip, c), src=x_refs[b]) for j, chip in enumerate(chips) for b in range(nb)]
        for cp in first:
            cp.start()
        passed = []
        for j, chip in enumerate(chips):
            for b in range(nb):
                copy(b, 1 + j, (*chip, c), me).wait_recv()
                passed.append(copy(b, 4 + j, (*chip, c), sibling))
                passed[-1].start()
        for b in range(nb):
            copy(b, 0, sibling, me).wait_recv()
        for j, chip in enumerate(chips):
            for b in range(nb):
                copy(b, 4 + j, (*chip, 1 - c), me).wait_recv()
        for cp in first + passed:
            cp.wait_send()

    outs = _hbm_call(body, name, blocks, [((N_DEV,) + b.shape, b.dtype) for b in blocks],
                     [pltpu.SemaphoreType.DMA((nb, 7)), pltpu.SemaphoreType.DMA((nb, 7))])
    return [_put_own(o, b[None], _my_slot()) for o, b in zip(outs, blocks)]


def _gather_peers():
    x, y, c = _coords()
    return (x, y, c), [(x, y, 1 - c), (1 - x, y, c), (x, 1 - y, c), (1 - x, 1 - y, c)]


def _gather_start(blocks, after, name):
    nb = len(blocks)
    hbm, sem = pl.BlockSpec(memory_space=pltpu.HBM), pl.BlockSpec(memory_space=pltpu.SEMAPHORE)

    def body(*refs):
        x_refs, out_refs = refs[:nb], refs[nb:2 * nb]
        send_sems, recv_sems, token = refs[2 * nb + 1], refs[2 * nb + 2], refs[-1]
        (x, y, c), peers = _gather_peers()
        for kk, peer in enumerate(peers):
            for b in range(nb):
                pltpu.make_async_remote_copy(
                    src_ref=x_refs[b], dst_ref=out_refs[b].at[4 * x + 2 * y + c], send_sem=send_sems.at[4 * b + kk],
                    recv_sem=recv_sems.at[4 * b + kk], device_id=peer, device_id_type=MESH).start()
        token[...] = jnp.zeros_like(token)

    gathers = [pltpu.with_memory_space_constraint(lax.empty((N_DEV,) + b.shape, b.dtype), pltpu.HBM) for b in blocks]
    outs = pl.pallas_call(
        body, name=name,
        out_shape=(pltpu.SemaphoreType.DMA((4 * nb,)), pltpu.SemaphoreType.DMA((4 * nb,)),
                   *[pltpu.HBM(b.shape, b.dtype) for b in blocks], *[pltpu.HBM(b.shape, b.dtype) for b in gathers],
                   jax.ShapeDtypeStruct((SUBLANE, LANE), F32)),
        in_specs=[hbm] * (2 * nb) + [pl.BlockSpec(memory_space=pl.ANY)],
        out_specs=(sem, sem, *[hbm] * (2 * nb), pl.BlockSpec(memory_space=pltpu.VMEM)),
        input_output_aliases={i: 2 + i for i in range(2 * nb)},
        compiler_params=pltpu.CompilerParams(has_side_effects=pltpu.SideEffectType.DATAFLOW_SIDE_EFFECTING),
    )(*[pltpu.with_memory_space_constraint(b, pltpu.HBM) for b in blocks], *gathers, after)
    return outs[0], outs[1], list(outs[2:2 + nb]), list(outs[2 + nb:2 + 2 * nb]), outs[-1]


def _gather_wait(send_sems, recv_sems, thru, gathers, after, name):
    nb = len(thru)
    hbm, sem = pl.BlockSpec(memory_space=pltpu.HBM), pl.BlockSpec(memory_space=pltpu.SEMAPHORE)

    def body(*refs):
        x_refs, out_refs = refs[:nb], refs[nb:2 * nb]
        send_sems_, recv_sems_ = refs[2 * nb], refs[2 * nb + 1]
        _, peers = _gather_peers()
        for kk, (px, py, pc) in enumerate(peers):
            for b in range(nb):
                cp = pltpu.make_async_remote_copy(
                    src_ref=x_refs[b], dst_ref=out_refs[b].at[4 * px + 2 * py + pc], send_sem=send_sems_.at[4 * b + kk],
                    recv_sem=recv_sems_.at[4 * b + kk], device_id=(px, py, pc), device_id_type=MESH)
                cp.wait_send()
                cp.wait_recv()

    outs = pl.pallas_call(
        body, name=name,
        out_shape=(*[pltpu.HBM(b.shape, b.dtype) for b in thru], *[pltpu.HBM(b.shape, b.dtype) for b in gathers]),
        in_specs=[hbm] * (2 * nb) + [sem, sem, pl.BlockSpec(memory_space=pl.ANY)], out_specs=[hbm] * (2 * nb),
        input_output_aliases={i: i for i in range(2 * nb)},
        compiler_params=pltpu.CompilerParams(has_side_effects=pltpu.SideEffectType.DATAFLOW_SIDE_EFFECTING),
    )(*thru, *gathers, send_sems, recv_sems, after)
    return list(outs[:nb]), list(outs[nb:])


def _gather_finish(blocks, gathers, name):
    nb = len(blocks)

    def body(*refs):
        x_refs, in_refs, out_refs = refs[:nb], refs[nb:2 * nb], refs[2 * nb:3 * nb]
        send_sems, recv_sems = refs[3 * nb:]
        (x, y, c), peers = _gather_peers()
        copies = []
        for j, (px, py, _) in enumerate(peers[1:]):
            for b in range(nb):
                copies.append(pltpu.make_async_remote_copy(
                    src_ref=in_refs[b].at[4 * px + 2 * py + c], dst_ref=out_refs[b].at[4 * px + 2 * py + c],
                    send_sem=send_sems.at[b, j], recv_sem=recv_sems.at[b, j], device_id=(x, y, 1 - c),
                    device_id_type=MESH))
                copies[-1].start()
        for j, (px, py, _) in enumerate(peers[1:]):
            for b in range(nb):
                pltpu.make_async_remote_copy(
                    src_ref=in_refs[b].at[4 * px + 2 * py + c], dst_ref=out_refs[b].at[4 * px + 2 * py + 1 - c],
                    send_sem=send_sems.at[b, j], recv_sem=recv_sems.at[b, j], device_id=(x, y, 1 - c),
                    device_id_type=MESH).wait_recv()
        for cp in copies:
            cp.wait_send()

    any_spec = pl.BlockSpec(memory_space=pl.ANY)
    outs = pl.pallas_call(
        body, name=name, out_shape=[jax.ShapeDtypeStruct(b.shape, b.dtype) for b in gathers],
        in_specs=[any_spec] * (2 * nb), out_specs=[any_spec] * nb,
        input_output_aliases={nb + i: i for i in range(nb)},
        scratch_shapes=[pltpu.SemaphoreType.DMA((nb, 3)), pltpu.SemaphoreType.DMA((nb, 3))],
    )(*blocks, *gathers)
    return [_put_own(o, b[None], _my_slot()) for o, b in zip(outs, blocks)]


def _sibling_swap(bufs, name):
    nb = len(bufs)

    def body(*refs):
        x_refs, out_refs = refs[:nb], refs[nb:2 * nb]
        send_sems, recv_sems = refs[2 * nb:]
        x, y, c = _coords()
        copies = [pltpu.make_async_remote_copy(
            src_ref=x_refs[b].at[2 * q + 1 - c], dst_ref=out_refs[b].at[q], send_sem=send_sems.at[b, q],
            recv_sem=recv_sems.at[b, q], device_id=(x, y, 1 - c), device_id_type=MESH)
            for b in range(nb) for q in range(4)]
        for cp in copies:
            cp.start()
        for cp in copies:
            cp.wait()

    return _hbm_call(body, name, bufs, [((4,) + b.shape[1:], b.dtype) for b in bufs],
                     [pltpu.SemaphoreType.DMA((nb, 4)), pltpu.SemaphoreType.DMA((nb, 4))])


def _chip_exchange_start(bufs, name):
    nb = len(bufs)
    hbm, sem = pl.BlockSpec(memory_space=pltpu.HBM), pl.BlockSpec(memory_space=pltpu.SEMAPHORE)

    def body(*refs):
        x_refs, land_refs = refs[:nb], refs[nb:2 * nb]
        send_sems, recv_sems, token = refs[2 * nb], refs[2 * nb + 1], refs[-1]
        x, y, c = _coords()
        for j, (px, py) in enumerate([(1 - x, y), (x, 1 - y), (1 - x, 1 - y)]):
            for b in range(nb):
                pltpu.make_async_remote_copy(
                    src_ref=x_refs[b].at[2 * px + py], dst_ref=land_refs[b].at[2 * x + y], send_sem=send_sems.at[3 * b + j],
                    recv_sem=recv_sems.at[3 * b + j], device_id=(px, py, c), device_id_type=MESH).start()
        token[...] = jnp.zeros_like(token)

    lands = [pltpu.with_memory_space_constraint(lax.empty(b.shape, b.dtype), pltpu.HBM) for b in bufs]
    outs = pl.pallas_call(
        body, name=name,
        out_shape=(pltpu.SemaphoreType.DMA((3 * nb,)), pltpu.SemaphoreType.DMA((3 * nb,)),
                   *[pltpu.HBM(b.shape, b.dtype) for b in bufs], *[pltpu.HBM(b.shape, b.dtype) for b in bufs],
                   jax.ShapeDtypeStruct((SUBLANE, LANE), F32)),
        in_specs=[hbm] * (2 * nb), out_specs=(sem, sem, *[hbm] * (2 * nb), pl.BlockSpec(memory_space=pltpu.VMEM)),
        input_output_aliases={i: 2 + i for i in range(2 * nb)},
        compiler_params=pltpu.CompilerParams(has_side_effects=pltpu.SideEffectType.DATAFLOW_SIDE_EFFECTING),
    )(*[pltpu.with_memory_space_constraint(b, pltpu.HBM) for b in bufs], *lands)
    return outs[0], outs[1], list(outs[2:2 + nb]), list(outs[2 + nb:2 + 2 * nb]), outs[-1]


def _chip_exchange_wait(send_sems, recv_sems, thru, lands, after, name):
    nb = len(thru)
    hbm, sem = pl.BlockSpec(memory_space=pltpu.HBM), pl.BlockSpec(memory_space=pltpu.SEMAPHORE)

    def body(*refs):
        x_refs, land_refs = refs[:nb], refs[nb:2 * nb]
        send_sems_, recv_sems_ = refs[2 * nb], refs[2 * nb + 1]
        x, y, c = _coords()
        for j, (px, py) in enumerate([(1 - x, y), (x, 1 - y), (1 - x, 1 - y)]):
            for b in range(nb):
                cp = pltpu.make_async_remote_copy(
                    src_ref=x_refs[b].at[2 * px + py], dst_ref=land_refs[b].at[2 * px + py],
                    send_sem=send_sems_.at[3 * b + j], recv_sem=recv_sems_.at[3 * b + j], device_id=(px, py, c),
                    device_id_type=MESH)
                cp.wait_send()
                cp.wait_recv()

    outs = pl.pallas_call(
        body, name=name,
        out_shape=(*[pltpu.HBM(b.shape, b.dtype) for b in thru], *[pltpu.HBM(b.shape, b.dtype) for b in lands]),
        in_specs=[hbm] * (2 * nb) + [sem, sem, pl.BlockSpec(memory_space=pl.ANY)], out_specs=[hbm] * (2 * nb),
        input_output_aliases={i: i for i in range(2 * nb)},
        compiler_params=pltpu.CompilerParams(has_side_effects=pltpu.SideEffectType.DATAFLOW_SIDE_EFFECTING),
    )(*thru, *lands, send_sems, recv_sems, after)
    return list(outs[:nb]), list(outs[nb:])


def _chip_sum(g, r1, c, name):
    _, r, cw = g.shape
    tr = _pick(r, (256, 176, 128))

    def body(c_ref, g_ref, r_ref, o_ref):
        o_ref[...] = (g_ref[...].astype(F32) + r_ref[...].astype(F32)).astype(GRAD_DT)

    grid_spec = pltpu.PrefetchScalarGridSpec(
        num_scalar_prefetch=1, grid=(4, r // tr),
        in_specs=[_BS((None, None, tr, cw), lambda q, i, c_ref: (q, c_ref[0], i, 0)),
                  _BS((None, tr, cw), lambda q, i, c_ref: (q, i, 0))],
        out_specs=_BS((None, tr, cw), lambda q, i, c_ref: (q, i, 0)))
    return pl.pallas_call(
        body, name=name, grid_spec=grid_spec, out_shape=jax.ShapeDtypeStruct((4, r, cw), GRAD_DT),
        compiler_params=_params(("parallel", "parallel")),
    )(c.reshape(1).astype(jnp.int32), g.reshape(4, 2, r, cw), r1)


def _adamw_math(w, g, m, v):
    m = B1 * m + (1.0 - B1) * g
    v = B2 * v + (1.0 - B2) * (g * g)
    m_hat = m / (1.0 - B1 ** STEP)
    v_hat = v / (1.0 - B2 ** STEP)
    return -LR * (m_hat / (jnp.sqrt(v_hat) + AEPS) + WD * w), m, v


def _sum_adamw(parts, w, m, v, name):
    r, c = w.shape
    tr = _pick(r, (256, 176, 128))
    tc = _pick(c, (256, 128)) if tr == r and r > 256 else c

    def body(p0, p1, p2, p3, w_ref, m_ref, v_ref, g_out, d_out, m_out, v_out):
        g = ((p0[...].astype(F32) + p1[...].astype(F32)) + p2[...].astype(F32)) + p3[...].astype(F32)
        g_out[...] = g
        d_out[...], m_out[...], v_out[...] = _adamw_math(w_ref[...], g, m_ref[...], v_ref[...])

    part = lambda q: _BS((None, tr, tc), functools.partial(lambda i, j, q: (q, i, j), q=q))
    plain = _BS((tr, tc), lambda i, j: (i, j))
    return pl.pallas_call(
        body, name=name, grid=(r // tr, c // tc), in_specs=[part(q) for q in range(4)] + [plain] * 3,
        out_specs=[plain] * 4, out_shape=[jax.ShapeDtypeStruct((r, c), F32)] * 4,
        compiler_params=_params(("parallel", "parallel")),
    )(parts, parts, parts, parts, w, m, v)


def _small_update(gathered, w, m, v):
    r = w.shape[0]

    def body(ga_ref, w_ref, m_ref, v_ref, g_out, d_out, m_out, v_out):
        g = ga_ref[0]
        for dev in range(1, N_DEV):
            g = g + ga_ref[dev]
        g_out[...] = g
        d_out[...], m_out[...], v_out[...] = _adamw_math(w_ref[...], g, m_ref[...], v_ref[...])

    return pl.pallas_call(
        body, name="small_update", out_shape=[jax.ShapeDtypeStruct((r, LANE), F32)] * 4,
    )(gathered, w, m, v)


def _pack_small(vals):
    rows = []
    for name, (r, n) in SMALL:
        flat = vals[name].reshape(-1)
        pad = (-flat.shape[0]) % (SUBLANE * LANE)
        rows.append(jnp.pad(flat, (0, pad)).reshape(-1, LANE))
    return jnp.concatenate(rows, axis=0)


def _unpack_small(packed):
    out, off = {}, 0
    for name, (r, n) in SMALL:
        nrow = -(-(r * n) // (SUBLANE * LANE)) * SUBLANE
        out[name] = packed[off:off + nrow].reshape(-1)[:r * n].reshape(r, n)
        off += nrow
    return out


def kernel(x, positions, ffn1_norm, ffn1_w_in, ffn1_w_out, mix_norm, w_in, hg_lb_table, hg_out_norm, w_hg_branch, mla_q_lora_norm, w_q_up, mla_kv_lora_norm, w_kv_up, q_head_norm, k_head_norm, w_mla_branch, w_merge, b_merge, w_out, ffn2_norm, ffn2_w_in, ffn2_w_out, final_norm, loss_target, m_ffn1_norm, m_ffn1_w_in, m_ffn1_w_out, m_mix_norm, m_w_in, m_hg_lb_table, m_hg_out_norm, m_w_hg_branch, m_mla_q_lora_norm, m_w_q_up, m_mla_kv_lora_norm, m_w_kv_up, m_q_head_norm, m_k_head_norm, m_w_mla_branch, m_w_merge, m_b_merge, m_w_out, m_ffn2_norm, m_ffn2_w_in, m_ffn2_w_out, m_final_norm, v_ffn1_norm, v_ffn1_w_in, v_ffn1_w_out, v_mix_norm, v_w_in, v_hg_lb_table, v_hg_out_norm, v_w_hg_branch, v_mla_q_lora_norm, v_w_q_up, v_mla_kv_lora_norm, v_w_kv_up, v_q_head_norm, v_k_head_norm, v_w_mla_branch, v_w_merge, v_b_merge, v_w_out, v_ffn2_norm, v_ffn2_w_in, v_ffn2_w_out, v_final_norm):
    args = dict(locals())
    t = x.shape[1]
    big_w = {n: args[n][0] for n, _, _ in BIG}
    small = {n: args[n].reshape(shape) for n, shape in SMALL}

    names = [n for n, _, _ in BIG]
    first, rest = names[:2], names[2:]
    full = dict(zip(first, _all_gather([big_w[n].astype(BF16) for n in first], "weights_all_gather_ffn1")))
    g_send, g_recv, g_thru, g_bufs, g_token = _gather_start([big_w[n].astype(BF16) for n in rest], full[first[0]],
                                                            "weights_gather_start")
    gains = dict(small, ffn1_norm=small["ffn1_norm"] + g_token[0:1, 0:1])

    def late_weights(after):
        blocks, bufs = _gather_wait(g_send, g_recv, g_thru, g_bufs, after, "weights_gather_wait")
        return dict(zip(rest, _gather_finish(blocks, bufs, "weights_gather_finish")))

    inv_freq = ROPE_THETA ** (-jnp.arange(0, ROPE, 2, dtype=F32) / ROPE)
    ang = positions[0].astype(F32)[:, None] * inv_freq
    cs = jnp.concatenate([jnp.cos(ang), jnp.cos(ang)], axis=1)
    sn = jnp.concatenate([jnp.sin(ang), jnp.sin(ang)], axis=1)

    c = lax.axis_index("c")
    chip = 2 * lax.axis_index("x") + lax.axis_index("y")
    early = {}

    def chip_sums_of(g, ns, tag):
        from_sibling = _sibling_swap([g[n] for n in ns], "grads_sibling_swap_" + tag)
        return [_chip_sum(g[n], r1, c, "chip_sum_" + n) for n, r1 in zip(ns, from_sibling)]

    def early_grads(g):
        early["names"] = [n for n in names if n in g]
        early["sums"] = chip_sums_of(g, early["names"], "early")
        early["send"], early["recv"], early["thru"], early["lands"], token = _chip_exchange_start(
            early["sums"], "grads_exchange_start")
        return token

    def last_grads(g):
        early["late"] = [n for n in names if n not in early["names"]]
        early["l_send"], early["l_recv"], early["l_thru"], early["l_lands"], token = _chip_exchange_start(
            chip_sums_of(g, early["late"], "late"), "grads_exchange_late_start")
        return token

    loss_row, grad_x, g = _local_step(x[0], loss_target[0], cs, sn, full, late_weights, gains, early_grads,
                                      last_grads)
    late, l_send, l_recv = early["late"], early["l_send"], early["l_recv"]
    l_thru, l_lands = early["l_thru"], early["l_lands"]
    sent, landed = _chip_exchange_wait(early["send"], early["recv"], early["thru"], early["lands"], grad_x,
                                       "grads_exchange_wait")
    exchanged = {n: _put_own(land, lax.dynamic_index_in_dim(own, chip, 0), chip)
                 for n, land, own in zip(early["names"], landed, sent)}

    small_packed = jnp.concatenate([_pack_small(g), jnp.pad(loss_row, ((0, SUBLANE - 1), (0, 0)))], axis=0)
    small_all = _all_gather([small_packed], "small_all_gather")[0]
    zero_tail = jnp.zeros((SUBLANE, LANE), F32)
    pk = lambda d: jnp.concatenate([_pack_small(d), zero_tail], axis=0)
    sg, sd, sm, sv = _small_update(
        small_all, pk(small), pk({n: args["m_" + n].reshape(shape) for n, shape in SMALL}),
        pk({n: args["v_" + n].reshape(shape) for n, shape in SMALL}))
    n_small_rows = _pack_small(small).shape[0]
    loss = sg[n_small_rows, 0]
    outs = {k_: _unpack_small(a) for k_, a in (("grad", sg), ("delta", sd), ("new_m", sm), ("new_v", sv))}

    def adamw(n):
        tr = (lambda a: a.T) if n in GRAD_T else (lambda a: a)
        res = _sum_adamw(exchanged[n], tr(big_w[n]), tr(args["m_" + n][0]), tr(args["v_" + n][0]), "adamw_" + n)
        outs["grad"][n], outs["delta"][n], outs["new_m"][n], outs["new_v"][n] = [tr(a) for a in res]

    for n in early["names"]:
        adamw(n)
    sent, landed = _chip_exchange_wait(l_send, l_recv, l_thru, l_lands, outs["delta"][early["names"][-1]],
                                       "grads_exchange_late_wait")
    for n, land, own in zip(late, landed, sent):
        exchanged[n] = _put_own(land, lax.dynamic_index_in_dim(own, chip, 0), chip)
        adamw(n)

    def shaped(kind, n):
        return outs[kind][n].reshape(args[n].shape)

    return (loss, grad_x[None], *[shaped("grad", n) for n in WEIGHT_ORDER], *[shaped("delta", n) for n in WEIGHT_ORDER],
            *[shaped("new_m", n) for n in WEIGHT_ORDER], *[shaped("new_v", n) for n in WEIGHT_ORDER])
```

```python
import functools

import jax
import jax.numpy as jnp
from jax import lax
from jax.experimental import pallas as pl
from jax.experimental.pallas import tpu as pltpu

F32 = jnp.float32
BF16 = jnp.bfloat16

D = 1024
FF = 2816
NH = 8
HD = 128
ROPE = 64
QK = HD + ROPE
QL = 384
KVL = 256
HGW = NH * HD
CHUNK = 64
EPS = 1e-6
ROPE_THETA = 10000.0
SCALE = QK ** -0.5
LOG2E = 1.4426950408889634

LR, B1, B2, AEPS, WD, STEP = 0.001, 0.9, 0.999, 1e-08, 0.01, 10

HB = 128
SUB = 16
EXP_CLAMP = 80.0
ATT_TILES = (512, 256, 128)
ATT_Q = (1024, 512, 256, 128)
ATT_KEY_TILES = 1
GRAD_DT = BF16
ROW_TILES = (1024, 512, 256, 128)

LANE = 128
SUBLANE = 8
VMEM_LIMIT = 56 << 20

N_DEV = 8
MESH = pl.DeviceIdType.MESH

BIG = (
    ("ffn1_w_in", (D, 2 * FF), 1), ("ffn1_w_out", (FF, D), 0), ("w_in", (D, 4800), 1),
    ("w_hg_branch", (HGW, D), 0), ("w_q_up", (QL, NH * QK), 1), ("w_kv_up", (KVL, NH * 2 * HD), 1),
    ("w_mla_branch", (NH * HD, D), 0), ("w_merge", (D, 2 * D), 1), ("w_out", (D, D), 0),
    ("ffn2_w_in", (D, 2 * FF), 1), ("ffn2_w_out", (FF, D), 0),
)
SMALL = (
    ("ffn1_norm", (1, D)), ("mix_norm", (1, D)), ("hg_lb_table", (2, HGW)), ("hg_out_norm", (1, HD)),
    ("mla_q_lora_norm", (1, QL)), ("mla_kv_lora_norm", (1, KVL)), ("q_head_norm", (1, QK)),
    ("k_head_norm", (1, QK)), ("b_merge", (1, 2 * D)), ("ffn2_norm", (1, D)), ("final_norm", (1, D)),
)
GRAD_T = ("ffn1_w_in", "ffn2_w_in", "w_in", "w_q_up")
WEIGHT_ORDER = ("ffn1_norm", "ffn1_w_in", "ffn1_w_out", "mix_norm", "w_in", "hg_lb_table", "hg_out_norm",
                "w_hg_branch", "mla_q_lora_norm", "w_q_up", "mla_kv_lora_norm", "w_kv_up", "q_head_norm",
                "k_head_norm", "w_mla_branch", "w_merge", "b_merge", "w_out", "ffn2_norm", "ffn2_w_in",
                "ffn2_w_out", "final_norm")


def _pick(n, cands):
    for c in cands:
        if n % c == 0:
            return c
    return n


def _params(sem):
    return pltpu.CompilerParams(dimension_semantics=sem, vmem_limit_bytes=VMEM_LIMIT)


def _sig(x):
    return 0.5 * jnp.tanh(0.5 * x) + 0.5


def _dot(a, b):
    return jnp.dot(a.astype(BF16), b.astype(BF16), preferred_element_type=F32)


def _dot_nt(a, b):
    return lax.dot_general(a.astype(BF16), b.astype(BF16), (((1,), (1,)), ((), ())),
                           preferred_element_type=F32)


def _dot_tn(a, b):
    return lax.dot_general(a.astype(BF16), b.astype(BF16), (((0,), (0,)), ((), ())),
                           preferred_element_type=F32)


def _split3(x):
    x1 = x.astype(BF16)
    r1 = x - x1.astype(F32)
    x2 = r1.astype(BF16)
    x3 = (r1 - x2.astype(F32)).astype(BF16)
    return x1, x2, x3


def _dot_sel(m, x):
    x1, x2, x3 = _split3(x)
    d = lambda p: jnp.dot(m, p, preferred_element_type=F32)
    return d(x1) + d(x2) + d(x3)


def _sel_dot(x, m):
    x1, x2, x3 = _split3(x)
    d = lambda p: jnp.dot(p, m, preferred_element_type=F32)
    return d(x1) + d(x2) + d(x3)


_TN = (1408, 1024, 768, 512, 384, 256, 128)


def _accumulate(acc, o_ref, axis, steps, term):
    step = pl.program_id(axis)

    @pl.when(step == 0)
    def _():
        acc[...] = jnp.zeros_like(acc)

    val = term()
    if isinstance(val, (list, tuple)):
        for e, v in enumerate(val):
            acc[e] += v
    else:
        acc[...] += val

    @pl.when(step == steps - 1)
    def _():
        o_ref[...] = acc[...].astype(o_ref.dtype)


def _mm(a, b, mode, name, out_dtype=F32):
    if mode == "tn":
        t, m = a.shape
        n = b.shape[1]
        tt, tm, tn = _pick(t, ROW_TILES), _pick(m, _TN), _pick(n, _TN)

        def body(a_ref, b_ref, o_ref, acc):
            _accumulate(acc, o_ref, 2, t // tt, lambda: _dot_tn(a_ref[...], b_ref[...]))

        return pl.pallas_call(
            body, name=name, grid=(m // tm, n // tn, t // tt),
            in_specs=[pl.BlockSpec((tt, tm), lambda i, j, k: (k, i)),
                      pl.BlockSpec((tt, tn), lambda i, j, k: (k, j))],
            out_specs=pl.BlockSpec((tm, tn), lambda i, j, k: (i, j)),
            out_shape=jax.ShapeDtypeStruct((m, n), GRAD_DT), scratch_shapes=[pltpu.VMEM((tm, tn), F32)],
            compiler_params=_params(("parallel", "parallel", "arbitrary")),
        )(a, b)

    m, k = a.shape
    tm = _pick(m, ROW_TILES)
    if mode == "nn":
        n = b.shape[1]
        tn = _pick(n, _TN)
        b_spec = pl.BlockSpec((k, tn), lambda i, j: (0, j))
        dot = _dot
    else:
        n = b.shape[0]
        tn = _pick(n, _TN if k <= 4096 else (512, 256, 128))
        b_spec = pl.BlockSpec((tn, k), lambda i, j: (j, 0))
        dot = _dot_nt

    def body(a_ref, b_ref, o_ref):
        o_ref[...] = dot(a_ref[...], b_ref[...]).astype(o_ref.dtype)

    return pl.pallas_call(
        body, name=name, grid=(m // tm, n // tn),
        in_specs=[pl.BlockSpec((tm, k), lambda i, j: (i, 0)), b_spec],
        out_specs=pl.BlockSpec((tm, tn), lambda i, j: (i, j)),
        out_shape=jax.ShapeDtypeStruct((m, n), out_dtype),
        compiler_params=_params(("parallel", "parallel")),
    )(a, b)


_DOTS = {"nn": _dot, "nt": _dot_nt, "tn": _dot_tn}
_BS = pl.BlockSpec


def _mmcall(name, kind, a, b, a_spec, b_spec, o_spec, o_shape, grid, red_axis=None, out_dtype=F32):
    dot = _DOTS[kind]

    def body(a_ref, b_ref, o_ref):
        if red_axis is None:
            o_ref[...] = dot(a_ref[...], b_ref[...]).astype(o_ref.dtype)
        else:
            @pl.when(pl.program_id(red_axis) == 0)
            def _():
                o_ref[...] = jnp.zeros_like(o_ref)

            o_ref[...] += dot(a_ref[...], b_ref[...])

    sem = tuple("arbitrary" if ax == red_axis else "parallel" for ax in range(len(grid)))
    return pl.pallas_call(
        body, name=name, grid=grid, in_specs=[a_spec, b_spec], out_specs=o_spec,
        out_shape=jax.ShapeDtypeStruct(o_shape, out_dtype), compiler_params=_params(sem),
    )(a, b)


class _AsRef:
    def __init__(self, value):
        self.value = value

    def __getitem__(self, idx):
        return self.value


def _mm_stack_red(a, w, name, kind, epilogue=None):
    s, t, n = a.shape
    nout = w.shape[2] if kind == "nn" else w.shape[1]
    if epilogue is None:
        tm = _pick(t, ROW_TILES)
        return _mmcall(name, kind, a, w, _BS((None, tm, n), lambda i, j: (j, i, 0)),
                       _BS((None,) + w.shape[1:], lambda i, j: (j, 0, 0)), _BS((tm, nout), lambda i, j: (i, 0)),
                       (t, nout), (t // tm, s), red_axis=1)
    fn, first, rows, vecs, outs, accs = epilogue[:6]
    rows = [first] + list(rows)
    tm = _pick(t, epilogue[6] if len(epilogue) > 6 else ROW_TILES)
    n_row, n_vec, n_out = len(rows), len(vecs), len(outs)
    dot = _DOTS[kind]

    def body(*refs):
        a_ref, w_ref, prod = refs[0], refs[1], refs[-1]
        row_refs = refs[2:2 + n_row]
        vec_refs = refs[2 + n_row:2 + n_row + n_vec]
        out_refs = refs[2 + n_row + n_vec:2 + n_row + n_vec + n_out]
        acc_refs = refs[2 + n_row + n_vec + n_out:-1]
        i, j = pl.program_id(0), pl.program_id(1)

        @pl.when(j == 0)
        def _():
            prod[...] = jnp.zeros_like(prod)

        prod[...] += dot(a_ref[...], w_ref[...])

        @pl.when(j == s - 1)
        def _():
            res = fn(row_refs[0], _AsRef(prod[...]), *row_refs[1:], *vec_refs, *out_refs)
            if acc_refs:
                @pl.when(i == 0)
                def _():
                    for r in acc_refs:
                        r[...] = jnp.zeros_like(r)

                for r, val in zip(acc_refs, res):
                    r[...] += val

    in_specs = [_BS((None, tm, n), lambda i, j: (j, i, 0)), _BS((None,) + w.shape[1:], lambda i, j: (j, 0, 0))]
    in_specs += [_BS((tm, r.shape[1]), lambda i, j: (i, 0)) for r in rows]
    in_specs += [_BS(v.shape, lambda i, j: (0, 0)) for v in vecs]
    out_specs = [_BS((tm, wd), lambda i, j: (i, 0)) for wd, _ in outs] + [_BS(sh, lambda i, j: (0, 0)) for sh in accs]
    out_shape = [jax.ShapeDtypeStruct((t, wd), dt) for wd, dt in outs] + [jax.ShapeDtypeStruct(sh, F32) for sh in accs]
    return pl.pallas_call(
        body, name=name, grid=(t // tm, s), in_specs=in_specs, out_specs=out_specs, out_shape=out_shape,
        scratch_shapes=[pltpu.VMEM((tm, nout), F32)],
        compiler_params=_params(("arbitrary" if accs else "parallel", "arbitrary")),
    )(a, w, *rows, *vecs)


def _mm_stack_tn(a, b, name):
    grp = 4
    if a.ndim == 2:
        t, k = a.shape
        s, _, n = b.shape
        tt = _pick(t, ROW_TILES)
        a_spec = _BS((tt, k), lambda j, r: (r, 0))
        b_spec, b_in = _BS((None, grp, tt, n), lambda j, r: (j, 0, r, 0)), b.reshape(s // grp, grp, t, n)
        a_in = a
    else:
        s, t, k = a.shape
        n = b.shape[1]
        tt = _pick(t, ROW_TILES)
        a_spec, a_in = _BS((None, grp, tt, k), lambda j, r: (j, 0, r, 0)), a.reshape(s // grp, grp, t, k)
        b_spec, b_in = _BS((tt, n), lambda j, r: (r, 0)), b

    def body(a_ref, b_ref, o_ref, acc):
        def terms():
            shared = a_ref[...] if a.ndim == 2 else b_ref[...]
            return [_dot_tn(shared, b_ref[e]) if a.ndim == 2 else _dot_tn(a_ref[e], shared) for e in range(grp)]

        _accumulate(acc, o_ref, 1, t // tt, terms)

    return pl.pallas_call(
        body, name=name, grid=(s // grp, t // tt), in_specs=[a_spec, b_spec],
        out_specs=_BS((None, grp, k, n), lambda j, r: (j, 0, 0, 0)),
        out_shape=jax.ShapeDtypeStruct((s // grp, grp, k, n), GRAD_DT), scratch_shapes=[pltpu.VMEM((grp, k, n), F32)],
        compiler_params=_params(("parallel", "arbitrary")),
    )(a_in, b_in).reshape(s, k, n)


def _cols_fwd(x, w, name):
    t, k = x.shape
    s, _, n = w.shape
    tm = _pick(t, ROW_TILES)

    def body(x_ref, w_ref, o_ref):
        x_ = x_ref[...]
        for j in range(s):
            o_ref[:, n * j:n * (j + 1)] = _dot(x_, w_ref[j])

    return pl.pallas_call(
        body, name=name, grid=(t // tm,),
        in_specs=[_BS((tm, k), lambda i: (i, 0)), _BS((s, k, n), lambda i: (0, 0, 0))],
        out_specs=_BS((tm, s * n), lambda i: (i, 0)), out_shape=jax.ShapeDtypeStruct((t, s * n), F32),
        compiler_params=_params(("parallel",)),
    )(x, w)


def _cols_dx(d, w, name):
    t = d.shape[0]
    s, k, n = w.shape
    tm = _pick(t, ROW_TILES)

    def body(d_ref, w_ref, o_ref):
        acc = _dot_nt(d_ref[:, 0:n], w_ref[0])
        for j in range(1, s):
            acc = acc + _dot_nt(d_ref[:, n * j:n * (j + 1)], w_ref[j])
        o_ref[...] = acc

    return pl.pallas_call(
        body, name=name, grid=(t // tm,),
        in_specs=[_BS((tm, s * n), lambda i: (i, 0)), _BS((s, k, n), lambda i: (0, 0, 0))],
        out_specs=_BS((tm, k), lambda i: (i, 0)), out_shape=jax.ShapeDtypeStruct((t, k), F32),
        compiler_params=_params(("parallel",)),
    )(d, w)


def _cols_dw(x, d, n, name):
    t, k = x.shape
    s = d.shape[1] // n
    tt = _pick(t, ROW_TILES)

    def body(x_ref, d_ref, o_ref, acc):
        def terms():
            x_ = x_ref[...]
            return [_dot_tn(x_, d_ref[:, n * j:n * (j + 1)]) for j in range(s)]

        _accumulate(acc, o_ref, 0, t // tt, terms)

    return pl.pallas_call(
        body, name=name, grid=(t // tt,),
        in_specs=[_BS((tt, k), lambda r: (r, 0)), _BS((tt, s * n), lambda r: (r, 0))],
        out_specs=_BS((s, k, n), lambda r: (0, 0, 0)), out_shape=jax.ShapeDtypeStruct((s, k, n), GRAD_DT),
        scratch_shapes=[pltpu.VMEM((s, k, n), F32)], compiler_params=_params(("arbitrary",)),
    )(x, d)


def _rows(fn, name, t, tm, ins, vecs, outs, accs=()):
    n_in, n_out, n_acc = len(ins) + len(vecs), len(outs), len(accs)

    def body(*refs):
        res = fn(*refs[:n_in + n_out])
        if n_acc:
            acc_refs = refs[n_in + n_out:]

            @pl.when(pl.program_id(0) == 0)
            def _():
                for r in acc_refs:
                    r[...] = jnp.zeros_like(r)

            for r, val in zip(acc_refs, res):
                r[...] += val

    in_specs = [pl.BlockSpec((tm, bw), functools.partial(lambda i, cb: (i, cb), cb=cb)) for _, bw, cb in ins]
    in_specs += [pl.BlockSpec(v.shape, lambda i: (0, 0)) for v in vecs]
    out_specs = [pl.BlockSpec((tm, w), lambda i: (i, 0)) for w, _ in outs]
    out_specs += [pl.BlockSpec(s, lambda i: (0, 0)) for s in accs]
    out_shape = [jax.ShapeDtypeStruct((t, w), dt) for w, dt in outs]
    out_shape += [jax.ShapeDtypeStruct(s, F32) for s in accs]
    return pl.pallas_call(
        body, name=name, grid=(t // tm,), in_specs=in_specs, out_specs=out_specs, out_shape=out_shape,
        compiler_params=_params(("arbitrary",) if n_acc else ("parallel",)),
    )(*[a for a, _, _ in ins], *vecs)


def _rms(x, g):
    return x * lax.rsqrt(jnp.mean(x * x, axis=-1, keepdims=True) + EPS) * g


def _rms_bwd(x, g, dy):
    xh = x * lax.rsqrt(jnp.mean(x * x, axis=-1, keepdims=True) + EPS)
    r = lax.rsqrt(jnp.mean(x * x, axis=-1, keepdims=True) + EPS)
    dyg = dy * g
    dx = r * (dyg - xh * jnp.mean(dyg * xh, axis=-1, keepdims=True))
    return dx, jnp.sum(dy * xh, axis=0, keepdims=True)


def _hgrn_mats():
    row = lax.broadcasted_iota(jnp.int32, (HB, HB), 0)
    col = lax.broadcasted_iota(jnp.int32, (HB, HB), 1)
    return row, col


def _hgrn_gates(qr, z, t0, t1):
    lb = 1.0 / (1.0 + jnp.exp(t1 - t0))
    th = jnp.tanh(0.5 * z)
    sz, sneg = 0.5 + 0.5 * th, 0.5 - 0.5 * th
    f = lb + (1.0 - lb) * sz
    return lb, sz, sneg, f, jnp.log(f), (1.0 - lb) * sneg, qr * _sig(qr)


def _hgrn_scores(q, k, cum):
    heads, nsub = range(len(q)), HB // SUB
    mids = [[cum[h][SUB * i + SUB // 2 - 1:SUB * i + SUB // 2, :] for i in range(nsub)] for h in heads]
    cmid = [jnp.concatenate([cum[h][SUB * i:SUB * (i + 1)] - mids[h][i] for i in range(nsub)], axis=0) for h in heads]
    qd = [q[h] * jnp.exp(jnp.minimum(cmid[h], EXP_CLAMP)) for h in heads]
    qd_b = [x.astype(BF16) for x in qd]
    kds = [[None] * nsub for _ in heads]
    parts = [[None] * nsub for _ in heads]
    for i in range(nsub):
        for h in heads:
            kds[h][i] = k[h] * jnp.exp(jnp.minimum(mids[h][i] - cum[h], EXP_CLAMP))
        for h in heads:
            parts[h][i] = _dot_nt(qd_b[h][SUB * i:SUB * (i + 1)], kds[h][i])
    return qd, kds, [jnp.concatenate(parts[h], axis=0) for h in heads], cmid


def _hgrn_fwd(p_hg, table, gain, t):
    nblk = t // HB

    def body(q_ref, f_ref, i_ref, g_ref, tab_ref, gain_ref, o_ref, y_ref, st_ref, state):
        @pl.when(pl.program_id(0) == 0)
        def _():
            state[...] = jnp.zeros_like(state)

        row, col = _hgrn_mats()
        causal = col <= row
        tri = causal.astype(BF16)
        heads = range(NH)
        sl = [slice(HD * h, HD * (h + 1)) for h in heads]
        gates = [_hgrn_gates(q_ref[:, sl[h]], f_ref[:, sl[h]], tab_ref[0:1, sl[h]], tab_ref[1:2, sl[h]]) for h in heads]
        lf, k, q = [g[4] for g in gates], [g[5] for g in gates], [g[6] for g in gates]
        v = [i_ref[:, sl[h]] for h in heads]
        cum = [_dot_sel(tri, lf[h]) for h in heads]
        _, _, s, _ = _hgrn_scores(q, k, cum)
        p = [jnp.where(causal, s[h], 0.0) for h in heads]
        st = [state[h] for h in heads]
        o = [_dot(p[h], v[h]) + _dot_nt(q[h] * jnp.exp(cum[h]), st[h]) for h in heads]
        last = [cum[h][HB - 1:HB, :] for h in heads]
        new_st = [st[h] * jnp.exp(last[h]) + _dot_tn(v[h], k[h] * jnp.exp(last[h] - cum[h])) for h in heads]
        for h in heads:
            gr = g_ref[:, sl[h]]
            st_ref[h, 0] = st[h]
            state[h] = new_st[h]
            o_ref[:, sl[h]] = o[h]
            y_ref[:, sl[h]] = (_rms(o[h], gain_ref[...]) * gr * _sig(gr)).astype(BF16)

    blk = lambda cb: pl.BlockSpec((HB, HGW), functools.partial(lambda n, cb: (n, cb), cb=cb))
    return pl.pallas_call(
        body, name="hgrn_fwd", grid=(nblk,),
        in_specs=[blk(0), blk(1), blk(2), blk(3), pl.BlockSpec((2, HGW), lambda n: (0, 0)),
                  pl.BlockSpec((1, HD), lambda n: (0, 0))],
        out_specs=[pl.BlockSpec((HB, HGW), lambda n: (n, 0)), pl.BlockSpec((HB, HGW), lambda n: (n, 0)),
                   pl.BlockSpec((NH, 1, HD, HD), lambda n: (0, n, 0, 0))],
        out_shape=[jax.ShapeDtypeStruct((t, HGW), F32), jax.ShapeDtypeStruct((t, HGW), BF16),
                   jax.ShapeDtypeStruct((NH, nblk, HD, HD), F32)],
        scratch_shapes=[pltpu.VMEM((NH, HD, HD), F32)],
        compiler_params=_params(("arbitrary",)),
    )(p_hg, p_hg, p_hg, p_hg, table, gain)


def _hgrn_bwd(p_hg, table, gain, o_pre, states, dy, t):
    nblk = t // HB

    def body(q_ref, f_ref, i_ref, g_ref, tab_ref, gain_ref, o_ref, st_ref, dy_ref, dp_ref, dtab_ref, dgain_ref,
             dstate):
        @pl.when(pl.program_id(0) == 0)
        def _():
            dstate[...] = jnp.zeros_like(dstate)
            dtab_ref[...] = jnp.zeros_like(dtab_ref)
            dgain_ref[...] = jnp.zeros_like(dgain_ref)

        row, col = _hgrn_mats()
        causal = col <= row
        tri = causal.astype(BF16)
        tri_t = (row <= col).astype(BF16)
        heads, nsub = range(NH), HB // SUB
        sl = [slice(HD * h, HD * (h + 1)) for h in heads]
        rows_of = lambda i: slice(SUB * i, SUB * (i + 1))
        gain_ = gain_ref[...]
        qr, z = [q_ref[:, sl[h]] for h in heads], [f_ref[:, sl[h]] for h in heads]
        v, gr = [i_ref[:, sl[h]] for h in heads], [g_ref[:, sl[h]] for h in heads]
        gates = [_hgrn_gates(qr[h], z[h], tab_ref[0:1, sl[h]], tab_ref[1:2, sl[h]]) for h in heads]
        lb, sz, sneg, f, lf, k, q = ([g[j] for g in gates] for j in range(7))
        cum = [_dot_sel(tri, lf[h]) for h in heads]
        qd, kds, s, cmid = _hgrn_scores(q, k, cum)
        p = [jnp.where(causal, s[h], 0.0) for h in heads]
        st, dst = [st_ref[h, 0] for h in heads], [dstate[h] for h in heads]
        o, dyh, sg = [o_ref[:, sl[h]] for h in heads], [dy_ref[:, sl[h]] for h in heads], [_sig(x) for x in gr]
        dgr = [dyh[h] * _rms(o[h], gain_) * sg[h] * (1.0 + gr[h] * (1.0 - sg[h])) for h in heads]
        norm_bwd = [_rms_bwd(o[h], gain_, dyh[h] * gr[h] * sg[h]) for h in heads]
        do = [x[0] for x in norm_bwd]
        do_b = [x.astype(BF16) for x in do]
        ecum, last = [jnp.exp(x) for x in cum], [x[HB - 1:HB, :] for x in cum]
        qc = [q[h] * ecum[h] for h in heads]
        edec = [jnp.exp(last[h] - cum[h]) for h in heads]
        kdec = [k[h] * edec[h] for h in heads]
        dp = [jnp.where(causal, _dot_nt(do_b[h], v[h]), 0.0) for h in heads]
        dv = [_dot(p[h].T, do_b[h]) + _dot_nt(kdec[h], dst[h]) for h in heads]
        dqc = [_dot(do_b[h], st[h]) for h in heads]
        dkdec = [_dot(v[h], dst[h]) for h in heads]
        new_dst = [dst[h] * jnp.exp(last[h]) + _dot(do[h].T, qc[h]) for h in heads]
        dp_b = [x.astype(BF16) for x in dp]
        dqd = [jnp.concatenate(parts, axis=0) for parts in
               zip(*[[_dot(dp_b[h][rows_of(i)], kds[h][i]) for h in heads] for i in range(nsub)])]
        gq = [dqd[h] * qd[h] for h in heads]
        dq = [dqd[h] * jnp.exp(jnp.minimum(cmid[h], EXP_CLAMP)) + dqc[h] * ecum[h] for h in heads]
        gs = [dkdec[h] * kdec[h] for h in heads]
        dk = [dkdec[h] * edec[h] for h in heads]
        dcum = [dqc[h] * qc[h] - gs[h] + gq[h]
                + jnp.where(row == HB - 1, jnp.sum(gs[h], axis=0, keepdims=True)
                            + jnp.exp(last[h]) * jnp.sum(st[h] * dst[h], axis=0, keepdims=True), 0.0) for h in heads]
        qd_b = [x.astype(BF16) for x in qd]
        for i in range(nsub):
            dkd = [_dot_tn(dp_b[h][rows_of(i)], qd_b[h][rows_of(i)]) for h in heads]
            for h in heads:
                mid = cum[h][SUB * i + SUB // 2 - 1:SUB * i + SUB // 2, :]
                dk[h] = dk[h] + dkd[h] * jnp.exp(jnp.minimum(mid - cum[h], EXP_CLAMP))
                gk = dkd[h] * kds[h][i]
                to_mid = jnp.sum(gk, axis=0, keepdims=True) - jnp.sum(gq[h][rows_of(i)], axis=0, keepdims=True)
                dcum[h] = dcum[h] - gk + jnp.where(row == SUB * i + SUB // 2 - 1, to_mid, 0.0)
        dlf = [_dot_sel(tri_t, dcum[h]) for h in heads]
        dgain = jnp.zeros((1, HD), F32)
        for h in heads:
            df = dlf[h] / f[h] - dk[h]
            dlb = jnp.sum(df * sneg[h], axis=0, keepdims=True) * lb[h] * (1.0 - lb[h])
            dtab_ref[0:1, sl[h]] += dlb
            dtab_ref[1:2, sl[h]] -= dlb
            sq = _sig(qr[h])
            dstate[h] = new_dst[h]
            dp_ref[:, sl[h]] = (dq[h] * sq * (1.0 + qr[h] * (1.0 - sq))).astype(BF16)
            dp_ref[:, HGW + HD * h:HGW + HD * (h + 1)] = (df * (1.0 - lb[h]) * sz[h] * sneg[h]).astype(BF16)
            dp_ref[:, 2 * HGW + HD * h:2 * HGW + HD * (h + 1)] = dv[h].astype(BF16)
            dp_ref[:, 3 * HGW + HD * h:3 * HGW + HD * (h + 1)] = dgr[h].astype(BF16)
            dgain = dgain + norm_bwd[h][1]
        dgain_ref[...] += dgain

    rev = lambda cb: pl.BlockSpec((HB, HGW), functools.partial(lambda n, cb: (nblk - 1 - n, cb), cb=cb))
    return pl.pallas_call(
        body, name="hgrn_bwd", grid=(nblk,),
        in_specs=[rev(0), rev(1), rev(2), rev(3), pl.BlockSpec((2, HGW), lambda n: (0, 0)),
                  pl.BlockSpec((1, HD), lambda n: (0, 0)), rev(0),
                  pl.BlockSpec((NH, 1, HD, HD), lambda n: (0, nblk - 1 - n, 0, 0)), rev(0)],
        out_specs=[pl.BlockSpec((HB, 4 * HGW), lambda n: (nblk - 1 - n, 0)),
                   pl.BlockSpec((2, HGW), lambda n: (0, 0)), pl.BlockSpec((1, HD), lambda n: (0, 0))],
        out_shape=[jax.ShapeDtypeStruct((t, 4 * HGW), BF16), jax.ShapeDtypeStruct((2, HGW), F32),
                   jax.ShapeDtypeStruct((1, HD), F32)],
        scratch_shapes=[pltpu.VMEM((NH, HD, HD), F32)],
        compiler_params=_params(("arbitrary",)),
    )(p_hg, p_hg, p_hg, p_hg, table, gain, o_pre, states, dy)


def _rope_mat():
    r = lax.broadcasted_iota(jnp.int32, (ROPE, ROPE), 0)
    c = lax.broadcasted_iota(jnp.int32, (ROPE, ROPE), 1)
    half = ROPE // 2
    return ((r == c - half).astype(F32) - (r == c + half).astype(F32)).astype(BF16)


def _mla_prep_fwd(p_mla, cs, sn, wq, wkv, gql, gkvl, gq, gk, t):
    tm = _pick(t, (512, 256, 128))

    def body(p_ref, cs_ref, sn_ref, wq_ref, wkv_ref, gql_ref, gkvl_ref, gq_ref, gk_ref,
             q_ref, k_ref, v_ref):
        rmat = _rope_mat()
        cqn = _rms(p_ref[:, 0:QL], gql_ref[...]).astype(BF16)
        ckvn = _rms(p_ref[:, QL:QL + KVL], gkvl_ref[...]).astype(BF16)
        kpe = p_ref[:, QL + KVL:QL + KVL + ROPE]
        c, s = cs_ref[...], sn_ref[...]
        rot = lambda x: x * c + _sel_dot(x, rmat) * s
        heads = range(NH)
        ssq = lambda x: jnp.sum(x * x, -1, keepdims=True)
        qa = [_dot(cqn, wq_ref[h, :, 0:HD]) for h in heads]
        qr = [_dot(cqn, wq_ref[h, :, HD:QK]) for h in heads]
        kn = [_dot(ckvn, wkv_ref[h, :, 0:HD]) for h in heads]
        vv = [_dot(ckvn, wkv_ref[h, :, HD:2 * HD]) for h in heads]
        kpe_ss = ssq(kpe)
        rq = [lax.rsqrt((ssq(qa[h]) + ssq(qr[h])) / QK + EPS) for h in heads]
        rk = [lax.rsqrt((ssq(kn[h]) + kpe_ss) / QK + EPS) for h in heads]
        q_rope = [rot(qr[h] * rq[h] * gq_ref[:, HD:QK]) for h in heads]
        k_rope = [rot(kpe * rk[h] * gk_ref[:, HD:QK]) for h in heads]
        for h in heads:
            q_ref[h, :, 0:HD] = (qa[h] * rq[h] * gq_ref[:, 0:HD] * (SCALE * LOG2E)).astype(BF16)
            q_ref[h, :, HD:QK] = (q_rope[h] * (SCALE * LOG2E)).astype(BF16)
            k_ref[h, :, 0:HD] = (kn[h] * rk[h] * gk_ref[:, 0:HD]).astype(BF16)
            k_ref[h, :, HD:QK] = k_rope[h].astype(BF16)
            v_ref[h] = vv[h].astype(BF16)

    whole = lambda a: pl.BlockSpec(a.shape, functools.partial(lambda i, nd: (0,) * nd, nd=a.ndim))
    return pl.pallas_call(
        body, name="mla_prep_fwd", grid=(t // tm,),
        in_specs=[pl.BlockSpec((tm, QL + KVL + ROPE), lambda i: (i, 0)), pl.BlockSpec((tm, ROPE), lambda i: (i, 0)),
                  pl.BlockSpec((tm, ROPE), lambda i: (i, 0))] + [whole(a) for a in (wq, wkv, gql, gkvl, gq, gk)],
        out_specs=[pl.BlockSpec((NH, tm, QK), lambda i: (0, i, 0)), pl.BlockSpec((NH, tm, QK), lambda i: (0, i, 0)),
                   pl.BlockSpec((NH, tm, HD), lambda i: (0, i, 0))],
        out_shape=[jax.ShapeDtypeStruct((NH, t, QK), BF16), jax.ShapeDtypeStruct((NH, t, QK), BF16),
                   jax.ShapeDtypeStruct((NH, t, HD), BF16)],
        compiler_params=_params(("parallel",)),
    )(p_mla, cs, sn, wq, wkv, gql, gkvl, gq, gk)


def _mla_prep_bwd(p_mla, cs, sn, wq, wkv, gql, gkvl, gq, gk, dq, dk, dv, t):
    tm = _pick(t, (512, 256, 128))

    def body(p_ref, cs_ref, sn_ref, wq_ref, wkv_ref, gql_ref, gkvl_ref, gq_ref, gk_ref,
             dq_ref, dk_ref, dv_ref,
             dp_ref, dwq_ref, dwkv_ref, dgql_ref, dgkvl_ref, dgq_ref, dgk_ref):
        accs = (dwq_ref, dwkv_ref, dgql_ref, dgkvl_ref, dgq_ref, dgk_ref)

        @pl.when(pl.program_id(0) == 0)
        def _():
            for r in accs:
                r[...] = jnp.zeros_like(r)

        rmat = _rope_mat()
        rmat_t = -rmat
        cq, ckv = p_ref[:, 0:QL], p_ref[:, QL:QL + KVL]
        kpe = p_ref[:, QL + KVL:QL + KVL + ROPE]
        cqn_f, ckvn_f = _rms(cq, gql_ref[...]), _rms(ckv, gkvl_ref[...])
        cqn, ckvn = cqn_f.astype(BF16), ckvn_f.astype(BF16)
        ckvn_t = ckvn_f.T.astype(BF16)
        c, s = cs_ref[...], sn_ref[...]
        unrot = lambda dy: dy * c + _sel_dot(dy * s, rmat_t)
        heads = range(NH)
        rsum = lambda x: jnp.sum(x, -1, keepdims=True)
        csum = lambda x: jnp.sum(x, axis=0, keepdims=True)
        qa = [_dot(cqn, wq_ref[h, :, 0:HD]) for h in heads]
        qr = [_dot(cqn, wq_ref[h, :, HD:QK]) for h in heads]
        kn = [_dot(ckvn, wkv_ref[h, :, 0:HD]) for h in heads]
        dyr = [unrot(dq_ref[h, :, HD:QK] * SCALE) for h in heads]
        dkr = [unrot(dk_ref[h, :, HD:QK]) for h in heads]
        rq = [lax.rsqrt((rsum(qa[h] * qa[h]) + rsum(qr[h] * qr[h])) / QK + EPS) for h in heads]
        xa, xr = [qa[h] * rq[h] for h in heads], [qr[h] * rq[h] for h in heads]
        dya = [dq_ref[h, :, 0:HD] * SCALE for h in heads]
        ga, gr_ = [dya[h] * gq_ref[:, 0:HD] for h in heads], [dyr[h] * gq_ref[:, HD:QK] for h in heads]
        mean = [(rsum(ga[h] * xa[h]) + rsum(gr_[h] * xr[h])) / QK for h in heads]
        dqa = [(rq[h] * (ga[h] - xa[h] * mean[h])).astype(BF16) for h in heads]
        dqr = [(rq[h] * (gr_[h] - xr[h] * mean[h])).astype(BF16) for h in heads]
        kpe_ss = rsum(kpe * kpe)
        rk = [lax.rsqrt((rsum(kn[h] * kn[h]) + kpe_ss) / QK + EPS) for h in heads]
        ya, yr = [kn[h] * rk[h] for h in heads], [kpe * rk[h] for h in heads]
        dka = [dk_ref[h, :, 0:HD] for h in heads]
        ha, hr = [dka[h] * gk_ref[:, 0:HD] for h in heads], [dkr[h] * gk_ref[:, HD:QK] for h in heads]
        mean_k = [(rsum(ha[h] * ya[h]) + rsum(hr[h] * yr[h])) / QK for h in heads]
        dkn = [(rk[h] * (ha[h] - ya[h] * mean_k[h])).astype(BF16) for h in heads]
        dvh = [dv_ref[h].astype(BF16) for h in heads]
        dw = [(_dot_tn(dqa[h], cqn), _dot_tn(dqr[h], cqn), _dot(ckvn_t, dkn[h]), _dot(ckvn_t, dvh[h])) for h in heads]
        back_q = [_dot_nt(dqa[h], wq_ref[h, :, 0:HD]) + _dot_nt(dqr[h], wq_ref[h, :, HD:QK]) for h in heads]
        back_kv = [_dot_nt(dkn[h], wkv_ref[h, :, 0:HD]) + _dot_nt(dvh[h], wkv_ref[h, :, HD:2 * HD]) for h in heads]
        dcqn, dckvn = sum(back_q[1:], back_q[0]), sum(back_kv[1:], back_kv[0])
        dkpe = sum([rk[h] * (hr[h] - yr[h] * mean_k[h]) for h in heads][1:], rk[0] * (hr[0] - yr[0] * mean_k[0]))
        dgq_a = sum([csum(dya[h] * xa[h]) for h in heads][1:], csum(dya[0] * xa[0]))
        dgq_r = sum([csum(dyr[h] * xr[h]) for h in heads][1:], csum(dyr[0] * xr[0]))
        dgk_a = sum([csum(dka[h] * ya[h]) for h in heads][1:], csum(dka[0] * ya[0]))
        dgk_r = sum([csum(dkr[h] * yr[h]) for h in heads][1:], csum(dkr[0] * yr[0]))
        for h in heads:
            dwq_ref[h, 0:HD, :] += dw[h][0]
            dwq_ref[h, HD:QK, :] += dw[h][1]
            dwkv_ref[h, :, 0:HD] += dw[h][2]
            dwkv_ref[h, :, HD:2 * HD] += dw[h][3]
        dcq, dg1 = _rms_bwd(cq, gql_ref[...], dcqn)
        dckv, dg2 = _rms_bwd(ckv, gkvl_ref[...], dckvn)
        dp_ref[:, 0:QL] = dcq.astype(BF16)
        dp_ref[:, QL:QL + KVL] = dckv.astype(BF16)
        dp_ref[:, QL + KVL:QL + KVL + ROPE] = dkpe.astype(BF16)
        dgql_ref[...] += dg1
        dgkvl_ref[...] += dg2
        dgq_ref[:, 0:HD] += dgq_a
        dgq_ref[:, HD:QK] += dgq_r
        dgk_ref[:, 0:HD] += dgk_a
        dgk_ref[:, HD:QK] += dgk_r

    whole = lambda a: pl.BlockSpec(a.shape, functools.partial(lambda i, nd: (0,) * nd, nd=a.ndim))
    acc_shapes = [(NH, QK, QL), wkv.shape, gql.shape, gkvl.shape, gq.shape, gk.shape]
    return pl.pallas_call(
        body, name="mla_prep_bwd", grid=(t // tm,),
        in_specs=[pl.BlockSpec((tm, QL + KVL + ROPE), lambda i: (i, 0)), pl.BlockSpec((tm, ROPE), lambda i: (i, 0)),
                  pl.BlockSpec((tm, ROPE), lambda i: (i, 0))]
        + [whole(a) for a in (wq, wkv, gql, gkvl, gq, gk)]
        + [pl.BlockSpec((NH, tm, QK), lambda i: (0, i, 0)), pl.BlockSpec((NH, tm, QK), lambda i: (0, i, 0)),
           pl.BlockSpec((NH, tm, HD), lambda i: (0, i, 0))],
        out_specs=[pl.BlockSpec((tm, QL + KVL + ROPE), lambda i: (i, 0))]
        + [pl.BlockSpec(s, functools.partial(lambda i, nd: (0,) * nd, nd=len(s))) for s in acc_shapes],
        out_shape=[jax.ShapeDtypeStruct((t, QL + KVL + ROPE), BF16)]
        + [jax.ShapeDtypeStruct(s, F32) for s in acc_shapes],
        compiler_params=_params(("arbitrary",)),
    )(p_mla, cs, sn, wq, wkv, gql, gkvl, gq, gk, dq, dk, dv)


def _chunk_mask(nq, nk, key0, keys_on_rows):
    shape = (nk, nq) if keys_on_rows else (nq, nk)
    qi = lax.broadcasted_iota(jnp.int32, shape, 1 if keys_on_rows else 0) // CHUNK
    ki = lax.broadcasted_iota(jnp.int32, shape, 0 if keys_on_rows else 1) // CHUNK + key0 // CHUNK
    return ki <= qi


def _flash_fwd(q, k, v, t):
    tq = _pick(t, ATT_Q)
    tk = tq // ATT_KEY_TILES

    def body(q_ref, k_ref, v_ref, o_ref, lse_ref):
        i = pl.program_id(1)
        qt = q_ref[0]

        def step(j, carry, key0):
            m, l, acc = carry
            cols = pl.ds(pl.multiple_of(j * tk, tk), tk)
            s = _dot_nt(qt, k_ref[0, cols, :])
            if key0 is not None:
                s = jnp.where(_chunk_mask(tq, tk, key0, False), s, -jnp.inf)
            m_new = jnp.maximum(m, jnp.max(s, axis=-1, keepdims=True))
            p = jnp.exp2(s - m_new)
            alpha = jnp.exp2(m - m_new)
            return m_new, alpha * l + jnp.sum(p, axis=-1, keepdims=True), alpha * acc + _dot(p, v_ref[0, cols, :])

        init = (jnp.full((tq, 1), -jnp.inf, F32), jnp.zeros((tq, 1), F32), jnp.zeros((tq, HD), F32))
        carry = lax.fori_loop(0, ATT_KEY_TILES * i, lambda j, cr: step(j, cr, None), init)
        for h in range(ATT_KEY_TILES):
            carry = step(ATT_KEY_TILES * i + h, carry, h * tk)
        m, l, acc = carry
        o_ref[...] = (acc / l).astype(BF16)
        x1, x2, x3 = _split3(jnp.broadcast_to((m + jnp.log2(l)) * (1.0 / HD), (tq, HD)))
        ones = jnp.ones((SUBLANE, HD), BF16)
        rowsum = lambda p: lax.dot_general(ones, p, (((1,), (1,)), ((), ())), preferred_element_type=F32)
        lse_ref[0] = (rowsum(x1) + rowsum(x2) + rowsum(x3))[0:1, :]

    return pl.pallas_call(
        body, name="flash_fwd", grid=(NH, t // tq),
        in_specs=[pl.BlockSpec((1, tq, QK), lambda h, i: (h, i, 0)), pl.BlockSpec((1, t, QK), lambda h, i: (h, 0, 0)),
                  pl.BlockSpec((1, t, HD), lambda h, i: (h, 0, 0))],
        out_specs=[pl.BlockSpec((tq, HD), lambda h, i: (i, h)), pl.BlockSpec((1, 1, tq), lambda h, i: (h, 0, i))],
        out_shape=[jax.ShapeDtypeStruct((t, NH * HD), BF16), jax.ShapeDtypeStruct((NH, 1, t), F32)],
        compiler_params=_params(("parallel", "parallel")),
    )(q, k, v)


def _attn_out_bwd(dy, w, o, t):
    tm = _pick(t, ATT_TILES)

    def body(dy_ref, w_ref, o_ref, do_ref, delta_ref):
        do = _dot_nt(dy_ref[...], w_ref[...]).astype(BF16)
        do_ref[...] = do
        ones = jnp.ones((SUBLANE, HD), BF16)
        rowsum = lambda p: lax.dot_general(ones, p, (((1,), (1,)), ((), ())), preferred_element_type=F32)
        sl = [slice(HD * h, HD * (h + 1)) for h in range(NH)]
        parts = [_split3(do[:, sl[h]].astype(F32) * o_ref[:, sl[h]].astype(F32)) for h in range(NH)]
        sums = [[rowsum(x) for x in parts[h]] for h in range(NH)]
        for h in range(NH):
            delta_ref[h] = (sums[h][0] + sums[h][1] + sums[h][2])[0:1, :]

    return pl.pallas_call(
        body, name="attn_out_bwd", grid=(t // tm,),
        in_specs=[_BS((tm, dy.shape[1]), lambda i: (i, 0)), _BS(w.shape, lambda i: (0, 0)),
                  _BS((tm, NH * HD), lambda i: (i, 0))],
        out_specs=[_BS((tm, NH * HD), lambda i: (i, 0)), _BS((NH, 1, tm), lambda i: (0, 0, i))],
        out_shape=[jax.ShapeDtypeStruct((t, NH * HD), BF16), jax.ShapeDtypeStruct((NH, 1, t), F32)],
        compiler_params=_params(("parallel",)),
    )(dy, w, o)


def _flash_bwd(q, k, v, do, lse_row, delta_row, t):
    tq = _pick(t, ATT_Q)
    tk = tq // ATT_KEY_TILES

    def body(q_ref, k_ref, v_ref, do_ref, lse_ref, delta_ref, dq_ref, dk_ref, dv_ref):
        j = pl.program_id(1)

        @pl.when(j == 0)
        def _():
            dq_ref[...] = jnp.zeros_like(dq_ref)

        kt, vt = k_ref[0], v_ref[0]

        def step(i, carry, key0):
            dk, dv = carry
            rows = pl.ds(pl.multiple_of(i * tq, tq), tq)
            qt, dot_ = q_ref[0, rows, :], do_ref[rows, :]
            p = jnp.exp2(_dot_nt(kt, qt) - lse_ref[0, :, rows])
            if key0 is not None:
                p = jnp.where(_chunk_mask(tq, tk, key0, True), p, 0.0)
            ds = (p * (_dot_nt(vt, dot_) - delta_ref[0, :, rows])).astype(BF16)
            dq_ref[0, rows, :] += _dot_tn(ds, kt)
            return dk + _dot(ds, qt), dv + _dot(p, dot_)

        own = j // ATT_KEY_TILES
        carry = step(own, (jnp.zeros((tk, QK), F32), jnp.zeros((tk, HD), F32)), (j % ATT_KEY_TILES) * tk)
        dk, dv = lax.fori_loop(own + 1, t // tq, lambda i, cr: step(i, cr, None), carry)
        dk_ref[0] = dk * (1.0 / LOG2E)
        dv_ref[0] = dv

    return pl.pallas_call(
        body, name="flash_bwd", grid=(NH, t // tk),
        in_specs=[pl.BlockSpec((1, t, QK), lambda h, j: (h, 0, 0)), pl.BlockSpec((1, tk, QK), lambda h, j: (h, j, 0)),
                  pl.BlockSpec((1, tk, HD), lambda h, j: (h, j, 0)), pl.BlockSpec((t, HD), lambda h, j: (0, h)),
                  pl.BlockSpec((1, 1, t), lambda h, j: (h, 0, 0)), pl.BlockSpec((1, 1, t), lambda h, j: (h, 0, 0))],
        out_specs=[pl.BlockSpec((1, t, QK), lambda h, j: (h, 0, 0)), pl.BlockSpec((1, tk, QK), lambda h, j: (h, j, 0)),
                   pl.BlockSpec((1, tk, HD), lambda h, j: (h, j, 0))],
        out_shape=[jax.ShapeDtypeStruct((NH, t, QK), F32), jax.ShapeDtypeStruct((NH, t, QK), F32),
                   jax.ShapeDtypeStruct((NH, t, HD), F32)],
        compiler_params=_params(("parallel", "arbitrary")),
    )(q, k, v, do, lse_row, delta_row)


def _ffn_in(xn, w_in, name):
    t, k = xn.shape
    s, _, n = w_in.shape
    tm = _pick(t, ROW_TILES)

    def body(x_ref, w_ref, gu_ref, a_ref):
        x = x_ref[...]
        g, u = _dot(x, w_ref[0]), _dot(x, w_ref[1])
        gu_ref[0] = g.astype(BF16)
        gu_ref[1] = u.astype(BF16)
        a_ref[...] = (g * _sig(g) * u).astype(BF16)

    return pl.pallas_call(
        body, name=name, grid=(t // tm, s // 2),
        in_specs=[_BS((tm, k), lambda i, j: (i, 0)), _BS((2, None, k, n), lambda i, j: (0, j, 0, 0))],
        out_specs=[_BS((2, None, tm, n), lambda i, j: (0, j, i, 0)), _BS((None, tm, n), lambda i, j: (j, i, 0))],
        out_shape=[jax.ShapeDtypeStruct((2, s // 2, t, n), BF16), jax.ShapeDtypeStruct((s // 2, t, n), BF16)],
        compiler_params=_params(("parallel", "parallel")),
    )(xn, w_in.reshape(2, s // 2, k, n))


def _ffn_dgu(dfo, w_out, gu, name):
    t, k = dfo.shape
    s, n, _ = w_out.shape
    tm = _pick(t, ROW_TILES)

    def body(d_ref, w_ref, gu_ref, o_ref):
        da = _dot_nt(d_ref[...], w_ref[...])
        g, u = gu_ref[0].astype(F32), gu_ref[1].astype(F32)
        sg = _sig(g)
        o_ref[0] = (da * u * sg * (1.0 + g * (1.0 - sg))).astype(BF16)
        o_ref[1] = (da * g * sg).astype(BF16)

    pair = _BS((2, None, tm, n), lambda i, j: (0, j, i, 0))
    return pl.pallas_call(
        body, name=name, grid=(t // tm, s),
        in_specs=[_BS((tm, k), lambda i, j: (i, 0)), _BS((None, n, k), lambda i, j: (j, 0, 0)), pair],
        out_specs=pair, out_shape=jax.ShapeDtypeStruct((2, s, t, n), BF16),
        compiler_params=_params(("parallel", "parallel")),
    )(dfo, w_out, gu).reshape(2 * s, t, n)


def _ffn_fwd(xn, w_in, w_out, tag, epilogue):
    gu, a = _ffn_in(xn, w_in, tag + "_in")
    return gu, a, _mm_stack_red(a, w_out, tag + "_out", "nn", epilogue)


def _ffn_bwd(dfo, xn, gu, a, w_in, w_out, tag, epilogue):
    dw_out = _mm_stack_tn(a, dfo, tag + "_dwout")
    dgu = _ffn_dgu(dfo, w_out, gu, tag + "_dgu")
    dw_in = _mm_stack_tn(dgu, xn, tag + "_dwin")
    if callable(epilogue):
        epilogue = epilogue(dw_in, dw_out)
    return _mm_stack_red(dgu, w_in, tag + "_dxn", "nt", epilogue), dw_in, dw_out


def _local_step(x, target, cs, sn, w, late_weights, s, early_grads, last_grads):
    t = x.shape[0]
    tm = _pick(t, (256, 128))
    g = {}
    ffn_out = lambda n: w[n].reshape(4, FF // 4, D)

    def norm_fn(x_ref, g_ref, o_ref):
        o_ref[...] = _rms(x_ref[...], g_ref[...]).astype(BF16)

    xn1 = _rows(norm_fn, "norm1", t, tm, [(x, D, 0)], [s["ffn1_norm"]], [(D, BF16)])[0]
    def res_norm_fn(scale):
        def fn(h_ref, f_ref, g_ref, h_out, n_out):
            h = h_ref[...] + scale * f_ref[...]
            h_out[...] = h
            n_out[...] = _rms(h, g_ref[...]).astype(BF16)
        return fn

    gu1, a1, (h1, u) = _ffn_fwd(xn1, w["ffn1_w_in"], ffn_out("ffn1_w_out"), "ffn1",
                                (res_norm_fn(0.5), x, [], [s["mix_norm"]], [(D, F32), (D, BF16)], []))
    w = {**w, **late_weights(h1)}
    rows_of = lambda n: w[n].reshape(-1, w[n].shape[2])
    w_hgb, w_mlab, w_o = rows_of("w_hg_branch"), rows_of("w_mla_branch"), rows_of("w_out")
    w_in_nat = w["w_in"].transpose(1, 0, 2).reshape(D, -1)
    w_mrg = w["w_merge"]
    mw = w_mrg.shape[2]
    w_in_hg, w_in_mla = w_in_nat[:, :4 * HGW], w_in_nat[:, 4 * HGW:]
    p_hg = _mm(u, w_in_hg, "nn", "proj_hg")
    p_mla = _mm(u, w_in_mla, "nn", "proj_mla")
    gpre = _cols_fwd(u, w_mrg, "proj_gate")
    o_pre, hgy, states = _hgrn_fwd(p_hg, s["hg_lb_table"], s["hg_out_norm"], t)
    prep_args = (p_mla, cs, sn, w["w_q_up"], w["w_kv_up"], s["mla_q_lora_norm"], s["mla_kv_lora_norm"],
                 s["q_head_norm"], s["k_head_norm"])
    q, k, v = _mla_prep_fwd(*prep_args, t)
    att, lse = _flash_fwd(q, k, v, t)
    y_hg = _mm(hgy, w_hgb, "nn", "branch_hg")
    def mix_fn(gp_ref, ym_ref, yh_ref, b_ref, o_ref, ym_out):
        gh = _sig(gp_ref[:, 0:D] + b_ref[:, 0:D])
        gm = _sig(gp_ref[:, D:2 * D] + b_ref[:, D:2 * D])
        ym = ym_ref[...]
        ym_out[...] = ym
        o_ref[...] = (gh * yh_ref[...] + gm * ym).astype(BF16)

    mixed, y_mla = _mm_stack_red(att[None], w_mlab[None], "branch_mla", "nn",
                                 (mix_fn, gpre, [y_hg], [s["b_merge"]], [(D, BF16), (D, F32)], [], (512, 256, 128)))
    h2, xn2 = _mm_stack_red(mixed[None], w_o[None], "mix_out", "nn",
                            (res_norm_fn(1.0), h1, [], [s["ffn2_norm"]], [(D, F32), (D, BF16)], []))

    def loss_fn(h_ref, f_ref, tg_ref, g_ref, dh_out, dhb_out):
        h = h_ref[...] + 0.5 * f_ref[...]
        e = _rms(h, g_ref[...]) - tg_ref[...]
        dh, dgain = _rms_bwd(h, g_ref[...], e / D)
        dh_out[...] = dh
        dhb_out[...] = (0.5 * dh).astype(BF16)
        return dgain, jnp.full((1, LANE), 0.5 / D * jnp.sum(e * e), F32)

    gu2, a2, (dh3, dfo2, g["final_norm"], loss) = _ffn_fwd(
        xn2, w["ffn2_w_in"], ffn_out("ffn2_w_out"), "ffn2",
        (loss_fn, h2, [target], [s["final_norm"]], [(D, F32), (D, BF16)], [(1, D), (1, LANE)]))

    def norm_bwd_fn(scale):
        def fn(h_ref, dxn_ref, dh_ref, g_ref, dh_out, dhb_out):
            dx, dgain = _rms_bwd(h_ref[...], g_ref[...], dxn_ref[...])
            dh = dh_ref[...] + dx
            dh_out[...] = dh
            dhb_out[...] = (scale * dh).astype(BF16)
            return (dgain,)
        return fn

    as_rows = lambda a: a.reshape((N_DEV, -1) + a.shape[-1:])
    (dh2, dh2b, g["ffn2_norm"]), g["ffn2_w_in"], dwo = _ffn_bwd(
        dfo2, xn2, gu2, a2, w["ffn2_w_in"], ffn_out("ffn2_w_out"), "ffn2",
        (norm_bwd_fn(1.0), h2, [dh3], [s["ffn2_norm"]], [(D, F32), (D, BF16)], [(1, D)]))
    g["ffn2_w_out"] = as_rows(dwo)
    g["w_out"] = as_rows(_mm(mixed, dh2b, "tn", "mix_out_dw"))

    def mix_bwd_fn(gp_ref, dm_ref, yh_ref, ym_ref, b_ref, dyh_out, dym_out, dg_out):
        gh = _sig(gp_ref[:, 0:D] + b_ref[:, 0:D])
        gm = _sig(gp_ref[:, D:2 * D] + b_ref[:, D:2 * D])
        dm = dm_ref[...]
        dyh_out[...] = (dm * gh).astype(BF16)
        dym_out[...] = (dm * gm).astype(BF16)
        dgh = dm * yh_ref[...] * gh * (1.0 - gh)
        dgm = dm * ym_ref[...] * gm * (1.0 - gm)
        dg_out[:, 0:D] = dgh.astype(BF16)
        dg_out[:, D:2 * D] = dgm.astype(BF16)
        return (jnp.concatenate([jnp.sum(dgh, axis=0, keepdims=True), jnp.sum(dgm, axis=0, keepdims=True)], axis=1),)

    dyh, dym, dgpre, g["b_merge"] = _mm_stack_red(
        dh2b[None], w_o[None], "mix_out_dx", "nt",
        (mix_bwd_fn, gpre, [y_hg, y_mla], [s["b_merge"]], [(D, BF16), (D, BF16), (2 * D, BF16)], [(1, 2 * D)],
         (512, 256, 128)))
    g["w_hg_branch"] = as_rows(_mm(hgy, dyh, "tn", "branch_hg_dw"))
    g["w_mla_branch"] = as_rows(_mm(att, dym, "tn", "branch_mla_dw"))
    g["w_merge"] = _cols_dw(u, dgpre, mw, "proj_gate_dw")
    dhgy = _mm(dyh, w_hgb, "nt", "branch_hg_dx")
    datt, delta = _attn_out_bwd(dym, w_mlab, att, t)
    du_gate = _cols_dx(dgpre, w_mrg, "proj_gate_dx")

    dq, dk, dv = _flash_bwd(q, k, v, datt, lse, delta, t)
    (dp_mla, dwq, dwkv, g["mla_q_lora_norm"], g["mla_kv_lora_norm"], g["q_head_norm"],
     g["k_head_norm"]) = _mla_prep_bwd(*prep_args, dq, dk, dv, t)
    g["w_q_up"], g["w_kv_up"] = dwq.astype(GRAD_DT), dwkv.astype(GRAD_DT)
    dp_hg, g["hg_lb_table"], g["hg_out_norm"] = _hgrn_bwd(p_hg, s["hg_lb_table"], s["hg_out_norm"], o_pre, states,
                                                          dhgy, t)
    dw_in_nat = jnp.concatenate([_mm(u, dp_hg, "tn", "proj_hg_dw"), _mm(u, dp_mla, "tn", "proj_mla_dw")], axis=1)
    g["w_in"] = dw_in_nat.T.reshape(N_DEV, -1, D)
    du_mla = _mm(dp_mla, w_in_mla, "nt", "proj_mla_dx")

    def mixnorm_bwd_fn(h_ref, a_ref, b_ref, c_ref, dh_ref, g_ref, dh_out, dhb_out):
        dx, dgain = _rms_bwd(h_ref[...], g_ref[...], a_ref[...] + b_ref[...] + c_ref[...])
        dh = dh_ref[...] + dx
        dh_out[...] = dh
        dhb_out[...] = (0.5 * dh).astype(BF16)
        return (dgain,)

    mix_gain = s["mix_norm"] + early_grads(g)[0:1, 0:1]
    dh1, dfo1, g["mix_norm"] = _mm_stack_red(
        dp_hg[None], w_in_hg[None], "proj_hg_dx", "nt",
        (mixnorm_bwd_fn, h1, [du_mla, du_gate, dh2], [mix_gain], [(D, F32), (D, BF16)], [(1, D)], (512, 256, 128)))
    def last_stage(dw_in, dw_out):
        g["ffn1_w_in"], g["ffn1_w_out"] = dw_in, as_rows(dw_out)
        gain = s["ffn1_norm"] + last_grads(g)[0:1, 0:1]
        return norm_bwd_fn(1.0), x, [dh1], [gain], [(D, F32), (D, BF16)], [(1, D)]

    (grad_x, _, g["ffn1_norm"]), _, _ = _ffn_bwd(dfo1, xn1, gu1, a1, w["ffn1_w_in"], ffn_out("ffn1_w_out"), "ffn1",
                                                 last_stage)
    return loss, grad_x, g


def _coords():
    return lax.axis_index("x"), lax.axis_index("y"), lax.axis_index("c")


def _hbm_call(body, name, ins, out_shapes, scratch):
    any_spec = pl.BlockSpec(memory_space=pl.ANY)
    return pl.pallas_call(
        body, name=name, out_shape=[jax.ShapeDtypeStruct(s, dt) for s, dt in out_shapes],
        in_specs=[any_spec] * len(ins), out_specs=[any_spec] * len(out_shapes), scratch_shapes=scratch,
    )(*ins)


def _my_slot():
    return 4 * lax.axis_index("x") + 2 * lax.axis_index("y") + lax.axis_index("c")


def _put_own(buf, own, index):
    return lax.dynamic_update_index_in_dim(buf, own, index, 0)


def _all_gather(blocks, name):
    nb = len(blocks)

    def body(*refs):
        x_refs, out_refs = refs[:nb], refs[nb:2 * nb]
        send_sems, recv_sems = refs[2 * nb:]
        x, y, c = _coords()
        me, sibling = (x, y, c), (x, y, 1 - c)
        chips = [(1 - x, y), (x, 1 - y), (1 - x, 1 - y)]

        def slot(b, px, py, pc):
            return out_refs[b].at[4 * px + 2 * py + pc]

        def copy(b, kk, block_of, to, src=None):
            return pltpu.make_async_remote_copy(
                src_ref=slot(b, *block_of) if src is None else src, dst_ref=slot(b, *block_of),
                send_sem=send_sems.at[b, kk], recv_sem=recv_sems.at[b, kk], device_id=to, device_id_type=MESH)

        first = [copy(b, 0, me, sibling, src=x_refs[b]) for b in range(nb)]
        first += [copy(b, 1 + j, me, (*chip, c), src=x_refs[b]) for j, chip in enumerate(chips) for b in range(nb)]
        for cp in first:
            cp.start()
        passed = []
        for j, chip in enumerate(chips):
            for b in range(nb):
                copy(b, 1 + j, (*chip, c), me).wait_recv()
                passed.append(copy(b, 4 + j, (*chip, c), sibling))
                passed[-1].start()
        for b in range(nb):
            copy(b, 0, sibling, me).wait_recv()
        for j, chip in enumerate(chips):
            for b in range(nb):
                copy(b, 4 + j, (*chip, 1 - c), me).wait_recv()
        for cp in first + passed:
            cp.wait_send()

    outs = _hbm_call(body, name, blocks, [((N_DEV,) + b.shape, b.dtype) for b in blocks],
                     [pltpu.SemaphoreType.DMA((nb, 7)), pltpu.SemaphoreType.DMA((nb, 7))])
    return [_put_own(o, b[None], _my_slot()) for o, b in zip(outs, blocks)]


def _gather_peers():
    x, y, c = _coords()
    return (x, y, c), [(x, y, 1 - c), (1 - x, y, c), (x, 1 - y, c), (1 - x, 1 - y, c)]


def _gather_start(blocks, after, name):
    nb = len(blocks)
    hbm, sem = pl.BlockSpec(memory_space=pltpu.HBM), pl.BlockSpec(memory_space=pltpu.SEMAPHORE)

    def body(*refs):
        x_refs, out_refs = refs[:nb], refs[nb:2 * nb]
        send_sems, recv_sems, token = refs[2 * nb + 1], refs[2 * nb + 2], refs[-1]
        (x, y, c), peers = _gather_peers()
        for kk, peer in enumerate(peers):
            for b in range(nb):
                pltpu.make_async_remote_copy(
                    src_ref=x_refs[b], dst_ref=out_refs[b].at[4 * x + 2 * y + c], send_sem=send_sems.at[4 * b + kk],
                    recv_sem=recv_sems.at[4 * b + kk], device_id=peer, device_id_type=MESH).start()
        token[...] = jnp.zeros_like(token)

    gathers = [pltpu.with_memory_space_constraint(lax.empty((N_DEV,) + b.shape, b.dtype), pltpu.HBM) for b in blocks]
    outs = pl.pallas_call(
        body, name=name,
        out_shape=(pltpu.SemaphoreType.DMA((4 * nb,)), pltpu.SemaphoreType.DMA((4 * nb,)),
                   *[pltpu.HBM(b.shape, b.dtype) for b in blocks], *[pltpu.HBM(b.shape, b.dtype) for b in gathers],
                   jax.ShapeDtypeStruct((SUBLANE, LANE), F32)),
        in_specs=[hbm] * (2 * nb) + [pl.BlockSpec(memory_space=pl.ANY)],
        out_specs=(sem, sem, *[hbm] * (2 * nb), pl.BlockSpec(memory_space=pltpu.VMEM)),
        input_output_aliases={i: 2 + i for i in range(2 * nb)},
        compiler_params=pltpu.CompilerParams(has_side_effects=pltpu.SideEffectType.DATAFLOW_SIDE_EFFECTING),
    )(*[pltpu.with_memory_space_constraint(b, pltpu.HBM) for b in blocks], *gathers, after)
    return outs[0], outs[1], list(outs[2:2 + nb]), list(outs[2 + nb:2 + 2 * nb]), outs[-1]


def _gather_wait(send_sems, recv_sems, thru, gathers, after, name):
    nb = len(thru)
    hbm, sem = pl.BlockSpec(memory_space=pltpu.HBM), pl.BlockSpec(memory_space=pltpu.SEMAPHORE)

    def body(*refs):
        x_refs, out_refs = refs[:nb], refs[nb:2 * nb]
        send_sems_, recv_sems_ = refs[2 * nb], refs[2 * nb + 1]
        _, peers = _gather_peers()
        for kk, (px, py, pc) in enumerate(peers):
            for b in range(nb):
                cp = pltpu.make_async_remote_copy(
                    src_ref=x_refs[b], dst_ref=out_refs[b].at[4 * px + 2 * py + pc], send_sem=send_sems_.at[4 * b + kk],
                    recv_sem=recv_sems_.at[4 * b + kk], device_id=(px, py, pc), device_id_type=MESH)
                cp.wait_send()
                cp.wait_recv()

    outs = pl.pallas_call(
        body, name=name,
        out_shape=(*[pltpu.HBM(b.shape, b.dtype) for b in thru], *[pltpu.HBM(b.shape, b.dtype) for b in gathers]),
        in_specs=[hbm] * (2 * nb) + [sem, sem, pl.BlockSpec(memory_space=pl.ANY)], out_specs=[hbm] * (2 * nb),
        input_output_aliases={i: i for i in range(2 * nb)},
        compiler_params=pltpu.CompilerParams(has_side_effects=pltpu.SideEffectType.DATAFLOW_SIDE_EFFECTING),
    )(*thru, *gathers, send_sems, recv_sems, after)
    return list(outs[:nb]), list(outs[nb:])


def _gather_finish(blocks, gathers, name):
    nb = len(blocks)

    def body(*refs):
        x_refs, in_refs, out_refs = refs[:nb], refs[nb:2 * nb], refs[2 * nb:3 * nb]
        send_sems, recv_sems = refs[3 * nb:]
        (x, y, c), peers = _gather_peers()
        copies = []
        for j, (px, py, _) in enumerate(peers[1:]):
            for b in range(nb):
                copies.append(pltpu.make_async_remote_copy(
                    src_ref=in_refs[b].at[4 * px + 2 * py + c], dst_ref=out_refs[b].at[4 * px + 2 * py + c],
                    send_sem=send_sems.at[b, j], recv_sem=recv_sems.at[b, j], device_id=(x, y, 1 - c),
                    device_id_type=MESH))
                copies[-1].start()
        for j, (px, py, _) in enumerate(peers[1:]):
            for b in range(nb):
                pltpu.make_async_remote_copy(
                    src_ref=in_refs[b].at[4 * px + 2 * py + c], dst_ref=out_refs[b].at[4 * px + 2 * py + 1 - c],
                    send_sem=send_sems.at[b, j], recv_sem=recv_sems.at[b, j], device_id=(x, y, 1 - c),
                    device_id_type=MESH).wait_recv()
        for cp in copies:
            cp.wait_send()

    any_spec = pl.BlockSpec(memory_space=pl.ANY)
    outs = pl.pallas_call(
        body, name=name, out_shape=[jax.ShapeDtypeStruct(b.shape, b.dtype) for b in gathers],
        in_specs=[any_spec] * (2 * nb), out_specs=[any_spec] * nb,
        input_output_aliases={nb + i: i for i in range(nb)},
        scratch_shapes=[pltpu.SemaphoreType.DMA((nb, 3)), pltpu.SemaphoreType.DMA((nb, 3))],
    )(*blocks, *gathers)
    return [_put_own(o, b[None], _my_slot()) for o, b in zip(outs, blocks)]


def _sibling_swap(bufs, name):
    nb = len(bufs)

    def body(*refs):
        x_refs, out_refs = refs[:nb], refs[nb:2 * nb]
        send_sems, recv_sems = refs[2 * nb:]
        x, y, c = _coords()
        copies = [pltpu.make_async_remote_copy(
            src_ref=x_refs[b].at[2 * q + 1 - c], dst_ref=out_refs[b].at[q], send_sem=send_sems.at[b, q],
            recv_sem=recv_sems.at[b, q], device_id=(x, y, 1 - c), device_id_type=MESH)
            for b in range(nb) for q in range(4)]
        for cp in copies:
            cp.start()
        for cp in copies:
            cp.wait()

    return _hbm_call(body, name, bufs, [((4,) + b.shape[1:], b.dtype) for b in bufs],
                     [pltpu.SemaphoreType.DMA((nb, 4)), pltpu.SemaphoreType.DMA((nb, 4))])


def _chip_exchange_start(bufs, name):
    nb = len(bufs)
    hbm, sem = pl.BlockSpec(memory_space=pltpu.HBM), pl.BlockSpec(memory_space=pltpu.SEMAPHORE)

    def body(*refs):
        x_refs, land_refs = refs[:nb], refs[nb:2 * nb]
        send_sems, recv_sems, token = refs[2 * nb], refs[2 * nb + 1], refs[-1]
        x, y, c = _coords()
        for j, (px, py) in enumerate([(1 - x, y), (x, 1 - y), (1 - x, 1 - y)]):
            for b in range(nb):
                pltpu.make_async_remote_copy(
                    src_ref=x_refs[b].at[2 * px + py], dst_ref=land_refs[b].at[2 * x + y], send_sem=send_sems.at[3 * b + j],
                    recv_sem=recv_sems.at[3 * b + j], device_id=(px, py, c), device_id_type=MESH).start()
        token[...] = jnp.zeros_like(token)

    lands = [pltpu.with_memory_space_constraint(lax.empty(b.shape, b.dtype), pltpu.HBM) for b in bufs]
    outs = pl.pallas_call(
        body, name=name,
        out_shape=(pltpu.SemaphoreType.DMA((3 * nb,)), pltpu.SemaphoreType.DMA((3 * nb,)),
                   *[pltpu.HBM(b.shape, b.dtype) for b in bufs], *[pltpu.HBM(b.shape, b.dtype) for b in bufs],
                   jax.ShapeDtypeStruct((SUBLANE, LANE), F32)),
        in_specs=[hbm] * (2 * nb), out_specs=(sem, sem, *[hbm] * (2 * nb), pl.BlockSpec(memory_space=pltpu.VMEM)),
        input_output_aliases={i: 2 + i for i in range(2 * nb)},
        compiler_params=pltpu.CompilerParams(has_side_effects=pltpu.SideEffectType.DATAFLOW_SIDE_EFFECTING),
    )(*[pltpu.with_memory_space_constraint(b, pltpu.HBM) for b in bufs], *lands)
    return outs[0], outs[1], list(outs[2:2 + nb]), list(outs[2 + nb:2 + 2 * nb]), outs[-1]


def _chip_exchange_wait(send_sems, recv_sems, thru, lands, after, name):
    nb = len(thru)
    hbm, sem = pl.BlockSpec(memory_space=pltpu.HBM), pl.BlockSpec(memory_space=pltpu.SEMAPHORE)

    def body(*refs):
        x_refs, land_refs = refs[:nb], refs[nb:2 * nb]
        send_sems_, recv_sems_ = refs[2 * nb], refs[2 * nb + 1]
        x, y, c = _coords()
        for j, (px, py) in enumerate([(1 - x, y), (x, 1 - y), (1 - x, 1 - y)]):
            for b in range(nb):
                cp = pltpu.make_async_remote_copy(
                    src_ref=x_refs[b].at[2 * px + py], dst_ref=land_refs[b].at[2 * px + py],
                    send_sem=send_sems_.at[3 * b + j], recv_sem=recv_sems_.at[3 * b + j], device_id=(px, py, c),
                    device_id_type=MESH)
                cp.wait_send()
                cp.wait_recv()

    outs = pl.pallas_call(
        body, name=name,
        out_shape=(*[pltpu.HBM(b.shape, b.dtype) for b in thru], *[pltpu.HBM(b.shape, b.dtype) for b in lands]),
        in_specs=[hbm] * (2 * nb) + [sem, sem, pl.BlockSpec(memory_space=pl.ANY)], out_specs=[hbm] * (2 * nb),
        input_output_aliases={i: i for i in range(2 * nb)},
        compiler_params=pltpu.CompilerParams(has_side_effects=pltpu.SideEffectType.DATAFLOW_SIDE_EFFECTING),
    )(*thru, *lands, send_sems, recv_sems, after)
    return list(outs[:nb]), list(outs[nb:])


def _chip_sum(g, r1, c, name):
    _, r, cw = g.shape
    tr = _pick(r, (256, 176, 128))

    def body(c_ref, g_ref, r_ref, o_ref):
        o_ref[...] = (g_ref[...].astype(F32) + r_ref[...].astype(F32)).astype(GRAD_DT)

    grid_spec = pltpu.PrefetchScalarGridSpec(
        num_scalar_prefetch=1, grid=(4, r // tr),
        in_specs=[_BS((None, None, tr, cw), lambda q, i, c_ref: (q, c_ref[0], i, 0)),
                  _BS((None, tr, cw), lambda q, i, c_ref: (q, i, 0))],
        out_specs=_BS((None, tr, cw), lambda q, i, c_ref: (q, i, 0)))
    return pl.pallas_call(
        body, name=name, grid_spec=grid_spec, out_shape=jax.ShapeDtypeStruct((4, r, cw), GRAD_DT),
        compiler_params=_params(("parallel", "parallel")),
    )(c.reshape(1).astype(jnp.int32), g.reshape(4, 2, r, cw), r1)


def _adamw_math(w, g, m, v):
    m = B1 * m + (1.0 - B1) * g
    v = B2 * v + (1.0 - B2) * (g * g)
    m_hat = m / (1.0 - B1 ** STEP)
    v_hat = v / (1.0 - B2 ** STEP)
    return -LR * (m_hat / (jnp.sqrt(v_hat) + AEPS) + WD * w), m, v


def _sum_adamw(parts, w, m, v, name):
    r, c = w.shape
    tr = _pick(r, (256, 176, 128))
    tc = _pick(c, (256, 128)) if tr == r and r > 256 else c

    def body(p0, p1, p2, p3, w_ref, m_ref, v_ref, g_out, d_out, m_out, v_out):
        g = ((p0[...].astype(F32) + p1[...].astype(F32)) + p2[...].astype(F32)) + p3[...].astype(F32)
        g_out[...] = g
        d_out[...], m_out[...], v_out[...] = _adamw_math(w_ref[...], g, m_ref[...], v_ref[...])

    part = lambda q: _BS((None, tr, tc), functools.partial(lambda i, j, q: (q, i, j), q=q))
    plain = _BS((tr, tc), lambda i, j: (i, j))
    return pl.pallas_call(
        body, name=name, grid=(r // tr, c // tc), in_specs=[part(q) for q in range(4)] + [plain] * 3,
        out_specs=[plain] * 4, out_shape=[jax.ShapeDtypeStruct((r, c), F32)] * 4,
        compiler_params=_params(("parallel", "parallel")),
    )(parts, parts, parts, parts, w, m, v)


def _small_update(gathered, w, m, v):
    r = w.shape[0]

    def body(ga_ref, w_ref, m_ref, v_ref, g_out, d_out, m_out, v_out):
        g = ga_ref[0]
        for dev in range(1, N_DEV):
            g = g + ga_ref[dev]
        g_out[...] = g
        d_out[...], m_out[...], v_out[...] = _adamw_math(w_ref[...], g, m_ref[...], v_ref[...])

    return pl.pallas_call(
        body, name="small_update", out_shape=[jax.ShapeDtypeStruct((r, LANE), F32)] * 4,
    )(gathered, w, m, v)


def _pack_small(vals):
    rows = []
    for name, (r, n) in SMALL:
        flat = vals[name].reshape(-1)
        pad = (-flat.shape[0]) % (SUBLANE * LANE)
        rows.append(jnp.pad(flat, (0, pad)).reshape(-1, LANE))
    return jnp.concatenate(rows, axis=0)


def _unpack_small(packed):
    out, off = {}, 0
    for name, (r, n) in SMALL:
        nrow = -(-(r * n) // (SUBLANE * LANE)) * SUBLANE
        out[name] = packed[off:off + nrow].reshape(-1)[:r * n].reshape(r, n)
        off += nrow
    return out


def kernel(x, positions, ffn1_norm, ffn1_w_in, ffn1_w_out, mix_norm, w_in, hg_lb_table, hg_out_norm, w_hg_branch, mla_q_lora_norm, w_q_up, mla_kv_lora_norm, w_kv_up, q_head_norm, k_head_norm, w_mla_branch, w_merge, b_merge, w_out, ffn2_norm, ffn2_w_in, ffn2_w_out, final_norm, loss_target, m_ffn1_norm, m_ffn1_w_in, m_ffn1_w_out, m_mix_norm, m_w_in, m_hg_lb_table, m_hg_out_norm, m_w_hg_branch, m_mla_q_lora_norm, m_w_q_up, m_mla_kv_lora_norm, m_w_kv_up, m_q_head_norm, m_k_head_norm, m_w_mla_branch, m_w_merge, m_b_merge, m_w_out, m_ffn2_norm, m_ffn2_w_in, m_ffn2_w_out, m_final_norm, v_ffn1_norm, v_ffn1_w_in, v_ffn1_w_out, v_mix_norm, v_w_in, v_hg_lb_table, v_hg_out_norm, v_w_hg_branch, v_mla_q_lora_norm, v_w_q_up, v_mla_kv_lora_norm, v_w_kv_up, v_q_head_norm, v_k_head_norm, v_w_mla_branch, v_w_merge, v_b_merge, v_w_out, v_ffn2_norm, v_ffn2_w_in, v_ffn2_w_out, v_final_norm):
    args = dict(locals())
    t = x.shape[1]
    big_w = {n: args[n][0] for n, _, _ in BIG}
    small = {n: args[n].reshape(shape) for n, shape in SMALL}

    names = [n for n, _, _ in BIG]
    first, rest = names[:2], names[2:]
    full = dict(zip(first, _all_gather([big_w[n].astype(BF16) for n in first], "weights_all_gather_ffn1")))
    g_send, g_recv, g_thru, g_bufs, g_token = _gather_start([big_w[n].astype(BF16) for n in rest], full[first[0]],
                                                            "weights_gather_start")
    gains = dict(small, ffn1_norm=small["ffn1_norm"] + g_token[0:1, 0:1])

    def late_weights(after):
        blocks, bufs = _gather_wait(g_send, g_recv, g_thru, g_bufs, after, "weights_gather_wait")
        return dict(zip(rest, _gather_finish(blocks, bufs, "weights_gather_finish")))

    inv_freq = ROPE_THETA ** (-jnp.arange(0, ROPE, 2, dtype=F32) / ROPE)
    ang = positions[0].astype(F32)[:, None] * inv_freq
    cs = jnp.concatenate([jnp.cos(ang), jnp.cos(ang)], axis=1)
    sn = jnp.concatenate([jnp.sin(ang), jnp.sin(ang)], axis=1)

    c = lax.axis_index("c")
    chip = 2 * lax.axis_index("x") + lax.axis_index("y")
    early = {}

    def chip_sums_of(g, ns, tag):
        from_sibling = _sibling_swap([g[n] for n in ns], "grads_sibling_swap_" + tag)
        return [_chip_sum(g[n], r1, c, "chip_sum_" + n) for n, r1 in zip(ns, from_sibling)]

    def early_grads(g):
        early["names"] = [n for n in names if n in g]
        early["sums"] = chip_sums_of(g, early["names"], "early")
        early["send"], early["recv"], early["thru"], early["lands"], token = _chip_exchange_start(
            early["sums"], "grads_exchange_start")
        return token

    def last_grads(g):
        early["late"] = [n for n in names if n not in early["names"]]
        early["l_send"], early["l_recv"], early["l_thru"], early["l_lands"], token = _chip_exchange_start(
            chip_sums_of(g, early["late"], "late"), "grads_exchange_late_start")
        return token

    loss_row, grad_x, g = _local_step(x[0], loss_target[0], cs, sn, full, late_weights, gains, early_grads,
                                      last_grads)
    late, l_send, l_recv = early["late"], early["l_send"], early["l_recv"]
    l_thru, l_lands = early["l_thru"], early["l_lands"]
    sent, landed = _chip_exchange_wait(early["send"], early["recv"], early["thru"], early["lands"], grad_x,
                                       "grads_exchange_wait")
    exchanged = {n: _put_own(land, lax.dynamic_index_in_dim(own, chip, 0), chip)
                 for n, land, own in zip(early["names"], landed, sent)}

    small_packed = jnp.concatenate([_pack_small(g), jnp.pad(loss_row, ((0, SUBLANE - 1), (0, 0)))], axis=0)
    small_all = _all_gather([small_packed], "small_all_gather")[0]
    zero_tail = jnp.zeros((SUBLANE, LANE), F32)
    pk = lambda d: jnp.concatenate([_pack_small(d), zero_tail], axis=0)
    sg, sd, sm, sv = _small_update(
        small_all, pk(small), pk({n: args["m_" + n].reshape(shape) for n, shape in SMALL}),
        pk({n: args["v_" + n].reshape(shape) for n, shape in SMALL}))
    n_small_rows = _pack_small(small).shape[0]
    loss = sg[n_small_rows, 0]
    outs = {k_: _unpack_small(a) for k_, a in (("grad", sg), ("delta", sd), ("new_m", sm), ("new_v", sv))}

    def adamw(n):
        tr = (lambda a: a.T) if n in GRAD_T else (lambda a: a)
        res = _sum_adamw(exchanged[n], tr(big_w[n]), tr(args["m_" + n][0]), tr(args["v_" + n][0]), "adamw_" + n)
        outs["grad"][n], outs["delta"][n], outs["new_m"][n], outs["new_v"][n] = [tr(a) for a in res]

    for n in early["names"]:
        adamw(n)
    sent, landed = _chip_exchange_wait(l_send, l_recv, l_thru, l_lands, outs["delta"][early["names"][-1]],
                                       "grads_exchange_late_wait")
    for n, land, own in zip(late, landed, sent):
        exchanged[n] = _put_own(land, lax.dynamic_index_in_dim(own, chip, 0), chip)
        adamw(n)

    def shaped(kind, n):
        return outs[kind][n].reshape(args[n].shape)

    return (loss, grad_x[None], *[shaped("grad", n) for n in WEIGHT_ORDER], *[shaped("delta", n) for n in WEIGHT_ORDER],
            *[shaped("new_m", n) for n in WEIGHT_ORDER], *[shaped("new_v", n) for n in WEIGHT_ORDER])
```

```python
import functools

import jax
import jax.numpy as jnp
from jax import lax
from jax.experimental import pallas as pl
from jax.experimental.pallas import tpu as pltpu

F32 = jnp.float32
BF16 = jnp.bfloat16

D = 1024
FF = 2816
NH = 8
HD = 128
ROPE = 64
QK = HD + ROPE
QL = 384
KVL = 256
HGW = NH * HD
CHUNK = 64
EPS = 1e-6
ROPE_THETA = 10000.0
SCALE = QK ** -0.5
LOG2E = 1.4426950408889634

LR, B1, B2, AEPS, WD, STEP = 0.001, 0.9, 0.999, 1e-08, 0.01, 10

HB = 128
SUB = 16
EXP_CLAMP = 80.0
ATT_TILES = (512, 256, 128)
ATT_Q = (1024, 512, 256, 128)
ATT_KEY_TILES = 1
GRAD_DT = BF16
ROW_TILES = (1024, 512, 256, 128)

LANE = 128
SUBLANE = 8
VMEM_LIMIT = 56 << 20

N_DEV = 8
MESH = pl.DeviceIdType.MESH

BIG = (
    ("ffn1_w_in", (D, 2 * FF), 1), ("ffn1_w_out", (FF, D), 0), ("w_in", (D, 4800), 1),
    ("w_hg_branch", (HGW, D), 0), ("w_q_up", (QL, NH * QK), 1), ("w_kv_up", (KVL, NH * 2 * HD), 1),
    ("w_mla_branch", (NH * HD, D), 0), ("w_merge", (D, 2 * D), 1), ("w_out", (D, D), 0),
    ("ffn2_w_in", (D, 2 * FF), 1), ("ffn2_w_out", (FF, D), 0),
)
SMALL = (
    ("ffn1_norm", (1, D)), ("mix_norm", (1, D)), ("hg_lb_table", (2, HGW)), ("hg_out_norm", (1, HD)),
    ("mla_q_lora_norm", (1, QL)), ("mla_kv_lora_norm", (1, KVL)), ("q_head_norm", (1, QK)),
    ("k_head_norm", (1, QK)), ("b_merge", (1, 2 * D)), ("ffn2_norm", (1, D)), ("final_norm", (1, D)),
)
GRAD_T = ("ffn1_w_in", "ffn2_w_in", "w_in", "w_q_up")
WEIGHT_ORDER = ("ffn1_norm", "ffn1_w_in", "ffn1_w_out", "mix_norm", "w_in", "hg_lb_table", "hg_out_norm",
                "w_hg_branch", "mla_q_lora_norm", "w_q_up", "mla_kv_lora_norm", "w_kv_up", "q_head_norm",
                "k_head_norm", "w_mla_branch", "w_merge", "b_merge", "w_out", "ffn2_norm", "ffn2_w_in",
                "ffn2_w_out", "final_norm")


def _pick(n, cands):
    for c in cands:
        if n % c == 0:
            return c
    return n


def _params(sem):
    return pltpu.CompilerParams(dimension_semantics=sem, vmem_limit_bytes=VMEM_LIMIT)


def _sig(x):
    return 0.5 * jnp.tanh(0.5 * x) + 0.5


def _dot(a, b):
    return jnp.dot(a.astype(BF16), b.astype(BF16), preferred_element_type=F32)


def _dot_nt(a, b):
    return lax.dot_general(a.astype(BF16), b.astype(BF16), (((1,), (1,)), ((), ())),
                           preferred_element_type=F32)


def _dot_tn(a, b):
    return lax.dot_general(a.astype(BF16), b.astype(BF16), (((0,), (0,)), ((), ())),
                           preferred_element_type=F32)


def _split3(x):
    x1 = x.astype(BF16)
    r1 = x - x1.astype(F32)
    x2 = r1.astype(BF16)
    x3 = (r1 - x2.astype(F32)).astype(BF16)
    return x1, x2, x3


def _dot_sel(m, x):
    x1, x2, x3 = _split3(x)
    d = lambda p: jnp.dot(m, p, preferred_element_type=F32)
    return d(x1) + d(x2) + d(x3)


def _sel_dot(x, m):
    x1, x2, x3 = _split3(x)
    d = lambda p: jnp.dot(p, m, preferred_element_type=F32)
    return d(x1) + d(x2) + d(x3)


_TN = (1408, 1024, 768, 512, 384, 256, 128)


def _accumulate(acc, o_ref, axis, steps, term):
    step = pl.program_id(axis)

    @pl.when(step == 0)
    def _():
        acc[...] = jnp.zeros_like(acc)

    val = term()
    if isinstance(val, (list, tuple)):
        for e, v in enumerate(val):
            acc[e] += v
    else:
        acc[...] += val

    @pl.when(step == steps - 1)
    def _():
        o_ref[...] = acc[...].astype(o_ref.dtype)


def _mm(a, b, mode, name, out_dtype=F32):
    if mode == "tn":
        t, m = a.shape
        n = b.shape[1]
        tt, tm, tn = _pick(t, ROW_TILES), _pick(m, _TN), _pick(n, _TN)

        def body(a_ref, b_ref, o_ref, acc):
            _accumulate(acc, o_ref, 2, t // tt, lambda: _dot_tn(a_ref[...], b_ref[...]))

        return pl.pallas_call(
            body, name=name, grid=(m // tm, n // tn, t // tt),
            in_specs=[pl.BlockSpec((tt, tm), lambda i, j, k: (k, i)),
                      pl.BlockSpec((tt, tn), lambda i, j, k: (k, j))],
            out_specs=pl.BlockSpec((tm, tn), lambda i, j, k: (i, j)),
            out_shape=jax.ShapeDtypeStruct((m, n), GRAD_DT), scratch_shapes=[pltpu.VMEM((tm, tn), F32)],
            compiler_params=_params(("parallel", "parallel", "arbitrary")),
        )(a, b)

    m, k = a.shape
    tm = _pick(m, ROW_TILES)
    if mode == "nn":
        n = b.shape[1]
        tn = _pick(n, _TN)
        b_spec = pl.BlockSpec((k, tn), lambda i, j: (0, j))
        dot = _dot
    else:
        n = b.shape[0]
        tn = _pick(n, _TN if k <= 4096 else (512, 256, 128))
        b_spec = pl.BlockSpec((tn, k), lambda i, j: (j, 0))
        dot = _dot_nt

    def body(a_ref, b_ref, o_ref):
        o_ref[...] = dot(a_ref[...], b_ref[...]).astype(o_ref.dtype)

    return pl.pallas_call(
        body, name=name, grid=(m // tm, n // tn),
        in_specs=[pl.BlockSpec((tm, k), lambda i, j: (i, 0)), b_spec],
        out_specs=pl.BlockSpec((tm, tn), lambda i, j: (i, j)),
        out_shape=jax.ShapeDtypeStruct((m, n), out_dtype),
        compiler_params=_params(("parallel", "parallel")),
    )(a, b)


_DOTS = {"nn": _dot, "nt": _dot_nt, "tn": _dot_tn}
_BS = pl.BlockSpec


class _AsRef:
    def __init__(self, value):
        self.value = value

    def __getitem__(self, idx):
        return self.value


def _mm_stack_red(a, w, name, kind, epilogue):
    s, t, n = a.shape
    nout = w.shape[2] if kind == "nn" else w.shape[1]
    fn, first, rows, vecs, outs, accs = epilogue[:6]
    rows = [first] + list(rows)
    tm = _pick(t, epilogue[6] if len(epilogue) > 6 else ROW_TILES)
    n_row, n_vec, n_out = len(rows), len(vecs), len(outs)
    dot = _DOTS[kind]

    def body(*refs):
        a_ref, w_ref, prod = refs[0], refs[1], refs[-1]
        row_refs = refs[2:2 + n_row]
        vec_refs = refs[2 + n_row:2 + n_row + n_vec]
        out_refs = refs[2 + n_row + n_vec:2 + n_row + n_vec + n_out]
        acc_refs = refs[2 + n_row + n_vec + n_out:-1]
        i, j = pl.program_id(0), pl.program_id(1)

        @pl.when(j == 0)
        def _():
            prod[...] = jnp.zeros_like(prod)

        prod[...] += dot(a_ref[...], w_ref[...])

        @pl.when(j == s - 1)
        def _():
            res = fn(row_refs[0], _AsRef(prod[...]), *row_refs[1:], *vec_refs, *out_refs)
            if acc_refs:
                @pl.when(i == 0)
                def _():
                    for r in acc_refs:
                        r[...] = jnp.zeros_like(r)

                for r, val in zip(acc_refs, res):
                    r[...] += val

    in_specs = [_BS((None, tm, n), lambda i, j: (j, i, 0)), _BS((None,) + w.shape[1:], lambda i, j: (j, 0, 0))]
    in_specs += [_BS((tm, r.shape[1]), lambda i, j: (i, 0)) for r in rows]
    in_specs += [_BS(v.shape, lambda i, j: (0, 0)) for v in vecs]
    out_specs = [_BS((tm, wd), lambda i, j: (i, 0)) for wd, _ in outs] + [_BS(sh, lambda i, j: (0, 0)) for sh in accs]
    out_shape = [jax.ShapeDtypeStruct((t, wd), dt) for wd, dt in outs] + [jax.ShapeDtypeStruct(sh, F32) for sh in accs]
    return pl.pallas_call(
        body, name=name, grid=(t // tm, s), in_specs=in_specs, out_specs=out_specs, out_shape=out_shape,
        scratch_shapes=[pltpu.VMEM((tm, nout), F32)],
        compiler_params=_params(("arbitrary" if accs else "parallel", "arbitrary")),
    )(a, w, *rows, *vecs)


def _mm_stack_tn(a, b, name):
    grp = 4
    s, t, k = a.shape
    n = b.shape[1]
    tt = _pick(t, ROW_TILES)

    def body(a_ref, b_ref, o_ref, acc):
        def terms():
            shared = b_ref[...]
            return [_dot_tn(a_ref[e], shared) for e in range(grp)]

        _accumulate(acc, o_ref, 1, t // tt, terms)

    return pl.pallas_call(
        body, name=name, grid=(s // grp, t // tt),
        in_specs=[_BS((None, grp, tt, k), lambda j, r: (j, 0, r, 0)), _BS((tt, n), lambda j, r: (r, 0))],
        out_specs=_BS((None, grp, k, n), lambda j, r: (j, 0, 0, 0)),
        out_shape=jax.ShapeDtypeStruct((s // grp, grp, k, n), GRAD_DT), scratch_shapes=[pltpu.VMEM((grp, k, n), F32)],
        compiler_params=_params(("parallel", "arbitrary")),
    )(a.reshape(s // grp, grp, t, k), b).reshape(s, k, n)


def _cols_fwd(x, w, name):
    t, k = x.shape
    s, _, n = w.shape
    tm = _pick(t, ROW_TILES)

    def body(x_ref, w_ref, o_ref):
        x_ = x_ref[...]
        for j in range(s):
            o_ref[:, n * j:n * (j + 1)] = _dot(x_, w_ref[j])

    return pl.pallas_call(
        body, name=name, grid=(t // tm,),
        in_specs=[_BS((tm, k), lambda i: (i, 0)), _BS((s, k, n), lambda i: (0, 0, 0))],
        out_specs=_BS((tm, s * n), lambda i: (i, 0)), out_shape=jax.ShapeDtypeStruct((t, s * n), F32),
        compiler_params=_params(("parallel",)),
    )(x, w)


def _cols_dx(d, w, name):
    t = d.shape[0]
    s, k, n = w.shape
    tm = _pick(t, ROW_TILES)

    def body(d_ref, w_ref, o_ref):
        acc = _dot_nt(d_ref[:, 0:n], w_ref[0])
        for j in range(1, s):
            acc = acc + _dot_nt(d_ref[:, n * j:n * (j + 1)], w_ref[j])
        o_ref[...] = acc

    return pl.pallas_call(
        body, name=name, grid=(t // tm,),
        in_specs=[_BS((tm, s * n), lambda i: (i, 0)), _BS((s, k, n), lambda i: (0, 0, 0))],
        out_specs=_BS((tm, k), lambda i: (i, 0)), out_shape=jax.ShapeDtypeStruct((t, k), F32),
        compiler_params=_params(("parallel",)),
    )(d, w)


def _cols_dw(x, d, n, name):
    t, k = x.shape
    s = d.shape[1] // n
    tt = _pick(t, ROW_TILES)

    def body(x_ref, d_ref, o_ref, acc):
        def terms():
            x_ = x_ref[...]
            return [_dot_tn(x_, d_ref[:, n * j:n * (j + 1)]) for j in range(s)]

        _accumulate(acc, o_ref, 0, t // tt, terms)

    return pl.pallas_call(
        body, name=name, grid=(t // tt,),
        in_specs=[_BS((tt, k), lambda r: (r, 0)), _BS((tt, s * n), lambda r: (r, 0))],
        out_specs=_BS((s, k, n), lambda r: (0, 0, 0)), out_shape=jax.ShapeDtypeStruct((s, k, n), GRAD_DT),
        scratch_shapes=[pltpu.VMEM((s, k, n), F32)], compiler_params=_params(("arbitrary",)),
    )(x, d)


def _rows(fn, name, t, tm, ins, vecs, outs, accs=()):
    n_in, n_out, n_acc = len(ins) + len(vecs), len(outs), len(accs)

    def body(*refs):
        res = fn(*refs[:n_in + n_out])
        if n_acc:
            acc_refs = refs[n_in + n_out:]

            @pl.when(pl.program_id(0) == 0)
            def _():
                for r in acc_refs:
                    r[...] = jnp.zeros_like(r)

            for r, val in zip(acc_refs, res):
                r[...] += val

    in_specs = [pl.BlockSpec((tm, bw), functools.partial(lambda i, cb: (i, cb), cb=cb)) for _, bw, cb in ins]
    in_specs += [pl.BlockSpec(v.shape, lambda i: (0, 0)) for v in vecs]
    out_specs = [pl.BlockSpec((tm, w), lambda i: (i, 0)) for w, _ in outs]
    out_specs += [pl.BlockSpec(s, lambda i: (0, 0)) for s in accs]
    out_shape = [jax.ShapeDtypeStruct((t, w), dt) for w, dt in outs]
    out_shape += [jax.ShapeDtypeStruct(s, F32) for s in accs]
    return pl.pallas_call(
        body, name=name, grid=(t // tm,), in_specs=in_specs, out_specs=out_specs, out_shape=out_shape,
        compiler_params=_params(("arbitrary",) if n_acc else ("parallel",)),
    )(*[a for a, _, _ in ins], *vecs)


def _rms(x, g):
    return x * lax.rsqrt(jnp.mean(x * x, axis=-1, keepdims=True) + EPS) * g


def _rms_bwd(x, g, dy):
    xh = x * lax.rsqrt(jnp.mean(x * x, axis=-1, keepdims=True) + EPS)
    r = lax.rsqrt(jnp.mean(x * x, axis=-1, keepdims=True) + EPS)
    dyg = dy * g
    dx = r * (dyg - xh * jnp.mean(dyg * xh, axis=-1, keepdims=True))
    return dx, jnp.sum(dy * xh, axis=0, keepdims=True)


def _hgrn_mats():
    row = lax.broadcasted_iota(jnp.int32, (HB, HB), 0)
    col = lax.broadcasted_iota(jnp.int32, (HB, HB), 1)
    return row, col


def _hgrn_gates(qr, z, t0, t1):
    lb = 1.0 / (1.0 + jnp.exp(t1 - t0))
    th = jnp.tanh(0.5 * z)
    sz, sneg = 0.5 + 0.5 * th, 0.5 - 0.5 * th
    f = lb + (1.0 - lb) * sz
    return lb, sz, sneg, f, jnp.log(f), (1.0 - lb) * sneg, qr * _sig(qr)


def _sub_decay(x, mid, cum, i):
    n = SUB * (i + 1)
    seen = x[:n] * jnp.exp(jnp.minimum(mid - cum[:n], EXP_CLAMP))
    return seen if n == HB else jnp.concatenate([seen, jnp.zeros((HB - n, x.shape[1]), F32)], axis=0)


def _hgrn_scores(q, k, cum):
    heads, nsub = range(len(q)), HB // SUB
    mids = [[cum[h][SUB * i + SUB // 2 - 1:SUB * i + SUB // 2, :] for i in range(nsub)] for h in heads]
    cmid = [jnp.concatenate([cum[h][SUB * i:SUB * (i + 1)] - mids[h][i] for i in range(nsub)], axis=0) for h in heads]
    qd = [q[h] * jnp.exp(jnp.minimum(cmid[h], EXP_CLAMP)) for h in heads]
    qd_b = [x.astype(BF16) for x in qd]
    kds = [[None] * nsub for _ in heads]
    parts = [[None] * nsub for _ in heads]
    for i in range(nsub):
        for h in heads:
            kds[h][i] = _sub_decay(k[h], mids[h][i], cum[h], i)
        for h in heads:
            parts[h][i] = _dot_nt(qd_b[h][SUB * i:SUB * (i + 1)], kds[h][i])
    return qd, kds, [jnp.concatenate(parts[h], axis=0) for h in heads], cmid


def _hgrn_fwd(p_hg, table, gain, t):
    nblk = t // HB

    def body(q_ref, f_ref, i_ref, g_ref, tab_ref, gain_ref, o_ref, y_ref, st_ref, state):
        @pl.when(pl.program_id(0) == 0)
        def _():
            state[...] = jnp.zeros_like(state)

        row, col = _hgrn_mats()
        causal = col <= row
        tri = causal.astype(BF16)
        heads = range(NH)
        sl = [slice(HD * h, HD * (h + 1)) for h in heads]
        gates = [_hgrn_gates(q_ref[:, sl[h]], f_ref[:, sl[h]], tab_ref[0:1, sl[h]], tab_ref[1:2, sl[h]]) for h in heads]
        lf, k, q = [g[4] for g in gates], [g[5] for g in gates], [g[6] for g in gates]
        v = [i_ref[:, sl[h]] for h in heads]
        cum = [_dot_sel(tri, lf[h]) for h in heads]
        _, _, s, _ = _hgrn_scores(q, k, cum)
        p = [jnp.where(causal, s[h], 0.0) for h in heads]
        st = [state[h] for h in heads]
        o = [_dot(p[h], v[h]) + _dot_nt(q[h] * jnp.exp(cum[h]), st[h]) for h in heads]
        last = [cum[h][HB - 1:HB, :] for h in heads]
        new_st = [st[h] * jnp.exp(last[h]) + _dot_tn(v[h], k[h] * jnp.exp(last[h] - cum[h])) for h in heads]
        for h in heads:
            gr = g_ref[:, sl[h]]
            st_ref[h, 0] = st[h]
            state[h] = new_st[h]
            o_ref[:, sl[h]] = o[h]
            y_ref[:, sl[h]] = (_rms(o[h], gain_ref[...]) * gr * _sig(gr)).astype(BF16)

    blk = lambda cb: pl.BlockSpec((HB, HGW), functools.partial(lambda n, cb: (n, cb), cb=cb))
    return pl.pallas_call(
        body, name="hgrn_fwd", grid=(nblk,),
        in_specs=[blk(0), blk(1), blk(2), blk(3), pl.BlockSpec((2, HGW), lambda n: (0, 0)),
                  pl.BlockSpec((1, HD), lambda n: (0, 0))],
        out_specs=[pl.BlockSpec((HB, HGW), lambda n: (n, 0)), pl.BlockSpec((HB, HGW), lambda n: (n, 0)),
                   pl.BlockSpec((NH, 1, HD, HD), lambda n: (0, n, 0, 0))],
        out_shape=[jax.ShapeDtypeStruct((t, HGW), F32), jax.ShapeDtypeStruct((t, HGW), BF16),
                   jax.ShapeDtypeStruct((NH, nblk, HD, HD), F32)],
        scratch_shapes=[pltpu.VMEM((NH, HD, HD), F32)],
        compiler_params=_params(("arbitrary",)),
    )(p_hg, p_hg, p_hg, p_hg, table, gain)


def _hgrn_bwd(p_hg, table, gain, o_pre, states, dy, t):
    nblk = t // HB

    def body(q_ref, f_ref, i_ref, g_ref, tab_ref, gain_ref, o_ref, st_ref, dy_ref, dp_ref, dtab_ref, dgain_ref,
             dstate):
        @pl.when(pl.program_id(0) == 0)
        def _():
            dstate[...] = jnp.zeros_like(dstate)
            dtab_ref[...] = jnp.zeros_like(dtab_ref)
            dgain_ref[...] = jnp.zeros_like(dgain_ref)

        row, col = _hgrn_mats()
        causal = col <= row
        tri = causal.astype(BF16)
        tri_t = (row <= col).astype(BF16)
        heads, nsub = range(NH), HB // SUB
        sl = [slice(HD * h, HD * (h + 1)) for h in heads]
        rows_of = lambda i: slice(SUB * i, SUB * (i + 1))
        gain_ = gain_ref[...]
        qr, z = [q_ref[:, sl[h]] for h in heads], [f_ref[:, sl[h]] for h in heads]
        v, gr = [i_ref[:, sl[h]] for h in heads], [g_ref[:, sl[h]] for h in heads]
        gates = [_hgrn_gates(qr[h], z[h], tab_ref[0:1, sl[h]], tab_ref[1:2, sl[h]]) for h in heads]
        lb, sz, sneg, f, lf, k, q = ([g[j] for g in gates] for j in range(7))
        cum = [_dot_sel(tri, lf[h]) for h in heads]
        qd, kds, s, cmid = _hgrn_scores(q, k, cum)
        p = [jnp.where(causal, s[h], 0.0) for h in heads]
        st, dst = [st_ref[h, 0] for h in heads], [dstate[h] for h in heads]
        o, dyh, sg = [o_ref[:, sl[h]] for h in heads], [dy_ref[:, sl[h]] for h in heads], [_sig(x) for x in gr]
        dgr = [dyh[h] * _rms(o[h], gain_) * sg[h] * (1.0 + gr[h] * (1.0 - sg[h])) for h in heads]
        norm_bwd = [_rms_bwd(o[h], gain_, dyh[h] * gr[h] * sg[h]) for h in heads]
        do = [x[0] for x in norm_bwd]
        do_b = [x.astype(BF16) for x in do]
        ecum, last = [jnp.exp(x) for x in cum], [x[HB - 1:HB, :] for x in cum]
        qc = [q[h] * ecum[h] for h in heads]
        edec = [jnp.exp(last[h] - cum[h]) for h in heads]
        kdec = [k[h] * edec[h] for h in heads]
        dp = [jnp.where(causal, _dot_nt(do_b[h], v[h]), 0.0) for h in heads]
        dv = [_dot(p[h].T, do_b[h]) + _dot_nt(kdec[h], dst[h]) for h in heads]
        dqc = [_dot(do_b[h], st[h]) for h in heads]
        dkdec = [_dot(v[h], dst[h]) for h in heads]
        new_dst = [dst[h] * jnp.exp(last[h]) + _dot(do[h].T, qc[h]) for h in heads]
        dp_b = [x.astype(BF16) for x in dp]
        dqd = [jnp.concatenate(parts, axis=0) for parts in
               zip(*[[_dot(dp_b[h][rows_of(i)], kds[h][i]) for h in heads] for i in range(nsub)])]
        gq = [dqd[h] * qd[h] for h in heads]
        dq = [dqd[h] * jnp.exp(jnp.minimum(cmid[h], EXP_CLAMP)) + dqc[h] * ecum[h] for h in heads]
        gs = [dkdec[h] * kdec[h] for h in heads]
        dk = [dkdec[h] * edec[h] for h in heads]
        dcum = [dqc[h] * qc[h] - gs[h] + gq[h]
                + jnp.where(row == HB - 1, jnp.sum(gs[h], axis=0, keepdims=True)
                            + jnp.exp(last[h]) * jnp.sum(st[h] * dst[h], axis=0, keepdims=True), 0.0) for h in heads]
        qd_b = [x.astype(BF16) for x in qd]
        for i in range(nsub):
            dkd = [_dot_tn(dp_b[h][rows_of(i)], qd_b[h][rows_of(i)]) for h in heads]
            for h in heads:
                mid = cum[h][SUB * i + SUB // 2 - 1:SUB * i + SUB // 2, :]
                dk[h] = dk[h] + _sub_decay(dkd[h], mid, cum[h], i)
                gk = dkd[h] * kds[h][i]
                to_mid = jnp.sum(gk, axis=0, keepdims=True) - jnp.sum(gq[h][rows_of(i)], axis=0, keepdims=True)
                dcum[h] = dcum[h] - gk + jnp.where(row == SUB * i + SUB // 2 - 1, to_mid, 0.0)
        dlf = [_dot_sel(tri_t, dcum[h]) for h in heads]
        dgain = jnp.zeros((1, HD), F32)
        for h in heads:
            df = dlf[h] / f[h] - dk[h]
            dlb = jnp.sum(df * sneg[h], axis=0, keepdims=True) * lb[h] * (1.0 - lb[h])
            dtab_ref[0:1, sl[h]] += dlb
            dtab_ref[1:2, sl[h]] -= dlb
            sq = _sig(qr[h])
            dstate[h] = new_dst[h]
            dp_ref[:, sl[h]] = (dq[h] * sq * (1.0 + qr[h] * (1.0 - sq))).astype(BF16)
            dp_ref[:, HGW + HD * h:HGW + HD * (h + 1)] = (df * (1.0 - lb[h]) * sz[h] * sneg[h]).astype(BF16)
            dp_ref[:, 2 * HGW + HD * h:2 * HGW + HD * (h + 1)] = dv[h].astype(BF16)
            dp_ref[:, 3 * HGW + HD * h:3 * HGW + HD * (h + 1)] = dgr[h].astype(BF16)
            dgain = dgain + norm_bwd[h][1]
        dgain_ref[...] += dgain

    rev = lambda cb: pl.BlockSpec((HB, HGW), functools.partial(lambda n, cb: (nblk - 1 - n, cb), cb=cb))
    return pl.pallas_call(
        body, name="hgrn_bwd", grid=(nblk,),
        in_specs=[rev(0), rev(1), rev(2), rev(3), pl.BlockSpec((2, HGW), lambda n: (0, 0)),
                  pl.BlockSpec((1, HD), lambda n: (0, 0)), rev(0),
                  pl.BlockSpec((NH, 1, HD, HD), lambda n: (0, nblk - 1 - n, 0, 0)), rev(0)],
        out_specs=[pl.BlockSpec((HB, 4 * HGW), lambda n: (nblk - 1 - n, 0)),
                   pl.BlockSpec((2, HGW), lambda n: (0, 0)), pl.BlockSpec((1, HD), lambda n: (0, 0))],
        out_shape=[jax.ShapeDtypeStruct((t, 4 * HGW), BF16), jax.ShapeDtypeStruct((2, HGW), F32),
                   jax.ShapeDtypeStruct((1, HD), F32)],
        scratch_shapes=[pltpu.VMEM((NH, HD, HD), F32)],
        compiler_params=_params(("arbitrary",)),
    )(p_hg, p_hg, p_hg, p_hg, table, gain, o_pre, states, dy)


def _rope_mat():
    r = lax.broadcasted_iota(jnp.int32, (ROPE, ROPE), 0)
    c = lax.broadcasted_iota(jnp.int32, (ROPE, ROPE), 1)
    half = ROPE // 2
    return ((r == c - half).astype(F32) - (r == c + half).astype(F32)).astype(BF16)


def _mla_prep_fwd(p_mla, cs, sn, wq, wkv, gql, gkvl, gq, gk, t):
    tm = _pick(t, (512, 256, 128))

    def body(p_ref, cs_ref, sn_ref, wq_ref, wkv_ref, gql_ref, gkvl_ref, gq_ref, gk_ref,
             q_ref, k_ref, v_ref):
        rmat = _rope_mat()
        cqn = _rms(p_ref[:, 0:QL], gql_ref[...]).astype(BF16)
        ckvn = _rms(p_ref[:, QL:QL + KVL], gkvl_ref[...]).astype(BF16)
        kpe = p_ref[:, QL + KVL:QL + KVL + ROPE]
        c, s = cs_ref[...], sn_ref[...]
        rot = lambda x: x * c + _sel_dot(x, rmat) * s
        heads = range(NH)
        ssq = lambda x: jnp.sum(x * x, -1, keepdims=True)
        qa = [_dot(cqn, wq_ref[h, :, 0:HD]) for h in heads]
        qr = [_dot(cqn, wq_ref[h, :, HD:QK]) for h in heads]
        kn = [_dot(ckvn, wkv_ref[h, :, 0:HD]) for h in heads]
        vv = [_dot(ckvn, wkv_ref[h, :, HD:2 * HD]) for h in heads]
        kpe_ss = ssq(kpe)
        rq = [lax.rsqrt((ssq(qa[h]) + ssq(qr[h])) / QK + EPS) for h in heads]
        rk = [lax.rsqrt((ssq(kn[h]) + kpe_ss) / QK + EPS) for h in heads]
        q_rope = [rot(qr[h] * rq[h] * gq_ref[:, HD:QK]) for h in heads]
        k_rope = [rot(kpe * rk[h] * gk_ref[:, HD:QK]) for h in heads]
        for h in heads:
            q_ref[h, :, 0:HD] = (qa[h] * rq[h] * gq_ref[:, 0:HD] * (SCALE * LOG2E)).astype(BF16)
            q_ref[h, :, HD:QK] = (q_rope[h] * (SCALE * LOG2E)).astype(BF16)
            k_ref[h, :, 0:HD] = (kn[h] * rk[h] * gk_ref[:, 0:HD]).astype(BF16)
            k_ref[h, :, HD:QK] = k_rope[h].astype(BF16)
            v_ref[h] = vv[h].astype(BF16)

    whole = lambda a: pl.BlockSpec(a.shape, functools.partial(lambda i, nd: (0,) * nd, nd=a.ndim))
    return pl.pallas_call(
        body, name="mla_prep_fwd", grid=(t // tm,),
        in_specs=[pl.BlockSpec((tm, QL + KVL + ROPE), lambda i: (i, 0)), pl.BlockSpec((tm, ROPE), lambda i: (i, 0)),
                  pl.BlockSpec((tm, ROPE), lambda i: (i, 0))] + [whole(a) for a in (wq, wkv, gql, gkvl, gq, gk)],
        out_specs=[pl.BlockSpec((NH, tm, QK), lambda i: (0, i, 0)), pl.BlockSpec((NH, tm, QK), lambda i: (0, i, 0)),
                   pl.BlockSpec((NH, tm, HD), lambda i: (0, i, 0))],
        out_shape=[jax.ShapeDtypeStruct((NH, t, QK), BF16), jax.ShapeDtypeStruct((NH, t, QK), BF16),
                   jax.ShapeDtypeStruct((NH, t, HD), BF16)],
        compiler_params=_params(("parallel",)),
    )(p_mla, cs, sn, wq, wkv, gql, gkvl, gq, gk)


def _mla_prep_bwd(p_mla, cs, sn, wq, wkv, gql, gkvl, gq, gk, dq, dk, dv, t):
    tm = _pick(t, (512, 256, 128))

    def body(p_ref, cs_ref, sn_ref, wq_ref, wkv_ref, gql_ref, gkvl_ref, gq_ref, gk_ref,
             dq_ref, dk_ref, dv_ref,
             dp_ref, dwq_ref, dwkv_ref, dgql_ref, dgkvl_ref, dgq_ref, dgk_ref):
        accs = (dwq_ref, dwkv_ref, dgql_ref, dgkvl_ref, dgq_ref, dgk_ref)

        @pl.when(pl.program_id(0) == 0)
        def _():
            for r in accs:
                r[...] = jnp.zeros_like(r)

        rmat = _rope_mat()
        rmat_t = -rmat
        cq, ckv = p_ref[:, 0:QL], p_ref[:, QL:QL + KVL]
        kpe = p_ref[:, QL + KVL:QL + KVL + ROPE]
        cqn_f, ckvn_f = _rms(cq, gql_ref[...]), _rms(ckv, gkvl_ref[...])
        cqn, ckvn = cqn_f.astype(BF16), ckvn_f.astype(BF16)
        ckvn_t = ckvn_f.T.astype(BF16)
        c, s = cs_ref[...], sn_ref[...]
        unrot = lambda dy: dy * c + _sel_dot(dy * s, rmat_t)
        heads = range(NH)
        rsum = lambda x: jnp.sum(x, -1, keepdims=True)
        csum = lambda x: jnp.sum(x, axis=0, keepdims=True)
        qa = [_dot(cqn, wq_ref[h, :, 0:HD]) for h in heads]
        qr = [_dot(cqn, wq_ref[h, :, HD:QK]) for h in heads]
        kn = [_dot(ckvn, wkv_ref[h, :, 0:HD]) for h in heads]
        dyr = [unrot(dq_ref[h, :, HD:QK] * SCALE) for h in heads]
        dkr = [unrot(dk_ref[h, :, HD:QK]) for h in heads]
        rq = [lax.rsqrt((rsum(qa[h] * qa[h]) + rsum(qr[h] * qr[h])) / QK + EPS) for h in heads]
        xa, xr = [qa[h] * rq[h] for h in heads], [qr[h] * rq[h] for h in heads]
        dya = [dq_ref[h, :, 0:HD] * SCALE for h in heads]
        ga, gr_ = [dya[h] * gq_ref[:, 0:HD] for h in heads], [dyr[h] * gq_ref[:, HD:QK] for h in heads]
        mean = [(rsum(ga[h] * xa[h]) + rsum(gr_[h] * xr[h])) / QK for h in heads]
        dqa = [(rq[h] * (ga[h] - xa[h] * mean[h])).astype(BF16) for h in heads]
        dqr = [(rq[h] * (gr_[h] - xr[h] * mean[h])).astype(BF16) for h in heads]
        kpe_ss = rsum(kpe * kpe)
        rk = [lax.rsqrt((rsum(kn[h] * kn[h]) + kpe_ss) / QK + EPS) for h in heads]
        ya, yr = [kn[h] * rk[h] for h in heads], [kpe * rk[h] for h in heads]
        dka = [dk_ref[h, :, 0:HD] for h in heads]
        ha, hr = [dka[h] * gk_ref[:, 0:HD] for h in heads], [dkr[h] * gk_ref[:, HD:QK] for h in heads]
        mean_k = [(rsum(ha[h] * ya[h]) + rsum(hr[h] * yr[h])) / QK for h in heads]
        dkn = [(rk[h] * (ha[h] - ya[h] * mean_k[h])).astype(BF16) for h in heads]
        dvh = [dv_ref[h].astype(BF16) for h in heads]
        dw = [(_dot_tn(dqa[h], cqn), _dot_tn(dqr[h], cqn), _dot(ckvn_t, dkn[h]), _dot(ckvn_t, dvh[h])) for h in heads]
        back_q = [_dot_nt(dqa[h], wq_ref[h, :, 0:HD]) + _dot_nt(dqr[h], wq_ref[h, :, HD:QK]) for h in heads]
        back_kv = [_dot_nt(dkn[h], wkv_ref[h, :, 0:HD]) + _dot_nt(dvh[h], wkv_ref[h, :, HD:2 * HD]) for h in heads]
        dcqn, dckvn = sum(back_q[1:], back_q[0]), sum(back_kv[1:], back_kv[0])
        dkpe = sum([rk[h] * (hr[h] - yr[h] * mean_k[h]) for h in heads][1:], rk[0] * (hr[0] - yr[0] * mean_k[0]))
        dgq_a = sum([csum(dya[h] * xa[h]) for h in heads][1:], csum(dya[0] * xa[0]))
        dgq_r = sum([csum(dyr[h] * xr[h]) for h in heads][1:], csum(dyr[0] * xr[0]))
        dgk_a = sum([csum(dka[h] * ya[h]) for h in heads][1:], csum(dka[0] * ya[0]))
        dgk_r = sum([csum(dkr[h] * yr[h]) for h in heads][1:], csum(dkr[0] * yr[0]))
        for h in heads:
            dwq_ref[h, 0:HD, :] += dw[h][0]
            dwq_ref[h, HD:QK, :] += dw[h][1]
            dwkv_ref[h, :, 0:HD] += dw[h][2]
            dwkv_ref[h, :, HD:2 * HD] += dw[h][3]
        dcq, dg1 = _rms_bwd(cq, gql_ref[...], dcqn)
        dckv, dg2 = _rms_bwd(ckv, gkvl_ref[...], dckvn)
        dp_ref[:, 0:QL] = dcq.astype(BF16)
        dp_ref[:, QL:QL + KVL] = dckv.astype(BF16)
        dp_ref[:, QL + KVL:QL + KVL + ROPE] = dkpe.astype(BF16)
        dgql_ref[...] += dg1
        dgkvl_ref[...] += dg2
        dgq_ref[:, 0:HD] += dgq_a
        dgq_ref[:, HD:QK] += dgq_r
        dgk_ref[:, 0:HD] += dgk_a
        dgk_ref[:, HD:QK] += dgk_r

    whole = lambda a: pl.BlockSpec(a.shape, functools.partial(lambda i, nd: (0,) * nd, nd=a.ndim))
    acc_shapes = [(NH, QK, QL), wkv.shape, gql.shape, gkvl.shape, gq.shape, gk.shape]
    return pl.pallas_call(
        body, name="mla_prep_bwd", grid=(t // tm,),
        in_specs=[pl.BlockSpec((tm, QL + KVL + ROPE), lambda i: (i, 0)), pl.BlockSpec((tm, ROPE), lambda i: (i, 0)),
                  pl.BlockSpec((tm, ROPE), lambda i: (i, 0))]
        + [whole(a) for a in (wq, wkv, gql, gkvl, gq, gk)]
        + [pl.BlockSpec((NH, tm, QK), lambda i: (0, i, 0)), pl.BlockSpec((NH, tm, QK), lambda i: (0, i, 0)),
           pl.BlockSpec((NH, tm, HD), lambda i: (0, i, 0))],
        out_specs=[pl.BlockSpec((tm, QL + KVL + ROPE), lambda i: (i, 0))]
        + [pl.BlockSpec(s, functools.partial(lambda i, nd: (0,) * nd, nd=len(s))) for s in acc_shapes],
        out_shape=[jax.ShapeDtypeStruct((t, QL + KVL + ROPE), BF16)]
        + [jax.ShapeDtypeStruct(s, F32) for s in acc_shapes],
        compiler_params=_params(("arbitrary",)),
    )(p_mla, cs, sn, wq, wkv, gql, gkvl, gq, gk, dq, dk, dv)


def _chunk_mask(nq, nk, key0, keys_on_rows):
    shape = (nk, nq) if keys_on_rows else (nq, nk)
    qi = lax.broadcasted_iota(jnp.int32, shape, 1 if keys_on_rows else 0) // CHUNK
    ki = lax.broadcasted_iota(jnp.int32, shape, 0 if keys_on_rows else 1) // CHUNK + key0 // CHUNK
    return ki <= qi


def _flash_fwd(q, k, v, t):
    tq = _pick(t, ATT_Q)
    tk = tq // ATT_KEY_TILES

    def body(q_ref, k_ref, v_ref, o_ref, lse_ref):
        i = pl.program_id(1)
        qt = q_ref[0]

        def step(j, carry, key0):
            m, l, acc = carry
            cols = pl.ds(pl.multiple_of(j * tk, tk), tk)
            s = _dot_nt(qt, k_ref[0, cols, :])
            if key0 is not None:
                s = jnp.where(_chunk_mask(tq, tk, key0, False), s, -jnp.inf)
            m_new = jnp.maximum(m, jnp.max(s, axis=-1, keepdims=True))
            p = jnp.exp2(s - m_new)
            alpha = jnp.exp2(m - m_new)
            return m_new, alpha * l + jnp.sum(p, axis=-1, keepdims=True), alpha * acc + _dot(p, v_ref[0, cols, :])

        init = (jnp.full((tq, 1), -jnp.inf, F32), jnp.zeros((tq, 1), F32), jnp.zeros((tq, HD), F32))
        carry = lax.fori_loop(0, ATT_KEY_TILES * i, lambda j, cr: step(j, cr, None), init)
        for h in range(ATT_KEY_TILES):
            carry = step(ATT_KEY_TILES * i + h, carry, h * tk)
        m, l, acc = carry
        o_ref[...] = (acc / l).astype(BF16)
        x1, x2, x3 = _split3(jnp.broadcast_to((m + jnp.log2(l)) * (1.0 / HD), (tq, HD)))
        ones = jnp.ones((SUBLANE, HD), BF16)
        rowsum = lambda p: lax.dot_general(ones, p, (((1,), (1,)), ((), ())), preferred_element_type=F32)
        lse_ref[0] = (rowsum(x1) + rowsum(x2) + rowsum(x3))[0:1, :]

    return pl.pallas_call(
        body, name="flash_fwd", grid=(NH, t // tq),
        in_specs=[pl.BlockSpec((1, tq, QK), lambda h, i: (h, i, 0)), pl.BlockSpec((1, t, QK), lambda h, i: (h, 0, 0)),
                  pl.BlockSpec((1, t, HD), lambda h, i: (h, 0, 0))],
        out_specs=[pl.BlockSpec((tq, HD), lambda h, i: (i, h)), pl.BlockSpec((1, 1, tq), lambda h, i: (h, 0, i))],
        out_shape=[jax.ShapeDtypeStruct((t, NH * HD), BF16), jax.ShapeDtypeStruct((NH, 1, t), F32)],
        compiler_params=_params(("parallel", "parallel")),
    )(q, k, v)


def _attn_out_bwd(dy, w, o, t):
    tm = _pick(t, ATT_TILES)

    def body(dy_ref, w_ref, o_ref, do_ref, delta_ref):
        do = _dot_nt(dy_ref[...], w_ref[...]).astype(BF16)
        do_ref[...] = do
        ones = jnp.ones((SUBLANE, HD), BF16)
        rowsum = lambda p: lax.dot_general(ones, p, (((1,), (1,)), ((), ())), preferred_element_type=F32)
        sl = [slice(HD * h, HD * (h + 1)) for h in range(NH)]
        parts = [_split3(do[:, sl[h]].astype(F32) * o_ref[:, sl[h]].astype(F32)) for h in range(NH)]
        sums = [[rowsum(x) for x in parts[h]] for h in range(NH)]
        for h in range(NH):
            delta_ref[h] = (sums[h][0] + sums[h][1] + sums[h][2])[0:1, :]

    return pl.pallas_call(
        body, name="attn_out_bwd", grid=(t // tm,),
        in_specs=[_BS((tm, dy.shape[1]), lambda i: (i, 0)), _BS(w.shape, lambda i: (0, 0)),
                  _BS((tm, NH * HD), lambda i: (i, 0))],
        out_specs=[_BS((tm, NH * HD), lambda i: (i, 0)), _BS((NH, 1, tm), lambda i: (0, 0, i))],
        out_shape=[jax.ShapeDtypeStruct((t, NH * HD), BF16), jax.ShapeDtypeStruct((NH, 1, t), F32)],
        compiler_params=_params(("parallel",)),
    )(dy, w, o)


def _flash_bwd(q, k, v, do, lse_row, delta_row, t):
    tq = _pick(t, ATT_Q)
    tk = tq // ATT_KEY_TILES

    def body(q_ref, k_ref, v_ref, do_ref, lse_ref, delta_ref, dq_ref, dk_ref, dv_ref):
        j = pl.program_id(1)

        @pl.when(j == 0)
        def _():
            dq_ref[...] = jnp.zeros_like(dq_ref)

        kt, vt = k_ref[0], v_ref[0]

        def step(i, carry, key0):
            dk, dv = carry
            rows = pl.ds(pl.multiple_of(i * tq, tq), tq)
            qt, dot_ = q_ref[0, rows, :], do_ref[rows, :]
            p = jnp.exp2(_dot_nt(kt, qt) - lse_ref[0, :, rows])
            if key0 is not None:
                p = jnp.where(_chunk_mask(tq, tk, key0, True), p, 0.0)
            ds = (p * (_dot_nt(vt, dot_) - delta_ref[0, :, rows])).astype(BF16)
            dq_ref[0, rows, :] += _dot_tn(ds, kt)
            return dk + _dot(ds, qt), dv + _dot(p, dot_)

        own = j // ATT_KEY_TILES
        carry = step(own, (jnp.zeros((tk, QK), F32), jnp.zeros((tk, HD), F32)), (j % ATT_KEY_TILES) * tk)
        dk, dv = lax.fori_loop(own + 1, t // tq, lambda i, cr: step(i, cr, None), carry)
        dk_ref[0] = dk * (1.0 / LOG2E)
        dv_ref[0] = dv

    return pl.pallas_call(
        body, name="flash_bwd", grid=(NH, t // tk),
        in_specs=[pl.BlockSpec((1, t, QK), lambda h, j: (h, 0, 0)), pl.BlockSpec((1, tk, QK), lambda h, j: (h, j, 0)),
                  pl.BlockSpec((1, tk, HD), lambda h, j: (h, j, 0)), pl.BlockSpec((t, HD), lambda h, j: (0, h)),
                  pl.BlockSpec((1, 1, t), lambda h, j: (h, 0, 0)), pl.BlockSpec((1, 1, t), lambda h, j: (h, 0, 0))],
        out_specs=[pl.BlockSpec((1, t, QK), lambda h, j: (h, 0, 0)), pl.BlockSpec((1, tk, QK), lambda h, j: (h, j, 0)),
                   pl.BlockSpec((1, tk, HD), lambda h, j: (h, j, 0))],
        out_shape=[jax.ShapeDtypeStruct((NH, t, QK), F32), jax.ShapeDtypeStruct((NH, t, QK), F32),
                   jax.ShapeDtypeStruct((NH, t, HD), F32)],
        compiler_params=_params(("parallel", "arbitrary")),
    )(q, k, v, do, lse_row, delta_row)


def _ffn_in(xn, w_in, name):
    t, k = xn.shape
    s, _, n = w_in.shape
    tm = _pick(t, ROW_TILES)

    def body(x_ref, w_ref, gu_ref, a_ref):
        x = x_ref[...]
        g, u = _dot(x, w_ref[0]), _dot(x, w_ref[1])
        gu_ref[0] = g.astype(BF16)
        gu_ref[1] = u.astype(BF16)
        a_ref[...] = (g * _sig(g) * u).astype(BF16)

    return pl.pallas_call(
        body, name=name, grid=(t // tm, s // 2),
        in_specs=[_BS((tm, k), lambda i, j: (i, 0)), _BS((2, None, k, n), lambda i, j: (0, j, 0, 0))],
        out_specs=[_BS((2, None, tm, n), lambda i, j: (0, j, i, 0)), _BS((None, tm, n), lambda i, j: (j, i, 0))],
        out_shape=[jax.ShapeDtypeStruct((2, s // 2, t, n), BF16), jax.ShapeDtypeStruct((s // 2, t, n), BF16)],
        compiler_params=_params(("parallel", "parallel")),
    )(xn, w_in.reshape(2, s // 2, k, n))


def _ffn_dgu(dfo, w_out, gu, name):
    t, k = dfo.shape
    s, n, _ = w_out.shape
    tm = _pick(t, ROW_TILES)

    def body(d_ref, w_ref, gu_ref, o_ref):
        da = _dot_nt(d_ref[...], w_ref[...])
        g, u = gu_ref[0].astype(F32), gu_ref[1].astype(F32)
        sg = _sig(g)
        o_ref[0] = (da * u * sg * (1.0 + g * (1.0 - sg))).astype(BF16)
        o_ref[1] = (da * g * sg).astype(BF16)

    pair = _BS((2, None, tm, n), lambda i, j: (0, j, i, 0))
    return pl.pallas_call(
        body, name=name, grid=(t // tm, s),
        in_specs=[_BS((tm, k), lambda i, j: (i, 0)), _BS((None, n, k), lambda i, j: (j, 0, 0)), pair],
        out_specs=pair, out_shape=jax.ShapeDtypeStruct((2, s, t, n), BF16),
        compiler_params=_params(("parallel", "parallel")),
    )(dfo, w_out, gu).reshape(2 * s, t, n)


def _ffn_fwd(xn, w_in, w_out, tag, epilogue):
    gu, a = _ffn_in(xn, w_in, tag + "_in")
    return gu, a, _mm_stack_red(a, w_out, tag + "_out", "nn", epilogue)


def _ffn_bwd(dfo, xn, gu, a, w_in, w_out, tag, epilogue):
    dw_out = _mm_stack_tn(a, dfo, tag + "_dwout")
    dgu = _ffn_dgu(dfo, w_out, gu, tag + "_dgu")
    dw_in = _mm_stack_tn(dgu, xn, tag + "_dwin")
    if callable(epilogue):
        epilogue = epilogue(dw_in, dw_out)
    return _mm_stack_red(dgu, w_in, tag + "_dxn", "nt", epilogue), dw_in, dw_out


def _local_step(x, target, cs, sn, w, late_weights, s, early_grads, last_grads):
    t = x.shape[0]
    tm = _pick(t, (256, 128))
    g = {}
    ffn_out = lambda n: w[n].reshape(4, FF // 4, D)

    def norm_fn(x_ref, g_ref, o_ref):
        o_ref[...] = _rms(x_ref[...], g_ref[...]).astype(BF16)

    xn1 = _rows(norm_fn, "norm1", t, tm, [(x, D, 0)], [s["ffn1_norm"]], [(D, BF16)])[0]
    def res_norm_fn(scale):
        def fn(h_ref, f_ref, g_ref, h_out, n_out):
            h = h_ref[...] + scale * f_ref[...]
            h_out[...] = h
            n_out[...] = _rms(h, g_ref[...]).astype(BF16)
        return fn

    gu1, a1, (h1, u) = _ffn_fwd(xn1, w["ffn1_w_in"], ffn_out("ffn1_w_out"), "ffn1",
                                (res_norm_fn(0.5), x, [], [s["mix_norm"]], [(D, F32), (D, BF16)], []))
    w = {**w, **late_weights(h1)}
    rows_of = lambda n: w[n].reshape(-1, w[n].shape[2])
    w_hgb, w_mlab, w_o = rows_of("w_hg_branch"), rows_of("w_mla_branch"), rows_of("w_out")
    w_in_nat = w["w_in"].transpose(1, 0, 2).reshape(D, -1)
    w_mrg = w["w_merge"]
    mw = w_mrg.shape[2]
    w_in_hg, w_in_mla = w_in_nat[:, :4 * HGW], w_in_nat[:, 4 * HGW:]
    p_hg = _mm(u, w_in_hg, "nn", "proj_hg")
    p_mla = _mm(u, w_in_mla, "nn", "proj_mla")
    gpre = _cols_fwd(u, w_mrg, "proj_gate")
    o_pre, hgy, states = _hgrn_fwd(p_hg, s["hg_lb_table"], s["hg_out_norm"], t)
    prep_args = (p_mla, cs, sn, w["w_q_up"], w["w_kv_up"], s["mla_q_lora_norm"], s["mla_kv_lora_norm"],
                 s["q_head_norm"], s["k_head_norm"])
    q, k, v = _mla_prep_fwd(*prep_args, t)
    att, lse = _flash_fwd(q, k, v, t)
    y_hg = _mm(hgy, w_hgb, "nn", "branch_hg")
    def mix_fn(gp_ref, ym_ref, yh_ref, b_ref, o_ref, ym_out):
        gh = _sig(gp_ref[:, 0:D] + b_ref[:, 0:D])
        gm = _sig(gp_ref[:, D:2 * D] + b_ref[:, D:2 * D])
        ym = ym_ref[...]
        ym_out[...] = ym
        o_ref[...] = (gh * yh_ref[...] + gm * ym).astype(BF16)

    mixed, y_mla = _mm_stack_red(att[None], w_mlab[None], "branch_mla", "nn",
                                 (mix_fn, gpre, [y_hg], [s["b_merge"]], [(D, BF16), (D, F32)], [], (512, 256, 128)))
    h2, xn2 = _mm_stack_red(mixed[None], w_o[None], "mix_out", "nn",
                            (res_norm_fn(1.0), h1, [], [s["ffn2_norm"]], [(D, F32), (D, BF16)], []))

    def loss_fn(h_ref, f_ref, tg_ref, g_ref, dh_out, dhb_out):
        h = h_ref[...] + 0.5 * f_ref[...]
        e = _rms(h, g_ref[...]) - tg_ref[...]
        dh, dgain = _rms_bwd(h, g_ref[...], e / D)
        dh_out[...] = dh
        dhb_out[...] = (0.5 * dh).astype(BF16)
        return dgain, jnp.full((1, LANE), 0.5 / D * jnp.sum(e * e), F32)

    gu2, a2, (dh3, dfo2, g["final_norm"], loss) = _ffn_fwd(
        xn2, w["ffn2_w_in"], ffn_out("ffn2_w_out"), "ffn2",
        (loss_fn, h2, [target], [s["final_norm"]], [(D, F32), (D, BF16)], [(1, D), (1, LANE)]))

    def norm_bwd_fn(scale):
        def fn(h_ref, dxn_ref, dh_ref, g_ref, dh_out, dhb_out):
            dx, dgain = _rms_bwd(h_ref[...], g_ref[...], dxn_ref[...])
            dh = dh_ref[...] + dx
            dh_out[...] = dh
            dhb_out[...] = (scale * dh).astype(BF16)
            return (dgain,)
        return fn

    as_rows = lambda a: a.reshape((N_DEV, -1) + a.shape[-1:])
    (dh2, dh2b, g["ffn2_norm"]), g["ffn2_w_in"], dwo = _ffn_bwd(
        dfo2, xn2, gu2, a2, w["ffn2_w_in"], ffn_out("ffn2_w_out"), "ffn2",
        (norm_bwd_fn(1.0), h2, [dh3], [s["ffn2_norm"]], [(D, F32), (D, BF16)], [(1, D)]))
    g["ffn2_w_out"] = as_rows(dwo)
    g["w_out"] = as_rows(_mm(mixed, dh2b, "tn", "mix_out_dw"))

    def mix_bwd_fn(gp_ref, dm_ref, yh_ref, ym_ref, b_ref, dyh_out, dym_out, dg_out):
        gh = _sig(gp_ref[:, 0:D] + b_ref[:, 0:D])
        gm = _sig(gp_ref[:, D:2 * D] + b_ref[:, D:2 * D])
        dm = dm_ref[...]
        dyh_out[...] = (dm * gh).astype(BF16)
        dym_out[...] = (dm * gm).astype(BF16)
        dgh = dm * yh_ref[...] * gh * (1.0 - gh)
        dgm = dm * ym_ref[...] * gm * (1.0 - gm)
        dg_out[:, 0:D] = dgh.astype(BF16)
        dg_out[:, D:2 * D] = dgm.astype(BF16)
        return (jnp.concatenate([jnp.sum(dgh, axis=0, keepdims=True), jnp.sum(dgm, axis=0, keepdims=True)], axis=1),)

    dyh, dym, dgpre, g["b_merge"] = _mm_stack_red(
        dh2b[None], w_o[None], "mix_out_dx", "nt",
        (mix_bwd_fn, gpre, [y_hg, y_mla], [s["b_merge"]], [(D, BF16), (D, BF16), (2 * D, BF16)], [(1, 2 * D)],
         (512, 256, 128)))
    g["w_hg_branch"] = as_rows(_mm(hgy, dyh, "tn", "branch_hg_dw"))
    g["w_mla_branch"] = as_rows(_mm(att, dym, "tn", "branch_mla_dw"))
    g["w_merge"] = _cols_dw(u, dgpre, mw, "proj_gate_dw")
    dhgy = _mm(dyh, w_hgb, "nt", "branch_hg_dx")
    datt, delta = _attn_out_bwd(dym, w_mlab, att, t)
    du_gate = _cols_dx(dgpre, w_mrg, "proj_gate_dx")

    dq, dk, dv = _flash_bwd(q, k, v, datt, lse, delta, t)
    (dp_mla, dwq, dwkv, g["mla_q_lora_norm"], g["mla_kv_lora_norm"], g["q_head_norm"],
     g["k_head_norm"]) = _mla_prep_bwd(*prep_args, dq, dk, dv, t)
    g["w_q_up"], g["w_kv_up"] = dwq.astype(GRAD_DT), dwkv.astype(GRAD_DT)
    dp_hg, g["hg_lb_table"], g["hg_out_norm"] = _hgrn_bwd(p_hg, s["hg_lb_table"], s["hg_out_norm"], o_pre, states,
                                                          dhgy, t)
    dw_in_nat = jnp.concatenate([_mm(u, dp_hg, "tn", "proj_hg_dw"), _mm(u, dp_mla, "tn", "proj_mla_dw")], axis=1)
    g["w_in"] = dw_in_nat.T.reshape(N_DEV, -1, D)
    du_mla = _mm(dp_mla, w_in_mla, "nt", "proj_mla_dx")

    def mixnorm_bwd_fn(h_ref, a_ref, b_ref, c_ref, dh_ref, g_ref, dh_out, dhb_out):
        dx, dgain = _rms_bwd(h_ref[...], g_ref[...], a_ref[...] + b_ref[...] + c_ref[...])
        dh = dh_ref[...] + dx
        dh_out[...] = dh
        dhb_out[...] = (0.5 * dh).astype(BF16)
        return (dgain,)

    mix_gain = s["mix_norm"] + early_grads(g)[0:1, 0:1]
    dh1, dfo1, g["mix_norm"] = _mm_stack_red(
        dp_hg[None], w_in_hg[None], "proj_hg_dx", "nt",
        (mixnorm_bwd_fn, h1, [du_mla, du_gate, dh2], [mix_gain], [(D, F32), (D, BF16)], [(1, D)], (512, 256, 128)))
    def last_stage(dw_in, dw_out):
        g["ffn1_w_in"], g["ffn1_w_out"] = dw_in, as_rows(dw_out)
        gain = s["ffn1_norm"] + last_grads(g)[0:1, 0:1]
        return norm_bwd_fn(1.0), x, [dh1], [gain], [(D, F32), (D, BF16)], [(1, D)]

    (grad_x, _, g["ffn1_norm"]), _, _ = _ffn_bwd(dfo1, xn1, gu1, a1, w["ffn1_w_in"], ffn_out("ffn1_w_out"), "ffn1",
                                                 last_stage)
    return loss, grad_x, g


def _coords():
    return lax.axis_index("x"), lax.axis_index("y"), lax.axis_index("c")


def _hbm_call(body, name, ins, out_shapes, scratch):
    any_spec = pl.BlockSpec(memory_space=pl.ANY)
    return pl.pallas_call(
        body, name=name, out_shape=[jax.ShapeDtypeStruct(s, dt) for s, dt in out_shapes],
        in_specs=[any_spec] * len(ins), out_specs=[any_spec] * len(out_shapes), scratch_shapes=scratch,
    )(*ins)


def _my_slot():
    return 4 * lax.axis_index("x") + 2 * lax.axis_index("y") + lax.axis_index("c")


def _put_own(buf, own, index):
    return lax.dynamic_update_index_in_dim(buf, own, index, 0)


def _all_gather(blocks, name):
    nb = len(blocks)

    def body(*refs):
        x_refs, out_refs = refs[:nb], refs[nb:2 * nb]
        send_sems, recv_sems = refs[2 * nb:]
        x, y, c = _coords()
        me, sibling = (x, y, c), (x, y, 1 - c)
        chips = [(1 - x, y), (x, 1 - y), (1 - x, 1 - y)]

        def slot(b, px, py, pc):
            return out_refs[b].at[4 * px + 2 * py + pc]

        def copy(b, kk, block_of, to, src=None):
            return pltpu.make_async_remote_copy(
                src_ref=slot(b, *block_of) if src is None else src, dst_ref=slot(b, *block_of),
                send_sem=send_sems.at[b, kk], recv_sem=recv_sems.at[b, kk], device_id=to, device_id_type=MESH)

        first = [copy(b, 0, me, sibling, src=x_refs[b]) for b in range(nb)]
        first += [copy(b, 1 + j, me, (*chip, c), src=x_refs[b]) for j, chip in enumerate(chips) for b in range(nb)]
        for cp in first:
            cp.start()
        passed = []
        for j, chip in enumerate(chips):
            for b in range(nb):
                copy(b, 1 + j, (*chip, c), me).wait_recv()
                passed.append(copy(b, 4 + j, (*chip, c), sibling))
                passed[-1].start()
        for b in range(nb):
            copy(b, 0, sibling, me).wait_recv()
        for j, chip in enumerate(chips):
            for b in range(nb):
                copy(b, 4 + j, (*chip, 1 - c), me).wait_recv()
        for cp in first + passed:
            cp.wait_send()

    outs = _hbm_call(body, name, blocks, [((N_DEV,) + b.shape, b.dtype) for b in blocks],
                     [pltpu.SemaphoreType.DMA((nb, 7)), pltpu.SemaphoreType.DMA((nb, 7))])
    return [_put_own(o, b[None], _my_slot()) for o, b in zip(outs, blocks)]


def _gather_peers():
    x, y, c = _coords()
    return (x, y, c), [(x, y, 1 - c), (1 - x, y, c), (x, 1 - y, c), (1 - x, 1 - y, c)]


def _gather_start(blocks, after, name):
    nb = len(blocks)
    hbm, sem = pl.BlockSpec(memory_space=pltpu.HBM), pl.BlockSpec(memory_space=pltpu.SEMAPHORE)

    def body(*refs):
        x_refs, out_refs = refs[:nb], refs[nb:2 * nb]
        send_sems, recv_sems, token = refs[2 * nb + 1], refs[2 * nb + 2], refs[-1]
        (x, y, c), peers = _gather_peers()
        for kk, peer in enumerate(peers):
            for b in range(nb):
                pltpu.make_async_remote_copy(
                    src_ref=x_refs[b], dst_ref=out_refs[b].at[4 * x + 2 * y + c], send_sem=send_sems.at[4 * b + kk],
                    recv_sem=recv_sems.at[4 * b + kk], device_id=peer, device_id_type=MESH).start()
        token[...] = jnp.zeros_like(token)

    gathers = [pltpu.with_memory_space_constraint(lax.empty((N_DEV,) + b.shape, b.dtype), pltpu.HBM) for b in blocks]
    outs = pl.pallas_call(
        body, name=name,
        out_shape=(pltpu.SemaphoreType.DMA((4 * nb,)), pltpu.SemaphoreType.DMA((4 * nb,)),
                   *[pltpu.HBM(b.shape, b.dtype) for b in blocks], *[pltpu.HBM(b.shape, b.dtype) for b in gathers],
                   jax.ShapeDtypeStruct((SUBLANE, LANE), F32)),
        in_specs=[hbm] * (2 * nb) + [pl.BlockSpec(memory_space=pl.ANY)],
        out_specs=(sem, sem, *[hbm] * (2 * nb), pl.BlockSpec(memory_space=pltpu.VMEM)),
        input_output_aliases={i: 2 + i for i in range(2 * nb)},
        compiler_params=pltpu.CompilerParams(has_side_effects=pltpu.SideEffectType.DATAFLOW_SIDE_EFFECTING),
    )(*[pltpu.with_memory_space_constraint(b, pltpu.HBM) for b in blocks], *gathers, after)
    return outs[0], outs[1], list(outs[2:2 + nb]), list(outs[2 + nb:2 + 2 * nb]), outs[-1]


def _gather_wait(send_sems, recv_sems, thru, gathers, after, name):
    nb = len(thru)
    hbm, sem = pl.BlockSpec(memory_space=pltpu.HBM), pl.BlockSpec(memory_space=pltpu.SEMAPHORE)

    def body(*refs):
        x_refs, out_refs = refs[:nb], refs[nb:2 * nb]
        send_sems_, recv_sems_ = refs[2 * nb], refs[2 * nb + 1]
        _, peers = _gather_peers()
        for kk, (px, py, pc) in enumerate(peers):
            for b in range(nb):
                cp = pltpu.make_async_remote_copy(
                    src_ref=x_refs[b], dst_ref=out_refs[b].at[4 * px + 2 * py + pc], send_sem=send_sems_.at[4 * b + kk],
                    recv_sem=recv_sems_.at[4 * b + kk], device_id=(px, py, pc), device_id_type=MESH)
                cp.wait_send()
                cp.wait_recv()

    outs = pl.pallas_call(
        body, name=name,
        out_shape=(*[pltpu.HBM(b.shape, b.dtype) for b in thru], *[pltpu.HBM(b.shape, b.dtype) for b in gathers]),
        in_specs=[hbm] * (2 * nb) + [sem, sem, pl.BlockSpec(memory_space=pl.ANY)], out_specs=[hbm] * (2 * nb),
        input_output_aliases={i: i for i in range(2 * nb)},
        compiler_params=pltpu.CompilerParams(has_side_effects=pltpu.SideEffectType.DATAFLOW_SIDE_EFFECTING),
    )(*thru, *gathers, send_sems, recv_sems, after)
    return list(outs[:nb]), list(outs[nb:])


def _gather_finish(blocks, gathers, name):
    nb = len(blocks)

    def body(*refs):
        x_refs, in_refs, out_refs = refs[:nb], refs[nb:2 * nb], refs[2 * nb:3 * nb]
        send_sems, recv_sems = refs[3 * nb:]
        (x, y, c), peers = _gather_peers()
        copies = []
        for j, (px, py, _) in enumerate(peers[1:]):
            for b in range(nb):
                copies.append(pltpu.make_async_remote_copy(
                    src_ref=in_refs[b].at[4 * px + 2 * py + c], dst_ref=out_refs[b].at[4 * px + 2 * py + c],
                    send_sem=send_sems.at[b, j], recv_sem=recv_sems.at[b, j], device_id=(x, y, 1 - c),
                    device_id_type=MESH))
                copies[-1].start()
        for j, (px, py, _) in enumerate(peers[1:]):
            for b in range(nb):
                pltpu.make_async_remote_copy(
                    src_ref=in_refs[b].at[4 * px + 2 * py + c], dst_ref=out_refs[b].at[4 * px + 2 * py + 1 - c],
                    send_sem=send_sems.at[b, j], recv_sem=recv_sems.at[b, j], device_id=(x, y, 1 - c),
                    device_id_type=MESH).wait_recv()
        for cp in copies:
            cp.wait_send()

    any_spec = pl.BlockSpec(memory_space=pl.ANY)
    outs = pl.pallas_call(
        body, name=name, out_shape=[jax.ShapeDtypeStruct(b.shape, b.dtype) for b in gathers],
        in_specs=[any_spec] * (2 * nb), out_specs=[any_spec] * nb,
        input_output_aliases={nb + i: i for i in range(nb)},
        scratch_shapes=[pltpu.SemaphoreType.DMA((nb, 3)), pltpu.SemaphoreType.DMA((nb, 3))],
    )(*blocks, *gathers)
    return [_put_own(o, b[None], _my_slot()) for o, b in zip(outs, blocks)]


def _sibling_swap(bufs, name):
    nb = len(bufs)

    def body(*refs):
        x_refs, out_refs = refs[:nb], refs[nb:2 * nb]
        send_sems, recv_sems = refs[2 * nb:]
        x, y, c = _coords()
        copies = [pltpu.make_async_remote_copy(
            src_ref=x_refs[b].at[2 * q + 1 - c], dst_ref=out_refs[b].at[q], send_sem=send_sems.at[b, q],
            recv_sem=recv_sems.at[b, q], device_id=(x, y, 1 - c), device_id_type=MESH)
            for b in range(nb) for q in range(4)]
        for cp in copies:
            cp.start()
        for cp in copies:
            cp.wait()

    return _hbm_call(body, name, bufs, [((4,) + b.shape[1:], b.dtype) for b in bufs],
                     [pltpu.SemaphoreType.DMA((nb, 4)), pltpu.SemaphoreType.DMA((nb, 4))])


def _chip_exchange_start(bufs, name):
    nb = len(bufs)
    hbm, sem = pl.BlockSpec(memory_space=pltpu.HBM), pl.BlockSpec(memory_space=pltpu.SEMAPHORE)

    def body(*refs):
        x_refs, land_refs = refs[:nb], refs[nb:2 * nb]
        send_sems, recv_sems, token = refs[2 * nb], refs[2 * nb + 1], refs[-1]
        x, y, c = _coords()
        for j, (px, py) in enumerate([(1 - x, y), (x, 1 - y), (1 - x, 1 - y)]):
            for b in range(nb):
                pltpu.make_async_remote_copy(
                    src_ref=x_refs[b].at[2 * px + py], dst_ref=land_refs[b].at[2 * x + y], send_sem=send_sems.at[3 * b + j],
                    recv_sem=recv_sems.at[3 * b + j], device_id=(px, py, c), device_id_type=MESH).start()
        token[...] = jnp.zeros_like(token)

    lands = [pltpu.with_memory_space_constraint(lax.empty(b.shape, b.dtype), pltpu.HBM) for b in bufs]
    outs = pl.pallas_call(
        body, name=name,
        out_shape=(pltpu.SemaphoreType.DMA((3 * nb,)), pltpu.SemaphoreType.DMA((3 * nb,)),
                   *[pltpu.HBM(b.shape, b.dtype) for b in bufs], *[pltpu.HBM(b.shape, b.dtype) for b in bufs],
                   jax.ShapeDtypeStruct((SUBLANE, LANE), F32)),
        in_specs=[hbm] * (2 * nb), out_specs=(sem, sem, *[hbm] * (2 * nb), pl.BlockSpec(memory_space=pltpu.VMEM)),
        input_output_aliases={i: 2 + i for i in range(2 * nb)},
        compiler_params=pltpu.CompilerParams(has_side_effects=pltpu.SideEffectType.DATAFLOW_SIDE_EFFECTING),
    )(*[pltpu.with_memory_space_constraint(b, pltpu.HBM) for b in bufs], *lands)
    return outs[0], outs[1], list(outs[2:2 + nb]), list(outs[2 + nb:2 + 2 * nb]), outs[-1]


def _chip_exchange_wait(send_sems, recv_sems, thru, lands, after, name):
    nb = len(thru)
    hbm, sem = pl.BlockSpec(memory_space=pltpu.HBM), pl.BlockSpec(memory_space=pltpu.SEMAPHORE)

    def body(*refs):
        x_refs, land_refs = refs[:nb], refs[nb:2 * nb]
        send_sems_, recv_sems_ = refs[2 * nb], refs[2 * nb + 1]
        x, y, c = _coords()
        for j, (px, py) in enumerate([(1 - x, y), (x, 1 - y), (1 - x, 1 - y)]):
            for b in range(nb):
                cp = pltpu.make_async_remote_copy(
                    src_ref=x_refs[b].at[2 * px + py], dst_ref=land_refs[b].at[2 * px + py],
                    send_sem=send_sems_.at[3 * b + j], recv_sem=recv_sems_.at[3 * b + j], device_id=(px, py, c),
                    device_id_type=MESH)
                cp.wait_send()
                cp.wait_recv()

    outs = pl.pallas_call(
        body, name=name,
        out_shape=(*[pltpu.HBM(b.shape, b.dtype) for b in thru], *[pltpu.HBM(b.shape, b.dtype) for b in lands]),
        in_specs=[hbm] * (2 * nb) + [sem, sem, pl.BlockSpec(memory_space=pl.ANY)], out_specs=[hbm] * (2 * nb),
        input_output_aliases={i: i for i in range(2 * nb)},
        compiler_params=pltpu.CompilerParams(has_side_effects=pltpu.SideEffectType.DATAFLOW_SIDE_EFFECTING),
    )(*thru, *lands, send_sems, recv_sems, after)
    return list(outs[:nb]), list(outs[nb:])


def _chip_sum(g, r1, c, name):
    _, r, cw = g.shape
    tr = _pick(r, (256, 176, 128))

    def body(c_ref, g_ref, r_ref, o_ref):
        o_ref[...] = (g_ref[...].astype(F32) + r_ref[...].astype(F32)).astype(GRAD_DT)

    grid_spec = pltpu.PrefetchScalarGridSpec(
        num_scalar_prefetch=1, grid=(4, r // tr),
        in_specs=[_BS((None, None, tr, cw), lambda q, i, c_ref: (q, c_ref[0], i, 0)),
                  _BS((None, tr, cw), lambda q, i, c_ref: (q, i, 0))],
        out_specs=_BS((None, tr, cw), lambda q, i, c_ref: (q, i, 0)))
    return pl.pallas_call(
        body, name=name, grid_spec=grid_spec, out_shape=jax.ShapeDtypeStruct((4, r, cw), GRAD_DT),
        compiler_params=_params(("parallel", "parallel")),
    )(c.reshape(1).astype(jnp.int32), g.reshape(4, 2, r, cw), r1)


def _adamw_math(w, g, m, v):
    m = B1 * m + (1.0 - B1) * g
    v = B2 * v + (1.0 - B2) * (g * g)
    m_hat = m / (1.0 - B1 ** STEP)
    v_hat = v / (1.0 - B2 ** STEP)
    return -LR * (m_hat / (jnp.sqrt(v_hat) + AEPS) + WD * w), m, v


def _sum_adamw(parts, w, m, v, name):
    r, c = w.shape
    tr = _pick(r, (256, 176, 128))
    tc = _pick(c, (256, 128)) if tr == r and r > 256 else c

    def body(p0, p1, p2, p3, w_ref, m_ref, v_ref, g_out, d_out, m_out, v_out):
        g = ((p0[...].astype(F32) + p1[...].astype(F32)) + p2[...].astype(F32)) + p3[...].astype(F32)
        g_out[...] = g
        d_out[...], m_out[...], v_out[...] = _adamw_math(w_ref[...], g, m_ref[...], v_ref[...])

    part = lambda q: _BS((None, tr, tc), functools.partial(lambda i, j, q: (q, i, j), q=q))
    plain = _BS((tr, tc), lambda i, j: (i, j))
    return pl.pallas_call(
        body, name=name, grid=(r // tr, c // tc), in_specs=[part(q) for q in range(4)] + [plain] * 3,
        out_specs=[plain] * 4, out_shape=[jax.ShapeDtypeStruct((r, c), F32)] * 4,
        compiler_params=_params(("parallel", "parallel")),
    )(parts, parts, parts, parts, w, m, v)


def _small_update(gathered, w, m, v):
    r = w.shape[0]

    def body(ga_ref, w_ref, m_ref, v_ref, g_out, d_out, m_out, v_out):
        g = ga_ref[0]
        for dev in range(1, N_DEV):
            g = g + ga_ref[dev]
        g_out[...] = g
        d_out[...], m_out[...], v_out[...] = _adamw_math(w_ref[...], g, m_ref[...], v_ref[...])

    return pl.pallas_call(
        body, name="small_update", out_shape=[jax.ShapeDtypeStruct((r, LANE), F32)] * 4,
    )(gathered, w, m, v)


def _pack_small(vals):
    rows = []
    for name, (r, n) in SMALL:
        flat = vals[name].reshape(-1)
        pad = (-flat.shape[0]) % (SUBLANE * LANE)
        rows.append(jnp.pad(flat, (0, pad)).reshape(-1, LANE))
    return jnp.concatenate(rows, axis=0)


def _unpack_small(packed):
    out, off = {}, 0
    for name, (r, n) in SMALL:
        nrow = -(-(r * n) // (SUBLANE * LANE)) * SUBLANE
        out[name] = packed[off:off + nrow].reshape(-1)[:r * n].reshape(r, n)
        off += nrow
    return out


def kernel(x, positions, ffn1_norm, ffn1_w_in, ffn1_w_out, mix_norm, w_in, hg_lb_table, hg_out_norm, w_hg_branch, mla_q_lora_norm, w_q_up, mla_kv_lora_norm, w_kv_up, q_head_norm, k_head_norm, w_mla_branch, w_merge, b_merge, w_out, ffn2_norm, ffn2_w_in, ffn2_w_out, final_norm, loss_target, m_ffn1_norm, m_ffn1_w_in, m_ffn1_w_out, m_mix_norm, m_w_in, m_hg_lb_table, m_hg_out_norm, m_w_hg_branch, m_mla_q_lora_norm, m_w_q_up, m_mla_kv_lora_norm, m_w_kv_up, m_q_head_norm, m_k_head_norm, m_w_mla_branch, m_w_merge, m_b_merge, m_w_out, m_ffn2_norm, m_ffn2_w_in, m_ffn2_w_out, m_final_norm, v_ffn1_norm, v_ffn1_w_in, v_ffn1_w_out, v_mix_norm, v_w_in, v_hg_lb_table, v_hg_out_norm, v_w_hg_branch, v_mla_q_lora_norm, v_w_q_up, v_mla_kv_lora_norm, v_w_kv_up, v_q_head_norm, v_k_head_norm, v_w_mla_branch, v_w_merge, v_b_merge, v_w_out, v_ffn2_norm, v_ffn2_w_in, v_ffn2_w_out, v_final_norm):
    args = dict(locals())
    t = x.shape[1]
    big_w = {n: args[n][0] for n, _, _ in BIG}
    small = {n: args[n].reshape(shape) for n, shape in SMALL}

    names = [n for n, _, _ in BIG]
    first, rest = names[:2], names[2:]
    full = dict(zip(first, _all_gather([big_w[n].astype(BF16) for n in first], "weights_all_gather_ffn1")))
    g_send, g_recv, g_thru, g_bufs, g_token = _gather_start([big_w[n].astype(BF16) for n in rest], full[first[0]],
                                                            "weights_gather_start")
    gains = dict(small, ffn1_norm=small["ffn1_norm"] + g_token[0:1, 0:1])

    def late_weights(after):
        blocks, bufs = _gather_wait(g_send, g_recv, g_thru, g_bufs, after, "weights_gather_wait")
        return dict(zip(rest, _gather_finish(blocks, bufs, "weights_gather_finish")))

    inv_freq = ROPE_THETA ** (-jnp.arange(0, ROPE, 2, dtype=F32) / ROPE)
    ang = positions[0].astype(F32)[:, None] * inv_freq
    cs = jnp.concatenate([jnp.cos(ang), jnp.cos(ang)], axis=1)
    sn = jnp.concatenate([jnp.sin(ang), jnp.sin(ang)], axis=1)

    c = lax.axis_index("c")
    chip = 2 * lax.axis_index("x") + lax.axis_index("y")
    early = {}

    def chip_sums_of(g, ns, tag):
        from_sibling = _sibling_swap([g[n] for n in ns], "grads_sibling_swap_" + tag)
        return [_chip_sum(g[n], r1, c, "chip_sum_" + n) for n, r1 in zip(ns, from_sibling)]

    def early_grads(g):
        early["names"] = [n for n in names if n in g]
        early["sums"] = chip_sums_of(g, early["names"], "early")
        early["send"], early["recv"], early["thru"], early["lands"], token = _chip_exchange_start(
            early["sums"], "grads_exchange_start")
        return token

    def last_grads(g):
        early["late"] = [n for n in names if n not in early["names"]]
        early["l_send"], early["l_recv"], early["l_thru"], early["l_lands"], token = _chip_exchange_start(
            chip_sums_of(g, early["late"], "late"), "grads_exchange_late_start")
        return token

    loss_row, grad_x, g = _local_step(x[0], loss_target[0], cs, sn, full, late_weights, gains, early_grads,
                                      last_grads)
    late, l_send, l_recv = early["late"], early["l_send"], early["l_recv"]
    l_thru, l_lands = early["l_thru"], early["l_lands"]
    sent, landed = _chip_exchange_wait(early["send"], early["recv"], early["thru"], early["lands"], grad_x,
                                       "grads_exchange_wait")
    exchanged = {n: _put_own(land, lax.dynamic_index_in_dim(own, chip, 0), chip)
                 for n, land, own in zip(early["names"], landed, sent)}

    small_packed = jnp.concatenate([_pack_small(g), jnp.pad(loss_row, ((0, SUBLANE - 1), (0, 0)))], axis=0)
    small_all = _all_gather([small_packed], "small_all_gather")[0]
    zero_tail = jnp.zeros((SUBLANE, LANE), F32)
    pk = lambda d: jnp.concatenate([_pack_small(d), zero_tail], axis=0)
    sg, sd, sm, sv = _small_update(
        small_all, pk(small), pk({n: args["m_" + n].reshape(shape) for n, shape in SMALL}),
        pk({n: args["v_" + n].reshape(shape) for n, shape in SMALL}))
    n_small_rows = _pack_small(small).shape[0]
    loss = sg[n_small_rows, 0]
    outs = {k_: _unpack_small(a) for k_, a in (("grad", sg), ("delta", sd), ("new_m", sm), ("new_v", sv))}

    def adamw(n):
        tr = (lambda a: a.T) if n in GRAD_T else (lambda a: a)
        res = _sum_adamw(exchanged[n], tr(big_w[n]), tr(args["m_" + n][0]), tr(args["v_" + n][0]), "adamw_" + n)
        outs["grad"][n], outs["delta"][n], outs["new_m"][n], outs["new_v"][n] = [tr(a) for a in res]

    for n in early["names"]:
        adamw(n)
    sent, landed = _chip_exchange_wait(l_send, l_recv, l_thru, l_lands, outs["delta"][early["names"][-1]],
                                       "grads_exchange_late_wait")
    for n, land, own in zip(late, landed, sent):
        exchanged[n] = _put_own(land, lax.dynamic_index_in_dim(own, chip, 0), chip)
        adamw(n)

    def shaped(kind, n):
        return outs[kind][n].reshape(args[n].shape)

    return (loss, grad_x[None], *[shaped("grad", n) for n in WEIGHT_ORDER], *[shaped("delta", n) for n in WEIGHT_ORDER],
            *[shaped("new_m", n) for n in WEIGHT_ORDER], *[shaped("new_v", n) for n in WEIGHT_ORDER])
```

```python
import functools

import jax
import jax.numpy as jnp
from jax import lax
from jax.experimental import pallas as pl
from jax.experimental.pallas import tpu as pltpu

F32 = jnp.float32
BF16 = jnp.bfloat16

D = 1024
FF = 2816
NH = 8
HD = 128
ROPE = 64
QK = HD + ROPE
QL = 384
KVL = 256
HGW = NH * HD
CHUNK = 64
EPS = 1e-6
ROPE_THETA = 10000.0
SCALE = QK ** -0.5
LOG2E = 1.4426950408889634

LR, B1, B2, AEPS, WD, STEP = 0.001, 0.9, 0.999, 1e-08, 0.01, 10

HB = 128
SUB = 16
EXP_CLAMP = 80.0
ATT_TILES = (1024, 512, 256, 128)
ATT_Q = (1024, 512, 256, 128)
ATT_KEY_TILES = 1
GRAD_DT = BF16
ROW_TILES = (1024, 512, 256, 128)

LANE = 128
SUBLANE = 8
VMEM_LIMIT = 56 << 20

N_DEV = 8
MESH = pl.DeviceIdType.MESH

BIG = (
    ("ffn1_w_in", (D, 2 * FF), 1), ("ffn1_w_out", (FF, D), 0), ("w_in", (D, 4800), 1),
    ("w_hg_branch", (HGW, D), 0), ("w_q_up", (QL, NH * QK), 1), ("w_kv_up", (KVL, NH * 2 * HD), 1),
    ("w_mla_branch", (NH * HD, D), 0), ("w_merge", (D, 2 * D), 1), ("w_out", (D, D), 0),
    ("ffn2_w_in", (D, 2 * FF), 1), ("ffn2_w_out", (FF, D), 0),
)
SMALL = (
    ("ffn1_norm", (1, D)), ("mix_norm", (1, D)), ("hg_lb_table", (2, HGW)), ("hg_out_norm", (1, HD)),
    ("mla_q_lora_norm", (1, QL)), ("mla_kv_lora_norm", (1, KVL)), ("q_head_norm", (1, QK)),
    ("k_head_norm", (1, QK)), ("b_merge", (1, 2 * D)), ("ffn2_norm", (1, D)), ("final_norm", (1, D)),
)
GRAD_T = ("ffn1_w_in", "ffn2_w_in", "w_in", "w_q_up")
WEIGHT_ORDER = ("ffn1_norm", "ffn1_w_in", "ffn1_w_out", "mix_norm", "w_in", "hg_lb_table", "hg_out_norm",
                "w_hg_branch", "mla_q_lora_norm", "w_q_up", "mla_kv_lora_norm", "w_kv_up", "q_head_norm",
                "k_head_norm", "w_mla_branch", "w_merge", "b_merge", "w_out", "ffn2_norm", "ffn2_w_in",
                "ffn2_w_out", "final_norm")


def _pick(n, cands):
    for c in cands:
        if n % c == 0:
            return c
    return n


def _params(sem):
    return pltpu.CompilerParams(dimension_semantics=sem, vmem_limit_bytes=VMEM_LIMIT)


def _sig(x):
    return 0.5 * jnp.tanh(0.5 * x) + 0.5


def _dot(a, b):
    return jnp.dot(a.astype(BF16), b.astype(BF16), preferred_element_type=F32)


def _dot_nt(a, b):
    return lax.dot_general(a.astype(BF16), b.astype(BF16), (((1,), (1,)), ((), ())),
                           preferred_element_type=F32)


def _dot_tn(a, b):
    return lax.dot_general(a.astype(BF16), b.astype(BF16), (((0,), (0,)), ((), ())),
                           preferred_element_type=F32)


def _split3(x):
    x1 = x.astype(BF16)
    r1 = x - x1.astype(F32)
    x2 = r1.astype(BF16)
    x3 = (r1 - x2.astype(F32)).astype(BF16)
    return x1, x2, x3


def _dot_sel(m, x):
    x1, x2, x3 = _split3(x)
    d = lambda p: jnp.dot(m, p, preferred_element_type=F32)
    return d(x1) + d(x2) + d(x3)


def _sel_dot(x, m):
    x1, x2, x3 = _split3(x)
    d = lambda p: jnp.dot(p, m, preferred_element_type=F32)
    return d(x1) + d(x2) + d(x3)


_TN = (1408, 1024, 768, 512, 384, 256, 128)


def _accumulate(acc, o_ref, axis, steps, term):
    step = pl.program_id(axis)

    @pl.when(step == 0)
    def _():
        acc[...] = jnp.zeros_like(acc)

    val = term()
    if isinstance(val, (list, tuple)):
        for e, v in enumerate(val):
            acc[e] += v
    else:
        acc[...] += val

    @pl.when(step == steps - 1)
    def _():
        o_ref[...] = acc[...].astype(o_ref.dtype)


def _mm(a, b, mode, name, out_dtype=F32):
    if mode == "tn":
        t, m = a.shape
        n = b.shape[1]
        tt, tm, tn = _pick(t, ROW_TILES), _pick(m, _TN), _pick(n, _TN)

        def body(a_ref, b_ref, o_ref, acc):
            _accumulate(acc, o_ref, 2, t // tt, lambda: _dot_tn(a_ref[...], b_ref[...]))

        return pl.pallas_call(
            body, name=name, grid=(m // tm, n // tn, t // tt),
            in_specs=[pl.BlockSpec((tt, tm), lambda i, j, k: (k, i)),
                      pl.BlockSpec((tt, tn), lambda i, j, k: (k, j))],
            out_specs=pl.BlockSpec((tm, tn), lambda i, j, k: (i, j)),
            out_shape=jax.ShapeDtypeStruct((m, n), GRAD_DT), scratch_shapes=[pltpu.VMEM((tm, tn), F32)],
            compiler_params=_params(("parallel", "parallel", "arbitrary")),
        )(a, b)

    m, k = a.shape
    tm = _pick(m, ROW_TILES)
    if mode == "nn":
        n = b.shape[1]
        tn = _pick(n, _TN)
        b_spec = pl.BlockSpec((k, tn), lambda i, j: (0, j))
        dot = _dot
    else:
        n = b.shape[0]
        tn = _pick(n, _TN if k <= 4096 else (512, 256, 128))
        b_spec = pl.BlockSpec((tn, k), lambda i, j: (j, 0))
        dot = _dot_nt

    def body(a_ref, b_ref, o_ref):
        o_ref[...] = dot(a_ref[...], b_ref[...]).astype(o_ref.dtype)

    return pl.pallas_call(
        body, name=name, grid=(m // tm, n // tn),
        in_specs=[pl.BlockSpec((tm, k), lambda i, j: (i, 0)), b_spec],
        out_specs=pl.BlockSpec((tm, tn), lambda i, j: (i, j)),
        out_shape=jax.ShapeDtypeStruct((m, n), out_dtype),
        compiler_params=_params(("parallel", "parallel")),
    )(a, b)


_DOTS = {"nn": _dot, "nt": _dot_nt, "tn": _dot_tn}
_BS = pl.BlockSpec


class _AsRef:
    def __init__(self, value):
        self.value = value

    def __getitem__(self, idx):
        return self.value


def _mm_stack_red(a, w, name, kind, epilogue):
    s, t, n = a.shape
    nout = w.shape[2] if kind == "nn" else w.shape[1]
    fn, first, rows, vecs, outs, accs = epilogue[:6]
    rows = [first] + list(rows)
    tm = _pick(t, epilogue[6] if len(epilogue) > 6 else ROW_TILES)
    n_row, n_vec, n_out = len(rows), len(vecs), len(outs)
    dot = _DOTS[kind]

    def body(*refs):
        a_ref, w_ref, prod = refs[0], refs[1], refs[-1]
        row_refs = refs[2:2 + n_row]
        vec_refs = refs[2 + n_row:2 + n_row + n_vec]
        out_refs = refs[2 + n_row + n_vec:2 + n_row + n_vec + n_out]
        acc_refs = refs[2 + n_row + n_vec + n_out:-1]
        i, j = pl.program_id(0), pl.program_id(1)

        @pl.when(j == 0)
        def _():
            prod[...] = jnp.zeros_like(prod)

        prod[...] += dot(a_ref[...], w_ref[...])

        @pl.when(j == s - 1)
        def _():
            res = fn(row_refs[0], _AsRef(prod[...]), *row_refs[1:], *vec_refs, *out_refs)
            if acc_refs:
                @pl.when(i == 0)
                def _():
                    for r in acc_refs:
                        r[...] = jnp.zeros_like(r)

                for r, val in zip(acc_refs, res):
                    r[...] += val

    in_specs = [_BS((None, tm, n), lambda i, j: (j, i, 0)), _BS((None,) + w.shape[1:], lambda i, j: (j, 0, 0))]
    in_specs += [_BS((tm, r.shape[1]), lambda i, j: (i, 0)) for r in rows]
    in_specs += [_BS(v.shape, lambda i, j: (0, 0)) for v in vecs]
    out_specs = [_BS((tm, wd), lambda i, j: (i, 0)) for wd, _ in outs] + [_BS(sh, lambda i, j: (0, 0)) for sh in accs]
    out_shape = [jax.ShapeDtypeStruct((t, wd), dt) for wd, dt in outs] + [jax.ShapeDtypeStruct(sh, F32) for sh in accs]
    return pl.pallas_call(
        body, name=name, grid=(t // tm, s), in_specs=in_specs, out_specs=out_specs, out_shape=out_shape,
        scratch_shapes=[pltpu.VMEM((tm, nout), F32)],
        compiler_params=_params(("arbitrary" if accs else "parallel", "arbitrary")),
    )(a, w, *rows, *vecs)


def _mm_stack_tn(a, b, name):
    grp = 4
    s, t, k = a.shape
    n = b.shape[1]
    tt = _pick(t, ROW_TILES)

    def body(a_ref, b_ref, o_ref, acc):
        def terms():
            shared = b_ref[...]
            return [_dot_tn(a_ref[e], shared) for e in range(grp)]

        _accumulate(acc, o_ref, 1, t // tt, terms)

    return pl.pallas_call(
        body, name=name, grid=(s // grp, t // tt),
        in_specs=[_BS((None, grp, tt, k), lambda j, r: (j, 0, r, 0)), _BS((tt, n), lambda j, r: (r, 0))],
        out_specs=_BS((None, grp, k, n), lambda j, r: (j, 0, 0, 0)),
        out_shape=jax.ShapeDtypeStruct((s // grp, grp, k, n), GRAD_DT), scratch_shapes=[pltpu.VMEM((grp, k, n), F32)],
        compiler_params=_params(("parallel", "arbitrary")),
    )(a.reshape(s // grp, grp, t, k), b).reshape(s, k, n)


def _cols_fwd(x, w, name):
    t, k = x.shape
    s, _, n = w.shape
    tm = _pick(t, ROW_TILES)

    def body(x_ref, w_ref, o_ref):
        x_ = x_ref[...]
        for j in range(s):
            o_ref[:, n * j:n * (j + 1)] = _dot(x_, w_ref[j])

    return pl.pallas_call(
        body, name=name, grid=(t // tm,),
        in_specs=[_BS((tm, k), lambda i: (i, 0)), _BS((s, k, n), lambda i: (0, 0, 0))],
        out_specs=_BS((tm, s * n), lambda i: (i, 0)), out_shape=jax.ShapeDtypeStruct((t, s * n), F32),
        compiler_params=_params(("parallel",)),
    )(x, w)


def _cols_dx(d, w, name):
    t = d.shape[0]
    s, k, n = w.shape
    tm = _pick(t, ROW_TILES)

    def body(d_ref, w_ref, o_ref):
        acc = _dot_nt(d_ref[:, 0:n], w_ref[0])
        for j in range(1, s):
            acc = acc + _dot_nt(d_ref[:, n * j:n * (j + 1)], w_ref[j])
        o_ref[...] = acc

    return pl.pallas_call(
        body, name=name, grid=(t // tm,),
        in_specs=[_BS((tm, s * n), lambda i: (i, 0)), _BS((s, k, n), lambda i: (0, 0, 0))],
        out_specs=_BS((tm, k), lambda i: (i, 0)), out_shape=jax.ShapeDtypeStruct((t, k), F32),
        compiler_params=_params(("parallel",)),
    )(d, w)


def _cols_dw(x, d, n, name):
    t, k = x.shape
    s = d.shape[1] // n
    tt = _pick(t, ROW_TILES)

    def body(x_ref, d_ref, o_ref, acc):
        def terms():
            x_ = x_ref[...]
            return [_dot_tn(x_, d_ref[:, n * j:n * (j + 1)]) for j in range(s)]

        _accumulate(acc, o_ref, 0, t // tt, terms)

    return pl.pallas_call(
        body, name=name, grid=(t // tt,),
        in_specs=[_BS((tt, k), lambda r: (r, 0)), _BS((tt, s * n), lambda r: (r, 0))],
        out_specs=_BS((s, k, n), lambda r: (0, 0, 0)), out_shape=jax.ShapeDtypeStruct((s, k, n), GRAD_DT),
        scratch_shapes=[pltpu.VMEM((s, k, n), F32)], compiler_params=_params(("arbitrary",)),
    )(x, d)


def _rows(fn, name, t, tm, ins, vecs, outs, accs=()):
    n_in, n_out, n_acc = len(ins) + len(vecs), len(outs), len(accs)

    def body(*refs):
        res = fn(*refs[:n_in + n_out])
        if n_acc:
            acc_refs = refs[n_in + n_out:]

            @pl.when(pl.program_id(0) == 0)
            def _():
                for r in acc_refs:
                    r[...] = jnp.zeros_like(r)

            for r, val in zip(acc_refs, res):
                r[...] += val

    in_specs = [pl.BlockSpec((tm, bw), functools.partial(lambda i, cb: (i, cb), cb=cb)) for _, bw, cb in ins]
    in_specs += [pl.BlockSpec(v.shape, lambda i: (0, 0)) for v in vecs]
    out_specs = [pl.BlockSpec((tm, w), lambda i: (i, 0)) for w, _ in outs]
    out_specs += [pl.BlockSpec(s, lambda i: (0, 0)) for s in accs]
    out_shape = [jax.ShapeDtypeStruct((t, w), dt) for w, dt in outs]
    out_shape += [jax.ShapeDtypeStruct(s, F32) for s in accs]
    return pl.pallas_call(
        body, name=name, grid=(t // tm,), in_specs=in_specs, out_specs=out_specs, out_shape=out_shape,
        compiler_params=_params(("arbitrary",) if n_acc else ("parallel",)),
    )(*[a for a, _, _ in ins], *vecs)


def _rms(x, g):
    return x * lax.rsqrt(jnp.mean(x * x, axis=-1, keepdims=True) + EPS) * g


def _rms_bwd(x, g, dy):
    xh = x * lax.rsqrt(jnp.mean(x * x, axis=-1, keepdims=True) + EPS)
    r = lax.rsqrt(jnp.mean(x * x, axis=-1, keepdims=True) + EPS)
    dyg = dy * g
    dx = r * (dyg - xh * jnp.mean(dyg * xh, axis=-1, keepdims=True))
    return dx, jnp.sum(dy * xh, axis=0, keepdims=True)


def _hgrn_mats():
    row = lax.broadcasted_iota(jnp.int32, (HB, HB), 0)
    col = lax.broadcasted_iota(jnp.int32, (HB, HB), 1)
    return row, col


def _hgrn_gates(qr, z, t0, t1):
    lb = 1.0 / (1.0 + jnp.exp(t1 - t0))
    th = jnp.tanh(0.5 * z)
    sz, sneg = 0.5 + 0.5 * th, 0.5 - 0.5 * th
    f = lb + (1.0 - lb) * sz
    return lb, sz, sneg, f, jnp.log(f), (1.0 - lb) * sneg, qr * _sig(qr)


def _sub_decay(x, mid, cum, i):
    n = SUB * (i + 1)
    seen = x[:n] * jnp.exp(jnp.minimum(mid - cum[:n], EXP_CLAMP))
    return seen if n == HB else jnp.concatenate([seen, jnp.zeros((HB - n, x.shape[1]), F32)], axis=0)


def _hgrn_scores(q, k, cum):
    heads, nsub = range(len(q)), HB // SUB
    mids = [[cum[h][SUB * i + SUB // 2 - 1:SUB * i + SUB // 2, :] for i in range(nsub)] for h in heads]
    cmid = [jnp.concatenate([cum[h][SUB * i:SUB * (i + 1)] - mids[h][i] for i in range(nsub)], axis=0) for h in heads]
    qd = [q[h] * jnp.exp(jnp.minimum(cmid[h], EXP_CLAMP)) for h in heads]
    qd_b = [x.astype(BF16) for x in qd]
    kds = [[None] * nsub for _ in heads]
    parts = [[None] * nsub for _ in heads]
    for i in range(nsub):
        for h in heads:
            kds[h][i] = _sub_decay(k[h], mids[h][i], cum[h], i)
        for h in heads:
            parts[h][i] = _dot_nt(qd_b[h][SUB * i:SUB * (i + 1)], kds[h][i])
    return qd, kds, [jnp.concatenate(parts[h], axis=0) for h in heads], cmid


def _hgrn_fwd(p_hg, table, gain, t):
    nblk = t // HB

    def body(q_ref, f_ref, i_ref, g_ref, tab_ref, gain_ref, o_ref, y_ref, st_ref, state):
        @pl.when(pl.program_id(0) == 0)
        def _():
            state[...] = jnp.zeros_like(state)

        row, col = _hgrn_mats()
        causal = col <= row
        tri = causal.astype(BF16)
        heads = range(NH)
        sl = [slice(HD * h, HD * (h + 1)) for h in heads]
        gates = [_hgrn_gates(q_ref[:, sl[h]], f_ref[:, sl[h]], tab_ref[0:1, sl[h]], tab_ref[1:2, sl[h]]) for h in heads]
        lf, k, q = [g[4] for g in gates], [g[5] for g in gates], [g[6] for g in gates]
        v = [i_ref[:, sl[h]] for h in heads]
        cum = [_dot_sel(tri, lf[h]) for h in heads]
        _, _, s, _ = _hgrn_scores(q, k, cum)
        p = [jnp.where(causal, s[h], 0.0) for h in heads]
        st = [state[h] for h in heads]
        o = [_dot(p[h], v[h]) + _dot_nt(q[h] * jnp.exp(cum[h]), st[h]) for h in heads]
        last = [cum[h][HB - 1:HB, :] for h in heads]
        new_st = [st[h] * jnp.exp(last[h]) + _dot_tn(v[h], k[h] * jnp.exp(last[h] - cum[h])) for h in heads]
        for h in heads:
            gr = g_ref[:, sl[h]]
            st_ref[h, 0] = st[h]
            state[h] = new_st[h]
            o_ref[:, sl[h]] = o[h]
            y_ref[:, sl[h]] = (_rms(o[h], gain_ref[...]) * gr * _sig(gr)).astype(BF16)

    blk = lambda cb: pl.BlockSpec((HB, HGW), functools.partial(lambda n, cb: (n, cb), cb=cb))
    return pl.pallas_call(
        body, name="hgrn_fwd", grid=(nblk,),
        in_specs=[blk(0), blk(1), blk(2), blk(3), pl.BlockSpec((2, HGW), lambda n: (0, 0)),
                  pl.BlockSpec((1, HD), lambda n: (0, 0))],
        out_specs=[pl.BlockSpec((HB, HGW), lambda n: (n, 0)), pl.BlockSpec((HB, HGW), lambda n: (n, 0)),
                   pl.BlockSpec((NH, 1, HD, HD), lambda n: (0, n, 0, 0))],
        out_shape=[jax.ShapeDtypeStruct((t, HGW), F32), jax.ShapeDtypeStruct((t, HGW), BF16),
                   jax.ShapeDtypeStruct((NH, nblk, HD, HD), F32)],
        scratch_shapes=[pltpu.VMEM((NH, HD, HD), F32)],
        compiler_params=_params(("arbitrary",)),
    )(p_hg, p_hg, p_hg, p_hg, table, gain)


def _hgrn_bwd(p_hg, table, gain, o_pre, states, dy, t):
    nblk = t // HB

    def body(q_ref, f_ref, i_ref, g_ref, tab_ref, gain_ref, o_ref, st_ref, dy_ref, dp_ref, dtab_ref, dgain_ref,
             dstate):
        @pl.when(pl.program_id(0) == 0)
        def _():
            dstate[...] = jnp.zeros_like(dstate)
            dtab_ref[...] = jnp.zeros_like(dtab_ref)
            dgain_ref[...] = jnp.zeros_like(dgain_ref)

        row, col = _hgrn_mats()
        causal = col <= row
        tri = causal.astype(BF16)
        tri_t = (row <= col).astype(BF16)
        heads, nsub = range(NH), HB // SUB
        sl = [slice(HD * h, HD * (h + 1)) for h in heads]
        rows_of = lambda i: slice(SUB * i, SUB * (i + 1))
        gain_ = gain_ref[...]
        qr, z = [q_ref[:, sl[h]] for h in heads], [f_ref[:, sl[h]] for h in heads]
        v, gr = [i_ref[:, sl[h]] for h in heads], [g_ref[:, sl[h]] for h in heads]
        gates = [_hgrn_gates(qr[h], z[h], tab_ref[0:1, sl[h]], tab_ref[1:2, sl[h]]) for h in heads]
        lb, sz, sneg, f, lf, k, q = ([g[j] for g in gates] for j in range(7))
        cum = [_dot_sel(tri, lf[h]) for h in heads]
        qd, kds, s, cmid = _hgrn_scores(q, k, cum)
        p = [jnp.where(causal, s[h], 0.0) for h in heads]
        st, dst = [st_ref[h, 0] for h in heads], [dstate[h] for h in heads]
        o, dyh, sg = [o_ref[:, sl[h]] for h in heads], [dy_ref[:, sl[h]] for h in heads], [_sig(x) for x in gr]
        dgr = [dyh[h] * _rms(o[h], gain_) * sg[h] * (1.0 + gr[h] * (1.0 - sg[h])) for h in heads]
        norm_bwd = [_rms_bwd(o[h], gain_, dyh[h] * gr[h] * sg[h]) for h in heads]
        do = [x[0] for x in norm_bwd]
        do_b = [x.astype(BF16) for x in do]
        ecum, last = [jnp.exp(x) for x in cum], [x[HB - 1:HB, :] for x in cum]
        qc = [q[h] * ecum[h] for h in heads]
        edec = [jnp.exp(last[h] - cum[h]) for h in heads]
        kdec = [k[h] * edec[h] for h in heads]
        dp = [jnp.where(causal, _dot_nt(do_b[h], v[h]), 0.0) for h in heads]
        dv = [_dot(p[h].T, do_b[h]) + _dot_nt(kdec[h], dst[h]) for h in heads]
        dqc = [_dot(do_b[h], st[h]) for h in heads]
        dkdec = [_dot(v[h], dst[h]) for h in heads]
        new_dst = [dst[h] * jnp.exp(last[h]) + _dot(do[h].T, qc[h]) for h in heads]
        dp_b = [x.astype(BF16) for x in dp]
        dqd = [jnp.concatenate(parts, axis=0) for parts in
               zip(*[[_dot(dp_b[h][rows_of(i)], kds[h][i]) for h in heads] for i in range(nsub)])]
        gq = [dqd[h] * qd[h] for h in heads]
        dq = [dqd[h] * jnp.exp(jnp.minimum(cmid[h], EXP_CLAMP)) + dqc[h] * ecum[h] for h in heads]
        gs = [dkdec[h] * kdec[h] for h in heads]
        dk = [dkdec[h] * edec[h] for h in heads]
        dcum = [dqc[h] * qc[h] - gs[h] + gq[h]
                + jnp.where(row == HB - 1, jnp.sum(gs[h], axis=0, keepdims=True)
                            + jnp.exp(last[h]) * jnp.sum(st[h] * dst[h], axis=0, keepdims=True), 0.0) for h in heads]
        qd_b = [x.astype(BF16) for x in qd]
        for i in range(nsub):
            dkd = [_dot_tn(dp_b[h][rows_of(i)], qd_b[h][rows_of(i)]) for h in heads]
            for h in heads:
                mid = cum[h][SUB * i + SUB // 2 - 1:SUB * i + SUB // 2, :]
                dk[h] = dk[h] + _sub_decay(dkd[h], mid, cum[h], i)
                gk = dkd[h] * kds[h][i]
                to_mid = jnp.sum(gk, axis=0, keepdims=True) - jnp.sum(gq[h][rows_of(i)], axis=0, keepdims=True)
                dcum[h] = dcum[h] - gk + jnp.where(row == SUB * i + SUB // 2 - 1, to_mid, 0.0)
        dlf = [_dot_sel(tri_t, dcum[h]) for h in heads]
        dgain = jnp.zeros((1, HD), F32)
        for h in heads:
            df = dlf[h] / f[h] - dk[h]
            dlb = jnp.sum(df * sneg[h], axis=0, keepdims=True) * lb[h] * (1.0 - lb[h])
            dtab_ref[0:1, sl[h]] += dlb
            dtab_ref[1:2, sl[h]] -= dlb
            sq = _sig(qr[h])
            dstate[h] = new_dst[h]
            dp_ref[:, sl[h]] = (dq[h] * sq * (1.0 + qr[h] * (1.0 - sq))).astype(BF16)
            dp_ref[:, HGW + HD * h:HGW + HD * (h + 1)] = (df * (1.0 - lb[h]) * sz[h] * sneg[h]).astype(BF16)
            dp_ref[:, 2 * HGW + HD * h:2 * HGW + HD * (h + 1)] = dv[h].astype(BF16)
            dp_ref[:, 3 * HGW + HD * h:3 * HGW + HD * (h + 1)] = dgr[h].astype(BF16)
            dgain = dgain + norm_bwd[h][1]
        dgain_ref[...] += dgain

    rev = lambda cb: pl.BlockSpec((HB, HGW), functools.partial(lambda n, cb: (nblk - 1 - n, cb), cb=cb))
    return pl.pallas_call(
        body, name="hgrn_bwd", grid=(nblk,),
        in_specs=[rev(0), rev(1), rev(2), rev(3), pl.BlockSpec((2, HGW), lambda n: (0, 0)),
                  pl.BlockSpec((1, HD), lambda n: (0, 0)), rev(0),
                  pl.BlockSpec((NH, 1, HD, HD), lambda n: (0, nblk - 1 - n, 0, 0)), rev(0)],
        out_specs=[pl.BlockSpec((HB, 4 * HGW), lambda n: (nblk - 1 - n, 0)),
                   pl.BlockSpec((2, HGW), lambda n: (0, 0)), pl.BlockSpec((1, HD), lambda n: (0, 0))],
        out_shape=[jax.ShapeDtypeStruct((t, 4 * HGW), BF16), jax.ShapeDtypeStruct((2, HGW), F32),
                   jax.ShapeDtypeStruct((1, HD), F32)],
        scratch_shapes=[pltpu.VMEM((NH, HD, HD), F32)],
        compiler_params=_params(("arbitrary",)),
    )(p_hg, p_hg, p_hg, p_hg, table, gain, o_pre, states, dy)


def _rope_mat():
    r = lax.broadcasted_iota(jnp.int32, (ROPE, ROPE), 0)
    c = lax.broadcasted_iota(jnp.int32, (ROPE, ROPE), 1)
    half = ROPE // 2
    return ((r == c - half).astype(F32) - (r == c + half).astype(F32)).astype(BF16)


def _mla_prep_fwd(p_mla, cs, sn, wq, wkv, gql, gkvl, gq, gk, t):
    tm = _pick(t, (512, 256, 128))

    def body(p_ref, cs_ref, sn_ref, wq_ref, wkv_ref, gql_ref, gkvl_ref, gq_ref, gk_ref,
             q_ref, k_ref, v_ref):
        rmat = _rope_mat()
        cqn = _rms(p_ref[:, 0:QL], gql_ref[...]).astype(BF16)
        ckvn = _rms(p_ref[:, QL:QL + KVL], gkvl_ref[...]).astype(BF16)
        kpe = p_ref[:, QL + KVL:QL + KVL + ROPE]
        c, s = cs_ref[...], sn_ref[...]
        rot = lambda x: x * c + _sel_dot(x, rmat) * s
        heads = range(NH)
        ssq = lambda x: jnp.sum(x * x, -1, keepdims=True)
        qa = [_dot(cqn, wq_ref[h, :, 0:HD]) for h in heads]
        qr = [_dot(cqn, wq_ref[h, :, HD:QK]) for h in heads]
        kn = [_dot(ckvn, wkv_ref[h, :, 0:HD]) for h in heads]
        vv = [_dot(ckvn, wkv_ref[h, :, HD:2 * HD]) for h in heads]
        kpe_ss = ssq(kpe)
        rq = [lax.rsqrt((ssq(qa[h]) + ssq(qr[h])) / QK + EPS) for h in heads]
        rk = [lax.rsqrt((ssq(kn[h]) + kpe_ss) / QK + EPS) for h in heads]
        q_rope = [rot(qr[h] * rq[h] * gq_ref[:, HD:QK]) for h in heads]
        k_rope = [rot(kpe * rk[h] * gk_ref[:, HD:QK]) for h in heads]
        for h in heads:
            q_ref[h, :, 0:HD] = (qa[h] * rq[h] * gq_ref[:, 0:HD] * (SCALE * LOG2E)).astype(BF16)
            q_ref[h, :, HD:QK] = (q_rope[h] * (SCALE * LOG2E)).astype(BF16)
            k_ref[h, :, 0:HD] = (kn[h] * rk[h] * gk_ref[:, 0:HD]).astype(BF16)
            k_ref[h, :, HD:QK] = k_rope[h].astype(BF16)
            v_ref[h] = vv[h].astype(BF16)

    whole = lambda a: pl.BlockSpec(a.shape, functools.partial(lambda i, nd: (0,) * nd, nd=a.ndim))
    return pl.pallas_call(
        body, name="mla_prep_fwd", grid=(t // tm,),
        in_specs=[pl.BlockSpec((tm, QL + KVL + ROPE), lambda i: (i, 0)), pl.BlockSpec((tm, ROPE), lambda i: (i, 0)),
                  pl.BlockSpec((tm, ROPE), lambda i: (i, 0))] + [whole(a) for a in (wq, wkv, gql, gkvl, gq, gk)],
        out_specs=[pl.BlockSpec((NH, tm, QK), lambda i: (0, i, 0)), pl.BlockSpec((NH, tm, QK), lambda i: (0, i, 0)),
                   pl.BlockSpec((NH, tm, HD), lambda i: (0, i, 0))],
        out_shape=[jax.ShapeDtypeStruct((NH, t, QK), BF16), jax.ShapeDtypeStruct((NH, t, QK), BF16),
                   jax.ShapeDtypeStruct((NH, t, HD), BF16)],
        compiler_params=_params(("parallel",)),
    )(p_mla, cs, sn, wq, wkv, gql, gkvl, gq, gk)


def _mla_prep_bwd(p_mla, cs, sn, wq, wkv, gql, gkvl, gq, gk, dq, dk, dv, t):
    tm = _pick(t, (512, 256, 128))

    def body(p_ref, cs_ref, sn_ref, wq_ref, wkv_ref, gql_ref, gkvl_ref, gq_ref, gk_ref,
             dq_ref, dk_ref, dv_ref,
             dp_ref, dwq_ref, dwkv_ref, dgql_ref, dgkvl_ref, dgq_ref, dgk_ref):
        accs = (dwq_ref, dwkv_ref, dgql_ref, dgkvl_ref, dgq_ref, dgk_ref)

        @pl.when(pl.program_id(0) == 0)
        def _():
            for r in accs:
                r[...] = jnp.zeros_like(r)

        rmat = _rope_mat()
        rmat_t = -rmat
        cq, ckv = p_ref[:, 0:QL], p_ref[:, QL:QL + KVL]
        kpe = p_ref[:, QL + KVL:QL + KVL + ROPE]
        cqn_f, ckvn_f = _rms(cq, gql_ref[...]), _rms(ckv, gkvl_ref[...])
        cqn, ckvn = cqn_f.astype(BF16), ckvn_f.astype(BF16)
        ckvn_t = ckvn_f.T.astype(BF16)
        c, s = cs_ref[...], sn_ref[...]
        unrot = lambda dy: dy * c + _sel_dot(dy * s, rmat_t)
        heads = range(NH)
        rsum = lambda x: jnp.sum(x, -1, keepdims=True)
        csum = lambda x: jnp.sum(x, axis=0, keepdims=True)
        qa = [_dot(cqn, wq_ref[h, :, 0:HD]) for h in heads]
        qr = [_dot(cqn, wq_ref[h, :, HD:QK]) for h in heads]
        kn = [_dot(ckvn, wkv_ref[h, :, 0:HD]) for h in heads]
        dyr = [unrot(dq_ref[h, :, HD:QK] * SCALE) for h in heads]
        dkr = [unrot(dk_ref[h, :, HD:QK]) for h in heads]
        rq = [lax.rsqrt((rsum(qa[h] * qa[h]) + rsum(qr[h] * qr[h])) / QK + EPS) for h in heads]
        xa, xr = [qa[h] * rq[h] for h in heads], [qr[h] * rq[h] for h in heads]
        dya = [dq_ref[h, :, 0:HD] * SCALE for h in heads]
        ga, gr_ = [dya[h] * gq_ref[:, 0:HD] for h in heads], [dyr[h] * gq_ref[:, HD:QK] for h in heads]
        mean = [(rsum(ga[h] * xa[h]) + rsum(gr_[h] * xr[h])) / QK for h in heads]
        dqa = [(rq[h] * (ga[h] - xa[h] * mean[h])).astype(BF16) for h in heads]
        dqr = [(rq[h] * (gr_[h] - xr[h] * mean[h])).astype(BF16) for h in heads]
        kpe_ss = rsum(kpe * kpe)
        rk = [lax.rsqrt((rsum(kn[h] * kn[h]) + kpe_ss) / QK + EPS) for h in heads]
        ya, yr = [kn[h] * rk[h] for h in heads], [kpe * rk[h] for h in heads]
        dka = [dk_ref[h, :, 0:HD] for h in heads]
        ha, hr = [dka[h] * gk_ref[:, 0:HD] for h in heads], [dkr[h] * gk_ref[:, HD:QK] for h in heads]
        mean_k = [(rsum(ha[h] * ya[h]) + rsum(hr[h] * yr[h])) / QK for h in heads]
        dkn = [(rk[h] * (ha[h] - ya[h] * mean_k[h])).astype(BF16) for h in heads]
        dvh = [dv_ref[h].astype(BF16) for h in heads]
        dw = [(_dot_tn(dqa[h], cqn), _dot_tn(dqr[h], cqn), _dot(ckvn_t, dkn[h]), _dot(ckvn_t, dvh[h])) for h in heads]
        back_q = [_dot_nt(dqa[h], wq_ref[h, :, 0:HD]) + _dot_nt(dqr[h], wq_ref[h, :, HD:QK]) for h in heads]
        back_kv = [_dot_nt(dkn[h], wkv_ref[h, :, 0:HD]) + _dot_nt(dvh[h], wkv_ref[h, :, HD:2 * HD]) for h in heads]
        dcqn, dckvn = sum(back_q[1:], back_q[0]), sum(back_kv[1:], back_kv[0])
        dkpe = sum([rk[h] * (hr[h] - yr[h] * mean_k[h]) for h in heads][1:], rk[0] * (hr[0] - yr[0] * mean_k[0]))
        dgq_a = sum([csum(dya[h] * xa[h]) for h in heads][1:], csum(dya[0] * xa[0]))
        dgq_r = sum([csum(dyr[h] * xr[h]) for h in heads][1:], csum(dyr[0] * xr[0]))
        dgk_a = sum([csum(dka[h] * ya[h]) for h in heads][1:], csum(dka[0] * ya[0]))
        dgk_r = sum([csum(dkr[h] * yr[h]) for h in heads][1:], csum(dkr[0] * yr[0]))
        for h in heads:
            dwq_ref[h, 0:HD, :] += dw[h][0]
            dwq_ref[h, HD:QK, :] += dw[h][1]
            dwkv_ref[h, :, 0:HD] += dw[h][2]
            dwkv_ref[h, :, HD:2 * HD] += dw[h][3]
        dcq, dg1 = _rms_bwd(cq, gql_ref[...], dcqn)
        dckv, dg2 = _rms_bwd(ckv, gkvl_ref[...], dckvn)
        dp_ref[:, 0:QL] = dcq.astype(BF16)
        dp_ref[:, QL:QL + KVL] = dckv.astype(BF16)
        dp_ref[:, QL + KVL:QL + KVL + ROPE] = dkpe.astype(BF16)
        dgql_ref[...] += dg1
        dgkvl_ref[...] += dg2
        dgq_ref[:, 0:HD] += dgq_a
        dgq_ref[:, HD:QK] += dgq_r
        dgk_ref[:, 0:HD] += dgk_a
        dgk_ref[:, HD:QK] += dgk_r

    whole = lambda a: pl.BlockSpec(a.shape, functools.partial(lambda i, nd: (0,) * nd, nd=a.ndim))
    acc_shapes = [(NH, QK, QL), wkv.shape, gql.shape, gkvl.shape, gq.shape, gk.shape]
    return pl.pallas_call(
        body, name="mla_prep_bwd", grid=(t // tm,),
        in_specs=[pl.BlockSpec((tm, QL + KVL + ROPE), lambda i: (i, 0)), pl.BlockSpec((tm, ROPE), lambda i: (i, 0)),
                  pl.BlockSpec((tm, ROPE), lambda i: (i, 0))]
        + [whole(a) for a in (wq, wkv, gql, gkvl, gq, gk)]
        + [pl.BlockSpec((NH, tm, QK), lambda i: (0, i, 0)), pl.BlockSpec((NH, tm, QK), lambda i: (0, i, 0)),
           pl.BlockSpec((NH, tm, HD), lambda i: (0, i, 0))],
        out_specs=[pl.BlockSpec((tm, QL + KVL + ROPE), lambda i: (i, 0))]
        + [pl.BlockSpec(s, functools.partial(lambda i, nd: (0,) * nd, nd=len(s))) for s in acc_shapes],
        out_shape=[jax.ShapeDtypeStruct((t, QL + KVL + ROPE), BF16)]
        + [jax.ShapeDtypeStruct(s, F32) for s in acc_shapes],
        compiler_params=_params(("arbitrary",)),
    )(p_mla, cs, sn, wq, wkv, gql, gkvl, gq, gk, dq, dk, dv)


def _chunk_mask(nq, nk, key0, keys_on_rows):
    shape = (nk, nq) if keys_on_rows else (nq, nk)
    qi = lax.broadcasted_iota(jnp.int32, shape, 1 if keys_on_rows else 0) // CHUNK
    ki = lax.broadcasted_iota(jnp.int32, shape, 0 if keys_on_rows else 1) // CHUNK + key0 // CHUNK
    return ki <= qi


def _flash_fwd(q, k, v, t):
    tq = _pick(t, ATT_Q)
    tk = tq // ATT_KEY_TILES

    def body(q_ref, k_ref, v_ref, o_ref, lse_ref):
        i = pl.program_id(1)
        qt = q_ref[0]

        def step(j, carry, key0):
            m, l, acc = carry
            cols = pl.ds(pl.multiple_of(j * tk, tk), tk)
            s = _dot_nt(qt, k_ref[0, cols, :])
            if key0 is not None:
                s = jnp.where(_chunk_mask(tq, tk, key0, False), s, -jnp.inf)
            m_new = jnp.maximum(m, jnp.max(s, axis=-1, keepdims=True))
            p = jnp.exp2(s - m_new)
            alpha = jnp.exp2(m - m_new)
            return m_new, alpha * l + jnp.sum(p, axis=-1, keepdims=True), alpha * acc + _dot(p, v_ref[0, cols, :])

        init = (jnp.full((tq, 1), -jnp.inf, F32), jnp.zeros((tq, 1), F32), jnp.zeros((tq, HD), F32))
        carry = lax.fori_loop(0, ATT_KEY_TILES * i, lambda j, cr: step(j, cr, None), init)
        for h in range(ATT_KEY_TILES):
            carry = step(ATT_KEY_TILES * i + h, carry, h * tk)
        m, l, acc = carry
        o_ref[...] = (acc / l).astype(BF16)
        x1, x2, x3 = _split3(jnp.broadcast_to((m + jnp.log2(l)) * (1.0 / HD), (tq, HD)))
        ones = jnp.ones((SUBLANE, HD), BF16)
        rowsum = lambda p: lax.dot_general(ones, p, (((1,), (1,)), ((), ())), preferred_element_type=F32)
        lse_ref[0] = (rowsum(x1) + rowsum(x2) + rowsum(x3))[0:1, :]

    return pl.pallas_call(
        body, name="flash_fwd", grid=(NH, t // tq),
        in_specs=[pl.BlockSpec((1, tq, QK), lambda h, i: (h, i, 0)), pl.BlockSpec((1, t, QK), lambda h, i: (h, 0, 0)),
                  pl.BlockSpec((1, t, HD), lambda h, i: (h, 0, 0))],
        out_specs=[pl.BlockSpec((tq, HD), lambda h, i: (i, h)), pl.BlockSpec((1, 1, tq), lambda h, i: (h, 0, i))],
        out_shape=[jax.ShapeDtypeStruct((t, NH * HD), BF16), jax.ShapeDtypeStruct((NH, 1, t), F32)],
        compiler_params=_params(("parallel", "parallel")),
    )(q, k, v)


def _attn_out_bwd(dy, w, o, t):
    tm = _pick(t, ATT_TILES)

    def body(dy_ref, w_ref, o_ref, do_ref, delta_ref):
        do = _dot_nt(dy_ref[...], w_ref[...]).astype(BF16)
        do_ref[...] = do
        ones = jnp.ones((SUBLANE, HD), BF16)
        rowsum = lambda p: lax.dot_general(ones, p, (((1,), (1,)), ((), ())), preferred_element_type=F32)
        sl = [slice(HD * h, HD * (h + 1)) for h in range(NH)]
        parts = [_split3(do[:, sl[h]].astype(F32) * o_ref[:, sl[h]].astype(F32)) for h in range(NH)]
        sums = [[rowsum(x) for x in parts[h]] for h in range(NH)]
        for h in range(NH):
            delta_ref[h] = (sums[h][0] + sums[h][1] + sums[h][2])[0:1, :]

    return pl.pallas_call(
        body, name="attn_out_bwd", grid=(t // tm,),
        in_specs=[_BS((tm, dy.shape[1]), lambda i: (i, 0)), _BS(w.shape, lambda i: (0, 0)),
                  _BS((tm, NH * HD), lambda i: (i, 0))],
        out_specs=[_BS((tm, NH * HD), lambda i: (i, 0)), _BS((NH, 1, tm), lambda i: (0, 0, i))],
        out_shape=[jax.ShapeDtypeStruct((t, NH * HD), BF16), jax.ShapeDtypeStruct((NH, 1, t), F32)],
        compiler_params=_params(("parallel",)),
    )(dy, w, o)


def _flash_bwd(q, k, v, do, lse_row, delta_row, t):
    tq = _pick(t, ATT_Q)
    tk = tq // ATT_KEY_TILES

    def body(q_ref, k_ref, v_ref, do_ref, lse_ref, delta_ref, dq_ref, dk_ref, dv_ref):
        j = pl.program_id(1)

        @pl.when(j == 0)
        def _():
            dq_ref[...] = jnp.zeros_like(dq_ref)

        kt, vt = k_ref[0], v_ref[0]

        def step(i, carry, key0):
            dk, dv = carry
            rows = pl.ds(pl.multiple_of(i * tq, tq), tq)
            qt, dot_ = q_ref[0, rows, :], do_ref[rows, :]
            p = jnp.exp2(_dot_nt(kt, qt) - lse_ref[0, :, rows])
            if key0 is not None:
                p = jnp.where(_chunk_mask(tq, tk, key0, True), p, 0.0)
            ds = (p * (_dot_nt(vt, dot_) - delta_ref[0, :, rows])).astype(BF16)
            dq_ref[0, rows, :] += _dot_tn(ds, kt)
            return dk + _dot(ds, qt), dv + _dot(p, dot_)

        own = j // ATT_KEY_TILES
        carry = step(own, (jnp.zeros((tk, QK), F32), jnp.zeros((tk, HD), F32)), (j % ATT_KEY_TILES) * tk)
        dk, dv = lax.fori_loop(own + 1, t // tq, lambda i, cr: step(i, cr, None), carry)
        dk_ref[0] = dk * (1.0 / LOG2E)
        dv_ref[0] = dv

    return pl.pallas_call(
        body, name="flash_bwd", grid=(NH, t // tk),
        in_specs=[pl.BlockSpec((1, t, QK), lambda h, j: (h, 0, 0)), pl.BlockSpec((1, tk, QK), lambda h, j: (h, j, 0)),
                  pl.BlockSpec((1, tk, HD), lambda h, j: (h, j, 0)), pl.BlockSpec((t, HD), lambda h, j: (0, h)),
                  pl.BlockSpec((1, 1, t), lambda h, j: (h, 0, 0)), pl.BlockSpec((1, 1, t), lambda h, j: (h, 0, 0))],
        out_specs=[pl.BlockSpec((1, t, QK), lambda h, j: (h, 0, 0)), pl.BlockSpec((1, tk, QK), lambda h, j: (h, j, 0)),
                   pl.BlockSpec((1, tk, HD), lambda h, j: (h, j, 0))],
        out_shape=[jax.ShapeDtypeStruct((NH, t, QK), F32), jax.ShapeDtypeStruct((NH, t, QK), F32),
                   jax.ShapeDtypeStruct((NH, t, HD), F32)],
        compiler_params=_params(("parallel", "arbitrary")),
    )(q, k, v, do, lse_row, delta_row)


def _ffn_in(xn, w_in, name):
    t, k = xn.shape
    s, _, n = w_in.shape
    tm = _pick(t, ROW_TILES)

    def body(x_ref, w_ref, gu_ref, a_ref):
        x = x_ref[...]
        g, u = _dot(x, w_ref[0]), _dot(x, w_ref[1])
        gu_ref[0] = g.astype(BF16)
        gu_ref[1] = u.astype(BF16)
        a_ref[...] = (g * _sig(g) * u).astype(BF16)

    return pl.pallas_call(
        body, name=name, grid=(t // tm, s // 2),
        in_specs=[_BS((tm, k), lambda i, j: (i, 0)), _BS((2, None, k, n), lambda i, j: (0, j, 0, 0))],
        out_specs=[_BS((2, None, tm, n), lambda i, j: (0, j, i, 0)), _BS((None, tm, n), lambda i, j: (j, i, 0))],
        out_shape=[jax.ShapeDtypeStruct((2, s // 2, t, n), BF16), jax.ShapeDtypeStruct((s // 2, t, n), BF16)],
        compiler_params=_params(("parallel", "parallel")),
    )(xn, w_in.reshape(2, s // 2, k, n))


def _ffn_dgu(dfo, w_out, gu, name):
    t, k = dfo.shape
    s, n, _ = w_out.shape
    tm = _pick(t, ROW_TILES)

    def body(d_ref, w_ref, gu_ref, o_ref):
        da = _dot_nt(d_ref[...], w_ref[...])
        g, u = gu_ref[0].astype(F32), gu_ref[1].astype(F32)
        sg = _sig(g)
        o_ref[0] = (da * u * sg * (1.0 + g * (1.0 - sg))).astype(BF16)
        o_ref[1] = (da * g * sg).astype(BF16)

    pair = _BS((2, None, tm, n), lambda i, j: (0, j, i, 0))
    return pl.pallas_call(
        body, name=name, grid=(t // tm, s),
        in_specs=[_BS((tm, k), lambda i, j: (i, 0)), _BS((None, n, k), lambda i, j: (j, 0, 0)), pair],
        out_specs=pair, out_shape=jax.ShapeDtypeStruct((2, s, t, n), BF16),
        compiler_params=_params(("parallel", "parallel")),
    )(dfo, w_out, gu).reshape(2 * s, t, n)


def _ffn_fwd(xn, w_in, w_out, tag, epilogue):
    gu, a = _ffn_in(xn, w_in, tag + "_in")
    return gu, a, _mm_stack_red(a, w_out, tag + "_out", "nn", epilogue)


def _ffn_bwd(dfo, xn, gu, a, w_in, w_out, tag, epilogue):
    dw_out = _mm_stack_tn(a, dfo, tag + "_dwout")
    dgu = _ffn_dgu(dfo, w_out, gu, tag + "_dgu")
    dw_in = _mm_stack_tn(dgu, xn, tag + "_dwin")
    if callable(epilogue):
        epilogue = epilogue(dw_in, dw_out)
    return _mm_stack_red(dgu, w_in, tag + "_dxn", "nt", epilogue), dw_in, dw_out


def _local_step(x, target, cs, sn, w, late_weights, s, early_grads, last_grads):
    t = x.shape[0]
    tm = _pick(t, (512, 256, 128))
    g = {}
    ffn_out = lambda n: w[n].reshape(4, FF // 4, D)

    def norm_fn(x_ref, g_ref, o_ref):
        o_ref[...] = _rms(x_ref[...], g_ref[...]).astype(BF16)

    xn1 = _rows(norm_fn, "norm1", t, tm, [(x, D, 0)], [s["ffn1_norm"]], [(D, BF16)])[0]
    def res_norm_fn(scale):
        def fn(h_ref, f_ref, g_ref, h_out, n_out):
            h = h_ref[...] + scale * f_ref[...]
            h_out[...] = h
            n_out[...] = _rms(h, g_ref[...]).astype(BF16)
        return fn

    gu1, a1, (h1, u) = _ffn_fwd(xn1, w["ffn1_w_in"], ffn_out("ffn1_w_out"), "ffn1",
                                (res_norm_fn(0.5), x, [], [s["mix_norm"]], [(D, F32), (D, BF16)], []))
    w = {**w, **late_weights(h1)}
    rows_of = lambda n: w[n].reshape(-1, w[n].shape[2])
    w_hgb, w_mlab, w_o = rows_of("w_hg_branch"), rows_of("w_mla_branch"), rows_of("w_out")
    w_in_nat = w["w_in"].transpose(1, 0, 2).reshape(D, -1)
    w_mrg = w["w_merge"]
    mw = w_mrg.shape[2]
    w_in_hg, w_in_mla = w_in_nat[:, :4 * HGW], w_in_nat[:, 4 * HGW:]
    p_hg = _mm(u, w_in_hg, "nn", "proj_hg")
    p_mla = _mm(u, w_in_mla, "nn", "proj_mla")
    gpre = _cols_fwd(u, w_mrg, "proj_gate")
    o_pre, hgy, states = _hgrn_fwd(p_hg, s["hg_lb_table"], s["hg_out_norm"], t)
    prep_args = (p_mla, cs, sn, w["w_q_up"], w["w_kv_up"], s["mla_q_lora_norm"], s["mla_kv_lora_norm"],
                 s["q_head_norm"], s["k_head_norm"])
    q, k, v = _mla_prep_fwd(*prep_args, t)
    att, lse = _flash_fwd(q, k, v, t)
    y_hg = _mm(hgy, w_hgb, "nn", "branch_hg")
    def mix_fn(gp_ref, ym_ref, yh_ref, b_ref, o_ref, ym_out):
        gh = _sig(gp_ref[:, 0:D] + b_ref[:, 0:D])
        gm = _sig(gp_ref[:, D:2 * D] + b_ref[:, D:2 * D])
        ym = ym_ref[...]
        ym_out[...] = ym
        o_ref[...] = (gh * yh_ref[...] + gm * ym).astype(BF16)

    mixed, y_mla = _mm_stack_red(att[None], w_mlab[None], "branch_mla", "nn",
                                 (mix_fn, gpre, [y_hg], [s["b_merge"]], [(D, BF16), (D, F32)], [], (512, 256, 128)))
    h2, xn2 = _mm_stack_red(mixed[None], w_o[None], "mix_out", "nn",
                            (res_norm_fn(1.0), h1, [], [s["ffn2_norm"]], [(D, F32), (D, BF16)], []))

    def loss_fn(h_ref, f_ref, tg_ref, g_ref, dh_out, dhb_out):
        h = h_ref[...] + 0.5 * f_ref[...]
        e = _rms(h, g_ref[...]) - tg_ref[...]
        dh, dgain = _rms_bwd(h, g_ref[...], e / D)
        dh_out[...] = dh
        dhb_out[...] = (0.5 * dh).astype(BF16)
        return dgain, jnp.full((1, LANE), 0.5 / D * jnp.sum(e * e), F32)

    gu2, a2, (dh3, dfo2, g["final_norm"], loss) = _ffn_fwd(
        xn2, w["ffn2_w_in"], ffn_out("ffn2_w_out"), "ffn2",
        (loss_fn, h2, [target], [s["final_norm"]], [(D, F32), (D, BF16)], [(1, D), (1, LANE)]))

    def norm_bwd_fn(scale):
        def fn(h_ref, dxn_ref, dh_ref, g_ref, dh_out, dhb_out):
            dx, dgain = _rms_bwd(h_ref[...], g_ref[...], dxn_ref[...])
            dh = dh_ref[...] + dx
            dh_out[...] = dh
            dhb_out[...] = (scale * dh).astype(BF16)
            return (dgain,)
        return fn

    as_rows = lambda a: a.reshape((N_DEV, -1) + a.shape[-1:])
    (dh2, dh2b, g["ffn2_norm"]), g["ffn2_w_in"], dwo = _ffn_bwd(
        dfo2, xn2, gu2, a2, w["ffn2_w_in"], ffn_out("ffn2_w_out"), "ffn2",
        (norm_bwd_fn(1.0), h2, [dh3], [s["ffn2_norm"]], [(D, F32), (D, BF16)], [(1, D)]))
    g["ffn2_w_out"] = as_rows(dwo)
    g["w_out"] = as_rows(_mm(mixed, dh2b, "tn", "mix_out_dw"))

    def mix_bwd_fn(gp_ref, dm_ref, yh_ref, ym_ref, b_ref, dyh_out, dym_out, dg_out):
        gh = _sig(gp_ref[:, 0:D] + b_ref[:, 0:D])
        gm = _sig(gp_ref[:, D:2 * D] + b_ref[:, D:2 * D])
        dm = dm_ref[...]
        dyh_out[...] = (dm * gh).astype(BF16)
        dym_out[...] = (dm * gm).astype(BF16)
        dgh = dm * yh_ref[...] * gh * (1.0 - gh)
        dgm = dm * ym_ref[...] * gm * (1.0 - gm)
        dg_out[:, 0:D] = dgh.astype(BF16)
        dg_out[:, D:2 * D] = dgm.astype(BF16)
        return (jnp.concatenate([jnp.sum(dgh, axis=0, keepdims=True), jnp.sum(dgm, axis=0, keepdims=True)], axis=1),)

    dyh, dym, dgpre, g["b_merge"] = _mm_stack_red(
        dh2b[None], w_o[None], "mix_out_dx", "nt",
        (mix_bwd_fn, gpre, [y_hg, y_mla], [s["b_merge"]], [(D, BF16), (D, BF16), (2 * D, BF16)], [(1, 2 * D)],
         (512, 256, 128)))
    g["w_hg_branch"] = as_rows(_mm(hgy, dyh, "tn", "branch_hg_dw"))
    g["w_mla_branch"] = as_rows(_mm(att, dym, "tn", "branch_mla_dw"))
    g["w_merge"] = _cols_dw(u, dgpre, mw, "proj_gate_dw")
    dhgy = _mm(dyh, w_hgb, "nt", "branch_hg_dx")
    datt, delta = _attn_out_bwd(dym, w_mlab, att, t)
    du_gate = _cols_dx(dgpre, w_mrg, "proj_gate_dx")

    dq, dk, dv = _flash_bwd(q, k, v, datt, lse, delta, t)
    (dp_mla, dwq, dwkv, g["mla_q_lora_norm"], g["mla_kv_lora_norm"], g["q_head_norm"],
     g["k_head_norm"]) = _mla_prep_bwd(*prep_args, dq, dk, dv, t)
    g["w_q_up"], g["w_kv_up"] = dwq.astype(GRAD_DT), dwkv.astype(GRAD_DT)
    dp_hg, g["hg_lb_table"], g["hg_out_norm"] = _hgrn_bwd(p_hg, s["hg_lb_table"], s["hg_out_norm"], o_pre, states,
                                                          dhgy, t)
    dw_in_nat = jnp.concatenate([_mm(u, dp_hg, "tn", "proj_hg_dw"), _mm(u, dp_mla, "tn", "proj_mla_dw")], axis=1)
    g["w_in"] = dw_in_nat.T.reshape(N_DEV, -1, D)
    du_mla = _mm(dp_mla, w_in_mla, "nt", "proj_mla_dx")

    def mixnorm_bwd_fn(h_ref, a_ref, b_ref, c_ref, dh_ref, g_ref, dh_out, dhb_out):
        dx, dgain = _rms_bwd(h_ref[...], g_ref[...], a_ref[...] + b_ref[...] + c_ref[...])
        dh = dh_ref[...] + dx
        dh_out[...] = dh
        dhb_out[...] = (0.5 * dh).astype(BF16)
        return (dgain,)

    mix_gain = s["mix_norm"] + early_grads(g)[0:1, 0:1]
    dh1, dfo1, g["mix_norm"] = _mm_stack_red(
        dp_hg[None], w_in_hg[None], "proj_hg_dx", "nt",
        (mixnorm_bwd_fn, h1, [du_mla, du_gate, dh2], [mix_gain], [(D, F32), (D, BF16)], [(1, D)], (512, 256, 128)))
    def last_stage(dw_in, dw_out):
        g["ffn1_w_in"], g["ffn1_w_out"] = dw_in, as_rows(dw_out)
        gain = s["ffn1_norm"] + last_grads(g)[0:1, 0:1]
        return norm_bwd_fn(1.0), x, [dh1], [gain], [(D, F32), (D, BF16)], [(1, D)]

    (grad_x, _, g["ffn1_norm"]), _, _ = _ffn_bwd(dfo1, xn1, gu1, a1, w["ffn1_w_in"], ffn_out("ffn1_w_out"), "ffn1",
                                                 last_stage)
    return loss, grad_x, g


def _coords():
    return lax.axis_index("x"), lax.axis_index("y"), lax.axis_index("c")


def _hbm_call(body, name, ins, out_shapes, scratch):
    any_spec = pl.BlockSpec(memory_space=pl.ANY)
    return pl.pallas_call(
        body, name=name, out_shape=[jax.ShapeDtypeStruct(s, dt) for s, dt in out_shapes],
        in_specs=[any_spec] * len(ins), out_specs=[any_spec] * len(out_shapes), scratch_shapes=scratch,
    )(*ins)


def _my_slot():
    return 4 * lax.axis_index("x") + 2 * lax.axis_index("y") + lax.axis_index("c")


def _put_own(buf, own, index):
    return lax.dynamic_update_index_in_dim(buf, own, index, 0)


def _all_gather(blocks, name):
    nb = len(blocks)

    def body(*refs):
        x_refs, out_refs = refs[:nb], refs[nb:2 * nb]
        send_sems, recv_sems = refs[2 * nb:]
        x, y, c = _coords()
        me, sibling = (x, y, c), (x, y, 1 - c)
        chips = [(1 - x, y), (x, 1 - y), (1 - x, 1 - y)]

        def slot(b, px, py, pc):
            return out_refs[b].at[4 * px + 2 * py + pc]

        def copy(b, kk, block_of, to, src=None):
            return pltpu.make_async_remote_copy(
                src_ref=slot(b, *block_of) if src is None else src, dst_ref=slot(b, *block_of),
                send_sem=send_sems.at[b, kk], recv_sem=recv_sems.at[b, kk], device_id=to, device_id_type=MESH)

        first = [copy(b, 0, me, sibling, src=x_refs[b]) for b in range(nb)]
        first += [copy(b, 1 + j, me, (*chip, c), src=x_refs[b]) for j, chip in enumerate(chips) for b in range(nb)]
        for cp in first:
            cp.start()
        passed = []
        for j, chip in enumerate(chips):
            for b in range(nb):
                copy(b, 1 + j, (*chip, c), me).wait_recv()
                passed.append(copy(b, 4 + j, (*chip, c), sibling))
                passed[-1].start()
        for b in range(nb):
            copy(b, 0, sibling, me).wait_recv()
        for j, chip in enumerate(chips):
            for b in range(nb):
                copy(b, 4 + j, (*chip, 1 - c), me).wait_recv()
        for cp in first + passed:
            cp.wait_send()

    outs = _hbm_call(body, name, blocks, [((N_DEV,) + b.shape, b.dtype) for b in blocks],
                     [pltpu.SemaphoreType.DMA((nb, 7)), pltpu.SemaphoreType.DMA((nb, 7))])
    return [_put_own(o, b[None], _my_slot()) for o, b in zip(outs, blocks)]


def _gather_peers():
    x, y, c = _coords()
    return (x, y, c), [(x, y, 1 - c), (1 - x, y, c), (x, 1 - y, c), (1 - x, 1 - y, c)]


def _gather_start(blocks, after, name):
    nb = len(blocks)
    hbm, sem = pl.BlockSpec(memory_space=pltpu.HBM), pl.BlockSpec(memory_space=pltpu.SEMAPHORE)

    def body(*refs):
        x_refs, out_refs = refs[:nb], refs[nb:2 * nb]
        send_sems, recv_sems, token = refs[2 * nb + 1], refs[2 * nb + 2], refs[-1]
        (x, y, c), peers = _gather_peers()
        for kk, peer in enumerate(peers):
            for b in range(nb):
                pltpu.make_async_remote_copy(
                    src_ref=x_refs[b], dst_ref=out_refs[b].at[4 * x + 2 * y + c], send_sem=send_sems.at[4 * b + kk],
                    recv_sem=recv_sems.at[4 * b + kk], device_id=peer, device_id_type=MESH).start()
        token[...] = jnp.zeros_like(token)

    gathers = [pltpu.with_memory_space_constraint(lax.empty((N_DEV,) + b.shape, b.dtype), pltpu.HBM) for b in blocks]
    outs = pl.pallas_call(
        body, name=name,
        out_shape=(pltpu.SemaphoreType.DMA((4 * nb,)), pltpu.SemaphoreType.DMA((4 * nb,)),
                   *[pltpu.HBM(b.shape, b.dtype) for b in blocks], *[pltpu.HBM(b.shape, b.dtype) for b in gathers],
                   jax.ShapeDtypeStruct((SUBLANE, LANE), F32)),
        in_specs=[hbm] * (2 * nb) + [pl.BlockSpec(memory_space=pl.ANY)],
        out_specs=(sem, sem, *[hbm] * (2 * nb), pl.BlockSpec(memory_space=pltpu.VMEM)),
        input_output_aliases={i: 2 + i for i in range(2 * nb)},
        compiler_params=pltpu.CompilerParams(has_side_effects=pltpu.SideEffectType.DATAFLOW_SIDE_EFFECTING),
    )(*[pltpu.with_memory_space_constraint(b, pltpu.HBM) for b in blocks], *gathers, after)
    return outs[0], outs[1], list(outs[2:2 + nb]), list(outs[2 + nb:2 + 2 * nb]), outs[-1]


def _gather_wait(send_sems, recv_sems, thru, gathers, after, name):
    nb = len(thru)
    hbm, sem = pl.BlockSpec(memory_space=pltpu.HBM), pl.BlockSpec(memory_space=pltpu.SEMAPHORE)

    def body(*refs):
        x_refs, out_refs = refs[:nb], refs[nb:2 * nb]
        send_sems_, recv_sems_ = refs[2 * nb], refs[2 * nb + 1]
        _, peers = _gather_peers()
        for kk, (px, py, pc) in enumerate(peers):
            for b in range(nb):
                cp = pltpu.make_async_remote_copy(
                    src_ref=x_refs[b], dst_ref=out_refs[b].at[4 * px + 2 * py + pc], send_sem=send_sems_.at[4 * b + kk],
                    recv_sem=recv_sems_.at[4 * b + kk], device_id=(px, py, pc), device_id_type=MESH)
                cp.wait_send()
                cp.wait_recv()

    outs = pl.pallas_call(
        body, name=name,
        out_shape=(*[pltpu.HBM(b.shape, b.dtype) for b in thru], *[pltpu.HBM(b.shape, b.dtype) for b in gathers]),
        in_specs=[hbm] * (2 * nb) + [sem, sem, pl.BlockSpec(memory_space=pl.ANY)], out_specs=[hbm] * (2 * nb),
        input_output_aliases={i: i for i in range(2 * nb)},
        compiler_params=pltpu.CompilerParams(has_side_effects=pltpu.SideEffectType.DATAFLOW_SIDE_EFFECTING),
    )(*thru, *gathers, send_sems, recv_sems, after)
    return list(outs[:nb]), list(outs[nb:])


def _gather_finish(blocks, gathers, name):
    nb = len(blocks)

    def body(*refs):
        x_refs, in_refs, out_refs = refs[:nb], refs[nb:2 * nb], refs[2 * nb:3 * nb]
        send_sems, recv_sems = refs[3 * nb:]
        (x, y, c), peers = _gather_peers()
        copies = []
        for j, (px, py, _) in enumerate(peers[1:]):
            for b in range(nb):
                copies.append(pltpu.make_async_remote_copy(
                    src_ref=in_refs[b].at[4 * px + 2 * py + c], dst_ref=out_refs[b].at[4 * px + 2 * py + c],
                    send_sem=send_sems.at[b, j], recv_sem=recv_sems.at[b, j], device_id=(x, y, 1 - c),
                    device_id_type=MESH))
                copies[-1].start()
        for j, (px, py, _) in enumerate(peers[1:]):
            for b in range(nb):
                pltpu.make_async_remote_copy(
                    src_ref=in_refs[b].at[4 * px + 2 * py + c], dst_ref=out_refs[b].at[4 * px + 2 * py + 1 - c],
                    send_sem=send_sems.at[b, j], recv_sem=recv_sems.at[b, j], device_id=(x, y, 1 - c),
                    device_id_type=MESH).wait_recv()
        for cp in copies:
            cp.wait_send()

    any_spec = pl.BlockSpec(memory_space=pl.ANY)
    outs = pl.pallas_call(
        body, name=name, out_shape=[jax.ShapeDtypeStruct(b.shape, b.dtype) for b in gathers],
        in_specs=[any_spec] * (2 * nb), out_specs=[any_spec] * nb,
        input_output_aliases={nb + i: i for i in range(nb)},
        scratch_shapes=[pltpu.SemaphoreType.DMA((nb, 3)), pltpu.SemaphoreType.DMA((nb, 3))],
    )(*blocks, *gathers)
    return [_put_own(o, b[None], _my_slot()) for o, b in zip(outs, blocks)]


def _sibling_swap(bufs, name):
    nb = len(bufs)

    def body(*refs):
        x_refs, out_refs = refs[:nb], refs[nb:2 * nb]
        send_sems, recv_sems = refs[2 * nb:]
        x, y, c = _coords()
        copies = [pltpu.make_async_remote_copy(
            src_ref=x_refs[b].at[2 * q + 1 - c], dst_ref=out_refs[b].at[q], send_sem=send_sems.at[b, q],
            recv_sem=recv_sems.at[b, q], device_id=(x, y, 1 - c), device_id_type=MESH)
            for b in range(nb) for q in range(4)]
        for cp in copies:
            cp.start()
        for cp in copies:
            cp.wait()

    return _hbm_call(body, name, bufs, [((4,) + b.shape[1:], b.dtype) for b in bufs],
                     [pltpu.SemaphoreType.DMA((nb, 4)), pltpu.SemaphoreType.DMA((nb, 4))])


def _chip_exchange_start(bufs, name):
    nb = len(bufs)
    hbm, sem = pl.BlockSpec(memory_space=pltpu.HBM), pl.BlockSpec(memory_space=pltpu.SEMAPHORE)

    def body(*refs):
        x_refs, land_refs = refs[:nb], refs[nb:2 * nb]
        send_sems, recv_sems, token = refs[2 * nb], refs[2 * nb + 1], refs[-1]
        x, y, c = _coords()
        for j, (px, py) in enumerate([(1 - x, y), (x, 1 - y), (1 - x, 1 - y)]):
            for b in range(nb):
                pltpu.make_async_remote_copy(
                    src_ref=x_refs[b].at[2 * px + py], dst_ref=land_refs[b].at[2 * x + y], send_sem=send_sems.at[3 * b + j],
                    recv_sem=recv_sems.at[3 * b + j], device_id=(px, py, c), device_id_type=MESH).start()
        token[...] = jnp.zeros_like(token)

    lands = [pltpu.with_memory_space_constraint(lax.empty(b.shape, b.dtype), pltpu.HBM) for b in bufs]
    outs = pl.pallas_call(
        body, name=name,
        out_shape=(pltpu.SemaphoreType.DMA((3 * nb,)), pltpu.SemaphoreType.DMA((3 * nb,)),
                   *[pltpu.HBM(b.shape, b.dtype) for b in bufs], *[pltpu.HBM(b.shape, b.dtype) for b in bufs],
                   jax.ShapeDtypeStruct((SUBLANE, LANE), F32)),
        in_specs=[hbm] * (2 * nb), out_specs=(sem, sem, *[hbm] * (2 * nb), pl.BlockSpec(memory_space=pltpu.VMEM)),
        input_output_aliases={i: 2 + i for i in range(2 * nb)},
        compiler_params=pltpu.CompilerParams(has_side_effects=pltpu.SideEffectType.DATAFLOW_SIDE_EFFECTING),
    )(*[pltpu.with_memory_space_constraint(b, pltpu.HBM) for b in bufs], *lands)
    return outs[0], outs[1], list(outs[2:2 + nb]), list(outs[2 + nb:2 + 2 * nb]), outs[-1]


def _chip_exchange_wait(send_sems, recv_sems, thru, lands, after, name):
    nb = len(thru)
    hbm, sem = pl.BlockSpec(memory_space=pltpu.HBM), pl.BlockSpec(memory_space=pltpu.SEMAPHORE)

    def body(*refs):
        x_refs, land_refs = refs[:nb], refs[nb:2 * nb]
        send_sems_, recv_sems_ = refs[2 * nb], refs[2 * nb + 1]
        x, y, c = _coords()
        for j, (px, py) in enumerate([(1 - x, y), (x, 1 - y), (1 - x, 1 - y)]):
            for b in range(nb):
                cp = pltpu.make_async_remote_copy(
                    src_ref=x_refs[b].at[2 * px + py], dst_ref=land_refs[b].at[2 * px + py],
                    send_sem=send_sems_.at[3 * b + j], recv_sem=recv_sems_.at[3 * b + j], device_id=(px, py, c),
                    device_id_type=MESH)
                cp.wait_send()
                cp.wait_recv()

    outs = pl.pallas_call(
        body, name=name,
        out_shape=(*[pltpu.HBM(b.shape, b.dtype) for b in thru], *[pltpu.HBM(b.shape, b.dtype) for b in lands]),
        in_specs=[hbm] * (2 * nb) + [sem, sem, pl.BlockSpec(memory_space=pl.ANY)], out_specs=[hbm] * (2 * nb),
        input_output_aliases={i: i for i in range(2 * nb)},
        compiler_params=pltpu.CompilerParams(has_side_effects=pltpu.SideEffectType.DATAFLOW_SIDE_EFFECTING),
    )(*thru, *lands, send_sems, recv_sems, after)
    return list(outs[:nb]), list(outs[nb:])


def _chip_sum(g, r1, c, name):
    _, r, cw = g.shape
    tr = _pick(r, (256, 176, 128))

    def body(c_ref, g_ref, r_ref, o_ref):
        o_ref[...] = (g_ref[...].astype(F32) + r_ref[...].astype(F32)).astype(GRAD_DT)

    grid_spec = pltpu.PrefetchScalarGridSpec(
        num_scalar_prefetch=1, grid=(4, r // tr),
        in_specs=[_BS((None, None, tr, cw), lambda q, i, c_ref: (q, c_ref[0], i, 0)),
                  _BS((None, tr, cw), lambda q, i, c_ref: (q, i, 0))],
        out_specs=_BS((None, tr, cw), lambda q, i, c_ref: (q, i, 0)))
    return pl.pallas_call(
        body, name=name, grid_spec=grid_spec, out_shape=jax.ShapeDtypeStruct((4, r, cw), GRAD_DT),
        compiler_params=_params(("parallel", "parallel")),
    )(c.reshape(1).astype(jnp.int32), g.reshape(4, 2, r, cw), r1)


def _adamw_math(w, g, m, v):
    m = B1 * m + (1.0 - B1) * g
    v = B2 * v + (1.0 - B2) * (g * g)
    m_hat = m / (1.0 - B1 ** STEP)
    v_hat = v / (1.0 - B2 ** STEP)
    return -LR * (m_hat / (jnp.sqrt(v_hat) + AEPS) + WD * w), m, v


def _sum_adamw(parts, w, m, v, name):
    r, c = w.shape
    tr = _pick(r, (256, 176, 128))
    tc = _pick(c, (256, 128)) if tr == r and r > 256 else c

    def body(p0, p1, p2, p3, w_ref, m_ref, v_ref, g_out, d_out, m_out, v_out):
        g = ((p0[...].astype(F32) + p1[...].astype(F32)) + p2[...].astype(F32)) + p3[...].astype(F32)
        g_out[...] = g
        d_out[...], m_out[...], v_out[...] = _adamw_math(w_ref[...], g, m_ref[...], v_ref[...])

    part = lambda q: _BS((None, tr, tc), functools.partial(lambda i, j, q: (q, i, j), q=q))
    plain = _BS((tr, tc), lambda i, j: (i, j))
    return pl.pallas_call(
        body, name=name, grid=(r // tr, c // tc), in_specs=[part(q) for q in range(4)] + [plain] * 3,
        out_specs=[plain] * 4, out_shape=[jax.ShapeDtypeStruct((r, c), F32)] * 4,
        compiler_params=_params(("parallel", "parallel")),
    )(parts, parts, parts, parts, w, m, v)


def _small_update(gathered, w, m, v):
    r = w.shape[0]

    def body(ga_ref, w_ref, m_ref, v_ref, g_out, d_out, m_out, v_out):
        g = ga_ref[0]
        for dev in range(1, N_DEV):
            g = g + ga_ref[dev]
        g_out[...] = g
        d_out[...], m_out[...], v_out[...] = _adamw_math(w_ref[...], g, m_ref[...], v_ref[...])

    return pl.pallas_call(
        body, name="small_update", out_shape=[jax.ShapeDtypeStruct((r, LANE), F32)] * 4,
    )(gathered, w, m, v)


def _pack_small(vals):
    rows = []
    for name, (r, n) in SMALL:
        flat = vals[name].reshape(-1)
        pad = (-flat.shape[0]) % (SUBLANE * LANE)
        rows.append(jnp.pad(flat, (0, pad)).reshape(-1, LANE))
    return jnp.concatenate(rows, axis=0)


def _unpack_small(packed):
    out, off = {}, 0
    for name, (r, n) in SMALL:
        nrow = -(-(r * n) // (SUBLANE * LANE)) * SUBLANE
        out[name] = packed[off:off + nrow].reshape(-1)[:r * n].reshape(r, n)
        off += nrow
    return out


def kernel(x, positions, ffn1_norm, ffn1_w_in, ffn1_w_out, mix_norm, w_in, hg_lb_table, hg_out_norm, w_hg_branch, mla_q_lora_norm, w_q_up, mla_kv_lora_norm, w_kv_up, q_head_norm, k_head_norm, w_mla_branch, w_merge, b_merge, w_out, ffn2_norm, ffn2_w_in, ffn2_w_out, final_norm, loss_target, m_ffn1_norm, m_ffn1_w_in, m_ffn1_w_out, m_mix_norm, m_w_in, m_hg_lb_table, m_hg_out_norm, m_w_hg_branch, m_mla_q_lora_norm, m_w_q_up, m_mla_kv_lora_norm, m_w_kv_up, m_q_head_norm, m_k_head_norm, m_w_mla_branch, m_w_merge, m_b_merge, m_w_out, m_ffn2_norm, m_ffn2_w_in, m_ffn2_w_out, m_final_norm, v_ffn1_norm, v_ffn1_w_in, v_ffn1_w_out, v_mix_norm, v_w_in, v_hg_lb_table, v_hg_out_norm, v_w_hg_branch, v_mla_q_lora_norm, v_w_q_up, v_mla_kv_lora_norm, v_w_kv_up, v_q_head_norm, v_k_head_norm, v_w_mla_branch, v_w_merge, v_b_merge, v_w_out, v_ffn2_norm, v_ffn2_w_in, v_ffn2_w_out, v_final_norm):
    args = dict(locals())
    t = x.shape[1]
    big_w = {n: args[n][0] for n, _, _ in BIG}
    small = {n: args[n].reshape(shape) for n, shape in SMALL}

    names = [n for n, _, _ in BIG]
    first, rest = names[:2], names[2:]
    full = dict(zip(first, _all_gather([big_w[n].astype(BF16) for n in first], "weights_all_gather_ffn1")))
    g_send, g_recv, g_thru, g_bufs, g_token = _gather_start([big_w[n].astype(BF16) for n in rest], full[first[0]],
                                                            "weights_gather_start")
    gains = dict(small, ffn1_norm=small["ffn1_norm"] + g_token[0:1, 0:1])

    def late_weights(after):
        blocks, bufs = _gather_wait(g_send, g_recv, g_thru, g_bufs, after, "weights_gather_wait")
        return dict(zip(rest, _gather_finish(blocks, bufs, "weights_gather_finish")))

    inv_freq = ROPE_THETA ** (-jnp.arange(0, ROPE, 2, dtype=F32) / ROPE)
    ang = positions[0].astype(F32)[:, None] * inv_freq
    cs = jnp.concatenate([jnp.cos(ang), jnp.cos(ang)], axis=1)
    sn = jnp.concatenate([jnp.sin(ang), jnp.sin(ang)], axis=1)

    c = lax.axis_index("c")
    chip = 2 * lax.axis_index("x") + lax.axis_index("y")
    early = {}

    def chip_sums_of(g, ns, tag):
        from_sibling = _sibling_swap([g[n] for n in ns], "grads_sibling_swap_" + tag)
        return [_chip_sum(g[n], r1, c, "chip_sum_" + n) for n, r1 in zip(ns, from_sibling)]

    def early_grads(g):
        early["names"] = [n for n in names if n in g]
        early["sums"] = chip_sums_of(g, early["names"], "early")
        early["send"], early["recv"], early["thru"], early["lands"], token = _chip_exchange_start(
            early["sums"], "grads_exchange_start")
        return token

    def last_grads(g):
        early["late"] = [n for n in names if n not in early["names"]]
        early["l_send"], early["l_recv"], early["l_thru"], early["l_lands"], token = _chip_exchange_start(
            chip_sums_of(g, early["late"], "late"), "grads_exchange_late_start")
        return token

    loss_row, grad_x, g = _local_step(x[0], loss_target[0], cs, sn, full, late_weights, gains, early_grads,
                                      last_grads)
    late, l_send, l_recv = early["late"], early["l_send"], early["l_recv"]
    l_thru, l_lands = early["l_thru"], early["l_lands"]
    sent, landed = _chip_exchange_wait(early["send"], early["recv"], early["thru"], early["lands"], grad_x,
                                       "grads_exchange_wait")
    exchanged = {n: _put_own(land, lax.dynamic_index_in_dim(own, chip, 0), chip)
                 for n, land, own in zip(early["names"], landed, sent)}

    small_packed = jnp.concatenate([_pack_small(g), jnp.pad(loss_row, ((0, SUBLANE - 1), (0, 0)))], axis=0)
    small_all = _all_gather([small_packed], "small_all_gather")[0]
    zero_tail = jnp.zeros((SUBLANE, LANE), F32)
    pk = lambda d: jnp.concatenate([_pack_small(d), zero_tail], axis=0)
    sg, sd, sm, sv = _small_update(
        small_all, pk(small), pk({n: args["m_" + n].reshape(shape) for n, shape in SMALL}),
        pk({n: args["v_" + n].reshape(shape) for n, shape in SMALL}))
    n_small_rows = _pack_small(small).shape[0]
    loss = sg[n_small_rows, 0]
    outs = {k_: _unpack_small(a) for k_, a in (("grad", sg), ("delta", sd), ("new_m", sm), ("new_v", sv))}

    def adamw(n):
        tr = (lambda a: a.T) if n in GRAD_T else (lambda a: a)
        res = _sum_adamw(exchanged[n], tr(big_w[n]), tr(args["m_" + n][0]), tr(args["v_" + n][0]), "adamw_" + n)
        outs["grad"][n], outs["delta"][n], outs["new_m"][n], outs["new_v"][n] = [tr(a) for a in res]

    for n in early["names"]:
        adamw(n)
    sent, landed = _chip_exchange_wait(l_send, l_recv, l_thru, l_lands, outs["delta"][early["names"][-1]],
                                       "grads_exchange_late_wait")
    for n, land, own in zip(late, landed, sent):
        exchanged[n] = _put_own(land, lax.dynamic_index_in_dim(own, chip, 0), chip)
        adamw(n)

    def shaped(kind, n):
        return outs[kind][n].reshape(args[n].shape)

    return (loss, grad_x[None], *[shaped("grad", n) for n in WEIGHT_ORDER], *[shaped("delta", n) for n in WEIGHT_ORDER],
            *[shaped("new_m", n) for n in WEIGHT_ORDER], *[shaped("new_v", n) for n in WEIGHT_ORDER])
```

```python
import functools

import jax
import jax.numpy as jnp
from jax import lax
from jax.experimental import pallas as pl
from jax.experimental.pallas import tpu as pltpu

F32 = jnp.float32
BF16 = jnp.bfloat16

D = 1024
FF = 2816
NH = 8
HD = 128
ROPE = 64
QK = HD + ROPE
QL = 384
KVL = 256
HGW = NH * HD
CHUNK = 64
EPS = 1e-6
ROPE_THETA = 10000.0
SCALE = QK ** -0.5
LOG2E = 1.4426950408889634

LR, B1, B2, AEPS, WD, STEP = 0.001, 0.9, 0.999, 1e-08, 0.01, 10

HB = 128
SUB = 16
EXP_CLAMP = 80.0
ATT_TILES = (1024, 512, 256, 128)
ATT_Q = (1024, 512, 256, 128)
ATT_KEY_TILES = 1
GRAD_DT = BF16
ROW_TILES = (1024, 512, 256, 128)

LANE = 128
SUBLANE = 8
VMEM_LIMIT = 56 << 20

N_DEV = 8
MESH = pl.DeviceIdType.MESH

BIG = (
    ("ffn1_w_in", (D, 2 * FF), 1), ("ffn1_w_out", (FF, D), 0), ("w_in", (D, 4800), 1),
    ("w_hg_branch", (HGW, D), 0), ("w_q_up", (QL, NH * QK), 1), ("w_kv_up", (KVL, NH * 2 * HD), 1),
    ("w_mla_branch", (NH * HD, D), 0), ("w_merge", (D, 2 * D), 1), ("w_out", (D, D), 0),
    ("ffn2_w_in", (D, 2 * FF), 1), ("ffn2_w_out", (FF, D), 0),
)
SMALL = (
    ("ffn1_norm", (1, D)), ("mix_norm", (1, D)), ("hg_lb_table", (2, HGW)), ("hg_out_norm", (1, HD)),
    ("mla_q_lora_norm", (1, QL)), ("mla_kv_lora_norm", (1, KVL)), ("q_head_norm", (1, QK)),
    ("k_head_norm", (1, QK)), ("b_merge", (1, 2 * D)), ("ffn2_norm", (1, D)), ("final_norm", (1, D)),
)
GRAD_T = ("ffn1_w_in", "ffn2_w_in", "w_in", "w_q_up")
WEIGHT_ORDER = ("ffn1_norm", "ffn1_w_in", "ffn1_w_out", "mix_norm", "w_in", "hg_lb_table", "hg_out_norm",
                "w_hg_branch", "mla_q_lora_norm", "w_q_up", "mla_kv_lora_norm", "w_kv_up", "q_head_norm",
                "k_head_norm", "w_mla_branch", "w_merge", "b_merge", "w_out", "ffn2_norm", "ffn2_w_in",
                "ffn2_w_out", "final_norm")


def _pick(n, cands):
    for c in cands:
        if n % c == 0:
            return c
    return n


def _params(sem):
    return pltpu.CompilerParams(dimension_semantics=sem, vmem_limit_bytes=VMEM_LIMIT)


def _sig(x):
    return 0.5 * jnp.tanh(0.5 * x) + 0.5


def _dot(a, b):
    return jnp.dot(a.astype(BF16), b.astype(BF16), preferred_element_type=F32)


def _dot_nt(a, b):
    return lax.dot_general(a.astype(BF16), b.astype(BF16), (((1,), (1,)), ((), ())),
                           preferred_element_type=F32)


def _dot_tn(a, b):
    return lax.dot_general(a.astype(BF16), b.astype(BF16), (((0,), (0,)), ((), ())),
                           preferred_element_type=F32)


def _split3(x):
    x1 = x.astype(BF16)
    r1 = x - x1.astype(F32)
    x2 = r1.astype(BF16)
    x3 = (r1 - x2.astype(F32)).astype(BF16)
    return x1, x2, x3


def _dot_sel(m, x):
    x1, x2, x3 = _split3(x)
    d = lambda p: jnp.dot(m, p, preferred_element_type=F32)
    return d(x1) + d(x2) + d(x3)


def _sel_dot(x, m):
    x1, x2, x3 = _split3(x)
    d = lambda p: jnp.dot(p, m, preferred_element_type=F32)
    return d(x1) + d(x2) + d(x3)


_TN = (1408, 1024, 768, 512, 384, 256, 128)


def _accumulate(acc, o_ref, axis, steps, term):
    step = pl.program_id(axis)

    @pl.when(step == 0)
    def _():
        acc[...] = jnp.zeros_like(acc)

    val = term()
    if isinstance(val, (list, tuple)):
        for e, v in enumerate(val):
            acc[e] += v
    else:
        acc[...] += val

    @pl.when(step == steps - 1)
    def _():
        o_ref[...] = acc[...].astype(o_ref.dtype)


def _mm(a, b, mode, name, out_dtype=F32):
    if mode == "tn":
        t, m = a.shape
        n = b.shape[1]
        tt, tm, tn = _pick(t, ROW_TILES), _pick(m, _TN), _pick(n, _TN)

        def body(a_ref, b_ref, o_ref, acc):
            _accumulate(acc, o_ref, 2, t // tt, lambda: _dot_tn(a_ref[...], b_ref[...]))

        return pl.pallas_call(
            body, name=name, grid=(m // tm, n // tn, t // tt),
            in_specs=[pl.BlockSpec((tt, tm), lambda i, j, k: (k, i)),
                      pl.BlockSpec((tt, tn), lambda i, j, k: (k, j))],
            out_specs=pl.BlockSpec((tm, tn), lambda i, j, k: (i, j)),
            out_shape=jax.ShapeDtypeStruct((m, n), GRAD_DT), scratch_shapes=[pltpu.VMEM((tm, tn), F32)],
            compiler_params=_params(("parallel", "parallel", "arbitrary")),
        )(a, b)

    m, k = a.shape
    tm = _pick(m, ROW_TILES)
    if mode == "nn":
        n = b.shape[1]
        tn = _pick(n, _TN)
        b_spec = pl.BlockSpec((k, tn), lambda i, j: (0, j))
        dot = _dot
    else:
        n = b.shape[0]
        tn = _pick(n, _TN if k <= 4096 else (512, 256, 128))
        b_spec = pl.BlockSpec((tn, k), lambda i, j: (j, 0))
        dot = _dot_nt

    def body(a_ref, b_ref, o_ref):
        o_ref[...] = dot(a_ref[...], b_ref[...]).astype(o_ref.dtype)

    return pl.pallas_call(
        body, name=name, grid=(m // tm, n // tn),
        in_specs=[pl.BlockSpec((tm, k), lambda i, j: (i, 0)), b_spec],
        out_specs=pl.BlockSpec((tm, tn), lambda i, j: (i, j)),
        out_shape=jax.ShapeDtypeStruct((m, n), out_dtype),
        compiler_params=_params(("parallel", "parallel")),
    )(a, b)


_DOTS = {"nn": _dot, "nt": _dot_nt, "tn": _dot_tn}
_BS = pl.BlockSpec


class _AsRef:
    def __init__(self, value):
        self.value = value

    def __getitem__(self, idx):
        return self.value


def _mm_stack_red(a, w, name, kind, epilogue):
    s, t, n = a.shape
    nout = w.shape[2] if kind == "nn" else w.shape[1]
    fn, first, rows, vecs, outs, accs = epilogue[:6]
    rows = [first] + list(rows)
    tm = _pick(t, epilogue[6] if len(epilogue) > 6 else ROW_TILES)
    n_row, n_vec, n_out = len(rows), len(vecs), len(outs)
    dot = _DOTS[kind]

    def body(*refs):
        a_ref, w_ref, prod = refs[0], refs[1], refs[-1]
        row_refs = refs[2:2 + n_row]
        vec_refs = refs[2 + n_row:2 + n_row + n_vec]
        out_refs = refs[2 + n_row + n_vec:2 + n_row + n_vec + n_out]
        acc_refs = refs[2 + n_row + n_vec + n_out:-1]
        i, j = pl.program_id(0), pl.program_id(1)

        @pl.when(j == 0)
        def _():
            prod[...] = jnp.zeros_like(prod)

        prod[...] += dot(a_ref[...], w_ref[...])

        @pl.when(j == s - 1)
        def _():
            res = fn(row_refs[0], _AsRef(prod[...]), *row_refs[1:], *vec_refs, *out_refs)
            if acc_refs:
                @pl.when(i == 0)
                def _():
                    for r in acc_refs:
                        r[...] = jnp.zeros_like(r)

                for r, val in zip(acc_refs, res):
                    r[...] += val

    in_specs = [_BS((None, tm, n), lambda i, j: (j, i, 0)), _BS((None,) + w.shape[1:], lambda i, j: (j, 0, 0))]
    in_specs += [_BS((tm, r.shape[1]), lambda i, j: (i, 0)) for r in rows]
    in_specs += [_BS(v.shape, lambda i, j: (0, 0)) for v in vecs]
    out_specs = [_BS((tm, wd), lambda i, j: (i, 0)) for wd, _ in outs] + [_BS(sh, lambda i, j: (0, 0)) for sh in accs]
    out_shape = [jax.ShapeDtypeStruct((t, wd), dt) for wd, dt in outs] + [jax.ShapeDtypeStruct(sh, F32) for sh in accs]
    return pl.pallas_call(
        body, name=name, grid=(t // tm, s), in_specs=in_specs, out_specs=out_specs, out_shape=out_shape,
        scratch_shapes=[pltpu.VMEM((tm, nout), F32)],
        compiler_params=_params(("arbitrary" if accs else "parallel", "arbitrary")),
    )(a, w, *rows, *vecs)


def _mm_stack_tn(a, b, name):
    grp = 4
    s, t, k = a.shape
    n = b.shape[1]
    tt = _pick(t, ROW_TILES)

    def body(a_ref, b_ref, o_ref, acc):
        def terms():
            shared = b_ref[...]
            return [_dot_tn(a_ref[e], shared) for e in range(grp)]

        _accumulate(acc, o_ref, 1, t // tt, terms)

    return pl.pallas_call(
        body, name=name, grid=(s // grp, t // tt),
        in_specs=[_BS((None, grp, tt, k), lambda j, r: (j, 0, r, 0)), _BS((tt, n), lambda j, r: (r, 0))],
        out_specs=_BS((None, grp, k, n), lambda j, r: (j, 0, 0, 0)),
        out_shape=jax.ShapeDtypeStruct((s // grp, grp, k, n), GRAD_DT), scratch_shapes=[pltpu.VMEM((grp, k, n), F32)],
        compiler_params=_params(("parallel", "arbitrary")),
    )(a.reshape(s // grp, grp, t, k), b).reshape(s, k, n)


def _cols_fwd(x, w, name):
    t, k = x.shape
    s, _, n = w.shape
    tm = _pick(t, ROW_TILES)

    def body(x_ref, w_ref, o_ref):
        x_ = x_ref[...]
        for j in range(s):
            o_ref[:, n * j:n * (j + 1)] = _dot(x_, w_ref[j])

    return pl.pallas_call(
        body, name=name, grid=(t // tm,),
        in_specs=[_BS((tm, k), lambda i: (i, 0)), _BS((s, k, n), lambda i: (0, 0, 0))],
        out_specs=_BS((tm, s * n), lambda i: (i, 0)), out_shape=jax.ShapeDtypeStruct((t, s * n), F32),
        compiler_params=_params(("parallel",)),
    )(x, w)


def _cols_dx(d, w, name):
    t = d.shape[0]
    s, k, n = w.shape
    tm = _pick(t, ROW_TILES)

    def body(d_ref, w_ref, o_ref):
        acc = _dot_nt(d_ref[:, 0:n], w_ref[0])
        for j in range(1, s):
            acc = acc + _dot_nt(d_ref[:, n * j:n * (j + 1)], w_ref[j])
        o_ref[...] = acc

    return pl.pallas_call(
        body, name=name, grid=(t // tm,),
        in_specs=[_BS((tm, s * n), lambda i: (i, 0)), _BS((s, k, n), lambda i: (0, 0, 0))],
        out_specs=_BS((tm, k), lambda i: (i, 0)), out_shape=jax.ShapeDtypeStruct((t, k), F32),
        compiler_params=_params(("parallel",)),
    )(d, w)


def _cols_dw(x, d, n, name):
    t, k = x.shape
    s = d.shape[1] // n
    tt = _pick(t, ROW_TILES)

    def body(x_ref, d_ref, o_ref, acc):
        def terms():
            x_ = x_ref[...]
            return [_dot_tn(x_, d_ref[:, n * j:n * (j + 1)]) for j in range(s)]

        _accumulate(acc, o_ref, 0, t // tt, terms)

    return pl.pallas_call(
        body, name=name, grid=(t // tt,),
        in_specs=[_BS((tt, k), lambda r: (r, 0)), _BS((tt, s * n), lambda r: (r, 0))],
        out_specs=_BS((s, k, n), lambda r: (0, 0, 0)), out_shape=jax.ShapeDtypeStruct((s, k, n), GRAD_DT),
        scratch_shapes=[pltpu.VMEM((s, k, n), F32)], compiler_params=_params(("arbitrary",)),
    )(x, d)


def _rows(fn, name, t, tm, ins, vecs, outs, accs=()):
    n_in, n_out, n_acc = len(ins) + len(vecs), len(outs), len(accs)

    def body(*refs):
        res = fn(*refs[:n_in + n_out])
        if n_acc:
            acc_refs = refs[n_in + n_out:]

            @pl.when(pl.program_id(0) == 0)
            def _():
                for r in acc_refs:
                    r[...] = jnp.zeros_like(r)

            for r, val in zip(acc_refs, res):
                r[...] += val

    in_specs = [pl.BlockSpec((tm, bw), functools.partial(lambda i, cb: (i, cb), cb=cb)) for _, bw, cb in ins]
    in_specs += [pl.BlockSpec(v.shape, lambda i: (0, 0)) for v in vecs]
    out_specs = [pl.BlockSpec((tm, w), lambda i: (i, 0)) for w, _ in outs]
    out_specs += [pl.BlockSpec(s, lambda i: (0, 0)) for s in accs]
    out_shape = [jax.ShapeDtypeStruct((t, w), dt) for w, dt in outs]
    out_shape += [jax.ShapeDtypeStruct(s, F32) for s in accs]
    return pl.pallas_call(
        body, name=name, grid=(t // tm,), in_specs=in_specs, out_specs=out_specs, out_shape=out_shape,
        compiler_params=_params(("arbitrary",) if n_acc else ("parallel",)),
    )(*[a for a, _, _ in ins], *vecs)


def _rms(x, g):
    return x * lax.rsqrt(jnp.mean(x * x, axis=-1, keepdims=True) + EPS) * g


def _rms_bwd(x, g, dy):
    xh = x * lax.rsqrt(jnp.mean(x * x, axis=-1, keepdims=True) + EPS)
    r = lax.rsqrt(jnp.mean(x * x, axis=-1, keepdims=True) + EPS)
    dyg = dy * g
    dx = r * (dyg - xh * jnp.mean(dyg * xh, axis=-1, keepdims=True))
    return dx, jnp.sum(dy * xh, axis=0, keepdims=True)


def _hgrn_mats():
    row = lax.broadcasted_iota(jnp.int32, (HB, HB), 0)
    col = lax.broadcasted_iota(jnp.int32, (HB, HB), 1)
    return row, col


def _hgrn_gates(qr, z, t0, t1):
    lb = 1.0 / (1.0 + jnp.exp(t1 - t0))
    th = jnp.tanh(0.5 * z)
    sz, sneg = 0.5 + 0.5 * th, 0.5 - 0.5 * th
    f = lb + (1.0 - lb) * sz
    return lb, sz, sneg, f, jnp.log(f), (1.0 - lb) * sneg, qr * _sig(qr)


def _sub_decay(x, mid, cum, i):
    n = SUB * (i + 1)
    seen = x[:n] * jnp.exp(jnp.minimum(mid - cum[:n], EXP_CLAMP))
    return seen if n == HB else jnp.concatenate([seen, jnp.zeros((HB - n, x.shape[1]), F32)], axis=0)


def _hgrn_scores(q, k, cum):
    heads, nsub = range(len(q)), HB // SUB
    mids = [[cum[h][SUB * i + SUB // 2 - 1:SUB * i + SUB // 2, :] for i in range(nsub)] for h in heads]
    cmid = [jnp.concatenate([cum[h][SUB * i:SUB * (i + 1)] - mids[h][i] for i in range(nsub)], axis=0) for h in heads]
    qd = [q[h] * jnp.exp(jnp.minimum(cmid[h], EXP_CLAMP)) for h in heads]
    qd_b = [x.astype(BF16) for x in qd]
    kds = [[None] * nsub for _ in heads]
    parts = [[None] * nsub for _ in heads]
    for i in range(nsub):
        for h in heads:
            kds[h][i] = _sub_decay(k[h], mids[h][i], cum[h], i)
        for h in heads:
            parts[h][i] = _dot_nt(qd_b[h][SUB * i:SUB * (i + 1)], kds[h][i])
    return qd, kds, [jnp.concatenate(parts[h], axis=0) for h in heads], cmid


def _hgrn_fwd(p_hg, table, gain, t):
    nblk = t // HB

    def body(q_ref, f_ref, i_ref, g_ref, tab_ref, gain_ref, o_ref, y_ref, st_ref, state):
        @pl.when(pl.program_id(0) == 0)
        def _():
            state[...] = jnp.zeros_like(state)

        row, col = _hgrn_mats()
        causal = col <= row
        tri = causal.astype(BF16)
        heads = range(NH)
        sl = [slice(HD * h, HD * (h + 1)) for h in heads]
        gates = [_hgrn_gates(q_ref[:, sl[h]], f_ref[:, sl[h]], tab_ref[0:1, sl[h]], tab_ref[1:2, sl[h]]) for h in heads]
        lf, k, q = [g[4] for g in gates], [g[5] for g in gates], [g[6] for g in gates]
        v = [i_ref[:, sl[h]] for h in heads]
        cum = [_dot_sel(tri, lf[h]) for h in heads]
        _, _, s, _ = _hgrn_scores(q, k, cum)
        p = [jnp.where(causal, s[h], 0.0) for h in heads]
        st = [state[h] for h in heads]
        o = [_dot(p[h], v[h]) + _dot_nt(q[h] * jnp.exp(cum[h]), st[h]) for h in heads]
        last = [cum[h][HB - 1:HB, :] for h in heads]
        new_st = [st[h] * jnp.exp(last[h]) + _dot_tn(v[h], k[h] * jnp.exp(last[h] - cum[h])) for h in heads]
        for h in heads:
            gr = g_ref[:, sl[h]]
            st_ref[h, 0] = st[h]
            state[h] = new_st[h]
            o_ref[:, sl[h]] = o[h]
            y_ref[:, sl[h]] = (_rms(o[h], gain_ref[...]) * gr * _sig(gr)).astype(BF16)

    blk = lambda cb: pl.BlockSpec((HB, HGW), functools.partial(lambda n, cb: (n, cb), cb=cb))
    return pl.pallas_call(
        body, name="hgrn_fwd", grid=(nblk,),
        in_specs=[blk(0), blk(1), blk(2), blk(3), pl.BlockSpec((2, HGW), lambda n: (0, 0)),
                  pl.BlockSpec((1, HD), lambda n: (0, 0))],
        out_specs=[pl.BlockSpec((HB, HGW), lambda n: (n, 0)), pl.BlockSpec((HB, HGW), lambda n: (n, 0)),
                   pl.BlockSpec((NH, 1, HD, HD), lambda n: (0, n, 0, 0))],
        out_shape=[jax.ShapeDtypeStruct((t, HGW), F32), jax.ShapeDtypeStruct((t, HGW), BF16),
                   jax.ShapeDtypeStruct((NH, nblk, HD, HD), F32)],
        scratch_shapes=[pltpu.VMEM((NH, HD, HD), F32)],
        compiler_params=_params(("arbitrary",)),
    )(p_hg, p_hg, p_hg, p_hg, table, gain)


def _hgrn_bwd(p_hg, table, gain, o_pre, states, dy, t):
    nblk = t // HB

    def body(q_ref, f_ref, i_ref, g_ref, tab_ref, gain_ref, o_ref, st_ref, dy_ref, dp_ref, dtab_ref, dgain_ref,
             dstate):
        @pl.when(pl.program_id(0) == 0)
        def _():
            dstate[...] = jnp.zeros_like(dstate)
            dtab_ref[...] = jnp.zeros_like(dtab_ref)
            dgain_ref[...] = jnp.zeros_like(dgain_ref)

        row, col = _hgrn_mats()
        causal = col <= row
        tri = causal.astype(BF16)
        tri_t = (row <= col).astype(BF16)
        heads, nsub = range(NH), HB // SUB
        sl = [slice(HD * h, HD * (h + 1)) for h in heads]
        rows_of = lambda i: slice(SUB * i, SUB * (i + 1))
        gain_ = gain_ref[...]
        qr, z = [q_ref[:, sl[h]] for h in heads], [f_ref[:, sl[h]] for h in heads]
        v, gr = [i_ref[:, sl[h]] for h in heads], [g_ref[:, sl[h]] for h in heads]
        gates = [_hgrn_gates(qr[h], z[h], tab_ref[0:1, sl[h]], tab_ref[1:2, sl[h]]) for h in heads]
        lb, sz, sneg, f, lf, k, q = ([g[j] for g in gates] for j in range(7))
        cum = [_dot_sel(tri, lf[h]) for h in heads]
        qd, kds, s, cmid = _hgrn_scores(q, k, cum)
        p = [jnp.where(causal, s[h], 0.0) for h in heads]
        st, dst = [st_ref[h, 0] for h in heads], [dstate[h] for h in heads]
        o, dyh, sg = [o_ref[:, sl[h]] for h in heads], [dy_ref[:, sl[h]] for h in heads], [_sig(x) for x in gr]
        dgr = [dyh[h] * _rms(o[h], gain_) * sg[h] * (1.0 + gr[h] * (1.0 - sg[h])) for h in heads]
        norm_bwd = [_rms_bwd(o[h], gain_, dyh[h] * gr[h] * sg[h]) for h in heads]
        do = [x[0] for x in norm_bwd]
        do_b = [x.astype(BF16) for x in do]
        ecum, last = [jnp.exp(x) for x in cum], [x[HB - 1:HB, :] for x in cum]
        qc = [q[h] * ecum[h] for h in heads]
        edec = [jnp.exp(last[h] - cum[h]) for h in heads]
        kdec = [k[h] * edec[h] for h in heads]
        dp = [jnp.where(causal, _dot_nt(do_b[h], v[h]), 0.0) for h in heads]
        dv = [_dot(p[h].T, do_b[h]) + _dot_nt(kdec[h], dst[h]) for h in heads]
        dqc = [_dot(do_b[h], st[h]) for h in heads]
        dkdec = [_dot(v[h], dst[h]) for h in heads]
        new_dst = [dst[h] * jnp.exp(last[h]) + _dot(do[h].T, qc[h]) for h in heads]
        dp_b = [x.astype(BF16) for x in dp]
        dqd = [jnp.concatenate(parts, axis=0) for parts in
               zip(*[[_dot(dp_b[h][rows_of(i)], kds[h][i]) for h in heads] for i in range(nsub)])]
        gq = [dqd[h] * qd[h] for h in heads]
        dq = [dqd[h] * jnp.exp(jnp.minimum(cmid[h], EXP_CLAMP)) + dqc[h] * ecum[h] for h in heads]
        gs = [dkdec[h] * kdec[h] for h in heads]
        dk = [dkdec[h] * edec[h] for h in heads]
        dcum = [dqc[h] * qc[h] - gs[h] + gq[h]
                + jnp.where(row == HB - 1, jnp.sum(gs[h], axis=0, keepdims=True)
                            + jnp.exp(last[h]) * jnp.sum(st[h] * dst[h], axis=0, keepdims=True), 0.0) for h in heads]
        qd_b = [x.astype(BF16) for x in qd]
        for i in range(nsub):
            dkd = [_dot_tn(dp_b[h][rows_of(i)], qd_b[h][rows_of(i)]) for h in heads]
            for h in heads:
                mid = cum[h][SUB * i + SUB // 2 - 1:SUB * i + SUB // 2, :]
                dk[h] = dk[h] + _sub_decay(dkd[h], mid, cum[h], i)
                gk = dkd[h] * kds[h][i]
                to_mid = jnp.sum(gk, axis=0, keepdims=True) - jnp.sum(gq[h][rows_of(i)], axis=0, keepdims=True)
                dcum[h] = dcum[h] - gk + jnp.where(row == SUB * i + SUB // 2 - 1, to_mid, 0.0)
        dlf = [_dot_sel(tri_t, dcum[h]) for h in heads]
        dgain = jnp.zeros((1, HD), F32)
        for h in heads:
            df = dlf[h] / f[h] - dk[h]
            dlb = jnp.sum(df * sneg[h], axis=0, keepdims=True) * lb[h] * (1.0 - lb[h])
            dtab_ref[0:1, sl[h]] += dlb
            dtab_ref[1:2, sl[h]] -= dlb
            sq = _sig(qr[h])
            dstate[h] = new_dst[h]
            dp_ref[:, sl[h]] = (dq[h] * sq * (1.0 + qr[h] * (1.0 - sq))).astype(BF16)
            dp_ref[:, HGW + HD * h:HGW + HD * (h + 1)] = (df * (1.0 - lb[h]) * sz[h] * sneg[h]).astype(BF16)
            dp_ref[:, 2 * HGW + HD * h:2 * HGW + HD * (h + 1)] = dv[h].astype(BF16)
            dp_ref[:, 3 * HGW + HD * h:3 * HGW + HD * (h + 1)] = dgr[h].astype(BF16)
            dgain = dgain + norm_bwd[h][1]
        dgain_ref[...] += dgain

    rev = lambda cb: pl.BlockSpec((HB, HGW), functools.partial(lambda n, cb: (nblk - 1 - n, cb), cb=cb))
    return pl.pallas_call(
        body, name="hgrn_bwd", grid=(nblk,),
        in_specs=[rev(0), rev(1), rev(2), rev(3), pl.BlockSpec((2, HGW), lambda n: (0, 0)),
                  pl.BlockSpec((1, HD), lambda n: (0, 0)), rev(0),
                  pl.BlockSpec((NH, 1, HD, HD), lambda n: (0, nblk - 1 - n, 0, 0)), rev(0)],
        out_specs=[pl.BlockSpec((HB, 4 * HGW), lambda n: (nblk - 1 - n, 0)),
                   pl.BlockSpec((2, HGW), lambda n: (0, 0)), pl.BlockSpec((1, HD), lambda n: (0, 0))],
        out_shape=[jax.ShapeDtypeStruct((t, 4 * HGW), BF16), jax.ShapeDtypeStruct((2, HGW), F32),
                   jax.ShapeDtypeStruct((1, HD), F32)],
        scratch_shapes=[pltpu.VMEM((NH, HD, HD), F32)],
        compiler_params=_params(("arbitrary",)),
    )(p_hg, p_hg, p_hg, p_hg, table, gain, o_pre, states, dy)


def _rope_mat():
    r = lax.broadcasted_iota(jnp.int32, (ROPE, ROPE), 0)
    c = lax.broadcasted_iota(jnp.int32, (ROPE, ROPE), 1)
    half = ROPE // 2
    return ((r == c - half).astype(F32) - (r == c + half).astype(F32)).astype(BF16)


def _mla_prep_fwd(p_mla, cs, sn, wq, wkv, gql, gkvl, gq, gk, t):
    tm = _pick(t, (512, 256, 128))

    def body(p_ref, cs_ref, sn_ref, wq_ref, wkv_ref, gql_ref, gkvl_ref, gq_ref, gk_ref,
             q_ref, k_ref, v_ref):
        rmat = _rope_mat()
        cqn = _rms(p_ref[:, 0:QL], gql_ref[...]).astype(BF16)
        ckvn = _rms(p_ref[:, QL:QL + KVL], gkvl_ref[...]).astype(BF16)
        kpe = p_ref[:, QL + KVL:QL + KVL + ROPE]
        c, s = cs_ref[...], sn_ref[...]
        rot = lambda x: x * c + _sel_dot(x, rmat) * s
        heads = range(NH)
        ssq = lambda x: jnp.sum(x * x, -1, keepdims=True)
        qa = [_dot(cqn, wq_ref[h, :, 0:HD]) for h in heads]
        qr = [_dot(cqn, wq_ref[h, :, HD:QK]) for h in heads]
        kn = [_dot(ckvn, wkv_ref[h, :, 0:HD]) for h in heads]
        vv = [_dot(ckvn, wkv_ref[h, :, HD:2 * HD]) for h in heads]
        kpe_ss = ssq(kpe)
        rq = [lax.rsqrt((ssq(qa[h]) + ssq(qr[h])) / QK + EPS) for h in heads]
        rk = [lax.rsqrt((ssq(kn[h]) + kpe_ss) / QK + EPS) for h in heads]
        q_rope = [rot(qr[h] * rq[h] * gq_ref[:, HD:QK]) for h in heads]
        k_rope = [rot(kpe * rk[h] * gk_ref[:, HD:QK]) for h in heads]
        for h in heads:
            q_ref[h, :, 0:HD] = (qa[h] * rq[h] * gq_ref[:, 0:HD] * (SCALE * LOG2E)).astype(BF16)
            q_ref[h, :, HD:QK] = (q_rope[h] * (SCALE * LOG2E)).astype(BF16)
            k_ref[h, :, 0:HD] = (kn[h] * rk[h] * gk_ref[:, 0:HD]).astype(BF16)
            k_ref[h, :, HD:QK] = k_rope[h].astype(BF16)
            v_ref[h] = vv[h].astype(BF16)

    whole = lambda a: pl.BlockSpec(a.shape, functools.partial(lambda i, nd: (0,) * nd, nd=a.ndim))
    return pl.pallas_call(
        body, name="mla_prep_fwd", grid=(t // tm,),
        in_specs=[pl.BlockSpec((tm, QL + KVL + ROPE), lambda i: (i, 0)), pl.BlockSpec((tm, ROPE), lambda i: (i, 0)),
                  pl.BlockSpec((tm, ROPE), lambda i: (i, 0))] + [whole(a) for a in (wq, wkv, gql, gkvl, gq, gk)],
        out_specs=[pl.BlockSpec((NH, tm, QK), lambda i: (0, i, 0)), pl.BlockSpec((NH, tm, QK), lambda i: (0, i, 0)),
                   pl.BlockSpec((NH, tm, HD), lambda i: (0, i, 0))],
        out_shape=[jax.ShapeDtypeStruct((NH, t, QK), BF16), jax.ShapeDtypeStruct((NH, t, QK), BF16),
                   jax.ShapeDtypeStruct((NH, t, HD), BF16)],
        compiler_params=_params(("parallel",)),
    )(p_mla, cs, sn, wq, wkv, gql, gkvl, gq, gk)


def _mla_prep_bwd(p_mla, cs, sn, wq, wkv, gql, gkvl, gq, gk, dq, dk, dv, t):
    tm = _pick(t, (512, 256, 128))

    def body(p_ref, cs_ref, sn_ref, wq_ref, wkv_ref, gql_ref, gkvl_ref, gq_ref, gk_ref,
             dq_ref, dk_ref, dv_ref,
             dp_ref, dwq_ref, dwkv_ref, dgql_ref, dgkvl_ref, dgq_ref, dgk_ref):
        accs = (dwq_ref, dwkv_ref, dgql_ref, dgkvl_ref, dgq_ref, dgk_ref)

        @pl.when(pl.program_id(0) == 0)
        def _():
            for r in accs:
                r[...] = jnp.zeros_like(r)

        rmat = _rope_mat()
        rmat_t = -rmat
        cq, ckv = p_ref[:, 0:QL], p_ref[:, QL:QL + KVL]
        kpe = p_ref[:, QL + KVL:QL + KVL + ROPE]
        cqn_f, ckvn_f = _rms(cq, gql_ref[...]), _rms(ckv, gkvl_ref[...])
        cqn, ckvn = cqn_f.astype(BF16), ckvn_f.astype(BF16)
        ckvn_t = ckvn_f.T.astype(BF16)
        c, s = cs_ref[...], sn_ref[...]
        unrot = lambda dy: dy * c + _sel_dot(dy * s, rmat_t)
        heads = range(NH)
        rsum = lambda x: jnp.sum(x, -1, keepdims=True)
        csum = lambda x: jnp.sum(x, axis=0, keepdims=True)
        qa = [_dot(cqn, wq_ref[h, :, 0:HD]) for h in heads]
        qr = [_dot(cqn, wq_ref[h, :, HD:QK]) for h in heads]
        kn = [_dot(ckvn, wkv_ref[h, :, 0:HD]) for h in heads]
        dyr = [unrot(dq_ref[h, :, HD:QK] * SCALE) for h in heads]
        dkr = [unrot(dk_ref[h, :, HD:QK]) for h in heads]
        rq = [lax.rsqrt((rsum(qa[h] * qa[h]) + rsum(qr[h] * qr[h])) / QK + EPS) for h in heads]
        xa, xr = [qa[h] * rq[h] for h in heads], [qr[h] * rq[h] for h in heads]
        dya = [dq_ref[h, :, 0:HD] * SCALE for h in heads]
        ga, gr_ = [dya[h] * gq_ref[:, 0:HD] for h in heads], [dyr[h] * gq_ref[:, HD:QK] for h in heads]
        mean = [(rsum(ga[h] * xa[h]) + rsum(gr_[h] * xr[h])) / QK for h in heads]
        dqa = [(rq[h] * (ga[h] - xa[h] * mean[h])).astype(BF16) for h in heads]
        dqr = [(rq[h] * (gr_[h] - xr[h] * mean[h])).astype(BF16) for h in heads]
        kpe_ss = rsum(kpe * kpe)
        rk = [lax.rsqrt((rsum(kn[h] * kn[h]) + kpe_ss) / QK + EPS) for h in heads]
        ya, yr = [kn[h] * rk[h] for h in heads], [kpe * rk[h] for h in heads]
        dka = [dk_ref[h, :, 0:HD] for h in heads]
        ha, hr = [dka[h] * gk_ref[:, 0:HD] for h in heads], [dkr[h] * gk_ref[:, HD:QK] for h in heads]
        mean_k = [(rsum(ha[h] * ya[h]) + rsum(hr[h] * yr[h])) / QK for h in heads]
        dkn = [(rk[h] * (ha[h] - ya[h] * mean_k[h])).astype(BF16) for h in heads]
        dvh = [dv_ref[h].astype(BF16) for h in heads]
        dw = [(_dot_tn(dqa[h], cqn), _dot_tn(dqr[h], cqn), _dot(ckvn_t, dkn[h]), _dot(ckvn_t, dvh[h])) for h in heads]
        back_q = [_dot_nt(dqa[h], wq_ref[h, :, 0:HD]) + _dot_nt(dqr[h], wq_ref[h, :, HD:QK]) for h in heads]
        back_kv = [_dot_nt(dkn[h], wkv_ref[h, :, 0:HD]) + _dot_nt(dvh[h], wkv_ref[h, :, HD:2 * HD]) for h in heads]
        dcqn, dckvn = sum(back_q[1:], back_q[0]), sum(back_kv[1:], back_kv[0])
        dkpe = sum([rk[h] * (hr[h] - yr[h] * mean_k[h]) for h in heads][1:], rk[0] * (hr[0] - yr[0] * mean_k[0]))
        dgq_a = sum([csum(dya[h] * xa[h]) for h in heads][1:], csum(dya[0] * xa[0]))
        dgq_r = sum([csum(dyr[h] * xr[h]) for h in heads][1:], csum(dyr[0] * xr[0]))
        dgk_a = sum([csum(dka[h] * ya[h]) for h in heads][1:], csum(dka[0] * ya[0]))
        dgk_r = sum([csum(dkr[h] * yr[h]) for h in heads][1:], csum(dkr[0] * yr[0]))
        for h in heads:
            dwq_ref[h, 0:HD, :] += dw[h][0]
            dwq_ref[h, HD:QK, :] += dw[h][1]
            dwkv_ref[h, :, 0:HD] += dw[h][2]
            dwkv_ref[h, :, HD:2 * HD] += dw[h][3]
        dcq, dg1 = _rms_bwd(cq, gql_ref[...], dcqn)
        dckv, dg2 = _rms_bwd(ckv, gkvl_ref[...], dckvn)
        dp_ref[:, 0:QL] = dcq.astype(BF16)
        dp_ref[:, QL:QL + KVL] = dckv.astype(BF16)
        dp_ref[:, QL + KVL:QL + KVL + ROPE] = dkpe.astype(BF16)
        dgql_ref[...] += dg1
        dgkvl_ref[...] += dg2
        dgq_ref[:, 0:HD] += dgq_a
        dgq_ref[:, HD:QK] += dgq_r
        dgk_ref[:, 0:HD] += dgk_a
        dgk_ref[:, HD:QK] += dgk_r

    whole = lambda a: pl.BlockSpec(a.shape, functools.partial(lambda i, nd: (0,) * nd, nd=a.ndim))
    acc_shapes = [(NH, QK, QL), wkv.shape, gql.shape, gkvl.shape, gq.shape, gk.shape]
    return pl.pallas_call(
        body, name="mla_prep_bwd", grid=(t // tm,),
        in_specs=[pl.BlockSpec((tm, QL + KVL + ROPE), lambda i: (i, 0)), pl.BlockSpec((tm, ROPE), lambda i: (i, 0)),
                  pl.BlockSpec((tm, ROPE), lambda i: (i, 0))]
        + [whole(a) for a in (wq, wkv, gql, gkvl, gq, gk)]
        + [pl.BlockSpec((NH, tm, QK), lambda i: (0, i, 0)), pl.BlockSpec((NH, tm, QK), lambda i: (0, i, 0)),
           pl.BlockSpec((NH, tm, HD), lambda i: (0, i, 0))],
        out_specs=[pl.BlockSpec((tm, QL + KVL + ROPE), lambda i: (i, 0))]
        + [pl.BlockSpec(s, functools.partial(lambda i, nd: (0,) * nd, nd=len(s))) for s in acc_shapes],
        out_shape=[jax.ShapeDtypeStruct((t, QL + KVL + ROPE), BF16)]
        + [jax.ShapeDtypeStruct(s, F32) for s in acc_shapes],
        compiler_params=_params(("arbitrary",)),
    )(p_mla, cs, sn, wq, wkv, gql, gkvl, gq, gk, dq, dk, dv)


def _chunk_mask(nq, nk, key0, keys_on_rows):
    shape = (nk, nq) if keys_on_rows else (nq, nk)
    qi = lax.broadcasted_iota(jnp.int32, shape, 1 if keys_on_rows else 0) // CHUNK
    ki = lax.broadcasted_iota(jnp.int32, shape, 0 if keys_on_rows else 1) // CHUNK + key0 // CHUNK
    return ki <= qi


def _flash_fwd(q, k, v, t):
    tq = _pick(t, ATT_Q)
    tk = tq // ATT_KEY_TILES

    def body(q_ref, k_ref, v_ref, o_ref, lse_ref):
        i = pl.program_id(1)
        qt = q_ref[0]

        def step(j, carry, key0):
            m, l, acc = carry
            cols = pl.ds(pl.multiple_of(j * tk, tk), tk)
            s = _dot_nt(qt, k_ref[0, cols, :])
            if key0 is not None:
                s = jnp.where(_chunk_mask(tq, tk, key0, False), s, -jnp.inf)
            m_new = jnp.maximum(m, jnp.max(s, axis=-1, keepdims=True))
            p = jnp.exp2(s - m_new)
            alpha = jnp.exp2(m - m_new)
            return m_new, alpha * l + jnp.sum(p, axis=-1, keepdims=True), alpha * acc + _dot(p, v_ref[0, cols, :])

        init = (jnp.full((tq, 1), -jnp.inf, F32), jnp.zeros((tq, 1), F32), jnp.zeros((tq, HD), F32))
        carry = lax.fori_loop(0, ATT_KEY_TILES * i, lambda j, cr: step(j, cr, None), init)
        for h in range(ATT_KEY_TILES):
            carry = step(ATT_KEY_TILES * i + h, carry, h * tk)
        m, l, acc = carry
        o_ref[...] = (acc / l).astype(BF16)
        x1, x2, x3 = _split3(jnp.broadcast_to((m + jnp.log2(l)) * (1.0 / HD), (tq, HD)))
        ones = jnp.ones((SUBLANE, HD), BF16)
        rowsum = lambda p: lax.dot_general(ones, p, (((1,), (1,)), ((), ())), preferred_element_type=F32)
        lse_ref[0] = (rowsum(x1) + rowsum(x2) + rowsum(x3))[0:1, :]

    return pl.pallas_call(
        body, name="flash_fwd", grid=(NH, t // tq),
        in_specs=[pl.BlockSpec((1, tq, QK), lambda h, i: (h, i, 0)), pl.BlockSpec((1, t, QK), lambda h, i: (h, 0, 0)),
                  pl.BlockSpec((1, t, HD), lambda h, i: (h, 0, 0))],
        out_specs=[pl.BlockSpec((tq, HD), lambda h, i: (i, h)), pl.BlockSpec((1, 1, tq), lambda h, i: (h, 0, i))],
        out_shape=[jax.ShapeDtypeStruct((t, NH * HD), BF16), jax.ShapeDtypeStruct((NH, 1, t), F32)],
        compiler_params=_params(("parallel", "parallel")),
    )(q, k, v)


def _attn_out_bwd(dy, w, o, t):
    tm = _pick(t, ATT_TILES)

    def body(dy_ref, w_ref, o_ref, do_ref, delta_ref):
        do = _dot_nt(dy_ref[...], w_ref[...]).astype(BF16)
        do_ref[...] = do
        ones = jnp.ones((SUBLANE, HD), BF16)
        rowsum = lambda p: lax.dot_general(ones, p, (((1,), (1,)), ((), ())), preferred_element_type=F32)
        sl = [slice(HD * h, HD * (h + 1)) for h in range(NH)]
        parts = [_split3(do[:, sl[h]].astype(F32) * o_ref[:, sl[h]].astype(F32)) for h in range(NH)]
        sums = [[rowsum(x) for x in parts[h]] for h in range(NH)]
        for h in range(NH):
            delta_ref[h] = (sums[h][0] + sums[h][1] + sums[h][2])[0:1, :]

    return pl.pallas_call(
        body, name="attn_out_bwd", grid=(t // tm,),
        in_specs=[_BS((tm, dy.shape[1]), lambda i: (i, 0)), _BS(w.shape, lambda i: (0, 0)),
                  _BS((tm, NH * HD), lambda i: (i, 0))],
        out_specs=[_BS((tm, NH * HD), lambda i: (i, 0)), _BS((NH, 1, tm), lambda i: (0, 0, i))],
        out_shape=[jax.ShapeDtypeStruct((t, NH * HD), BF16), jax.ShapeDtypeStruct((NH, 1, t), F32)],
        compiler_params=_params(("parallel",)),
    )(dy, w, o)


def _flash_bwd(q, k, v, do, lse_row, delta_row, t):
    tq = _pick(t, ATT_Q)
    tk = tq // ATT_KEY_TILES

    def body(q_ref, k_ref, v_ref, do_ref, lse_ref, delta_ref, dq_ref, dk_ref, dv_ref):
        j = pl.program_id(1)

        @pl.when(j == 0)
        def _():
            dq_ref[...] = jnp.zeros_like(dq_ref)

        kt, vt = k_ref[0], v_ref[0]

        def step(i, carry, key0):
            dk, dv = carry
            rows = pl.ds(pl.multiple_of(i * tq, tq), tq)
            qt, dot_ = q_ref[0, rows, :], do_ref[rows, :]
            p = jnp.exp2(_dot_nt(kt, qt) - lse_ref[0, :, rows])
            if key0 is not None:
                p = jnp.where(_chunk_mask(tq, tk, key0, True), p, 0.0)
            ds = (p * (_dot_nt(vt, dot_) - delta_ref[0, :, rows])).astype(BF16)
            dq_ref[0, rows, :] += _dot_tn(ds, kt)
            return dk + _dot(ds, qt), dv + _dot(p, dot_)

        own = j // ATT_KEY_TILES
        carry = step(own, (jnp.zeros((tk, QK), F32), jnp.zeros((tk, HD), F32)), (j % ATT_KEY_TILES) * tk)
        dk, dv = lax.fori_loop(own + 1, t // tq, lambda i, cr: step(i, cr, None), carry)
        dk_ref[0] = dk * (1.0 / LOG2E)
        dv_ref[0] = dv

    return pl.pallas_call(
        body, name="flash_bwd", grid=(NH, t // tk),
        in_specs=[pl.BlockSpec((1, t, QK), lambda h, j: (h, 0, 0)), pl.BlockSpec((1, tk, QK), lambda h, j: (h, j, 0)),
                  pl.BlockSpec((1, tk, HD), lambda h, j: (h, j, 0)), pl.BlockSpec((t, HD), lambda h, j: (0, h)),
                  pl.BlockSpec((1, 1, t), lambda h, j: (h, 0, 0)), pl.BlockSpec((1, 1, t), lambda h, j: (h, 0, 0))],
        out_specs=[pl.BlockSpec((1, t, QK), lambda h, j: (h, 0, 0)), pl.BlockSpec((1, tk, QK), lambda h, j: (h, j, 0)),
                   pl.BlockSpec((1, tk, HD), lambda h, j: (h, j, 0))],
        out_shape=[jax.ShapeDtypeStruct((NH, t, QK), F32), jax.ShapeDtypeStruct((NH, t, QK), F32),
                   jax.ShapeDtypeStruct((NH, t, HD), F32)],
        compiler_params=_params(("parallel", "arbitrary")),
    )(q, k, v, do, lse_row, delta_row)


def _ffn_in(xn, w_in, name):
    t, k = xn.shape
    s, _, n = w_in.shape
    tm = _pick(t, ROW_TILES)

    def body(x_ref, w_ref, gu_ref, a_ref):
        x = x_ref[...]
        g, u = _dot(x, w_ref[0]), _dot(x, w_ref[1])
        gu_ref[0] = g.astype(BF16)
        gu_ref[1] = u.astype(BF16)
        a_ref[...] = (g * _sig(g) * u).astype(BF16)

    return pl.pallas_call(
        body, name=name, grid=(t // tm, s // 2),
        in_specs=[_BS((tm, k), lambda i, j: (i, 0)), _BS((2, None, k, n), lambda i, j: (0, j, 0, 0))],
        out_specs=[_BS((2, None, tm, n), lambda i, j: (0, j, i, 0)), _BS((None, tm, n), lambda i, j: (j, i, 0))],
        out_shape=[jax.ShapeDtypeStruct((2, s // 2, t, n), BF16), jax.ShapeDtypeStruct((s // 2, t, n), BF16)],
        compiler_params=_params(("parallel", "parallel")),
    )(xn, w_in.reshape(2, s // 2, k, n))


def _ffn_dgu(dfo, w_out, gu, name):
    t, k = dfo.shape
    s, n, _ = w_out.shape
    tm = _pick(t, ROW_TILES)

    def body(d_ref, w_ref, gu_ref, o_ref):
        da = _dot_nt(d_ref[...], w_ref[...])
        g, u = gu_ref[0].astype(F32), gu_ref[1].astype(F32)
        sg = _sig(g)
        o_ref[0] = (da * u * sg * (1.0 + g * (1.0 - sg))).astype(BF16)
        o_ref[1] = (da * g * sg).astype(BF16)

    pair = _BS((2, None, tm, n), lambda i, j: (0, j, i, 0))
    return pl.pallas_call(
        body, name=name, grid=(t // tm, s),
        in_specs=[_BS((tm, k), lambda i, j: (i, 0)), _BS((None, n, k), lambda i, j: (j, 0, 0)), pair],
        out_specs=pair, out_shape=jax.ShapeDtypeStruct((2, s, t, n), BF16),
        compiler_params=_params(("parallel", "parallel")),
    )(dfo, w_out, gu).reshape(2 * s, t, n)


def _ffn_fwd(xn, w_in, w_out, tag, epilogue):
    gu, a = _ffn_in(xn, w_in, tag + "_in")
    return gu, a, _mm_stack_red(a, w_out, tag + "_out", "nn", epilogue)


def _ffn_bwd(dfo, xn, gu, a, w_in, w_out, tag, epilogue):
    dw_out = _mm_stack_tn(a, dfo, tag + "_dwout")
    dgu = _ffn_dgu(dfo, w_out, gu, tag + "_dgu")
    dw_in = _mm_stack_tn(dgu, xn, tag + "_dwin")
    if callable(epilogue):
        epilogue = epilogue(dw_in, dw_out)
    return _mm_stack_red(dgu, w_in, tag + "_dxn", "nt", epilogue), dw_in, dw_out


def _local_step(x, target, cs, sn, w, late_weights, s, early_grads, last_grads):
    t = x.shape[0]
    tm = _pick(t, (512, 256, 128))
    g = {}
    ffn_out = lambda n: w[n].reshape(4, FF // 4, D)

    def norm_fn(x_ref, g_ref, o_ref):
        o_ref[...] = _rms(x_ref[...], g_ref[...]).astype(BF16)

    xn1 = _rows(norm_fn, "norm1", t, tm, [(x, D, 0)], [s["ffn1_norm"]], [(D, BF16)])[0]
    def res_norm_fn(scale):
        def fn(h_ref, f_ref, g_ref, h_out, n_out):
            h = h_ref[...] + scale * f_ref[...]
            h_out[...] = h
            n_out[...] = _rms(h, g_ref[...]).astype(BF16)
        return fn

    gu1, a1, (h1, u) = _ffn_fwd(xn1, w["ffn1_w_in"], ffn_out("ffn1_w_out"), "ffn1",
                                (res_norm_fn(0.5), x, [], [s["mix_norm"]], [(D, F32), (D, BF16)], []))
    w = {**w, **late_weights(h1)}
    rows_of = lambda n: w[n].reshape(-1, w[n].shape[2])
    w_hgb, w_mlab, w_o = rows_of("w_hg_branch"), rows_of("w_mla_branch"), rows_of("w_out")
    w_in_nat = w["w_in"].transpose(1, 0, 2).reshape(D, -1)
    w_mrg = w["w_merge"]
    mw = w_mrg.shape[2]
    w_in_hg, w_in_mla = w_in_nat[:, :4 * HGW], w_in_nat[:, 4 * HGW:]
    p_hg = _mm(u, w_in_hg, "nn", "proj_hg")
    p_mla = _mm(u, w_in_mla, "nn", "proj_mla")
    gpre = _cols_fwd(u, w_mrg, "proj_gate")
    o_pre, hgy, states = _hgrn_fwd(p_hg, s["hg_lb_table"], s["hg_out_norm"], t)
    prep_args = (p_mla, cs, sn, w["w_q_up"], w["w_kv_up"], s["mla_q_lora_norm"], s["mla_kv_lora_norm"],
                 s["q_head_norm"], s["k_head_norm"])
    q, k, v = _mla_prep_fwd(*prep_args, t)
    att, lse = _flash_fwd(q, k, v, t)
    y_hg = _mm(hgy, w_hgb, "nn", "branch_hg")
    def mix_fn(gp_ref, ym_ref, yh_ref, b_ref, o_ref, ym_out):
        gh = _sig(gp_ref[:, 0:D] + b_ref[:, 0:D])
        gm = _sig(gp_ref[:, D:2 * D] + b_ref[:, D:2 * D])
        ym = ym_ref[...]
        ym_out[...] = ym
        o_ref[...] = (gh * yh_ref[...] + gm * ym).astype(BF16)

    mixed, y_mla = _mm_stack_red(att[None], w_mlab[None], "branch_mla", "nn",
                                 (mix_fn, gpre, [y_hg], [s["b_merge"]], [(D, BF16), (D, F32)], [], (512, 256, 128)))
    h2, xn2 = _mm_stack_red(mixed[None], w_o[None], "mix_out", "nn",
                            (res_norm_fn(1.0), h1, [], [s["ffn2_norm"]], [(D, F32), (D, BF16)], []))

    def loss_fn(h_ref, f_ref, tg_ref, g_ref, dh_out, dhb_out):
        h = h_ref[...] + 0.5 * f_ref[...]
        e = _rms(h, g_ref[...]) - tg_ref[...]
        dh, dgain = _rms_bwd(h, g_ref[...], e / D)
        dh_out[...] = dh
        dhb_out[...] = (0.5 * dh).astype(BF16)
        return dgain, jnp.full((1, LANE), 0.5 / D * jnp.sum(e * e), F32)

    gu2, a2, (dh3, dfo2, g["final_norm"], loss) = _ffn_fwd(
        xn2, w["ffn2_w_in"], ffn_out("ffn2_w_out"), "ffn2",
        (loss_fn, h2, [target], [s["final_norm"]], [(D, F32), (D, BF16)], [(1, D), (1, LANE)]))

    def norm_bwd_fn(scale):
        def fn(h_ref, dxn_ref, dh_ref, g_ref, dh_out, dhb_out):
            dx, dgain = _rms_bwd(h_ref[...], g_ref[...], dxn_ref[...])
            dh = dh_ref[...] + dx
            dh_out[...] = dh
            dhb_out[...] = (scale * dh).astype(BF16)
            return (dgain,)
        return fn

    as_rows = lambda a: a.reshape((N_DEV, -1) + a.shape[-1:])
    (dh2, dh2b, g["ffn2_norm"]), g["ffn2_w_in"], dwo = _ffn_bwd(
        dfo2, xn2, gu2, a2, w["ffn2_w_in"], ffn_out("ffn2_w_out"), "ffn2",
        (norm_bwd_fn(1.0), h2, [dh3], [s["ffn2_norm"]], [(D, F32), (D, BF16)], [(1, D)]))
    g["ffn2_w_out"] = as_rows(dwo)
    g["w_out"] = as_rows(_mm(mixed, dh2b, "tn", "mix_out_dw"))

    def mix_bwd_fn(gp_ref, dm_ref, yh_ref, ym_ref, b_ref, dyh_out, dym_out, dg_out):
        gh = _sig(gp_ref[:, 0:D] + b_ref[:, 0:D])
        gm = _sig(gp_ref[:, D:2 * D] + b_ref[:, D:2 * D])
        dm = dm_ref[...]
        dyh_out[...] = (dm * gh).astype(BF16)
        dym_out[...] = (dm * gm).astype(BF16)
        dgh = dm * yh_ref[...] * gh * (1.0 - gh)
        dgm = dm * ym_ref[...] * gm * (1.0 - gm)
        dg_out[:, 0:D] = dgh.astype(BF16)
        dg_out[:, D:2 * D] = dgm.astype(BF16)
        return (jnp.concatenate([jnp.sum(dgh, axis=0, keepdims=True), jnp.sum(dgm, axis=0, keepdims=True)], axis=1),)

    dyh, dym, dgpre, g["b_merge"] = _mm_stack_red(
        dh2b[None], w_o[None], "mix_out_dx", "nt",
        (mix_bwd_fn, gpre, [y_hg, y_mla], [s["b_merge"]], [(D, BF16), (D, BF16), (2 * D, BF16)], [(1, 2 * D)],
         (512, 256, 128)))
    g["w_hg_branch"] = as_rows(_mm(hgy, dyh, "tn", "branch_hg_dw"))
    g["w_mla_branch"] = as_rows(_mm(att, dym, "tn", "branch_mla_dw"))
    g["w_merge"] = _cols_dw(u, dgpre, mw, "proj_gate_dw")
    dhgy = _mm(dyh, w_hgb, "nt", "branch_hg_dx")
    datt, delta = _attn_out_bwd(dym, w_mlab, att, t)
    du_gate = _cols_dx(dgpre, w_mrg, "proj_gate_dx")

    dq, dk, dv = _flash_bwd(q, k, v, datt, lse, delta, t)
    (dp_mla, dwq, dwkv, g["mla_q_lora_norm"], g["mla_kv_lora_norm"], g["q_head_norm"],
     g["k_head_norm"]) = _mla_prep_bwd(*prep_args, dq, dk, dv, t)
    g["w_q_up"], g["w_kv_up"] = dwq.astype(GRAD_DT), dwkv.astype(GRAD_DT)
    dp_hg, g["hg_lb_table"], g["hg_out_norm"] = _hgrn_bwd(p_hg, s["hg_lb_table"], s["hg_out_norm"], o_pre, states,
                                                          dhgy, t)
    dw_in_t = jnp.concatenate([_mm(dp_hg, u, "tn", "proj_hg_dw"), _mm(dp_mla, u, "tn", "proj_mla_dw")], axis=0)
    g["w_in"] = dw_in_t.reshape(N_DEV, -1, D)
    du_mla = _mm(dp_mla, w_in_mla, "nt", "proj_mla_dx")

    def mixnorm_bwd_fn(h_ref, a_ref, b_ref, c_ref, dh_ref, g_ref, dh_out, dhb_out):
        dx, dgain = _rms_bwd(h_ref[...], g_ref[...], a_ref[...] + b_ref[...] + c_ref[...])
        dh = dh_ref[...] + dx
        dh_out[...] = dh
        dhb_out[...] = (0.5 * dh).astype(BF16)
        return (dgain,)

    mix_gain = s["mix_norm"] + early_grads(g)[0:1, 0:1]
    dh1, dfo1, g["mix_norm"] = _mm_stack_red(
        dp_hg[None], w_in_hg[None], "proj_hg_dx", "nt",
        (mixnorm_bwd_fn, h1, [du_mla, du_gate, dh2], [mix_gain], [(D, F32), (D, BF16)], [(1, D)], (512, 256, 128)))
    def last_stage(dw_in, dw_out):
        g["ffn1_w_in"], g["ffn1_w_out"] = dw_in, as_rows(dw_out)
        gain = s["ffn1_norm"] + last_grads(g)[0:1, 0:1]
        return norm_bwd_fn(1.0), x, [dh1], [gain], [(D, F32), (D, BF16)], [(1, D)]

    (grad_x, _, g["ffn1_norm"]), _, _ = _ffn_bwd(dfo1, xn1, gu1, a1, w["ffn1_w_in"], ffn_out("ffn1_w_out"), "ffn1",
                                                 last_stage)
    return loss, grad_x, g


def _coords():
    return lax.axis_index("x"), lax.axis_index("y"), lax.axis_index("c")


def _hbm_call(body, name, ins, out_shapes, scratch):
    any_spec = pl.BlockSpec(memory_space=pl.ANY)
    return pl.pallas_call(
        body, name=name, out_shape=[jax.ShapeDtypeStruct(s, dt) for s, dt in out_shapes],
        in_specs=[any_spec] * len(ins), out_specs=[any_spec] * len(out_shapes), scratch_shapes=scratch,
    )(*ins)


def _my_slot():
    return 4 * lax.axis_index("x") + 2 * lax.axis_index("y") + lax.axis_index("c")


def _put_own(buf, own, index):
    return lax.dynamic_update_index_in_dim(buf, own, index, 0)


def _all_gather(blocks, name):
    nb = len(blocks)

    def body(*refs):
        x_refs, out_refs = refs[:nb], refs[nb:2 * nb]
        send_sems, recv_sems = refs[2 * nb:]
        x, y, c = _coords()
        me, sibling = (x, y, c), (x, y, 1 - c)
        chips = [(1 - x, y), (x, 1 - y), (1 - x, 1 - y)]

        def slot(b, px, py, pc):
            return out_refs[b].at[4 * px + 2 * py + pc]

        def copy(b, kk, block_of, to, src=None):
            return pltpu.make_async_remote_copy(
                src_ref=slot(b, *block_of) if src is None else src, dst_ref=slot(b, *block_of),
                send_sem=send_sems.at[b, kk], recv_sem=recv_sems.at[b, kk], device_id=to, device_id_type=MESH)

        first = [copy(b, 0, me, sibling, src=x_refs[b]) for b in range(nb)]
        first += [copy(b, 1 + j, me, (*chip, c), src=x_refs[b]) for j, chip in enumerate(chips) for b in range(nb)]
        for cp in first:
            cp.start()
        passed = []
        for j, chip in enumerate(chips):
            for b in range(nb):
                copy(b, 1 + j, (*chip, c), me).wait_recv()
                passed.append(copy(b, 4 + j, (*chip, c), sibling))
                passed[-1].start()
        for b in range(nb):
            copy(b, 0, sibling, me).wait_recv()
        for j, chip in enumerate(chips):
            for b in range(nb):
                copy(b, 4 + j, (*chip, 1 - c), me).wait_recv()
        for cp in first + passed:
            cp.wait_send()

    outs = _hbm_call(body, name, blocks, [((N_DEV,) + b.shape, b.dtype) for b in blocks],
                     [pltpu.SemaphoreType.DMA((nb, 7)), pltpu.SemaphoreType.DMA((nb, 7))])
    return [_put_own(o, b[None], _my_slot()) for o, b in zip(outs, blocks)]


def _gather_peers():
    x, y, c = _coords()
    return (x, y, c), [(x, y, 1 - c), (1 - x, y, c), (x, 1 - y, c), (1 - x, 1 - y, c)]


def _gather_start(blocks, after, name):
    nb = len(blocks)
    hbm, sem = pl.BlockSpec(memory_space=pltpu.HBM), pl.BlockSpec(memory_space=pltpu.SEMAPHORE)

    def body(*refs):
        x_refs, out_refs = refs[:nb], refs[nb:2 * nb]
        send_sems, recv_sems, token = refs[2 * nb + 1], refs[2 * nb + 2], refs[-1]
        (x, y, c), peers = _gather_peers()
        for kk, peer in enumerate(peers):
            for b in range(nb):
                pltpu.make_async_remote_copy(
                    src_ref=x_refs[b], dst_ref=out_refs[b].at[4 * x + 2 * y + c], send_sem=send_sems.at[4 * b + kk],
                    recv_sem=recv_sems.at[4 * b + kk], device_id=peer, device_id_type=MESH).start()
        token[...] = jnp.zeros_like(token)

    gathers = [pltpu.with_memory_space_constraint(lax.empty((N_DEV,) + b.shape, b.dtype), pltpu.HBM) for b in blocks]
    outs = pl.pallas_call(
        body, name=name,
        out_shape=(pltpu.SemaphoreType.DMA((4 * nb,)), pltpu.SemaphoreType.DMA((4 * nb,)),
                   *[pltpu.HBM(b.shape, b.dtype) for b in blocks], *[pltpu.HBM(b.shape, b.dtype) for b in gathers],
                   jax.ShapeDtypeStruct((SUBLANE, LANE), F32)),
        in_specs=[hbm] * (2 * nb) + [pl.BlockSpec(memory_space=pl.ANY)],
        out_specs=(sem, sem, *[hbm] * (2 * nb), pl.BlockSpec(memory_space=pltpu.VMEM)),
        input_output_aliases={i: 2 + i for i in range(2 * nb)},
        compiler_params=pltpu.CompilerParams(has_side_effects=pltpu.SideEffectType.DATAFLOW_SIDE_EFFECTING),
    )(*[pltpu.with_memory_space_constraint(b, pltpu.HBM) for b in blocks], *gathers, after)
    return outs[0], outs[1], list(outs[2:2 + nb]), list(outs[2 + nb:2 + 2 * nb]), outs[-1]


def _gather_wait(send_sems, recv_sems, thru, gathers, after, name):
    nb = len(thru)
    hbm, sem = pl.BlockSpec(memory_space=pltpu.HBM), pl.BlockSpec(memory_space=pltpu.SEMAPHORE)

    def body(*refs):
        x_refs, out_refs = refs[:nb], refs[nb:2 * nb]
        send_sems_, recv_sems_ = refs[2 * nb], refs[2 * nb + 1]
        _, peers = _gather_peers()
        for kk, (px, py, pc) in enumerate(peers):
            for b in range(nb):
                cp = pltpu.make_async_remote_copy(
                    src_ref=x_refs[b], dst_ref=out_refs[b].at[4 * px + 2 * py + pc], send_sem=send_sems_.at[4 * b + kk],
                    recv_sem=recv_sems_.at[4 * b + kk], device_id=(px, py, pc), device_id_type=MESH)
                cp.wait_send()
                cp.wait_recv()

    outs = pl.pallas_call(
        body, name=name,
        out_shape=(*[pltpu.HBM(b.shape, b.dtype) for b in thru], *[pltpu.HBM(b.shape, b.dtype) for b in gathers]),
        in_specs=[hbm] * (2 * nb) + [sem, sem, pl.BlockSpec(memory_space=pl.ANY)], out_specs=[hbm] * (2 * nb),
        input_output_aliases={i: i for i in range(2 * nb)},
        compiler_params=pltpu.CompilerParams(has_side_effects=pltpu.SideEffectType.DATAFLOW_SIDE_EFFECTING),
    )(*thru, *gathers, send_sems, recv_sems, after)
    return list(outs[:nb]), list(outs[nb:])


def _gather_finish(blocks, gathers, name):
    nb = len(blocks)

    def body(*refs):
        x_refs, in_refs, out_refs = refs[:nb], refs[nb:2 * nb], refs[2 * nb:3 * nb]
        send_sems, recv_sems = refs[3 * nb:]
        (x, y, c), peers = _gather_peers()
        copies = []
        for j, (px, py, _) in enumerate(peers[1:]):
            for b in range(nb):
                copies.append(pltpu.make_async_remote_copy(
                    src_ref=in_refs[b].at[4 * px + 2 * py + c], dst_ref=out_refs[b].at[4 * px + 2 * py + c],
                    send_sem=send_sems.at[b, j], recv_sem=recv_sems.at[b, j], device_id=(x, y, 1 - c),
                    device_id_type=MESH))
                copies[-1].start()
        for j, (px, py, _) in enumerate(peers[1:]):
            for b in range(nb):
                pltpu.make_async_remote_copy(
                    src_ref=in_refs[b].at[4 * px + 2 * py + c], dst_ref=out_refs[b].at[4 * px + 2 * py + 1 - c],
                    send_sem=send_sems.at[b, j], recv_sem=recv_sems.at[b, j], device_id=(x, y, 1 - c),
                    device_id_type=MESH).wait_recv()
        for cp in copies:
            cp.wait_send()

    any_spec = pl.BlockSpec(memory_space=pl.ANY)
    outs = pl.pallas_call(
        body, name=name, out_shape=[jax.ShapeDtypeStruct(b.shape, b.dtype) for b in gathers],
        in_specs=[any_spec] * (2 * nb), out_specs=[any_spec] * nb,
        input_output_aliases={nb + i: i for i in range(nb)},
        scratch_shapes=[pltpu.SemaphoreType.DMA((nb, 3)), pltpu.SemaphoreType.DMA((nb, 3))],
    )(*blocks, *gathers)
    return [_put_own(o, b[None], _my_slot()) for o, b in zip(outs, blocks)]


def _sibling_swap(bufs, name):
    nb = len(bufs)

    def body(*refs):
        x_refs, out_refs = refs[:nb], refs[nb:2 * nb]
        send_sems, recv_sems = refs[2 * nb:]
        x, y, c = _coords()
        copies = [pltpu.make_async_remote_copy(
            src_ref=x_refs[b].at[2 * q + 1 - c], dst_ref=out_refs[b].at[q], send_sem=send_sems.at[b, q],
            recv_sem=recv_sems.at[b, q], device_id=(x, y, 1 - c), device_id_type=MESH)
            for b in range(nb) for q in range(4)]
        for cp in copies:
            cp.start()
        for cp in copies:
            cp.wait()

    return _hbm_call(body, name, bufs, [((4,) + b.shape[1:], b.dtype) for b in bufs],
                     [pltpu.SemaphoreType.DMA((nb, 4)), pltpu.SemaphoreType.DMA((nb, 4))])


def _chip_exchange_start(bufs, name):
    nb = len(bufs)
    hbm, sem = pl.BlockSpec(memory_space=pltpu.HBM), pl.BlockSpec(memory_space=pltpu.SEMAPHORE)

    def body(*refs):
        x_refs, land_refs = refs[:nb], refs[nb:2 * nb]
        send_sems, recv_sems, token = refs[2 * nb], refs[2 * nb + 1], refs[-1]
        x, y, c = _coords()
        for j, (px, py) in enumerate([(1 - x, y), (x, 1 - y), (1 - x, 1 - y)]):
            for b in range(nb):
                pltpu.make_async_remote_copy(
                    src_ref=x_refs[b].at[2 * px + py], dst_ref=land_refs[b].at[2 * x + y], send_sem=send_sems.at[3 * b + j],
                    recv_sem=recv_sems.at[3 * b + j], device_id=(px, py, c), device_id_type=MESH).start()
        token[...] = jnp.zeros_like(token)

    lands = [pltpu.with_memory_space_constraint(lax.empty(b.shape, b.dtype), pltpu.HBM) for b in bufs]
    outs = pl.pallas_call(
        body, name=name,
        out_shape=(pltpu.SemaphoreType.DMA((3 * nb,)), pltpu.SemaphoreType.DMA((3 * nb,)),
                   *[pltpu.HBM(b.shape, b.dtype) for b in bufs], *[pltpu.HBM(b.shape, b.dtype) for b in bufs],
                   jax.ShapeDtypeStruct((SUBLANE, LANE), F32)),
        in_specs=[hbm] * (2 * nb), out_specs=(sem, sem, *[hbm] * (2 * nb), pl.BlockSpec(memory_space=pltpu.VMEM)),
        input_output_aliases={i: 2 + i for i in range(2 * nb)},
        compiler_params=pltpu.CompilerParams(has_side_effects=pltpu.SideEffectType.DATAFLOW_SIDE_EFFECTING),
    )(*[pltpu.with_memory_space_constraint(b, pltpu.HBM) for b in bufs], *lands)
    return outs[0], outs[1], list(outs[2:2 + nb]), list(outs[2 + nb:2 + 2 * nb]), outs[-1]


def _chip_exchange_wait(send_sems, recv_sems, thru, lands, after, name):
    nb = len(thru)
    hbm, sem = pl.BlockSpec(memory_space=pltpu.HBM), pl.BlockSpec(memory_space=pltpu.SEMAPHORE)

    def body(*refs):
        x_refs, land_refs = refs[:nb], refs[nb:2 * nb]
        send_sems_, recv_sems_ = refs[2 * nb], refs[2 * nb + 1]
        x, y, c = _coords()
        for j, (px, py) in enumerate([(1 - x, y), (x, 1 - y), (1 - x, 1 - y)]):
            for b in range(nb):
                cp = pltpu.make_async_remote_copy(
                    src_ref=x_refs[b].at[2 * px + py], dst_ref=land_refs[b].at[2 * px + py],
                    send_sem=send_sems_.at[3 * b + j], recv_sem=recv_sems_.at[3 * b + j], device_id=(px, py, c),
                    device_id_type=MESH)
                cp.wait_send()
                cp.wait_recv()

    outs = pl.pallas_call(
        body, name=name,
        out_shape=(*[pltpu.HBM(b.shape, b.dtype) for b in thru], *[pltpu.HBM(b.shape, b.dtype) for b in lands]),
        in_specs=[hbm] * (2 * nb) + [sem, sem, pl.BlockSpec(memory_space=pl.ANY)], out_specs=[hbm] * (2 * nb),
        input_output_aliases={i: i for i in range(2 * nb)},
        compiler_params=pltpu.CompilerParams(has_side_effects=pltpu.SideEffectType.DATAFLOW_SIDE_EFFECTING),
    )(*thru, *lands, send_sems, recv_sems, after)
    return list(outs[:nb]), list(outs[nb:])


def _chip_sum(g, r1, c, name):
    _, r, cw = g.shape
    tr = _pick(r, (256, 176, 128))

    def body(c_ref, g_ref, r_ref, o_ref):
        o_ref[...] = (g_ref[...].astype(F32) + r_ref[...].astype(F32)).astype(GRAD_DT)

    grid_spec = pltpu.PrefetchScalarGridSpec(
        num_scalar_prefetch=1, grid=(4, r // tr),
        in_specs=[_BS((None, None, tr, cw), lambda q, i, c_ref: (q, c_ref[0], i, 0)),
                  _BS((None, tr, cw), lambda q, i, c_ref: (q, i, 0))],
        out_specs=_BS((None, tr, cw), lambda q, i, c_ref: (q, i, 0)))
    return pl.pallas_call(
        body, name=name, grid_spec=grid_spec, out_shape=jax.ShapeDtypeStruct((4, r, cw), GRAD_DT),
        compiler_params=_params(("parallel", "parallel")),
    )(c.reshape(1).astype(jnp.int32), g.reshape(4, 2, r, cw), r1)


def _adamw_math(w, g, m, v):
    m = B1 * m + (1.0 - B1) * g
    v = B2 * v + (1.0 - B2) * (g * g)
    m_hat = m / (1.0 - B1 ** STEP)
    v_hat = v / (1.0 - B2 ** STEP)
    return -LR * (m_hat / (jnp.sqrt(v_hat) + AEPS) + WD * w), m, v


def _sum_adamw(parts, w, m, v, name):
    r, c = w.shape
    tr = _pick(r, (256, 176, 128))
    tc = _pick(c, (256, 128)) if tr == r and r > 256 else c

    def body(p0, p1, p2, p3, w_ref, m_ref, v_ref, g_out, d_out, m_out, v_out):
        g = ((p0[...].astype(F32) + p1[...].astype(F32)) + p2[...].astype(F32)) + p3[...].astype(F32)
        g_out[...] = g
        d_out[...], m_out[...], v_out[...] = _adamw_math(w_ref[...], g, m_ref[...], v_ref[...])

    part = lambda q: _BS((None, tr, tc), functools.partial(lambda i, j, q: (q, i, j), q=q))
    plain = _BS((tr, tc), lambda i, j: (i, j))
    return pl.pallas_call(
        body, name=name, grid=(r // tr, c // tc), in_specs=[part(q) for q in range(4)] + [plain] * 3,
        out_specs=[plain] * 4, out_shape=[jax.ShapeDtypeStruct((r, c), F32)] * 4,
        compiler_params=_params(("parallel", "parallel")),
    )(parts, parts, parts, parts, w, m, v)


def _small_update(gathered, w, m, v):
    r = w.shape[0]

    def body(ga_ref, w_ref, m_ref, v_ref, g_out, d_out, m_out, v_out):
        g = ga_ref[0]
        for dev in range(1, N_DEV):
            g = g + ga_ref[dev]
        g_out[...] = g
        d_out[...], m_out[...], v_out[...] = _adamw_math(w_ref[...], g, m_ref[...], v_ref[...])

    return pl.pallas_call(
        body, name="small_update", out_shape=[jax.ShapeDtypeStruct((r, LANE), F32)] * 4,
    )(gathered, w, m, v)


def _pack_small(vals):
    rows = []
    for name, (r, n) in SMALL:
        flat = vals[name].reshape(-1)
        pad = (-flat.shape[0]) % (SUBLANE * LANE)
        rows.append(jnp.pad(flat, (0, pad)).reshape(-1, LANE))
    return jnp.concatenate(rows, axis=0)


def _unpack_small(packed):
    out, off = {}, 0
    for name, (r, n) in SMALL:
        nrow = -(-(r * n) // (SUBLANE * LANE)) * SUBLANE
        out[name] = packed[off:off + nrow].reshape(-1)[:r * n].reshape(r, n)
        off += nrow
    return out


def kernel(x, positions, ffn1_norm, ffn1_w_in, ffn1_w_out, mix_norm, w_in, hg_lb_table, hg_out_norm, w_hg_branch, mla_q_lora_norm, w_q_up, mla_kv_lora_norm, w_kv_up, q_head_norm, k_head_norm, w_mla_branch, w_merge, b_merge, w_out, ffn2_norm, ffn2_w_in, ffn2_w_out, final_norm, loss_target, m_ffn1_norm, m_ffn1_w_in, m_ffn1_w_out, m_mix_norm, m_w_in, m_hg_lb_table, m_hg_out_norm, m_w_hg_branch, m_mla_q_lora_norm, m_w_q_up, m_mla_kv_lora_norm, m_w_kv_up, m_q_head_norm, m_k_head_norm, m_w_mla_branch, m_w_merge, m_b_merge, m_w_out, m_ffn2_norm, m_ffn2_w_in, m_ffn2_w_out, m_final_norm, v_ffn1_norm, v_ffn1_w_in, v_ffn1_w_out, v_mix_norm, v_w_in, v_hg_lb_table, v_hg_out_norm, v_w_hg_branch, v_mla_q_lora_norm, v_w_q_up, v_mla_kv_lora_norm, v_w_kv_up, v_q_head_norm, v_k_head_norm, v_w_mla_branch, v_w_merge, v_b_merge, v_w_out, v_ffn2_norm, v_ffn2_w_in, v_ffn2_w_out, v_final_norm):
    args = dict(locals())
    t = x.shape[1]
    big_w = {n: args[n][0] for n, _, _ in BIG}
    small = {n: args[n].reshape(shape) for n, shape in SMALL}

    names = [n for n, _, _ in BIG]
    first, rest = names[:2], names[2:]
    full = dict(zip(first, _all_gather([big_w[n].astype(BF16) for n in first], "weights_all_gather_ffn1")))
    g_send, g_recv, g_thru, g_bufs, g_token = _gather_start([big_w[n].astype(BF16) for n in rest], full[first[0]],
                                                            "weights_gather_start")
    gains = dict(small, ffn1_norm=small["ffn1_norm"] + g_token[0:1, 0:1])

    def late_weights(after):
        blocks, bufs = _gather_wait(g_send, g_recv, g_thru, g_bufs, after, "weights_gather_wait")
        return dict(zip(rest, _gather_finish(blocks, bufs, "weights_gather_finish")))

    inv_freq = ROPE_THETA ** (-jnp.arange(0, ROPE, 2, dtype=F32) / ROPE)
    ang = positions[0].astype(F32)[:, None] * inv_freq
    cs = jnp.concatenate([jnp.cos(ang), jnp.cos(ang)], axis=1)
    sn = jnp.concatenate([jnp.sin(ang), jnp.sin(ang)], axis=1)

    c = lax.axis_index("c")
    chip = 2 * lax.axis_index("x") + lax.axis_index("y")
    early = {}

    def chip_sums_of(g, ns, tag):
        from_sibling = _sibling_swap([g[n] for n in ns], "grads_sibling_swap_" + tag)
        return [_chip_sum(g[n], r1, c, "chip_sum_" + n) for n, r1 in zip(ns, from_sibling)]

    def early_grads(g):
        early["names"] = [n for n in names if n in g]
        early["sums"] = chip_sums_of(g, early["names"], "early")
        early["send"], early["recv"], early["thru"], early["lands"], token = _chip_exchange_start(
            early["sums"], "grads_exchange_start")
        return token

    def last_grads(g):
        early["late"] = [n for n in names if n not in early["names"]]
        early["l_send"], early["l_recv"], early["l_thru"], early["l_lands"], token = _chip_exchange_start(
            chip_sums_of(g, early["late"], "late"), "grads_exchange_late_start")
        return token

    loss_row, grad_x, g = _local_step(x[0], loss_target[0], cs, sn, full, late_weights, gains, early_grads,
                                      last_grads)
    late, l_send, l_recv = early["late"], early["l_send"], early["l_recv"]
    l_thru, l_lands = early["l_thru"], early["l_lands"]
    sent, landed = _chip_exchange_wait(early["send"], early["recv"], early["thru"], early["lands"], grad_x,
                                       "grads_exchange_wait")
    exchanged = {n: _put_own(land, lax.dynamic_index_in_dim(own, chip, 0), chip)
                 for n, land, own in zip(early["names"], landed, sent)}

    small_packed = jnp.concatenate([_pack_small(g), jnp.pad(loss_row, ((0, SUBLANE - 1), (0, 0)))], axis=0)
    small_all = _all_gather([small_packed], "small_all_gather")[0]
    zero_tail = jnp.zeros((SUBLANE, LANE), F32)
    pk = lambda d: jnp.concatenate([_pack_small(d), zero_tail], axis=0)
    sg, sd, sm, sv = _small_update(
        small_all, pk(small), pk({n: args["m_" + n].reshape(shape) for n, shape in SMALL}),
        pk({n: args["v_" + n].reshape(shape) for n, shape in SMALL}))
    n_small_rows = _pack_small(small).shape[0]
    loss = sg[n_small_rows, 0]
    outs = {k_: _unpack_small(a) for k_, a in (("grad", sg), ("delta", sd), ("new_m", sm), ("new_v", sv))}

    def adamw(n):
        tr = (lambda a: a.T) if n in GRAD_T else (lambda a: a)
        res = _sum_adamw(exchanged[n], tr(big_w[n]), tr(args["m_" + n][0]), tr(args["v_" + n][0]), "adamw_" + n)
        outs["grad"][n], outs["delta"][n], outs["new_m"][n], outs["new_v"][n] = [tr(a) for a in res]

    for n in early["names"]:
        adamw(n)
    sent, landed = _chip_exchange_wait(l_send, l_recv, l_thru, l_lands, outs["delta"][early["names"][-1]],
                                       "grads_exchange_late_wait")
    for n, land, own in zip(late, landed, sent):
        exchanged[n] = _put_own(land, lax.dynamic_index_in_dim(own, chip, 0), chip)
        adamw(n)

    def shaped(kind, n):
        return outs[kind][n].reshape(args[n].shape)

    return (loss, grad_x[None], *[shaped("grad", n) for n in WEIGHT_ORDER], *[shaped("delta", n) for n in WEIGHT_ORDER],
            *[shaped("new_m", n) for n in WEIGHT_ORDER], *[shaped("new_v", n) for n in WEIGHT_ORDER])
```

```python
import functools

import jax
import jax.numpy as jnp
from jax import lax
from jax.experimental import pallas as pl
from jax.experimental.pallas import tpu as pltpu

F32 = jnp.float32
BF16 = jnp.bfloat16

D = 1024
FF = 2816
NH = 8
HD = 128
ROPE = 64
QK = HD + ROPE
QL = 384
KVL = 256
HGW = NH * HD
CHUNK = 64
EPS = 1e-6
ROPE_THETA = 10000.0
SCALE = QK ** -0.5
LOG2E = 1.4426950408889634

LR, B1, B2, AEPS, WD, STEP = 0.001, 0.9, 0.999, 1e-08, 0.01, 10

HB = 128
SUB = 16
EXP_CLAMP = 80.0
ATT_TILES = (1024, 512, 256, 128)
ATT_Q = (1024, 512, 256, 128)
ATT_KEY_TILES = 1
GRAD_DT = BF16
ROW_TILES = (1024, 512, 256, 128)

LANE = 128
SUBLANE = 8
VMEM_LIMIT = 56 << 20

N_DEV = 8
MESH = pl.DeviceIdType.MESH

BIG = (
    ("ffn1_w_in", (D, 2 * FF), 1), ("ffn1_w_out", (FF, D), 0), ("w_in", (D, 4800), 1),
    ("w_hg_branch", (HGW, D), 0), ("w_q_up", (QL, NH * QK), 1), ("w_kv_up", (KVL, NH * 2 * HD), 1),
    ("w_mla_branch", (NH * HD, D), 0), ("w_merge", (D, 2 * D), 1), ("w_out", (D, D), 0),
    ("ffn2_w_in", (D, 2 * FF), 1), ("ffn2_w_out", (FF, D), 0),
)
SMALL = (
    ("ffn1_norm", (1, D)), ("mix_norm", (1, D)), ("hg_lb_table", (2, HGW)), ("hg_out_norm", (1, HD)),
    ("mla_q_lora_norm", (1, QL)), ("mla_kv_lora_norm", (1, KVL)), ("q_head_norm", (1, QK)),
    ("k_head_norm", (1, QK)), ("b_merge", (1, 2 * D)), ("ffn2_norm", (1, D)), ("final_norm", (1, D)),
)
GRAD_T = ("ffn1_w_in", "ffn2_w_in", "w_in", "w_q_up")
WEIGHT_ORDER = ("ffn1_norm", "ffn1_w_in", "ffn1_w_out", "mix_norm", "w_in", "hg_lb_table", "hg_out_norm",
                "w_hg_branch", "mla_q_lora_norm", "w_q_up", "mla_kv_lora_norm", "w_kv_up", "q_head_norm",
                "k_head_norm", "w_mla_branch", "w_merge", "b_merge", "w_out", "ffn2_norm", "ffn2_w_in",
                "ffn2_w_out", "final_norm")


def _pick(n, cands):
    for c in cands:
        if n % c == 0:
            return c
    return n


def _params(sem):
    return pltpu.CompilerParams(dimension_semantics=sem, vmem_limit_bytes=VMEM_LIMIT)


def _sig(x):
    return 0.5 * jnp.tanh(0.5 * x) + 0.5


def _dot(a, b):
    return jnp.dot(a.astype(BF16), b.astype(BF16), preferred_element_type=F32)


def _dot_nt(a, b):
    return lax.dot_general(a.astype(BF16), b.astype(BF16), (((1,), (1,)), ((), ())),
                           preferred_element_type=F32)


def _dot_tn(a, b):
    return lax.dot_general(a.astype(BF16), b.astype(BF16), (((0,), (0,)), ((), ())),
                           preferred_element_type=F32)


def _split3(x):
    x1 = x.astype(BF16)
    r1 = x - x1.astype(F32)
    x2 = r1.astype(BF16)
    x3 = (r1 - x2.astype(F32)).astype(BF16)
    return x1, x2, x3


def _dot_sel(m, x):
    x1, x2, x3 = _split3(x)
    d = lambda p: jnp.dot(m, p, preferred_element_type=F32)
    return d(x1) + d(x2) + d(x3)


def _sel_dot(x, m):
    x1, x2, x3 = _split3(x)
    d = lambda p: jnp.dot(p, m, preferred_element_type=F32)
    return d(x1) + d(x2) + d(x3)


_TN = (1408, 1024, 768, 512, 384, 256, 128)


def _accumulate(acc, o_ref, axis, steps, term):
    step = pl.program_id(axis)

    @pl.when(step == 0)
    def _():
        acc[...] = jnp.zeros_like(acc)

    val = term()
    if isinstance(val, (list, tuple)):
        for e, v in enumerate(val):
            acc[e] += v
    else:
        acc[...] += val

    @pl.when(step == steps - 1)
    def _():
        o_ref[...] = acc[...].astype(o_ref.dtype)


def _mm(a, b, mode, name, out_dtype=F32):
    if mode == "tn":
        t, m = a.shape
        n = b.shape[1]
        tt, tm, tn = _pick(t, ROW_TILES), _pick(m, _TN), _pick(n, _TN)

        def body(a_ref, b_ref, o_ref, acc):
            _accumulate(acc, o_ref, 2, t // tt, lambda: _dot_tn(a_ref[...], b_ref[...]))

        return pl.pallas_call(
            body, name=name, grid=(m // tm, n // tn, t // tt),
            in_specs=[pl.BlockSpec((tt, tm), lambda i, j, k: (k, i)),
                      pl.BlockSpec((tt, tn), lambda i, j, k: (k, j))],
            out_specs=pl.BlockSpec((tm, tn), lambda i, j, k: (i, j)),
            out_shape=jax.ShapeDtypeStruct((m, n), GRAD_DT), scratch_shapes=[pltpu.VMEM((tm, tn), F32)],
            compiler_params=_params(("parallel", "parallel", "arbitrary")),
        )(a, b)

    m, k = a.shape
    tm = _pick(m, ROW_TILES)
    if mode == "nn":
        n = b.shape[1]
        tn = _pick(n, _TN)
        b_spec = pl.BlockSpec((k, tn), lambda i, j: (0, j))
        dot = _dot
    else:
        n = b.shape[0]
        tn = _pick(n, _TN if k <= 4096 else (512, 256, 128))
        b_spec = pl.BlockSpec((tn, k), lambda i, j: (j, 0))
        dot = _dot_nt

    def body(a_ref, b_ref, o_ref):
        o_ref[...] = dot(a_ref[...], b_ref[...]).astype(o_ref.dtype)

    return pl.pallas_call(
        body, name=name, grid=(m // tm, n // tn),
        in_specs=[pl.BlockSpec((tm, k), lambda i, j: (i, 0)), b_spec],
        out_specs=pl.BlockSpec((tm, tn), lambda i, j: (i, j)),
        out_shape=jax.ShapeDtypeStruct((m, n), out_dtype),
        compiler_params=_params(("parallel", "parallel")),
    )(a, b)


_DOTS = {"nn": _dot, "nt": _dot_nt, "tn": _dot_tn}
_BS = pl.BlockSpec


class _AsRef:
    def __init__(self, value):
        self.value = value

    def __getitem__(self, idx):
        return self.value


def _mm_stack_red(a, w, name, kind, epilogue):
    s, t, n = a.shape
    nout = w.shape[2] if kind == "nn" else w.shape[1]
    fn, first, rows, vecs, outs, accs = epilogue[:6]
    rows = [first] + list(rows)
    tm = _pick(t, epilogue[6] if len(epilogue) > 6 else ROW_TILES)
    n_row, n_vec, n_out = len(rows), len(vecs), len(outs)
    dot = _DOTS[kind]

    def body(*refs):
        a_ref, w_ref, prod = refs[0], refs[1], refs[-1]
        row_refs = refs[2:2 + n_row]
        vec_refs = refs[2 + n_row:2 + n_row + n_vec]
        out_refs = refs[2 + n_row + n_vec:2 + n_row + n_vec + n_out]
        acc_refs = refs[2 + n_row + n_vec + n_out:-1]
        i, j = pl.program_id(0), pl.program_id(1)

        @pl.when(j == 0)
        def _():
            prod[...] = jnp.zeros_like(prod)

        prod[...] += dot(a_ref[...], w_ref[...])

        @pl.when(j == s - 1)
        def _():
            res = fn(row_refs[0], _AsRef(prod[...]), *row_refs[1:], *vec_refs, *out_refs)
            if acc_refs:
                @pl.when(i == 0)
                def _():
                    for r in acc_refs:
                        r[...] = jnp.zeros_like(r)

                for r, val in zip(acc_refs, res):
                    r[...] += val

    in_specs = [_BS((None, tm, n), lambda i, j: (j, i, 0)), _BS((None,) + w.shape[1:], lambda i, j: (j, 0, 0))]
    in_specs += [_BS((tm, r.shape[1]), lambda i, j: (i, 0)) for r in rows]
    in_specs += [_BS(v.shape, lambda i, j: (0, 0)) for v in vecs]
    out_specs = [_BS((tm, wd), lambda i, j: (i, 0)) for wd, _ in outs] + [_BS(sh, lambda i, j: (0, 0)) for sh in accs]
    out_shape = [jax.ShapeDtypeStruct((t, wd), dt) for wd, dt in outs] + [jax.ShapeDtypeStruct(sh, F32) for sh in accs]
    return pl.pallas_call(
        body, name=name, grid=(t // tm, s), in_specs=in_specs, out_specs=out_specs, out_shape=out_shape,
        scratch_shapes=[pltpu.VMEM((tm, nout), F32)],
        compiler_params=_params(("arbitrary" if accs else "parallel", "arbitrary")),
    )(a, w, *rows, *vecs)


def _mm_stack_tn(a, b, name):
    grp = 4
    s, t, k = a.shape
    n = b.shape[1]
    tt = _pick(t, ROW_TILES)

    def body(a_ref, b_ref, o_ref, acc):
        def terms():
            shared = b_ref[...]
            return [_dot_tn(a_ref[e], shared) for e in range(grp)]

        _accumulate(acc, o_ref, 1, t // tt, terms)

    return pl.pallas_call(
        body, name=name, grid=(s // grp, t // tt),
        in_specs=[_BS((None, grp, tt, k), lambda j, r: (j, 0, r, 0)), _BS((tt, n), lambda j, r: (r, 0))],
        out_specs=_BS((None, grp, k, n), lambda j, r: (j, 0, 0, 0)),
        out_shape=jax.ShapeDtypeStruct((s // grp, grp, k, n), GRAD_DT), scratch_shapes=[pltpu.VMEM((grp, k, n), F32)],
        compiler_params=_params(("parallel", "arbitrary")),
    )(a.reshape(s // grp, grp, t, k), b).reshape(s, k, n)


def _cols_fwd(x, w, name):
    t, k = x.shape
    s, _, n = w.shape
    tm = _pick(t, ROW_TILES)

    def body(x_ref, w_ref, o_ref):
        x_ = x_ref[...]
        for j in range(s):
            o_ref[:, n * j:n * (j + 1)] = _dot(x_, w_ref[j])

    return pl.pallas_call(
        body, name=name, grid=(t // tm,),
        in_specs=[_BS((tm, k), lambda i: (i, 0)), _BS((s, k, n), lambda i: (0, 0, 0))],
        out_specs=_BS((tm, s * n), lambda i: (i, 0)), out_shape=jax.ShapeDtypeStruct((t, s * n), F32),
        compiler_params=_params(("parallel",)),
    )(x, w)


def _cols_dx(d, w, name):
    t = d.shape[0]
    s, k, n = w.shape
    tm = _pick(t, ROW_TILES)

    def body(d_ref, w_ref, o_ref):
        acc = _dot_nt(d_ref[:, 0:n], w_ref[0])
        for j in range(1, s):
            acc = acc + _dot_nt(d_ref[:, n * j:n * (j + 1)], w_ref[j])
        o_ref[...] = acc

    return pl.pallas_call(
        body, name=name, grid=(t // tm,),
        in_specs=[_BS((tm, s * n), lambda i: (i, 0)), _BS((s, k, n), lambda i: (0, 0, 0))],
        out_specs=_BS((tm, k), lambda i: (i, 0)), out_shape=jax.ShapeDtypeStruct((t, k), F32),
        compiler_params=_params(("parallel",)),
    )(d, w)


def _cols_dw(x, d, n, name):
    t, k = x.shape
    s = d.shape[1] // n
    tt = _pick(t, ROW_TILES)

    def body(x_ref, d_ref, o_ref, acc):
        def terms():
            x_ = x_ref[...]
            return [_dot_tn(x_, d_ref[:, n * j:n * (j + 1)]) for j in range(s)]

        _accumulate(acc, o_ref, 0, t // tt, terms)

    return pl.pallas_call(
        body, name=name, grid=(t // tt,),
        in_specs=[_BS((tt, k), lambda r: (r, 0)), _BS((tt, s * n), lambda r: (r, 0))],
        out_specs=_BS((s, k, n), lambda r: (0, 0, 0)), out_shape=jax.ShapeDtypeStruct((s, k, n), GRAD_DT),
        scratch_shapes=[pltpu.VMEM((s, k, n), F32)], compiler_params=_params(("arbitrary",)),
    )(x, d)


def _rows(fn, name, t, tm, ins, vecs, outs, accs=()):
    n_in, n_out, n_acc = len(ins) + len(vecs), len(outs), len(accs)

    def body(*refs):
        res = fn(*refs[:n_in + n_out])
        if n_acc:
            acc_refs = refs[n_in + n_out:]

            @pl.when(pl.program_id(0) == 0)
            def _():
                for r in acc_refs:
                    r[...] = jnp.zeros_like(r)

            for r, val in zip(acc_refs, res):
                r[...] += val

    in_specs = [pl.BlockSpec((tm, bw), functools.partial(lambda i, cb: (i, cb), cb=cb)) for _, bw, cb in ins]
    in_specs += [pl.BlockSpec(v.shape, lambda i: (0, 0)) for v in vecs]
    out_specs = [pl.BlockSpec((tm, w), lambda i: (i, 0)) for w, _ in outs]
    out_specs += [pl.BlockSpec(s, lambda i: (0, 0)) for s in accs]
    out_shape = [jax.ShapeDtypeStruct((t, w), dt) for w, dt in outs]
    out_shape += [jax.ShapeDtypeStruct(s, F32) for s in accs]
    return pl.pallas_call(
        body, name=name, grid=(t // tm,), in_specs=in_specs, out_specs=out_specs, out_shape=out_shape,
        compiler_params=_params(("arbitrary",) if n_acc else ("parallel",)),
    )(*[a for a, _, _ in ins], *vecs)


def _rms(x, g):
    return x * lax.rsqrt(jnp.mean(x * x, axis=-1, keepdims=True) + EPS) * g


def _rms_bwd(x, g, dy):
    xh = x * lax.rsqrt(jnp.mean(x * x, axis=-1, keepdims=True) + EPS)
    r = lax.rsqrt(jnp.mean(x * x, axis=-1, keepdims=True) + EPS)
    dyg = dy * g
    dx = r * (dyg - xh * jnp.mean(dyg * xh, axis=-1, keepdims=True))
    return dx, jnp.sum(dy * xh, axis=0, keepdims=True)


def _hgrn_mats():
    row = lax.broadcasted_iota(jnp.int32, (HB, HB), 0)
    col = lax.broadcasted_iota(jnp.int32, (HB, HB), 1)
    return row, col


def _hgrn_gates(qr, z, t0, t1):
    lb = 1.0 / (1.0 + jnp.exp(t1 - t0))
    th = jnp.tanh(0.5 * z)
    sz, sneg = 0.5 + 0.5 * th, 0.5 - 0.5 * th
    f = lb + (1.0 - lb) * sz
    return lb, sz, sneg, f, jnp.log(f), (1.0 - lb) * sneg, qr * _sig(qr)


def _sub_decay(x, mid, cum, i):
    n = SUB * (i + 1)
    seen = x[:n] * jnp.exp(jnp.minimum(mid - cum[:n], EXP_CLAMP))
    return seen if n == HB else jnp.concatenate([seen, jnp.zeros((HB - n, x.shape[1]), F32)], axis=0)


def _hgrn_scores(q, k, cum):
    heads, nsub = range(len(q)), HB // SUB
    mids = [[cum[h][SUB * i + SUB // 2 - 1:SUB * i + SUB // 2, :] for i in range(nsub)] for h in heads]
    cmid = [jnp.concatenate([cum[h][SUB * i:SUB * (i + 1)] - mids[h][i] for i in range(nsub)], axis=0) for h in heads]
    qd = [q[h] * jnp.exp(jnp.minimum(cmid[h], EXP_CLAMP)) for h in heads]
    qd_b = [x.astype(BF16) for x in qd]
    kds = [[None] * nsub for _ in heads]
    parts = [[None] * nsub for _ in heads]
    for i in range(nsub):
        for h in heads:
            kds[h][i] = _sub_decay(k[h], mids[h][i], cum[h], i)
        for h in heads:
            parts[h][i] = _dot_nt(qd_b[h][SUB * i:SUB * (i + 1)], kds[h][i])
    return qd, kds, [jnp.concatenate(parts[h], axis=0) for h in heads], cmid


def _hgrn_fwd(p_hg, table, gain, t):
    nblk = t // HB

    def body(q_ref, f_ref, i_ref, g_ref, tab_ref, gain_ref, o_ref, y_ref, st_ref, state):
        @pl.when(pl.program_id(0) == 0)
        def _():
            state[...] = jnp.zeros_like(state)

        row, col = _hgrn_mats()
        causal = col <= row
        tri = causal.astype(BF16)
        heads = range(NH)
        sl = [slice(HD * h, HD * (h + 1)) for h in heads]
        gates = [_hgrn_gates(q_ref[:, sl[h]], f_ref[:, sl[h]], tab_ref[0:1, sl[h]], tab_ref[1:2, sl[h]]) for h in heads]
        lf, k, q = [g[4] for g in gates], [g[5] for g in gates], [g[6] for g in gates]
        v = [i_ref[:, sl[h]] for h in heads]
        cum = [_dot_sel(tri, lf[h]) for h in heads]
        _, _, s, _ = _hgrn_scores(q, k, cum)
        p = [jnp.where(causal, s[h], 0.0) for h in heads]
        st = [state[h] for h in heads]
        o = [_dot(p[h], v[h]) + _dot_nt(q[h] * jnp.exp(cum[h]), st[h]) for h in heads]
        last = [cum[h][HB - 1:HB, :] for h in heads]
        new_st = [st[h] * jnp.exp(last[h]) + _dot_tn(v[h], k[h] * jnp.exp(last[h] - cum[h])) for h in heads]
        for h in heads:
            gr = g_ref[:, sl[h]]
            st_ref[h, 0] = st[h]
            state[h] = new_st[h]
            o_ref[:, sl[h]] = o[h]
            y_ref[:, sl[h]] = (_rms(o[h], gain_ref[...]) * gr * _sig(gr)).astype(BF16)

    blk = lambda cb: pl.BlockSpec((HB, HGW), functools.partial(lambda n, cb: (n, cb), cb=cb))
    return pl.pallas_call(
        body, name="hgrn_fwd", grid=(nblk,),
        in_specs=[blk(0), blk(1), blk(2), blk(3), pl.BlockSpec((2, HGW), lambda n: (0, 0)),
                  pl.BlockSpec((1, HD), lambda n: (0, 0))],
        out_specs=[pl.BlockSpec((HB, HGW), lambda n: (n, 0)), pl.BlockSpec((HB, HGW), lambda n: (n, 0)),
                   pl.BlockSpec((NH, 1, HD, HD), lambda n: (0, n, 0, 0))],
        out_shape=[jax.ShapeDtypeStruct((t, HGW), F32), jax.ShapeDtypeStruct((t, HGW), BF16),
                   jax.ShapeDtypeStruct((NH, nblk, HD, HD), F32)],
        scratch_shapes=[pltpu.VMEM((NH, HD, HD), F32)],
        compiler_params=_params(("arbitrary",)),
    )(p_hg, p_hg, p_hg, p_hg, table, gain)


def _hgrn_bwd(p_hg, table, gain, o_pre, states, dy, t):
    nblk = t // HB

    def body(q_ref, f_ref, i_ref, g_ref, tab_ref, gain_ref, o_ref, st_ref, dy_ref, dp_ref, dtab_ref, dgain_ref,
             dstate):
        @pl.when(pl.program_id(0) == 0)
        def _():
            dstate[...] = jnp.zeros_like(dstate)
            dtab_ref[...] = jnp.zeros_like(dtab_ref)
            dgain_ref[...] = jnp.zeros_like(dgain_ref)

        row, col = _hgrn_mats()
        causal = col <= row
        tri = causal.astype(BF16)
        tri_t = (row <= col).astype(BF16)
        heads, nsub = range(NH), HB // SUB
        sl = [slice(HD * h, HD * (h + 1)) for h in heads]
        rows_of = lambda i: slice(SUB * i, SUB * (i + 1))
        gain_ = gain_ref[...]
        qr, z = [q_ref[:, sl[h]] for h in heads], [f_ref[:, sl[h]] for h in heads]
        v, gr = [i_ref[:, sl[h]] for h in heads], [g_ref[:, sl[h]] for h in heads]
        gates = [_hgrn_gates(qr[h], z[h], tab_ref[0:1, sl[h]], tab_ref[1:2, sl[h]]) for h in heads]
        lb, sz, sneg, f, lf, k, q = ([g[j] for g in gates] for j in range(7))
        cum = [_dot_sel(tri, lf[h]) for h in heads]
        qd, kds, s, cmid = _hgrn_scores(q, k, cum)
        p = [jnp.where(causal, s[h], 0.0) for h in heads]
        st, dst = [st_ref[h, 0] for h in heads], [dstate[h] for h in heads]
        o, dyh, sg = [o_ref[:, sl[h]] for h in heads], [dy_ref[:, sl[h]] for h in heads], [_sig(x) for x in gr]
        dgr = [dyh[h] * _rms(o[h], gain_) * sg[h] * (1.0 + gr[h] * (1.0 - sg[h])) for h in heads]
        norm_bwd = [_rms_bwd(o[h], gain_, dyh[h] * gr[h] * sg[h]) for h in heads]
        do = [x[0] for x in norm_bwd]
        do_b = [x.astype(BF16) for x in do]
        ecum, last = [jnp.exp(x) for x in cum], [x[HB - 1:HB, :] for x in cum]
        qc = [q[h] * ecum[h] for h in heads]
        edec = [jnp.exp(last[h] - cum[h]) for h in heads]
        kdec = [k[h] * edec[h] for h in heads]
        dp = [jnp.where(causal, _dot_nt(do_b[h], v[h]), 0.0) for h in heads]
        dv = [_dot(p[h].T, do_b[h]) + _dot_nt(kdec[h], dst[h]) for h in heads]
        dqc = [_dot(do_b[h], st[h]) for h in heads]
        dkdec = [_dot(v[h], dst[h]) for h in heads]
        new_dst = [dst[h] * jnp.exp(last[h]) + _dot(do[h].T, qc[h]) for h in heads]
        dp_b = [x.astype(BF16) for x in dp]
        dqd = [jnp.concatenate(parts, axis=0) for parts in
               zip(*[[_dot(dp_b[h][rows_of(i)], kds[h][i]) for h in heads] for i in range(nsub)])]
        gq = [dqd[h] * qd[h] for h in heads]
        dq = [dqd[h] * jnp.exp(jnp.minimum(cmid[h], EXP_CLAMP)) + dqc[h] * ecum[h] for h in heads]
        gs = [dkdec[h] * kdec[h] for h in heads]
        dk = [dkdec[h] * edec[h] for h in heads]
        dcum = [dqc[h] * qc[h] - gs[h] + gq[h]
                + jnp.where(row == HB - 1, jnp.sum(gs[h], axis=0, keepdims=True)
                            + jnp.exp(last[h]) * jnp.sum(st[h] * dst[h], axis=0, keepdims=True), 0.0) for h in heads]
        qd_b = [x.astype(BF16) for x in qd]
        for i in range(nsub):
            dkd = [_dot_tn(dp_b[h][rows_of(i)], qd_b[h][rows_of(i)]) for h in heads]
            for h in heads:
                mid = cum[h][SUB * i + SUB // 2 - 1:SUB * i + SUB // 2, :]
                dk[h] = dk[h] + _sub_decay(dkd[h], mid, cum[h], i)
                gk = dkd[h] * kds[h][i]
                to_mid = jnp.sum(gk, axis=0, keepdims=True) - jnp.sum(gq[h][rows_of(i)], axis=0, keepdims=True)
                dcum[h] = dcum[h] - gk + jnp.where(row == SUB * i + SUB // 2 - 1, to_mid, 0.0)
        dlf = [_dot_sel(tri_t, dcum[h]) for h in heads]
        dgain = jnp.zeros((1, HD), F32)
        for h in heads:
            df = dlf[h] / f[h] - dk[h]
            dlb = jnp.sum(df * sneg[h], axis=0, keepdims=True) * lb[h] * (1.0 - lb[h])
            dtab_ref[0:1, sl[h]] += dlb
            dtab_ref[1:2, sl[h]] -= dlb
            sq = _sig(qr[h])
            dstate[h] = new_dst[h]
            dp_ref[:, sl[h]] = (dq[h] * sq * (1.0 + qr[h] * (1.0 - sq))).astype(BF16)
            dp_ref[:, HGW + HD * h:HGW + HD * (h + 1)] = (df * (1.0 - lb[h]) * sz[h] * sneg[h]).astype(BF16)
            dp_ref[:, 2 * HGW + HD * h:2 * HGW + HD * (h + 1)] = dv[h].astype(BF16)
            dp_ref[:, 3 * HGW + HD * h:3 * HGW + HD * (h + 1)] = dgr[h].astype(BF16)
            dgain = dgain + norm_bwd[h][1]
        dgain_ref[...] += dgain

    rev = lambda cb: pl.BlockSpec((HB, HGW), functools.partial(lambda n, cb: (nblk - 1 - n, cb), cb=cb))
    return pl.pallas_call(
        body, name="hgrn_bwd", grid=(nblk,),
        in_specs=[rev(0), rev(1), rev(2), rev(3), pl.BlockSpec((2, HGW), lambda n: (0, 0)),
                  pl.BlockSpec((1, HD), lambda n: (0, 0)), rev(0),
                  pl.BlockSpec((NH, 1, HD, HD), lambda n: (0, nblk - 1 - n, 0, 0)), rev(0)],
        out_specs=[pl.BlockSpec((HB, 4 * HGW), lambda n: (nblk - 1 - n, 0)),
                   pl.BlockSpec((2, HGW), lambda n: (0, 0)), pl.BlockSpec((1, HD), lambda n: (0, 0))],
        out_shape=[jax.ShapeDtypeStruct((t, 4 * HGW), BF16), jax.ShapeDtypeStruct((2, HGW), F32),
                   jax.ShapeDtypeStruct((1, HD), F32)],
        scratch_shapes=[pltpu.VMEM((NH, HD, HD), F32)],
        compiler_params=_params(("arbitrary",)),
    )(p_hg, p_hg, p_hg, p_hg, table, gain, o_pre, states, dy)


def _rope_mat():
    r = lax.broadcasted_iota(jnp.int32, (ROPE, ROPE), 0)
    c = lax.broadcasted_iota(jnp.int32, (ROPE, ROPE), 1)
    half = ROPE // 2
    return ((r == c - half).astype(F32) - (r == c + half).astype(F32)).astype(BF16)


def _mla_prep_fwd(p_mla, cs, sn, wq, wkv, gql, gkvl, gq, gk, t):
    tm = _pick(t, (512, 256, 128))

    def body(p_ref, cs_ref, sn_ref, wq_ref, wkv_ref, gql_ref, gkvl_ref, gq_ref, gk_ref,
             q_ref, k_ref, v_ref):
        rmat = _rope_mat()
        cqn = _rms(p_ref[:, 0:QL], gql_ref[...]).astype(BF16)
        ckvn = _rms(p_ref[:, QL:QL + KVL], gkvl_ref[...]).astype(BF16)
        kpe = p_ref[:, QL + KVL:QL + KVL + ROPE]
        c, s = cs_ref[...], sn_ref[...]
        rot = lambda x: x * c + _sel_dot(x, rmat) * s
        heads = range(NH)
        ssq = lambda x: jnp.sum(x * x, -1, keepdims=True)
        qa = [_dot(cqn, wq_ref[h, :, 0:HD]) for h in heads]
        qr = [_dot(cqn, wq_ref[h, :, HD:QK]) for h in heads]
        kn = [_dot(ckvn, wkv_ref[h, :, 0:HD]) for h in heads]
        vv = [_dot(ckvn, wkv_ref[h, :, HD:2 * HD]) for h in heads]
        kpe_ss = ssq(kpe)
        rq = [lax.rsqrt((ssq(qa[h]) + ssq(qr[h])) / QK + EPS) for h in heads]
        rk = [lax.rsqrt((ssq(kn[h]) + kpe_ss) / QK + EPS) for h in heads]
        q_rope = [rot(qr[h] * rq[h] * gq_ref[:, HD:QK]) for h in heads]
        k_rope = [rot(kpe * rk[h] * gk_ref[:, HD:QK]) for h in heads]
        for h in heads:
            q_ref[h, :, 0:HD] = (qa[h] * rq[h] * gq_ref[:, 0:HD] * (SCALE * LOG2E)).astype(BF16)
            q_ref[h, :, HD:QK] = (q_rope[h] * (SCALE * LOG2E)).astype(BF16)
            k_ref[h, :, 0:HD] = (kn[h] * rk[h] * gk_ref[:, 0:HD]).astype(BF16)
            k_ref[h, :, HD:QK] = k_rope[h].astype(BF16)
            v_ref[h] = vv[h].astype(BF16)

    whole = lambda a: pl.BlockSpec(a.shape, functools.partial(lambda i, nd: (0,) * nd, nd=a.ndim))
    return pl.pallas_call(
        body, name="mla_prep_fwd", grid=(t // tm,),
        in_specs=[pl.BlockSpec((tm, QL + KVL + ROPE), lambda i: (i, 0)), pl.BlockSpec((tm, ROPE), lambda i: (i, 0)),
                  pl.BlockSpec((tm, ROPE), lambda i: (i, 0))] + [whole(a) for a in (wq, wkv, gql, gkvl, gq, gk)],
        out_specs=[pl.BlockSpec((NH, tm, QK), lambda i: (0, i, 0)), pl.BlockSpec((NH, tm, QK), lambda i: (0, i, 0)),
                   pl.BlockSpec((NH, tm, HD), lambda i: (0, i, 0))],
        out_shape=[jax.ShapeDtypeStruct((NH, t, QK), BF16), jax.ShapeDtypeStruct((NH, t, QK), BF16),
                   jax.ShapeDtypeStruct((NH, t, HD), BF16)],
        compiler_params=_params(("parallel",)),
    )(p_mla, cs, sn, wq, wkv, gql, gkvl, gq, gk)


def _mla_prep_bwd(p_mla, cs, sn, wq, wkv, gql, gkvl, gq, gk, dq, dk, dv, t):
    tm = _pick(t, (512, 256, 128))

    def body(p_ref, cs_ref, sn_ref, wq_ref, wkv_ref, gql_ref, gkvl_ref, gq_ref, gk_ref,
             dq_ref, dk_ref, dv_ref,
             dp_ref, dwq_ref, dwkv_ref, dgql_ref, dgkvl_ref, dgq_ref, dgk_ref):
        accs = (dwq_ref, dwkv_ref, dgql_ref, dgkvl_ref, dgq_ref, dgk_ref)

        @pl.when(pl.program_id(0) == 0)
        def _():
            for r in accs:
                r[...] = jnp.zeros_like(r)

        rmat = _rope_mat()
        rmat_t = -rmat
        cq, ckv = p_ref[:, 0:QL], p_ref[:, QL:QL + KVL]
        kpe = p_ref[:, QL + KVL:QL + KVL + ROPE]
        cqn_f, ckvn_f = _rms(cq, gql_ref[...]), _rms(ckv, gkvl_ref[...])
        cqn, ckvn = cqn_f.astype(BF16), ckvn_f.astype(BF16)
        ckvn_t = ckvn_f.T.astype(BF16)
        c, s = cs_ref[...], sn_ref[...]
        unrot = lambda dy: dy * c + _sel_dot(dy * s, rmat_t)
        heads = range(NH)
        rsum = lambda x: jnp.sum(x, -1, keepdims=True)
        csum = lambda x: jnp.sum(x, axis=0, keepdims=True)
        qa = [_dot(cqn, wq_ref[h, :, 0:HD]) for h in heads]
        qr = [_dot(cqn, wq_ref[h, :, HD:QK]) for h in heads]
        kn = [_dot(ckvn, wkv_ref[h, :, 0:HD]) for h in heads]
        dyr = [unrot(dq_ref[h, :, HD:QK] * SCALE) for h in heads]
        dkr = [unrot(dk_ref[h, :, HD:QK]) for h in heads]
        rq = [lax.rsqrt((rsum(qa[h] * qa[h]) + rsum(qr[h] * qr[h])) / QK + EPS) for h in heads]
        xa, xr = [qa[h] * rq[h] for h in heads], [qr[h] * rq[h] for h in heads]
        dya = [dq_ref[h, :, 0:HD] * SCALE for h in heads]
        ga, gr_ = [dya[h] * gq_ref[:, 0:HD] for h in heads], [dyr[h] * gq_ref[:, HD:QK] for h in heads]
        mean = [(rsum(ga[h] * xa[h]) + rsum(gr_[h] * xr[h])) / QK for h in heads]
        dqa = [(rq[h] * (ga[h] - xa[h] * mean[h])).astype(BF16) for h in heads]
        dqr = [(rq[h] * (gr_[h] - xr[h] * mean[h])).astype(BF16) for h in heads]
        kpe_ss = rsum(kpe * kpe)
        rk = [lax.rsqrt((rsum(kn[h] * kn[h]) + kpe_ss) / QK + EPS) for h in heads]
        ya, yr = [kn[h] * rk[h] for h in heads], [kpe * rk[h] for h in heads]
        dka = [dk_ref[h, :, 0:HD] for h in heads]
        ha, hr = [dka[h] * gk_ref[:, 0:HD] for h in heads], [dkr[h] * gk_ref[:, HD:QK] for h in heads]
        mean_k = [(rsum(ha[h] * ya[h]) + rsum(hr[h] * yr[h])) / QK for h in heads]
        dkn = [(rk[h] * (ha[h] - ya[h] * mean_k[h])).astype(BF16) for h in heads]
        dvh = [dv_ref[h].astype(BF16) for h in heads]
        dw = [(_dot_tn(dqa[h], cqn), _dot_tn(dqr[h], cqn), _dot(ckvn_t, dkn[h]), _dot(ckvn_t, dvh[h])) for h in heads]
        back_q = [_dot_nt(dqa[h], wq_ref[h, :, 0:HD]) + _dot_nt(dqr[h], wq_ref[h, :, HD:QK]) for h in heads]
        back_kv = [_dot_nt(dkn[h], wkv_ref[h, :, 0:HD]) + _dot_nt(dvh[h], wkv_ref[h, :, HD:2 * HD]) for h in heads]
        dcqn, dckvn = sum(back_q[1:], back_q[0]), sum(back_kv[1:], back_kv[0])
        dkpe = sum([rk[h] * (hr[h] - yr[h] * mean_k[h]) for h in heads][1:], rk[0] * (hr[0] - yr[0] * mean_k[0]))
        dgq_a = sum([csum(dya[h] * xa[h]) for h in heads][1:], csum(dya[0] * xa[0]))
        dgq_r = sum([csum(dyr[h] * xr[h]) for h in heads][1:], csum(dyr[0] * xr[0]))
        dgk_a = sum([csum(dka[h] * ya[h]) for h in heads][1:], csum(dka[0] * ya[0]))
        dgk_r = sum([csum(dkr[h] * yr[h]) for h in heads][1:], csum(dkr[0] * yr[0]))
        for h in heads:
            dwq_ref[h, 0:HD, :] += dw[h][0]
            dwq_ref[h, HD:QK, :] += dw[h][1]
            dwkv_ref[h, :, 0:HD] += dw[h][2]
            dwkv_ref[h, :, HD:2 * HD] += dw[h][3]
        dcq, dg1 = _rms_bwd(cq, gql_ref[...], dcqn)
        dckv, dg2 = _rms_bwd(ckv, gkvl_ref[...], dckvn)
        dp_ref[:, 0:QL] = dcq.astype(BF16)
        dp_ref[:, QL:QL + KVL] = dckv.astype(BF16)
        dp_ref[:, QL + KVL:QL + KVL + ROPE] = dkpe.astype(BF16)
        dgql_ref[...] += dg1
        dgkvl_ref[...] += dg2
        dgq_ref[:, 0:HD] += dgq_a
        dgq_ref[:, HD:QK] += dgq_r
        dgk_ref[:, 0:HD] += dgk_a
        dgk_ref[:, HD:QK] += dgk_r

    whole = lambda a: pl.BlockSpec(a.shape, functools.partial(lambda i, nd: (0,) * nd, nd=a.ndim))
    acc_shapes = [(NH, QK, QL), wkv.shape, gql.shape, gkvl.shape, gq.shape, gk.shape]
    return pl.pallas_call(
        body, name="mla_prep_bwd", grid=(t // tm,),
        in_specs=[pl.BlockSpec((tm, QL + KVL + ROPE), lambda i: (i, 0)), pl.BlockSpec((tm, ROPE), lambda i: (i, 0)),
                  pl.BlockSpec((tm, ROPE), lambda i: (i, 0))]
        + [whole(a) for a in (wq, wkv, gql, gkvl, gq, gk)]
        + [pl.BlockSpec((NH, tm, QK), lambda i: (0, i, 0)), pl.BlockSpec((NH, tm, QK), lambda i: (0, i, 0)),
           pl.BlockSpec((NH, tm, HD), lambda i: (0, i, 0))],
        out_specs=[pl.BlockSpec((tm, QL + KVL + ROPE), lambda i: (i, 0))]
        + [pl.BlockSpec(s, functools.partial(lambda i, nd: (0,) * nd, nd=len(s))) for s in acc_shapes],
        out_shape=[jax.ShapeDtypeStruct((t, QL + KVL + ROPE), BF16)]
        + [jax.ShapeDtypeStruct(s, F32) for s in acc_shapes],
        compiler_params=_params(("arbitrary",)),
    )(p_mla, cs, sn, wq, wkv, gql, gkvl, gq, gk, dq, dk, dv)


def _chunk_mask(nq, nk, key0, keys_on_rows):
    shape = (nk, nq) if keys_on_rows else (nq, nk)
    qi = lax.broadcasted_iota(jnp.int32, shape, 1 if keys_on_rows else 0) // CHUNK
    ki = lax.broadcasted_iota(jnp.int32, shape, 0 if keys_on_rows else 1) // CHUNK + key0 // CHUNK
    return ki <= qi


def _flash_fwd(q, k, v, t):
    tq = _pick(t, ATT_Q)
    tk = tq // ATT_KEY_TILES

    def body(q_ref, k_ref, v_ref, o_ref, lse_ref):
        i = pl.program_id(1)
        qt = q_ref[0]

        def step(j, carry, key0):
            m, l, acc = carry
            cols = pl.ds(pl.multiple_of(j * tk, tk), tk)
            s = _dot_nt(qt, k_ref[0, cols, :])
            if key0 is not None:
                s = jnp.where(_chunk_mask(tq, tk, key0, False), s, -jnp.inf)
            m_new = jnp.maximum(m, jnp.max(s, axis=-1, keepdims=True))
            p = jnp.exp2(s - m_new)
            alpha = jnp.exp2(m - m_new)
            return m_new, alpha * l + jnp.sum(p, axis=-1, keepdims=True), alpha * acc + _dot(p, v_ref[0, cols, :])

        init = (jnp.full((tq, 1), -jnp.inf, F32), jnp.zeros((tq, 1), F32), jnp.zeros((tq, HD), F32))
        carry = lax.fori_loop(0, ATT_KEY_TILES * i, lambda j, cr: step(j, cr, None), init)
        for h in range(ATT_KEY_TILES):
            carry = step(ATT_KEY_TILES * i + h, carry, h * tk)
        m, l, acc = carry
        o_ref[...] = (acc / l).astype(BF16)
        x1, x2, x3 = _split3(jnp.broadcast_to((m + jnp.log2(l)) * (1.0 / HD), (tq, HD)))
        ones = jnp.ones((SUBLANE, HD), BF16)
        rowsum = lambda p: lax.dot_general(ones, p, (((1,), (1,)), ((), ())), preferred_element_type=F32)
        lse_ref[0] = (rowsum(x1) + rowsum(x2) + rowsum(x3))[0:1, :]

    return pl.pallas_call(
        body, name="flash_fwd", grid=(NH, t // tq),
        in_specs=[pl.BlockSpec((1, tq, QK), lambda h, i: (h, i, 0)), pl.BlockSpec((1, t, QK), lambda h, i: (h, 0, 0)),
                  pl.BlockSpec((1, t, HD), lambda h, i: (h, 0, 0))],
        out_specs=[pl.BlockSpec((tq, HD), lambda h, i: (i, h)), pl.BlockSpec((1, 1, tq), lambda h, i: (h, 0, i))],
        out_shape=[jax.ShapeDtypeStruct((t, NH * HD), BF16), jax.ShapeDtypeStruct((NH, 1, t), F32)],
        compiler_params=_params(("parallel", "parallel")),
    )(q, k, v)


def _attn_out_bwd(dy, w, o, t):
    tm = _pick(t, ATT_TILES)

    def body(dy_ref, w_ref, o_ref, do_ref, delta_ref):
        do = _dot_nt(dy_ref[...], w_ref[...]).astype(BF16)
        do_ref[...] = do
        ones = jnp.ones((SUBLANE, HD), BF16)
        rowsum = lambda p: lax.dot_general(ones, p, (((1,), (1,)), ((), ())), preferred_element_type=F32)
        sl = [slice(HD * h, HD * (h + 1)) for h in range(NH)]
        parts = [_split3(do[:, sl[h]].astype(F32) * o_ref[:, sl[h]].astype(F32)) for h in range(NH)]
        sums = [[rowsum(x) for x in parts[h]] for h in range(NH)]
        for h in range(NH):
            delta_ref[h] = (sums[h][0] + sums[h][1] + sums[h][2])[0:1, :]

    return pl.pallas_call(
        body, name="attn_out_bwd", grid=(t // tm,),
        in_specs=[_BS((tm, dy.shape[1]), lambda i: (i, 0)), _BS(w.shape, lambda i: (0, 0)),
                  _BS((tm, NH * HD), lambda i: (i, 0))],
        out_specs=[_BS((tm, NH * HD), lambda i: (i, 0)), _BS((NH, 1, tm), lambda i: (0, 0, i))],
        out_shape=[jax.ShapeDtypeStruct((t, NH * HD), BF16), jax.ShapeDtypeStruct((NH, 1, t), F32)],
        compiler_params=_params(("parallel",)),
    )(dy, w, o)


def _flash_bwd(q, k, v, do, lse_row, delta_row, t):
    tq = _pick(t, ATT_Q)
    tk = tq // ATT_KEY_TILES

    def body(q_ref, k_ref, v_ref, do_ref, lse_ref, delta_ref, dq_ref, dk_ref, dv_ref):
        j = pl.program_id(1)

        @pl.when(j == 0)
        def _():
            dq_ref[...] = jnp.zeros_like(dq_ref)

        kt, vt = k_ref[0], v_ref[0]

        def step(i, carry, key0):
            dk, dv = carry
            rows = pl.ds(pl.multiple_of(i * tq, tq), tq)
            qt, dot_ = q_ref[0, rows, :], do_ref[rows, :]
            p = jnp.exp2(_dot_nt(kt, qt) - lse_ref[0, :, rows])
            if key0 is not None:
                p = jnp.where(_chunk_mask(tq, tk, key0, True), p, 0.0)
            ds = (p * (_dot_nt(vt, dot_) - delta_ref[0, :, rows])).astype(BF16)
            dq_ref[0, rows, :] += _dot_tn(ds, kt)
            return dk + _dot(ds, qt), dv + _dot(p, dot_)

        own = j // ATT_KEY_TILES
        carry = step(own, (jnp.zeros((tk, QK), F32), jnp.zeros((tk, HD), F32)), (j % ATT_KEY_TILES) * tk)
        dk, dv = lax.fori_loop(own + 1, t // tq, lambda i, cr: step(i, cr, None), carry)
        dk_ref[0] = dk * (1.0 / LOG2E)
        dv_ref[0] = dv

    return pl.pallas_call(
        body, name="flash_bwd", grid=(NH, t // tk),
        in_specs=[pl.BlockSpec((1, t, QK), lambda h, j: (h, 0, 0)), pl.BlockSpec((1, tk, QK), lambda h, j: (h, j, 0)),
                  pl.BlockSpec((1, tk, HD), lambda h, j: (h, j, 0)), pl.BlockSpec((t, HD), lambda h, j: (0, h)),
                  pl.BlockSpec((1, 1, t), lambda h, j: (h, 0, 0)), pl.BlockSpec((1, 1, t), lambda h, j: (h, 0, 0))],
        out_specs=[pl.BlockSpec((1, t, QK), lambda h, j: (h, 0, 0)), pl.BlockSpec((1, tk, QK), lambda h, j: (h, j, 0)),
                   pl.BlockSpec((1, tk, HD), lambda h, j: (h, j, 0))],
        out_shape=[jax.ShapeDtypeStruct((NH, t, QK), F32), jax.ShapeDtypeStruct((NH, t, QK), F32),
                   jax.ShapeDtypeStruct((NH, t, HD), F32)],
        compiler_params=_params(("parallel", "arbitrary")),
    )(q, k, v, do, lse_row, delta_row)


def _ffn_in(xn, w_in, name):
    t, k = xn.shape
    s, _, n = w_in.shape
    tm = _pick(t, ROW_TILES)

    def body(x_ref, w_ref, gu_ref, a_ref):
        x = x_ref[...]
        g, u = _dot(x, w_ref[0]), _dot(x, w_ref[1])
        gu_ref[0] = g.astype(BF16)
        gu_ref[1] = u.astype(BF16)
        a_ref[...] = (g * _sig(g) * u).astype(BF16)

    return pl.pallas_call(
        body, name=name, grid=(t // tm, s // 2),
        in_specs=[_BS((tm, k), lambda i, j: (i, 0)), _BS((2, None, k, n), lambda i, j: (0, j, 0, 0))],
        out_specs=[_BS((2, None, tm, n), lambda i, j: (0, j, i, 0)), _BS((None, tm, n), lambda i, j: (j, i, 0))],
        out_shape=[jax.ShapeDtypeStruct((2, s // 2, t, n), BF16), jax.ShapeDtypeStruct((s // 2, t, n), BF16)],
        compiler_params=_params(("parallel", "parallel")),
    )(xn, w_in.reshape(2, s // 2, k, n))


def _ffn_dgu(dfo, w_out, gu, name):
    t, k = dfo.shape
    s, n, _ = w_out.shape
    tm = _pick(t, ROW_TILES)

    def body(d_ref, w_ref, gu_ref, o_ref):
        da = _dot_nt(d_ref[...], w_ref[...])
        g, u = gu_ref[0].astype(F32), gu_ref[1].astype(F32)
        sg = _sig(g)
        o_ref[0] = (da * u * sg * (1.0 + g * (1.0 - sg))).astype(BF16)
        o_ref[1] = (da * g * sg).astype(BF16)

    pair = _BS((2, None, tm, n), lambda i, j: (0, j, i, 0))
    return pl.pallas_call(
        body, name=name, grid=(t // tm, s),
        in_specs=[_BS((tm, k), lambda i, j: (i, 0)), _BS((None, n, k), lambda i, j: (j, 0, 0)), pair],
        out_specs=pair, out_shape=jax.ShapeDtypeStruct((2, s, t, n), BF16),
        compiler_params=_params(("parallel", "parallel")),
    )(dfo, w_out, gu).reshape(2 * s, t, n)


def _ffn_fwd(xn, w_in, w_out, tag, epilogue):
    gu, a = _ffn_in(xn, w_in, tag + "_in")
    return gu, a, _mm_stack_red(a, w_out, tag + "_out", "nn", epilogue)


def _ffn_bwd(dfo, xn, gu, a, w_in, w_out, tag, epilogue):
    dw_out = _mm_stack_tn(a, dfo, tag + "_dwout")
    dgu = _ffn_dgu(dfo, w_out, gu, tag + "_dgu")
    dw_in = _mm_stack_tn(dgu, xn, tag + "_dwin")
    if callable(epilogue):
        epilogue = epilogue(dw_in, dw_out)
    return _mm_stack_red(dgu, w_in, tag + "_dxn", "nt", epilogue), dw_in, dw_out


def _local_step(x, target, cs, sn, w, late_weights, s, early_grads, last_grads):
    t = x.shape[0]
    tm = _pick(t, (512, 256, 128))
    g = {}
    ffn_out = lambda n: w[n].reshape(4, FF // 4, D)

    def norm_fn(x_ref, g_ref, o_ref):
        o_ref[...] = _rms(x_ref[...], g_ref[...]).astype(BF16)

    xn1 = _rows(norm_fn, "norm1", t, tm, [(x, D, 0)], [s["ffn1_norm"]], [(D, BF16)])[0]
    def res_norm_fn(scale):
        def fn(h_ref, f_ref, g_ref, h_out, n_out):
            h = h_ref[...] + scale * f_ref[...]
            h_out[...] = h
            n_out[...] = _rms(h, g_ref[...]).astype(BF16)
        return fn

    gu1, a1, (h1, u) = _ffn_fwd(xn1, w["ffn1_w_in"], ffn_out("ffn1_w_out"), "ffn1",
                                (res_norm_fn(0.5), x, [], [s["mix_norm"]], [(D, F32), (D, BF16)], []))
    w = {**w, **late_weights(h1)}
    rows_of = lambda n: w[n].reshape(-1, w[n].shape[2])
    w_hgb, w_mlab, w_o = rows_of("w_hg_branch"), rows_of("w_mla_branch"), rows_of("w_out")
    w_in_nat = w["w_in"].transpose(1, 0, 2).reshape(D, -1)
    w_mrg = w["w_merge"]
    mw = w_mrg.shape[2]
    w_in_hg, w_in_mla = w_in_nat[:, :4 * HGW], w_in_nat[:, 4 * HGW:]
    p_hg = _mm(u, w_in_hg, "nn", "proj_hg")
    p_mla = _mm(u, w_in_mla, "nn", "proj_mla")
    gpre = _cols_fwd(u, w_mrg, "proj_gate")
    o_pre, hgy, states = _hgrn_fwd(p_hg, s["hg_lb_table"], s["hg_out_norm"], t)
    prep_args = (p_mla, cs, sn, w["w_q_up"], w["w_kv_up"], s["mla_q_lora_norm"], s["mla_kv_lora_norm"],
                 s["q_head_norm"], s["k_head_norm"])
    q, k, v = _mla_prep_fwd(*prep_args, t)
    att, lse = _flash_fwd(q, k, v, t)
    y_hg = _mm(hgy, w_hgb, "nn", "branch_hg")
    def mix_fn(gp_ref, ym_ref, yh_ref, b_ref, o_ref, ym_out):
        gh = _sig(gp_ref[:, 0:D] + b_ref[:, 0:D])
        gm = _sig(gp_ref[:, D:2 * D] + b_ref[:, D:2 * D])
        ym = ym_ref[...]
        ym_out[...] = ym
        o_ref[...] = (gh * yh_ref[...] + gm * ym).astype(BF16)

    mixed, y_mla = _mm_stack_red(att[None], w_mlab[None], "branch_mla", "nn",
                                 (mix_fn, gpre, [y_hg], [s["b_merge"]], [(D, BF16), (D, F32)], []))
    h2, xn2 = _mm_stack_red(mixed[None], w_o[None], "mix_out", "nn",
                            (res_norm_fn(1.0), h1, [], [s["ffn2_norm"]], [(D, F32), (D, BF16)], []))

    def loss_fn(h_ref, f_ref, tg_ref, g_ref, dh_out, dhb_out):
        h = h_ref[...] + 0.5 * f_ref[...]
        e = _rms(h, g_ref[...]) - tg_ref[...]
        dh, dgain = _rms_bwd(h, g_ref[...], e / D)
        dh_out[...] = dh
        dhb_out[...] = (0.5 * dh).astype(BF16)
        return dgain, jnp.full((1, LANE), 0.5 / D * jnp.sum(e * e), F32)

    gu2, a2, (dh3, dfo2, g["final_norm"], loss) = _ffn_fwd(
        xn2, w["ffn2_w_in"], ffn_out("ffn2_w_out"), "ffn2",
        (loss_fn, h2, [target], [s["final_norm"]], [(D, F32), (D, BF16)], [(1, D), (1, LANE)]))

    def norm_bwd_fn(scale):
        def fn(h_ref, dxn_ref, dh_ref, g_ref, dh_out, dhb_out):
            dx, dgain = _rms_bwd(h_ref[...], g_ref[...], dxn_ref[...])
            dh = dh_ref[...] + dx
            dh_out[...] = dh
            dhb_out[...] = (scale * dh).astype(BF16)
            return (dgain,)
        return fn

    as_rows = lambda a: a.reshape((N_DEV, -1) + a.shape[-1:])
    (dh2, dh2b, g["ffn2_norm"]), g["ffn2_w_in"], dwo = _ffn_bwd(
        dfo2, xn2, gu2, a2, w["ffn2_w_in"], ffn_out("ffn2_w_out"), "ffn2",
        (norm_bwd_fn(1.0), h2, [dh3], [s["ffn2_norm"]], [(D, F32), (D, BF16)], [(1, D)]))
    g["ffn2_w_out"] = as_rows(dwo)
    g["w_out"] = as_rows(_mm(mixed, dh2b, "tn", "mix_out_dw"))

    def mix_bwd_fn(gp_ref, dm_ref, yh_ref, ym_ref, b_ref, dyh_out, dym_out, dg_out):
        gh = _sig(gp_ref[:, 0:D] + b_ref[:, 0:D])
        gm = _sig(gp_ref[:, D:2 * D] + b_ref[:, D:2 * D])
        dm = dm_ref[...]
        dyh_out[...] = (dm * gh).astype(BF16)
        dym_out[...] = (dm * gm).astype(BF16)
        dgh = dm * yh_ref[...] * gh * (1.0 - gh)
        dgm = dm * ym_ref[...] * gm * (1.0 - gm)
        dg_out[:, 0:D] = dgh.astype(BF16)
        dg_out[:, D:2 * D] = dgm.astype(BF16)
        return (jnp.concatenate([jnp.sum(dgh, axis=0, keepdims=True), jnp.sum(dgm, axis=0, keepdims=True)], axis=1),)

    dyh, dym, dgpre, g["b_merge"] = _mm_stack_red(
        dh2b[None], w_o[None], "mix_out_dx", "nt",
        (mix_bwd_fn, gpre, [y_hg, y_mla], [s["b_merge"]], [(D, BF16), (D, BF16), (2 * D, BF16)], [(1, 2 * D)],
         (512, 256, 128)))
    g["w_hg_branch"] = as_rows(_mm(hgy, dyh, "tn", "branch_hg_dw"))
    g["w_mla_branch"] = as_rows(_mm(att, dym, "tn", "branch_mla_dw"))
    g["w_merge"] = _cols_dw(u, dgpre, mw, "proj_gate_dw")
    dhgy = _mm(dyh, w_hgb, "nt", "branch_hg_dx")
    datt, delta = _attn_out_bwd(dym, w_mlab, att, t)
    du_gate = _cols_dx(dgpre, w_mrg, "proj_gate_dx")

    dq, dk, dv = _flash_bwd(q, k, v, datt, lse, delta, t)
    (dp_mla, dwq, dwkv, g["mla_q_lora_norm"], g["mla_kv_lora_norm"], g["q_head_norm"],
     g["k_head_norm"]) = _mla_prep_bwd(*prep_args, dq, dk, dv, t)
    g["w_q_up"], g["w_kv_up"] = dwq.astype(GRAD_DT), dwkv.astype(GRAD_DT)
    dp_hg, g["hg_lb_table"], g["hg_out_norm"] = _hgrn_bwd(p_hg, s["hg_lb_table"], s["hg_out_norm"], o_pre, states,
                                                          dhgy, t)
    dw_in_t = jnp.concatenate([_mm(dp_hg, u, "tn", "proj_hg_dw"), _mm(dp_mla, u, "tn", "proj_mla_dw")], axis=0)
    g["w_in"] = dw_in_t.reshape(N_DEV, -1, D)
    du_mla = _mm(dp_mla, w_in_mla, "nt", "proj_mla_dx")

    def mixnorm_bwd_fn(h_ref, a_ref, b_ref, c_ref, dh_ref, g_ref, dh_out, dhb_out):
        dx, dgain = _rms_bwd(h_ref[...], g_ref[...], a_ref[...] + b_ref[...] + c_ref[...])
        dh = dh_ref[...] + dx
        dh_out[...] = dh
        dhb_out[...] = (0.5 * dh).astype(BF16)
        return (dgain,)

    mix_gain = s["mix_norm"] + early_grads(g)[0:1, 0:1]
    dh1, dfo1, g["mix_norm"] = _mm_stack_red(
        dp_hg[None], w_in_hg[None], "proj_hg_dx", "nt",
        (mixnorm_bwd_fn, h1, [du_mla, du_gate, dh2], [mix_gain], [(D, F32), (D, BF16)], [(1, D)], (512, 256, 128)))
    def last_stage(dw_in, dw_out):
        g["ffn1_w_in"], g["ffn1_w_out"] = dw_in, as_rows(dw_out)
        gain = s["ffn1_norm"] + last_grads(g)[0:1, 0:1]
        return norm_bwd_fn(1.0), x, [dh1], [gain], [(D, F32), (D, BF16)], [(1, D)]

    (grad_x, _, g["ffn1_norm"]), _, _ = _ffn_bwd(dfo1, xn1, gu1, a1, w["ffn1_w_in"], ffn_out("ffn1_w_out"), "ffn1",
                                                 last_stage)
    return loss, grad_x, g


def _coords():
    return lax.axis_index("x"), lax.axis_index("y"), lax.axis_index("c")


def _hbm_call(body, name, ins, out_shapes, scratch):
    any_spec = pl.BlockSpec(memory_space=pl.ANY)
    return pl.pallas_call(
        body, name=name, out_shape=[jax.ShapeDtypeStruct(s, dt) for s, dt in out_shapes],
        in_specs=[any_spec] * len(ins), out_specs=[any_spec] * len(out_shapes), scratch_shapes=scratch,
    )(*ins)


def _my_slot():
    return 4 * lax.axis_index("x") + 2 * lax.axis_index("y") + lax.axis_index("c")


def _put_own(buf, own, index):
    return lax.dynamic_update_index_in_dim(buf, own, index, 0)


def _all_gather(blocks, name):
    nb = len(blocks)

    def body(*refs):
        x_refs, out_refs = refs[:nb], refs[nb:2 * nb]
        send_sems, recv_sems = refs[2 * nb:]
        x, y, c = _coords()
        me, sibling = (x, y, c), (x, y, 1 - c)
        chips = [(1 - x, y), (x, 1 - y), (1 - x, 1 - y)]

        def slot(b, px, py, pc):
            return out_refs[b].at[4 * px + 2 * py + pc]

        def copy(b, kk, block_of, to, src=None):
            return pltpu.make_async_remote_copy(
                src_ref=slot(b, *block_of) if src is None else src, dst_ref=slot(b, *block_of),
                send_sem=send_sems.at[b, kk], recv_sem=recv_sems.at[b, kk], device_id=to, device_id_type=MESH)

        first = [copy(b, 0, me, sibling, src=x_refs[b]) for b in range(nb)]
        first += [copy(b, 1 + j, me, (*chip, c), src=x_refs[b]) for j, chip in enumerate(chips) for b in range(nb)]
        for cp in first:
            cp.start()
        passed = []
        for j, chip in enumerate(chips):
            for b in range(nb):
                copy(b, 1 + j, (*chip, c), me).wait_recv()
                passed.append(copy(b, 4 + j, (*chip, c), sibling))
                passed[-1].start()
        for b in range(nb):
            copy(b, 0, sibling, me).wait_recv()
        for j, chip in enumerate(chips):
            for b in range(nb):
                copy(b, 4 + j, (*chip, 1 - c), me).wait_recv()
        for cp in first + passed:
            cp.wait_send()

    outs = _hbm_call(body, name, blocks, [((N_DEV,) + b.shape, b.dtype) for b in blocks],
                     [pltpu.SemaphoreType.DMA((nb, 7)), pltpu.SemaphoreType.DMA((nb, 7))])
    return [_put_own(o, b[None], _my_slot()) for o, b in zip(outs, blocks)]


def _gather_peers():
    x, y, c = _coords()
    return (x, y, c), [(x, y, 1 - c), (1 - x, y, c), (x, 1 - y, c), (1 - x, 1 - y, c)]


def _gather_start(blocks, after, name):
    nb = len(blocks)
    hbm, sem = pl.BlockSpec(memory_space=pltpu.HBM), pl.BlockSpec(memory_space=pltpu.SEMAPHORE)

    def body(*refs):
        x_refs, out_refs = refs[:nb], refs[nb:2 * nb]
        send_sems, recv_sems, token = refs[2 * nb + 1], refs[2 * nb + 2], refs[-1]
        (x, y, c), peers = _gather_peers()
        for kk, peer in enumerate(peers):
            for b in range(nb):
                pltpu.make_async_remote_copy(
                    src_ref=x_refs[b], dst_ref=out_refs[b].at[4 * x + 2 * y + c], send_sem=send_sems.at[4 * b + kk],
                    recv_sem=recv_sems.at[4 * b + kk], device_id=peer, device_id_type=MESH).start()
        token[...] = jnp.zeros_like(token)

    gathers = [pltpu.with_memory_space_constraint(lax.empty((N_DEV,) + b.shape, b.dtype), pltpu.HBM) for b in blocks]
    outs = pl.pallas_call(
        body, name=name,
        out_shape=(pltpu.SemaphoreType.DMA((4 * nb,)), pltpu.SemaphoreType.DMA((4 * nb,)),
                   *[pltpu.HBM(b.shape, b.dtype) for b in blocks], *[pltpu.HBM(b.shape, b.dtype) for b in gathers],
                   jax.ShapeDtypeStruct((SUBLANE, LANE), F32)),
        in_specs=[hbm] * (2 * nb) + [pl.BlockSpec(memory_space=pl.ANY)],
        out_specs=(sem, sem, *[hbm] * (2 * nb), pl.BlockSpec(memory_space=pltpu.VMEM)),
        input_output_aliases={i: 2 + i for i in range(2 * nb)},
        compiler_params=pltpu.CompilerParams(has_side_effects=pltpu.SideEffectType.DATAFLOW_SIDE_EFFECTING),
    )(*[pltpu.with_memory_space_constraint(b, pltpu.HBM) for b in blocks], *gathers, after)
    return outs[0], outs[1], list(outs[2:2 + nb]), list(outs[2 + nb:2 + 2 * nb]), outs[-1]


def _gather_wait(send_sems, recv_sems, thru, gathers, after, name):
    nb = len(thru)
    hbm, sem = pl.BlockSpec(memory_space=pltpu.HBM), pl.BlockSpec(memory_space=pltpu.SEMAPHORE)

    def body(*refs):
        x_refs, out_refs = refs[:nb], refs[nb:2 * nb]
        send_sems_, recv_sems_ = refs[2 * nb], refs[2 * nb + 1]
        _, peers = _gather_peers()
        for kk, (px, py, pc) in enumerate(peers):
            for b in range(nb):
                cp = pltpu.make_async_remote_copy(
                    src_ref=x_refs[b], dst_ref=out_refs[b].at[4 * px + 2 * py + pc], send_sem=send_sems_.at[4 * b + kk],
                    recv_sem=recv_sems_.at[4 * b + kk], device_id=(px, py, pc), device_id_type=MESH)
                cp.wait_send()
                cp.wait_recv()

    outs = pl.pallas_call(
        body, name=name,
        out_shape=(*[pltpu.HBM(b.shape, b.dtype) for b in thru], *[pltpu.HBM(b.shape, b.dtype) for b in gathers]),
        in_specs=[hbm] * (2 * nb) + [sem, sem, pl.BlockSpec(memory_space=pl.ANY)], out_specs=[hbm] * (2 * nb),
        input_output_aliases={i: i for i in range(2 * nb)},
        compiler_params=pltpu.CompilerParams(has_side_effects=pltpu.SideEffectType.DATAFLOW_SIDE_EFFECTING),
    )(*thru, *gathers, send_sems, recv_sems, after)
    return list(outs[:nb]), list(outs[nb:])


def _gather_finish(blocks, gathers, name):
    nb = len(blocks)

    def body(*refs):
        x_refs, in_refs, out_refs = refs[:nb], refs[nb:2 * nb], refs[2 * nb:3 * nb]
        send_sems, recv_sems = refs[3 * nb:]
        (x, y, c), peers = _gather_peers()
        copies = []
        for j, (px, py, _) in enumerate(peers[1:]):
            for b in range(nb):
                copies.append(pltpu.make_async_remote_copy(
                    src_ref=in_refs[b].at[4 * px + 2 * py + c], dst_ref=out_refs[b].at[4 * px + 2 * py + c],
                    send_sem=send_sems.at[b, j], recv_sem=recv_sems.at[b, j], device_id=(x, y, 1 - c),
                    device_id_type=MESH))
                copies[-1].start()
        for j, (px, py, _) in enumerate(peers[1:]):
            for b in range(nb):
                pltpu.make_async_remote_copy(
                    src_ref=in_refs[b].at[4 * px + 2 * py + c], dst_ref=out_refs[b].at[4 * px + 2 * py + 1 - c],
                    send_sem=send_sems.at[b, j], recv_sem=recv_sems.at[b, j], device_id=(x, y, 1 - c),
                    device_id_type=MESH).wait_recv()
        for cp in copies:
            cp.wait_send()

    any_spec = pl.BlockSpec(memory_space=pl.ANY)
    outs = pl.pallas_call(
        body, name=name, out_shape=[jax.ShapeDtypeStruct(b.shape, b.dtype) for b in gathers],
        in_specs=[any_spec] * (2 * nb), out_specs=[any_spec] * nb,
        input_output_aliases={nb + i: i for i in range(nb)},
        scratch_shapes=[pltpu.SemaphoreType.DMA((nb, 3)), pltpu.SemaphoreType.DMA((nb, 3))],
    )(*blocks, *gathers)
    return [_put_own(o, b[None], _my_slot()) for o, b in zip(outs, blocks)]


def _sibling_swap(bufs, name):
    nb = len(bufs)

    def body(*refs):
        x_refs, out_refs = refs[:nb], refs[nb:2 * nb]
        send_sems, recv_sems = refs[2 * nb:]
        x, y, c = _coords()
        copies = [pltpu.make_async_remote_copy(
            src_ref=x_refs[b].at[2 * q + 1 - c], dst_ref=out_refs[b].at[q], send_sem=send_sems.at[b, q],
            recv_sem=recv_sems.at[b, q], device_id=(x, y, 1 - c), device_id_type=MESH)
            for b in range(nb) for q in range(4)]
        for cp in copies:
            cp.start()
        for cp in copies:
            cp.wait()

    return _hbm_call(body, name, bufs, [((4,) + b.shape[1:], b.dtype) for b in bufs],
                     [pltpu.SemaphoreType.DMA((nb, 4)), pltpu.SemaphoreType.DMA((nb, 4))])


def _chip_exchange_start(bufs, name):
    nb = len(bufs)
    hbm, sem = pl.BlockSpec(memory_space=pltpu.HBM), pl.BlockSpec(memory_space=pltpu.SEMAPHORE)

    def body(*refs):
        x_refs, land_refs = refs[:nb], refs[nb:2 * nb]
        send_sems, recv_sems, token = refs[2 * nb], refs[2 * nb + 1], refs[-1]
        x, y, c = _coords()
        for j, (px, py) in enumerate([(1 - x, y), (x, 1 - y), (1 - x, 1 - y)]):
            for b in range(nb):
                pltpu.make_async_remote_copy(
                    src_ref=x_refs[b].at[2 * px + py], dst_ref=land_refs[b].at[2 * x + y], send_sem=send_sems.at[3 * b + j],
                    recv_sem=recv_sems.at[3 * b + j], device_id=(px, py, c), device_id_type=MESH).start()
        token[...] = jnp.zeros_like(token)

    lands = [pltpu.with_memory_space_constraint(lax.empty(b.shape, b.dtype), pltpu.HBM) for b in bufs]
    outs = pl.pallas_call(
        body, name=name,
        out_shape=(pltpu.SemaphoreType.DMA((3 * nb,)), pltpu.SemaphoreType.DMA((3 * nb,)),
                   *[pltpu.HBM(b.shape, b.dtype) for b in bufs], *[pltpu.HBM(b.shape, b.dtype) for b in bufs],
                   jax.ShapeDtypeStruct((SUBLANE, LANE), F32)),
        in_specs=[hbm] * (2 * nb), out_specs=(sem, sem, *[hbm] * (2 * nb), pl.BlockSpec(memory_space=pltpu.VMEM)),
        input_output_aliases={i: 2 + i for i in range(2 * nb)},
        compiler_params=pltpu.CompilerParams(has_side_effects=pltpu.SideEffectType.DATAFLOW_SIDE_EFFECTING),
    )(*[pltpu.with_memory_space_constraint(b, pltpu.HBM) for b in bufs], *lands)
    return outs[0], outs[1], list(outs[2:2 + nb]), list(outs[2 + nb:2 + 2 * nb]), outs[-1]


def _chip_exchange_wait(send_sems, recv_sems, thru, lands, after, name):
    nb = len(thru)
    hbm, sem = pl.BlockSpec(memory_space=pltpu.HBM), pl.BlockSpec(memory_space=pltpu.SEMAPHORE)

    def body(*refs):
        x_refs, land_refs = refs[:nb], refs[nb:2 * nb]
        send_sems_, recv_sems_ = refs[2 * nb], refs[2 * nb + 1]
        x, y, c = _coords()
        for j, (px, py) in enumerate([(1 - x, y), (x, 1 - y), (1 - x, 1 - y)]):
            for b in range(nb):
                cp = pltpu.make_async_remote_copy(
                    src_ref=x_refs[b].at[2 * px + py], dst_ref=land_refs[b].at[2 * px + py],
                    send_sem=send_sems_.at[3 * b + j], recv_sem=recv_sems_.at[3 * b + j], device_id=(px, py, c),
                    device_id_type=MESH)
                cp.wait_send()
                cp.wait_recv()

    outs = pl.pallas_call(
        body, name=name,
        out_shape=(*[pltpu.HBM(b.shape, b.dtype) for b in thru], *[pltpu.HBM(b.shape, b.dtype) for b in lands]),
        in_specs=[hbm] * (2 * nb) + [sem, sem, pl.BlockSpec(memory_space=pl.ANY)], out_specs=[hbm] * (2 * nb),
        input_output_aliases={i: i for i in range(2 * nb)},
        compiler_params=pltpu.CompilerParams(has_side_effects=pltpu.SideEffectType.DATAFLOW_SIDE_EFFECTING),
    )(*thru, *lands, send_sems, recv_sems, after)
    return list(outs[:nb]), list(outs[nb:])


def _chip_sum(g, r1, c, name):
    _, r, cw = g.shape
    tr = _pick(r, (256, 176, 128))

    def body(c_ref, g_ref, r_ref, o_ref):
        o_ref[...] = (g_ref[...].astype(F32) + r_ref[...].astype(F32)).astype(GRAD_DT)

    grid_spec = pltpu.PrefetchScalarGridSpec(
        num_scalar_prefetch=1, grid=(4, r // tr),
        in_specs=[_BS((None, None, tr, cw), lambda q, i, c_ref: (q, c_ref[0], i, 0)),
                  _BS((None, tr, cw), lambda q, i, c_ref: (q, i, 0))],
        out_specs=_BS((None, tr, cw), lambda q, i, c_ref: (q, i, 0)))
    return pl.pallas_call(
        body, name=name, grid_spec=grid_spec, out_shape=jax.ShapeDtypeStruct((4, r, cw), GRAD_DT),
        compiler_params=_params(("parallel", "parallel")),
    )(c.reshape(1).astype(jnp.int32), g.reshape(4, 2, r, cw), r1)


def _adamw_math(w, g, m, v):
    m = B1 * m + (1.0 - B1) * g
    v = B2 * v + (1.0 - B2) * (g * g)
    m_hat = m / (1.0 - B1 ** STEP)
    v_hat = v / (1.0 - B2 ** STEP)
    return -LR * (m_hat / (jnp.sqrt(v_hat) + AEPS) + WD * w), m, v


def _sum_adamw(parts, w, m, v, name):
    r, c = w.shape
    tr = _pick(r, (256, 176, 128))
    tc = _pick(c, (256, 128)) if tr == r and r > 256 else c

    def body(p0, p1, p2, p3, w_ref, m_ref, v_ref, g_out, d_out, m_out, v_out):
        g = ((p0[...].astype(F32) + p1[...].astype(F32)) + p2[...].astype(F32)) + p3[...].astype(F32)
        g_out[...] = g
        d_out[...], m_out[...], v_out[...] = _adamw_math(w_ref[...], g, m_ref[...], v_ref[...])

    part = lambda q: _BS((None, tr, tc), functools.partial(lambda i, j, q: (q, i, j), q=q))
    plain = _BS((tr, tc), lambda i, j: (i, j))
    return pl.pallas_call(
        body, name=name, grid=(r // tr, c // tc), in_specs=[part(q) for q in range(4)] + [plain] * 3,
        out_specs=[plain] * 4, out_shape=[jax.ShapeDtypeStruct((r, c), F32)] * 4,
        compiler_params=_params(("parallel", "parallel")),
    )(parts, parts, parts, parts, w, m, v)


def _small_update(gathered, w, m, v):
    r = w.shape[0]

    def body(ga_ref, w_ref, m_ref, v_ref, g_out, d_out, m_out, v_out):
        g = ga_ref[0]
        for dev in range(1, N_DEV):
            g = g + ga_ref[dev]
        g_out[...] = g
        d_out[...], m_out[...], v_out[...] = _adamw_math(w_ref[...], g, m_ref[...], v_ref[...])

    return pl.pallas_call(
        body, name="small_update", out_shape=[jax.ShapeDtypeStruct((r, LANE), F32)] * 4,
    )(gathered, w, m, v)


def _pack_small(vals):
    rows = []
    for name, (r, n) in SMALL:
        flat = vals[name].reshape(-1)
        pad = (-flat.shape[0]) % (SUBLANE * LANE)
        rows.append(jnp.pad(flat, (0, pad)).reshape(-1, LANE))
    return jnp.concatenate(rows, axis=0)


def _unpack_small(packed):
    out, off = {}, 0
    for name, (r, n) in SMALL:
        nrow = -(-(r * n) // (SUBLANE * LANE)) * SUBLANE
        out[name] = packed[off:off + nrow].reshape(-1)[:r * n].reshape(r, n)
        off += nrow
    return out


def kernel(x, positions, ffn1_norm, ffn1_w_in, ffn1_w_out, mix_norm, w_in, hg_lb_table, hg_out_norm, w_hg_branch, mla_q_lora_norm, w_q_up, mla_kv_lora_norm, w_kv_up, q_head_norm, k_head_norm, w_mla_branch, w_merge, b_merge, w_out, ffn2_norm, ffn2_w_in, ffn2_w_out, final_norm, loss_target, m_ffn1_norm, m_ffn1_w_in, m_ffn1_w_out, m_mix_norm, m_w_in, m_hg_lb_table, m_hg_out_norm, m_w_hg_branch, m_mla_q_lora_norm, m_w_q_up, m_mla_kv_lora_norm, m_w_kv_up, m_q_head_norm, m_k_head_norm, m_w_mla_branch, m_w_merge, m_b_merge, m_w_out, m_ffn2_norm, m_ffn2_w_in, m_ffn2_w_out, m_final_norm, v_ffn1_norm, v_ffn1_w_in, v_ffn1_w_out, v_mix_norm, v_w_in, v_hg_lb_table, v_hg_out_norm, v_w_hg_branch, v_mla_q_lora_norm, v_w_q_up, v_mla_kv_lora_norm, v_w_kv_up, v_q_head_norm, v_k_head_norm, v_w_mla_branch, v_w_merge, v_b_merge, v_w_out, v_ffn2_norm, v_ffn2_w_in, v_ffn2_w_out, v_final_norm):
    args = dict(locals())
    t = x.shape[1]
    big_w = {n: args[n][0] for n, _, _ in BIG}
    small = {n: args[n].reshape(shape) for n, shape in SMALL}

    names = [n for n, _, _ in BIG]
    first, rest = names[:2], names[2:]
    full = dict(zip(first, _all_gather([big_w[n].astype(BF16) for n in first], "weights_all_gather_ffn1")))
    g_send, g_recv, g_thru, g_bufs, g_token = _gather_start([big_w[n].astype(BF16) for n in rest], full[first[0]],
                                                            "weights_gather_start")
    gains = dict(small, ffn1_norm=small["ffn1_norm"] + g_token[0:1, 0:1])

    def late_weights(after):
        blocks, bufs = _gather_wait(g_send, g_recv, g_thru, g_bufs, after, "weights_gather_wait")
        return dict(zip(rest, _gather_finish(blocks, bufs, "weights_gather_finish")))

    inv_freq = ROPE_THETA ** (-jnp.arange(0, ROPE, 2, dtype=F32) / ROPE)
    ang = positions[0].astype(F32)[:, None] * inv_freq
    cs = jnp.concatenate([jnp.cos(ang), jnp.cos(ang)], axis=1)
    sn = jnp.concatenate([jnp.sin(ang), jnp.sin(ang)], axis=1)

    c = lax.axis_index("c")
    chip = 2 * lax.axis_index("x") + lax.axis_index("y")
    early = {}

    def chip_sums_of(g, ns, tag):
        from_sibling = _sibling_swap([g[n] for n in ns], "grads_sibling_swap_" + tag)
        return [_chip_sum(g[n], r1, c, "chip_sum_" + n) for n, r1 in zip(ns, from_sibling)]

    def early_grads(g):
        early["names"] = [n for n in names if n in g]
        early["sums"] = chip_sums_of(g, early["names"], "early")
        early["send"], early["recv"], early["thru"], early["lands"], token = _chip_exchange_start(
            early["sums"], "grads_exchange_start")
        return token

    def last_grads(g):
        early["late"] = [n for n in names if n not in early["names"]]
        early["l_send"], early["l_recv"], early["l_thru"], early["l_lands"], token = _chip_exchange_start(
            chip_sums_of(g, early["late"], "late"), "grads_exchange_late_start")
        return token

    loss_row, grad_x, g = _local_step(x[0], loss_target[0], cs, sn, full, late_weights, gains, early_grads,
                                      last_grads)
    late, l_send, l_recv = early["late"], early["l_send"], early["l_recv"]
    l_thru, l_lands = early["l_thru"], early["l_lands"]
    sent, landed = _chip_exchange_wait(early["send"], early["recv"], early["thru"], early["lands"], grad_x,
                                       "grads_exchange_wait")
    exchanged = {n: _put_own(land, lax.dynamic_index_in_dim(own, chip, 0), chip)
                 for n, land, own in zip(early["names"], landed, sent)}

    small_packed = jnp.concatenate([_pack_small(g), jnp.pad(loss_row, ((0, SUBLANE - 1), (0, 0)))], axis=0)
    small_all = _all_gather([small_packed], "small_all_gather")[0]
    zero_tail = jnp.zeros((SUBLANE, LANE), F32)
    pk = lambda d: jnp.concatenate([_pack_small(d), zero_tail], axis=0)
    sg, sd, sm, sv = _small_update(
        small_all, pk(small), pk({n: args["m_" + n].reshape(shape) for n, shape in SMALL}),
        pk({n: args["v_" + n].reshape(shape) for n, shape in SMALL}))
    n_small_rows = _pack_small(small).shape[0]
    loss = sg[n_small_rows, 0]
    outs = {k_: _unpack_small(a) for k_, a in (("grad", sg), ("delta", sd), ("new_m", sm), ("new_v", sv))}

    def adamw(n):
        tr = (lambda a: a.T) if n in GRAD_T else (lambda a: a)
        res = _sum_adamw(exchanged[n], tr(big_w[n]), tr(args["m_" + n][0]), tr(args["v_" + n][0]), "adamw_" + n)
        outs["grad"][n], outs["delta"][n], outs["new_m"][n], outs["new_v"][n] = [tr(a) for a in res]

    for n in early["names"]:
        adamw(n)
    sent, landed = _chip_exchange_wait(l_send, l_recv, l_thru, l_lands, outs["delta"][early["names"][-1]],
                                       "grads_exchange_late_wait")
    for n, land, own in zip(late, landed, sent):
        exchanged[n] = _put_own(land, lax.dynamic_index_in_dim(own, chip, 0), chip)
        adamw(n)

    def shaped(kind, n):
        return outs[kind][n].reshape(args[n].shape)

    return (loss, grad_x[None], *[shaped("grad", n) for n in WEIGHT_ORDER], *[shaped("delta", n) for n in WEIGHT_ORDER],
            *[shaped("new_m", n) for n in WEIGHT_ORDER], *[shaped("new_v", n) for n in WEIGHT_ORDER])
```
